```python
import jax, jax.numpy as jnp
from jax import lax
import numpy as np

D_MODEL = 1024
BATCH = 16
SEQ = 2048
DEPTH = 1

MEM_LEN = 256
GM_GROUPS = 4
GM_CHUNK = 128
GM_WIDTH = 512
GM_GROUP_DIM = GM_WIDTH // GM_GROUPS
HG_HEADS = 4
HG_KDIM = 128
HG_VDIM = 128
HG_WIDTH = HG_HEADS * HG_KDIM
HG_VWIDTH = HG_HEADS * HG_VDIM
HG_CHUNK = 64
XA_HEADS = 4
XA_HEAD_DIM = 128
XA_WIDTH = XA_HEADS * XA_HEAD_DIM
N_BRANCH = 3
BRANCH_WIDTH = 512
D_FF = 2816
CONV_WIDTH = 3
EPS = 1e-6
IN_SPLITS = (GM_WIDTH, GM_WIDTH, HG_WIDTH, HG_WIDTH, HG_VWIDTH, HG_VWIDTH, XA_WIDTH, N_BRANCH * D_MODEL)
IN_WIDTH = GM_WIDTH * 2 + HG_WIDTH * 2 + HG_VWIDTH * 2 + XA_WIDTH + N_BRANCH * D_MODEL

kernel_name = 'hybrid_gmlp_hgrn2_memattn_convffn'


def rms_norm(x, g):
    xf = x.astype(jnp.float32)
    y = xf * lax.rsqrt(jnp.mean(xf * xf, axis=-1, keepdims=True) + EPS)
    return y.astype(x.dtype) * g


def layer_norm(x, g, b):
    xf = x.astype(jnp.float32)
    mu = jnp.mean(xf, axis=-1, keepdims=True)
    var = jnp.mean(jnp.square(xf - mu), axis=-1, keepdims=True)
    return ((xf - mu) * lax.rsqrt(var + EPS)).astype(x.dtype) * g + b


def spatial_gating(z_u, z_v, ln_g, ln_b, w_s, b_s):
    B, S, _ = z_u.shape
    v = layer_norm(z_v, ln_g, ln_b).reshape(B, S // GM_CHUNK, GM_CHUNK, GM_GROUPS, GM_GROUP_DIM)
    causal = jnp.tril(jnp.ones((GM_CHUNK, GM_CHUNK), dtype=bool))
    w = jnp.where(causal[None], w_s, 0)
    mixed = jnp.einsum('gts,bnsgc->bntgc', w, v) + b_s.T[None, None, :, :, None]
    return z_u * mixed.reshape(B, S, GM_WIDTH)


def hgrn2(q, f_raw, i, g, lb, norm_g):
    B, S, _ = q.shape
    N = S // HG_CHUNK
    f32 = jnp.float32
    lbf = lb.astype(f32).reshape(HG_HEADS, HG_KDIM)
    fg = lbf + (1.0 - lbf) * jax.nn.sigmoid(f_raw.astype(f32).reshape(B, S, HG_HEADS, HG_KDIM))
    k = 1.0 - fg
    logf = jnp.log(fg)
    qf = jax.nn.silu(q.astype(f32)).reshape(B, S, HG_HEADS, HG_KDIM)
    vf = i.astype(f32).reshape(B, S, HG_HEADS, HG_VDIM)

    def to_chunks(t):
        return t.reshape(B, N, HG_CHUNK, HG_HEADS, t.shape[-1]).transpose(1, 0, 3, 2, 4)

    causal = jnp.tril(jnp.ones((HG_CHUNK, HG_CHUNK), dtype=bool))

    def step(state, xs):
        qc, kc, vc, lfc = xs
        a = jnp.cumsum(lfc, axis=2)
        inter = jnp.einsum('bhtk,bhkv->bhtv', qc * jnp.exp(a), state)
        diff = a[:, :, :, None, :] - a[:, :, None, :, :]
        decay = jnp.exp(jnp.where(causal[:, :, None], diff, -jnp.inf))
        scores = jnp.einsum('bhtk,bhtsk,bhsk->bhts', qc, decay, kc)
        intra = jnp.einsum('bhts,bhsv->bhtv', scores, vc)
        a_last = a[:, :, -1:, :]
        new_state = (jnp.exp(a_last[:, :, 0, :])[..., None] * state
                     + jnp.einsum('bhsk,bhsv->bhkv', kc * jnp.exp(a_last - a), vc))
        return new_state, inter + intra

    s0 = jnp.zeros((B, HG_HEADS, HG_KDIM, HG_VDIM), f32)
    _, o = lax.scan(step, s0, (to_chunks(qf), to_chunks(k), to_chunks(vf), to_chunks(logf)))
    o = o.transpose(1, 0, 3, 2, 4).reshape(B, S, HG_HEADS, HG_VDIM)
    o = rms_norm(o, norm_g.astype(f32)) * jax.nn.silu(g.astype(f32).reshape(B, S, HG_HEADS, HG_VDIM))
    return o.reshape(B, S, HG_VWIDTH).astype(q.dtype)


def memory_attention(q, mem, mem_g, w_kv):
    B, S, _ = q.shape
    M = mem.shape[1]
    kv = rms_norm(mem, mem_g) @ w_kv
    k, v = jnp.split(kv, 2, axis=-1)
    k = k.reshape(B, M, XA_HEADS, XA_HEAD_DIM)
    v = v.reshape(B, M, XA_HEADS, XA_HEAD_DIM)
    qh = q.reshape(B, S, XA_HEADS, XA_HEAD_DIM)
    s = jnp.einsum('bshd,bmhd->bhsm', qh, k).astype(jnp.float32) * (XA_HEAD_DIM ** -0.5)
    p = jax.nn.softmax(s, axis=-1).astype(q.dtype)
    return jnp.einsum('bhsm,bmhd->bshd', p, v).reshape(B, S, XA_WIDTH)


def conv_ffn(h, w_up, conv_w, conv_b, w_down):
    S = h.shape[1]
    a, b = jnp.split(h @ w_up, 2, axis=-1)
    ap = jnp.pad(a, ((0, 0), (CONV_WIDTH - 1, 0), (0, 0)))
    ac = conv_b + conv_w[0] * ap[:, 0:S]
    for j in range(1, CONV_WIDTH):
        ac = ac + conv_w[j] * ap[:, j:j + S]
    return (jax.nn.silu(ac) * b) @ w_down


def _fwd_setup_inputs(seed: int = 0) -> dict:
    key = jax.random.key(seed)
    ks = jax.random.split(key, 24)
    f32 = jnp.float32
    nrm = lambda k, shape, scale: jax.random.normal(k, shape, f32) * scale
    return {
        'x': nrm(ks[0], (BATCH, SEQ, D_MODEL), 1.0),
        'mem': nrm(ks[1], (BATCH, MEM_LEN, D_MODEL), 1.0),
        'norm1_g': 1.0 + nrm(ks[2], (DEPTH, D_MODEL), 0.02),
        'w_in': nrm(ks[3], (DEPTH, D_MODEL, IN_WIDTH), D_MODEL ** -0.5),
        'ln_v_g': 1.0 + nrm(ks[4], (DEPTH, GM_WIDTH), 0.02),
        'ln_v_b': nrm(ks[5], (DEPTH, GM_WIDTH), 0.02),
        'w_spatial': nrm(ks[6], (DEPTH, GM_GROUPS, GM_CHUNK, GM_CHUNK), GM_CHUNK ** -0.5),
        'b_spatial': 1.0 + nrm(ks[7], (DEPTH, GM_GROUPS, GM_CHUNK), 0.02),
        'lb_logits': nrm(ks[8], (DEPTH + 1, HG_WIDTH), 0.5),
        'hgrn_norm_g': 1.0 + nrm(ks[9], (DEPTH, HG_VDIM), 0.02),
        'mem_norm_g': 1.0 + nrm(ks[10], (DEPTH, D_MODEL), 0.02),
        'w_mem_kv': nrm(ks[11], (DEPTH, D_MODEL, 2 * XA_WIDTH), D_MODEL ** -0.5),
        'w_branch': nrm(ks[12], (DEPTH, N_BRANCH, BRANCH_WIDTH, D_MODEL), BRANCH_WIDTH ** -0.5),
        'w_out': nrm(ks[13], (DEPTH, D_MODEL, D_MODEL), D_MODEL ** -0.5),
        'norm2_g': 1.0 + nrm(ks[14], (DEPTH, D_MODEL), 0.02),
        'w_up': nrm(ks[15], (DEPTH, D_MODEL, 2 * D_FF), D_MODEL ** -0.5),
        'conv_w': nrm(ks[16], (DEPTH, CONV_WIDTH, D_FF), CONV_WIDTH ** -0.5),
        'conv_b': nrm(ks[17], (DEPTH, D_FF), 0.02),
        'w_down': nrm(ks[18], (DEPTH, D_FF, D_MODEL), D_FF ** -0.5),
        'final_g': 1.0 + nrm(ks[19], (D_MODEL,), 0.02),
    }


def _fwd_reference(x, mem, norm1_g, w_in, ln_v_g, ln_v_b, w_spatial, b_spatial, lb_logits,
              hgrn_norm_g, mem_norm_g, w_mem_kv, w_branch, w_out, norm2_g, w_up,
              conv_w, conv_b, w_down, final_g):
    B, S, D = x.shape
    split_points = np.cumsum(IN_SPLITS)[:-1].tolist()
    lb_all = jnp.cumsum(jax.nn.softmax(lb_logits.astype(jnp.float32), axis=0), axis=0)
    for l in range(DEPTH):
        h = rms_norm(x, norm1_g[l])
        proj = h @ w_in[l]
        zu, zv, hq, hf, hi, hg, xq, gate_logits = jnp.split(proj, split_points, axis=-1)
        a_out = spatial_gating(jax.nn.gelu(zu), jax.nn.gelu(zv), ln_v_g[l], ln_v_b[l],
                               w_spatial[l], b_spatial[l])
        b_out = hgrn2(hq, hf, hi, hg, lb_all[l], hgrn_norm_g[l])
        c_out = memory_attention(xq, mem, mem_norm_g[l], w_mem_kv[l])
        branches = jnp.stack([a_out, b_out, c_out], axis=0)
        up = jnp.einsum('nbsc,ncd->nbsd', branches, w_branch[l])
        gates = jax.nn.sigmoid(gate_logits.reshape(B, S, N_BRANCH, D))
        merged = jnp.einsum('bsnd,nbsd->bsd', gates, up)
        x = x + merged @ w_out[l]
        x = x + conv_ffn(rms_norm(x, norm2_g[l]), w_up[l], conv_w[l], conv_b[l], w_down[l])
    return rms_norm(x, final_g)


import jax as _jax
import jax.numpy as _jnp

TWIN_FORMAT = 'train_step'
FWD_PARAMS = ['x', 'mem', 'norm1_g', 'w_in', 'ln_v_g', 'ln_v_b', 'w_spatial', 'b_spatial', 'lb_logits', 'hgrn_norm_g', 'mem_norm_g', 'w_mem_kv', 'w_branch', 'w_out', 'norm2_g', 'w_up', 'conv_w', 'conv_b', 'w_down', 'final_g']
TWIN_WEIGHTS = ['norm1_g', 'w_in', 'ln_v_g', 'ln_v_b', 'w_spatial', 'b_spatial', 'lb_logits', 'hgrn_norm_g', 'mem_norm_g', 'w_mem_kv', 'w_branch', 'w_out', 'norm2_g', 'w_up', 'conv_w', 'conv_b', 'w_down', 'final_g']
TWIN_DIFF_INPUT = 'x'
TWIN_INPUTS = ['x', 'mem', 'norm1_g', 'w_in', 'ln_v_g', 'ln_v_b', 'w_spatial', 'b_spatial', 'lb_logits', 'hgrn_norm_g', 'mem_norm_g', 'w_mem_kv', 'w_branch', 'w_out', 'norm2_g', 'w_up', 'conv_w', 'conv_b', 'w_down', 'final_g', 'loss_target', 'm_norm1_g', 'm_w_in', 'm_ln_v_g', 'm_ln_v_b', 'm_w_spatial', 'm_b_spatial', 'm_lb_logits', 'm_hgrn_norm_g', 'm_mem_norm_g', 'm_w_mem_kv', 'm_w_branch', 'm_w_out', 'm_norm2_g', 'm_w_up', 'm_conv_w', 'm_conv_b', 'm_w_down', 'm_final_g', 'v_norm1_g', 'v_w_in', 'v_ln_v_g', 'v_ln_v_b', 'v_w_spatial', 'v_b_spatial', 'v_lb_logits', 'v_hgrn_norm_g', 'v_mem_norm_g', 'v_w_mem_kv', 'v_w_branch', 'v_w_out', 'v_norm2_g', 'v_w_up', 'v_conv_w', 'v_conv_b', 'v_w_down', 'v_final_g']
TWIN_OUTPUTS = ['loss', 'grad_x', 'grad_norm1_g', 'grad_w_in', 'grad_ln_v_g', 'grad_ln_v_b', 'grad_w_spatial', 'grad_b_spatial', 'grad_lb_logits', 'grad_hgrn_norm_g', 'grad_mem_norm_g', 'grad_w_mem_kv', 'grad_w_branch', 'grad_w_out', 'grad_norm2_g', 'grad_w_up', 'grad_conv_w', 'grad_conv_b', 'grad_w_down', 'grad_final_g', 'delta_norm1_g', 'delta_w_in', 'delta_ln_v_g', 'delta_ln_v_b', 'delta_w_spatial', 'delta_b_spatial', 'delta_lb_logits', 'delta_hgrn_norm_g', 'delta_mem_norm_g', 'delta_w_mem_kv', 'delta_w_branch', 'delta_w_out', 'delta_norm2_g', 'delta_w_up', 'delta_conv_w', 'delta_conv_b', 'delta_w_down', 'delta_final_g', 'new_m_norm1_g', 'new_m_w_in', 'new_m_ln_v_g', 'new_m_ln_v_b', 'new_m_w_spatial', 'new_m_b_spatial', 'new_m_lb_logits', 'new_m_hgrn_norm_g', 'new_m_mem_norm_g', 'new_m_w_mem_kv', 'new_m_w_branch', 'new_m_w_out', 'new_m_norm2_g', 'new_m_w_up', 'new_m_conv_w', 'new_m_conv_b', 'new_m_w_down', 'new_m_final_g', 'new_v_norm1_g', 'new_v_w_in', 'new_v_ln_v_g', 'new_v_ln_v_b', 'new_v_w_spatial', 'new_v_b_spatial', 'new_v_lb_logits', 'new_v_hgrn_norm_g', 'new_v_mem_norm_g', 'new_v_w_mem_kv', 'new_v_w_branch', 'new_v_w_out', 'new_v_norm2_g', 'new_v_w_up', 'new_v_conv_w', 'new_v_conv_b', 'new_v_w_down', 'new_v_final_g']
TWIN_LEAF_KINDS = {'loss': 'loss', 'grad_x': 'grad_x', 'grad_norm1_g': 'grad_w', 'grad_w_in': 'grad_w', 'grad_ln_v_g': 'grad_w', 'grad_ln_v_b': 'grad_w', 'grad_w_spatial': 'grad_w', 'grad_b_spatial': 'grad_w', 'grad_lb_logits': 'grad_w', 'grad_hgrn_norm_g': 'grad_w', 'grad_mem_norm_g': 'grad_w', 'grad_w_mem_kv': 'grad_w', 'grad_w_branch': 'grad_w', 'grad_w_out': 'grad_w', 'grad_norm2_g': 'grad_w', 'grad_w_up': 'grad_w', 'grad_conv_w': 'grad_w', 'grad_conv_b': 'grad_w', 'grad_w_down': 'grad_w', 'grad_final_g': 'grad_w', 'delta_norm1_g': 'delta_w', 'delta_w_in': 'delta_w', 'delta_ln_v_g': 'delta_w', 'delta_ln_v_b': 'delta_w', 'delta_w_spatial': 'delta_w', 'delta_b_spatial': 'delta_w', 'delta_lb_logits': 'delta_w', 'delta_hgrn_norm_g': 'delta_w', 'delta_mem_norm_g': 'delta_w', 'delta_w_mem_kv': 'delta_w', 'delta_w_branch': 'delta_w', 'delta_w_out': 'delta_w', 'delta_norm2_g': 'delta_w', 'delta_w_up': 'delta_w', 'delta_conv_w': 'delta_w', 'delta_conv_b': 'delta_w', 'delta_w_down': 'delta_w', 'delta_final_g': 'delta_w', 'new_m_norm1_g': 'new_m', 'new_m_w_in': 'new_m', 'new_m_ln_v_g': 'new_m', 'new_m_ln_v_b': 'new_m', 'new_m_w_spatial': 'new_m', 'new_m_b_spatial': 'new_m', 'new_m_lb_logits': 'new_m', 'new_m_hgrn_norm_g': 'new_m', 'new_m_mem_norm_g': 'new_m', 'new_m_w_mem_kv': 'new_m', 'new_m_w_branch': 'new_m', 'new_m_w_out': 'new_m', 'new_m_norm2_g': 'new_m', 'new_m_w_up': 'new_m', 'new_m_conv_w': 'new_m', 'new_m_conv_b': 'new_m', 'new_m_w_down': 'new_m', 'new_m_final_g': 'new_m', 'new_v_norm1_g': 'new_v', 'new_v_w_in': 'new_v', 'new_v_ln_v_g': 'new_v', 'new_v_ln_v_b': 'new_v', 'new_v_w_spatial': 'new_v', 'new_v_b_spatial': 'new_v', 'new_v_lb_logits': 'new_v', 'new_v_hgrn_norm_g': 'new_v', 'new_v_mem_norm_g': 'new_v', 'new_v_w_mem_kv': 'new_v', 'new_v_w_branch': 'new_v', 'new_v_w_out': 'new_v', 'new_v_norm2_g': 'new_v', 'new_v_w_up': 'new_v', 'new_v_conv_w': 'new_v', 'new_v_conv_b': 'new_v', 'new_v_w_down': 'new_v', 'new_v_final_g': 'new_v'}


def _forward(args):
    return _fwd_reference(*[args[k] for k in FWD_PARAMS])


def _output_shape():
    out = _jax.eval_shape(lambda: _forward(_fwd_setup_inputs(0)))
    return out.shape, out.dtype

N_MICROBATCH = 1
ADAM_LR = 0.001
ADAM_B1 = 0.9
ADAM_B2 = 0.999
ADAM_EPS = 1e-08
ADAM_WD = 0.01
ADAM_STEP = 10
PER_EXAMPLE_BATCH_AXIS = {'x': 0, 'mem': 0, 'loss_target': 0}
SHARED_INPUTS = []
_WEIGHT_DTYPES = {'norm1_g': _jnp.float32, 'w_in': _jnp.float32, 'ln_v_g': _jnp.float32, 'ln_v_b': _jnp.float32, 'w_spatial': _jnp.float32, 'b_spatial': _jnp.float32, 'lb_logits': _jnp.float32, 'hgrn_norm_g': _jnp.float32, 'mem_norm_g': _jnp.float32, 'w_mem_kv': _jnp.float32, 'w_branch': _jnp.float32, 'w_out': _jnp.float32, 'norm2_g': _jnp.float32, 'w_up': _jnp.float32, 'conv_w': _jnp.float32, 'conv_b': _jnp.float32, 'w_down': _jnp.float32, 'final_g': _jnp.float32}
MOMENT_SCALE = {'norm1_g': 1.289560e-01, 'w_in': 5.033862e-02, 'ln_v_g': 6.275982e-02, 'ln_v_b': 6.618359e-02, 'w_spatial': 6.100067e-02, 'b_spatial': 8.679473e-02, 'lb_logits': 7.033376e-03, 'hgrn_norm_g': 1.584890e-01, 'mem_norm_g': 1.510075e-02, 'w_mem_kv': 1.402170e-02, 'w_branch': 5.547670e-02, 'w_out': 9.472240e-02, 'norm2_g': 1.207200e-01, 'w_up': 5.142491e-02, 'conv_w': 5.231362e-02, 'conv_b': 5.195323e-02, 'w_down': 8.411157e-02, 'final_g': 3.190134e+01}


def _to_microbatches(a, axis):
    t = _jnp.moveaxis(a, axis, 0)
    t = t.reshape((N_MICROBATCH, t.shape[0] // N_MICROBATCH) + t.shape[1:])
    return _jnp.moveaxis(t, 1, axis + 1)


def setup_inputs(seed: int = 0) -> dict:
    inp = _fwd_setup_inputs(seed)
    key = _jax.random.fold_in(_jax.random.key(seed), 7919)
    shape, _ = _output_shape()
    out = dict(inp)
    out["loss_target"] = _jax.random.normal(_jax.random.fold_in(key, 0), shape, _jnp.float32)
    for i, name in enumerate(TWIN_WEIGHTS):
        w = inp[name].astype(_jnp.float32)
        if MOMENT_SCALE is None:
            s = _jnp.sqrt(_jnp.mean(_jnp.square(w)) + 1e-30)
        else:
            s = MOMENT_SCALE[name]
        km, kv = _jax.random.split(_jax.random.fold_in(key, i + 1))
        out[name] = w
        out["m_" + name] = s * _jax.random.normal(km, w.shape, _jnp.float32)
        out["v_" + name] = (s * s) * _jax.random.uniform(kv, w.shape, _jnp.float32, 0.5, 1.5)
    if N_MICROBATCH > 1:
        for name, axis in PER_EXAMPLE_BATCH_AXIS.items():
            out[name] = _to_microbatches(out[name], axis)
    return {'x': out['x'], 'mem': out['mem'], 'norm1_g': out['norm1_g'], 'w_in': out['w_in'], 'ln_v_g': out['ln_v_g'], 'ln_v_b': out['ln_v_b'], 'w_spatial': out['w_spatial'], 'b_spatial': out['b_spatial'], 'lb_logits': out['lb_logits'], 'hgrn_norm_g': out['hgrn_norm_g'], 'mem_norm_g': out['mem_norm_g'], 'w_mem_kv': out['w_mem_kv'], 'w_branch': out['w_branch'], 'w_out': out['w_out'], 'norm2_g': out['norm2_g'], 'w_up': out['w_up'], 'conv_w': out['conv_w'], 'conv_b': out['conv_b'], 'w_down': out['w_down'], 'final_g': out['final_g'], 'loss_target': out['loss_target'], 'm_norm1_g': out['m_norm1_g'], 'm_w_in': out['m_w_in'], 'm_ln_v_g': out['m_ln_v_g'], 'm_ln_v_b': out['m_ln_v_b'], 'm_w_spatial': out['m_w_spatial'], 'm_b_spatial': out['m_b_spatial'], 'm_lb_logits': out['m_lb_logits'], 'm_hgrn_norm_g': out['m_hgrn_norm_g'], 'm_mem_norm_g': out['m_mem_norm_g'], 'm_w_mem_kv': out['m_w_mem_kv'], 'm_w_branch': out['m_w_branch'], 'm_w_out': out['m_w_out'], 'm_norm2_g': out['m_norm2_g'], 'm_w_up': out['m_w_up'], 'm_conv_w': out['m_conv_w'], 'm_conv_b': out['m_conv_b'], 'm_w_down': out['m_w_down'], 'm_final_g': out['m_final_g'], 'v_norm1_g': out['v_norm1_g'], 'v_w_in': out['v_w_in'], 'v_ln_v_g': out['v_ln_v_g'], 'v_ln_v_b': out['v_ln_v_b'], 'v_w_spatial': out['v_w_spatial'], 'v_b_spatial': out['v_b_spatial'], 'v_lb_logits': out['v_lb_logits'], 'v_hgrn_norm_g': out['v_hgrn_norm_g'], 'v_mem_norm_g': out['v_mem_norm_g'], 'v_w_mem_kv': out['v_w_mem_kv'], 'v_w_branch': out['v_w_branch'], 'v_w_out': out['v_w_out'], 'v_norm2_g': out['v_norm2_g'], 'v_w_up': out['v_w_up'], 'v_conv_w': out['v_conv_w'], 'v_conv_b': out['v_conv_b'], 'v_w_down': out['v_w_down'], 'v_final_g': out['v_final_g']}


def _loss(weights, diff, rest, loss_target):
    with _jax.named_scope("forward"):
        args = {**rest, TWIN_DIFF_INPUT: diff, **{k: w.astype(_WEIGHT_DTYPES[k]) for k, w in weights.items()}}
        y = _forward(args)
    with _jax.named_scope("loss_head"):
        err = _jnp.square(y.astype(_jnp.float32) - loss_target)
        return 0.5 * _jnp.sum(_jnp.mean(err, axis=-1)) if err.ndim else 0.5 * err


def _adamw(w, g, m, v):
    m = ADAM_B1 * m + (1.0 - ADAM_B1) * g
    v = ADAM_B2 * v + (1.0 - ADAM_B2) * _jnp.square(g)
    m_hat = m / (1.0 - ADAM_B1 ** ADAM_STEP)
    v_hat = v / (1.0 - ADAM_B2 ** ADAM_STEP)
    delta = -ADAM_LR * (m_hat / (_jnp.sqrt(v_hat) + ADAM_EPS) + ADAM_WD * w)
    return delta, m, v


def reference(x, mem, norm1_g, w_in, ln_v_g, ln_v_b, w_spatial, b_spatial, lb_logits, hgrn_norm_g, mem_norm_g, w_mem_kv, w_branch, w_out, norm2_g, w_up, conv_w, conv_b, w_down, final_g, loss_target, m_norm1_g, m_w_in, m_ln_v_g, m_ln_v_b, m_w_spatial, m_b_spatial, m_lb_logits, m_hgrn_norm_g, m_mem_norm_g, m_w_mem_kv, m_w_branch, m_w_out, m_norm2_g, m_w_up, m_conv_w, m_conv_b, m_w_down, m_final_g, v_norm1_g, v_w_in, v_ln_v_g, v_ln_v_b, v_w_spatial, v_b_spatial, v_lb_logits, v_hgrn_norm_g, v_mem_norm_g, v_w_mem_kv, v_w_branch, v_w_out, v_norm2_g, v_w_up, v_conv_w, v_conv_b, v_w_down, v_final_g):
    given = dict(x=x, mem=mem, norm1_g=norm1_g, w_in=w_in, ln_v_g=ln_v_g, ln_v_b=ln_v_b, w_spatial=w_spatial, b_spatial=b_spatial, lb_logits=lb_logits, hgrn_norm_g=hgrn_norm_g, mem_norm_g=mem_norm_g, w_mem_kv=w_mem_kv, w_branch=w_branch, w_out=w_out, norm2_g=norm2_g, w_up=w_up, conv_w=conv_w, conv_b=conv_b, w_down=w_down, final_g=final_g, loss_target=loss_target, m_norm1_g=m_norm1_g, m_w_in=m_w_in, m_ln_v_g=m_ln_v_g, m_ln_v_b=m_ln_v_b, m_w_spatial=m_w_spatial, m_b_spatial=m_b_spatial, m_lb_logits=m_lb_logits, m_hgrn_norm_g=m_hgrn_norm_g, m_mem_norm_g=m_mem_norm_g, m_w_mem_kv=m_w_mem_kv, m_w_branch=m_w_branch, m_w_out=m_w_out, m_norm2_g=m_norm2_g, m_w_up=m_w_up, m_conv_w=m_conv_w, m_conv_b=m_conv_b, m_w_down=m_w_down, m_final_g=m_final_g, v_norm1_g=v_norm1_g, v_w_in=v_w_in, v_ln_v_g=v_ln_v_g, v_ln_v_b=v_ln_v_b, v_w_spatial=v_w_spatial, v_b_spatial=v_b_spatial, v_lb_logits=v_lb_logits, v_hgrn_norm_g=v_hgrn_norm_g, v_mem_norm_g=v_mem_norm_g, v_w_mem_kv=v_w_mem_kv, v_w_branch=v_w_branch, v_w_out=v_w_out, v_norm2_g=v_norm2_g, v_w_up=v_w_up, v_conv_w=v_conv_w, v_conv_b=v_conv_b, v_w_down=v_w_down, v_final_g=v_final_g)
    weights = {n: given[n] for n in TWIN_WEIGHTS}
    shared = {n: given[n] for n in SHARED_INPUTS}
    per_example = {n: given[n] for n in ['x', 'mem']}
    grad_fn = _jax.value_and_grad(_loss, argnums=(0, 1))

    def one_microbatch(ex, loss_target):
        ex = dict(ex)
        diff = ex.pop(TWIN_DIFF_INPUT)
        return grad_fn(weights, diff, {**shared, **ex}, loss_target)

    if N_MICROBATCH == 1:
        loss, (grad_w, grad_x) = one_microbatch(per_example, given["loss_target"])
    else:
        def body(carry, xs):
            loss_sum, grad_sum = carry
            l_k, (gw_k, gx_k) = one_microbatch(xs[0], xs[1])
            with _jax.named_scope("update"):
                return (loss_sum + l_k, _jax.tree.map(_jnp.add, grad_sum, gw_k)), gx_k

        init = (_jnp.zeros((), _jnp.float32), _jax.tree.map(_jnp.zeros_like, weights))
        (loss, grad_w), grad_x = _jax.lax.scan(body, init, (per_example, given["loss_target"]))
    with _jax.named_scope("update"):
        delta_w, new_m, new_v = {}, {}, {}
        for n in TWIN_WEIGHTS:
            delta_w[n], new_m[n], new_v[n] = _adamw(weights[n], grad_w[n], given["m_" + n], given["v_" + n])
    return (loss, grad_x, *[grad_w[n] for n in TWIN_WEIGHTS], *[delta_w[n] for n in TWIN_WEIGHTS],
            *[new_m[n] for n in TWIN_WEIGHTS], *[new_v[n] for n in TWIN_WEIGHTS])
```

```python
import functools
import math

import jax
import jax.numpy as jnp
from jax import lax
from jax.experimental import pallas as pl
from jax.experimental.pallas import tpu as pltpu

F32 = jnp.float32
BF16 = jnp.bfloat16
EPS = 1e-6

D_MODEL = 1024
MEM_LEN = 256
GM_WIDTH = 512
GM_CHUNK = 128
GM_GROUPS = 4
HG_HEADS = 4
HG_DIM = 128
HG_CHUNK = 64
XA_HEADS = 4
XA_DIM = 128
BR_WIDTH = 512
D_FF = 2816
IN_WIDTH = 6656
N_CHIPS = 4
N_DEV = 8

ADAM_LR = 0.001
ADAM_B1 = 0.9
ADAM_B2 = 0.999
ADAM_EPS = 1e-08
ADAM_WD = 0.01
ADAM_STEP = 10

COL_ZU, COL_ZV, COL_HQ, COL_HF, COL_HI, COL_HG, COL_XQ = 0, 1, 2, 3, 4, 5, 6
COL_GATE0 = 3584

VMEM_LIMIT_BYTES = 48 * 1024 * 1024
MESH_ID = pl.DeviceIdType.MESH


def _cp(*sem):
    return pltpu.CompilerParams(dimension_semantics=sem, vmem_limit_bytes=VMEM_LIMIT_BYTES)


def _dot(a, b):
    return lax.dot_general(a.astype(BF16), b.astype(BF16), (((1,), (0,)), ((), ())), preferred_element_type=F32)


def _dot_nt(a, b):
    return lax.dot_general(a.astype(BF16), b.astype(BF16), (((1,), (1,)), ((), ())), preferred_element_type=F32)


def _dot_tn(a, b):
    return lax.dot_general(a.astype(BF16), b.astype(BF16), (((0,), (0,)), ((), ())), preferred_element_type=F32)


def _split2(x):
    hi = x.astype(BF16)
    return hi, (x - hi.astype(F32)).astype(BF16)


def _dot3(a, b, dims):
    ah, al = _split2(a)
    bh, bl = _split2(b)
    dn = (dims, ((), ()))
    return (lax.dot_general(ah, bh, dn, preferred_element_type=F32)
            + lax.dot_general(ah, bl, dn, preferred_element_type=F32)
            + lax.dot_general(al, bh, dn, preferred_element_type=F32))


def _dot_01(mask01, x):
    hi = x.astype(BF16)
    r1 = x - hi.astype(F32)
    mid = r1.astype(BF16)
    lo = (r1 - mid.astype(F32)).astype(BF16)
    m = mask01.astype(BF16)
    dn = (((1,), (0,)), ((), ()))
    return (lax.dot_general(m, hi, dn, preferred_element_type=F32)
            + lax.dot_general(m, mid, dn, preferred_element_type=F32)
            + lax.dot_general(m, lo, dn, preferred_element_type=F32))


def _sigmoid(z):
    return 1.0 / (1.0 + jnp.exp(-z))


_GELU_C = math.sqrt(2.0 / math.pi)


def _gelu_and_grad(z):
    inner = _GELU_C * (z + 0.044715 * z * z * z)
    t = jnp.tanh(inner)
    val = 0.5 * z * (1.0 + t)
    grad = 0.5 * (1.0 + t) + 0.5 * z * (1.0 - t * t) * _GELU_C * (1.0 + 3.0 * 0.044715 * z * z)
    return val, grad


def _row_tile(n, want=512):
    t = min(want, n)
    assert n % t == 0
    return t


def _matmul(name, operands, *, grid, in_specs, o_spec, out_shape, out_dtype, acc_shape, dims, has_res=False):
    nk = grid[2]

    def body(*refs):
        if has_res:
            a_ref, b_ref, r_ref, o_ref, acc = refs
        else:
            a_ref, b_ref, o_ref, acc = refs
            r_ref = None
        k = pl.program_id(2)

        @pl.when(k == 0)
        def _():
            acc[...] = jnp.zeros_like(acc)

        acc[...] += lax.dot_general(a_ref[...].astype(BF16), b_ref[...].astype(BF16), (dims, ((), ())),
                                    preferred_element_type=F32)

        @pl.when(k == nk - 1)
        def _():
            r = acc[...]
            if r_ref is not None:
                r = r + r_ref[...]
            o_ref[...] = r.astype(o_ref.dtype)

    return pl.pallas_call(
        body, name=name, grid=grid, in_specs=in_specs, out_specs=o_spec,
        out_shape=jax.ShapeDtypeStruct(out_shape, out_dtype),
        scratch_shapes=[pltpu.VMEM(acc_shape, F32)],
        compiler_params=_cp("parallel", "parallel", "arbitrary"),
    )(*operands)


NN = ((1,), (0,))
NT = ((1,), (1,))
TN = ((0,), (0,))


def _mm_cs(name, a, w, out_dtype):
    M, K = a.shape
    nq, _, wd = w.shape
    tm = _row_tile(M)
    return _matmul(name, (a, w), grid=(M // tm, nq, 1),
                   in_specs=[pl.BlockSpec((tm, K), lambda i, j, k: (i, 0)),
                             pl.BlockSpec((None, K, wd), lambda i, j, k: (j, 0, 0))],
                   o_spec=pl.BlockSpec((tm, wd), lambda i, j, k: (i, j)),
                   out_shape=(M, nq * wd), out_dtype=out_dtype, acc_shape=(tm, wd), dims=NN)


def _mm_rs(name, a, w, out_dtype, res=None, tn=512):
    M, K = a.shape
    N = w.shape[1]
    tm = _row_tile(M)
    tn = min(tn, N)
    ops = (a, w) if res is None else (a, w, res)
    in_specs = [pl.BlockSpec((tm, K), lambda i, j, k: (i, 0)),
                pl.BlockSpec((K, tn), lambda i, j, k: (0, j))]
    if res is not None:
        in_specs.append(pl.BlockSpec((tm, tn), lambda i, j, k: (i, j)))
    return _matmul(name, ops, grid=(M // tm, N // tn, 1), in_specs=in_specs,
                   o_spec=pl.BlockSpec((tm, tn), lambda i, j, k: (i, j)),
                   out_shape=(M, N), out_dtype=out_dtype, acc_shape=(tm, tn), dims=NN, has_res=res is not None)


def _mm_nt_rs(name, g, w, out_dtype, to):
    M, N = g.shape
    K = w.shape[0]
    tm = _row_tile(M)
    return _matmul(name, (g, w), grid=(M // tm, K // to, 1),
                   in_specs=[pl.BlockSpec((tm, N), lambda i, j, k: (i, 0)),
                             pl.BlockSpec((to, N), lambda i, j, k: (j, 0))],
                   o_spec=pl.BlockSpec((tm, to), lambda i, j, k: (i, j)),
                   out_shape=(M, K), out_dtype=out_dtype, acc_shape=(tm, to), dims=NT)


def _mm_nt_cs(name, g, w, out_dtype):
    M = g.shape[0]
    nq, K, wd = w.shape
    tm = _row_tile(M)
    return _matmul(name, (g, w), grid=(M // tm, 1, nq),
                   in_specs=[pl.BlockSpec((tm, wd), lambda i, j, k: (i, k)),
                             pl.BlockSpec((None, K, wd), lambda i, j, k: (k, 0, 0))],
                   o_spec=pl.BlockSpec((tm, K), lambda i, j, k: (i, 0)),
                   out_shape=(M, K), out_dtype=out_dtype, acc_shape=(tm, K), dims=NT)


def _mm_tn_rs(name, a, g, to, tn=512):
    T, M = a.shape
    N = g.shape[1]
    tt = _row_tile(T)
    tn = min(tn, N)
    return _matmul(name, (a, g), grid=(M // to, N // tn, T // tt),
                   in_specs=[pl.BlockSpec((tt, to), lambda i, j, k: (k, i)),
                             pl.BlockSpec((tt, tn), lambda i, j, k: (k, j))],
                   o_spec=pl.BlockSpec((to, tn), lambda i, j, k: (i, j)),
                   out_shape=(M, N), out_dtype=F32, acc_shape=(to, tn), dims=TN)


def _mm_tn_cs(name, a, g, nq, to):
    T, M = a.shape
    wd = g.shape[1] // nq
    tt = _row_tile(T)
    return _matmul(name, (a, g), grid=(M // to, nq, T // tt),
                   in_specs=[pl.BlockSpec((tt, to), lambda i, j, k: (k, i)),
                             pl.BlockSpec((tt, wd), lambda i, j, k: (k, j))],
                   o_spec=pl.BlockSpec((None, to, wd), lambda i, j, k: (j, i, 0)),
                   out_shape=(nq, M, wd), out_dtype=F32, acc_shape=(to, wd), dims=TN)


def _rms_fwd(name, x, g):
    T, D = x.shape
    tm = _row_tile(T)

    def body(x_ref, g_ref, o_ref):
        xv = x_ref[...]
        r = lax.rsqrt(jnp.mean(xv * xv, axis=-1, keepdims=True) + EPS)
        o_ref[...] = (xv * r * g_ref[...]).astype(o_ref.dtype)

    return pl.pallas_call(
        body, name=name, grid=(T // tm,),
        in_specs=[pl.BlockSpec((tm, D), lambda i: (i, 0)), pl.BlockSpec((1, D), lambda i: (0, 0))],
        out_specs=pl.BlockSpec((tm, D), lambda i: (i, 0)),
        out_shape=jax.ShapeDtypeStruct((T, D), BF16), compiler_params=_cp("parallel"),
    )(x, g)


def _rms_bwd(name, x, g, dh, dres):
    T, D = x.shape
    tm = _row_tile(T)
    has_res = dres is not None

    def body(*refs):
        if has_res:
            x_ref, g_ref, dh_ref, dr_ref, dx_ref, dg_ref = refs
        else:
            x_ref, g_ref, dh_ref, dx_ref, dg_ref = refs

        @pl.when(pl.program_id(0) == 0)
        def _():
            dg_ref[...] = jnp.zeros_like(dg_ref)

        xv = x_ref[...]
        r = lax.rsqrt(jnp.mean(xv * xv, axis=-1, keepdims=True) + EPS)
        n = xv * r
        dhv = dh_ref[...]
        dg_ref[...] += jnp.sum(dhv * n, axis=0, keepdims=True)
        dn = dhv * g_ref[...]
        dx = r * (dn - n * jnp.mean(dn * n, axis=-1, keepdims=True))
        if has_res:
            dx = dx + dr_ref[...]
        dx_ref[...] = dx

    row = pl.BlockSpec((tm, D), lambda i: (i, 0))
    vec = pl.BlockSpec((1, D), lambda i: (0, 0))
    ops = (x, g, dh, dres) if has_res else (x, g, dh)
    return pl.pallas_call(
        body, name=name, grid=(T // tm,),
        in_specs=[row, vec, row] + ([row] if has_res else []),
        out_specs=(row, vec),
        out_shape=(jax.ShapeDtypeStruct((T, D), F32), jax.ShapeDtypeStruct((1, D), F32)),
        compiler_params=_cp("arbitrary"),
    )(*ops)


def _loss_head(x2, tgt, g):
    T, D = x2.shape
    tm = _row_tile(T)

    def body(x_ref, t_ref, g_ref, dx_ref, dg_ref, loss_ref):
        @pl.when(pl.program_id(0) == 0)
        def _():
            dg_ref[...] = jnp.zeros_like(dg_ref)
            loss_ref[...] = jnp.zeros_like(loss_ref)

        xv = x_ref[...]
        gv = g_ref[...]
        r = lax.rsqrt(jnp.mean(xv * xv, axis=-1, keepdims=True) + EPS)
        n = xv * r
        diff = n * gv - t_ref[...]
        loss_ref[...] += 0.5 * jnp.sum(jnp.mean(diff * diff, axis=-1, keepdims=True))
        dy = diff * (1.0 / D)
        dg_ref[...] += jnp.sum(dy * n, axis=0, keepdims=True)
        dn = dy * gv
        dx_ref[...] = r * (dn - n * jnp.mean(dn * n, axis=-1, keepdims=True))

    row = pl.BlockSpec((tm, D), lambda i: (i, 0))
    vec = pl.BlockSpec((1, D), lambda i: (0, 0))
    return pl.pallas_call(
        body, name="loss_head", grid=(T // tm,),
        in_specs=[row, row, vec],
        out_specs=(row, vec, pl.BlockSpec((8, 128), lambda i: (0, 0))),
        out_shape=(jax.ShapeDtypeStruct((T, D), F32), jax.ShapeDtypeStruct((1, D), F32),
                   jax.ShapeDtypeStruct((8, 128), F32)),
        compiler_params=_cp("arbitrary"),
    )(x2, tgt, g)


def _gmlp_pieces(zu, zv, lng, lnb, ws_ref, bs_ref):
    u, du = _gelu_and_grad(zu)
    v, dv = _gelu_and_grad(zv)
    mu = jnp.mean(v, axis=-1, keepdims=True)
    vc = v - mu
    rstd = lax.rsqrt(jnp.mean(vc * vc, axis=-1, keepdims=True) + EPS)
    vhat = vc * rstd
    vn = vhat * lng + lnb
    row = lax.broadcasted_iota(jnp.int32, (GM_CHUNK, GM_CHUNK), 0)
    col = lax.broadcasted_iota(jnp.int32, (GM_CHUNK, GM_CHUNK), 1)
    tril = row >= col
    wms, mixed = [], []
    for g in range(GM_GROUPS):
        sl = slice(g * 128, (g + 1) * 128)
        wm = jnp.where(tril, ws_ref[g], 0.0)
        wms.append(wm)
        mixed.append(_dot(wm, vn[:, sl]) + bs_ref[g])
    return u, du, dv, rstd, vhat, vn, wms, mixed, tril


def _gmlp_fwd(proj, lng, lnb, ws, bs_col):
    T = proj.shape[0]
    n = T // GM_CHUNK

    def body(zu_ref, zv_ref, lng_ref, lnb_ref, ws_ref, bs_ref, o_ref):
        u, _, _, _, _, _, _, mixed, _ = _gmlp_pieces(zu_ref[...], zv_ref[...], lng_ref[...], lnb_ref[...],
                                                     ws_ref, bs_ref)
        for g in range(GM_GROUPS):
            sl = slice(g * 128, (g + 1) * 128)
            o_ref[:, sl] = (u[:, sl] * mixed[g]).astype(o_ref.dtype)

    vec = pl.BlockSpec((1, GM_WIDTH), lambda i: (0, 0))
    return pl.pallas_call(
        body, name="gmlp_fwd", grid=(n,),
        in_specs=[pl.BlockSpec((GM_CHUNK, 512), lambda i: (i, COL_ZU)),
                  pl.BlockSpec((GM_CHUNK, 512), lambda i: (i, COL_ZV)),
                  vec, vec,
                  pl.BlockSpec((GM_GROUPS, 128, 128), lambda i: (0, 0, 0)),
                  pl.BlockSpec((GM_GROUPS, 128, 1), lambda i: (0, 0, 0))],
        out_specs=pl.BlockSpec((GM_CHUNK, 512), lambda i: (i, 0)),
        out_shape=jax.ShapeDtypeStruct((T, GM_WIDTH), BF16), compiler_params=_cp("parallel"),
    )(proj, proj, lng, lnb, ws, bs_col)


def _gmlp_bwd(proj, d_out, lng, lnb, ws, bs_col):
    T = proj.shape[0]
    n = T // GM_CHUNK

    def body(zu_ref, zv_ref, do_ref, lng_ref, lnb_ref, ws_ref, bs_ref,
             dzu_ref, dzv_ref, dws_ref, dbs_ref, dlng_ref, dlnb_ref, dm_acc):
        i = pl.program_id(0)

        @pl.when(i == 0)
        def _():
            dws_ref[...] = jnp.zeros_like(dws_ref)
            dlng_ref[...] = jnp.zeros_like(dlng_ref)
            dlnb_ref[...] = jnp.zeros_like(dlnb_ref)
            dm_acc[...] = jnp.zeros_like(dm_acc)

        lng_v = lng_ref[...]
        u, du, dv, rstd, vhat, vn, wms, mixed, tril = _gmlp_pieces(zu_ref[...], zv_ref[...], lng_v, lnb_ref[...],
                                                                  ws_ref, bs_ref)
        do = do_ref[...]
        dvn_parts = []
        for g in range(GM_GROUPS):
            sl = slice(g * 128, (g + 1) * 128)
            dog = do[:, sl]
            dzu_ref[:, sl] = (dog * mixed[g] * du[:, sl]).astype(dzu_ref.dtype)
            dmix = dog * u[:, sl]
            dm_acc[:, sl] += dmix
            dws_ref[g] += jnp.where(tril, _dot_nt(dmix, vn[:, sl]), 0.0)
            dvn_parts.append(_dot_tn(wms[g], dmix))
        dvn = jnp.concatenate(dvn_parts, axis=1)
        dlng_ref[...] += jnp.sum(dvn * vhat, axis=0, keepdims=True)
        dlnb_ref[...] += jnp.sum(dvn, axis=0, keepdims=True)
        dvh = dvn * lng_v
        dvv = rstd * (dvh - jnp.mean(dvh, axis=-1, keepdims=True)
                      - vhat * jnp.mean(dvh * vhat, axis=-1, keepdims=True))
        dzv_ref[...] = (dvv * dv).astype(dzv_ref.dtype)

        @pl.when(i == n - 1)
        def _():
            for g in range(GM_GROUPS):
                dbs_ref[g] = jnp.sum(dm_acc[:, g * 128:(g + 1) * 128], axis=1, keepdims=True)

    vec = pl.BlockSpec((1, GM_WIDTH), lambda i: (0, 0))
    wsp = pl.BlockSpec((GM_GROUPS, 128, 128), lambda i: (0, 0, 0))
    bsp = pl.BlockSpec((GM_GROUPS, 128, 1), lambda i: (0, 0, 0))
    tile = pl.BlockSpec((GM_CHUNK, 512), lambda i: (i, 0))
    return pl.pallas_call(
        body, name="gmlp_bwd", grid=(n,),
        in_specs=[pl.BlockSpec((GM_CHUNK, 512), lambda i: (i, COL_ZU)),
                  pl.BlockSpec((GM_CHUNK, 512), lambda i: (i, COL_ZV)),
                  tile, vec, vec, wsp, bsp],
        out_specs=(tile, tile, wsp, bsp, vec, vec),
        out_shape=(jax.ShapeDtypeStruct((T, GM_WIDTH), BF16), jax.ShapeDtypeStruct((T, GM_WIDTH), BF16),
                   jax.ShapeDtypeStruct((GM_GROUPS, 128, 128), F32), jax.ShapeDtypeStruct((GM_GROUPS, 128, 1), F32),
                   jax.ShapeDtypeStruct((1, GM_WIDTH), F32), jax.ShapeDtypeStruct((1, GM_WIDTH), F32)),
        scratch_shapes=[pltpu.VMEM((GM_CHUNK, GM_WIDTH), F32)],
        compiler_params=_cp("arbitrary"),
    )(proj, proj, d_out, lng, lnb, ws, bs_col)


def _hgrn_lower_bound(lbl):
    return 1.0 / (1.0 + jnp.exp(lbl[1:2, :] - lbl[0:1, :]))


def _hgrn_gates(hq, hf, lb):
    sg = _sigmoid(hf)
    fg = lb + (1.0 - lb) * sg
    sq = _sigmoid(hq)
    C = HG_CHUNK
    row = lax.broadcasted_iota(jnp.int32, (C, C), 0)
    col = lax.broadcasted_iota(jnp.int32, (C, C), 1)
    tril = row >= col
    logf = jnp.log(fg)
    a = _dot_01(tril, logf)
    a_last = jnp.sum(logf, axis=0, keepdims=True)
    first_half = lax.broadcasted_iota(jnp.int32, logf.shape, 0) < (C // 2)
    a_mid = jnp.sum(jnp.where(first_half, logf, 0.0), axis=0, keepdims=True)
    return sg, fg, sq, tril, a, a_last, a_mid


def _hgrn_fwd(proj, lbl, gh, B, S):
    T = B * S
    C = HG_CHUNK
    NC = S // C
    W = HG_HEADS * HG_DIM

    def body(q_ref, f_ref, i_ref, g_ref, lbl_ref, gh_ref, o_ref, bo_ref, st_ref, state):
        @pl.when(pl.program_id(1) == 0)
        def _():
            state[...] = jnp.zeros_like(state)

        lb_all = _hgrn_lower_bound(lbl_ref[...])
        ghv = gh_ref[...]
        for h in range(HG_HEADS):
            sl = slice(h * 128, (h + 1) * 128)
            hq = q_ref[:, sl]
            sg, fg, sq, tril, a, a_last, a_mid = _hgrn_gates(hq, f_ref[:, sl], lb_all[:, sl])
            k = 1.0 - fg
            q = hq * sq
            v = i_ref[:, sl]
            qe = q * jnp.exp(a)
            qi = q * jnp.exp(a - a_mid)
            ki = k * jnp.exp(a_mid - a)
            kl = k * jnp.exp(a_last - a)
            p = jnp.where(tril, _dot_nt(qi, ki), 0.0)
            st = state[h]
            st_ref[h] = st
            o = _dot_nt(qe, st) + _dot(p, v)
            state[h] = st * jnp.exp(a_last) + _dot_tn(v, kl)
            o_ref[:, sl] = o
            r = lax.rsqrt(jnp.mean(o * o, axis=-1, keepdims=True) + EPS)
            hg = g_ref[:, sl]
            bo_ref[:, sl] = (o * r * ghv * (hg * _sigmoid(hg))).astype(bo_ref.dtype)

    def col(cb):
        return pl.BlockSpec((C, 512), lambda b, c: (b * NC + c, cb))

    tile = pl.BlockSpec((C, W), lambda b, c: (b * NC + c, 0))
    return pl.pallas_call(
        body, name="hgrn_fwd", grid=(B, NC),
        in_specs=[col(COL_HQ), col(COL_HF), col(COL_HI), col(COL_HG),
                  pl.BlockSpec((2, W), lambda b, c: (0, 0)), pl.BlockSpec((1, HG_DIM), lambda b, c: (0, 0))],
        out_specs=(tile, tile, pl.BlockSpec((None, HG_HEADS, 128, 128), lambda b, c: (b * NC + c, 0, 0, 0))),
        out_shape=(jax.ShapeDtypeStruct((T, W), F32), jax.ShapeDtypeStruct((T, W), BF16),
                   jax.ShapeDtypeStruct((B * NC, HG_HEADS, 128, 128), F32)),
        scratch_shapes=[pltpu.VMEM((HG_HEADS, 128, 128), F32)],
        compiler_params=_cp("parallel", "arbitrary"),
    )(proj, proj, proj, proj, lbl, gh)


def _hgrn_bwd(proj, o_saved, states, d_out, lbl, gh, B, S):
    T = B * S
    C = HG_CHUNK
    NC = S // C
    W = HG_HEADS * HG_DIM

    def body(q_ref, f_ref, i_ref, g_ref, o_ref, st_ref, do_ref, lbl_ref, gh_ref,
             dq_ref, df_ref, di_ref, dg_ref, dlbl_ref, dgh_ref, dstate, dlb_acc):
        b = pl.program_id(0)
        c = pl.program_id(1)

        @pl.when(c == 0)
        def _():
            dstate[...] = jnp.zeros_like(dstate)

        @pl.when((b == 0) & (c == 0))
        def _():
            dgh_ref[...] = jnp.zeros_like(dgh_ref)
            dlb_acc[...] = jnp.zeros_like(dlb_acc)

        lbl_v = lbl_ref[...]
        lb_all = _hgrn_lower_bound(lbl_v)
        ghv = gh_ref[...]
        row = lax.broadcasted_iota(jnp.int32, (C, C), 0)
        colm = lax.broadcasted_iota(jnp.int32, (C, C), 1)
        triu = colm >= row
        for h in range(HG_HEADS):
            sl = slice(h * 128, (h + 1) * 128)
            hq = q_ref[:, sl]
            lb = lb_all[:, sl]
            sg, fg, sq, tril, a, a_last, a_mid = _hgrn_gates(hq, f_ref[:, sl], lb)
            k = 1.0 - fg
            q = hq * sq
            v = i_ref[:, sl]
            ea = jnp.exp(a)
            ei = jnp.exp(a - a_mid)
            eki = jnp.exp(a_mid - a)
            ekl = jnp.exp(a_last - a)
            e_last = jnp.exp(a_last)
            qe = q * ea
            qi = q * ei
            ki = k * eki
            kl = k * ekl
            p = jnp.where(tril, _dot_nt(qi, ki), 0.0)
            st = st_ref[h]
            o = o_ref[:, sl]
            hg = g_ref[:, sl]
            sgg = _sigmoid(hg)
            r = lax.rsqrt(jnp.mean(o * o, axis=-1, keepdims=True) + EPS)
            n = o * r
            dbo = do_ref[:, sl]
            dg_ref[:, sl] = (dbo * n * ghv * (sgg * (1.0 + hg * (1.0 - sgg)))).astype(dg_ref.dtype)
            don = dbo * (hg * sgg)
            dgh_ref[...] += jnp.sum(don * n, axis=0, keepdims=True)
            dn = don * ghv
            d_o = r * (dn - n * jnp.mean(dn * n, axis=-1, keepdims=True))
            dst = dstate[h]
            dp = jnp.where(tril, _dot3(d_o, v, NT), 0.0)
            d_qe = _dot3(d_o, st, NN)
            d_qi = _dot3(dp, ki, NN)
            d_ki = _dot3(dp, qi, TN)
            d_kl = _dot3(v, dst, NN)
            dv = _dot_tn(p, d_o) + _dot_nt(kl, dst)
            dstate[h] = dst * e_last + _dot3(d_o, qe, TN)
            d_a_last = jnp.sum(dst * st, axis=0, keepdims=True) * e_last + jnp.sum(d_kl * kl, axis=0, keepdims=True)
            dq = d_qe * ea + d_qi * ei
            dk = d_ki * eki + d_kl * ekl
            da = d_qe * qe + d_qi * qi - d_ki * ki - d_kl * kl
            dlogf = _dot_01(triu, da) + d_a_last
            dfg = dlogf / fg - dk
            df_ref[:, sl] = (dfg * (1.0 - lb) * sg * (1.0 - sg)).astype(df_ref.dtype)
            dlb_acc[:, sl] += jnp.sum(dfg * (1.0 - sg), axis=0, keepdims=True)
            dq_ref[:, sl] = (dq * (sq * (1.0 + hq * (1.0 - sq)))).astype(dq_ref.dtype)
            di_ref[:, sl] = dv.astype(di_ref.dtype)

        @pl.when((b == B - 1) & (c == NC - 1))
        def _():
            dlb = dlb_acc[...]
            p0 = lb_all
            first = lax.broadcasted_iota(jnp.int32, (2, W), 0) == 0
            dlbl_ref[...] = jnp.where(first, dlb * p0 * (1.0 - p0), -dlb * p0 * (1.0 - p0))

    def col(cb):
        return pl.BlockSpec((C, 512), lambda b, c: (b * NC + NC - 1 - c, cb))

    tile = pl.BlockSpec((C, W), lambda b, c: (b * NC + NC - 1 - c, 0))
    return pl.pallas_call(
        body, name="hgrn_bwd", grid=(B, NC),
        in_specs=[col(COL_HQ), col(COL_HF), col(COL_HI), col(COL_HG), tile,
                  pl.BlockSpec((None, HG_HEADS, 128, 128), lambda b, c: (b * NC + NC - 1 - c, 0, 0, 0)),
                  tile, pl.BlockSpec((2, W), lambda b, c: (0, 0)), pl.BlockSpec((1, HG_DIM), lambda b, c: (0, 0))],
        out_specs=(tile, tile, tile, tile,
                   pl.BlockSpec((2, W), lambda b, c: (0, 0)), pl.BlockSpec((1, HG_DIM), lambda b, c: (0, 0))),
        out_shape=(jax.ShapeDtypeStruct((T, W), BF16),) * 4
        + (jax.ShapeDtypeStruct((2, W), F32), jax.ShapeDtypeStruct((1, HG_DIM), F32)),
        scratch_shapes=[pltpu.VMEM((HG_HEADS, 128, 128), F32), pltpu.VMEM((1, W), F32)],
        compiler_params=_cp("arbitrary", "arbitrary"),
    )(proj, proj, proj, proj, o_saved, states, d_out, lbl, gh)


_XA_SCALE = XA_DIM ** -0.5


def _attn_probs(qh, kh):
    s = _dot_nt(qh, kh) * _XA_SCALE
    e = jnp.exp(s - jnp.max(s, axis=-1, keepdims=True))
    return e / jnp.sum(e, axis=-1, keepdims=True)


def _attn_fwd(proj, kv, B, S):
    T = B * S
    tq = _row_tile(S)
    nq = S // tq
    W = XA_HEADS * XA_DIM

    def body(q_ref, kv_ref, o_ref):
        for h in range(XA_HEADS):
            sl = slice(h * 128, (h + 1) * 128)
            p = _attn_probs(q_ref[:, sl], kv_ref[:, sl])
            o_ref[:, sl] = _dot(p, kv_ref[:, W + h * 128:W + (h + 1) * 128]).astype(o_ref.dtype)

    return pl.pallas_call(
        body, name="attn_fwd", grid=(B, nq),
        in_specs=[pl.BlockSpec((tq, 512), lambda b, i: (b * nq + i, COL_XQ)),
                  pl.BlockSpec((MEM_LEN, 2 * W), lambda b, i: (b, 0))],
        out_specs=pl.BlockSpec((tq, W), lambda b, i: (b * nq + i, 0)),
        out_shape=jax.ShapeDtypeStruct((T, W), BF16), compiler_params=_cp("parallel", "parallel"),
    )(proj, kv)


def _attn_bwd(proj, kv, d_out, B, S):
    T = B * S
    tq = _row_tile(S)
    nq = S // tq
    W = XA_HEADS * XA_DIM

    def body(q_ref, kv_ref, do_ref, dq_ref, dkv_ref):
        @pl.when(pl.program_id(1) == 0)
        def _():
            dkv_ref[...] = jnp.zeros_like(dkv_ref)

        for h in range(XA_HEADS):
            sl = slice(h * 128, (h + 1) * 128)
            slv = slice(W + h * 128, W + (h + 1) * 128)
            qh = q_ref[:, sl]
            kh = kv_ref[:, sl]
            p = _attn_probs(qh, kh)
            dc = do_ref[:, sl]
            dp = _dot_nt(dc, kv_ref[:, slv])
            ds = p * (dp - jnp.sum(dp * p, axis=-1, keepdims=True)) * _XA_SCALE
            dq_ref[:, sl] = _dot(ds, kh).astype(dq_ref.dtype)
            dkv_ref[:, sl] += _dot_tn(ds, qh)
            dkv_ref[:, slv] += _dot_tn(p, dc)

    kvspec = pl.BlockSpec((MEM_LEN, 2 * W), lambda b, i: (b, 0))
    tile = pl.BlockSpec((tq, W), lambda b, i: (b * nq + i, 0))
    return pl.pallas_call(
        body, name="attn_bwd", grid=(B, nq),
        in_specs=[pl.BlockSpec((tq, 512), lambda b, i: (b * nq + i, COL_XQ)), kvspec, tile],
        out_specs=(tile, kvspec),
        out_shape=(jax.ShapeDtypeStruct((T, W), BF16), jax.ShapeDtypeStruct((B * MEM_LEN, 2 * W), F32)),
        compiler_params=_cp("parallel", "arbitrary"),
    )(proj, kv, d_out)


_MERGE_TN = 256


def _gate_specs(tm):
    base = COL_GATE0 // _MERGE_TN
    per = D_MODEL // _MERGE_TN
    return [pl.BlockSpec((tm, _MERGE_TN), functools.partial(lambda i, j, n: (i, base + per * n + j), n=n))
            for n in range(3)]


def _merge_fwd(a_out, b_out, c_out, wb, proj):
    T = a_out.shape[0]
    tm = _row_tile(T)
    tn = _MERGE_TN

    def body(a_ref, b_ref, c_ref, w_ref, g0_ref, g1_ref, g2_ref, m_ref, up_ref):
        acc = None
        for n, (br, gr) in enumerate(((a_ref, g0_ref), (b_ref, g1_ref), (c_ref, g2_ref))):
            up = _dot(br[...], w_ref[n * BR_WIDTH:(n + 1) * BR_WIDTH, :])
            up_ref[n] = up.astype(up_ref.dtype)
            term = _sigmoid(gr[...]) * up
            acc = term if acc is None else acc + term
        m_ref[...] = acc.astype(m_ref.dtype)

    br_spec = pl.BlockSpec((tm, BR_WIDTH), lambda i, j: (i, 0))
    return pl.pallas_call(
        body, name="merge_fwd", grid=(T // tm, D_MODEL // tn),
        in_specs=[br_spec, br_spec, br_spec,
                  pl.BlockSpec((None, 3 * BR_WIDTH, tn), lambda i, j: (j, 0, 0))] + _gate_specs(tm),
        out_specs=(pl.BlockSpec((tm, tn), lambda i, j: (i, j)), pl.BlockSpec((3, tm, tn), lambda i, j: (0, i, j))),
        out_shape=(jax.ShapeDtypeStruct((T, D_MODEL), BF16), jax.ShapeDtypeStruct((3, T, D_MODEL), BF16)),
        compiler_params=_cp("parallel", "parallel"),
    )(a_out, b_out, c_out, wb, proj, proj, proj)


def _merge_bwd(d_merged, ups, proj):
    T = d_merged.shape[0]
    tm = _row_tile(T)
    tn = _MERGE_TN

    def body(dm_ref, up_ref, g0_ref, g1_ref, g2_ref, dup_ref, dg0_ref, dg1_ref, dg2_ref):
        dm = dm_ref[...]
        for n, (gr, dgr) in enumerate(((g0_ref, dg0_ref), (g1_ref, dg1_ref), (g2_ref, dg2_ref))):
            gate = _sigmoid(gr[...])
            dup_ref[n] = (dm * gate).astype(dup_ref.dtype)
            dgr[...] = (dm * up_ref[n].astype(F32) * gate * (1.0 - gate)).astype(dgr.dtype)

    tile = pl.BlockSpec((tm, tn), lambda i, j: (i, j))
    tile3 = pl.BlockSpec((3, tm, tn), lambda i, j: (0, i, j))
    return pl.pallas_call(
        body, name="merge_bwd", grid=(T // tm, D_MODEL // tn),
        in_specs=[tile, tile3] + _gate_specs(tm),
        out_specs=(tile3, tile, tile, tile),
        out_shape=(jax.ShapeDtypeStruct((3, T, D_MODEL), BF16),) + (jax.ShapeDtypeStruct((T, D_MODEL), BF16),) * 3,
        compiler_params=_cp("parallel", "parallel"),
    )(d_merged, ups, proj, proj, proj)


_CONV_TF = 256
_HALO = 8


def _conv_fwd(ab, cw, cb, B, S):
    T = B * S
    ts = _row_tile(S)
    tf = _CONV_TF
    nb = D_FF // tf
    tps = S // ts
    hb = ts // _HALO

    def body(a_ref, p_ref, b_ref, w_ref, cb_ref, o_ref):
        start = (pl.program_id(0) % tps) == 0
        prev = jnp.where(start, 0.0, p_ref[...])
        ext = jnp.concatenate([prev, a_ref[...]], axis=0)
        a1 = pltpu.roll(ext, 1, 0)[_HALO:, :]
        a2 = pltpu.roll(ext, 2, 0)[_HALO:, :]
        ac = cb_ref[...] + w_ref[0] * a2 + w_ref[1] * a1 + w_ref[2] * a_ref[...]
        o_ref[...] = (ac * _sigmoid(ac) * b_ref[...]).astype(o_ref.dtype)

    return pl.pallas_call(
        body, name="conv_fwd", grid=(T // ts, nb),
        in_specs=[pl.BlockSpec((ts, tf), lambda i, j: (i, j)),
                  pl.BlockSpec((_HALO, tf), lambda i, j: (jnp.maximum(i * hb - 1, 0), j)),
                  pl.BlockSpec((ts, tf), lambda i, j: (i, j + nb)),
                  pl.BlockSpec((3, 1, tf), lambda i, j: (0, 0, j)),
                  pl.BlockSpec((1, tf), lambda i, j: (0, j))],
        out_specs=pl.BlockSpec((ts, tf), lambda i, j: (i, j)),
        out_shape=jax.ShapeDtypeStruct((T, D_FF), BF16), compiler_params=_cp("parallel", "parallel"),
    )(ab, ab, ab, cw, cb)


def _conv_bwd(ab, d_ff, cw, cb, B, S):
    T = B * S
    ts = _row_tile(S)
    tf = _CONV_TF
    nb = D_FF // tf
    tps = S // ts
    hb = ts // _HALO
    last_h = T // _HALO - 1
    n_ext = ts + _HALO

    def body(a_ref, ap_ref, an_ref, b_ref, bn_ref, d_ref, dn_ref, w_ref, cb_ref,
             da_ref, db_ref, dw_ref, dcb_ref):
        i = pl.program_id(1)

        @pl.when(i == 0)
        def _():
            dw_ref[...] = jnp.zeros_like(dw_ref)
            dcb_ref[...] = jnp.zeros_like(dcb_ref)

        start = (i % tps) == 0
        end = (i % tps) == tps - 1
        a = a_ref[...]
        ext = jnp.concatenate([jnp.where(start, 0.0, ap_ref[...]), a, an_ref[...]], axis=0)
        r1 = pltpu.roll(ext, 1, 0)[_HALO:, :]
        r2 = pltpu.roll(ext, 2, 0)[_HALO:, :]
        ac = cb_ref[...] + w_ref[0] * r2 + w_ref[1] * r1 + w_ref[2] * ext[_HALO:, :]
        sg = _sigmoid(ac)
        d_e = jnp.concatenate([d_ref[...], jnp.where(end, 0.0, dn_ref[...])], axis=0)
        b_e = jnp.concatenate([b_ref[...], bn_ref[...]], axis=0)
        db_ref[...] = (d_e[:ts, :] * (ac * sg)[:ts, :]).astype(db_ref.dtype)
        dac = d_e * b_e * sg * (1.0 + ac * (1.0 - sg))
        u1 = pltpu.roll(dac, n_ext - 1, 0)[:ts, :]
        u2 = pltpu.roll(dac, n_ext - 2, 0)[:ts, :]
        dac0 = dac[:ts, :]
        da_ref[...] = (w_ref[2] * dac0 + w_ref[1] * u1 + w_ref[0] * u2).astype(da_ref.dtype)
        dcb_ref[...] += jnp.sum(dac0, axis=0, keepdims=True)
        dw_ref[2] += jnp.sum(dac0 * a, axis=0, keepdims=True)
        dw_ref[1] += jnp.sum(dac0 * r1[:ts, :], axis=0, keepdims=True)
        dw_ref[0] += jnp.sum(dac0 * r2[:ts, :], axis=0, keepdims=True)

    def cur(off):
        return pl.BlockSpec((ts, tf), lambda j, i: (i, j + off))

    def nxt(off):
        return pl.BlockSpec((_HALO, tf), lambda j, i: (jnp.minimum((i + 1) * hb, last_h), j + off))

    return pl.pallas_call(
        body, name="conv_bwd", grid=(nb, T // ts),
        in_specs=[cur(0), pl.BlockSpec((_HALO, tf), lambda j, i: (jnp.maximum(i * hb - 1, 0), j)), nxt(0),
                  cur(nb), nxt(nb), cur(0), nxt(0),
                  pl.BlockSpec((3, 1, tf), lambda j, i: (0, 0, j)), pl.BlockSpec((1, tf), lambda j, i: (0, j))],
        out_specs=(cur(0), cur(0), pl.BlockSpec((3, 1, tf), lambda j, i: (0, 0, j)),
                   pl.BlockSpec((1, tf), lambda j, i: (0, j))),
        out_shape=(jax.ShapeDtypeStruct((T, D_FF), BF16), jax.ShapeDtypeStruct((T, D_FF), BF16),
                   jax.ShapeDtypeStruct((3, 1, D_FF), F32), jax.ShapeDtypeStruct((1, D_FF), F32)),
        compiler_params=_cp("parallel", "arbitrary"),
    )(ab, ab, ab, ab, ab, d_ff, d_ff, cw, cb)


def _local_step(x, mem, tgt, p, B, S):
    g = {}
    h = _rms_fwd("norm1", x, p["norm1_g"])
    proj = _mm_cs("in_proj", h, p["w_in"], F32)
    a_out = _gmlp_fwd(proj, p["ln_v_g"], p["ln_v_b"], p["w_spatial"], p["b_spatial"])
    o_h, b_out, states = _hgrn_fwd(proj, p["lb_logits"], p["hgrn_norm_g"], B, S)
    memn = _rms_fwd("mem_norm", mem, p["mem_norm_g"])
    kv = _mm_rs("mem_kv", memn, p["w_mem_kv"], F32)
    c_out = _attn_fwd(proj, kv, B, S)
    merged, ups = _merge_fwd(a_out, b_out, c_out, p["w_branch"], proj)
    x1 = _mm_rs("out_proj", merged, p["w_out"], F32, res=x)
    h2 = _rms_fwd("norm2", x1, p["norm2_g"])
    ab = _mm_cs("up_proj", h2, p["w_up"], F32)
    ff = _conv_fwd(ab, p["conv_w"], p["conv_b"], B, S)
    x2 = _mm_rs("down_proj", ff, p["w_down"], F32, res=x1)
    dx2, g["final_g"], loss = _loss_head(x2, tgt, p["final_g"])

    d_ff = _mm_nt_rs("d_ff", dx2, p["w_down"], F32, to=D_FF // 2)
    g["w_down"] = _mm_tn_rs("g_w_down", ff, dx2, to=D_FF // 2)
    d_a, d_b, g["conv_w"], g["conv_b"] = _conv_bwd(ab, d_ff, p["conv_w"], p["conv_b"], B, S)
    d_ab = jnp.concatenate([d_a, d_b], axis=1)
    d_h2 = _mm_nt_cs("d_h2", d_ab, p["w_up"], F32)
    g["w_up"] = _mm_tn_cs("g_w_up", h2, d_ab, N_CHIPS, to=512)
    d_x1, g["norm2_g"] = _rms_bwd("norm2_bwd", x1, p["norm2_g"], d_h2, dx2)
    d_merged = _mm_nt_rs("d_merged", d_x1, p["w_out"], F32, to=512)
    g["w_out"] = _mm_tn_rs("g_w_out", merged, d_x1, to=512)
    d_ups, d_g0, d_g1, d_g2 = _merge_bwd(d_merged, ups, proj)

    T = x.shape[0]
    tm = _row_tile(T)
    d_br, g_wb = [], []
    for n, br in enumerate((a_out, b_out, c_out)):
        d_br.append(_matmul(
            "d_branch%d" % n, (d_ups, p["w_branch"]), grid=(T // tm, 1, N_CHIPS),
            in_specs=[pl.BlockSpec((None, tm, _MERGE_TN), functools.partial(lambda i, j, k, n: (n, i, k), n=n)),
                      pl.BlockSpec((None, BR_WIDTH, _MERGE_TN), functools.partial(lambda i, j, k, n: (k, n, 0), n=n))],
            o_spec=pl.BlockSpec((tm, BR_WIDTH), lambda i, j, k: (i, 0)),
            out_shape=(T, BR_WIDTH), out_dtype=F32, acc_shape=(tm, BR_WIDTH), dims=NT))
        g_wb.append(_matmul(
            "g_w_branch%d" % n, (br, d_ups), grid=(1, N_CHIPS, T // tm),
            in_specs=[pl.BlockSpec((tm, BR_WIDTH), lambda i, j, k: (k, 0)),
                      pl.BlockSpec((None, tm, _MERGE_TN), functools.partial(lambda i, j, k, n: (n, k, j), n=n))],
            o_spec=pl.BlockSpec((None, BR_WIDTH, _MERGE_TN), lambda i, j, k: (j, 0, 0)),
            out_shape=(N_CHIPS, BR_WIDTH, _MERGE_TN), out_dtype=F32, acc_shape=(BR_WIDTH, _MERGE_TN), dims=TN))
    g["w_branch"] = jnp.concatenate(g_wb, axis=1)

    d_zu, d_zv, g["w_spatial"], g["b_spatial"], g["ln_v_g"], g["ln_v_b"] = _gmlp_bwd(
        proj, d_br[0], p["ln_v_g"], p["ln_v_b"], p["w_spatial"], p["b_spatial"])
    d_hq, d_hf, d_hi, d_hg, g["lb_logits"], g["hgrn_norm_g"] = _hgrn_bwd(
        proj, o_h, states, d_br[1], p["lb_logits"], p["hgrn_norm_g"], B, S)
    d_xq, d_kv = _attn_bwd(proj, kv, d_br[2], B, S)
    g["w_mem_kv"] = _mm_tn_rs("g_w_mem_kv", memn, d_kv, to=512)
    d_memn = _mm_nt_rs("d_memn", d_kv, p["w_mem_kv"], F32, to=512)
    _, g["mem_norm_g"] = _rms_bwd("mem_norm_bwd", mem, p["mem_norm_g"], d_memn, None)
    d_proj = jnp.concatenate([d_zu, d_zv, d_hq, d_hf, d_hi, d_hg, d_xq, d_g0, d_g1, d_g2], axis=1)
    d_h = _mm_nt_cs("d_h", d_proj, p["w_in"], F32)
    g["w_in"] = _mm_tn_cs("g_w_in", h, d_proj, N_CHIPS, to=512)
    grad_x, g["norm1_g"] = _rms_bwd("norm1_bwd", x, p["norm1_g"], d_h, d_x1)
    return loss[0, 0], grad_x, g


ANY = pl.BlockSpec(memory_space=pl.ANY)


def _place():
    x, y, c = lax.axis_index("x"), lax.axis_index("y"), lax.axis_index("c")
    other_chips = [(1 - x, y), (x, 1 - y), (1 - x, 1 - y)]
    return x, y, c, other_chips


def _remote(src, dst, send_sem, recv_sem, dev):
    return pltpu.make_async_remote_copy(src_ref=src, dst_ref=dst, send_sem=send_sem, recv_sem=recv_sem,
                                        device_id=dev, device_id_type=MESH_ID)


def _all_gather_weights(shards, conv_w):
    n = len(shards)
    halves = [s.shape[0] // 2 for s in shards]

    def body(*refs):
        ins, cin = refs[:n], refs[n]
        outs, cout = refs[n + 1:2 * n + 1], refs[2 * n + 1]
        send, recv, fsend, frecv, csend, crecv, lsem = refs[2 * n + 2:]
        x, y, c, chips = _place()
        q = 2 * x + y
        sib = (x, y, 1 - c)

        def half(a, cc):
            return pl.ds(cc * halves[a], halves[a])

        local = [pltpu.make_async_copy(ins[a], outs[a].at[q], lsem.at[a]) for a in range(n)]
        local.append(pltpu.make_async_copy(cin, cout.at[q], lsem.at[n]))
        for cp in local:
            cp.start()
        sends = []
        for j, (px, py) in enumerate(chips):
            for a in range(n):
                sends.append(_remote(ins[a].at[half(a, c), :], outs[a].at[q, half(a, c), :],
                                     send.at[3 * a + j], recv.at[3 * a + j], (px, py, c)))
            sends.append(_remote(cin, cout.at[q], csend.at[j], crecv.at[j], (px, py, c)))
        for cp in sends:
            cp.start()
        passed = []
        for j, (px, py) in enumerate(chips):
            qj = 2 * px + py
            for a in range(n):
                piece = outs[a].at[qj, half(a, c), :]
                _remote(piece, piece, send.at[3 * a + j], recv.at[3 * a + j], (px, py, c)).wait_recv()
                fw = _remote(piece, piece, fsend.at[3 * a + j], frecv.at[3 * a + j], sib)
                fw.start()
                passed.append(fw)
            _remote(cin, cout.at[qj], csend.at[j], crecv.at[j], (px, py, c)).wait_recv()
        for j, (px, py) in enumerate(chips):
            qj = 2 * px + py
            for a in range(n):
                piece = outs[a].at[qj, half(a, 1 - c), :]
                _remote(piece, piece, fsend.at[3 * a + j], frecv.at[3 * a + j], sib).wait_recv()
        for cp in sends + passed:
            cp.wait_send()
        for cp in local:
            cp.wait()

    out_shape = tuple(jax.ShapeDtypeStruct((N_CHIPS,) + s.shape, s.dtype) for s in shards)
    out_shape += (jax.ShapeDtypeStruct((N_CHIPS,) + conv_w.shape, conv_w.dtype),)
    res = pl.pallas_call(
        body, name="all_gather_weights", in_specs=[ANY] * (n + 1), out_specs=(ANY,) * (n + 1), out_shape=out_shape,
        scratch_shapes=[pltpu.SemaphoreType.DMA((3 * n,)), pltpu.SemaphoreType.DMA((3 * n,)),
                        pltpu.SemaphoreType.DMA((3 * n,)), pltpu.SemaphoreType.DMA((3 * n,)),
                        pltpu.SemaphoreType.DMA((3,)), pltpu.SemaphoreType.DMA((3,)),
                        pltpu.SemaphoreType.DMA((n + 1,))],
    )(*shards, conv_w)
    return res[:n], res[n]


def _rs_to_sibling(grads):
    n = len(grads)

    def body(*refs):
        ins, outs = refs[:n], refs[n:2 * n]
        send, recv = refs[2 * n:]
        x, y, c, _ = _place()
        cps = []
        for a in range(n):
            hr = grads[a].shape[1] // 2
            cps.append(_remote(ins[a].at[:, pl.ds((1 - c) * hr, hr), :], outs[a], send.at[a], recv.at[a],
                               (x, y, 1 - c)))
        for cp in cps:
            cp.start()
        for cp in cps:
            cp.wait()

    out_shape = tuple(jax.ShapeDtypeStruct((g.shape[0], g.shape[1] // 2, g.shape[2]), g.dtype) for g in grads)
    return pl.pallas_call(
        body, name="rs_to_sibling", in_specs=[ANY] * n, out_specs=(ANY,) * n, out_shape=out_shape,
        scratch_shapes=[pltpu.SemaphoreType.DMA((n,)), pltpu.SemaphoreType.DMA((n,))],
    )(*grads)


def _rs_to_owner(parts):
    n = len(parts)

    def body(*refs):
        ins, outs = refs[:n], refs[n:2 * n]
        send, recv = refs[2 * n:]
        x, y, c, chips = _place()
        cps = []
        for j, (px, py) in enumerate(chips):
            for a in range(n):
                cps.append(_remote(ins[a].at[2 * px + py], outs[a].at[j], send.at[3 * a + j], recv.at[3 * a + j],
                                   (px, py, c)))
        for cp in cps:
            cp.start()
        for cp in cps:
            cp.wait()

    out_shape = tuple(jax.ShapeDtypeStruct((3,) + p.shape[1:], p.dtype) for p in parts)
    return pl.pallas_call(
        body, name="rs_to_owner", in_specs=[ANY] * n, out_specs=(ANY,) * n, out_shape=out_shape,
        scratch_shapes=[pltpu.SemaphoreType.DMA((3 * n,)), pltpu.SemaphoreType.DMA((3 * n,))],
    )(*parts)


def _rs_share_halves(sums):
    n = len(sums)

    def body(*refs):
        ins, outs = refs[:n], refs[n:2 * n]
        send, recv, lsem = refs[2 * n:]
        x, y, c, _ = _place()
        loc, cps = [], []
        for a in range(n):
            hr = sums[a].shape[0]
            loc.append(pltpu.make_async_copy(ins[a], outs[a].at[pl.ds(c * hr, hr), :], lsem.at[a]))
            cps.append(_remote(ins[a], outs[a].at[pl.ds(c * hr, hr), :], send.at[a], recv.at[a], (x, y, 1 - c)))
        for cp in loc + cps:
            cp.start()
        for a in range(n):
            hr = sums[a].shape[0]
            cps[a].wait_send()
            other = outs[a].at[pl.ds((1 - c) * hr, hr), :]
            _remote(other, other, send.at[a], recv.at[a], (x, y, 1 - c)).wait_recv()
            loc[a].wait()

    out_shape = tuple(jax.ShapeDtypeStruct((2 * s.shape[0], s.shape[1]), s.dtype) for s in sums)
    return pl.pallas_call(
        body, name="rs_share_halves", in_specs=[ANY] * n, out_specs=(ANY,) * n, out_shape=out_shape,
        scratch_shapes=[pltpu.SemaphoreType.DMA((n,)), pltpu.SemaphoreType.DMA((n,)), pltpu.SemaphoreType.DMA((n,))],
    )(*sums)


def _gather_small(packed):
    def body(in_ref, out_ref, send, recv, lsem):
        x, y, c, _ = _place()
        me = 4 * x + 2 * y + c
        loc = pltpu.make_async_copy(in_ref, out_ref.at[me], lsem)
        loc.start()
        cps = []
        for m in range(1, N_DEV):
            peer = (1 - x if m & 4 else x, 1 - y if m & 2 else y, 1 - c if m & 1 else c)
            cps.append(_remote(in_ref, out_ref.at[me], send.at[m - 1], recv.at[m - 1], peer))
        for cp in cps:
            cp.start()
        for m in range(1, N_DEV):
            px, py, pc = (1 - x if m & 4 else x, 1 - y if m & 2 else y, 1 - c if m & 1 else c)
            slot = out_ref.at[4 * px + 2 * py + pc]
            cps[m - 1].wait_send()
            _remote(slot, slot, send.at[m - 1], recv.at[m - 1], (px, py, pc)).wait_recv()
        loc.wait()

    return pl.pallas_call(
        body, name="gather_small", in_specs=[ANY], out_specs=ANY,
        out_shape=jax.ShapeDtypeStruct((N_DEV,) + packed.shape, packed.dtype),
        scratch_shapes=[pltpu.SemaphoreType.DMA((N_DEV - 1,)), pltpu.SemaphoreType.DMA((N_DEV - 1,)),
                        pltpu.SemaphoreType.DMA],
    )(packed)


def _div_tile(n, want):
    best = None
    for t in range(8, min(n, want) + 1, 8):
        if n % t == 0:
            best = t
    assert best is not None, n
    return best


def _add_half(name, g, rcv, c_idx):
    nq, r, cc = g.shape
    hr = r // 2

    def body(s_ref, g_ref, r_ref, o_ref):
        o_ref[...] = g_ref[...] + r_ref[...]

    spec = pl.BlockSpec((None, hr, cc), lambda i, s: (i, 0, 0))
    return pl.pallas_call(
        body, name=name,
        grid_spec=pltpu.PrefetchScalarGridSpec(
            num_scalar_prefetch=1, grid=(nq,),
            in_specs=[pl.BlockSpec((None, hr, cc), lambda i, s: (i, s[0], 0)), spec], out_specs=spec),
        out_shape=jax.ShapeDtypeStruct((nq, hr, cc), g.dtype), compiler_params=_cp("parallel"),
    )(c_idx, g, rcv)


def _sum_owner(name, part, rcv, q_idx):
    _, hr, cc = part.shape
    tr = _div_tile(hr, 128)

    def body(s_ref, p_ref, r_ref, o_ref):
        o_ref[...] = ((p_ref[...] + r_ref[0]) + r_ref[1]) + r_ref[2]

    return pl.pallas_call(
        body, name=name,
        grid_spec=pltpu.PrefetchScalarGridSpec(
            num_scalar_prefetch=1, grid=(hr // tr,),
            in_specs=[pl.BlockSpec((None, tr, cc), lambda i, s: (s[0], i, 0)),
                      pl.BlockSpec((3, tr, cc), lambda i, s: (0, i, 0))],
            out_specs=pl.BlockSpec((tr, cc), lambda i, s: (i, 0))),
        out_shape=jax.ShapeDtypeStruct((hr, cc), part.dtype), compiler_params=_cp("parallel"),
    )(q_idx, part, rcv)


def _sum_small(gathered):
    nd, r, cc = gathered.shape

    def body(g_ref, o_ref):
        acc = g_ref[0]
        for d in range(1, nd):
            acc = acc + g_ref[d]
        o_ref[...] = acc

    return pl.pallas_call(
        body, name="sum_small", grid=(1,),
        in_specs=[pl.BlockSpec((nd, r, cc), lambda i: (0, 0, 0))], out_specs=pl.BlockSpec((r, cc), lambda i: (0, 0)),
        out_shape=jax.ShapeDtypeStruct((r, cc), gathered.dtype), compiler_params=_cp("arbitrary"),
    )(gathered)


def _adamw(name, w, g, m, v):
    r, cc = w.shape
    tr = r if r * cc <= 128 * 1024 else _div_tile(r, 256)

    def body(w_ref, g_ref, m_ref, v_ref, d_ref, mo_ref, vo_ref):
        gv = g_ref[...]
        mn = ADAM_B1 * m_ref[...] + (1.0 - ADAM_B1) * gv
        vn = ADAM_B2 * v_ref[...] + (1.0 - ADAM_B2) * (gv * gv)
        m_hat = mn / (1.0 - ADAM_B1 ** ADAM_STEP)
        v_hat = vn / (1.0 - ADAM_B2 ** ADAM_STEP)
        d_ref[...] = -ADAM_LR * (m_hat / (jnp.sqrt(v_hat) + ADAM_EPS) + ADAM_WD * w_ref[...])
        mo_ref[...] = mn
        vo_ref[...] = vn

    spec = pl.BlockSpec((tr, cc), lambda i: (i, 0))
    sd = jax.ShapeDtypeStruct((r, cc), F32)
    return pl.pallas_call(
        body, name=name, grid=(r // tr,), in_specs=[spec] * 4, out_specs=(spec,) * 3, out_shape=(sd,) * 3,
        compiler_params=_cp("parallel"),
    )(w, g, m, v)


_BIG = ("w_in", "w_up", "w_branch", "w_mem_kv", "w_out", "w_down")
_BIG_SHARD_SHAPE = {"w_in": (1024, 1664), "w_up": (1024, 1408), "w_branch": (1536, 256),
                    "w_mem_kv": (256, 1024), "w_out": (256, 1024), "w_down": (704, 1024)}
_SMALL = (("loss", 1), ("norm1_g", 1024), ("ln_v_g", 512), ("ln_v_b", 512), ("w_spatial", 65536),
          ("b_spatial", 512), ("lb_logits", 1024), ("hgrn_norm_g", 128), ("mem_norm_g", 1024),
          ("norm2_g", 1024), ("conv_w", 3 * D_FF), ("conv_b", D_FF), ("final_g", 1024))
_SMALL_ROWS = 656
_PARAM_ORDER = ("norm1_g", "w_in", "ln_v_g", "ln_v_b", "w_spatial", "b_spatial", "lb_logits", "hgrn_norm_g",
                "mem_norm_g", "w_mem_kv", "w_branch", "w_out", "norm2_g", "w_up", "conv_w", "conv_b", "w_down",
                "final_g")


def _pack_small(parts):
    flat = [jnp.reshape(parts[name], (-1,)) if name in parts else jnp.zeros((cnt,), F32) for name, cnt in _SMALL]
    total = sum(cnt for _, cnt in _SMALL)
    flat.append(jnp.zeros((_SMALL_ROWS * 128 - total,), F32))
    return jnp.concatenate(flat).reshape(_SMALL_ROWS, 128)


def _unpack_small(packed):
    flat = packed.reshape(-1)
    out, off = {}, 0
    for name, cnt in _SMALL:
        out[name] = flat[off:off + cnt]
        off += cnt
    return out


def kernel(x, mem, norm1_g, w_in, ln_v_g, ln_v_b, w_spatial, b_spatial, lb_logits, hgrn_norm_g, mem_norm_g, w_mem_kv, w_branch, w_out, norm2_g, w_up, conv_w, conv_b, w_down, final_g, loss_target, m_norm1_g, m_w_in, m_ln_v_g, m_ln_v_b, m_w_spatial, m_b_spatial, m_lb_logits, m_hgrn_norm_g, m_mem_norm_g, m_w_mem_kv, m_w_branch, m_w_out, m_norm2_g, m_w_up, m_conv_w, m_conv_b, m_w_down, m_final_g, v_norm1_g, v_w_in, v_ln_v_g, v_ln_v_b, v_w_spatial, v_b_spatial, v_lb_logits, v_hgrn_norm_g, v_mem_norm_g, v_w_mem_kv, v_w_branch, v_w_out, v_norm2_g, v_w_up, v_conv_w, v_conv_b, v_w_down, v_final_g):
    w = dict(norm1_g=norm1_g, w_in=w_in, ln_v_g=ln_v_g, ln_v_b=ln_v_b, w_spatial=w_spatial, b_spatial=b_spatial,
             lb_logits=lb_logits, hgrn_norm_g=hgrn_norm_g, mem_norm_g=mem_norm_g, w_mem_kv=w_mem_kv,
             w_branch=w_branch, w_out=w_out, norm2_g=norm2_g, w_up=w_up, conv_w=conv_w, conv_b=conv_b,
             w_down=w_down, final_g=final_g)
    mom = dict(norm1_g=m_norm1_g, w_in=m_w_in, ln_v_g=m_ln_v_g, ln_v_b=m_ln_v_b, w_spatial=m_w_spatial,
               b_spatial=m_b_spatial, lb_logits=m_lb_logits, hgrn_norm_g=m_hgrn_norm_g, mem_norm_g=m_mem_norm_g,
               w_mem_kv=m_w_mem_kv, w_branch=m_w_branch, w_out=m_w_out, norm2_g=m_norm2_g, w_up=m_w_up,
               conv_w=m_conv_w, conv_b=m_conv_b, w_down=m_w_down, final_g=m_final_g)
    var = dict(norm1_g=v_norm1_g, w_in=v_w_in, ln_v_g=v_ln_v_g, ln_v_b=v_ln_v_b, w_spatial=v_w_spatial,
               b_spatial=v_b_spatial, lb_logits=v_lb_logits, hgrn_norm_g=v_hgrn_norm_g, mem_norm_g=v_mem_norm_g,
               w_mem_kv=v_w_mem_kv, w_branch=v_w_branch, w_out=v_w_out, norm2_g=v_norm2_g, w_up=v_w_up,
               conv_w=v_conv_w, conv_b=v_conv_b, w_down=v_w_down, final_g=v_final_g)
    B, S, D = x.shape
    T = B * S
    c_idx = jnp.reshape(lax.axis_index("c"), (1,)).astype(jnp.int32)
    q = 2 * lax.axis_index("x") + lax.axis_index("y")
    q_idx = jnp.reshape(q, (1,)).astype(jnp.int32)

    shards = [w[n].reshape(_BIG_SHARD_SHAPE[n]).astype(BF16) for n in _BIG]
    full, conv_all = _all_gather_weights(shards, conv_w[0])
    full = dict(zip(_BIG, full))
    p = dict(
        norm1_g=norm1_g, ln_v_g=ln_v_g, ln_v_b=ln_v_b, w_spatial=w_spatial[0],
        b_spatial=b_spatial.reshape(GM_GROUPS, GM_CHUNK, 1), lb_logits=lb_logits, hgrn_norm_g=hgrn_norm_g,
        mem_norm_g=mem_norm_g, norm2_g=norm2_g, conv_b=conv_b, final_g=final_g.reshape(1, D),
        conv_w=jnp.transpose(conv_all, (1, 0, 2)).reshape(3, 1, D_FF),
        w_in=full["w_in"], w_up=full["w_up"], w_branch=full["w_branch"],
        w_mem_kv=full["w_mem_kv"].reshape(D, 2 * XA_HEADS * XA_DIM), w_out=full["w_out"].reshape(D, D),
        w_down=full["w_down"].reshape(D_FF, D))

    loss, grad_x, g = _local_step(x.reshape(T, D), mem.reshape(B * MEM_LEN, D), loss_target.reshape(T, D), p, B, S)

    big = [g[n].reshape((N_CHIPS,) + _BIG_SHARD_SHAPE[n]) for n in _BIG]
    from_sib = _rs_to_sibling(big)
    parts = [_add_half("rs_add_" + n, gb, r, c_idx) for n, gb, r in zip(_BIG, big, from_sib)]
    from_chips = _rs_to_owner(parts)
    sums = [_sum_owner("rs_sum_" + n, pt, r, q_idx) for n, pt, r in zip(_BIG, parts, from_chips)]
    shard_grads = dict(zip(_BIG, _rs_share_halves(sums)))

    small = dict(g)
    small["loss"] = loss
    total = _unpack_small(_sum_small(_gather_small(_pack_small(small))))

    grads, delta, new_m, new_v = {}, {}, {}, {}
    for n in _BIG:
        shp = _BIG_SHARD_SHAPE[n]
        grads[n] = shard_grads[n]
        delta[n], new_m[n], new_v[n] = _adamw("adamw_" + n, w[n].reshape(shp), shard_grads[n],
                                              mom[n].reshape(shp), var[n].reshape(shp))
    small_names = [n for n, _ in _SMALL if n not in ("loss", "conv_w")]
    pk = _adamw("adamw_small", _pack_small({n: w[n] for n in small_names}),
                _pack_small({n: total[n] for n in small_names}),
                _pack_small({n: mom[n] for n in small_names}), _pack_small({n: var[n] for n in small_names}))
    pk = [_unpack_small(a) for a in pk]
    for n in small_names:
        grads[n] = total[n]
        delta[n], new_m[n], new_v[n] = pk[0][n], pk[1][n], pk[2][n]
    cw_shard = D_FF // N_CHIPS
    grads["conv_w"] = lax.dynamic_slice(total["conv_w"].reshape(3, D_FF), (0, q * cw_shard), (3, cw_shard))
    delta["conv_w"], new_m["conv_w"], new_v["conv_w"] = _adamw(
        "adamw_conv_w", conv_w[0], grads["conv_w"], m_conv_w[0], v_conv_w[0])

    def shaped(d):
        return [d[n].reshape(w[n].shape) for n in _PARAM_ORDER]

    return (total["loss"].reshape(()), grad_x.reshape(B, S, D), *shaped(grads), *shaped(delta), *shaped(new_m),
            *shaped(new_v))
```

```python
import functools
import math

import jax
import jax.numpy as jnp
from jax import lax
from jax.experimental import pallas as pl
from jax.experimental.pallas import tpu as pltpu

F32 = jnp.float32
BF16 = jnp.bfloat16
EPS = 1e-6

D_MODEL = 1024
MEM_LEN = 256
GM_WIDTH = 512
GM_CHUNK = 128
GM_GROUPS = 4
HG_HEADS = 4
HG_DIM = 128
HG_CHUNK = 64
XA_HEADS = 4
XA_DIM = 128
BR_WIDTH = 512
D_FF = 2816
IN_WIDTH = 6656
N_CHIPS = 4
N_DEV = 8

ADAM_LR = 0.001
ADAM_B1 = 0.9
ADAM_B2 = 0.999
ADAM_EPS = 1e-08
ADAM_WD = 0.01
ADAM_STEP = 10

COL_ZU, COL_ZV, COL_HQ, COL_HF, COL_HI, COL_HG, COL_XQ = 0, 1, 2, 3, 4, 5, 6
COL_GATE0 = 3584

VMEM_LIMIT_BYTES = 48 * 1024 * 1024
MESH_ID = pl.DeviceIdType.MESH


def _cp(*sem):
    return pltpu.CompilerParams(dimension_semantics=sem, vmem_limit_bytes=VMEM_LIMIT_BYTES)


def _dot(a, b):
    return lax.dot_general(a.astype(BF16), b.astype(BF16), (((1,), (0,)), ((), ())), preferred_element_type=F32)


def _dot_nt(a, b):
    return lax.dot_general(a.astype(BF16), b.astype(BF16), (((1,), (1,)), ((), ())), preferred_element_type=F32)


def _dot_tn(a, b):
    return lax.dot_general(a.astype(BF16), b.astype(BF16), (((0,), (0,)), ((), ())), preferred_element_type=F32)


def _split2(x):
    hi = x.astype(BF16)
    return hi, (x - hi.astype(F32)).astype(BF16)


def _dot3(a, b, dims):
    ah, al = _split2(a)
    bh, bl = _split2(b)
    dn = (dims, ((), ()))
    return (lax.dot_general(ah, bh, dn, preferred_element_type=F32)
            + lax.dot_general(ah, bl, dn, preferred_element_type=F32)
            + lax.dot_general(al, bh, dn, preferred_element_type=F32))


def _dot_01(mask01, x):
    hi = x.astype(BF16)
    r1 = x - hi.astype(F32)
    mid = r1.astype(BF16)
    lo = (r1 - mid.astype(F32)).astype(BF16)
    m = mask01.astype(BF16)
    dn = (((1,), (0,)), ((), ()))
    return (lax.dot_general(m, hi, dn, preferred_element_type=F32)
            + lax.dot_general(m, mid, dn, preferred_element_type=F32)
            + lax.dot_general(m, lo, dn, preferred_element_type=F32))


def _sigmoid(z):
    return 1.0 / (1.0 + jnp.exp(-z))


_GELU_C = math.sqrt(2.0 / math.pi)


def _gelu_and_grad(z):
    inner = _GELU_C * (z + 0.044715 * z * z * z)
    t = jnp.tanh(inner)
    val = 0.5 * z * (1.0 + t)
    grad = 0.5 * (1.0 + t) + 0.5 * z * (1.0 - t * t) * _GELU_C * (1.0 + 3.0 * 0.044715 * z * z)
    return val, grad


def _row_tile(n, want=512):
    t = min(want, n)
    assert n % t == 0
    return t


def _matmul(name, operands, *, grid, in_specs, o_spec, out_shape, out_dtype, acc_shape, dims, has_res=False):
    nk = grid[2]

    def body(*refs):
        if has_res:
            a_ref, b_ref, r_ref, o_ref, acc = refs
        else:
            a_ref, b_ref, o_ref, acc = refs
            r_ref = None
        k = pl.program_id(2)

        @pl.when(k == 0)
        def _():
            acc[...] = jnp.zeros_like(acc)

        acc[...] += lax.dot_general(a_ref[...].astype(BF16), b_ref[...].astype(BF16), (dims, ((), ())),
                                    preferred_element_type=F32)

        @pl.when(k == nk - 1)
        def _():
            r = acc[...]
            if r_ref is not None:
                r = r + r_ref[...]
            o_ref[...] = r.astype(o_ref.dtype)

    return pl.pallas_call(
        body, name=name, grid=grid, in_specs=in_specs, out_specs=o_spec,
        out_shape=jax.ShapeDtypeStruct(out_shape, out_dtype),
        scratch_shapes=[pltpu.VMEM(acc_shape, F32)],
        compiler_params=_cp("parallel", "parallel", "arbitrary"),
    )(*operands)


NN = ((1,), (0,))
NT = ((1,), (1,))
TN = ((0,), (0,))


def _mm_cs(name, a, w, out_dtype):
    M, K = a.shape
    nq, _, wd = w.shape
    tm = _row_tile(M)
    return _matmul(name, (a, w), grid=(M // tm, nq, 1),
                   in_specs=[pl.BlockSpec((tm, K), lambda i, j, k: (i, 0)),
                             pl.BlockSpec((None, K, wd), lambda i, j, k: (j, 0, 0))],
                   o_spec=pl.BlockSpec((tm, wd), lambda i, j, k: (i, j)),
                   out_shape=(M, nq * wd), out_dtype=out_dtype, acc_shape=(tm, wd), dims=NN)


def _mm_rs(name, a, w, out_dtype, res=None, tn=512):
    M, K = a.shape
    N = w.shape[1]
    tm = _row_tile(M)
    tn = min(tn, N)
    ops = (a, w) if res is None else (a, w, res)
    in_specs = [pl.BlockSpec((tm, K), lambda i, j, k: (i, 0)),
                pl.BlockSpec((K, tn), lambda i, j, k: (0, j))]
    if res is not None:
        in_specs.append(pl.BlockSpec((tm, tn), lambda i, j, k: (i, j)))
    return _matmul(name, ops, grid=(M // tm, N // tn, 1), in_specs=in_specs,
                   o_spec=pl.BlockSpec((tm, tn), lambda i, j, k: (i, j)),
                   out_shape=(M, N), out_dtype=out_dtype, acc_shape=(tm, tn), dims=NN, has_res=res is not None)


def _mm_nt_rs(name, g, w, out_dtype, to):
    M, N = g.shape
    K = w.shape[0]
    tm = _row_tile(M)
    return _matmul(name, (g, w), grid=(M // tm, K // to, 1),
                   in_specs=[pl.BlockSpec((tm, N), lambda i, j, k: (i, 0)),
                             pl.BlockSpec((to, N), lambda i, j, k: (j, 0))],
                   o_spec=pl.BlockSpec((tm, to), lambda i, j, k: (i, j)),
                   out_shape=(M, K), out_dtype=out_dtype, acc_shape=(tm, to), dims=NT)


def _mm_nt_cs(name, g, w, out_dtype):
    M = g.shape[0]
    nq, K, wd = w.shape
    tm = _row_tile(M)
    return _matmul(name, (g, w), grid=(M // tm, 1, nq),
                   in_specs=[pl.BlockSpec((tm, wd), lambda i, j, k: (i, k)),
                             pl.BlockSpec((None, K, wd), lambda i, j, k: (k, 0, 0))],
                   o_spec=pl.BlockSpec((tm, K), lambda i, j, k: (i, 0)),
                   out_shape=(M, K), out_dtype=out_dtype, acc_shape=(tm, K), dims=NT)


def _mm_tn_rs(name, a, g, to, tn=512):
    T, M = a.shape
    N = g.shape[1]
    tt = _row_tile(T)
    tn = min(tn, N)
    return _matmul(name, (a, g), grid=(M // to, N // tn, T // tt),
                   in_specs=[pl.BlockSpec((tt, to), lambda i, j, k: (k, i)),
                             pl.BlockSpec((tt, tn), lambda i, j, k: (k, j))],
                   o_spec=pl.BlockSpec((to, tn), lambda i, j, k: (i, j)),
                   out_shape=(M, N), out_dtype=F32, acc_shape=(to, tn), dims=TN)


def _mm_tn_cs(name, a, g, nq, to):
    T, M = a.shape
    wd = g.shape[1] // nq
    tt = _row_tile(T)
    return _matmul(name, (a, g), grid=(M // to, nq, T // tt),
                   in_specs=[pl.BlockSpec((tt, to), lambda i, j, k: (k, i)),
                             pl.BlockSpec((tt, wd), lambda i, j, k: (k, j))],
                   o_spec=pl.BlockSpec((None, to, wd), lambda i, j, k: (j, i, 0)),
                   out_shape=(nq, M, wd), out_dtype=F32, acc_shape=(to, wd), dims=TN)


def _rms_fwd(name, x, g):
    T, D = x.shape
    tm = _row_tile(T)

    def body(x_ref, g_ref, o_ref):
        xv = x_ref[...]
        r = lax.rsqrt(jnp.mean(xv * xv, axis=-1, keepdims=True) + EPS)
        o_ref[...] = (xv * r * g_ref[...]).astype(o_ref.dtype)

    return pl.pallas_call(
        body, name=name, grid=(T // tm,),
        in_specs=[pl.BlockSpec((tm, D), lambda i: (i, 0)), pl.BlockSpec((1, D), lambda i: (0, 0))],
        out_specs=pl.BlockSpec((tm, D), lambda i: (i, 0)),
        out_shape=jax.ShapeDtypeStruct((T, D), BF16), compiler_params=_cp("parallel"),
    )(x, g)


def _rms_bwd(name, x, g, dh, dres):
    T, D = x.shape
    tm = _row_tile(T)
    has_res = dres is not None

    def body(*refs):
        if has_res:
            x_ref, g_ref, dh_ref, dr_ref, dx_ref, dg_ref = refs
        else:
            x_ref, g_ref, dh_ref, dx_ref, dg_ref = refs

        @pl.when(pl.program_id(0) == 0)
        def _():
            dg_ref[...] = jnp.zeros_like(dg_ref)

        xv = x_ref[...]
        r = lax.rsqrt(jnp.mean(xv * xv, axis=-1, keepdims=True) + EPS)
        n = xv * r
        dhv = dh_ref[...]
        dg_ref[...] += jnp.sum(dhv * n, axis=0, keepdims=True)
        dn = dhv * g_ref[...]
        dx = r * (dn - n * jnp.mean(dn * n, axis=-1, keepdims=True))
        if has_res:
            dx = dx + dr_ref[...]
        dx_ref[...] = dx

    row = pl.BlockSpec((tm, D), lambda i: (i, 0))
    vec = pl.BlockSpec((1, D), lambda i: (0, 0))
    ops = (x, g, dh, dres) if has_res else (x, g, dh)
    return pl.pallas_call(
        body, name=name, grid=(T // tm,),
        in_specs=[row, vec, row] + ([row] if has_res else []),
        out_specs=(row, vec),
        out_shape=(jax.ShapeDtypeStruct((T, D), F32), jax.ShapeDtypeStruct((1, D), F32)),
        compiler_params=_cp("arbitrary"),
    )(*ops)


def _loss_head(x2, tgt, g):
    T, D = x2.shape
    tm = _row_tile(T)

    def body(x_ref, t_ref, g_ref, dx_ref, dg_ref, loss_ref):
        @pl.when(pl.program_id(0) == 0)
        def _():
            dg_ref[...] = jnp.zeros_like(dg_ref)
            loss_ref[...] = jnp.zeros_like(loss_ref)

        xv = x_ref[...]
        gv = g_ref[...]
        r = lax.rsqrt(jnp.mean(xv * xv, axis=-1, keepdims=True) + EPS)
        n = xv * r
        diff = n * gv - t_ref[...]
        loss_ref[...] += 0.5 * jnp.sum(jnp.mean(diff * diff, axis=-1, keepdims=True))
        dy = diff * (1.0 / D)
        dg_ref[...] += jnp.sum(dy * n, axis=0, keepdims=True)
        dn = dy * gv
        dx_ref[...] = r * (dn - n * jnp.mean(dn * n, axis=-1, keepdims=True))

    row = pl.BlockSpec((tm, D), lambda i: (i, 0))
    vec = pl.BlockSpec((1, D), lambda i: (0, 0))
    return pl.pallas_call(
        body, name="loss_head", grid=(T // tm,),
        in_specs=[row, row, vec],
        out_specs=(row, vec, pl.BlockSpec((8, 128), lambda i: (0, 0))),
        out_shape=(jax.ShapeDtypeStruct((T, D), F32), jax.ShapeDtypeStruct((1, D), F32),
                   jax.ShapeDtypeStruct((8, 128), F32)),
        compiler_params=_cp("arbitrary"),
    )(x2, tgt, g)


def _gmlp_pieces(zu, zv, lng, lnb, ws_ref, bs_ref):
    u, du = _gelu_and_grad(zu)
    v, dv = _gelu_and_grad(zv)
    mu = jnp.mean(v, axis=-1, keepdims=True)
    vc = v - mu
    rstd = lax.rsqrt(jnp.mean(vc * vc, axis=-1, keepdims=True) + EPS)
    vhat = vc * rstd
    vn = vhat * lng + lnb
    row = lax.broadcasted_iota(jnp.int32, (GM_CHUNK, GM_CHUNK), 0)
    col = lax.broadcasted_iota(jnp.int32, (GM_CHUNK, GM_CHUNK), 1)
    tril = row >= col
    wms, mixed = [], []
    for g in range(GM_GROUPS):
        sl = slice(g * 128, (g + 1) * 128)
        wm = jnp.where(tril, ws_ref[g], 0.0)
        wms.append(wm)
        mixed.append(_dot(wm, vn[:, sl]) + bs_ref[g])
    return u, du, dv, rstd, vhat, vn, wms, mixed, tril


def _gmlp_fwd(proj, lng, lnb, ws, bs_col):
    T = proj.shape[0]
    n = T // GM_CHUNK

    def body(zu_ref, zv_ref, lng_ref, lnb_ref, ws_ref, bs_ref, o_ref):
        u, _, _, _, _, _, _, mixed, _ = _gmlp_pieces(zu_ref[...], zv_ref[...], lng_ref[...], lnb_ref[...],
                                                     ws_ref, bs_ref)
        for g in range(GM_GROUPS):
            sl = slice(g * 128, (g + 1) * 128)
            o_ref[:, sl] = (u[:, sl] * mixed[g]).astype(o_ref.dtype)

    vec = pl.BlockSpec((1, GM_WIDTH), lambda i: (0, 0))
    return pl.pallas_call(
        body, name="gmlp_fwd", grid=(n,),
        in_specs=[pl.BlockSpec((GM_CHUNK, 512), lambda i: (i, COL_ZU)),
                  pl.BlockSpec((GM_CHUNK, 512), lambda i: (i, COL_ZV)),
                  vec, vec,
                  pl.BlockSpec((GM_GROUPS, 128, 128), lambda i: (0, 0, 0)),
                  pl.BlockSpec((GM_GROUPS, 128, 1), lambda i: (0, 0, 0))],
        out_specs=pl.BlockSpec((GM_CHUNK, 512), lambda i: (i, 0)),
        out_shape=jax.ShapeDtypeStruct((T, GM_WIDTH), BF16), compiler_params=_cp("parallel"),
    )(proj, proj, lng, lnb, ws, bs_col)


def _gmlp_bwd(proj, d_out, lng, lnb, ws, bs_col):
    T = proj.shape[0]
    n = T // GM_CHUNK

    def body(zu_ref, zv_ref, do_ref, lng_ref, lnb_ref, ws_ref, bs_ref,
             dzu_ref, dzv_ref, dws_ref, dbs_ref, dlng_ref, dlnb_ref, dm_acc):
        i = pl.program_id(0)

        @pl.when(i == 0)
        def _():
            dws_ref[...] = jnp.zeros_like(dws_ref)
            dlng_ref[...] = jnp.zeros_like(dlng_ref)
            dlnb_ref[...] = jnp.zeros_like(dlnb_ref)
            dm_acc[...] = jnp.zeros_like(dm_acc)

        lng_v = lng_ref[...]
        u, du, dv, rstd, vhat, vn, wms, mixed, tril = _gmlp_pieces(zu_ref[...], zv_ref[...], lng_v, lnb_ref[...],
                                                                  ws_ref, bs_ref)
        do = do_ref[...]
        dvn_parts = []
        for g in range(GM_GROUPS):
            sl = slice(g * 128, (g + 1) * 128)
            dog = do[:, sl]
            dzu_ref[:, sl] = (dog * mixed[g] * du[:, sl]).astype(dzu_ref.dtype)
            dmix = dog * u[:, sl]
            dm_acc[:, sl] += dmix
            dws_ref[g] += jnp.where(tril, _dot_nt(dmix, vn[:, sl]), 0.0)
            dvn_parts.append(_dot_tn(wms[g], dmix))
        dvn = jnp.concatenate(dvn_parts, axis=1)
        dlng_ref[...] += jnp.sum(dvn * vhat, axis=0, keepdims=True)
        dlnb_ref[...] += jnp.sum(dvn, axis=0, keepdims=True)
        dvh = dvn * lng_v
        dvv = rstd * (dvh - jnp.mean(dvh, axis=-1, keepdims=True)
                      - vhat * jnp.mean(dvh * vhat, axis=-1, keepdims=True))
        dzv_ref[...] = (dvv * dv).astype(dzv_ref.dtype)

        @pl.when(i == n - 1)
        def _():
            for g in range(GM_GROUPS):
                dbs_ref[g] = jnp.sum(dm_acc[:, g * 128:(g + 1) * 128], axis=1, keepdims=True)

    vec = pl.BlockSpec((1, GM_WIDTH), lambda i: (0, 0))
    wsp = pl.BlockSpec((GM_GROUPS, 128, 128), lambda i: (0, 0, 0))
    bsp = pl.BlockSpec((GM_GROUPS, 128, 1), lambda i: (0, 0, 0))
    tile = pl.BlockSpec((GM_CHUNK, 512), lambda i: (i, 0))
    return pl.pallas_call(
        body, name="gmlp_bwd", grid=(n,),
        in_specs=[pl.BlockSpec((GM_CHUNK, 512), lambda i: (i, COL_ZU)),
                  pl.BlockSpec((GM_CHUNK, 512), lambda i: (i, COL_ZV)),
                  tile, vec, vec, wsp, bsp],
        out_specs=(tile, tile, wsp, bsp, vec, vec),
        out_shape=(jax.ShapeDtypeStruct((T, GM_WIDTH), BF16), jax.ShapeDtypeStruct((T, GM_WIDTH), BF16),
                   jax.ShapeDtypeStruct((GM_GROUPS, 128, 128), F32), jax.ShapeDtypeStruct((GM_GROUPS, 128, 1), F32),
                   jax.ShapeDtypeStruct((1, GM_WIDTH), F32), jax.ShapeDtypeStruct((1, GM_WIDTH), F32)),
        scratch_shapes=[pltpu.VMEM((GM_CHUNK, GM_WIDTH), F32)],
        compiler_params=_cp("arbitrary"),
    )(proj, proj, d_out, lng, lnb, ws, bs_col)


def _hgrn_lower_bound(lbl):
    return 1.0 / (1.0 + jnp.exp(lbl[1:2, :] - lbl[0:1, :]))


def _hgrn_gates(hq, hf, lb):
    sg = _sigmoid(hf)
    fg = lb + (1.0 - lb) * sg
    sq = _sigmoid(hq)
    C = HG_CHUNK
    row = lax.broadcasted_iota(jnp.int32, (C, C), 0)
    col = lax.broadcasted_iota(jnp.int32, (C, C), 1)
    tril = row >= col
    logf = jnp.log(fg)
    a = _dot_01(tril, logf)
    a_last = jnp.sum(logf, axis=0, keepdims=True)
    first_half = lax.broadcasted_iota(jnp.int32, logf.shape, 0) < (C // 2)
    a_mid = jnp.sum(jnp.where(first_half, logf, 0.0), axis=0, keepdims=True)
    return sg, fg, sq, tril, a, a_last, a_mid


def _hgrn_fwd(proj, lbl, gh, B, S):
    T = B * S
    C = HG_CHUNK
    NC = S // C
    W = HG_HEADS * HG_DIM

    def body(q_ref, f_ref, i_ref, g_ref, lbl_ref, gh_ref, o_ref, bo_ref, st_ref, state):
        @pl.when(pl.program_id(1) == 0)
        def _():
            state[...] = jnp.zeros_like(state)

        lb_all = _hgrn_lower_bound(lbl_ref[...])
        ghv = gh_ref[...]
        for h in range(HG_HEADS):
            sl = slice(h * 128, (h + 1) * 128)
            hq = q_ref[:, sl]
            sg, fg, sq, tril, a, a_last, a_mid = _hgrn_gates(hq, f_ref[:, sl], lb_all[:, sl])
            k = 1.0 - fg
            q = hq * sq
            v = i_ref[:, sl]
            qe = q * jnp.exp(a)
            qi = q * jnp.exp(a - a_mid)
            ki = k * jnp.exp(a_mid - a)
            kl = k * jnp.exp(a_last - a)
            p = jnp.where(tril, _dot_nt(qi, ki), 0.0)
            st = state[h]
            st_ref[h] = st
            o = _dot_nt(qe, st) + _dot(p, v)
            state[h] = st * jnp.exp(a_last) + _dot_tn(v, kl)
            o_ref[:, sl] = o
            r = lax.rsqrt(jnp.mean(o * o, axis=-1, keepdims=True) + EPS)
            hg = g_ref[:, sl]
            bo_ref[:, sl] = (o * r * ghv * (hg * _sigmoid(hg))).astype(bo_ref.dtype)

    def col(cb):
        return pl.BlockSpec((C, 512), lambda b, c: (b * NC + c, cb))

    tile = pl.BlockSpec((C, W), lambda b, c: (b * NC + c, 0))
    return pl.pallas_call(
        body, name="hgrn_fwd", grid=(B, NC),
        in_specs=[col(COL_HQ), col(COL_HF), col(COL_HI), col(COL_HG),
                  pl.BlockSpec((2, W), lambda b, c: (0, 0)), pl.BlockSpec((1, HG_DIM), lambda b, c: (0, 0))],
        out_specs=(tile, tile, pl.BlockSpec((None, HG_HEADS, 128, 128), lambda b, c: (b * NC + c, 0, 0, 0))),
        out_shape=(jax.ShapeDtypeStruct((T, W), F32), jax.ShapeDtypeStruct((T, W), BF16),
                   jax.ShapeDtypeStruct((B * NC, HG_HEADS, 128, 128), F32)),
        scratch_shapes=[pltpu.VMEM((HG_HEADS, 128, 128), F32)],
        compiler_params=_cp("parallel", "arbitrary"),
    )(proj, proj, proj, proj, lbl, gh)


def _hgrn_bwd(proj, o_saved, states, d_out, lbl, gh, B, S):
    T = B * S
    C = HG_CHUNK
    NC = S // C
    W = HG_HEADS * HG_DIM

    def body(q_ref, f_ref, i_ref, g_ref, o_ref, st_ref, do_ref, lbl_ref, gh_ref,
             dq_ref, df_ref, di_ref, dg_ref, dlbl_ref, dgh_ref, dstate, dlb_acc):
        b = pl.program_id(0)
        c = pl.program_id(1)

        @pl.when(c == 0)
        def _():
            dstate[...] = jnp.zeros_like(dstate)

        @pl.when((b == 0) & (c == 0))
        def _():
            dgh_ref[...] = jnp.zeros_like(dgh_ref)
            dlb_acc[...] = jnp.zeros_like(dlb_acc)

        lbl_v = lbl_ref[...]
        lb_all = _hgrn_lower_bound(lbl_v)
        ghv = gh_ref[...]
        row = lax.broadcasted_iota(jnp.int32, (C, C), 0)
        colm = lax.broadcasted_iota(jnp.int32, (C, C), 1)
        triu = colm >= row
        for h in range(HG_HEADS):
            sl = slice(h * 128, (h + 1) * 128)
            hq = q_ref[:, sl]
            lb = lb_all[:, sl]
            sg, fg, sq, tril, a, a_last, a_mid = _hgrn_gates(hq, f_ref[:, sl], lb)
            k = 1.0 - fg
            q = hq * sq
            v = i_ref[:, sl]
            ea = jnp.exp(a)
            ei = jnp.exp(a - a_mid)
            eki = jnp.exp(a_mid - a)
            ekl = jnp.exp(a_last - a)
            e_last = jnp.exp(a_last)
            qe = q * ea
            qi = q * ei
            ki = k * eki
            kl = k * ekl
            p = jnp.where(tril, _dot_nt(qi, ki), 0.0)
            st = st_ref[h]
            o = o_ref[:, sl]
            hg = g_ref[:, sl]
            sgg = _sigmoid(hg)
            r = lax.rsqrt(jnp.mean(o * o, axis=-1, keepdims=True) + EPS)
            n = o * r
            dbo = do_ref[:, sl]
            dg_ref[:, sl] = (dbo * n * ghv * (sgg * (1.0 + hg * (1.0 - sgg)))).astype(dg_ref.dtype)
            don = dbo * (hg * sgg)
            dgh_ref[...] += jnp.sum(don * n, axis=0, keepdims=True)
            dn = don * ghv
            d_o = r * (dn - n * jnp.mean(dn * n, axis=-1, keepdims=True))
            dst = dstate[h]
            dp = jnp.where(tril, _dot3(d_o, v, NT), 0.0)
            d_qe = _dot3(d_o, st, NN)
            d_qi = _dot3(dp, ki, NN)
            d_ki = _dot3(dp, qi, TN)
            d_kl = _dot3(v, dst, NN)
            dv = _dot_tn(p, d_o) + _dot_nt(kl, dst)
            dstate[h] = dst * e_last + _dot3(d_o, qe, TN)
            d_a_last = jnp.sum(dst * st, axis=0, keepdims=True) * e_last + jnp.sum(d_kl * kl, axis=0, keepdims=True)
            dq = d_qe * ea + d_qi * ei
            dk = d_ki * eki + d_kl * ekl
            da = d_qe * qe + d_qi * qi - d_ki * ki - d_kl * kl
            dlogf = _dot_01(triu, da) + d_a_last
            dfg = dlogf / fg - dk
            df_ref[:, sl] = (dfg * (1.0 - lb) * sg * (1.0 - sg)).astype(df_ref.dtype)
            dlb_acc[:, sl] += jnp.sum(dfg * (1.0 - sg), axis=0, keepdims=True)
            dq_ref[:, sl] = (dq * (sq * (1.0 + hq * (1.0 - sq)))).astype(dq_ref.dtype)
            di_ref[:, sl] = dv.astype(di_ref.dtype)

        @pl.when((b == B - 1) & (c == NC - 1))
        def _():
            dlb = dlb_acc[...]
            p0 = lb_all
            first = lax.broadcasted_iota(jnp.int32, (2, W), 0) == 0
            dlbl_ref[...] = jnp.where(first, dlb * p0 * (1.0 - p0), -dlb * p0 * (1.0 - p0))

    def col(cb):
        return pl.BlockSpec((C, 512), lambda b, c: (b * NC + NC - 1 - c, cb))

    tile = pl.BlockSpec((C, W), lambda b, c: (b * NC + NC - 1 - c, 0))
    return pl.pallas_call(
        body, name="hgrn_bwd", grid=(B, NC),
        in_specs=[col(COL_HQ), col(COL_HF), col(COL_HI), col(COL_HG), tile,
                  pl.BlockSpec((None, HG_HEADS, 128, 128), lambda b, c: (b * NC + NC - 1 - c, 0, 0, 0)),
                  tile, pl.BlockSpec((2, W), lambda b, c: (0, 0)), pl.BlockSpec((1, HG_DIM), lambda b, c: (0, 0))],
        out_specs=(tile, tile, tile, tile,
                   pl.BlockSpec((2, W), lambda b, c: (0, 0)), pl.BlockSpec((1, HG_DIM), lambda b, c: (0, 0))),
        out_shape=(jax.ShapeDtypeStruct((T, W), BF16),) * 4
        + (jax.ShapeDtypeStruct((2, W), F32), jax.ShapeDtypeStruct((1, HG_DIM), F32)),
        scratch_shapes=[pltpu.VMEM((HG_HEADS, 128, 128), F32), pltpu.VMEM((1, W), F32)],
        compiler_params=_cp("arbitrary", "arbitrary"),
    )(proj, proj, proj, proj, o_saved, states, d_out, lbl, gh)


_XA_SCALE = XA_DIM ** -0.5


def _attn_probs(qh, kh):
    s = _dot_nt(qh, kh) * _XA_SCALE
    e = jnp.exp(s - jnp.max(s, axis=-1, keepdims=True))
    return e / jnp.sum(e, axis=-1, keepdims=True)


def _attn_fwd(proj, kv, B, S):
    T = B * S
    tq = _row_tile(S)
    nq = S // tq
    W = XA_HEADS * XA_DIM

    def body(q_ref, kv_ref, o_ref):
        for h in range(XA_HEADS):
            sl = slice(h * 128, (h + 1) * 128)
            p = _attn_probs(q_ref[:, sl], kv_ref[:, sl])
            o_ref[:, sl] = _dot(p, kv_ref[:, W + h * 128:W + (h + 1) * 128]).astype(o_ref.dtype)

    return pl.pallas_call(
        body, name="attn_fwd", grid=(B, nq),
        in_specs=[pl.BlockSpec((tq, 512), lambda b, i: (b * nq + i, COL_XQ)),
                  pl.BlockSpec((MEM_LEN, 2 * W), lambda b, i: (b, 0))],
        out_specs=pl.BlockSpec((tq, W), lambda b, i: (b * nq + i, 0)),
        out_shape=jax.ShapeDtypeStruct((T, W), BF16), compiler_params=_cp("parallel", "parallel"),
    )(proj, kv)


def _attn_bwd(proj, kv, d_out, B, S):
    T = B * S
    tq = _row_tile(S)
    nq = S // tq
    W = XA_HEADS * XA_DIM

    def body(q_ref, kv_ref, do_ref, dq_ref, dkv_ref):
        @pl.when(pl.program_id(1) == 0)
        def _():
            dkv_ref[...] = jnp.zeros_like(dkv_ref)

        for h in range(XA_HEADS):
            sl = slice(h * 128, (h + 1) * 128)
            slv = slice(W + h * 128, W + (h + 1) * 128)
            qh = q_ref[:, sl]
            kh = kv_ref[:, sl]
            p = _attn_probs(qh, kh)
            dc = do_ref[:, sl]
            dp = _dot_nt(dc, kv_ref[:, slv])
            ds = p * (dp - jnp.sum(dp * p, axis=-1, keepdims=True)) * _XA_SCALE
            dq_ref[:, sl] = _dot(ds, kh).astype(dq_ref.dtype)
            dkv_ref[:, sl] += _dot_tn(ds, qh)
            dkv_ref[:, slv] += _dot_tn(p, dc)

    kvspec = pl.BlockSpec((MEM_LEN, 2 * W), lambda b, i: (b, 0))
    tile = pl.BlockSpec((tq, W), lambda b, i: (b * nq + i, 0))
    return pl.pallas_call(
        body, name="attn_bwd", grid=(B, nq),
        in_specs=[pl.BlockSpec((tq, 512), lambda b, i: (b * nq + i, COL_XQ)), kvspec, tile],
        out_specs=(tile, kvspec),
        out_shape=(jax.ShapeDtypeStruct((T, W), BF16), jax.ShapeDtypeStruct((B * MEM_LEN, 2 * W), F32)),
        compiler_params=_cp("parallel", "arbitrary"),
    )(proj, kv, d_out)


_MERGE_TN = 256


def _gate_specs(tm):
    base = COL_GATE0 // _MERGE_TN
    per = D_MODEL // _MERGE_TN
    return [pl.BlockSpec((tm, _MERGE_TN), functools.partial(lambda i, j, n: (i, base + per * n + j), n=n))
            for n in range(3)]


def _merge_fwd(a_out, b_out, c_out, wb, proj):
    T = a_out.shape[0]
    tm = _row_tile(T)
    tn = _MERGE_TN

    def body(a_ref, b_ref, c_ref, w_ref, g0_ref, g1_ref, g2_ref, m_ref, up_ref):
        acc = None
        for n, (br, gr) in enumerate(((a_ref, g0_ref), (b_ref, g1_ref), (c_ref, g2_ref))):
            up = _dot(br[...], w_ref[n * BR_WIDTH:(n + 1) * BR_WIDTH, :])
            up_ref[n] = up.astype(up_ref.dtype)
            term = _sigmoid(gr[...]) * up
            acc = term if acc is None else acc + term
        m_ref[...] = acc.astype(m_ref.dtype)

    br_spec = pl.BlockSpec((tm, BR_WIDTH), lambda i, j: (i, 0))
    return pl.pallas_call(
        body, name="merge_fwd", grid=(T // tm, D_MODEL // tn),
        in_specs=[br_spec, br_spec, br_spec,
                  pl.BlockSpec((None, 3 * BR_WIDTH, tn), lambda i, j: (j, 0, 0))] + _gate_specs(tm),
        out_specs=(pl.BlockSpec((tm, tn), lambda i, j: (i, j)), pl.BlockSpec((3, tm, tn), lambda i, j: (0, i, j))),
        out_shape=(jax.ShapeDtypeStruct((T, D_MODEL), BF16), jax.ShapeDtypeStruct((3, T, D_MODEL), BF16)),
        compiler_params=_cp("parallel", "parallel"),
    )(a_out, b_out, c_out, wb, proj, proj, proj)


def _merge_bwd(d_merged, ups, proj):
    T = d_merged.shape[0]
    tm = _row_tile(T)
    tn = _MERGE_TN

    def body(dm_ref, up_ref, g0_ref, g1_ref, g2_ref, dup_ref, dg0_ref, dg1_ref, dg2_ref):
        dm = dm_ref[...]
        for n, (gr, dgr) in enumerate(((g0_ref, dg0_ref), (g1_ref, dg1_ref), (g2_ref, dg2_ref))):
            gate = _sigmoid(gr[...])
            dup_ref[n] = (dm * gate).astype(dup_ref.dtype)
            dgr[...] = (dm * up_ref[n].astype(F32) * gate * (1.0 - gate)).astype(dgr.dtype)

    tile = pl.BlockSpec((tm, tn), lambda i, j: (i, j))
    tile3 = pl.BlockSpec((3, tm, tn), lambda i, j: (0, i, j))
    return pl.pallas_call(
        body, name="merge_bwd", grid=(T // tm, D_MODEL // tn),
        in_specs=[tile, tile3] + _gate_specs(tm),
        out_specs=(tile3, tile, tile, tile),
        out_shape=(jax.ShapeDtypeStruct((3, T, D_MODEL), BF16),) + (jax.ShapeDtypeStruct((T, D_MODEL), BF16),) * 3,
        compiler_params=_cp("parallel", "parallel"),
    )(d_merged, ups, proj, proj, proj)


_CONV_TF = 256
_HALO = 8


def _conv_fwd(ab, cw, cb, B, S):
    T = B * S
    ts = _row_tile(S)
    tf = _CONV_TF
    nb = D_FF // tf
    tps = S // ts
    hb = ts // _HALO

    def body(a_ref, p_ref, b_ref, w_ref, cb_ref, o_ref):
        start = (pl.program_id(0) % tps) == 0
        prev = jnp.where(start, 0.0, p_ref[...])
        ext = jnp.concatenate([prev, a_ref[...]], axis=0)
        a1 = pltpu.roll(ext, 1, 0)[_HALO:, :]
        a2 = pltpu.roll(ext, 2, 0)[_HALO:, :]
        ac = cb_ref[...] + w_ref[0] * a2 + w_ref[1] * a1 + w_ref[2] * a_ref[...]
        o_ref[...] = (ac * _sigmoid(ac) * b_ref[...]).astype(o_ref.dtype)

    return pl.pallas_call(
        body, name="conv_fwd", grid=(T // ts, nb),
        in_specs=[pl.BlockSpec((ts, tf), lambda i, j: (i, j)),
                  pl.BlockSpec((_HALO, tf), lambda i, j: (jnp.maximum(i * hb - 1, 0), j)),
                  pl.BlockSpec((ts, tf), lambda i, j: (i, j + nb)),
                  pl.BlockSpec((3, 1, tf), lambda i, j: (0, 0, j)),
                  pl.BlockSpec((1, tf), lambda i, j: (0, j))],
        out_specs=pl.BlockSpec((ts, tf), lambda i, j: (i, j)),
        out_shape=jax.ShapeDtypeStruct((T, D_FF), BF16), compiler_params=_cp("parallel", "parallel"),
    )(ab, ab, ab, cw, cb)


def _conv_bwd(ab, d_ff, cw, cb, B, S):
    T = B * S
    ts = _row_tile(S)
    tf = _CONV_TF
    nb = D_FF // tf
    tps = S // ts
    hb = ts // _HALO
    last_h = T // _HALO - 1
    n_ext = ts + _HALO

    def body(a_ref, ap_ref, an_ref, b_ref, bn_ref, d_ref, dn_ref, w_ref, cb_ref,
             da_ref, db_ref, dw_ref, dcb_ref):
        i = pl.program_id(1)

        @pl.when(i == 0)
        def _():
            dw_ref[...] = jnp.zeros_like(dw_ref)
            dcb_ref[...] = jnp.zeros_like(dcb_ref)

        start = (i % tps) == 0
        end = (i % tps) == tps - 1
        a = a_ref[...]
        ext = jnp.concatenate([jnp.where(start, 0.0, ap_ref[...]), a, an_ref[...]], axis=0)
        r1 = pltpu.roll(ext, 1, 0)[_HALO:, :]
        r2 = pltpu.roll(ext, 2, 0)[_HALO:, :]
        ac = cb_ref[...] + w_ref[0] * r2 + w_ref[1] * r1 + w_ref[2] * ext[_HALO:, :]
        sg = _sigmoid(ac)
        d_e = jnp.concatenate([d_ref[...], jnp.where(end, 0.0, dn_ref[...])], axis=0)
        b_e = jnp.concatenate([b_ref[...], bn_ref[...]], axis=0)
        db_ref[...] = (d_e[:ts, :] * (ac * sg)[:ts, :]).astype(db_ref.dtype)
        dac = d_e * b_e * sg * (1.0 + ac * (1.0 - sg))
        u1 = pltpu.roll(dac, n_ext - 1, 0)[:ts, :]
        u2 = pltpu.roll(dac, n_ext - 2, 0)[:ts, :]
        dac0 = dac[:ts, :]
        da_ref[...] = (w_ref[2] * dac0 + w_ref[1] * u1 + w_ref[0] * u2).astype(da_ref.dtype)
        dcb_ref[...] += jnp.sum(dac0, axis=0, keepdims=True)
        dw_ref[2] += jnp.sum(dac0 * a, axis=0, keepdims=True)
        dw_ref[1] += jnp.sum(dac0 * r1[:ts, :], axis=0, keepdims=True)
        dw_ref[0] += jnp.sum(dac0 * r2[:ts, :], axis=0, keepdims=True)

    def cur(off):
        return pl.BlockSpec((ts, tf), lambda j, i: (i, j + off))

    def nxt(off):
        return pl.BlockSpec((_HALO, tf), lambda j, i: (jnp.minimum((i + 1) * hb, last_h), j + off))

    return pl.pallas_call(
        body, name="conv_bwd", grid=(nb, T // ts),
        in_specs=[cur(0), pl.BlockSpec((_HALO, tf), lambda j, i: (jnp.maximum(i * hb - 1, 0), j)), nxt(0),
                  cur(nb), nxt(nb), cur(0), nxt(0),
                  pl.BlockSpec((3, 1, tf), lambda j, i: (0, 0, j)), pl.BlockSpec((1, tf), lambda j, i: (0, j))],
        out_specs=(cur(0), cur(0), pl.BlockSpec((3, 1, tf), lambda j, i: (0, 0, j)),
                   pl.BlockSpec((1, tf), lambda j, i: (0, j))),
        out_shape=(jax.ShapeDtypeStruct((T, D_FF), BF16), jax.ShapeDtypeStruct((T, D_FF), BF16),
                   jax.ShapeDtypeStruct((3, 1, D_FF), F32), jax.ShapeDtypeStruct((1, D_FF), F32)),
        compiler_params=_cp("parallel", "arbitrary"),
    )(ab, ab, ab, ab, ab, d_ff, d_ff, cw, cb)


def _local_step(x, mem, tgt, p, B, S):
    g = {}
    h = _rms_fwd("norm1", x, p["norm1_g"])
    proj = _mm_cs("in_proj", h, p["w_in"], F32)
    a_out = _gmlp_fwd(proj, p["ln_v_g"], p["ln_v_b"], p["w_spatial"], p["b_spatial"])
    o_h, b_out, states = _hgrn_fwd(proj, p["lb_logits"], p["hgrn_norm_g"], B, S)
    memn = _rms_fwd("mem_norm", mem, p["mem_norm_g"])
    kv = _mm_rs("mem_kv", memn, p["w_mem_kv"], F32)
    c_out = _attn_fwd(proj, kv, B, S)
    merged, ups = _merge_fwd(a_out, b_out, c_out, p["w_branch"], proj)
    x1 = _mm_rs("out_proj", merged, p["w_out"], F32, res=x)
    h2 = _rms_fwd("norm2", x1, p["norm2_g"])
    ab = _mm_cs("up_proj", h2, p["w_up"], F32)
    ff = _conv_fwd(ab, p["conv_w"], p["conv_b"], B, S)
    x2 = _mm_rs("down_proj", ff, p["w_down"], F32, res=x1)
    dx2, g["final_g"], loss = _loss_head(x2, tgt, p["final_g"])

    d_ff = _mm_nt_rs("d_ff", dx2, p["w_down"], F32, to=D_FF // 2)
    g["w_down"] = _mm_tn_rs("g_w_down", ff, dx2, to=D_FF // 2)
    d_a, d_b, g["conv_w"], g["conv_b"] = _conv_bwd(ab, d_ff, p["conv_w"], p["conv_b"], B, S)
    d_ab = jnp.concatenate([d_a, d_b], axis=1)
    d_h2 = _mm_nt_cs("d_h2", d_ab, p["w_up"], F32)
    g["w_up"] = _mm_tn_cs("g_w_up", h2, d_ab, N_CHIPS, to=512)
    d_x1, g["norm2_g"] = _rms_bwd("norm2_bwd", x1, p["norm2_g"], d_h2, dx2)
    d_merged = _mm_nt_rs("d_merged", d_x1, p["w_out"], F32, to=512)
    g["w_out"] = _mm_tn_rs("g_w_out", merged, d_x1, to=512)
    d_ups, d_g0, d_g1, d_g2 = _merge_bwd(d_merged, ups, proj)

    T = x.shape[0]
    tm = _row_tile(T)
    d_br, g_wb = [], []
    for n, br in enumerate((a_out, b_out, c_out)):
        d_br.append(_matmul(
            "d_branch%d" % n, (d_ups, p["w_branch"]), grid=(T // tm, 1, N_CHIPS),
            in_specs=[pl.BlockSpec((None, tm, _MERGE_TN), functools.partial(lambda i, j, k, n: (n, i, k), n=n)),
                      pl.BlockSpec((None, BR_WIDTH, _MERGE_TN), functools.partial(lambda i, j, k, n: (k, n, 0), n=n))],
            o_spec=pl.BlockSpec((tm, BR_WIDTH), lambda i, j, k: (i, 0)),
            out_shape=(T, BR_WIDTH), out_dtype=F32, acc_shape=(tm, BR_WIDTH), dims=NT))
        g_wb.append(_matmul(
            "g_w_branch%d" % n, (br, d_ups), grid=(1, N_CHIPS, T // tm),
            in_specs=[pl.BlockSpec((tm, BR_WIDTH), lambda i, j, k: (k, 0)),
                      pl.BlockSpec((None, tm, _MERGE_TN), functools.partial(lambda i, j, k, n: (n, k, j), n=n))],
            o_spec=pl.BlockSpec((None, BR_WIDTH, _MERGE_TN), lambda i, j, k: (j, 0, 0)),
            out_shape=(N_CHIPS, BR_WIDTH, _MERGE_TN), out_dtype=F32, acc_shape=(BR_WIDTH, _MERGE_TN), dims=TN))
    g["w_branch"] = jnp.concatenate(g_wb, axis=1)

    d_zu, d_zv, g["w_spatial"], g["b_spatial"], g["ln_v_g"], g["ln_v_b"] = _gmlp_bwd(
        proj, d_br[0], p["ln_v_g"], p["ln_v_b"], p["w_spatial"], p["b_spatial"])
    d_hq, d_hf, d_hi, d_hg, g["lb_logits"], g["hgrn_norm_g"] = _hgrn_bwd(
        proj, o_h, states, d_br[1], p["lb_logits"], p["hgrn_norm_g"], B, S)
    d_xq, d_kv = _attn_bwd(proj, kv, d_br[2], B, S)
    g["w_mem_kv"] = _mm_tn_rs("g_w_mem_kv", memn, d_kv, to=512)
    d_memn = _mm_nt_rs("d_memn", d_kv, p["w_mem_kv"], F32, to=512)
    _, g["mem_norm_g"] = _rms_bwd("mem_norm_bwd", mem, p["mem_norm_g"], d_memn, None)
    d_proj = jnp.concatenate([d_zu, d_zv, d_hq, d_hf, d_hi, d_hg, d_xq, d_g0, d_g1, d_g2], axis=1)
    d_h = _mm_nt_cs("d_h", d_proj, p["w_in"], F32)
    g["w_in"] = _mm_tn_cs("g_w_in", h, d_proj, N_CHIPS, to=512)
    grad_x, g["norm1_g"] = _rms_bwd("norm1_bwd", x, p["norm1_g"], d_h, d_x1)
    return loss[0, 0], grad_x, g


ANY = pl.BlockSpec(memory_space=pl.ANY)


def _place():
    x, y, c = lax.axis_index("x"), lax.axis_index("y"), lax.axis_index("c")
    other_chips = [(1 - x, y), (x, 1 - y), (1 - x, 1 - y)]
    return x, y, c, other_chips


def _remote(src, dst, send_sem, recv_sem, dev):
    return pltpu.make_async_remote_copy(src_ref=src, dst_ref=dst, send_sem=send_sem, recv_sem=recv_sem,
                                        device_id=dev, device_id_type=MESH_ID)


class _Exchange:
    def __init__(self, operands, out_shape, aliases, scratch, start, finish):
        self.operands, self.out_shape, self.aliases, self.scratch = operands, out_shape, aliases, scratch
        self.start, self.finish = start, finish


def _run_exchange(name, ex):
    n_in, n_out = len(ex.operands), len(ex.out_shape)

    def body(*refs):
        ins, outs, scr = refs[:n_in], refs[n_in:n_in + n_out], refs[n_in + n_out:]
        ex.start(ins, outs, scr)
        ex.finish(ins, outs, scr)

    return pl.pallas_call(
        body, name=name, in_specs=[ANY] * n_in, out_specs=(ANY,) * n_out, out_shape=tuple(ex.out_shape),
        input_output_aliases=ex.aliases, scratch_shapes=list(ex.scratch),
    )(*ex.operands)


def _ex_all_gather(slabs, halved):
    n = len(slabs)

    def rows(a, cc):
        if not halved[a]:
            return slice(None)
        hr = slabs[a].shape[1] // 2
        return pl.ds(cc * hr, hr)

    def ici(bufs, scr, a, j, chip, c, mine):
        px, py = chip
        x, y, _, _ = _place()
        qs = 2 * x + y if mine else 2 * px + py
        piece = bufs[a].at[qs, rows(a, c)]
        return _remote(piece, piece, scr[0].at[3 * a + j], scr[1].at[3 * a + j], (px, py, c))

    def d2d(bufs, scr, a, j, chip, cc):
        px, py = chip
        x, y, c, _ = _place()
        piece = bufs[a].at[2 * px + py, rows(a, cc)]
        return _remote(piece, piece, scr[2].at[3 * a + j], scr[3].at[3 * a + j], (x, y, 1 - c))

    def start(ins, outs, scr):
        _, _, c, chips = _place()
        for j, chip in enumerate(chips):
            for a in range(n):
                ici(outs, scr, a, j, chip, c, True).start()

    def finish(ins, outs, scr):
        _, _, c, chips = _place()
        for j, chip in enumerate(chips):
            for a in range(n):
                ici(outs, scr, a, j, chip, c, False).wait_recv()
                if halved[a]:
                    d2d(outs, scr, a, j, chip, c).start()
        for j, chip in enumerate(chips):
            for a in range(n):
                if halved[a]:
                    d2d(outs, scr, a, j, chip, 1 - c).wait_recv()
        for j, chip in enumerate(chips):
            for a in range(n):
                ici(outs, scr, a, j, chip, c, True).wait_send()
                if halved[a]:
                    d2d(outs, scr, a, j, chip, c).wait_send()

    return _Exchange(list(slabs), [jax.ShapeDtypeStruct(s.shape, s.dtype) for s in slabs],
                     {a: a for a in range(n)}, [pltpu.SemaphoreType.DMA((3 * n,))] * 4, start, finish)


def _ex_to_sibling(grads):
    n = len(grads)

    def copy(ins, outs, scr, a):
        x, y, c, _ = _place()
        hr = grads[a].shape[1] // 2
        return _remote(ins[a].at[:, pl.ds((1 - c) * hr, hr), :], outs[a], scr[0].at[a], scr[1].at[a], (x, y, 1 - c))

    def start(ins, outs, scr):
        for a in range(n):
            copy(ins, outs, scr, a).start()

    def finish(ins, outs, scr):
        for a in range(n):
            copy(ins, outs, scr, a).wait()

    out_shape = [jax.ShapeDtypeStruct((g.shape[0], g.shape[1] // 2, g.shape[2]), g.dtype) for g in grads]
    return _Exchange(list(grads), out_shape, {}, [pltpu.SemaphoreType.DMA((n,))] * 2, start, finish)


def _ex_to_owner(parts):
    n = len(parts)

    def copy(ins, outs, scr, a, j, chip):
        _, _, c, _ = _place()
        px, py = chip
        return _remote(ins[a].at[2 * px + py], outs[a].at[j], scr[0].at[3 * a + j], scr[1].at[3 * a + j],
                       (px, py, c))

    def start(ins, outs, scr):
        for j, chip in enumerate(_place()[3]):
            for a in range(n):
                copy(ins, outs, scr, a, j, chip).start()

    def finish(ins, outs, scr):
        for j, chip in enumerate(_place()[3]):
            for a in range(n):
                copy(ins, outs, scr, a, j, chip).wait()

    out_shape = [jax.ShapeDtypeStruct((3,) + p.shape[1:], p.dtype) for p in parts]
    return _Exchange(list(parts), out_shape, {}, [pltpu.SemaphoreType.DMA((3 * n,))] * 2, start, finish)


def _ex_share_halves(bufs):
    n = len(bufs)

    def copy(outs, scr, a, cc):
        x, y, c, _ = _place()
        hr = bufs[a].shape[0] // 2
        piece = outs[a].at[pl.ds(cc * hr, hr), :]
        return _remote(piece, piece, scr[0].at[a], scr[1].at[a], (x, y, 1 - c))

    def start(ins, outs, scr):
        c = _place()[2]
        for a in range(n):
            copy(outs, scr, a, c).start()

    def finish(ins, outs, scr):
        c = _place()[2]
        for a in range(n):
            copy(outs, scr, a, c).wait_send()
            copy(outs, scr, a, 1 - c).wait_recv()

    return _Exchange(list(bufs), [jax.ShapeDtypeStruct(b.shape, b.dtype) for b in bufs], {a: a for a in range(n)},
                     [pltpu.SemaphoreType.DMA((n,))] * 2, start, finish)


def _ex_gather_small(arrs):
    n = len(arrs)

    def peer_of(m):
        x, y, c, _ = _place()
        return (1 - x if m & 4 else x, 1 - y if m & 2 else y, 1 - c if m & 1 else c)

    def start(ins, outs, scr):
        x, y, c, _ = _place()
        for m in range(1, N_DEV):
            for a in range(n):
                k = (N_DEV - 1) * a + m - 1
                _remote(ins[a], outs[a].at[4 * x + 2 * y + c], scr[0].at[k], scr[1].at[k], peer_of(m)).start()

    def finish(ins, outs, scr):
        for m in range(1, N_DEV):
            px, py, pc = peer_of(m)
            for a in range(n):
                k = (N_DEV - 1) * a + m - 1
                slot = outs[a].at[4 * px + 2 * py + pc]
                cp = _remote(ins[a], slot, scr[0].at[k], scr[1].at[k], (px, py, pc))
                cp.wait_send()
                cp.wait_recv()

    slots = [jnp.zeros((N_DEV,) + a.shape, a.dtype) for a in arrs]
    out_shape = [jax.ShapeDtypeStruct(s.shape, s.dtype) for s in slots]
    return _Exchange(list(arrs) + slots, out_shape, {n + a: a for a in range(n)},
                     [pltpu.SemaphoreType.DMA(((N_DEV - 1) * n,))] * 2, start, finish)


def _div_tile(n, want):
    best = None
    for t in range(8, min(n, want) + 1, 8):
        if n % t == 0:
            best = t
    assert best is not None, n
    return best


def _cast_into_slab(name, w, place, dtype):
    r, cc = w.shape
    tr = r if r * cc <= 128 * 1024 else _div_tile(r, 256)

    def body(s_ref, w_ref, o_ref):
        o_ref[...] = w_ref[...].astype(o_ref.dtype)

    return pl.pallas_call(
        body, name=name,
        grid_spec=pltpu.PrefetchScalarGridSpec(
            num_scalar_prefetch=1, grid=(r // tr,),
            in_specs=[pl.BlockSpec((tr, cc), lambda i, s: (i, 0))],
            out_specs=pl.BlockSpec((None, tr, cc), lambda i, s: (s[0], i, 0))),
        out_shape=jax.ShapeDtypeStruct((N_CHIPS, r, cc), dtype), compiler_params=_cp("parallel"),
    )(place, w)


def _add_half(name, g, rcv, place):
    nq, r, cc = g.shape
    hr = r // 2

    def body(s_ref, g_ref, r_ref, o_ref):
        o_ref[...] = (g_ref[...] + r_ref[...]).astype(o_ref.dtype)

    spec = pl.BlockSpec((None, hr, cc), lambda i, s: (i, 0, 0))
    return pl.pallas_call(
        body, name=name,
        grid_spec=pltpu.PrefetchScalarGridSpec(
            num_scalar_prefetch=1, grid=(nq,),
            in_specs=[pl.BlockSpec((None, hr, cc), lambda i, s: (i, s[1], 0)), spec], out_specs=spec),
        out_shape=jax.ShapeDtypeStruct((nq, hr, cc), BF16), compiler_params=_cp("parallel"),
    )(place, g, rcv)


def _sum_owner(name, part, rcv, place):
    _, hr, cc = part.shape
    tr = _div_tile(hr, 128)
    nb = hr // tr

    def body(s_ref, p_ref, r_ref, o_ref):
        o_ref[...] = ((p_ref[...].astype(F32) + r_ref[0].astype(F32)) + r_ref[1].astype(F32)) + r_ref[2].astype(F32)

    return pl.pallas_call(
        body, name=name,
        grid_spec=pltpu.PrefetchScalarGridSpec(
            num_scalar_prefetch=1, grid=(nb,),
            in_specs=[pl.BlockSpec((None, tr, cc), lambda i, s: (s[0], i, 0)),
                      pl.BlockSpec((3, tr, cc), lambda i, s: (0, i, 0))],
            out_specs=pl.BlockSpec((tr, cc), lambda i, s: (s[1] * nb + i, 0))),
        out_shape=jax.ShapeDtypeStruct((2 * hr, cc), F32), compiler_params=_cp("parallel"),
    )(place, part, rcv)


def _sum_small(gathered, local, place):
    n = len(gathered)

    def body(s_ref, *refs):
        g_refs, l_refs, o_refs = refs[:n], refs[n:2 * n], refs[2 * n:]
        me = s_ref[2]
        for g_ref, l_ref, o_ref in zip(g_refs, l_refs, o_refs):
            acc = None
            for d in range(N_DEV):
                term = jnp.where(me == d, l_ref[...], g_ref[d])
                acc = term if acc is None else acc + term
            o_ref[...] = acc

    def whole(shape):
        return pl.BlockSpec(shape, lambda i, s, nd=len(shape): (0,) * nd)

    return pl.pallas_call(
        body, name="sum_small",
        grid_spec=pltpu.PrefetchScalarGridSpec(
            num_scalar_prefetch=1, grid=(1,),
            in_specs=[whole(g.shape) for g in gathered] + [whole(a.shape) for a in local],
            out_specs=tuple(whole(a.shape) for a in local)),
        out_shape=tuple(jax.ShapeDtypeStruct(a.shape, a.dtype) for a in local), compiler_params=_cp("arbitrary"),
    )(place, *gathered, *local)


def _adamw(name, w, g, m, v):
    r, cc = w.shape
    tr = r if r * cc <= 128 * 1024 else _div_tile(r, 256)

    def body(w_ref, g_ref, m_ref, v_ref, d_ref, mo_ref, vo_ref):
        gv = g_ref[...]
        mn = ADAM_B1 * m_ref[...] + (1.0 - ADAM_B1) * gv
        vn = ADAM_B2 * v_ref[...] + (1.0 - ADAM_B2) * (gv * gv)
        m_hat = mn / (1.0 - ADAM_B1 ** ADAM_STEP)
        v_hat = vn / (1.0 - ADAM_B2 ** ADAM_STEP)
        d_ref[...] = -ADAM_LR * (m_hat / (jnp.sqrt(v_hat) + ADAM_EPS) + ADAM_WD * w_ref[...])
        mo_ref[...] = mn
        vo_ref[...] = vn

    spec = pl.BlockSpec((tr, cc), lambda i: (i, 0))
    sd = jax.ShapeDtypeStruct((r, cc), F32)
    return pl.pallas_call(
        body, name=name, grid=(r // tr,), in_specs=[spec] * 4, out_specs=(spec,) * 3, out_shape=(sd,) * 3,
        compiler_params=_cp("parallel"),
    )(w, g, m, v)


_BIG = ("w_in", "w_up", "w_branch", "w_mem_kv", "w_out", "w_down")
_BIG_SHARD_SHAPE = {"w_in": (1024, 1664), "w_up": (1024, 1408), "w_branch": (1536, 256),
                    "w_mem_kv": (256, 1024), "w_out": (256, 1024), "w_down": (704, 1024)}
_MISC = (("loss", 1), ("norm1_g", 1024), ("ln_v_g", 512), ("ln_v_b", 512), ("b_spatial", 512), ("lb_logits", 1024),
         ("hgrn_norm_g", 128), ("mem_norm_g", 1024), ("norm2_g", 1024), ("conv_b", D_FF), ("final_g", 1024))
_PARAM_ORDER = ("norm1_g", "w_in", "ln_v_g", "ln_v_b", "w_spatial", "b_spatial", "lb_logits", "hgrn_norm_g",
                "mem_norm_g", "w_mem_kv", "w_branch", "w_out", "norm2_g", "w_up", "conv_w", "conv_b", "w_down",
                "final_g")


def _misc_rows(cnt):
    return -(-cnt // 1024) * 8


def _pack_misc(parts):
    pieces = []
    for name, cnt in _MISC:
        rows = _misc_rows(cnt)
        if name in parts:
            flat = jnp.reshape(parts[name], (-1,)).astype(F32)
            pieces.append(jnp.pad(flat, (0, rows * 128 - cnt)).reshape(rows, 128))
        else:
            pieces.append(jnp.zeros((rows, 128), F32))
    return jnp.concatenate(pieces, axis=0)


def _unpack_misc(packed):
    out, off = {}, 0
    for name, cnt in _MISC:
        rows = _misc_rows(cnt)
        out[name] = packed[off:off + rows].reshape(-1)[:cnt]
        off += rows
    return out


def kernel(x, mem, norm1_g, w_in, ln_v_g, ln_v_b, w_spatial, b_spatial, lb_logits, hgrn_norm_g, mem_norm_g, w_mem_kv, w_branch, w_out, norm2_g, w_up, conv_w, conv_b, w_down, final_g, loss_target, m_norm1_g, m_w_in, m_ln_v_g, m_ln_v_b, m_w_spatial, m_b_spatial, m_lb_logits, m_hgrn_norm_g, m_mem_norm_g, m_w_mem_kv, m_w_branch, m_w_out, m_norm2_g, m_w_up, m_conv_w, m_conv_b, m_w_down, m_final_g, v_norm1_g, v_w_in, v_ln_v_g, v_ln_v_b, v_w_spatial, v_b_spatial, v_lb_logits, v_hgrn_norm_g, v_mem_norm_g, v_w_mem_kv, v_w_branch, v_w_out, v_norm2_g, v_w_up, v_conv_w, v_conv_b, v_w_down, v_final_g):
    w = dict(norm1_g=norm1_g, w_in=w_in, ln_v_g=ln_v_g, ln_v_b=ln_v_b, w_spatial=w_spatial, b_spatial=b_spatial,
             lb_logits=lb_logits, hgrn_norm_g=hgrn_norm_g, mem_norm_g=mem_norm_g, w_mem_kv=w_mem_kv,
             w_branch=w_branch, w_out=w_out, norm2_g=norm2_g, w_up=w_up, conv_w=conv_w, conv_b=conv_b,
             w_down=w_down, final_g=final_g)
    mom = dict(norm1_g=m_norm1_g, w_in=m_w_in, ln_v_g=m_ln_v_g, ln_v_b=m_ln_v_b, w_spatial=m_w_spatial,
               b_spatial=m_b_spatial, lb_logits=m_lb_logits, hgrn_norm_g=m_hgrn_norm_g, mem_norm_g=m_mem_norm_g,
               w_mem_kv=m_w_mem_kv, w_branch=m_w_branch, w_out=m_w_out, norm2_g=m_norm2_g, w_up=m_w_up,
               conv_w=m_conv_w, conv_b=m_conv_b, w_down=m_w_down, final_g=m_final_g)
    var = dict(norm1_g=v_norm1_g, w_in=v_w_in, ln_v_g=v_ln_v_g, ln_v_b=v_ln_v_b, w_spatial=v_w_spatial,
               b_spatial=v_b_spatial, lb_logits=v_lb_logits, hgrn_norm_g=v_hgrn_norm_g, mem_norm_g=v_mem_norm_g,
               w_mem_kv=v_w_mem_kv, w_branch=v_w_branch, w_out=v_w_out, norm2_g=v_norm2_g, w_up=v_w_up,
               conv_w=v_conv_w, conv_b=v_conv_b, w_down=v_w_down, final_g=v_final_g)
    B, S, D = x.shape
    T = B * S
    ci = lax.axis_index("c")
    q = 2 * lax.axis_index("x") + lax.axis_index("y")
    place = jnp.stack([q, ci, 2 * q + ci]).astype(jnp.int32)

    slabs = [_cast_into_slab("slab_" + n, w[n].reshape(_BIG_SHARD_SHAPE[n]), place, BF16) for n in _BIG]
    slabs.append(_cast_into_slab("slab_conv_w", conv_w[0], place, F32))
    gathered = _run_exchange("all_gather_weights", _ex_all_gather(slabs, [True] * len(_BIG) + [False]))
    full, conv_all = dict(zip(_BIG, gathered[:-1])), gathered[-1]
    p = dict(
        norm1_g=norm1_g, ln_v_g=ln_v_g, ln_v_b=ln_v_b, w_spatial=w_spatial[0],
        b_spatial=b_spatial.reshape(GM_GROUPS, GM_CHUNK, 1), lb_logits=lb_logits, hgrn_norm_g=hgrn_norm_g,
        mem_norm_g=mem_norm_g, norm2_g=norm2_g, conv_b=conv_b, final_g=final_g.reshape(1, D),
        conv_w=jnp.transpose(conv_all, (1, 0, 2)).reshape(3, 1, D_FF),
        w_in=full["w_in"], w_up=full["w_up"], w_branch=full["w_branch"],
        w_mem_kv=full["w_mem_kv"].reshape(D, 2 * XA_HEADS * XA_DIM), w_out=full["w_out"].reshape(D, D),
        w_down=full["w_down"].reshape(D_FF, D))

    loss, grad_x, g = _local_step(x.reshape(T, D), mem.reshape(B * MEM_LEN, D), loss_target.reshape(T, D), p, B, S)

    big = [g[n].reshape((N_CHIPS,) + _BIG_SHARD_SHAPE[n]) for n in _BIG]
    from_sib = _run_exchange("rs_to_sibling", _ex_to_sibling(big))
    parts = [_add_half("rs_add_" + n, gb, r, place) for n, gb, r in zip(_BIG, big, from_sib)]
    from_chips = _run_exchange("rs_to_owner", _ex_to_owner(parts))
    sums = [_sum_owner("rs_sum_" + n, pt, r, place) for n, pt, r in zip(_BIG, parts, from_chips)]
    shard_grads = dict(zip(_BIG, _run_exchange("rs_share_halves", _ex_share_halves(sums))))

    misc = dict(g)
    misc["loss"] = loss
    local_small = [g["w_spatial"].reshape(GM_GROUPS * GM_CHUNK, GM_CHUNK), g["conv_w"].reshape(3, D_FF),
                   _pack_misc(misc)]
    everyone = _run_exchange("gather_small", _ex_gather_small(local_small))
    tot_ws, tot_cw, tot_misc = _sum_small(everyone, local_small, place)
    total = _unpack_misc(tot_misc)

    grads, delta, new_m, new_v = {}, {}, {}, {}
    for n in _BIG:
        shp = _BIG_SHARD_SHAPE[n]
        grads[n] = shard_grads[n]
        delta[n], new_m[n], new_v[n] = _adamw("adamw_" + n, w[n].reshape(shp), shard_grads[n],
                                              mom[n].reshape(shp), var[n].reshape(shp))
    misc_names = [n for n, _ in _MISC if n != "loss"]
    pk = _adamw("adamw_misc", _pack_misc({n: w[n] for n in misc_names}), tot_misc,
                _pack_misc({n: mom[n] for n in misc_names}), _pack_misc({n: var[n] for n in misc_names}))
    pk = [_unpack_misc(a) for a in pk]
    for n in misc_names:
        grads[n] = total[n]
        delta[n], new_m[n], new_v[n] = pk[0][n], pk[1][n], pk[2][n]
    ws_shape = (GM_GROUPS * GM_CHUNK, GM_CHUNK)
    grads["w_spatial"] = tot_ws
    delta["w_spatial"], new_m["w_spatial"], new_v["w_spatial"] = _adamw(
        "adamw_w_spatial", w_spatial.reshape(ws_shape), tot_ws, m_w_spatial.reshape(ws_shape),
        v_w_spatial.reshape(ws_shape))
    cw_shard = D_FF // N_CHIPS
    grads["conv_w"] = lax.dynamic_slice(tot_cw, (0, q * cw_shard), (3, cw_shard))
    delta["conv_w"], new_m["conv_w"], new_v["conv_w"] = _adamw(
        "adamw_conv_w", conv_w[0], grads["conv_w"], m_conv_w[0], v_conv_w[0])

    def shaped(d):
        return [d[n].reshape(w[n].shape) for n in _PARAM_ORDER]

    return (total["loss"].reshape(()), grad_x.reshape(B, S, D), *shaped(grads), *shaped(delta), *shaped(new_m),
            *shaped(new_v))
```

```python
import functools
import math

import jax
import jax.numpy as jnp
from jax import lax
from jax.experimental import pallas as pl
from jax.experimental.pallas import tpu as pltpu

F32 = jnp.float32
BF16 = jnp.bfloat16
EPS = 1e-6

D_MODEL = 1024
MEM_LEN = 256
GM_WIDTH = 512
GM_CHUNK = 128
GM_GROUPS = 4
HG_HEADS = 4
HG_DIM = 128
HG_CHUNK = 64
XA_HEADS = 4
XA_DIM = 128
BR_WIDTH = 512
D_FF = 2816
IN_WIDTH = 6656
N_CHIPS = 4
N_DEV = 8

ADAM_LR = 0.001
ADAM_B1 = 0.9
ADAM_B2 = 0.999
ADAM_EPS = 1e-08
ADAM_WD = 0.01
ADAM_STEP = 10

COL_ZU, COL_ZV, COL_HQ, COL_HF, COL_HI, COL_HG, COL_XQ = 0, 1, 2, 3, 4, 5, 6
COL_GATE0 = 3584

VMEM_LIMIT_BYTES = 48 * 1024 * 1024
MESH_ID = pl.DeviceIdType.MESH


def _cp(*sem):
    return pltpu.CompilerParams(dimension_semantics=sem, vmem_limit_bytes=VMEM_LIMIT_BYTES)


def _dot(a, b):
    return lax.dot_general(a.astype(BF16), b.astype(BF16), (((1,), (0,)), ((), ())), preferred_element_type=F32)


def _dot_nt(a, b):
    return lax.dot_general(a.astype(BF16), b.astype(BF16), (((1,), (1,)), ((), ())), preferred_element_type=F32)


def _dot_tn(a, b):
    return lax.dot_general(a.astype(BF16), b.astype(BF16), (((0,), (0,)), ((), ())), preferred_element_type=F32)


def _split2(x):
    hi = x.astype(BF16)
    return hi, (x - hi.astype(F32)).astype(BF16)


def _dot3(a, b, dims):
    ah, al = _split2(a)
    bh, bl = _split2(b)
    dn = (dims, ((), ()))
    return (lax.dot_general(ah, bh, dn, preferred_element_type=F32)
            + lax.dot_general(ah, bl, dn, preferred_element_type=F32)
            + lax.dot_general(al, bh, dn, preferred_element_type=F32))


def _dot_01(mask01, x):
    hi = x.astype(BF16)
    r1 = x - hi.astype(F32)
    mid = r1.astype(BF16)
    lo = (r1 - mid.astype(F32)).astype(BF16)
    m = mask01.astype(BF16)
    dn = (((1,), (0,)), ((), ()))
    return (lax.dot_general(m, hi, dn, preferred_element_type=F32)
            + lax.dot_general(m, mid, dn, preferred_element_type=F32)
            + lax.dot_general(m, lo, dn, preferred_element_type=F32))


def _sigmoid(z):
    return 1.0 / (1.0 + jnp.exp(-z))


_GELU_C = math.sqrt(2.0 / math.pi)


def _gelu_and_grad(z):
    inner = _GELU_C * (z + 0.044715 * z * z * z)
    t = jnp.tanh(inner)
    val = 0.5 * z * (1.0 + t)
    grad = 0.5 * (1.0 + t) + 0.5 * z * (1.0 - t * t) * _GELU_C * (1.0 + 3.0 * 0.044715 * z * z)
    return val, grad


def _row_tile(n, want=512):
    t = min(want, n)
    assert n % t == 0
    return t


def _pcall(body, operands, *, name, grid, in_specs, out_specs, out_shape, scratch_shapes=(), semantics, riders=()):
    single = not isinstance(out_shape, (tuple, list))
    out_specs = (out_specs,) if single else tuple(out_specs)
    out_shape = (out_shape,) if single else tuple(out_shape)
    if not riders:
        res = pl.pallas_call(body, name=name, grid=grid, in_specs=list(in_specs), out_specs=out_specs,
                             out_shape=out_shape, scratch_shapes=list(scratch_shapes),
                             compiler_params=_cp(*semantics))(*operands)
        return (res[0] if single else res), []
    n_in, n_out, n_scr = len(in_specs), len(out_shape), len(scratch_shapes)
    ex_in = [len(ex.operands) for ex in riders]
    ex_out = [len(ex.out_shape) for ex in riders]
    ex_scr = [len(ex.scratch) for ex in riders]
    tot_in, tot_out = n_in + sum(ex_in), n_out + sum(ex_out)

    def wrapped(*refs):
        ins, outs, scr = refs[:tot_in], refs[tot_in:tot_in + tot_out], refs[tot_in + tot_out:]
        ids = [pl.program_id(d) for d in range(len(grid))]
        first = functools.reduce(lambda p, t: p & t, [i == 0 for i in ids])
        last = functools.reduce(lambda p, t: p & t, [i == n - 1 for i, n in zip(ids, grid)])
        parts, oi, oo, os_ = [], n_in, n_out, n_scr
        for k in range(len(riders)):
            parts.append((ins[oi:oi + ex_in[k]], outs[oo:oo + ex_out[k]], scr[os_:os_ + ex_scr[k]]))
            oi, oo, os_ = oi + ex_in[k], oo + ex_out[k], os_ + ex_scr[k]

        @pl.when(first)
        def _():
            for ex, part in zip(riders, parts):
                ex.start(*part)

        body(*ins[:n_in], *outs[:n_out], *scr[:n_scr])

        @pl.when(last)
        def _():
            for ex, part in zip(riders, parts):
                ex.finish(*part)

    aliases, oi, oo = {}, n_in, n_out
    all_ops, all_shapes, all_scr = list(operands), list(out_shape), list(scratch_shapes)
    for k, ex in enumerate(riders):
        aliases.update({oi + a: oo + b for a, b in ex.aliases.items()})
        oi, oo = oi + ex_in[k], oo + ex_out[k]
        all_ops += list(ex.operands)
        all_shapes += list(ex.out_shape)
        all_scr += list(ex.scratch)
    res = pl.pallas_call(
        wrapped, name=name, grid=grid, in_specs=list(in_specs) + [ANY] * sum(ex_in),
        out_specs=out_specs + (ANY,) * sum(ex_out), out_shape=tuple(all_shapes), scratch_shapes=all_scr,
        input_output_aliases=aliases, compiler_params=_cp(*(["arbitrary"] * len(grid))))(*all_ops)
    own = res[0] if single else tuple(res[:n_out])
    carried, oo = [], n_out
    for k in range(len(riders)):
        carried.append(list(res[oo:oo + ex_out[k]]))
        oo += ex_out[k]
    return own, carried


def _matmul(name, operands, *, grid, in_specs, o_spec, out_shape, out_dtype, acc_shape, dims, has_res=False,
            riders=()):
    nk = grid[2]

    def body(*refs):
        if has_res:
            a_ref, b_ref, r_ref, o_ref, acc = refs
        else:
            a_ref, b_ref, o_ref, acc = refs
            r_ref = None
        k = pl.program_id(2)

        @pl.when(k == 0)
        def _():
            acc[...] = jnp.zeros_like(acc)

        acc[...] += lax.dot_general(a_ref[...].astype(BF16), b_ref[...].astype(BF16), (dims, ((), ())),
                                    preferred_element_type=F32)

        @pl.when(k == nk - 1)
        def _():
            r = acc[...]
            if r_ref is not None:
                r = r + r_ref[...]
            o_ref[...] = r.astype(o_ref.dtype)

    out, carried = _pcall(body, operands, name=name, grid=grid, in_specs=in_specs, out_specs=o_spec,
                          out_shape=jax.ShapeDtypeStruct(out_shape, out_dtype),
                          scratch_shapes=[pltpu.VMEM(acc_shape, F32)],
                          semantics=("parallel", "parallel", "arbitrary"), riders=riders)
    return (out, carried) if riders else out


NN = ((1,), (0,))
NT = ((1,), (1,))
TN = ((0,), (0,))


def _mm_cs(name, a, w, out_dtype, riders=()):
    M, K = a.shape
    nq, _, wd = w.shape
    tm = _row_tile(M)
    return _matmul(name, (a, w), grid=(M // tm, nq, 1),
                   in_specs=[pl.BlockSpec((tm, K), lambda i, j, k: (i, 0)),
                             pl.BlockSpec((None, K, wd), lambda i, j, k: (j, 0, 0))],
                   o_spec=pl.BlockSpec((tm, wd), lambda i, j, k: (i, j)),
                   out_shape=(M, nq * wd), out_dtype=out_dtype, acc_shape=(tm, wd), dims=NN, riders=riders)


def _mm_rs(name, a, w, out_dtype, res=None, tn=512):
    M, K = a.shape
    N = w.shape[1]
    tm = _row_tile(M)
    tn = min(tn, N)
    ops = (a, w) if res is None else (a, w, res)
    in_specs = [pl.BlockSpec((tm, K), lambda i, j, k: (i, 0)),
                pl.BlockSpec((K, tn), lambda i, j, k: (0, j))]
    if res is not None:
        in_specs.append(pl.BlockSpec((tm, tn), lambda i, j, k: (i, j)))
    return _matmul(name, ops, grid=(M // tm, N // tn, 1), in_specs=in_specs,
                   o_spec=pl.BlockSpec((tm, tn), lambda i, j, k: (i, j)),
                   out_shape=(M, N), out_dtype=out_dtype, acc_shape=(tm, tn), dims=NN, has_res=res is not None)


def _mm_nt_rs(name, g, w, out_dtype, to):
    M, N = g.shape
    K = w.shape[0]
    tm = _row_tile(M)
    return _matmul(name, (g, w), grid=(M // tm, K // to, 1),
                   in_specs=[pl.BlockSpec((tm, N), lambda i, j, k: (i, 0)),
                             pl.BlockSpec((to, N), lambda i, j, k: (j, 0))],
                   o_spec=pl.BlockSpec((tm, to), lambda i, j, k: (i, j)),
                   out_shape=(M, K), out_dtype=out_dtype, acc_shape=(tm, to), dims=NT)


def _mm_nt_cs(name, g, w, out_dtype, riders=()):
    M = g.shape[0]
    nq, K, wd = w.shape
    tm = _row_tile(M)
    return _matmul(name, (g, w), grid=(M // tm, 1, nq),
                   in_specs=[pl.BlockSpec((tm, wd), lambda i, j, k: (i, k)),
                             pl.BlockSpec((None, K, wd), lambda i, j, k: (k, 0, 0))],
                   o_spec=pl.BlockSpec((tm, K), lambda i, j, k: (i, 0)),
                   out_shape=(M, K), out_dtype=out_dtype, acc_shape=(tm, K), dims=NT, riders=riders)


def _mm_tn_rs(name, a, g, to, tn=512):
    T, M = a.shape
    N = g.shape[1]
    tt = _row_tile(T)
    tn = min(tn, N)
    return _matmul(name, (a, g), grid=(M // to, N // tn, T // tt),
                   in_specs=[pl.BlockSpec((tt, to), lambda i, j, k: (k, i)),
                             pl.BlockSpec((tt, tn), lambda i, j, k: (k, j))],
                   o_spec=pl.BlockSpec((to, tn), lambda i, j, k: (i, j)),
                   out_shape=(M, N), out_dtype=F32, acc_shape=(to, tn), dims=TN)


def _mm_tn_cs(name, a, g, nq, to, riders=()):
    T, M = a.shape
    wd = g.shape[1] // nq
    tt = _row_tile(T)
    return _matmul(name, (a, g), grid=(M // to, nq, T // tt),
                   in_specs=[pl.BlockSpec((tt, to), lambda i, j, k: (k, i)),
                             pl.BlockSpec((tt, wd), lambda i, j, k: (k, j))],
                   o_spec=pl.BlockSpec((None, to, wd), lambda i, j, k: (j, i, 0)),
                   out_shape=(nq, M, wd), out_dtype=F32, acc_shape=(to, wd), dims=TN, riders=riders)


def _rms_fwd(name, x, g):
    T, D = x.shape
    tm = _row_tile(T)

    def body(x_ref, g_ref, o_ref):
        xv = x_ref[...]
        r = lax.rsqrt(jnp.mean(xv * xv, axis=-1, keepdims=True) + EPS)
        o_ref[...] = (xv * r * g_ref[...]).astype(o_ref.dtype)

    return pl.pallas_call(
        body, name=name, grid=(T // tm,),
        in_specs=[pl.BlockSpec((tm, D), lambda i: (i, 0)), pl.BlockSpec((1, D), lambda i: (0, 0))],
        out_specs=pl.BlockSpec((tm, D), lambda i: (i, 0)),
        out_shape=jax.ShapeDtypeStruct((T, D), BF16), compiler_params=_cp("parallel"),
    )(x, g)


def _rms_bwd(name, x, g, dh, dres, riders=()):
    T, D = x.shape
    tm = _row_tile(T)
    has_res = dres is not None

    def body(*refs):
        if has_res:
            x_ref, g_ref, dh_ref, dr_ref, dx_ref, dg_ref = refs
        else:
            x_ref, g_ref, dh_ref, dx_ref, dg_ref = refs

        @pl.when(pl.program_id(0) == 0)
        def _():
            dg_ref[...] = jnp.zeros_like(dg_ref)

        xv = x_ref[...]
        r = lax.rsqrt(jnp.mean(xv * xv, axis=-1, keepdims=True) + EPS)
        n = xv * r
        dhv = dh_ref[...]
        dg_ref[...] += jnp.sum(dhv * n, axis=0, keepdims=True)
        dn = dhv * g_ref[...]
        dx = r * (dn - n * jnp.mean(dn * n, axis=-1, keepdims=True))
        if has_res:
            dx = dx + dr_ref[...]
        dx_ref[...] = dx

    row = pl.BlockSpec((tm, D), lambda i: (i, 0))
    vec = pl.BlockSpec((1, D), lambda i: (0, 0))
    ops = (x, g, dh, dres) if has_res else (x, g, dh)
    out, carried = _pcall(
        body, ops, name=name, grid=(T // tm,),
        in_specs=[row, vec, row] + ([row] if has_res else []),
        out_specs=(row, vec),
        out_shape=(jax.ShapeDtypeStruct((T, D), F32), jax.ShapeDtypeStruct((1, D), F32)),
        semantics=("arbitrary",), riders=riders)
    return (out, carried) if riders else out


def _loss_head(x2, tgt, g):
    T, D = x2.shape
    tm = _row_tile(T)

    def body(x_ref, t_ref, g_ref, dx_ref, dg_ref, loss_ref):
        @pl.when(pl.program_id(0) == 0)
        def _():
            dg_ref[...] = jnp.zeros_like(dg_ref)
            loss_ref[...] = jnp.zeros_like(loss_ref)

        xv = x_ref[...]
        gv = g_ref[...]
        r = lax.rsqrt(jnp.mean(xv * xv, axis=-1, keepdims=True) + EPS)
        n = xv * r
        diff = n * gv - t_ref[...]
        loss_ref[...] += 0.5 * jnp.sum(jnp.mean(diff * diff, axis=-1, keepdims=True))
        dy = diff * (1.0 / D)
        dg_ref[...] += jnp.sum(dy * n, axis=0, keepdims=True)
        dn = dy * gv
        dx_ref[...] = r * (dn - n * jnp.mean(dn * n, axis=-1, keepdims=True))

    row = pl.BlockSpec((tm, D), lambda i: (i, 0))
    vec = pl.BlockSpec((1, D), lambda i: (0, 0))
    return pl.pallas_call(
        body, name="loss_head", grid=(T // tm,),
        in_specs=[row, row, vec],
        out_specs=(row, vec, pl.BlockSpec((8, 128), lambda i: (0, 0))),
        out_shape=(jax.ShapeDtypeStruct((T, D), F32), jax.ShapeDtypeStruct((1, D), F32),
                   jax.ShapeDtypeStruct((8, 128), F32)),
        compiler_params=_cp("arbitrary"),
    )(x2, tgt, g)


def _gmlp_pieces(zu, zv, lng, lnb, ws_ref, bs_ref):
    u, du = _gelu_and_grad(zu)
    v, dv = _gelu_and_grad(zv)
    mu = jnp.mean(v, axis=-1, keepdims=True)
    vc = v - mu
    rstd = lax.rsqrt(jnp.mean(vc * vc, axis=-1, keepdims=True) + EPS)
    vhat = vc * rstd
    vn = vhat * lng + lnb
    row = lax.broadcasted_iota(jnp.int32, (GM_CHUNK, GM_CHUNK), 0)
    col = lax.broadcasted_iota(jnp.int32, (GM_CHUNK, GM_CHUNK), 1)
    tril = row >= col
    wms, mixed = [], []
    for g in range(GM_GROUPS):
        sl = slice(g * 128, (g + 1) * 128)
        wm = jnp.where(tril, ws_ref[g], 0.0)
        wms.append(wm)
        mixed.append(_dot(wm, vn[:, sl]) + bs_ref[g])
    return u, du, dv, rstd, vhat, vn, wms, mixed, tril


def _gmlp_fwd(proj, lng, lnb, ws, bs_col):
    T = proj.shape[0]
    n = T // GM_CHUNK

    def body(zu_ref, zv_ref, lng_ref, lnb_ref, ws_ref, bs_ref, o_ref):
        u, _, _, _, _, _, _, mixed, _ = _gmlp_pieces(zu_ref[...], zv_ref[...], lng_ref[...], lnb_ref[...],
                                                     ws_ref, bs_ref)
        for g in range(GM_GROUPS):
            sl = slice(g * 128, (g + 1) * 128)
            o_ref[:, sl] = (u[:, sl] * mixed[g]).astype(o_ref.dtype)

    vec = pl.BlockSpec((1, GM_WIDTH), lambda i: (0, 0))
    return pl.pallas_call(
        body, name="gmlp_fwd", grid=(n,),
        in_specs=[pl.BlockSpec((GM_CHUNK, 512), lambda i: (i, COL_ZU)),
                  pl.BlockSpec((GM_CHUNK, 512), lambda i: (i, COL_ZV)),
                  vec, vec,
                  pl.BlockSpec((GM_GROUPS, 128, 128), lambda i: (0, 0, 0)),
                  pl.BlockSpec((GM_GROUPS, 128, 1), lambda i: (0, 0, 0))],
        out_specs=pl.BlockSpec((GM_CHUNK, 512), lambda i: (i, 0)),
        out_shape=jax.ShapeDtypeStruct((T, GM_WIDTH), BF16), compiler_params=_cp("parallel"),
    )(proj, proj, lng, lnb, ws, bs_col)


def _gmlp_bwd(proj, d_out, lng, lnb, ws, bs_col):
    T = proj.shape[0]
    n = T // GM_CHUNK

    def body(zu_ref, zv_ref, do_ref, lng_ref, lnb_ref, ws_ref, bs_ref,
             dzu_ref, dzv_ref, dws_ref, dbs_ref, dlng_ref, dlnb_ref, dm_acc):
        i = pl.program_id(0)

        @pl.when(i == 0)
        def _():
            dws_ref[...] = jnp.zeros_like(dws_ref)
            dlng_ref[...] = jnp.zeros_like(dlng_ref)
            dlnb_ref[...] = jnp.zeros_like(dlnb_ref)
            dm_acc[...] = jnp.zeros_like(dm_acc)

        lng_v = lng_ref[...]
        u, du, dv, rstd, vhat, vn, wms, mixed, tril = _gmlp_pieces(zu_ref[...], zv_ref[...], lng_v, lnb_ref[...],
                                                                  ws_ref, bs_ref)
        do = do_ref[...]
        dvn_parts = []
        for g in range(GM_GROUPS):
            sl = slice(g * 128, (g + 1) * 128)
            dog = do[:, sl]
            dzu_ref[:, sl] = (dog * mixed[g] * du[:, sl]).astype(dzu_ref.dtype)
            dmix = dog * u[:, sl]
            dm_acc[:, sl] += dmix
            dws_ref[g] += jnp.where(tril, _dot_nt(dmix, vn[:, sl]), 0.0)
            dvn_parts.append(_dot_tn(wms[g], dmix))
        dvn = jnp.concatenate(dvn_parts, axis=1)
        dlng_ref[...] += jnp.sum(dvn * vhat, axis=0, keepdims=True)
        dlnb_ref[...] += jnp.sum(dvn, axis=0, keepdims=True)
        dvh = dvn * lng_v
        dvv = rstd * (dvh - jnp.mean(dvh, axis=-1, keepdims=True)
                      - vhat * jnp.mean(dvh * vhat, axis=-1, keepdims=True))
        dzv_ref[...] = (dvv * dv).astype(dzv_ref.dtype)

        @pl.when(i == n - 1)
        def _():
            for g in range(GM_GROUPS):
                dbs_ref[g] = jnp.sum(dm_acc[:, g * 128:(g + 1) * 128], axis=1, keepdims=True)

    vec = pl.BlockSpec((1, GM_WIDTH), lambda i: (0, 0))
    wsp = pl.BlockSpec((GM_GROUPS, 128, 128), lambda i: (0, 0, 0))
    bsp = pl.BlockSpec((GM_GROUPS, 128, 1), lambda i: (0, 0, 0))
    tile = pl.BlockSpec((GM_CHUNK, 512), lambda i: (i, 0))
    return pl.pallas_call(
        body, name="gmlp_bwd", grid=(n,),
        in_specs=[pl.BlockSpec((GM_CHUNK, 512), lambda i: (i, COL_ZU)),
                  pl.BlockSpec((GM_CHUNK, 512), lambda i: (i, COL_ZV)),
                  tile, vec, vec, wsp, bsp],
        out_specs=(tile, tile, wsp, bsp, vec, vec),
        out_shape=(jax.ShapeDtypeStruct((T, GM_WIDTH), BF16), jax.ShapeDtypeStruct((T, GM_WIDTH), BF16),
                   jax.ShapeDtypeStruct((GM_GROUPS, 128, 128), F32), jax.ShapeDtypeStruct((GM_GROUPS, 128, 1), F32),
                   jax.ShapeDtypeStruct((1, GM_WIDTH), F32), jax.ShapeDtypeStruct((1, GM_WIDTH), F32)),
        scratch_shapes=[pltpu.VMEM((GM_CHUNK, GM_WIDTH), F32)],
        compiler_params=_cp("arbitrary"),
    )(proj, proj, d_out, lng, lnb, ws, bs_col)


def _hgrn_lower_bound(lbl):
    return 1.0 / (1.0 + jnp.exp(lbl[1:2, :] - lbl[0:1, :]))


def _hgrn_gates(hq, hf, lb):
    sg = _sigmoid(hf)
    fg = lb + (1.0 - lb) * sg
    sq = _sigmoid(hq)
    C = HG_CHUNK
    row = lax.broadcasted_iota(jnp.int32, (C, C), 0)
    col = lax.broadcasted_iota(jnp.int32, (C, C), 1)
    tril = row >= col
    logf = jnp.log(fg)
    a = _dot_01(tril, logf)
    a_last = jnp.sum(logf, axis=0, keepdims=True)
    first_half = lax.broadcasted_iota(jnp.int32, logf.shape, 0) < (C // 2)
    a_mid = jnp.sum(jnp.where(first_half, logf, 0.0), axis=0, keepdims=True)
    return sg, fg, sq, tril, a, a_last, a_mid


def _hgrn_fwd(proj, lbl, gh, B, S, riders=()):
    T = B * S
    C = HG_CHUNK
    NC = S // C
    W = HG_HEADS * HG_DIM

    def body(q_ref, f_ref, i_ref, g_ref, lbl_ref, gh_ref, o_ref, bo_ref, st_ref, state):
        @pl.when(pl.program_id(1) == 0)
        def _():
            state[...] = jnp.zeros_like(state)

        lb_all = _hgrn_lower_bound(lbl_ref[...])
        ghv = gh_ref[...]
        for h in range(HG_HEADS):
            sl = slice(h * 128, (h + 1) * 128)
            hq = q_ref[:, sl]
            sg, fg, sq, tril, a, a_last, a_mid = _hgrn_gates(hq, f_ref[:, sl], lb_all[:, sl])
            k = 1.0 - fg
            q = hq * sq
            v = i_ref[:, sl]
            qe = q * jnp.exp(a)
            qi = q * jnp.exp(a - a_mid)
            ki = k * jnp.exp(a_mid - a)
            kl = k * jnp.exp(a_last - a)
            p = jnp.where(tril, _dot_nt(qi, ki), 0.0)
            st = state[h]
            st_ref[h] = st
            o = _dot_nt(qe, st) + _dot(p, v)
            state[h] = st * jnp.exp(a_last) + _dot_tn(v, kl)
            o_ref[:, sl] = o
            r = lax.rsqrt(jnp.mean(o * o, axis=-1, keepdims=True) + EPS)
            hg = g_ref[:, sl]
            bo_ref[:, sl] = (o * r * ghv * (hg * _sigmoid(hg))).astype(bo_ref.dtype)

    def col(cb):
        return pl.BlockSpec((C, 512), lambda b, c: (b * NC + c, cb))

    tile = pl.BlockSpec((C, W), lambda b, c: (b * NC + c, 0))
    out, carried = _pcall(
        body, (proj, proj, proj, proj, lbl, gh), name="hgrn_fwd", grid=(B, NC),
        in_specs=[col(COL_HQ), col(COL_HF), col(COL_HI), col(COL_HG),
                  pl.BlockSpec((2, W), lambda b, c: (0, 0)), pl.BlockSpec((1, HG_DIM), lambda b, c: (0, 0))],
        out_specs=(tile, tile, pl.BlockSpec((None, HG_HEADS, 128, 128), lambda b, c: (b * NC + c, 0, 0, 0))),
        out_shape=(jax.ShapeDtypeStruct((T, W), F32), jax.ShapeDtypeStruct((T, W), BF16),
                   jax.ShapeDtypeStruct((B * NC, HG_HEADS, 128, 128), F32)),
        scratch_shapes=[pltpu.VMEM((HG_HEADS, 128, 128), F32)],
        semantics=("parallel", "arbitrary"), riders=riders)
    return (out, carried) if riders else out


def _hgrn_bwd(proj, o_saved, states, d_out, lbl, gh, B, S, riders=()):
    T = B * S
    C = HG_CHUNK
    NC = S // C
    W = HG_HEADS * HG_DIM

    def body(q_ref, f_ref, i_ref, g_ref, o_ref, st_ref, do_ref, lbl_ref, gh_ref,
             dq_ref, df_ref, di_ref, dg_ref, dlbl_ref, dgh_ref, dstate, dlb_acc):
        b = pl.program_id(0)
        c = pl.program_id(1)

        @pl.when(c == 0)
        def _():
            dstate[...] = jnp.zeros_like(dstate)

        @pl.when((b == 0) & (c == 0))
        def _():
            dgh_ref[...] = jnp.zeros_like(dgh_ref)
            dlb_acc[...] = jnp.zeros_like(dlb_acc)

        lbl_v = lbl_ref[...]
        lb_all = _hgrn_lower_bound(lbl_v)
        ghv = gh_ref[...]
        row = lax.broadcasted_iota(jnp.int32, (C, C), 0)
        colm = lax.broadcasted_iota(jnp.int32, (C, C), 1)
        triu = colm >= row
        for h in range(HG_HEADS):
            sl = slice(h * 128, (h + 1) * 128)
            hq = q_ref[:, sl]
            lb = lb_all[:, sl]
            sg, fg, sq, tril, a, a_last, a_mid = _hgrn_gates(hq, f_ref[:, sl], lb)
            k = 1.0 - fg
            q = hq * sq
            v = i_ref[:, sl]
            ea = jnp.exp(a)
            ei = jnp.exp(a - a_mid)
            eki = jnp.exp(a_mid - a)
            ekl = jnp.exp(a_last - a)
            e_last = jnp.exp(a_last)
            qe = q * ea
            qi = q * ei
            ki = k * eki
            kl = k * ekl
            p = jnp.where(tril, _dot_nt(qi, ki), 0.0)
            st = st_ref[h]
            o = o_ref[:, sl]
            hg = g_ref[:, sl]
            sgg = _sigmoid(hg)
            r = lax.rsqrt(jnp.mean(o * o, axis=-1, keepdims=True) + EPS)
            n = o * r
            dbo = do_ref[:, sl]
            dg_ref[:, sl] = (dbo * n * ghv * (sgg * (1.0 + hg * (1.0 - sgg)))).astype(dg_ref.dtype)
            don = dbo * (hg * sgg)
            dgh_ref[...] += jnp.sum(don * n, axis=0, keepdims=True)
            dn = don * ghv
            d_o = r * (dn - n * jnp.mean(dn * n, axis=-1, keepdims=True))
            dst = dstate[h]
            dp = jnp.where(tril, _dot3(d_o, v, NT), 0.0)
            d_qe = _dot3(d_o, st, NN)
            d_qi = _dot3(dp, ki, NN)
            d_ki = _dot3(dp, qi, TN)
            d_kl = _dot3(v, dst, NN)
            dv = _dot_tn(p, d_o) + _dot_nt(kl, dst)
            dstate[h] = dst * e_last + _dot3(d_o, qe, TN)
            d_a_last = jnp.sum(dst * st, axis=0, keepdims=True) * e_last + jnp.sum(d_kl * kl, axis=0, keepdims=True)
            dq = d_qe * ea + d_qi * ei
            dk = d_ki * eki + d_kl * ekl
            da = d_qe * qe + d_qi * qi - d_ki * ki - d_kl * kl
            dlogf = _dot_01(triu, da) + d_a_last
            dfg = dlogf / fg - dk
            df_ref[:, sl] = (dfg * (1.0 - lb) * sg * (1.0 - sg)).astype(df_ref.dtype)
            dlb_acc[:, sl] += jnp.sum(dfg * (1.0 - sg), axis=0, keepdims=True)
            dq_ref[:, sl] = (dq * (sq * (1.0 + hq * (1.0 - sq)))).astype(dq_ref.dtype)
            di_ref[:, sl] = dv.astype(di_ref.dtype)

        @pl.when((b == B - 1) & (c == NC - 1))
        def _():
            dlb = dlb_acc[...]
            p0 = lb_all
            first = lax.broadcasted_iota(jnp.int32, (2, W), 0) == 0
            dlbl_ref[...] = jnp.where(first, dlb * p0 * (1.0 - p0), -dlb * p0 * (1.0 - p0))

    def col(cb):
        return pl.BlockSpec((C, 512), lambda b, c: (b * NC + NC - 1 - c, cb))

    tile = pl.BlockSpec((C, W), lambda b, c: (b * NC + NC - 1 - c, 0))
    out, carried = _pcall(
        body, (proj, proj, proj, proj, o_saved, states, d_out, lbl, gh), name="hgrn_bwd", grid=(B, NC),
        in_specs=[col(COL_HQ), col(COL_HF), col(COL_HI), col(COL_HG), tile,
                  pl.BlockSpec((None, HG_HEADS, 128, 128), lambda b, c: (b * NC + NC - 1 - c, 0, 0, 0)),
                  tile, pl.BlockSpec((2, W), lambda b, c: (0, 0)), pl.BlockSpec((1, HG_DIM), lambda b, c: (0, 0))],
        out_specs=(tile, tile, tile, tile,
                   pl.BlockSpec((2, W), lambda b, c: (0, 0)), pl.BlockSpec((1, HG_DIM), lambda b, c: (0, 0))),
        out_shape=(jax.ShapeDtypeStruct((T, W), BF16),) * 4
        + (jax.ShapeDtypeStruct((2, W), F32), jax.ShapeDtypeStruct((1, HG_DIM), F32)),
        scratch_shapes=[pltpu.VMEM((HG_HEADS, 128, 128), F32), pltpu.VMEM((1, W), F32)],
        semantics=("arbitrary", "arbitrary"), riders=riders)
    return (out, carried) if riders else out


_XA_SCALE = XA_DIM ** -0.5


def _attn_probs(qh, kh):
    s = _dot_nt(qh, kh) * _XA_SCALE
    e = jnp.exp(s - jnp.max(s, axis=-1, keepdims=True))
    return e / jnp.sum(e, axis=-1, keepdims=True)


def _attn_fwd(proj, kv, B, S):
    T = B * S
    tq = _row_tile(S)
    nq = S // tq
    W = XA_HEADS * XA_DIM

    def body(q_ref, kv_ref, o_ref):
        for h in range(XA_HEADS):
            sl = slice(h * 128, (h + 1) * 128)
            p = _attn_probs(q_ref[:, sl], kv_ref[:, sl])
            o_ref[:, sl] = _dot(p, kv_ref[:, W + h * 128:W + (h + 1) * 128]).astype(o_ref.dtype)

    return pl.pallas_call(
        body, name="attn_fwd", grid=(B, nq),
        in_specs=[pl.BlockSpec((tq, 512), lambda b, i: (b * nq + i, COL_XQ)),
                  pl.BlockSpec((MEM_LEN, 2 * W), lambda b, i: (b, 0))],
        out_specs=pl.BlockSpec((tq, W), lambda b, i: (b * nq + i, 0)),
        out_shape=jax.ShapeDtypeStruct((T, W), BF16), compiler_params=_cp("parallel", "parallel"),
    )(proj, kv)


def _attn_bwd(proj, kv, d_out, B, S):
    T = B * S
    tq = _row_tile(S)
    nq = S // tq
    W = XA_HEADS * XA_DIM

    def body(q_ref, kv_ref, do_ref, dq_ref, dkv_ref):
        @pl.when(pl.program_id(1) == 0)
        def _():
            dkv_ref[...] = jnp.zeros_like(dkv_ref)

        for h in range(XA_HEADS):
            sl = slice(h * 128, (h + 1) * 128)
            slv = slice(W + h * 128, W + (h + 1) * 128)
            qh = q_ref[:, sl]
            kh = kv_ref[:, sl]
            p = _attn_probs(qh, kh)
            dc = do_ref[:, sl]
            dp = _dot_nt(dc, kv_ref[:, slv])
            ds = p * (dp - jnp.sum(dp * p, axis=-1, keepdims=True)) * _XA_SCALE
            dq_ref[:, sl] = _dot(ds, kh).astype(dq_ref.dtype)
            dkv_ref[:, sl] += _dot_tn(ds, qh)
            dkv_ref[:, slv] += _dot_tn(p, dc)

    kvspec = pl.BlockSpec((MEM_LEN, 2 * W), lambda b, i: (b, 0))
    tile = pl.BlockSpec((tq, W), lambda b, i: (b * nq + i, 0))
    return pl.pallas_call(
        body, name="attn_bwd", grid=(B, nq),
        in_specs=[pl.BlockSpec((tq, 512), lambda b, i: (b * nq + i, COL_XQ)), kvspec, tile],
        out_specs=(tile, kvspec),
        out_shape=(jax.ShapeDtypeStruct((T, W), BF16), jax.ShapeDtypeStruct((B * MEM_LEN, 2 * W), F32)),
        compiler_params=_cp("parallel", "arbitrary"),
    )(proj, kv, d_out)


_MERGE_TN = 256


def _gate_specs(tm):
    base = COL_GATE0 // _MERGE_TN
    per = D_MODEL // _MERGE_TN
    return [pl.BlockSpec((tm, _MERGE_TN), functools.partial(lambda i, j, n: (i, base + per * n + j), n=n))
            for n in range(3)]


def _merge_fwd(a_out, b_out, c_out, wb, proj):
    T = a_out.shape[0]
    tm = _row_tile(T)
    tn = _MERGE_TN

    def body(a_ref, b_ref, c_ref, w_ref, g0_ref, g1_ref, g2_ref, m_ref, up_ref):
        acc = None
        for n, (br, gr) in enumerate(((a_ref, g0_ref), (b_ref, g1_ref), (c_ref, g2_ref))):
            up = _dot(br[...], w_ref[n * BR_WIDTH:(n + 1) * BR_WIDTH, :])
            up_ref[n] = up.astype(up_ref.dtype)
            term = _sigmoid(gr[...]) * up
            acc = term if acc is None else acc + term
        m_ref[...] = acc.astype(m_ref.dtype)

    br_spec = pl.BlockSpec((tm, BR_WIDTH), lambda i, j: (i, 0))
    return pl.pallas_call(
        body, name="merge_fwd", grid=(T // tm, D_MODEL // tn),
        in_specs=[br_spec, br_spec, br_spec,
                  pl.BlockSpec((None, 3 * BR_WIDTH, tn), lambda i, j: (j, 0, 0))] + _gate_specs(tm),
        out_specs=(pl.BlockSpec((tm, tn), lambda i, j: (i, j)), pl.BlockSpec((3, tm, tn), lambda i, j: (0, i, j))),
        out_shape=(jax.ShapeDtypeStruct((T, D_MODEL), BF16), jax.ShapeDtypeStruct((3, T, D_MODEL), BF16)),
        compiler_params=_cp("parallel", "parallel"),
    )(a_out, b_out, c_out, wb, proj, proj, proj)


def _merge_bwd(d_merged, ups, proj):
    T = d_merged.shape[0]
    tm = _row_tile(T)
    tn = _MERGE_TN

    def body(dm_ref, up_ref, g0_ref, g1_ref, g2_ref, dup_ref, dg0_ref, dg1_ref, dg2_ref):
        dm = dm_ref[...]
        for n, (gr, dgr) in enumerate(((g0_ref, dg0_ref), (g1_ref, dg1_ref), (g2_ref, dg2_ref))):
            gate = _sigmoid(gr[...])
            dup_ref[n] = (dm * gate).astype(dup_ref.dtype)
            dgr[...] = (dm * up_ref[n].astype(F32) * gate * (1.0 - gate)).astype(dgr.dtype)

    tile = pl.BlockSpec((tm, tn), lambda i, j: (i, j))
    tile3 = pl.BlockSpec((3, tm, tn), lambda i, j: (0, i, j))
    return pl.pallas_call(
        body, name="merge_bwd", grid=(T // tm, D_MODEL // tn),
        in_specs=[tile, tile3] + _gate_specs(tm),
        out_specs=(tile3, tile, tile, tile),
        out_shape=(jax.ShapeDtypeStruct((3, T, D_MODEL), BF16),) + (jax.ShapeDtypeStruct((T, D_MODEL), BF16),) * 3,
        compiler_params=_cp("parallel", "parallel"),
    )(d_merged, ups, proj, proj, proj)


_CONV_TF = 256
_HALO = 8


def _conv_fwd(ab, cw, cb, B, S):
    T = B * S
    ts = _row_tile(S)
    tf = _CONV_TF
    nb = D_FF // tf
    tps = S // ts
    hb = ts // _HALO

    def body(a_ref, p_ref, b_ref, w_ref, cb_ref, o_ref):
        start = (pl.program_id(0) % tps) == 0
        prev = jnp.where(start, 0.0, p_ref[...])
        ext = jnp.concatenate([prev, a_ref[...]], axis=0)
        a1 = pltpu.roll(ext, 1, 0)[_HALO:, :]
        a2 = pltpu.roll(ext, 2, 0)[_HALO:, :]
        ac = cb_ref[...] + w_ref[0] * a2 + w_ref[1] * a1 + w_ref[2] * a_ref[...]
        o_ref[...] = (ac * _sigmoid(ac) * b_ref[...]).astype(o_ref.dtype)

    return pl.pallas_call(
        body, name="conv_fwd", grid=(T // ts, nb),
        in_specs=[pl.BlockSpec((ts, tf), lambda i, j: (i, j)),
                  pl.BlockSpec((_HALO, tf), lambda i, j: (jnp.maximum(i * hb - 1, 0), j)),
                  pl.BlockSpec((ts, tf), lambda i, j: (i, j + nb)),
                  pl.BlockSpec((3, 1, tf), lambda i, j: (0, 0, j)),
                  pl.BlockSpec((1, tf), lambda i, j: (0, j))],
        out_specs=pl.BlockSpec((ts, tf), lambda i, j: (i, j)),
        out_shape=jax.ShapeDtypeStruct((T, D_FF), BF16), compiler_params=_cp("parallel", "parallel"),
    )(ab, ab, ab, cw, cb)


def _conv_bwd(ab, d_ff, cw, cb, B, S):
    T = B * S
    ts = _row_tile(S)
    tf = _CONV_TF
    nb = D_FF // tf
    tps = S // ts
    hb = ts // _HALO
    last_h = T // _HALO - 1
    n_ext = ts + _HALO

    def body(a_ref, ap_ref, an_ref, b_ref, bn_ref, d_ref, dn_ref, w_ref, cb_ref,
             da_ref, db_ref, dw_ref, dcb_ref):
        i = pl.program_id(1)

        @pl.when(i == 0)
        def _():
            dw_ref[...] = jnp.zeros_like(dw_ref)
            dcb_ref[...] = jnp.zeros_like(dcb_ref)

        start = (i % tps) == 0
        end = (i % tps) == tps - 1
        a = a_ref[...]
        ext = jnp.concatenate([jnp.where(start, 0.0, ap_ref[...]), a, an_ref[...]], axis=0)
        r1 = pltpu.roll(ext, 1, 0)[_HALO:, :]
        r2 = pltpu.roll(ext, 2, 0)[_HALO:, :]
        ac = cb_ref[...] + w_ref[0] * r2 + w_ref[1] * r1 + w_ref[2] * ext[_HALO:, :]
        sg = _sigmoid(ac)
        d_e = jnp.concatenate([d_ref[...], jnp.where(end, 0.0, dn_ref[...])], axis=0)
        b_e = jnp.concatenate([b_ref[...], bn_ref[...]], axis=0)
        db_ref[...] = (d_e[:ts, :] * (ac * sg)[:ts, :]).astype(db_ref.dtype)
        dac = d_e * b_e * sg * (1.0 + ac * (1.0 - sg))
        u1 = pltpu.roll(dac, n_ext - 1, 0)[:ts, :]
        u2 = pltpu.roll(dac, n_ext - 2, 0)[:ts, :]
        dac0 = dac[:ts, :]
        da_ref[...] = (w_ref[2] * dac0 + w_ref[1] * u1 + w_ref[0] * u2).astype(da_ref.dtype)
        dcb_ref[...] += jnp.sum(dac0, axis=0, keepdims=True)
        dw_ref[2] += jnp.sum(dac0 * a, axis=0, keepdims=True)
        dw_ref[1] += jnp.sum(dac0 * r1[:ts, :], axis=0, keepdims=True)
        dw_ref[0] += jnp.sum(dac0 * r2[:ts, :], axis=0, keepdims=True)

    def cur(off):
        return pl.BlockSpec((ts, tf), lambda j, i: (i, j + off))

    def nxt(off):
        return pl.BlockSpec((_HALO, tf), lambda j, i: (jnp.minimum((i + 1) * hb, last_h), j + off))

    return pl.pallas_call(
        body, name="conv_bwd", grid=(nb, T // ts),
        in_specs=[cur(0), pl.BlockSpec((_HALO, tf), lambda j, i: (jnp.maximum(i * hb - 1, 0), j)), nxt(0),
                  cur(nb), nxt(nb), cur(0), nxt(0),
                  pl.BlockSpec((3, 1, tf), lambda j, i: (0, 0, j)), pl.BlockSpec((1, tf), lambda j, i: (0, j))],
        out_specs=(cur(0), cur(0), pl.BlockSpec((3, 1, tf), lambda j, i: (0, 0, j)),
                   pl.BlockSpec((1, tf), lambda j, i: (0, j))),
        out_shape=(jax.ShapeDtypeStruct((T, D_FF), BF16), jax.ShapeDtypeStruct((T, D_FF), BF16),
                   jax.ShapeDtypeStruct((3, 1, D_FF), F32), jax.ShapeDtypeStruct((1, D_FF), F32)),
        compiler_params=_cp("parallel", "arbitrary"),
    )(ab, ab, ab, ab, ab, d_ff, d_ff, cw, cb)


def _local_step(x, mem, tgt, p, comm, B, S):
    g = {}
    h = _rms_fwd("norm1", x, p["norm1_g"])
    proj = comm.carry("in_proj", lambda r: _mm_cs("in_proj", h, comm.w("w_in"), F32, riders=r))
    a_out = _gmlp_fwd(proj, p["ln_v_g"], p["ln_v_b"], p["w_spatial"], p["b_spatial"])
    o_h, b_out, states = comm.carry(
        "hgrn_fwd", lambda r: _hgrn_fwd(proj, p["lb_logits"], p["hgrn_norm_g"], B, S, riders=r))
    memn = _rms_fwd("mem_norm", mem, p["mem_norm_g"])
    kv = _mm_rs("mem_kv", memn, comm.w("w_mem_kv"), F32)
    c_out = _attn_fwd(proj, kv, B, S)
    merged, ups = _merge_fwd(a_out, b_out, c_out, comm.w("w_branch"), proj)
    x1 = _mm_rs("out_proj", merged, comm.w("w_out"), F32, res=x)
    h2 = _rms_fwd("norm2", x1, p["norm2_g"])
    ab = _mm_cs("up_proj", h2, comm.w("w_up"), F32)
    conv_w = comm.w("conv_w")
    ff = _conv_fwd(ab, conv_w, p["conv_b"], B, S)
    x2 = _mm_rs("down_proj", ff, comm.w("w_down"), F32, res=x1)
    dx2, g["final_g"], loss = _loss_head(x2, tgt, p["final_g"])

    comm.grad("w_down", _mm_tn_rs("g_w_down", ff, dx2, to=D_FF // 2))
    d_ff = _mm_nt_rs("d_ff", dx2, comm.w("w_down"), F32, to=D_FF // 2)
    d_a, d_b, g["conv_w"], g["conv_b"] = _conv_bwd(ab, d_ff, conv_w, p["conv_b"], B, S)
    d_ab = jnp.concatenate([d_a, d_b], axis=1)
    comm.grad("w_up", _mm_tn_cs("g_w_up", h2, d_ab, N_CHIPS, to=512))
    d_h2 = comm.carry("d_h2", lambda r: _mm_nt_cs("d_h2", d_ab, comm.w("w_up"), F32, riders=r))
    d_x1, g["norm2_g"] = _rms_bwd("norm2_bwd", x1, p["norm2_g"], d_h2, dx2)
    comm.grad("w_out", _mm_tn_rs("g_w_out", merged, d_x1, to=512))
    d_merged = _mm_nt_rs("d_merged", d_x1, comm.w("w_out"), F32, to=512)
    d_ups, d_g0, d_g1, d_g2 = _merge_bwd(d_merged, ups, proj)

    T = x.shape[0]
    tm = _row_tile(T)
    d_br, g_wb = [], []
    for n, br in enumerate((a_out, b_out, c_out)):
        d_br.append(_matmul(
            "d_branch%d" % n, (d_ups, comm.w("w_branch")), grid=(T // tm, 1, N_CHIPS),
            in_specs=[pl.BlockSpec((None, tm, _MERGE_TN), functools.partial(lambda i, j, k, n: (n, i, k), n=n)),
                      pl.BlockSpec((None, BR_WIDTH, _MERGE_TN), functools.partial(lambda i, j, k, n: (k, n, 0), n=n))],
            o_spec=pl.BlockSpec((tm, BR_WIDTH), lambda i, j, k: (i, 0)),
            out_shape=(T, BR_WIDTH), out_dtype=F32, acc_shape=(tm, BR_WIDTH), dims=NT))
        g_wb.append(_matmul(
            "g_w_branch%d" % n, (br, d_ups), grid=(1, N_CHIPS, T // tm),
            in_specs=[pl.BlockSpec((tm, BR_WIDTH), lambda i, j, k: (k, 0)),
                      pl.BlockSpec((None, tm, _MERGE_TN), functools.partial(lambda i, j, k, n: (n, k, j), n=n))],
            o_spec=pl.BlockSpec((None, BR_WIDTH, _MERGE_TN), lambda i, j, k: (j, 0, 0)),
            out_shape=(N_CHIPS, BR_WIDTH, _MERGE_TN), out_dtype=F32, acc_shape=(BR_WIDTH, _MERGE_TN), dims=TN))
    comm.grad("w_branch", jnp.concatenate(g_wb, axis=1))

    d_zu, d_zv, g["w_spatial"], g["b_spatial"], g["ln_v_g"], g["ln_v_b"] = _gmlp_bwd(
        proj, d_br[0], p["ln_v_g"], p["ln_v_b"], p["w_spatial"], p["b_spatial"])
    d_xq, d_kv = _attn_bwd(proj, kv, d_br[2], B, S)
    comm.grad("w_mem_kv", _mm_tn_rs("g_w_mem_kv", memn, d_kv, to=512))
    d_memn = _mm_nt_rs("d_memn", d_kv, comm.w("w_mem_kv"), F32, to=512)
    _, g["mem_norm_g"] = _rms_bwd("mem_norm_bwd", mem, p["mem_norm_g"], d_memn, None)
    d_hq, d_hf, d_hi, d_hg, g["lb_logits"], g["hgrn_norm_g"] = comm.carry(
        "hgrn_bwd", lambda r: _hgrn_bwd(proj, o_h, states, d_br[1], p["lb_logits"], p["hgrn_norm_g"], B, S,
                                        riders=r))
    d_proj = jnp.concatenate([d_zu, d_zv, d_hq, d_hf, d_hi, d_hg, d_xq, d_g0, d_g1, d_g2], axis=1)
    comm.grad("w_in", comm.carry("g_w_in", lambda r: _mm_tn_cs("g_w_in", h, d_proj, N_CHIPS, to=512, riders=r)))
    d_h = comm.carry("d_h", lambda r: _mm_nt_cs("d_h", d_proj, comm.w("w_in"), F32, riders=r))
    grad_x, g["norm1_g"] = comm.carry(
        "norm1_bwd", lambda r: _rms_bwd("norm1_bwd", x, p["norm1_g"], d_h, d_x1, riders=r))
    return loss[0, 0], grad_x, g


ANY = pl.BlockSpec(memory_space=pl.ANY)


def _place():
    x, y, c = lax.axis_index("x"), lax.axis_index("y"), lax.axis_index("c")
    other_chips = [(1 - x, y), (x, 1 - y), (1 - x, 1 - y)]
    return x, y, c, other_chips


def _remote(src, dst, send_sem, recv_sem, dev):
    return pltpu.make_async_remote_copy(src_ref=src, dst_ref=dst, send_sem=send_sem, recv_sem=recv_sem,
                                        device_id=dev, device_id_type=MESH_ID)


class _Exchange:
    def __init__(self, operands, out_shape, aliases, scratch, start, finish):
        self.operands, self.out_shape, self.aliases, self.scratch = operands, out_shape, aliases, scratch
        self.start, self.finish = start, finish


def _run_exchanges(name, exs):
    n_in = [len(ex.operands) for ex in exs]
    n_out = [len(ex.out_shape) for ex in exs]
    n_scr = [len(ex.scratch) for ex in exs]

    def body(*refs):
        ins, outs, scr = refs[:sum(n_in)], refs[sum(n_in):sum(n_in) + sum(n_out)], refs[sum(n_in) + sum(n_out):]
        parts, oi, oo, os_ = [], 0, 0, 0
        for k in range(len(exs)):
            parts.append((ins[oi:oi + n_in[k]], outs[oo:oo + n_out[k]], scr[os_:os_ + n_scr[k]]))
            oi, oo, os_ = oi + n_in[k], oo + n_out[k], os_ + n_scr[k]
        for ex, part in zip(exs, parts):
            ex.start(*part)
        for ex, part in zip(exs, parts):
            ex.finish(*part)

    aliases, ops, shapes, scratch, oi, oo = {}, [], [], [], 0, 0
    for k, ex in enumerate(exs):
        aliases.update({oi + a: oo + b for a, b in ex.aliases.items()})
        oi, oo = oi + n_in[k], oo + n_out[k]
        ops += list(ex.operands)
        shapes += list(ex.out_shape)
        scratch += list(ex.scratch)
    res = pl.pallas_call(
        body, name=name, in_specs=[ANY] * len(ops), out_specs=(ANY,) * len(shapes), out_shape=tuple(shapes),
        input_output_aliases=aliases, scratch_shapes=scratch,
    )(*ops)
    out, oo = [], 0
    for k in range(len(exs)):
        out.append(list(res[oo:oo + n_out[k]]))
        oo += n_out[k]
    return out


def _ex_all_gather(slabs, halved):
    n = len(slabs)

    def rows(a, cc):
        if not halved[a]:
            return slice(None)
        hr = slabs[a].shape[1] // 2
        return pl.ds(cc * hr, hr)

    def ici(bufs, scr, a, j, chip, c, mine):
        px, py = chip
        x, y, _, _ = _place()
        qs = 2 * x + y if mine else 2 * px + py
        piece = bufs[a].at[qs, rows(a, c)]
        return _remote(piece, piece, scr[0].at[3 * a + j], scr[1].at[3 * a + j], (px, py, c))

    def d2d(bufs, scr, a, j, chip, cc):
        px, py = chip
        x, y, c, _ = _place()
        piece = bufs[a].at[2 * px + py, rows(a, cc)]
        return _remote(piece, piece, scr[2].at[3 * a + j], scr[3].at[3 * a + j], (x, y, 1 - c))

    def start(ins, outs, scr):
        _, _, c, chips = _place()
        for j, chip in enumerate(chips):
            for a in range(n):
                ici(outs, scr, a, j, chip, c, True).start()

    def finish(ins, outs, scr):
        _, _, c, chips = _place()
        for j, chip in enumerate(chips):
            for a in range(n):
                ici(outs, scr, a, j, chip, c, False).wait_recv()
                if halved[a]:
                    d2d(outs, scr, a, j, chip, c).start()
        for j, chip in enumerate(chips):
            for a in range(n):
                if halved[a]:
                    d2d(outs, scr, a, j, chip, 1 - c).wait_recv()
        for j, chip in enumerate(chips):
            for a in range(n):
                ici(outs, scr, a, j, chip, c, True).wait_send()
                if halved[a]:
                    d2d(outs, scr, a, j, chip, c).wait_send()

    return _Exchange(list(slabs), [jax.ShapeDtypeStruct(s.shape, s.dtype) for s in slabs],
                     {a: a for a in range(n)}, [pltpu.SemaphoreType.DMA((3 * n,))] * 4, start, finish)


def _ex_to_sibling(grads):
    n = len(grads)

    def copy(ins, outs, scr, a):
        x, y, c, _ = _place()
        hr = grads[a].shape[1] // 2
        return _remote(ins[a].at[:, pl.ds((1 - c) * hr, hr), :], outs[a], scr[0].at[a], scr[1].at[a], (x, y, 1 - c))

    def start(ins, outs, scr):
        for a in range(n):
            copy(ins, outs, scr, a).start()

    def finish(ins, outs, scr):
        for a in range(n):
            copy(ins, outs, scr, a).wait()

    out_shape = [jax.ShapeDtypeStruct((g.shape[0], g.shape[1] // 2, g.shape[2]), g.dtype) for g in grads]
    return _Exchange(list(grads), out_shape, {}, [pltpu.SemaphoreType.DMA((n,))] * 2, start, finish)


def _ex_to_owner(parts):
    n = len(parts)

    def copy(ins, outs, scr, a, j, chip):
        _, _, c, _ = _place()
        px, py = chip
        return _remote(ins[a].at[2 * px + py], outs[a].at[j], scr[0].at[3 * a + j], scr[1].at[3 * a + j],
                       (px, py, c))

    def start(ins, outs, scr):
        for j, chip in enumerate(_place()[3]):
            for a in range(n):
                copy(ins, outs, scr, a, j, chip).start()

    def finish(ins, outs, scr):
        for j, chip in enumerate(_place()[3]):
            for a in range(n):
                copy(ins, outs, scr, a, j, chip).wait()

    out_shape = [jax.ShapeDtypeStruct((3,) + p.shape[1:], p.dtype) for p in parts]
    return _Exchange(list(parts), out_shape, {}, [pltpu.SemaphoreType.DMA((3 * n,))] * 2, start, finish)


def _ex_share_halves(bufs):
    n = len(bufs)

    def copy(outs, scr, a, cc):
        x, y, c, _ = _place()
        hr = bufs[a].shape[0] // 2
        piece = outs[a].at[pl.ds(cc * hr, hr), :]
        return _remote(piece, piece, scr[0].at[a], scr[1].at[a], (x, y, 1 - c))

    def start(ins, outs, scr):
        c = _place()[2]
        for a in range(n):
            copy(outs, scr, a, c).start()

    def finish(ins, outs, scr):
        c = _place()[2]
        for a in range(n):
            copy(outs, scr, a, c).wait_send()
            copy(outs, scr, a, 1 - c).wait_recv()

    return _Exchange(list(bufs), [jax.ShapeDtypeStruct(b.shape, b.dtype) for b in bufs], {a: a for a in range(n)},
                     [pltpu.SemaphoreType.DMA((n,))] * 2, start, finish)


def _ex_gather_small(arrs):
    n = len(arrs)

    def peer_of(m):
        x, y, c, _ = _place()
        return (1 - x if m & 4 else x, 1 - y if m & 2 else y, 1 - c if m & 1 else c)

    def start(ins, outs, scr):
        x, y, c, _ = _place()
        for m in range(1, N_DEV):
            for a in range(n):
                k = (N_DEV - 1) * a + m - 1
                _remote(ins[a], outs[a].at[4 * x + 2 * y + c], scr[0].at[k], scr[1].at[k], peer_of(m)).start()

    def finish(ins, outs, scr):
        for m in range(1, N_DEV):
            px, py, pc = peer_of(m)
            for a in range(n):
                k = (N_DEV - 1) * a + m - 1
                slot = outs[a].at[4 * px + 2 * py + pc]
                cp = _remote(ins[a], slot, scr[0].at[k], scr[1].at[k], (px, py, pc))
                cp.wait_send()
                cp.wait_recv()

    slots = [jnp.zeros((N_DEV,) + a.shape, a.dtype) for a in arrs]
    out_shape = [jax.ShapeDtypeStruct(s.shape, s.dtype) for s in slots]
    return _Exchange(list(arrs) + slots, out_shape, {n + a: a for a in range(n)},
                     [pltpu.SemaphoreType.DMA(((N_DEV - 1) * n,))] * 2, start, finish)


def _div_tile(n, want):
    best = None
    for t in range(8, min(n, want) + 1, 8):
        if n % t == 0:
            best = t
    assert best is not None, n
    return best


def _cast_into_slab(name, w, place, dtype):
    r, cc = w.shape
    tr = r if r * cc <= 128 * 1024 else _div_tile(r, 256)

    def body(s_ref, w_ref, o_ref):
        o_ref[...] = w_ref[...].astype(o_ref.dtype)

    return pl.pallas_call(
        body, name=name,
        grid_spec=pltpu.PrefetchScalarGridSpec(
            num_scalar_prefetch=1, grid=(r // tr,),
            in_specs=[pl.BlockSpec((tr, cc), lambda i, s: (i, 0))],
            out_specs=pl.BlockSpec((None, tr, cc), lambda i, s: (s[0], i, 0))),
        out_shape=jax.ShapeDtypeStruct((N_CHIPS, r, cc), dtype), compiler_params=_cp("parallel"),
    )(place, w)


def _add_half(name, g, rcv, place):
    nq, r, cc = g.shape
    hr = r // 2

    def body(s_ref, g_ref, r_ref, o_ref):
        o_ref[...] = (g_ref[...] + r_ref[...]).astype(o_ref.dtype)

    spec = pl.BlockSpec((None, hr, cc), lambda i, s: (i, 0, 0))
    return pl.pallas_call(
        body, name=name,
        grid_spec=pltpu.PrefetchScalarGridSpec(
            num_scalar_prefetch=1, grid=(nq,),
            in_specs=[pl.BlockSpec((None, hr, cc), lambda i, s: (i, s[1], 0)), spec], out_specs=spec),
        out_shape=jax.ShapeDtypeStruct((nq, hr, cc), BF16), compiler_params=_cp("parallel"),
    )(place, g, rcv)


def _sum_owner(name, part, rcv, place):
    _, hr, cc = part.shape
    tr = _div_tile(hr, 128)
    nb = hr // tr

    def body(s_ref, p_ref, r_ref, o_ref):
        o_ref[...] = ((p_ref[...].astype(F32) + r_ref[0].astype(F32)) + r_ref[1].astype(F32)) + r_ref[2].astype(F32)

    return pl.pallas_call(
        body, name=name,
        grid_spec=pltpu.PrefetchScalarGridSpec(
            num_scalar_prefetch=1, grid=(nb,),
            in_specs=[pl.BlockSpec((None, tr, cc), lambda i, s: (s[0], i, 0)),
                      pl.BlockSpec((3, tr, cc), lambda i, s: (0, i, 0))],
            out_specs=pl.BlockSpec((tr, cc), lambda i, s: (s[1] * nb + i, 0))),
        out_shape=jax.ShapeDtypeStruct((2 * hr, cc), F32), compiler_params=_cp("parallel"),
    )(place, part, rcv)


def _sum_small(gathered, local, place):
    n = len(gathered)

    def body(s_ref, *refs):
        g_refs, l_refs, o_refs = refs[:n], refs[n:2 * n], refs[2 * n:]
        me = s_ref[2]
        for g_ref, l_ref, o_ref in zip(g_refs, l_refs, o_refs):
            acc = None
            for d in range(N_DEV):
                term = jnp.where(me == d, l_ref[...], g_ref[d])
                acc = term if acc is None else acc + term
            o_ref[...] = acc

    def whole(shape):
        return pl.BlockSpec(shape, lambda i, s, nd=len(shape): (0,) * nd)

    return pl.pallas_call(
        body, name="sum_small",
        grid_spec=pltpu.PrefetchScalarGridSpec(
            num_scalar_prefetch=1, grid=(1,),
            in_specs=[whole(g.shape) for g in gathered] + [whole(a.shape) for a in local],
            out_specs=tuple(whole(a.shape) for a in local)),
        out_shape=tuple(jax.ShapeDtypeStruct(a.shape, a.dtype) for a in local), compiler_params=_cp("arbitrary"),
    )(place, *gathered, *local)


def _adamw(name, w, g, m, v):
    r, cc = w.shape
    tr = r if r * cc <= 128 * 1024 else _div_tile(r, 256)

    def body(w_ref, g_ref, m_ref, v_ref, d_ref, mo_ref, vo_ref):
        gv = g_ref[...]
        mn = ADAM_B1 * m_ref[...] + (1.0 - ADAM_B1) * gv
        vn = ADAM_B2 * v_ref[...] + (1.0 - ADAM_B2) * (gv * gv)
        m_hat = mn / (1.0 - ADAM_B1 ** ADAM_STEP)
        v_hat = vn / (1.0 - ADAM_B2 ** ADAM_STEP)
        d_ref[...] = -ADAM_LR * (m_hat / (jnp.sqrt(v_hat) + ADAM_EPS) + ADAM_WD * w_ref[...])
        mo_ref[...] = mn
        vo_ref[...] = vn

    spec = pl.BlockSpec((tr, cc), lambda i: (i, 0))
    sd = jax.ShapeDtypeStruct((r, cc), F32)
    return pl.pallas_call(
        body, name=name, grid=(r // tr,), in_specs=[spec] * 4, out_specs=(spec,) * 3, out_shape=(sd,) * 3,
        compiler_params=_cp("parallel"),
    )(w, g, m, v)


_BIG = ("w_in", "w_up", "w_branch", "w_mem_kv", "w_out", "w_down")
_BIG_SHARD_SHAPE = {"w_in": (1024, 1664), "w_up": (1024, 1408), "w_branch": (1536, 256),
                    "w_mem_kv": (256, 1024), "w_out": (256, 1024), "w_down": (704, 1024)}
_MISC = (("loss", 1), ("norm1_g", 1024), ("ln_v_g", 512), ("ln_v_b", 512), ("b_spatial", 512), ("lb_logits", 1024),
         ("hgrn_norm_g", 128), ("mem_norm_g", 1024), ("norm2_g", 1024), ("conv_b", D_FF), ("final_g", 1024))
_PARAM_ORDER = ("norm1_g", "w_in", "ln_v_g", "ln_v_b", "w_spatial", "b_spatial", "lb_logits", "hgrn_norm_g",
                "mem_norm_g", "w_mem_kv", "w_branch", "w_out", "norm2_g", "w_up", "conv_w", "conv_b", "w_down",
                "final_g")


def _misc_rows(cnt):
    return -(-cnt // 1024) * 8


def _pack_misc(parts):
    pieces = []
    for name, cnt in _MISC:
        rows = _misc_rows(cnt)
        if name in parts:
            flat = jnp.reshape(parts[name], (-1,)).astype(F32)
            pieces.append(jnp.pad(flat, (0, rows * 128 - cnt)).reshape(rows, 128))
        else:
            pieces.append(jnp.zeros((rows, 128), F32))
    return jnp.concatenate(pieces, axis=0)


def _unpack_misc(packed):
    out, off = {}, 0
    for name, cnt in _MISC:
        rows = _misc_rows(cnt)
        out[name] = packed[off:off + rows].reshape(-1)[:cnt]
        off += rows
    return out


class _Comm:
    _ROW_SHARDED = ("w_mem_kv", "w_out", "w_down")

    def __init__(self, slabs, place):
        self.slabs, self.place = slabs, place
        self.full, self.raw, self.parts, self.bufs, self.done = {}, {}, {}, {}, {}
        ex, deliver = self._gather(["w_in"])
        deliver(_run_exchanges("all_gather_w_in", [ex])[0])

    def w(self, name):
        a = self.full[name]
        if name in self._ROW_SHARDED:
            return a.reshape(-1, a.shape[-1])
        if name == "conv_w":
            return jnp.transpose(a, (1, 0, 2)).reshape(3, 1, D_FF)
        return a

    def grad(self, name, arr):
        self.raw[name] = arr.reshape((N_CHIPS,) + _BIG_SHARD_SHAPE[name])

    def carry(self, tag, call):
        plan = self._plan(tag)
        if not plan:
            return call(())
        out, carried = call([ex for ex, _ in plan])
        for (_, deliver), res in zip(plan, carried):
            deliver(res)
        return out

    def finish(self, small_exchange):
        ex, deliver = self._share(["w_out", "w_branch", "w_mem_kv", "w_in"])
        shared, small = _run_exchanges("share_and_gather_small", [ex, small_exchange])
        deliver(shared)
        return self.done, small

    def _plan(self, tag):
        if tag == "in_proj":
            return [self._gather(["w_branch", "w_out", "w_mem_kv", "w_down", "conv_w"])]
        if tag == "hgrn_fwd":
            return [self._gather(["w_up"])]
        if tag == "d_h2":
            return [self._to_sibling(["w_down", "w_up"])]
        if tag == "hgrn_bwd":
            return [self._to_owner(["w_down", "w_up"]), self._to_sibling(["w_out", "w_branch", "w_mem_kv"])]
        if tag == "g_w_in":
            return [self._to_owner(["w_out", "w_branch", "w_mem_kv"]), self._share(["w_down", "w_up"])]
        if tag == "d_h":
            return [self._to_sibling(["w_in"])]
        if tag == "norm1_bwd":
            return [self._to_owner(["w_in"])]
        return []

    def _gather(self, names):
        ex = _ex_all_gather([self.slabs[n] for n in names], [n != "conv_w" for n in names])
        return ex, lambda res: self.full.update(zip(names, res))

    def _to_sibling(self, names):
        def deliver(res):
            for n, r in zip(names, res):
                self.parts[n] = _add_half("rs_add_" + n, self.raw[n], r, self.place)

        return _ex_to_sibling([self.raw[n] for n in names]), deliver

    def _to_owner(self, names):
        def deliver(res):
            for n, r in zip(names, res):
                self.bufs[n] = _sum_owner("rs_sum_" + n, self.parts[n], r, self.place)

        return _ex_to_owner([self.parts[n] for n in names]), deliver

    def _share(self, names):
        return _ex_share_halves([self.bufs[n] for n in names]), lambda res: self.done.update(zip(names, res))


def kernel(x, mem, norm1_g, w_in, ln_v_g, ln_v_b, w_spatial, b_spatial, lb_logits, hgrn_norm_g, mem_norm_g, w_mem_kv, w_branch, w_out, norm2_g, w_up, conv_w, conv_b, w_down, final_g, loss_target, m_norm1_g, m_w_in, m_ln_v_g, m_ln_v_b, m_w_spatial, m_b_spatial, m_lb_logits, m_hgrn_norm_g, m_mem_norm_g, m_w_mem_kv, m_w_branch, m_w_out, m_norm2_g, m_w_up, m_conv_w, m_conv_b, m_w_down, m_final_g, v_norm1_g, v_w_in, v_ln_v_g, v_ln_v_b, v_w_spatial, v_b_spatial, v_lb_logits, v_hgrn_norm_g, v_mem_norm_g, v_w_mem_kv, v_w_branch, v_w_out, v_norm2_g, v_w_up, v_conv_w, v_conv_b, v_w_down, v_final_g):
    w = dict(norm1_g=norm1_g, w_in=w_in, ln_v_g=ln_v_g, ln_v_b=ln_v_b, w_spatial=w_spatial, b_spatial=b_spatial,
             lb_logits=lb_logits, hgrn_norm_g=hgrn_norm_g, mem_norm_g=mem_norm_g, w_mem_kv=w_mem_kv,
             w_branch=w_branch, w_out=w_out, norm2_g=norm2_g, w_up=w_up, conv_w=conv_w, conv_b=conv_b,
             w_down=w_down, final_g=final_g)
    mom = dict(norm1_g=m_norm1_g, w_in=m_w_in, ln_v_g=m_ln_v_g, ln_v_b=m_ln_v_b, w_spatial=m_w_spatial,
               b_spatial=m_b_spatial, lb_logits=m_lb_logits, hgrn_norm_g=m_hgrn_norm_g, mem_norm_g=m_mem_norm_g,
               w_mem_kv=m_w_mem_kv, w_branch=m_w_branch, w_out=m_w_out, norm2_g=m_norm2_g, w_up=m_w_up,
               conv_w=m_conv_w, conv_b=m_conv_b, w_down=m_w_down, final_g=m_final_g)
    var = dict(norm1_g=v_norm1_g, w_in=v_w_in, ln_v_g=v_ln_v_g, ln_v_b=v_ln_v_b, w_spatial=v_w_spatial,
               b_spatial=v_b_spatial, lb_logits=v_lb_logits, hgrn_norm_g=v_hgrn_norm_g, mem_norm_g=v_mem_norm_g,
               w_mem_kv=v_w_mem_kv, w_branch=v_w_branch, w_out=v_w_out, norm2_g=v_norm2_g, w_up=v_w_up,
               conv_w=v_conv_w, conv_b=v_conv_b, w_down=v_w_down, final_g=v_final_g)
    B, S, D = x.shape
    T = B * S
    ci = lax.axis_index("c")
    q = 2 * lax.axis_index("x") + lax.axis_index("y")
    place = jnp.stack([q, ci, 2 * q + ci]).astype(jnp.int32)

    slabs = {n: _cast_into_slab("slab_" + n, w[n].reshape(_BIG_SHARD_SHAPE[n]), place, BF16) for n in _BIG}
    slabs["conv_w"] = _cast_into_slab("slab_conv_w", conv_w[0], place, F32)
    comm = _Comm(slabs, place)
    p = dict(
        norm1_g=norm1_g, ln_v_g=ln_v_g, ln_v_b=ln_v_b, w_spatial=w_spatial[0],
        b_spatial=b_spatial.reshape(GM_GROUPS, GM_CHUNK, 1), lb_logits=lb_logits, hgrn_norm_g=hgrn_norm_g,
        mem_norm_g=mem_norm_g, norm2_g=norm2_g, conv_b=conv_b, final_g=final_g.reshape(1, D))

    loss, grad_x, g = _local_step(x.reshape(T, D), mem.reshape(B * MEM_LEN, D), loss_target.reshape(T, D), p, comm,
                                  B, S)

    misc = dict(g)
    misc["loss"] = loss
    local_small = [g["w_spatial"].reshape(GM_GROUPS * GM_CHUNK, GM_CHUNK), g["conv_w"].reshape(3, D_FF),
                   _pack_misc(misc)]
    shard_grads, everyone = comm.finish(_ex_gather_small(local_small))
    tot_ws, tot_cw, tot_misc = _sum_small(everyone, local_small, place)
    total = _unpack_misc(tot_misc)

    grads, delta, new_m, new_v = {}, {}, {}, {}
    for n in _BIG:
        shp = _BIG_SHARD_SHAPE[n]
        grads[n] = shard_grads[n]
        delta[n], new_m[n], new_v[n] = _adamw("adamw_" + n, w[n].reshape(shp), shard_grads[n],
                                              mom[n].reshape(shp), var[n].reshape(shp))
    misc_names = [n for n, _ in _MISC if n != "loss"]
    pk = _adamw("adamw_misc", _pack_misc({n: w[n] for n in misc_names}), tot_misc,
                _pack_misc({n: mom[n] for n in misc_names}), _pack_misc({n: var[n] for n in misc_names}))
    pk = [_unpack_misc(a) for a in pk]
    for n in misc_names:
        grads[n] = total[n]
        delta[n], new_m[n], new_v[n] = pk[0][n], pk[1][n], pk[2][n]
    ws_shape = (GM_GROUPS * GM_CHUNK, GM_CHUNK)
    grads["w_spatial"] = tot_ws
    delta["w_spatial"], new_m["w_spatial"], new_v["w_spatial"] = _adamw(
        "adamw_w_spatial", w_spatial.reshape(ws_shape), tot_ws, m_w_spatial.reshape(ws_shape),
        v_w_spatial.reshape(ws_shape))
    cw_shard = D_FF // N_CHIPS
    grads["conv_w"] = lax.dynamic_slice(tot_cw, (0, q * cw_shard), (3, cw_shard))
    delta["conv_w"], new_m["conv_w"], new_v["conv_w"] = _adamw(
        "adamw_conv_w", conv_w[0], grads["conv_w"], m_conv_w[0], v_conv_w[0])

    def shaped(d):
        return [d[n].reshape(w[n].shape) for n in _PARAM_ORDER]

    return (total["loss"].reshape(()), grad_x.reshape(B, S, D), *shaped(grads), *shaped(delta), *shaped(new_m),
            *shaped(new_v))
```

```python
import functools
import math

import jax
import jax.numpy as jnp
from jax import lax
from jax.experimental import pallas as pl
from jax.experimental.pallas import tpu as pltpu

F32 = jnp.float32
BF16 = jnp.bfloat16
EPS = 1e-6

D_MODEL = 1024
MEM_LEN = 256
GM_WIDTH = 512
GM_CHUNK = 128
GM_GROUPS = 4
HG_HEADS = 4
HG_DIM = 128
HG_CHUNK = 64
XA_HEADS = 4
XA_DIM = 128
BR_WIDTH = 512
D_FF = 2816
IN_WIDTH = 6656
N_CHIPS = 4
N_DEV = 8

ADAM_LR = 0.001
ADAM_B1 = 0.9
ADAM_B2 = 0.999
ADAM_EPS = 1e-08
ADAM_WD = 0.01
ADAM_STEP = 10

COL_ZU, COL_ZV, COL_HQ, COL_HF, COL_HI, COL_HG, COL_XQ = 0, 1, 2, 3, 4, 5, 6
COL_GATE0 = 3584

VMEM_LIMIT_BYTES = 48 * 1024 * 1024
MESH_ID = pl.DeviceIdType.MESH


def _cp(*sem):
    return pltpu.CompilerParams(dimension_semantics=sem, vmem_limit_bytes=VMEM_LIMIT_BYTES)


def _dot(a, b):
    return lax.dot_general(a.astype(BF16), b.astype(BF16), (((1,), (0,)), ((), ())), preferred_element_type=F32)


def _dot_nt(a, b):
    return lax.dot_general(a.astype(BF16), b.astype(BF16), (((1,), (1,)), ((), ())), preferred_element_type=F32)


def _dot_tn(a, b):
    return lax.dot_general(a.astype(BF16), b.astype(BF16), (((0,), (0,)), ((), ())), preferred_element_type=F32)


def _split2(x):
    hi = x.astype(BF16)
    return hi, (x - hi.astype(F32)).astype(BF16)


def _dot3(a, b, dims):
    ah, al = _split2(a)
    bh, bl = _split2(b)
    dn = (dims, ((), ()))
    return (lax.dot_general(ah, bh, dn, preferred_element_type=F32)
            + lax.dot_general(ah, bl, dn, preferred_element_type=F32)
            + lax.dot_general(al, bh, dn, preferred_element_type=F32))


def _dot_01(mask01, x):
    hi = x.astype(BF16)
    r1 = x - hi.astype(F32)
    mid = r1.astype(BF16)
    lo = (r1 - mid.astype(F32)).astype(BF16)
    m = mask01.astype(BF16)
    dn = (((1,), (0,)), ((), ()))
    return (lax.dot_general(m, hi, dn, preferred_element_type=F32)
            + lax.dot_general(m, mid, dn, preferred_element_type=F32)
            + lax.dot_general(m, lo, dn, preferred_element_type=F32))


def _sigmoid(z):
    return 1.0 / (1.0 + jnp.exp(-z))


_GELU_C = math.sqrt(2.0 / math.pi)


def _gelu_and_grad(z):
    inner = _GELU_C * (z + 0.044715 * z * z * z)
    t = jnp.tanh(inner)
    val = 0.5 * z * (1.0 + t)
    grad = 0.5 * (1.0 + t) + 0.5 * z * (1.0 - t * t) * _GELU_C * (1.0 + 3.0 * 0.044715 * z * z)
    return val, grad


def _row_tile(n, want=512):
    t = min(want, n)
    assert n % t == 0
    return t


def _pcall(body, operands, *, name, grid, in_specs, out_specs, out_shape, scratch_shapes=(), semantics, riders=()):
    single = not isinstance(out_shape, (tuple, list))
    out_specs = (out_specs,) if single else tuple(out_specs)
    out_shape = (out_shape,) if single else tuple(out_shape)
    if not riders:
        res = pl.pallas_call(body, name=name, grid=grid, in_specs=list(in_specs), out_specs=out_specs,
                             out_shape=out_shape, scratch_shapes=list(scratch_shapes),
                             compiler_params=_cp(*semantics))(*operands)
        return (res[0] if single else res), []
    n_in, n_out, n_scr = len(in_specs), len(out_shape), len(scratch_shapes)
    ex_in = [len(ex.operands) for ex in riders]
    ex_out = [len(ex.out_shape) for ex in riders]
    ex_scr = [len(ex.scratch) for ex in riders]
    tot_in, tot_out = n_in + sum(ex_in), n_out + sum(ex_out)

    def wrapped(*refs):
        ins, outs, scr = refs[:tot_in], refs[tot_in:tot_in + tot_out], refs[tot_in + tot_out:]
        ids = [pl.program_id(d) for d in range(len(grid))]
        first = functools.reduce(lambda p, t: p & t, [i == 0 for i in ids])
        last = functools.reduce(lambda p, t: p & t, [i == n - 1 for i, n in zip(ids, grid)])
        parts, oi, oo, os_ = [], n_in, n_out, n_scr
        for k in range(len(riders)):
            parts.append((ins[oi:oi + ex_in[k]], outs[oo:oo + ex_out[k]], scr[os_:os_ + ex_scr[k]]))
            oi, oo, os_ = oi + ex_in[k], oo + ex_out[k], os_ + ex_scr[k]

        @pl.when(first)
        def _():
            for ex, part in zip(riders, parts):
                ex.start(*part)

        body(*ins[:n_in], *outs[:n_out], *scr[:n_scr])

        @pl.when(last)
        def _():
            for ex, part in zip(riders, parts):
                ex.finish(*part)

    aliases, oi, oo = {}, n_in, n_out
    all_ops, all_shapes, all_scr = list(operands), list(out_shape), list(scratch_shapes)
    for k, ex in enumerate(riders):
        aliases.update({oi + a: oo + b for a, b in ex.aliases.items()})
        oi, oo = oi + ex_in[k], oo + ex_out[k]
        all_ops += list(ex.operands)
        all_shapes += list(ex.out_shape)
        all_scr += list(ex.scratch)
    res = pl.pallas_call(
        wrapped, name=name, grid=grid, in_specs=list(in_specs) + [ANY] * sum(ex_in),
        out_specs=out_specs + (ANY,) * sum(ex_out), out_shape=tuple(all_shapes), scratch_shapes=all_scr,
        input_output_aliases=aliases, compiler_params=_cp(*(["arbitrary"] * len(grid))))(*all_ops)
    own = res[0] if single else tuple(res[:n_out])
    carried, oo = [], n_out
    for k in range(len(riders)):
        carried.append(list(res[oo:oo + ex_out[k]]))
        oo += ex_out[k]
    return own, carried


def _matmul(name, operands, *, grid, in_specs, o_spec, out_shape, out_dtype, dims, has_res=False, riders=()):
    nk = grid[2]
    assert nk == 1 or (out_dtype == F32 and not has_res)

    def body(*refs):
        if has_res:
            a_ref, b_ref, r_ref, o_ref = refs
        else:
            a_ref, b_ref, o_ref = refs
            r_ref = None
        part = lax.dot_general(a_ref[...].astype(BF16), b_ref[...].astype(BF16), (dims, ((), ())),
                               preferred_element_type=F32)
        if nk == 1:
            if r_ref is not None:
                part = part + r_ref[...]
            o_ref[...] = part.astype(o_ref.dtype)
        else:
            k = pl.program_id(2)

            @pl.when(k == 0)
            def _():
                o_ref[...] = part

            @pl.when(k > 0)
            def _():
                o_ref[...] += part

    out, carried = _pcall(body, operands, name=name, grid=grid, in_specs=in_specs, out_specs=o_spec,
                          out_shape=jax.ShapeDtypeStruct(out_shape, out_dtype),
                          semantics=("parallel", "parallel", "arbitrary"), riders=riders)
    return (out, carried) if riders else out


NN = ((1,), (0,))
NT = ((1,), (1,))
TN = ((0,), (0,))


def _mm_cs(name, a, w, out_dtype, riders=()):
    M, K = a.shape
    nq, _, wd = w.shape
    tm = _row_tile(M)
    return _matmul(name, (a, w), grid=(nq, M // tm, 1),
                   in_specs=[pl.BlockSpec((tm, K), lambda j, i, k: (i, 0)),
                             pl.BlockSpec((None, K, wd), lambda j, i, k: (j, 0, 0))],
                   o_spec=pl.BlockSpec((tm, wd), lambda j, i, k: (i, j)),
                   out_shape=(M, nq * wd), out_dtype=out_dtype, dims=NN, riders=riders)


def _mm_rs(name, a, w, out_dtype, res=None, tn=512):
    M, K = a.shape
    N = w.shape[1]
    tm = _row_tile(M)
    tn = min(tn, N)
    ops = (a, w) if res is None else (a, w, res)
    in_specs = [pl.BlockSpec((tm, K), lambda i, j, k: (i, 0)),
                pl.BlockSpec((K, tn), lambda i, j, k: (0, j))]
    if res is not None:
        in_specs.append(pl.BlockSpec((tm, tn), lambda i, j, k: (i, j)))
    return _matmul(name, ops, grid=(M // tm, N // tn, 1), in_specs=in_specs,
                   o_spec=pl.BlockSpec((tm, tn), lambda i, j, k: (i, j)),
                   out_shape=(M, N), out_dtype=out_dtype, dims=NN, has_res=res is not None)


def _mm_nt_rs(name, g, w, out_dtype, to):
    M, N = g.shape
    K = w.shape[0]
    tm = _row_tile(M)
    return _matmul(name, (g, w), grid=(M // tm, K // to, 1),
                   in_specs=[pl.BlockSpec((tm, N), lambda i, j, k: (i, 0)),
                             pl.BlockSpec((to, N), lambda i, j, k: (j, 0))],
                   o_spec=pl.BlockSpec((tm, to), lambda i, j, k: (i, j)),
                   out_shape=(M, K), out_dtype=out_dtype, dims=NT)


def _mm_nt_cs(name, g, w, out_dtype, riders=(), stacked=False):
    M = g.shape[-2]
    nq, K, wd = w.shape
    tm = _row_tile(M, 256)

    def body(g_ref, w_ref, o_ref):
        acc = None
        for q in range(nq):
            gq = g_ref[q // 2, :, (q % 2) * wd:(q % 2 + 1) * wd] if stacked else g_ref[:, q * wd:(q + 1) * wd]
            part = _dot_nt(gq, w_ref[q])
            acc = part if acc is None else acc + part
        o_ref[...] = acc.astype(o_ref.dtype)

    g_spec = (pl.BlockSpec((2, tm, 2 * wd), lambda i: (0, i, 0)) if stacked
              else pl.BlockSpec((tm, nq * wd), lambda i: (i, 0)))
    out, carried = _pcall(
        body, (g, w), name=name, grid=(M // tm,),
        in_specs=[g_spec, pl.BlockSpec((nq, K, wd), lambda i: (0, 0, 0))],
        out_specs=pl.BlockSpec((tm, K), lambda i: (i, 0)),
        out_shape=jax.ShapeDtypeStruct((M, K), out_dtype), semantics=("parallel",), riders=riders)
    return (out, carried) if riders else out


def _mm_tn_rs(name, a, g, to, tn=512):
    T, M = a.shape
    N = g.shape[1]
    tt = _row_tile(T, 1024)
    tn = min(tn, N)
    return _matmul(name, (a, g), grid=(M // to, N // tn, T // tt),
                   in_specs=[pl.BlockSpec((tt, to), lambda i, j, k: (k, i)),
                             pl.BlockSpec((tt, tn), lambda i, j, k: (k, j))],
                   o_spec=pl.BlockSpec((to, tn), lambda i, j, k: (i, j)),
                   out_shape=(M, N), out_dtype=F32, dims=TN)


def _mm_tn_cs(name, a, g, nq, to, riders=(), stacked=False):
    T, M = a.shape
    wd = g.shape[-1] * (2 if stacked else 1) // nq
    tt = _row_tile(T, 1024)
    g_spec = (pl.BlockSpec((None, tt, wd), lambda i, j, k: (j // 2, k, j % 2)) if stacked
              else pl.BlockSpec((tt, wd), lambda i, j, k: (k, j)))
    return _matmul(name, (a, g), grid=(M // to, nq, T // tt),
                   in_specs=[pl.BlockSpec((tt, to), lambda i, j, k: (k, i)), g_spec],
                   o_spec=pl.BlockSpec((None, to, wd), lambda i, j, k: (j, i, 0)),
                   out_shape=(nq, M, wd), out_dtype=F32, dims=TN, riders=riders)


def _rms_fwd(name, x, g):
    T, D = x.shape
    tm = _row_tile(T)

    def body(x_ref, g_ref, o_ref):
        xv = x_ref[...]
        r = lax.rsqrt(jnp.mean(xv * xv, axis=-1, keepdims=True) + EPS)
        o_ref[...] = (xv * r * g_ref[...]).astype(o_ref.dtype)

    return pl.pallas_call(
        body, name=name, grid=(T // tm,),
        in_specs=[pl.BlockSpec((tm, D), lambda i: (i, 0)), pl.BlockSpec((1, D), lambda i: (0, 0))],
        out_specs=pl.BlockSpec((tm, D), lambda i: (i, 0)),
        out_shape=jax.ShapeDtypeStruct((T, D), BF16), compiler_params=_cp("parallel"),
    )(x, g)


def _rms_bwd(name, x, g, dh, dres, riders=()):
    T, D = x.shape
    tm = _row_tile(T)
    has_res = dres is not None

    def body(*refs):
        if has_res:
            x_ref, g_ref, dh_ref, dr_ref, dx_ref, dg_ref = refs
        else:
            x_ref, g_ref, dh_ref, dx_ref, dg_ref = refs

        @pl.when(pl.program_id(0) == 0)
        def _():
            dg_ref[...] = jnp.zeros_like(dg_ref)

        xv = x_ref[...]
        r = lax.rsqrt(jnp.mean(xv * xv, axis=-1, keepdims=True) + EPS)
        n = xv * r
        dhv = dh_ref[...]
        dg_ref[...] += jnp.sum(dhv * n, axis=0, keepdims=True)
        dn = dhv * g_ref[...]
        dx = r * (dn - n * jnp.mean(dn * n, axis=-1, keepdims=True))
        if has_res:
            dx = dx + dr_ref[...]
        dx_ref[...] = dx

    row = pl.BlockSpec((tm, D), lambda i: (i, 0))
    vec = pl.BlockSpec((1, D), lambda i: (0, 0))
    ops = (x, g, dh, dres) if has_res else (x, g, dh)
    out, carried = _pcall(
        body, ops, name=name, grid=(T // tm,),
        in_specs=[row, vec, row] + ([row] if has_res else []),
        out_specs=(row, vec),
        out_shape=(jax.ShapeDtypeStruct((T, D), F32), jax.ShapeDtypeStruct((1, D), F32)),
        semantics=("arbitrary",), riders=riders)
    return (out, carried) if riders else out


def _loss_head(x2, tgt, g):
    T, D = x2.shape
    tm = _row_tile(T)

    def body(x_ref, t_ref, g_ref, dx_ref, dg_ref, loss_ref):
        @pl.when(pl.program_id(0) == 0)
        def _():
            dg_ref[...] = jnp.zeros_like(dg_ref)
            loss_ref[...] = jnp.zeros_like(loss_ref)

        xv = x_ref[...]
        gv = g_ref[...]
        r = lax.rsqrt(jnp.mean(xv * xv, axis=-1, keepdims=True) + EPS)
        n = xv * r
        diff = n * gv - t_ref[...]
        loss_ref[...] += 0.5 * jnp.sum(jnp.mean(diff * diff, axis=-1, keepdims=True))
        dy = diff * (1.0 / D)
        dg_ref[...] += jnp.sum(dy * n, axis=0, keepdims=True)
        dn = dy * gv
        dx_ref[...] = r * (dn - n * jnp.mean(dn * n, axis=-1, keepdims=True))

    row = pl.BlockSpec((tm, D), lambda i: (i, 0))
    vec = pl.BlockSpec((1, D), lambda i: (0, 0))
    return pl.pallas_call(
        body, name="loss_head", grid=(T // tm,),
        in_specs=[row, row, vec],
        out_specs=(row, vec, pl.BlockSpec((8, 128), lambda i: (0, 0))),
        out_shape=(jax.ShapeDtypeStruct((T, D), F32), jax.ShapeDtypeStruct((1, D), F32),
                   jax.ShapeDtypeStruct((8, 128), F32)),
        compiler_params=_cp("arbitrary"),
    )(x2, tgt, g)


def _gmlp_pieces(zu, zv, lng, lnb, ws_ref, bs_ref):
    u, du = _gelu_and_grad(zu)
    v, dv = _gelu_and_grad(zv)
    mu = jnp.mean(v, axis=-1, keepdims=True)
    vc = v - mu
    rstd = lax.rsqrt(jnp.mean(vc * vc, axis=-1, keepdims=True) + EPS)
    vhat = vc * rstd
    vn = vhat * lng + lnb
    row = lax.broadcasted_iota(jnp.int32, (GM_CHUNK, GM_CHUNK), 0)
    col = lax.broadcasted_iota(jnp.int32, (GM_CHUNK, GM_CHUNK), 1)
    tril = row >= col
    wms, mixed = [], []
    for g in range(GM_GROUPS):
        sl = slice(g * 128, (g + 1) * 128)
        wm = jnp.where(tril, ws_ref[g], 0.0)
        wms.append(wm)
        mixed.append(_dot(wm, vn[:, sl]) + bs_ref[g])
    return u, du, dv, rstd, vhat, vn, wms, mixed, tril


def _gmlp_fwd(proj, lng, lnb, ws, bs_col):
    T = proj.shape[0]
    n = T // GM_CHUNK

    def body(zu_ref, zv_ref, lng_ref, lnb_ref, ws_ref, bs_ref, o_ref):
        u, _, _, _, _, _, _, mixed, _ = _gmlp_pieces(zu_ref[...], zv_ref[...], lng_ref[...], lnb_ref[...],
                                                     ws_ref, bs_ref)
        for g in range(GM_GROUPS):
            sl = slice(g * 128, (g + 1) * 128)
            o_ref[:, sl] = (u[:, sl] * mixed[g]).astype(o_ref.dtype)

    vec = pl.BlockSpec((1, GM_WIDTH), lambda i: (0, 0))
    return pl.pallas_call(
        body, name="gmlp_fwd", grid=(n,),
        in_specs=[pl.BlockSpec((GM_CHUNK, 512), lambda i: (i, COL_ZU)),
                  pl.BlockSpec((GM_CHUNK, 512), lambda i: (i, COL_ZV)),
                  vec, vec,
                  pl.BlockSpec((GM_GROUPS, 128, 128), lambda i: (0, 0, 0)),
                  pl.BlockSpec((GM_GROUPS, 128, 1), lambda i: (0, 0, 0))],
        out_specs=pl.BlockSpec((GM_CHUNK, 512), lambda i: (i, 0)),
        out_shape=jax.ShapeDtypeStruct((T, GM_WIDTH), BF16), compiler_params=_cp("parallel"),
    )(proj, proj, lng, lnb, ws, bs_col)


def _gmlp_bwd(proj, d_out, lng, lnb, ws, bs_col):
    T = proj.shape[0]
    n = T // GM_CHUNK

    def body(zu_ref, zv_ref, do_ref, lng_ref, lnb_ref, ws_ref, bs_ref,
             dzu_ref, dzv_ref, dws_ref, dbs_ref, dlng_ref, dlnb_ref, dm_acc):
        i = pl.program_id(0)

        @pl.when(i == 0)
        def _():
            dws_ref[...] = jnp.zeros_like(dws_ref)
            dlng_ref[...] = jnp.zeros_like(dlng_ref)
            dlnb_ref[...] = jnp.zeros_like(dlnb_ref)
            dm_acc[...] = jnp.zeros_like(dm_acc)

        lng_v = lng_ref[...]
        u, du, dv, rstd, vhat, vn, wms, mixed, tril = _gmlp_pieces(zu_ref[...], zv_ref[...], lng_v, lnb_ref[...],
                                                                  ws_ref, bs_ref)
        do = do_ref[...]
        dvn_parts = []
        for g in range(GM_GROUPS):
            sl = slice(g * 128, (g + 1) * 128)
            dog = do[:, sl]
            dzu_ref[:, sl] = (dog * mixed[g] * du[:, sl]).astype(dzu_ref.dtype)
            dmix = dog * u[:, sl]
            dm_acc[:, sl] += dmix
            dws_ref[g] += jnp.where(tril, _dot_nt(dmix, vn[:, sl]), 0.0)
            dvn_parts.append(_dot_tn(wms[g], dmix))
        dvn = jnp.concatenate(dvn_parts, axis=1)
        dlng_ref[...] += jnp.sum(dvn * vhat, axis=0, keepdims=True)
        dlnb_ref[...] += jnp.sum(dvn, axis=0, keepdims=True)
        dvh = dvn * lng_v
        dvv = rstd * (dvh - jnp.mean(dvh, axis=-1, keepdims=True)
                      - vhat * jnp.mean(dvh * vhat, axis=-1, keepdims=True))
        dzv_ref[...] = (dvv * dv).astype(dzv_ref.dtype)

        @pl.when(i == n - 1)
        def _():
            for g in range(GM_GROUPS):
                dbs_ref[g] = jnp.sum(dm_acc[:, g * 128:(g + 1) * 128], axis=1, keepdims=True)

    vec = pl.BlockSpec((1, GM_WIDTH), lambda i: (0, 0))
    wsp = pl.BlockSpec((GM_GROUPS, 128, 128), lambda i: (0, 0, 0))
    bsp = pl.BlockSpec((GM_GROUPS, 128, 1), lambda i: (0, 0, 0))
    tile = pl.BlockSpec((GM_CHUNK, 512), lambda i: (i, 0))
    return pl.pallas_call(
        body, name="gmlp_bwd", grid=(n,),
        in_specs=[pl.BlockSpec((GM_CHUNK, 512), lambda i: (i, COL_ZU)),
                  pl.BlockSpec((GM_CHUNK, 512), lambda i: (i, COL_ZV)),
                  pl.BlockSpec((None, GM_CHUNK, 512), lambda i: (0, i, 0)), vec, vec, wsp, bsp],
        out_specs=(tile, tile, wsp, bsp, vec, vec),
        out_shape=(jax.ShapeDtypeStruct((T, GM_WIDTH), BF16), jax.ShapeDtypeStruct((T, GM_WIDTH), BF16),
                   jax.ShapeDtypeStruct((GM_GROUPS, 128, 128), F32), jax.ShapeDtypeStruct((GM_GROUPS, 128, 1), F32),
                   jax.ShapeDtypeStruct((1, GM_WIDTH), F32), jax.ShapeDtypeStruct((1, GM_WIDTH), F32)),
        scratch_shapes=[pltpu.VMEM((GM_CHUNK, GM_WIDTH), F32)],
        compiler_params=_cp("arbitrary"),
    )(proj, proj, d_out, lng, lnb, ws, bs_col)


def _hgrn_lower_bound(lbl):
    return 1.0 / (1.0 + jnp.exp(lbl[1:2, :] - lbl[0:1, :]))


def _hgrn_gates(hq, hf, lb):
    sg = _sigmoid(hf)
    fg = lb + (1.0 - lb) * sg
    sq = _sigmoid(hq)
    C = HG_CHUNK
    row = lax.broadcasted_iota(jnp.int32, (C, C), 0)
    col = lax.broadcasted_iota(jnp.int32, (C, C), 1)
    tril = row >= col
    logf = jnp.log(fg)
    a = _dot_01(tril, logf)
    a_last = jnp.sum(logf, axis=0, keepdims=True)
    first_half = lax.broadcasted_iota(jnp.int32, logf.shape, 0) < (C // 2)
    a_mid = jnp.sum(jnp.where(first_half, logf, 0.0), axis=0, keepdims=True)
    return sg, fg, sq, tril, a, a_last, a_mid


def _hgrn_fwd(proj, lbl, gh, B, S, riders=()):
    T = B * S
    C = HG_CHUNK
    NC = S // C
    W = HG_HEADS * HG_DIM

    def body(q_ref, f_ref, i_ref, g_ref, lbl_ref, gh_ref, o_ref, bo_ref, st_ref, state):
        @pl.when(pl.program_id(1) == 0)
        def _():
            state[...] = jnp.zeros_like(state)

        lb_all = _hgrn_lower_bound(lbl_ref[...])
        ghv = gh_ref[...]
        for h in range(HG_HEADS):
            sl = slice(h * 128, (h + 1) * 128)
            hq = q_ref[:, sl]
            sg, fg, sq, tril, a, a_last, a_mid = _hgrn_gates(hq, f_ref[:, sl], lb_all[:, sl])
            k = 1.0 - fg
            q = hq * sq
            v = i_ref[:, sl]
            qe = q * jnp.exp(a)
            qi = q * jnp.exp(a - a_mid)
            ki = k * jnp.exp(a_mid - a)
            kl = k * jnp.exp(a_last - a)
            p = jnp.where(tril, _dot_nt(qi, ki), 0.0)
            st = state[h]
            st_ref[h] = st
            o = _dot_nt(qe, st) + _dot(p, v)
            state[h] = st * jnp.exp(a_last) + _dot_tn(v, kl)
            o_ref[:, sl] = o
            r = lax.rsqrt(jnp.mean(o * o, axis=-1, keepdims=True) + EPS)
            hg = g_ref[:, sl]
            bo_ref[:, sl] = (o * r * ghv * (hg * _sigmoid(hg))).astype(bo_ref.dtype)

    def col(cb):
        return pl.BlockSpec((C, 512), lambda b, c: (b * NC + c, cb))

    tile = pl.BlockSpec((C, W), lambda b, c: (b * NC + c, 0))
    out, carried = _pcall(
        body, (proj, proj, proj, proj, lbl, gh), name="hgrn_fwd", grid=(B, NC),
        in_specs=[col(COL_HQ), col(COL_HF), col(COL_HI), col(COL_HG),
                  pl.BlockSpec((2, W), lambda b, c: (0, 0)), pl.BlockSpec((1, HG_DIM), lambda b, c: (0, 0))],
        out_specs=(tile, tile, pl.BlockSpec((None, HG_HEADS, 128, 128), lambda b, c: (b * NC + c, 0, 0, 0))),
        out_shape=(jax.ShapeDtypeStruct((T, W), F32), jax.ShapeDtypeStruct((T, W), BF16),
                   jax.ShapeDtypeStruct((B * NC, HG_HEADS, 128, 128), F32)),
        scratch_shapes=[pltpu.VMEM((HG_HEADS, 128, 128), F32)],
        semantics=("parallel", "arbitrary"), riders=riders)
    return (out, carried) if riders else out


def _hgrn_bwd(proj, o_saved, states, d_out, lbl, gh, B, S, riders=()):
    T = B * S
    C = HG_CHUNK
    NC = S // C
    W = HG_HEADS * HG_DIM

    def body(q_ref, f_ref, i_ref, g_ref, o_ref, st_ref, do_ref, lbl_ref, gh_ref,
             dq_ref, df_ref, di_ref, dg_ref, dlbl_ref, dgh_ref, dstate, dlb_acc):
        b = pl.program_id(0)
        c = pl.program_id(1)

        @pl.when(c == 0)
        def _():
            dstate[...] = jnp.zeros_like(dstate)

        @pl.when((b == 0) & (c == 0))
        def _():
            dgh_ref[...] = jnp.zeros_like(dgh_ref)
            dlb_acc[...] = jnp.zeros_like(dlb_acc)

        lbl_v = lbl_ref[...]
        lb_all = _hgrn_lower_bound(lbl_v)
        ghv = gh_ref[...]
        row = lax.broadcasted_iota(jnp.int32, (C, C), 0)
        colm = lax.broadcasted_iota(jnp.int32, (C, C), 1)
        triu = colm >= row
        for h in range(HG_HEADS):
            sl = slice(h * 128, (h + 1) * 128)
            hq = q_ref[:, sl]
            lb = lb_all[:, sl]
            sg, fg, sq, tril, a, a_last, a_mid = _hgrn_gates(hq, f_ref[:, sl], lb)
            k = 1.0 - fg
            q = hq * sq
            v = i_ref[:, sl]
            ea = jnp.exp(a)
            ei = jnp.exp(a - a_mid)
            eki = jnp.exp(a_mid - a)
            ekl = jnp.exp(a_last - a)
            e_last = jnp.exp(a_last)
            qe = q * ea
            qi = q * ei
            ki = k * eki
            kl = k * ekl
            p = jnp.where(tril, _dot_nt(qi, ki), 0.0)
            st = st_ref[h]
            o = o_ref[:, sl]
            hg = g_ref[:, sl]
            sgg = _sigmoid(hg)
            r = lax.rsqrt(jnp.mean(o * o, axis=-1, keepdims=True) + EPS)
            n = o * r
            dbo = do_ref[:, sl]
            dg_ref[:, sl] = (dbo * n * ghv * (sgg * (1.0 + hg * (1.0 - sgg)))).astype(dg_ref.dtype)
            don = dbo * (hg * sgg)
            dgh_ref[...] += jnp.sum(don * n, axis=0, keepdims=True)
            dn = don * ghv
            d_o = r * (dn - n * jnp.mean(dn * n, axis=-1, keepdims=True))
            dst = dstate[h]
            dp = jnp.where(tril, _dot3(d_o, v, NT), 0.0)
            d_qe = _dot3(d_o, st, NN)
            d_qi = _dot3(dp, ki, NN)
            d_ki = _dot3(dp, qi, TN)
            d_kl = _dot3(v, dst, NN)
            dv = _dot_tn(p, d_o) + _dot_nt(kl, dst)
            dstate[h] = dst * e_last + _dot3(d_o, qe, TN)
            d_a_last = jnp.sum(dst * st, axis=0, keepdims=True) * e_last + jnp.sum(d_kl * kl, axis=0, keepdims=True)
            dq = d_qe * ea + d_qi * ei
            dk = d_ki * eki + d_kl * ekl
            da = d_qe * qe + d_qi * qi - d_ki * ki - d_kl * kl
            dlogf = _dot_01(triu, da) + d_a_last
            dfg = dlogf / fg - dk
            df_ref[:, sl] = (dfg * (1.0 - lb) * sg * (1.0 - sg)).astype(df_ref.dtype)
            dlb_acc[:, sl] += jnp.sum(dfg * (1.0 - sg), axis=0, keepdims=True)
            dq_ref[:, sl] = (dq * (sq * (1.0 + hq * (1.0 - sq)))).astype(dq_ref.dtype)
            di_ref[:, sl] = dv.astype(di_ref.dtype)

        @pl.when((b == B - 1) & (c == NC - 1))
        def _():
            dlb = dlb_acc[...]
            p0 = lb_all
            first = lax.broadcasted_iota(jnp.int32, (2, W), 0) == 0
            dlbl_ref[...] = jnp.where(first, dlb * p0 * (1.0 - p0), -dlb * p0 * (1.0 - p0))

    def col(cb):
        return pl.BlockSpec((C, 512), lambda b, c: (b * NC + NC - 1 - c, cb))

    tile = pl.BlockSpec((C, W), lambda b, c: (b * NC + NC - 1 - c, 0))
    out, carried = _pcall(
        body, (proj, proj, proj, proj, o_saved, states, d_out, lbl, gh), name="hgrn_bwd", grid=(B, NC),
        in_specs=[col(COL_HQ), col(COL_HF), col(COL_HI), col(COL_HG), tile,
                  pl.BlockSpec((None, HG_HEADS, 128, 128), lambda b, c: (b * NC + NC - 1 - c, 0, 0, 0)),
                  pl.BlockSpec((None, C, W), lambda b, c: (1, b * NC + NC - 1 - c, 0)),
                  pl.BlockSpec((2, W), lambda b, c: (0, 0)), pl.BlockSpec((1, HG_DIM), lambda b, c: (0, 0))],
        out_specs=(tile, tile, tile, tile,
                   pl.BlockSpec((2, W), lambda b, c: (0, 0)), pl.BlockSpec((1, HG_DIM), lambda b, c: (0, 0))),
        out_shape=(jax.ShapeDtypeStruct((T, W), BF16),) * 4
        + (jax.ShapeDtypeStruct((2, W), F32), jax.ShapeDtypeStruct((1, HG_DIM), F32)),
        scratch_shapes=[pltpu.VMEM((HG_HEADS, 128, 128), F32), pltpu.VMEM((1, W), F32)],
        semantics=("arbitrary", "arbitrary"), riders=riders)
    return (out, carried) if riders else out


_XA_SCALE = XA_DIM ** -0.5


def _attn_probs(qh, kh):
    s = _dot_nt(qh, kh) * _XA_SCALE
    e = jnp.exp(s - jnp.max(s, axis=-1, keepdims=True))
    return e / jnp.sum(e, axis=-1, keepdims=True)


def _attn_fwd(proj, kv, B, S):
    T = B * S
    tq = _row_tile(S)
    nq = S // tq
    W = XA_HEADS * XA_DIM

    def body(q_ref, kv_ref, o_ref):
        for h in range(XA_HEADS):
            sl = slice(h * 128, (h + 1) * 128)
            p = _attn_probs(q_ref[:, sl], kv_ref[:, sl])
            o_ref[:, sl] = _dot(p, kv_ref[:, W + h * 128:W + (h + 1) * 128]).astype(o_ref.dtype)

    return pl.pallas_call(
        body, name="attn_fwd", grid=(B, nq),
        in_specs=[pl.BlockSpec((tq, 512), lambda b, i: (b * nq + i, COL_XQ)),
                  pl.BlockSpec((MEM_LEN, 2 * W), lambda b, i: (b, 0))],
        out_specs=pl.BlockSpec((tq, W), lambda b, i: (b * nq + i, 0)),
        out_shape=jax.ShapeDtypeStruct((T, W), BF16), compiler_params=_cp("parallel", "parallel"),
    )(proj, kv)


def _attn_bwd(proj, kv, d_out, B, S):
    T = B * S
    tq = _row_tile(S)
    nq = S // tq
    W = XA_HEADS * XA_DIM

    def body(q_ref, kv_ref, do_ref, dq_ref, dkv_ref):
        @pl.when(pl.program_id(1) == 0)
        def _():
            dkv_ref[...] = jnp.zeros_like(dkv_ref)

        for h in range(XA_HEADS):
            sl = slice(h * 128, (h + 1) * 128)
            slv = slice(W + h * 128, W + (h + 1) * 128)
            qh = q_ref[:, sl]
            kh = kv_ref[:, sl]
            p = _attn_probs(qh, kh)
            dc = do_ref[:, sl]
            dp = _dot_nt(dc, kv_ref[:, slv])
            ds = p * (dp - jnp.sum(dp * p, axis=-1, keepdims=True)) * _XA_SCALE
            dq_ref[:, sl] = _dot(ds, kh).astype(dq_ref.dtype)
            dkv_ref[:, sl] += _dot_tn(ds, qh)
            dkv_ref[:, slv] += _dot_tn(p, dc)

    kvspec = pl.BlockSpec((MEM_LEN, 2 * W), lambda b, i: (b, 0))
    tile = pl.BlockSpec((tq, W), lambda b, i: (b * nq + i, 0))
    return pl.pallas_call(
        body, name="attn_bwd", grid=(B, nq),
        in_specs=[pl.BlockSpec((tq, 512), lambda b, i: (b * nq + i, COL_XQ)), kvspec,
                  pl.BlockSpec((None, tq, W), lambda b, i: (2, b * nq + i, 0))],
        out_specs=(tile, kvspec),
        out_shape=(jax.ShapeDtypeStruct((T, W), BF16), jax.ShapeDtypeStruct((B * MEM_LEN, 2 * W), F32)),
        compiler_params=_cp("parallel", "arbitrary"),
    )(proj, kv, d_out)


_MERGE_TN = 256


def _gate_specs(tm):
    base = COL_GATE0 // _MERGE_TN
    per = D_MODEL // _MERGE_TN
    return [pl.BlockSpec((tm, _MERGE_TN), functools.partial(lambda i, j, n: (i, base + per * n + j), n=n))
            for n in range(3)]


def _merge_fwd(a_out, b_out, c_out, wb, proj):
    T = a_out.shape[0]
    tm = _row_tile(T)
    tn = _MERGE_TN

    def body(a_ref, b_ref, c_ref, w_ref, g0_ref, g1_ref, g2_ref, m_ref, up_ref):
        acc = None
        for n, (br, gr) in enumerate(((a_ref, g0_ref), (b_ref, g1_ref), (c_ref, g2_ref))):
            up = _dot(br[...], w_ref[n * BR_WIDTH:(n + 1) * BR_WIDTH, :])
            up_ref[n] = up.astype(up_ref.dtype)
            term = _sigmoid(gr[...]) * up
            acc = term if acc is None else acc + term
        m_ref[...] = acc.astype(m_ref.dtype)

    br_spec = pl.BlockSpec((tm, BR_WIDTH), lambda i, j: (i, 0))
    return pl.pallas_call(
        body, name="merge_fwd", grid=(T // tm, D_MODEL // tn),
        in_specs=[br_spec, br_spec, br_spec,
                  pl.BlockSpec((None, 3 * BR_WIDTH, tn), lambda i, j: (j, 0, 0))] + _gate_specs(tm),
        out_specs=(pl.BlockSpec((tm, tn), lambda i, j: (i, j)), pl.BlockSpec((3, tm, tn), lambda i, j: (0, i, j))),
        out_shape=(jax.ShapeDtypeStruct((T, D_MODEL), BF16), jax.ShapeDtypeStruct((3, T, D_MODEL), BF16)),
        compiler_params=_cp("parallel", "parallel"),
    )(a_out, b_out, c_out, wb, proj, proj, proj)


def _branch_bwd_act(d_ups, wb):
    _, T, D = d_ups.shape
    nq, _, wd = wb.shape
    tm = _row_tile(T)

    def body(d_ref, w_ref, o_ref):
        acc = None
        for q in range(nq):
            part = _dot_nt(d_ref[:, q * wd:(q + 1) * wd], w_ref[q])
            acc = part if acc is None else acc + part
        o_ref[...] = acc

    return pl.pallas_call(
        body, name="d_branch", grid=(3, T // tm),
        in_specs=[pl.BlockSpec((None, tm, D), lambda n, i: (n, i, 0)),
                  pl.BlockSpec((nq, BR_WIDTH, wd), lambda n, i: (0, n, 0))],
        out_specs=pl.BlockSpec((None, tm, BR_WIDTH), lambda n, i: (n, i, 0)),
        out_shape=jax.ShapeDtypeStruct((3, T, BR_WIDTH), F32), compiler_params=_cp("parallel", "parallel"),
    )(d_ups, wb)


def _branch_bwd_weight(name, br, d_ups, n):
    T = br.shape[0]
    D = d_ups.shape[2]
    wd = D // N_CHIPS
    tt = _row_tile(T, 1024)

    def body(b_ref, d_ref, o_ref):
        k = pl.program_id(0)
        for q in range(N_CHIPS):
            part = _dot_tn(b_ref[...], d_ref[:, q * wd:(q + 1) * wd])

            @pl.when(k == 0)
            def _():
                o_ref[q] = part

            @pl.when(k > 0)
            def _():
                o_ref[q] += part

    return pl.pallas_call(
        body, name=name, grid=(T // tt,),
        in_specs=[pl.BlockSpec((tt, BR_WIDTH), lambda k: (k, 0)),
                  pl.BlockSpec((None, tt, D), lambda k: (n, k, 0))],
        out_specs=pl.BlockSpec((N_CHIPS, BR_WIDTH, wd), lambda k: (0, 0, 0)),
        out_shape=jax.ShapeDtypeStruct((N_CHIPS, BR_WIDTH, wd), F32), compiler_params=_cp("arbitrary"),
    )(br, d_ups)


def _merge_bwd(d_merged, ups, proj):
    T = d_merged.shape[0]
    tm = _row_tile(T)
    tn = _MERGE_TN

    def body(dm_ref, up_ref, g0_ref, g1_ref, g2_ref, dup_ref, dg0_ref, dg1_ref, dg2_ref):
        dm = dm_ref[...]
        for n, (gr, dgr) in enumerate(((g0_ref, dg0_ref), (g1_ref, dg1_ref), (g2_ref, dg2_ref))):
            gate = _sigmoid(gr[...])
            dup_ref[n] = (dm * gate).astype(dup_ref.dtype)
            dgr[...] = (dm * up_ref[n].astype(F32) * gate * (1.0 - gate)).astype(dgr.dtype)

    tile = pl.BlockSpec((tm, tn), lambda i, j: (i, j))
    tile3 = pl.BlockSpec((3, tm, tn), lambda i, j: (0, i, j))
    return pl.pallas_call(
        body, name="merge_bwd", grid=(T // tm, D_MODEL // tn),
        in_specs=[tile, tile3] + _gate_specs(tm),
        out_specs=(tile3, tile, tile, tile),
        out_shape=(jax.ShapeDtypeStruct((3, T, D_MODEL), BF16),) + (jax.ShapeDtypeStruct((T, D_MODEL), BF16),) * 3,
        compiler_params=_cp("parallel", "parallel"),
    )(d_merged, ups, proj, proj, proj)


_CONV_TF = D_FF // 2
_CONV_TS = 256
_HALO = 8


def _conv_fwd(ab, cw, cb, B, S):
    T = B * S
    ts = _row_tile(S, _CONV_TS)
    tf = _CONV_TF
    nb = D_FF // tf
    tps = S // ts
    hb = ts // _HALO

    def body(a_ref, p_ref, b_ref, w_ref, cb_ref, o_ref):
        start = (pl.program_id(0) % tps) == 0
        prev = jnp.where(start, 0.0, p_ref[...])
        ext = jnp.concatenate([prev, a_ref[...]], axis=0)
        a1 = pltpu.roll(ext, 1, 0)[_HALO:, :]
        a2 = pltpu.roll(ext, 2, 0)[_HALO:, :]
        ac = cb_ref[...] + w_ref[0] * a2 + w_ref[1] * a1 + w_ref[2] * a_ref[...]
        o_ref[...] = (ac * _sigmoid(ac) * b_ref[...]).astype(o_ref.dtype)

    return pl.pallas_call(
        body, name="conv_fwd", grid=(T // ts, nb),
        in_specs=[pl.BlockSpec((ts, tf), lambda i, j: (i, j)),
                  pl.BlockSpec((_HALO, tf), lambda i, j: (jnp.maximum(i * hb - 1, 0), j)),
                  pl.BlockSpec((ts, tf), lambda i, j: (i, j + nb)),
                  pl.BlockSpec((3, 1, tf), lambda i, j: (0, 0, j)),
                  pl.BlockSpec((1, tf), lambda i, j: (0, j))],
        out_specs=pl.BlockSpec((ts, tf), lambda i, j: (i, j)),
        out_shape=jax.ShapeDtypeStruct((T, D_FF), BF16), compiler_params=_cp("parallel", "parallel"),
    )(ab, ab, ab, cw, cb)


def _conv_bwd(ab, d_ff, cw, cb, B, S):
    T = B * S
    ts = _row_tile(S, _CONV_TS)
    tf = _CONV_TF
    nb = D_FF // tf
    tps = S // ts
    hb = ts // _HALO
    last_h = T // _HALO - 1
    n_ext = ts + _HALO

    def body(a_ref, ap_ref, an_ref, b_ref, bn_ref, d_ref, dn_ref, w_ref, cb_ref, dab_ref, dw_ref, dcb_ref):
        i = pl.program_id(1)

        @pl.when(i == 0)
        def _():
            dw_ref[...] = jnp.zeros_like(dw_ref)
            dcb_ref[...] = jnp.zeros_like(dcb_ref)

        start = (i % tps) == 0
        end = (i % tps) == tps - 1
        a = a_ref[...]
        ext = jnp.concatenate([jnp.where(start, 0.0, ap_ref[...]), a, an_ref[...]], axis=0)
        r1 = pltpu.roll(ext, 1, 0)[_HALO:, :]
        r2 = pltpu.roll(ext, 2, 0)[_HALO:, :]
        ac = cb_ref[...] + w_ref[0] * r2 + w_ref[1] * r1 + w_ref[2] * ext[_HALO:, :]
        sg = _sigmoid(ac)
        d_e = jnp.concatenate([d_ref[...], jnp.where(end, 0.0, dn_ref[...])], axis=0)
        b_e = jnp.concatenate([b_ref[...], bn_ref[...]], axis=0)
        dab_ref[1] = (d_e[:ts, :] * (ac * sg)[:ts, :]).astype(dab_ref.dtype)
        dac = d_e * b_e * sg * (1.0 + ac * (1.0 - sg))
        u1 = pltpu.roll(dac, n_ext - 1, 0)[:ts, :]
        u2 = pltpu.roll(dac, n_ext - 2, 0)[:ts, :]
        dac0 = dac[:ts, :]
        dab_ref[0] = (w_ref[2] * dac0 + w_ref[1] * u1 + w_ref[0] * u2).astype(dab_ref.dtype)
        dcb_ref[...] += jnp.sum(dac0, axis=0, keepdims=True)
        dw_ref[2] += jnp.sum(dac0 * a, axis=0, keepdims=True)
        dw_ref[1] += jnp.sum(dac0 * r1[:ts, :], axis=0, keepdims=True)
        dw_ref[0] += jnp.sum(dac0 * r2[:ts, :], axis=0, keepdims=True)

    def cur(off):
        return pl.BlockSpec((ts, tf), lambda j, i: (i, j + off))

    def nxt(off):
        return pl.BlockSpec((_HALO, tf), lambda j, i: (jnp.minimum((i + 1) * hb, last_h), j + off))

    return pl.pallas_call(
        body, name="conv_bwd", grid=(nb, T // ts),
        in_specs=[cur(0), pl.BlockSpec((_HALO, tf), lambda j, i: (jnp.maximum(i * hb - 1, 0), j)), nxt(0),
                  cur(nb), nxt(nb), cur(0), nxt(0),
                  pl.BlockSpec((3, 1, tf), lambda j, i: (0, 0, j)), pl.BlockSpec((1, tf), lambda j, i: (0, j))],
        out_specs=(pl.BlockSpec((2, ts, tf), lambda j, i: (0, i, j)), pl.BlockSpec((3, 1, tf), lambda j, i: (0, 0, j)),
                   pl.BlockSpec((1, tf), lambda j, i: (0, j))),
        out_shape=(jax.ShapeDtypeStruct((2, T, D_FF), BF16),
                   jax.ShapeDtypeStruct((3, 1, D_FF), F32), jax.ShapeDtypeStruct((1, D_FF), F32)),
        compiler_params=_cp("parallel", "arbitrary"),
    )(ab, ab, ab, ab, ab, d_ff, d_ff, cw, cb)


def _local_step(x, mem, tgt, p, comm, B, S):
    g = {}
    h = _rms_fwd("norm1", x, p["norm1_g"])
    proj = comm.carry("in_proj", lambda r: _mm_cs("in_proj", h, comm.w("w_in"), F32, riders=r))
    a_out = _gmlp_fwd(proj, p["ln_v_g"], p["ln_v_b"], p["w_spatial"], p["b_spatial"])
    o_h, b_out, states = comm.carry(
        "hgrn_fwd", lambda r: _hgrn_fwd(proj, p["lb_logits"], p["hgrn_norm_g"], B, S, riders=r))
    memn = _rms_fwd("mem_norm", mem, p["mem_norm_g"])
    kv = _mm_rs("mem_kv", memn, comm.w("w_mem_kv"), F32)
    c_out = _attn_fwd(proj, kv, B, S)
    merged, ups = _merge_fwd(a_out, b_out, c_out, comm.w("w_branch"), proj)
    x1 = _mm_rs("out_proj", merged, comm.w("w_out"), F32, res=x)
    h2 = _rms_fwd("norm2", x1, p["norm2_g"])
    ab = _mm_cs("up_proj", h2, comm.w("w_up"), F32)
    conv_w = comm.w("conv_w")
    ff = _conv_fwd(ab, conv_w, p["conv_b"], B, S)
    x2 = _mm_rs("down_proj", ff, comm.w("w_down"), F32, res=x1)
    dx2, g["final_g"], loss = _loss_head(x2, tgt, p["final_g"])

    comm.grad("w_down", _mm_tn_rs("g_w_down", ff, dx2, to=D_FF // 2))
    d_ff = _mm_nt_rs("d_ff", dx2, comm.w("w_down"), F32, to=D_FF // 2)
    d_ab, g["conv_w"], g["conv_b"] = _conv_bwd(ab, d_ff, conv_w, p["conv_b"], B, S)
    comm.grad("w_up", _mm_tn_cs("g_w_up", h2, d_ab, N_CHIPS, to=512, stacked=True))
    d_h2 = comm.carry("d_h2", lambda r: _mm_nt_cs("d_h2", d_ab, comm.w("w_up"), F32, riders=r, stacked=True))
    d_x1, g["norm2_g"] = _rms_bwd("norm2_bwd", x1, p["norm2_g"], d_h2, dx2)
    comm.grad("w_out", _mm_tn_rs("g_w_out", merged, d_x1, to=512))
    d_merged = _mm_nt_rs("d_merged", d_x1, comm.w("w_out"), F32, to=512)
    d_ups, d_g0, d_g1, d_g2 = _merge_bwd(d_merged, ups, proj)

    d_br = _branch_bwd_act(d_ups, comm.w("w_branch"))
    comm.grad("w_branch", jnp.concatenate(
        [_branch_bwd_weight("g_w_branch%d" % n, br, d_ups, n) for n, br in enumerate((a_out, b_out, c_out))],
        axis=1))

    d_zu, d_zv, g["w_spatial"], g["b_spatial"], g["ln_v_g"], g["ln_v_b"] = _gmlp_bwd(
        proj, d_br, p["ln_v_g"], p["ln_v_b"], p["w_spatial"], p["b_spatial"])
    d_xq, d_kv = _attn_bwd(proj, kv, d_br, B, S)
    comm.grad("w_mem_kv", _mm_tn_rs("g_w_mem_kv", memn, d_kv, to=512))
    d_memn = _mm_nt_rs("d_memn", d_kv, comm.w("w_mem_kv"), F32, to=512)
    _, g["mem_norm_g"] = _rms_bwd("mem_norm_bwd", mem, p["mem_norm_g"], d_memn, None)
    d_hq, d_hf, d_hi, d_hg, g["lb_logits"], g["hgrn_norm_g"] = comm.carry(
        "hgrn_bwd", lambda r: _hgrn_bwd(proj, o_h, states, d_br, p["lb_logits"], p["hgrn_norm_g"], B, S, riders=r))
    d_proj = jnp.concatenate([d_zu, d_zv, d_hq, d_hf, d_hi, d_hg, d_xq, d_g0, d_g1, d_g2], axis=1)
    comm.grad("w_in", comm.carry("g_w_in", lambda r: _mm_tn_cs("g_w_in", h, d_proj, N_CHIPS, to=512, riders=r)))
    d_h = comm.carry("d_h", lambda r: _mm_nt_cs("d_h", d_proj, comm.w("w_in"), F32, riders=r))
    grad_x, g["norm1_g"] = comm.carry(
        "norm1_bwd", lambda r: _rms_bwd("norm1_bwd", x, p["norm1_g"], d_h, d_x1, riders=r))
    return loss, grad_x, g


ANY = pl.BlockSpec(memory_space=pl.ANY)


def _place():
    x, y, c = lax.axis_index("x"), lax.axis_index("y"), lax.axis_index("c")
    other_chips = [(1 - x, y), (x, 1 - y), (1 - x, 1 - y)]
    return x, y, c, other_chips


def _remote(src, dst, send_sem, recv_sem, dev):
    return pltpu.make_async_remote_copy(src_ref=src, dst_ref=dst, send_sem=send_sem, recv_sem=recv_sem,
                                        device_id=dev, device_id_type=MESH_ID)


class _Exchange:
    def __init__(self, operands, out_shape, aliases, scratch, start, finish):
        self.operands, self.out_shape, self.aliases, self.scratch = operands, out_shape, aliases, scratch
        self.start, self.finish = start, finish


def _run_exchanges(name, exs):
    n_in = [len(ex.operands) for ex in exs]
    n_out = [len(ex.out_shape) for ex in exs]
    n_scr = [len(ex.scratch) for ex in exs]

    def body(*refs):
        ins, outs, scr = refs[:sum(n_in)], refs[sum(n_in):sum(n_in) + sum(n_out)], refs[sum(n_in) + sum(n_out):]
        parts, oi, oo, os_ = [], 0, 0, 0
        for k in range(len(exs)):
            parts.append((ins[oi:oi + n_in[k]], outs[oo:oo + n_out[k]], scr[os_:os_ + n_scr[k]]))
            oi, oo, os_ = oi + n_in[k], oo + n_out[k], os_ + n_scr[k]
        for ex, part in zip(exs, parts):
            ex.start(*part)
        for ex, part in zip(exs, parts):
            ex.finish(*part)

    aliases, ops, shapes, scratch, oi, oo = {}, [], [], [], 0, 0
    for k, ex in enumerate(exs):
        aliases.update({oi + a: oo + b for a, b in ex.aliases.items()})
        oi, oo = oi + n_in[k], oo + n_out[k]
        ops += list(ex.operands)
        shapes += list(ex.out_shape)
        scratch += list(ex.scratch)
    res = pl.pallas_call(
        body, name=name, in_specs=[ANY] * len(ops), out_specs=(ANY,) * len(shapes), out_shape=tuple(shapes),
        input_output_aliases=aliases, scratch_shapes=scratch,
    )(*ops)
    out, oo = [], 0
    for k in range(len(exs)):
        out.append(list(res[oo:oo + n_out[k]]))
        oo += n_out[k]
    return out


def _ex_all_gather(slabs, halved):
    n = len(slabs)

    def rows(a, cc):
        if not halved[a]:
            return slice(None)
        hr = slabs[a].shape[1] // 2
        return pl.ds(cc * hr, hr)

    def ici(bufs, scr, a, j, chip, c, mine):
        px, py = chip
        x, y, _, _ = _place()
        qs = 2 * x + y if mine else 2 * px + py
        piece = bufs[a].at[qs, rows(a, c)]
        return _remote(piece, piece, scr[0].at[3 * a + j], scr[1].at[3 * a + j], (px, py, c))

    def d2d(bufs, scr, a, j, chip, cc):
        px, py = chip
        x, y, c, _ = _place()
        piece = bufs[a].at[2 * px + py, rows(a, cc)]
        return _remote(piece, piece, scr[2].at[3 * a + j], scr[3].at[3 * a + j], (x, y, 1 - c))

    def start(ins, outs, scr):
        _, _, c, chips = _place()
        for j, chip in enumerate(chips):
            for a in range(n):
                ici(outs, scr, a, j, chip, c, True).start()

    def finish(ins, outs, scr):
        _, _, c, chips = _place()
        for j, chip in enumerate(chips):
            for a in range(n):
                ici(outs, scr, a, j, chip, c, False).wait_recv()
                if halved[a]:
                    d2d(outs, scr, a, j, chip, c).start()
        for j, chip in enumerate(chips):
            for a in range(n):
                if halved[a]:
                    d2d(outs, scr, a, j, chip, 1 - c).wait_recv()
        for j, chip in enumerate(chips):
            for a in range(n):
                ici(outs, scr, a, j, chip, c, True).wait_send()
                if halved[a]:
                    d2d(outs, scr, a, j, chip, c).wait_send()

    return _Exchange(list(slabs), [jax.ShapeDtypeStruct(s.shape, s.dtype) for s in slabs],
                     {a: a for a in range(n)}, [pltpu.SemaphoreType.DMA((3 * n,))] * 4, start, finish)


def _ex_to_sibling(grads):
    n = len(grads)

    def copy(ins, outs, scr, a):
        x, y, c, _ = _place()
        hr = grads[a].shape[1] // 2
        return _remote(ins[a].at[:, pl.ds((1 - c) * hr, hr), :], outs[a], scr[0].at[a], scr[1].at[a], (x, y, 1 - c))

    def start(ins, outs, scr):
        for a in range(n):
            copy(ins, outs, scr, a).start()

    def finish(ins, outs, scr):
        for a in range(n):
            copy(ins, outs, scr, a).wait()

    out_shape = [jax.ShapeDtypeStruct((g.shape[0], g.shape[1] // 2, g.shape[2]), g.dtype) for g in grads]
    return _Exchange(list(grads), out_shape, {}, [pltpu.SemaphoreType.DMA((n,))] * 2, start, finish)


def _ex_to_owner(parts):
    n = len(parts)

    def copy(ins, outs, scr, a, j, chip):
        _, _, c, _ = _place()
        px, py = chip
        return _remote(ins[a].at[2 * px + py], outs[a].at[j], scr[0].at[3 * a + j], scr[1].at[3 * a + j],
                       (px, py, c))

    def start(ins, outs, scr):
        for j, chip in enumerate(_place()[3]):
            for a in range(n):
                copy(ins, outs, scr, a, j, chip).start()

    def finish(ins, outs, scr):
        for j, chip in enumerate(_place()[3]):
            for a in range(n):
                copy(ins, outs, scr, a, j, chip).wait()

    out_shape = [jax.ShapeDtypeStruct((3,) + p.shape[1:], p.dtype) for p in parts]
    return _Exchange(list(parts), out_shape, {}, [pltpu.SemaphoreType.DMA((3 * n,))] * 2, start, finish)


def _ex_share_halves(bufs):
    n = len(bufs)

    def copy(outs, scr, a, cc):
        x, y, c, _ = _place()
        hr = bufs[a].shape[0] // 2
        piece = outs[a].at[pl.ds(cc * hr, hr), :]
        return _remote(piece, piece, scr[0].at[a], scr[1].at[a], (x, y, 1 - c))

    def start(ins, outs, scr):
        c = _place()[2]
        for a in range(n):
            copy(outs, scr, a, c).start()

    def finish(ins, outs, scr):
        c = _place()[2]
        for a in range(n):
            copy(outs, scr, a, c).wait_send()
            copy(outs, scr, a, 1 - c).wait_recv()

    return _Exchange(list(bufs), [jax.ShapeDtypeStruct(b.shape, b.dtype) for b in bufs], {a: a for a in range(n)},
                     [pltpu.SemaphoreType.DMA((n,))] * 2, start, finish)


def _ex_gather_small(arrs):
    n = len(arrs)

    def peer_of(m):
        x, y, c, _ = _place()
        return (1 - x if m & 4 else x, 1 - y if m & 2 else y, 1 - c if m & 1 else c)

    def start(ins, outs, scr):
        x, y, c, _ = _place()
        for m in range(1, N_DEV):
            for a in range(n):
                k = (N_DEV - 1) * a + m - 1
                _remote(ins[a], outs[a].at[4 * x + 2 * y + c], scr[0].at[k], scr[1].at[k], peer_of(m)).start()

    def finish(ins, outs, scr):
        for m in range(1, N_DEV):
            px, py, pc = peer_of(m)
            for a in range(n):
                k = (N_DEV - 1) * a + m - 1
                slot = outs[a].at[4 * px + 2 * py + pc]
                cp = _remote(ins[a], slot, scr[0].at[k], scr[1].at[k], (px, py, pc))
                cp.wait_send()
                cp.wait_recv()

    slots = [jnp.zeros((N_DEV,) + a.shape, a.dtype) for a in arrs]
    out_shape = [jax.ShapeDtypeStruct(s.shape, s.dtype) for s in slots]
    return _Exchange(list(arrs) + slots, out_shape, {n + a: a for a in range(n)},
                     [pltpu.SemaphoreType.DMA(((N_DEV - 1) * n,))] * 2, start, finish)


def _div_tile(n, want):
    best = None
    for t in range(8, min(n, want) + 1, 8):
        if n % t == 0:
            best = t
    assert best is not None, n
    return best


def _cast_into_slab(name, w, place, dtype):
    r, cc = w.shape
    tr = r if r * cc <= 128 * 1024 else _div_tile(r, 256)

    def body(s_ref, w_ref, o_ref):
        o_ref[...] = w_ref[...].astype(o_ref.dtype)

    return pl.pallas_call(
        body, name=name,
        grid_spec=pltpu.PrefetchScalarGridSpec(
            num_scalar_prefetch=1, grid=(r // tr,),
            in_specs=[pl.BlockSpec((tr, cc), lambda i, s: (i, 0))],
            out_specs=pl.BlockSpec((None, tr, cc), lambda i, s: (s[0], i, 0))),
        out_shape=jax.ShapeDtypeStruct((N_CHIPS, r, cc), dtype), compiler_params=_cp("parallel"),
    )(place, w)


def _add_half(name, g, rcv, place):
    nq, r, cc = g.shape
    hr = r // 2

    def body(s_ref, g_ref, r_ref, o_ref):
        o_ref[...] = (g_ref[...] + r_ref[...]).astype(o_ref.dtype)

    spec = pl.BlockSpec((None, hr, cc), lambda i, s: (i, 0, 0))
    return pl.pallas_call(
        body, name=name,
        grid_spec=pltpu.PrefetchScalarGridSpec(
            num_scalar_prefetch=1, grid=(nq,),
            in_specs=[pl.BlockSpec((None, hr, cc), lambda i, s: (i, s[1], 0)), spec], out_specs=spec),
        out_shape=jax.ShapeDtypeStruct((nq, hr, cc), BF16), compiler_params=_cp("parallel"),
    )(place, g, rcv)


def _sum_owner(name, part, rcv, place):
    _, hr, cc = part.shape
    tr = _div_tile(hr, 128)
    nb = hr // tr

    def body(s_ref, p_ref, r_ref, o_ref):
        o_ref[...] = ((p_ref[...].astype(F32) + r_ref[0].astype(F32)) + r_ref[1].astype(F32)) + r_ref[2].astype(F32)

    return pl.pallas_call(
        body, name=name,
        grid_spec=pltpu.PrefetchScalarGridSpec(
            num_scalar_prefetch=1, grid=(nb,),
            in_specs=[pl.BlockSpec((None, tr, cc), lambda i, s: (s[0], i, 0)),
                      pl.BlockSpec((3, tr, cc), lambda i, s: (0, i, 0))],
            out_specs=pl.BlockSpec((tr, cc), lambda i, s: (s[1] * nb + i, 0))),
        out_shape=jax.ShapeDtypeStruct((2 * hr, cc), F32), compiler_params=_cp("parallel"),
    )(place, part, rcv)


def _sum_small(gathered, local, place):
    n = len(gathered)

    def body(s_ref, *refs):
        g_refs, l_refs, o_refs = refs[:n], refs[n:2 * n], refs[2 * n:]
        me = s_ref[2]
        for g_ref, l_ref, o_ref in zip(g_refs, l_refs, o_refs):
            acc = None
            for d in range(N_DEV):
                term = jnp.where(me == d, l_ref[...], g_ref[d])
                acc = term if acc is None else acc + term
            o_ref[...] = acc

    def whole(shape):
        return pl.BlockSpec(shape, lambda i, s, nd=len(shape): (0,) * nd)

    return pl.pallas_call(
        body, name="sum_small",
        grid_spec=pltpu.PrefetchScalarGridSpec(
            num_scalar_prefetch=1, grid=(1,),
            in_specs=[whole(g.shape) for g in gathered] + [whole(a.shape) for a in local],
            out_specs=tuple(whole(a.shape) for a in local)),
        out_shape=tuple(jax.ShapeDtypeStruct(a.shape, a.dtype) for a in local), compiler_params=_cp("arbitrary"),
    )(place, *gathered, *local)


def _adamw(name, w, g, m, v):
    r, cc = w.shape
    tr = r if r * cc <= 128 * 1024 else _div_tile(r, 256)

    def body(w_ref, g_ref, m_ref, v_ref, d_ref, mo_ref, vo_ref):
        gv = g_ref[...]
        mn = ADAM_B1 * m_ref[...] + (1.0 - ADAM_B1) * gv
        vn = ADAM_B2 * v_ref[...] + (1.0 - ADAM_B2) * (gv * gv)
        m_hat = mn / (1.0 - ADAM_B1 ** ADAM_STEP)
        v_hat = vn / (1.0 - ADAM_B2 ** ADAM_STEP)
        d_ref[...] = -ADAM_LR * (m_hat / (jnp.sqrt(v_hat) + ADAM_EPS) + ADAM_WD * w_ref[...])
        mo_ref[...] = mn
        vo_ref[...] = vn

    spec = pl.BlockSpec((tr, cc), lambda i: (i, 0))
    sd = jax.ShapeDtypeStruct((r, cc), F32)
    return pl.pallas_call(
        body, name=name, grid=(r // tr,), in_specs=[spec] * 4, out_specs=(spec,) * 3, out_shape=(sd,) * 3,
        compiler_params=_cp("parallel"),
    )(w, g, m, v)


_BIG = ("w_in", "w_up", "w_branch", "w_mem_kv", "w_out", "w_down")
_BIG_SHARD_SHAPE = {"w_in": (1024, 1664), "w_up": (1024, 1408), "w_branch": (1536, 256),
                    "w_mem_kv": (256, 1024), "w_out": (256, 1024), "w_down": (704, 1024)}
_SMALL_SHAPE = {"norm1_g": (1, D_MODEL), "ln_v_g": (1, GM_WIDTH), "ln_v_b": (1, GM_WIDTH),
                "w_spatial": (GM_GROUPS * GM_CHUNK, GM_CHUNK), "b_spatial": (GM_GROUPS, GM_CHUNK),
                "lb_logits": (2, HG_HEADS * HG_DIM), "hgrn_norm_g": (1, HG_DIM), "mem_norm_g": (1, D_MODEL),
                "norm2_g": (1, D_MODEL), "conv_w": (3, D_FF), "conv_b": (1, D_FF), "final_g": (1, D_MODEL)}
_PARAM_ORDER = ("norm1_g", "w_in", "ln_v_g", "ln_v_b", "w_spatial", "b_spatial", "lb_logits", "hgrn_norm_g",
                "mem_norm_g", "w_mem_kv", "w_branch", "w_out", "norm2_g", "w_up", "conv_w", "conv_b", "w_down",
                "final_g")


def _adamw_small(ws, gs, ms, vs):
    n = len(ws)

    def body(*refs):
        w_refs, g_refs, m_refs, v_refs = refs[:n], refs[n:2 * n], refs[2 * n:3 * n], refs[3 * n:4 * n]
        d_refs, mo_refs, vo_refs = refs[4 * n:5 * n], refs[5 * n:6 * n], refs[6 * n:]
        for k in range(n):
            gv = g_refs[k][...]
            mn = ADAM_B1 * m_refs[k][...] + (1.0 - ADAM_B1) * gv
            vn = ADAM_B2 * v_refs[k][...] + (1.0 - ADAM_B2) * (gv * gv)
            m_hat = mn / (1.0 - ADAM_B1 ** ADAM_STEP)
            v_hat = vn / (1.0 - ADAM_B2 ** ADAM_STEP)
            d_refs[k][...] = -ADAM_LR * (m_hat / (jnp.sqrt(v_hat) + ADAM_EPS) + ADAM_WD * w_refs[k][...])
            mo_refs[k][...] = mn
            vo_refs[k][...] = vn

    specs = [pl.BlockSpec(a.shape, lambda i: (0, 0)) for a in ws]
    shapes = tuple(jax.ShapeDtypeStruct(a.shape, F32) for a in ws)
    res = pl.pallas_call(
        body, name="adamw_small", grid=(1,), in_specs=specs * 4, out_specs=tuple(specs * 3), out_shape=shapes * 3,
        compiler_params=_cp("arbitrary"),
    )(*ws, *gs, *ms, *vs)
    return res[:n], res[n:2 * n], res[2 * n:]


class _Comm:
    _ROW_SHARDED = ("w_mem_kv", "w_out", "w_down")

    def __init__(self, slabs, place):
        self.slabs, self.place = slabs, place
        self.full, self.raw, self.parts, self.bufs, self.done = {}, {}, {}, {}, {}
        ex, deliver = self._gather(["w_in"])
        deliver(_run_exchanges("all_gather_w_in", [ex])[0])

    def w(self, name):
        a = self.full[name]
        if name in self._ROW_SHARDED:
            return a.reshape(-1, a.shape[-1])
        if name == "conv_w":
            return jnp.transpose(a, (1, 0, 2)).reshape(3, 1, D_FF)
        return a

    def grad(self, name, arr):
        self.raw[name] = arr.reshape((N_CHIPS,) + _BIG_SHARD_SHAPE[name])

    def carry(self, tag, call):
        plan = self._plan(tag)
        if not plan:
            return call(())
        out, carried = call([ex for ex, _ in plan])
        for (_, deliver), res in zip(plan, carried):
            deliver(res)
        return out

    def finish(self, small_exchange):
        ex, deliver = self._share(["w_out", "w_branch", "w_mem_kv", "w_in"])
        shared, small = _run_exchanges("share_and_gather_small", [ex, small_exchange])
        deliver(shared)
        return self.done, small

    def _plan(self, tag):
        if tag == "in_proj":
            return [self._gather(["w_branch", "w_out", "w_mem_kv", "w_down", "conv_w"])]
        if tag == "hgrn_fwd":
            return [self._gather(["w_up"])]
        if tag == "d_h2":
            return [self._to_sibling(["w_down", "w_up"])]
        if tag == "hgrn_bwd":
            return [self._to_owner(["w_down", "w_up"]), self._to_sibling(["w_out", "w_branch", "w_mem_kv"])]
        if tag == "g_w_in":
            return [self._to_owner(["w_out", "w_branch", "w_mem_kv"]), self._share(["w_down", "w_up"])]
        if tag == "d_h":
            return [self._to_sibling(["w_in"])]
        if tag == "norm1_bwd":
            return [self._to_owner(["w_in"])]
        return []

    def _gather(self, names):
        ex = _ex_all_gather([self.slabs[n] for n in names], [n != "conv_w" for n in names])
        return ex, lambda res: self.full.update(zip(names, res))

    def _to_sibling(self, names):
        def deliver(res):
            for n, r in zip(names, res):
                self.parts[n] = _add_half("rs_add_" + n, self.raw[n], r, self.place)

        return _ex_to_sibling([self.raw[n] for n in names]), deliver

    def _to_owner(self, names):
        def deliver(res):
            for n, r in zip(names, res):
                self.bufs[n] = _sum_owner("rs_sum_" + n, self.parts[n], r, self.place)

        return _ex_to_owner([self.parts[n] for n in names]), deliver

    def _share(self, names):
        return _ex_share_halves([self.bufs[n] for n in names]), lambda res: self.done.update(zip(names, res))


def kernel(x, mem, norm1_g, w_in, ln_v_g, ln_v_b, w_spatial, b_spatial, lb_logits, hgrn_norm_g, mem_norm_g, w_mem_kv, w_branch, w_out, norm2_g, w_up, conv_w, conv_b, w_down, final_g, loss_target, m_norm1_g, m_w_in, m_ln_v_g, m_ln_v_b, m_w_spatial, m_b_spatial, m_lb_logits, m_hgrn_norm_g, m_mem_norm_g, m_w_mem_kv, m_w_branch, m_w_out, m_norm2_g, m_w_up, m_conv_w, m_conv_b, m_w_down, m_final_g, v_norm1_g, v_w_in, v_ln_v_g, v_ln_v_b, v_w_spatial, v_b_spatial, v_lb_logits, v_hgrn_norm_g, v_mem_norm_g, v_w_mem_kv, v_w_branch, v_w_out, v_norm2_g, v_w_up, v_conv_w, v_conv_b, v_w_down, v_final_g):
    w = dict(norm1_g=norm1_g, w_in=w_in, ln_v_g=ln_v_g, ln_v_b=ln_v_b, w_spatial=w_spatial, b_spatial=b_spatial,
             lb_logits=lb_logits, hgrn_norm_g=hgrn_norm_g, mem_norm_g=mem_norm_g, w_mem_kv=w_mem_kv,
             w_branch=w_branch, w_out=w_out, norm2_g=norm2_g, w_up=w_up, conv_w=conv_w, conv_b=conv_b,
             w_down=w_down, final_g=final_g)
    mom = dict(norm1_g=m_norm1_g, w_in=m_w_in, ln_v_g=m_ln_v_g, ln_v_b=m_ln_v_b, w_spatial=m_w_spatial,
               b_spatial=m_b_spatial, lb_logits=m_lb_logits, hgrn_norm_g=m_hgrn_norm_g, mem_norm_g=m_mem_norm_g,
               w_mem_kv=m_w_mem_kv, w_branch=m_w_branch, w_out=m_w_out, norm2_g=m_norm2_g, w_up=m_w_up,
               conv_w=m_conv_w, conv_b=m_conv_b, w_down=m_w_down, final_g=m_final_g)
    var = dict(norm1_g=v_norm1_g, w_in=v_w_in, ln_v_g=v_ln_v_g, ln_v_b=v_ln_v_b, w_spatial=v_w_spatial,
               b_spatial=v_b_spatial, lb_logits=v_lb_logits, hgrn_norm_g=v_hgrn_norm_g, mem_norm_g=v_mem_norm_g,
               w_mem_kv=v_w_mem_kv, w_branch=v_w_branch, w_out=v_w_out, norm2_g=v_norm2_g, w_up=v_w_up,
               conv_w=v_conv_w, conv_b=v_conv_b, w_down=v_w_down, final_g=v_final_g)
    B, S, D = x.shape
    T = B * S
    ci = lax.axis_index("c")
    q = 2 * lax.axis_index("x") + lax.axis_index("y")
    place = jnp.stack([q, ci, 2 * q + ci]).astype(jnp.int32)

    slabs = {n: _cast_into_slab("slab_" + n, w[n].reshape(_BIG_SHARD_SHAPE[n]), place, BF16) for n in _BIG}
    slabs["conv_w"] = _cast_into_slab("slab_conv_w", conv_w[0], place, F32)
    comm = _Comm(slabs, place)
    p = dict(
        norm1_g=norm1_g, ln_v_g=ln_v_g, ln_v_b=ln_v_b, w_spatial=w_spatial[0],
        b_spatial=b_spatial.reshape(GM_GROUPS, GM_CHUNK, 1), lb_logits=lb_logits, hgrn_norm_g=hgrn_norm_g,
        mem_norm_g=mem_norm_g, norm2_g=norm2_g, conv_b=conv_b, final_g=final_g.reshape(1, D))

    loss, grad_x, g = _local_step(x.reshape(T, D), mem.reshape(B * MEM_LEN, D), loss_target.reshape(T, D), p, comm,
                                  B, S)

    small_names = list(_SMALL_SHAPE)
    local_small = [g[n].reshape(_SMALL_SHAPE[n]) for n in small_names] + [loss]
    shard_grads, everyone = comm.finish(_ex_gather_small(local_small))
    summed = _sum_small(everyone, local_small, place)
    total = dict(zip(small_names, summed[:-1]))
    loss_total = summed[-1][0, 0]

    grads, delta, new_m, new_v = {}, {}, {}, {}
    for n in _BIG:
        shp = _BIG_SHARD_SHAPE[n]
        grads[n] = shard_grads[n]
        delta[n], new_m[n], new_v[n] = _adamw("adamw_" + n, w[n].reshape(shp), shard_grads[n],
                                              mom[n].reshape(shp), var[n].reshape(shp))
    cw_shard = D_FF // N_CHIPS
    total["conv_w"] = lax.dynamic_slice(total["conv_w"], (0, q * cw_shard), (3, cw_shard))

    def flat2d(d, n):
        return d[n].reshape(total[n].shape)

    upd = _adamw_small([flat2d(w, n) for n in small_names], [total[n] for n in small_names],
                       [flat2d(mom, n) for n in small_names], [flat2d(var, n) for n in small_names])
    for k, n in enumerate(small_names):
        grads[n], delta[n], new_m[n], new_v[n] = total[n], upd[0][k], upd[1][k], upd[2][k]

    def shaped(d):
        return [d[n].reshape(w[n].shape) for n in _PARAM_ORDER]

    return (loss_total, grad_x.reshape(B, S, D), *shaped(grads), *shaped(delta), *shaped(new_m), *shaped(new_v))
```

```python
import functools
import math

import jax
import jax.numpy as jnp
from jax import lax
from jax.experimental import pallas as pl
from jax.experimental.pallas import tpu as pltpu

F32 = jnp.float32
BF16 = jnp.bfloat16
EPS = 1e-6

D_MODEL = 1024
MEM_LEN = 256
GM_WIDTH = 512
GM_CHUNK = 128
GM_GROUPS = 4
HG_HEADS = 4
HG_DIM = 128
HG_CHUNK = 64
XA_HEADS = 4
XA_DIM = 128
BR_WIDTH = 512
D_FF = 2816
IN_WIDTH = 6656
N_CHIPS = 4
N_DEV = 8

ADAM_LR = 0.001
ADAM_B1 = 0.9
ADAM_B2 = 0.999
ADAM_EPS = 1e-08
ADAM_WD = 0.01
ADAM_STEP = 10

COL_ZU, COL_ZV, COL_HQ, COL_HF, COL_HI, COL_HG, COL_XQ = 0, 1, 2, 3, 4, 5, 6
COL_GATE0 = 3584

VMEM_LIMIT_BYTES = 48 * 1024 * 1024
MESH_ID = pl.DeviceIdType.MESH


def _cp(*sem):
    return pltpu.CompilerParams(dimension_semantics=sem, vmem_limit_bytes=VMEM_LIMIT_BYTES)


def _dot(a, b):
    return lax.dot_general(a.astype(BF16), b.astype(BF16), (((1,), (0,)), ((), ())), preferred_element_type=F32)


def _dot_nt(a, b):
    return lax.dot_general(a.astype(BF16), b.astype(BF16), (((1,), (1,)), ((), ())), preferred_element_type=F32)


def _dot_tn(a, b):
    return lax.dot_general(a.astype(BF16), b.astype(BF16), (((0,), (0,)), ((), ())), preferred_element_type=F32)


def _split2(x):
    hi = x.astype(BF16)
    return hi, (x - hi.astype(F32)).astype(BF16)


def _dot3(a, b, dims):
    ah, al = _split2(a)
    bh, bl = _split2(b)
    dn = (dims, ((), ()))
    return (lax.dot_general(ah, bh, dn, preferred_element_type=F32)
            + lax.dot_general(ah, bl, dn, preferred_element_type=F32)
            + lax.dot_general(al, bh, dn, preferred_element_type=F32))


def _dot_01(mask01, x):
    hi = x.astype(BF16)
    r1 = x - hi.astype(F32)
    mid = r1.astype(BF16)
    lo = (r1 - mid.astype(F32)).astype(BF16)
    m = mask01.astype(BF16)
    dn = (((1,), (0,)), ((), ()))
    return (lax.dot_general(m, hi, dn, preferred_element_type=F32)
            + lax.dot_general(m, mid, dn, preferred_element_type=F32)
            + lax.dot_general(m, lo, dn, preferred_element_type=F32))


def _sigmoid(z):
    return 1.0 / (1.0 + jnp.exp(-z))


_GELU_C = math.sqrt(2.0 / math.pi)


def _gelu_and_grad(z):
    inner = _GELU_C * (z + 0.044715 * z * z * z)
    t = jnp.tanh(inner)
    val = 0.5 * z * (1.0 + t)
    grad = 0.5 * (1.0 + t) + 0.5 * z * (1.0 - t * t) * _GELU_C * (1.0 + 3.0 * 0.044715 * z * z)
    return val, grad


def _row_tile(n, want=512):
    t = min(want, n)
    assert n % t == 0
    return t


def _pcall(body, operands, *, name, grid, in_specs, out_specs, out_shape, scratch_shapes=(), semantics, riders=()):
    single = not isinstance(out_shape, (tuple, list))
    out_specs = (out_specs,) if single else tuple(out_specs)
    out_shape = (out_shape,) if single else tuple(out_shape)
    if not riders:
        res = pl.pallas_call(body, name=name, grid=grid, in_specs=list(in_specs), out_specs=out_specs,
                             out_shape=out_shape, scratch_shapes=list(scratch_shapes),
                             compiler_params=_cp(*semantics))(*operands)
        return (res[0] if single else res), []
    n_in, n_out, n_scr = len(in_specs), len(out_shape), len(scratch_shapes)
    ex_in = [len(ex.operands) for ex in riders]
    ex_out = [len(ex.out_shape) for ex in riders]
    ex_scr = [len(ex.scratch) for ex in riders]
    tot_in, tot_out = n_in + sum(ex_in), n_out + sum(ex_out)

    def wrapped(*refs):
        ins, outs, scr = refs[:tot_in], refs[tot_in:tot_in + tot_out], refs[tot_in + tot_out:]
        ids = [pl.program_id(d) for d in range(len(grid))]
        first = functools.reduce(lambda p, t: p & t, [i == 0 for i in ids])
        last = functools.reduce(lambda p, t: p & t, [i == n - 1 for i, n in zip(ids, grid)])
        parts, oi, oo, os_ = [], n_in, n_out, n_scr
        for k in range(len(riders)):
            parts.append((ins[oi:oi + ex_in[k]], outs[oo:oo + ex_out[k]], scr[os_:os_ + ex_scr[k]]))
            oi, oo, os_ = oi + ex_in[k], oo + ex_out[k], os_ + ex_scr[k]

        @pl.when(first)
        def _():
            for ex, part in zip(riders, parts):
                ex.start(*part)

        body(*ins[:n_in], *outs[:n_out], *scr[:n_scr])

        @pl.when(last)
        def _():
            for ex, part in zip(riders, parts):
                ex.finish(*part)

    aliases, oi, oo = {}, n_in, n_out
    all_ops, all_shapes, all_scr = list(operands), list(out_shape), list(scratch_shapes)
    for k, ex in enumerate(riders):
        aliases.update({oi + a: oo + b for a, b in ex.aliases.items()})
        oi, oo = oi + ex_in[k], oo + ex_out[k]
        all_ops += list(ex.operands)
        all_shapes += list(ex.out_shape)
        all_scr += list(ex.scratch)
    res = pl.pallas_call(
        wrapped, name=name, grid=grid, in_specs=list(in_specs) + [ANY] * sum(ex_in),
        out_specs=out_specs + (ANY,) * sum(ex_out), out_shape=tuple(all_shapes), scratch_shapes=all_scr,
        input_output_aliases=aliases, compiler_params=_cp(*(["arbitrary"] * len(grid))))(*all_ops)
    own = res[0] if single else tuple(res[:n_out])
    carried, oo = [], n_out
    for k in range(len(riders)):
        carried.append(list(res[oo:oo + ex_out[k]]))
        oo += ex_out[k]
    return own, carried


def _matmul(name, operands, *, grid, in_specs, o_spec, out_shape, out_dtype, dims, has_res=False, riders=()):
    nk = grid[2]
    assert nk == 1 or (out_dtype == F32 and not has_res)

    def body(*refs):
        if has_res:
            a_ref, b_ref, r_ref, o_ref = refs
        else:
            a_ref, b_ref, o_ref = refs
            r_ref = None
        part = lax.dot_general(a_ref[...].astype(BF16), b_ref[...].astype(BF16), (dims, ((), ())),
                               preferred_element_type=F32)
        if nk == 1:
            if r_ref is not None:
                part = part + r_ref[...]
            o_ref[...] = part.astype(o_ref.dtype)
        else:
            k = pl.program_id(2)

            @pl.when(k == 0)
            def _():
                o_ref[...] = part

            @pl.when(k > 0)
            def _():
                o_ref[...] += part

    out, carried = _pcall(body, operands, name=name, grid=grid, in_specs=in_specs, out_specs=o_spec,
                          out_shape=jax.ShapeDtypeStruct(out_shape, out_dtype),
                          semantics=("parallel", "parallel", "arbitrary"), riders=riders)
    return (out, carried) if riders else out


NN = ((1,), (0,))
NT = ((1,), (1,))
TN = ((0,), (0,))


def _mm_cs(name, a, w, out_dtype, riders=()):
    M, K = a.shape
    nq, _, wd = w.shape
    tm = _row_tile(M)
    return _matmul(name, (a, w), grid=(nq, M // tm, 1),
                   in_specs=[pl.BlockSpec((tm, K), lambda j, i, k: (i, 0)),
                             pl.BlockSpec((None, K, wd), lambda j, i, k: (j, 0, 0))],
                   o_spec=pl.BlockSpec((tm, wd), lambda j, i, k: (i, j)),
                   out_shape=(M, nq * wd), out_dtype=out_dtype, dims=NN, riders=riders)


def _mm_rs(name, a, w, out_dtype, res=None, tn=512):
    M, K = a.shape
    N = w.shape[1]
    tm = _row_tile(M)
    tn = min(tn, N)
    ops = (a, w) if res is None else (a, w, res)
    in_specs = [pl.BlockSpec((tm, K), lambda i, j, k: (i, 0)),
                pl.BlockSpec((K, tn), lambda i, j, k: (0, j))]
    if res is not None:
        in_specs.append(pl.BlockSpec((tm, tn), lambda i, j, k: (i, j)))
    return _matmul(name, ops, grid=(M // tm, N // tn, 1), in_specs=in_specs,
                   o_spec=pl.BlockSpec((tm, tn), lambda i, j, k: (i, j)),
                   out_shape=(M, N), out_dtype=out_dtype, dims=NN, has_res=res is not None)


def _mm_nt_rs(name, g, w, out_dtype, to):
    M, N = g.shape
    K = w.shape[0]
    tm = _row_tile(M)
    return _matmul(name, (g, w), grid=(M // tm, K // to, 1),
                   in_specs=[pl.BlockSpec((tm, N), lambda i, j, k: (i, 0)),
                             pl.BlockSpec((to, N), lambda i, j, k: (j, 0))],
                   o_spec=pl.BlockSpec((tm, to), lambda i, j, k: (i, j)),
                   out_shape=(M, K), out_dtype=out_dtype, dims=NT)


def _mm_nt_cs(name, g, w, out_dtype, riders=(), stacked=False):
    M = g.shape[-2]
    nq, K, wd = w.shape
    tm = _row_tile(M, 256)

    def body(g_ref, w_ref, o_ref):
        acc = None
        for q in range(nq):
            gq = g_ref[q // 2, :, (q % 2) * wd:(q % 2 + 1) * wd] if stacked else g_ref[:, q * wd:(q + 1) * wd]
            part = _dot_nt(gq, w_ref[q])
            acc = part if acc is None else acc + part
        o_ref[...] = acc.astype(o_ref.dtype)

    g_spec = (pl.BlockSpec((2, tm, 2 * wd), lambda i: (0, i, 0)) if stacked
              else pl.BlockSpec((tm, nq * wd), lambda i: (i, 0)))
    out, carried = _pcall(
        body, (g, w), name=name, grid=(M // tm,),
        in_specs=[g_spec, pl.BlockSpec((nq, K, wd), lambda i: (0, 0, 0))],
        out_specs=pl.BlockSpec((tm, K), lambda i: (i, 0)),
        out_shape=jax.ShapeDtypeStruct((M, K), out_dtype), semantics=("parallel",), riders=riders)
    return (out, carried) if riders else out


def _mm_tn_rs(name, a, g, to, tn=512):
    T, M = a.shape
    N = g.shape[1]
    tt = _row_tile(T, 1024)
    tn = min(tn, N)
    return _matmul(name, (a, g), grid=(M // to, N // tn, T // tt),
                   in_specs=[pl.BlockSpec((tt, to), lambda i, j, k: (k, i)),
                             pl.BlockSpec((tt, tn), lambda i, j, k: (k, j))],
                   o_spec=pl.BlockSpec((to, tn), lambda i, j, k: (i, j)),
                   out_shape=(M, N), out_dtype=F32, dims=TN)


def _mm_tn_cs(name, a, g, nq, to, riders=(), stacked=False):
    T, M = a.shape
    wd = g.shape[-1] * (2 if stacked else 1) // nq
    tt = _row_tile(T, 1024)
    g_spec = (pl.BlockSpec((None, tt, wd), lambda i, j, k: (j // 2, k, j % 2)) if stacked
              else pl.BlockSpec((tt, wd), lambda i, j, k: (k, j)))
    return _matmul(name, (a, g), grid=(M // to, nq, T // tt),
                   in_specs=[pl.BlockSpec((tt, to), lambda i, j, k: (k, i)), g_spec],
                   o_spec=pl.BlockSpec((None, to, wd), lambda i, j, k: (j, i, 0)),
                   out_shape=(nq, M, wd), out_dtype=F32, dims=TN, riders=riders)


def _rms_fwd(name, x, g):
    T, D = x.shape
    tm = _row_tile(T)

    def body(x_ref, g_ref, o_ref):
        xv = x_ref[...]
        r = lax.rsqrt(jnp.mean(xv * xv, axis=-1, keepdims=True) + EPS)
        o_ref[...] = (xv * r * g_ref[...]).astype(o_ref.dtype)

    return pl.pallas_call(
        body, name=name, grid=(T // tm,),
        in_specs=[pl.BlockSpec((tm, D), lambda i: (i, 0)), pl.BlockSpec((1, D), lambda i: (0, 0))],
        out_specs=pl.BlockSpec((tm, D), lambda i: (i, 0)),
        out_shape=jax.ShapeDtypeStruct((T, D), BF16), compiler_params=_cp("parallel"),
    )(x, g)


def _rms_bwd(name, x, g, dh, dres, riders=()):
    T, D = x.shape
    tm = _row_tile(T)
    has_res = dres is not None

    def body(*refs):
        if has_res:
            x_ref, g_ref, dh_ref, dr_ref, dx_ref, dg_ref = refs
        else:
            x_ref, g_ref, dh_ref, dx_ref, dg_ref = refs

        @pl.when(pl.program_id(0) == 0)
        def _():
            dg_ref[...] = jnp.zeros_like(dg_ref)

        xv = x_ref[...]
        r = lax.rsqrt(jnp.mean(xv * xv, axis=-1, keepdims=True) + EPS)
        n = xv * r
        dhv = dh_ref[...]
        dg_ref[...] += jnp.sum(dhv * n, axis=0, keepdims=True)
        dn = dhv * g_ref[...]
        dx = r * (dn - n * jnp.mean(dn * n, axis=-1, keepdims=True))
        if has_res:
            dx = dx + dr_ref[...]
        dx_ref[...] = dx

    row = pl.BlockSpec((tm, D), lambda i: (i, 0))
    vec = pl.BlockSpec((1, D), lambda i: (0, 0))
    ops = (x, g, dh, dres) if has_res else (x, g, dh)
    out, carried = _pcall(
        body, ops, name=name, grid=(T // tm,),
        in_specs=[row, vec, row] + ([row] if has_res else []),
        out_specs=(row, vec),
        out_shape=(jax.ShapeDtypeStruct((T, D), F32), jax.ShapeDtypeStruct((1, D), F32)),
        semantics=("arbitrary",), riders=riders)
    return (out, carried) if riders else out


def _loss_head(x2, tgt, g):
    T, D = x2.shape
    tm = _row_tile(T)

    def body(x_ref, t_ref, g_ref, dx_ref, dg_ref, loss_ref):
        @pl.when(pl.program_id(0) == 0)
        def _():
            dg_ref[...] = jnp.zeros_like(dg_ref)
            loss_ref[...] = jnp.zeros_like(loss_ref)

        xv = x_ref[...]
        gv = g_ref[...]
        r = lax.rsqrt(jnp.mean(xv * xv, axis=-1, keepdims=True) + EPS)
        n = xv * r
        diff = n * gv - t_ref[...]
        loss_ref[...] += 0.5 * jnp.sum(jnp.mean(diff * diff, axis=-1, keepdims=True))
        dy = diff * (1.0 / D)
        dg_ref[...] += jnp.sum(dy * n, axis=0, keepdims=True)
        dn = dy * gv
        dx_ref[...] = r * (dn - n * jnp.mean(dn * n, axis=-1, keepdims=True))

    row = pl.BlockSpec((tm, D), lambda i: (i, 0))
    vec = pl.BlockSpec((1, D), lambda i: (0, 0))
    return pl.pallas_call(
        body, name="loss_head", grid=(T // tm,),
        in_specs=[row, row, vec],
        out_specs=(row, vec, pl.BlockSpec((8, 128), lambda i: (0, 0))),
        out_shape=(jax.ShapeDtypeStruct((T, D), F32), jax.ShapeDtypeStruct((1, D), F32),
                   jax.ShapeDtypeStruct((8, 128), F32)),
        compiler_params=_cp("arbitrary"),
    )(x2, tgt, g)


def _gmlp_pieces(zu, zv, lng, lnb, ws_ref, bs_ref):
    u, du = _gelu_and_grad(zu)
    v, dv = _gelu_and_grad(zv)
    mu = jnp.mean(v, axis=-1, keepdims=True)
    vc = v - mu
    rstd = lax.rsqrt(jnp.mean(vc * vc, axis=-1, keepdims=True) + EPS)
    vhat = vc * rstd
    vn = vhat * lng + lnb
    row = lax.broadcasted_iota(jnp.int32, (GM_CHUNK, GM_CHUNK), 0)
    col = lax.broadcasted_iota(jnp.int32, (GM_CHUNK, GM_CHUNK), 1)
    tril = row >= col
    wms, mixed = [], []
    for g in range(GM_GROUPS):
        sl = slice(g * 128, (g + 1) * 128)
        wm = jnp.where(tril, ws_ref[g], 0.0)
        wms.append(wm)
        mixed.append(_dot(wm, vn[:, sl]) + bs_ref[g])
    return u, du, dv, rstd, vhat, vn, wms, mixed, tril


def _gmlp_fwd(proj, lng, lnb, ws, bs_col):
    T = proj.shape[0]
    n = T // GM_CHUNK

    def body(zu_ref, zv_ref, lng_ref, lnb_ref, ws_ref, bs_ref, o_ref):
        u, _, _, _, _, _, _, mixed, _ = _gmlp_pieces(zu_ref[...], zv_ref[...], lng_ref[...], lnb_ref[...],
                                                     ws_ref, bs_ref)
        for g in range(GM_GROUPS):
            sl = slice(g * 128, (g + 1) * 128)
            o_ref[:, sl] = (u[:, sl] * mixed[g]).astype(o_ref.dtype)

    vec = pl.BlockSpec((1, GM_WIDTH), lambda i: (0, 0))
    return pl.pallas_call(
        body, name="gmlp_fwd", grid=(n,),
        in_specs=[pl.BlockSpec((GM_CHUNK, 512), lambda i: (i, COL_ZU)),
                  pl.BlockSpec((GM_CHUNK, 512), lambda i: (i, COL_ZV)),
                  vec, vec,
                  pl.BlockSpec((GM_GROUPS, 128, 128), lambda i: (0, 0, 0)),
                  pl.BlockSpec((GM_GROUPS, 128, 1), lambda i: (0, 0, 0))],
        out_specs=pl.BlockSpec((GM_CHUNK, 512), lambda i: (i, 0)),
        out_shape=jax.ShapeDtypeStruct((T, GM_WIDTH), BF16), compiler_params=_cp("parallel"),
    )(proj, proj, lng, lnb, ws, bs_col)


def _gmlp_bwd(proj, d_out, lng, lnb, ws, bs_col):
    T = proj.shape[0]
    n = T // GM_CHUNK

    def body(zu_ref, zv_ref, do_ref, lng_ref, lnb_ref, ws_ref, bs_ref,
             dzu_ref, dzv_ref, dws_ref, dbs_ref, dlng_ref, dlnb_ref, dm_acc):
        i = pl.program_id(0)

        @pl.when(i == 0)
        def _():
            dws_ref[...] = jnp.zeros_like(dws_ref)
            dlng_ref[...] = jnp.zeros_like(dlng_ref)
            dlnb_ref[...] = jnp.zeros_like(dlnb_ref)
            dm_acc[...] = jnp.zeros_like(dm_acc)

        lng_v = lng_ref[...]
        u, du, dv, rstd, vhat, vn, wms, mixed, tril = _gmlp_pieces(zu_ref[...], zv_ref[...], lng_v, lnb_ref[...],
                                                                  ws_ref, bs_ref)
        do = do_ref[...]
        dvn_parts = []
        for g in range(GM_GROUPS):
            sl = slice(g * 128, (g + 1) * 128)
            dog = do[:, sl]
            dzu_ref[:, sl] = (dog * mixed[g] * du[:, sl]).astype(dzu_ref.dtype)
            dmix = dog * u[:, sl]
            dm_acc[:, sl] += dmix
            dws_ref[g] += jnp.where(tril, _dot_nt(dmix, vn[:, sl]), 0.0)
            dvn_parts.append(_dot_tn(wms[g], dmix))
        dvn = jnp.concatenate(dvn_parts, axis=1)
        dlng_ref[...] += jnp.sum(dvn * vhat, axis=0, keepdims=True)
        dlnb_ref[...] += jnp.sum(dvn, axis=0, keepdims=True)
        dvh = dvn * lng_v
        dvv = rstd * (dvh - jnp.mean(dvh, axis=-1, keepdims=True)
                      - vhat * jnp.mean(dvh * vhat, axis=-1, keepdims=True))
        dzv_ref[...] = (dvv * dv).astype(dzv_ref.dtype)

        @pl.when(i == n - 1)
        def _():
            for g in range(GM_GROUPS):
                dbs_ref[g] = jnp.sum(dm_acc[:, g * 128:(g + 1) * 128], axis=1, keepdims=True)

    vec = pl.BlockSpec((1, GM_WIDTH), lambda i: (0, 0))
    wsp = pl.BlockSpec((GM_GROUPS, 128, 128), lambda i: (0, 0, 0))
    bsp = pl.BlockSpec((GM_GROUPS, 128, 1), lambda i: (0, 0, 0))
    tile = pl.BlockSpec((GM_CHUNK, 512), lambda i: (i, 0))
    return pl.pallas_call(
        body, name="gmlp_bwd", grid=(n,),
        in_specs=[pl.BlockSpec((GM_CHUNK, 512), lambda i: (i, COL_ZU)),
                  pl.BlockSpec((GM_CHUNK, 512), lambda i: (i, COL_ZV)),
                  pl.BlockSpec((None, GM_CHUNK, 512), lambda i: (0, i, 0)), vec, vec, wsp, bsp],
        out_specs=(tile, tile, wsp, bsp, vec, vec),
        out_shape=(jax.ShapeDtypeStruct((T, GM_WIDTH), BF16), jax.ShapeDtypeStruct((T, GM_WIDTH), BF16),
                   jax.ShapeDtypeStruct((GM_GROUPS, 128, 128), F32), jax.ShapeDtypeStruct((GM_GROUPS, 128, 1), F32),
                   jax.ShapeDtypeStruct((1, GM_WIDTH), F32), jax.ShapeDtypeStruct((1, GM_WIDTH), F32)),
        scratch_shapes=[pltpu.VMEM((GM_CHUNK, GM_WIDTH), F32)],
        compiler_params=_cp("arbitrary"),
    )(proj, proj, d_out, lng, lnb, ws, bs_col)


def _hgrn_lower_bound(lbl):
    return 1.0 / (1.0 + jnp.exp(lbl[1:2, :] - lbl[0:1, :]))


def _hgrn_gates(hq, hf, lb):
    C = HG_CHUNK
    sg = _sigmoid(hf)
    fg = lb + (1.0 - lb) * sg
    sq = _sigmoid(hq)
    row = lax.broadcasted_iota(jnp.int32, (C, C), 0)
    col = lax.broadcasted_iota(jnp.int32, (C, C), 1)
    tril = row >= col
    logf = jnp.log(fg)
    a = _dot_01(tril, logf)
    a_last = jnp.sum(logf, axis=0, keepdims=True)
    first_half = lax.broadcasted_iota(jnp.int32, logf.shape, 0) < (C // 2)
    a_mid = jnp.sum(jnp.where(first_half, logf, 0.0), axis=0, keepdims=True)
    ea, ei, eki, ekl = jnp.exp(a), jnp.exp(a - a_mid), jnp.exp(a_mid - a), jnp.exp(a_last - a)
    k = 1.0 - fg
    q = hq * sq
    return dict(sg=sg, fg=fg, sq=sq, tril=tril, ea=ea, ei=ei, eki=eki, ekl=ekl, e_last=jnp.exp(a_last),
                qe=q * ea, qi=q * ei, ki=k * eki, kl=k * ekl)


def _heads(x):
    return [x[:, h * HG_DIM:(h + 1) * HG_DIM] for h in range(HG_HEADS)]


def _hgrn_fwd(proj, lbl, gh, B, S, riders=()):
    C = HG_CHUNK
    NC = S // C
    W = HG_HEADS * HG_DIM

    def body(q_ref, f_ref, i_ref, g_ref, lbl_ref, gh_ref, o_ref, bo_ref, st_ref, state):
        @pl.when(pl.program_id(0) == 0)
        def _():
            state[...] = jnp.zeros_like(state)

        lb = _hgrn_lower_bound(lbl_ref[...])
        ghv = gh_ref[...]
        for b in range(B):
            gt = _hgrn_gates(q_ref[b], f_ref[b], lb)
            v = _heads(i_ref[b])
            qe, qi, ki, kl, e_last = (_heads(gt[n]) for n in ("qe", "qi", "ki", "kl", "e_last"))
            outs, normed = [], []
            for h in range(HG_HEADS):
                p = jnp.where(gt["tril"], _dot_nt(qi[h], ki[h]), 0.0)
                st = state[b, h]
                st_ref[b, h] = st
                o = _dot_nt(qe[h], st) + _dot(p, v[h])
                state[b, h] = st * e_last[h] + _dot_tn(v[h], kl[h])
                outs.append(o)
                normed.append(o * lax.rsqrt(jnp.mean(o * o, axis=-1, keepdims=True) + EPS) * ghv)
            o_ref[b] = jnp.concatenate(outs, axis=1)
            hg = g_ref[b]
            bo_ref[b] = (jnp.concatenate(normed, axis=1) * (hg * _sigmoid(hg))).astype(bo_ref.dtype)

    def col(cb):
        return pl.BlockSpec((B, C, 512), lambda c: (0, c, cb))

    tile = pl.BlockSpec((B, C, W), lambda c: (0, c, 0))
    proj3 = proj.reshape(B, S, proj.shape[-1])
    out, carried = _pcall(
        body, (proj3, proj3, proj3, proj3, lbl, gh), name="hgrn_fwd", grid=(NC,),
        in_specs=[col(COL_HQ), col(COL_HF), col(COL_HI), col(COL_HG),
                  pl.BlockSpec((2, W), lambda c: (0, 0)), pl.BlockSpec((1, HG_DIM), lambda c: (0, 0))],
        out_specs=(tile, tile, pl.BlockSpec((B, None, HG_HEADS, 128, 128), lambda c: (0, c, 0, 0, 0))),
        out_shape=(jax.ShapeDtypeStruct((B, S, W), F32), jax.ShapeDtypeStruct((B, S, W), BF16),
                   jax.ShapeDtypeStruct((B, NC, HG_HEADS, 128, 128), F32)),
        scratch_shapes=[pltpu.VMEM((B, HG_HEADS, 128, 128), F32)],
        semantics=("arbitrary",), riders=riders)
    o_h, b_out, states = out
    out = (o_h, b_out.reshape(B * S, W), states)
    return (out, carried) if riders else out


def _hgrn_bwd(proj, o_saved, states, d_out, lbl, gh, B, S, riders=()):
    C = HG_CHUNK
    NC = S // C
    W = HG_HEADS * HG_DIM

    def body(q_ref, f_ref, i_ref, g_ref, o_ref, st_ref, do_ref, lbl_ref, gh_ref,
             dq_ref, df_ref, di_ref, dg_ref, dlbl_ref, dgh_ref, dstate, dlb_acc):
        c = pl.program_id(0)

        @pl.when(c == 0)
        def _():
            dstate[...] = jnp.zeros_like(dstate)
            dgh_ref[...] = jnp.zeros_like(dgh_ref)
            dlb_acc[...] = jnp.zeros_like(dlb_acc)

        lb = _hgrn_lower_bound(lbl_ref[...])
        ghv = gh_ref[...]
        row = lax.broadcasted_iota(jnp.int32, (C, C), 0)
        colm = lax.broadcasted_iota(jnp.int32, (C, C), 1)
        triu = colm >= row
        for b in range(B):
            hq, hg = q_ref[b], g_ref[b]
            gt = _hgrn_gates(hq, f_ref[b], lb)
            tril = gt["tril"]
            v = _heads(i_ref[b])
            qe, qi, ki, kl, e_last = (_heads(gt[n]) for n in ("qe", "qi", "ki", "kl", "e_last"))
            sgg = _sigmoid(hg)
            don_all = do_ref[b] * (hg * sgg)
            o, don = _heads(o_ref[b]), _heads(don_all)
            d_qe, d_qi, d_ki, d_kl, dv, n_all, dal = [], [], [], [], [], [], []
            for h in range(HG_HEADS):
                r = lax.rsqrt(jnp.mean(o[h] * o[h], axis=-1, keepdims=True) + EPS)
                n = o[h] * r
                n_all.append(n)
                dgh_ref[...] += jnp.sum(don[h] * n, axis=0, keepdims=True)
                dn = don[h] * ghv
                d_o = r * (dn - n * jnp.mean(dn * n, axis=-1, keepdims=True))
                st, dst = st_ref[b, h], dstate[b, h]
                p = jnp.where(tril, _dot_nt(qi[h], ki[h]), 0.0)
                dp = jnp.where(tril, _dot3(d_o, v[h], NT), 0.0)
                d_qe.append(_dot3(d_o, st, NN))
                d_qi.append(_dot3(dp, ki[h], NN))
                d_ki.append(_dot3(dp, qi[h], TN))
                d_kl.append(_dot3(v[h], dst, NN))
                dv.append(_dot_tn(p, d_o) + _dot_nt(kl[h], dst))
                dstate[b, h] = dst * e_last[h] + _dot3(d_o, qe[h], TN)
                dal.append(jnp.sum(dst * st, axis=0, keepdims=True) * e_last[h])
            d_qe, d_qi, d_ki, d_kl, n_all, dal = (jnp.concatenate(t, axis=1)
                                                  for t in (d_qe, d_qi, d_ki, d_kl, n_all, dal))
            dg_ref[b] = (do_ref[b] * n_all * jnp.tile(ghv, (1, HG_HEADS))
                         * (sgg * (1.0 + hg * (1.0 - sgg)))).astype(dg_ref.dtype)
            di_ref[b] = jnp.concatenate(dv, axis=1).astype(di_ref.dtype)
            d_a_last = dal + jnp.sum(d_kl * gt["kl"], axis=0, keepdims=True)
            dq = d_qe * gt["ea"] + d_qi * gt["ei"]
            dk = d_ki * gt["eki"] + d_kl * gt["ekl"]
            da = d_qe * gt["qe"] + d_qi * gt["qi"] - d_ki * gt["ki"] - d_kl * gt["kl"]
            dlogf = _dot_01(triu, da) + d_a_last
            sg, sq = gt["sg"], gt["sq"]
            dfg = dlogf / gt["fg"] - dk
            df_ref[b] = (dfg * (1.0 - lb) * sg * (1.0 - sg)).astype(df_ref.dtype)
            dlb_acc[...] += jnp.sum(dfg * (1.0 - sg), axis=0, keepdims=True)
            dq_ref[b] = (dq * (sq * (1.0 + hq * (1.0 - sq)))).astype(dq_ref.dtype)

        @pl.when(c == NC - 1)
        def _():
            dlb = dlb_acc[...]
            first = lax.broadcasted_iota(jnp.int32, (2, W), 0) == 0
            dlbl_ref[...] = jnp.where(first, dlb * lb * (1.0 - lb), -dlb * lb * (1.0 - lb))

    def col(cb):
        return pl.BlockSpec((B, C, 512), lambda c: (0, NC - 1 - c, cb))

    tile = pl.BlockSpec((B, C, W), lambda c: (0, NC - 1 - c, 0))
    proj3 = proj.reshape(B, S, proj.shape[-1])
    d3 = jax.ShapeDtypeStruct((B, S, W), BF16)
    out, carried = _pcall(
        body, (proj3, proj3, proj3, proj3, o_saved, states, d_out.reshape(3, B, S, W), lbl, gh), name="hgrn_bwd",
        grid=(NC,),
        in_specs=[col(COL_HQ), col(COL_HF), col(COL_HI), col(COL_HG), tile,
                  pl.BlockSpec((B, None, HG_HEADS, 128, 128), lambda c: (0, NC - 1 - c, 0, 0, 0)),
                  pl.BlockSpec((None, B, C, W), lambda c: (1, 0, NC - 1 - c, 0)),
                  pl.BlockSpec((2, W), lambda c: (0, 0)), pl.BlockSpec((1, HG_DIM), lambda c: (0, 0))],
        out_specs=(tile, tile, tile, tile,
                   pl.BlockSpec((2, W), lambda c: (0, 0)), pl.BlockSpec((1, HG_DIM), lambda c: (0, 0))),
        out_shape=(d3, d3, d3, d3, jax.ShapeDtypeStruct((2, W), F32), jax.ShapeDtypeStruct((1, HG_DIM), F32)),
        scratch_shapes=[pltpu.VMEM((B, HG_HEADS, 128, 128), F32), pltpu.VMEM((1, W), F32)],
        semantics=("arbitrary",), riders=riders)
    out = tuple(t.reshape(B * S, W) for t in out[:4]) + tuple(out[4:])
    return (out, carried) if riders else out


_XA_SCALE = XA_DIM ** -0.5


def _attn_probs(qh, kh):
    s = _dot_nt(qh, kh) * _XA_SCALE
    e = jnp.exp(s - jnp.max(s, axis=-1, keepdims=True))
    return e / jnp.sum(e, axis=-1, keepdims=True)


def _attn_fwd(proj, kv, B, S):
    T = B * S
    tq = _row_tile(S)
    nq = S // tq
    W = XA_HEADS * XA_DIM

    def body(q_ref, kv_ref, o_ref):
        for h in range(XA_HEADS):
            sl = slice(h * 128, (h + 1) * 128)
            p = _attn_probs(q_ref[:, sl], kv_ref[:, sl])
            o_ref[:, sl] = _dot(p, kv_ref[:, W + h * 128:W + (h + 1) * 128]).astype(o_ref.dtype)

    return pl.pallas_call(
        body, name="attn_fwd", grid=(B, nq),
        in_specs=[pl.BlockSpec((tq, 512), lambda b, i: (b * nq + i, COL_XQ)),
                  pl.BlockSpec((MEM_LEN, 2 * W), lambda b, i: (b, 0))],
        out_specs=pl.BlockSpec((tq, W), lambda b, i: (b * nq + i, 0)),
        out_shape=jax.ShapeDtypeStruct((T, W), BF16), compiler_params=_cp("parallel", "parallel"),
    )(proj, kv)


def _attn_bwd(proj, kv, d_out, B, S):
    T = B * S
    tq = _row_tile(S)
    nq = S // tq
    W = XA_HEADS * XA_DIM

    def body(q_ref, kv_ref, do_ref, dq_ref, dkv_ref):
        @pl.when(pl.program_id(1) == 0)
        def _():
            dkv_ref[...] = jnp.zeros_like(dkv_ref)

        for h in range(XA_HEADS):
            sl = slice(h * 128, (h + 1) * 128)
            slv = slice(W + h * 128, W + (h + 1) * 128)
            qh = q_ref[:, sl]
            kh = kv_ref[:, sl]
            p = _attn_probs(qh, kh)
            dc = do_ref[:, sl]
            dp = _dot_nt(dc, kv_ref[:, slv])
            ds = p * (dp - jnp.sum(dp * p, axis=-1, keepdims=True)) * _XA_SCALE
            dq_ref[:, sl] = _dot(ds, kh).astype(dq_ref.dtype)
            dkv_ref[:, sl] += _dot_tn(ds, qh)
            dkv_ref[:, slv] += _dot_tn(p, dc)

    kvspec = pl.BlockSpec((MEM_LEN, 2 * W), lambda b, i: (b, 0))
    tile = pl.BlockSpec((tq, W), lambda b, i: (b * nq + i, 0))
    return pl.pallas_call(
        body, name="attn_bwd", grid=(B, nq),
        in_specs=[pl.BlockSpec((tq, 512), lambda b, i: (b * nq + i, COL_XQ)), kvspec,
                  pl.BlockSpec((None, tq, W), lambda b, i: (2, b * nq + i, 0))],
        out_specs=(tile, kvspec),
        out_shape=(jax.ShapeDtypeStruct((T, W), BF16), jax.ShapeDtypeStruct((B * MEM_LEN, 2 * W), F32)),
        compiler_params=_cp("parallel", "arbitrary"),
    )(proj, kv, d_out)


_MERGE_TM = 256
_GATE_W = 512


def _gate_specs(tm):
    base = COL_GATE0 // _GATE_W
    return [pl.BlockSpec((tm, _GATE_W), functools.partial(lambda i, k: (i, base + k), k=k)) for k in range(6)]


def _merge_fwd(a_out, b_out, c_out, wb, proj):
    T = a_out.shape[0]
    tm = _row_tile(T, _MERGE_TM)
    nq, _, wd = wb.shape
    per_half = _GATE_W // wd

    def body(a_ref, b_ref, c_ref, w_ref, *rest):
        gates, (m_ref, up_ref) = rest[:6], rest[6:]
        for hf in range(2):
            cols = slice(hf * _GATE_W, (hf + 1) * _GATE_W)
            acc = None
            for n, br in enumerate((a_ref, b_ref, c_ref)):
                x = br[...]
                up = jnp.concatenate([_dot(x, w_ref[per_half * hf + j, n * BR_WIDTH:(n + 1) * BR_WIDTH, :])
                                      for j in range(per_half)], axis=1)
                up_ref[n, :, cols] = up.astype(up_ref.dtype)
                term = _sigmoid(gates[2 * n + hf][...]) * up
                acc = term if acc is None else acc + term
            m_ref[:, cols] = acc.astype(m_ref.dtype)

    br_spec = pl.BlockSpec((tm, BR_WIDTH), lambda i: (i, 0))
    return pl.pallas_call(
        body, name="merge_fwd", grid=(T // tm,),
        in_specs=[br_spec, br_spec, br_spec,
                  pl.BlockSpec((nq, 3 * BR_WIDTH, wd), lambda i: (0, 0, 0))] + _gate_specs(tm),
        out_specs=(pl.BlockSpec((tm, D_MODEL), lambda i: (i, 0)), pl.BlockSpec((3, tm, D_MODEL), lambda i: (0, i, 0))),
        out_shape=(jax.ShapeDtypeStruct((T, D_MODEL), BF16), jax.ShapeDtypeStruct((3, T, D_MODEL), BF16)),
        compiler_params=_cp("parallel"),
    )(a_out, b_out, c_out, wb, *([proj] * 6))


def _branch_bwd_act(d_ups, wb):
    _, T, D = d_ups.shape
    nq, _, wd = wb.shape
    tm = _row_tile(T)

    def body(d_ref, w_ref, o_ref):
        acc = None
        for q in range(nq):
            part = _dot_nt(d_ref[:, q * wd:(q + 1) * wd], w_ref[q])
            acc = part if acc is None else acc + part
        o_ref[...] = acc

    return pl.pallas_call(
        body, name="d_branch", grid=(3, T // tm),
        in_specs=[pl.BlockSpec((None, tm, D), lambda n, i: (n, i, 0)),
                  pl.BlockSpec((nq, BR_WIDTH, wd), lambda n, i: (0, n, 0))],
        out_specs=pl.BlockSpec((None, tm, BR_WIDTH), lambda n, i: (n, i, 0)),
        out_shape=jax.ShapeDtypeStruct((3, T, BR_WIDTH), F32), compiler_params=_cp("parallel", "parallel"),
    )(d_ups, wb)


def _branch_bwd_weight(name, br, d_ups, n):
    T = br.shape[0]
    D = d_ups.shape[2]
    wd = D // N_CHIPS
    tt = _row_tile(T, 1024)

    def body(b_ref, d_ref, o_ref):
        k = pl.program_id(0)
        for q in range(N_CHIPS):
            part = _dot_tn(b_ref[...], d_ref[:, q * wd:(q + 1) * wd])

            @pl.when(k == 0)
            def _():
                o_ref[q] = part

            @pl.when(k > 0)
            def _():
                o_ref[q] += part

    return pl.pallas_call(
        body, name=name, grid=(T // tt,),
        in_specs=[pl.BlockSpec((tt, BR_WIDTH), lambda k: (k, 0)),
                  pl.BlockSpec((None, tt, D), lambda k: (n, k, 0))],
        out_specs=pl.BlockSpec((N_CHIPS, BR_WIDTH, wd), lambda k: (0, 0, 0)),
        out_shape=jax.ShapeDtypeStruct((N_CHIPS, BR_WIDTH, wd), F32), compiler_params=_cp("arbitrary"),
    )(br, d_ups)


def _merge_bwd(d_merged, ups, proj):
    T = d_merged.shape[0]
    tm = _row_tile(T, _MERGE_TM)

    def body(dm_ref, up_ref, *rest):
        gates, (dup_ref, dg0_ref, dg1_ref, dg2_ref) = rest[:6], rest[6:]
        for hf in range(2):
            cols = slice(hf * _GATE_W, (hf + 1) * _GATE_W)
            dm = dm_ref[:, cols]
            for n, dgr in enumerate((dg0_ref, dg1_ref, dg2_ref)):
                gate = _sigmoid(gates[2 * n + hf][...])
                dup_ref[n, :, cols] = (dm * gate).astype(dup_ref.dtype)
                dgr[:, cols] = (dm * up_ref[n, :, cols].astype(F32) * gate * (1.0 - gate)).astype(dgr.dtype)

    tile = pl.BlockSpec((tm, D_MODEL), lambda i: (i, 0))
    tile3 = pl.BlockSpec((3, tm, D_MODEL), lambda i: (0, i, 0))
    return pl.pallas_call(
        body, name="merge_bwd", grid=(T // tm,),
        in_specs=[tile, tile3] + _gate_specs(tm),
        out_specs=(tile3, tile, tile, tile),
        out_shape=(jax.ShapeDtypeStruct((3, T, D_MODEL), BF16),) + (jax.ShapeDtypeStruct((T, D_MODEL), BF16),) * 3,
        compiler_params=_cp("parallel"),
    )(d_merged, ups, *([proj] * 6))


_CONV_TF = D_FF // 2
_CONV_TS = 256
_HALO = 8


def _conv_fwd(ab, cw, cb, B, S):
    T = B * S
    ts = _row_tile(S, _CONV_TS)
    tf = _CONV_TF
    nb = D_FF // tf
    tps = S // ts
    hb = ts // _HALO

    def body(a_ref, p_ref, b_ref, w_ref, cb_ref, o_ref):
        start = (pl.program_id(0) % tps) == 0
        prev = jnp.where(start, 0.0, p_ref[...])
        ext = jnp.concatenate([prev, a_ref[...]], axis=0)
        a1 = pltpu.roll(ext, 1, 0)[_HALO:, :]
        a2 = pltpu.roll(ext, 2, 0)[_HALO:, :]
        ac = cb_ref[...] + w_ref[0] * a2 + w_ref[1] * a1 + w_ref[2] * a_ref[...]
        o_ref[...] = (ac * _sigmoid(ac) * b_ref[...]).astype(o_ref.dtype)

    return pl.pallas_call(
        body, name="conv_fwd", grid=(T // ts, nb),
        in_specs=[pl.BlockSpec((ts, tf), lambda i, j: (i, j)),
                  pl.BlockSpec((_HALO, tf), lambda i, j: (jnp.maximum(i * hb - 1, 0), j)),
                  pl.BlockSpec((ts, tf), lambda i, j: (i, j + nb)),
                  pl.BlockSpec((3, 1, tf), lambda i, j: (0, 0, j)),
                  pl.BlockSpec((1, tf), lambda i, j: (0, j))],
        out_specs=pl.BlockSpec((ts, tf), lambda i, j: (i, j)),
        out_shape=jax.ShapeDtypeStruct((T, D_FF), BF16), compiler_params=_cp("parallel", "parallel"),
    )(ab, ab, ab, cw, cb)


def _conv_bwd(ab, d_ff, cw, cb, B, S):
    T = B * S
    ts = _row_tile(S, _CONV_TS)
    tf = _CONV_TF
    nb = D_FF // tf
    tps = S // ts
    hb = ts // _HALO
    last_h = T // _HALO - 1
    n_ext = ts + _HALO

    def body(a_ref, ap_ref, an_ref, b_ref, bn_ref, d_ref, dn_ref, w_ref, cb_ref, dab_ref, dw_ref, dcb_ref):
        i = pl.program_id(1)

        @pl.when(i == 0)
        def _():
            dw_ref[...] = jnp.zeros_like(dw_ref)
            dcb_ref[...] = jnp.zeros_like(dcb_ref)

        start = (i % tps) == 0
        end = (i % tps) == tps - 1
        a = a_ref[...]
        ext = jnp.concatenate([jnp.where(start, 0.0, ap_ref[...]), a, an_ref[...]], axis=0)
        r1 = pltpu.roll(ext, 1, 0)[_HALO:, :]
        r2 = pltpu.roll(ext, 2, 0)[_HALO:, :]
        ac = cb_ref[...] + w_ref[0] * r2 + w_ref[1] * r1 + w_ref[2] * ext[_HALO:, :]
        sg = _sigmoid(ac)
        d_e = jnp.concatenate([d_ref[...], jnp.where(end, 0.0, dn_ref[...])], axis=0)
        b_e = jnp.concatenate([b_ref[...], bn_ref[...]], axis=0)
        dab_ref[1] = (d_e[:ts, :] * (ac * sg)[:ts, :]).astype(dab_ref.dtype)
        dac = d_e * b_e * sg * (1.0 + ac * (1.0 - sg))
        u1 = pltpu.roll(dac, n_ext - 1, 0)[:ts, :]
        u2 = pltpu.roll(dac, n_ext - 2, 0)[:ts, :]
        dac0 = dac[:ts, :]
        dab_ref[0] = (w_ref[2] * dac0 + w_ref[1] * u1 + w_ref[0] * u2).astype(dab_ref.dtype)
        dcb_ref[...] += jnp.sum(dac0, axis=0, keepdims=True)
        dw_ref[2] += jnp.sum(dac0 * a, axis=0, keepdims=True)
        dw_ref[1] += jnp.sum(dac0 * r1[:ts, :], axis=0, keepdims=True)
        dw_ref[0] += jnp.sum(dac0 * r2[:ts, :], axis=0, keepdims=True)

    def cur(off):
        return pl.BlockSpec((ts, tf), lambda j, i: (i, j + off))

    def nxt(off):
        return pl.BlockSpec((_HALO, tf), lambda j, i: (jnp.minimum((i + 1) * hb, last_h), j + off))

    return pl.pallas_call(
        body, name="conv_bwd", grid=(nb, T // ts),
        in_specs=[cur(0), pl.BlockSpec((_HALO, tf), lambda j, i: (jnp.maximum(i * hb - 1, 0), j)), nxt(0),
                  cur(nb), nxt(nb), cur(0), nxt(0),
                  pl.BlockSpec((3, 1, tf), lambda j, i: (0, 0, j)), pl.BlockSpec((1, tf), lambda j, i: (0, j))],
        out_specs=(pl.BlockSpec((2, ts, tf), lambda j, i: (0, i, j)), pl.BlockSpec((3, 1, tf), lambda j, i: (0, 0, j)),
                   pl.BlockSpec((1, tf), lambda j, i: (0, j))),
        out_shape=(jax.ShapeDtypeStruct((2, T, D_FF), BF16),
                   jax.ShapeDtypeStruct((3, 1, D_FF), F32), jax.ShapeDtypeStruct((1, D_FF), F32)),
        compiler_params=_cp("parallel", "arbitrary"),
    )(ab, ab, ab, ab, ab, d_ff, d_ff, cw, cb)


def _local_step(x, mem, tgt, p, comm, B, S):
    g = {}
    h = _rms_fwd("norm1", x, p["norm1_g"])
    proj = comm.carry("in_proj", lambda r: _mm_cs("in_proj", h, comm.w("w_in"), F32, riders=r))
    a_out = _gmlp_fwd(proj, p["ln_v_g"], p["ln_v_b"], p["w_spatial"], p["b_spatial"])
    o_h, b_out, states = comm.carry(
        "hgrn_fwd", lambda r: _hgrn_fwd(proj, p["lb_logits"], p["hgrn_norm_g"], B, S, riders=r))
    memn = _rms_fwd("mem_norm", mem, p["mem_norm_g"])
    kv = _mm_rs("mem_kv", memn, comm.w("w_mem_kv"), F32)
    c_out = _attn_fwd(proj, kv, B, S)
    merged, ups = _merge_fwd(a_out, b_out, c_out, comm.w("w_branch"), proj)
    x1 = _mm_rs("out_proj", merged, comm.w("w_out"), F32, res=x)
    h2 = _rms_fwd("norm2", x1, p["norm2_g"])
    ab = _mm_cs("up_proj", h2, comm.w("w_up"), F32)
    conv_w = comm.w("conv_w")
    ff = _conv_fwd(ab, conv_w, p["conv_b"], B, S)
    x2 = _mm_rs("down_proj", ff, comm.w("w_down"), F32, res=x1)
    dx2, g["final_g"], loss = _loss_head(x2, tgt, p["final_g"])

    comm.grad("w_down", _mm_tn_rs("g_w_down", ff, dx2, to=D_FF // 2))
    d_ff = _mm_nt_rs("d_ff", dx2, comm.w("w_down"), F32, to=D_FF // 2)
    d_ab, g["conv_w"], g["conv_b"] = _conv_bwd(ab, d_ff, conv_w, p["conv_b"], B, S)
    comm.grad("w_up", _mm_tn_cs("g_w_up", h2, d_ab, N_CHIPS, to=512, stacked=True))
    d_h2 = comm.carry("d_h2", lambda r: _mm_nt_cs("d_h2", d_ab, comm.w("w_up"), F32, riders=r, stacked=True))
    d_x1, g["norm2_g"] = _rms_bwd("norm2_bwd", x1, p["norm2_g"], d_h2, dx2)
    comm.grad("w_out", _mm_tn_rs("g_w_out", merged, d_x1, to=512))
    d_merged = _mm_nt_rs("d_merged", d_x1, comm.w("w_out"), F32, to=512)
    d_ups, d_g0, d_g1, d_g2 = _merge_bwd(d_merged, ups, proj)

    d_br = _branch_bwd_act(d_ups, comm.w("w_branch"))
    comm.grad("w_branch", jnp.concatenate(
        [_branch_bwd_weight("g_w_branch%d" % n, br, d_ups, n) for n, br in enumerate((a_out, b_out, c_out))],
        axis=1))

    d_zu, d_zv, g["w_spatial"], g["b_spatial"], g["ln_v_g"], g["ln_v_b"] = _gmlp_bwd(
        proj, d_br, p["ln_v_g"], p["ln_v_b"], p["w_spatial"], p["b_spatial"])
    d_xq, d_kv = _attn_bwd(proj, kv, d_br, B, S)
    comm.grad("w_mem_kv", _mm_tn_rs("g_w_mem_kv", memn, d_kv, to=512))
    d_memn = _mm_nt_rs("d_memn", d_kv, comm.w("w_mem_kv"), F32, to=512)
    _, g["mem_norm_g"] = _rms_bwd("mem_norm_bwd", mem, p["mem_norm_g"], d_memn, None)
    d_hq, d_hf, d_hi, d_hg, g["lb_logits"], g["hgrn_norm_g"] = comm.carry(
        "hgrn_bwd", lambda r: _hgrn_bwd(proj, o_h, states, d_br, p["lb_logits"], p["hgrn_norm_g"], B, S, riders=r))
    d_proj = jnp.concatenate([d_zu, d_zv, d_hq, d_hf, d_hi, d_hg, d_xq, d_g0, d_g1, d_g2], axis=1)
    comm.grad("w_in", comm.carry("g_w_in", lambda r: _mm_tn_cs("g_w_in", h, d_proj, N_CHIPS, to=512, riders=r)))
    comm.small_grads([g[n].reshape(_SMALL_SHAPE[n]) for n in _SMALL_EARLY] + [loss])
    d_h = comm.carry("d_h", lambda r: _mm_nt_cs("d_h", d_proj, comm.w("w_in"), F32, riders=r))
    grad_x, g["norm1_g"] = _rms_bwd("norm1_bwd", x, p["norm1_g"], d_h, d_x1)
    return loss, grad_x, g


ANY = pl.BlockSpec(memory_space=pl.ANY)


def _place():
    x, y, c = lax.axis_index("x"), lax.axis_index("y"), lax.axis_index("c")
    other_chips = [(1 - x, y), (x, 1 - y), (1 - x, 1 - y)]
    return x, y, c, other_chips


def _remote(src, dst, send_sem, recv_sem, dev):
    return pltpu.make_async_remote_copy(src_ref=src, dst_ref=dst, send_sem=send_sem, recv_sem=recv_sem,
                                        device_id=dev, device_id_type=MESH_ID)


class _Exchange:
    def __init__(self, operands, out_shape, aliases, scratch, start, finish):
        self.operands, self.out_shape, self.aliases, self.scratch = operands, out_shape, aliases, scratch
        self.start, self.finish = start, finish


def _run_exchanges(name, exs):
    n_in = [len(ex.operands) for ex in exs]
    n_out = [len(ex.out_shape) for ex in exs]
    n_scr = [len(ex.scratch) for ex in exs]

    def body(*refs):
        ins, outs, scr = refs[:sum(n_in)], refs[sum(n_in):sum(n_in) + sum(n_out)], refs[sum(n_in) + sum(n_out):]
        parts, oi, oo, os_ = [], 0, 0, 0
        for k in range(len(exs)):
            parts.append((ins[oi:oi + n_in[k]], outs[oo:oo + n_out[k]], scr[os_:os_ + n_scr[k]]))
            oi, oo, os_ = oi + n_in[k], oo + n_out[k], os_ + n_scr[k]
        for ex, part in zip(exs, parts):
            ex.start(*part)
        for ex, part in zip(exs, parts):
            ex.finish(*part)

    aliases, ops, shapes, scratch, oi, oo = {}, [], [], [], 0, 0
    for k, ex in enumerate(exs):
        aliases.update({oi + a: oo + b for a, b in ex.aliases.items()})
        oi, oo = oi + n_in[k], oo + n_out[k]
        ops += list(ex.operands)
        shapes += list(ex.out_shape)
        scratch += list(ex.scratch)
    res = pl.pallas_call(
        body, name=name, in_specs=[ANY] * len(ops), out_specs=(ANY,) * len(shapes), out_shape=tuple(shapes),
        input_output_aliases=aliases, scratch_shapes=scratch,
    )(*ops)
    out, oo = [], 0
    for k in range(len(exs)):
        out.append(list(res[oo:oo + n_out[k]]))
        oo += n_out[k]
    return out


def _ex_all_gather(slabs, halved):
    n = len(slabs)

    def rows(a, cc):
        if not halved[a]:
            return slice(None)
        hr = slabs[a].shape[1] // 2
        return pl.ds(cc * hr, hr)

    def ici(bufs, scr, a, j, chip, c, mine):
        px, py = chip
        x, y, _, _ = _place()
        qs = 2 * x + y if mine else 2 * px + py
        piece = bufs[a].at[qs, rows(a, c)]
        return _remote(piece, piece, scr[0].at[3 * a + j], scr[1].at[3 * a + j], (px, py, c))

    def d2d(bufs, scr, a, j, chip, cc):
        px, py = chip
        x, y, c, _ = _place()
        piece = bufs[a].at[2 * px + py, rows(a, cc)]
        return _remote(piece, piece, scr[2].at[3 * a + j], scr[3].at[3 * a + j], (x, y, 1 - c))

    def start(ins, outs, scr):
        _, _, c, chips = _place()
        for j, chip in enumerate(chips):
            for a in range(n):
                ici(outs, scr, a, j, chip, c, True).start()

    def finish(ins, outs, scr):
        _, _, c, chips = _place()
        for j, chip in enumerate(chips):
            for a in range(n):
                ici(outs, scr, a, j, chip, c, False).wait_recv()
                if halved[a]:
                    d2d(outs, scr, a, j, chip, c).start()
        for j, chip in enumerate(chips):
            for a in range(n):
                if halved[a]:
                    d2d(outs, scr, a, j, chip, 1 - c).wait_recv()
        for j, chip in enumerate(chips):
            for a in range(n):
                ici(outs, scr, a, j, chip, c, True).wait_send()
                if halved[a]:
                    d2d(outs, scr, a, j, chip, c).wait_send()

    return _Exchange(list(slabs), [jax.ShapeDtypeStruct(s.shape, s.dtype) for s in slabs],
                     {a: a for a in range(n)}, [pltpu.SemaphoreType.DMA((3 * n,))] * 4, start, finish)


def _ex_to_sibling(grads):
    n = len(grads)

    def copy(ins, outs, scr, a):
        x, y, c, _ = _place()
        hr = grads[a].shape[1] // 2
        return _remote(ins[a].at[:, pl.ds((1 - c) * hr, hr), :], outs[a], scr[0].at[a], scr[1].at[a], (x, y, 1 - c))

    def start(ins, outs, scr):
        for a in range(n):
            copy(ins, outs, scr, a).start()

    def finish(ins, outs, scr):
        for a in range(n):
            copy(ins, outs, scr, a).wait()

    out_shape = [jax.ShapeDtypeStruct((g.shape[0], g.shape[1] // 2, g.shape[2]), g.dtype) for g in grads]
    return _Exchange(list(grads), out_shape, {}, [pltpu.SemaphoreType.DMA((n,))] * 2, start, finish)


def _ex_to_owner(parts):
    n = len(parts)

    def copy(ins, outs, scr, a, j, chip):
        _, _, c, _ = _place()
        px, py = chip
        return _remote(ins[a].at[2 * px + py], outs[a].at[j], scr[0].at[3 * a + j], scr[1].at[3 * a + j],
                       (px, py, c))

    def start(ins, outs, scr):
        for j, chip in enumerate(_place()[3]):
            for a in range(n):
                copy(ins, outs, scr, a, j, chip).start()

    def finish(ins, outs, scr):
        for j, chip in enumerate(_place()[3]):
            for a in range(n):
                copy(ins, outs, scr, a, j, chip).wait()

    out_shape = [jax.ShapeDtypeStruct((3,) + p.shape[1:], p.dtype) for p in parts]
    return _Exchange(list(parts), out_shape, {}, [pltpu.SemaphoreType.DMA((3 * n,))] * 2, start, finish)


def _ex_share_halves(bufs):
    n = len(bufs)

    def copy(outs, scr, a, cc):
        x, y, c, _ = _place()
        hr = bufs[a].shape[0] // 2
        piece = outs[a].at[pl.ds(cc * hr, hr), :]
        return _remote(piece, piece, scr[0].at[a], scr[1].at[a], (x, y, 1 - c))

    def start(ins, outs, scr):
        c = _place()[2]
        for a in range(n):
            copy(outs, scr, a, c).start()

    def finish(ins, outs, scr):
        c = _place()[2]
        for a in range(n):
            copy(outs, scr, a, c).wait_send()
            copy(outs, scr, a, 1 - c).wait_recv()

    return _Exchange(list(bufs), [jax.ShapeDtypeStruct(b.shape, b.dtype) for b in bufs], {a: a for a in range(n)},
                     [pltpu.SemaphoreType.DMA((n,))] * 2, start, finish)


def _ex_gather_small(arrs):
    n = len(arrs)

    def peer_of(m):
        x, y, c, _ = _place()
        return (1 - x if m & 4 else x, 1 - y if m & 2 else y, 1 - c if m & 1 else c)

    def start(ins, outs, scr):
        x, y, c, _ = _place()
        for m in range(1, N_DEV):
            for a in range(n):
                k = (N_DEV - 1) * a + m - 1
                _remote(ins[a], outs[a].at[4 * x + 2 * y + c], scr[0].at[k], scr[1].at[k], peer_of(m)).start()

    def finish(ins, outs, scr):
        for m in range(1, N_DEV):
            px, py, pc = peer_of(m)
            for a in range(n):
                k = (N_DEV - 1) * a + m - 1
                slot = outs[a].at[4 * px + 2 * py + pc]
                cp = _remote(ins[a], slot, scr[0].at[k], scr[1].at[k], (px, py, pc))
                cp.wait_send()
                cp.wait_recv()

    slots = [jnp.zeros((N_DEV,) + a.shape, a.dtype) for a in arrs]
    out_shape = [jax.ShapeDtypeStruct(s.shape, s.dtype) for s in slots]
    return _Exchange(list(arrs) + slots, out_shape, {n + a: a for a in range(n)},
                     [pltpu.SemaphoreType.DMA(((N_DEV - 1) * n,))] * 2, start, finish)


def _div_tile(n, want):
    best = None
    for t in range(8, min(n, want) + 1, 8):
        if n % t == 0:
            best = t
    assert best is not None, n
    return best


def _cast_into_slab(name, w, place, dtype):
    r, cc = w.shape
    tr = r if r * cc <= 128 * 1024 else _div_tile(r, 256)

    def body(s_ref, w_ref, o_ref):
        o_ref[...] = w_ref[...].astype(o_ref.dtype)

    return pl.pallas_call(
        body, name=name,
        grid_spec=pltpu.PrefetchScalarGridSpec(
            num_scalar_prefetch=1, grid=(r // tr,),
            in_specs=[pl.BlockSpec((tr, cc), lambda i, s: (i, 0))],
            out_specs=pl.BlockSpec((None, tr, cc), lambda i, s: (s[0], i, 0))),
        out_shape=jax.ShapeDtypeStruct((N_CHIPS, r, cc), dtype), compiler_params=_cp("parallel"),
    )(place, w)


def _add_half(name, g, rcv, place):
    nq, r, cc = g.shape
    hr = r // 2

    def body(s_ref, g_ref, r_ref, o_ref):
        o_ref[...] = (g_ref[...] + r_ref[...]).astype(o_ref.dtype)

    spec = pl.BlockSpec((None, hr, cc), lambda i, s: (i, 0, 0))
    return pl.pallas_call(
        body, name=name,
        grid_spec=pltpu.PrefetchScalarGridSpec(
            num_scalar_prefetch=1, grid=(nq,),
            in_specs=[pl.BlockSpec((None, hr, cc), lambda i, s: (i, s[1], 0)), spec], out_specs=spec),
        out_shape=jax.ShapeDtypeStruct((nq, hr, cc), BF16), compiler_params=_cp("parallel"),
    )(place, g, rcv)


def _sum_owner(name, part, rcv, place):
    _, hr, cc = part.shape
    tr = _div_tile(hr, 128)
    nb = hr // tr

    def body(s_ref, p_ref, r_ref, o_ref):
        o_ref[...] = ((p_ref[...].astype(F32) + r_ref[0].astype(F32)) + r_ref[1].astype(F32)) + r_ref[2].astype(F32)

    return pl.pallas_call(
        body, name=name,
        grid_spec=pltpu.PrefetchScalarGridSpec(
            num_scalar_prefetch=1, grid=(nb,),
            in_specs=[pl.BlockSpec((None, tr, cc), lambda i, s: (s[0], i, 0)),
                      pl.BlockSpec((3, tr, cc), lambda i, s: (0, i, 0))],
            out_specs=pl.BlockSpec((tr, cc), lambda i, s: (s[1] * nb + i, 0))),
        out_shape=jax.ShapeDtypeStruct((2 * hr, cc), F32), compiler_params=_cp("parallel"),
    )(place, part, rcv)


def _sum_small(gathered, local, place):
    n = len(gathered)

    def body(s_ref, *refs):
        g_refs, l_refs, o_refs = refs[:n], refs[n:2 * n], refs[2 * n:]
        me = s_ref[2]
        for g_ref, l_ref, o_ref in zip(g_refs, l_refs, o_refs):
            acc = None
            for d in range(N_DEV):
                term = jnp.where(me == d, l_ref[...], g_ref[d])
                acc = term if acc is None else acc + term
            o_ref[...] = acc

    def whole(shape):
        return pl.BlockSpec(shape, lambda i, s, nd=len(shape): (0,) * nd)

    return pl.pallas_call(
        body, name="sum_small",
        grid_spec=pltpu.PrefetchScalarGridSpec(
            num_scalar_prefetch=1, grid=(1,),
            in_specs=[whole(g.shape) for g in gathered] + [whole(a.shape) for a in local],
            out_specs=tuple(whole(a.shape) for a in local)),
        out_shape=tuple(jax.ShapeDtypeStruct(a.shape, a.dtype) for a in local), compiler_params=_cp("arbitrary"),
    )(place, *gathered, *local)


def _adamw(name, w, g, m, v):
    r, cc = w.shape
    tr = r if r * cc <= 128 * 1024 else _div_tile(r, 256)

    def body(w_ref, g_ref, m_ref, v_ref, d_ref, mo_ref, vo_ref):
        gv = g_ref[...]
        mn = ADAM_B1 * m_ref[...] + (1.0 - ADAM_B1) * gv
        vn = ADAM_B2 * v_ref[...] + (1.0 - ADAM_B2) * (gv * gv)
        m_hat = mn / (1.0 - ADAM_B1 ** ADAM_STEP)
        v_hat = vn / (1.0 - ADAM_B2 ** ADAM_STEP)
        d_ref[...] = -ADAM_LR * (m_hat / (jnp.sqrt(v_hat) + ADAM_EPS) + ADAM_WD * w_ref[...])
        mo_ref[...] = mn
        vo_ref[...] = vn

    spec = pl.BlockSpec((tr, cc), lambda i: (i, 0))
    sd = jax.ShapeDtypeStruct((r, cc), F32)
    return pl.pallas_call(
        body, name=name, grid=(r // tr,), in_specs=[spec] * 4, out_specs=(spec,) * 3, out_shape=(sd,) * 3,
        compiler_params=_cp("parallel"),
    )(w, g, m, v)


_BIG = ("w_in", "w_up", "w_branch", "w_mem_kv", "w_out", "w_down")
_BIG_SHARD_SHAPE = {"w_in": (1024, 1664), "w_up": (1024, 1408), "w_branch": (1536, 256),
                    "w_mem_kv": (256, 1024), "w_out": (256, 1024), "w_down": (704, 1024)}
_SMALL_SHAPE = {"norm1_g": (1, D_MODEL), "ln_v_g": (1, GM_WIDTH), "ln_v_b": (1, GM_WIDTH),
                "w_spatial": (GM_GROUPS * GM_CHUNK, GM_CHUNK), "b_spatial": (GM_GROUPS, GM_CHUNK),
                "lb_logits": (2, HG_HEADS * HG_DIM), "hgrn_norm_g": (1, HG_DIM), "mem_norm_g": (1, D_MODEL),
                "norm2_g": (1, D_MODEL), "conv_w": (3, D_FF), "conv_b": (1, D_FF), "final_g": (1, D_MODEL)}
_SMALL_EARLY = tuple(n for n in _SMALL_SHAPE if n != "norm1_g")
_PARAM_ORDER = ("norm1_g", "w_in", "ln_v_g", "ln_v_b", "w_spatial", "b_spatial", "lb_logits", "hgrn_norm_g",
                "mem_norm_g", "w_mem_kv", "w_branch", "w_out", "norm2_g", "w_up", "conv_w", "conv_b", "w_down",
                "final_g")


def _adamw_small(ws, gs, ms, vs):
    n = len(ws)

    def body(*refs):
        w_refs, g_refs, m_refs, v_refs = refs[:n], refs[n:2 * n], refs[2 * n:3 * n], refs[3 * n:4 * n]
        d_refs, mo_refs, vo_refs = refs[4 * n:5 * n], refs[5 * n:6 * n], refs[6 * n:]
        for k in range(n):
            gv = g_refs[k][...]
            mn = ADAM_B1 * m_refs[k][...] + (1.0 - ADAM_B1) * gv
            vn = ADAM_B2 * v_refs[k][...] + (1.0 - ADAM_B2) * (gv * gv)
            m_hat = mn / (1.0 - ADAM_B1 ** ADAM_STEP)
            v_hat = vn / (1.0 - ADAM_B2 ** ADAM_STEP)
            d_refs[k][...] = -ADAM_LR * (m_hat / (jnp.sqrt(v_hat) + ADAM_EPS) + ADAM_WD * w_refs[k][...])
            mo_refs[k][...] = mn
            vo_refs[k][...] = vn

    specs = [pl.BlockSpec(a.shape, lambda i: (0, 0)) for a in ws]
    shapes = tuple(jax.ShapeDtypeStruct(a.shape, F32) for a in ws)
    res = pl.pallas_call(
        body, name="adamw_small", grid=(1,), in_specs=specs * 4, out_specs=tuple(specs * 3), out_shape=shapes * 3,
        compiler_params=_cp("arbitrary"),
    )(*ws, *gs, *ms, *vs)
    return res[:n], res[n:2 * n], res[2 * n:]


class _Comm:
    _ROW_SHARDED = ("w_mem_kv", "w_out", "w_down")

    def __init__(self, slabs, place):
        self.slabs, self.place = slabs, place
        self.full, self.raw, self.parts, self.bufs, self.done = {}, {}, {}, {}, {}
        ex, deliver = self._gather(["w_in"])
        deliver(_run_exchanges("all_gather_w_in", [ex])[0])

    def w(self, name):
        a = self.full[name]
        if name in self._ROW_SHARDED:
            return a.reshape(-1, a.shape[-1])
        if name == "conv_w":
            return jnp.transpose(a, (1, 0, 2)).reshape(3, 1, D_FF)
        return a

    def grad(self, name, arr):
        self.raw[name] = arr.reshape((N_CHIPS,) + _BIG_SHARD_SHAPE[name])
        if name == "w_in":
            ex, deliver = self._to_sibling(["w_in"])
            deliver(_run_exchanges("rs_sibling_w_in", [ex])[0])

    def small_grads(self, arrays):
        self.small_local = list(arrays)

    def carry(self, tag, call):
        plan = self._plan(tag)
        if not plan:
            return call(())
        out, carried = call([ex for ex, _ in plan])
        for (_, deliver), res in zip(plan, carried):
            deliver(res)
        return out

    def finish(self, last_small):
        ex, deliver = self._share(["w_out", "w_branch", "w_mem_kv", "w_in"])
        shared, small = _run_exchanges("share_and_gather_last", [ex, _ex_gather_small(last_small)])
        deliver(shared)
        return self.done, self.small_local + list(last_small), self.small_everyone + small

    def _plan(self, tag):
        if tag == "in_proj":
            return [self._gather(["w_branch", "w_out", "w_mem_kv", "w_down", "conv_w"])]
        if tag == "hgrn_fwd":
            return [self._gather(["w_up"])]
        if tag == "d_h2":
            return [self._to_sibling(["w_down", "w_up"])]
        if tag == "hgrn_bwd":
            return [self._to_owner(["w_down", "w_up"]), self._to_sibling(["w_out", "w_branch", "w_mem_kv"])]
        if tag == "g_w_in":
            return [self._to_owner(["w_out", "w_branch", "w_mem_kv"]), self._share(["w_down", "w_up"])]
        if tag == "d_h":
            def keep(res):
                self.small_everyone = res

            return [self._to_owner(["w_in"]), (_ex_gather_small(self.small_local), keep)]
        return []

    def _gather(self, names):
        ex = _ex_all_gather([self.slabs[n] for n in names], [n != "conv_w" for n in names])
        return ex, lambda res: self.full.update(zip(names, res))

    def _to_sibling(self, names):
        def deliver(res):
            for n, r in zip(names, res):
                self.parts[n] = _add_half("rs_add_" + n, self.raw[n], r, self.place)

        return _ex_to_sibling([self.raw[n] for n in names]), deliver

    def _to_owner(self, names):
        def deliver(res):
            for n, r in zip(names, res):
                self.bufs[n] = _sum_owner("rs_sum_" + n, self.parts[n], r, self.place)

        return _ex_to_owner([self.parts[n] for n in names]), deliver

    def _share(self, names):
        return _ex_share_halves([self.bufs[n] for n in names]), lambda res: self.done.update(zip(names, res))


def kernel(x, mem, norm1_g, w_in, ln_v_g, ln_v_b, w_spatial, b_spatial, lb_logits, hgrn_norm_g, mem_norm_g, w_mem_kv, w_branch, w_out, norm2_g, w_up, conv_w, conv_b, w_down, final_g, loss_target, m_norm1_g, m_w_in, m_ln_v_g, m_ln_v_b, m_w_spatial, m_b_spatial, m_lb_logits, m_hgrn_norm_g, m_mem_norm_g, m_w_mem_kv, m_w_branch, m_w_out, m_norm2_g, m_w_up, m_conv_w, m_conv_b, m_w_down, m_final_g, v_norm1_g, v_w_in, v_ln_v_g, v_ln_v_b, v_w_spatial, v_b_spatial, v_lb_logits, v_hgrn_norm_g, v_mem_norm_g, v_w_mem_kv, v_w_branch, v_w_out, v_norm2_g, v_w_up, v_conv_w, v_conv_b, v_w_down, v_final_g):
    w = dict(norm1_g=norm1_g, w_in=w_in, ln_v_g=ln_v_g, ln_v_b=ln_v_b, w_spatial=w_spatial, b_spatial=b_spatial,
             lb_logits=lb_logits, hgrn_norm_g=hgrn_norm_g, mem_norm_g=mem_norm_g, w_mem_kv=w_mem_kv,
             w_branch=w_branch, w_out=w_out, norm2_g=norm2_g, w_up=w_up, conv_w=conv_w, conv_b=conv_b,
             w_down=w_down, final_g=final_g)
    mom = dict(norm1_g=m_norm1_g, w_in=m_w_in, ln_v_g=m_ln_v_g, ln_v_b=m_ln_v_b, w_spatial=m_w_spatial,
               b_spatial=m_b_spatial, lb_logits=m_lb_logits, hgrn_norm_g=m_hgrn_norm_g, mem_norm_g=m_mem_norm_g,
               w_mem_kv=m_w_mem_kv, w_branch=m_w_branch, w_out=m_w_out, norm2_g=m_norm2_g, w_up=m_w_up,
               conv_w=m_conv_w, conv_b=m_conv_b, w_down=m_w_down, final_g=m_final_g)
    var = dict(norm1_g=v_norm1_g, w_in=v_w_in, ln_v_g=v_ln_v_g, ln_v_b=v_ln_v_b, w_spatial=v_w_spatial,
               b_spatial=v_b_spatial, lb_logits=v_lb_logits, hgrn_norm_g=v_hgrn_norm_g, mem_norm_g=v_mem_norm_g,
               w_mem_kv=v_w_mem_kv, w_branch=v_w_branch, w_out=v_w_out, norm2_g=v_norm2_g, w_up=v_w_up,
               conv_w=v_conv_w, conv_b=v_conv_b, w_down=v_w_down, final_g=v_final_g)
    B, S, D = x.shape
    T = B * S
    ci = lax.axis_index("c")
    q = 2 * lax.axis_index("x") + lax.axis_index("y")
    place = jnp.stack([q, ci, 2 * q + ci]).astype(jnp.int32)

    slabs = {n: _cast_into_slab("slab_" + n, w[n].reshape(_BIG_SHARD_SHAPE[n]), place, BF16) for n in _BIG}
    slabs["conv_w"] = _cast_into_slab("slab_conv_w", conv_w[0], place, F32)
    comm = _Comm(slabs, place)
    p = dict(
        norm1_g=norm1_g, ln_v_g=ln_v_g, ln_v_b=ln_v_b, w_spatial=w_spatial[0],
        b_spatial=b_spatial.reshape(GM_GROUPS, GM_CHUNK, 1), lb_logits=lb_logits, hgrn_norm_g=hgrn_norm_g,
        mem_norm_g=mem_norm_g, norm2_g=norm2_g, conv_b=conv_b, final_g=final_g.reshape(1, D))

    loss, grad_x, g = _local_step(x.reshape(T, D), mem.reshape(B * MEM_LEN, D), loss_target.reshape(T, D), p, comm,
                                  B, S)

    shard_grads, local_small, everyone = comm.finish([g["norm1_g"]])
    summed = _sum_small(everyone, local_small, place)
    small_names = list(_SMALL_EARLY) + ["norm1_g"]
    total = dict(zip(_SMALL_EARLY, summed))
    loss_total, total["norm1_g"] = summed[len(_SMALL_EARLY)][0, 0], summed[-1]

    grads, delta, new_m, new_v = {}, {}, {}, {}
    for n in _BIG:
        shp = _BIG_SHARD_SHAPE[n]
        grads[n] = shard_grads[n]
        delta[n], new_m[n], new_v[n] = _adamw("adamw_" + n, w[n].reshape(shp), shard_grads[n],
                                              mom[n].reshape(shp), var[n].reshape(shp))
    cw_shard = D_FF // N_CHIPS
    total["conv_w"] = lax.dynamic_slice(total["conv_w"], (0, q * cw_shard), (3, cw_shard))

    def flat2d(d, n):
        return d[n].reshape(total[n].shape)

    upd = _adamw_small([flat2d(w, n) for n in small_names], [total[n] for n in small_names],
                       [flat2d(mom, n) for n in small_names], [flat2d(var, n) for n in small_names])
    for k, n in enumerate(small_names):
        grads[n], delta[n], new_m[n], new_v[n] = total[n], upd[0][k], upd[1][k], upd[2][k]

    def shaped(d):
        return [d[n].reshape(w[n].shape) for n in _PARAM_ORDER]

    return (loss_total, grad_x.reshape(B, S, D), *shaped(grads), *shaped(delta), *shaped(new_m), *shaped(new_v))
```

```python
import functools
import math

import jax
import jax.numpy as jnp
from jax import lax
from jax.experimental import pallas as pl
from jax.experimental.pallas import tpu as pltpu

F32 = jnp.float32
BF16 = jnp.bfloat16
EPS = 1e-6

D_MODEL = 1024
MEM_LEN = 256
GM_WIDTH = 512
GM_CHUNK = 128
GM_GROUPS = 4
HG_HEADS = 4
HG_DIM = 128
HG_CHUNK = 64
XA_HEADS = 4
XA_DIM = 128
BR_WIDTH = 512
D_FF = 2816
IN_WIDTH = 6656
N_CHIPS = 4
N_DEV = 8

ADAM_LR = 0.001
ADAM_B1 = 0.9
ADAM_B2 = 0.999
ADAM_EPS = 1e-08
ADAM_WD = 0.01
ADAM_STEP = 10

COL_ZU, COL_ZV, COL_HQ, COL_HF, COL_HI, COL_HG, COL_XQ = 0, 1, 2, 3, 4, 5, 6
COL_GATE0 = 3584

VMEM_LIMIT_BYTES = 48 * 1024 * 1024
MESH_ID = pl.DeviceIdType.MESH


def _cp(*sem):
    return pltpu.CompilerParams(dimension_semantics=sem, vmem_limit_bytes=VMEM_LIMIT_BYTES)


def _pallas(body, *, out_shape, **kw):
    def pin(s):
        return pltpu.HBM(s.shape, s.dtype) if isinstance(s, jax.ShapeDtypeStruct) else s

    out_shape = tuple(pin(s) for s in out_shape) if isinstance(out_shape, (tuple, list)) else pin(out_shape)
    call = pl.pallas_call(body, out_shape=out_shape, **kw)

    def run(*operands):
        return call(*[pltpu.with_memory_space_constraint(o, pltpu.HBM) if jnp.issubdtype(o.dtype, jnp.floating)
                      else o for o in operands])

    return run


def _dot(a, b):
    return lax.dot_general(a.astype(BF16), b.astype(BF16), (((1,), (0,)), ((), ())), preferred_element_type=F32)


def _dot_nt(a, b):
    return lax.dot_general(a.astype(BF16), b.astype(BF16), (((1,), (1,)), ((), ())), preferred_element_type=F32)


def _dot_tn(a, b):
    return lax.dot_general(a.astype(BF16), b.astype(BF16), (((0,), (0,)), ((), ())), preferred_element_type=F32)


def _split2(x):
    hi = x.astype(BF16)
    return hi, (x - hi.astype(F32)).astype(BF16)


def _dot3(a, b, dims):
    ah, al = _split2(a)
    bh, bl = _split2(b)
    dn = (dims, ((), ()))
    return (lax.dot_general(ah, bh, dn, preferred_element_type=F32)
            + lax.dot_general(ah, bl, dn, preferred_element_type=F32)
            + lax.dot_general(al, bh, dn, preferred_element_type=F32))


def _dot_01(mask01, x):
    hi = x.astype(BF16)
    r1 = x - hi.astype(F32)
    mid = r1.astype(BF16)
    lo = (r1 - mid.astype(F32)).astype(BF16)
    m = mask01.astype(BF16)
    dn = (((1,), (0,)), ((), ()))
    return (lax.dot_general(m, hi, dn, preferred_element_type=F32)
            + lax.dot_general(m, mid, dn, preferred_element_type=F32)
            + lax.dot_general(m, lo, dn, preferred_element_type=F32))


def _sigmoid(z):
    return 1.0 / (1.0 + jnp.exp(-z))


_GELU_C = math.sqrt(2.0 / math.pi)


def _gelu_and_grad(z):
    inner = _GELU_C * (z + 0.044715 * z * z * z)
    t = jnp.tanh(inner)
    val = 0.5 * z * (1.0 + t)
    grad = 0.5 * (1.0 + t) + 0.5 * z * (1.0 - t * t) * _GELU_C * (1.0 + 3.0 * 0.044715 * z * z)
    return val, grad


def _row_tile(n, want=512):
    t = min(want, n)
    assert n % t == 0
    return t


def _pcall(body, operands, *, name, grid, in_specs, out_specs, out_shape, scratch_shapes=(), semantics, riders=()):
    single = not isinstance(out_shape, (tuple, list))
    out_specs = (out_specs,) if single else tuple(out_specs)
    out_shape = (out_shape,) if single else tuple(out_shape)
    if not riders:
        res = _pallas(body, name=name, grid=grid, in_specs=list(in_specs), out_specs=out_specs,
                             out_shape=out_shape, scratch_shapes=list(scratch_shapes),
                             compiler_params=_cp(*semantics))(*operands)
        return (res[0] if single else res), []
    n_in, n_out, n_scr = len(in_specs), len(out_shape), len(scratch_shapes)
    ex_in = [len(ex.operands) for ex in riders]
    ex_out = [len(ex.out_shape) for ex in riders]
    ex_scr = [len(ex.scratch) for ex in riders]
    tot_in, tot_out = n_in + sum(ex_in), n_out + sum(ex_out)

    def wrapped(*refs):
        ins, outs, scr = refs[:tot_in], refs[tot_in:tot_in + tot_out], refs[tot_in + tot_out:]
        ids = [pl.program_id(d) for d in range(len(grid))]
        first = functools.reduce(lambda p, t: p & t, [i == 0 for i in ids])
        last = functools.reduce(lambda p, t: p & t, [i == n - 1 for i, n in zip(ids, grid)])
        parts, oi, oo, os_ = [], n_in, n_out, n_scr
        for k in range(len(riders)):
            parts.append((ins[oi:oi + ex_in[k]], outs[oo:oo + ex_out[k]], scr[os_:os_ + ex_scr[k]]))
            oi, oo, os_ = oi + ex_in[k], oo + ex_out[k], os_ + ex_scr[k]

        @pl.when(first)
        def _():
            for ex, part in zip(riders, parts):
                ex.start(*part)

        body(*ins[:n_in], *outs[:n_out], *scr[:n_scr])

        @pl.when(last)
        def _():
            for ex, part in zip(riders, parts):
                ex.finish(*part)

    aliases, oi, oo = {}, n_in, n_out
    all_ops, all_shapes, all_scr = list(operands), list(out_shape), list(scratch_shapes)
    for k, ex in enumerate(riders):
        aliases.update({oi + a: oo + b for a, b in ex.aliases.items()})
        oi, oo = oi + ex_in[k], oo + ex_out[k]
        all_ops += list(ex.operands)
        all_shapes += [pltpu.HBM(s.shape, s.dtype) for s in ex.out_shape]
        all_scr += list(ex.scratch)
    res = _pallas(
        wrapped, name=name, grid=grid, in_specs=list(in_specs) + [HBM_SPEC] * sum(ex_in),
        out_specs=out_specs + (HBM_SPEC,) * sum(ex_out), out_shape=tuple(all_shapes), scratch_shapes=all_scr,
        input_output_aliases=aliases, compiler_params=_cp(*(["arbitrary"] * len(grid))))(*all_ops)
    own = res[0] if single else tuple(res[:n_out])
    carried, oo = [], n_out
    for k in range(len(riders)):
        carried.append(list(res[oo:oo + ex_out[k]]))
        oo += ex_out[k]
    return own, carried


def _matmul(name, operands, *, grid, in_specs, o_spec, out_shape, out_dtype, dims, has_res=False, riders=()):
    nk = grid[2]
    assert nk == 1 or (out_dtype == F32 and not has_res)

    def body(*refs):
        if has_res:
            a_ref, b_ref, r_ref, o_ref = refs
        else:
            a_ref, b_ref, o_ref = refs
            r_ref = None
        part = lax.dot_general(a_ref[...].astype(BF16), b_ref[...].astype(BF16), (dims, ((), ())),
                               preferred_element_type=F32)
        if nk == 1:
            if r_ref is not None:
                part = part + r_ref[...]
            o_ref[...] = part.astype(o_ref.dtype)
        else:
            k = pl.program_id(2)

            @pl.when(k == 0)
            def _():
                o_ref[...] = part

            @pl.when(k > 0)
            def _():
                o_ref[...] += part

    out, carried = _pcall(body, operands, name=name, grid=grid, in_specs=in_specs, out_specs=o_spec,
                          out_shape=jax.ShapeDtypeStruct(out_shape, out_dtype),
                          semantics=("parallel", "parallel", "arbitrary"), riders=riders)
    return (out, carried) if riders else out


NN = ((1,), (0,))
NT = ((1,), (1,))
TN = ((0,), (0,))


def _mm_cs(name, a, w, out_dtype, riders=()):
    M, K = a.shape
    nq, _, wd = w.shape
    tm = _row_tile(M)
    return _matmul(name, (a, w), grid=(nq, M // tm, 1),
                   in_specs=[pl.BlockSpec((tm, K), lambda j, i, k: (i, 0)),
                             pl.BlockSpec((None, K, wd), lambda j, i, k: (j, 0, 0))],
                   o_spec=pl.BlockSpec((tm, wd), lambda j, i, k: (i, j)),
                   out_shape=(M, nq * wd), out_dtype=out_dtype, dims=NN, riders=riders)


def _mm_rs(name, a, w, out_dtype, res=None, tn=512):
    M, K = a.shape
    N = w.shape[1]
    tm = _row_tile(M)
    tn = min(tn, N)
    ops = (a, w) if res is None else (a, w, res)
    in_specs = [pl.BlockSpec((tm, K), lambda i, j, k: (i, 0)),
                pl.BlockSpec((K, tn), lambda i, j, k: (0, j))]
    if res is not None:
        in_specs.append(pl.BlockSpec((tm, tn), lambda i, j, k: (i, j)))
    return _matmul(name, ops, grid=(M // tm, N // tn, 1), in_specs=in_specs,
                   o_spec=pl.BlockSpec((tm, tn), lambda i, j, k: (i, j)),
                   out_shape=(M, N), out_dtype=out_dtype, dims=NN, has_res=res is not None)


def _mm_nt_rs(name, g, w, out_dtype, to):
    M, N = g.shape
    K = w.shape[0]
    tm = _row_tile(M)
    return _matmul(name, (g, w), grid=(M // tm, K // to, 1),
                   in_specs=[pl.BlockSpec((tm, N), lambda i, j, k: (i, 0)),
                             pl.BlockSpec((to, N), lambda i, j, k: (j, 0))],
                   o_spec=pl.BlockSpec((tm, to), lambda i, j, k: (i, j)),
                   out_shape=(M, K), out_dtype=out_dtype, dims=NT)


def _mm_nt_cs(name, g, w, out_dtype, riders=(), stacked=False):
    M = g.shape[-2]
    nq, K, wd = w.shape
    tm = _row_tile(M, 256)

    def body(g_ref, w_ref, o_ref):
        acc = None
        for q in range(nq):
            gq = g_ref[q // 2, :, (q % 2) * wd:(q % 2 + 1) * wd] if stacked else g_ref[:, q * wd:(q + 1) * wd]
            part = _dot_nt(gq, w_ref[q])
            acc = part if acc is None else acc + part
        o_ref[...] = acc.astype(o_ref.dtype)

    g_spec = (pl.BlockSpec((2, tm, 2 * wd), lambda i: (0, i, 0)) if stacked
              else pl.BlockSpec((tm, nq * wd), lambda i: (i, 0)))
    out, carried = _pcall(
        body, (g, w), name=name, grid=(M // tm,),
        in_specs=[g_spec, pl.BlockSpec((nq, K, wd), lambda i: (0, 0, 0))],
        out_specs=pl.BlockSpec((tm, K), lambda i: (i, 0)),
        out_shape=jax.ShapeDtypeStruct((M, K), out_dtype), semantics=("parallel",), riders=riders)
    return (out, carried) if riders else out


def _mm_tn_rs(name, a, g, to, tn=512):
    T, M = a.shape
    N = g.shape[1]
    tt = _row_tile(T, 1024)
    tn = min(tn, N)
    return _matmul(name, (a, g), grid=(M // to, N // tn, T // tt),
                   in_specs=[pl.BlockSpec((tt, to), lambda i, j, k: (k, i)),
                             pl.BlockSpec((tt, tn), lambda i, j, k: (k, j))],
                   o_spec=pl.BlockSpec((to, tn), lambda i, j, k: (i, j)),
                   out_shape=(M, N), out_dtype=F32, dims=TN)


def _mm_tn_cs(name, a, g, nq, to, riders=(), stacked=False):
    T, M = a.shape
    wd = g.shape[-1] * (2 if stacked else 1) // nq
    tt = _row_tile(T, 1024)
    g_spec = (pl.BlockSpec((None, tt, wd), lambda i, j, k: (j // 2, k, j % 2)) if stacked
              else pl.BlockSpec((tt, wd), lambda i, j, k: (k, j)))
    return _matmul(name, (a, g), grid=(M // to, nq, T // tt),
                   in_specs=[pl.BlockSpec((tt, to), lambda i, j, k: (k, i)), g_spec],
                   o_spec=pl.BlockSpec((None, to, wd), lambda i, j, k: (j, i, 0)),
                   out_shape=(nq, M, wd), out_dtype=F32, dims=TN, riders=riders)


def _rms_fwd(name, x, g):
    T, D = x.shape
    tm = _row_tile(T)

    def body(x_ref, g_ref, o_ref):
        xv = x_ref[...]
        r = lax.rsqrt(jnp.mean(xv * xv, axis=-1, keepdims=True) + EPS)
        o_ref[...] = (xv * r * g_ref[...]).astype(o_ref.dtype)

    return _pallas(
        body, name=name, grid=(T // tm,),
        in_specs=[pl.BlockSpec((tm, D), lambda i: (i, 0)), pl.BlockSpec((1, D), lambda i: (0, 0))],
        out_specs=pl.BlockSpec((tm, D), lambda i: (i, 0)),
        out_shape=jax.ShapeDtypeStruct((T, D), BF16), compiler_params=_cp("parallel"),
    )(x, g)


def _rms_bwd(name, x, g, dh, dres, riders=()):
    T, D = x.shape
    tm = _row_tile(T)
    has_res = dres is not None

    def body(*refs):
        if has_res:
            x_ref, g_ref, dh_ref, dr_ref, dx_ref, dg_ref = refs
        else:
            x_ref, g_ref, dh_ref, dx_ref, dg_ref = refs

        @pl.when(pl.program_id(0) == 0)
        def _():
            dg_ref[...] = jnp.zeros_like(dg_ref)

        xv = x_ref[...]
        r = lax.rsqrt(jnp.mean(xv * xv, axis=-1, keepdims=True) + EPS)
        n = xv * r
        dhv = dh_ref[...]
        dg_ref[...] += jnp.sum(dhv * n, axis=0, keepdims=True)
        dn = dhv * g_ref[...]
        dx = r * (dn - n * jnp.mean(dn * n, axis=-1, keepdims=True))
        if has_res:
            dx = dx + dr_ref[...]
        dx_ref[...] = dx

    row = pl.BlockSpec((tm, D), lambda i: (i, 0))
    vec = pl.BlockSpec((1, D), lambda i: (0, 0))
    ops = (x, g, dh, dres) if has_res else (x, g, dh)
    out, carried = _pcall(
        body, ops, name=name, grid=(T // tm,),
        in_specs=[row, vec, row] + ([row] if has_res else []),
        out_specs=(row, vec),
        out_shape=(jax.ShapeDtypeStruct((T, D), F32), jax.ShapeDtypeStruct((1, D), F32)),
        semantics=("arbitrary",), riders=riders)
    return (out, carried) if riders else out


def _loss_head(x2, tgt, g):
    T, D = x2.shape
    tm = _row_tile(T)

    def body(x_ref, t_ref, g_ref, dx_ref, dg_ref, loss_ref):
        @pl.when(pl.program_id(0) == 0)
        def _():
            dg_ref[...] = jnp.zeros_like(dg_ref)
            loss_ref[...] = jnp.zeros_like(loss_ref)

        xv = x_ref[...]
        gv = g_ref[...]
        r = lax.rsqrt(jnp.mean(xv * xv, axis=-1, keepdims=True) + EPS)
        n = xv * r
        diff = n * gv - t_ref[...]
        loss_ref[...] += 0.5 * jnp.sum(jnp.mean(diff * diff, axis=-1, keepdims=True))
        dy = diff * (1.0 / D)
        dg_ref[...] += jnp.sum(dy * n, axis=0, keepdims=True)
        dn = dy * gv
        dx_ref[...] = r * (dn - n * jnp.mean(dn * n, axis=-1, keepdims=True))

    row = pl.BlockSpec((tm, D), lambda i: (i, 0))
    vec = pl.BlockSpec((1, D), lambda i: (0, 0))
    return _pallas(
        body, name="loss_head", grid=(T // tm,),
        in_specs=[row, row, vec],
        out_specs=(row, vec, pl.BlockSpec((8, 128), lambda i: (0, 0))),
        out_shape=(jax.ShapeDtypeStruct((T, D), F32), jax.ShapeDtypeStruct((1, D), F32),
                   jax.ShapeDtypeStruct((8, 128), F32)),
        compiler_params=_cp("arbitrary"),
    )(x2, tgt, g)


def _gmlp_pieces(zu, zv, lng, lnb, ws_ref, bs_ref):
    u, du = _gelu_and_grad(zu)
    v, dv = _gelu_and_grad(zv)
    mu = jnp.mean(v, axis=-1, keepdims=True)
    vc = v - mu
    rstd = lax.rsqrt(jnp.mean(vc * vc, axis=-1, keepdims=True) + EPS)
    vhat = vc * rstd
    vn = vhat * lng + lnb
    row = lax.broadcasted_iota(jnp.int32, (GM_CHUNK, GM_CHUNK), 0)
    col = lax.broadcasted_iota(jnp.int32, (GM_CHUNK, GM_CHUNK), 1)
    tril = row >= col
    wms, mixed = [], []
    for g in range(GM_GROUPS):
        sl = slice(g * 128, (g + 1) * 128)
        wm = jnp.where(tril, ws_ref[g], 0.0)
        wms.append(wm)
        mixed.append(_dot(wm, vn[:, sl]) + bs_ref[g])
    return u, du, dv, rstd, vhat, vn, wms, mixed, tril


def _gmlp_fwd(proj, lng, lnb, ws, bs_col):
    T = proj.shape[0]
    n = T // GM_CHUNK

    def body(zu_ref, zv_ref, lng_ref, lnb_ref, ws_ref, bs_ref, o_ref):
        u, _, _, _, _, _, _, mixed, _ = _gmlp_pieces(zu_ref[...], zv_ref[...], lng_ref[...], lnb_ref[...],
                                                     ws_ref, bs_ref)
        for g in range(GM_GROUPS):
            sl = slice(g * 128, (g + 1) * 128)
            o_ref[:, sl] = (u[:, sl] * mixed[g]).astype(o_ref.dtype)

    vec = pl.BlockSpec((1, GM_WIDTH), lambda i: (0, 0))
    return _pallas(
        body, name="gmlp_fwd", grid=(n,),
        in_specs=[pl.BlockSpec((GM_CHUNK, 512), lambda i: (i, COL_ZU)),
                  pl.BlockSpec((GM_CHUNK, 512), lambda i: (i, COL_ZV)),
                  vec, vec,
                  pl.BlockSpec((GM_GROUPS, 128, 128), lambda i: (0, 0, 0)),
                  pl.BlockSpec((GM_GROUPS, 128, 1), lambda i: (0, 0, 0))],
        out_specs=pl.BlockSpec((GM_CHUNK, 512), lambda i: (i, 0)),
        out_shape=jax.ShapeDtypeStruct((T, GM_WIDTH), BF16), compiler_params=_cp("parallel"),
    )(proj, proj, lng, lnb, ws, bs_col)


def _gmlp_bwd(proj, d_out, lng, lnb, ws, bs_col):
    T = proj.shape[0]
    n = T // GM_CHUNK

    def body(zu_ref, zv_ref, do_ref, lng_ref, lnb_ref, ws_ref, bs_ref,
             dzu_ref, dzv_ref, dws_ref, dbs_ref, dlng_ref, dlnb_ref, dm_acc):
        i = pl.program_id(0)

        @pl.when(i == 0)
        def _():
            dws_ref[...] = jnp.zeros_like(dws_ref)
            dlng_ref[...] = jnp.zeros_like(dlng_ref)
            dlnb_ref[...] = jnp.zeros_like(dlnb_ref)
            dm_acc[...] = jnp.zeros_like(dm_acc)

        lng_v = lng_ref[...]
        u, du, dv, rstd, vhat, vn, wms, mixed, tril = _gmlp_pieces(zu_ref[...], zv_ref[...], lng_v, lnb_ref[...],
                                                                  ws_ref, bs_ref)
        do = do_ref[...]
        dvn_parts = []
        for g in range(GM_GROUPS):
            sl = slice(g * 128, (g + 1) * 128)
            dog = do[:, sl]
            dzu_ref[:, sl] = (dog * mixed[g] * du[:, sl]).astype(dzu_ref.dtype)
            dmix = dog * u[:, sl]
            dm_acc[:, sl] += dmix
            dws_ref[g] += jnp.where(tril, _dot_nt(dmix, vn[:, sl]), 0.0)
            dvn_parts.append(_dot_tn(wms[g], dmix))
        dvn = jnp.concatenate(dvn_parts, axis=1)
        dlng_ref[...] += jnp.sum(dvn * vhat, axis=0, keepdims=True)
        dlnb_ref[...] += jnp.sum(dvn, axis=0, keepdims=True)
        dvh = dvn * lng_v
        dvv = rstd * (dvh - jnp.mean(dvh, axis=-1, keepdims=True)
                      - vhat * jnp.mean(dvh * vhat, axis=-1, keepdims=True))
        dzv_ref[...] = (dvv * dv).astype(dzv_ref.dtype)

        @pl.when(i == n - 1)
        def _():
            for g in range(GM_GROUPS):
                dbs_ref[g] = jnp.sum(dm_acc[:, g * 128:(g + 1) * 128], axis=1, keepdims=True)

    vec = pl.BlockSpec((1, GM_WIDTH), lambda i: (0, 0))
    wsp = pl.BlockSpec((GM_GROUPS, 128, 128), lambda i: (0, 0, 0))
    bsp = pl.BlockSpec((GM_GROUPS, 128, 1), lambda i: (0, 0, 0))
    tile = pl.BlockSpec((GM_CHUNK, 512), lambda i: (i, 0))
    return _pallas(
        body, name="gmlp_bwd", grid=(n,),
        in_specs=[pl.BlockSpec((GM_CHUNK, 512), lambda i: (i, COL_ZU)),
                  pl.BlockSpec((GM_CHUNK, 512), lambda i: (i, COL_ZV)),
                  pl.BlockSpec((None, GM_CHUNK, 512), lambda i: (0, i, 0)), vec, vec, wsp, bsp],
        out_specs=(tile, tile, wsp, bsp, vec, vec),
        out_shape=(jax.ShapeDtypeStruct((T, GM_WIDTH), BF16), jax.ShapeDtypeStruct((T, GM_WIDTH), BF16),
                   jax.ShapeDtypeStruct((GM_GROUPS, 128, 128), F32), jax.ShapeDtypeStruct((GM_GROUPS, 128, 1), F32),
                   jax.ShapeDtypeStruct((1, GM_WIDTH), F32), jax.ShapeDtypeStruct((1, GM_WIDTH), F32)),
        scratch_shapes=[pltpu.VMEM((GM_CHUNK, GM_WIDTH), F32)],
        compiler_params=_cp("arbitrary"),
    )(proj, proj, d_out, lng, lnb, ws, bs_col)


def _hgrn_lower_bound(lbl):
    return 1.0 / (1.0 + jnp.exp(lbl[1:2, :] - lbl[0:1, :]))


def _hgrn_gates(hq, hf, lb):
    C = HG_CHUNK
    sg = _sigmoid(hf)
    fg = lb + (1.0 - lb) * sg
    sq = _sigmoid(hq)
    row = lax.broadcasted_iota(jnp.int32, (C, C), 0)
    col = lax.broadcasted_iota(jnp.int32, (C, C), 1)
    tril = row >= col
    logf = jnp.log(fg)
    a = _dot_01(tril, logf)
    a_last = jnp.sum(logf, axis=0, keepdims=True)
    first_half = lax.broadcasted_iota(jnp.int32, logf.shape, 0) < (C // 2)
    a_mid = jnp.sum(jnp.where(first_half, logf, 0.0), axis=0, keepdims=True)
    ea, ei, eki, ekl = jnp.exp(a), jnp.exp(a - a_mid), jnp.exp(a_mid - a), jnp.exp(a_last - a)
    k = 1.0 - fg
    q = hq * sq
    return dict(sg=sg, fg=fg, sq=sq, tril=tril, ea=ea, ei=ei, eki=eki, ekl=ekl, e_last=jnp.exp(a_last),
                qe=q * ea, qi=q * ei, ki=k * eki, kl=k * ekl)


def _heads(x):
    return [x[:, h * HG_DIM:(h + 1) * HG_DIM] for h in range(HG_HEADS)]


def _hgrn_fwd(proj, lbl, gh, B, S, riders=()):
    C = HG_CHUNK
    NC = S // C
    W = HG_HEADS * HG_DIM

    def body(q_ref, f_ref, i_ref, g_ref, lbl_ref, gh_ref, o_ref, bo_ref, st_ref, state):
        @pl.when(pl.program_id(0) == 0)
        def _():
            state[...] = jnp.zeros_like(state)

        lb = _hgrn_lower_bound(lbl_ref[...])
        ghv = gh_ref[...]
        for b in range(B):
            gt = _hgrn_gates(q_ref[b], f_ref[b], lb)
            v = _heads(i_ref[b])
            qe, qi, ki, kl, e_last = (_heads(gt[n]) for n in ("qe", "qi", "ki", "kl", "e_last"))
            outs, normed = [], []
            for h in range(HG_HEADS):
                p = jnp.where(gt["tril"], _dot_nt(qi[h], ki[h]), 0.0)
                st = state[b, h]
                st_ref[b, h] = st
                o = _dot_nt(qe[h], st) + _dot(p, v[h])
                state[b, h] = st * e_last[h] + _dot_tn(v[h], kl[h])
                outs.append(o)
                normed.append(o * lax.rsqrt(jnp.mean(o * o, axis=-1, keepdims=True) + EPS) * ghv)
            o_ref[b] = jnp.concatenate(outs, axis=1)
            hg = g_ref[b]
            bo_ref[b] = (jnp.concatenate(normed, axis=1) * (hg * _sigmoid(hg))).astype(bo_ref.dtype)

    def col(cb):
        return pl.BlockSpec((B, C, 512), lambda c: (0, c, cb))

    tile = pl.BlockSpec((B, C, W), lambda c: (0, c, 0))
    proj3 = proj.reshape(B, S, proj.shape[-1])
    out, carried = _pcall(
        body, (proj3, proj3, proj3, proj3, lbl, gh), name="hgrn_fwd", grid=(NC,),
        in_specs=[col(COL_HQ), col(COL_HF), col(COL_HI), col(COL_HG),
                  pl.BlockSpec((2, W), lambda c: (0, 0)), pl.BlockSpec((1, HG_DIM), lambda c: (0, 0))],
        out_specs=(tile, tile, pl.BlockSpec((B, None, HG_HEADS, 128, 128), lambda c: (0, c, 0, 0, 0))),
        out_shape=(jax.ShapeDtypeStruct((B, S, W), F32), jax.ShapeDtypeStruct((B, S, W), BF16),
                   jax.ShapeDtypeStruct((B, NC, HG_HEADS, 128, 128), F32)),
        scratch_shapes=[pltpu.VMEM((B, HG_HEADS, 128, 128), F32)],
        semantics=("arbitrary",), riders=riders)
    o_h, b_out, states = out
    out = (o_h, b_out.reshape(B * S, W), states)
    return (out, carried) if riders else out


def _hgrn_bwd(proj, o_saved, states, d_out, lbl, gh, B, S, riders=()):
    C = HG_CHUNK
    NC = S // C
    W = HG_HEADS * HG_DIM

    def body(q_ref, f_ref, i_ref, g_ref, o_ref, st_ref, do_ref, lbl_ref, gh_ref,
             dq_ref, df_ref, di_ref, dg_ref, dlbl_ref, dgh_ref, dstate, dlb_acc):
        c = pl.program_id(0)

        @pl.when(c == 0)
        def _():
            dstate[...] = jnp.zeros_like(dstate)
            dgh_ref[...] = jnp.zeros_like(dgh_ref)
            dlb_acc[...] = jnp.zeros_like(dlb_acc)

        lb = _hgrn_lower_bound(lbl_ref[...])
        ghv = gh_ref[...]
        row = lax.broadcasted_iota(jnp.int32, (C, C), 0)
        colm = lax.broadcasted_iota(jnp.int32, (C, C), 1)
        triu = colm >= row
        for b in range(B):
            hq, hg = q_ref[b], g_ref[b]
            gt = _hgrn_gates(hq, f_ref[b], lb)
            tril = gt["tril"]
            v = _heads(i_ref[b])
            qe, qi, ki, kl, e_last = (_heads(gt[n]) for n in ("qe", "qi", "ki", "kl", "e_last"))
            sgg = _sigmoid(hg)
            don_all = do_ref[b] * (hg * sgg)
            o, don = _heads(o_ref[b]), _heads(don_all)
            d_qe, d_qi, d_ki, d_kl, dv, n_all, dal = [], [], [], [], [], [], []
            for h in range(HG_HEADS):
                r = lax.rsqrt(jnp.mean(o[h] * o[h], axis=-1, keepdims=True) + EPS)
                n = o[h] * r
                n_all.append(n)
                dgh_ref[...] += jnp.sum(don[h] * n, axis=0, keepdims=True)
                dn = don[h] * ghv
                d_o = r * (dn - n * jnp.mean(dn * n, axis=-1, keepdims=True))
                st, dst = st_ref[b, h], dstate[b, h]
                p = jnp.where(tril, _dot_nt(qi[h], ki[h]), 0.0)
                dp = jnp.where(tril, _dot3(d_o, v[h], NT), 0.0)
                d_qe.append(_dot3(d_o, st, NN))
                d_qi.append(_dot3(dp, ki[h], NN))
                d_ki.append(_dot3(dp, qi[h], TN))
                d_kl.append(_dot3(v[h], dst, NN))
                dv.append(_dot_tn(p, d_o) + _dot_nt(kl[h], dst))
                dstate[b, h] = dst * e_last[h] + _dot3(d_o, qe[h], TN)
                dal.append(jnp.sum(dst * st, axis=0, keepdims=True) * e_last[h])
            d_qe, d_qi, d_ki, d_kl, n_all, dal = (jnp.concatenate(t, axis=1)
                                                  for t in (d_qe, d_qi, d_ki, d_kl, n_all, dal))
            dg_ref[b] = (do_ref[b] * n_all * jnp.tile(ghv, (1, HG_HEADS))
                         * (sgg * (1.0 + hg * (1.0 - sgg)))).astype(dg_ref.dtype)
            di_ref[b] = jnp.concatenate(dv, axis=1).astype(di_ref.dtype)
            d_a_last = dal + jnp.sum(d_kl * gt["kl"], axis=0, keepdims=True)
            dq = d_qe * gt["ea"] + d_qi * gt["ei"]
            dk = d_ki * gt["eki"] + d_kl * gt["ekl"]
            da = d_qe * gt["qe"] + d_qi * gt["qi"] - d_ki * gt["ki"] - d_kl * gt["kl"]
            dlogf = _dot_01(triu, da) + d_a_last
            sg, sq = gt["sg"], gt["sq"]
            dfg = dlogf / gt["fg"] - dk
            df_ref[b] = (dfg * (1.0 - lb) * sg * (1.0 - sg)).astype(df_ref.dtype)
            dlb_acc[...] += jnp.sum(dfg * (1.0 - sg), axis=0, keepdims=True)
            dq_ref[b] = (dq * (sq * (1.0 + hq * (1.0 - sq)))).astype(dq_ref.dtype)

        @pl.when(c == NC - 1)
        def _():
            dlb = dlb_acc[...]
            first = lax.broadcasted_iota(jnp.int32, (2, W), 0) == 0
            dlbl_ref[...] = jnp.where(first, dlb * lb * (1.0 - lb), -dlb * lb * (1.0 - lb))

    def col(cb):
        return pl.BlockSpec((B, C, 512), lambda c: (0, NC - 1 - c, cb))

    tile = pl.BlockSpec((B, C, W), lambda c: (0, NC - 1 - c, 0))
    proj3 = proj.reshape(B, S, proj.shape[-1])
    d3 = jax.ShapeDtypeStruct((B, S, W), BF16)
    out, carried = _pcall(
        body, (proj3, proj3, proj3, proj3, o_saved, states, d_out.reshape(3, B, S, W), lbl, gh), name="hgrn_bwd",
        grid=(NC,),
        in_specs=[col(COL_HQ), col(COL_HF), col(COL_HI), col(COL_HG), tile,
                  pl.BlockSpec((B, None, HG_HEADS, 128, 128), lambda c: (0, NC - 1 - c, 0, 0, 0)),
                  pl.BlockSpec((None, B, C, W), lambda c: (1, 0, NC - 1 - c, 0)),
                  pl.BlockSpec((2, W), lambda c: (0, 0)), pl.BlockSpec((1, HG_DIM), lambda c: (0, 0))],
        out_specs=(tile, tile, tile, tile,
                   pl.BlockSpec((2, W), lambda c: (0, 0)), pl.BlockSpec((1, HG_DIM), lambda c: (0, 0))),
        out_shape=(d3, d3, d3, d3, jax.ShapeDtypeStruct((2, W), F32), jax.ShapeDtypeStruct((1, HG_DIM), F32)),
        scratch_shapes=[pltpu.VMEM((B, HG_HEADS, 128, 128), F32), pltpu.VMEM((1, W), F32)],
        semantics=("arbitrary",), riders=riders)
    out = tuple(t.reshape(B * S, W) for t in out[:4]) + tuple(out[4:])
    return (out, carried) if riders else out


_XA_SCALE = XA_DIM ** -0.5


def _attn_probs(qh, kh):
    s = _dot_nt(qh, kh) * _XA_SCALE
    e = jnp.exp(s - jnp.max(s, axis=-1, keepdims=True))
    return e / jnp.sum(e, axis=-1, keepdims=True)


def _attn_fwd(proj, kv, B, S):
    T = B * S
    tq = _row_tile(S)
    nq = S // tq
    W = XA_HEADS * XA_DIM

    def body(q_ref, kv_ref, o_ref):
        for h in range(XA_HEADS):
            sl = slice(h * 128, (h + 1) * 128)
            p = _attn_probs(q_ref[:, sl], kv_ref[:, sl])
            o_ref[:, sl] = _dot(p, kv_ref[:, W + h * 128:W + (h + 1) * 128]).astype(o_ref.dtype)

    return _pallas(
        body, name="attn_fwd", grid=(B, nq),
        in_specs=[pl.BlockSpec((tq, 512), lambda b, i: (b * nq + i, COL_XQ)),
                  pl.BlockSpec((MEM_LEN, 2 * W), lambda b, i: (b, 0))],
        out_specs=pl.BlockSpec((tq, W), lambda b, i: (b * nq + i, 0)),
        out_shape=jax.ShapeDtypeStruct((T, W), BF16), compiler_params=_cp("parallel", "parallel"),
    )(proj, kv)


def _attn_bwd(proj, kv, d_out, B, S):
    T = B * S
    tq = _row_tile(S)
    nq = S // tq
    W = XA_HEADS * XA_DIM

    def body(q_ref, kv_ref, do_ref, dq_ref, dkv_ref):
        @pl.when(pl.program_id(1) == 0)
        def _():
            dkv_ref[...] = jnp.zeros_like(dkv_ref)

        for h in range(XA_HEADS):
            sl = slice(h * 128, (h + 1) * 128)
            slv = slice(W + h * 128, W + (h + 1) * 128)
            qh = q_ref[:, sl]
            kh = kv_ref[:, sl]
            p = _attn_probs(qh, kh)
            dc = do_ref[:, sl]
            dp = _dot_nt(dc, kv_ref[:, slv])
            ds = p * (dp - jnp.sum(dp * p, axis=-1, keepdims=True)) * _XA_SCALE
            dq_ref[:, sl] = _dot(ds, kh).astype(dq_ref.dtype)
            dkv_ref[:, sl] += _dot_tn(ds, qh)
            dkv_ref[:, slv] += _dot_tn(p, dc)

    kvspec = pl.BlockSpec((MEM_LEN, 2 * W), lambda b, i: (b, 0))
    tile = pl.BlockSpec((tq, W), lambda b, i: (b * nq + i, 0))
    return _pallas(
        body, name="attn_bwd", grid=(B, nq),
        in_specs=[pl.BlockSpec((tq, 512), lambda b, i: (b * nq + i, COL_XQ)), kvspec,
                  pl.BlockSpec((None, tq, W), lambda b, i: (2, b * nq + i, 0))],
        out_specs=(tile, kvspec),
        out_shape=(jax.ShapeDtypeStruct((T, W), BF16), jax.ShapeDtypeStruct((B * MEM_LEN, 2 * W), F32)),
        compiler_params=_cp("parallel", "arbitrary"),
    )(proj, kv, d_out)


_MERGE_TM = 256
_GATE_W = 512


def _gate_specs(tm):
    base = COL_GATE0 // _GATE_W
    return [pl.BlockSpec((tm, _GATE_W), functools.partial(lambda i, k: (i, base + k), k=k)) for k in range(6)]


def _merge_fwd(a_out, b_out, c_out, wb, proj):
    T = a_out.shape[0]
    tm = _row_tile(T, _MERGE_TM)
    nq, _, wd = wb.shape
    per_half = _GATE_W // wd

    def body(a_ref, b_ref, c_ref, w_ref, *rest):
        gates, (m_ref, up_ref) = rest[:6], rest[6:]
        for hf in range(2):
            cols = slice(hf * _GATE_W, (hf + 1) * _GATE_W)
            acc = None
            for n, br in enumerate((a_ref, b_ref, c_ref)):
                x = br[...]
                up = jnp.concatenate([_dot(x, w_ref[per_half * hf + j, n * BR_WIDTH:(n + 1) * BR_WIDTH, :])
                                      for j in range(per_half)], axis=1)
                up_ref[n, :, cols] = up.astype(up_ref.dtype)
                term = _sigmoid(gates[2 * n + hf][...]) * up
                acc = term if acc is None else acc + term
            m_ref[:, cols] = acc.astype(m_ref.dtype)

    br_spec = pl.BlockSpec((tm, BR_WIDTH), lambda i: (i, 0))
    return _pallas(
        body, name="merge_fwd", grid=(T // tm,),
        in_specs=[br_spec, br_spec, br_spec,
                  pl.BlockSpec((nq, 3 * BR_WIDTH, wd), lambda i: (0, 0, 0))] + _gate_specs(tm),
        out_specs=(pl.BlockSpec((tm, D_MODEL), lambda i: (i, 0)), pl.BlockSpec((3, tm, D_MODEL), lambda i: (0, i, 0))),
        out_shape=(jax.ShapeDtypeStruct((T, D_MODEL), BF16), jax.ShapeDtypeStruct((3, T, D_MODEL), BF16)),
        compiler_params=_cp("parallel"),
    )(a_out, b_out, c_out, wb, *([proj] * 6))


def _branch_bwd_act(d_ups, wb):
    _, T, D = d_ups.shape
    nq, _, wd = wb.shape
    tm = _row_tile(T)

    def body(d_ref, w_ref, o_ref):
        acc = None
        for q in range(nq):
            part = _dot_nt(d_ref[:, q * wd:(q + 1) * wd], w_ref[q])
            acc = part if acc is None else acc + part
        o_ref[...] = acc

    return _pallas(
        body, name="d_branch", grid=(3, T // tm),
        in_specs=[pl.BlockSpec((None, tm, D), lambda n, i: (n, i, 0)),
                  pl.BlockSpec((nq, BR_WIDTH, wd), lambda n, i: (0, n, 0))],
        out_specs=pl.BlockSpec((None, tm, BR_WIDTH), lambda n, i: (n, i, 0)),
        out_shape=jax.ShapeDtypeStruct((3, T, BR_WIDTH), F32), compiler_params=_cp("parallel", "parallel"),
    )(d_ups, wb)


def _branch_bwd_weight(name, br, d_ups, n):
    T = br.shape[0]
    D = d_ups.shape[2]
    wd = D // N_CHIPS
    tt = _row_tile(T, 1024)

    def body(b_ref, d_ref, o_ref):
        k = pl.program_id(0)
        for q in range(N_CHIPS):
            part = _dot_tn(b_ref[...], d_ref[:, q * wd:(q + 1) * wd])

            @pl.when(k == 0)
            def _():
                o_ref[q] = part

            @pl.when(k > 0)
            def _():
                o_ref[q] += part

    return _pallas(
        body, name=name, grid=(T // tt,),
        in_specs=[pl.BlockSpec((tt, BR_WIDTH), lambda k: (k, 0)),
                  pl.BlockSpec((None, tt, D), lambda k: (n, k, 0))],
        out_specs=pl.BlockSpec((N_CHIPS, BR_WIDTH, wd), lambda k: (0, 0, 0)),
        out_shape=jax.ShapeDtypeStruct((N_CHIPS, BR_WIDTH, wd), F32), compiler_params=_cp("arbitrary"),
    )(br, d_ups)


def _merge_bwd(d_merged, ups, proj):
    T = d_merged.shape[0]
    tm = _row_tile(T, _MERGE_TM)

    def body(dm_ref, up_ref, *rest):
        gates, (dup_ref, dg0_ref, dg1_ref, dg2_ref) = rest[:6], rest[6:]
        for hf in range(2):
            cols = slice(hf * _GATE_W, (hf + 1) * _GATE_W)
            dm = dm_ref[:, cols]
            for n, dgr in enumerate((dg0_ref, dg1_ref, dg2_ref)):
                gate = _sigmoid(gates[2 * n + hf][...])
                dup_ref[n, :, cols] = (dm * gate).astype(dup_ref.dtype)
                dgr[:, cols] = (dm * up_ref[n, :, cols].astype(F32) * gate * (1.0 - gate)).astype(dgr.dtype)

    tile = pl.BlockSpec((tm, D_MODEL), lambda i: (i, 0))
    tile3 = pl.BlockSpec((3, tm, D_MODEL), lambda i: (0, i, 0))
    return _pallas(
        body, name="merge_bwd", grid=(T // tm,),
        in_specs=[tile, tile3] + _gate_specs(tm),
        out_specs=(tile3, tile, tile, tile),
        out_shape=(jax.ShapeDtypeStruct((3, T, D_MODEL), BF16),) + (jax.ShapeDtypeStruct((T, D_MODEL), BF16),) * 3,
        compiler_params=_cp("parallel"),
    )(d_merged, ups, *([proj] * 6))


_CONV_TF = D_FF // 2
_CONV_TS = 256
_HALO = 8


def _conv_fwd(ab, cw, cb, B, S):
    T = B * S
    ts = _row_tile(S, _CONV_TS)
    tf = _CONV_TF
    nb = D_FF // tf
    tps = S // ts
    hb = ts // _HALO

    def body(a_ref, p_ref, b_ref, w_ref, cb_ref, o_ref):
        start = (pl.program_id(0) % tps) == 0
        prev = jnp.where(start, 0.0, p_ref[...])
        ext = jnp.concatenate([prev, a_ref[...]], axis=0)
        a1 = pltpu.roll(ext, 1, 0)[_HALO:, :]
        a2 = pltpu.roll(ext, 2, 0)[_HALO:, :]
        ac = cb_ref[...] + w_ref[0] * a2 + w_ref[1] * a1 + w_ref[2] * a_ref[...]
        o_ref[...] = (ac * _sigmoid(ac) * b_ref[...]).astype(o_ref.dtype)

    return _pallas(
        body, name="conv_fwd", grid=(T // ts, nb),
        in_specs=[pl.BlockSpec((ts, tf), lambda i, j: (i, j)),
                  pl.BlockSpec((_HALO, tf), lambda i, j: (jnp.maximum(i * hb - 1, 0), j)),
                  pl.BlockSpec((ts, tf), lambda i, j: (i, j + nb)),
                  pl.BlockSpec((3, 1, tf), lambda i, j: (0, 0, j)),
                  pl.BlockSpec((1, tf), lambda i, j: (0, j))],
        out_specs=pl.BlockSpec((ts, tf), lambda i, j: (i, j)),
        out_shape=jax.ShapeDtypeStruct((T, D_FF), BF16), compiler_params=_cp("parallel", "parallel"),
    )(ab, ab, ab, cw, cb)


def _conv_bwd(ab, d_ff, cw, cb, B, S):
    T = B * S
    ts = _row_tile(S, _CONV_TS)
    tf = _CONV_TF
    nb = D_FF // tf
    tps = S // ts
    hb = ts // _HALO
    last_h = T // _HALO - 1
    n_ext = ts + _HALO

    def body(a_ref, ap_ref, an_ref, b_ref, bn_ref, d_ref, dn_ref, w_ref, cb_ref, dab_ref, dw_ref, dcb_ref):
        i = pl.program_id(1)

        @pl.when(i == 0)
        def _():
            dw_ref[...] = jnp.zeros_like(dw_ref)
            dcb_ref[...] = jnp.zeros_like(dcb_ref)

        start = (i % tps) == 0
        end = (i % tps) == tps - 1
        a = a_ref[...]
        ext = jnp.concatenate([jnp.where(start, 0.0, ap_ref[...]), a, an_ref[...]], axis=0)
        r1 = pltpu.roll(ext, 1, 0)[_HALO:, :]
        r2 = pltpu.roll(ext, 2, 0)[_HALO:, :]
        ac = cb_ref[...] + w_ref[0] * r2 + w_ref[1] * r1 + w_ref[2] * ext[_HALO:, :]
        sg = _sigmoid(ac)
        d_e = jnp.concatenate([d_ref[...], jnp.where(end, 0.0, dn_ref[...])], axis=0)
        b_e = jnp.concatenate([b_ref[...], bn_ref[...]], axis=0)
        dab_ref[1] = (d_e[:ts, :] * (ac * sg)[:ts, :]).astype(dab_ref.dtype)
        dac = d_e * b_e * sg * (1.0 + ac * (1.0 - sg))
        u1 = pltpu.roll(dac, n_ext - 1, 0)[:ts, :]
        u2 = pltpu.roll(dac, n_ext - 2, 0)[:ts, :]
        dac0 = dac[:ts, :]
        dab_ref[0] = (w_ref[2] * dac0 + w_ref[1] * u1 + w_ref[0] * u2).astype(dab_ref.dtype)
        dcb_ref[...] += jnp.sum(dac0, axis=0, keepdims=True)
        dw_ref[2] += jnp.sum(dac0 * a, axis=0, keepdims=True)
        dw_ref[1] += jnp.sum(dac0 * r1[:ts, :], axis=0, keepdims=True)
        dw_ref[0] += jnp.sum(dac0 * r2[:ts, :], axis=0, keepdims=True)

    def cur(off):
        return pl.BlockSpec((ts, tf), lambda j, i: (i, j + off))

    def nxt(off):
        return pl.BlockSpec((_HALO, tf), lambda j, i: (jnp.minimum((i + 1) * hb, last_h), j + off))

    return _pallas(
        body, name="conv_bwd", grid=(nb, T // ts),
        in_specs=[cur(0), pl.BlockSpec((_HALO, tf), lambda j, i: (jnp.maximum(i * hb - 1, 0), j)), nxt(0),
                  cur(nb), nxt(nb), cur(0), nxt(0),
                  pl.BlockSpec((3, 1, tf), lambda j, i: (0, 0, j)), pl.BlockSpec((1, tf), lambda j, i: (0, j))],
        out_specs=(pl.BlockSpec((2, ts, tf), lambda j, i: (0, i, j)), pl.BlockSpec((3, 1, tf), lambda j, i: (0, 0, j)),
                   pl.BlockSpec((1, tf), lambda j, i: (0, j))),
        out_shape=(jax.ShapeDtypeStruct((2, T, D_FF), BF16),
                   jax.ShapeDtypeStruct((3, 1, D_FF), F32), jax.ShapeDtypeStruct((1, D_FF), F32)),
        compiler_params=_cp("parallel", "arbitrary"),
    )(ab, ab, ab, ab, ab, d_ff, d_ff, cw, cb)


def _local_step(x, mem, tgt, p, comm, B, S):
    g = {}
    h = _rms_fwd("norm1", x, p["norm1_g"])
    proj = comm.carry("in_proj", lambda r: _mm_cs("in_proj", h, comm.w("w_in"), F32, riders=r))
    a_out = _gmlp_fwd(proj, p["ln_v_g"], p["ln_v_b"], p["w_spatial"], p["b_spatial"])
    o_h, b_out, states = comm.carry(
        "hgrn_fwd", lambda r: _hgrn_fwd(proj, p["lb_logits"], p["hgrn_norm_g"], B, S, riders=r))
    memn = _rms_fwd("mem_norm", mem, p["mem_norm_g"])
    kv = _mm_rs("mem_kv", memn, comm.w("w_mem_kv"), F32)
    c_out = _attn_fwd(proj, kv, B, S)
    merged, ups = _merge_fwd(a_out, b_out, c_out, comm.w("w_branch"), proj)
    x1 = _mm_rs("out_proj", merged, comm.w("w_out"), F32, res=x)
    h2 = _rms_fwd("norm2", x1, p["norm2_g"])
    ab = _mm_cs("up_proj", h2, comm.w("w_up"), F32)
    conv_w = comm.w("conv_w")
    ff = _conv_fwd(ab, conv_w, p["conv_b"], B, S)
    x2 = _mm_rs("down_proj", ff, comm.w("w_down"), F32, res=x1)
    dx2, g["final_g"], loss = _loss_head(x2, tgt, p["final_g"])

    comm.grad("w_down", _mm_tn_rs("g_w_down", ff, dx2, to=D_FF // 2))
    d_ff = _mm_nt_rs("d_ff", dx2, comm.w("w_down"), F32, to=D_FF // 2)
    d_ab, g["conv_w"], g["conv_b"] = _conv_bwd(ab, d_ff, conv_w, p["conv_b"], B, S)
    comm.grad("w_up", _mm_tn_cs("g_w_up", h2, d_ab, N_CHIPS, to=512, stacked=True))
    d_h2 = comm.carry("d_h2", lambda r: _mm_nt_cs("d_h2", d_ab, comm.w("w_up"), F32, riders=r, stacked=True))
    d_x1, g["norm2_g"] = _rms_bwd("norm2_bwd", x1, p["norm2_g"], d_h2, dx2)
    comm.grad("w_out", _mm_tn_rs("g_w_out", merged, d_x1, to=512))
    d_merged = _mm_nt_rs("d_merged", d_x1, comm.w("w_out"), F32, to=512)
    d_ups, d_g0, d_g1, d_g2 = _merge_bwd(d_merged, ups, proj)

    d_br = _branch_bwd_act(d_ups, comm.w("w_branch"))
    comm.grad("w_branch", jnp.concatenate(
        [_branch_bwd_weight("g_w_branch%d" % n, br, d_ups, n) for n, br in enumerate((a_out, b_out, c_out))],
        axis=1))

    d_zu, d_zv, g["w_spatial"], g["b_spatial"], g["ln_v_g"], g["ln_v_b"] = _gmlp_bwd(
        proj, d_br, p["ln_v_g"], p["ln_v_b"], p["w_spatial"], p["b_spatial"])
    d_xq, d_kv = _attn_bwd(proj, kv, d_br, B, S)
    comm.grad("w_mem_kv", _mm_tn_rs("g_w_mem_kv", memn, d_kv, to=512))
    d_memn = _mm_nt_rs("d_memn", d_kv, comm.w("w_mem_kv"), F32, to=512)
    _, g["mem_norm_g"] = _rms_bwd("mem_norm_bwd", mem, p["mem_norm_g"], d_memn, None)
    d_hq, d_hf, d_hi, d_hg, g["lb_logits"], g["hgrn_norm_g"] = comm.carry(
        "hgrn_bwd", lambda r: _hgrn_bwd(proj, o_h, states, d_br, p["lb_logits"], p["hgrn_norm_g"], B, S, riders=r))
    d_proj = jnp.concatenate([d_zu, d_zv, d_hq, d_hf, d_hi, d_hg, d_xq, d_g0, d_g1, d_g2], axis=1)
    comm.grad("w_in", comm.carry("g_w_in", lambda r: _mm_tn_cs("g_w_in", h, d_proj, N_CHIPS, to=512, riders=r)))
    comm.small_grads([g[n].reshape(_SMALL_SHAPE[n]) for n in _SMALL_EARLY] + [loss])
    d_h = comm.carry("d_h", lambda r: _mm_nt_cs("d_h", d_proj, comm.w("w_in"), F32, riders=r))
    grad_x, g["norm1_g"] = _rms_bwd("norm1_bwd", x, p["norm1_g"], d_h, d_x1)
    return loss, grad_x, g


HBM_SPEC = pl.BlockSpec(memory_space=pltpu.HBM)


def _place():
    x, y, c = lax.axis_index("x"), lax.axis_index("y"), lax.axis_index("c")
    other_chips = [(1 - x, y), (x, 1 - y), (1 - x, 1 - y)]
    return x, y, c, other_chips


def _remote(src, dst, send_sem, recv_sem, dev):
    return pltpu.make_async_remote_copy(src_ref=src, dst_ref=dst, send_sem=send_sem, recv_sem=recv_sem,
                                        device_id=dev, device_id_type=MESH_ID)


class _Exchange:
    def __init__(self, operands, out_shape, aliases, scratch, start, finish):
        self.operands, self.out_shape, self.aliases, self.scratch = operands, out_shape, aliases, scratch
        self.start, self.finish = start, finish


def _run_exchanges(name, exs):
    n_in = [len(ex.operands) for ex in exs]
    n_out = [len(ex.out_shape) for ex in exs]
    n_scr = [len(ex.scratch) for ex in exs]

    def body(*refs):
        ins, outs, scr = refs[:sum(n_in)], refs[sum(n_in):sum(n_in) + sum(n_out)], refs[sum(n_in) + sum(n_out):]
        parts, oi, oo, os_ = [], 0, 0, 0
        for k in range(len(exs)):
            parts.append((ins[oi:oi + n_in[k]], outs[oo:oo + n_out[k]], scr[os_:os_ + n_scr[k]]))
            oi, oo, os_ = oi + n_in[k], oo + n_out[k], os_ + n_scr[k]
        for ex, part in zip(exs, parts):
            ex.start(*part)
        for ex, part in zip(exs, parts):
            ex.finish(*part)

    aliases, ops, shapes, scratch, oi, oo = {}, [], [], [], 0, 0
    for k, ex in enumerate(exs):
        aliases.update({oi + a: oo + b for a, b in ex.aliases.items()})
        oi, oo = oi + n_in[k], oo + n_out[k]
        ops += list(ex.operands)
        shapes += [pltpu.HBM(s.shape, s.dtype) for s in ex.out_shape]
        scratch += list(ex.scratch)
    res = _pallas(
        body, name=name, in_specs=[HBM_SPEC] * len(ops), out_specs=(HBM_SPEC,) * len(shapes), out_shape=tuple(shapes),
        input_output_aliases=aliases, scratch_shapes=scratch,
    )(*ops)
    out, oo = [], 0
    for k in range(len(exs)):
        out.append(list(res[oo:oo + n_out[k]]))
        oo += n_out[k]
    return out


def _ex_all_gather(slabs, halved):
    n = len(slabs)

    def rows(a, cc):
        if not halved[a]:
            return slice(None)
        hr = slabs[a].shape[1] // 2
        return pl.ds(cc * hr, hr)

    def ici(bufs, scr, a, j, chip, c, mine):
        px, py = chip
        x, y, _, _ = _place()
        qs = 2 * x + y if mine else 2 * px + py
        piece = bufs[a].at[qs, rows(a, c)]
        return _remote(piece, piece, scr[0].at[3 * a + j], scr[1].at[3 * a + j], (px, py, c))

    def d2d(bufs, scr, a, j, chip, cc):
        px, py = chip
        x, y, c, _ = _place()
        piece = bufs[a].at[2 * px + py, rows(a, cc)]
        return _remote(piece, piece, scr[2].at[3 * a + j], scr[3].at[3 * a + j], (x, y, 1 - c))

    def start(ins, outs, scr):
        _, _, c, chips = _place()
        for j, chip in enumerate(chips):
            for a in range(n):
                ici(outs, scr, a, j, chip, c, True).start()

    def finish(ins, outs, scr):
        _, _, c, chips = _place()
        for j, chip in enumerate(chips):
            for a in range(n):
                ici(outs, scr, a, j, chip, c, False).wait_recv()
                if halved[a]:
                    d2d(outs, scr, a, j, chip, c).start()
        for j, chip in enumerate(chips):
            for a in range(n):
                if halved[a]:
                    d2d(outs, scr, a, j, chip, 1 - c).wait_recv()
        for j, chip in enumerate(chips):
            for a in range(n):
                ici(outs, scr, a, j, chip, c, True).wait_send()
                if halved[a]:
                    d2d(outs, scr, a, j, chip, c).wait_send()

    return _Exchange(list(slabs), [jax.ShapeDtypeStruct(s.shape, s.dtype) for s in slabs],
                     {a: a for a in range(n)}, [pltpu.SemaphoreType.DMA((3 * n,))] * 4, start, finish)


def _ex_to_sibling(grads):
    n = len(grads)

    def copy(ins, outs, scr, a):
        x, y, c, _ = _place()
        hr = grads[a].shape[1] // 2
        return _remote(ins[a].at[:, pl.ds((1 - c) * hr, hr), :], outs[a], scr[0].at[a], scr[1].at[a], (x, y, 1 - c))

    def start(ins, outs, scr):
        for a in range(n):
            copy(ins, outs, scr, a).start()

    def finish(ins, outs, scr):
        for a in range(n):
            copy(ins, outs, scr, a).wait()

    out_shape = [jax.ShapeDtypeStruct((g.shape[0], g.shape[1] // 2, g.shape[2]), g.dtype) for g in grads]
    return _Exchange(list(grads), out_shape, {}, [pltpu.SemaphoreType.DMA((n,))] * 2, start, finish)


def _ex_to_owner(parts):
    n = len(parts)

    def copy(ins, outs, scr, a, j, chip):
        _, _, c, _ = _place()
        px, py = chip
        return _remote(ins[a].at[2 * px + py], outs[a].at[j], scr[0].at[3 * a + j], scr[1].at[3 * a + j],
                       (px, py, c))

    def start(ins, outs, scr):
        for j, chip in enumerate(_place()[3]):
            for a in range(n):
                copy(ins, outs, scr, a, j, chip).start()

    def finish(ins, outs, scr):
        for j, chip in enumerate(_place()[3]):
            for a in range(n):
                copy(ins, outs, scr, a, j, chip).wait()

    out_shape = [jax.ShapeDtypeStruct((3,) + p.shape[1:], p.dtype) for p in parts]
    return _Exchange(list(parts), out_shape, {}, [pltpu.SemaphoreType.DMA((3 * n,))] * 2, start, finish)


def _ex_share_halves(bufs):
    n = len(bufs)

    def copy(outs, scr, a, cc):
        x, y, c, _ = _place()
        hr = bufs[a].shape[0] // 2
        piece = outs[a].at[pl.ds(cc * hr, hr), :]
        return _remote(piece, piece, scr[0].at[a], scr[1].at[a], (x, y, 1 - c))

    def start(ins, outs, scr):
        c = _place()[2]
        for a in range(n):
            copy(outs, scr, a, c).start()

    def finish(ins, outs, scr):
        c = _place()[2]
        for a in range(n):
            copy(outs, scr, a, c).wait_send()
            copy(outs, scr, a, 1 - c).wait_recv()

    return _Exchange(list(bufs), [jax.ShapeDtypeStruct(b.shape, b.dtype) for b in bufs], {a: a for a in range(n)},
                     [pltpu.SemaphoreType.DMA((n,))] * 2, start, finish)


def _ex_gather_small(arrs):
    n = len(arrs)

    def peer_of(m):
        x, y, c, _ = _place()
        return (1 - x if m & 4 else x, 1 - y if m & 2 else y, 1 - c if m & 1 else c)

    def start(ins, outs, scr):
        x, y, c, _ = _place()
        for m in range(1, N_DEV):
            for a in range(n):
                k = (N_DEV - 1) * a + m - 1
                _remote(ins[a], outs[a].at[4 * x + 2 * y + c], scr[0].at[k], scr[1].at[k], peer_of(m)).start()

    def finish(ins, outs, scr):
        for m in range(1, N_DEV):
            px, py, pc = peer_of(m)
            for a in range(n):
                k = (N_DEV - 1) * a + m - 1
                slot = outs[a].at[4 * px + 2 * py + pc]
                cp = _remote(ins[a], slot, scr[0].at[k], scr[1].at[k], (px, py, pc))
                cp.wait_send()
                cp.wait_recv()

    slots = [jnp.zeros((N_DEV,) + a.shape, a.dtype) for a in arrs]
    out_shape = [jax.ShapeDtypeStruct(s.shape, s.dtype) for s in slots]
    return _Exchange(list(arrs) + slots, out_shape, {n + a: a for a in range(n)},
                     [pltpu.SemaphoreType.DMA(((N_DEV - 1) * n,))] * 2, start, finish)


def _div_tile(n, want):
    best = None
    for t in range(8, min(n, want) + 1, 8):
        if n % t == 0:
            best = t
    assert best is not None, n
    return best


def _cast_into_slab(name, w, place, dtype):
    r, cc = w.shape
    tr = r if r * cc <= 128 * 1024 else _div_tile(r, 256)

    def body(s_ref, w_ref, o_ref):
        o_ref[...] = w_ref[...].astype(o_ref.dtype)

    return _pallas(
        body, name=name,
        grid_spec=pltpu.PrefetchScalarGridSpec(
            num_scalar_prefetch=1, grid=(r // tr,),
            in_specs=[pl.BlockSpec((tr, cc), lambda i, s: (i, 0))],
            out_specs=pl.BlockSpec((None, tr, cc), lambda i, s: (s[0], i, 0))),
        out_shape=jax.ShapeDtypeStruct((N_CHIPS, r, cc), dtype), compiler_params=_cp("parallel"),
    )(place, w)


def _add_half(name, g, rcv, place):
    nq, r, cc = g.shape
    hr = r // 2

    def body(s_ref, g_ref, r_ref, o_ref):
        o_ref[...] = (g_ref[...] + r_ref[...]).astype(o_ref.dtype)

    spec = pl.BlockSpec((None, hr, cc), lambda i, s: (i, 0, 0))
    return _pallas(
        body, name=name,
        grid_spec=pltpu.PrefetchScalarGridSpec(
            num_scalar_prefetch=1, grid=(nq,),
            in_specs=[pl.BlockSpec((None, hr, cc), lambda i, s: (i, s[1], 0)), spec], out_specs=spec),
        out_shape=jax.ShapeDtypeStruct((nq, hr, cc), BF16), compiler_params=_cp("parallel"),
    )(place, g, rcv)


def _sum_owner(name, part, rcv, place):
    _, hr, cc = part.shape
    tr = _div_tile(hr, 128)
    nb = hr // tr

    def body(s_ref, p_ref, r_ref, o_ref):
        o_ref[...] = ((p_ref[...].astype(F32) + r_ref[0].astype(F32)) + r_ref[1].astype(F32)) + r_ref[2].astype(F32)

    return _pallas(
        body, name=name,
        grid_spec=pltpu.PrefetchScalarGridSpec(
            num_scalar_prefetch=1, grid=(nb,),
            in_specs=[pl.BlockSpec((None, tr, cc), lambda i, s: (s[0], i, 0)),
                      pl.BlockSpec((3, tr, cc), lambda i, s: (0, i, 0))],
            out_specs=pl.BlockSpec((tr, cc), lambda i, s: (s[1] * nb + i, 0))),
        out_shape=jax.ShapeDtypeStruct((2 * hr, cc), F32), compiler_params=_cp("parallel"),
    )(place, part, rcv)


def _sum_small(gathered, local, place):
    n = len(gathered)

    def body(s_ref, *refs):
        g_refs, l_refs, o_refs = refs[:n], refs[n:2 * n], refs[2 * n:]
        me = s_ref[2]
        for g_ref, l_ref, o_ref in zip(g_refs, l_refs, o_refs):
            acc = None
            for d in range(N_DEV):
                term = jnp.where(me == d, l_ref[...], g_ref[d])
                acc = term if acc is None else acc + term
            o_ref[...] = acc

    def whole(shape):
        return pl.BlockSpec(shape, lambda i, s, nd=len(shape): (0,) * nd)

    return _pallas(
        body, name="sum_small",
        grid_spec=pltpu.PrefetchScalarGridSpec(
            num_scalar_prefetch=1, grid=(1,),
            in_specs=[whole(g.shape) for g in gathered] + [whole(a.shape) for a in local],
            out_specs=tuple(whole(a.shape) for a in local)),
        out_shape=tuple(jax.ShapeDtypeStruct(a.shape, a.dtype) for a in local), compiler_params=_cp("arbitrary"),
    )(place, *gathered, *local)


def _adamw(name, w, g, m, v):
    r, cc = w.shape
    tr = r if r * cc <= 128 * 1024 else _div_tile(r, 256)

    def body(w_ref, g_ref, m_ref, v_ref, d_ref, mo_ref, vo_ref):
        gv = g_ref[...]
        mn = ADAM_B1 * m_ref[...] + (1.0 - ADAM_B1) * gv
        vn = ADAM_B2 * v_ref[...] + (1.0 - ADAM_B2) * (gv * gv)
        m_hat = mn / (1.0 - ADAM_B1 ** ADAM_STEP)
        v_hat = vn / (1.0 - ADAM_B2 ** ADAM_STEP)
        d_ref[...] = -ADAM_LR * (m_hat / (jnp.sqrt(v_hat) + ADAM_EPS) + ADAM_WD * w_ref[...])
        mo_ref[...] = mn
        vo_ref[...] = vn

    spec = pl.BlockSpec((tr, cc), lambda i: (i, 0))
    sd = jax.ShapeDtypeStruct((r, cc), F32)
    return _pallas(
        body, name=name, grid=(r // tr,), in_specs=[spec] * 4, out_specs=(spec,) * 3, out_shape=(sd,) * 3,
        compiler_params=_cp("parallel"),
    )(w, g, m, v)


_BIG = ("w_in", "w_up", "w_branch", "w_mem_kv", "w_out", "w_down")
_BIG_SHARD_SHAPE = {"w_in": (1024, 1664), "w_up": (1024, 1408), "w_branch": (1536, 256),
                    "w_mem_kv": (256, 1024), "w_out": (256, 1024), "w_down": (704, 1024)}
_SMALL_SHAPE = {"norm1_g": (1, D_MODEL), "ln_v_g": (1, GM_WIDTH), "ln_v_b": (1, GM_WIDTH),
                "w_spatial": (GM_GROUPS * GM_CHUNK, GM_CHUNK), "b_spatial": (GM_GROUPS, GM_CHUNK),
                "lb_logits": (2, HG_HEADS * HG_DIM), "hgrn_norm_g": (1, HG_DIM), "mem_norm_g": (1, D_MODEL),
                "norm2_g": (1, D_MODEL), "conv_w": (3, D_FF), "conv_b": (1, D_FF), "final_g": (1, D_MODEL)}
_SMALL_EARLY = tuple(n for n in _SMALL_SHAPE if n != "norm1_g")
_PARAM_ORDER = ("norm1_g", "w_in", "ln_v_g", "ln_v_b", "w_spatial", "b_spatial", "lb_logits", "hgrn_norm_g",
                "mem_norm_g", "w_mem_kv", "w_branch", "w_out", "norm2_g", "w_up", "conv_w", "conv_b", "w_down",
                "final_g")


def _adamw_small(ws, gs, ms, vs):
    n = len(ws)

    def body(*refs):
        w_refs, g_refs, m_refs, v_refs = refs[:n], refs[n:2 * n], refs[2 * n:3 * n], refs[3 * n:4 * n]
        d_refs, mo_refs, vo_refs = refs[4 * n:5 * n], refs[5 * n:6 * n], refs[6 * n:]
        for k in range(n):
            gv = g_refs[k][...]
            mn = ADAM_B1 * m_refs[k][...] + (1.0 - ADAM_B1) * gv
            vn = ADAM_B2 * v_refs[k][...] + (1.0 - ADAM_B2) * (gv * gv)
            m_hat = mn / (1.0 - ADAM_B1 ** ADAM_STEP)
            v_hat = vn / (1.0 - ADAM_B2 ** ADAM_STEP)
            d_refs[k][...] = -ADAM_LR * (m_hat / (jnp.sqrt(v_hat) + ADAM_EPS) + ADAM_WD * w_refs[k][...])
            mo_refs[k][...] = mn
            vo_refs[k][...] = vn

    specs = [pl.BlockSpec(a.shape, lambda i: (0, 0)) for a in ws]
    shapes = tuple(jax.ShapeDtypeStruct(a.shape, F32) for a in ws)
    res = _pallas(
        body, name="adamw_small", grid=(1,), in_specs=specs * 4, out_specs=tuple(specs * 3), out_shape=shapes * 3,
        compiler_params=_cp("arbitrary"),
    )(*ws, *gs, *ms, *vs)
    return res[:n], res[n:2 * n], res[2 * n:]


class _Comm:
    _ROW_SHARDED = ("w_mem_kv", "w_out", "w_down")

    def __init__(self, slabs, place):
        self.slabs, self.place = slabs, place
        self.full, self.raw, self.parts, self.bufs, self.done = {}, {}, {}, {}, {}
        ex, deliver = self._gather(["w_in"])
        deliver(_run_exchanges("all_gather_w_in", [ex])[0])

    def w(self, name):
        a = self.full[name]
        if name in self._ROW_SHARDED:
            return a.reshape(-1, a.shape[-1])
        if name == "conv_w":
            return jnp.transpose(a, (1, 0, 2)).reshape(3, 1, D_FF)
        return a

    def grad(self, name, arr):
        self.raw[name] = arr.reshape((N_CHIPS,) + _BIG_SHARD_SHAPE[name])
        if name == "w_in":
            ex, deliver = self._to_sibling(["w_in"])
            deliver(_run_exchanges("rs_sibling_w_in", [ex])[0])

    def small_grads(self, arrays):
        self.small_local = list(arrays)

    def carry(self, tag, call):
        plan = self._plan(tag)
        if not plan:
            return call(())
        out, carried = call([ex for ex, _ in plan])
        for (_, deliver), res in zip(plan, carried):
            deliver(res)
        return out

    def finish(self, last_small):
        ex, deliver = self._share(["w_out", "w_branch", "w_mem_kv", "w_in"])
        shared, small = _run_exchanges("share_and_gather_last", [ex, _ex_gather_small(last_small)])
        deliver(shared)
        return self.done, self.small_local + list(last_small), self.small_everyone + small

    def _plan(self, tag):
        if tag == "in_proj":
            return [self._gather(["w_branch", "w_out", "w_mem_kv", "w_down", "conv_w"])]
        if tag == "hgrn_fwd":
            return [self._gather(["w_up"])]
        if tag == "d_h2":
            return [self._to_sibling(["w_down", "w_up"])]
        if tag == "hgrn_bwd":
            return [self._to_owner(["w_down", "w_up"]), self._to_sibling(["w_out", "w_branch", "w_mem_kv"])]
        if tag == "g_w_in":
            return [self._to_owner(["w_out", "w_branch", "w_mem_kv"]), self._share(["w_down", "w_up"])]
        if tag == "d_h":
            def keep(res):
                self.small_everyone = res

            return [self._to_owner(["w_in"]), (_ex_gather_small(self.small_local), keep)]
        return []

    def _gather(self, names):
        ex = _ex_all_gather([self.slabs[n] for n in names], [n != "conv_w" for n in names])
        return ex, lambda res: self.full.update(zip(names, res))

    def _to_sibling(self, names):
        def deliver(res):
            for n, r in zip(names, res):
                self.parts[n] = _add_half("rs_add_" + n, self.raw[n], r, self.place)

        return _ex_to_sibling([self.raw[n] for n in names]), deliver

    def _to_owner(self, names):
        def deliver(res):
            for n, r in zip(names, res):
                self.bufs[n] = _sum_owner("rs_sum_" + n, self.parts[n], r, self.place)

        return _ex_to_owner([self.parts[n] for n in names]), deliver

    def _share(self, names):
        return _ex_share_halves([self.bufs[n] for n in names]), lambda res: self.done.update(zip(names, res))


def kernel(x, mem, norm1_g, w_in, ln_v_g, ln_v_b, w_spatial, b_spatial, lb_logits, hgrn_norm_g, mem_norm_g, w_mem_kv, w_branch, w_out, norm2_g, w_up, conv_w, conv_b, w_down, final_g, loss_target, m_norm1_g, m_w_in, m_ln_v_g, m_ln_v_b, m_w_spatial, m_b_spatial, m_lb_logits, m_hgrn_norm_g, m_mem_norm_g, m_w_mem_kv, m_w_branch, m_w_out, m_norm2_g, m_w_up, m_conv_w, m_conv_b, m_w_down, m_final_g, v_norm1_g, v_w_in, v_ln_v_g, v_ln_v_b, v_w_spatial, v_b_spatial, v_lb_logits, v_hgrn_norm_g, v_mem_norm_g, v_w_mem_kv, v_w_branch, v_w_out, v_norm2_g, v_w_up, v_conv_w, v_conv_b, v_w_down, v_final_g):
    w = dict(norm1_g=norm1_g, w_in=w_in, ln_v_g=ln_v_g, ln_v_b=ln_v_b, w_spatial=w_spatial, b_spatial=b_spatial,
             lb_logits=lb_logits, hgrn_norm_g=hgrn_norm_g, mem_norm_g=mem_norm_g, w_mem_kv=w_mem_kv,
             w_branch=w_branch, w_out=w_out, norm2_g=norm2_g, w_up=w_up, conv_w=conv_w, conv_b=conv_b,
             w_down=w_down, final_g=final_g)
    mom = dict(norm1_g=m_norm1_g, w_in=m_w_in, ln_v_g=m_ln_v_g, ln_v_b=m_ln_v_b, w_spatial=m_w_spatial,
               b_spatial=m_b_spatial, lb_logits=m_lb_logits, hgrn_norm_g=m_hgrn_norm_g, mem_norm_g=m_mem_norm_g,
               w_mem_kv=m_w_mem_kv, w_branch=m_w_branch, w_out=m_w_out, norm2_g=m_norm2_g, w_up=m_w_up,
               conv_w=m_conv_w, conv_b=m_conv_b, w_down=m_w_down, final_g=m_final_g)
    var = dict(norm1_g=v_norm1_g, w_in=v_w_in, ln_v_g=v_ln_v_g, ln_v_b=v_ln_v_b, w_spatial=v_w_spatial,
               b_spatial=v_b_spatial, lb_logits=v_lb_logits, hgrn_norm_g=v_hgrn_norm_g, mem_norm_g=v_mem_norm_g,
               w_mem_kv=v_w_mem_kv, w_branch=v_w_branch, w_out=v_w_out, norm2_g=v_norm2_g, w_up=v_w_up,
               conv_w=v_conv_w, conv_b=v_conv_b, w_down=v_w_down, final_g=v_final_g)
    B, S, D = x.shape
    T = B * S
    ci = lax.axis_index("c")
    q = 2 * lax.axis_index("x") + lax.axis_index("y")
    place = jnp.stack([q, ci, 2 * q + ci]).astype(jnp.int32)

    slabs = {n: _cast_into_slab("slab_" + n, w[n].reshape(_BIG_SHARD_SHAPE[n]), place, BF16) for n in _BIG}
    slabs["conv_w"] = _cast_into_slab("slab_conv_w", conv_w[0], place, F32)
    comm = _Comm(slabs, place)
    p = dict(
        norm1_g=norm1_g, ln_v_g=ln_v_g, ln_v_b=ln_v_b, w_spatial=w_spatial[0],
        b_spatial=b_spatial.reshape(GM_GROUPS, GM_CHUNK, 1), lb_logits=lb_logits, hgrn_norm_g=hgrn_norm_g,
        mem_norm_g=mem_norm_g, norm2_g=norm2_g, conv_b=conv_b, final_g=final_g.reshape(1, D))

    loss, grad_x, g = _local_step(x.reshape(T, D), mem.reshape(B * MEM_LEN, D), loss_target.reshape(T, D), p, comm,
                                  B, S)

    shard_grads, local_small, everyone = comm.finish([g["norm1_g"]])
    summed = _sum_small(everyone, local_small, place)
    small_names = list(_SMALL_EARLY) + ["norm1_g"]
    total = dict(zip(_SMALL_EARLY, summed))
    loss_total, total["norm1_g"] = summed[len(_SMALL_EARLY)][0, 0], summed[-1]

    grads, delta, new_m, new_v = {}, {}, {}, {}
    for n in _BIG:
        shp = _BIG_SHARD_SHAPE[n]
        grads[n] = shard_grads[n]
        delta[n], new_m[n], new_v[n] = _adamw("adamw_" + n, w[n].reshape(shp), shard_grads[n],
                                              mom[n].reshape(shp), var[n].reshape(shp))
    cw_shard = D_FF // N_CHIPS
    total["conv_w"] = lax.dynamic_slice(total["conv_w"], (0, q * cw_shard), (3, cw_shard))

    def flat2d(d, n):
        return d[n].reshape(total[n].shape)

    upd = _adamw_small([flat2d(w, n) for n in small_names], [total[n] for n in small_names],
                       [flat2d(mom, n) for n in small_names], [flat2d(var, n) for n in small_names])
    for k, n in enumerate(small_names):
        grads[n], delta[n], new_m[n], new_v[n] = total[n], upd[0][k], upd[1][k], upd[2][k]

    def shaped(d):
        return [d[n].reshape(w[n].shape) for n in _PARAM_ORDER]

    return (loss_total, grad_x.reshape(B, S, D), *shaped(grads), *shaped(delta), *shaped(new_m), *shaped(new_v))
```

```python
import functools
import math

import jax
import jax.numpy as jnp
from jax import lax
from jax.experimental import pallas as pl
from jax.experimental.pallas import tpu as pltpu

F32 = jnp.float32
BF16 = jnp.bfloat16
EPS = 1e-6

D_MODEL = 1024
MEM_LEN = 256
GM_WIDTH = 512
GM_CHUNK = 128
GM_GROUPS = 4
HG_HEADS = 4
HG_DIM = 128
HG_CHUNK = 64
XA_HEADS = 4
XA_DIM = 128
BR_WIDTH = 512
D_FF = 2816
IN_WIDTH = 6656
N_CHIPS = 4
N_DEV = 8

ADAM_LR = 0.001
ADAM_B1 = 0.9
ADAM_B2 = 0.999
ADAM_EPS = 1e-08
ADAM_WD = 0.01
ADAM_STEP = 10

COL_ZU, COL_ZV, COL_HQ, COL_HF, COL_HI, COL_HG, COL_XQ = 0, 1, 2, 3, 4, 5, 6
COL_GATE0 = 3584

VMEM_LIMIT_BYTES = 48 * 1024 * 1024
MESH_ID = pl.DeviceIdType.MESH


def _cp(*sem):
    return pltpu.CompilerParams(dimension_semantics=sem, vmem_limit_bytes=VMEM_LIMIT_BYTES)


def _pallas(body, *, out_shape, **kw):
    def pin(s):
        return pltpu.HBM(s.shape, s.dtype) if isinstance(s, jax.ShapeDtypeStruct) else s

    out_shape = tuple(pin(s) for s in out_shape) if isinstance(out_shape, (tuple, list)) else pin(out_shape)
    call = pl.pallas_call(body, out_shape=out_shape, **kw)

    def run(*operands):
        return call(*[pltpu.with_memory_space_constraint(o, pltpu.HBM) if jnp.issubdtype(o.dtype, jnp.floating)
                      else o for o in operands])

    return run


def _dot(a, b):
    return lax.dot_general(a.astype(BF16), b.astype(BF16), (((1,), (0,)), ((), ())), preferred_element_type=F32)


def _dot_nt(a, b):
    return lax.dot_general(a.astype(BF16), b.astype(BF16), (((1,), (1,)), ((), ())), preferred_element_type=F32)


def _dot_tn(a, b):
    return lax.dot_general(a.astype(BF16), b.astype(BF16), (((0,), (0,)), ((), ())), preferred_element_type=F32)


def _split2(x):
    hi = x.astype(BF16)
    return hi, (x - hi.astype(F32)).astype(BF16)


def _dot3(a, b, dims):
    ah, al = _split2(a)
    bh, bl = _split2(b)
    dn = (dims, ((), ()))
    return (lax.dot_general(ah, bh, dn, preferred_element_type=F32)
            + lax.dot_general(ah, bl, dn, preferred_element_type=F32)
            + lax.dot_general(al, bh, dn, preferred_element_type=F32))


def _dot_01(mask01, x):
    hi = x.astype(BF16)
    r1 = x - hi.astype(F32)
    mid = r1.astype(BF16)
    lo = (r1 - mid.astype(F32)).astype(BF16)
    m = mask01.astype(BF16)
    dn = (((1,), (0,)), ((), ()))
    return (lax.dot_general(m, hi, dn, preferred_element_type=F32)
            + lax.dot_general(m, mid, dn, preferred_element_type=F32)
            + lax.dot_general(m, lo, dn, preferred_element_type=F32))


def _sigmoid(z):
    return 1.0 / (1.0 + jnp.exp(-z))


_GELU_C = math.sqrt(2.0 / math.pi)


def _gelu_and_grad(z):
    inner = _GELU_C * (z + 0.044715 * z * z * z)
    t = jnp.tanh(inner)
    val = 0.5 * z * (1.0 + t)
    grad = 0.5 * (1.0 + t) + 0.5 * z * (1.0 - t * t) * _GELU_C * (1.0 + 3.0 * 0.044715 * z * z)
    return val, grad


def _row_tile(n, want=512):
    t = min(want, n)
    assert n % t == 0
    return t


def _pcall(body, operands, *, name, grid, in_specs, out_specs, out_shape, scratch_shapes=(), semantics, riders=()):
    single = not isinstance(out_shape, (tuple, list))
    out_specs = (out_specs,) if single else tuple(out_specs)
    out_shape = (out_shape,) if single else tuple(out_shape)
    if not riders:
        res = _pallas(body, name=name, grid=grid, in_specs=list(in_specs), out_specs=out_specs,
                             out_shape=out_shape, scratch_shapes=list(scratch_shapes),
                             compiler_params=_cp(*semantics))(*operands)
        return (res[0] if single else res), []
    n_in, n_out, n_scr = len(in_specs), len(out_shape), len(scratch_shapes)
    ex_in = [len(ex.operands) for ex in riders]
    ex_out = [len(ex.out_shape) for ex in riders]
    ex_scr = [len(ex.scratch) for ex in riders]
    tot_in, tot_out = n_in + sum(ex_in), n_out + sum(ex_out)

    def wrapped(*refs):
        ins, outs, scr = refs[:tot_in], refs[tot_in:tot_in + tot_out], refs[tot_in + tot_out:]
        ids = [pl.program_id(d) for d in range(len(grid))]
        first = functools.reduce(lambda p, t: p & t, [i == 0 for i in ids])
        last = functools.reduce(lambda p, t: p & t, [i == n - 1 for i, n in zip(ids, grid)])
        parts, oi, oo, os_ = [], n_in, n_out, n_scr
        for k in range(len(riders)):
            parts.append((ins[oi:oi + ex_in[k]], outs[oo:oo + ex_out[k]], scr[os_:os_ + ex_scr[k]]))
            oi, oo, os_ = oi + ex_in[k], oo + ex_out[k], os_ + ex_scr[k]

        @pl.when(first)
        def _():
            for ex, part in zip(riders, parts):
                ex.start(*part)

        body(*ins[:n_in], *outs[:n_out], *scr[:n_scr])

        @pl.when(last)
        def _():
            for ex, part in zip(riders, parts):
                ex.finish(*part)

    aliases, oi, oo = {}, n_in, n_out
    all_ops, all_shapes, all_scr = list(operands), list(out_shape), list(scratch_shapes)
    for k, ex in enumerate(riders):
        aliases.update({oi + a: oo + b for a, b in ex.aliases.items()})
        oi, oo = oi + ex_in[k], oo + ex_out[k]
        all_ops += list(ex.operands)
        all_shapes += [pltpu.HBM(s.shape, s.dtype) for s in ex.out_shape]
        all_scr += list(ex.scratch)
    res = _pallas(
        wrapped, name=name, grid=grid, in_specs=list(in_specs) + [HBM_SPEC] * sum(ex_in),
        out_specs=out_specs + (HBM_SPEC,) * sum(ex_out), out_shape=tuple(all_shapes), scratch_shapes=all_scr,
        input_output_aliases=aliases, compiler_params=_cp(*(["arbitrary"] * len(grid))))(*all_ops)
    own = res[0] if single else tuple(res[:n_out])
    carried, oo = [], n_out
    for k in range(len(riders)):
        carried.append(list(res[oo:oo + ex_out[k]]))
        oo += ex_out[k]
    return own, carried


def _matmul(name, operands, *, grid, in_specs, o_spec, out_shape, out_dtype, dims, has_res=False, riders=()):
    nk = grid[2]
    assert nk == 1 or (out_dtype == F32 and not has_res)

    def body(*refs):
        if has_res:
            a_ref, b_ref, r_ref, o_ref = refs
        else:
            a_ref, b_ref, o_ref = refs
            r_ref = None
        part = lax.dot_general(a_ref[...].astype(BF16), b_ref[...].astype(BF16), (dims, ((), ())),
                               preferred_element_type=F32)
        if nk == 1:
            if r_ref is not None:
                part = part + r_ref[...]
            o_ref[...] = part.astype(o_ref.dtype)
        else:
            k = pl.program_id(2)

            @pl.when(k == 0)
            def _():
                o_ref[...] = part

            @pl.when(k > 0)
            def _():
                o_ref[...] += part

    out, carried = _pcall(body, operands, name=name, grid=grid, in_specs=in_specs, out_specs=o_spec,
                          out_shape=jax.ShapeDtypeStruct(out_shape, out_dtype),
                          semantics=("parallel", "parallel", "arbitrary"), riders=riders)
    return (out, carried) if riders else out


NN = ((1,), (0,))
NT = ((1,), (1,))
TN = ((0,), (0,))


def _mm_cs(name, a, w, out_dtype, riders=()):
    M, K = a.shape
    nq, _, wd = w.shape
    tm = _row_tile(M)
    return _matmul(name, (a, w), grid=(nq, M // tm, 1),
                   in_specs=[pl.BlockSpec((tm, K), lambda j, i, k: (i, 0)),
                             pl.BlockSpec((None, K, wd), lambda j, i, k: (j, 0, 0))],
                   o_spec=pl.BlockSpec((tm, wd), lambda j, i, k: (i, j)),
                   out_shape=(M, nq * wd), out_dtype=out_dtype, dims=NN, riders=riders)


def _mm_rs(name, a, w, out_dtype, res=None, tn=512):
    M, K = a.shape
    N = w.shape[1]
    tm = _row_tile(M)
    tn = min(tn, N)
    ops = (a, w) if res is None else (a, w, res)
    in_specs = [pl.BlockSpec((tm, K), lambda i, j, k: (i, 0)),
                pl.BlockSpec((K, tn), lambda i, j, k: (0, j))]
    if res is not None:
        in_specs.append(pl.BlockSpec((tm, tn), lambda i, j, k: (i, j)))
    return _matmul(name, ops, grid=(M // tm, N // tn, 1), in_specs=in_specs,
                   o_spec=pl.BlockSpec((tm, tn), lambda i, j, k: (i, j)),
                   out_shape=(M, N), out_dtype=out_dtype, dims=NN, has_res=res is not None)


def _mm_nt_rs(name, g, w, out_dtype, to):
    M, N = g.shape
    K = w.shape[0]
    tm = _row_tile(M)
    return _matmul(name, (g, w), grid=(M // tm, K // to, 1),
                   in_specs=[pl.BlockSpec((tm, N), lambda i, j, k: (i, 0)),
                             pl.BlockSpec((to, N), lambda i, j, k: (j, 0))],
                   o_spec=pl.BlockSpec((tm, to), lambda i, j, k: (i, j)),
                   out_shape=(M, K), out_dtype=out_dtype, dims=NT)


def _mm_nt_cs(name, g, w, out_dtype, riders=(), stacked=False):
    M = g.shape[-2]
    nq, K, wd = w.shape
    tm = _row_tile(M, 256)

    def body(g_ref, w_ref, o_ref):
        acc = None
        for q in range(nq):
            gq = g_ref[q // 2, :, (q % 2) * wd:(q % 2 + 1) * wd] if stacked else g_ref[:, q * wd:(q + 1) * wd]
            part = _dot_nt(gq, w_ref[q])
            acc = part if acc is None else acc + part
        o_ref[...] = acc.astype(o_ref.dtype)

    g_spec = (pl.BlockSpec((2, tm, 2 * wd), lambda i: (0, i, 0)) if stacked
              else pl.BlockSpec((tm, nq * wd), lambda i: (i, 0)))
    out, carried = _pcall(
        body, (g, w), name=name, grid=(M // tm,),
        in_specs=[g_spec, pl.BlockSpec((nq, K, wd), lambda i: (0, 0, 0))],
        out_specs=pl.BlockSpec((tm, K), lambda i: (i, 0)),
        out_shape=jax.ShapeDtypeStruct((M, K), out_dtype), semantics=("parallel",), riders=riders)
    return (out, carried) if riders else out


def _mm_tn_rs(name, a, g, to, tn=512):
    T, M = a.shape
    N = g.shape[1]
    tt = _row_tile(T, 1024)
    tn = min(tn, N)
    return _matmul(name, (a, g), grid=(M // to, N // tn, T // tt),
                   in_specs=[pl.BlockSpec((tt, to), lambda i, j, k: (k, i)),
                             pl.BlockSpec((tt, tn), lambda i, j, k: (k, j))],
                   o_spec=pl.BlockSpec((to, tn), lambda i, j, k: (i, j)),
                   out_shape=(M, N), out_dtype=F32, dims=TN)


def _mm_tn_cs(name, a, g, nq, to, riders=(), stacked=False):
    T, M = a.shape
    wd = g.shape[-1] * (2 if stacked else 1) // nq
    tt = _row_tile(T, 1024)
    g_spec = (pl.BlockSpec((None, tt, wd), lambda i, j, k: (j // 2, k, j % 2)) if stacked
              else pl.BlockSpec((tt, wd), lambda i, j, k: (k, j)))
    return _matmul(name, (a, g), grid=(M // to, nq, T // tt),
                   in_specs=[pl.BlockSpec((tt, to), lambda i, j, k: (k, i)), g_spec],
                   o_spec=pl.BlockSpec((None, to, wd), lambda i, j, k: (j, i, 0)),
                   out_shape=(nq, M, wd), out_dtype=F32, dims=TN, riders=riders)


def _rms_fwd(name, x, g):
    T, D = x.shape
    tm = _row_tile(T)

    def body(x_ref, g_ref, o_ref):
        xv = x_ref[...]
        r = lax.rsqrt(jnp.mean(xv * xv, axis=-1, keepdims=True) + EPS)
        o_ref[...] = (xv * r * g_ref[...]).astype(o_ref.dtype)

    return _pallas(
        body, name=name, grid=(T // tm,),
        in_specs=[pl.BlockSpec((tm, D), lambda i: (i, 0)), pl.BlockSpec((1, D), lambda i: (0, 0))],
        out_specs=pl.BlockSpec((tm, D), lambda i: (i, 0)),
        out_shape=jax.ShapeDtypeStruct((T, D), BF16), compiler_params=_cp("parallel"),
    )(x, g)


def _rms_bwd(name, x, g, dh, dres, riders=()):
    T, D = x.shape
    tm = _row_tile(T)
    has_res = dres is not None

    def body(*refs):
        if has_res:
            x_ref, g_ref, dh_ref, dr_ref, dx_ref, dg_ref = refs
        else:
            x_ref, g_ref, dh_ref, dx_ref, dg_ref = refs

        @pl.when(pl.program_id(0) == 0)
        def _():
            dg_ref[...] = jnp.zeros_like(dg_ref)

        xv = x_ref[...]
        r = lax.rsqrt(jnp.mean(xv * xv, axis=-1, keepdims=True) + EPS)
        n = xv * r
        dhv = dh_ref[...]
        dg_ref[...] += jnp.sum(dhv * n, axis=0, keepdims=True)
        dn = dhv * g_ref[...]
        dx = r * (dn - n * jnp.mean(dn * n, axis=-1, keepdims=True))
        if has_res:
            dx = dx + dr_ref[...]
        dx_ref[...] = dx

    row = pl.BlockSpec((tm, D), lambda i: (i, 0))
    vec = pl.BlockSpec((1, D), lambda i: (0, 0))
    ops = (x, g, dh, dres) if has_res else (x, g, dh)
    out, carried = _pcall(
        body, ops, name=name, grid=(T // tm,),
        in_specs=[row, vec, row] + ([row] if has_res else []),
        out_specs=(row, vec),
        out_shape=(jax.ShapeDtypeStruct((T, D), F32), jax.ShapeDtypeStruct((1, D), F32)),
        semantics=("arbitrary",), riders=riders)
    return (out, carried) if riders else out


def _loss_head(x2, tgt, g):
    T, D = x2.shape
    tm = _row_tile(T)

    def body(x_ref, t_ref, g_ref, dx_ref, dg_ref, loss_ref):
        @pl.when(pl.program_id(0) == 0)
        def _():
            dg_ref[...] = jnp.zeros_like(dg_ref)
            loss_ref[...] = jnp.zeros_like(loss_ref)

        xv = x_ref[...]
        gv = g_ref[...]
        r = lax.rsqrt(jnp.mean(xv * xv, axis=-1, keepdims=True) + EPS)
        n = xv * r
        diff = n * gv - t_ref[...]
        loss_ref[...] += 0.5 * jnp.sum(jnp.mean(diff * diff, axis=-1, keepdims=True))
        dy = diff * (1.0 / D)
        dg_ref[...] += jnp.sum(dy * n, axis=0, keepdims=True)
        dn = dy * gv
        dx_ref[...] = r * (dn - n * jnp.mean(dn * n, axis=-1, keepdims=True))

    row = pl.BlockSpec((tm, D), lambda i: (i, 0))
    vec = pl.BlockSpec((1, D), lambda i: (0, 0))
    return _pallas(
        body, name="loss_head", grid=(T // tm,),
        in_specs=[row, row, vec],
        out_specs=(row, vec, pl.BlockSpec((8, 128), lambda i: (0, 0))),
        out_shape=(jax.ShapeDtypeStruct((T, D), F32), jax.ShapeDtypeStruct((1, D), F32),
                   jax.ShapeDtypeStruct((8, 128), F32)),
        compiler_params=_cp("arbitrary"),
    )(x2, tgt, g)


def _gmlp_pieces(zu, zv, lng, lnb, ws_ref, bs_ref):
    u, du = _gelu_and_grad(zu)
    v, dv = _gelu_and_grad(zv)
    mu = jnp.mean(v, axis=-1, keepdims=True)
    vc = v - mu
    rstd = lax.rsqrt(jnp.mean(vc * vc, axis=-1, keepdims=True) + EPS)
    vhat = vc * rstd
    vn = vhat * lng + lnb
    row = lax.broadcasted_iota(jnp.int32, (GM_CHUNK, GM_CHUNK), 0)
    col = lax.broadcasted_iota(jnp.int32, (GM_CHUNK, GM_CHUNK), 1)
    tril = row >= col
    wms, mixed = [], []
    for g in range(GM_GROUPS):
        sl = slice(g * 128, (g + 1) * 128)
        wm = jnp.where(tril, ws_ref[g], 0.0)
        wms.append(wm)
        mixed.append(_dot(wm, vn[:, sl]) + bs_ref[g])
    return u, du, dv, rstd, vhat, vn, wms, mixed, tril


def _gmlp_fwd(proj, lng, lnb, ws, bs_col):
    T = proj.shape[0]
    n = T // GM_CHUNK

    def body(zu_ref, zv_ref, lng_ref, lnb_ref, ws_ref, bs_ref, o_ref):
        u, _, _, _, _, _, _, mixed, _ = _gmlp_pieces(zu_ref[...].astype(F32), zv_ref[...].astype(F32),
                                                     lng_ref[...], lnb_ref[...],
                                                     ws_ref, bs_ref)
        for g in range(GM_GROUPS):
            sl = slice(g * 128, (g + 1) * 128)
            o_ref[:, sl] = (u[:, sl] * mixed[g]).astype(o_ref.dtype)

    vec = pl.BlockSpec((1, GM_WIDTH), lambda i: (0, 0))
    return _pallas(
        body, name="gmlp_fwd", grid=(n,),
        in_specs=[pl.BlockSpec((GM_CHUNK, 512), lambda i: (i, COL_ZU)),
                  pl.BlockSpec((GM_CHUNK, 512), lambda i: (i, COL_ZV)),
                  vec, vec,
                  pl.BlockSpec((GM_GROUPS, 128, 128), lambda i: (0, 0, 0)),
                  pl.BlockSpec((GM_GROUPS, 128, 1), lambda i: (0, 0, 0))],
        out_specs=pl.BlockSpec((GM_CHUNK, 512), lambda i: (i, 0)),
        out_shape=jax.ShapeDtypeStruct((T, GM_WIDTH), BF16), compiler_params=_cp("parallel"),
    )(proj, proj, lng, lnb, ws, bs_col)


def _gmlp_bwd(proj, d_out, lng, lnb, ws, bs_col):
    T = proj.shape[0]
    n = T // GM_CHUNK

    def body(zu_ref, zv_ref, do_ref, lng_ref, lnb_ref, ws_ref, bs_ref,
             dzu_ref, dzv_ref, dws_ref, dbs_ref, dlng_ref, dlnb_ref, dm_acc):
        i = pl.program_id(0)

        @pl.when(i == 0)
        def _():
            dws_ref[...] = jnp.zeros_like(dws_ref)
            dlng_ref[...] = jnp.zeros_like(dlng_ref)
            dlnb_ref[...] = jnp.zeros_like(dlnb_ref)
            dm_acc[...] = jnp.zeros_like(dm_acc)

        lng_v = lng_ref[...]
        u, du, dv, rstd, vhat, vn, wms, mixed, tril = _gmlp_pieces(zu_ref[...].astype(F32), zv_ref[...].astype(F32),
                                                                  lng_v, lnb_ref[...],
                                                                  ws_ref, bs_ref)
        do = do_ref[...]
        dvn_parts = []
        for g in range(GM_GROUPS):
            sl = slice(g * 128, (g + 1) * 128)
            dog = do[:, sl]
            dzu_ref[:, sl] = (dog * mixed[g] * du[:, sl]).astype(dzu_ref.dtype)
            dmix = dog * u[:, sl]
            dm_acc[:, sl] += dmix
            dws_ref[g] += jnp.where(tril, _dot_nt(dmix, vn[:, sl]), 0.0)
            dvn_parts.append(_dot_tn(wms[g], dmix))
        dvn = jnp.concatenate(dvn_parts, axis=1)
        dlng_ref[...] += jnp.sum(dvn * vhat, axis=0, keepdims=True)
        dlnb_ref[...] += jnp.sum(dvn, axis=0, keepdims=True)
        dvh = dvn * lng_v
        dvv = rstd * (dvh - jnp.mean(dvh, axis=-1, keepdims=True)
                      - vhat * jnp.mean(dvh * vhat, axis=-1, keepdims=True))
        dzv_ref[...] = (dvv * dv).astype(dzv_ref.dtype)

        @pl.when(i == n - 1)
        def _():
            for g in range(GM_GROUPS):
                dbs_ref[g] = jnp.sum(dm_acc[:, g * 128:(g + 1) * 128], axis=1, keepdims=True)

    vec = pl.BlockSpec((1, GM_WIDTH), lambda i: (0, 0))
    wsp = pl.BlockSpec((GM_GROUPS, 128, 128), lambda i: (0, 0, 0))
    bsp = pl.BlockSpec((GM_GROUPS, 128, 1), lambda i: (0, 0, 0))
    tile = pl.BlockSpec((GM_CHUNK, 512), lambda i: (i, 0))
    return _pallas(
        body, name="gmlp_bwd", grid=(n,),
        in_specs=[pl.BlockSpec((GM_CHUNK, 512), lambda i: (i, COL_ZU)),
                  pl.BlockSpec((GM_CHUNK, 512), lambda i: (i, COL_ZV)),
                  pl.BlockSpec((None, GM_CHUNK, 512), lambda i: (0, i, 0)), vec, vec, wsp, bsp],
        out_specs=(tile, tile, wsp, bsp, vec, vec),
        out_shape=(jax.ShapeDtypeStruct((T, GM_WIDTH), BF16), jax.ShapeDtypeStruct((T, GM_WIDTH), BF16),
                   jax.ShapeDtypeStruct((GM_GROUPS, 128, 128), F32), jax.ShapeDtypeStruct((GM_GROUPS, 128, 1), F32),
                   jax.ShapeDtypeStruct((1, GM_WIDTH), F32), jax.ShapeDtypeStruct((1, GM_WIDTH), F32)),
        scratch_shapes=[pltpu.VMEM((GM_CHUNK, GM_WIDTH), F32)],
        compiler_params=_cp("arbitrary"),
    )(proj, proj, d_out, lng, lnb, ws, bs_col)


def _hgrn_lower_bound(lbl):
    return 1.0 / (1.0 + jnp.exp(lbl[1:2, :] - lbl[0:1, :]))


def _hgrn_gates(hq, hf, lb):
    C = HG_CHUNK
    sg = _sigmoid(hf)
    fg = lb + (1.0 - lb) * sg
    sq = _sigmoid(hq)
    row = lax.broadcasted_iota(jnp.int32, (C, C), 0)
    col = lax.broadcasted_iota(jnp.int32, (C, C), 1)
    tril = row >= col
    logf = jnp.log(fg)
    a = _dot_01(tril, logf)
    a_last = jnp.sum(logf, axis=0, keepdims=True)
    first_half = lax.broadcasted_iota(jnp.int32, logf.shape, 0) < (C // 2)
    a_mid = jnp.sum(jnp.where(first_half, logf, 0.0), axis=0, keepdims=True)
    ea, ei, eki, ekl = jnp.exp(a), jnp.exp(a - a_mid), jnp.exp(a_mid - a), jnp.exp(a_last - a)
    k = 1.0 - fg
    q = hq * sq
    return dict(sg=sg, fg=fg, sq=sq, tril=tril, ea=ea, ei=ei, eki=eki, ekl=ekl, e_last=jnp.exp(a_last),
                qe=q * ea, qi=q * ei, ki=k * eki, kl=k * ekl)


def _heads(x):
    return [x[:, h * HG_DIM:(h + 1) * HG_DIM] for h in range(HG_HEADS)]


def _hgrn_fwd(proj, lbl, gh, B, S, riders=()):
    C = HG_CHUNK
    NC = S // C
    W = HG_HEADS * HG_DIM

    def body(q_ref, f_ref, i_ref, g_ref, lbl_ref, gh_ref, o_ref, bo_ref, st_ref, state):
        @pl.when(pl.program_id(0) == 0)
        def _():
            state[...] = jnp.zeros_like(state)

        lb = _hgrn_lower_bound(lbl_ref[...])
        ghv = gh_ref[...]
        for b in range(B):
            gt = _hgrn_gates(q_ref[b].astype(F32), f_ref[b].astype(F32), lb)
            v = _heads(i_ref[b])
            qe, qi, ki, kl, e_last = (_heads(gt[n]) for n in ("qe", "qi", "ki", "kl", "e_last"))
            outs, normed = [], []
            for h in range(HG_HEADS):
                p = jnp.where(gt["tril"], _dot_nt(qi[h], ki[h]), 0.0)
                st = state[b, h]
                st_ref[b, h] = st
                o = _dot_nt(qe[h], st) + _dot(p, v[h])
                state[b, h] = st * e_last[h] + _dot_tn(v[h], kl[h])
                outs.append(o)
                normed.append(o * lax.rsqrt(jnp.mean(o * o, axis=-1, keepdims=True) + EPS) * ghv)
            o_ref[b] = jnp.concatenate(outs, axis=1)
            hg = g_ref[b].astype(F32)
            bo_ref[b] = (jnp.concatenate(normed, axis=1) * (hg * _sigmoid(hg))).astype(bo_ref.dtype)

    def col(cb):
        return pl.BlockSpec((B, C, 512), lambda c: (0, c, cb))

    tile = pl.BlockSpec((B, C, W), lambda c: (0, c, 0))
    proj3 = proj.reshape(B, S, proj.shape[-1])
    out, carried = _pcall(
        body, (proj3, proj3, proj3, proj3, lbl, gh), name="hgrn_fwd", grid=(NC,),
        in_specs=[col(COL_HQ), col(COL_HF), col(COL_HI), col(COL_HG),
                  pl.BlockSpec((2, W), lambda c: (0, 0)), pl.BlockSpec((1, HG_DIM), lambda c: (0, 0))],
        out_specs=(tile, tile, pl.BlockSpec((B, None, HG_HEADS, 128, 128), lambda c: (0, c, 0, 0, 0))),
        out_shape=(jax.ShapeDtypeStruct((B, S, W), F32), jax.ShapeDtypeStruct((B, S, W), BF16),
                   jax.ShapeDtypeStruct((B, NC, HG_HEADS, 128, 128), F32)),
        scratch_shapes=[pltpu.VMEM((B, HG_HEADS, 128, 128), F32)],
        semantics=("arbitrary",), riders=riders)
    o_h, b_out, states = out
    out = (o_h, b_out.reshape(B * S, W), states)
    return (out, carried) if riders else out


def _hgrn_bwd(proj, o_saved, states, d_out, lbl, gh, B, S, riders=()):
    C = HG_CHUNK
    NC = S // C
    W = HG_HEADS * HG_DIM

    def body(q_ref, f_ref, i_ref, g_ref, o_ref, st_ref, do_ref, lbl_ref, gh_ref,
             dq_ref, df_ref, di_ref, dg_ref, dlbl_ref, dgh_ref, dstate, dlb_acc):
        c = pl.program_id(0)

        @pl.when(c == 0)
        def _():
            dstate[...] = jnp.zeros_like(dstate)
            dgh_ref[...] = jnp.zeros_like(dgh_ref)
            dlb_acc[...] = jnp.zeros_like(dlb_acc)

        lb = _hgrn_lower_bound(lbl_ref[...])
        ghv = gh_ref[...]
        row = lax.broadcasted_iota(jnp.int32, (C, C), 0)
        colm = lax.broadcasted_iota(jnp.int32, (C, C), 1)
        triu = colm >= row
        for b in range(B):
            hq, hg = q_ref[b].astype(F32), g_ref[b].astype(F32)
            gt = _hgrn_gates(hq, f_ref[b].astype(F32), lb)
            tril = gt["tril"]
            v = _heads(i_ref[b])
            qe, qi, ki, kl, e_last = (_heads(gt[n]) for n in ("qe", "qi", "ki", "kl", "e_last"))
            sgg = _sigmoid(hg)
            don_all = do_ref[b] * (hg * sgg)
            o, don = _heads(o_ref[b]), _heads(don_all)
            d_qe, d_qi, d_ki, d_kl, dv, n_all, dal = [], [], [], [], [], [], []
            for h in range(HG_HEADS):
                r = lax.rsqrt(jnp.mean(o[h] * o[h], axis=-1, keepdims=True) + EPS)
                n = o[h] * r
                n_all.append(n)
                dgh_ref[...] += jnp.sum(don[h] * n, axis=0, keepdims=True)
                dn = don[h] * ghv
                d_o = r * (dn - n * jnp.mean(dn * n, axis=-1, keepdims=True))
                st, dst = st_ref[b, h], dstate[b, h]
                p = jnp.where(tril, _dot_nt(qi[h], ki[h]), 0.0)
                dp = jnp.where(tril, _dot3(d_o, v[h], NT), 0.0)
                d_qe.append(_dot3(d_o, st, NN))
                d_qi.append(_dot3(dp, ki[h], NN))
                d_ki.append(_dot3(dp, qi[h], TN))
                d_kl.append(_dot3(v[h], dst, NN))
                dv.append(_dot_tn(p, d_o) + _dot_nt(kl[h], dst))
                dstate[b, h] = dst * e_last[h] + _dot3(d_o, qe[h], TN)
                dal.append(jnp.sum(dst * st, axis=0, keepdims=True) * e_last[h])
            d_qe, d_qi, d_ki, d_kl, n_all, dal = (jnp.concatenate(t, axis=1)
                                                  for t in (d_qe, d_qi, d_ki, d_kl, n_all, dal))
            dg_ref[b] = (do_ref[b] * n_all * jnp.tile(ghv, (1, HG_HEADS))
                         * (sgg * (1.0 + hg * (1.0 - sgg)))).astype(dg_ref.dtype)
            di_ref[b] = jnp.concatenate(dv, axis=1).astype(di_ref.dtype)
            d_a_last = dal + jnp.sum(d_kl * gt["kl"], axis=0, keepdims=True)
            dq = d_qe * gt["ea"] + d_qi * gt["ei"]
            dk = d_ki * gt["eki"] + d_kl * gt["ekl"]
            da = d_qe * gt["qe"] + d_qi * gt["qi"] - d_ki * gt["ki"] - d_kl * gt["kl"]
            dlogf = _dot_01(triu, da) + d_a_last
            sg, sq = gt["sg"], gt["sq"]
            dfg = dlogf / gt["fg"] - dk
            df_ref[b] = (dfg * (1.0 - lb) * sg * (1.0 - sg)).astype(df_ref.dtype)
            dlb_acc[...] += jnp.sum(dfg * (1.0 - sg), axis=0, keepdims=True)
            dq_ref[b] = (dq * (sq * (1.0 + hq * (1.0 - sq)))).astype(dq_ref.dtype)

        @pl.when(c == NC - 1)
        def _():
            dlb = dlb_acc[...]
            first = lax.broadcasted_iota(jnp.int32, (2, W), 0) == 0
            dlbl_ref[...] = jnp.where(first, dlb * lb * (1.0 - lb), -dlb * lb * (1.0 - lb))

    def col(cb):
        return pl.BlockSpec((B, C, 512), lambda c: (0, NC - 1 - c, cb))

    tile = pl.BlockSpec((B, C, W), lambda c: (0, NC - 1 - c, 0))
    proj3 = proj.reshape(B, S, proj.shape[-1])
    d3 = jax.ShapeDtypeStruct((B, S, W), BF16)
    out, carried = _pcall(
        body, (proj3, proj3, proj3, proj3, o_saved, states, d_out.reshape(3, B, S, W), lbl, gh), name="hgrn_bwd",
        grid=(NC,),
        in_specs=[col(COL_HQ), col(COL_HF), col(COL_HI), col(COL_HG), tile,
                  pl.BlockSpec((B, None, HG_HEADS, 128, 128), lambda c: (0, NC - 1 - c, 0, 0, 0)),
                  pl.BlockSpec((None, B, C, W), lambda c: (1, 0, NC - 1 - c, 0)),
                  pl.BlockSpec((2, W), lambda c: (0, 0)), pl.BlockSpec((1, HG_DIM), lambda c: (0, 0))],
        out_specs=(tile, tile, tile, tile,
                   pl.BlockSpec((2, W), lambda c: (0, 0)), pl.BlockSpec((1, HG_DIM), lambda c: (0, 0))),
        out_shape=(d3, d3, d3, d3, jax.ShapeDtypeStruct((2, W), F32), jax.ShapeDtypeStruct((1, HG_DIM), F32)),
        scratch_shapes=[pltpu.VMEM((B, HG_HEADS, 128, 128), F32), pltpu.VMEM((1, W), F32)],
        semantics=("arbitrary",), riders=riders)
    out = tuple(t.reshape(B * S, W) for t in out[:4]) + tuple(out[4:])
    return (out, carried) if riders else out


_XA_SCALE = XA_DIM ** -0.5


def _attn_probs(qh, kh):
    s = _dot_nt(qh, kh) * _XA_SCALE
    e = jnp.exp(s - jnp.max(s, axis=-1, keepdims=True))
    return e / jnp.sum(e, axis=-1, keepdims=True)


def _attn_fwd(proj, kv, B, S):
    T = B * S
    tq = _row_tile(S)
    nq = S // tq
    W = XA_HEADS * XA_DIM

    def body(q_ref, kv_ref, o_ref):
        for h in range(XA_HEADS):
            sl = slice(h * 128, (h + 1) * 128)
            p = _attn_probs(q_ref[:, sl], kv_ref[:, sl])
            o_ref[:, sl] = _dot(p, kv_ref[:, W + h * 128:W + (h + 1) * 128]).astype(o_ref.dtype)

    return _pallas(
        body, name="attn_fwd", grid=(B, nq),
        in_specs=[pl.BlockSpec((tq, 512), lambda b, i: (b * nq + i, COL_XQ)),
                  pl.BlockSpec((MEM_LEN, 2 * W), lambda b, i: (b, 0))],
        out_specs=pl.BlockSpec((tq, W), lambda b, i: (b * nq + i, 0)),
        out_shape=jax.ShapeDtypeStruct((T, W), BF16), compiler_params=_cp("parallel", "parallel"),
    )(proj, kv)


def _attn_bwd(proj, kv, d_out, B, S):
    T = B * S
    tq = _row_tile(S)
    nq = S // tq
    W = XA_HEADS * XA_DIM

    def body(q_ref, kv_ref, do_ref, dq_ref, dkv_ref):
        @pl.when(pl.program_id(1) == 0)
        def _():
            dkv_ref[...] = jnp.zeros_like(dkv_ref)

        for h in range(XA_HEADS):
            sl = slice(h * 128, (h + 1) * 128)
            slv = slice(W + h * 128, W + (h + 1) * 128)
            qh = q_ref[:, sl]
            kh = kv_ref[:, sl]
            p = _attn_probs(qh, kh)
            dc = do_ref[:, sl]
            dp = _dot_nt(dc, kv_ref[:, slv])
            ds = p * (dp - jnp.sum(dp * p, axis=-1, keepdims=True)) * _XA_SCALE
            dq_ref[:, sl] = _dot(ds, kh).astype(dq_ref.dtype)
            dkv_ref[:, sl] += _dot_tn(ds, qh)
            dkv_ref[:, slv] += _dot_tn(p, dc)

    kvspec = pl.BlockSpec((MEM_LEN, 2 * W), lambda b, i: (b, 0))
    tile = pl.BlockSpec((tq, W), lambda b, i: (b * nq + i, 0))
    return _pallas(
        body, name="attn_bwd", grid=(B, nq),
        in_specs=[pl.BlockSpec((tq, 512), lambda b, i: (b * nq + i, COL_XQ)), kvspec,
                  pl.BlockSpec((None, tq, W), lambda b, i: (2, b * nq + i, 0))],
        out_specs=(tile, kvspec),
        out_shape=(jax.ShapeDtypeStruct((T, W), BF16), jax.ShapeDtypeStruct((B * MEM_LEN, 2 * W), F32)),
        compiler_params=_cp("parallel", "arbitrary"),
    )(proj, kv, d_out)


_MERGE_TM = 256
_GATE_W = 512


def _gate_specs(tm):
    base = COL_GATE0 // _GATE_W
    return [pl.BlockSpec((tm, _GATE_W), functools.partial(lambda i, k: (i, base + k), k=k)) for k in range(6)]


def _merge_fwd(a_out, b_out, c_out, wb, proj):
    T = a_out.shape[0]
    tm = _row_tile(T, _MERGE_TM)
    nq, _, wd = wb.shape
    per_half = _GATE_W // wd

    def body(a_ref, b_ref, c_ref, w_ref, *rest):
        gates, (m_ref, up_ref) = rest[:6], rest[6:]
        for hf in range(2):
            cols = slice(hf * _GATE_W, (hf + 1) * _GATE_W)
            acc = None
            for n, br in enumerate((a_ref, b_ref, c_ref)):
                x = br[...]
                up = jnp.concatenate([_dot(x, w_ref[per_half * hf + j, n * BR_WIDTH:(n + 1) * BR_WIDTH, :])
                                      for j in range(per_half)], axis=1)
                up_ref[n, :, cols] = up.astype(up_ref.dtype)
                term = _sigmoid(gates[2 * n + hf][...].astype(F32)) * up
                acc = term if acc is None else acc + term
            m_ref[:, cols] = acc.astype(m_ref.dtype)

    br_spec = pl.BlockSpec((tm, BR_WIDTH), lambda i: (i, 0))
    return _pallas(
        body, name="merge_fwd", grid=(T // tm,),
        in_specs=[br_spec, br_spec, br_spec,
                  pl.BlockSpec((nq, 3 * BR_WIDTH, wd), lambda i: (0, 0, 0))] + _gate_specs(tm),
        out_specs=(pl.BlockSpec((tm, D_MODEL), lambda i: (i, 0)), pl.BlockSpec((3, tm, D_MODEL), lambda i: (0, i, 0))),
        out_shape=(jax.ShapeDtypeStruct((T, D_MODEL), BF16), jax.ShapeDtypeStruct((3, T, D_MODEL), BF16)),
        compiler_params=_cp("parallel"),
    )(a_out, b_out, c_out, wb, *([proj] * 6))


def _branch_bwd_act(d_ups, wb):
    _, T, D = d_ups.shape
    nq, _, wd = wb.shape
    tm = _row_tile(T)

    def body(d_ref, w_ref, o_ref):
        acc = None
        for q in range(nq):
            part = _dot_nt(d_ref[:, q * wd:(q + 1) * wd], w_ref[q])
            acc = part if acc is None else acc + part
        o_ref[...] = acc

    return _pallas(
        body, name="d_branch", grid=(3, T // tm),
        in_specs=[pl.BlockSpec((None, tm, D), lambda n, i: (n, i, 0)),
                  pl.BlockSpec((nq, BR_WIDTH, wd), lambda n, i: (0, n, 0))],
        out_specs=pl.BlockSpec((None, tm, BR_WIDTH), lambda n, i: (n, i, 0)),
        out_shape=jax.ShapeDtypeStruct((3, T, BR_WIDTH), F32), compiler_params=_cp("parallel", "parallel"),
    )(d_ups, wb)


def _branch_bwd_weight(name, br, d_ups, n):
    T = br.shape[0]
    D = d_ups.shape[2]
    wd = D // N_CHIPS
    tt = _row_tile(T, 1024)

    def body(b_ref, d_ref, o_ref):
        k = pl.program_id(0)
        for q in range(N_CHIPS):
            part = _dot_tn(b_ref[...], d_ref[:, q * wd:(q + 1) * wd])

            @pl.when(k == 0)
            def _():
                o_ref[q] = part

            @pl.when(k > 0)
            def _():
                o_ref[q] += part

    return _pallas(
        body, name=name, grid=(T // tt,),
        in_specs=[pl.BlockSpec((tt, BR_WIDTH), lambda k: (k, 0)),
                  pl.BlockSpec((None, tt, D), lambda k: (n, k, 0))],
        out_specs=pl.BlockSpec((N_CHIPS, BR_WIDTH, wd), lambda k: (0, 0, 0)),
        out_shape=jax.ShapeDtypeStruct((N_CHIPS, BR_WIDTH, wd), F32), compiler_params=_cp("arbitrary"),
    )(br, d_ups)


def _merge_bwd(d_merged, ups, proj):
    T = d_merged.shape[0]
    tm = _row_tile(T, _MERGE_TM)

    def body(dm_ref, up_ref, *rest):
        gates, (dup_ref, dg0_ref, dg1_ref, dg2_ref) = rest[:6], rest[6:]
        for hf in range(2):
            cols = slice(hf * _GATE_W, (hf + 1) * _GATE_W)
            dm = dm_ref[:, cols]
            for n, dgr in enumerate((dg0_ref, dg1_ref, dg2_ref)):
                gate = _sigmoid(gates[2 * n + hf][...].astype(F32))
                dup_ref[n, :, cols] = (dm * gate).astype(dup_ref.dtype)
                dgr[:, cols] = (dm * up_ref[n, :, cols].astype(F32) * gate * (1.0 - gate)).astype(dgr.dtype)

    tile = pl.BlockSpec((tm, D_MODEL), lambda i: (i, 0))
    tile3 = pl.BlockSpec((3, tm, D_MODEL), lambda i: (0, i, 0))
    return _pallas(
        body, name="merge_bwd", grid=(T // tm,),
        in_specs=[tile, tile3] + _gate_specs(tm),
        out_specs=(tile3, tile, tile, tile),
        out_shape=(jax.ShapeDtypeStruct((3, T, D_MODEL), BF16),) + (jax.ShapeDtypeStruct((T, D_MODEL), BF16),) * 3,
        compiler_params=_cp("parallel"),
    )(d_merged, ups, *([proj] * 6))


_CONV_TF = D_FF // 2
_CONV_TS = 256
_HALO = 16


def _conv_fwd(ab, cw, cb, B, S):
    T = B * S
    ts = _row_tile(S, _CONV_TS)
    tf = _CONV_TF
    nb = D_FF // tf
    tps = S // ts
    hb = ts // _HALO

    def body(a_ref, p_ref, b_ref, w_ref, cb_ref, o_ref):
        start = (pl.program_id(0) % tps) == 0
        a = a_ref[...].astype(F32)
        prev = jnp.where(start, 0.0, p_ref[...].astype(F32))
        ext = jnp.concatenate([prev, a], axis=0)
        a1 = pltpu.roll(ext, 1, 0)[_HALO:, :]
        a2 = pltpu.roll(ext, 2, 0)[_HALO:, :]
        ac = cb_ref[...] + w_ref[0] * a2 + w_ref[1] * a1 + w_ref[2] * a
        o_ref[...] = (ac * _sigmoid(ac) * b_ref[...].astype(F32)).astype(o_ref.dtype)

    return _pallas(
        body, name="conv_fwd", grid=(T // ts, nb),
        in_specs=[pl.BlockSpec((ts, tf), lambda i, j: (i, j)),
                  pl.BlockSpec((_HALO, tf), lambda i, j: (jnp.maximum(i * hb - 1, 0), j)),
                  pl.BlockSpec((ts, tf), lambda i, j: (i, j + nb)),
                  pl.BlockSpec((3, 1, tf), lambda i, j: (0, 0, j)),
                  pl.BlockSpec((1, tf), lambda i, j: (0, j))],
        out_specs=pl.BlockSpec((ts, tf), lambda i, j: (i, j)),
        out_shape=jax.ShapeDtypeStruct((T, D_FF), BF16), compiler_params=_cp("parallel", "parallel"),
    )(ab, ab, ab, cw, cb)


def _conv_bwd(ab, d_ff, cw, cb, B, S):
    T = B * S
    ts = _row_tile(S, _CONV_TS)
    tf = _CONV_TF
    nb = D_FF // tf
    tps = S // ts
    hb = ts // _HALO
    last_h = T // _HALO - 1
    n_ext = ts + _HALO

    def body(a_ref, ap_ref, an_ref, b_ref, bn_ref, d_ref, dn_ref, w_ref, cb_ref, dab_ref, dw_ref, dcb_ref):
        i = pl.program_id(1)

        @pl.when(i == 0)
        def _():
            dw_ref[...] = jnp.zeros_like(dw_ref)
            dcb_ref[...] = jnp.zeros_like(dcb_ref)

        start = (i % tps) == 0
        end = (i % tps) == tps - 1
        a = a_ref[...].astype(F32)
        ext = jnp.concatenate([jnp.where(start, 0.0, ap_ref[...].astype(F32)), a, an_ref[...].astype(F32)], axis=0)
        r1 = pltpu.roll(ext, 1, 0)[_HALO:, :]
        r2 = pltpu.roll(ext, 2, 0)[_HALO:, :]
        ac = cb_ref[...] + w_ref[0] * r2 + w_ref[1] * r1 + w_ref[2] * ext[_HALO:, :]
        sg = _sigmoid(ac)
        d_e = jnp.concatenate([d_ref[...].astype(F32), jnp.where(end, 0.0, dn_ref[...].astype(F32))], axis=0)
        b_e = jnp.concatenate([b_ref[...].astype(F32), bn_ref[...].astype(F32)], axis=0)
        dab_ref[1] = (d_e[:ts, :] * (ac * sg)[:ts, :]).astype(dab_ref.dtype)
        dac = d_e * b_e * sg * (1.0 + ac * (1.0 - sg))
        u1 = pltpu.roll(dac, n_ext - 1, 0)[:ts, :]
        u2 = pltpu.roll(dac, n_ext - 2, 0)[:ts, :]
        dac0 = dac[:ts, :]
        dab_ref[0] = (w_ref[2] * dac0 + w_ref[1] * u1 + w_ref[0] * u2).astype(dab_ref.dtype)
        dcb_ref[...] += jnp.sum(dac0, axis=0, keepdims=True)
        dw_ref[2] += jnp.sum(dac0 * a, axis=0, keepdims=True)
        dw_ref[1] += jnp.sum(dac0 * r1[:ts, :], axis=0, keepdims=True)
        dw_ref[0] += jnp.sum(dac0 * r2[:ts, :], axis=0, keepdims=True)

    def cur(off):
        return pl.BlockSpec((ts, tf), lambda j, i: (i, j + off))

    def nxt(off):
        return pl.BlockSpec((_HALO, tf), lambda j, i: (jnp.minimum((i + 1) * hb, last_h), j + off))

    return _pallas(
        body, name="conv_bwd", grid=(nb, T // ts),
        in_specs=[cur(0), pl.BlockSpec((_HALO, tf), lambda j, i: (jnp.maximum(i * hb - 1, 0), j)), nxt(0),
                  cur(nb), nxt(nb), cur(0), nxt(0),
                  pl.BlockSpec((3, 1, tf), lambda j, i: (0, 0, j)), pl.BlockSpec((1, tf), lambda j, i: (0, j))],
        out_specs=(pl.BlockSpec((2, ts, tf), lambda j, i: (0, i, j)), pl.BlockSpec((3, 1, tf), lambda j, i: (0, 0, j)),
                   pl.BlockSpec((1, tf), lambda j, i: (0, j))),
        out_shape=(jax.ShapeDtypeStruct((2, T, D_FF), BF16),
                   jax.ShapeDtypeStruct((3, 1, D_FF), F32), jax.ShapeDtypeStruct((1, D_FF), F32)),
        compiler_params=_cp("parallel", "arbitrary"),
    )(ab, ab, ab, ab, ab, d_ff, d_ff, cw, cb)


def _local_step(x, mem, tgt, p, comm, B, S):
    g = {}
    h = _rms_fwd("norm1", x, p["norm1_g"])
    proj = comm.carry("in_proj", lambda r: _mm_cs("in_proj", h, comm.w("w_in"), BF16, riders=r))
    a_out = _gmlp_fwd(proj, p["ln_v_g"], p["ln_v_b"], p["w_spatial"], p["b_spatial"])
    o_h, b_out, states = comm.carry(
        "hgrn_fwd", lambda r: _hgrn_fwd(proj, p["lb_logits"], p["hgrn_norm_g"], B, S, riders=r))
    memn = _rms_fwd("mem_norm", mem, p["mem_norm_g"])
    kv = _mm_rs("mem_kv", memn, comm.w("w_mem_kv"), F32)
    c_out = _attn_fwd(proj, kv, B, S)
    merged, ups = _merge_fwd(a_out, b_out, c_out, comm.w("w_branch"), proj)
    x1 = _mm_rs("out_proj", merged, comm.w("w_out"), F32, res=x)
    h2 = _rms_fwd("norm2", x1, p["norm2_g"])
    ab = _mm_cs("up_proj", h2, comm.w("w_up"), BF16)
    conv_w = comm.w("conv_w")
    ff = _conv_fwd(ab, conv_w, p["conv_b"], B, S)
    x2 = _mm_rs("down_proj", ff, comm.w("w_down"), F32, res=x1)
    dx2, g["final_g"], loss = _loss_head(x2, tgt, p["final_g"])

    comm.grad("w_down", _mm_tn_rs("g_w_down", ff, dx2, to=D_FF // 2))
    d_ff = _mm_nt_rs("d_ff", dx2, comm.w("w_down"), BF16, to=D_FF // 2)
    d_ab, g["conv_w"], g["conv_b"] = _conv_bwd(ab, d_ff, conv_w, p["conv_b"], B, S)
    comm.grad("w_up", _mm_tn_cs("g_w_up", h2, d_ab, N_CHIPS, to=512, stacked=True))
    d_h2 = comm.carry("d_h2", lambda r: _mm_nt_cs("d_h2", d_ab, comm.w("w_up"), F32, riders=r, stacked=True))
    d_x1, g["norm2_g"] = _rms_bwd("norm2_bwd", x1, p["norm2_g"], d_h2, dx2)
    comm.grad("w_out", _mm_tn_rs("g_w_out", merged, d_x1, to=512))
    d_merged = _mm_nt_rs("d_merged", d_x1, comm.w("w_out"), F32, to=512)
    d_ups, d_g0, d_g1, d_g2 = _merge_bwd(d_merged, ups, proj)

    d_br = _branch_bwd_act(d_ups, comm.w("w_branch"))
    comm.grad("w_branch", jnp.concatenate(
        [_branch_bwd_weight("g_w_branch%d" % n, br, d_ups, n) for n, br in enumerate((a_out, b_out, c_out))],
        axis=1))

    d_zu, d_zv, g["w_spatial"], g["b_spatial"], g["ln_v_g"], g["ln_v_b"] = _gmlp_bwd(
        proj, d_br, p["ln_v_g"], p["ln_v_b"], p["w_spatial"], p["b_spatial"])
    d_xq, d_kv = _attn_bwd(proj, kv, d_br, B, S)
    comm.grad("w_mem_kv", _mm_tn_rs("g_w_mem_kv", memn, d_kv, to=512))
    d_memn = _mm_nt_rs("d_memn", d_kv, comm.w("w_mem_kv"), F32, to=512)
    _, g["mem_norm_g"] = _rms_bwd("mem_norm_bwd", mem, p["mem_norm_g"], d_memn, None)
    d_hq, d_hf, d_hi, d_hg, g["lb_logits"], g["hgrn_norm_g"] = comm.carry(
        "hgrn_bwd", lambda r: _hgrn_bwd(proj, o_h, states, d_br, p["lb_logits"], p["hgrn_norm_g"], B, S, riders=r))
    d_proj = jnp.concatenate([d_zu, d_zv, d_hq, d_hf, d_hi, d_hg, d_xq, d_g0, d_g1, d_g2], axis=1)
    comm.small_grads([g[n].reshape(_SMALL_SHAPE[n]) for n in _SMALL_EARLY] + [loss])
    comm.grad("w_in", comm.carry("g_w_in", lambda r: _mm_tn_cs("g_w_in", h, d_proj, N_CHIPS, to=512, riders=r)))
    d_h =comm.carry("d_h", lambda r: _mm_nt_cs("d_h", d_proj, comm.w("w_in"), F32, riders=r))
    grad_x, g["norm1_g"] = _rms_bwd("norm1_bwd", x, p["norm1_g"], d_h, d_x1)
    return loss, grad_x, g


HBM_SPEC = pl.BlockSpec(memory_space=pltpu.HBM)


def _place():
    x, y, c = lax.axis_index("x"), lax.axis_index("y"), lax.axis_index("c")
    other_chips = [(1 - x, y), (x, 1 - y), (1 - x, 1 - y)]
    return x, y, c, other_chips


def _remote(src, dst, send_sem, recv_sem, dev):
    return pltpu.make_async_remote_copy(src_ref=src, dst_ref=dst, send_sem=send_sem, recv_sem=recv_sem,
                                        device_id=dev, device_id_type=MESH_ID)


class _Exchange:
    def __init__(self, operands, out_shape, aliases, scratch, start, finish):
        self.operands, self.out_shape, self.aliases, self.scratch = operands, out_shape, aliases, scratch
        self.start, self.finish = start, finish


def _run_exchanges(name, exs):
    n_in = [len(ex.operands) for ex in exs]
    n_out = [len(ex.out_shape) for ex in exs]
    n_scr = [len(ex.scratch) for ex in exs]

    def body(*refs):
        ins, outs, scr = refs[:sum(n_in)], refs[sum(n_in):sum(n_in) + sum(n_out)], refs[sum(n_in) + sum(n_out):]
        parts, oi, oo, os_ = [], 0, 0, 0
        for k in range(len(exs)):
            parts.append((ins[oi:oi + n_in[k]], outs[oo:oo + n_out[k]], scr[os_:os_ + n_scr[k]]))
            oi, oo, os_ = oi + n_in[k], oo + n_out[k], os_ + n_scr[k]
        for ex, part in zip(exs, parts):
            ex.start(*part)
        for ex, part in zip(exs, parts):
            ex.finish(*part)

    aliases, ops, shapes, scratch, oi, oo = {}, [], [], [], 0, 0
    for k, ex in enumerate(exs):
        aliases.update({oi + a: oo + b for a, b in ex.aliases.items()})
        oi, oo = oi + n_in[k], oo + n_out[k]
        ops += list(ex.operands)
        shapes += [pltpu.HBM(s.shape, s.dtype) for s in ex.out_shape]
        scratch += list(ex.scratch)
    res = _pallas(
        body, name=name, in_specs=[HBM_SPEC] * len(ops), out_specs=(HBM_SPEC,) * len(shapes), out_shape=tuple(shapes),
        input_output_aliases=aliases, scratch_shapes=scratch,
    )(*ops)
    out, oo = [], 0
    for k in range(len(exs)):
        out.append(list(res[oo:oo + n_out[k]]))
        oo += n_out[k]
    return out


def _ex_all_gather(slabs, halved):
    n = len(slabs)

    def rows(a, cc):
        if not halved[a]:
            return slice(None)
        hr = slabs[a].shape[1] // 2
        return pl.ds(cc * hr, hr)

    def ici(bufs, scr, a, j, chip, c, mine):
        px, py = chip
        x, y, _, _ = _place()
        qs = 2 * x + y if mine else 2 * px + py
        piece = bufs[a].at[qs, rows(a, c)]
        return _remote(piece, piece, scr[0].at[3 * a + j], scr[1].at[3 * a + j], (px, py, c))

    def d2d(bufs, scr, a, j, chip, cc):
        px, py = chip
        x, y, c, _ = _place()
        piece = bufs[a].at[2 * px + py, rows(a, cc)]
        return _remote(piece, piece, scr[2].at[3 * a + j], scr[3].at[3 * a + j], (x, y, 1 - c))

    def start(ins, outs, scr):
        _, _, c, chips = _place()
        for j, chip in enumerate(chips):
            for a in range(n):
                ici(outs, scr, a, j, chip, c, True).start()

    def finish(ins, outs, scr):
        _, _, c, chips = _place()
        for j, chip in enumerate(chips):
            for a in range(n):
                ici(outs, scr, a, j, chip, c, False).wait_recv()
                if halved[a]:
                    d2d(outs, scr, a, j, chip, c).start()
        for j, chip in enumerate(chips):
            for a in range(n):
                if halved[a]:
                    d2d(outs, scr, a, j, chip, 1 - c).wait_recv()
        for j, chip in enumerate(chips):
            for a in range(n):
                ici(outs, scr, a, j, chip, c, True).wait_send()
                if halved[a]:
                    d2d(outs, scr, a, j, chip, c).wait_send()

    return _Exchange(list(slabs), [jax.ShapeDtypeStruct(s.shape, s.dtype) for s in slabs],
                     {a: a for a in range(n)}, [pltpu.SemaphoreType.DMA((3 * n,))] * 4, start, finish)


def _ex_to_sibling(grads):
    n = len(grads)

    def copy(ins, outs, scr, a):
        x, y, c, _ = _place()
        hr = grads[a].shape[1] // 2
        return _remote(ins[a].at[:, pl.ds((1 - c) * hr, hr), :], outs[a], scr[0].at[a], scr[1].at[a], (x, y, 1 - c))

    def start(ins, outs, scr):
        for a in range(n):
            copy(ins, outs, scr, a).start()

    def finish(ins, outs, scr):
        for a in range(n):
            copy(ins, outs, scr, a).wait()

    out_shape = [jax.ShapeDtypeStruct((g.shape[0], g.shape[1] // 2, g.shape[2]), g.dtype) for g in grads]
    return _Exchange(list(grads), out_shape, {}, [pltpu.SemaphoreType.DMA((n,))] * 2, start, finish)


def _ex_to_owner(parts):
    n = len(parts)

    def copy(ins, outs, scr, a, j, chip):
        _, _, c, _ = _place()
        px, py = chip
        return _remote(ins[a].at[2 * px + py], outs[a].at[j], scr[0].at[3 * a + j], scr[1].at[3 * a + j],
                       (px, py, c))

    def start(ins, outs, scr):
        for j, chip in enumerate(_place()[3]):
            for a in range(n):
                copy(ins, outs, scr, a, j, chip).start()

    def finish(ins, outs, scr):
        for j, chip in enumerate(_place()[3]):
            for a in range(n):
                copy(ins, outs, scr, a, j, chip).wait()

    out_shape = [jax.ShapeDtypeStruct((3,) + p.shape[1:], p.dtype) for p in parts]
    return _Exchange(list(parts), out_shape, {}, [pltpu.SemaphoreType.DMA((3 * n,))] * 2, start, finish)


def _ex_share_halves(bufs):
    n = len(bufs)

    def copy(outs, scr, a, cc):
        x, y, c, _ = _place()
        hr = bufs[a].shape[0] // 2
        piece = outs[a].at[pl.ds(cc * hr, hr), :]
        return _remote(piece, piece, scr[0].at[a], scr[1].at[a], (x, y, 1 - c))

    def start(ins, outs, scr):
        c = _place()[2]
        for a in range(n):
            copy(outs, scr, a, c).start()

    def finish(ins, outs, scr):
        c = _place()[2]
        for a in range(n):
            copy(outs, scr, a, c).wait_send()
            copy(outs, scr, a, 1 - c).wait_recv()

    return _Exchange(list(bufs), [jax.ShapeDtypeStruct(b.shape, b.dtype) for b in bufs], {a: a for a in range(n)},
                     [pltpu.SemaphoreType.DMA((n,))] * 2, start, finish)


def _ex_gather_small(arrs):
    n = len(arrs)

    def peer_of(m):
        x, y, c, _ = _place()
        return (1 - x if m & 4 else x, 1 - y if m & 2 else y, 1 - c if m & 1 else c)

    def start(ins, outs, scr):
        x, y, c, _ = _place()
        for m in range(1, N_DEV):
            for a in range(n):
                k = (N_DEV - 1) * a + m - 1
                _remote(ins[a], outs[a].at[4 * x + 2 * y + c], scr[0].at[k], scr[1].at[k], peer_of(m)).start()

    def finish(ins, outs, scr):
        for m in range(1, N_DEV):
            px, py, pc = peer_of(m)
            for a in range(n):
                k = (N_DEV - 1) * a + m - 1
                slot = outs[a].at[4 * px + 2 * py + pc]
                cp = _remote(ins[a], slot, scr[0].at[k], scr[1].at[k], (px, py, pc))
                cp.wait_send()
                cp.wait_recv()

    slots = [jnp.zeros((N_DEV,) + a.shape, a.dtype) for a in arrs]
    out_shape = [jax.ShapeDtypeStruct(s.shape, s.dtype) for s in slots]
    return _Exchange(list(arrs) + slots, out_shape, {n + a: a for a in range(n)},
                     [pltpu.SemaphoreType.DMA(((N_DEV - 1) * n,))] * 2, start, finish)


def _div_tile(n, want):
    best = None
    for t in range(8, min(n, want) + 1, 8):
        if n % t == 0:
            best = t
    assert best is not None, n
    return best


def _cast_into_slab(name, w, place, dtype):
    r, cc = w.shape
    tr = r if r * cc <= 128 * 1024 else _div_tile(r, 256)

    def body(s_ref, w_ref, o_ref):
        o_ref[...] = w_ref[...].astype(o_ref.dtype)

    return _pallas(
        body, name=name,
        grid_spec=pltpu.PrefetchScalarGridSpec(
            num_scalar_prefetch=1, grid=(r // tr,),
            in_specs=[pl.BlockSpec((tr, cc), lambda i, s: (i, 0))],
            out_specs=pl.BlockSpec((None, tr, cc), lambda i, s: (s[0], i, 0))),
        out_shape=jax.ShapeDtypeStruct((N_CHIPS, r, cc), dtype), compiler_params=_cp("parallel"),
    )(place, w)


def _add_half(name, g, rcv, place):
    nq, r, cc = g.shape
    hr = r // 2

    def body(s_ref, g_ref, r_ref, o_ref):
        o_ref[...] = (g_ref[...] + r_ref[...]).astype(o_ref.dtype)

    spec = pl.BlockSpec((None, hr, cc), lambda i, s: (i, 0, 0))
    return _pallas(
        body, name=name,
        grid_spec=pltpu.PrefetchScalarGridSpec(
            num_scalar_prefetch=1, grid=(nq,),
            in_specs=[pl.BlockSpec((None, hr, cc), lambda i, s: (i, s[1], 0)), spec], out_specs=spec),
        out_shape=jax.ShapeDtypeStruct((nq, hr, cc), BF16), compiler_params=_cp("parallel"),
    )(place, g, rcv)


def _sum_owner(name, part, rcv, place):
    _, hr, cc = part.shape
    tr = _div_tile(hr, 128)
    nb = hr // tr

    def body(s_ref, p_ref, r_ref, o_ref):
        o_ref[...] = ((p_ref[...].astype(F32) + r_ref[0].astype(F32)) + r_ref[1].astype(F32)) + r_ref[2].astype(F32)

    return _pallas(
        body, name=name,
        grid_spec=pltpu.PrefetchScalarGridSpec(
            num_scalar_prefetch=1, grid=(nb,),
            in_specs=[pl.BlockSpec((None, tr, cc), lambda i, s: (s[0], i, 0)),
                      pl.BlockSpec((3, tr, cc), lambda i, s: (0, i, 0))],
            out_specs=pl.BlockSpec((tr, cc), lambda i, s: (s[1] * nb + i, 0))),
        out_shape=jax.ShapeDtypeStruct((2 * hr, cc), F32), compiler_params=_cp("parallel"),
    )(place, part, rcv)


def _sum_small(gathered, local, place):
    n = len(gathered)

    def body(s_ref, *refs):
        g_refs, l_refs, o_refs = refs[:n], refs[n:2 * n], refs[2 * n:]
        me = s_ref[2]
        for g_ref, l_ref, o_ref in zip(g_refs, l_refs, o_refs):
            acc = None
            for d in range(N_DEV):
                term = jnp.where(me == d, l_ref[...], g_ref[d])
                acc = term if acc is None else acc + term
            o_ref[...] = acc

    def whole(shape):
        return pl.BlockSpec(shape, lambda i, s, nd=len(shape): (0,) * nd)

    return _pallas(
        body, name="sum_small",
        grid_spec=pltpu.PrefetchScalarGridSpec(
            num_scalar_prefetch=1, grid=(1,),
            in_specs=[whole(g.shape) for g in gathered] + [whole(a.shape) for a in local],
            out_specs=tuple(whole(a.shape) for a in local)),
        out_shape=tuple(jax.ShapeDtypeStruct(a.shape, a.dtype) for a in local), compiler_params=_cp("arbitrary"),
    )(place, *gathered, *local)


def _adamw(name, w, g, m, v):
    r, cc = w.shape
    tr = r if r * cc <= 128 * 1024 else _div_tile(r, 256)

    def body(w_ref, g_ref, m_ref, v_ref, d_ref, mo_ref, vo_ref):
        gv = g_ref[...]
        mn = ADAM_B1 * m_ref[...] + (1.0 - ADAM_B1) * gv
        vn = ADAM_B2 * v_ref[...] + (1.0 - ADAM_B2) * (gv * gv)
        m_hat = mn / (1.0 - ADAM_B1 ** ADAM_STEP)
        v_hat = vn / (1.0 - ADAM_B2 ** ADAM_STEP)
        d_ref[...] = -ADAM_LR * (m_hat / (jnp.sqrt(v_hat) + ADAM_EPS) + ADAM_WD * w_ref[...])
        mo_ref[...] = mn
        vo_ref[...] = vn

    spec = pl.BlockSpec((tr, cc), lambda i: (i, 0))
    sd = jax.ShapeDtypeStruct((r, cc), F32)
    return _pallas(
        body, name=name, grid=(r // tr,), in_specs=[spec] * 4, out_specs=(spec,) * 3, out_shape=(sd,) * 3,
        compiler_params=_cp("parallel"),
    )(w, g, m, v)


_BIG = ("w_in", "w_up", "w_branch", "w_mem_kv", "w_out", "w_down")
_BIG_SHARD_SHAPE = {"w_in": (1024, 1664), "w_up": (1024, 1408), "w_branch": (1536, 256),
                    "w_mem_kv": (256, 1024), "w_out": (256, 1024), "w_down": (704, 1024)}
_SMALL_SHAPE = {"norm1_g": (1, D_MODEL), "ln_v_g": (1, GM_WIDTH), "ln_v_b": (1, GM_WIDTH),
                "w_spatial": (GM_GROUPS * GM_CHUNK, GM_CHUNK), "b_spatial": (GM_GROUPS, GM_CHUNK),
                "lb_logits": (2, HG_HEADS * HG_DIM), "hgrn_norm_g": (1, HG_DIM), "mem_norm_g": (1, D_MODEL),
                "norm2_g": (1, D_MODEL), "conv_w": (3, D_FF), "conv_b": (1, D_FF), "final_g": (1, D_MODEL)}
_SMALL_EARLY = tuple(n for n in _SMALL_SHAPE if n != "norm1_g")
_PARAM_ORDER = ("norm1_g", "w_in", "ln_v_g", "ln_v_b", "w_spatial", "b_spatial", "lb_logits", "hgrn_norm_g",
                "mem_norm_g", "w_mem_kv", "w_branch", "w_out", "norm2_g", "w_up", "conv_w", "conv_b", "w_down",
                "final_g")


def _adamw_small(ws, gs, ms, vs):
    n = len(ws)

    def body(*refs):
        w_refs, g_refs, m_refs, v_refs = refs[:n], refs[n:2 * n], refs[2 * n:3 * n], refs[3 * n:4 * n]
        d_refs, mo_refs, vo_refs = refs[4 * n:5 * n], refs[5 * n:6 * n], refs[6 * n:]
        for k in range(n):
            gv = g_refs[k][...]
            mn = ADAM_B1 * m_refs[k][...] + (1.0 - ADAM_B1) * gv
            vn = ADAM_B2 * v_refs[k][...] + (1.0 - ADAM_B2) * (gv * gv)
            m_hat = mn / (1.0 - ADAM_B1 ** ADAM_STEP)
            v_hat = vn / (1.0 - ADAM_B2 ** ADAM_STEP)
            d_refs[k][...] = -ADAM_LR * (m_hat / (jnp.sqrt(v_hat) + ADAM_EPS) + ADAM_WD * w_refs[k][...])
            mo_refs[k][...] = mn
            vo_refs[k][...] = vn

    specs = [pl.BlockSpec(a.shape, lambda i: (0, 0)) for a in ws]
    shapes = tuple(jax.ShapeDtypeStruct(a.shape, F32) for a in ws)
    res = _pallas(
        body, name="adamw_small", grid=(1,), in_specs=specs * 4, out_specs=tuple(specs * 3), out_shape=shapes * 3,
        compiler_params=_cp("arbitrary"),
    )(*ws, *gs, *ms, *vs)
    return res[:n], res[n:2 * n], res[2 * n:]


class _Comm:
    _ROW_SHARDED = ("w_mem_kv", "w_out", "w_down")

    def __init__(self, slabs, place):
        self.slabs, self.place = slabs, place
        self.full, self.raw, self.parts, self.bufs, self.done = {}, {}, {}, {}, {}
        ex, deliver = self._gather(["w_in"])
        deliver(_run_exchanges("all_gather_w_in", [ex])[0])

    def w(self, name):
        a = self.full[name]
        if name in self._ROW_SHARDED:
            return a.reshape(-1, a.shape[-1])
        if name == "conv_w":
            return jnp.transpose(a, (1, 0, 2)).reshape(3, 1, D_FF)
        return a

    def grad(self, name, arr):
        self.raw[name] = arr.reshape((N_CHIPS,) + _BIG_SHARD_SHAPE[name])
        if name == "w_in":
            ex, deliver = self._to_sibling(["w_in"])
            deliver(_run_exchanges("rs_sibling_w_in", [ex])[0])

    def small_grads(self, arrays):
        self.small_local = list(arrays)

    def carry(self, tag, call):
        plan = self._plan(tag)
        if not plan:
            return call(())
        out, carried = call([ex for ex, _ in plan])
        for (_, deliver), res in zip(plan, carried):
            deliver(res)
        return out

    def finish(self, last_small):
        ex, deliver = self._share(["w_out", "w_branch", "w_mem_kv", "w_in"])
        shared, small = _run_exchanges("share_and_gather_last", [ex, _ex_gather_small(last_small)])
        deliver(shared)
        return self.done, self.small_local + list(last_small), self.small_everyone + small

    def _plan(self, tag):
        if tag == "in_proj":
            return [self._gather(["w_branch", "w_out", "w_mem_kv", "w_down", "conv_w"])]
        if tag == "hgrn_fwd":
            return [self._gather(["w_up"])]
        if tag == "d_h2":
            return [self._to_sibling(["w_down", "w_up"])]
        if tag == "hgrn_bwd":
            return [self._to_owner(["w_down", "w_up"]), self._to_sibling(["w_out", "w_branch", "w_mem_kv"])]
        if tag == "g_w_in":
            def keep(res):
                self.small_everyone = res

            return [self._to_owner(["w_out", "w_branch", "w_mem_kv"]), self._share(["w_down", "w_up"]),
                    (_ex_gather_small(self.small_local), keep)]
        if tag == "d_h":
            return [self._to_owner(["w_in"])]
        return []

    def _gather(self, names):
        ex = _ex_all_gather([self.slabs[n] for n in names], [n != "conv_w" for n in names])
        return ex, lambda res: self.full.update(zip(names, res))

    def _to_sibling(self, names):
        def deliver(res):
            for n, r in zip(names, res):
                self.parts[n] = _add_half("rs_add_" + n, self.raw[n], r, self.place)

        return _ex_to_sibling([self.raw[n] for n in names]), deliver

    def _to_owner(self, names):
        def deliver(res):
            for n, r in zip(names, res):
                self.bufs[n] = _sum_owner("rs_sum_" + n, self.parts[n], r, self.place)

        return _ex_to_owner([self.parts[n] for n in names]), deliver

    def _share(self, names):
        return _ex_share_halves([self.bufs[n] for n in names]), lambda res: self.done.update(zip(names, res))


def kernel(x, mem, norm1_g, w_in, ln_v_g, ln_v_b, w_spatial, b_spatial, lb_logits, hgrn_norm_g, mem_norm_g, w_mem_kv, w_branch, w_out, norm2_g, w_up, conv_w, conv_b, w_down, final_g, loss_target, m_norm1_g, m_w_in, m_ln_v_g, m_ln_v_b, m_w_spatial, m_b_spatial, m_lb_logits, m_hgrn_norm_g, m_mem_norm_g, m_w_mem_kv, m_w_branch, m_w_out, m_norm2_g, m_w_up, m_conv_w, m_conv_b, m_w_down, m_final_g, v_norm1_g, v_w_in, v_ln_v_g, v_ln_v_b, v_w_spatial, v_b_spatial, v_lb_logits, v_hgrn_norm_g, v_mem_norm_g, v_w_mem_kv, v_w_branch, v_w_out, v_norm2_g, v_w_up, v_conv_w, v_conv_b, v_w_down, v_final_g):
    w = dict(norm1_g=norm1_g, w_in=w_in, ln_v_g=ln_v_g, ln_v_b=ln_v_b, w_spatial=w_spatial, b_spatial=b_spatial,
             lb_logits=lb_logits, hgrn_norm_g=hgrn_norm_g, mem_norm_g=mem_norm_g, w_mem_kv=w_mem_kv,
             w_branch=w_branch, w_out=w_out, norm2_g=norm2_g, w_up=w_up, conv_w=conv_w, conv_b=conv_b,
             w_down=w_down, final_g=final_g)
    mom = dict(norm1_g=m_norm1_g, w_in=m_w_in, ln_v_g=m_ln_v_g, ln_v_b=m_ln_v_b, w_spatial=m_w_spatial,
               b_spatial=m_b_spatial, lb_logits=m_lb_logits, hgrn_norm_g=m_hgrn_norm_g, mem_norm_g=m_mem_norm_g,
               w_mem_kv=m_w_mem_kv, w_branch=m_w_branch, w_out=m_w_out, norm2_g=m_norm2_g, w_up=m_w_up,
               conv_w=m_conv_w, conv_b=m_conv_b, w_down=m_w_down, final_g=m_final_g)
    var = dict(norm1_g=v_norm1_g, w_in=v_w_in, ln_v_g=v_ln_v_g, ln_v_b=v_ln_v_b, w_spatial=v_w_spatial,
               b_spatial=v_b_spatial, lb_logits=v_lb_logits, hgrn_norm_g=v_hgrn_norm_g, mem_norm_g=v_mem_norm_g,
               w_mem_kv=v_w_mem_kv, w_branch=v_w_branch, w_out=v_w_out, norm2_g=v_norm2_g, w_up=v_w_up,
               conv_w=v_conv_w, conv_b=v_conv_b, w_down=v_w_down, final_g=v_final_g)
    B, S, D = x.shape
    T = B * S
    ci = lax.axis_index("c")
    q = 2 * lax.axis_index("x") + lax.axis_index("y")
    place = jnp.stack([q, ci, 2 * q + ci]).astype(jnp.int32)

    slabs = {n: _cast_into_slab("slab_" + n, w[n].reshape(_BIG_SHARD_SHAPE[n]), place, BF16) for n in _BIG}
    slabs["conv_w"] = _cast_into_slab("slab_conv_w", conv_w[0], place, F32)
    comm = _Comm(slabs, place)
    p = dict(
        norm1_g=norm1_g, ln_v_g=ln_v_g, ln_v_b=ln_v_b, w_spatial=w_spatial[0],
        b_spatial=b_spatial.reshape(GM_GROUPS, GM_CHUNK, 1), lb_logits=lb_logits, hgrn_norm_g=hgrn_norm_g,
        mem_norm_g=mem_norm_g, norm2_g=norm2_g, conv_b=conv_b, final_g=final_g.reshape(1, D))

    loss, grad_x, g = _local_step(x.reshape(T, D), mem.reshape(B * MEM_LEN, D), loss_target.reshape(T, D), p, comm,
                                  B, S)

    shard_grads, local_small, everyone = comm.finish([g["norm1_g"]])
    summed = _sum_small(everyone, local_small, place)
    small_names = list(_SMALL_EARLY) + ["norm1_g"]
    total = dict(zip(_SMALL_EARLY, summed))
    loss_total, total["norm1_g"] = summed[len(_SMALL_EARLY)][0, 0], summed[-1]

    grads, delta, new_m, new_v = {}, {}, {}, {}
    for n in _BIG:
        shp = _BIG_SHARD_SHAPE[n]
        grads[n] = shard_grads[n]
        delta[n], new_m[n], new_v[n] = _adamw("adamw_" + n, w[n].reshape(shp), shard_grads[n],
                                              mom[n].reshape(shp), var[n].reshape(shp))
    cw_shard = D_FF // N_CHIPS
    total["conv_w"] = lax.dynamic_slice(total["conv_w"], (0, q * cw_shard), (3, cw_shard))

    def flat2d(d, n):
        return d[n].reshape(total[n].shape)

    upd = _adamw_small([flat2d(w, n) for n in small_names], [total[n] for n in small_names],
                       [flat2d(mom, n) for n in small_names], [flat2d(var, n) for n in small_names])
    for k, n in enumerate(small_names):
        grads[n], delta[n], new_m[n], new_v[n] = total[n], upd[0][k], upd[1][k], upd[2][k]

    def shaped(d):
        return [d[n].reshape(w[n].shape) for n in _PARAM_ORDER]

    return (loss_total, grad_x.reshape(B, S, D), *shaped(grads), *shaped(delta), *shaped(new_m), *shaped(new_v))
```

```python
import functools
import math

import jax
import jax.numpy as jnp
from jax import lax
from jax.experimental import pallas as pl
from jax.experimental.pallas import tpu as pltpu

F32 = jnp.float32
BF16 = jnp.bfloat16
EPS = 1e-6

D_MODEL = 1024
MEM_LEN = 256
GM_WIDTH = 512
GM_CHUNK = 128
GM_GROUPS = 4
HG_HEADS = 4
HG_DIM = 128
HG_CHUNK = 64
XA_HEADS = 4
XA_DIM = 128
BR_WIDTH = 512
D_FF = 2816
IN_WIDTH = 6656
N_CHIPS = 4
N_DEV = 8

ADAM_LR = 0.001
ADAM_B1 = 0.9
ADAM_B2 = 0.999
ADAM_EPS = 1e-08
ADAM_WD = 0.01
ADAM_STEP = 10

COL_ZU, COL_ZV, COL_HQ, COL_HF, COL_HI, COL_HG, COL_XQ = 0, 1, 2, 3, 4, 5, 6
COL_GATE0 = 3584

VMEM_LIMIT_BYTES = 48 * 1024 * 1024
MESH_ID = pl.DeviceIdType.MESH


def _cp(*sem):
    return pltpu.CompilerParams(dimension_semantics=sem, vmem_limit_bytes=VMEM_LIMIT_BYTES)


def _pallas(body, *, out_shape, **kw):
    def pin(s):
        return pltpu.HBM(s.shape, s.dtype) if isinstance(s, jax.ShapeDtypeStruct) else s

    out_shape = tuple(pin(s) for s in out_shape) if isinstance(out_shape, (tuple, list)) else pin(out_shape)
    call = pl.pallas_call(body, out_shape=out_shape, **kw)

    def run(*operands):
        return call(*[pltpu.with_memory_space_constraint(o, pltpu.HBM) if jnp.issubdtype(o.dtype, jnp.floating)
                      else o for o in operands])

    return run


def _dot(a, b):
    return lax.dot_general(a.astype(BF16), b.astype(BF16), (((1,), (0,)), ((), ())), preferred_element_type=F32)


def _dot_nt(a, b):
    return lax.dot_general(a.astype(BF16), b.astype(BF16), (((1,), (1,)), ((), ())), preferred_element_type=F32)


def _dot_tn(a, b):
    return lax.dot_general(a.astype(BF16), b.astype(BF16), (((0,), (0,)), ((), ())), preferred_element_type=F32)


def _dot_01(mask01, x):
    hi = x.astype(BF16)
    r1 = x - hi.astype(F32)
    mid = r1.astype(BF16)
    lo = (r1 - mid.astype(F32)).astype(BF16)
    m = mask01.astype(BF16)
    dn = (((1,), (0,)), ((), ()))
    return (lax.dot_general(m, hi, dn, preferred_element_type=F32)
            + lax.dot_general(m, mid, dn, preferred_element_type=F32)
            + lax.dot_general(m, lo, dn, preferred_element_type=F32))


def _sigmoid(z):
    return 1.0 / (1.0 + jnp.exp(-z))


_GELU_C = math.sqrt(2.0 / math.pi)


def _gelu_and_grad(z):
    inner = _GELU_C * (z + 0.044715 * z * z * z)
    t = jnp.tanh(inner)
    val = 0.5 * z * (1.0 + t)
    grad = 0.5 * (1.0 + t) + 0.5 * z * (1.0 - t * t) * _GELU_C * (1.0 + 3.0 * 0.044715 * z * z)
    return val, grad


def _row_tile(n, want=512):
    t = min(want, n)
    assert n % t == 0
    return t


def _pcall(body, operands, *, name, grid, in_specs, out_specs, out_shape, scratch_shapes=(), semantics, riders=()):
    single = not isinstance(out_shape, (tuple, list))
    out_specs = (out_specs,) if single else tuple(out_specs)
    out_shape = (out_shape,) if single else tuple(out_shape)
    if not riders:
        res = _pallas(body, name=name, grid=grid, in_specs=list(in_specs), out_specs=out_specs,
                             out_shape=out_shape, scratch_shapes=list(scratch_shapes),
                             compiler_params=_cp(*semantics))(*operands)
        return (res[0] if single else res), []
    n_in, n_out, n_scr = len(in_specs), len(out_shape), len(scratch_shapes)
    ex_in = [len(ex.operands) for ex in riders]
    ex_out = [len(ex.out_shape) for ex in riders]
    ex_scr = [len(ex.scratch) for ex in riders]
    tot_in, tot_out = n_in + sum(ex_in), n_out + sum(ex_out)

    def wrapped(*refs):
        ins, outs, scr = refs[:tot_in], refs[tot_in:tot_in + tot_out], refs[tot_in + tot_out:]
        ids = [pl.program_id(d) for d in range(len(grid))]
        first = functools.reduce(lambda p, t: p & t, [i == 0 for i in ids])
        last = functools.reduce(lambda p, t: p & t, [i == n - 1 for i, n in zip(ids, grid)])
        parts, oi, oo, os_ = [], n_in, n_out, n_scr
        for k in range(len(riders)):
            parts.append((ins[oi:oi + ex_in[k]], outs[oo:oo + ex_out[k]], scr[os_:os_ + ex_scr[k]]))
            oi, oo, os_ = oi + ex_in[k], oo + ex_out[k], os_ + ex_scr[k]

        @pl.when(first)
        def _():
            for ex, part in zip(riders, parts):
                ex.start(*part)

        body(*ins[:n_in], *outs[:n_out], *scr[:n_scr])

        @pl.when(last)
        def _():
            for ex, part in zip(riders, parts):
                ex.finish(*part)

    aliases, oi, oo = {}, n_in, n_out
    all_ops, all_shapes, all_scr = list(operands), list(out_shape), list(scratch_shapes)
    for k, ex in enumerate(riders):
        aliases.update({oi + a: oo + b for a, b in ex.aliases.items()})
        oi, oo = oi + ex_in[k], oo + ex_out[k]
        all_ops += list(ex.operands)
        all_shapes += [pltpu.HBM(s.shape, s.dtype) for s in ex.out_shape]
        all_scr += list(ex.scratch)
    res = _pallas(
        wrapped, name=name, grid=grid, in_specs=list(in_specs) + [HBM_SPEC] * sum(ex_in),
        out_specs=out_specs + (HBM_SPEC,) * sum(ex_out), out_shape=tuple(all_shapes), scratch_shapes=all_scr,
        input_output_aliases=aliases, compiler_params=_cp(*(["arbitrary"] * len(grid))))(*all_ops)
    own = res[0] if single else tuple(res[:n_out])
    carried, oo = [], n_out
    for k in range(len(riders)):
        carried.append(list(res[oo:oo + ex_out[k]]))
        oo += ex_out[k]
    return own, carried


def _matmul(name, operands, *, grid, in_specs, o_spec, out_shape, out_dtype, dims, has_res=False, riders=()):
    nk = grid[2]
    assert nk == 1 or (out_dtype == F32 and not has_res)

    def body(*refs):
        if has_res:
            a_ref, b_ref, r_ref, o_ref = refs
        else:
            a_ref, b_ref, o_ref = refs
            r_ref = None
        part = lax.dot_general(a_ref[...].astype(BF16), b_ref[...].astype(BF16), (dims, ((), ())),
                               preferred_element_type=F32)
        if nk == 1:
            if r_ref is not None:
                part = part + r_ref[...]
            o_ref[...] = part.astype(o_ref.dtype)
        else:
            k = pl.program_id(2)

            @pl.when(k == 0)
            def _():
                o_ref[...] = part

            @pl.when(k > 0)
            def _():
                o_ref[...] += part

    out, carried = _pcall(body, operands, name=name, grid=grid, in_specs=in_specs, out_specs=o_spec,
                          out_shape=jax.ShapeDtypeStruct(out_shape, out_dtype),
                          semantics=("parallel", "parallel", "arbitrary"), riders=riders)
    return (out, carried) if riders else out


NN = ((1,), (0,))
NT = ((1,), (1,))
TN = ((0,), (0,))
_TN_TOKENS = 4096


def _mm_cs(name, a, w, out_dtype, riders=()):
    M, K = a.shape
    nq, _, wd = w.shape
    tm = _row_tile(M)
    return _matmul(name, (a, w), grid=(nq, M // tm, 1),
                   in_specs=[pl.BlockSpec((tm, K), lambda j, i, k: (i, 0)),
                             pl.BlockSpec((None, K, wd), lambda j, i, k: (j, 0, 0))],
                   o_spec=pl.BlockSpec((tm, wd), lambda j, i, k: (i, j)),
                   out_shape=(M, nq * wd), out_dtype=out_dtype, dims=NN, riders=riders)


def _mm_rs(name, a, w, out_dtype, res=None, tn=512):
    M, K = a.shape
    N = w.shape[1]
    tm = _row_tile(M)
    tn = min(tn, N)
    ops = (a, w) if res is None else (a, w, res)
    in_specs = [pl.BlockSpec((tm, K), lambda i, j, k: (i, 0)),
                pl.BlockSpec((K, tn), lambda i, j, k: (0, j))]
    if res is not None:
        in_specs.append(pl.BlockSpec((tm, tn), lambda i, j, k: (i, j)))
    return _matmul(name, ops, grid=(M // tm, N // tn, 1), in_specs=in_specs,
                   o_spec=pl.BlockSpec((tm, tn), lambda i, j, k: (i, j)),
                   out_shape=(M, N), out_dtype=out_dtype, dims=NN, has_res=res is not None)


def _mm_nt_rs(name, g, w, out_dtype, to):
    M, N = g.shape
    K = w.shape[0]
    tm = _row_tile(M)
    return _matmul(name, (g, w), grid=(M // tm, K // to, 1),
                   in_specs=[pl.BlockSpec((tm, N), lambda i, j, k: (i, 0)),
                             pl.BlockSpec((to, N), lambda i, j, k: (j, 0))],
                   o_spec=pl.BlockSpec((tm, to), lambda i, j, k: (i, j)),
                   out_shape=(M, K), out_dtype=out_dtype, dims=NT)


def _mm_nt_cs(name, g, w, out_dtype, riders=(), stacked=False):
    M = g.shape[-2]
    nq, K, wd = w.shape
    tm = _row_tile(M, 256)

    def body(g_ref, w_ref, o_ref):
        acc = None
        for q in range(nq):
            gq = g_ref[q // 2, :, (q % 2) * wd:(q % 2 + 1) * wd] if stacked else g_ref[:, q * wd:(q + 1) * wd]
            part = _dot_nt(gq, w_ref[q])
            acc = part if acc is None else acc + part
        o_ref[...] = acc.astype(o_ref.dtype)

    g_spec = (pl.BlockSpec((2, tm, 2 * wd), lambda i: (0, i, 0)) if stacked
              else pl.BlockSpec((tm, nq * wd), lambda i: (i, 0)))
    out, carried = _pcall(
        body, (g, w), name=name, grid=(M // tm,),
        in_specs=[g_spec, pl.BlockSpec((nq, K, wd), lambda i: (0, 0, 0))],
        out_specs=pl.BlockSpec((tm, K), lambda i: (i, 0)),
        out_shape=jax.ShapeDtypeStruct((M, K), out_dtype), semantics=("parallel",), riders=riders)
    return (out, carried) if riders else out


def _mm_tn_rs(name, a, g, to, tn=512):
    T, M = a.shape
    N = g.shape[1]
    tt = _row_tile(T, _TN_TOKENS)
    tn = min(tn, N)
    return _matmul(name, (a, g), grid=(M // to, N // tn, T // tt),
                   in_specs=[pl.BlockSpec((tt, to), lambda i, j, k: (k, i)),
                             pl.BlockSpec((tt, tn), lambda i, j, k: (k, j))],
                   o_spec=pl.BlockSpec((to, tn), lambda i, j, k: (i, j)),
                   out_shape=(M, N), out_dtype=F32, dims=TN)


def _mm_tn_cs(name, a, g, nq, to, riders=(), stacked=False):
    T, M = a.shape
    wd = g.shape[-1] * (2 if stacked else 1) // nq
    tt = _row_tile(T, _TN_TOKENS)
    g_spec = (pl.BlockSpec((None, tt, wd), lambda i, j, k: (j // 2, k, j % 2)) if stacked
              else pl.BlockSpec((tt, wd), lambda i, j, k: (k, j)))
    return _matmul(name, (a, g), grid=(M // to, nq, T // tt),
                   in_specs=[pl.BlockSpec((tt, to), lambda i, j, k: (k, i)), g_spec],
                   o_spec=pl.BlockSpec((None, to, wd), lambda i, j, k: (j, i, 0)),
                   out_shape=(nq, M, wd), out_dtype=F32, dims=TN, riders=riders)


def _rms_fwd(name, x, g):
    T, D = x.shape
    tm = _row_tile(T)

    def body(x_ref, g_ref, o_ref):
        xv = x_ref[...]
        r = lax.rsqrt(jnp.mean(xv * xv, axis=-1, keepdims=True) + EPS)
        o_ref[...] = (xv * r * g_ref[...]).astype(o_ref.dtype)

    return _pallas(
        body, name=name, grid=(T // tm,),
        in_specs=[pl.BlockSpec((tm, D), lambda i: (i, 0)), pl.BlockSpec((1, D), lambda i: (0, 0))],
        out_specs=pl.BlockSpec((tm, D), lambda i: (i, 0)),
        out_shape=jax.ShapeDtypeStruct((T, D), BF16), compiler_params=_cp("parallel"),
    )(x, g)


def _rms_bwd(name, x, g, dh, dres, riders=()):
    T, D = x.shape
    tm = _row_tile(T)
    has_res = dres is not None

    def body(*refs):
        if has_res:
            x_ref, g_ref, dh_ref, dr_ref, dx_ref, dg_ref = refs
        else:
            x_ref, g_ref, dh_ref, dx_ref, dg_ref = refs

        @pl.when(pl.program_id(0) == 0)
        def _():
            dg_ref[...] = jnp.zeros_like(dg_ref)

        xv = x_ref[...]
        r = lax.rsqrt(jnp.mean(xv * xv, axis=-1, keepdims=True) + EPS)
        n = xv * r
        dhv = dh_ref[...]
        dg_ref[...] += jnp.sum(dhv * n, axis=0, keepdims=True)
        dn = dhv * g_ref[...]
        dx = r * (dn - n * jnp.mean(dn * n, axis=-1, keepdims=True))
        if has_res:
            dx = dx + dr_ref[...]
        dx_ref[...] = dx

    row = pl.BlockSpec((tm, D), lambda i: (i, 0))
    vec = pl.BlockSpec((1, D), lambda i: (0, 0))
    ops = (x, g, dh, dres) if has_res else (x, g, dh)
    out, carried = _pcall(
        body, ops, name=name, grid=(T // tm,),
        in_specs=[row, vec, row] + ([row] if has_res else []),
        out_specs=(row, vec),
        out_shape=(jax.ShapeDtypeStruct((T, D), F32), jax.ShapeDtypeStruct((1, D), F32)),
        semantics=("arbitrary",), riders=riders)
    return (out, carried) if riders else out


def _loss_head(x2, tgt, g):
    T, D = x2.shape
    tm = _row_tile(T)

    def body(x_ref, t_ref, g_ref, dx_ref, dg_ref, loss_ref):
        @pl.when(pl.program_id(0) == 0)
        def _():
            dg_ref[...] = jnp.zeros_like(dg_ref)
            loss_ref[...] = jnp.zeros_like(loss_ref)

        xv = x_ref[...]
        gv = g_ref[...]
        r = lax.rsqrt(jnp.mean(xv * xv, axis=-1, keepdims=True) + EPS)
        n = xv * r
        diff = n * gv - t_ref[...]
        loss_ref[...] += 0.5 * jnp.sum(jnp.mean(diff * diff, axis=-1, keepdims=True))
        dy = diff * (1.0 / D)
        dg_ref[...] += jnp.sum(dy * n, axis=0, keepdims=True)
        dn = dy * gv
        dx_ref[...] = r * (dn - n * jnp.mean(dn * n, axis=-1, keepdims=True))

    row = pl.BlockSpec((tm, D), lambda i: (i, 0))
    vec = pl.BlockSpec((1, D), lambda i: (0, 0))
    return _pallas(
        body, name="loss_head", grid=(T // tm,),
        in_specs=[row, row, vec],
        out_specs=(row, vec, pl.BlockSpec((8, 128), lambda i: (0, 0))),
        out_shape=(jax.ShapeDtypeStruct((T, D), F32), jax.ShapeDtypeStruct((1, D), F32),
                   jax.ShapeDtypeStruct((8, 128), F32)),
        compiler_params=_cp("arbitrary"),
    )(x2, tgt, g)


def _gmlp_pieces(zu, zv, lng, lnb, ws_ref, bs_ref):
    u, du = _gelu_and_grad(zu)
    v, dv = _gelu_and_grad(zv)
    mu = jnp.mean(v, axis=-1, keepdims=True)
    vc = v - mu
    rstd = lax.rsqrt(jnp.mean(vc * vc, axis=-1, keepdims=True) + EPS)
    vhat = vc * rstd
    vn = vhat * lng + lnb
    row = lax.broadcasted_iota(jnp.int32, (GM_CHUNK, GM_CHUNK), 0)
    col = lax.broadcasted_iota(jnp.int32, (GM_CHUNK, GM_CHUNK), 1)
    tril = row >= col
    wms, mixed = [], []
    for g in range(GM_GROUPS):
        sl = slice(g * 128, (g + 1) * 128)
        wm = jnp.where(tril, ws_ref[g], 0.0)
        wms.append(wm)
        mixed.append(_dot(wm, vn[:, sl]) + bs_ref[g])
    return u, du, dv, rstd, vhat, vn, wms, mixed, tril


def _gmlp_fwd(proj, lng, lnb, ws, bs_col):
    T = proj.shape[0]
    n = T // GM_CHUNK

    def body(zu_ref, zv_ref, lng_ref, lnb_ref, ws_ref, bs_ref, o_ref):
        u, _, _, _, _, _, _, mixed, _ = _gmlp_pieces(zu_ref[...].astype(F32), zv_ref[...].astype(F32),
                                                     lng_ref[...], lnb_ref[...],
                                                     ws_ref, bs_ref)
        for g in range(GM_GROUPS):
            sl = slice(g * 128, (g + 1) * 128)
            o_ref[:, sl] = (u[:, sl] * mixed[g]).astype(o_ref.dtype)

    vec = pl.BlockSpec((1, GM_WIDTH), lambda i: (0, 0))
    return _pallas(
        body, name="gmlp_fwd", grid=(n,),
        in_specs=[pl.BlockSpec((GM_CHUNK, 512), lambda i: (i, COL_ZU)),
                  pl.BlockSpec((GM_CHUNK, 512), lambda i: (i, COL_ZV)),
                  vec, vec,
                  pl.BlockSpec((GM_GROUPS, 128, 128), lambda i: (0, 0, 0)),
                  pl.BlockSpec((GM_GROUPS, 128, 1), lambda i: (0, 0, 0))],
        out_specs=pl.BlockSpec((GM_CHUNK, 512), lambda i: (i, 0)),
        out_shape=jax.ShapeDtypeStruct((T, GM_WIDTH), BF16), compiler_params=_cp("parallel"),
    )(proj, proj, lng, lnb, ws, bs_col)


def _gmlp_bwd(proj, d_out, lng, lnb, ws, bs_col):
    T = proj.shape[0]
    n = T // GM_CHUNK

    def body(zu_ref, zv_ref, do_ref, lng_ref, lnb_ref, ws_ref, bs_ref,
             dzu_ref, dzv_ref, dws_ref, dbs_ref, dlng_ref, dlnb_ref, dm_acc):
        i = pl.program_id(0)

        @pl.when(i == 0)
        def _():
            dws_ref[...] = jnp.zeros_like(dws_ref)
            dlng_ref[...] = jnp.zeros_like(dlng_ref)
            dlnb_ref[...] = jnp.zeros_like(dlnb_ref)
            dm_acc[...] = jnp.zeros_like(dm_acc)

        lng_v = lng_ref[...]
        u, du, dv, rstd, vhat, vn, wms, mixed, tril = _gmlp_pieces(zu_ref[...].astype(F32), zv_ref[...].astype(F32),
                                                                  lng_v, lnb_ref[...],
                                                                  ws_ref, bs_ref)
        do = do_ref[...]
        dvn_parts = []
        for g in range(GM_GROUPS):
            sl = slice(g * 128, (g + 1) * 128)
            dog = do[:, sl]
            dzu_ref[:, sl] = (dog * mixed[g] * du[:, sl]).astype(dzu_ref.dtype)
            dmix = dog * u[:, sl]
            dm_acc[:, sl] += dmix
            dws_ref[g] += jnp.where(tril, _dot_nt(dmix, vn[:, sl]), 0.0)
            dvn_parts.append(_dot_tn(wms[g], dmix))
        dvn = jnp.concatenate(dvn_parts, axis=1)
        dlng_ref[...] += jnp.sum(dvn * vhat, axis=0, keepdims=True)
        dlnb_ref[...] += jnp.sum(dvn, axis=0, keepdims=True)
        dvh = dvn * lng_v
        dvv = rstd * (dvh - jnp.mean(dvh, axis=-1, keepdims=True)
                      - vhat * jnp.mean(dvh * vhat, axis=-1, keepdims=True))
        dzv_ref[...] = (dvv * dv).astype(dzv_ref.dtype)

        @pl.when(i == n - 1)
        def _():
            for g in range(GM_GROUPS):
                dbs_ref[g] = jnp.sum(dm_acc[:, g * 128:(g + 1) * 128], axis=1, keepdims=True)

    vec = pl.BlockSpec((1, GM_WIDTH), lambda i: (0, 0))
    wsp = pl.BlockSpec((GM_GROUPS, 128, 128), lambda i: (0, 0, 0))
    bsp = pl.BlockSpec((GM_GROUPS, 128, 1), lambda i: (0, 0, 0))
    tile = pl.BlockSpec((GM_CHUNK, 512), lambda i: (i, 0))
    return _pallas(
        body, name="gmlp_bwd", grid=(n,),
        in_specs=[pl.BlockSpec((GM_CHUNK, 512), lambda i: (i, COL_ZU)),
                  pl.BlockSpec((GM_CHUNK, 512), lambda i: (i, COL_ZV)),
                  pl.BlockSpec((None, GM_CHUNK, 512), lambda i: (0, i, 0)), vec, vec, wsp, bsp],
        out_specs=(tile, tile, wsp, bsp, vec, vec),
        out_shape=(jax.ShapeDtypeStruct((T, GM_WIDTH), BF16), jax.ShapeDtypeStruct((T, GM_WIDTH), BF16),
                   jax.ShapeDtypeStruct((GM_GROUPS, 128, 128), F32), jax.ShapeDtypeStruct((GM_GROUPS, 128, 1), F32),
                   jax.ShapeDtypeStruct((1, GM_WIDTH), F32), jax.ShapeDtypeStruct((1, GM_WIDTH), F32)),
        scratch_shapes=[pltpu.VMEM((GM_CHUNK, GM_WIDTH), F32)],
        compiler_params=_cp("arbitrary"),
    )(proj, proj, d_out, lng, lnb, ws, bs_col)


def _hgrn_lower_bound(lbl):
    return 1.0 / (1.0 + jnp.exp(lbl[1:2, :] - lbl[0:1, :]))


def _hgrn_gates(hq, hf, lb):
    C = HG_CHUNK
    sg = _sigmoid(hf)
    fg = lb + (1.0 - lb) * sg
    sq = _sigmoid(hq)
    row = lax.broadcasted_iota(jnp.int32, (C, C), 0)
    col = lax.broadcasted_iota(jnp.int32, (C, C), 1)
    tril = row >= col
    logf = jnp.log(fg)
    a = _dot_01(tril, logf)
    a_last = jnp.sum(logf, axis=0, keepdims=True)
    first_half = lax.broadcasted_iota(jnp.int32, logf.shape, 0) < (C // 2)
    a_mid = jnp.sum(jnp.where(first_half, logf, 0.0), axis=0, keepdims=True)
    ea, ei, eki, ekl = jnp.exp(a), jnp.exp(a - a_mid), jnp.exp(a_mid - a), jnp.exp(a_last - a)
    k = 1.0 - fg
    q = hq * sq
    qi = (q * ei).astype(BF16).astype(F32)
    ki = (k * eki).astype(BF16).astype(F32)
    return dict(sg=sg, fg=fg, sq=sq, tril=tril, ea=ea, ei=ei, eki=eki, ekl=ekl, e_last=jnp.exp(a_last),
                qe=q * ea, qi=qi, ki=ki, kl=k * ekl)


def _heads(x):
    return [x[:, h * HG_DIM:(h + 1) * HG_DIM] for h in range(HG_HEADS)]


def _hgrn_fwd(proj, lbl, gh, B, S, riders=()):
    C = HG_CHUNK
    NC = S // C
    W = HG_HEADS * HG_DIM

    def body(q_ref, f_ref, i_ref, g_ref, lbl_ref, gh_ref, o_ref, bo_ref, st_ref, state):
        @pl.when(pl.program_id(0) == 0)
        def _():
            state[...] = jnp.zeros_like(state)

        lb = _hgrn_lower_bound(lbl_ref[...])
        ghv = gh_ref[...]
        for b in range(B):
            gt = _hgrn_gates(q_ref[b].astype(F32), f_ref[b].astype(F32), lb)
            v = _heads(i_ref[b])
            qe, qi, ki, kl, e_last = (_heads(gt[n]) for n in ("qe", "qi", "ki", "kl", "e_last"))
            outs, normed = [], []
            for h in range(HG_HEADS):
                p = jnp.where(gt["tril"], _dot_nt(qi[h], ki[h]), 0.0)
                st = state[b, h]
                st_ref[b, h] = st
                o = _dot_nt(qe[h], st) + _dot(p, v[h])
                state[b, h] = st * e_last[h] + _dot_tn(v[h], kl[h])
                outs.append(o)
                normed.append(o * lax.rsqrt(jnp.mean(o * o, axis=-1, keepdims=True) + EPS) * ghv)
            o_ref[b] = jnp.concatenate(outs, axis=1)
            hg = g_ref[b].astype(F32)
            bo_ref[b] = (jnp.concatenate(normed, axis=1) * (hg * _sigmoid(hg))).astype(bo_ref.dtype)

    def col(cb):
        return pl.BlockSpec((B, C, 512), lambda c: (0, c, cb))

    tile = pl.BlockSpec((B, C, W), lambda c: (0, c, 0))
    proj3 = proj.reshape(B, S, proj.shape[-1])
    out, carried = _pcall(
        body, (proj3, proj3, proj3, proj3, lbl, gh), name="hgrn_fwd", grid=(NC,),
        in_specs=[col(COL_HQ), col(COL_HF), col(COL_HI), col(COL_HG),
                  pl.BlockSpec((2, W), lambda c: (0, 0)), pl.BlockSpec((1, HG_DIM), lambda c: (0, 0))],
        out_specs=(tile, tile, pl.BlockSpec((B, None, HG_HEADS, 128, 128), lambda c: (0, c, 0, 0, 0))),
        out_shape=(jax.ShapeDtypeStruct((B, S, W), F32), jax.ShapeDtypeStruct((B, S, W), BF16),
                   jax.ShapeDtypeStruct((B, NC, HG_HEADS, 128, 128), F32)),
        scratch_shapes=[pltpu.VMEM((B, HG_HEADS, 128, 128), F32)],
        semantics=("arbitrary",), riders=riders)
    o_h, b_out, states = out
    out = (o_h, b_out.reshape(B * S, W), states)
    return (out, carried) if riders else out


def _hgrn_bwd(proj, o_saved, states, d_out, lbl, gh, B, S, riders=()):
    C = HG_CHUNK
    NC = S // C
    W = HG_HEADS * HG_DIM

    def body(q_ref, f_ref, i_ref, g_ref, o_ref, st_ref, do_ref, lbl_ref, gh_ref,
             dq_ref, df_ref, di_ref, dg_ref, dlbl_ref, dgh_ref, dstate, dlb_acc):
        c = pl.program_id(0)

        @pl.when(c == 0)
        def _():
            dstate[...] = jnp.zeros_like(dstate)
            dgh_ref[...] = jnp.zeros_like(dgh_ref)
            dlb_acc[...] = jnp.zeros_like(dlb_acc)

        lb = _hgrn_lower_bound(lbl_ref[...])
        ghv = gh_ref[...]
        row = lax.broadcasted_iota(jnp.int32, (C, C), 0)
        colm = lax.broadcasted_iota(jnp.int32, (C, C), 1)
        triu = colm >= row
        for b in range(B):
            hq, hg = q_ref[b].astype(F32), g_ref[b].astype(F32)
            gt = _hgrn_gates(hq, f_ref[b].astype(F32), lb)
            tril = gt["tril"]
            v = _heads(i_ref[b])
            qe, qi, ki, kl, e_last = (_heads(gt[n]) for n in ("qe", "qi", "ki", "kl", "e_last"))
            sgg = _sigmoid(hg)
            don_all = do_ref[b] * (hg * sgg)
            o, don = _heads(o_ref[b]), _heads(don_all)
            d_qe, d_qi, d_ki, d_kl, dv, n_all, dal = [], [], [], [], [], [], []
            for h in range(HG_HEADS):
                r = lax.rsqrt(jnp.mean(o[h] * o[h], axis=-1, keepdims=True) + EPS)
                n = o[h] * r
                n_all.append(n)
                dgh_ref[...] += jnp.sum(don[h] * n, axis=0, keepdims=True)
                dn = don[h] * ghv
                d_o = r * (dn - n * jnp.mean(dn * n, axis=-1, keepdims=True))
                st, dst = st_ref[b, h], dstate[b, h]
                p = jnp.where(tril, _dot_nt(qi[h], ki[h]), 0.0)
                dp = jnp.where(tril, _dot_nt(d_o, v[h]), 0.0)
                d_qe.append(_dot(d_o, st))
                d_qi.append(_dot(dp, ki[h]))
                d_ki.append(_dot_tn(dp, qi[h]))
                d_kl.append(_dot(v[h], dst))
                dv.append(_dot_tn(p, d_o) + _dot_nt(kl[h], dst))
                dstate[b, h] = dst * e_last[h] + _dot_tn(d_o, qe[h])
                dal.append(jnp.sum(dst * st, axis=0, keepdims=True) * e_last[h])
            d_qe, d_qi, d_ki, d_kl, n_all, dal = (jnp.concatenate(t, axis=1)
                                                  for t in (d_qe, d_qi, d_ki, d_kl, n_all, dal))
            dg_ref[b] = (do_ref[b] * n_all * jnp.tile(ghv, (1, HG_HEADS))
                         * (sgg * (1.0 + hg * (1.0 - sgg)))).astype(dg_ref.dtype)
            di_ref[b] = jnp.concatenate(dv, axis=1).astype(di_ref.dtype)
            d_a_last = dal + jnp.sum(d_kl * gt["kl"], axis=0, keepdims=True)
            dq = d_qe * gt["ea"] + d_qi * gt["ei"]
            dk = d_ki * gt["eki"] + d_kl * gt["ekl"]
            da = d_qe * gt["qe"] + d_qi * gt["qi"] - d_ki * gt["ki"] - d_kl * gt["kl"]
            dlogf = _dot_01(triu, da) + d_a_last
            sg, sq = gt["sg"], gt["sq"]
            dfg = dlogf / gt["fg"] - dk
            df_ref[b] = (dfg * (1.0 - lb) * sg * (1.0 - sg)).astype(df_ref.dtype)
            dlb_acc[...] += jnp.sum(dfg * (1.0 - sg), axis=0, keepdims=True)
            dq_ref[b] = (dq * (sq * (1.0 + hq * (1.0 - sq)))).astype(dq_ref.dtype)

        @pl.when(c == NC - 1)
        def _():
            dlb = dlb_acc[...]
            first = lax.broadcasted_iota(jnp.int32, (2, W), 0) == 0
            dlbl_ref[...] = jnp.where(first, dlb * lb * (1.0 - lb), -dlb * lb * (1.0 - lb))

    def col(cb):
        return pl.BlockSpec((B, C, 512), lambda c: (0, NC - 1 - c, cb))

    tile = pl.BlockSpec((B, C, W), lambda c: (0, NC - 1 - c, 0))
    proj3 = proj.reshape(B, S, proj.shape[-1])
    d3 = jax.ShapeDtypeStruct((B, S, W), BF16)
    out, carried = _pcall(
        body, (proj3, proj3, proj3, proj3, o_saved, states, d_out.reshape(3, B, S, W), lbl, gh), name="hgrn_bwd",
        grid=(NC,),
        in_specs=[col(COL_HQ), col(COL_HF), col(COL_HI), col(COL_HG), tile,
                  pl.BlockSpec((B, None, HG_HEADS, 128, 128), lambda c: (0, NC - 1 - c, 0, 0, 0)),
                  pl.BlockSpec((None, B, C, W), lambda c: (1, 0, NC - 1 - c, 0)),
                  pl.BlockSpec((2, W), lambda c: (0, 0)), pl.BlockSpec((1, HG_DIM), lambda c: (0, 0))],
        out_specs=(tile, tile, tile, tile,
                   pl.BlockSpec((2, W), lambda c: (0, 0)), pl.BlockSpec((1, HG_DIM), lambda c: (0, 0))),
        out_shape=(d3, d3, d3, d3, jax.ShapeDtypeStruct((2, W), F32), jax.ShapeDtypeStruct((1, HG_DIM), F32)),
        scratch_shapes=[pltpu.VMEM((B, HG_HEADS, 128, 128), F32), pltpu.VMEM((1, W), F32)],
        semantics=("arbitrary",), riders=riders)
    out = tuple(t.reshape(B * S, W) for t in out[:4]) + tuple(out[4:])
    return (out, carried) if riders else out


_XA_SCALE = XA_DIM ** -0.5


def _attn_probs(qh, kh):
    s = _dot_nt(qh, kh) * _XA_SCALE
    e = jnp.exp(s - jnp.max(s, axis=-1, keepdims=True))
    return e / jnp.sum(e, axis=-1, keepdims=True)


def _attn_fwd(proj, kv, B, S):
    T = B * S
    tq = _row_tile(S)
    nq = S // tq
    W = XA_HEADS * XA_DIM

    def body(q_ref, kv_ref, o_ref):
        for h in range(XA_HEADS):
            sl = slice(h * 128, (h + 1) * 128)
            p = _attn_probs(q_ref[:, sl], kv_ref[:, sl])
            o_ref[:, sl] = _dot(p, kv_ref[:, W + h * 128:W + (h + 1) * 128]).astype(o_ref.dtype)

    return _pallas(
        body, name="attn_fwd", grid=(B, nq),
        in_specs=[pl.BlockSpec((tq, 512), lambda b, i: (b * nq + i, COL_XQ)),
                  pl.BlockSpec((MEM_LEN, 2 * W), lambda b, i: (b, 0))],
        out_specs=pl.BlockSpec((tq, W), lambda b, i: (b * nq + i, 0)),
        out_shape=jax.ShapeDtypeStruct((T, W), BF16), compiler_params=_cp("parallel", "parallel"),
    )(proj, kv)


def _attn_bwd(proj, kv, d_out, B, S):
    T = B * S
    tq = _row_tile(S)
    nq = S // tq
    W = XA_HEADS * XA_DIM

    def body(q_ref, kv_ref, do_ref, dq_ref, dkv_ref):
        @pl.when(pl.program_id(1) == 0)
        def _():
            dkv_ref[...] = jnp.zeros_like(dkv_ref)

        for h in range(XA_HEADS):
            sl = slice(h * 128, (h + 1) * 128)
            slv = slice(W + h * 128, W + (h + 1) * 128)
            qh = q_ref[:, sl]
            kh = kv_ref[:, sl]
            p = _attn_probs(qh, kh)
            dc = do_ref[:, sl]
            dp = _dot_nt(dc, kv_ref[:, slv])
            ds = p * (dp - jnp.sum(dp * p, axis=-1, keepdims=True)) * _XA_SCALE
            dq_ref[:, sl] = _dot(ds, kh).astype(dq_ref.dtype)
            dkv_ref[:, sl] += _dot_tn(ds, qh)
            dkv_ref[:, slv] += _dot_tn(p, dc)

    kvspec = pl.BlockSpec((MEM_LEN, 2 * W), lambda b, i: (b, 0))
    tile = pl.BlockSpec((tq, W), lambda b, i: (b * nq + i, 0))
    return _pallas(
        body, name="attn_bwd", grid=(B, nq),
        in_specs=[pl.BlockSpec((tq, 512), lambda b, i: (b * nq + i, COL_XQ)), kvspec,
                  pl.BlockSpec((None, tq, W), lambda b, i: (2, b * nq + i, 0))],
        out_specs=(tile, kvspec),
        out_shape=(jax.ShapeDtypeStruct((T, W), BF16), jax.ShapeDtypeStruct((B * MEM_LEN, 2 * W), F32)),
        compiler_params=_cp("parallel", "arbitrary"),
    )(proj, kv, d_out)


_MERGE_TM = 256
_GATE_W = 512


def _gate_specs(tm):
    base = COL_GATE0 // _GATE_W
    return [pl.BlockSpec((tm, _GATE_W), functools.partial(lambda i, k: (i, base + k), k=k)) for k in range(6)]


def _merge_fwd(a_out, b_out, c_out, wb, proj):
    T = a_out.shape[0]
    tm = _row_tile(T, _MERGE_TM)
    nq, _, wd = wb.shape
    per_half = _GATE_W // wd

    def body(a_ref, b_ref, c_ref, w_ref, *rest):
        gates, (m_ref, up_ref) = rest[:6], rest[6:]
        for hf in range(2):
            cols = slice(hf * _GATE_W, (hf + 1) * _GATE_W)
            acc = None
            for n, br in enumerate((a_ref, b_ref, c_ref)):
                x = br[...]
                up = jnp.concatenate([_dot(x, w_ref[per_half * hf + j, n * BR_WIDTH:(n + 1) * BR_WIDTH, :])
                                      for j in range(per_half)], axis=1)
                up_ref[n, :, cols] = up.astype(up_ref.dtype)
                term = _sigmoid(gates[2 * n + hf][...].astype(F32)) * up
                acc = term if acc is None else acc + term
            m_ref[:, cols] = acc.astype(m_ref.dtype)

    br_spec = pl.BlockSpec((tm, BR_WIDTH), lambda i: (i, 0))
    return _pallas(
        body, name="merge_fwd", grid=(T // tm,),
        in_specs=[br_spec, br_spec, br_spec,
                  pl.BlockSpec((nq, 3 * BR_WIDTH, wd), lambda i: (0, 0, 0))] + _gate_specs(tm),
        out_specs=(pl.BlockSpec((tm, D_MODEL), lambda i: (i, 0)), pl.BlockSpec((3, tm, D_MODEL), lambda i: (0, i, 0))),
        out_shape=(jax.ShapeDtypeStruct((T, D_MODEL), BF16), jax.ShapeDtypeStruct((3, T, D_MODEL), BF16)),
        compiler_params=_cp("parallel"),
    )(a_out, b_out, c_out, wb, *([proj] * 6))


def _branch_bwd_act(d_ups, wb):
    _, T, D = d_ups.shape
    nq, _, wd = wb.shape
    tm = _row_tile(T)

    def body(d_ref, w_ref, o_ref):
        acc = None
        for q in range(nq):
            part = _dot_nt(d_ref[:, q * wd:(q + 1) * wd], w_ref[q])
            acc = part if acc is None else acc + part
        o_ref[...] = acc

    return _pallas(
        body, name="d_branch", grid=(3, T // tm),
        in_specs=[pl.BlockSpec((None, tm, D), lambda n, i: (n, i, 0)),
                  pl.BlockSpec((nq, BR_WIDTH, wd), lambda n, i: (0, n, 0))],
        out_specs=pl.BlockSpec((None, tm, BR_WIDTH), lambda n, i: (n, i, 0)),
        out_shape=jax.ShapeDtypeStruct((3, T, BR_WIDTH), F32), compiler_params=_cp("parallel", "parallel"),
    )(d_ups, wb)


def _branch_bwd_weight(name, br, d_ups, n):
    T = br.shape[0]
    D = d_ups.shape[2]
    wd = D // N_CHIPS
    tt = _row_tile(T, _TN_TOKENS)

    def body(b_ref, d_ref, o_ref):
        k = pl.program_id(0)
        for q in range(N_CHIPS):
            part = _dot_tn(b_ref[...], d_ref[:, q * wd:(q + 1) * wd])

            @pl.when(k == 0)
            def _():
                o_ref[q] = part

            @pl.when(k > 0)
            def _():
                o_ref[q] += part

    return _pallas(
        body, name=name, grid=(T // tt,),
        in_specs=[pl.BlockSpec((tt, BR_WIDTH), lambda k: (k, 0)),
                  pl.BlockSpec((None, tt, D), lambda k: (n, k, 0))],
        out_specs=pl.BlockSpec((N_CHIPS, BR_WIDTH, wd), lambda k: (0, 0, 0)),
        out_shape=jax.ShapeDtypeStruct((N_CHIPS, BR_WIDTH, wd), F32), compiler_params=_cp("arbitrary"),
    )(br, d_ups)


def _merge_bwd(d_merged, ups, proj):
    T = d_merged.shape[0]
    tm = _row_tile(T, _MERGE_TM)

    def body(dm_ref, up_ref, *rest):
        gates, (dup_ref, dg0_ref, dg1_ref, dg2_ref) = rest[:6], rest[6:]
        for hf in range(2):
            cols = slice(hf * _GATE_W, (hf + 1) * _GATE_W)
            dm = dm_ref[:, cols]
            for n, dgr in enumerate((dg0_ref, dg1_ref, dg2_ref)):
                gate = _sigmoid(gates[2 * n + hf][...].astype(F32))
                dup_ref[n, :, cols] = (dm * gate).astype(dup_ref.dtype)
                dgr[:, cols] = (dm * up_ref[n, :, cols].astype(F32) * gate * (1.0 - gate)).astype(dgr.dtype)

    tile = pl.BlockSpec((tm, D_MODEL), lambda i: (i, 0))
    tile3 = pl.BlockSpec((3, tm, D_MODEL), lambda i: (0, i, 0))
    return _pallas(
        body, name="merge_bwd", grid=(T // tm,),
        in_specs=[tile, tile3] + _gate_specs(tm),
        out_specs=(tile3, tile, tile, tile),
        out_shape=(jax.ShapeDtypeStruct((3, T, D_MODEL), BF16),) + (jax.ShapeDtypeStruct((T, D_MODEL), BF16),) * 3,
        compiler_params=_cp("parallel"),
    )(d_merged, ups, *([proj] * 6))


_CONV_TF = D_FF // 2
_CONV_TS = 256
_HALO = 16


def _conv_fwd(ab, cw, cb, B, S):
    T = B * S
    ts = _row_tile(S, _CONV_TS)
    tf = _CONV_TF
    nb = D_FF // tf
    tps = S // ts
    hb = ts // _HALO

    def body(a_ref, p_ref, b_ref, w_ref, cb_ref, o_ref):
        start = (pl.program_id(0) % tps) == 0
        a = a_ref[...].astype(F32)
        prev = jnp.where(start, 0.0, p_ref[...].astype(F32))
        ext = jnp.concatenate([prev, a], axis=0)
        a1 = pltpu.roll(ext, 1, 0)[_HALO:, :]
        a2 = pltpu.roll(ext, 2, 0)[_HALO:, :]
        ac = cb_ref[...] + w_ref[0] * a2 + w_ref[1] * a1 + w_ref[2] * a
        o_ref[...] = (ac * _sigmoid(ac) * b_ref[...].astype(F32)).astype(o_ref.dtype)

    return _pallas(
        body, name="conv_fwd", grid=(T // ts, nb),
        in_specs=[pl.BlockSpec((ts, tf), lambda i, j: (i, j)),
                  pl.BlockSpec((_HALO, tf), lambda i, j: (jnp.maximum(i * hb - 1, 0), j)),
                  pl.BlockSpec((ts, tf), lambda i, j: (i, j + nb)),
                  pl.BlockSpec((3, 1, tf), lambda i, j: (0, 0, j)),
                  pl.BlockSpec((1, tf), lambda i, j: (0, j))],
        out_specs=pl.BlockSpec((ts, tf), lambda i, j: (i, j)),
        out_shape=jax.ShapeDtypeStruct((T, D_FF), BF16), compiler_params=_cp("parallel", "parallel"),
    )(ab, ab, ab, cw, cb)


def _conv_bwd(ab, d_ff, cw, cb, B, S):
    T = B * S
    ts = _row_tile(S, _CONV_TS)
    tf = _CONV_TF
    nb = D_FF // tf
    tps = S // ts
    hb = ts // _HALO
    last_h = T // _HALO - 1
    n_ext = ts + _HALO

    def body(a_ref, ap_ref, an_ref, b_ref, bn_ref, d_ref, dn_ref, w_ref, cb_ref, dab_ref, dw_ref, dcb_ref):
        i = pl.program_id(1)

        @pl.when(i == 0)
        def _():
            dw_ref[...] = jnp.zeros_like(dw_ref)
            dcb_ref[...] = jnp.zeros_like(dcb_ref)

        start = (i % tps) == 0
        end = (i % tps) == tps - 1
        a = a_ref[...].astype(F32)
        ext = jnp.concatenate([jnp.where(start, 0.0, ap_ref[...].astype(F32)), a, an_ref[...].astype(F32)], axis=0)
        r1 = pltpu.roll(ext, 1, 0)[_HALO:, :]
        r2 = pltpu.roll(ext, 2, 0)[_HALO:, :]
        ac = cb_ref[...] + w_ref[0] * r2 + w_ref[1] * r1 + w_ref[2] * ext[_HALO:, :]
        sg = _sigmoid(ac)
        d_e = jnp.concatenate([d_ref[...].astype(F32), jnp.where(end, 0.0, dn_ref[...].astype(F32))], axis=0)
        b_e = jnp.concatenate([b_ref[...].astype(F32), bn_ref[...].astype(F32)], axis=0)
        dab_ref[1] = (d_e[:ts, :] * (ac * sg)[:ts, :]).astype(dab_ref.dtype)
        dac = d_e * b_e * sg * (1.0 + ac * (1.0 - sg))
        u1 = pltpu.roll(dac, n_ext - 1, 0)[:ts, :]
        u2 = pltpu.roll(dac, n_ext - 2, 0)[:ts, :]
        dac0 = dac[:ts, :]
        dab_ref[0] = (w_ref[2] * dac0 + w_ref[1] * u1 + w_ref[0] * u2).astype(dab_ref.dtype)
        dcb_ref[...] += jnp.sum(dac0, axis=0, keepdims=True)
        dw_ref[2] += jnp.sum(dac0 * a, axis=0, keepdims=True)
        dw_ref[1] += jnp.sum(dac0 * r1[:ts, :], axis=0, keepdims=True)
        dw_ref[0] += jnp.sum(dac0 * r2[:ts, :], axis=0, keepdims=True)

    def cur(off):
        return pl.BlockSpec((ts, tf), lambda j, i: (i, j + off))

    def nxt(off):
        return pl.BlockSpec((_HALO, tf), lambda j, i: (jnp.minimum((i + 1) * hb, last_h), j + off))

    return _pallas(
        body, name="conv_bwd", grid=(nb, T // ts),
        in_specs=[cur(0), pl.BlockSpec((_HALO, tf), lambda j, i: (jnp.maximum(i * hb - 1, 0), j)), nxt(0),
                  cur(nb), nxt(nb), cur(0), nxt(0),
                  pl.BlockSpec((3, 1, tf), lambda j, i: (0, 0, j)), pl.BlockSpec((1, tf), lambda j, i: (0, j))],
        out_specs=(pl.BlockSpec((2, ts, tf), lambda j, i: (0, i, j)), pl.BlockSpec((3, 1, tf), lambda j, i: (0, 0, j)),
                   pl.BlockSpec((1, tf), lambda j, i: (0, j))),
        out_shape=(jax.ShapeDtypeStruct((2, T, D_FF), BF16),
                   jax.ShapeDtypeStruct((3, 1, D_FF), F32), jax.ShapeDtypeStruct((1, D_FF), F32)),
        compiler_params=_cp("parallel", "arbitrary"),
    )(ab, ab, ab, ab, ab, d_ff, d_ff, cw, cb)


def _local_step(x, mem, tgt, p, comm, B, S):
    g = {}
    h = _rms_fwd("norm1", x, p["norm1_g"])
    proj = comm.carry("in_proj", lambda r: _mm_cs("in_proj", h, comm.w("w_in"), BF16, riders=r))
    a_out = _gmlp_fwd(proj, p["ln_v_g"], p["ln_v_b"], p["w_spatial"], p["b_spatial"])
    o_h, b_out, states = comm.carry(
        "hgrn_fwd", lambda r: _hgrn_fwd(proj, p["lb_logits"], p["hgrn_norm_g"], B, S, riders=r))
    memn = _rms_fwd("mem_norm", mem, p["mem_norm_g"])
    kv = _mm_rs("mem_kv", memn, comm.w("w_mem_kv"), F32)
    c_out = _attn_fwd(proj, kv, B, S)
    merged, ups = _merge_fwd(a_out, b_out, c_out, comm.w("w_branch"), proj)
    x1 = _mm_rs("out_proj", merged, comm.w("w_out"), F32, res=x)
    h2 = _rms_fwd("norm2", x1, p["norm2_g"])
    ab = _mm_cs("up_proj", h2, comm.w("w_up"), BF16)
    conv_w = comm.w("conv_w")
    ff = _conv_fwd(ab, conv_w, p["conv_b"], B, S)
    x2 = _mm_rs("down_proj", ff, comm.w("w_down"), F32, res=x1)
    dx2, g["final_g"], loss = _loss_head(x2, tgt, p["final_g"])

    comm.grad("w_down", _mm_tn_rs("g_w_down", ff, dx2, to=D_FF // 2))
    d_ff = _mm_nt_rs("d_ff", dx2, comm.w("w_down"), BF16, to=D_FF // 2)
    d_ab, g["conv_w"], g["conv_b"] = _conv_bwd(ab, d_ff, conv_w, p["conv_b"], B, S)
    comm.grad("w_up", _mm_tn_cs("g_w_up", h2, d_ab, N_CHIPS, to=512, stacked=True))
    d_h2 = comm.carry("d_h2", lambda r: _mm_nt_cs("d_h2", d_ab, comm.w("w_up"), F32, riders=r, stacked=True))
    d_x1, g["norm2_g"] = _rms_bwd("norm2_bwd", x1, p["norm2_g"], d_h2, dx2)
    comm.grad("w_out", _mm_tn_rs("g_w_out", merged, d_x1, to=512))
    d_merged = _mm_nt_rs("d_merged", d_x1, comm.w("w_out"), F32, to=512)
    d_ups, d_g0, d_g1, d_g2 = _merge_bwd(d_merged, ups, proj)

    d_br = _branch_bwd_act(d_ups, comm.w("w_branch"))
    comm.grad("w_branch", jnp.concatenate(
        [_branch_bwd_weight("g_w_branch%d" % n, br, d_ups, n) for n, br in enumerate((a_out, b_out, c_out))],
        axis=1))

    d_zu, d_zv, g["w_spatial"], g["b_spatial"], g["ln_v_g"], g["ln_v_b"] = _gmlp_bwd(
        proj, d_br, p["ln_v_g"], p["ln_v_b"], p["w_spatial"], p["b_spatial"])
    d_xq, d_kv = _attn_bwd(proj, kv, d_br, B, S)
    comm.grad("w_mem_kv", _mm_tn_rs("g_w_mem_kv", memn, d_kv, to=512))
    d_memn = _mm_nt_rs("d_memn", d_kv, comm.w("w_mem_kv"), F32, to=512)
    _, g["mem_norm_g"] = _rms_bwd("mem_norm_bwd", mem, p["mem_norm_g"], d_memn, None)
    d_hq, d_hf, d_hi, d_hg, g["lb_logits"], g["hgrn_norm_g"] = comm.carry(
        "hgrn_bwd", lambda r: _hgrn_bwd(proj, o_h, states, d_br, p["lb_logits"], p["hgrn_norm_g"], B, S, riders=r))
    d_proj = jnp.concatenate([d_zu, d_zv, d_hq, d_hf, d_hi, d_hg, d_xq, d_g0, d_g1, d_g2], axis=1)
    comm.small_grads([g[n].reshape(_SMALL_SHAPE[n]) for n in _SMALL_EARLY] + [loss])
    comm.grad("w_in", comm.carry("g_w_in", lambda r: _mm_tn_cs("g_w_in", h, d_proj, N_CHIPS, to=512, riders=r)))
    d_h =comm.carry("d_h", lambda r: _mm_nt_cs("d_h", d_proj, comm.w("w_in"), F32, riders=r))
    grad_x, g["norm1_g"] = _rms_bwd("norm1_bwd", x, p["norm1_g"], d_h, d_x1)
    return loss, grad_x, g


HBM_SPEC = pl.BlockSpec(memory_space=pltpu.HBM)


def _place():
    x, y, c = lax.axis_index("x"), lax.axis_index("y"), lax.axis_index("c")
    other_chips = [(1 - x, y), (x, 1 - y), (1 - x, 1 - y)]
    return x, y, c, other_chips


def _remote(src, dst, send_sem, recv_sem, dev):
    return pltpu.make_async_remote_copy(src_ref=src, dst_ref=dst, send_sem=send_sem, recv_sem=recv_sem,
                                        device_id=dev, device_id_type=MESH_ID)


class _Exchange:
    def __init__(self, operands, out_shape, aliases, scratch, start, finish):
        self.operands, self.out_shape, self.aliases, self.scratch = operands, out_shape, aliases, scratch
        self.start, self.finish = start, finish


def _run_exchanges(name, exs):
    n_in = [len(ex.operands) for ex in exs]
    n_out = [len(ex.out_shape) for ex in exs]
    n_scr = [len(ex.scratch) for ex in exs]

    def body(*refs):
        ins, outs, scr = refs[:sum(n_in)], refs[sum(n_in):sum(n_in) + sum(n_out)], refs[sum(n_in) + sum(n_out):]
        parts, oi, oo, os_ = [], 0, 0, 0
        for k in range(len(exs)):
            parts.append((ins[oi:oi + n_in[k]], outs[oo:oo + n_out[k]], scr[os_:os_ + n_scr[k]]))
            oi, oo, os_ = oi + n_in[k], oo + n_out[k], os_ + n_scr[k]
        for ex, part in zip(exs, parts):
            ex.start(*part)
        for ex, part in zip(exs, parts):
            ex.finish(*part)

    aliases, ops, shapes, scratch, oi, oo = {}, [], [], [], 0, 0
    for k, ex in enumerate(exs):
        aliases.update({oi + a: oo + b for a, b in ex.aliases.items()})
        oi, oo = oi + n_in[k], oo + n_out[k]
        ops += list(ex.operands)
        shapes += [pltpu.HBM(s.shape, s.dtype) for s in ex.out_shape]
        scratch += list(ex.scratch)
    res = _pallas(
        body, name=name, in_specs=[HBM_SPEC] * len(ops), out_specs=(HBM_SPEC,) * len(shapes), out_shape=tuple(shapes),
        input_output_aliases=aliases, scratch_shapes=scratch,
    )(*ops)
    out, oo = [], 0
    for k in range(len(exs)):
        out.append(list(res[oo:oo + n_out[k]]))
        oo += n_out[k]
    return out


def _ex_all_gather(slabs, halved):
    n = len(slabs)

    def rows(a, cc):
        if not halved[a]:
            return slice(None)
        hr = slabs[a].shape[1] // 2
        return pl.ds(cc * hr, hr)

    def ici(bufs, scr, a, j, chip, c, mine):
        px, py = chip
        x, y, _, _ = _place()
        qs = 2 * x + y if mine else 2 * px + py
        piece = bufs[a].at[qs, rows(a, c)]
        return _remote(piece, piece, scr[0].at[3 * a + j], scr[1].at[3 * a + j], (px, py, c))

    def d2d(bufs, scr, a, j, chip, cc):
        px, py = chip
        x, y, c, _ = _place()
        piece = bufs[a].at[2 * px + py, rows(a, cc)]
        return _remote(piece, piece, scr[2].at[3 * a + j], scr[3].at[3 * a + j], (x, y, 1 - c))

    def start(ins, outs, scr):
        _, _, c, chips = _place()
        for j, chip in enumerate(chips):
            for a in range(n):
                ici(outs, scr, a, j, chip, c, True).start()

    def finish(ins, outs, scr):
        _, _, c, chips = _place()
        for j, chip in enumerate(chips):
            for a in range(n):
                ici(outs, scr, a, j, chip, c, False).wait_recv()
                if halved[a]:
                    d2d(outs, scr, a, j, chip, c).start()
        for j, chip in enumerate(chips):
            for a in range(n):
                if halved[a]:
                    d2d(outs, scr, a, j, chip, 1 - c).wait_recv()
        for j, chip in enumerate(chips):
            for a in range(n):
                ici(outs, scr, a, j, chip, c, True).wait_send()
                if halved[a]:
                    d2d(outs, scr, a, j, chip, c).wait_send()

    return _Exchange(list(slabs), [jax.ShapeDtypeStruct(s.shape, s.dtype) for s in slabs],
                     {a: a for a in range(n)}, [pltpu.SemaphoreType.DMA((3 * n,))] * 4, start, finish)


def _ex_to_sibling(grads):
    n = len(grads)

    def copy(ins, outs, scr, a):
        x, y, c, _ = _place()
        hr = grads[a].shape[1] // 2
        return _remote(ins[a].at[:, pl.ds((1 - c) * hr, hr), :], outs[a], scr[0].at[a], scr[1].at[a], (x, y, 1 - c))

    def start(ins, outs, scr):
        for a in range(n):
            copy(ins, outs, scr, a).start()

    def finish(ins, outs, scr):
        for a in range(n):
            copy(ins, outs, scr, a).wait()

    out_shape = [jax.ShapeDtypeStruct((g.shape[0], g.shape[1] // 2, g.shape[2]), g.dtype) for g in grads]
    return _Exchange(list(grads), out_shape, {}, [pltpu.SemaphoreType.DMA((n,))] * 2, start, finish)


def _ex_to_owner(parts):
    n = len(parts)

    def copy(ins, outs, scr, a, j, chip):
        _, _, c, _ = _place()
        px, py = chip
        return _remote(ins[a].at[2 * px + py], outs[a].at[j], scr[0].at[3 * a + j], scr[1].at[3 * a + j],
                       (px, py, c))

    def start(ins, outs, scr):
        for j, chip in enumerate(_place()[3]):
            for a in range(n):
                copy(ins, outs, scr, a, j, chip).start()

    def finish(ins, outs, scr):
        for j, chip in enumerate(_place()[3]):
            for a in range(n):
                copy(ins, outs, scr, a, j, chip).wait()

    out_shape = [jax.ShapeDtypeStruct((3,) + p.shape[1:], p.dtype) for p in parts]
    return _Exchange(list(parts), out_shape, {}, [pltpu.SemaphoreType.DMA((3 * n,))] * 2, start, finish)


def _ex_share_halves(bufs):
    n = len(bufs)

    def copy(outs, scr, a, cc):
        x, y, c, _ = _place()
        hr = bufs[a].shape[0] // 2
        piece = outs[a].at[pl.ds(cc * hr, hr), :]
        return _remote(piece, piece, scr[0].at[a], scr[1].at[a], (x, y, 1 - c))

    def start(ins, outs, scr):
        c = _place()[2]
        for a in range(n):
            copy(outs, scr, a, c).start()

    def finish(ins, outs, scr):
        c = _place()[2]
        for a in range(n):
            copy(outs, scr, a, c).wait_send()
            copy(outs, scr, a, 1 - c).wait_recv()

    return _Exchange(list(bufs), [jax.ShapeDtypeStruct(b.shape, b.dtype) for b in bufs], {a: a for a in range(n)},
                     [pltpu.SemaphoreType.DMA((n,))] * 2, start, finish)


def _ex_gather_small(arrs):
    n = len(arrs)

    def peer_of(m):
        x, y, c, _ = _place()
        return (1 - x if m & 4 else x, 1 - y if m & 2 else y, 1 - c if m & 1 else c)

    def start(ins, outs, scr):
        x, y, c, _ = _place()
        for m in range(1, N_DEV):
            for a in range(n):
                k = (N_DEV - 1) * a + m - 1
                _remote(ins[a], outs[a].at[4 * x + 2 * y + c], scr[0].at[k], scr[1].at[k], peer_of(m)).start()

    def finish(ins, outs, scr):
        for m in range(1, N_DEV):
            px, py, pc = peer_of(m)
            for a in range(n):
                k = (N_DEV - 1) * a + m - 1
                slot = outs[a].at[4 * px + 2 * py + pc]
                cp = _remote(ins[a], slot, scr[0].at[k], scr[1].at[k], (px, py, pc))
                cp.wait_send()
                cp.wait_recv()

    slots = [jnp.zeros((N_DEV,) + a.shape, a.dtype) for a in arrs]
    out_shape = [jax.ShapeDtypeStruct(s.shape, s.dtype) for s in slots]
    return _Exchange(list(arrs) + slots, out_shape, {n + a: a for a in range(n)},
                     [pltpu.SemaphoreType.DMA(((N_DEV - 1) * n,))] * 2, start, finish)


def _div_tile(n, want):
    best = None
    for t in range(8, min(n, want) + 1, 8):
        if n % t == 0:
            best = t
    assert best is not None, n
    return best


def _cast_into_slab(name, w, place, dtype):
    r, cc = w.shape
    tr = r if r * cc <= 128 * 1024 else _div_tile(r, 256)

    def body(s_ref, w_ref, o_ref):
        o_ref[...] = w_ref[...].astype(o_ref.dtype)

    return _pallas(
        body, name=name,
        grid_spec=pltpu.PrefetchScalarGridSpec(
            num_scalar_prefetch=1, grid=(r // tr,),
            in_specs=[pl.BlockSpec((tr, cc), lambda i, s: (i, 0))],
            out_specs=pl.BlockSpec((None, tr, cc), lambda i, s: (s[0], i, 0))),
        out_shape=jax.ShapeDtypeStruct((N_CHIPS, r, cc), dtype), compiler_params=_cp("parallel"),
    )(place, w)


def _add_half(name, g, rcv, place):
    nq, r, cc = g.shape
    hr = r // 2

    def body(s_ref, g_ref, r_ref, o_ref):
        o_ref[...] = (g_ref[...] + r_ref[...]).astype(o_ref.dtype)

    spec = pl.BlockSpec((None, hr, cc), lambda i, s: (i, 0, 0))
    return _pallas(
        body, name=name,
        grid_spec=pltpu.PrefetchScalarGridSpec(
            num_scalar_prefetch=1, grid=(nq,),
            in_specs=[pl.BlockSpec((None, hr, cc), lambda i, s: (i, s[1], 0)), spec], out_specs=spec),
        out_shape=jax.ShapeDtypeStruct((nq, hr, cc), BF16), compiler_params=_cp("parallel"),
    )(place, g, rcv)


def _sum_owner(name, part, rcv, place):
    _, hr, cc = part.shape
    tr = _div_tile(hr, 128)
    nb = hr // tr

    def body(s_ref, p_ref, r_ref, o_ref):
        o_ref[...] = ((p_ref[...].astype(F32) + r_ref[0].astype(F32)) + r_ref[1].astype(F32)) + r_ref[2].astype(F32)

    return _pallas(
        body, name=name,
        grid_spec=pltpu.PrefetchScalarGridSpec(
            num_scalar_prefetch=1, grid=(nb,),
            in_specs=[pl.BlockSpec((None, tr, cc), lambda i, s: (s[0], i, 0)),
                      pl.BlockSpec((3, tr, cc), lambda i, s: (0, i, 0))],
            out_specs=pl.BlockSpec((tr, cc), lambda i, s: (s[1] * nb + i, 0))),
        out_shape=jax.ShapeDtypeStruct((2 * hr, cc), F32), compiler_params=_cp("parallel"),
    )(place, part, rcv)


def _sum_small(gathered, local, place):
    n = len(gathered)

    def body(s_ref, *refs):
        g_refs, l_refs, o_refs = refs[:n], refs[n:2 * n], refs[2 * n:]
        me = s_ref[2]
        for g_ref, l_ref, o_ref in zip(g_refs, l_refs, o_refs):
            acc = None
            for d in range(N_DEV):
                term = jnp.where(me == d, l_ref[...], g_ref[d])
                acc = term if acc is None else acc + term
            o_ref[...] = acc

    def whole(shape):
        return pl.BlockSpec(shape, lambda i, s, nd=len(shape): (0,) * nd)

    return _pallas(
        body, name="sum_small",
        grid_spec=pltpu.PrefetchScalarGridSpec(
            num_scalar_prefetch=1, grid=(1,),
            in_specs=[whole(g.shape) for g in gathered] + [whole(a.shape) for a in local],
            out_specs=tuple(whole(a.shape) for a in local)),
        out_shape=tuple(jax.ShapeDtypeStruct(a.shape, a.dtype) for a in local), compiler_params=_cp("arbitrary"),
    )(place, *gathered, *local)


def _adamw(name, w, g, m, v):
    r, cc = w.shape
    tr = r if r * cc <= 128 * 1024 else _div_tile(r, 256)

    def body(w_ref, g_ref, m_ref, v_ref, d_ref, mo_ref, vo_ref):
        gv = g_ref[...]
        mn = ADAM_B1 * m_ref[...] + (1.0 - ADAM_B1) * gv
        vn = ADAM_B2 * v_ref[...] + (1.0 - ADAM_B2) * (gv * gv)
        m_hat = mn / (1.0 - ADAM_B1 ** ADAM_STEP)
        v_hat = vn / (1.0 - ADAM_B2 ** ADAM_STEP)
        d_ref[...] = -ADAM_LR * (m_hat / (jnp.sqrt(v_hat) + ADAM_EPS) + ADAM_WD * w_ref[...])
        mo_ref[...] = mn
        vo_ref[...] = vn

    spec = pl.BlockSpec((tr, cc), lambda i: (i, 0))
    sd = jax.ShapeDtypeStruct((r, cc), F32)
    return _pallas(
        body, name=name, grid=(r // tr,), in_specs=[spec] * 4, out_specs=(spec,) * 3, out_shape=(sd,) * 3,
        compiler_params=_cp("parallel"),
    )(w, g, m, v)


_BIG = ("w_in", "w_up", "w_branch", "w_mem_kv", "w_out", "w_down")
_BIG_SHARD_SHAPE = {"w_in": (1024, 1664), "w_up": (1024, 1408), "w_branch": (1536, 256),
                    "w_mem_kv": (256, 1024), "w_out": (256, 1024), "w_down": (704, 1024)}
_SMALL_SHAPE = {"norm1_g": (1, D_MODEL), "ln_v_g": (1, GM_WIDTH), "ln_v_b": (1, GM_WIDTH),
                "w_spatial": (GM_GROUPS * GM_CHUNK, GM_CHUNK), "b_spatial": (GM_GROUPS, GM_CHUNK),
                "lb_logits": (2, HG_HEADS * HG_DIM), "hgrn_norm_g": (1, HG_DIM), "mem_norm_g": (1, D_MODEL),
                "norm2_g": (1, D_MODEL), "conv_w": (3, D_FF), "conv_b": (1, D_FF), "final_g": (1, D_MODEL)}
_SMALL_EARLY = tuple(n for n in _SMALL_SHAPE if n != "norm1_g")
_PARAM_ORDER = ("norm1_g", "w_in", "ln_v_g", "ln_v_b", "w_spatial", "b_spatial", "lb_logits", "hgrn_norm_g",
                "mem_norm_g", "w_mem_kv", "w_branch", "w_out", "norm2_g", "w_up", "conv_w", "conv_b", "w_down",
                "final_g")


def _adamw_small(ws, gs, ms, vs):
    n = len(ws)

    def body(*refs):
        w_refs, g_refs, m_refs, v_refs = refs[:n], refs[n:2 * n], refs[2 * n:3 * n], refs[3 * n:4 * n]
        d_refs, mo_refs, vo_refs = refs[4 * n:5 * n], refs[5 * n:6 * n], refs[6 * n:]
        for k in range(n):
            gv = g_refs[k][...]
            mn = ADAM_B1 * m_refs[k][...] + (1.0 - ADAM_B1) * gv
            vn = ADAM_B2 * v_refs[k][...] + (1.0 - ADAM_B2) * (gv * gv)
            m_hat = mn / (1.0 - ADAM_B1 ** ADAM_STEP)
            v_hat = vn / (1.0 - ADAM_B2 ** ADAM_STEP)
            d_refs[k][...] = -ADAM_LR * (m_hat / (jnp.sqrt(v_hat) + ADAM_EPS) + ADAM_WD * w_refs[k][...])
            mo_refs[k][...] = mn
            vo_refs[k][...] = vn

    specs = [pl.BlockSpec(a.shape, lambda i: (0, 0)) for a in ws]
    shapes = tuple(jax.ShapeDtypeStruct(a.shape, F32) for a in ws)
    res = _pallas(
        body, name="adamw_small", grid=(1,), in_specs=specs * 4, out_specs=tuple(specs * 3), out_shape=shapes * 3,
        compiler_params=_cp("arbitrary"),
    )(*ws, *gs, *ms, *vs)
    return res[:n], res[n:2 * n], res[2 * n:]


class _Comm:
    _ROW_SHARDED = ("w_mem_kv", "w_out", "w_down")

    def __init__(self, slabs, place):
        self.slabs, self.place = slabs, place
        self.full, self.raw, self.parts, self.bufs, self.done = {}, {}, {}, {}, {}
        ex, deliver = self._gather(["w_in"])
        deliver(_run_exchanges("all_gather_w_in", [ex])[0])

    def w(self, name):
        a = self.full[name]
        if name in self._ROW_SHARDED:
            return a.reshape(-1, a.shape[-1])
        if name == "conv_w":
            return jnp.transpose(a, (1, 0, 2)).reshape(3, 1, D_FF)
        return a

    def grad(self, name, arr):
        self.raw[name] = arr.reshape((N_CHIPS,) + _BIG_SHARD_SHAPE[name])
        if name == "w_in":
            ex, deliver = self._to_sibling(["w_in"])
            deliver(_run_exchanges("rs_sibling_w_in", [ex])[0])

    def small_grads(self, arrays):
        self.small_local = list(arrays)

    def carry(self, tag, call):
        plan = self._plan(tag)
        if not plan:
            return call(())
        out, carried = call([ex for ex, _ in plan])
        for (_, deliver), res in zip(plan, carried):
            deliver(res)
        return out

    def finish(self, last_small):
        ex, deliver = self._share(["w_out", "w_branch", "w_mem_kv", "w_in"])
        shared, small = _run_exchanges("share_and_gather_last", [ex, _ex_gather_small(last_small)])
        deliver(shared)
        return self.done, self.small_local + list(last_small), self.small_everyone + small

    def _plan(self, tag):
        if tag == "in_proj":
            return [self._gather(["w_branch", "w_out", "w_mem_kv", "w_down", "conv_w"])]
        if tag == "hgrn_fwd":
            return [self._gather(["w_up"])]
        if tag == "d_h2":
            return [self._to_sibling(["w_down", "w_up"])]
        if tag == "hgrn_bwd":
            return [self._to_owner(["w_down", "w_up"]), self._to_sibling(["w_out", "w_branch", "w_mem_kv"])]
        if tag == "g_w_in":
            def keep(res):
                self.small_everyone = res

            return [self._to_owner(["w_out", "w_branch", "w_mem_kv"]), self._share(["w_down", "w_up"]),
                    (_ex_gather_small(self.small_local), keep)]
        if tag == "d_h":
            return [self._to_owner(["w_in"])]
        return []

    def _gather(self, names):
        ex = _ex_all_gather([self.slabs[n] for n in names], [n != "conv_w" for n in names])
        return ex, lambda res: self.full.update(zip(names, res))

    def _to_sibling(self, names):
        def deliver(res):
            for n, r in zip(names, res):
                self.parts[n] = _add_half("rs_add_" + n, self.raw[n], r, self.place)

        return _ex_to_sibling([self.raw[n] for n in names]), deliver

    def _to_owner(self, names):
        def deliver(res):
            for n, r in zip(names, res):
                self.bufs[n] = _sum_owner("rs_sum_" + n, self.parts[n], r, self.place)

        return _ex_to_owner([self.parts[n] for n in names]), deliver

    def _share(self, names):
        return _ex_share_halves([self.bufs[n] for n in names]), lambda res: self.done.update(zip(names, res))


def kernel(x, mem, norm1_g, w_in, ln_v_g, ln_v_b, w_spatial, b_spatial, lb_logits, hgrn_norm_g, mem_norm_g, w_mem_kv, w_branch, w_out, norm2_g, w_up, conv_w, conv_b, w_down, final_g, loss_target, m_norm1_g, m_w_in, m_ln_v_g, m_ln_v_b, m_w_spatial, m_b_spatial, m_lb_logits, m_hgrn_norm_g, m_mem_norm_g, m_w_mem_kv, m_w_branch, m_w_out, m_norm2_g, m_w_up, m_conv_w, m_conv_b, m_w_down, m_final_g, v_norm1_g, v_w_in, v_ln_v_g, v_ln_v_b, v_w_spatial, v_b_spatial, v_lb_logits, v_hgrn_norm_g, v_mem_norm_g, v_w_mem_kv, v_w_branch, v_w_out, v_norm2_g, v_w_up, v_conv_w, v_conv_b, v_w_down, v_final_g):
    w = dict(norm1_g=norm1_g, w_in=w_in, ln_v_g=ln_v_g, ln_v_b=ln_v_b, w_spatial=w_spatial, b_spatial=b_spatial,
             lb_logits=lb_logits, hgrn_norm_g=hgrn_norm_g, mem_norm_g=mem_norm_g, w_mem_kv=w_mem_kv,
             w_branch=w_branch, w_out=w_out, norm2_g=norm2_g, w_up=w_up, conv_w=conv_w, conv_b=conv_b,
             w_down=w_down, final_g=final_g)
    mom = dict(norm1_g=m_norm1_g, w_in=m_w_in, ln_v_g=m_ln_v_g, ln_v_b=m_ln_v_b, w_spatial=m_w_spatial,
               b_spatial=m_b_spatial, lb_logits=m_lb_logits, hgrn_norm_g=m_hgrn_norm_g, mem_norm_g=m_mem_norm_g,
               w_mem_kv=m_w_mem_kv, w_branch=m_w_branch, w_out=m_w_out, norm2_g=m_norm2_g, w_up=m_w_up,
               conv_w=m_conv_w, conv_b=m_conv_b, w_down=m_w_down, final_g=m_final_g)
    var = dict(norm1_g=v_norm1_g, w_in=v_w_in, ln_v_g=v_ln_v_g, ln_v_b=v_ln_v_b, w_spatial=v_w_spatial,
               b_spatial=v_b_spatial, lb_logits=v_lb_logits, hgrn_norm_g=v_hgrn_norm_g, mem_norm_g=v_mem_norm_g,
               w_mem_kv=v_w_mem_kv, w_branch=v_w_branch, w_out=v_w_out, norm2_g=v_norm2_g, w_up=v_w_up,
               conv_w=v_conv_w, conv_b=v_conv_b, w_down=v_w_down, final_g=v_final_g)
    B, S, D = x.shape
    T = B * S
    ci = lax.axis_index("c")
    q = 2 * lax.axis_index("x") + lax.axis_index("y")
    place = jnp.stack([q, ci, 2 * q + ci]).astype(jnp.int32)

    slabs = {n: _cast_into_slab("slab_" + n, w[n].reshape(_BIG_SHARD_SHAPE[n]), place, BF16) for n in _BIG}
    slabs["conv_w"] = _cast_into_slab("slab_conv_w", conv_w[0], place, F32)
    comm = _Comm(slabs, place)
    p = dict(
        norm1_g=norm1_g, ln_v_g=ln_v_g, ln_v_b=ln_v_b, w_spatial=w_spatial[0],
        b_spatial=b_spatial.reshape(GM_GROUPS, GM_CHUNK, 1), lb_logits=lb_logits, hgrn_norm_g=hgrn_norm_g,
        mem_norm_g=mem_norm_g, norm2_g=norm2_g, conv_b=conv_b, final_g=final_g.reshape(1, D))

    loss, grad_x, g = _local_step(x.reshape(T, D), mem.reshape(B * MEM_LEN, D), loss_target.reshape(T, D), p, comm,
                                  B, S)

    shard_grads, local_small, everyone = comm.finish([g["norm1_g"]])
    summed = _sum_small(everyone, local_small, place)
    small_names = list(_SMALL_EARLY) + ["norm1_g"]
    total = dict(zip(_SMALL_EARLY, summed))
    loss_total, total["norm1_g"] = summed[len(_SMALL_EARLY)][0, 0], summed[-1]

    grads, delta, new_m, new_v = {}, {}, {}, {}
    for n in _BIG:
        shp = _BIG_SHARD_SHAPE[n]
        grads[n] = shard_grads[n]
        delta[n], new_m[n], new_v[n] = _adamw("adamw_" + n, w[n].reshape(shp), shard_grads[n],
                                              mom[n].reshape(shp), var[n].reshape(shp))
    cw_shard = D_FF // N_CHIPS
    total["conv_w"] = lax.dynamic_slice(total["conv_w"], (0, q * cw_shard), (3, cw_shard))

    def flat2d(d, n):
        return d[n].reshape(total[n].shape)

    upd = _adamw_small([flat2d(w, n) for n in small_names], [total[n] for n in small_names],
                       [flat2d(mom, n) for n in small_names], [flat2d(var, n) for n in small_names])
    for k, n in enumerate(small_names):
        grads[n], delta[n], new_m[n], new_v[n] = total[n], upd[0][k], upd[1][k], upd[2][k]

    def shaped(d):
        return [d[n].reshape(w[n].shape) for n in _PARAM_ORDER]

    return (loss_total, grad_x.reshape(B, S, D), *shaped(grads), *shaped(delta), *shaped(new_m), *shaped(new_v))
```

```python
import functools
import math

import jax
import jax.numpy as jnp
from jax import lax
from jax.experimental import pallas as pl
from jax.experimental.pallas import tpu as pltpu

F32 = jnp.float32
BF16 = jnp.bfloat16
EPS = 1e-6

D_MODEL = 1024
MEM_LEN = 256
GM_WIDTH = 512
GM_CHUNK = 128
GM_GROUPS = 4
HG_HEADS = 4
HG_DIM = 128
HG_CHUNK = 64
XA_HEADS = 4
XA_DIM = 128
BR_WIDTH = 512
D_FF = 2816
IN_WIDTH = 6656
N_CHIPS = 4
N_DEV = 8

ADAM_LR = 0.001
ADAM_B1 = 0.9
ADAM_B2 = 0.999
ADAM_EPS = 1e-08
ADAM_WD = 0.01
ADAM_STEP = 10

COL_ZU, COL_ZV, COL_HQ, COL_HF, COL_HI, COL_HG, COL_XQ = 0, 1, 2, 3, 4, 5, 6
COL_GATE0 = 3584

VMEM_LIMIT_BYTES = 48 * 1024 * 1024
MESH_ID = pl.DeviceIdType.MESH


def _cp(*sem):
    return pltpu.CompilerParams(dimension_semantics=sem, vmem_limit_bytes=VMEM_LIMIT_BYTES)


def _pallas(body, *, out_shape, **kw):
    def pin(s):
        return pltpu.HBM(s.shape, s.dtype) if isinstance(s, jax.ShapeDtypeStruct) else s

    out_shape = tuple(pin(s) for s in out_shape) if isinstance(out_shape, (tuple, list)) else pin(out_shape)
    call = pl.pallas_call(body, out_shape=out_shape, **kw)

    def run(*operands):
        return call(*[pltpu.with_memory_space_constraint(o, pltpu.HBM) if jnp.issubdtype(o.dtype, jnp.floating)
                      else o for o in operands])

    return run


def _dot(a, b):
    return lax.dot_general(a.astype(BF16), b.astype(BF16), (((1,), (0,)), ((), ())), preferred_element_type=F32)


def _dot_nt(a, b):
    return lax.dot_general(a.astype(BF16), b.astype(BF16), (((1,), (1,)), ((), ())), preferred_element_type=F32)


def _dot_tn(a, b):
    return lax.dot_general(a.astype(BF16), b.astype(BF16), (((0,), (0,)), ((), ())), preferred_element_type=F32)


def _dot_01(mask01, x):
    hi = x.astype(BF16)
    r1 = x - hi.astype(F32)
    mid = r1.astype(BF16)
    lo = (r1 - mid.astype(F32)).astype(BF16)
    m = mask01.astype(BF16)
    dn = (((1,), (0,)), ((), ()))
    return (lax.dot_general(m, hi, dn, preferred_element_type=F32)
            + lax.dot_general(m, mid, dn, preferred_element_type=F32)
            + lax.dot_general(m, lo, dn, preferred_element_type=F32))


def _sigmoid(z):
    return 1.0 / (1.0 + jnp.exp(-z))


_GELU_C = math.sqrt(2.0 / math.pi)


def _gelu_and_grad(z):
    inner = _GELU_C * (z + 0.044715 * z * z * z)
    t = jnp.tanh(inner)
    val = 0.5 * z * (1.0 + t)
    grad = 0.5 * (1.0 + t) + 0.5 * z * (1.0 - t * t) * _GELU_C * (1.0 + 3.0 * 0.044715 * z * z)
    return val, grad


def _row_tile(n, want=512):
    t = min(want, n)
    assert n % t == 0
    return t


def _pcall(body, operands, *, name, grid, in_specs, out_specs, out_shape, scratch_shapes=(), semantics, riders=()):
    single = not isinstance(out_shape, (tuple, list))
    out_specs = (out_specs,) if single else tuple(out_specs)
    out_shape = (out_shape,) if single else tuple(out_shape)
    if not riders:
        res = _pallas(body, name=name, grid=grid, in_specs=list(in_specs), out_specs=out_specs,
                             out_shape=out_shape, scratch_shapes=list(scratch_shapes),
                             compiler_params=_cp(*semantics))(*operands)
        return (res[0] if single else res), []
    n_in, n_out, n_scr = len(in_specs), len(out_shape), len(scratch_shapes)
    ex_in = [len(ex.operands) for ex in riders]
    ex_out = [len(ex.out_shape) for ex in riders]
    ex_scr = [len(ex.scratch) for ex in riders]
    tot_in, tot_out = n_in + sum(ex_in), n_out + sum(ex_out)

    def wrapped(*refs):
        ins, outs, scr = refs[:tot_in], refs[tot_in:tot_in + tot_out], refs[tot_in + tot_out:]
        ids = [pl.program_id(d) for d in range(len(grid))]
        first = functools.reduce(lambda p, t: p & t, [i == 0 for i in ids])
        last = functools.reduce(lambda p, t: p & t, [i == n - 1 for i, n in zip(ids, grid)])
        parts, oi, oo, os_ = [], n_in, n_out, n_scr
        for k in range(len(riders)):
            parts.append((ins[oi:oi + ex_in[k]], outs[oo:oo + ex_out[k]], scr[os_:os_ + ex_scr[k]]))
            oi, oo, os_ = oi + ex_in[k], oo + ex_out[k], os_ + ex_scr[k]

        @pl.when(first)
        def _():
            for ex, part in zip(riders, parts):
                ex.start(*part)

        body(*ins[:n_in], *outs[:n_out], *scr[:n_scr])

        @pl.when(last)
        def _():
            for ex, part in zip(riders, parts):
                ex.finish(*part)

    aliases, oi, oo = {}, n_in, n_out
    all_ops, all_shapes, all_scr = list(operands), list(out_shape), list(scratch_shapes)
    for k, ex in enumerate(riders):
        aliases.update({oi + a: oo + b for a, b in ex.aliases.items()})
        oi, oo = oi + ex_in[k], oo + ex_out[k]
        all_ops += list(ex.operands)
        all_shapes += [pltpu.HBM(s.shape, s.dtype) for s in ex.out_shape]
        all_scr += list(ex.scratch)
    res = _pallas(
        wrapped, name=name, grid=grid, in_specs=list(in_specs) + [HBM_SPEC] * sum(ex_in),
        out_specs=out_specs + (HBM_SPEC,) * sum(ex_out), out_shape=tuple(all_shapes), scratch_shapes=all_scr,
        input_output_aliases=aliases, compiler_params=_cp(*(["arbitrary"] * len(grid))))(*all_ops)
    own = res[0] if single else tuple(res[:n_out])
    carried, oo = [], n_out
    for k in range(len(riders)):
        carried.append(list(res[oo:oo + ex_out[k]]))
        oo += ex_out[k]
    return own, carried


def _carried(out, carried, riders):
    return (out, carried) if riders else out


def _matmul(name, operands, *, grid, in_specs, o_spec, out_shape, out_dtype, dims, has_res=False, riders=()):
    nk = grid[2]
    assert nk == 1 or (out_dtype == F32 and not has_res)

    def body(*refs):
        if has_res:
            a_ref, b_ref, r_ref, o_ref = refs
        else:
            a_ref, b_ref, o_ref = refs
            r_ref = None
        part = lax.dot_general(a_ref[...].astype(BF16), b_ref[...].astype(BF16), (dims, ((), ())),
                               preferred_element_type=F32)
        if nk == 1:
            if r_ref is not None:
                part = part + r_ref[...]
            o_ref[...] = part.astype(o_ref.dtype)
        else:
            k = pl.program_id(2)

            @pl.when(k == 0)
            def _():
                o_ref[...] = part

            @pl.when(k > 0)
            def _():
                o_ref[...] += part

    out, carried = _pcall(body, operands, name=name, grid=grid, in_specs=in_specs, out_specs=o_spec,
                          out_shape=jax.ShapeDtypeStruct(out_shape, out_dtype),
                          semantics=("parallel", "parallel", "arbitrary"), riders=riders)
    return (out, carried) if riders else out


NN = ((1,), (0,))
NT = ((1,), (1,))
TN = ((0,), (0,))
_TN_TOKENS = 4096


def _mm_cs(name, a, w, out_dtype, riders=()):
    M, K = a.shape
    nq, _, wd = w.shape
    tm = _row_tile(M)
    return _matmul(name, (a, w), grid=(nq, M // tm, 1),
                   in_specs=[pl.BlockSpec((tm, K), lambda j, i, k: (i, 0)),
                             pl.BlockSpec((None, K, wd), lambda j, i, k: (j, 0, 0))],
                   o_spec=pl.BlockSpec((tm, wd), lambda j, i, k: (i, j)),
                   out_shape=(M, nq * wd), out_dtype=out_dtype, dims=NN, riders=riders)


def _mm_rs(name, a, w, out_dtype, res=None, tn=512):
    M, K = a.shape
    N = w.shape[1]
    tm = _row_tile(M)
    tn = min(tn, N)
    ops = (a, w) if res is None else (a, w, res)
    in_specs = [pl.BlockSpec((tm, K), lambda i, j, k: (i, 0)),
                pl.BlockSpec((K, tn), lambda i, j, k: (0, j))]
    if res is not None:
        in_specs.append(pl.BlockSpec((tm, tn), lambda i, j, k: (i, j)))
    return _matmul(name, ops, grid=(M // tm, N // tn, 1), in_specs=in_specs,
                   o_spec=pl.BlockSpec((tm, tn), lambda i, j, k: (i, j)),
                   out_shape=(M, N), out_dtype=out_dtype, dims=NN, has_res=res is not None)


def _mm_nt_rs(name, g, w, out_dtype, to, riders=()):
    M, N = g.shape
    K = w.shape[0]
    tm = _row_tile(M)
    return _matmul(name, (g, w), grid=(M // tm, K // to, 1),
                   in_specs=[pl.BlockSpec((tm, N), lambda i, j, k: (i, 0)),
                             pl.BlockSpec((to, N), lambda i, j, k: (j, 0))],
                   o_spec=pl.BlockSpec((tm, to), lambda i, j, k: (i, j)),
                   out_shape=(M, K), out_dtype=out_dtype, dims=NT, riders=riders)


def _mm_nt_cs(name, g, w, out_dtype, riders=(), stacked=False):
    M = g.shape[-2]
    nq, K, wd = w.shape
    tm = _row_tile(M, 256)

    def body(g_ref, w_ref, o_ref):
        acc = None
        for q in range(nq):
            gq = g_ref[q // 2, :, (q % 2) * wd:(q % 2 + 1) * wd] if stacked else g_ref[:, q * wd:(q + 1) * wd]
            part = _dot_nt(gq, w_ref[q])
            acc = part if acc is None else acc + part
        o_ref[...] = acc.astype(o_ref.dtype)

    g_spec = (pl.BlockSpec((2, tm, 2 * wd), lambda i: (0, i, 0)) if stacked
              else pl.BlockSpec((tm, nq * wd), lambda i: (i, 0)))
    out, carried = _pcall(
        body, (g, w), name=name, grid=(M // tm,),
        in_specs=[g_spec, pl.BlockSpec((nq, K, wd), lambda i: (0, 0, 0))],
        out_specs=pl.BlockSpec((tm, K), lambda i: (i, 0)),
        out_shape=jax.ShapeDtypeStruct((M, K), out_dtype), semantics=("parallel",), riders=riders)
    return (out, carried) if riders else out


def _mm_tn_rs(name, a, g, to, tn=512):
    T, M = a.shape
    N = g.shape[1]
    tt = _row_tile(T, _TN_TOKENS)
    tn = min(tn, N)
    return _matmul(name, (a, g), grid=(M // to, N // tn, T // tt),
                   in_specs=[pl.BlockSpec((tt, to), lambda i, j, k: (k, i)),
                             pl.BlockSpec((tt, tn), lambda i, j, k: (k, j))],
                   o_spec=pl.BlockSpec((to, tn), lambda i, j, k: (i, j)),
                   out_shape=(M, N), out_dtype=F32, dims=TN)


def _mm_tn_cs(name, a, g, nq, to, riders=(), stacked=False, m_blocks=None):
    T, M = a.shape
    first, count = m_blocks if m_blocks is not None else (0, M // to)
    M = count * to
    wd = g.shape[-1] * (2 if stacked else 1) // nq
    tt = _row_tile(T, _TN_TOKENS)
    g_spec = (pl.BlockSpec((None, tt, wd), lambda i, j, k: (j // 2, k, j % 2)) if stacked
              else pl.BlockSpec((tt, wd), lambda i, j, k: (k, j)))
    return _matmul(name, (a, g), grid=(count, nq, T // tt),
                   in_specs=[pl.BlockSpec((tt, to), lambda i, j, k: (k, first + i)), g_spec],
                   o_spec=pl.BlockSpec((None, to, wd), lambda i, j, k: (j, i, 0)),
                   out_shape=(nq, M, wd), out_dtype=F32, dims=TN, riders=riders)


def _rms_fwd(name, x, g):
    T, D = x.shape
    tm = _row_tile(T)

    def body(x_ref, g_ref, o_ref):
        xv = x_ref[...]
        r = lax.rsqrt(jnp.mean(xv * xv, axis=-1, keepdims=True) + EPS)
        o_ref[...] = (xv * r * g_ref[...]).astype(o_ref.dtype)

    return _pallas(
        body, name=name, grid=(T // tm,),
        in_specs=[pl.BlockSpec((tm, D), lambda i: (i, 0)), pl.BlockSpec((1, D), lambda i: (0, 0))],
        out_specs=pl.BlockSpec((tm, D), lambda i: (i, 0)),
        out_shape=jax.ShapeDtypeStruct((T, D), BF16), compiler_params=_cp("parallel"),
    )(x, g)


def _rms_bwd(name, x, g, dh, dres, riders=()):
    T, D = x.shape
    tm = _row_tile(T)
    has_res = dres is not None

    def body(*refs):
        if has_res:
            x_ref, g_ref, dh_ref, dr_ref, dx_ref, dg_ref = refs
        else:
            x_ref, g_ref, dh_ref, dx_ref, dg_ref = refs

        @pl.when(pl.program_id(0) == 0)
        def _():
            dg_ref[...] = jnp.zeros_like(dg_ref)

        xv = x_ref[...]
        r = lax.rsqrt(jnp.mean(xv * xv, axis=-1, keepdims=True) + EPS)
        n = xv * r
        dhv = dh_ref[...]
        dg_ref[...] += jnp.sum(dhv * n, axis=0, keepdims=True)
        dn = dhv * g_ref[...]
        dx = r * (dn - n * jnp.mean(dn * n, axis=-1, keepdims=True))
        if has_res:
            dx = dx + dr_ref[...]
        dx_ref[...] = dx

    row = pl.BlockSpec((tm, D), lambda i: (i, 0))
    vec = pl.BlockSpec((1, D), lambda i: (0, 0))
    ops = (x, g, dh, dres) if has_res else (x, g, dh)
    out, carried = _pcall(
        body, ops, name=name, grid=(T // tm,),
        in_specs=[row, vec, row] + ([row] if has_res else []),
        out_specs=(row, vec),
        out_shape=(jax.ShapeDtypeStruct((T, D), F32), jax.ShapeDtypeStruct((1, D), F32)),
        semantics=("arbitrary",), riders=riders)
    return (out, carried) if riders else out


def _loss_head(x2, tgt, g):
    T, D = x2.shape
    tm = _row_tile(T)

    def body(x_ref, t_ref, g_ref, dx_ref, dg_ref, loss_ref):
        @pl.when(pl.program_id(0) == 0)
        def _():
            dg_ref[...] = jnp.zeros_like(dg_ref)
            loss_ref[...] = jnp.zeros_like(loss_ref)

        xv = x_ref[...]
        gv = g_ref[...]
        r = lax.rsqrt(jnp.mean(xv * xv, axis=-1, keepdims=True) + EPS)
        n = xv * r
        diff = n * gv - t_ref[...]
        loss_ref[...] += 0.5 * jnp.sum(jnp.mean(diff * diff, axis=-1, keepdims=True))
        dy = diff * (1.0 / D)
        dg_ref[...] += jnp.sum(dy * n, axis=0, keepdims=True)
        dn = dy * gv
        dx_ref[...] = r * (dn - n * jnp.mean(dn * n, axis=-1, keepdims=True))

    row = pl.BlockSpec((tm, D), lambda i: (i, 0))
    vec = pl.BlockSpec((1, D), lambda i: (0, 0))
    return _pallas(
        body, name="loss_head", grid=(T // tm,),
        in_specs=[row, row, vec],
        out_specs=(row, vec, pl.BlockSpec((8, 128), lambda i: (0, 0))),
        out_shape=(jax.ShapeDtypeStruct((T, D), F32), jax.ShapeDtypeStruct((1, D), F32),
                   jax.ShapeDtypeStruct((8, 128), F32)),
        compiler_params=_cp("arbitrary"),
    )(x2, tgt, g)


def _gmlp_pieces(zu, zv, lng, lnb, ws_ref, bs_ref):
    u, du = _gelu_and_grad(zu)
    v, dv = _gelu_and_grad(zv)
    mu = jnp.mean(v, axis=-1, keepdims=True)
    vc = v - mu
    rstd = lax.rsqrt(jnp.mean(vc * vc, axis=-1, keepdims=True) + EPS)
    vhat = vc * rstd
    vn = vhat * lng + lnb
    row = lax.broadcasted_iota(jnp.int32, (GM_CHUNK, GM_CHUNK), 0)
    col = lax.broadcasted_iota(jnp.int32, (GM_CHUNK, GM_CHUNK), 1)
    tril = row >= col
    wms, mixed = [], []
    for g in range(GM_GROUPS):
        sl = slice(g * 128, (g + 1) * 128)
        wm = jnp.where(tril, ws_ref[g], 0.0)
        wms.append(wm)
        mixed.append(_dot(wm, vn[:, sl]) + bs_ref[g])
    return u, du, dv, rstd, vhat, vn, wms, mixed, tril


def _gmlp_fwd(proj, lng, lnb, ws, bs_col):
    T = proj.shape[0]
    n = T // GM_CHUNK

    def body(zu_ref, zv_ref, lng_ref, lnb_ref, ws_ref, bs_ref, o_ref):
        u, _, _, _, _, _, _, mixed, _ = _gmlp_pieces(zu_ref[...].astype(F32), zv_ref[...].astype(F32),
                                                     lng_ref[...], lnb_ref[...],
                                                     ws_ref, bs_ref)
        for g in range(GM_GROUPS):
            sl = slice(g * 128, (g + 1) * 128)
            o_ref[:, sl] = (u[:, sl] * mixed[g]).astype(o_ref.dtype)

    vec = pl.BlockSpec((1, GM_WIDTH), lambda i: (0, 0))
    return _pallas(
        body, name="gmlp_fwd", grid=(n,),
        in_specs=[pl.BlockSpec((GM_CHUNK, 512), lambda i: (i, COL_ZU)),
                  pl.BlockSpec((GM_CHUNK, 512), lambda i: (i, COL_ZV)),
                  vec, vec,
                  pl.BlockSpec((GM_GROUPS, 128, 128), lambda i: (0, 0, 0)),
                  pl.BlockSpec((GM_GROUPS, 128, 1), lambda i: (0, 0, 0))],
        out_specs=pl.BlockSpec((GM_CHUNK, 512), lambda i: (i, 0)),
        out_shape=jax.ShapeDtypeStruct((T, GM_WIDTH), BF16), compiler_params=_cp("parallel"),
    )(proj, proj, lng, lnb, ws, bs_col)


def _gmlp_bwd(proj, d_out, lng, lnb, ws, bs_col, riders=()):
    T = proj.shape[0]
    n = T // GM_CHUNK

    def body(zu_ref, zv_ref, do_ref, lng_ref, lnb_ref, ws_ref, bs_ref,
             dzu_ref, dzv_ref, dws_ref, dbs_ref, dlng_ref, dlnb_ref, dm_acc):
        i = pl.program_id(0)

        @pl.when(i == 0)
        def _():
            dws_ref[...] = jnp.zeros_like(dws_ref)
            dlng_ref[...] = jnp.zeros_like(dlng_ref)
            dlnb_ref[...] = jnp.zeros_like(dlnb_ref)
            dm_acc[...] = jnp.zeros_like(dm_acc)

        lng_v = lng_ref[...]
        u, du, dv, rstd, vhat, vn, wms, mixed, tril = _gmlp_pieces(zu_ref[...].astype(F32), zv_ref[...].astype(F32),
                                                                  lng_v, lnb_ref[...],
                                                                  ws_ref, bs_ref)
        do = do_ref[...]
        dvn_parts = []
        for g in range(GM_GROUPS):
            sl = slice(g * 128, (g + 1) * 128)
            dog = do[:, sl]
            dzu_ref[:, sl] = (dog * mixed[g] * du[:, sl]).astype(dzu_ref.dtype)
            dmix = dog * u[:, sl]
            dm_acc[:, sl] += dmix
            dws_ref[g] += jnp.where(tril, _dot_nt(dmix, vn[:, sl]), 0.0)
            dvn_parts.append(_dot_tn(wms[g], dmix))
        dvn = jnp.concatenate(dvn_parts, axis=1)
        dlng_ref[...] += jnp.sum(dvn * vhat, axis=0, keepdims=True)
        dlnb_ref[...] += jnp.sum(dvn, axis=0, keepdims=True)
        dvh = dvn * lng_v
        dvv = rstd * (dvh - jnp.mean(dvh, axis=-1, keepdims=True)
                      - vhat * jnp.mean(dvh * vhat, axis=-1, keepdims=True))
        dzv_ref[...] = (dvv * dv).astype(dzv_ref.dtype)

        @pl.when(i == n - 1)
        def _():
            for g in range(GM_GROUPS):
                dbs_ref[g] = jnp.sum(dm_acc[:, g * 128:(g + 1) * 128], axis=1, keepdims=True)

    vec = pl.BlockSpec((1, GM_WIDTH), lambda i: (0, 0))
    wsp = pl.BlockSpec((GM_GROUPS, 128, 128), lambda i: (0, 0, 0))
    bsp = pl.BlockSpec((GM_GROUPS, 128, 1), lambda i: (0, 0, 0))
    tile = pl.BlockSpec((GM_CHUNK, 512), lambda i: (i, 0))
    return _carried(*_pcall(
        body, (proj, proj, d_out, lng, lnb, ws, bs_col), name="gmlp_bwd", grid=(n,),
        in_specs=[pl.BlockSpec((GM_CHUNK, 512), lambda i: (i, COL_ZU)),
                  pl.BlockSpec((GM_CHUNK, 512), lambda i: (i, COL_ZV)),
                  pl.BlockSpec((None, GM_CHUNK, 512), lambda i: (0, i, 0)), vec, vec, wsp, bsp],
        out_specs=(tile, tile, wsp, bsp, vec, vec),
        out_shape=(jax.ShapeDtypeStruct((T, GM_WIDTH), BF16), jax.ShapeDtypeStruct((T, GM_WIDTH), BF16),
                   jax.ShapeDtypeStruct((GM_GROUPS, 128, 128), F32), jax.ShapeDtypeStruct((GM_GROUPS, 128, 1), F32),
                   jax.ShapeDtypeStruct((1, GM_WIDTH), F32), jax.ShapeDtypeStruct((1, GM_WIDTH), F32)),
        scratch_shapes=[pltpu.VMEM((GM_CHUNK, GM_WIDTH), F32)],
        semantics=("arbitrary",), riders=riders), riders)


def _hgrn_lower_bound(lbl):
    return 1.0 / (1.0 + jnp.exp(lbl[1:2, :] - lbl[0:1, :]))


def _hgrn_gates(hq, hf, lb):
    C = HG_CHUNK
    sg = _sigmoid(hf)
    fg = lb + (1.0 - lb) * sg
    sq = _sigmoid(hq)
    row = lax.broadcasted_iota(jnp.int32, (C, C), 0)
    col = lax.broadcasted_iota(jnp.int32, (C, C), 1)
    tril = row >= col
    logf = jnp.log(fg)
    a = _dot_01(tril, logf)
    a_last = jnp.sum(logf, axis=0, keepdims=True)
    first_half = lax.broadcasted_iota(jnp.int32, logf.shape, 0) < (C // 2)
    a_mid = jnp.sum(jnp.where(first_half, logf, 0.0), axis=0, keepdims=True)
    ea, ei, eki, ekl = jnp.exp(a), jnp.exp(a - a_mid), jnp.exp(a_mid - a), jnp.exp(a_last - a)
    k = 1.0 - fg
    q = hq * sq
    qi = (q * ei).astype(BF16).astype(F32)
    ki = (k * eki).astype(BF16).astype(F32)
    return dict(sg=sg, fg=fg, sq=sq, tril=tril, ea=ea, ei=ei, eki=eki, ekl=ekl, e_last=jnp.exp(a_last),
                qe=q * ea, qi=qi, ki=ki, kl=k * ekl)


def _heads(x):
    return [x[:, h * HG_DIM:(h + 1) * HG_DIM] for h in range(HG_HEADS)]


def _hgrn_fwd(proj, lbl, gh, B, S, riders=()):
    C = HG_CHUNK
    NC = S // C
    W = HG_HEADS * HG_DIM

    def body(q_ref, f_ref, i_ref, g_ref, lbl_ref, gh_ref, o_ref, bo_ref, st_ref, state):
        @pl.when(pl.program_id(0) == 0)
        def _():
            state[...] = jnp.zeros_like(state)

        lb = _hgrn_lower_bound(lbl_ref[...])
        ghv = gh_ref[...]
        for b in range(B):
            gt = _hgrn_gates(q_ref[b].astype(F32), f_ref[b].astype(F32), lb)
            v = _heads(i_ref[b])
            qe, qi, ki, kl, e_last = (_heads(gt[n]) for n in ("qe", "qi", "ki", "kl", "e_last"))
            outs, normed = [], []
            for h in range(HG_HEADS):
                p = jnp.where(gt["tril"], _dot_nt(qi[h], ki[h]), 0.0)
                st = state[b, h]
                st_ref[b, h] = st
                o = _dot_nt(qe[h], st) + _dot(p, v[h])
                state[b, h] = st * e_last[h] + _dot_tn(v[h], kl[h])
                outs.append(o)
                normed.append(o * lax.rsqrt(jnp.mean(o * o, axis=-1, keepdims=True) + EPS) * ghv)
            o_ref[b] = jnp.concatenate(outs, axis=1)
            hg = g_ref[b].astype(F32)
            bo_ref[b] = (jnp.concatenate(normed, axis=1) * (hg * _sigmoid(hg))).astype(bo_ref.dtype)

    def col(cb):
        return pl.BlockSpec((B, C, 512), lambda c: (0, c, cb))

    tile = pl.BlockSpec((B, C, W), lambda c: (0, c, 0))
    proj3 = proj.reshape(B, S, proj.shape[-1])
    out, carried = _pcall(
        body, (proj3, proj3, proj3, proj3, lbl, gh), name="hgrn_fwd", grid=(NC,),
        in_specs=[col(COL_HQ), col(COL_HF), col(COL_HI), col(COL_HG),
                  pl.BlockSpec((2, W), lambda c: (0, 0)), pl.BlockSpec((1, HG_DIM), lambda c: (0, 0))],
        out_specs=(tile, tile, pl.BlockSpec((B, None, HG_HEADS, 128, 128), lambda c: (0, c, 0, 0, 0))),
        out_shape=(jax.ShapeDtypeStruct((B, S, W), F32), jax.ShapeDtypeStruct((B, S, W), BF16),
                   jax.ShapeDtypeStruct((B, NC, HG_HEADS, 128, 128), F32)),
        scratch_shapes=[pltpu.VMEM((B, HG_HEADS, 128, 128), F32)],
        semantics=("arbitrary",), riders=riders)
    o_h, b_out, states = out
    out = (o_h, b_out.reshape(B * S, W), states)
    return (out, carried) if riders else out


def _hgrn_bwd(proj, o_saved, states, d_out, lbl, gh, B, S, riders=()):
    C = HG_CHUNK
    NC = S // C
    W = HG_HEADS * HG_DIM

    def body(q_ref, f_ref, i_ref, g_ref, o_ref, st_ref, do_ref, lbl_ref, gh_ref,
             dq_ref, df_ref, di_ref, dg_ref, dlbl_ref, dgh_ref, dstate, dlb_acc):
        c = pl.program_id(0)

        @pl.when(c == 0)
        def _():
            dstate[...] = jnp.zeros_like(dstate)
            dgh_ref[...] = jnp.zeros_like(dgh_ref)
            dlb_acc[...] = jnp.zeros_like(dlb_acc)

        lb = _hgrn_lower_bound(lbl_ref[...])
        ghv = gh_ref[...]
        row = lax.broadcasted_iota(jnp.int32, (C, C), 0)
        colm = lax.broadcasted_iota(jnp.int32, (C, C), 1)
        triu = colm >= row
        for b in range(B):
            hq, hg = q_ref[b].astype(F32), g_ref[b].astype(F32)
            gt = _hgrn_gates(hq, f_ref[b].astype(F32), lb)
            tril = gt["tril"]
            v = _heads(i_ref[b])
            qe, qi, ki, kl, e_last = (_heads(gt[n]) for n in ("qe", "qi", "ki", "kl", "e_last"))
            sgg = _sigmoid(hg)
            don_all = do_ref[b] * (hg * sgg)
            o, don = _heads(o_ref[b]), _heads(don_all)
            d_qe, d_qi, d_ki, d_kl, dv, n_all, dal = [], [], [], [], [], [], []
            for h in range(HG_HEADS):
                r = lax.rsqrt(jnp.mean(o[h] * o[h], axis=-1, keepdims=True) + EPS)
                n = o[h] * r
                n_all.append(n)
                dgh_ref[...] += jnp.sum(don[h] * n, axis=0, keepdims=True)
                dn = don[h] * ghv
                d_o = r * (dn - n * jnp.mean(dn * n, axis=-1, keepdims=True))
                st, dst = st_ref[b, h], dstate[b, h]
                p = jnp.where(tril, _dot_nt(qi[h], ki[h]), 0.0)
                dp = jnp.where(tril, _dot_nt(d_o, v[h]), 0.0)
                d_qe.append(_dot(d_o, st))
                d_qi.append(_dot(dp, ki[h]))
                d_ki.append(_dot_tn(dp, qi[h]))
                d_kl.append(_dot(v[h], dst))
                dv.append(_dot_tn(p, d_o) + _dot_nt(kl[h], dst))
                dstate[b, h] = dst * e_last[h] + _dot_tn(d_o, qe[h])
                dal.append(jnp.sum(dst * st, axis=0, keepdims=True) * e_last[h])
            d_qe, d_qi, d_ki, d_kl, n_all, dal = (jnp.concatenate(t, axis=1)
                                                  for t in (d_qe, d_qi, d_ki, d_kl, n_all, dal))
            dg_ref[b] = (do_ref[b] * n_all * jnp.tile(ghv, (1, HG_HEADS))
                         * (sgg * (1.0 + hg * (1.0 - sgg)))).astype(dg_ref.dtype)
            di_ref[b] = jnp.concatenate(dv, axis=1).astype(di_ref.dtype)
            d_a_last = dal + jnp.sum(d_kl * gt["kl"], axis=0, keepdims=True)
            dq = d_qe * gt["ea"] + d_qi * gt["ei"]
            dk = d_ki * gt["eki"] + d_kl * gt["ekl"]
            da = d_qe * gt["qe"] + d_qi * gt["qi"] - d_ki * gt["ki"] - d_kl * gt["kl"]
            dlogf = _dot_01(triu, da) + d_a_last
            sg, sq = gt["sg"], gt["sq"]
            dfg = dlogf / gt["fg"] - dk
            df_ref[b] = (dfg * (1.0 - lb) * sg * (1.0 - sg)).astype(df_ref.dtype)
            dlb_acc[...] += jnp.sum(dfg * (1.0 - sg), axis=0, keepdims=True)
            dq_ref[b] = (dq * (sq * (1.0 + hq * (1.0 - sq)))).astype(dq_ref.dtype)

        @pl.when(c == NC - 1)
        def _():
            dlb = dlb_acc[...]
            first = lax.broadcasted_iota(jnp.int32, (2, W), 0) == 0
            dlbl_ref[...] = jnp.where(first, dlb * lb * (1.0 - lb), -dlb * lb * (1.0 - lb))

    def col(cb):
        return pl.BlockSpec((B, C, 512), lambda c: (0, NC - 1 - c, cb))

    tile = pl.BlockSpec((B, C, W), lambda c: (0, NC - 1 - c, 0))
    proj3 = proj.reshape(B, S, proj.shape[-1])
    d3 = jax.ShapeDtypeStruct((B, S, W), BF16)
    out, carried = _pcall(
        body, (proj3, proj3, proj3, proj3, o_saved, states, d_out.reshape(3, B, S, W), lbl, gh), name="hgrn_bwd",
        grid=(NC,),
        in_specs=[col(COL_HQ), col(COL_HF), col(COL_HI), col(COL_HG), tile,
                  pl.BlockSpec((B, None, HG_HEADS, 128, 128), lambda c: (0, NC - 1 - c, 0, 0, 0)),
                  pl.BlockSpec((None, B, C, W), lambda c: (1, 0, NC - 1 - c, 0)),
                  pl.BlockSpec((2, W), lambda c: (0, 0)), pl.BlockSpec((1, HG_DIM), lambda c: (0, 0))],
        out_specs=(tile, tile, tile, tile,
                   pl.BlockSpec((2, W), lambda c: (0, 0)), pl.BlockSpec((1, HG_DIM), lambda c: (0, 0))),
        out_shape=(d3, d3, d3, d3, jax.ShapeDtypeStruct((2, W), F32), jax.ShapeDtypeStruct((1, HG_DIM), F32)),
        scratch_shapes=[pltpu.VMEM((B, HG_HEADS, 128, 128), F32), pltpu.VMEM((1, W), F32)],
        semantics=("arbitrary",), riders=riders)
    out = tuple(t.reshape(B * S, W) for t in out[:4]) + tuple(out[4:])
    return (out, carried) if riders else out


_XA_SCALE = XA_DIM ** -0.5


def _attn_probs(qh, kh):
    s = _dot_nt(qh, kh) * _XA_SCALE
    e = jnp.exp(s - jnp.max(s, axis=-1, keepdims=True))
    return e / jnp.sum(e, axis=-1, keepdims=True)


def _attn_fwd(proj, kv, B, S):
    T = B * S
    tq = _row_tile(S)
    nq = S // tq
    W = XA_HEADS * XA_DIM

    def body(q_ref, kv_ref, o_ref):
        for h in range(XA_HEADS):
            sl = slice(h * 128, (h + 1) * 128)
            p = _attn_probs(q_ref[:, sl], kv_ref[:, sl])
            o_ref[:, sl] = _dot(p, kv_ref[:, W + h * 128:W + (h + 1) * 128]).astype(o_ref.dtype)

    return _pallas(
        body, name="attn_fwd", grid=(B, nq),
        in_specs=[pl.BlockSpec((tq, 512), lambda b, i: (b * nq + i, COL_XQ)),
                  pl.BlockSpec((MEM_LEN, 2 * W), lambda b, i: (b, 0))],
        out_specs=pl.BlockSpec((tq, W), lambda b, i: (b * nq + i, 0)),
        out_shape=jax.ShapeDtypeStruct((T, W), BF16), compiler_params=_cp("parallel", "parallel"),
    )(proj, kv)


def _attn_bwd(proj, kv, d_out, B, S):
    T = B * S
    tq = _row_tile(S)
    nq = S // tq
    W = XA_HEADS * XA_DIM

    def body(q_ref, kv_ref, do_ref, dq_ref, dkv_ref):
        @pl.when(pl.program_id(1) == 0)
        def _():
            dkv_ref[...] = jnp.zeros_like(dkv_ref)

        for h in range(XA_HEADS):
            sl = slice(h * 128, (h + 1) * 128)
            slv = slice(W + h * 128, W + (h + 1) * 128)
            qh = q_ref[:, sl]
            kh = kv_ref[:, sl]
            p = _attn_probs(qh, kh)
            dc = do_ref[:, sl]
            dp = _dot_nt(dc, kv_ref[:, slv])
            ds = p * (dp - jnp.sum(dp * p, axis=-1, keepdims=True)) * _XA_SCALE
            dq_ref[:, sl] = _dot(ds, kh).astype(dq_ref.dtype)
            dkv_ref[:, sl] += _dot_tn(ds, qh)
            dkv_ref[:, slv] += _dot_tn(p, dc)

    kvspec = pl.BlockSpec((MEM_LEN, 2 * W), lambda b, i: (b, 0))
    tile = pl.BlockSpec((tq, W), lambda b, i: (b * nq + i, 0))
    return _pallas(
        body, name="attn_bwd", grid=(B, nq),
        in_specs=[pl.BlockSpec((tq, 512), lambda b, i: (b * nq + i, COL_XQ)), kvspec,
                  pl.BlockSpec((None, tq, W), lambda b, i: (2, b * nq + i, 0))],
        out_specs=(tile, kvspec),
        out_shape=(jax.ShapeDtypeStruct((T, W), BF16), jax.ShapeDtypeStruct((B * MEM_LEN, 2 * W), F32)),
        compiler_params=_cp("parallel", "arbitrary"),
    )(proj, kv, d_out)


_MERGE_TM = 256
_GATE_W = 512


def _gate_specs(tm):
    base = COL_GATE0 // _GATE_W
    return [pl.BlockSpec((tm, _GATE_W), functools.partial(lambda i, k: (i, base + k), k=k)) for k in range(6)]


def _merge_fwd(a_out, b_out, c_out, wb, proj, riders=()):
    T = a_out.shape[0]
    tm = _row_tile(T, _MERGE_TM)
    nq, _, wd = wb.shape
    per_half = _GATE_W // wd

    def body(a_ref, b_ref, c_ref, w_ref, *rest):
        gates, (m_ref, up_ref) = rest[:6], rest[6:]
        for hf in range(2):
            cols = slice(hf * _GATE_W, (hf + 1) * _GATE_W)
            acc = None
            for n, br in enumerate((a_ref, b_ref, c_ref)):
                x = br[...]
                up = jnp.concatenate([_dot(x, w_ref[per_half * hf + j, n * BR_WIDTH:(n + 1) * BR_WIDTH, :])
                                      for j in range(per_half)], axis=1)
                up_ref[n, :, cols] = up.astype(up_ref.dtype)
                term = _sigmoid(gates[2 * n + hf][...].astype(F32)) * up
                acc = term if acc is None else acc + term
            m_ref[:, cols] = acc.astype(m_ref.dtype)

    br_spec = pl.BlockSpec((tm, BR_WIDTH), lambda i: (i, 0))
    return _carried(*_pcall(
        body, (a_out, b_out, c_out, wb, *([proj] * 6)), name="merge_fwd", grid=(T // tm,),
        in_specs=[br_spec, br_spec, br_spec,
                  pl.BlockSpec((nq, 3 * BR_WIDTH, wd), lambda i: (0, 0, 0))] + _gate_specs(tm),
        out_specs=(pl.BlockSpec((tm, D_MODEL), lambda i: (i, 0)), pl.BlockSpec((3, tm, D_MODEL), lambda i: (0, i, 0))),
        out_shape=(jax.ShapeDtypeStruct((T, D_MODEL), BF16), jax.ShapeDtypeStruct((3, T, D_MODEL), BF16)),
        semantics=("parallel",), riders=riders), riders)


def _branch_bwd_act(d_ups, wb, riders=()):
    _, T, D = d_ups.shape
    nq, _, wd = wb.shape
    tm = _row_tile(T)

    def body(d_ref, w_ref, o_ref):
        acc = None
        for q in range(nq):
            part = _dot_nt(d_ref[:, q * wd:(q + 1) * wd], w_ref[q])
            acc = part if acc is None else acc + part
        o_ref[...] = acc

    return _carried(*_pcall(
        body, (d_ups, wb), name="d_branch", grid=(3, T // tm),
        in_specs=[pl.BlockSpec((None, tm, D), lambda n, i: (n, i, 0)),
                  pl.BlockSpec((nq, BR_WIDTH, wd), lambda n, i: (0, n, 0))],
        out_specs=pl.BlockSpec((None, tm, BR_WIDTH), lambda n, i: (n, i, 0)),
        out_shape=jax.ShapeDtypeStruct((3, T, BR_WIDTH), F32), semantics=("parallel", "parallel"),
        riders=riders), riders)


def _branch_bwd_weight(name, br, d_ups, n):
    T = br.shape[0]
    D = d_ups.shape[2]
    wd = D // N_CHIPS
    tt = _row_tile(T, _TN_TOKENS)

    def body(b_ref, d_ref, o_ref):
        k = pl.program_id(0)
        for q in range(N_CHIPS):
            part = _dot_tn(b_ref[...], d_ref[:, q * wd:(q + 1) * wd])

            @pl.when(k == 0)
            def _():
                o_ref[q] = part

            @pl.when(k > 0)
            def _():
                o_ref[q] += part

    return _pallas(
        body, name=name, grid=(T // tt,),
        in_specs=[pl.BlockSpec((tt, BR_WIDTH), lambda k: (k, 0)),
                  pl.BlockSpec((None, tt, D), lambda k: (n, k, 0))],
        out_specs=pl.BlockSpec((N_CHIPS, BR_WIDTH, wd), lambda k: (0, 0, 0)),
        out_shape=jax.ShapeDtypeStruct((N_CHIPS, BR_WIDTH, wd), F32), compiler_params=_cp("arbitrary"),
    )(br, d_ups)


def _merge_bwd(d_merged, ups, proj, riders=()):
    T = d_merged.shape[0]
    tm = _row_tile(T, _MERGE_TM)

    def body(dm_ref, up_ref, *rest):
        gates, (dup_ref, dg0_ref, dg1_ref, dg2_ref) = rest[:6], rest[6:]
        for hf in range(2):
            cols = slice(hf * _GATE_W, (hf + 1) * _GATE_W)
            dm = dm_ref[:, cols]
            for n, dgr in enumerate((dg0_ref, dg1_ref, dg2_ref)):
                gate = _sigmoid(gates[2 * n + hf][...].astype(F32))
                dup_ref[n, :, cols] = (dm * gate).astype(dup_ref.dtype)
                dgr[:, cols] = (dm * up_ref[n, :, cols].astype(F32) * gate * (1.0 - gate)).astype(dgr.dtype)

    tile = pl.BlockSpec((tm, D_MODEL), lambda i: (i, 0))
    tile3 = pl.BlockSpec((3, tm, D_MODEL), lambda i: (0, i, 0))
    return _carried(*_pcall(
        body, (d_merged, ups, *([proj] * 6)), name="merge_bwd", grid=(T // tm,),
        in_specs=[tile, tile3] + _gate_specs(tm),
        out_specs=(tile3, tile, tile, tile),
        out_shape=(jax.ShapeDtypeStruct((3, T, D_MODEL), BF16),) + (jax.ShapeDtypeStruct((T, D_MODEL), BF16),) * 3,
        semantics=("parallel",), riders=riders), riders)


_CONV_TF = D_FF // 2
_CONV_TS = 256
_HALO = 16


def _conv_fwd(ab, cw, cb, B, S):
    T = B * S
    ts = _row_tile(S, _CONV_TS)
    tf = _CONV_TF
    nb = D_FF // tf
    tps = S // ts
    hb = ts // _HALO

    def body(a_ref, p_ref, b_ref, w_ref, cb_ref, o_ref):
        start = (pl.program_id(0) % tps) == 0
        a = a_ref[...].astype(F32)
        prev = jnp.where(start, 0.0, p_ref[...].astype(F32))
        ext = jnp.concatenate([prev, a], axis=0)
        a1 = pltpu.roll(ext, 1, 0)[_HALO:, :]
        a2 = pltpu.roll(ext, 2, 0)[_HALO:, :]
        ac = cb_ref[...] + w_ref[0] * a2 + w_ref[1] * a1 + w_ref[2] * a
        o_ref[...] = (ac * _sigmoid(ac) * b_ref[...].astype(F32)).astype(o_ref.dtype)

    return _pallas(
        body, name="conv_fwd", grid=(T // ts, nb),
        in_specs=[pl.BlockSpec((ts, tf), lambda i, j: (i, j)),
                  pl.BlockSpec((_HALO, tf), lambda i, j: (jnp.maximum(i * hb - 1, 0), j)),
                  pl.BlockSpec((ts, tf), lambda i, j: (i, j + nb)),
                  pl.BlockSpec((3, 1, tf), lambda i, j: (0, 0, j)),
                  pl.BlockSpec((1, tf), lambda i, j: (0, j))],
        out_specs=pl.BlockSpec((ts, tf), lambda i, j: (i, j)),
        out_shape=jax.ShapeDtypeStruct((T, D_FF), BF16), compiler_params=_cp("parallel", "parallel"),
    )(ab, ab, ab, cw, cb)


def _conv_bwd(ab, d_ff, cw, cb, B, S, riders=()):
    T = B * S
    ts = _row_tile(S, _CONV_TS)
    tf = _CONV_TF
    nb = D_FF // tf
    tps = S // ts
    hb = ts // _HALO
    last_h = T // _HALO - 1
    n_ext = ts + _HALO

    def body(a_ref, ap_ref, an_ref, b_ref, bn_ref, d_ref, dn_ref, w_ref, cb_ref, dab_ref, dw_ref, dcb_ref):
        i = pl.program_id(1)

        @pl.when(i == 0)
        def _():
            dw_ref[...] = jnp.zeros_like(dw_ref)
            dcb_ref[...] = jnp.zeros_like(dcb_ref)

        start = (i % tps) == 0
        end = (i % tps) == tps - 1
        a = a_ref[...].astype(F32)
        ext = jnp.concatenate([jnp.where(start, 0.0, ap_ref[...].astype(F32)), a, an_ref[...].astype(F32)], axis=0)
        r1 = pltpu.roll(ext, 1, 0)[_HALO:, :]
        r2 = pltpu.roll(ext, 2, 0)[_HALO:, :]
        ac = cb_ref[...] + w_ref[0] * r2 + w_ref[1] * r1 + w_ref[2] * ext[_HALO:, :]
        sg = _sigmoid(ac)
        d_e = jnp.concatenate([d_ref[...].astype(F32), jnp.where(end, 0.0, dn_ref[...].astype(F32))], axis=0)
        b_e = jnp.concatenate([b_ref[...].astype(F32), bn_ref[...].astype(F32)], axis=0)
        dab_ref[1] = (d_e[:ts, :] * (ac * sg)[:ts, :]).astype(dab_ref.dtype)
        dac = d_e * b_e * sg * (1.0 + ac * (1.0 - sg))
        u1 = pltpu.roll(dac, n_ext - 1, 0)[:ts, :]
        u2 = pltpu.roll(dac, n_ext - 2, 0)[:ts, :]
        dac0 = dac[:ts, :]
        dab_ref[0] = (w_ref[2] * dac0 + w_ref[1] * u1 + w_ref[0] * u2).astype(dab_ref.dtype)
        dcb_ref[...] += jnp.sum(dac0, axis=0, keepdims=True)
        dw_ref[2] += jnp.sum(dac0 * a, axis=0, keepdims=True)
        dw_ref[1] += jnp.sum(dac0 * r1[:ts, :], axis=0, keepdims=True)
        dw_ref[0] += jnp.sum(dac0 * r2[:ts, :], axis=0, keepdims=True)

    def cur(off):
        return pl.BlockSpec((ts, tf), lambda j, i: (i, j + off))

    def nxt(off):
        return pl.BlockSpec((_HALO, tf), lambda j, i: (jnp.minimum((i + 1) * hb, last_h), j + off))

    return _carried(*_pcall(
        body, (ab, ab, ab, ab, ab, d_ff, d_ff, cw, cb), name="conv_bwd", grid=(nb, T // ts),
        in_specs=[cur(0), pl.BlockSpec((_HALO, tf), lambda j, i: (jnp.maximum(i * hb - 1, 0), j)), nxt(0),
                  cur(nb), nxt(nb), cur(0), nxt(0),
                  pl.BlockSpec((3, 1, tf), lambda j, i: (0, 0, j)), pl.BlockSpec((1, tf), lambda j, i: (0, j))],
        out_specs=(pl.BlockSpec((2, ts, tf), lambda j, i: (0, i, j)), pl.BlockSpec((3, 1, tf), lambda j, i: (0, 0, j)),
                   pl.BlockSpec((1, tf), lambda j, i: (0, j))),
        out_shape=(jax.ShapeDtypeStruct((2, T, D_FF), BF16),
                   jax.ShapeDtypeStruct((3, 1, D_FF), F32), jax.ShapeDtypeStruct((1, D_FF), F32)),
        semantics=("parallel", "arbitrary"), riders=riders), riders)


def _local_step(x, mem, tgt, p, comm, B, S):
    g = {}
    h = _rms_fwd("norm1", x, p["norm1_g"])
    proj = comm.carry("in_proj", lambda r: _mm_cs("in_proj", h, comm.w("w_in"), BF16, riders=r))
    a_out = _gmlp_fwd(proj, p["ln_v_g"], p["ln_v_b"], p["w_spatial"], p["b_spatial"])
    o_h, b_out, states = comm.carry(
        "hgrn_fwd", lambda r: _hgrn_fwd(proj, p["lb_logits"], p["hgrn_norm_g"], B, S, riders=r))
    memn = _rms_fwd("mem_norm", mem, p["mem_norm_g"])
    kv = _mm_rs("mem_kv", memn, comm.w("w_mem_kv"), F32)
    c_out = _attn_fwd(proj, kv, B, S)
    merged, ups = comm.carry(
        "merge_fwd", lambda r: _merge_fwd(a_out, b_out, c_out, comm.w("w_branch"), proj, riders=r))
    x1 = _mm_rs("out_proj", merged, comm.w("w_out"), F32, res=x)
    h2 = _rms_fwd("norm2", x1, p["norm2_g"])
    ab = comm.carry("up_proj", lambda r: _mm_cs("up_proj", h2, comm.w("w_up"), BF16, riders=r))
    conv_w = comm.w("conv_w")
    ff = _conv_fwd(ab, conv_w, p["conv_b"], B, S)
    x2 = _mm_rs("down_proj", ff, comm.w("w_down"), F32, res=x1)
    dx2, g["final_g"], loss = _loss_head(x2, tgt, p["final_g"])

    comm.grad("w_down", _mm_tn_rs("g_w_down", ff, dx2, to=D_FF // 2))
    d_ff = comm.carry("d_ff", lambda r: _mm_nt_rs("d_ff", dx2, comm.w("w_down"), BF16, to=D_FF // 2, riders=r))
    d_ab, g["conv_w"], g["conv_b"] = comm.carry(
        "conv_bwd", lambda r: _conv_bwd(ab, d_ff, conv_w, p["conv_b"], B, S, riders=r))
    comm.grad("w_up", _mm_tn_cs("g_w_up", h2, d_ab, N_CHIPS, to=512, stacked=True))
    d_h2 = comm.carry("d_h2", lambda r: _mm_nt_cs("d_h2", d_ab, comm.w("w_up"), F32, riders=r, stacked=True))
    d_x1, g["norm2_g"] = _rms_bwd("norm2_bwd", x1, p["norm2_g"], d_h2, dx2)
    comm.grad("w_out", _mm_tn_rs("g_w_out", merged, d_x1, to=512))
    d_merged = _mm_nt_rs("d_merged", d_x1, comm.w("w_out"), F32, to=512)
    d_ups, d_g0, d_g1, d_g2 = comm.carry("merge_bwd", lambda r: _merge_bwd(d_merged, ups, proj, riders=r))

    d_br = comm.carry("d_branch", lambda r: _branch_bwd_act(d_ups, comm.w("w_branch"), riders=r))
    comm.grad("w_branch", jnp.concatenate(
        [_branch_bwd_weight("g_w_branch%d" % n, br, d_ups, n) for n, br in enumerate((a_out, b_out, c_out))],
        axis=1))

    d_zu, d_zv, g["w_spatial"], g["b_spatial"], g["ln_v_g"], g["ln_v_b"] = comm.carry(
        "gmlp_bwd", lambda r: _gmlp_bwd(proj, d_br, p["ln_v_g"], p["ln_v_b"], p["w_spatial"], p["b_spatial"],
                                        riders=r))
    d_xq, d_kv = _attn_bwd(proj, kv, d_br, B, S)
    comm.grad("w_mem_kv", _mm_tn_rs("g_w_mem_kv", memn, d_kv, to=512))
    d_memn = _mm_nt_rs("d_memn", d_kv, comm.w("w_mem_kv"), F32, to=512)
    _, g["mem_norm_g"] = _rms_bwd("mem_norm_bwd", mem, p["mem_norm_g"], d_memn, None)
    d_hq, d_hf, d_hi, d_hg, g["lb_logits"], g["hgrn_norm_g"] = comm.carry(
        "hgrn_bwd", lambda r: _hgrn_bwd(proj, o_h, states, d_br, p["lb_logits"], p["hgrn_norm_g"], B, S, riders=r))
    d_proj = jnp.concatenate([d_zu, d_zv, d_hq, d_hf, d_hi, d_hg, d_xq, d_g0, d_g1, d_g2], axis=1)
    comm.small_grads([g[n].reshape(_SMALL_SHAPE[n]) for n in _SMALL_EARLY] + [loss])
    half_blocks = D_MODEL // 512 // 2
    for tag, name, first in (("g_w_in_top", "w_in_top", 0), ("g_w_in_bot", "w_in_bot", half_blocks)):
        comm.grad(name, comm.carry(tag, lambda r, tag=tag, first=first: _mm_tn_cs(
            tag, h, d_proj, N_CHIPS, to=512, riders=r, m_blocks=(first, half_blocks))))
    d_h = comm.carry("d_h", lambda r: _mm_nt_cs("d_h", d_proj, comm.w("w_in"), F32, riders=r))
    grad_x, g["norm1_g"] = comm.carry(
        "norm1_bwd", lambda r: _rms_bwd("norm1_bwd", x, p["norm1_g"], d_h, d_x1, riders=r))
    return loss, grad_x, g


HBM_SPEC = pl.BlockSpec(memory_space=pltpu.HBM)


def _place():
    x, y, c = lax.axis_index("x"), lax.axis_index("y"), lax.axis_index("c")
    other_chips = [(1 - x, y), (x, 1 - y), (1 - x, 1 - y)]
    return x, y, c, other_chips


def _remote(src, dst, send_sem, recv_sem, dev):
    return pltpu.make_async_remote_copy(src_ref=src, dst_ref=dst, send_sem=send_sem, recv_sem=recv_sem,
                                        device_id=dev, device_id_type=MESH_ID)


class _Exchange:
    def __init__(self, operands, out_shape, aliases, scratch, start, finish):
        self.operands, self.out_shape, self.aliases, self.scratch = operands, out_shape, aliases, scratch
        self.start, self.finish = start, finish


def _run_exchanges(name, exs):
    n_in = [len(ex.operands) for ex in exs]
    n_out = [len(ex.out_shape) for ex in exs]
    n_scr = [len(ex.scratch) for ex in exs]

    def body(*refs):
        ins, outs, scr = refs[:sum(n_in)], refs[sum(n_in):sum(n_in) + sum(n_out)], refs[sum(n_in) + sum(n_out):]
        parts, oi, oo, os_ = [], 0, 0, 0
        for k in range(len(exs)):
            parts.append((ins[oi:oi + n_in[k]], outs[oo:oo + n_out[k]], scr[os_:os_ + n_scr[k]]))
            oi, oo, os_ = oi + n_in[k], oo + n_out[k], os_ + n_scr[k]
        for ex, part in zip(exs, parts):
            ex.start(*part)
        for ex, part in zip(exs, parts):
            ex.finish(*part)

    aliases, ops, shapes, scratch, oi, oo = {}, [], [], [], 0, 0
    for k, ex in enumerate(exs):
        aliases.update({oi + a: oo + b for a, b in ex.aliases.items()})
        oi, oo = oi + n_in[k], oo + n_out[k]
        ops += list(ex.operands)
        shapes += [pltpu.HBM(s.shape, s.dtype) for s in ex.out_shape]
        scratch += list(ex.scratch)
    res = _pallas(
        body, name=name, in_specs=[HBM_SPEC] * len(ops), out_specs=(HBM_SPEC,) * len(shapes), out_shape=tuple(shapes),
        input_output_aliases=aliases, scratch_shapes=scratch,
    )(*ops)
    out, oo = [], 0
    for k in range(len(exs)):
        out.append(list(res[oo:oo + n_out[k]]))
        oo += n_out[k]
    return out


def _ex_all_gather(slabs, halved, part=(0, 1)):
    n = len(slabs)

    def rows(a, cc):
        if not halved[a]:
            return slice(None)
        pr = slabs[a].shape[1] // part[1]
        return pl.ds(part[0] * pr + cc * (pr // 2), pr // 2)

    def ici(bufs, scr, a, j, chip, c, mine):
        px, py = chip
        x, y, _, _ = _place()
        qs = 2 * x + y if mine else 2 * px + py
        piece = bufs[a].at[qs, rows(a, c)]
        return _remote(piece, piece, scr[0].at[3 * a + j], scr[1].at[3 * a + j], (px, py, c))

    def d2d(bufs, scr, a, j, chip, cc):
        px, py = chip
        x, y, c, _ = _place()
        piece = bufs[a].at[2 * px + py, rows(a, cc)]
        return _remote(piece, piece, scr[2].at[3 * a + j], scr[3].at[3 * a + j], (x, y, 1 - c))

    def start(ins, outs, scr):
        _, _, c, chips = _place()
        for j, chip in enumerate(chips):
            for a in range(n):
                ici(outs, scr, a, j, chip, c, True).start()

    def finish(ins, outs, scr):
        _, _, c, chips = _place()
        for j, chip in enumerate(chips):
            for a in range(n):
                ici(outs, scr, a, j, chip, c, False).wait_recv()
                if halved[a]:
                    d2d(outs, scr, a, j, chip, c).start()
        for j, chip in enumerate(chips):
            for a in range(n):
                if halved[a]:
                    d2d(outs, scr, a, j, chip, 1 - c).wait_recv()
        for j, chip in enumerate(chips):
            for a in range(n):
                ici(outs, scr, a, j, chip, c, True).wait_send()
                if halved[a]:
                    d2d(outs, scr, a, j, chip, c).wait_send()

    return _Exchange(list(slabs), [jax.ShapeDtypeStruct(s.shape, s.dtype) for s in slabs],
                     {a: a for a in range(n)}, [pltpu.SemaphoreType.DMA((3 * n,))] * 4, start, finish)


def _ex_to_sibling(grads):
    n = len(grads)

    def copy(ins, outs, scr, a):
        x, y, c, _ = _place()
        hr = grads[a].shape[1] // 2
        return _remote(ins[a].at[:, pl.ds((1 - c) * hr, hr), :], outs[a], scr[0].at[a], scr[1].at[a], (x, y, 1 - c))

    def start(ins, outs, scr):
        for a in range(n):
            copy(ins, outs, scr, a).start()

    def finish(ins, outs, scr):
        for a in range(n):
            copy(ins, outs, scr, a).wait()

    out_shape = [jax.ShapeDtypeStruct((g.shape[0], g.shape[1] // 2, g.shape[2]), g.dtype) for g in grads]
    return _Exchange(list(grads), out_shape, {}, [pltpu.SemaphoreType.DMA((n,))] * 2, start, finish)


def _ex_to_owner(parts, part=(0, 1), landing=None):
    n = len(parts)

    def copy(ins, outs, scr, a, j, chip):
        _, _, c, _ = _place()
        px, py = chip
        pr = parts[a].shape[1] // part[1]
        rows = pl.ds(part[0] * pr, pr)
        return _remote(ins[a].at[2 * px + py, rows], outs[a].at[j, rows], scr[0].at[3 * a + j],
                       scr[1].at[3 * a + j], (px, py, c))

    def start(ins, outs, scr):
        for j, chip in enumerate(_place()[3]):
            for a in range(n):
                copy(ins, outs, scr, a, j, chip).start()

    def finish(ins, outs, scr):
        for j, chip in enumerate(_place()[3]):
            for a in range(n):
                copy(ins, outs, scr, a, j, chip).wait()

    out_shape = [jax.ShapeDtypeStruct((3,) + p.shape[1:], p.dtype) for p in parts]
    operands, aliases = list(parts), {}
    if landing is not None:
        operands, aliases = operands + list(landing), {n + a: a for a in range(n)}
    return _Exchange(operands, out_shape, aliases, [pltpu.SemaphoreType.DMA((3 * n,))] * 2, start, finish)


def _ex_share_halves(bufs):
    n = len(bufs)

    def copy(outs, scr, a, cc):
        x, y, c, _ = _place()
        hr = bufs[a].shape[0] // 2
        piece = outs[a].at[pl.ds(cc * hr, hr), :]
        return _remote(piece, piece, scr[0].at[a], scr[1].at[a], (x, y, 1 - c))

    def start(ins, outs, scr):
        c = _place()[2]
        for a in range(n):
            copy(outs, scr, a, c).start()

    def finish(ins, outs, scr):
        c = _place()[2]
        for a in range(n):
            copy(outs, scr, a, c).wait_send()
            copy(outs, scr, a, 1 - c).wait_recv()

    return _Exchange(list(bufs), [jax.ShapeDtypeStruct(b.shape, b.dtype) for b in bufs], {a: a for a in range(n)},
                     [pltpu.SemaphoreType.DMA((n,))] * 2, start, finish)


def _ex_gather_small(arrs):
    n = len(arrs)

    def peer_of(m):
        x, y, c, _ = _place()
        return (1 - x if m & 4 else x, 1 - y if m & 2 else y, 1 - c if m & 1 else c)

    def start(ins, outs, scr):
        x, y, c, _ = _place()
        for m in range(1, N_DEV):
            for a in range(n):
                k = (N_DEV - 1) * a + m - 1
                _remote(ins[a], outs[a].at[4 * x + 2 * y + c], scr[0].at[k], scr[1].at[k], peer_of(m)).start()

    def finish(ins, outs, scr):
        for m in range(1, N_DEV):
            px, py, pc = peer_of(m)
            for a in range(n):
                k = (N_DEV - 1) * a + m - 1
                slot = outs[a].at[4 * px + 2 * py + pc]
                cp = _remote(ins[a], slot, scr[0].at[k], scr[1].at[k], (px, py, pc))
                cp.wait_send()
                cp.wait_recv()

    slots = [jnp.zeros((N_DEV,) + a.shape, a.dtype) for a in arrs]
    out_shape = [jax.ShapeDtypeStruct(s.shape, s.dtype) for s in slots]
    return _Exchange(list(arrs) + slots, out_shape, {n + a: a for a in range(n)},
                     [pltpu.SemaphoreType.DMA(((N_DEV - 1) * n,))] * 2, start, finish)


def _div_tile(n, want):
    best = None
    for t in range(8, min(n, want) + 1, 8):
        if n % t == 0:
            best = t
    assert best is not None, n
    return best


def _cast_into_slab(name, w, place, dtype):
    r, cc = w.shape
    tr = r if r * cc <= 128 * 1024 else _div_tile(r, 256)

    def body(s_ref, w_ref, o_ref):
        o_ref[...] = w_ref[...].astype(o_ref.dtype)

    return _pallas(
        body, name=name,
        grid_spec=pltpu.PrefetchScalarGridSpec(
            num_scalar_prefetch=1, grid=(r // tr,),
            in_specs=[pl.BlockSpec((tr, cc), lambda i, s: (i, 0))],
            out_specs=pl.BlockSpec((None, tr, cc), lambda i, s: (s[0], i, 0))),
        out_shape=jax.ShapeDtypeStruct((N_CHIPS, r, cc), dtype), compiler_params=_cp("parallel"),
    )(place, w)


def _add_half(name, g, rcv, place):
    nq, r, cc = g.shape
    hr = r // 2

    def body(s_ref, g_ref, r_ref, o_ref):
        o_ref[...] = (g_ref[...] + r_ref[...]).astype(o_ref.dtype)

    spec = pl.BlockSpec((None, hr, cc), lambda i, s: (i, 0, 0))
    return _pallas(
        body, name=name,
        grid_spec=pltpu.PrefetchScalarGridSpec(
            num_scalar_prefetch=1, grid=(nq,),
            in_specs=[pl.BlockSpec((None, hr, cc), lambda i, s: (i, s[1], 0)), spec], out_specs=spec),
        out_shape=jax.ShapeDtypeStruct((nq, hr, cc), BF16), compiler_params=_cp("parallel"),
    )(place, g, rcv)


def _sum_owner(name, part, rcv, place):
    _, hr, cc = part.shape
    tr = _div_tile(hr, 128)
    nb = hr // tr

    def body(s_ref, p_ref, r_ref, o_ref):
        o_ref[...] = ((p_ref[...].astype(F32) + r_ref[0].astype(F32)) + r_ref[1].astype(F32)) + r_ref[2].astype(F32)

    return _pallas(
        body, name=name,
        grid_spec=pltpu.PrefetchScalarGridSpec(
            num_scalar_prefetch=1, grid=(nb,),
            in_specs=[pl.BlockSpec((None, tr, cc), lambda i, s: (s[0], i, 0)),
                      pl.BlockSpec((3, tr, cc), lambda i, s: (0, i, 0))],
            out_specs=pl.BlockSpec((tr, cc), lambda i, s: (s[1] * nb + i, 0))),
        out_shape=jax.ShapeDtypeStruct((2 * hr, cc), F32), compiler_params=_cp("parallel"),
    )(place, part, rcv)


def _sum_small(gathered, local, place):
    n = len(gathered)

    def body(s_ref, *refs):
        g_refs, l_refs, o_refs = refs[:n], refs[n:2 * n], refs[2 * n:]
        me = s_ref[2]
        for g_ref, l_ref, o_ref in zip(g_refs, l_refs, o_refs):
            acc = None
            for d in range(N_DEV):
                term = jnp.where(me == d, l_ref[...], g_ref[d])
                acc = term if acc is None else acc + term
            o_ref[...] = acc

    def whole(shape):
        return pl.BlockSpec(shape, lambda i, s, nd=len(shape): (0,) * nd)

    return _pallas(
        body, name="sum_small",
        grid_spec=pltpu.PrefetchScalarGridSpec(
            num_scalar_prefetch=1, grid=(1,),
            in_specs=[whole(g.shape) for g in gathered] + [whole(a.shape) for a in local],
            out_specs=tuple(whole(a.shape) for a in local)),
        out_shape=tuple(jax.ShapeDtypeStruct(a.shape, a.dtype) for a in local), compiler_params=_cp("arbitrary"),
    )(place, *gathered, *local)


def _adamw(name, w, g, m, v):
    r, cc = w.shape
    tr = r if r * cc <= 128 * 1024 else _div_tile(r, 256)

    def body(w_ref, g_ref, m_ref, v_ref, d_ref, mo_ref, vo_ref):
        gv = g_ref[...]
        mn = ADAM_B1 * m_ref[...] + (1.0 - ADAM_B1) * gv
        vn = ADAM_B2 * v_ref[...] + (1.0 - ADAM_B2) * (gv * gv)
        m_hat = mn / (1.0 - ADAM_B1 ** ADAM_STEP)
        v_hat = vn / (1.0 - ADAM_B2 ** ADAM_STEP)
        d_ref[...] = -ADAM_LR * (m_hat / (jnp.sqrt(v_hat) + ADAM_EPS) + ADAM_WD * w_ref[...])
        mo_ref[...] = mn
        vo_ref[...] = vn

    spec = pl.BlockSpec((tr, cc), lambda i: (i, 0))
    sd = jax.ShapeDtypeStruct((r, cc), F32)
    return _pallas(
        body, name=name, grid=(r // tr,), in_specs=[spec] * 4, out_specs=(spec,) * 3, out_shape=(sd,) * 3,
        compiler_params=_cp("parallel"),
    )(w, g, m, v)


_BIG = ("w_in", "w_up", "w_branch", "w_mem_kv", "w_out", "w_down")
_BIG_SHARD_SHAPE = {"w_in": (1024, 1664), "w_up": (1024, 1408), "w_branch": (1536, 256),
                    "w_mem_kv": (256, 1024), "w_out": (256, 1024), "w_down": (704, 1024)}
_SMALL_SHAPE = {"norm1_g": (1, D_MODEL), "ln_v_g": (1, GM_WIDTH), "ln_v_b": (1, GM_WIDTH),
                "w_spatial": (GM_GROUPS * GM_CHUNK, GM_CHUNK), "b_spatial": (GM_GROUPS, GM_CHUNK),
                "lb_logits": (2, HG_HEADS * HG_DIM), "hgrn_norm_g": (1, HG_DIM), "mem_norm_g": (1, D_MODEL),
                "norm2_g": (1, D_MODEL), "conv_w": (3, D_FF), "conv_b": (1, D_FF), "final_g": (1, D_MODEL)}
_SMALL_EARLY = tuple(n for n in _SMALL_SHAPE if n != "norm1_g")
_PARAM_ORDER = ("norm1_g", "w_in", "ln_v_g", "ln_v_b", "w_spatial", "b_spatial", "lb_logits", "hgrn_norm_g",
                "mem_norm_g", "w_mem_kv", "w_branch", "w_out", "norm2_g", "w_up", "conv_w", "conv_b", "w_down",
                "final_g")


def _adamw_small(ws, gs, ms, vs):
    n = len(ws)

    def body(*refs):
        w_refs, g_refs, m_refs, v_refs = refs[:n], refs[n:2 * n], refs[2 * n:3 * n], refs[3 * n:4 * n]
        d_refs, mo_refs, vo_refs = refs[4 * n:5 * n], refs[5 * n:6 * n], refs[6 * n:]
        for k in range(n):
            gv = g_refs[k][...]
            mn = ADAM_B1 * m_refs[k][...] + (1.0 - ADAM_B1) * gv
            vn = ADAM_B2 * v_refs[k][...] + (1.0 - ADAM_B2) * (gv * gv)
            m_hat = mn / (1.0 - ADAM_B1 ** ADAM_STEP)
            v_hat = vn / (1.0 - ADAM_B2 ** ADAM_STEP)
            d_refs[k][...] = -ADAM_LR * (m_hat / (jnp.sqrt(v_hat) + ADAM_EPS) + ADAM_WD * w_refs[k][...])
            mo_refs[k][...] = mn
            vo_refs[k][...] = vn

    specs = [pl.BlockSpec(a.shape, lambda i: (0, 0)) for a in ws]
    shapes = tuple(jax.ShapeDtypeStruct(a.shape, F32) for a in ws)
    res = _pallas(
        body, name="adamw_small", grid=(1,), in_specs=specs * 4, out_specs=tuple(specs * 3), out_shape=shapes * 3,
        compiler_params=_cp("arbitrary"),
    )(*ws, *gs, *ms, *vs)
    return res[:n], res[n:2 * n], res[2 * n:]


class _Comm:
    _ROW_SHARDED = ("w_mem_kv", "w_out", "w_down")

    def __init__(self, slabs, place):
        self.slabs, self.place = slabs, place
        self.full, self.raw, self.parts, self.landing, self.bufs, self.done = {}, {}, {}, {}, {}, {}
        ex, deliver = self._gather(["w_in"])
        deliver(_run_exchanges("all_gather_w_in", [ex])[0])

    def w(self, name):
        a = self.full[name]
        if name in self._ROW_SHARDED:
            return a.reshape(-1, a.shape[-1])
        if name == "conv_w":
            return jnp.transpose(a, (1, 0, 2)).reshape(3, 1, D_FF)
        return a

    def grad(self, name, arr):
        self.raw[name] = arr.reshape((N_CHIPS, -1, arr.shape[-1]))

    def small_grads(self, arrays):
        self.small_local = list(arrays)

    def carry(self, tag, call):
        plan = self._plan(tag)
        if not plan:
            return call(())
        out, carried = call([ex for ex, _ in plan])
        for (_, deliver), res in zip(plan, carried):
            deliver(res)
        return out

    def finish(self, last_small):
        ex, deliver = self._share(["w_mem_kv", "w_in_top", "w_in_bot"])
        shared, small = _run_exchanges("share_and_gather_last", [ex, _ex_gather_small(last_small)])
        deliver(shared)
        self.done["w_in"] = jnp.concatenate([self.done.pop("w_in_top"), self.done.pop("w_in_bot")], axis=0)
        return self.done, self.small_local + list(last_small), self.small_everyone + small

    def _plan(self, tag):
        if tag == "in_proj":
            return [self._gather(["w_branch", "w_out", "w_mem_kv", "conv_w"])]
        if tag == "hgrn_fwd":
            return [self._gather(["w_up"], (0, 2))]
        if tag == "merge_fwd":
            return [self._gather(["w_up"], (1, 2))]
        if tag == "up_proj":
            return [self._gather(["w_down"])]
        if tag == "d_ff":
            return [self._to_sibling(["w_down"])]
        if tag == "conv_bwd":
            return [self._to_owner(["w_down"])]
        if tag == "d_h2":
            return [self._to_sibling(["w_up"]), self._share(["w_down"])]
        if tag == "merge_bwd":
            return [self._to_owner(["w_up"], (0, 2))]
        if tag == "d_branch":
            return [self._to_owner(["w_up"], (1, 2))]
        if tag == "gmlp_bwd":
            return [self._to_sibling(["w_out", "w_branch"])]
        if tag == "hgrn_bwd":
            return [self._to_owner(["w_out", "w_branch"]), self._share(["w_up"]), self._to_sibling(["w_mem_kv"])]
        if tag == "g_w_in_top":
            def keep(res):
                self.small_everyone = res

            return [self._to_owner(["w_mem_kv"]), (_ex_gather_small(self.small_local), keep)]
        if tag == "g_w_in_bot":
            return [self._to_sibling(["w_in_top"]), self._share(["w_out", "w_branch"])]
        if tag == "d_h":
            return [self._to_owner(["w_in_top"]), self._to_sibling(["w_in_bot"])]
        if tag == "norm1_bwd":
            return [self._to_owner(["w_in_bot"])]
        return []

    def _gather(self, names, part=(0, 1)):
        def deliver(res):
            self.slabs.update(zip(names, res))
            self.full.update(zip(names, res))

        return _ex_all_gather([self.slabs[n] for n in names], [n != "conv_w" for n in names], part), deliver

    def _to_sibling(self, names):
        def deliver(res):
            for n, r in zip(names, res):
                self.parts[n] = _add_half("rs_add_" + n, self.raw[n], r, self.place)

        return _ex_to_sibling([self.raw[n] for n in names]), deliver

    def _to_owner(self, names, part=(0, 1)):
        def deliver(res):
            for n, r in zip(names, res):
                if part[0] + 1 < part[1]:
                    self.landing[n] = r
                else:
                    self.bufs[n] = _sum_owner("rs_sum_" + n, self.parts[n], r, self.place)

        landing = [self.landing[n] for n in names] if part[0] else None
        return _ex_to_owner([self.parts[n] for n in names], part, landing), deliver

    def _share(self, names):
        return _ex_share_halves([self.bufs[n] for n in names]), lambda res: self.done.update(zip(names, res))


def kernel(x, mem, norm1_g, w_in, ln_v_g, ln_v_b, w_spatial, b_spatial, lb_logits, hgrn_norm_g, mem_norm_g, w_mem_kv, w_branch, w_out, norm2_g, w_up, conv_w, conv_b, w_down, final_g, loss_target, m_norm1_g, m_w_in, m_ln_v_g, m_ln_v_b, m_w_spatial, m_b_spatial, m_lb_logits, m_hgrn_norm_g, m_mem_norm_g, m_w_mem_kv, m_w_branch, m_w_out, m_norm2_g, m_w_up, m_conv_w, m_conv_b, m_w_down, m_final_g, v_norm1_g, v_w_in, v_ln_v_g, v_ln_v_b, v_w_spatial, v_b_spatial, v_lb_logits, v_hgrn_norm_g, v_mem_norm_g, v_w_mem_kv, v_w_branch, v_w_out, v_norm2_g, v_w_up, v_conv_w, v_conv_b, v_w_down, v_final_g):
    w = dict(norm1_g=norm1_g, w_in=w_in, ln_v_g=ln_v_g, ln_v_b=ln_v_b, w_spatial=w_spatial, b_spatial=b_spatial,
             lb_logits=lb_logits, hgrn_norm_g=hgrn_norm_g, mem_norm_g=mem_norm_g, w_mem_kv=w_mem_kv,
             w_branch=w_branch, w_out=w_out, norm2_g=norm2_g, w_up=w_up, conv_w=conv_w, conv_b=conv_b,
             w_down=w_down, final_g=final_g)
    mom = dict(norm1_g=m_norm1_g, w_in=m_w_in, ln_v_g=m_ln_v_g, ln_v_b=m_ln_v_b, w_spatial=m_w_spatial,
               b_spatial=m_b_spatial, lb_logits=m_lb_logits, hgrn_norm_g=m_hgrn_norm_g, mem_norm_g=m_mem_norm_g,
               w_mem_kv=m_w_mem_kv, w_branch=m_w_branch, w_out=m_w_out, norm2_g=m_norm2_g, w_up=m_w_up,
               conv_w=m_conv_w, conv_b=m_conv_b, w_down=m_w_down, final_g=m_final_g)
    var = dict(norm1_g=v_norm1_g, w_in=v_w_in, ln_v_g=v_ln_v_g, ln_v_b=v_ln_v_b, w_spatial=v_w_spatial,
               b_spatial=v_b_spatial, lb_logits=v_lb_logits, hgrn_norm_g=v_hgrn_norm_g, mem_norm_g=v_mem_norm_g,
               w_mem_kv=v_w_mem_kv, w_branch=v_w_branch, w_out=v_w_out, norm2_g=v_norm2_g, w_up=v_w_up,
               conv_w=v_conv_w, conv_b=v_conv_b, w_down=v_w_down, final_g=v_final_g)
    B, S, D = x.shape
    T = B * S
    ci = lax.axis_index("c")
    q = 2 * lax.axis_index("x") + lax.axis_index("y")
    place = jnp.stack([q, ci, 2 * q + ci]).astype(jnp.int32)

    slabs = {n: _cast_into_slab("slab_" + n, w[n].reshape(_BIG_SHARD_SHAPE[n]), place, BF16) for n in _BIG}
    slabs["conv_w"] = _cast_into_slab("slab_conv_w", conv_w[0], place, F32)
    comm = _Comm(slabs, place)
    p = dict(
        norm1_g=norm1_g, ln_v_g=ln_v_g, ln_v_b=ln_v_b, w_spatial=w_spatial[0],
        b_spatial=b_spatial.reshape(GM_GROUPS, GM_CHUNK, 1), lb_logits=lb_logits, hgrn_norm_g=hgrn_norm_g,
        mem_norm_g=mem_norm_g, norm2_g=norm2_g, conv_b=conv_b, final_g=final_g.reshape(1, D))

    loss, grad_x, g = _local_step(x.reshape(T, D), mem.reshape(B * MEM_LEN, D), loss_target.reshape(T, D), p, comm,
                                  B, S)

    shard_grads, local_small, everyone = comm.finish([g["norm1_g"]])
    summed = _sum_small(everyone, local_small, place)
    small_names = list(_SMALL_EARLY) + ["norm1_g"]
    total = dict(zip(_SMALL_EARLY, summed))
    loss_total, total["norm1_g"] = summed[len(_SMALL_EARLY)][0, 0], summed[-1]

    grads, delta, new_m, new_v = {}, {}, {}, {}
    for n in _BIG:
        shp = _BIG_SHARD_SHAPE[n]
        grads[n] = shard_grads[n]
        delta[n], new_m[n], new_v[n] = _adamw("adamw_" + n, w[n].reshape(shp), shard_grads[n],
                                              mom[n].reshape(shp), var[n].reshape(shp))
    cw_shard = D_FF // N_CHIPS
    total["conv_w"] = lax.dynamic_slice(total["conv_w"], (0, q * cw_shard), (3, cw_shard))

    def flat2d(d, n):
        return d[n].reshape(total[n].shape)

    upd = _adamw_small([flat2d(w, n) for n in small_names], [total[n] for n in small_names],
                       [flat2d(mom, n) for n in small_names], [flat2d(var, n) for n in small_names])
    for k, n in enumerate(small_names):
        grads[n], delta[n], new_m[n], new_v[n] = total[n], upd[0][k], upd[1][k], upd[2][k]

    def shaped(d):
        return [d[n].reshape(w[n].shape) for n in _PARAM_ORDER]

    return (loss_total, grad_x.reshape(B, S, D), *shaped(grads), *shaped(delta), *shaped(new_m), *shaped(new_v))
```

```python
import functools
import math

import jax
import jax.numpy as jnp
from jax import lax
from jax.experimental import pallas as pl
from jax.experimental.pallas import tpu as pltpu

F32 = jnp.float32
BF16 = jnp.bfloat16
EPS = 1e-6

D_MODEL = 1024
MEM_LEN = 256
GM_WIDTH = 512
GM_CHUNK = 128
GM_GROUPS = 4
HG_HEADS = 4
HG_DIM = 128
HG_CHUNK = 64
XA_HEADS = 4
XA_DIM = 128
BR_WIDTH = 512
D_FF = 2816
IN_WIDTH = 6656
N_CHIPS = 4
N_DEV = 8

ADAM_LR = 0.001
ADAM_B1 = 0.9
ADAM_B2 = 0.999
ADAM_EPS = 1e-08
ADAM_WD = 0.01
ADAM_STEP = 10

COL_ZU, COL_ZV, COL_HQ, COL_HF, COL_HI, COL_HG, COL_XQ = 0, 1, 2, 3, 4, 5, 6
COL_GATE0 = 3584

VMEM_LIMIT_BYTES = 48 * 1024 * 1024
MESH_ID = pl.DeviceIdType.MESH


def _cp(*sem):
    return pltpu.CompilerParams(dimension_semantics=sem, vmem_limit_bytes=VMEM_LIMIT_BYTES)


def _pallas(body, *, out_shape, **kw):
    def pin(s):
        return pltpu.HBM(s.shape, s.dtype) if isinstance(s, jax.ShapeDtypeStruct) else s

    out_shape = tuple(pin(s) for s in out_shape) if isinstance(out_shape, (tuple, list)) else pin(out_shape)
    call = pl.pallas_call(body, out_shape=out_shape, **kw)

    def run(*operands):
        return call(*[pltpu.with_memory_space_constraint(o, pltpu.HBM) if jnp.issubdtype(o.dtype, jnp.floating)
                      else o for o in operands])

    return run


def _dot(a, b):
    return lax.dot_general(a.astype(BF16), b.astype(BF16), (((1,), (0,)), ((), ())), preferred_element_type=F32)


def _dot_nt(a, b):
    return lax.dot_general(a.astype(BF16), b.astype(BF16), (((1,), (1,)), ((), ())), preferred_element_type=F32)


def _dot_tn(a, b):
    return lax.dot_general(a.astype(BF16), b.astype(BF16), (((0,), (0,)), ((), ())), preferred_element_type=F32)


def _dot_01(mask01, x):
    hi = x.astype(BF16)
    r1 = x - hi.astype(F32)
    mid = r1.astype(BF16)
    lo = (r1 - mid.astype(F32)).astype(BF16)
    m = mask01.astype(BF16)
    dn = (((1,), (0,)), ((), ()))
    return (lax.dot_general(m, hi, dn, preferred_element_type=F32)
            + lax.dot_general(m, mid, dn, preferred_element_type=F32)
            + lax.dot_general(m, lo, dn, preferred_element_type=F32))


def _sigmoid(z):
    return 1.0 / (1.0 + jnp.exp(-z))


_GELU_C = math.sqrt(2.0 / math.pi)


def _gelu_and_grad(z):
    inner = _GELU_C * (z + 0.044715 * z * z * z)
    t = jnp.tanh(inner)
    val = 0.5 * z * (1.0 + t)
    grad = 0.5 * (1.0 + t) + 0.5 * z * (1.0 - t * t) * _GELU_C * (1.0 + 3.0 * 0.044715 * z * z)
    return val, grad


def _row_tile(n, want=512):
    t = min(want, n)
    assert n % t == 0
    return t


def _pcall(body, operands, *, name, grid, in_specs, out_specs, out_shape, scratch_shapes=(), semantics, riders=()):
    single = not isinstance(out_shape, (tuple, list))
    out_specs = (out_specs,) if single else tuple(out_specs)
    out_shape = (out_shape,) if single else tuple(out_shape)
    if not riders:
        res = _pallas(body, name=name, grid=grid, in_specs=list(in_specs), out_specs=out_specs,
                             out_shape=out_shape, scratch_shapes=list(scratch_shapes),
                             compiler_params=_cp(*semantics))(*operands)
        return (res[0] if single else res), []
    n_in, n_out, n_scr = len(in_specs), len(out_shape), len(scratch_shapes)
    ex_in = [len(ex.operands) for ex in riders]
    ex_out = [len(ex.out_shape) for ex in riders]
    ex_scr = [len(ex.scratch) for ex in riders]
    tot_in, tot_out = n_in + sum(ex_in), n_out + sum(ex_out)

    def wrapped(*refs):
        ins, outs, scr = refs[:tot_in], refs[tot_in:tot_in + tot_out], refs[tot_in + tot_out:]
        ids = [pl.program_id(d) for d in range(len(grid))]
        first = functools.reduce(lambda p, t: p & t, [i == 0 for i in ids])
        last = functools.reduce(lambda p, t: p & t, [i == n - 1 for i, n in zip(ids, grid)])
        parts, oi, oo, os_ = [], n_in, n_out, n_scr
        for k in range(len(riders)):
            parts.append((ins[oi:oi + ex_in[k]], outs[oo:oo + ex_out[k]], scr[os_:os_ + ex_scr[k]]))
            oi, oo, os_ = oi + ex_in[k], oo + ex_out[k], os_ + ex_scr[k]

        @pl.when(first)
        def _():
            for ex, part in zip(riders, parts):
                ex.start(*part)

        body(*ins[:n_in], *outs[:n_out], *scr[:n_scr])

        @pl.when(last)
        def _():
            for ex, part in zip(riders, parts):
                ex.finish(*part)

    aliases, oi, oo = {}, n_in, n_out
    all_ops, all_shapes, all_scr = list(operands), list(out_shape), list(scratch_shapes)
    for k, ex in enumerate(riders):
        aliases.update({oi + a: oo + b for a, b in ex.aliases.items()})
        oi, oo = oi + ex_in[k], oo + ex_out[k]
        all_ops += list(ex.operands)
        all_shapes += [pltpu.HBM(s.shape, s.dtype) for s in ex.out_shape]
        all_scr += list(ex.scratch)
    res = _pallas(
        wrapped, name=name, grid=grid, in_specs=list(in_specs) + [HBM_SPEC] * sum(ex_in),
        out_specs=out_specs + (HBM_SPEC,) * sum(ex_out), out_shape=tuple(all_shapes), scratch_shapes=all_scr,
        input_output_aliases=aliases, compiler_params=_cp(*(["arbitrary"] * len(grid))))(*all_ops)
    own = res[0] if single else tuple(res[:n_out])
    carried, oo = [], n_out
    for k in range(len(riders)):
        carried.append(list(res[oo:oo + ex_out[k]]))
        oo += ex_out[k]
    return own, carried


def _carried(out, carried, riders):
    return (out, carried) if riders else out


def _matmul(name, operands, *, grid, in_specs, o_spec, out_shape, out_dtype, dims, has_res=False, riders=()):
    nk = grid[2]
    assert nk == 1 or (out_dtype == F32 and not has_res)

    def body(*refs):
        if has_res:
            a_ref, b_ref, r_ref, o_ref = refs
        else:
            a_ref, b_ref, o_ref = refs
            r_ref = None
        part = lax.dot_general(a_ref[...].astype(BF16), b_ref[...].astype(BF16), (dims, ((), ())),
                               preferred_element_type=F32)
        if nk == 1:
            if r_ref is not None:
                part = part + r_ref[...]
            o_ref[...] = part.astype(o_ref.dtype)
        else:
            k = pl.program_id(2)

            @pl.when(k == 0)
            def _():
                o_ref[...] = part

            @pl.when(k > 0)
            def _():
                o_ref[...] += part

    out, carried = _pcall(body, operands, name=name, grid=grid, in_specs=in_specs, out_specs=o_spec,
                          out_shape=jax.ShapeDtypeStruct(out_shape, out_dtype),
                          semantics=("parallel", "parallel", "arbitrary"), riders=riders)
    return (out, carried) if riders else out


NN = ((1,), (0,))
NT = ((1,), (1,))
TN = ((0,), (0,))
_TN_TOKENS = 4096


def _mm_cs(name, a, w, out_dtype, riders=()):
    M, K = a.shape
    nq, _, wd = w.shape
    tm = _row_tile(M)
    return _matmul(name, (a, w), grid=(nq, M // tm, 1),
                   in_specs=[pl.BlockSpec((tm, K), lambda j, i, k: (i, 0)),
                             pl.BlockSpec((None, K, wd), lambda j, i, k: (j, 0, 0))],
                   o_spec=pl.BlockSpec((tm, wd), lambda j, i, k: (i, j)),
                   out_shape=(M, nq * wd), out_dtype=out_dtype, dims=NN, riders=riders)


def _mm_rs(name, a, w, out_dtype, res=None):
    M, K = a.shape
    N = w.shape[1]
    tm = _row_tile(M)
    tn = N
    ops = (a, w) if res is None else (a, w, res)
    in_specs = [pl.BlockSpec((tm, K), lambda i, j, k: (i, 0)),
                pl.BlockSpec((K, tn), lambda i, j, k: (0, j))]
    if res is not None:
        in_specs.append(pl.BlockSpec((tm, tn), lambda i, j, k: (i, j)))
    return _matmul(name, ops, grid=(M // tm, N // tn, 1), in_specs=in_specs,
                   o_spec=pl.BlockSpec((tm, tn), lambda i, j, k: (i, j)),
                   out_shape=(M, N), out_dtype=out_dtype, dims=NN, has_res=res is not None)


def _mm_nt_rs(name, g, w, out_dtype, riders=()):
    M, N = g.shape
    K = w.shape[0]
    to = K
    tm = _row_tile(M)
    return _matmul(name, (g, w), grid=(M // tm, K // to, 1),
                   in_specs=[pl.BlockSpec((tm, N), lambda i, j, k: (i, 0)),
                             pl.BlockSpec((to, N), lambda i, j, k: (j, 0))],
                   o_spec=pl.BlockSpec((tm, to), lambda i, j, k: (i, j)),
                   out_shape=(M, K), out_dtype=out_dtype, dims=NT, riders=riders)


def _mm_nt_cs(name, g, w, out_dtype, riders=(), stacked=False):
    M = g.shape[-2]
    nq, K, wd = w.shape
    tm = _row_tile(M, 256)

    def body(g_ref, w_ref, o_ref):
        acc = None
        for q in range(nq):
            gq = g_ref[q // 2, :, (q % 2) * wd:(q % 2 + 1) * wd] if stacked else g_ref[:, q * wd:(q + 1) * wd]
            part = _dot_nt(gq, w_ref[q])
            acc = part if acc is None else acc + part
        o_ref[...] = acc.astype(o_ref.dtype)

    g_spec = (pl.BlockSpec((2, tm, 2 * wd), lambda i: (0, i, 0)) if stacked
              else pl.BlockSpec((tm, nq * wd), lambda i: (i, 0)))
    out, carried = _pcall(
        body, (g, w), name=name, grid=(M // tm,),
        in_specs=[g_spec, pl.BlockSpec((nq, K, wd), lambda i: (0, 0, 0))],
        out_specs=pl.BlockSpec((tm, K), lambda i: (i, 0)),
        out_shape=jax.ShapeDtypeStruct((M, K), out_dtype), semantics=("parallel",), riders=riders)
    return (out, carried) if riders else out


def _mm_tn_rs(name, a, g, to, tn=512):
    T, M = a.shape
    N = g.shape[1]
    tt = _row_tile(T, _TN_TOKENS)
    tn = min(tn, N)
    return _matmul(name, (a, g), grid=(M // to, N // tn, T // tt),
                   in_specs=[pl.BlockSpec((tt, to), lambda i, j, k: (k, i)),
                             pl.BlockSpec((tt, tn), lambda i, j, k: (k, j))],
                   o_spec=pl.BlockSpec((to, tn), lambda i, j, k: (i, j)),
                   out_shape=(M, N), out_dtype=F32, dims=TN)


def _mm_tn_cs(name, a, g, nq, to, riders=(), stacked=False):
    T, M = a.shape
    wd = g.shape[-1] * (2 if stacked else 1) // nq
    tt = _row_tile(T, _TN_TOKENS)
    g_spec = (pl.BlockSpec((None, tt, wd), lambda i, j, k: (j // 2, k, j % 2)) if stacked
              else pl.BlockSpec((tt, wd), lambda i, j, k: (k, j)))
    return _matmul(name, (a, g), grid=(M // to, nq, T // tt),
                   in_specs=[pl.BlockSpec((tt, to), lambda i, j, k: (k, i)), g_spec],
                   o_spec=pl.BlockSpec((None, to, wd), lambda i, j, k: (j, i, 0)),
                   out_shape=(nq, M, wd), out_dtype=F32, dims=TN, riders=riders)


def _rms_fwd(name, x, g):
    T, D = x.shape
    tm = _row_tile(T)

    def body(x_ref, g_ref, o_ref):
        xv = x_ref[...]
        r = lax.rsqrt(jnp.mean(xv * xv, axis=-1, keepdims=True) + EPS)
        o_ref[...] = (xv * r * g_ref[...]).astype(o_ref.dtype)

    return _pallas(
        body, name=name, grid=(T // tm,),
        in_specs=[pl.BlockSpec((tm, D), lambda i: (i, 0)), pl.BlockSpec((1, D), lambda i: (0, 0))],
        out_specs=pl.BlockSpec((tm, D), lambda i: (i, 0)),
        out_shape=jax.ShapeDtypeStruct((T, D), BF16), compiler_params=_cp("parallel"),
    )(x, g)


def _rms_bwd(name, x, g, dh, dres, riders=()):
    T, D = x.shape
    tm = _row_tile(T)
    has_res = dres is not None

    def body(*refs):
        if has_res:
            x_ref, g_ref, dh_ref, dr_ref, dx_ref, dg_ref = refs
        else:
            x_ref, g_ref, dh_ref, dx_ref, dg_ref = refs

        @pl.when(pl.program_id(0) == 0)
        def _():
            dg_ref[...] = jnp.zeros_like(dg_ref)

        xv = x_ref[...]
        r = lax.rsqrt(jnp.mean(xv * xv, axis=-1, keepdims=True) + EPS)
        n = xv * r
        dhv = dh_ref[...]
        dg_ref[...] += jnp.sum(dhv * n, axis=0, keepdims=True)
        dn = dhv * g_ref[...]
        dx = r * (dn - n * jnp.mean(dn * n, axis=-1, keepdims=True))
        if has_res:
            dx = dx + dr_ref[...]
        dx_ref[...] = dx

    row = pl.BlockSpec((tm, D), lambda i: (i, 0))
    vec = pl.BlockSpec((1, D), lambda i: (0, 0))
    ops = (x, g, dh, dres) if has_res else (x, g, dh)
    out, carried = _pcall(
        body, ops, name=name, grid=(T // tm,),
        in_specs=[row, vec, row] + ([row] if has_res else []),
        out_specs=(row, vec),
        out_shape=(jax.ShapeDtypeStruct((T, D), F32), jax.ShapeDtypeStruct((1, D), F32)),
        semantics=("arbitrary",), riders=riders)
    return (out, carried) if riders else out


def _loss_head(x2, tgt, g):
    T, D = x2.shape
    tm = _row_tile(T)

    def body(x_ref, t_ref, g_ref, dx_ref, dg_ref, loss_ref):
        @pl.when(pl.program_id(0) == 0)
        def _():
            dg_ref[...] = jnp.zeros_like(dg_ref)
            loss_ref[...] = jnp.zeros_like(loss_ref)

        xv = x_ref[...]
        gv = g_ref[...]
        r = lax.rsqrt(jnp.mean(xv * xv, axis=-1, keepdims=True) + EPS)
        n = xv * r
        diff = n * gv - t_ref[...]
        loss_ref[...] += 0.5 * jnp.sum(jnp.mean(diff * diff, axis=-1, keepdims=True))
        dy = diff * (1.0 / D)
        dg_ref[...] += jnp.sum(dy * n, axis=0, keepdims=True)
        dn = dy * gv
        dx_ref[...] = r * (dn - n * jnp.mean(dn * n, axis=-1, keepdims=True))

    row = pl.BlockSpec((tm, D), lambda i: (i, 0))
    vec = pl.BlockSpec((1, D), lambda i: (0, 0))
    return _pallas(
        body, name="loss_head", grid=(T // tm,),
        in_specs=[row, row, vec],
        out_specs=(row, vec, pl.BlockSpec((8, 128), lambda i: (0, 0))),
        out_shape=(jax.ShapeDtypeStruct((T, D), F32), jax.ShapeDtypeStruct((1, D), F32),
                   jax.ShapeDtypeStruct((8, 128), F32)),
        compiler_params=_cp("arbitrary"),
    )(x2, tgt, g)


def _gmlp_pieces(zu, zv, lng, lnb, ws_ref, bs_ref):
    u, du = _gelu_and_grad(zu)
    v, dv = _gelu_and_grad(zv)
    mu = jnp.mean(v, axis=-1, keepdims=True)
    vc = v - mu
    rstd = lax.rsqrt(jnp.mean(vc * vc, axis=-1, keepdims=True) + EPS)
    vhat = vc * rstd
    vn = vhat * lng + lnb
    row = lax.broadcasted_iota(jnp.int32, (GM_CHUNK, GM_CHUNK), 0)
    col = lax.broadcasted_iota(jnp.int32, (GM_CHUNK, GM_CHUNK), 1)
    tril = row >= col
    wms, mixed = [], []
    for g in range(GM_GROUPS):
        sl = slice(g * 128, (g + 1) * 128)
        wm = jnp.where(tril, ws_ref[g], 0.0)
        wms.append(wm)
        mixed.append(_dot(wm, vn[:, sl]) + bs_ref[g])
    return u, du, dv, rstd, vhat, vn, wms, mixed, tril


def _gmlp_fwd(proj, lng, lnb, ws, bs_col):
    T = proj.shape[0]
    n = T // GM_CHUNK

    def body(zu_ref, zv_ref, lng_ref, lnb_ref, ws_ref, bs_ref, o_ref):
        u, _, _, _, _, _, _, mixed, _ = _gmlp_pieces(zu_ref[...].astype(F32), zv_ref[...].astype(F32),
                                                     lng_ref[...], lnb_ref[...],
                                                     ws_ref, bs_ref)
        for g in range(GM_GROUPS):
            sl = slice(g * 128, (g + 1) * 128)
            o_ref[:, sl] = (u[:, sl] * mixed[g]).astype(o_ref.dtype)

    vec = pl.BlockSpec((1, GM_WIDTH), lambda i: (0, 0))
    return _pallas(
        body, name="gmlp_fwd", grid=(n,),
        in_specs=[pl.BlockSpec((GM_CHUNK, 512), lambda i: (i, COL_ZU)),
                  pl.BlockSpec((GM_CHUNK, 512), lambda i: (i, COL_ZV)),
                  vec, vec,
                  pl.BlockSpec((GM_GROUPS, 128, 128), lambda i: (0, 0, 0)),
                  pl.BlockSpec((GM_GROUPS, 128, 1), lambda i: (0, 0, 0))],
        out_specs=pl.BlockSpec((GM_CHUNK, 512), lambda i: (i, 0)),
        out_shape=jax.ShapeDtypeStruct((T, GM_WIDTH), BF16), compiler_params=_cp("parallel"),
    )(proj, proj, lng, lnb, ws, bs_col)


def _gmlp_bwd(proj, d_out, lng, lnb, ws, bs_col, riders=()):
    T = proj.shape[0]
    n = T // GM_CHUNK

    def body(zu_ref, zv_ref, do_ref, lng_ref, lnb_ref, ws_ref, bs_ref,
             dzu_ref, dzv_ref, dws_ref, dbs_ref, dlng_ref, dlnb_ref, dm_acc):
        i = pl.program_id(0)

        @pl.when(i == 0)
        def _():
            dws_ref[...] = jnp.zeros_like(dws_ref)
            dlng_ref[...] = jnp.zeros_like(dlng_ref)
            dlnb_ref[...] = jnp.zeros_like(dlnb_ref)
            dm_acc[...] = jnp.zeros_like(dm_acc)

        lng_v = lng_ref[...]
        u, du, dv, rstd, vhat, vn, wms, mixed, tril = _gmlp_pieces(zu_ref[...].astype(F32), zv_ref[...].astype(F32),
                                                                  lng_v, lnb_ref[...],
                                                                  ws_ref, bs_ref)
        do = do_ref[...]
        dvn_parts = []
        for g in range(GM_GROUPS):
            sl = slice(g * 128, (g + 1) * 128)
            dog = do[:, sl]
            dzu_ref[:, sl] = (dog * mixed[g] * du[:, sl]).astype(dzu_ref.dtype)
            dmix = dog * u[:, sl]
            dm_acc[:, sl] += dmix
            dws_ref[g] += jnp.where(tril, _dot_nt(dmix, vn[:, sl]), 0.0)
            dvn_parts.append(_dot_tn(wms[g], dmix))
        dvn = jnp.concatenate(dvn_parts, axis=1)
        dlng_ref[...] += jnp.sum(dvn * vhat, axis=0, keepdims=True)
        dlnb_ref[...] += jnp.sum(dvn, axis=0, keepdims=True)
        dvh = dvn * lng_v
        dvv = rstd * (dvh - jnp.mean(dvh, axis=-1, keepdims=True)
                      - vhat * jnp.mean(dvh * vhat, axis=-1, keepdims=True))
        dzv_ref[...] = (dvv * dv).astype(dzv_ref.dtype)

        @pl.when(i == n - 1)
        def _():
            for g in range(GM_GROUPS):
                dbs_ref[g] = jnp.sum(dm_acc[:, g * 128:(g + 1) * 128], axis=1, keepdims=True)

    vec = pl.BlockSpec((1, GM_WIDTH), lambda i: (0, 0))
    wsp = pl.BlockSpec((GM_GROUPS, 128, 128), lambda i: (0, 0, 0))
    bsp = pl.BlockSpec((GM_GROUPS, 128, 1), lambda i: (0, 0, 0))
    tile = pl.BlockSpec((GM_CHUNK, 512), lambda i: (i, 0))
    return _carried(*_pcall(
        body, (proj, proj, d_out, lng, lnb, ws, bs_col), name="gmlp_bwd", grid=(n,),
        in_specs=[pl.BlockSpec((GM_CHUNK, 512), lambda i: (i, COL_ZU)),
                  pl.BlockSpec((GM_CHUNK, 512), lambda i: (i, COL_ZV)),
                  pl.BlockSpec((None, GM_CHUNK, 512), lambda i: (0, i, 0)), vec, vec, wsp, bsp],
        out_specs=(tile, tile, wsp, bsp, vec, vec),
        out_shape=(jax.ShapeDtypeStruct((T, GM_WIDTH), BF16), jax.ShapeDtypeStruct((T, GM_WIDTH), BF16),
                   jax.ShapeDtypeStruct((GM_GROUPS, 128, 128), F32), jax.ShapeDtypeStruct((GM_GROUPS, 128, 1), F32),
                   jax.ShapeDtypeStruct((1, GM_WIDTH), F32), jax.ShapeDtypeStruct((1, GM_WIDTH), F32)),
        scratch_shapes=[pltpu.VMEM((GM_CHUNK, GM_WIDTH), F32)],
        semantics=("arbitrary",), riders=riders), riders)


def _hgrn_lower_bound(lbl):
    return 1.0 / (1.0 + jnp.exp(lbl[1:2, :] - lbl[0:1, :]))


def _hgrn_gates(hq, hf, lb):
    C = HG_CHUNK
    sg = _sigmoid(hf)
    fg = lb + (1.0 - lb) * sg
    sq = _sigmoid(hq)
    row = lax.broadcasted_iota(jnp.int32, (C, C), 0)
    col = lax.broadcasted_iota(jnp.int32, (C, C), 1)
    tril = row >= col
    logf = jnp.log(fg)
    a = _dot_01(tril, logf)
    a_last = jnp.sum(logf, axis=0, keepdims=True)
    first_half = lax.broadcasted_iota(jnp.int32, logf.shape, 0) < (C // 2)
    a_mid = jnp.sum(jnp.where(first_half, logf, 0.0), axis=0, keepdims=True)
    ea, ei, eki, ekl = jnp.exp(a), jnp.exp(a - a_mid), jnp.exp(a_mid - a), jnp.exp(a_last - a)
    k = 1.0 - fg
    q = hq * sq
    qi = (q * ei).astype(BF16).astype(F32)
    ki = (k * eki).astype(BF16).astype(F32)
    return dict(sg=sg, fg=fg, sq=sq, tril=tril, ea=ea, ei=ei, eki=eki, ekl=ekl, e_last=jnp.exp(a_last),
                qe=q * ea, qi=qi, ki=ki, kl=k * ekl)


def _heads(x):
    return [x[:, h * HG_DIM:(h + 1) * HG_DIM] for h in range(HG_HEADS)]


def _hgrn_fwd(proj, lbl, gh, B, S, riders=()):
    C = HG_CHUNK
    NC = S // C
    W = HG_HEADS * HG_DIM

    def body(q_ref, f_ref, i_ref, g_ref, lbl_ref, gh_ref, o_ref, bo_ref, st_ref, state):
        @pl.when(pl.program_id(0) == 0)
        def _():
            state[...] = jnp.zeros_like(state)

        lb = _hgrn_lower_bound(lbl_ref[...])
        ghv = gh_ref[...]
        for b in range(B):
            gt = _hgrn_gates(q_ref[b].astype(F32), f_ref[b].astype(F32), lb)
            v = _heads(i_ref[b])
            qe, qi, ki, kl, e_last = (_heads(gt[n]) for n in ("qe", "qi", "ki", "kl", "e_last"))
            outs, normed = [], []
            for h in range(HG_HEADS):
                p = jnp.where(gt["tril"], _dot_nt(qi[h], ki[h]), 0.0)
                st = state[b, h]
                st_ref[b, h] = st
                o = _dot_nt(qe[h], st) + _dot(p, v[h])
                state[b, h] = st * e_last[h] + _dot_tn(v[h], kl[h])
                outs.append(o)
                normed.append(o * lax.rsqrt(jnp.mean(o * o, axis=-1, keepdims=True) + EPS) * ghv)
            o_ref[b] = jnp.concatenate(outs, axis=1)
            hg = g_ref[b].astype(F32)
            bo_ref[b] = (jnp.concatenate(normed, axis=1) * (hg * _sigmoid(hg))).astype(bo_ref.dtype)

    def col(cb):
        return pl.BlockSpec((B, C, 512), lambda c: (0, c, cb))

    tile = pl.BlockSpec((B, C, W), lambda c: (0, c, 0))
    proj3 = proj.reshape(B, S, proj.shape[-1])
    out, carried = _pcall(
        body, (proj3, proj3, proj3, proj3, lbl, gh), name="hgrn_fwd", grid=(NC,),
        in_specs=[col(COL_HQ), col(COL_HF), col(COL_HI), col(COL_HG),
                  pl.BlockSpec((2, W), lambda c: (0, 0)), pl.BlockSpec((1, HG_DIM), lambda c: (0, 0))],
        out_specs=(tile, tile, pl.BlockSpec((B, None, HG_HEADS, 128, 128), lambda c: (0, c, 0, 0, 0))),
        out_shape=(jax.ShapeDtypeStruct((B, S, W), F32), jax.ShapeDtypeStruct((B, S, W), BF16),
                   jax.ShapeDtypeStruct((B, NC, HG_HEADS, 128, 128), F32)),
        scratch_shapes=[pltpu.VMEM((B, HG_HEADS, 128, 128), F32)],
        semantics=("arbitrary",), riders=riders)
    o_h, b_out, states = out
    out = (o_h, b_out.reshape(B * S, W), states)
    return (out, carried) if riders else out


def _hgrn_bwd(proj, o_saved, states, d_out, lbl, gh, B, S, riders=()):
    C = HG_CHUNK
    NC = S // C
    W = HG_HEADS * HG_DIM

    def body(q_ref, f_ref, i_ref, g_ref, o_ref, st_ref, do_ref, lbl_ref, gh_ref,
             dq_ref, df_ref, di_ref, dg_ref, dlbl_ref, dgh_ref, dstate, dlb_acc):
        c = pl.program_id(0)

        @pl.when(c == 0)
        def _():
            dstate[...] = jnp.zeros_like(dstate)
            dgh_ref[...] = jnp.zeros_like(dgh_ref)
            dlb_acc[...] = jnp.zeros_like(dlb_acc)

        lb = _hgrn_lower_bound(lbl_ref[...])
        ghv = gh_ref[...]
        row = lax.broadcasted_iota(jnp.int32, (C, C), 0)
        colm = lax.broadcasted_iota(jnp.int32, (C, C), 1)
        triu = colm >= row
        for b in range(B):
            hq, hg = q_ref[b].astype(F32), g_ref[b].astype(F32)
            gt = _hgrn_gates(hq, f_ref[b].astype(F32), lb)
            tril = gt["tril"]
            v = _heads(i_ref[b])
            qe, qi, ki, kl, e_last = (_heads(gt[n]) for n in ("qe", "qi", "ki", "kl", "e_last"))
            sgg = _sigmoid(hg)
            don_all = do_ref[b] * (hg * sgg)
            o, don = _heads(o_ref[b]), _heads(don_all)
            d_qe, d_qi, d_ki, d_kl, dv, n_all, dal = [], [], [], [], [], [], []
            for h in range(HG_HEADS):
                r = lax.rsqrt(jnp.mean(o[h] * o[h], axis=-1, keepdims=True) + EPS)
                n = o[h] * r
                n_all.append(n)
                dgh_ref[...] += jnp.sum(don[h] * n, axis=0, keepdims=True)
                dn = don[h] * ghv
                d_o = r * (dn - n * jnp.mean(dn * n, axis=-1, keepdims=True))
                st, dst = st_ref[b, h], dstate[b, h]
                p = jnp.where(tril, _dot_nt(qi[h], ki[h]), 0.0)
                dp = jnp.where(tril, _dot_nt(d_o, v[h]), 0.0)
                d_qe.append(_dot(d_o, st))
                d_qi.append(_dot(dp, ki[h]))
                d_ki.append(_dot_tn(dp, qi[h]))
                d_kl.append(_dot(v[h], dst))
                dv.append(_dot_tn(p, d_o) + _dot_nt(kl[h], dst))
                dstate[b, h] = dst * e_last[h] + _dot_tn(d_o, qe[h])
                dal.append(jnp.sum(dst * st, axis=0, keepdims=True) * e_last[h])
            d_qe, d_qi, d_ki, d_kl, n_all, dal = (jnp.concatenate(t, axis=1)
                                                  for t in (d_qe, d_qi, d_ki, d_kl, n_all, dal))
            dg_ref[b] = (do_ref[b] * n_all * jnp.tile(ghv, (1, HG_HEADS))
                         * (sgg * (1.0 + hg * (1.0 - sgg)))).astype(dg_ref.dtype)
            di_ref[b] = jnp.concatenate(dv, axis=1).astype(di_ref.dtype)
            d_a_last = dal + jnp.sum(d_kl * gt["kl"], axis=0, keepdims=True)
            dq = d_qe * gt["ea"] + d_qi * gt["ei"]
            dk = d_ki * gt["eki"] + d_kl * gt["ekl"]
            da = d_qe * gt["qe"] + d_qi * gt["qi"] - d_ki * gt["ki"] - d_kl * gt["kl"]
            dlogf = _dot_01(triu, da) + d_a_last
            sg, sq = gt["sg"], gt["sq"]
            dfg = dlogf / gt["fg"] - dk
            df_ref[b] = (dfg * (1.0 - lb) * sg * (1.0 - sg)).astype(df_ref.dtype)
            dlb_acc[...] += jnp.sum(dfg * (1.0 - sg), axis=0, keepdims=True)
            dq_ref[b] = (dq * (sq * (1.0 + hq * (1.0 - sq)))).astype(dq_ref.dtype)

        @pl.when(c == NC - 1)
        def _():
            dlb = dlb_acc[...]
            first = lax.broadcasted_iota(jnp.int32, (2, W), 0) == 0
            dlbl_ref[...] = jnp.where(first, dlb * lb * (1.0 - lb), -dlb * lb * (1.0 - lb))

    def col(cb):
        return pl.BlockSpec((B, C, 512), lambda c: (0, NC - 1 - c, cb))

    tile = pl.BlockSpec((B, C, W), lambda c: (0, NC - 1 - c, 0))
    proj3 = proj.reshape(B, S, proj.shape[-1])
    d3 = jax.ShapeDtypeStruct((B, S, W), BF16)
    out, carried = _pcall(
        body, (proj3, proj3, proj3, proj3, o_saved, states, d_out.reshape(3, B, S, W), lbl, gh), name="hgrn_bwd",
        grid=(NC,),
        in_specs=[col(COL_HQ), col(COL_HF), col(COL_HI), col(COL_HG), tile,
                  pl.BlockSpec((B, None, HG_HEADS, 128, 128), lambda c: (0, NC - 1 - c, 0, 0, 0)),
                  pl.BlockSpec((None, B, C, W), lambda c: (1, 0, NC - 1 - c, 0)),
                  pl.BlockSpec((2, W), lambda c: (0, 0)), pl.BlockSpec((1, HG_DIM), lambda c: (0, 0))],
        out_specs=(tile, tile, tile, tile,
                   pl.BlockSpec((2, W), lambda c: (0, 0)), pl.BlockSpec((1, HG_DIM), lambda c: (0, 0))),
        out_shape=(d3, d3, d3, d3, jax.ShapeDtypeStruct((2, W), F32), jax.ShapeDtypeStruct((1, HG_DIM), F32)),
        scratch_shapes=[pltpu.VMEM((B, HG_HEADS, 128, 128), F32), pltpu.VMEM((1, W), F32)],
        semantics=("arbitrary",), riders=riders)
    out = tuple(t.reshape(B * S, W) for t in out[:4]) + tuple(out[4:])
    return (out, carried) if riders else out


_XA_SCALE = XA_DIM ** -0.5


def _attn_probs(qh, kh):
    s = _dot_nt(qh, kh) * _XA_SCALE
    e = jnp.exp(s - jnp.max(s, axis=-1, keepdims=True))
    return e / jnp.sum(e, axis=-1, keepdims=True)


def _attn_fwd(proj, kv, B, S):
    T = B * S
    tq = _row_tile(S)
    nq = S // tq
    W = XA_HEADS * XA_DIM

    def body(q_ref, kv_ref, o_ref):
        for h in range(XA_HEADS):
            sl = slice(h * 128, (h + 1) * 128)
            p = _attn_probs(q_ref[:, sl], kv_ref[:, sl])
            o_ref[:, sl] = _dot(p, kv_ref[:, W + h * 128:W + (h + 1) * 128]).astype(o_ref.dtype)

    return _pallas(
        body, name="attn_fwd", grid=(B, nq),
        in_specs=[pl.BlockSpec((tq, 512), lambda b, i: (b * nq + i, COL_XQ)),
                  pl.BlockSpec((MEM_LEN, 2 * W), lambda b, i: (b, 0))],
        out_specs=pl.BlockSpec((tq, W), lambda b, i: (b * nq + i, 0)),
        out_shape=jax.ShapeDtypeStruct((T, W), BF16), compiler_params=_cp("parallel", "parallel"),
    )(proj, kv)


def _attn_bwd(proj, kv, d_out, B, S):
    T = B * S
    tq = _row_tile(S)
    nq = S // tq
    W = XA_HEADS * XA_DIM

    def body(q_ref, kv_ref, do_ref, dq_ref, dkv_ref):
        @pl.when(pl.program_id(1) == 0)
        def _():
            dkv_ref[...] = jnp.zeros_like(dkv_ref)

        for h in range(XA_HEADS):
            sl = slice(h * 128, (h + 1) * 128)
            slv = slice(W + h * 128, W + (h + 1) * 128)
            qh = q_ref[:, sl]
            kh = kv_ref[:, sl]
            p = _attn_probs(qh, kh)
            dc = do_ref[:, sl]
            dp = _dot_nt(dc, kv_ref[:, slv])
            ds = p * (dp - jnp.sum(dp * p, axis=-1, keepdims=True)) * _XA_SCALE
            dq_ref[:, sl] = _dot(ds, kh).astype(dq_ref.dtype)
            dkv_ref[:, sl] += _dot_tn(ds, qh)
            dkv_ref[:, slv] += _dot_tn(p, dc)

    kvspec = pl.BlockSpec((MEM_LEN, 2 * W), lambda b, i: (b, 0))
    tile = pl.BlockSpec((tq, W), lambda b, i: (b * nq + i, 0))
    return _pallas(
        body, name="attn_bwd", grid=(B, nq),
        in_specs=[pl.BlockSpec((tq, 512), lambda b, i: (b * nq + i, COL_XQ)), kvspec,
                  pl.BlockSpec((None, tq, W), lambda b, i: (2, b * nq + i, 0))],
        out_specs=(tile, kvspec),
        out_shape=(jax.ShapeDtypeStruct((T, W), BF16), jax.ShapeDtypeStruct((B * MEM_LEN, 2 * W), F32)),
        compiler_params=_cp("parallel", "arbitrary"),
    )(proj, kv, d_out)


_MERGE_TM = 256
_GATE_W = 512


def _gate_specs(tm):
    base = COL_GATE0 // _GATE_W
    return [pl.BlockSpec((tm, _GATE_W), functools.partial(lambda i, k: (i, base + k), k=k)) for k in range(6)]


def _merge_fwd(a_out, b_out, c_out, wb, proj, riders=()):
    T = a_out.shape[0]
    tm = _row_tile(T, _MERGE_TM)
    nq, _, wd = wb.shape
    per_half = _GATE_W // wd

    def body(a_ref, b_ref, c_ref, w_ref, *rest):
        gates, (m_ref, up_ref) = rest[:6], rest[6:]
        for hf in range(2):
            cols = slice(hf * _GATE_W, (hf + 1) * _GATE_W)
            acc = None
            for n, br in enumerate((a_ref, b_ref, c_ref)):
                x = br[...]
                up = jnp.concatenate([_dot(x, w_ref[per_half * hf + j, n * BR_WIDTH:(n + 1) * BR_WIDTH, :])
                                      for j in range(per_half)], axis=1)
                up_ref[n, :, cols] = up.astype(up_ref.dtype)
                term = _sigmoid(gates[2 * n + hf][...].astype(F32)) * up
                acc = term if acc is None else acc + term
            m_ref[:, cols] = acc.astype(m_ref.dtype)

    br_spec = pl.BlockSpec((tm, BR_WIDTH), lambda i: (i, 0))
    return _carried(*_pcall(
        body, (a_out, b_out, c_out, wb, *([proj] * 6)), name="merge_fwd", grid=(T // tm,),
        in_specs=[br_spec, br_spec, br_spec,
                  pl.BlockSpec((nq, 3 * BR_WIDTH, wd), lambda i: (0, 0, 0))] + _gate_specs(tm),
        out_specs=(pl.BlockSpec((tm, D_MODEL), lambda i: (i, 0)), pl.BlockSpec((3, tm, D_MODEL), lambda i: (0, i, 0))),
        out_shape=(jax.ShapeDtypeStruct((T, D_MODEL), BF16), jax.ShapeDtypeStruct((3, T, D_MODEL), BF16)),
        semantics=("parallel",), riders=riders), riders)


def _branch_bwd_act(d_ups, wb, riders=()):
    _, T, D = d_ups.shape
    nq, _, wd = wb.shape
    tm = _row_tile(T)

    def body(d_ref, w_ref, o_ref):
        acc = None
        for q in range(nq):
            part = _dot_nt(d_ref[:, q * wd:(q + 1) * wd], w_ref[q])
            acc = part if acc is None else acc + part
        o_ref[...] = acc

    return _carried(*_pcall(
        body, (d_ups, wb), name="d_branch", grid=(3, T // tm),
        in_specs=[pl.BlockSpec((None, tm, D), lambda n, i: (n, i, 0)),
                  pl.BlockSpec((nq, BR_WIDTH, wd), lambda n, i: (0, n, 0))],
        out_specs=pl.BlockSpec((None, tm, BR_WIDTH), lambda n, i: (n, i, 0)),
        out_shape=jax.ShapeDtypeStruct((3, T, BR_WIDTH), F32), semantics=("parallel", "parallel"),
        riders=riders), riders)


def _branch_bwd_weight(name, br, d_ups, n):
    T = br.shape[0]
    D = d_ups.shape[2]
    wd = D // N_CHIPS
    tt = _row_tile(T, _TN_TOKENS)

    def body(b_ref, d_ref, o_ref):
        k = pl.program_id(0)
        for q in range(N_CHIPS):
            part = _dot_tn(b_ref[...], d_ref[:, q * wd:(q + 1) * wd])

            @pl.when(k == 0)
            def _():
                o_ref[q] = part

            @pl.when(k > 0)
            def _():
                o_ref[q] += part

    return _pallas(
        body, name=name, grid=(T // tt,),
        in_specs=[pl.BlockSpec((tt, BR_WIDTH), lambda k: (k, 0)),
                  pl.BlockSpec((None, tt, D), lambda k: (n, k, 0))],
        out_specs=pl.BlockSpec((N_CHIPS, BR_WIDTH, wd), lambda k: (0, 0, 0)),
        out_shape=jax.ShapeDtypeStruct((N_CHIPS, BR_WIDTH, wd), F32), compiler_params=_cp("arbitrary"),
    )(br, d_ups)


def _merge_bwd(d_merged, ups, proj, riders=()):
    T = d_merged.shape[0]
    tm = _row_tile(T, _MERGE_TM)

    def body(dm_ref, up_ref, *rest):
        gates, (dup_ref, dg0_ref, dg1_ref, dg2_ref) = rest[:6], rest[6:]
        for hf in range(2):
            cols = slice(hf * _GATE_W, (hf + 1) * _GATE_W)
            dm = dm_ref[:, cols]
            for n, dgr in enumerate((dg0_ref, dg1_ref, dg2_ref)):
                gate = _sigmoid(gates[2 * n + hf][...].astype(F32))
                dup_ref[n, :, cols] = (dm * gate).astype(dup_ref.dtype)
                dgr[:, cols] = (dm * up_ref[n, :, cols].astype(F32) * gate * (1.0 - gate)).astype(dgr.dtype)

    tile = pl.BlockSpec((tm, D_MODEL), lambda i: (i, 0))
    tile3 = pl.BlockSpec((3, tm, D_MODEL), lambda i: (0, i, 0))
    return _carried(*_pcall(
        body, (d_merged, ups, *([proj] * 6)), name="merge_bwd", grid=(T // tm,),
        in_specs=[tile, tile3] + _gate_specs(tm),
        out_specs=(tile3, tile, tile, tile),
        out_shape=(jax.ShapeDtypeStruct((3, T, D_MODEL), BF16),) + (jax.ShapeDtypeStruct((T, D_MODEL), BF16),) * 3,
        semantics=("parallel",), riders=riders), riders)


_CONV_TF = D_FF // 2
_CONV_TS = 256
_HALO = 16


def _conv_fwd(ab, cw, cb, B, S):
    T = B * S
    ts = _row_tile(S, _CONV_TS)
    tf = _CONV_TF
    nb = D_FF // tf
    tps = S // ts
    hb = ts // _HALO

    def body(a_ref, p_ref, b_ref, w_ref, cb_ref, o_ref):
        start = (pl.program_id(0) % tps) == 0
        a = a_ref[...].astype(F32)
        prev = jnp.where(start, 0.0, p_ref[...].astype(F32))
        ext = jnp.concatenate([prev, a], axis=0)
        a1 = pltpu.roll(ext, 1, 0)[_HALO:, :]
        a2 = pltpu.roll(ext, 2, 0)[_HALO:, :]
        ac = cb_ref[...] + w_ref[0] * a2 + w_ref[1] * a1 + w_ref[2] * a
        o_ref[...] = (ac * _sigmoid(ac) * b_ref[...].astype(F32)).astype(o_ref.dtype)

    return _pallas(
        body, name="conv_fwd", grid=(T // ts, nb),
        in_specs=[pl.BlockSpec((ts, tf), lambda i, j: (i, j)),
                  pl.BlockSpec((_HALO, tf), lambda i, j: (jnp.maximum(i * hb - 1, 0), j)),
                  pl.BlockSpec((ts, tf), lambda i, j: (i, j + nb)),
                  pl.BlockSpec((3, 1, tf), lambda i, j: (0, 0, j)),
                  pl.BlockSpec((1, tf), lambda i, j: (0, j))],
        out_specs=pl.BlockSpec((ts, tf), lambda i, j: (i, j)),
        out_shape=jax.ShapeDtypeStruct((T, D_FF), BF16), compiler_params=_cp("parallel", "parallel"),
    )(ab, ab, ab, cw, cb)


def _conv_bwd(ab, d_ff, cw, cb, B, S, riders=()):
    T = B * S
    ts = _row_tile(S, _CONV_TS)
    tf = _CONV_TF
    nb = D_FF // tf
    tps = S // ts
    hb = ts // _HALO
    last_h = T // _HALO - 1
    n_ext = ts + _HALO

    def body(a_ref, ap_ref, an_ref, b_ref, bn_ref, d_ref, dn_ref, w_ref, cb_ref, dab_ref, dw_ref, dcb_ref):
        i = pl.program_id(1)

        @pl.when(i == 0)
        def _():
            dw_ref[...] = jnp.zeros_like(dw_ref)
            dcb_ref[...] = jnp.zeros_like(dcb_ref)

        start = (i % tps) == 0
        end = (i % tps) == tps - 1
        a = a_ref[...].astype(F32)
        ext = jnp.concatenate([jnp.where(start, 0.0, ap_ref[...].astype(F32)), a, an_ref[...].astype(F32)], axis=0)
        r1 = pltpu.roll(ext, 1, 0)[_HALO:, :]
        r2 = pltpu.roll(ext, 2, 0)[_HALO:, :]
        ac = cb_ref[...] + w_ref[0] * r2 + w_ref[1] * r1 + w_ref[2] * ext[_HALO:, :]
        sg = _sigmoid(ac)
        d_e = jnp.concatenate([d_ref[...].astype(F32), jnp.where(end, 0.0, dn_ref[...].astype(F32))], axis=0)
        b_e = jnp.concatenate([b_ref[...].astype(F32), bn_ref[...].astype(F32)], axis=0)
        dab_ref[1] = (d_e[:ts, :] * (ac * sg)[:ts, :]).astype(dab_ref.dtype)
        dac = d_e * b_e * sg * (1.0 + ac * (1.0 - sg))
        u1 = pltpu.roll(dac, n_ext - 1, 0)[:ts, :]
        u2 = pltpu.roll(dac, n_ext - 2, 0)[:ts, :]
        dac0 = dac[:ts, :]
        dab_ref[0] = (w_ref[2] * dac0 + w_ref[1] * u1 + w_ref[0] * u2).astype(dab_ref.dtype)
        dcb_ref[...] += jnp.sum(dac0, axis=0, keepdims=True)
        dw_ref[2] += jnp.sum(dac0 * a, axis=0, keepdims=True)
        dw_ref[1] += jnp.sum(dac0 * r1[:ts, :], axis=0, keepdims=True)
        dw_ref[0] += jnp.sum(dac0 * r2[:ts, :], axis=0, keepdims=True)

    def cur(off):
        return pl.BlockSpec((ts, tf), lambda j, i: (i, j + off))

    def nxt(off):
        return pl.BlockSpec((_HALO, tf), lambda j, i: (jnp.minimum((i + 1) * hb, last_h), j + off))

    return _carried(*_pcall(
        body, (ab, ab, ab, ab, ab, d_ff, d_ff, cw, cb), name="conv_bwd", grid=(nb, T // ts),
        in_specs=[cur(0), pl.BlockSpec((_HALO, tf), lambda j, i: (jnp.maximum(i * hb - 1, 0), j)), nxt(0),
                  cur(nb), nxt(nb), cur(0), nxt(0),
                  pl.BlockSpec((3, 1, tf), lambda j, i: (0, 0, j)), pl.BlockSpec((1, tf), lambda j, i: (0, j))],
        out_specs=(pl.BlockSpec((2, ts, tf), lambda j, i: (0, i, j)), pl.BlockSpec((3, 1, tf), lambda j, i: (0, 0, j)),
                   pl.BlockSpec((1, tf), lambda j, i: (0, j))),
        out_shape=(jax.ShapeDtypeStruct((2, T, D_FF), BF16),
                   jax.ShapeDtypeStruct((3, 1, D_FF), F32), jax.ShapeDtypeStruct((1, D_FF), F32)),
        semantics=("parallel", "arbitrary"), riders=riders), riders)


def _local_step(x, mem, tgt, p, comm, B, S):
    g = {}
    h = _rms_fwd("norm1", x, p["norm1_g"])
    proj = comm.carry("in_proj", lambda r: _mm_cs("in_proj", h, comm.w("w_in"), BF16, riders=r))
    a_out = _gmlp_fwd(proj, p["ln_v_g"], p["ln_v_b"], p["w_spatial"], p["b_spatial"])
    o_h, b_out, states = comm.carry(
        "hgrn_fwd", lambda r: _hgrn_fwd(proj, p["lb_logits"], p["hgrn_norm_g"], B, S, riders=r))
    memn = _rms_fwd("mem_norm", mem, p["mem_norm_g"])
    kv = _mm_rs("mem_kv", memn, comm.w("w_mem_kv"), F32)
    c_out = _attn_fwd(proj, kv, B, S)
    merged, ups = comm.carry(
        "merge_fwd", lambda r: _merge_fwd(a_out, b_out, c_out, comm.w("w_branch"), proj, riders=r))
    x1 = _mm_rs("out_proj", merged, comm.w("w_out"), F32, res=x)
    h2 = _rms_fwd("norm2", x1, p["norm2_g"])
    ab = comm.carry("up_proj", lambda r: _mm_cs("up_proj", h2, comm.w("w_up"), BF16, riders=r))
    conv_w = comm.w("conv_w")
    ff = _conv_fwd(ab, conv_w, p["conv_b"], B, S)
    x2 = _mm_rs("down_proj", ff, comm.w("w_down"), F32, res=x1)
    dx2, g["final_g"], loss = _loss_head(x2, tgt, p["final_g"])

    comm.grad("w_down", _mm_tn_rs("g_w_down", ff, dx2, to=D_FF // 2))
    d_ff = comm.carry("d_ff", lambda r: _mm_nt_rs("d_ff", dx2, comm.w("w_down"), BF16, riders=r))
    d_ab, g["conv_w"], g["conv_b"] = comm.carry(
        "conv_bwd", lambda r: _conv_bwd(ab, d_ff, conv_w, p["conv_b"], B, S, riders=r))
    comm.grad("w_up", _mm_tn_cs("g_w_up", h2, d_ab, N_CHIPS, to=512, stacked=True))
    d_h2 = comm.carry("d_h2", lambda r: _mm_nt_cs("d_h2", d_ab, comm.w("w_up"), F32, riders=r, stacked=True))
    d_x1, g["norm2_g"] = _rms_bwd("norm2_bwd", x1, p["norm2_g"], d_h2, dx2)
    comm.grad("w_out", _mm_tn_rs("g_w_out", merged, d_x1, to=512))
    d_merged = _mm_nt_rs("d_merged", d_x1, comm.w("w_out"), F32)
    d_ups, d_g0, d_g1, d_g2 = comm.carry("merge_bwd", lambda r: _merge_bwd(d_merged, ups, proj, riders=r))

    d_br = comm.carry("d_branch", lambda r: _branch_bwd_act(d_ups, comm.w("w_branch"), riders=r))
    comm.grad("w_branch", jnp.concatenate(
        [_branch_bwd_weight("g_w_branch%d" % n, br, d_ups, n) for n, br in enumerate((a_out, b_out, c_out))],
        axis=1))

    d_zu, d_zv, g["w_spatial"], g["b_spatial"], g["ln_v_g"], g["ln_v_b"] = comm.carry(
        "gmlp_bwd", lambda r: _gmlp_bwd(proj, d_br, p["ln_v_g"], p["ln_v_b"], p["w_spatial"], p["b_spatial"],
                                        riders=r))
    d_xq, d_kv = _attn_bwd(proj, kv, d_br, B, S)
    comm.grad("w_mem_kv", _mm_tn_rs("g_w_mem_kv", memn, d_kv, to=512))
    d_memn = _mm_nt_rs("d_memn", d_kv, comm.w("w_mem_kv"), F32)
    _, g["mem_norm_g"] = _rms_bwd("mem_norm_bwd", mem, p["mem_norm_g"], d_memn, None)
    d_hq, d_hf, d_hi, d_hg, g["lb_logits"], g["hgrn_norm_g"] = comm.carry(
        "hgrn_bwd", lambda r: _hgrn_bwd(proj, o_h, states, d_br, p["lb_logits"], p["hgrn_norm_g"], B, S, riders=r))
    d_proj = jnp.concatenate([d_zu, d_zv, d_hq, d_hf, d_hi, d_hg, d_xq, d_g0, d_g1, d_g2], axis=1)
    comm.small_grads([g[n].reshape(_SMALL_SHAPE[n]) for n in _SMALL_EARLY] + [loss])
    comm.grad("w_in", comm.carry("g_w_in", lambda r: _mm_tn_cs("g_w_in", h, d_proj, N_CHIPS, to=512, riders=r)))
    d_h = comm.carry("d_h", lambda r: _mm_nt_cs("d_h", d_proj, comm.w("w_in"), F32, riders=r))
    grad_x, g["norm1_g"] = _rms_bwd("norm1_bwd", x, p["norm1_g"], d_h, d_x1)
    return loss, grad_x, g


HBM_SPEC = pl.BlockSpec(memory_space=pltpu.HBM)


def _place():
    x, y, c = lax.axis_index("x"), lax.axis_index("y"), lax.axis_index("c")
    other_chips = [(1 - x, y), (x, 1 - y), (1 - x, 1 - y)]
    return x, y, c, other_chips


def _remote(src, dst, send_sem, recv_sem, dev):
    return pltpu.make_async_remote_copy(src_ref=src, dst_ref=dst, send_sem=send_sem, recv_sem=recv_sem,
                                        device_id=dev, device_id_type=MESH_ID)


class _Exchange:
    def __init__(self, operands, out_shape, aliases, scratch, start, finish):
        self.operands, self.out_shape, self.aliases, self.scratch = operands, out_shape, aliases, scratch
        self.start, self.finish = start, finish


def _run_exchanges(name, exs):
    n_in = [len(ex.operands) for ex in exs]
    n_out = [len(ex.out_shape) for ex in exs]
    n_scr = [len(ex.scratch) for ex in exs]

    def body(*refs):
        ins, outs, scr = refs[:sum(n_in)], refs[sum(n_in):sum(n_in) + sum(n_out)], refs[sum(n_in) + sum(n_out):]
        parts, oi, oo, os_ = [], 0, 0, 0
        for k in range(len(exs)):
            parts.append((ins[oi:oi + n_in[k]], outs[oo:oo + n_out[k]], scr[os_:os_ + n_scr[k]]))
            oi, oo, os_ = oi + n_in[k], oo + n_out[k], os_ + n_scr[k]
        for ex, part in zip(exs, parts):
            ex.start(*part)
        for ex, part in zip(exs, parts):
            ex.finish(*part)

    aliases, ops, shapes, scratch, oi, oo = {}, [], [], [], 0, 0
    for k, ex in enumerate(exs):
        aliases.update({oi + a: oo + b for a, b in ex.aliases.items()})
        oi, oo = oi + n_in[k], oo + n_out[k]
        ops += list(ex.operands)
        shapes += [pltpu.HBM(s.shape, s.dtype) for s in ex.out_shape]
        scratch += list(ex.scratch)
    res = _pallas(
        body, name=name, in_specs=[HBM_SPEC] * len(ops), out_specs=(HBM_SPEC,) * len(shapes), out_shape=tuple(shapes),
        input_output_aliases=aliases, scratch_shapes=scratch,
    )(*ops)
    out, oo = [], 0
    for k in range(len(exs)):
        out.append(list(res[oo:oo + n_out[k]]))
        oo += n_out[k]
    return out


def _ex_all_gather(slabs, halved, part=(0, 1)):
    n = len(slabs)

    def rows(a, cc):
        if not halved[a]:
            return slice(None)
        pr = slabs[a].shape[1] // part[1]
        return pl.ds(part[0] * pr + cc * (pr // 2), pr // 2)

    def ici(bufs, scr, a, j, chip, c, mine):
        px, py = chip
        x, y, _, _ = _place()
        qs = 2 * x + y if mine else 2 * px + py
        piece = bufs[a].at[qs, rows(a, c)]
        return _remote(piece, piece, scr[0].at[3 * a + j], scr[1].at[3 * a + j], (px, py, c))

    def d2d(bufs, scr, a, j, chip, cc):
        px, py = chip
        x, y, c, _ = _place()
        piece = bufs[a].at[2 * px + py, rows(a, cc)]
        return _remote(piece, piece, scr[2].at[3 * a + j], scr[3].at[3 * a + j], (x, y, 1 - c))

    def start(ins, outs, scr):
        _, _, c, chips = _place()
        for j, chip in enumerate(chips):
            for a in range(n):
                ici(outs, scr, a, j, chip, c, True).start()

    def finish(ins, outs, scr):
        _, _, c, chips = _place()
        for j, chip in enumerate(chips):
            for a in range(n):
                ici(outs, scr, a, j, chip, c, False).wait_recv()
                if halved[a]:
                    d2d(outs, scr, a, j, chip, c).start()
        for j, chip in enumerate(chips):
            for a in range(n):
                if halved[a]:
                    d2d(outs, scr, a, j, chip, 1 - c).wait_recv()
        for j, chip in enumerate(chips):
            for a in range(n):
                ici(outs, scr, a, j, chip, c, True).wait_send()
                if halved[a]:
                    d2d(outs, scr, a, j, chip, c).wait_send()

    return _Exchange(list(slabs), [jax.ShapeDtypeStruct(s.shape, s.dtype) for s in slabs],
                     {a: a for a in range(n)}, [pltpu.SemaphoreType.DMA((3 * n,))] * 4, start, finish)


def _ex_to_sibling(grads):
    n = len(grads)

    def copy(ins, outs, scr, a):
        x, y, c, _ = _place()
        hr = grads[a].shape[1] // 2
        return _remote(ins[a].at[:, pl.ds((1 - c) * hr, hr), :], outs[a], scr[0].at[a], scr[1].at[a], (x, y, 1 - c))

    def start(ins, outs, scr):
        for a in range(n):
            copy(ins, outs, scr, a).start()

    def finish(ins, outs, scr):
        for a in range(n):
            copy(ins, outs, scr, a).wait()

    out_shape = [jax.ShapeDtypeStruct((g.shape[0], g.shape[1] // 2, g.shape[2]), g.dtype) for g in grads]
    return _Exchange(list(grads), out_shape, {}, [pltpu.SemaphoreType.DMA((n,))] * 2, start, finish)


def _ex_to_owner(parts, part=(0, 1), landing=None):
    n = len(parts)

    def copy(ins, outs, scr, a, j, chip):
        _, _, c, _ = _place()
        px, py = chip
        pr = parts[a].shape[1] // part[1]
        rows = pl.ds(part[0] * pr, pr)
        return _remote(ins[a].at[2 * px + py, rows], outs[a].at[j, rows], scr[0].at[3 * a + j],
                       scr[1].at[3 * a + j], (px, py, c))

    def start(ins, outs, scr):
        for j, chip in enumerate(_place()[3]):
            for a in range(n):
                copy(ins, outs, scr, a, j, chip).start()

    def finish(ins, outs, scr):
        for j, chip in enumerate(_place()[3]):
            for a in range(n):
                copy(ins, outs, scr, a, j, chip).wait()

    out_shape = [jax.ShapeDtypeStruct((3,) + p.shape[1:], p.dtype) for p in parts]
    operands, aliases = list(parts), {}
    if landing is not None:
        operands, aliases = operands + list(landing), {n + a: a for a in range(n)}
    return _Exchange(operands, out_shape, aliases, [pltpu.SemaphoreType.DMA((3 * n,))] * 2, start, finish)


def _ex_share_halves(bufs):
    n = len(bufs)

    def copy(outs, scr, a, cc):
        x, y, c, _ = _place()
        hr = bufs[a].shape[0] // 2
        piece = outs[a].at[pl.ds(cc * hr, hr), :]
        return _remote(piece, piece, scr[0].at[a], scr[1].at[a], (x, y, 1 - c))

    def start(ins, outs, scr):
        c = _place()[2]
        for a in range(n):
            copy(outs, scr, a, c).start()

    def finish(ins, outs, scr):
        c = _place()[2]
        for a in range(n):
            copy(outs, scr, a, c).wait_send()
            copy(outs, scr, a, 1 - c).wait_recv()

    return _Exchange(list(bufs), [jax.ShapeDtypeStruct(b.shape, b.dtype) for b in bufs], {a: a for a in range(n)},
                     [pltpu.SemaphoreType.DMA((n,))] * 2, start, finish)


def _ex_gather_small(arrs):
    n = len(arrs)

    def peer_of(m):
        x, y, c, _ = _place()
        return (1 - x if m & 4 else x, 1 - y if m & 2 else y, 1 - c if m & 1 else c)

    def start(ins, outs, scr):
        x, y, c, _ = _place()
        for m in range(1, N_DEV):
            for a in range(n):
                k = (N_DEV - 1) * a + m - 1
                _remote(ins[a], outs[a].at[4 * x + 2 * y + c], scr[0].at[k], scr[1].at[k], peer_of(m)).start()

    def finish(ins, outs, scr):
        for m in range(1, N_DEV):
            px, py, pc = peer_of(m)
            for a in range(n):
                k = (N_DEV - 1) * a + m - 1
                slot = outs[a].at[4 * px + 2 * py + pc]
                cp = _remote(ins[a], slot, scr[0].at[k], scr[1].at[k], (px, py, pc))
                cp.wait_send()
                cp.wait_recv()

    slots = [jnp.zeros((N_DEV,) + a.shape, a.dtype) for a in arrs]
    out_shape = [jax.ShapeDtypeStruct(s.shape, s.dtype) for s in slots]
    return _Exchange(list(arrs) + slots, out_shape, {n + a: a for a in range(n)},
                     [pltpu.SemaphoreType.DMA(((N_DEV - 1) * n,))] * 2, start, finish)


def _div_tile(n, want):
    best = None
    for t in range(8, min(n, want) + 1, 8):
        if n % t == 0:
            best = t
    assert best is not None, n
    return best


def _cast_into_slab(name, w, place, dtype):
    r, cc = w.shape
    tr = r if r * cc <= 128 * 1024 else _div_tile(r, 256)

    def body(s_ref, w_ref, o_ref):
        o_ref[...] = w_ref[...].astype(o_ref.dtype)

    return _pallas(
        body, name=name,
        grid_spec=pltpu.PrefetchScalarGridSpec(
            num_scalar_prefetch=1, grid=(r // tr,),
            in_specs=[pl.BlockSpec((tr, cc), lambda i, s: (i, 0))],
            out_specs=pl.BlockSpec((None, tr, cc), lambda i, s: (s[0], i, 0))),
        out_shape=jax.ShapeDtypeStruct((N_CHIPS, r, cc), dtype), compiler_params=_cp("parallel"),
    )(place, w)


def _add_half(name, g, rcv, place):
    nq, r, cc = g.shape
    hr = r // 2

    def body(s_ref, g_ref, r_ref, o_ref):
        o_ref[...] = (g_ref[...] + r_ref[...]).astype(o_ref.dtype)

    spec = pl.BlockSpec((None, hr, cc), lambda i, s: (i, 0, 0))
    return _pallas(
        body, name=name,
        grid_spec=pltpu.PrefetchScalarGridSpec(
            num_scalar_prefetch=1, grid=(nq,),
            in_specs=[pl.BlockSpec((None, hr, cc), lambda i, s: (i, s[1], 0)), spec], out_specs=spec),
        out_shape=jax.ShapeDtypeStruct((nq, hr, cc), BF16), compiler_params=_cp("parallel"),
    )(place, g, rcv)


def _sum_owner(name, part, rcv, place):
    _, hr, cc = part.shape
    tr = _div_tile(hr, 128)
    nb = hr // tr

    def body(s_ref, p_ref, r_ref, o_ref):
        o_ref[...] = ((p_ref[...].astype(F32) + r_ref[0].astype(F32)) + r_ref[1].astype(F32)) + r_ref[2].astype(F32)

    return _pallas(
        body, name=name,
        grid_spec=pltpu.PrefetchScalarGridSpec(
            num_scalar_prefetch=1, grid=(nb,),
            in_specs=[pl.BlockSpec((None, tr, cc), lambda i, s: (s[0], i, 0)),
                      pl.BlockSpec((3, tr, cc), lambda i, s: (0, i, 0))],
            out_specs=pl.BlockSpec((tr, cc), lambda i, s: (s[1] * nb + i, 0))),
        out_shape=jax.ShapeDtypeStruct((2 * hr, cc), F32), compiler_params=_cp("parallel"),
    )(place, part, rcv)


def _sum_small(gathered, local, place):
    n = len(gathered)

    def body(s_ref, *refs):
        g_refs, l_refs, o_refs = refs[:n], refs[n:2 * n], refs[2 * n:]
        me = s_ref[2]
        for g_ref, l_ref, o_ref in zip(g_refs, l_refs, o_refs):
            acc = None
            for d in range(N_DEV):
                term = jnp.where(me == d, l_ref[...], g_ref[d])
                acc = term if acc is None else acc + term
            o_ref[...] = acc

    def whole(shape):
        return pl.BlockSpec(shape, lambda i, s, nd=len(shape): (0,) * nd)

    return _pallas(
        body, name="sum_small",
        grid_spec=pltpu.PrefetchScalarGridSpec(
            num_scalar_prefetch=1, grid=(1,),
            in_specs=[whole(g.shape) for g in gathered] + [whole(a.shape) for a in local],
            out_specs=tuple(whole(a.shape) for a in local)),
        out_shape=tuple(jax.ShapeDtypeStruct(a.shape, a.dtype) for a in local), compiler_params=_cp("arbitrary"),
    )(place, *gathered, *local)


def _adamw(name, w, g, m, v):
    r, cc = w.shape
    tr = r if r * cc <= 128 * 1024 else _div_tile(r, 256)

    def body(w_ref, g_ref, m_ref, v_ref, d_ref, mo_ref, vo_ref):
        gv = g_ref[...]
        mn = ADAM_B1 * m_ref[...] + (1.0 - ADAM_B1) * gv
        vn = ADAM_B2 * v_ref[...] + (1.0 - ADAM_B2) * (gv * gv)
        m_hat = mn / (1.0 - ADAM_B1 ** ADAM_STEP)
        v_hat = vn / (1.0 - ADAM_B2 ** ADAM_STEP)
        d_ref[...] = -ADAM_LR * (m_hat / (jnp.sqrt(v_hat) + ADAM_EPS) + ADAM_WD * w_ref[...])
        mo_ref[...] = mn
        vo_ref[...] = vn

    spec = pl.BlockSpec((tr, cc), lambda i: (i, 0))
    sd = jax.ShapeDtypeStruct((r, cc), F32)
    return _pallas(
        body, name=name, grid=(r // tr,), in_specs=[spec] * 4, out_specs=(spec,) * 3, out_shape=(sd,) * 3,
        compiler_params=_cp("parallel"),
    )(w, g, m, v)


_BIG = ("w_in", "w_up", "w_branch", "w_mem_kv", "w_out", "w_down")
_BIG_SHARD_SHAPE = {"w_in": (1024, 1664), "w_up": (1024, 1408), "w_branch": (1536, 256),
                    "w_mem_kv": (256, 1024), "w_out": (256, 1024), "w_down": (704, 1024)}
_SMALL_SHAPE = {"norm1_g": (1, D_MODEL), "ln_v_g": (1, GM_WIDTH), "ln_v_b": (1, GM_WIDTH),
                "w_spatial": (GM_GROUPS * GM_CHUNK, GM_CHUNK), "b_spatial": (GM_GROUPS, GM_CHUNK),
                "lb_logits": (2, HG_HEADS * HG_DIM), "hgrn_norm_g": (1, HG_DIM), "mem_norm_g": (1, D_MODEL),
                "norm2_g": (1, D_MODEL), "conv_w": (3, D_FF), "conv_b": (1, D_FF), "final_g": (1, D_MODEL)}
_SMALL_EARLY = tuple(n for n in _SMALL_SHAPE if n != "norm1_g")
_PARAM_ORDER = ("norm1_g", "w_in", "ln_v_g", "ln_v_b", "w_spatial", "b_spatial", "lb_logits", "hgrn_norm_g",
                "mem_norm_g", "w_mem_kv", "w_branch", "w_out", "norm2_g", "w_up", "conv_w", "conv_b", "w_down",
                "final_g")


def _adamw_small(ws, gs, ms, vs):
    n = len(ws)

    def body(*refs):
        w_refs, g_refs, m_refs, v_refs = refs[:n], refs[n:2 * n], refs[2 * n:3 * n], refs[3 * n:4 * n]
        d_refs, mo_refs, vo_refs = refs[4 * n:5 * n], refs[5 * n:6 * n], refs[6 * n:]
        for k in range(n):
            gv = g_refs[k][...]
            mn = ADAM_B1 * m_refs[k][...] + (1.0 - ADAM_B1) * gv
            vn = ADAM_B2 * v_refs[k][...] + (1.0 - ADAM_B2) * (gv * gv)
            m_hat = mn / (1.0 - ADAM_B1 ** ADAM_STEP)
            v_hat = vn / (1.0 - ADAM_B2 ** ADAM_STEP)
            d_refs[k][...] = -ADAM_LR * (m_hat / (jnp.sqrt(v_hat) + ADAM_EPS) + ADAM_WD * w_refs[k][...])
            mo_refs[k][...] = mn
            vo_refs[k][...] = vn

    specs = [pl.BlockSpec(a.shape, lambda i: (0, 0)) for a in ws]
    shapes = tuple(jax.ShapeDtypeStruct(a.shape, F32) for a in ws)
    res = _pallas(
        body, name="adamw_small", grid=(1,), in_specs=specs * 4, out_specs=tuple(specs * 3), out_shape=shapes * 3,
        compiler_params=_cp("arbitrary"),
    )(*ws, *gs, *ms, *vs)
    return res[:n], res[n:2 * n], res[2 * n:]


class _Comm:
    _ROW_SHARDED = ("w_mem_kv", "w_out", "w_down")

    def __init__(self, slabs, place):
        self.slabs, self.place = slabs, place
        self.full, self.raw, self.parts, self.landing, self.bufs, self.done = {}, {}, {}, {}, {}, {}
        ex, deliver = self._gather(["w_in"])
        deliver(_run_exchanges("all_gather_w_in", [ex])[0])

    def w(self, name):
        a = self.full[name]
        if name in self._ROW_SHARDED:
            return a.reshape(-1, a.shape[-1])
        if name == "conv_w":
            return jnp.transpose(a, (1, 0, 2)).reshape(3, 1, D_FF)
        return a

    def grad(self, name, arr):
        self.raw[name] = arr.reshape((N_CHIPS, -1, arr.shape[-1]))
        if name == "w_in":
            ex, deliver = self._to_sibling(["w_in"])
            deliver(_run_exchanges("rs_sibling_w_in", [ex])[0])

    def small_grads(self, arrays):
        self.small_local = list(arrays)

    def carry(self, tag, call):
        plan = self._plan(tag)
        if not plan:
            return call(())
        out, carried = call([ex for ex, _ in plan])
        for (_, deliver), res in zip(plan, carried):
            deliver(res)
        return out

    def finish(self, last_small):
        ex, deliver = self._share(["w_out", "w_branch", "w_mem_kv", "w_in"])
        shared, small = _run_exchanges("share_and_gather_last", [ex, _ex_gather_small(last_small)])
        deliver(shared)
        return self.done, self.small_local + list(last_small), self.small_everyone + small

    def _plan(self, tag):
        if tag == "in_proj":
            return [self._gather(["w_branch", "w_out", "w_mem_kv", "w_down", "conv_w"])]
        if tag == "hgrn_fwd":
            return [self._gather(["w_up"])]
        if tag == "d_h2":
            return [self._to_sibling(["w_down", "w_up"])]
        if tag == "hgrn_bwd":
            return [self._to_owner(["w_down", "w_up"]), self._to_sibling(["w_out", "w_branch", "w_mem_kv"])]
        if tag == "g_w_in":
            def keep(res):
                self.small_everyone = res

            return [self._to_owner(["w_out", "w_branch", "w_mem_kv"]), self._share(["w_down", "w_up"]),
                    (_ex_gather_small(self.small_local), keep)]
        if tag == "d_h":
            return [self._to_owner(["w_in"])]
        return []

    def _gather(self, names, part=(0, 1)):
        def deliver(res):
            self.slabs.update(zip(names, res))
            self.full.update(zip(names, res))

        return _ex_all_gather([self.slabs[n] for n in names], [n != "conv_w" for n in names], part), deliver

    def _to_sibling(self, names):
        def deliver(res):
            for n, r in zip(names, res):
                self.parts[n] = _add_half("rs_add_" + n, self.raw[n], r, self.place)

        return _ex_to_sibling([self.raw[n] for n in names]), deliver

    def _to_owner(self, names, part=(0, 1)):
        def deliver(res):
            for n, r in zip(names, res):
                if part[0] + 1 < part[1]:
                    self.landing[n] = r
                else:
                    self.bufs[n] = _sum_owner("rs_sum_" + n, self.parts[n], r, self.place)

        landing = [self.landing[n] for n in names] if part[0] else None
        return _ex_to_owner([self.parts[n] for n in names], part, landing), deliver

    def _share(self, names):
        return _ex_share_halves([self.bufs[n] for n in names]), lambda res: self.done.update(zip(names, res))


def kernel(x, mem, norm1_g, w_in, ln_v_g, ln_v_b, w_spatial, b_spatial, lb_logits, hgrn_norm_g, mem_norm_g, w_mem_kv, w_branch, w_out, norm2_g, w_up, conv_w, conv_b, w_down, final_g, loss_target, m_norm1_g, m_w_in, m_ln_v_g, m_ln_v_b, m_w_spatial, m_b_spatial, m_lb_logits, m_hgrn_norm_g, m_mem_norm_g, m_w_mem_kv, m_w_branch, m_w_out, m_norm2_g, m_w_up, m_conv_w, m_conv_b, m_w_down, m_final_g, v_norm1_g, v_w_in, v_ln_v_g, v_ln_v_b, v_w_spatial, v_b_spatial, v_lb_logits, v_hgrn_norm_g, v_mem_norm_g, v_w_mem_kv, v_w_branch, v_w_out, v_norm2_g, v_w_up, v_conv_w, v_conv_b, v_w_down, v_final_g):
    w = dict(norm1_g=norm1_g, w_in=w_in, ln_v_g=ln_v_g, ln_v_b=ln_v_b, w_spatial=w_spatial, b_spatial=b_spatial,
             lb_logits=lb_logits, hgrn_norm_g=hgrn_norm_g, mem_norm_g=mem_norm_g, w_mem_kv=w_mem_kv,
             w_branch=w_branch, w_out=w_out, norm2_g=norm2_g, w_up=w_up, conv_w=conv_w, conv_b=conv_b,
             w_down=w_down, final_g=final_g)
    mom = dict(norm1_g=m_norm1_g, w_in=m_w_in, ln_v_g=m_ln_v_g, ln_v_b=m_ln_v_b, w_spatial=m_w_spatial,
               b_spatial=m_b_spatial, lb_logits=m_lb_logits, hgrn_norm_g=m_hgrn_norm_g, mem_norm_g=m_mem_norm_g,
               w_mem_kv=m_w_mem_kv, w_branch=m_w_branch, w_out=m_w_out, norm2_g=m_norm2_g, w_up=m_w_up,
               conv_w=m_conv_w, conv_b=m_conv_b, w_down=m_w_down, final_g=m_final_g)
    var = dict(norm1_g=v_norm1_g, w_in=v_w_in, ln_v_g=v_ln_v_g, ln_v_b=v_ln_v_b, w_spatial=v_w_spatial,
               b_spatial=v_b_spatial, lb_logits=v_lb_logits, hgrn_norm_g=v_hgrn_norm_g, mem_norm_g=v_mem_norm_g,
               w_mem_kv=v_w_mem_kv, w_branch=v_w_branch, w_out=v_w_out, norm2_g=v_norm2_g, w_up=v_w_up,
               conv_w=v_conv_w, conv_b=v_conv_b, w_down=v_w_down, final_g=v_final_g)
    B, S, D = x.shape
    T = B * S
    ci = lax.axis_index("c")
    q = 2 * lax.axis_index("x") + lax.axis_index("y")
    place = jnp.stack([q, ci, 2 * q + ci]).astype(jnp.int32)

    slabs = {n: _cast_into_slab("slab_" + n, w[n].reshape(_BIG_SHARD_SHAPE[n]), place, BF16) for n in _BIG}
    slabs["conv_w"] = _cast_into_slab("slab_conv_w", conv_w[0], place, F32)
    comm = _Comm(slabs, place)
    p = dict(
        norm1_g=norm1_g, ln_v_g=ln_v_g, ln_v_b=ln_v_b, w_spatial=w_spatial[0],
        b_spatial=b_spatial.reshape(GM_GROUPS, GM_CHUNK, 1), lb_logits=lb_logits, hgrn_norm_g=hgrn_norm_g,
        mem_norm_g=mem_norm_g, norm2_g=norm2_g, conv_b=conv_b, final_g=final_g.reshape(1, D))

    loss, grad_x, g = _local_step(x.reshape(T, D), mem.reshape(B * MEM_LEN, D), loss_target.reshape(T, D), p, comm,
                                  B, S)

    shard_grads, local_small, everyone = comm.finish([g["norm1_g"]])
    summed = _sum_small(everyone, local_small, place)
    small_names = list(_SMALL_EARLY) + ["norm1_g"]
    total = dict(zip(_SMALL_EARLY, summed))
    loss_total, total["norm1_g"] = summed[len(_SMALL_EARLY)][0, 0], summed[-1]

    grads, delta, new_m, new_v = {}, {}, {}, {}
    for n in _BIG:
        shp = _BIG_SHARD_SHAPE[n]
        grads[n] = shard_grads[n]
        delta[n], new_m[n], new_v[n] = _adamw("adamw_" + n, w[n].reshape(shp), shard_grads[n],
                                              mom[n].reshape(shp), var[n].reshape(shp))
    cw_shard = D_FF // N_CHIPS
    total["conv_w"] = lax.dynamic_slice(total["conv_w"], (0, q * cw_shard), (3, cw_shard))

    def flat2d(d, n):
        return d[n].reshape(total[n].shape)

    upd = _adamw_small([flat2d(w, n) for n in small_names], [total[n] for n in small_names],
                       [flat2d(mom, n) for n in small_names], [flat2d(var, n) for n in small_names])
    for k, n in enumerate(small_names):
        grads[n], delta[n], new_m[n], new_v[n] = total[n], upd[0][k], upd[1][k], upd[2][k]

    def shaped(d):
        return [d[n].reshape(w[n].shape) for n in _PARAM_ORDER]

    return (loss_total, grad_x.reshape(B, S, D), *shaped(grads), *shaped(delta), *shaped(new_m), *shaped(new_v))
```

```python
import functools
import math

import jax
import jax.numpy as jnp
from jax import lax
from jax.experimental import pallas as pl
from jax.experimental.pallas import tpu as pltpu

F32 = jnp.float32
BF16 = jnp.bfloat16
EPS = 1e-6

D_MODEL = 1024
MEM_LEN = 256
GM_WIDTH = 512
GM_CHUNK = 128
GM_GROUPS = 4
HG_HEADS = 4
HG_DIM = 128
HG_CHUNK = 64
XA_HEADS = 4
XA_DIM = 128
BR_WIDTH = 512
D_FF = 2816
IN_WIDTH = 6656
N_CHIPS = 4
N_DEV = 8

ADAM_LR = 0.001
ADAM_B1 = 0.9
ADAM_B2 = 0.999
ADAM_EPS = 1e-08
ADAM_WD = 0.01
ADAM_STEP = 10

COL_ZU, COL_ZV, COL_HQ, COL_HF, COL_HI, COL_HG, COL_XQ = 0, 1, 2, 3, 4, 5, 6
COL_GATE0 = 3584

VMEM_LIMIT_BYTES = 48 * 1024 * 1024
MESH_ID = pl.DeviceIdType.MESH


def _cp(*sem):
    return pltpu.CompilerParams(dimension_semantics=sem, vmem_limit_bytes=VMEM_LIMIT_BYTES)


def _pallas(body, *, out_shape, **kw):
    def pin(s):
        return pltpu.HBM(s.shape, s.dtype) if isinstance(s, jax.ShapeDtypeStruct) else s

    out_shape = tuple(pin(s) for s in out_shape) if isinstance(out_shape, (tuple, list)) else pin(out_shape)
    call = pl.pallas_call(body, out_shape=out_shape, **kw)

    def run(*operands):
        return call(*[pltpu.with_memory_space_constraint(o, pltpu.HBM) if jnp.issubdtype(o.dtype, jnp.floating)
                      else o for o in operands])

    return run


def _dot(a, b):
    return lax.dot_general(a.astype(BF16), b.astype(BF16), (((1,), (0,)), ((), ())), preferred_element_type=F32)


def _dot_nt(a, b):
    return lax.dot_general(a.astype(BF16), b.astype(BF16), (((1,), (1,)), ((), ())), preferred_element_type=F32)


def _dot_tn(a, b):
    return lax.dot_general(a.astype(BF16), b.astype(BF16), (((0,), (0,)), ((), ())), preferred_element_type=F32)


def _dot_01(mask01, x):
    hi = x.astype(BF16)
    r1 = x - hi.astype(F32)
    mid = r1.astype(BF16)
    lo = (r1 - mid.astype(F32)).astype(BF16)
    m = mask01.astype(BF16)
    dn = (((1,), (0,)), ((), ()))
    return (lax.dot_general(m, hi, dn, preferred_element_type=F32)
            + lax.dot_general(m, mid, dn, preferred_element_type=F32)
            + lax.dot_general(m, lo, dn, preferred_element_type=F32))


def _sigmoid(z):
    return 1.0 / (1.0 + jnp.exp(-z))


_GELU_C = math.sqrt(2.0 / math.pi)


def _gelu_and_grad(z):
    inner = _GELU_C * (z + 0.044715 * z * z * z)
    t = jnp.tanh(inner)
    val = 0.5 * z * (1.0 + t)
    grad = 0.5 * (1.0 + t) + 0.5 * z * (1.0 - t * t) * _GELU_C * (1.0 + 3.0 * 0.044715 * z * z)
    return val, grad


def _row_tile(n, want=512):
    t = min(want, n)
    assert n % t == 0
    return t


def _pcall(body, operands, *, name, grid, in_specs, out_specs, out_shape, scratch_shapes=(), semantics, riders=(),
           own_aliases=None):
    single = not isinstance(out_shape, (tuple, list))
    out_specs = (out_specs,) if single else tuple(out_specs)
    out_shape = (out_shape,) if single else tuple(out_shape)
    own_aliases = dict(own_aliases or {})
    if not riders:
        res = _pallas(body, name=name, grid=grid, in_specs=list(in_specs), out_specs=out_specs,
                      out_shape=out_shape, scratch_shapes=list(scratch_shapes), input_output_aliases=own_aliases,
                      compiler_params=_cp(*semantics))(*operands)
        return (res[0] if single else res), []
    n_in, n_out, n_scr = len(in_specs), len(out_shape), len(scratch_shapes)
    ex_in = [len(ex.operands) for ex in riders]
    ex_out = [len(ex.out_shape) for ex in riders]
    ex_scr = [len(ex.scratch) for ex in riders]
    tot_in, tot_out = n_in + sum(ex_in), n_out + sum(ex_out)

    def wrapped(*refs):
        ins, outs, scr = refs[:tot_in], refs[tot_in:tot_in + tot_out], refs[tot_in + tot_out:]
        ids = [pl.program_id(d) for d in range(len(grid))]
        first = functools.reduce(lambda p, t: p & t, [i == 0 for i in ids])
        last = functools.reduce(lambda p, t: p & t, [i == n - 1 for i, n in zip(ids, grid)])
        parts, oi, oo, os_ = [], n_in, n_out, n_scr
        for k in range(len(riders)):
            parts.append((ins[oi:oi + ex_in[k]], outs[oo:oo + ex_out[k]], scr[os_:os_ + ex_scr[k]]))
            oi, oo, os_ = oi + ex_in[k], oo + ex_out[k], os_ + ex_scr[k]

        @pl.when(first)
        def _():
            for ex, part in zip(riders, parts):
                ex.start(*part)

        body(*ins[:n_in], *outs[:n_out], *scr[:n_scr])

        @pl.when(last)
        def _():
            for ex, part in zip(riders, parts):
                ex.finish(*part)

    aliases, oi, oo = own_aliases, n_in, n_out
    all_ops, all_shapes, all_scr = list(operands), list(out_shape), list(scratch_shapes)
    for k, ex in enumerate(riders):
        aliases.update({oi + a: oo + b for a, b in ex.aliases.items()})
        oi, oo = oi + ex_in[k], oo + ex_out[k]
        all_ops += list(ex.operands)
        all_shapes += [pltpu.HBM(s.shape, s.dtype) for s in ex.out_shape]
        all_scr += list(ex.scratch)
    res = _pallas(
        wrapped, name=name, grid=grid, in_specs=list(in_specs) + [HBM_SPEC] * sum(ex_in),
        out_specs=out_specs + (HBM_SPEC,) * sum(ex_out), out_shape=tuple(all_shapes), scratch_shapes=all_scr,
        input_output_aliases=aliases, compiler_params=_cp(*(["arbitrary"] * len(grid))))(*all_ops)
    own = res[0] if single else tuple(res[:n_out])
    carried, oo = [], n_out
    for k in range(len(riders)):
        carried.append(list(res[oo:oo + ex_out[k]]))
        oo += ex_out[k]
    return own, carried


def _carried(out, carried, riders):
    return (out, carried) if riders else out


def _matmul(name, operands, *, grid, in_specs, o_spec, out_shape, out_dtype, dims, has_res=False, riders=()):
    nk = grid[2]
    assert nk == 1 or (out_dtype == F32 and not has_res)

    def body(*refs):
        if has_res:
            a_ref, b_ref, r_ref, o_ref = refs
        else:
            a_ref, b_ref, o_ref = refs
            r_ref = None
        part = lax.dot_general(a_ref[...].astype(BF16), b_ref[...].astype(BF16), (dims, ((), ())),
                               preferred_element_type=F32)
        if nk == 1:
            if r_ref is not None:
                part = part + r_ref[...]
            o_ref[...] = part.astype(o_ref.dtype)
        else:
            k = pl.program_id(2)

            @pl.when(k == 0)
            def _():
                o_ref[...] = part

            @pl.when(k > 0)
            def _():
                o_ref[...] += part

    out, carried = _pcall(body, operands, name=name, grid=grid, in_specs=in_specs, out_specs=o_spec,
                          out_shape=jax.ShapeDtypeStruct(out_shape, out_dtype),
                          semantics=("parallel", "parallel", "arbitrary"), riders=riders)
    return (out, carried) if riders else out


NN = ((1,), (0,))
NT = ((1,), (1,))
TN = ((0,), (0,))
_TN_TOKENS = 4096


def _mm_cs(name, a, w, out_dtype, riders=()):
    M, K = a.shape
    nq, _, wd = w.shape
    tm = _row_tile(M)
    return _matmul(name, (a, w), grid=(nq, M // tm, 1),
                   in_specs=[pl.BlockSpec((tm, K), lambda j, i, k: (i, 0)),
                             pl.BlockSpec((None, K, wd), lambda j, i, k: (j, 0, 0))],
                   o_spec=pl.BlockSpec((tm, wd), lambda j, i, k: (i, j)),
                   out_shape=(M, nq * wd), out_dtype=out_dtype, dims=NN, riders=riders)


def _mm_rs(name, a, w, out_dtype, res=None):
    M, K = a.shape
    N = w.shape[1]
    tm = _row_tile(M)
    tn = N
    ops = (a, w) if res is None else (a, w, res)
    in_specs = [pl.BlockSpec((tm, K), lambda i, j, k: (i, 0)),
                pl.BlockSpec((K, tn), lambda i, j, k: (0, j))]
    if res is not None:
        in_specs.append(pl.BlockSpec((tm, tn), lambda i, j, k: (i, j)))
    return _matmul(name, ops, grid=(M // tm, N // tn, 1), in_specs=in_specs,
                   o_spec=pl.BlockSpec((tm, tn), lambda i, j, k: (i, j)),
                   out_shape=(M, N), out_dtype=out_dtype, dims=NN, has_res=res is not None)


def _mm_nt_rs(name, g, w, out_dtype, riders=()):
    M, N = g.shape
    K = w.shape[0]
    to = K
    tm = _row_tile(M)
    return _matmul(name, (g, w), grid=(M // tm, K // to, 1),
                   in_specs=[pl.BlockSpec((tm, N), lambda i, j, k: (i, 0)),
                             pl.BlockSpec((to, N), lambda i, j, k: (j, 0))],
                   o_spec=pl.BlockSpec((tm, to), lambda i, j, k: (i, j)),
                   out_shape=(M, K), out_dtype=out_dtype, dims=NT, riders=riders)


def _mm_nt_cs(name, g, w, out_dtype, riders=(), stacked=False):
    M = g.shape[-2]
    nq, K, wd = w.shape
    tm = _row_tile(M, 256)

    def body(g_ref, w_ref, o_ref):
        acc = None
        for q in range(nq):
            gq = g_ref[q // 2, :, (q % 2) * wd:(q % 2 + 1) * wd] if stacked else g_ref[:, q * wd:(q + 1) * wd]
            part = _dot_nt(gq, w_ref[q])
            acc = part if acc is None else acc + part
        o_ref[...] = acc.astype(o_ref.dtype)

    g_spec = (pl.BlockSpec((2, tm, 2 * wd), lambda i: (0, i, 0)) if stacked
              else pl.BlockSpec((tm, nq * wd), lambda i: (i, 0)))
    out, carried = _pcall(
        body, (g, w), name=name, grid=(M // tm,),
        in_specs=[g_spec, pl.BlockSpec((nq, K, wd), lambda i: (0, 0, 0))],
        out_specs=pl.BlockSpec((tm, K), lambda i: (i, 0)),
        out_shape=jax.ShapeDtypeStruct((M, K), out_dtype), semantics=("parallel",), riders=riders)
    return (out, carried) if riders else out


def _mm_tn_rs(name, a, g, to, tn=512):
    T, M = a.shape
    N = g.shape[1]
    tt = _row_tile(T, _TN_TOKENS)
    tn = min(tn, N)
    return _matmul(name, (a, g), grid=(M // to, N // tn, T // tt),
                   in_specs=[pl.BlockSpec((tt, to), lambda i, j, k: (k, i)),
                             pl.BlockSpec((tt, tn), lambda i, j, k: (k, j))],
                   o_spec=pl.BlockSpec((to, tn), lambda i, j, k: (i, j)),
                   out_shape=(M, N), out_dtype=F32, dims=TN)


def _mm_tn_cs(name, a, g, nq, to, riders=(), stacked=False):
    T, M = a.shape
    wd = g.shape[-1] * (2 if stacked else 1) // nq
    tt = _row_tile(T, _TN_TOKENS)
    g_spec = (pl.BlockSpec((None, tt, wd), lambda i, j, k: (j // 2, k, j % 2)) if stacked
              else pl.BlockSpec((tt, wd), lambda i, j, k: (k, j)))
    return _matmul(name, (a, g), grid=(M // to, nq, T // tt),
                   in_specs=[pl.BlockSpec((tt, to), lambda i, j, k: (k, i)), g_spec],
                   o_spec=pl.BlockSpec((None, to, wd), lambda i, j, k: (j, i, 0)),
                   out_shape=(nq, M, wd), out_dtype=F32, dims=TN, riders=riders)


def _slab_segments(widths, nq, wd):
    segs, slab, room = [[] for _ in range(nq)], 0, wd
    for p, w in enumerate(widths):
        start = 0
        while start < w:
            take = min(room, w - start)
            segs[slab].append((p, start, take))
            start, room = start + take, room - take
            if room == 0:
                slab, room = slab + 1, wd
    return segs


def _mm_nt_pieces(name, pieces, w, out_dtype, riders=()):
    M = pieces[0].shape[0]
    nq, K, wd = w.shape
    tm = _row_tile(M, 256)
    segs = _slab_segments([p.shape[1] for p in pieces], nq, wd)
    flat = [(q, p, s, wdt) for q in range(nq) for (p, s, wdt) in segs[q]]

    def body(*refs):
        seg_refs, w_ref, o_ref = refs[:len(flat)], refs[len(flat)], refs[len(flat) + 1]
        acc, off = None, [0] * nq
        for ref, (q, _, _, wdt) in zip(seg_refs, flat):
            part = _dot_nt(ref[...], w_ref[q, :, off[q]:off[q] + wdt])
            off[q] += wdt
            acc = part if acc is None else acc + part
        o_ref[...] = acc.astype(o_ref.dtype)

    in_specs = [pl.BlockSpec((pl.Element(tm), pl.Element(wdt)), functools.partial(lambda i, s: (i * tm, s), s=s))
                for (_, _, s, wdt) in flat]
    return _carried(*_pcall(
        body, tuple(pieces[p] for (_, p, _, _) in flat) + (w,), name=name, grid=(M // tm,),
        in_specs=in_specs + [pl.BlockSpec((nq, K, wd), lambda i: (0, 0, 0))],
        out_specs=pl.BlockSpec((tm, K), lambda i: (i, 0)),
        out_shape=jax.ShapeDtypeStruct((M, K), out_dtype), semantics=("parallel",), riders=riders), riders)


def _mm_tn_pieces(name, a, pieces, nq, wd, to, riders=()):
    T, M = a.shape
    segs = _slab_segments([p.shape[1] for p in pieces], nq, wd)
    out, carried = None, []
    for q in range(nq):
        runs = segs[q]

        def body(*refs, runs=runs):
            a_ref, seg_refs, o_ref = refs[0], refs[1:1 + len(runs)], refs[-1]
            av, off = a_ref[...], 0
            for ref, (_, _, wdt) in zip(seg_refs, runs):
                o_ref[:, off:off + wdt] = _dot_tn(av, ref[...])
                off += wdt

        prev = () if out is None else (out,)
        in_specs = [pl.BlockSpec((T, to), lambda i: (0, i))]
        in_specs += [pl.BlockSpec((pl.Element(T), pl.Element(wdt)), functools.partial(lambda i, s: (0, s), s=s))
                     for (_, s, wdt) in runs]
        in_specs += [HBM_SPEC] * len(prev)
        last = q == nq - 1
        out, carried = _pcall(
            body, (a,) + tuple(pieces[p] for (p, _, _) in runs) + prev, name="%s_%d" % (name, q), grid=(M // to,),
            in_specs=in_specs, out_specs=pl.BlockSpec((None, to, wd), functools.partial(lambda i, q: (q, i, 0), q=q)),
            out_shape=jax.ShapeDtypeStruct((nq, M, wd), F32), semantics=("parallel",),
            riders=riders if last else (), own_aliases={1 + len(runs): 0} if prev else None)
    return _carried(out, carried, riders)


def _rms_fwd(name, x, g):
    T, D = x.shape
    tm = _row_tile(T)

    def body(x_ref, g_ref, o_ref):
        xv = x_ref[...]
        r = lax.rsqrt(jnp.mean(xv * xv, axis=-1, keepdims=True) + EPS)
        o_ref[...] = (xv * r * g_ref[...]).astype(o_ref.dtype)

    return _pallas(
        body, name=name, grid=(T // tm,),
        in_specs=[pl.BlockSpec((tm, D), lambda i: (i, 0)), pl.BlockSpec((1, D), lambda i: (0, 0))],
        out_specs=pl.BlockSpec((tm, D), lambda i: (i, 0)),
        out_shape=jax.ShapeDtypeStruct((T, D), BF16), compiler_params=_cp("parallel"),
    )(x, g)


def _rms_bwd(name, x, g, dh, dres, riders=()):
    T, D = x.shape
    tm = _row_tile(T)
    has_res = dres is not None

    def body(*refs):
        if has_res:
            x_ref, g_ref, dh_ref, dr_ref, dx_ref, dg_ref = refs
        else:
            x_ref, g_ref, dh_ref, dx_ref, dg_ref = refs

        @pl.when(pl.program_id(0) == 0)
        def _():
            dg_ref[...] = jnp.zeros_like(dg_ref)

        xv = x_ref[...]
        r = lax.rsqrt(jnp.mean(xv * xv, axis=-1, keepdims=True) + EPS)
        n = xv * r
        dhv = dh_ref[...]
        dg_ref[...] += jnp.sum(dhv * n, axis=0, keepdims=True)
        dn = dhv * g_ref[...]
        dx = r * (dn - n * jnp.mean(dn * n, axis=-1, keepdims=True))
        if has_res:
            dx = dx + dr_ref[...]
        dx_ref[...] = dx

    row = pl.BlockSpec((tm, D), lambda i: (i, 0))
    vec = pl.BlockSpec((1, D), lambda i: (0, 0))
    ops = (x, g, dh, dres) if has_res else (x, g, dh)
    out, carried = _pcall(
        body, ops, name=name, grid=(T // tm,),
        in_specs=[row, vec, row] + ([row] if has_res else []),
        out_specs=(row, vec),
        out_shape=(jax.ShapeDtypeStruct((T, D), F32), jax.ShapeDtypeStruct((1, D), F32)),
        semantics=("arbitrary",), riders=riders)
    return (out, carried) if riders else out


def _loss_head(x2, tgt, g):
    T, D = x2.shape
    tm = _row_tile(T)

    def body(x_ref, t_ref, g_ref, dx_ref, dg_ref, loss_ref):
        @pl.when(pl.program_id(0) == 0)
        def _():
            dg_ref[...] = jnp.zeros_like(dg_ref)
            loss_ref[...] = jnp.zeros_like(loss_ref)

        xv = x_ref[...]
        gv = g_ref[...]
        r = lax.rsqrt(jnp.mean(xv * xv, axis=-1, keepdims=True) + EPS)
        n = xv * r
        diff = n * gv - t_ref[...]
        loss_ref[...] += 0.5 * jnp.sum(jnp.mean(diff * diff, axis=-1, keepdims=True))
        dy = diff * (1.0 / D)
        dg_ref[...] += jnp.sum(dy * n, axis=0, keepdims=True)
        dn = dy * gv
        dx_ref[...] = r * (dn - n * jnp.mean(dn * n, axis=-1, keepdims=True))

    row = pl.BlockSpec((tm, D), lambda i: (i, 0))
    vec = pl.BlockSpec((1, D), lambda i: (0, 0))
    return _pallas(
        body, name="loss_head", grid=(T // tm,),
        in_specs=[row, row, vec],
        out_specs=(row, vec, pl.BlockSpec((8, 128), lambda i: (0, 0))),
        out_shape=(jax.ShapeDtypeStruct((T, D), F32), jax.ShapeDtypeStruct((1, D), F32),
                   jax.ShapeDtypeStruct((8, 128), F32)),
        compiler_params=_cp("arbitrary"),
    )(x2, tgt, g)


def _gmlp_pieces(zu, zv, lng, lnb, ws_ref, bs_ref):
    u, du = _gelu_and_grad(zu)
    v, dv = _gelu_and_grad(zv)
    mu = jnp.mean(v, axis=-1, keepdims=True)
    vc = v - mu
    rstd = lax.rsqrt(jnp.mean(vc * vc, axis=-1, keepdims=True) + EPS)
    vhat = vc * rstd
    vn = vhat * lng + lnb
    row = lax.broadcasted_iota(jnp.int32, (GM_CHUNK, GM_CHUNK), 0)
    col = lax.broadcasted_iota(jnp.int32, (GM_CHUNK, GM_CHUNK), 1)
    tril = row >= col
    wms, mixed = [], []
    for g in range(GM_GROUPS):
        sl = slice(g * 128, (g + 1) * 128)
        wm = jnp.where(tril, ws_ref[g], 0.0)
        wms.append(wm)
        mixed.append(_dot(wm, vn[:, sl]) + bs_ref[g])
    return u, du, dv, rstd, vhat, vn, wms, mixed, tril


def _gmlp_fwd(proj, lng, lnb, ws, bs_col):
    T = proj.shape[0]
    n = T // GM_CHUNK

    def body(zu_ref, zv_ref, lng_ref, lnb_ref, ws_ref, bs_ref, o_ref):
        u, _, _, _, _, _, _, mixed, _ = _gmlp_pieces(zu_ref[...].astype(F32), zv_ref[...].astype(F32),
                                                     lng_ref[...], lnb_ref[...],
                                                     ws_ref, bs_ref)
        for g in range(GM_GROUPS):
            sl = slice(g * 128, (g + 1) * 128)
            o_ref[:, sl] = (u[:, sl] * mixed[g]).astype(o_ref.dtype)

    vec = pl.BlockSpec((1, GM_WIDTH), lambda i: (0, 0))
    return _pallas(
        body, name="gmlp_fwd", grid=(n,),
        in_specs=[pl.BlockSpec((GM_CHUNK, 512), lambda i: (i, COL_ZU)),
                  pl.BlockSpec((GM_CHUNK, 512), lambda i: (i, COL_ZV)),
                  vec, vec,
                  pl.BlockSpec((GM_GROUPS, 128, 128), lambda i: (0, 0, 0)),
                  pl.BlockSpec((GM_GROUPS, 128, 1), lambda i: (0, 0, 0))],
        out_specs=pl.BlockSpec((GM_CHUNK, 512), lambda i: (i, 0)),
        out_shape=jax.ShapeDtypeStruct((T, GM_WIDTH), BF16), compiler_params=_cp("parallel"),
    )(proj, proj, lng, lnb, ws, bs_col)


def _gmlp_bwd(proj, d_out, lng, lnb, ws, bs_col, riders=()):
    T = proj.shape[0]
    n = T // GM_CHUNK

    def body(zu_ref, zv_ref, do_ref, lng_ref, lnb_ref, ws_ref, bs_ref,
             dz_ref, dws_ref, dbs_ref, dlng_ref, dlnb_ref, dm_acc):
        i = pl.program_id(0)

        @pl.when(i == 0)
        def _():
            dws_ref[...] = jnp.zeros_like(dws_ref)
            dlng_ref[...] = jnp.zeros_like(dlng_ref)
            dlnb_ref[...] = jnp.zeros_like(dlnb_ref)
            dm_acc[...] = jnp.zeros_like(dm_acc)

        lng_v = lng_ref[...]
        u, du, dv, rstd, vhat, vn, wms, mixed, tril = _gmlp_pieces(zu_ref[...].astype(F32), zv_ref[...].astype(F32),
                                                                  lng_v, lnb_ref[...],
                                                                  ws_ref, bs_ref)
        do = do_ref[...]
        dvn_parts = []
        for g in range(GM_GROUPS):
            sl = slice(g * 128, (g + 1) * 128)
            dog = do[:, sl]
            dz_ref[:, sl] = (dog * mixed[g] * du[:, sl]).astype(dz_ref.dtype)
            dmix = dog * u[:, sl]
            dm_acc[:, sl] += dmix
            dws_ref[g] += jnp.where(tril, _dot_nt(dmix, vn[:, sl]), 0.0)
            dvn_parts.append(_dot_tn(wms[g], dmix))
        dvn = jnp.concatenate(dvn_parts, axis=1)
        dlng_ref[...] += jnp.sum(dvn * vhat, axis=0, keepdims=True)
        dlnb_ref[...] += jnp.sum(dvn, axis=0, keepdims=True)
        dvh = dvn * lng_v
        dvv = rstd * (dvh - jnp.mean(dvh, axis=-1, keepdims=True)
                      - vhat * jnp.mean(dvh * vhat, axis=-1, keepdims=True))
        dz_ref[:, GM_WIDTH:] = (dvv * dv).astype(dz_ref.dtype)

        @pl.when(i == n - 1)
        def _():
            for g in range(GM_GROUPS):
                dbs_ref[g] = jnp.sum(dm_acc[:, g * 128:(g + 1) * 128], axis=1, keepdims=True)

    vec = pl.BlockSpec((1, GM_WIDTH), lambda i: (0, 0))
    wsp = pl.BlockSpec((GM_GROUPS, 128, 128), lambda i: (0, 0, 0))
    bsp = pl.BlockSpec((GM_GROUPS, 128, 1), lambda i: (0, 0, 0))
    return _carried(*_pcall(
        body, (proj, proj, d_out, lng, lnb, ws, bs_col), name="gmlp_bwd", grid=(n,),
        in_specs=[pl.BlockSpec((GM_CHUNK, 512), lambda i: (i, COL_ZU)),
                  pl.BlockSpec((GM_CHUNK, 512), lambda i: (i, COL_ZV)),
                  pl.BlockSpec((None, GM_CHUNK, 512), lambda i: (0, i, 0)), vec, vec, wsp, bsp],
        out_specs=(pl.BlockSpec((GM_CHUNK, 2 * GM_WIDTH), lambda i: (i, 0)), wsp, bsp, vec, vec),
        out_shape=(jax.ShapeDtypeStruct((T, 2 * GM_WIDTH), BF16),
                   jax.ShapeDtypeStruct((GM_GROUPS, 128, 128), F32), jax.ShapeDtypeStruct((GM_GROUPS, 128, 1), F32),
                   jax.ShapeDtypeStruct((1, GM_WIDTH), F32), jax.ShapeDtypeStruct((1, GM_WIDTH), F32)),
        scratch_shapes=[pltpu.VMEM((GM_CHUNK, GM_WIDTH), F32)],
        semantics=("arbitrary",), riders=riders), riders)


def _hgrn_lower_bound(lbl):
    return 1.0 / (1.0 + jnp.exp(lbl[1:2, :] - lbl[0:1, :]))


def _hgrn_gates(hq, hf, lb):
    C = HG_CHUNK
    sg = _sigmoid(hf)
    fg = lb + (1.0 - lb) * sg
    sq = _sigmoid(hq)
    row = lax.broadcasted_iota(jnp.int32, (C, C), 0)
    col = lax.broadcasted_iota(jnp.int32, (C, C), 1)
    tril = row >= col
    logf = jnp.log(fg)
    a = _dot_01(tril, logf)
    a_last = jnp.sum(logf, axis=0, keepdims=True)
    first_half = lax.broadcasted_iota(jnp.int32, logf.shape, 0) < (C // 2)
    a_mid = jnp.sum(jnp.where(first_half, logf, 0.0), axis=0, keepdims=True)
    ea, ei, eki, ekl = jnp.exp(a), jnp.exp(a - a_mid), jnp.exp(a_mid - a), jnp.exp(a_last - a)
    k = 1.0 - fg
    q = hq * sq
    qi = (q * ei).astype(BF16).astype(F32)
    ki = (k * eki).astype(BF16).astype(F32)
    return dict(sg=sg, fg=fg, sq=sq, tril=tril, ea=ea, ei=ei, eki=eki, ekl=ekl, e_last=jnp.exp(a_last),
                qe=q * ea, qi=qi, ki=ki, kl=k * ekl)


def _heads(x):
    return [x[:, h * HG_DIM:(h + 1) * HG_DIM] for h in range(HG_HEADS)]


def _hgrn_fwd(proj, lbl, gh, B, S, riders=()):
    C = HG_CHUNK
    NC = S // C
    W = HG_HEADS * HG_DIM

    def body(q_ref, f_ref, i_ref, g_ref, lbl_ref, gh_ref, o_ref, bo_ref, st_ref, state):
        @pl.when(pl.program_id(0) == 0)
        def _():
            state[...] = jnp.zeros_like(state)

        lb = _hgrn_lower_bound(lbl_ref[...])
        ghv = gh_ref[...]
        for b in range(B):
            gt = _hgrn_gates(q_ref[b].astype(F32), f_ref[b].astype(F32), lb)
            v = _heads(i_ref[b])
            qe, qi, ki, kl, e_last = (_heads(gt[n]) for n in ("qe", "qi", "ki", "kl", "e_last"))
            outs, normed = [], []
            for h in range(HG_HEADS):
                p = jnp.where(gt["tril"], _dot_nt(qi[h], ki[h]), 0.0)
                st = state[b, h]
                st_ref[b, h] = st
                o = _dot_nt(qe[h], st) + _dot(p, v[h])
                state[b, h] = st * e_last[h] + _dot_tn(v[h], kl[h])
                outs.append(o)
                normed.append(o * lax.rsqrt(jnp.mean(o * o, axis=-1, keepdims=True) + EPS) * ghv)
            o_ref[b] = jnp.concatenate(outs, axis=1)
            hg = g_ref[b].astype(F32)
            bo_ref[b] = (jnp.concatenate(normed, axis=1) * (hg * _sigmoid(hg))).astype(bo_ref.dtype)

    def col(cb):
        return pl.BlockSpec((B, C, 512), lambda c: (0, c, cb))

    tile = pl.BlockSpec((B, C, W), lambda c: (0, c, 0))
    proj3 = proj.reshape(B, S, proj.shape[-1])
    out, carried = _pcall(
        body, (proj3, proj3, proj3, proj3, lbl, gh), name="hgrn_fwd", grid=(NC,),
        in_specs=[col(COL_HQ), col(COL_HF), col(COL_HI), col(COL_HG),
                  pl.BlockSpec((2, W), lambda c: (0, 0)), pl.BlockSpec((1, HG_DIM), lambda c: (0, 0))],
        out_specs=(tile, tile, pl.BlockSpec((B, None, HG_HEADS, 128, 128), lambda c: (0, c, 0, 0, 0))),
        out_shape=(jax.ShapeDtypeStruct((B, S, W), F32), jax.ShapeDtypeStruct((B, S, W), BF16),
                   jax.ShapeDtypeStruct((B, NC, HG_HEADS, 128, 128), F32)),
        scratch_shapes=[pltpu.VMEM((B, HG_HEADS, 128, 128), F32)],
        semantics=("arbitrary",), riders=riders)
    o_h, b_out, states = out
    out = (o_h, b_out.reshape(B * S, W), states)
    return (out, carried) if riders else out


def _hgrn_bwd(proj, o_saved, states, d_out, lbl, gh, B, S, riders=()):
    C = HG_CHUNK
    NC = S // C
    W = HG_HEADS * HG_DIM

    def body(q_ref, f_ref, i_ref, g_ref, o_ref, st_ref, do_ref, lbl_ref, gh_ref,
             d_ref, dlbl_ref, dgh_ref, dstate, dlb_acc):
        c = pl.program_id(0)

        def put(b, k, val):
            d_ref[b, :, k * W:(k + 1) * W] = val.astype(d_ref.dtype)

        @pl.when(c == 0)
        def _():
            dstate[...] = jnp.zeros_like(dstate)
            dgh_ref[...] = jnp.zeros_like(dgh_ref)
            dlb_acc[...] = jnp.zeros_like(dlb_acc)

        lb = _hgrn_lower_bound(lbl_ref[...])
        ghv = gh_ref[...]
        row = lax.broadcasted_iota(jnp.int32, (C, C), 0)
        colm = lax.broadcasted_iota(jnp.int32, (C, C), 1)
        triu = colm >= row
        for b in range(B):
            hq, hg = q_ref[b].astype(F32), g_ref[b].astype(F32)
            gt = _hgrn_gates(hq, f_ref[b].astype(F32), lb)
            tril = gt["tril"]
            v = _heads(i_ref[b])
            qe, qi, ki, kl, e_last = (_heads(gt[n]) for n in ("qe", "qi", "ki", "kl", "e_last"))
            sgg = _sigmoid(hg)
            don_all = do_ref[b] * (hg * sgg)
            o, don = _heads(o_ref[b]), _heads(don_all)
            d_qe, d_qi, d_ki, d_kl, dv, n_all, dal = [], [], [], [], [], [], []
            for h in range(HG_HEADS):
                r = lax.rsqrt(jnp.mean(o[h] * o[h], axis=-1, keepdims=True) + EPS)
                n = o[h] * r
                n_all.append(n)
                dgh_ref[...] += jnp.sum(don[h] * n, axis=0, keepdims=True)
                dn = don[h] * ghv
                d_o = r * (dn - n * jnp.mean(dn * n, axis=-1, keepdims=True))
                st, dst = st_ref[b, h], dstate[b, h]
                p = jnp.where(tril, _dot_nt(qi[h], ki[h]), 0.0)
                dp = jnp.where(tril, _dot_nt(d_o, v[h]), 0.0)
                d_qe.append(_dot(d_o, st))
                d_qi.append(_dot(dp, ki[h]))
                d_ki.append(_dot_tn(dp, qi[h]))
                d_kl.append(_dot(v[h], dst))
                dv.append(_dot_tn(p, d_o) + _dot_nt(kl[h], dst))
                dstate[b, h] = dst * e_last[h] + _dot_tn(d_o, qe[h])
                dal.append(jnp.sum(dst * st, axis=0, keepdims=True) * e_last[h])
            d_qe, d_qi, d_ki, d_kl, n_all, dal = (jnp.concatenate(t, axis=1)
                                                  for t in (d_qe, d_qi, d_ki, d_kl, n_all, dal))
            put(b, 3, do_ref[b] * n_all * jnp.tile(ghv, (1, HG_HEADS)) * (sgg * (1.0 + hg * (1.0 - sgg))))
            put(b, 2, jnp.concatenate(dv, axis=1))
            d_a_last = dal + jnp.sum(d_kl * gt["kl"], axis=0, keepdims=True)
            dq = d_qe * gt["ea"] + d_qi * gt["ei"]
            dk = d_ki * gt["eki"] + d_kl * gt["ekl"]
            da = d_qe * gt["qe"] + d_qi * gt["qi"] - d_ki * gt["ki"] - d_kl * gt["kl"]
            dlogf = _dot_01(triu, da) + d_a_last
            sg, sq = gt["sg"], gt["sq"]
            dfg = dlogf / gt["fg"] - dk
            put(b, 1, dfg * (1.0 - lb) * sg * (1.0 - sg))
            dlb_acc[...] += jnp.sum(dfg * (1.0 - sg), axis=0, keepdims=True)
            put(b, 0, dq * (sq * (1.0 + hq * (1.0 - sq))))

        @pl.when(c == NC - 1)
        def _():
            dlb = dlb_acc[...]
            first = lax.broadcasted_iota(jnp.int32, (2, W), 0) == 0
            dlbl_ref[...] = jnp.where(first, dlb * lb * (1.0 - lb), -dlb * lb * (1.0 - lb))

    def col(cb):
        return pl.BlockSpec((B, C, 512), lambda c: (0, NC - 1 - c, cb))

    tile = pl.BlockSpec((B, C, W), lambda c: (0, NC - 1 - c, 0))
    proj3 = proj.reshape(B, S, proj.shape[-1])
    out, carried = _pcall(
        body, (proj3, proj3, proj3, proj3, o_saved, states, d_out.reshape(3, B, S, W), lbl, gh), name="hgrn_bwd",
        grid=(NC,),
        in_specs=[col(COL_HQ), col(COL_HF), col(COL_HI), col(COL_HG), tile,
                  pl.BlockSpec((B, None, HG_HEADS, 128, 128), lambda c: (0, NC - 1 - c, 0, 0, 0)),
                  pl.BlockSpec((None, B, C, W), lambda c: (1, 0, NC - 1 - c, 0)),
                  pl.BlockSpec((2, W), lambda c: (0, 0)), pl.BlockSpec((1, HG_DIM), lambda c: (0, 0))],
        out_specs=(pl.BlockSpec((B, C, 4 * W), lambda c: (0, NC - 1 - c, 0)),
                   pl.BlockSpec((2, W), lambda c: (0, 0)), pl.BlockSpec((1, HG_DIM), lambda c: (0, 0))),
        out_shape=(jax.ShapeDtypeStruct((B, S, 4 * W), BF16), jax.ShapeDtypeStruct((2, W), F32),
                   jax.ShapeDtypeStruct((1, HG_DIM), F32)),
        scratch_shapes=[pltpu.VMEM((B, HG_HEADS, 128, 128), F32), pltpu.VMEM((1, W), F32)],
        semantics=("arbitrary",), riders=riders)
    out = (out[0].reshape(B * S, 4 * W),) + tuple(out[1:])
    return (out, carried) if riders else out


_XA_SCALE = XA_DIM ** -0.5


def _attn_probs(qh, kh):
    s = _dot_nt(qh, kh) * _XA_SCALE
    e = jnp.exp(s - jnp.max(s, axis=-1, keepdims=True))
    return e / jnp.sum(e, axis=-1, keepdims=True)


def _attn_fwd(proj, kv, B, S):
    T = B * S
    tq = _row_tile(S)
    nq = S // tq
    W = XA_HEADS * XA_DIM

    def body(q_ref, kv_ref, o_ref):
        for h in range(XA_HEADS):
            sl = slice(h * 128, (h + 1) * 128)
            p = _attn_probs(q_ref[:, sl], kv_ref[:, sl])
            o_ref[:, sl] = _dot(p, kv_ref[:, W + h * 128:W + (h + 1) * 128]).astype(o_ref.dtype)

    return _pallas(
        body, name="attn_fwd", grid=(B, nq),
        in_specs=[pl.BlockSpec((tq, 512), lambda b, i: (b * nq + i, COL_XQ)),
                  pl.BlockSpec((MEM_LEN, 2 * W), lambda b, i: (b, 0))],
        out_specs=pl.BlockSpec((tq, W), lambda b, i: (b * nq + i, 0)),
        out_shape=jax.ShapeDtypeStruct((T, W), BF16), compiler_params=_cp("parallel", "parallel"),
    )(proj, kv)


def _attn_bwd(proj, kv, d_out, B, S):
    T = B * S
    tq = _row_tile(S)
    nq = S // tq
    W = XA_HEADS * XA_DIM

    def body(q_ref, kv_ref, do_ref, dq_ref, dkv_ref):
        @pl.when(pl.program_id(1) == 0)
        def _():
            dkv_ref[...] = jnp.zeros_like(dkv_ref)

        for h in range(XA_HEADS):
            sl = slice(h * 128, (h + 1) * 128)
            slv = slice(W + h * 128, W + (h + 1) * 128)
            qh = q_ref[:, sl]
            kh = kv_ref[:, sl]
            p = _attn_probs(qh, kh)
            dc = do_ref[:, sl]
            dp = _dot_nt(dc, kv_ref[:, slv])
            ds = p * (dp - jnp.sum(dp * p, axis=-1, keepdims=True)) * _XA_SCALE
            dq_ref[:, sl] = _dot(ds, kh).astype(dq_ref.dtype)
            dkv_ref[:, sl] += _dot_tn(ds, qh)
            dkv_ref[:, slv] += _dot_tn(p, dc)

    kvspec = pl.BlockSpec((MEM_LEN, 2 * W), lambda b, i: (b, 0))
    tile = pl.BlockSpec((tq, W), lambda b, i: (b * nq + i, 0))
    return _pallas(
        body, name="attn_bwd", grid=(B, nq),
        in_specs=[pl.BlockSpec((tq, 512), lambda b, i: (b * nq + i, COL_XQ)), kvspec,
                  pl.BlockSpec((None, tq, W), lambda b, i: (2, b * nq + i, 0))],
        out_specs=(tile, kvspec),
        out_shape=(jax.ShapeDtypeStruct((T, W), BF16), jax.ShapeDtypeStruct((B * MEM_LEN, 2 * W), F32)),
        compiler_params=_cp("parallel", "arbitrary"),
    )(proj, kv, d_out)


_MERGE_TM = 256
_GATE_W = 512


def _gate_specs(tm):
    base = COL_GATE0 // _GATE_W
    return [pl.BlockSpec((tm, _GATE_W), functools.partial(lambda i, k: (i, base + k), k=k)) for k in range(6)]


def _merge_fwd(a_out, b_out, c_out, wb, proj, riders=()):
    T = a_out.shape[0]
    tm = _row_tile(T, _MERGE_TM)
    nq, _, wd = wb.shape
    per_half = _GATE_W // wd

    def body(a_ref, b_ref, c_ref, w_ref, *rest):
        gates, (m_ref, up_ref) = rest[:6], rest[6:]
        for hf in range(2):
            cols = slice(hf * _GATE_W, (hf + 1) * _GATE_W)
            acc = None
            for n, br in enumerate((a_ref, b_ref, c_ref)):
                x = br[...]
                up = jnp.concatenate([_dot(x, w_ref[per_half * hf + j, n * BR_WIDTH:(n + 1) * BR_WIDTH, :])
                                      for j in range(per_half)], axis=1)
                up_ref[n, :, cols] = up.astype(up_ref.dtype)
                term = _sigmoid(gates[2 * n + hf][...].astype(F32)) * up
                acc = term if acc is None else acc + term
            m_ref[:, cols] = acc.astype(m_ref.dtype)

    br_spec = pl.BlockSpec((tm, BR_WIDTH), lambda i: (i, 0))
    return _carried(*_pcall(
        body, (a_out, b_out, c_out, wb, *([proj] * 6)), name="merge_fwd", grid=(T // tm,),
        in_specs=[br_spec, br_spec, br_spec,
                  pl.BlockSpec((nq, 3 * BR_WIDTH, wd), lambda i: (0, 0, 0))] + _gate_specs(tm),
        out_specs=(pl.BlockSpec((tm, D_MODEL), lambda i: (i, 0)), pl.BlockSpec((3, tm, D_MODEL), lambda i: (0, i, 0))),
        out_shape=(jax.ShapeDtypeStruct((T, D_MODEL), BF16), jax.ShapeDtypeStruct((3, T, D_MODEL), BF16)),
        semantics=("parallel",), riders=riders), riders)


def _branch_bwd_act(d_ups, wb, riders=()):
    _, T, D = d_ups.shape
    nq, _, wd = wb.shape
    tm = _row_tile(T)

    def body(d_ref, w_ref, o_ref):
        acc = None
        for q in range(nq):
            part = _dot_nt(d_ref[:, q * wd:(q + 1) * wd], w_ref[q])
            acc = part if acc is None else acc + part
        o_ref[...] = acc

    return _carried(*_pcall(
        body, (d_ups, wb), name="d_branch", grid=(3, T // tm),
        in_specs=[pl.BlockSpec((None, tm, D), lambda n, i: (n, i, 0)),
                  pl.BlockSpec((nq, BR_WIDTH, wd), lambda n, i: (0, n, 0))],
        out_specs=pl.BlockSpec((None, tm, BR_WIDTH), lambda n, i: (n, i, 0)),
        out_shape=jax.ShapeDtypeStruct((3, T, BR_WIDTH), F32), semantics=("parallel", "parallel"),
        riders=riders), riders)


def _branch_bwd_weight(name, br, d_ups, n):
    T = br.shape[0]
    D = d_ups.shape[2]
    wd = D // N_CHIPS
    tt = _row_tile(T, _TN_TOKENS)

    def body(b_ref, d_ref, o_ref):
        k = pl.program_id(0)
        for q in range(N_CHIPS):
            part = _dot_tn(b_ref[...], d_ref[:, q * wd:(q + 1) * wd])

            @pl.when(k == 0)
            def _():
                o_ref[q] = part

            @pl.when(k > 0)
            def _():
                o_ref[q] += part

    return _pallas(
        body, name=name, grid=(T // tt,),
        in_specs=[pl.BlockSpec((tt, BR_WIDTH), lambda k: (k, 0)),
                  pl.BlockSpec((None, tt, D), lambda k: (n, k, 0))],
        out_specs=pl.BlockSpec((N_CHIPS, BR_WIDTH, wd), lambda k: (0, 0, 0)),
        out_shape=jax.ShapeDtypeStruct((N_CHIPS, BR_WIDTH, wd), F32), compiler_params=_cp("arbitrary"),
    )(br, d_ups)


def _merge_bwd(d_merged, ups, proj, riders=()):
    T = d_merged.shape[0]
    tm = _row_tile(T, _MERGE_TM)

    def body(dm_ref, up_ref, *rest):
        gates, (dup_ref, dg_ref) = rest[:6], rest[6:]
        for hf in range(2):
            cols = slice(hf * _GATE_W, (hf + 1) * _GATE_W)
            dm = dm_ref[:, cols]
            for n in range(3):
                gate = _sigmoid(gates[2 * n + hf][...].astype(F32))
                dup_ref[n, :, cols] = (dm * gate).astype(dup_ref.dtype)
                dg_ref[:, n * D_MODEL + hf * _GATE_W:n * D_MODEL + (hf + 1) * _GATE_W] = (
                    dm * up_ref[n, :, cols].astype(F32) * gate * (1.0 - gate)).astype(dg_ref.dtype)

    tile = pl.BlockSpec((tm, D_MODEL), lambda i: (i, 0))
    tile3 = pl.BlockSpec((3, tm, D_MODEL), lambda i: (0, i, 0))
    return _carried(*_pcall(
        body, (d_merged, ups, *([proj] * 6)), name="merge_bwd", grid=(T // tm,),
        in_specs=[tile, tile3] + _gate_specs(tm),
        out_specs=(tile3, pl.BlockSpec((tm, 3 * D_MODEL), lambda i: (i, 0))),
        out_shape=(jax.ShapeDtypeStruct((3, T, D_MODEL), BF16), jax.ShapeDtypeStruct((T, 3 * D_MODEL), BF16)),
        semantics=("parallel",), riders=riders), riders)


_CONV_TF = D_FF // 2
_CONV_TS = 256
_HALO = 16


def _conv_fwd(ab, cw, cb, B, S):
    T = B * S
    ts = _row_tile(S, _CONV_TS)
    tf = _CONV_TF
    nb = D_FF // tf
    tps = S // ts
    hb = ts // _HALO

    def body(a_ref, p_ref, b_ref, w_ref, cb_ref, o_ref):
        start = (pl.program_id(0) % tps) == 0
        a = a_ref[...].astype(F32)
        prev = jnp.where(start, 0.0, p_ref[...].astype(F32))
        ext = jnp.concatenate([prev, a], axis=0)
        a1 = pltpu.roll(ext, 1, 0)[_HALO:, :]
        a2 = pltpu.roll(ext, 2, 0)[_HALO:, :]
        ac = cb_ref[...] + w_ref[0] * a2 + w_ref[1] * a1 + w_ref[2] * a
        o_ref[...] = (ac * _sigmoid(ac) * b_ref[...].astype(F32)).astype(o_ref.dtype)

    return _pallas(
        body, name="conv_fwd", grid=(T // ts, nb),
        in_specs=[pl.BlockSpec((ts, tf), lambda i, j: (i, j)),
                  pl.BlockSpec((_HALO, tf), lambda i, j: (jnp.maximum(i * hb - 1, 0), j)),
                  pl.BlockSpec((ts, tf), lambda i, j: (i, j + nb)),
                  pl.BlockSpec((3, 1, tf), lambda i, j: (0, 0, j)),
                  pl.BlockSpec((1, tf), lambda i, j: (0, j))],
        out_specs=pl.BlockSpec((ts, tf), lambda i, j: (i, j)),
        out_shape=jax.ShapeDtypeStruct((T, D_FF), BF16), compiler_params=_cp("parallel", "parallel"),
    )(ab, ab, ab, cw, cb)


def _conv_bwd(ab, d_ff, cw, cb, B, S, riders=()):
    T = B * S
    ts = _row_tile(S, _CONV_TS)
    tf = _CONV_TF
    nb = D_FF // tf
    tps = S // ts
    hb = ts // _HALO
    last_h = T // _HALO - 1
    n_ext = ts + _HALO

    def body(a_ref, ap_ref, an_ref, b_ref, bn_ref, d_ref, dn_ref, w_ref, cb_ref, dab_ref, dw_ref, dcb_ref):
        i = pl.program_id(1)

        @pl.when(i == 0)
        def _():
            dw_ref[...] = jnp.zeros_like(dw_ref)
            dcb_ref[...] = jnp.zeros_like(dcb_ref)

        start = (i % tps) == 0
        end = (i % tps) == tps - 1
        a = a_ref[...].astype(F32)
        ext = jnp.concatenate([jnp.where(start, 0.0, ap_ref[...].astype(F32)), a, an_ref[...].astype(F32)], axis=0)
        r1 = pltpu.roll(ext, 1, 0)[_HALO:, :]
        r2 = pltpu.roll(ext, 2, 0)[_HALO:, :]
        ac = cb_ref[...] + w_ref[0] * r2 + w_ref[1] * r1 + w_ref[2] * ext[_HALO:, :]
        sg = _sigmoid(ac)
        d_e = jnp.concatenate([d_ref[...].astype(F32), jnp.where(end, 0.0, dn_ref[...].astype(F32))], axis=0)
        b_e = jnp.concatenate([b_ref[...].astype(F32), bn_ref[...].astype(F32)], axis=0)
        dab_ref[1] = (d_e[:ts, :] * (ac * sg)[:ts, :]).astype(dab_ref.dtype)
        dac = d_e * b_e * sg * (1.0 + ac * (1.0 - sg))
        u1 = pltpu.roll(dac, n_ext - 1, 0)[:ts, :]
        u2 = pltpu.roll(dac, n_ext - 2, 0)[:ts, :]
        dac0 = dac[:ts, :]
        dab_ref[0] = (w_ref[2] * dac0 + w_ref[1] * u1 + w_ref[0] * u2).astype(dab_ref.dtype)
        dcb_ref[...] += jnp.sum(dac0, axis=0, keepdims=True)
        dw_ref[2] += jnp.sum(dac0 * a, axis=0, keepdims=True)
        dw_ref[1] += jnp.sum(dac0 * r1[:ts, :], axis=0, keepdims=True)
        dw_ref[0] += jnp.sum(dac0 * r2[:ts, :], axis=0, keepdims=True)

    def cur(off):
        return pl.BlockSpec((ts, tf), lambda j, i: (i, j + off))

    def nxt(off):
        return pl.BlockSpec((_HALO, tf), lambda j, i: (jnp.minimum((i + 1) * hb, last_h), j + off))

    return _carried(*_pcall(
        body, (ab, ab, ab, ab, ab, d_ff, d_ff, cw, cb), name="conv_bwd", grid=(nb, T // ts),
        in_specs=[cur(0), pl.BlockSpec((_HALO, tf), lambda j, i: (jnp.maximum(i * hb - 1, 0), j)), nxt(0),
                  cur(nb), nxt(nb), cur(0), nxt(0),
                  pl.BlockSpec((3, 1, tf), lambda j, i: (0, 0, j)), pl.BlockSpec((1, tf), lambda j, i: (0, j))],
        out_specs=(pl.BlockSpec((2, ts, tf), lambda j, i: (0, i, j)), pl.BlockSpec((3, 1, tf), lambda j, i: (0, 0, j)),
                   pl.BlockSpec((1, tf), lambda j, i: (0, j))),
        out_shape=(jax.ShapeDtypeStruct((2, T, D_FF), BF16),
                   jax.ShapeDtypeStruct((3, 1, D_FF), F32), jax.ShapeDtypeStruct((1, D_FF), F32)),
        semantics=("parallel", "arbitrary"), riders=riders), riders)


def _local_step(x, mem, tgt, p, comm, B, S):
    g = {}
    h = _rms_fwd("norm1", x, p["norm1_g"])
    proj = comm.carry("in_proj", lambda r: _mm_cs("in_proj", h, comm.w("w_in"), BF16, riders=r))
    a_out = _gmlp_fwd(proj, p["ln_v_g"], p["ln_v_b"], p["w_spatial"], p["b_spatial"])
    o_h, b_out, states = comm.carry(
        "hgrn_fwd", lambda r: _hgrn_fwd(proj, p["lb_logits"], p["hgrn_norm_g"], B, S, riders=r))
    memn = _rms_fwd("mem_norm", mem, p["mem_norm_g"])
    kv = _mm_rs("mem_kv", memn, comm.w("w_mem_kv"), F32)
    c_out = _attn_fwd(proj, kv, B, S)
    merged, ups = comm.carry(
        "merge_fwd", lambda r: _merge_fwd(a_out, b_out, c_out, comm.w("w_branch"), proj, riders=r))
    x1 = _mm_rs("out_proj", merged, comm.w("w_out"), F32, res=x)
    h2 = _rms_fwd("norm2", x1, p["norm2_g"])
    ab = comm.carry("up_proj", lambda r: _mm_cs("up_proj", h2, comm.w("w_up"), BF16, riders=r))
    conv_w = comm.w("conv_w")
    ff = _conv_fwd(ab, conv_w, p["conv_b"], B, S)
    x2 = _mm_rs("down_proj", ff, comm.w("w_down"), F32, res=x1)
    dx2, g["final_g"], loss = _loss_head(x2, tgt, p["final_g"])

    comm.grad("w_down", _mm_tn_rs("g_w_down", ff, dx2, to=D_FF // 2))
    d_ff = comm.carry("d_ff", lambda r: _mm_nt_rs("d_ff", dx2, comm.w("w_down"), BF16, riders=r))
    d_ab, g["conv_w"], g["conv_b"] = comm.carry(
        "conv_bwd", lambda r: _conv_bwd(ab, d_ff, conv_w, p["conv_b"], B, S, riders=r))
    comm.grad("w_up", _mm_tn_cs("g_w_up", h2, d_ab, N_CHIPS, to=512, stacked=True))
    d_h2 = comm.carry("d_h2", lambda r: _mm_nt_cs("d_h2", d_ab, comm.w("w_up"), F32, riders=r, stacked=True))
    d_x1, g["norm2_g"] = _rms_bwd("norm2_bwd", x1, p["norm2_g"], d_h2, dx2)
    comm.grad("w_out", _mm_tn_rs("g_w_out", merged, d_x1, to=512))
    d_merged = _mm_nt_rs("d_merged", d_x1, comm.w("w_out"), F32)
    d_ups, d_gates = comm.carry("merge_bwd", lambda r: _merge_bwd(d_merged, ups, proj, riders=r))

    d_br = comm.carry("d_branch", lambda r: _branch_bwd_act(d_ups, comm.w("w_branch"), riders=r))
    comm.grad("w_branch", jnp.concatenate(
        [_branch_bwd_weight("g_w_branch%d" % n, br, d_ups, n) for n, br in enumerate((a_out, b_out, c_out))],
        axis=1))

    d_gm, g["w_spatial"], g["b_spatial"], g["ln_v_g"], g["ln_v_b"] = comm.carry(
        "gmlp_bwd", lambda r: _gmlp_bwd(proj, d_br, p["ln_v_g"], p["ln_v_b"], p["w_spatial"], p["b_spatial"],
                                        riders=r))
    d_xq, d_kv = _attn_bwd(proj, kv, d_br, B, S)
    comm.grad("w_mem_kv", _mm_tn_rs("g_w_mem_kv", memn, d_kv, to=512))
    d_memn = _mm_nt_rs("d_memn", d_kv, comm.w("w_mem_kv"), F32)
    _, g["mem_norm_g"] = _rms_bwd("mem_norm_bwd", mem, p["mem_norm_g"], d_memn, None)
    d_hgrn, g["lb_logits"], g["hgrn_norm_g"] = comm.carry(
        "hgrn_bwd", lambda r: _hgrn_bwd(proj, o_h, states, d_br, p["lb_logits"], p["hgrn_norm_g"], B, S, riders=r))
    d_proj = (d_gm, d_hgrn, d_xq, d_gates)
    comm.small_grads([g[n].reshape(_SMALL_SHAPE[n]) for n in _SMALL_EARLY] + [loss])
    w_in = comm.w("w_in")
    comm.grad("w_in", comm.carry("g_w_in", lambda r: _mm_tn_pieces(
        "g_w_in", h, d_proj, N_CHIPS, w_in.shape[-1], to=512, riders=r)))
    d_h = comm.carry("d_h", lambda r: _mm_nt_pieces("d_h", d_proj, w_in, F32, riders=r))
    grad_x, g["norm1_g"] = _rms_bwd("norm1_bwd", x, p["norm1_g"], d_h, d_x1)
    return loss, grad_x, g


HBM_SPEC = pl.BlockSpec(memory_space=pltpu.HBM)


def _place():
    x, y, c = lax.axis_index("x"), lax.axis_index("y"), lax.axis_index("c")
    other_chips = [(1 - x, y), (x, 1 - y), (1 - x, 1 - y)]
    return x, y, c, other_chips


def _remote(src, dst, send_sem, recv_sem, dev):
    return pltpu.make_async_remote_copy(src_ref=src, dst_ref=dst, send_sem=send_sem, recv_sem=recv_sem,
                                        device_id=dev, device_id_type=MESH_ID)


class _Exchange:
    def __init__(self, operands, out_shape, aliases, scratch, start, finish):
        self.operands, self.out_shape, self.aliases, self.scratch = operands, out_shape, aliases, scratch
        self.start, self.finish = start, finish


def _run_exchanges(name, exs):
    n_in = [len(ex.operands) for ex in exs]
    n_out = [len(ex.out_shape) for ex in exs]
    n_scr = [len(ex.scratch) for ex in exs]

    def body(*refs):
        ins, outs, scr = refs[:sum(n_in)], refs[sum(n_in):sum(n_in) + sum(n_out)], refs[sum(n_in) + sum(n_out):]
        parts, oi, oo, os_ = [], 0, 0, 0
        for k in range(len(exs)):
            parts.append((ins[oi:oi + n_in[k]], outs[oo:oo + n_out[k]], scr[os_:os_ + n_scr[k]]))
            oi, oo, os_ = oi + n_in[k], oo + n_out[k], os_ + n_scr[k]
        for ex, part in zip(exs, parts):
            ex.start(*part)
        for ex, part in zip(exs, parts):
            ex.finish(*part)

    aliases, ops, shapes, scratch, oi, oo = {}, [], [], [], 0, 0
    for k, ex in enumerate(exs):
        aliases.update({oi + a: oo + b for a, b in ex.aliases.items()})
        oi, oo = oi + n_in[k], oo + n_out[k]
        ops += list(ex.operands)
        shapes += [pltpu.HBM(s.shape, s.dtype) for s in ex.out_shape]
        scratch += list(ex.scratch)
    res = _pallas(
        body, name=name, in_specs=[HBM_SPEC] * len(ops), out_specs=(HBM_SPEC,) * len(shapes), out_shape=tuple(shapes),
        input_output_aliases=aliases, scratch_shapes=scratch,
    )(*ops)
    out, oo = [], 0
    for k in range(len(exs)):
        out.append(list(res[oo:oo + n_out[k]]))
        oo += n_out[k]
    return out


def _ex_all_gather(slabs, halved, part=(0, 1)):
    n = len(slabs)

    def rows(a, cc):
        if not halved[a]:
            return slice(None)
        pr = slabs[a].shape[1] // part[1]
        return pl.ds(part[0] * pr + cc * (pr // 2), pr // 2)

    def ici(bufs, scr, a, j, chip, c, mine):
        px, py = chip
        x, y, _, _ = _place()
        qs = 2 * x + y if mine else 2 * px + py
        piece = bufs[a].at[qs, rows(a, c)]
        return _remote(piece, piece, scr[0].at[3 * a + j], scr[1].at[3 * a + j], (px, py, c))

    def d2d(bufs, scr, a, j, chip, cc):
        px, py = chip
        x, y, c, _ = _place()
        piece = bufs[a].at[2 * px + py, rows(a, cc)]
        return _remote(piece, piece, scr[2].at[3 * a + j], scr[3].at[3 * a + j], (x, y, 1 - c))

    def start(ins, outs, scr):
        _, _, c, chips = _place()
        for j, chip in enumerate(chips):
            for a in range(n):
                ici(outs, scr, a, j, chip, c, True).start()

    def finish(ins, outs, scr):
        _, _, c, chips = _place()
        for j, chip in enumerate(chips):
            for a in range(n):
                ici(outs, scr, a, j, chip, c, False).wait_recv()
                if halved[a]:
                    d2d(outs, scr, a, j, chip, c).start()
        for j, chip in enumerate(chips):
            for a in range(n):
                if halved[a]:
                    d2d(outs, scr, a, j, chip, 1 - c).wait_recv()
        for j, chip in enumerate(chips):
            for a in range(n):
                ici(outs, scr, a, j, chip, c, True).wait_send()
                if halved[a]:
                    d2d(outs, scr, a, j, chip, c).wait_send()

    return _Exchange(list(slabs), [jax.ShapeDtypeStruct(s.shape, s.dtype) for s in slabs],
                     {a: a for a in range(n)}, [pltpu.SemaphoreType.DMA((3 * n,))] * 4, start, finish)


def _ex_to_sibling(grads):
    n = len(grads)

    def copy(ins, outs, scr, a):
        x, y, c, _ = _place()
        hr = grads[a].shape[1] // 2
        return _remote(ins[a].at[:, pl.ds((1 - c) * hr, hr), :], outs[a], scr[0].at[a], scr[1].at[a], (x, y, 1 - c))

    def start(ins, outs, scr):
        for a in range(n):
            copy(ins, outs, scr, a).start()

    def finish(ins, outs, scr):
        for a in range(n):
            copy(ins, outs, scr, a).wait()

    out_shape = [jax.ShapeDtypeStruct((g.shape[0], g.shape[1] // 2, g.shape[2]), g.dtype) for g in grads]
    return _Exchange(list(grads), out_shape, {}, [pltpu.SemaphoreType.DMA((n,))] * 2, start, finish)


def _ex_to_owner(parts, part=(0, 1), landing=None):
    n = len(parts)

    def copy(ins, outs, scr, a, j, chip):
        _, _, c, _ = _place()
        px, py = chip
        pr = parts[a].shape[1] // part[1]
        rows = pl.ds(part[0] * pr, pr)
        return _remote(ins[a].at[2 * px + py, rows], outs[a].at[j, rows], scr[0].at[3 * a + j],
                       scr[1].at[3 * a + j], (px, py, c))

    def start(ins, outs, scr):
        for j, chip in enumerate(_place()[3]):
            for a in range(n):
                copy(ins, outs, scr, a, j, chip).start()

    def finish(ins, outs, scr):
        for j, chip in enumerate(_place()[3]):
            for a in range(n):
                copy(ins, outs, scr, a, j, chip).wait()

    out_shape = [jax.ShapeDtypeStruct((3,) + p.shape[1:], p.dtype) for p in parts]
    operands, aliases = list(parts), {}
    if landing is not None:
        operands, aliases = operands + list(landing), {n + a: a for a in range(n)}
    return _Exchange(operands, out_shape, aliases, [pltpu.SemaphoreType.DMA((3 * n,))] * 2, start, finish)


def _ex_share_halves(bufs):
    n = len(bufs)

    def copy(outs, scr, a, cc):
        x, y, c, _ = _place()
        hr = bufs[a].shape[0] // 2
        piece = outs[a].at[pl.ds(cc * hr, hr), :]
        return _remote(piece, piece, scr[0].at[a], scr[1].at[a], (x, y, 1 - c))

    def start(ins, outs, scr):
        c = _place()[2]
        for a in range(n):
            copy(outs, scr, a, c).start()

    def finish(ins, outs, scr):
        c = _place()[2]
        for a in range(n):
            copy(outs, scr, a, c).wait_send()
            copy(outs, scr, a, 1 - c).wait_recv()

    return _Exchange(list(bufs), [jax.ShapeDtypeStruct(b.shape, b.dtype) for b in bufs], {a: a for a in range(n)},
                     [pltpu.SemaphoreType.DMA((n,))] * 2, start, finish)


def _ex_gather_small(arrs):
    n = len(arrs)

    def peer_of(m):
        x, y, c, _ = _place()
        return (1 - x if m & 4 else x, 1 - y if m & 2 else y, 1 - c if m & 1 else c)

    def start(ins, outs, scr):
        x, y, c, _ = _place()
        for m in range(1, N_DEV):
            for a in range(n):
                k = (N_DEV - 1) * a + m - 1
                _remote(ins[a], outs[a].at[4 * x + 2 * y + c], scr[0].at[k], scr[1].at[k], peer_of(m)).start()

    def finish(ins, outs, scr):
        for m in range(1, N_DEV):
            px, py, pc = peer_of(m)
            for a in range(n):
                k = (N_DEV - 1) * a + m - 1
                slot = outs[a].at[4 * px + 2 * py + pc]
                cp = _remote(ins[a], slot, scr[0].at[k], scr[1].at[k], (px, py, pc))
                cp.wait_send()
                cp.wait_recv()

    slots = [jnp.zeros((N_DEV,) + a.shape, a.dtype) for a in arrs]
    out_shape = [jax.ShapeDtypeStruct(s.shape, s.dtype) for s in slots]
    return _Exchange(list(arrs) + slots, out_shape, {n + a: a for a in range(n)},
                     [pltpu.SemaphoreType.DMA(((N_DEV - 1) * n,))] * 2, start, finish)


def _div_tile(n, want):
    best = None
    for t in range(8, min(n, want) + 1, 8):
        if n % t == 0:
            best = t
    assert best is not None, n
    return best


def _cast_into_slab(name, w, place, dtype):
    r, cc = w.shape
    tr = r if r * cc <= 128 * 1024 else _div_tile(r, 256)

    def body(s_ref, w_ref, o_ref):
        o_ref[...] = w_ref[...].astype(o_ref.dtype)

    return _pallas(
        body, name=name,
        grid_spec=pltpu.PrefetchScalarGridSpec(
            num_scalar_prefetch=1, grid=(r // tr,),
            in_specs=[pl.BlockSpec((tr, cc), lambda i, s: (i, 0))],
            out_specs=pl.BlockSpec((None, tr, cc), lambda i, s: (s[0], i, 0))),
        out_shape=jax.ShapeDtypeStruct((N_CHIPS, r, cc), dtype), compiler_params=_cp("parallel"),
    )(place, w)


def _add_half(name, g, rcv, place):
    nq, r, cc = g.shape
    hr = r // 2

    def body(s_ref, g_ref, r_ref, o_ref):
        o_ref[...] = (g_ref[...] + r_ref[...]).astype(o_ref.dtype)

    spec = pl.BlockSpec((None, hr, cc), lambda i, s: (i, 0, 0))
    return _pallas(
        body, name=name,
        grid_spec=pltpu.PrefetchScalarGridSpec(
            num_scalar_prefetch=1, grid=(nq,),
            in_specs=[pl.BlockSpec((None, hr, cc), lambda i, s: (i, s[1], 0)), spec], out_specs=spec),
        out_shape=jax.ShapeDtypeStruct((nq, hr, cc), BF16), compiler_params=_cp("parallel"),
    )(place, g, rcv)


def _sum_owner(name, part, rcv, place):
    _, hr, cc = part.shape
    tr = _div_tile(hr, 128)
    nb = hr // tr

    def body(s_ref, p_ref, r_ref, o_ref):
        o_ref[...] = ((p_ref[...].astype(F32) + r_ref[0].astype(F32)) + r_ref[1].astype(F32)) + r_ref[2].astype(F32)

    return _pallas(
        body, name=name,
        grid_spec=pltpu.PrefetchScalarGridSpec(
            num_scalar_prefetch=1, grid=(nb,),
            in_specs=[pl.BlockSpec((None, tr, cc), lambda i, s: (s[0], i, 0)),
                      pl.BlockSpec((3, tr, cc), lambda i, s: (0, i, 0))],
            out_specs=pl.BlockSpec((tr, cc), lambda i, s: (s[1] * nb + i, 0))),
        out_shape=jax.ShapeDtypeStruct((2 * hr, cc), F32), compiler_params=_cp("parallel"),
    )(place, part, rcv)


def _sum_small(gathered, local, place):
    n = len(gathered)

    def body(s_ref, *refs):
        g_refs, l_refs, o_refs = refs[:n], refs[n:2 * n], refs[2 * n:]
        me = s_ref[2]
        for g_ref, l_ref, o_ref in zip(g_refs, l_refs, o_refs):
            acc = None
            for d in range(N_DEV):
                term = jnp.where(me == d, l_ref[...], g_ref[d])
                acc = term if acc is None else acc + term
            o_ref[...] = acc

    def whole(shape):
        return pl.BlockSpec(shape, lambda i, s, nd=len(shape): (0,) * nd)

    return _pallas(
        body, name="sum_small",
        grid_spec=pltpu.PrefetchScalarGridSpec(
            num_scalar_prefetch=1, grid=(1,),
            in_specs=[whole(g.shape) for g in gathered] + [whole(a.shape) for a in local],
            out_specs=tuple(whole(a.shape) for a in local)),
        out_shape=tuple(jax.ShapeDtypeStruct(a.shape, a.dtype) for a in local), compiler_params=_cp("arbitrary"),
    )(place, *gathered, *local)


def _adamw(name, w, g, m, v):
    r, cc = w.shape
    tr = r if r * cc <= 128 * 1024 else _div_tile(r, 256)

    def body(w_ref, g_ref, m_ref, v_ref, d_ref, mo_ref, vo_ref):
        gv = g_ref[...]
        mn = ADAM_B1 * m_ref[...] + (1.0 - ADAM_B1) * gv
        vn = ADAM_B2 * v_ref[...] + (1.0 - ADAM_B2) * (gv * gv)
        m_hat = mn / (1.0 - ADAM_B1 ** ADAM_STEP)
        v_hat = vn / (1.0 - ADAM_B2 ** ADAM_STEP)
        d_ref[...] = -ADAM_LR * (m_hat / (jnp.sqrt(v_hat) + ADAM_EPS) + ADAM_WD * w_ref[...])
        mo_ref[...] = mn
        vo_ref[...] = vn

    spec = pl.BlockSpec((tr, cc), lambda i: (i, 0))
    sd = jax.ShapeDtypeStruct((r, cc), F32)
    return _pallas(
        body, name=name, grid=(r // tr,), in_specs=[spec] * 4, out_specs=(spec,) * 3, out_shape=(sd,) * 3,
        compiler_params=_cp("parallel"),
    )(w, g, m, v)


_BIG = ("w_in", "w_up", "w_branch", "w_mem_kv", "w_out", "w_down")
_BIG_SHARD_SHAPE = {"w_in": (1024, 1664), "w_up": (1024, 1408), "w_branch": (1536, 256),
                    "w_mem_kv": (256, 1024), "w_out": (256, 1024), "w_down": (704, 1024)}
_SMALL_SHAPE = {"norm1_g": (1, D_MODEL), "ln_v_g": (1, GM_WIDTH), "ln_v_b": (1, GM_WIDTH),
                "w_spatial": (GM_GROUPS * GM_CHUNK, GM_CHUNK), "b_spatial": (GM_GROUPS, GM_CHUNK),
                "lb_logits": (2, HG_HEADS * HG_DIM), "hgrn_norm_g": (1, HG_DIM), "mem_norm_g": (1, D_MODEL),
                "norm2_g": (1, D_MODEL), "conv_w": (3, D_FF), "conv_b": (1, D_FF), "final_g": (1, D_MODEL)}
_SMALL_EARLY = tuple(n for n in _SMALL_SHAPE if n != "norm1_g")
_PARAM_ORDER = ("norm1_g", "w_in", "ln_v_g", "ln_v_b", "w_spatial", "b_spatial", "lb_logits", "hgrn_norm_g",
                "mem_norm_g", "w_mem_kv", "w_branch", "w_out", "norm2_g", "w_up", "conv_w", "conv_b", "w_down",
                "final_g")


def _adamw_small(ws, gs, ms, vs):
    n = len(ws)

    def body(*refs):
        w_refs, g_refs, m_refs, v_refs = refs[:n], refs[n:2 * n], refs[2 * n:3 * n], refs[3 * n:4 * n]
        d_refs, mo_refs, vo_refs = refs[4 * n:5 * n], refs[5 * n:6 * n], refs[6 * n:]
        for k in range(n):
            gv = g_refs[k][...]
            mn = ADAM_B1 * m_refs[k][...] + (1.0 - ADAM_B1) * gv
            vn = ADAM_B2 * v_refs[k][...] + (1.0 - ADAM_B2) * (gv * gv)
            m_hat = mn / (1.0 - ADAM_B1 ** ADAM_STEP)
            v_hat = vn / (1.0 - ADAM_B2 ** ADAM_STEP)
            d_refs[k][...] = -ADAM_LR * (m_hat / (jnp.sqrt(v_hat) + ADAM_EPS) + ADAM_WD * w_refs[k][...])
            mo_refs[k][...] = mn
            vo_refs[k][...] = vn

    specs = [pl.BlockSpec(a.shape, lambda i: (0, 0)) for a in ws]
    shapes = tuple(jax.ShapeDtypeStruct(a.shape, F32) for a in ws)
    res = _pallas(
        body, name="adamw_small", grid=(1,), in_specs=specs * 4, out_specs=tuple(specs * 3), out_shape=shapes * 3,
        compiler_params=_cp("arbitrary"),
    )(*ws, *gs, *ms, *vs)
    return res[:n], res[n:2 * n], res[2 * n:]


class _Comm:
    _ROW_SHARDED = ("w_mem_kv", "w_out", "w_down")

    def __init__(self, slabs, place):
        self.slabs, self.place = slabs, place
        self.full, self.raw, self.parts, self.landing, self.bufs, self.done = {}, {}, {}, {}, {}, {}
        ex, deliver = self._gather(["w_in"])
        deliver(_run_exchanges("all_gather_w_in", [ex])[0])

    def w(self, name):
        a = self.full[name]
        if name in self._ROW_SHARDED:
            return a.reshape(-1, a.shape[-1])
        if name == "conv_w":
            return jnp.transpose(a, (1, 0, 2)).reshape(3, 1, D_FF)
        return a

    def grad(self, name, arr):
        self.raw[name] = arr.reshape((N_CHIPS, -1, arr.shape[-1]))
        if name == "w_in":
            ex, deliver = self._to_sibling(["w_in"])
            deliver(_run_exchanges("rs_sibling_w_in", [ex])[0])

    def small_grads(self, arrays):
        self.small_local = list(arrays)

    def carry(self, tag, call):
        plan = self._plan(tag)
        if not plan:
            return call(())
        out, carried = call([ex for ex, _ in plan])
        for (_, deliver), res in zip(plan, carried):
            deliver(res)
        return out

    def finish(self, last_small):
        ex, deliver = self._share(["w_out", "w_branch", "w_mem_kv", "w_in"])
        shared, small = _run_exchanges("share_and_gather_last", [ex, _ex_gather_small(last_small)])
        deliver(shared)
        return self.done, self.small_local + list(last_small), self.small_everyone + small

    def _plan(self, tag):
        if tag == "in_proj":
            return [self._gather(["w_branch", "w_out", "w_mem_kv", "w_down", "conv_w"])]
        if tag == "hgrn_fwd":
            return [self._gather(["w_up"])]
        if tag == "d_h2":
            return [self._to_sibling(["w_down", "w_up"])]
        if tag == "hgrn_bwd":
            return [self._to_owner(["w_down", "w_up"]), self._to_sibling(["w_out", "w_branch", "w_mem_kv"])]
        if tag == "g_w_in":
            def keep(res):
                self.small_everyone = res

            return [self._to_owner(["w_out", "w_branch", "w_mem_kv"]), self._share(["w_down", "w_up"]),
                    (_ex_gather_small(self.small_local), keep)]
        if tag == "d_h":
            return [self._to_owner(["w_in"])]
        return []

    def _gather(self, names, part=(0, 1)):
        def deliver(res):
            self.slabs.update(zip(names, res))
            self.full.update(zip(names, res))

        return _ex_all_gather([self.slabs[n] for n in names], [n != "conv_w" for n in names], part), deliver

    def _to_sibling(self, names):
        def deliver(res):
            for n, r in zip(names, res):
                self.parts[n] = _add_half("rs_add_" + n, self.raw[n], r, self.place)

        return _ex_to_sibling([self.raw[n] for n in names]), deliver

    def _to_owner(self, names, part=(0, 1)):
        def deliver(res):
            for n, r in zip(names, res):
                if part[0] + 1 < part[1]:
                    self.landing[n] = r
                else:
                    self.bufs[n] = _sum_owner("rs_sum_" + n, self.parts[n], r, self.place)

        landing = [self.landing[n] for n in names] if part[0] else None
        return _ex_to_owner([self.parts[n] for n in names], part, landing), deliver

    def _share(self, names):
        return _ex_share_halves([self.bufs[n] for n in names]), lambda res: self.done.update(zip(names, res))


def kernel(x, mem, norm1_g, w_in, ln_v_g, ln_v_b, w_spatial, b_spatial, lb_logits, hgrn_norm_g, mem_norm_g, w_mem_kv, w_branch, w_out, norm2_g, w_up, conv_w, conv_b, w_down, final_g, loss_target, m_norm1_g, m_w_in, m_ln_v_g, m_ln_v_b, m_w_spatial, m_b_spatial, m_lb_logits, m_hgrn_norm_g, m_mem_norm_g, m_w_mem_kv, m_w_branch, m_w_out, m_norm2_g, m_w_up, m_conv_w, m_conv_b, m_w_down, m_final_g, v_norm1_g, v_w_in, v_ln_v_g, v_ln_v_b, v_w_spatial, v_b_spatial, v_lb_logits, v_hgrn_norm_g, v_mem_norm_g, v_w_mem_kv, v_w_branch, v_w_out, v_norm2_g, v_w_up, v_conv_w, v_conv_b, v_w_down, v_final_g):
    w = dict(norm1_g=norm1_g, w_in=w_in, ln_v_g=ln_v_g, ln_v_b=ln_v_b, w_spatial=w_spatial, b_spatial=b_spatial,
             lb_logits=lb_logits, hgrn_norm_g=hgrn_norm_g, mem_norm_g=mem_norm_g, w_mem_kv=w_mem_kv,
             w_branch=w_branch, w_out=w_out, norm2_g=norm2_g, w_up=w_up, conv_w=conv_w, conv_b=conv_b,
             w_down=w_down, final_g=final_g)
    mom = dict(norm1_g=m_norm1_g, w_in=m_w_in, ln_v_g=m_ln_v_g, ln_v_b=m_ln_v_b, w_spatial=m_w_spatial,
               b_spatial=m_b_spatial, lb_logits=m_lb_logits, hgrn_norm_g=m_hgrn_norm_g, mem_norm_g=m_mem_norm_g,
               w_mem_kv=m_w_mem_kv, w_branch=m_w_branch, w_out=m_w_out, norm2_g=m_norm2_g, w_up=m_w_up,
               conv_w=m_conv_w, conv_b=m_conv_b, w_down=m_w_down, final_g=m_final_g)
    var = dict(norm1_g=v_norm1_g, w_in=v_w_in, ln_v_g=v_ln_v_g, ln_v_b=v_ln_v_b, w_spatial=v_w_spatial,
               b_spatial=v_b_spatial, lb_logits=v_lb_logits, hgrn_norm_g=v_hgrn_norm_g, mem_norm_g=v_mem_norm_g,
               w_mem_kv=v_w_mem_kv, w_branch=v_w_branch, w_out=v_w_out, norm2_g=v_norm2_g, w_up=v_w_up,
               conv_w=v_conv_w, conv_b=v_conv_b, w_down=v_w_down, final_g=v_final_g)
    B, S, D = x.shape
    T = B * S
    ci = lax.axis_index("c")
    q = 2 * lax.axis_index("x") + lax.axis_index("y")
    place = jnp.stack([q, ci, 2 * q + ci]).astype(jnp.int32)

    slabs = {n: _cast_into_slab("slab_" + n, w[n].reshape(_BIG_SHARD_SHAPE[n]), place, BF16) for n in _BIG}
    slabs["conv_w"] = _cast_into_slab("slab_conv_w", conv_w[0], place, F32)
    comm = _Comm(slabs, place)
    p = dict(
        norm1_g=norm1_g, ln_v_g=ln_v_g, ln_v_b=ln_v_b, w_spatial=w_spatial[0],
        b_spatial=b_spatial.reshape(GM_GROUPS, GM_CHUNK, 1), lb_logits=lb_logits, hgrn_norm_g=hgrn_norm_g,
        mem_norm_g=mem_norm_g, norm2_g=norm2_g, conv_b=conv_b, final_g=final_g.reshape(1, D))

    loss, grad_x, g = _local_step(x.reshape(T, D), mem.reshape(B * MEM_LEN, D), loss_target.reshape(T, D), p, comm,
                                  B, S)

    shard_grads, local_small, everyone = comm.finish([g["norm1_g"]])
    summed = _sum_small(everyone, local_small, place)
    small_names = list(_SMALL_EARLY) + ["norm1_g"]
    total = dict(zip(_SMALL_EARLY, summed))
    loss_total, total["norm1_g"] = summed[len(_SMALL_EARLY)][0, 0], summed[-1]

    grads, delta, new_m, new_v = {}, {}, {}, {}
    for n in _BIG:
        shp = _BIG_SHARD_SHAPE[n]
        grads[n] = shard_grads[n]
        delta[n], new_m[n], new_v[n] = _adamw("adamw_" + n, w[n].reshape(shp), shard_grads[n],
                                              mom[n].reshape(shp), var[n].reshape(shp))
    cw_shard = D_FF // N_CHIPS
    total["conv_w"] = lax.dynamic_slice(total["conv_w"], (0, q * cw_shard), (3, cw_shard))

    def flat2d(d, n):
        return d[n].reshape(total[n].shape)

    upd = _adamw_small([flat2d(w, n) for n in small_names], [total[n] for n in small_names],
                       [flat2d(mom, n) for n in small_names], [flat2d(var, n) for n in small_names])
    for k, n in enumerate(small_names):
        grads[n], delta[n], new_m[n], new_v[n] = total[n], upd[0][k], upd[1][k], upd[2][k]

    def shaped(d):
        return [d[n].reshape(w[n].shape) for n in _PARAM_ORDER]

    return (loss_total, grad_x.reshape(B, S, D), *shaped(grads), *shaped(delta), *shaped(new_m), *shaped(new_v))
```

```python
import functools
import math

import jax
import jax.numpy as jnp
from jax import lax
from jax.experimental import pallas as pl
from jax.experimental.pallas import tpu as pltpu

F32 = jnp.float32
BF16 = jnp.bfloat16
EPS = 1e-6

D_MODEL = 1024
MEM_LEN = 256
GM_WIDTH = 512
GM_CHUNK = 128
GM_GROUPS = 4
HG_HEADS = 4
HG_DIM = 128
HG_CHUNK = 64
XA_HEADS = 4
XA_DIM = 128
BR_WIDTH = 512
D_FF = 2816
IN_WIDTH = 6656
N_CHIPS = 4
N_DEV = 8

ADAM_LR = 0.001
ADAM_B1 = 0.9
ADAM_B2 = 0.999
ADAM_EPS = 1e-08
ADAM_WD = 0.01
ADAM_STEP = 10

COL_ZU, COL_ZV, COL_HQ, COL_HF, COL_HI, COL_HG, COL_XQ = 0, 1, 2, 3, 4, 5, 6
COL_GATE0 = 3584

VMEM_LIMIT_BYTES = 48 * 1024 * 1024
MESH_ID = pl.DeviceIdType.MESH


def _cp(*sem):
    return pltpu.CompilerParams(dimension_semantics=sem, vmem_limit_bytes=VMEM_LIMIT_BYTES)


def _pallas(body, *, out_shape, **kw):
    def pin(s):
        return pltpu.HBM(s.shape, s.dtype) if isinstance(s, jax.ShapeDtypeStruct) else s

    out_shape = tuple(pin(s) for s in out_shape) if isinstance(out_shape, (tuple, list)) else pin(out_shape)
    call = pl.pallas_call(body, out_shape=out_shape, **kw)

    def run(*operands):
        return call(*[pltpu.with_memory_space_constraint(o, pltpu.HBM) if jnp.issubdtype(o.dtype, jnp.floating)
                      else o for o in operands])

    return run


def _dot(a, b):
    return lax.dot_general(a.astype(BF16), b.astype(BF16), (((1,), (0,)), ((), ())), preferred_element_type=F32)


def _dot_nt(a, b):
    return lax.dot_general(a.astype(BF16), b.astype(BF16), (((1,), (1,)), ((), ())), preferred_element_type=F32)


def _dot_tn(a, b):
    return lax.dot_general(a.astype(BF16), b.astype(BF16), (((0,), (0,)), ((), ())), preferred_element_type=F32)


def _dot_01(mask01, x):
    hi = x.astype(BF16)
    r1 = x - hi.astype(F32)
    mid = r1.astype(BF16)
    lo = (r1 - mid.astype(F32)).astype(BF16)
    m = mask01.astype(BF16)
    dn = (((1,), (0,)), ((), ()))
    return (lax.dot_general(m, hi, dn, preferred_element_type=F32)
            + lax.dot_general(m, mid, dn, preferred_element_type=F32)
            + lax.dot_general(m, lo, dn, preferred_element_type=F32))


def _sigmoid(z):
    return 1.0 / (1.0 + jnp.exp(-z))


_GELU_C = math.sqrt(2.0 / math.pi)


def _gelu_and_grad(z):
    inner = _GELU_C * (z + 0.044715 * z * z * z)
    t = jnp.tanh(inner)
    val = 0.5 * z * (1.0 + t)
    grad = 0.5 * (1.0 + t) + 0.5 * z * (1.0 - t * t) * _GELU_C * (1.0 + 3.0 * 0.044715 * z * z)
    return val, grad


def _row_tile(n, want=512):
    t = min(want, n)
    assert n % t == 0
    return t


def _pcall(body, operands, *, name, grid, in_specs, out_specs, out_shape, scratch_shapes=(), semantics, riders=()):
    single = not isinstance(out_shape, (tuple, list))
    out_specs = (out_specs,) if single else tuple(out_specs)
    out_shape = (out_shape,) if single else tuple(out_shape)
    if not riders:
        res = _pallas(body, name=name, grid=grid, in_specs=list(in_specs), out_specs=out_specs,
                      out_shape=out_shape, scratch_shapes=list(scratch_shapes),
                      compiler_params=_cp(*semantics))(*operands)
        return (res[0] if single else res), []
    n_in, n_out, n_scr = len(in_specs), len(out_shape), len(scratch_shapes)
    ex_in = [len(ex.operands) for ex in riders]
    ex_out = [len(ex.out_shape) for ex in riders]
    ex_scr = [len(ex.scratch) for ex in riders]
    tot_in, tot_out = n_in + sum(ex_in), n_out + sum(ex_out)

    def wrapped(*refs):
        ins, outs, scr = refs[:tot_in], refs[tot_in:tot_in + tot_out], refs[tot_in + tot_out:]
        ids = [pl.program_id(d) for d in range(len(grid))]
        first = functools.reduce(lambda p, t: p & t, [i == 0 for i in ids])
        last = functools.reduce(lambda p, t: p & t, [i == n - 1 for i, n in zip(ids, grid)])
        parts, oi, oo, os_ = [], n_in, n_out, n_scr
        for k in range(len(riders)):
            parts.append((ins[oi:oi + ex_in[k]], outs[oo:oo + ex_out[k]], scr[os_:os_ + ex_scr[k]]))
            oi, oo, os_ = oi + ex_in[k], oo + ex_out[k], os_ + ex_scr[k]

        @pl.when(first)
        def _():
            for ex, part in zip(riders, parts):
                ex.start(*part)

        body(*ins[:n_in], *outs[:n_out], *scr[:n_scr])

        @pl.when(last)
        def _():
            for ex, part in zip(riders, parts):
                ex.finish(*part)

    aliases, oi, oo = {}, n_in, n_out
    all_ops, all_shapes, all_scr = list(operands), list(out_shape), list(scratch_shapes)
    for k, ex in enumerate(riders):
        aliases.update({oi + a: oo + b for a, b in ex.aliases.items()})
        oi, oo = oi + ex_in[k], oo + ex_out[k]
        all_ops += list(ex.operands)
        all_shapes += [pltpu.HBM(s.shape, s.dtype) for s in ex.out_shape]
        all_scr += list(ex.scratch)
    res = _pallas(
        wrapped, name=name, grid=grid, in_specs=list(in_specs) + [HBM_SPEC] * sum(ex_in),
        out_specs=out_specs + (HBM_SPEC,) * sum(ex_out), out_shape=tuple(all_shapes), scratch_shapes=all_scr,
        input_output_aliases=aliases, compiler_params=_cp(*(["arbitrary"] * len(grid))))(*all_ops)
    own = res[0] if single else tuple(res[:n_out])
    carried, oo = [], n_out
    for k in range(len(riders)):
        carried.append(list(res[oo:oo + ex_out[k]]))
        oo += ex_out[k]
    return own, carried


def _carried(out, carried, riders):
    return (out, carried) if riders else out


def _matmul(name, operands, *, grid, in_specs, o_spec, out_shape, out_dtype, dims, has_res=False, riders=()):
    nk = grid[2]
    assert nk == 1 or (out_dtype == F32 and not has_res)

    def body(*refs):
        if has_res:
            a_ref, b_ref, r_ref, o_ref = refs
        else:
            a_ref, b_ref, o_ref = refs
            r_ref = None
        part = lax.dot_general(a_ref[...].astype(BF16), b_ref[...].astype(BF16), (dims, ((), ())),
                               preferred_element_type=F32)
        if nk == 1:
            if r_ref is not None:
                part = part + r_ref[...]
            o_ref[...] = part.astype(o_ref.dtype)
        else:
            k = pl.program_id(2)

            @pl.when(k == 0)
            def _():
                o_ref[...] = part

            @pl.when(k > 0)
            def _():
                o_ref[...] += part

    out, carried = _pcall(body, operands, name=name, grid=grid, in_specs=in_specs, out_specs=o_spec,
                          out_shape=jax.ShapeDtypeStruct(out_shape, out_dtype),
                          semantics=("parallel", "parallel", "arbitrary"), riders=riders)
    return (out, carried) if riders else out


NN = ((1,), (0,))
NT = ((1,), (1,))
TN = ((0,), (0,))
_TN_TOKENS = 4096


def _mm_cs(name, a, w, out_dtype, riders=()):
    M, K = a.shape
    nq, _, wd = w.shape
    tm = _row_tile(M)
    return _matmul(name, (a, w), grid=(nq, M // tm, 1),
                   in_specs=[pl.BlockSpec((tm, K), lambda j, i, k: (i, 0)),
                             pl.BlockSpec((None, K, wd), lambda j, i, k: (j, 0, 0))],
                   o_spec=pl.BlockSpec((tm, wd), lambda j, i, k: (i, j)),
                   out_shape=(M, nq * wd), out_dtype=out_dtype, dims=NN, riders=riders)


def _mm_rs(name, a, w, out_dtype, res=None):
    M, K = a.shape
    N = w.shape[1]
    tm = _row_tile(M)
    tn = N
    ops = (a, w) if res is None else (a, w, res)
    in_specs = [pl.BlockSpec((tm, K), lambda i, j, k: (i, 0)),
                pl.BlockSpec((K, tn), lambda i, j, k: (0, j))]
    if res is not None:
        in_specs.append(pl.BlockSpec((tm, tn), lambda i, j, k: (i, j)))
    return _matmul(name, ops, grid=(M // tm, N // tn, 1), in_specs=in_specs,
                   o_spec=pl.BlockSpec((tm, tn), lambda i, j, k: (i, j)),
                   out_shape=(M, N), out_dtype=out_dtype, dims=NN, has_res=res is not None)


def _mm_nt_rs(name, g, w, out_dtype, riders=()):
    M, N = g.shape
    K = w.shape[0]
    to = K
    tm = _row_tile(M)
    return _matmul(name, (g, w), grid=(M // tm, K // to, 1),
                   in_specs=[pl.BlockSpec((tm, N), lambda i, j, k: (i, 0)),
                             pl.BlockSpec((to, N), lambda i, j, k: (j, 0))],
                   o_spec=pl.BlockSpec((tm, to), lambda i, j, k: (i, j)),
                   out_shape=(M, K), out_dtype=out_dtype, dims=NT, riders=riders)


def _mm_nt_cs(name, g, w, out_dtype, riders=(), stacked=False):
    M = g.shape[-2]
    nq, K, wd = w.shape
    tm = _row_tile(M, 256)

    def body(g_ref, w_ref, o_ref):
        acc = None
        for q in range(nq):
            gq = g_ref[q // 2, :, (q % 2) * wd:(q % 2 + 1) * wd] if stacked else g_ref[:, q * wd:(q + 1) * wd]
            part = _dot_nt(gq, w_ref[q])
            acc = part if acc is None else acc + part
        o_ref[...] = acc.astype(o_ref.dtype)

    g_spec = (pl.BlockSpec((2, tm, 2 * wd), lambda i: (0, i, 0)) if stacked
              else pl.BlockSpec((tm, nq * wd), lambda i: (i, 0)))
    out, carried = _pcall(
        body, (g, w), name=name, grid=(M // tm,),
        in_specs=[g_spec, pl.BlockSpec((nq, K, wd), lambda i: (0, 0, 0))],
        out_specs=pl.BlockSpec((tm, K), lambda i: (i, 0)),
        out_shape=jax.ShapeDtypeStruct((M, K), out_dtype), semantics=("parallel",), riders=riders)
    return (out, carried) if riders else out


def _mm_tn_rs(name, a, g, to, tn=512):
    T, M = a.shape
    N = g.shape[1]
    tt = _row_tile(T, _TN_TOKENS)
    tn = min(tn, N)
    return _matmul(name, (a, g), grid=(M // to, N // tn, T // tt),
                   in_specs=[pl.BlockSpec((tt, to), lambda i, j, k: (k, i)),
                             pl.BlockSpec((tt, tn), lambda i, j, k: (k, j))],
                   o_spec=pl.BlockSpec((to, tn), lambda i, j, k: (i, j)),
                   out_shape=(M, N), out_dtype=F32, dims=TN)


def _mm_tn_cs(name, a, g, nq, to, riders=(), stacked=False):
    T, M = a.shape
    wd = g.shape[-1] * (2 if stacked else 1) // nq
    tt = _row_tile(T, _TN_TOKENS)
    g_spec = (pl.BlockSpec((None, tt, wd), lambda i, j, k: (j // 2, k, j % 2)) if stacked
              else pl.BlockSpec((tt, wd), lambda i, j, k: (k, j)))
    return _matmul(name, (a, g), grid=(M // to, nq, T // tt),
                   in_specs=[pl.BlockSpec((tt, to), lambda i, j, k: (k, i)), g_spec],
                   o_spec=pl.BlockSpec((None, to, wd), lambda i, j, k: (j, i, 0)),
                   out_shape=(nq, M, wd), out_dtype=F32, dims=TN, riders=riders)


def _rms_fwd(name, x, g):
    T, D = x.shape
    tm = _row_tile(T)

    def body(x_ref, g_ref, o_ref):
        xv = x_ref[...]
        r = lax.rsqrt(jnp.mean(xv * xv, axis=-1, keepdims=True) + EPS)
        o_ref[...] = (xv * r * g_ref[...]).astype(o_ref.dtype)

    return _pallas(
        body, name=name, grid=(T // tm,),
        in_specs=[pl.BlockSpec((tm, D), lambda i: (i, 0)), pl.BlockSpec((1, D), lambda i: (0, 0))],
        out_specs=pl.BlockSpec((tm, D), lambda i: (i, 0)),
        out_shape=jax.ShapeDtypeStruct((T, D), BF16), compiler_params=_cp("parallel"),
    )(x, g)


def _rms_bwd(name, x, g, dh, dres, riders=()):
    T, D = x.shape
    tm = _row_tile(T)
    has_res = dres is not None

    def body(*refs):
        if has_res:
            x_ref, g_ref, dh_ref, dr_ref, dx_ref, dg_ref = refs
        else:
            x_ref, g_ref, dh_ref, dx_ref, dg_ref = refs

        @pl.when(pl.program_id(0) == 0)
        def _():
            dg_ref[...] = jnp.zeros_like(dg_ref)

        xv = x_ref[...]
        r = lax.rsqrt(jnp.mean(xv * xv, axis=-1, keepdims=True) + EPS)
        n = xv * r
        dhv = dh_ref[...]
        dg_ref[...] += jnp.sum(dhv * n, axis=0, keepdims=True)
        dn = dhv * g_ref[...]
        dx = r * (dn - n * jnp.mean(dn * n, axis=-1, keepdims=True))
        if has_res:
            dx = dx + dr_ref[...]
        dx_ref[...] = dx

    row = pl.BlockSpec((tm, D), lambda i: (i, 0))
    vec = pl.BlockSpec((1, D), lambda i: (0, 0))
    ops = (x, g, dh, dres) if has_res else (x, g, dh)
    out, carried = _pcall(
        body, ops, name=name, grid=(T // tm,),
        in_specs=[row, vec, row] + ([row] if has_res else []),
        out_specs=(row, vec),
        out_shape=(jax.ShapeDtypeStruct((T, D), F32), jax.ShapeDtypeStruct((1, D), F32)),
        semantics=("arbitrary",), riders=riders)
    return (out, carried) if riders else out


def _loss_head(x2, tgt, g):
    T, D = x2.shape
    tm = _row_tile(T)

    def body(x_ref, t_ref, g_ref, dx_ref, dg_ref, loss_ref):
        @pl.when(pl.program_id(0) == 0)
        def _():
            dg_ref[...] = jnp.zeros_like(dg_ref)
            loss_ref[...] = jnp.zeros_like(loss_ref)

        xv = x_ref[...]
        gv = g_ref[...]
        r = lax.rsqrt(jnp.mean(xv * xv, axis=-1, keepdims=True) + EPS)
        n = xv * r
        diff = n * gv - t_ref[...]
        loss_ref[...] += 0.5 * jnp.sum(jnp.mean(diff * diff, axis=-1, keepdims=True))
        dy = diff * (1.0 / D)
        dg_ref[...] += jnp.sum(dy * n, axis=0, keepdims=True)
        dn = dy * gv
        dx_ref[...] = r * (dn - n * jnp.mean(dn * n, axis=-1, keepdims=True))

    row = pl.BlockSpec((tm, D), lambda i: (i, 0))
    vec = pl.BlockSpec((1, D), lambda i: (0, 0))
    return _pallas(
        body, name="loss_head", grid=(T // tm,),
        in_specs=[row, row, vec],
        out_specs=(row, vec, pl.BlockSpec((8, 128), lambda i: (0, 0))),
        out_shape=(jax.ShapeDtypeStruct((T, D), F32), jax.ShapeDtypeStruct((1, D), F32),
                   jax.ShapeDtypeStruct((8, 128), F32)),
        compiler_params=_cp("arbitrary"),
    )(x2, tgt, g)


def _gmlp_pieces(zu, zv, lng, lnb, ws_ref, bs_ref):
    u, du = _gelu_and_grad(zu)
    v, dv = _gelu_and_grad(zv)
    mu = jnp.mean(v, axis=-1, keepdims=True)
    vc = v - mu
    rstd = lax.rsqrt(jnp.mean(vc * vc, axis=-1, keepdims=True) + EPS)
    vhat = vc * rstd
    vn = vhat * lng + lnb
    row = lax.broadcasted_iota(jnp.int32, (GM_CHUNK, GM_CHUNK), 0)
    col = lax.broadcasted_iota(jnp.int32, (GM_CHUNK, GM_CHUNK), 1)
    tril = row >= col
    wms, mixed = [], []
    for g in range(GM_GROUPS):
        sl = slice(g * 128, (g + 1) * 128)
        wm = jnp.where(tril, ws_ref[g], 0.0)
        wms.append(wm)
        mixed.append(_dot(wm, vn[:, sl]) + bs_ref[g])
    return u, du, dv, rstd, vhat, vn, wms, mixed, tril


def _gmlp_fwd(proj, lng, lnb, ws, bs_col):
    T = proj.shape[0]
    n = T // GM_CHUNK

    def body(zu_ref, zv_ref, lng_ref, lnb_ref, ws_ref, bs_ref, o_ref):
        u, _, _, _, _, _, _, mixed, _ = _gmlp_pieces(zu_ref[...].astype(F32), zv_ref[...].astype(F32),
                                                     lng_ref[...], lnb_ref[...],
                                                     ws_ref, bs_ref)
        for g in range(GM_GROUPS):
            sl = slice(g * 128, (g + 1) * 128)
            o_ref[:, sl] = (u[:, sl] * mixed[g]).astype(o_ref.dtype)

    vec = pl.BlockSpec((1, GM_WIDTH), lambda i: (0, 0))
    return _pallas(
        body, name="gmlp_fwd", grid=(n,),
        in_specs=[pl.BlockSpec((GM_CHUNK, 512), lambda i: (i, COL_ZU)),
                  pl.BlockSpec((GM_CHUNK, 512), lambda i: (i, COL_ZV)),
                  vec, vec,
                  pl.BlockSpec((GM_GROUPS, 128, 128), lambda i: (0, 0, 0)),
                  pl.BlockSpec((GM_GROUPS, 128, 1), lambda i: (0, 0, 0))],
        out_specs=pl.BlockSpec((GM_CHUNK, 512), lambda i: (i, 0)),
        out_shape=jax.ShapeDtypeStruct((T, GM_WIDTH), BF16), compiler_params=_cp("parallel"),
    )(proj, proj, lng, lnb, ws, bs_col)


def _gmlp_bwd(proj, d_out, lng, lnb, ws, bs_col, riders=()):
    T = proj.shape[0]
    n = T // GM_CHUNK

    def body(zu_ref, zv_ref, do_ref, lng_ref, lnb_ref, ws_ref, bs_ref,
             dz_ref, dws_ref, dbs_ref, dlng_ref, dlnb_ref, dm_acc):
        i = pl.program_id(0)

        @pl.when(i == 0)
        def _():
            dws_ref[...] = jnp.zeros_like(dws_ref)
            dlng_ref[...] = jnp.zeros_like(dlng_ref)
            dlnb_ref[...] = jnp.zeros_like(dlnb_ref)
            dm_acc[...] = jnp.zeros_like(dm_acc)

        lng_v = lng_ref[...]
        u, du, dv, rstd, vhat, vn, wms, mixed, tril = _gmlp_pieces(zu_ref[...].astype(F32), zv_ref[...].astype(F32),
                                                                  lng_v, lnb_ref[...],
                                                                  ws_ref, bs_ref)
        do = do_ref[...]
        dvn_parts = []
        for g in range(GM_GROUPS):
            sl = slice(g * 128, (g + 1) * 128)
            dog = do[:, sl]
            dz_ref[:, sl] = (dog * mixed[g] * du[:, sl]).astype(dz_ref.dtype)
            dmix = dog * u[:, sl]
            dm_acc[:, sl] += dmix
            dws_ref[g] += jnp.where(tril, _dot_nt(dmix, vn[:, sl]), 0.0)
            dvn_parts.append(_dot_tn(wms[g], dmix))
        dvn = jnp.concatenate(dvn_parts, axis=1)
        dlng_ref[...] += jnp.sum(dvn * vhat, axis=0, keepdims=True)
        dlnb_ref[...] += jnp.sum(dvn, axis=0, keepdims=True)
        dvh = dvn * lng_v
        dvv = rstd * (dvh - jnp.mean(dvh, axis=-1, keepdims=True)
                      - vhat * jnp.mean(dvh * vhat, axis=-1, keepdims=True))
        dz_ref[:, GM_WIDTH:] = (dvv * dv).astype(dz_ref.dtype)

        @pl.when(i == n - 1)
        def _():
            for g in range(GM_GROUPS):
                dbs_ref[g] = jnp.sum(dm_acc[:, g * 128:(g + 1) * 128], axis=1, keepdims=True)

    vec = pl.BlockSpec((1, GM_WIDTH), lambda i: (0, 0))
    wsp = pl.BlockSpec((GM_GROUPS, 128, 128), lambda i: (0, 0, 0))
    bsp = pl.BlockSpec((GM_GROUPS, 128, 1), lambda i: (0, 0, 0))
    return _carried(*_pcall(
        body, (proj, proj, d_out, lng, lnb, ws, bs_col), name="gmlp_bwd", grid=(n,),
        in_specs=[pl.BlockSpec((GM_CHUNK, 512), lambda i: (i, COL_ZU)),
                  pl.BlockSpec((GM_CHUNK, 512), lambda i: (i, COL_ZV)),
                  pl.BlockSpec((None, GM_CHUNK, 512), lambda i: (0, i, 0)), vec, vec, wsp, bsp],
        out_specs=(pl.BlockSpec((GM_CHUNK, 2 * GM_WIDTH), lambda i: (i, 0)), wsp, bsp, vec, vec),
        out_shape=(jax.ShapeDtypeStruct((T, 2 * GM_WIDTH), BF16),
                   jax.ShapeDtypeStruct((GM_GROUPS, 128, 128), F32), jax.ShapeDtypeStruct((GM_GROUPS, 128, 1), F32),
                   jax.ShapeDtypeStruct((1, GM_WIDTH), F32), jax.ShapeDtypeStruct((1, GM_WIDTH), F32)),
        scratch_shapes=[pltpu.VMEM((GM_CHUNK, GM_WIDTH), F32)],
        semantics=("arbitrary",), riders=riders), riders)


def _hgrn_lower_bound(lbl):
    return 1.0 / (1.0 + jnp.exp(lbl[1:2, :] - lbl[0:1, :]))


def _hgrn_gates(hq, hf, lb):
    C = HG_CHUNK
    sg = _sigmoid(hf)
    fg = lb + (1.0 - lb) * sg
    sq = _sigmoid(hq)
    row = lax.broadcasted_iota(jnp.int32, (C, C), 0)
    col = lax.broadcasted_iota(jnp.int32, (C, C), 1)
    tril = row >= col
    logf = jnp.log(fg)
    a = _dot_01(tril, logf)
    a_last = jnp.sum(logf, axis=0, keepdims=True)
    first_half = lax.broadcasted_iota(jnp.int32, logf.shape, 0) < (C // 2)
    a_mid = jnp.sum(jnp.where(first_half, logf, 0.0), axis=0, keepdims=True)
    ea, ei, eki, ekl = jnp.exp(a), jnp.exp(a - a_mid), jnp.exp(a_mid - a), jnp.exp(a_last - a)
    k = 1.0 - fg
    q = hq * sq
    qi = (q * ei).astype(BF16).astype(F32)
    ki = (k * eki).astype(BF16).astype(F32)
    return dict(sg=sg, fg=fg, sq=sq, tril=tril, ea=ea, ei=ei, eki=eki, ekl=ekl, e_last=jnp.exp(a_last),
                qe=q * ea, qi=qi, ki=ki, kl=k * ekl)


def _heads(x):
    return [x[:, h * HG_DIM:(h + 1) * HG_DIM] for h in range(HG_HEADS)]


def _hgrn_fwd(proj, lbl, gh, B, S, riders=()):
    C = HG_CHUNK
    NC = S // C
    W = HG_HEADS * HG_DIM

    def body(q_ref, f_ref, i_ref, g_ref, lbl_ref, gh_ref, o_ref, bo_ref, st_ref, state):
        @pl.when(pl.program_id(0) == 0)
        def _():
            state[...] = jnp.zeros_like(state)

        lb = _hgrn_lower_bound(lbl_ref[...])
        ghv = gh_ref[...]
        for b in range(B):
            gt = _hgrn_gates(q_ref[b].astype(F32), f_ref[b].astype(F32), lb)
            v = _heads(i_ref[b])
            qe, qi, ki, kl, e_last = (_heads(gt[n]) for n in ("qe", "qi", "ki", "kl", "e_last"))
            outs, normed = [], []
            for h in range(HG_HEADS):
                p = jnp.where(gt["tril"], _dot_nt(qi[h], ki[h]), 0.0)
                st = state[b, h]
                st_ref[b, h] = st
                o = _dot_nt(qe[h], st) + _dot(p, v[h])
                state[b, h] = st * e_last[h] + _dot_tn(v[h], kl[h])
                outs.append(o)
                normed.append(o * lax.rsqrt(jnp.mean(o * o, axis=-1, keepdims=True) + EPS) * ghv)
            o_ref[b] = jnp.concatenate(outs, axis=1)
            hg = g_ref[b].astype(F32)
            bo_ref[b] = (jnp.concatenate(normed, axis=1) * (hg * _sigmoid(hg))).astype(bo_ref.dtype)

    def col(cb):
        return pl.BlockSpec((B, C, 512), lambda c: (0, c, cb))

    tile = pl.BlockSpec((B, C, W), lambda c: (0, c, 0))
    proj3 = proj.reshape(B, S, proj.shape[-1])
    out, carried = _pcall(
        body, (proj3, proj3, proj3, proj3, lbl, gh), name="hgrn_fwd", grid=(NC,),
        in_specs=[col(COL_HQ), col(COL_HF), col(COL_HI), col(COL_HG),
                  pl.BlockSpec((2, W), lambda c: (0, 0)), pl.BlockSpec((1, HG_DIM), lambda c: (0, 0))],
        out_specs=(tile, tile, pl.BlockSpec((B, None, HG_HEADS, 128, 128), lambda c: (0, c, 0, 0, 0))),
        out_shape=(jax.ShapeDtypeStruct((B, S, W), F32), jax.ShapeDtypeStruct((B, S, W), BF16),
                   jax.ShapeDtypeStruct((B, NC, HG_HEADS, 128, 128), F32)),
        scratch_shapes=[pltpu.VMEM((B, HG_HEADS, 128, 128), F32)],
        semantics=("arbitrary",), riders=riders)
    o_h, b_out, states = out
    out = (o_h, b_out.reshape(B * S, W), states)
    return (out, carried) if riders else out


def _hgrn_bwd(proj, o_saved, states, d_out, lbl, gh, others, B, S, riders=()):
    C = HG_CHUNK
    NC = S // C
    W = HG_HEADS * HG_DIM
    d_gm, d_xq, d_gates = (t.reshape(B, S, t.shape[-1]) for t in others)
    own0 = d_gm.shape[-1]
    xq0 = own0 + 4 * W
    gates0 = xq0 + d_xq.shape[-1]

    def body(q_ref, f_ref, i_ref, g_ref, o_ref, st_ref, do_ref, lbl_ref, gh_ref, gm_ref, xq_ref, gates_ref,
             d_ref, dlbl_ref, dgh_ref, dstate, dlb_acc):
        c = pl.program_id(0)
        d_ref[:, :, :own0] = gm_ref[...]
        d_ref[:, :, xq0:gates0] = xq_ref[...]
        d_ref[:, :, gates0:] = gates_ref[...]

        def put(b, k, val):
            d_ref[b, :, own0 + k * W:own0 + (k + 1) * W] = val.astype(d_ref.dtype)

        @pl.when(c == 0)
        def _():
            dstate[...] = jnp.zeros_like(dstate)
            dgh_ref[...] = jnp.zeros_like(dgh_ref)
            dlb_acc[...] = jnp.zeros_like(dlb_acc)

        lb = _hgrn_lower_bound(lbl_ref[...])
        ghv = gh_ref[...]
        row = lax.broadcasted_iota(jnp.int32, (C, C), 0)
        colm = lax.broadcasted_iota(jnp.int32, (C, C), 1)
        triu = colm >= row
        for b in range(B):
            hq, hg = q_ref[b].astype(F32), g_ref[b].astype(F32)
            gt = _hgrn_gates(hq, f_ref[b].astype(F32), lb)
            tril = gt["tril"]
            v = _heads(i_ref[b])
            qe, qi, ki, kl, e_last = (_heads(gt[n]) for n in ("qe", "qi", "ki", "kl", "e_last"))
            sgg = _sigmoid(hg)
            don_all = do_ref[b] * (hg * sgg)
            o, don = _heads(o_ref[b]), _heads(don_all)
            d_qe, d_qi, d_ki, d_kl, dv, n_all, dal = [], [], [], [], [], [], []
            for h in range(HG_HEADS):
                r = lax.rsqrt(jnp.mean(o[h] * o[h], axis=-1, keepdims=True) + EPS)
                n = o[h] * r
                n_all.append(n)
                dgh_ref[...] += jnp.sum(don[h] * n, axis=0, keepdims=True)
                dn = don[h] * ghv
                d_o = r * (dn - n * jnp.mean(dn * n, axis=-1, keepdims=True))
                st, dst = st_ref[b, h], dstate[b, h]
                p = jnp.where(tril, _dot_nt(qi[h], ki[h]), 0.0)
                dp = jnp.where(tril, _dot_nt(d_o, v[h]), 0.0)
                d_qe.append(_dot(d_o, st))
                d_qi.append(_dot(dp, ki[h]))
                d_ki.append(_dot_tn(dp, qi[h]))
                d_kl.append(_dot(v[h], dst))
                dv.append(_dot_tn(p, d_o) + _dot_nt(kl[h], dst))
                dstate[b, h] = dst * e_last[h] + _dot_tn(d_o, qe[h])
                dal.append(jnp.sum(dst * st, axis=0, keepdims=True) * e_last[h])
            d_qe, d_qi, d_ki, d_kl, n_all, dal = (jnp.concatenate(t, axis=1)
                                                  for t in (d_qe, d_qi, d_ki, d_kl, n_all, dal))
            put(b, 3, do_ref[b] * n_all * jnp.tile(ghv, (1, HG_HEADS)) * (sgg * (1.0 + hg * (1.0 - sgg))))
            put(b, 2, jnp.concatenate(dv, axis=1))
            d_a_last = dal + jnp.sum(d_kl * gt["kl"], axis=0, keepdims=True)
            dq = d_qe * gt["ea"] + d_qi * gt["ei"]
            dk = d_ki * gt["eki"] + d_kl * gt["ekl"]
            da = d_qe * gt["qe"] + d_qi * gt["qi"] - d_ki * gt["ki"] - d_kl * gt["kl"]
            dlogf = _dot_01(triu, da) + d_a_last
            sg, sq = gt["sg"], gt["sq"]
            dfg = dlogf / gt["fg"] - dk
            put(b, 1, dfg * (1.0 - lb) * sg * (1.0 - sg))
            dlb_acc[...] += jnp.sum(dfg * (1.0 - sg), axis=0, keepdims=True)
            put(b, 0, dq * (sq * (1.0 + hq * (1.0 - sq))))

        @pl.when(c == NC - 1)
        def _():
            dlb = dlb_acc[...]
            first = lax.broadcasted_iota(jnp.int32, (2, W), 0) == 0
            dlbl_ref[...] = jnp.where(first, dlb * lb * (1.0 - lb), -dlb * lb * (1.0 - lb))

    def col(cb):
        return pl.BlockSpec((B, C, 512), lambda c: (0, NC - 1 - c, cb))

    tile = pl.BlockSpec((B, C, W), lambda c: (0, NC - 1 - c, 0))
    proj3 = proj.reshape(B, S, proj.shape[-1])
    def rows(width):
        return pl.BlockSpec((B, C, width), lambda c: (0, NC - 1 - c, 0))

    width = proj.shape[-1]
    out, carried = _pcall(
        body, (proj3, proj3, proj3, proj3, o_saved, states, d_out.reshape(3, B, S, W), lbl, gh, d_gm, d_xq, d_gates),
        name="hgrn_bwd", grid=(NC,),
        in_specs=[col(COL_HQ), col(COL_HF), col(COL_HI), col(COL_HG), tile,
                  pl.BlockSpec((B, None, HG_HEADS, 128, 128), lambda c: (0, NC - 1 - c, 0, 0, 0)),
                  pl.BlockSpec((None, B, C, W), lambda c: (1, 0, NC - 1 - c, 0)),
                  pl.BlockSpec((2, W), lambda c: (0, 0)), pl.BlockSpec((1, HG_DIM), lambda c: (0, 0)),
                  rows(d_gm.shape[-1]), rows(d_xq.shape[-1]), rows(d_gates.shape[-1])],
        out_specs=(rows(width), pl.BlockSpec((2, W), lambda c: (0, 0)), pl.BlockSpec((1, HG_DIM), lambda c: (0, 0))),
        out_shape=(jax.ShapeDtypeStruct((B, S, width), BF16), jax.ShapeDtypeStruct((2, W), F32),
                   jax.ShapeDtypeStruct((1, HG_DIM), F32)),
        scratch_shapes=[pltpu.VMEM((B, HG_HEADS, 128, 128), F32), pltpu.VMEM((1, W), F32)],
        semantics=("arbitrary",), riders=riders)
    out = (out[0].reshape(B * S, width),) + tuple(out[1:])
    return (out, carried) if riders else out


_XA_SCALE = XA_DIM ** -0.5


def _attn_probs(qh, kh):
    s = _dot_nt(qh, kh) * _XA_SCALE
    e = jnp.exp(s - jnp.max(s, axis=-1, keepdims=True))
    return e / jnp.sum(e, axis=-1, keepdims=True)


def _attn_fwd(proj, kv, B, S):
    T = B * S
    tq = _row_tile(S)
    nq = S // tq
    W = XA_HEADS * XA_DIM

    def body(q_ref, kv_ref, o_ref):
        for h in range(XA_HEADS):
            sl = slice(h * 128, (h + 1) * 128)
            p = _attn_probs(q_ref[:, sl], kv_ref[:, sl])
            o_ref[:, sl] = _dot(p, kv_ref[:, W + h * 128:W + (h + 1) * 128]).astype(o_ref.dtype)

    return _pallas(
        body, name="attn_fwd", grid=(B, nq),
        in_specs=[pl.BlockSpec((tq, 512), lambda b, i: (b * nq + i, COL_XQ)),
                  pl.BlockSpec((MEM_LEN, 2 * W), lambda b, i: (b, 0))],
        out_specs=pl.BlockSpec((tq, W), lambda b, i: (b * nq + i, 0)),
        out_shape=jax.ShapeDtypeStruct((T, W), BF16), compiler_params=_cp("parallel", "parallel"),
    )(proj, kv)


def _attn_bwd(proj, kv, d_out, B, S):
    T = B * S
    tq = _row_tile(S)
    nq = S // tq
    W = XA_HEADS * XA_DIM

    def body(q_ref, kv_ref, do_ref, dq_ref, dkv_ref):
        @pl.when(pl.program_id(1) == 0)
        def _():
            dkv_ref[...] = jnp.zeros_like(dkv_ref)

        for h in range(XA_HEADS):
            sl = slice(h * 128, (h + 1) * 128)
            slv = slice(W + h * 128, W + (h + 1) * 128)
            qh = q_ref[:, sl]
            kh = kv_ref[:, sl]
            p = _attn_probs(qh, kh)
            dc = do_ref[:, sl]
            dp = _dot_nt(dc, kv_ref[:, slv])
            ds = p * (dp - jnp.sum(dp * p, axis=-1, keepdims=True)) * _XA_SCALE
            dq_ref[:, sl] = _dot(ds, kh).astype(dq_ref.dtype)
            dkv_ref[:, sl] += _dot_tn(ds, qh)
            dkv_ref[:, slv] += _dot_tn(p, dc)

    kvspec = pl.BlockSpec((MEM_LEN, 2 * W), lambda b, i: (b, 0))
    tile = pl.BlockSpec((tq, W), lambda b, i: (b * nq + i, 0))
    return _pallas(
        body, name="attn_bwd", grid=(B, nq),
        in_specs=[pl.BlockSpec((tq, 512), lambda b, i: (b * nq + i, COL_XQ)), kvspec,
                  pl.BlockSpec((None, tq, W), lambda b, i: (2, b * nq + i, 0))],
        out_specs=(tile, kvspec),
        out_shape=(jax.ShapeDtypeStruct((T, W), BF16), jax.ShapeDtypeStruct((B * MEM_LEN, 2 * W), F32)),
        compiler_params=_cp("parallel", "arbitrary"),
    )(proj, kv, d_out)


_MERGE_TM = 256
_GATE_W = 512


def _gate_specs(tm):
    base = COL_GATE0 // _GATE_W
    return [pl.BlockSpec((tm, _GATE_W), functools.partial(lambda i, k: (i, base + k), k=k)) for k in range(6)]


def _merge_fwd(a_out, b_out, c_out, wb, proj, riders=()):
    T = a_out.shape[0]
    tm = _row_tile(T, _MERGE_TM)
    nq, _, wd = wb.shape
    per_half = _GATE_W // wd

    def body(a_ref, b_ref, c_ref, w_ref, *rest):
        gates, (m_ref, up_ref) = rest[:6], rest[6:]
        for hf in range(2):
            cols = slice(hf * _GATE_W, (hf + 1) * _GATE_W)
            acc = None
            for n, br in enumerate((a_ref, b_ref, c_ref)):
                x = br[...]
                up = jnp.concatenate([_dot(x, w_ref[per_half * hf + j, n * BR_WIDTH:(n + 1) * BR_WIDTH, :])
                                      for j in range(per_half)], axis=1)
                up_ref[n, :, cols] = up.astype(up_ref.dtype)
                term = _sigmoid(gates[2 * n + hf][...].astype(F32)) * up
                acc = term if acc is None else acc + term
            m_ref[:, cols] = acc.astype(m_ref.dtype)

    br_spec = pl.BlockSpec((tm, BR_WIDTH), lambda i: (i, 0))
    return _carried(*_pcall(
        body, (a_out, b_out, c_out, wb, *([proj] * 6)), name="merge_fwd", grid=(T // tm,),
        in_specs=[br_spec, br_spec, br_spec,
                  pl.BlockSpec((nq, 3 * BR_WIDTH, wd), lambda i: (0, 0, 0))] + _gate_specs(tm),
        out_specs=(pl.BlockSpec((tm, D_MODEL), lambda i: (i, 0)), pl.BlockSpec((3, tm, D_MODEL), lambda i: (0, i, 0))),
        out_shape=(jax.ShapeDtypeStruct((T, D_MODEL), BF16), jax.ShapeDtypeStruct((3, T, D_MODEL), BF16)),
        semantics=("parallel",), riders=riders), riders)


def _branch_bwd_act(d_ups, wb, riders=()):
    _, T, D = d_ups.shape
    nq, _, wd = wb.shape
    tm = _row_tile(T)

    def body(d_ref, w_ref, o_ref):
        acc = None
        for q in range(nq):
            part = _dot_nt(d_ref[:, q * wd:(q + 1) * wd], w_ref[q])
            acc = part if acc is None else acc + part
        o_ref[...] = acc

    return _carried(*_pcall(
        body, (d_ups, wb), name="d_branch", grid=(3, T // tm),
        in_specs=[pl.BlockSpec((None, tm, D), lambda n, i: (n, i, 0)),
                  pl.BlockSpec((nq, BR_WIDTH, wd), lambda n, i: (0, n, 0))],
        out_specs=pl.BlockSpec((None, tm, BR_WIDTH), lambda n, i: (n, i, 0)),
        out_shape=jax.ShapeDtypeStruct((3, T, BR_WIDTH), F32), semantics=("parallel", "parallel"),
        riders=riders), riders)


def _branch_bwd_weight(name, br, d_ups, n):
    T = br.shape[0]
    D = d_ups.shape[2]
    wd = D // N_CHIPS
    tt = _row_tile(T, _TN_TOKENS)

    def body(b_ref, d_ref, o_ref):
        k = pl.program_id(0)
        for q in range(N_CHIPS):
            part = _dot_tn(b_ref[...], d_ref[:, q * wd:(q + 1) * wd])

            @pl.when(k == 0)
            def _():
                o_ref[q] = part

            @pl.when(k > 0)
            def _():
                o_ref[q] += part

    return _pallas(
        body, name=name, grid=(T // tt,),
        in_specs=[pl.BlockSpec((tt, BR_WIDTH), lambda k: (k, 0)),
                  pl.BlockSpec((None, tt, D), lambda k: (n, k, 0))],
        out_specs=pl.BlockSpec((N_CHIPS, BR_WIDTH, wd), lambda k: (0, 0, 0)),
        out_shape=jax.ShapeDtypeStruct((N_CHIPS, BR_WIDTH, wd), F32), compiler_params=_cp("arbitrary"),
    )(br, d_ups)


def _merge_bwd(d_merged, ups, proj, riders=()):
    T = d_merged.shape[0]
    tm = _row_tile(T, _MERGE_TM)

    def body(dm_ref, up_ref, *rest):
        gates, (dup_ref, dg_ref) = rest[:6], rest[6:]
        for hf in range(2):
            cols = slice(hf * _GATE_W, (hf + 1) * _GATE_W)
            dm = dm_ref[:, cols]
            for n in range(3):
                gate = _sigmoid(gates[2 * n + hf][...].astype(F32))
                dup_ref[n, :, cols] = (dm * gate).astype(dup_ref.dtype)
                dg_ref[:, n * D_MODEL + hf * _GATE_W:n * D_MODEL + (hf + 1) * _GATE_W] = (
                    dm * up_ref[n, :, cols].astype(F32) * gate * (1.0 - gate)).astype(dg_ref.dtype)

    tile = pl.BlockSpec((tm, D_MODEL), lambda i: (i, 0))
    tile3 = pl.BlockSpec((3, tm, D_MODEL), lambda i: (0, i, 0))
    return _carried(*_pcall(
        body, (d_merged, ups, *([proj] * 6)), name="merge_bwd", grid=(T // tm,),
        in_specs=[tile, tile3] + _gate_specs(tm),
        out_specs=(tile3, pl.BlockSpec((tm, 3 * D_MODEL), lambda i: (i, 0))),
        out_shape=(jax.ShapeDtypeStruct((3, T, D_MODEL), BF16), jax.ShapeDtypeStruct((T, 3 * D_MODEL), BF16)),
        semantics=("parallel",), riders=riders), riders)


_CONV_TF = D_FF // 2
_CONV_TS = 256
_HALO = 16


def _conv_fwd(ab, cw, cb, B, S):
    T = B * S
    ts = _row_tile(S, _CONV_TS)
    tf = _CONV_TF
    nb = D_FF // tf
    tps = S // ts
    hb = ts // _HALO

    def body(a_ref, p_ref, b_ref, w_ref, cb_ref, o_ref):
        start = (pl.program_id(0) % tps) == 0
        a = a_ref[...].astype(F32)
        prev = jnp.where(start, 0.0, p_ref[...].astype(F32))
        ext = jnp.concatenate([prev, a], axis=0)
        a1 = pltpu.roll(ext, 1, 0)[_HALO:, :]
        a2 = pltpu.roll(ext, 2, 0)[_HALO:, :]
        ac = cb_ref[...] + w_ref[0] * a2 + w_ref[1] * a1 + w_ref[2] * a
        o_ref[...] = (ac * _sigmoid(ac) * b_ref[...].astype(F32)).astype(o_ref.dtype)

    return _pallas(
        body, name="conv_fwd", grid=(T // ts, nb),
        in_specs=[pl.BlockSpec((ts, tf), lambda i, j: (i, j)),
                  pl.BlockSpec((_HALO, tf), lambda i, j: (jnp.maximum(i * hb - 1, 0), j)),
                  pl.BlockSpec((ts, tf), lambda i, j: (i, j + nb)),
                  pl.BlockSpec((3, 1, tf), lambda i, j: (0, 0, j)),
                  pl.BlockSpec((1, tf), lambda i, j: (0, j))],
        out_specs=pl.BlockSpec((ts, tf), lambda i, j: (i, j)),
        out_shape=jax.ShapeDtypeStruct((T, D_FF), BF16), compiler_params=_cp("parallel", "parallel"),
    )(ab, ab, ab, cw, cb)


def _conv_bwd(ab, d_ff, cw, cb, B, S, riders=()):
    T = B * S
    ts = _row_tile(S, _CONV_TS)
    tf = _CONV_TF
    nb = D_FF // tf
    tps = S // ts
    hb = ts // _HALO
    last_h = T // _HALO - 1
    n_ext = ts + _HALO

    def body(a_ref, ap_ref, an_ref, b_ref, bn_ref, d_ref, dn_ref, w_ref, cb_ref, dab_ref, dw_ref, dcb_ref):
        i = pl.program_id(1)

        @pl.when(i == 0)
        def _():
            dw_ref[...] = jnp.zeros_like(dw_ref)
            dcb_ref[...] = jnp.zeros_like(dcb_ref)

        start = (i % tps) == 0
        end = (i % tps) == tps - 1
        a = a_ref[...].astype(F32)
        ext = jnp.concatenate([jnp.where(start, 0.0, ap_ref[...].astype(F32)), a, an_ref[...].astype(F32)], axis=0)
        r1 = pltpu.roll(ext, 1, 0)[_HALO:, :]
        r2 = pltpu.roll(ext, 2, 0)[_HALO:, :]
        ac = cb_ref[...] + w_ref[0] * r2 + w_ref[1] * r1 + w_ref[2] * ext[_HALO:, :]
        sg = _sigmoid(ac)
        d_e = jnp.concatenate([d_ref[...].astype(F32), jnp.where(end, 0.0, dn_ref[...].astype(F32))], axis=0)
        b_e = jnp.concatenate([b_ref[...].astype(F32), bn_ref[...].astype(F32)], axis=0)
        dab_ref[1] = (d_e[:ts, :] * (ac * sg)[:ts, :]).astype(dab_ref.dtype)
        dac = d_e * b_e * sg * (1.0 + ac * (1.0 - sg))
        u1 = pltpu.roll(dac, n_ext - 1, 0)[:ts, :]
        u2 = pltpu.roll(dac, n_ext - 2, 0)[:ts, :]
        dac0 = dac[:ts, :]
        dab_ref[0] = (w_ref[2] * dac0 + w_ref[1] * u1 + w_ref[0] * u2).astype(dab_ref.dtype)
        dcb_ref[...] += jnp.sum(dac0, axis=0, keepdims=True)
        dw_ref[2] += jnp.sum(dac0 * a, axis=0, keepdims=True)
        dw_ref[1] += jnp.sum(dac0 * r1[:ts, :], axis=0, keepdims=True)
        dw_ref[0] += jnp.sum(dac0 * r2[:ts, :], axis=0, keepdims=True)

    def cur(off):
        return pl.BlockSpec((ts, tf), lambda j, i: (i, j + off))

    def nxt(off):
        return pl.BlockSpec((_HALO, tf), lambda j, i: (jnp.minimum((i + 1) * hb, last_h), j + off))

    return _carried(*_pcall(
        body, (ab, ab, ab, ab, ab, d_ff, d_ff, cw, cb), name="conv_bwd", grid=(nb, T // ts),
        in_specs=[cur(0), pl.BlockSpec((_HALO, tf), lambda j, i: (jnp.maximum(i * hb - 1, 0), j)), nxt(0),
                  cur(nb), nxt(nb), cur(0), nxt(0),
                  pl.BlockSpec((3, 1, tf), lambda j, i: (0, 0, j)), pl.BlockSpec((1, tf), lambda j, i: (0, j))],
        out_specs=(pl.BlockSpec((2, ts, tf), lambda j, i: (0, i, j)), pl.BlockSpec((3, 1, tf), lambda j, i: (0, 0, j)),
                   pl.BlockSpec((1, tf), lambda j, i: (0, j))),
        out_shape=(jax.ShapeDtypeStruct((2, T, D_FF), BF16),
                   jax.ShapeDtypeStruct((3, 1, D_FF), F32), jax.ShapeDtypeStruct((1, D_FF), F32)),
        semantics=("parallel", "arbitrary"), riders=riders), riders)


def _local_step(x, mem, tgt, p, comm, B, S):
    g = {}
    h = _rms_fwd("norm1", x, p["norm1_g"])
    proj = comm.carry("in_proj", lambda r: _mm_cs("in_proj", h, comm.w("w_in"), BF16, riders=r))
    a_out = _gmlp_fwd(proj, p["ln_v_g"], p["ln_v_b"], p["w_spatial"], p["b_spatial"])
    o_h, b_out, states = comm.carry(
        "hgrn_fwd", lambda r: _hgrn_fwd(proj, p["lb_logits"], p["hgrn_norm_g"], B, S, riders=r))
    memn = _rms_fwd("mem_norm", mem, p["mem_norm_g"])
    kv = _mm_rs("mem_kv", memn, comm.w("w_mem_kv"), F32)
    c_out = _attn_fwd(proj, kv, B, S)
    merged, ups = comm.carry(
        "merge_fwd", lambda r: _merge_fwd(a_out, b_out, c_out, comm.w("w_branch"), proj, riders=r))
    x1 = _mm_rs("out_proj", merged, comm.w("w_out"), F32, res=x)
    h2 = _rms_fwd("norm2", x1, p["norm2_g"])
    ab = comm.carry("up_proj", lambda r: _mm_cs("up_proj", h2, comm.w("w_up"), BF16, riders=r))
    conv_w = comm.w("conv_w")
    ff = _conv_fwd(ab, conv_w, p["conv_b"], B, S)
    x2 = _mm_rs("down_proj", ff, comm.w("w_down"), F32, res=x1)
    dx2, g["final_g"], loss = _loss_head(x2, tgt, p["final_g"])

    comm.grad("w_down", _mm_tn_rs("g_w_down", ff, dx2, to=D_FF // 2))
    d_ff = comm.carry("d_ff", lambda r: _mm_nt_rs("d_ff", dx2, comm.w("w_down"), BF16, riders=r))
    d_ab, g["conv_w"], g["conv_b"] = comm.carry(
        "conv_bwd", lambda r: _conv_bwd(ab, d_ff, conv_w, p["conv_b"], B, S, riders=r))
    comm.grad("w_up", _mm_tn_cs("g_w_up", h2, d_ab, N_CHIPS, to=512, stacked=True))
    d_h2 = comm.carry("d_h2", lambda r: _mm_nt_cs("d_h2", d_ab, comm.w("w_up"), F32, riders=r, stacked=True))
    d_x1, g["norm2_g"] = _rms_bwd("norm2_bwd", x1, p["norm2_g"], d_h2, dx2)
    comm.grad("w_out", _mm_tn_rs("g_w_out", merged, d_x1, to=512))
    d_merged = _mm_nt_rs("d_merged", d_x1, comm.w("w_out"), F32)
    d_ups, d_gates = comm.carry("merge_bwd", lambda r: _merge_bwd(d_merged, ups, proj, riders=r))

    d_br = comm.carry("d_branch", lambda r: _branch_bwd_act(d_ups, comm.w("w_branch"), riders=r))
    comm.grad("w_branch", jnp.concatenate(
        [_branch_bwd_weight("g_w_branch%d" % n, br, d_ups, n) for n, br in enumerate((a_out, b_out, c_out))],
        axis=1))

    d_gm, g["w_spatial"], g["b_spatial"], g["ln_v_g"], g["ln_v_b"] = comm.carry(
        "gmlp_bwd", lambda r: _gmlp_bwd(proj, d_br, p["ln_v_g"], p["ln_v_b"], p["w_spatial"], p["b_spatial"],
                                        riders=r))
    d_xq, d_kv = _attn_bwd(proj, kv, d_br, B, S)
    comm.grad("w_mem_kv", _mm_tn_rs("g_w_mem_kv", memn, d_kv, to=512))
    d_memn = _mm_nt_rs("d_memn", d_kv, comm.w("w_mem_kv"), F32)
    _, g["mem_norm_g"] = _rms_bwd("mem_norm_bwd", mem, p["mem_norm_g"], d_memn, None)
    d_proj, g["lb_logits"], g["hgrn_norm_g"] = comm.carry(
        "hgrn_bwd", lambda r: _hgrn_bwd(proj, o_h, states, d_br, p["lb_logits"], p["hgrn_norm_g"],
                                        (d_gm, d_xq, d_gates), B, S, riders=r))
    comm.small_grads([g[n].reshape(_SMALL_SHAPE[n]) for n in _SMALL_EARLY] + [loss])
    comm.grad("w_in", comm.carry("g_w_in", lambda r: _mm_tn_cs("g_w_in", h, d_proj, N_CHIPS, to=512, riders=r)))
    d_h = comm.carry("d_h", lambda r: _mm_nt_cs("d_h", d_proj, comm.w("w_in"), F32, riders=r))
    grad_x, g["norm1_g"] = _rms_bwd("norm1_bwd", x, p["norm1_g"], d_h, d_x1)
    return loss, grad_x, g


HBM_SPEC = pl.BlockSpec(memory_space=pltpu.HBM)


def _place():
    x, y, c = lax.axis_index("x"), lax.axis_index("y"), lax.axis_index("c")
    other_chips = [(1 - x, y), (x, 1 - y), (1 - x, 1 - y)]
    return x, y, c, other_chips


def _remote(src, dst, send_sem, recv_sem, dev):
    return pltpu.make_async_remote_copy(src_ref=src, dst_ref=dst, send_sem=send_sem, recv_sem=recv_sem,
                                        device_id=dev, device_id_type=MESH_ID)


class _Exchange:
    def __init__(self, operands, out_shape, aliases, scratch, start, finish):
        self.operands, self.out_shape, self.aliases, self.scratch = operands, out_shape, aliases, scratch
        self.start, self.finish = start, finish


def _run_exchanges(name, exs):
    n_in = [len(ex.operands) for ex in exs]
    n_out = [len(ex.out_shape) for ex in exs]
    n_scr = [len(ex.scratch) for ex in exs]

    def body(*refs):
        ins, outs, scr = refs[:sum(n_in)], refs[sum(n_in):sum(n_in) + sum(n_out)], refs[sum(n_in) + sum(n_out):]
        parts, oi, oo, os_ = [], 0, 0, 0
        for k in range(len(exs)):
            parts.append((ins[oi:oi + n_in[k]], outs[oo:oo + n_out[k]], scr[os_:os_ + n_scr[k]]))
            oi, oo, os_ = oi + n_in[k], oo + n_out[k], os_ + n_scr[k]
        for ex, part in zip(exs, parts):
            ex.start(*part)
        for ex, part in zip(exs, parts):
            ex.finish(*part)

    aliases, ops, shapes, scratch, oi, oo = {}, [], [], [], 0, 0
    for k, ex in enumerate(exs):
        aliases.update({oi + a: oo + b for a, b in ex.aliases.items()})
        oi, oo = oi + n_in[k], oo + n_out[k]
        ops += list(ex.operands)
        shapes += [pltpu.HBM(s.shape, s.dtype) for s in ex.out_shape]
        scratch += list(ex.scratch)
    res = _pallas(
        body, name=name, in_specs=[HBM_SPEC] * len(ops), out_specs=(HBM_SPEC,) * len(shapes), out_shape=tuple(shapes),
        input_output_aliases=aliases, scratch_shapes=scratch,
    )(*ops)
    out, oo = [], 0
    for k in range(len(exs)):
        out.append(list(res[oo:oo + n_out[k]]))
        oo += n_out[k]
    return out


def _ex_all_gather(slabs, halved, part=(0, 1)):
    n = len(slabs)

    def rows(a, cc):
        if not halved[a]:
            return slice(None)
        pr = slabs[a].shape[1] // part[1]
        return pl.ds(part[0] * pr + cc * (pr // 2), pr // 2)

    def ici(bufs, scr, a, j, chip, c, mine):
        px, py = chip
        x, y, _, _ = _place()
        qs = 2 * x + y if mine else 2 * px + py
        piece = bufs[a].at[qs, rows(a, c)]
        return _remote(piece, piece, scr[0].at[3 * a + j], scr[1].at[3 * a + j], (px, py, c))

    def d2d(bufs, scr, a, j, chip, cc):
        px, py = chip
        x, y, c, _ = _place()
        piece = bufs[a].at[2 * px + py, rows(a, cc)]
        return _remote(piece, piece, scr[2].at[3 * a + j], scr[3].at[3 * a + j], (x, y, 1 - c))

    def start(ins, outs, scr):
        _, _, c, chips = _place()
        for j, chip in enumerate(chips):
            for a in range(n):
                ici(outs, scr, a, j, chip, c, True).start()

    def finish(ins, outs, scr):
        _, _, c, chips = _place()
        for j, chip in enumerate(chips):
            for a in range(n):
                ici(outs, scr, a, j, chip, c, False).wait_recv()
                if halved[a]:
                    d2d(outs, scr, a, j, chip, c).start()
        for j, chip in enumerate(chips):
            for a in range(n):
                if halved[a]:
                    d2d(outs, scr, a, j, chip, 1 - c).wait_recv()
        for j, chip in enumerate(chips):
            for a in range(n):
                ici(outs, scr, a, j, chip, c, True).wait_send()
                if halved[a]:
                    d2d(outs, scr, a, j, chip, c).wait_send()

    return _Exchange(list(slabs), [jax.ShapeDtypeStruct(s.shape, s.dtype) for s in slabs],
                     {a: a for a in range(n)}, [pltpu.SemaphoreType.DMA((3 * n,))] * 4, start, finish)


def _ex_to_sibling(grads):
    n = len(grads)

    def copy(ins, outs, scr, a):
        x, y, c, _ = _place()
        hr = grads[a].shape[1] // 2
        return _remote(ins[a].at[:, pl.ds((1 - c) * hr, hr), :], outs[a], scr[0].at[a], scr[1].at[a], (x, y, 1 - c))

    def start(ins, outs, scr):
        for a in range(n):
            copy(ins, outs, scr, a).start()

    def finish(ins, outs, scr):
        for a in range(n):
            copy(ins, outs, scr, a).wait()

    out_shape = [jax.ShapeDtypeStruct((g.shape[0], g.shape[1] // 2, g.shape[2]), g.dtype) for g in grads]
    return _Exchange(list(grads), out_shape, {}, [pltpu.SemaphoreType.DMA((n,))] * 2, start, finish)


def _ex_to_owner(parts, part=(0, 1), landing=None):
    n = len(parts)

    def copy(ins, outs, scr, a, j, chip):
        _, _, c, _ = _place()
        px, py = chip
        pr = parts[a].shape[1] // part[1]
        rows = pl.ds(part[0] * pr, pr)
        return _remote(ins[a].at[2 * px + py, rows], outs[a].at[j, rows], scr[0].at[3 * a + j],
                       scr[1].at[3 * a + j], (px, py, c))

    def start(ins, outs, scr):
        for j, chip in enumerate(_place()[3]):
            for a in range(n):
                copy(ins, outs, scr, a, j, chip).start()

    def finish(ins, outs, scr):
        for j, chip in enumerate(_place()[3]):
            for a in range(n):
                copy(ins, outs, scr, a, j, chip).wait()

    out_shape = [jax.ShapeDtypeStruct((3,) + p.shape[1:], p.dtype) for p in parts]
    operands, aliases = list(parts), {}
    if landing is not None:
        operands, aliases = operands + list(landing), {n + a: a for a in range(n)}
    return _Exchange(operands, out_shape, aliases, [pltpu.SemaphoreType.DMA((3 * n,))] * 2, start, finish)


def _ex_share_halves(bufs):
    n = len(bufs)

    def copy(outs, scr, a, cc):
        x, y, c, _ = _place()
        hr = bufs[a].shape[0] // 2
        piece = outs[a].at[pl.ds(cc * hr, hr), :]
        return _remote(piece, piece, scr[0].at[a], scr[1].at[a], (x, y, 1 - c))

    def start(ins, outs, scr):
        c = _place()[2]
        for a in range(n):
            copy(outs, scr, a, c).start()

    def finish(ins, outs, scr):
        c = _place()[2]
        for a in range(n):
            copy(outs, scr, a, c).wait_send()
            copy(outs, scr, a, 1 - c).wait_recv()

    return _Exchange(list(bufs), [jax.ShapeDtypeStruct(b.shape, b.dtype) for b in bufs], {a: a for a in range(n)},
                     [pltpu.SemaphoreType.DMA((n,))] * 2, start, finish)


def _ex_gather_small(arrs):
    n = len(arrs)

    def peer_of(m):
        x, y, c, _ = _place()
        return (1 - x if m & 4 else x, 1 - y if m & 2 else y, 1 - c if m & 1 else c)

    def start(ins, outs, scr):
        x, y, c, _ = _place()
        for m in range(1, N_DEV):
            for a in range(n):
                k = (N_DEV - 1) * a + m - 1
                _remote(ins[a], outs[a].at[4 * x + 2 * y + c], scr[0].at[k], scr[1].at[k], peer_of(m)).start()

    def finish(ins, outs, scr):
        for m in range(1, N_DEV):
            px, py, pc = peer_of(m)
            for a in range(n):
                k = (N_DEV - 1) * a + m - 1
                slot = outs[a].at[4 * px + 2 * py + pc]
                cp = _remote(ins[a], slot, scr[0].at[k], scr[1].at[k], (px, py, pc))
                cp.wait_send()
                cp.wait_recv()

    slots = [jnp.zeros((N_DEV,) + a.shape, a.dtype) for a in arrs]
    out_shape = [jax.ShapeDtypeStruct(s.shape, s.dtype) for s in slots]
    return _Exchange(list(arrs) + slots, out_shape, {n + a: a for a in range(n)},
                     [pltpu.SemaphoreType.DMA(((N_DEV - 1) * n,))] * 2, start, finish)


def _div_tile(n, want):
    best = None
    for t in range(8, min(n, want) + 1, 8):
        if n % t == 0:
            best = t
    assert best is not None, n
    return best


def _cast_into_slab(name, w, place, dtype):
    r, cc = w.shape
    tr = r if r * cc <= 128 * 1024 else _div_tile(r, 256)

    def body(s_ref, w_ref, o_ref):
        o_ref[...] = w_ref[...].astype(o_ref.dtype)

    return _pallas(
        body, name=name,
        grid_spec=pltpu.PrefetchScalarGridSpec(
            num_scalar_prefetch=1, grid=(r // tr,),
            in_specs=[pl.BlockSpec((tr, cc), lambda i, s: (i, 0))],
            out_specs=pl.BlockSpec((None, tr, cc), lambda i, s: (s[0], i, 0))),
        out_shape=jax.ShapeDtypeStruct((N_CHIPS, r, cc), dtype), compiler_params=_cp("parallel"),
    )(place, w)


def _add_half(name, g, rcv, place):
    nq, r, cc = g.shape
    hr = r // 2

    def body(s_ref, g_ref, r_ref, o_ref):
        o_ref[...] = (g_ref[...] + r_ref[...]).astype(o_ref.dtype)

    spec = pl.BlockSpec((None, hr, cc), lambda i, s: (i, 0, 0))
    return _pallas(
        body, name=name,
        grid_spec=pltpu.PrefetchScalarGridSpec(
            num_scalar_prefetch=1, grid=(nq,),
            in_specs=[pl.BlockSpec((None, hr, cc), lambda i, s: (i, s[1], 0)), spec], out_specs=spec),
        out_shape=jax.ShapeDtypeStruct((nq, hr, cc), BF16), compiler_params=_cp("parallel"),
    )(place, g, rcv)


def _sum_owner(name, part, rcv, place):
    _, hr, cc = part.shape
    tr = _div_tile(hr, 128)
    nb = hr // tr

    def body(s_ref, p_ref, r_ref, o_ref):
        o_ref[...] = ((p_ref[...].astype(F32) + r_ref[0].astype(F32)) + r_ref[1].astype(F32)) + r_ref[2].astype(F32)

    return _pallas(
        body, name=name,
        grid_spec=pltpu.PrefetchScalarGridSpec(
            num_scalar_prefetch=1, grid=(nb,),
            in_specs=[pl.BlockSpec((None, tr, cc), lambda i, s: (s[0], i, 0)),
                      pl.BlockSpec((3, tr, cc), lambda i, s: (0, i, 0))],
            out_specs=pl.BlockSpec((tr, cc), lambda i, s: (s[1] * nb + i, 0))),
        out_shape=jax.ShapeDtypeStruct((2 * hr, cc), F32), compiler_params=_cp("parallel"),
    )(place, part, rcv)


def _sum_small(gathered, local, place):
    n = len(gathered)

    def body(s_ref, *refs):
        g_refs, l_refs, o_refs = refs[:n], refs[n:2 * n], refs[2 * n:]
        me = s_ref[2]
        for g_ref, l_ref, o_ref in zip(g_refs, l_refs, o_refs):
            acc = None
            for d in range(N_DEV):
                term = jnp.where(me == d, l_ref[...], g_ref[d])
                acc = term if acc is None else acc + term
            o_ref[...] = acc

    def whole(shape):
        return pl.BlockSpec(shape, lambda i, s, nd=len(shape): (0,) * nd)

    return _pallas(
        body, name="sum_small",
        grid_spec=pltpu.PrefetchScalarGridSpec(
            num_scalar_prefetch=1, grid=(1,),
            in_specs=[whole(g.shape) for g in gathered] + [whole(a.shape) for a in local],
            out_specs=tuple(whole(a.shape) for a in local)),
        out_shape=tuple(jax.ShapeDtypeStruct(a.shape, a.dtype) for a in local), compiler_params=_cp("arbitrary"),
    )(place, *gathered, *local)


def _adamw(name, w, g, m, v):
    r, cc = w.shape
    tr = r if r * cc <= 128 * 1024 else _div_tile(r, 256)

    def body(w_ref, g_ref, m_ref, v_ref, d_ref, mo_ref, vo_ref):
        gv = g_ref[...]
        mn = ADAM_B1 * m_ref[...] + (1.0 - ADAM_B1) * gv
        vn = ADAM_B2 * v_ref[...] + (1.0 - ADAM_B2) * (gv * gv)
        m_hat = mn / (1.0 - ADAM_B1 ** ADAM_STEP)
        v_hat = vn / (1.0 - ADAM_B2 ** ADAM_STEP)
        d_ref[...] = -ADAM_LR * (m_hat / (jnp.sqrt(v_hat) + ADAM_EPS) + ADAM_WD * w_ref[...])
        mo_ref[...] = mn
        vo_ref[...] = vn

    spec = pl.BlockSpec((tr, cc), lambda i: (i, 0))
    sd = jax.ShapeDtypeStruct((r, cc), F32)
    return _pallas(
        body, name=name, grid=(r // tr,), in_specs=[spec] * 4, out_specs=(spec,) * 3, out_shape=(sd,) * 3,
        compiler_params=_cp("parallel"),
    )(w, g, m, v)


_BIG = ("w_in", "w_up", "w_branch", "w_mem_kv", "w_out", "w_down")
_BIG_SHARD_SHAPE = {"w_in": (1024, 1664), "w_up": (1024, 1408), "w_branch": (1536, 256),
                    "w_mem_kv": (256, 1024), "w_out": (256, 1024), "w_down": (704, 1024)}
_SMALL_SHAPE = {"norm1_g": (1, D_MODEL), "ln_v_g": (1, GM_WIDTH), "ln_v_b": (1, GM_WIDTH),
                "w_spatial": (GM_GROUPS * GM_CHUNK, GM_CHUNK), "b_spatial": (GM_GROUPS, GM_CHUNK),
                "lb_logits": (2, HG_HEADS * HG_DIM), "hgrn_norm_g": (1, HG_DIM), "mem_norm_g": (1, D_MODEL),
                "norm2_g": (1, D_MODEL), "conv_w": (3, D_FF), "conv_b": (1, D_FF), "final_g": (1, D_MODEL)}
_SMALL_EARLY = tuple(n for n in _SMALL_SHAPE if n != "norm1_g")
_PARAM_ORDER = ("norm1_g", "w_in", "ln_v_g", "ln_v_b", "w_spatial", "b_spatial", "lb_logits", "hgrn_norm_g",
                "mem_norm_g", "w_mem_kv", "w_branch", "w_out", "norm2_g", "w_up", "conv_w", "conv_b", "w_down",
                "final_g")


def _adamw_small(ws, gs, ms, vs):
    n = len(ws)

    def body(*refs):
        w_refs, g_refs, m_refs, v_refs = refs[:n], refs[n:2 * n], refs[2 * n:3 * n], refs[3 * n:4 * n]
        d_refs, mo_refs, vo_refs = refs[4 * n:5 * n], refs[5 * n:6 * n], refs[6 * n:]
        for k in range(n):
            gv = g_refs[k][...]
            mn = ADAM_B1 * m_refs[k][...] + (1.0 - ADAM_B1) * gv
            vn = ADAM_B2 * v_refs[k][...] + (1.0 - ADAM_B2) * (gv * gv)
            m_hat = mn / (1.0 - ADAM_B1 ** ADAM_STEP)
            v_hat = vn / (1.0 - ADAM_B2 ** ADAM_STEP)
            d_refs[k][...] = -ADAM_LR * (m_hat / (jnp.sqrt(v_hat) + ADAM_EPS) + ADAM_WD * w_refs[k][...])
            mo_refs[k][...] = mn
            vo_refs[k][...] = vn

    specs = [pl.BlockSpec(a.shape, lambda i: (0, 0)) for a in ws]
    shapes = tuple(jax.ShapeDtypeStruct(a.shape, F32) for a in ws)
    res = _pallas(
        body, name="adamw_small", grid=(1,), in_specs=specs * 4, out_specs=tuple(specs * 3), out_shape=shapes * 3,
        compiler_params=_cp("arbitrary"),
    )(*ws, *gs, *ms, *vs)
    return res[:n], res[n:2 * n], res[2 * n:]


class _Comm:
    _ROW_SHARDED = ("w_mem_kv", "w_out", "w_down")

    def __init__(self, slabs, place):
        self.slabs, self.place = slabs, place
        self.full, self.raw, self.parts, self.landing, self.bufs, self.done = {}, {}, {}, {}, {}, {}
        ex, deliver = self._gather(["w_in"])
        deliver(_run_exchanges("all_gather_w_in", [ex])[0])

    def w(self, name):
        a = self.full[name]
        if name in self._ROW_SHARDED:
            return a.reshape(-1, a.shape[-1])
        if name == "conv_w":
            return jnp.transpose(a, (1, 0, 2)).reshape(3, 1, D_FF)
        return a

    def grad(self, name, arr):
        self.raw[name] = arr.reshape((N_CHIPS, -1, arr.shape[-1]))
        if name == "w_in":
            ex, deliver = self._to_sibling(["w_in"])
            deliver(_run_exchanges("rs_sibling_w_in", [ex])[0])

    def small_grads(self, arrays):
        self.small_local = list(arrays)

    def carry(self, tag, call):
        plan = self._plan(tag)
        if not plan:
            return call(())
        out, carried = call([ex for ex, _ in plan])
        for (_, deliver), res in zip(plan, carried):
            deliver(res)
        return out

    def finish(self, last_small):
        ex, deliver = self._share(["w_out", "w_branch", "w_mem_kv", "w_in"])
        shared, small = _run_exchanges("share_and_gather_last", [ex, _ex_gather_small(last_small)])
        deliver(shared)
        return self.done, self.small_local + list(last_small), self.small_everyone + small

    def _plan(self, tag):
        if tag == "in_proj":
            return [self._gather(["w_branch", "w_out", "w_mem_kv", "w_down", "conv_w"])]
        if tag == "hgrn_fwd":
            return [self._gather(["w_up"])]
        if tag == "d_h2":
            return [self._to_sibling(["w_down", "w_up"])]
        if tag == "hgrn_bwd":
            return [self._to_owner(["w_down", "w_up"]), self._to_sibling(["w_out", "w_branch", "w_mem_kv"])]
        if tag == "g_w_in":
            def keep(res):
                self.small_everyone = res

            return [self._to_owner(["w_out", "w_branch", "w_mem_kv"]), self._share(["w_down", "w_up"]),
                    (_ex_gather_small(self.small_local), keep)]
        if tag == "d_h":
            return [self._to_owner(["w_in"])]
        return []

    def _gather(self, names, part=(0, 1)):
        def deliver(res):
            self.slabs.update(zip(names, res))
            self.full.update(zip(names, res))

        return _ex_all_gather([self.slabs[n] for n in names], [n != "conv_w" for n in names], part), deliver

    def _to_sibling(self, names):
        def deliver(res):
            for n, r in zip(names, res):
                self.parts[n] = _add_half("rs_add_" + n, self.raw[n], r, self.place)

        return _ex_to_sibling([self.raw[n] for n in names]), deliver

    def _to_owner(self, names, part=(0, 1)):
        def deliver(res):
            for n, r in zip(names, res):
                if part[0] + 1 < part[1]:
                    self.landing[n] = r
                else:
                    self.bufs[n] = _sum_owner("rs_sum_" + n, self.parts[n], r, self.place)

        landing = [self.landing[n] for n in names] if part[0] else None
        return _ex_to_owner([self.parts[n] for n in names], part, landing), deliver

    def _share(self, names):
        return _ex_share_halves([self.bufs[n] for n in names]), lambda res: self.done.update(zip(names, res))


def kernel(x, mem, norm1_g, w_in, ln_v_g, ln_v_b, w_spatial, b_spatial, lb_logits, hgrn_norm_g, mem_norm_g, w_mem_kv, w_branch, w_out, norm2_g, w_up, conv_w, conv_b, w_down, final_g, loss_target, m_norm1_g, m_w_in, m_ln_v_g, m_ln_v_b, m_w_spatial, m_b_spatial, m_lb_logits, m_hgrn_norm_g, m_mem_norm_g, m_w_mem_kv, m_w_branch, m_w_out, m_norm2_g, m_w_up, m_conv_w, m_conv_b, m_w_down, m_final_g, v_norm1_g, v_w_in, v_ln_v_g, v_ln_v_b, v_w_spatial, v_b_spatial, v_lb_logits, v_hgrn_norm_g, v_mem_norm_g, v_w_mem_kv, v_w_branch, v_w_out, v_norm2_g, v_w_up, v_conv_w, v_conv_b, v_w_down, v_final_g):
    w = dict(norm1_g=norm1_g, w_in=w_in, ln_v_g=ln_v_g, ln_v_b=ln_v_b, w_spatial=w_spatial, b_spatial=b_spatial,
             lb_logits=lb_logits, hgrn_norm_g=hgrn_norm_g, mem_norm_g=mem_norm_g, w_mem_kv=w_mem_kv,
             w_branch=w_branch, w_out=w_out, norm2_g=norm2_g, w_up=w_up, conv_w=conv_w, conv_b=conv_b,
             w_down=w_down, final_g=final_g)
    mom = dict(norm1_g=m_norm1_g, w_in=m_w_in, ln_v_g=m_ln_v_g, ln_v_b=m_ln_v_b, w_spatial=m_w_spatial,
               b_spatial=m_b_spatial, lb_logits=m_lb_logits, hgrn_norm_g=m_hgrn_norm_g, mem_norm_g=m_mem_norm_g,
               w_mem_kv=m_w_mem_kv, w_branch=m_w_branch, w_out=m_w_out, norm2_g=m_norm2_g, w_up=m_w_up,
               conv_w=m_conv_w, conv_b=m_conv_b, w_down=m_w_down, final_g=m_final_g)
    var = dict(norm1_g=v_norm1_g, w_in=v_w_in, ln_v_g=v_ln_v_g, ln_v_b=v_ln_v_b, w_spatial=v_w_spatial,
               b_spatial=v_b_spatial, lb_logits=v_lb_logits, hgrn_norm_g=v_hgrn_norm_g, mem_norm_g=v_mem_norm_g,
               w_mem_kv=v_w_mem_kv, w_branch=v_w_branch, w_out=v_w_out, norm2_g=v_norm2_g, w_up=v_w_up,
               conv_w=v_conv_w, conv_b=v_conv_b, w_down=v_w_down, final_g=v_final_g)
    B, S, D = x.shape
    T = B * S
    ci = lax.axis_index("c")
    q = 2 * lax.axis_index("x") + lax.axis_index("y")
    place = jnp.stack([q, ci, 2 * q + ci]).astype(jnp.int32)

    slabs = {n: _cast_into_slab("slab_" + n, w[n].reshape(_BIG_SHARD_SHAPE[n]), place, BF16) for n in _BIG}
    slabs["conv_w"] = _cast_into_slab("slab_conv_w", conv_w[0], place, F32)
    comm = _Comm(slabs, place)
    p = dict(
        norm1_g=norm1_g, ln_v_g=ln_v_g, ln_v_b=ln_v_b, w_spatial=w_spatial[0],
        b_spatial=b_spatial.reshape(GM_GROUPS, GM_CHUNK, 1), lb_logits=lb_logits, hgrn_norm_g=hgrn_norm_g,
        mem_norm_g=mem_norm_g, norm2_g=norm2_g, conv_b=conv_b, final_g=final_g.reshape(1, D))

    loss, grad_x, g = _local_step(x.reshape(T, D), mem.reshape(B * MEM_LEN, D), loss_target.reshape(T, D), p, comm,
                                  B, S)

    shard_grads, local_small, everyone = comm.finish([g["norm1_g"]])
    summed = _sum_small(everyone, local_small, place)
    small_names = list(_SMALL_EARLY) + ["norm1_g"]
    total = dict(zip(_SMALL_EARLY, summed))
    loss_total, total["norm1_g"] = summed[len(_SMALL_EARLY)][0, 0], summed[-1]

    grads, delta, new_m, new_v = {}, {}, {}, {}
    for n in _BIG:
        shp = _BIG_SHARD_SHAPE[n]
        grads[n] = shard_grads[n]
        delta[n], new_m[n], new_v[n] = _adamw("adamw_" + n, w[n].reshape(shp), shard_grads[n],
                                              mom[n].reshape(shp), var[n].reshape(shp))
    cw_shard = D_FF // N_CHIPS
    total["conv_w"] = lax.dynamic_slice(total["conv_w"], (0, q * cw_shard), (3, cw_shard))

    def flat2d(d, n):
        return d[n].reshape(total[n].shape)

    upd = _adamw_small([flat2d(w, n) for n in small_names], [total[n] for n in small_names],
                       [flat2d(mom, n) for n in small_names], [flat2d(var, n) for n in small_names])
    for k, n in enumerate(small_names):
        grads[n], delta[n], new_m[n], new_v[n] = total[n], upd[0][k], upd[1][k], upd[2][k]

    def shaped(d):
        return [d[n].reshape(w[n].shape) for n in _PARAM_ORDER]

    return (loss_total, grad_x.reshape(B, S, D), *shaped(grads), *shaped(delta), *shaped(new_m), *shaped(new_v))
```

```python
import functools
import math

import jax
import jax.numpy as jnp
from jax import lax
from jax.experimental import pallas as pl
from jax.experimental.pallas import tpu as pltpu

F32 = jnp.float32
BF16 = jnp.bfloat16
EPS = 1e-6

D_MODEL = 1024
MEM_LEN = 256
GM_WIDTH = 512
GM_CHUNK = 128
GM_GROUPS = 4
HG_HEADS = 4
HG_DIM = 128
HG_CHUNK = 64
XA_HEADS = 4
XA_DIM = 128
BR_WIDTH = 512
D_FF = 2816
IN_WIDTH = 6656
N_CHIPS = 4
N_DEV = 8

ADAM_LR = 0.001
ADAM_B1 = 0.9
ADAM_B2 = 0.999
ADAM_EPS = 1e-08
ADAM_WD = 0.01
ADAM_STEP = 10

COL_ZU, COL_ZV, COL_HQ, COL_HF, COL_HI, COL_HG, COL_XQ = 0, 1, 2, 3, 4, 5, 6
COL_GATE0 = 3584

VMEM_LIMIT_BYTES = 48 * 1024 * 1024
MESH_ID = pl.DeviceIdType.MESH


def _cp(*sem):
    return pltpu.CompilerParams(dimension_semantics=sem, vmem_limit_bytes=VMEM_LIMIT_BYTES)


def _pallas(body, *, out_shape, **kw):
    def pin(s):
        return pltpu.HBM(s.shape, s.dtype) if isinstance(s, jax.ShapeDtypeStruct) else s

    out_shape = tuple(pin(s) for s in out_shape) if isinstance(out_shape, (tuple, list)) else pin(out_shape)
    call = pl.pallas_call(body, out_shape=out_shape, **kw)

    def run(*operands):
        return call(*[pltpu.with_memory_space_constraint(o, pltpu.HBM) if jnp.issubdtype(o.dtype, jnp.floating)
                      else o for o in operands])

    return run


def _dot(a, b):
    return lax.dot_general(a.astype(BF16), b.astype(BF16), (((1,), (0,)), ((), ())), preferred_element_type=F32)


def _dot_nt(a, b):
    return lax.dot_general(a.astype(BF16), b.astype(BF16), (((1,), (1,)), ((), ())), preferred_element_type=F32)


def _dot_tn(a, b):
    return lax.dot_general(a.astype(BF16), b.astype(BF16), (((0,), (0,)), ((), ())), preferred_element_type=F32)


def _dot_01(mask01, x):
    hi = x.astype(BF16)
    r1 = x - hi.astype(F32)
    mid = r1.astype(BF16)
    lo = (r1 - mid.astype(F32)).astype(BF16)
    m = mask01.astype(BF16)
    dn = (((1,), (0,)), ((), ()))
    return (lax.dot_general(m, hi, dn, preferred_element_type=F32)
            + lax.dot_general(m, mid, dn, preferred_element_type=F32)
            + lax.dot_general(m, lo, dn, preferred_element_type=F32))


def _sigmoid(z):
    return 1.0 / (1.0 + jnp.exp(-z))


_GELU_C = math.sqrt(2.0 / math.pi)


def _gelu_and_grad(z):
    inner = _GELU_C * (z + 0.044715 * z * z * z)
    t = jnp.tanh(inner)
    val = 0.5 * z * (1.0 + t)
    grad = 0.5 * (1.0 + t) + 0.5 * z * (1.0 - t * t) * _GELU_C * (1.0 + 3.0 * 0.044715 * z * z)
    return val, grad


def _row_tile(n, want=512):
    t = min(want, n)
    assert n % t == 0
    return t


def _pcall(body, operands, *, name, grid, in_specs, out_specs, out_shape, scratch_shapes=(), semantics, riders=()):
    single = not isinstance(out_shape, (tuple, list))
    out_specs = (out_specs,) if single else tuple(out_specs)
    out_shape = (out_shape,) if single else tuple(out_shape)
    if not riders:
        res = _pallas(body, name=name, grid=grid, in_specs=list(in_specs), out_specs=out_specs,
                      out_shape=out_shape, scratch_shapes=list(scratch_shapes),
                      compiler_params=_cp(*semantics))(*operands)
        return (res[0] if single else res), []
    n_in, n_out, n_scr = len(in_specs), len(out_shape), len(scratch_shapes)
    ex_in = [len(ex.operands) for ex in riders]
    ex_out = [len(ex.out_shape) for ex in riders]
    ex_scr = [len(ex.scratch) for ex in riders]
    tot_in, tot_out = n_in + sum(ex_in), n_out + sum(ex_out)

    def wrapped(*refs):
        ins, outs, scr = refs[:tot_in], refs[tot_in:tot_in + tot_out], refs[tot_in + tot_out:]
        ids = [pl.program_id(d) for d in range(len(grid))]
        first = functools.reduce(lambda p, t: p & t, [i == 0 for i in ids])
        last = functools.reduce(lambda p, t: p & t, [i == n - 1 for i, n in zip(ids, grid)])
        parts, oi, oo, os_ = [], n_in, n_out, n_scr
        for k in range(len(riders)):
            parts.append((ins[oi:oi + ex_in[k]], outs[oo:oo + ex_out[k]], scr[os_:os_ + ex_scr[k]]))
            oi, oo, os_ = oi + ex_in[k], oo + ex_out[k], os_ + ex_scr[k]

        @pl.when(first)
        def _():
            for ex, part in zip(riders, parts):
                ex.start(*part)

        body(*ins[:n_in], *outs[:n_out], *scr[:n_scr])

        @pl.when(last)
        def _():
            for ex, part in zip(riders, parts):
                ex.finish(*part)

    aliases, oi, oo = {}, n_in, n_out
    all_ops, all_shapes, all_scr = list(operands), list(out_shape), list(scratch_shapes)
    for k, ex in enumerate(riders):
        aliases.update({oi + a: oo + b for a, b in ex.aliases.items()})
        oi, oo = oi + ex_in[k], oo + ex_out[k]
        all_ops += list(ex.operands)
        all_shapes += [pltpu.HBM(s.shape, s.dtype) for s in ex.out_shape]
        all_scr += list(ex.scratch)
    res = _pallas(
        wrapped, name=name, grid=grid, in_specs=list(in_specs) + [HBM_SPEC] * sum(ex_in),
        out_specs=out_specs + (HBM_SPEC,) * sum(ex_out), out_shape=tuple(all_shapes), scratch_shapes=all_scr,
        input_output_aliases=aliases, compiler_params=_cp(*(["arbitrary"] * len(grid))))(*all_ops)
    own = res[0] if single else tuple(res[:n_out])
    carried, oo = [], n_out
    for k in range(len(riders)):
        carried.append(list(res[oo:oo + ex_out[k]]))
        oo += ex_out[k]
    return own, carried


def _carried(out, carried, riders):
    return (out, carried) if riders else out


def _matmul(name, operands, *, grid, in_specs, o_spec, out_shape, out_dtype, dims, has_res=False, riders=()):
    nk = grid[2]
    assert nk == 1 or (out_dtype == F32 and not has_res)

    def body(*refs):
        if has_res:
            a_ref, b_ref, r_ref, o_ref = refs
        else:
            a_ref, b_ref, o_ref = refs
            r_ref = None
        part = lax.dot_general(a_ref[...].astype(BF16), b_ref[...].astype(BF16), (dims, ((), ())),
                               preferred_element_type=F32)
        if nk == 1:
            if r_ref is not None:
                part = part + r_ref[...]
            o_ref[...] = part.astype(o_ref.dtype)
        else:
            k = pl.program_id(2)

            @pl.when(k == 0)
            def _():
                o_ref[...] = part

            @pl.when(k > 0)
            def _():
                o_ref[...] += part

    out, carried = _pcall(body, operands, name=name, grid=grid, in_specs=in_specs, out_specs=o_spec,
                          out_shape=jax.ShapeDtypeStruct(out_shape, out_dtype),
                          semantics=("parallel", "parallel", "arbitrary"), riders=riders)
    return (out, carried) if riders else out


NN = ((1,), (0,))
NT = ((1,), (1,))
TN = ((0,), (0,))
_TN_TOKENS = 4096


def _mm_cs(name, a, w, out_dtype, riders=()):
    M, K = a.shape
    nq, _, wd = w.shape
    tm = _row_tile(M)
    return _matmul(name, (a, w), grid=(nq, M // tm, 1),
                   in_specs=[pl.BlockSpec((tm, K), lambda j, i, k: (i, 0)),
                             pl.BlockSpec((None, K, wd), lambda j, i, k: (j, 0, 0))],
                   o_spec=pl.BlockSpec((tm, wd), lambda j, i, k: (i, j)),
                   out_shape=(M, nq * wd), out_dtype=out_dtype, dims=NN, riders=riders)


def _mm_rs(name, a, w, out_dtype, res=None):
    M, K = a.shape
    N = w.shape[1]
    tm = _row_tile(M)
    tn = N
    ops = (a, w) if res is None else (a, w, res)
    in_specs = [pl.BlockSpec((tm, K), lambda i, j, k: (i, 0)),
                pl.BlockSpec((K, tn), lambda i, j, k: (0, j))]
    if res is not None:
        in_specs.append(pl.BlockSpec((tm, tn), lambda i, j, k: (i, j)))
    return _matmul(name, ops, grid=(M // tm, N // tn, 1), in_specs=in_specs,
                   o_spec=pl.BlockSpec((tm, tn), lambda i, j, k: (i, j)),
                   out_shape=(M, N), out_dtype=out_dtype, dims=NN, has_res=res is not None)


def _mm_nt_rs(name, g, w, out_dtype, riders=()):
    M, N = g.shape
    K = w.shape[0]
    to = K
    tm = _row_tile(M)
    return _matmul(name, (g, w), grid=(M // tm, K // to, 1),
                   in_specs=[pl.BlockSpec((tm, N), lambda i, j, k: (i, 0)),
                             pl.BlockSpec((to, N), lambda i, j, k: (j, 0))],
                   o_spec=pl.BlockSpec((tm, to), lambda i, j, k: (i, j)),
                   out_shape=(M, K), out_dtype=out_dtype, dims=NT, riders=riders)


def _mm_nt_cs(name, g, w, out_dtype, riders=(), stacked=False, norm_bwd=None):
    M = g.shape[-2]
    nq, K, wd = w.shape
    tm = _row_tile(M, 256)

    def product(g_ref, w_ref):
        acc = None
        for q in range(nq):
            gq = g_ref[q // 2, :, (q % 2) * wd:(q % 2 + 1) * wd] if stacked else g_ref[:, q * wd:(q + 1) * wd]
            part = _dot_nt(gq, w_ref[q])
            acc = part if acc is None else acc + part
        return acc

    def body(g_ref, w_ref, o_ref):
        o_ref[...] = product(g_ref, w_ref).astype(o_ref.dtype)

    def body_norm(g_ref, w_ref, x_ref, gain_ref, dr_ref, dx_ref, dg_ref):
        @pl.when(pl.program_id(0) == 0)
        def _():
            dg_ref[...] = jnp.zeros_like(dg_ref)

        dx, dg = _rms_bwd_rows(x_ref[...], gain_ref[...], product(g_ref, w_ref))
        dg_ref[...] += dg
        dx_ref[...] = dx + dr_ref[...]

    g_spec = (pl.BlockSpec((2, tm, 2 * wd), lambda i: (0, i, 0)) if stacked
              else pl.BlockSpec((tm, nq * wd), lambda i: (i, 0)))
    w_spec = pl.BlockSpec((nq, K, wd), lambda i: (0, 0, 0))
    row = pl.BlockSpec((tm, K), lambda i: (i, 0))
    if norm_bwd is None:
        return _carried(*_pcall(
            body, (g, w), name=name, grid=(M // tm,), in_specs=[g_spec, w_spec], out_specs=row,
            out_shape=jax.ShapeDtypeStruct((M, K), out_dtype), semantics=("parallel",), riders=riders), riders)
    vec = pl.BlockSpec((1, K), lambda i: (0, 0))
    return _carried(*_pcall(
        body_norm, (g, w) + tuple(norm_bwd), name=name, grid=(M // tm,),
        in_specs=[g_spec, w_spec, row, vec, row], out_specs=(row, vec),
        out_shape=(jax.ShapeDtypeStruct((M, K), F32), jax.ShapeDtypeStruct((1, K), F32)),
        semantics=("arbitrary",), riders=riders), riders)


def _mm_tn_rs(name, a, g, to, tn=512):
    T, M = a.shape
    N = g.shape[1]
    tt = _row_tile(T, _TN_TOKENS)
    tn = min(tn, N)
    return _matmul(name, (a, g), grid=(M // to, N // tn, T // tt),
                   in_specs=[pl.BlockSpec((tt, to), lambda i, j, k: (k, i)),
                             pl.BlockSpec((tt, tn), lambda i, j, k: (k, j))],
                   o_spec=pl.BlockSpec((to, tn), lambda i, j, k: (i, j)),
                   out_shape=(M, N), out_dtype=F32, dims=TN)


def _mm_tn_cs(name, a, g, nq, to, riders=(), stacked=False):
    T, M = a.shape
    wd = g.shape[-1] * (2 if stacked else 1) // nq
    tt = _row_tile(T, _TN_TOKENS)
    g_spec = (pl.BlockSpec((None, tt, wd), lambda i, j, k: (j // 2, k, j % 2)) if stacked
              else pl.BlockSpec((tt, wd), lambda i, j, k: (k, j)))
    return _matmul(name, (a, g), grid=(M // to, nq, T // tt),
                   in_specs=[pl.BlockSpec((tt, to), lambda i, j, k: (k, i)), g_spec],
                   o_spec=pl.BlockSpec((None, to, wd), lambda i, j, k: (j, i, 0)),
                   out_shape=(nq, M, wd), out_dtype=F32, dims=TN, riders=riders)


def _rms_fwd(name, x, g):
    T, D = x.shape
    tm = _row_tile(T)

    def body(x_ref, g_ref, o_ref):
        xv = x_ref[...]
        r = lax.rsqrt(jnp.mean(xv * xv, axis=-1, keepdims=True) + EPS)
        o_ref[...] = (xv * r * g_ref[...]).astype(o_ref.dtype)

    return _pallas(
        body, name=name, grid=(T // tm,),
        in_specs=[pl.BlockSpec((tm, D), lambda i: (i, 0)), pl.BlockSpec((1, D), lambda i: (0, 0))],
        out_specs=pl.BlockSpec((tm, D), lambda i: (i, 0)),
        out_shape=jax.ShapeDtypeStruct((T, D), BF16), compiler_params=_cp("parallel"),
    )(x, g)


def _rms_rows(xv, gain):
    return xv * lax.rsqrt(jnp.mean(xv * xv, axis=-1, keepdims=True) + EPS) * gain


def _rms_bwd_rows(xv, gain, dh):
    r = lax.rsqrt(jnp.mean(xv * xv, axis=-1, keepdims=True) + EPS)
    n = xv * r
    dn = dh * gain
    return r * (dn - n * jnp.mean(dn * n, axis=-1, keepdims=True)), jnp.sum(dh * n, axis=0, keepdims=True)


def _rms_bwd(name, x, g, dh, dres):
    T, D = x.shape
    tm = _row_tile(T)
    has_res = dres is not None

    def body(*refs):
        if has_res:
            x_ref, g_ref, dh_ref, dr_ref, dx_ref, dg_ref = refs
        else:
            x_ref, g_ref, dh_ref, dx_ref, dg_ref = refs

        @pl.when(pl.program_id(0) == 0)
        def _():
            dg_ref[...] = jnp.zeros_like(dg_ref)

        dx, dg = _rms_bwd_rows(x_ref[...], g_ref[...], dh_ref[...])
        dg_ref[...] += dg
        if has_res:
            dx = dx + dr_ref[...]
        dx_ref[...] = dx

    row = pl.BlockSpec((tm, D), lambda i: (i, 0))
    vec = pl.BlockSpec((1, D), lambda i: (0, 0))
    ops = (x, g, dh, dres) if has_res else (x, g, dh)
    return _pallas(
        body, name=name, grid=(T // tm,), in_specs=[row, vec, row] + ([row] if has_res else []),
        out_specs=(row, vec),
        out_shape=(jax.ShapeDtypeStruct((T, D), F32), jax.ShapeDtypeStruct((1, D), F32)),
        compiler_params=_cp("arbitrary"),
    )(*ops)


def _proj_res_norm(name, a, w, res, gain):
    M, K = a.shape
    N = w.shape[1]
    tm = _row_tile(M)

    def body(a_ref, w_ref, r_ref, g_ref, x_ref, h_ref):
        xv = _dot(a_ref[...], w_ref[...]) + r_ref[...]
        x_ref[...] = xv
        h_ref[...] = _rms_rows(xv, g_ref[...]).astype(h_ref.dtype)

    row = pl.BlockSpec((tm, N), lambda i: (i, 0))
    return _pallas(
        body, name=name, grid=(M // tm,),
        in_specs=[pl.BlockSpec((tm, K), lambda i: (i, 0)), pl.BlockSpec((K, N), lambda i: (0, 0)), row,
                  pl.BlockSpec((1, N), lambda i: (0, 0))],
        out_specs=(row, row), out_shape=(jax.ShapeDtypeStruct((M, N), F32), jax.ShapeDtypeStruct((M, N), BF16)),
        compiler_params=_cp("parallel"),
    )(a, w, res, gain)


def _proj_res_loss(name, a, w, res, tgt, gain):
    M, K = a.shape
    D = w.shape[1]
    tm = _row_tile(M)

    def body(a_ref, w_ref, r_ref, t_ref, g_ref, dx_ref, dg_ref, loss_ref):
        @pl.when(pl.program_id(0) == 0)
        def _():
            dg_ref[...] = jnp.zeros_like(dg_ref)
            loss_ref[...] = jnp.zeros_like(loss_ref)

        xv = _dot(a_ref[...], w_ref[...]) + r_ref[...]
        gv = g_ref[...]
        diff = _rms_rows(xv, gv) - t_ref[...]
        loss_ref[...] += 0.5 * jnp.sum(jnp.mean(diff * diff, axis=-1, keepdims=True))
        dx, dg = _rms_bwd_rows(xv, gv, diff * (1.0 / D))
        dg_ref[...] += dg
        dx_ref[...] = dx

    row = pl.BlockSpec((tm, D), lambda i: (i, 0))
    vec = pl.BlockSpec((1, D), lambda i: (0, 0))
    return _pallas(
        body, name=name, grid=(M // tm,),
        in_specs=[pl.BlockSpec((tm, K), lambda i: (i, 0)), pl.BlockSpec((K, D), lambda i: (0, 0)), row, row, vec],
        out_specs=(row, vec, pl.BlockSpec((8, 128), lambda i: (0, 0))),
        out_shape=(jax.ShapeDtypeStruct((M, D), F32), jax.ShapeDtypeStruct((1, D), F32),
                   jax.ShapeDtypeStruct((8, 128), F32)),
        compiler_params=_cp("arbitrary"),
    )(a, w, res, tgt, gain)


def _gmlp_pieces(zu, zv, lng, lnb, ws_ref, bs_ref):
    u, du = _gelu_and_grad(zu)
    v, dv = _gelu_and_grad(zv)
    mu = jnp.mean(v, axis=-1, keepdims=True)
    vc = v - mu
    rstd = lax.rsqrt(jnp.mean(vc * vc, axis=-1, keepdims=True) + EPS)
    vhat = vc * rstd
    vn = vhat * lng + lnb
    row = lax.broadcasted_iota(jnp.int32, (GM_CHUNK, GM_CHUNK), 0)
    col = lax.broadcasted_iota(jnp.int32, (GM_CHUNK, GM_CHUNK), 1)
    tril = row >= col
    wms, mixed = [], []
    for g in range(GM_GROUPS):
        sl = slice(g * 128, (g + 1) * 128)
        wm = jnp.where(tril, ws_ref[g], 0.0)
        wms.append(wm)
        mixed.append(_dot(wm, vn[:, sl]) + bs_ref[g])
    return u, du, dv, rstd, vhat, vn, wms, mixed, tril


def _gmlp_fwd(proj, lng, lnb, ws, bs_col):
    T = proj.shape[0]
    n = T // GM_CHUNK

    def body(zu_ref, zv_ref, lng_ref, lnb_ref, ws_ref, bs_ref, o_ref):
        u, _, _, _, _, _, _, mixed, _ = _gmlp_pieces(zu_ref[...].astype(F32), zv_ref[...].astype(F32),
                                                     lng_ref[...], lnb_ref[...],
                                                     ws_ref, bs_ref)
        for g in range(GM_GROUPS):
            sl = slice(g * 128, (g + 1) * 128)
            o_ref[:, sl] = (u[:, sl] * mixed[g]).astype(o_ref.dtype)

    vec = pl.BlockSpec((1, GM_WIDTH), lambda i: (0, 0))
    return _pallas(
        body, name="gmlp_fwd", grid=(n,),
        in_specs=[pl.BlockSpec((GM_CHUNK, 512), lambda i: (i, COL_ZU)),
                  pl.BlockSpec((GM_CHUNK, 512), lambda i: (i, COL_ZV)),
                  vec, vec,
                  pl.BlockSpec((GM_GROUPS, 128, 128), lambda i: (0, 0, 0)),
                  pl.BlockSpec((GM_GROUPS, 128, 1), lambda i: (0, 0, 0))],
        out_specs=pl.BlockSpec((GM_CHUNK, 512), lambda i: (i, 0)),
        out_shape=jax.ShapeDtypeStruct((T, GM_WIDTH), BF16), compiler_params=_cp("parallel"),
    )(proj, proj, lng, lnb, ws, bs_col)


def _gmlp_bwd(proj, d_out, lng, lnb, ws, bs_col, riders=()):
    T = proj.shape[0]
    n = T // GM_CHUNK

    def body(zu_ref, zv_ref, do_ref, lng_ref, lnb_ref, ws_ref, bs_ref,
             dz_ref, dws_ref, dbs_ref, dlng_ref, dlnb_ref, dm_acc):
        i = pl.program_id(0)

        @pl.when(i == 0)
        def _():
            dws_ref[...] = jnp.zeros_like(dws_ref)
            dlng_ref[...] = jnp.zeros_like(dlng_ref)
            dlnb_ref[...] = jnp.zeros_like(dlnb_ref)
            dm_acc[...] = jnp.zeros_like(dm_acc)

        lng_v = lng_ref[...]
        u, du, dv, rstd, vhat, vn, wms, mixed, tril = _gmlp_pieces(zu_ref[...].astype(F32), zv_ref[...].astype(F32),
                                                                  lng_v, lnb_ref[...],
                                                                  ws_ref, bs_ref)
        do = do_ref[...]
        dvn_parts = []
        for g in range(GM_GROUPS):
            sl = slice(g * 128, (g + 1) * 128)
            dog = do[:, sl]
            dz_ref[:, sl] = (dog * mixed[g] * du[:, sl]).astype(dz_ref.dtype)
            dmix = dog * u[:, sl]
            dm_acc[:, sl] += dmix
            dws_ref[g] += jnp.where(tril, _dot_nt(dmix, vn[:, sl]), 0.0)
            dvn_parts.append(_dot_tn(wms[g], dmix))
        dvn = jnp.concatenate(dvn_parts, axis=1)
        dlng_ref[...] += jnp.sum(dvn * vhat, axis=0, keepdims=True)
        dlnb_ref[...] += jnp.sum(dvn, axis=0, keepdims=True)
        dvh = dvn * lng_v
        dvv = rstd * (dvh - jnp.mean(dvh, axis=-1, keepdims=True)
                      - vhat * jnp.mean(dvh * vhat, axis=-1, keepdims=True))
        dz_ref[:, GM_WIDTH:] = (dvv * dv).astype(dz_ref.dtype)

        @pl.when(i == n - 1)
        def _():
            for g in range(GM_GROUPS):
                dbs_ref[g] = jnp.sum(dm_acc[:, g * 128:(g + 1) * 128], axis=1, keepdims=True)

    vec = pl.BlockSpec((1, GM_WIDTH), lambda i: (0, 0))
    wsp = pl.BlockSpec((GM_GROUPS, 128, 128), lambda i: (0, 0, 0))
    bsp = pl.BlockSpec((GM_GROUPS, 128, 1), lambda i: (0, 0, 0))
    return _carried(*_pcall(
        body, (proj, proj, d_out, lng, lnb, ws, bs_col), name="gmlp_bwd", grid=(n,),
        in_specs=[pl.BlockSpec((GM_CHUNK, 512), lambda i: (i, COL_ZU)),
                  pl.BlockSpec((GM_CHUNK, 512), lambda i: (i, COL_ZV)),
                  pl.BlockSpec((None, GM_CHUNK, 512), lambda i: (0, i, 0)), vec, vec, wsp, bsp],
        out_specs=(pl.BlockSpec((GM_CHUNK, 2 * GM_WIDTH), lambda i: (i, 0)), wsp, bsp, vec, vec),
        out_shape=(jax.ShapeDtypeStruct((T, 2 * GM_WIDTH), BF16),
                   jax.ShapeDtypeStruct((GM_GROUPS, 128, 128), F32), jax.ShapeDtypeStruct((GM_GROUPS, 128, 1), F32),
                   jax.ShapeDtypeStruct((1, GM_WIDTH), F32), jax.ShapeDtypeStruct((1, GM_WIDTH), F32)),
        scratch_shapes=[pltpu.VMEM((GM_CHUNK, GM_WIDTH), F32)],
        semantics=("arbitrary",), riders=riders), riders)


def _hgrn_lower_bound(lbl):
    return 1.0 / (1.0 + jnp.exp(lbl[1:2, :] - lbl[0:1, :]))


def _hgrn_gates(hq, hf, lb):
    C = HG_CHUNK
    sg = _sigmoid(hf)
    fg = lb + (1.0 - lb) * sg
    sq = _sigmoid(hq)
    row = lax.broadcasted_iota(jnp.int32, (C, C), 0)
    col = lax.broadcasted_iota(jnp.int32, (C, C), 1)
    tril = row >= col
    logf = jnp.log(fg)
    a = _dot_01(tril, logf)
    a_last = jnp.sum(logf, axis=0, keepdims=True)
    first_half = lax.broadcasted_iota(jnp.int32, logf.shape, 0) < (C // 2)
    a_mid = jnp.sum(jnp.where(first_half, logf, 0.0), axis=0, keepdims=True)
    ea, ei, eki, ekl = jnp.exp(a), jnp.exp(a - a_mid), jnp.exp(a_mid - a), jnp.exp(a_last - a)
    k = 1.0 - fg
    q = hq * sq
    qi = (q * ei).astype(BF16).astype(F32)
    ki = (k * eki).astype(BF16).astype(F32)
    return dict(sg=sg, fg=fg, sq=sq, tril=tril, ea=ea, ei=ei, eki=eki, ekl=ekl, e_last=jnp.exp(a_last),
                qe=q * ea, qi=qi, ki=ki, kl=k * ekl)


def _heads(x):
    return [x[:, h * HG_DIM:(h + 1) * HG_DIM] for h in range(HG_HEADS)]


def _hgrn_fwd(proj, lbl, gh, B, S, riders=()):
    C = HG_CHUNK
    NC = S // C
    W = HG_HEADS * HG_DIM

    def body(q_ref, f_ref, i_ref, g_ref, lbl_ref, gh_ref, o_ref, bo_ref, st_ref, state):
        @pl.when(pl.program_id(0) == 0)
        def _():
            state[...] = jnp.zeros_like(state)

        lb = _hgrn_lower_bound(lbl_ref[...])
        ghv = gh_ref[...]
        for b in range(B):
            gt = _hgrn_gates(q_ref[b].astype(F32), f_ref[b].astype(F32), lb)
            v = _heads(i_ref[b])
            qe, qi, ki, kl, e_last = (_heads(gt[n]) for n in ("qe", "qi", "ki", "kl", "e_last"))
            outs, normed = [], []
            for h in range(HG_HEADS):
                p = jnp.where(gt["tril"], _dot_nt(qi[h], ki[h]), 0.0)
                st = state[b, h]
                st_ref[b, h] = st
                o = _dot_nt(qe[h], st) + _dot(p, v[h])
                state[b, h] = st * e_last[h] + _dot_tn(v[h], kl[h])
                outs.append(o)
                normed.append(o * lax.rsqrt(jnp.mean(o * o, axis=-1, keepdims=True) + EPS) * ghv)
            o_ref[b] = jnp.concatenate(outs, axis=1)
            hg = g_ref[b].astype(F32)
            bo_ref[b] = (jnp.concatenate(normed, axis=1) * (hg * _sigmoid(hg))).astype(bo_ref.dtype)

    def col(cb):
        return pl.BlockSpec((B, C, 512), lambda c: (0, c, cb))

    tile = pl.BlockSpec((B, C, W), lambda c: (0, c, 0))
    proj3 = proj.reshape(B, S, proj.shape[-1])
    out, carried = _pcall(
        body, (proj3, proj3, proj3, proj3, lbl, gh), name="hgrn_fwd", grid=(NC,),
        in_specs=[col(COL_HQ), col(COL_HF), col(COL_HI), col(COL_HG),
                  pl.BlockSpec((2, W), lambda c: (0, 0)), pl.BlockSpec((1, HG_DIM), lambda c: (0, 0))],
        out_specs=(tile, tile, pl.BlockSpec((B, None, HG_HEADS, 128, 128), lambda c: (0, c, 0, 0, 0))),
        out_shape=(jax.ShapeDtypeStruct((B, S, W), F32), jax.ShapeDtypeStruct((B, S, W), BF16),
                   jax.ShapeDtypeStruct((B, NC, HG_HEADS, 128, 128), F32)),
        scratch_shapes=[pltpu.VMEM((B, HG_HEADS, 128, 128), F32)],
        semantics=("arbitrary",), riders=riders)
    o_h, b_out, states = out
    out = (o_h, b_out.reshape(B * S, W), states)
    return (out, carried) if riders else out


def _hgrn_bwd(proj, o_saved, states, d_out, lbl, gh, others, B, S, riders=()):
    C = HG_CHUNK
    NC = S // C
    W = HG_HEADS * HG_DIM
    d_gm, d_xq, d_gates = (t.reshape(B, S, t.shape[-1]) for t in others)
    own0 = d_gm.shape[-1]
    xq0 = own0 + 4 * W
    gates0 = xq0 + d_xq.shape[-1]

    def body(q_ref, f_ref, i_ref, g_ref, o_ref, st_ref, do_ref, lbl_ref, gh_ref, gm_ref, xq_ref, gates_ref,
             d_ref, dlbl_ref, dgh_ref, dstate, dlb_acc):
        c = pl.program_id(0)
        d_ref[:, :, :own0] = gm_ref[...]
        d_ref[:, :, xq0:gates0] = xq_ref[...]
        d_ref[:, :, gates0:] = gates_ref[...]

        def put(b, k, val):
            d_ref[b, :, own0 + k * W:own0 + (k + 1) * W] = val.astype(d_ref.dtype)

        @pl.when(c == 0)
        def _():
            dstate[...] = jnp.zeros_like(dstate)
            dgh_ref[...] = jnp.zeros_like(dgh_ref)
            dlb_acc[...] = jnp.zeros_like(dlb_acc)

        lb = _hgrn_lower_bound(lbl_ref[...])
        ghv = gh_ref[...]
        row = lax.broadcasted_iota(jnp.int32, (C, C), 0)
        colm = lax.broadcasted_iota(jnp.int32, (C, C), 1)
        triu = colm >= row
        for b in range(B):
            hq, hg = q_ref[b].astype(F32), g_ref[b].astype(F32)
            gt = _hgrn_gates(hq, f_ref[b].astype(F32), lb)
            tril = gt["tril"]
            v = _heads(i_ref[b])
            qe, qi, ki, kl, e_last = (_heads(gt[n]) for n in ("qe", "qi", "ki", "kl", "e_last"))
            sgg = _sigmoid(hg)
            don_all = do_ref[b] * (hg * sgg)
            o, don = _heads(o_ref[b]), _heads(don_all)
            d_qe, d_qi, d_ki, d_kl, dv, n_all, dal = [], [], [], [], [], [], []
            for h in range(HG_HEADS):
                r = lax.rsqrt(jnp.mean(o[h] * o[h], axis=-1, keepdims=True) + EPS)
                n = o[h] * r
                n_all.append(n)
                dgh_ref[...] += jnp.sum(don[h] * n, axis=0, keepdims=True)
                dn = don[h] * ghv
                d_o = r * (dn - n * jnp.mean(dn * n, axis=-1, keepdims=True))
                st, dst = st_ref[b, h], dstate[b, h]
                p = jnp.where(tril, _dot_nt(qi[h], ki[h]), 0.0)
                dp = jnp.where(tril, _dot_nt(d_o, v[h]), 0.0)
                d_qe.append(_dot(d_o, st))
                d_qi.append(_dot(dp, ki[h]))
                d_ki.append(_dot_tn(dp, qi[h]))
                d_kl.append(_dot(v[h], dst))
                dv.append(_dot_tn(p, d_o) + _dot_nt(kl[h], dst))
                dstate[b, h] = dst * e_last[h] + _dot_tn(d_o, qe[h])
                dal.append(jnp.sum(dst * st, axis=0, keepdims=True) * e_last[h])
            d_qe, d_qi, d_ki, d_kl, n_all, dal = (jnp.concatenate(t, axis=1)
                                                  for t in (d_qe, d_qi, d_ki, d_kl, n_all, dal))
            put(b, 3, do_ref[b] * n_all * jnp.tile(ghv, (1, HG_HEADS)) * (sgg * (1.0 + hg * (1.0 - sgg))))
            put(b, 2, jnp.concatenate(dv, axis=1))
            d_a_last = dal + jnp.sum(d_kl * gt["kl"], axis=0, keepdims=True)
            dq = d_qe * gt["ea"] + d_qi * gt["ei"]
            dk = d_ki * gt["eki"] + d_kl * gt["ekl"]
            da = d_qe * gt["qe"] + d_qi * gt["qi"] - d_ki * gt["ki"] - d_kl * gt["kl"]
            dlogf = _dot_01(triu, da) + d_a_last
            sg, sq = gt["sg"], gt["sq"]
            dfg = dlogf / gt["fg"] - dk
            put(b, 1, dfg * (1.0 - lb) * sg * (1.0 - sg))
            dlb_acc[...] += jnp.sum(dfg * (1.0 - sg), axis=0, keepdims=True)
            put(b, 0, dq * (sq * (1.0 + hq * (1.0 - sq))))

        @pl.when(c == NC - 1)
        def _():
            dlb = dlb_acc[...]
            first = lax.broadcasted_iota(jnp.int32, (2, W), 0) == 0
            dlbl_ref[...] = jnp.where(first, dlb * lb * (1.0 - lb), -dlb * lb * (1.0 - lb))

    def col(cb):
        return pl.BlockSpec((B, C, 512), lambda c: (0, NC - 1 - c, cb))

    tile = pl.BlockSpec((B, C, W), lambda c: (0, NC - 1 - c, 0))
    proj3 = proj.reshape(B, S, proj.shape[-1])
    def rows(width):
        return pl.BlockSpec((B, C, width), lambda c: (0, NC - 1 - c, 0))

    width = proj.shape[-1]
    out, carried = _pcall(
        body, (proj3, proj3, proj3, proj3, o_saved, states, d_out.reshape(3, B, S, W), lbl, gh, d_gm, d_xq, d_gates),
        name="hgrn_bwd", grid=(NC,),
        in_specs=[col(COL_HQ), col(COL_HF), col(COL_HI), col(COL_HG), tile,
                  pl.BlockSpec((B, None, HG_HEADS, 128, 128), lambda c: (0, NC - 1 - c, 0, 0, 0)),
                  pl.BlockSpec((None, B, C, W), lambda c: (1, 0, NC - 1 - c, 0)),
                  pl.BlockSpec((2, W), lambda c: (0, 0)), pl.BlockSpec((1, HG_DIM), lambda c: (0, 0)),
                  rows(d_gm.shape[-1]), rows(d_xq.shape[-1]), rows(d_gates.shape[-1])],
        out_specs=(rows(width), pl.BlockSpec((2, W), lambda c: (0, 0)), pl.BlockSpec((1, HG_DIM), lambda c: (0, 0))),
        out_shape=(jax.ShapeDtypeStruct((B, S, width), BF16), jax.ShapeDtypeStruct((2, W), F32),
                   jax.ShapeDtypeStruct((1, HG_DIM), F32)),
        scratch_shapes=[pltpu.VMEM((B, HG_HEADS, 128, 128), F32), pltpu.VMEM((1, W), F32)],
        semantics=("arbitrary",), riders=riders)
    out = (out[0].reshape(B * S, width),) + tuple(out[1:])
    return (out, carried) if riders else out


_XA_SCALE = XA_DIM ** -0.5


def _attn_probs(qh, kh):
    s = _dot_nt(qh, kh) * _XA_SCALE
    e = jnp.exp(s - jnp.max(s, axis=-1, keepdims=True))
    return e / jnp.sum(e, axis=-1, keepdims=True)


def _attn_fwd(proj, kv, B, S):
    T = B * S
    tq = _row_tile(S)
    nq = S // tq
    W = XA_HEADS * XA_DIM

    def body(q_ref, kv_ref, o_ref):
        for h in range(XA_HEADS):
            sl = slice(h * 128, (h + 1) * 128)
            p = _attn_probs(q_ref[:, sl], kv_ref[:, sl])
            o_ref[:, sl] = _dot(p, kv_ref[:, W + h * 128:W + (h + 1) * 128]).astype(o_ref.dtype)

    return _pallas(
        body, name="attn_fwd", grid=(B, nq),
        in_specs=[pl.BlockSpec((tq, 512), lambda b, i: (b * nq + i, COL_XQ)),
                  pl.BlockSpec((MEM_LEN, 2 * W), lambda b, i: (b, 0))],
        out_specs=pl.BlockSpec((tq, W), lambda b, i: (b * nq + i, 0)),
        out_shape=jax.ShapeDtypeStruct((T, W), BF16), compiler_params=_cp("parallel", "parallel"),
    )(proj, kv)


def _attn_bwd(proj, kv, d_out, B, S):
    T = B * S
    tq = _row_tile(S)
    nq = S // tq
    W = XA_HEADS * XA_DIM

    def body(q_ref, kv_ref, do_ref, dq_ref, dkv_ref):
        @pl.when(pl.program_id(1) == 0)
        def _():
            dkv_ref[...] = jnp.zeros_like(dkv_ref)

        for h in range(XA_HEADS):
            sl = slice(h * 128, (h + 1) * 128)
            slv = slice(W + h * 128, W + (h + 1) * 128)
            qh = q_ref[:, sl]
            kh = kv_ref[:, sl]
            p = _attn_probs(qh, kh)
            dc = do_ref[:, sl]
            dp = _dot_nt(dc, kv_ref[:, slv])
            ds = p * (dp - jnp.sum(dp * p, axis=-1, keepdims=True)) * _XA_SCALE
            dq_ref[:, sl] = _dot(ds, kh).astype(dq_ref.dtype)
            dkv_ref[:, sl] += _dot_tn(ds, qh)
            dkv_ref[:, slv] += _dot_tn(p, dc)

    kvspec = pl.BlockSpec((MEM_LEN, 2 * W), lambda b, i: (b, 0))
    tile = pl.BlockSpec((tq, W), lambda b, i: (b * nq + i, 0))
    return _pallas(
        body, name="attn_bwd", grid=(B, nq),
        in_specs=[pl.BlockSpec((tq, 512), lambda b, i: (b * nq + i, COL_XQ)), kvspec,
                  pl.BlockSpec((None, tq, W), lambda b, i: (2, b * nq + i, 0))],
        out_specs=(tile, kvspec),
        out_shape=(jax.ShapeDtypeStruct((T, W), BF16), jax.ShapeDtypeStruct((B * MEM_LEN, 2 * W), F32)),
        compiler_params=_cp("parallel", "arbitrary"),
    )(proj, kv, d_out)


_MERGE_TM = 256
_GATE_W = 512


def _gate_specs(tm):
    base = COL_GATE0 // _GATE_W
    return [pl.BlockSpec((tm, _GATE_W), functools.partial(lambda i, k: (i, base + k), k=k)) for k in range(6)]


def _merge_fwd(a_out, b_out, c_out, wb, proj, riders=()):
    T = a_out.shape[0]
    tm = _row_tile(T, _MERGE_TM)
    nq, _, wd = wb.shape
    per_half = _GATE_W // wd

    def body(a_ref, b_ref, c_ref, w_ref, *rest):
        gates, (m_ref, up_ref) = rest[:6], rest[6:]
        for hf in range(2):
            cols = slice(hf * _GATE_W, (hf + 1) * _GATE_W)
            acc = None
            for n, br in enumerate((a_ref, b_ref, c_ref)):
                x = br[...]
                up = jnp.concatenate([_dot(x, w_ref[per_half * hf + j, n * BR_WIDTH:(n + 1) * BR_WIDTH, :])
                                      for j in range(per_half)], axis=1)
                up_ref[n, :, cols] = up.astype(up_ref.dtype)
                term = _sigmoid(gates[2 * n + hf][...].astype(F32)) * up
                acc = term if acc is None else acc + term
            m_ref[:, cols] = acc.astype(m_ref.dtype)

    br_spec = pl.BlockSpec((tm, BR_WIDTH), lambda i: (i, 0))
    return _carried(*_pcall(
        body, (a_out, b_out, c_out, wb, *([proj] * 6)), name="merge_fwd", grid=(T // tm,),
        in_specs=[br_spec, br_spec, br_spec,
                  pl.BlockSpec((nq, 3 * BR_WIDTH, wd), lambda i: (0, 0, 0))] + _gate_specs(tm),
        out_specs=(pl.BlockSpec((tm, D_MODEL), lambda i: (i, 0)), pl.BlockSpec((3, tm, D_MODEL), lambda i: (0, i, 0))),
        out_shape=(jax.ShapeDtypeStruct((T, D_MODEL), BF16), jax.ShapeDtypeStruct((3, T, D_MODEL), BF16)),
        semantics=("parallel",), riders=riders), riders)


def _branch_bwd_act(d_ups, wb, riders=()):
    _, T, D = d_ups.shape
    nq, _, wd = wb.shape
    tm = _row_tile(T)

    def body(d_ref, w_ref, o_ref):
        acc = None
        for q in range(nq):
            part = _dot_nt(d_ref[:, q * wd:(q + 1) * wd], w_ref[q])
            acc = part if acc is None else acc + part
        o_ref[...] = acc

    return _carried(*_pcall(
        body, (d_ups, wb), name="d_branch", grid=(3, T // tm),
        in_specs=[pl.BlockSpec((None, tm, D), lambda n, i: (n, i, 0)),
                  pl.BlockSpec((nq, BR_WIDTH, wd), lambda n, i: (0, n, 0))],
        out_specs=pl.BlockSpec((None, tm, BR_WIDTH), lambda n, i: (n, i, 0)),
        out_shape=jax.ShapeDtypeStruct((3, T, BR_WIDTH), F32), semantics=("parallel", "parallel"),
        riders=riders), riders)


def _branch_bwd_weight(name, br, d_ups, n):
    T = br.shape[0]
    D = d_ups.shape[2]
    wd = D // N_CHIPS
    tt = _row_tile(T, _TN_TOKENS)

    def body(b_ref, d_ref, o_ref):
        k = pl.program_id(0)
        for q in range(N_CHIPS):
            part = _dot_tn(b_ref[...], d_ref[:, q * wd:(q + 1) * wd])

            @pl.when(k == 0)
            def _():
                o_ref[q] = part

            @pl.when(k > 0)
            def _():
                o_ref[q] += part

    return _pallas(
        body, name=name, grid=(T // tt,),
        in_specs=[pl.BlockSpec((tt, BR_WIDTH), lambda k: (k, 0)),
                  pl.BlockSpec((None, tt, D), lambda k: (n, k, 0))],
        out_specs=pl.BlockSpec((N_CHIPS, BR_WIDTH, wd), lambda k: (0, 0, 0)),
        out_shape=jax.ShapeDtypeStruct((N_CHIPS, BR_WIDTH, wd), F32), compiler_params=_cp("arbitrary"),
    )(br, d_ups)


def _merge_bwd(d_merged, ups, proj, riders=()):
    T = d_merged.shape[0]
    tm = _row_tile(T, _MERGE_TM)

    def body(dm_ref, up_ref, *rest):
        gates, (dup_ref, dg_ref) = rest[:6], rest[6:]
        for hf in range(2):
            cols = slice(hf * _GATE_W, (hf + 1) * _GATE_W)
            dm = dm_ref[:, cols]
            for n in range(3):
                gate = _sigmoid(gates[2 * n + hf][...].astype(F32))
                dup_ref[n, :, cols] = (dm * gate).astype(dup_ref.dtype)
                dg_ref[:, n * D_MODEL + hf * _GATE_W:n * D_MODEL + (hf + 1) * _GATE_W] = (
                    dm * up_ref[n, :, cols].astype(F32) * gate * (1.0 - gate)).astype(dg_ref.dtype)

    tile = pl.BlockSpec((tm, D_MODEL), lambda i: (i, 0))
    tile3 = pl.BlockSpec((3, tm, D_MODEL), lambda i: (0, i, 0))
    return _carried(*_pcall(
        body, (d_merged, ups, *([proj] * 6)), name="merge_bwd", grid=(T // tm,),
        in_specs=[tile, tile3] + _gate_specs(tm),
        out_specs=(tile3, pl.BlockSpec((tm, 3 * D_MODEL), lambda i: (i, 0))),
        out_shape=(jax.ShapeDtypeStruct((3, T, D_MODEL), BF16), jax.ShapeDtypeStruct((T, 3 * D_MODEL), BF16)),
        semantics=("parallel",), riders=riders), riders)


_CONV_TF = D_FF // 2
_CONV_TS = 256
_HALO = 16


def _conv_fwd(ab, cw, cb, B, S):
    T = B * S
    ts = _row_tile(S, _CONV_TS)
    tf = _CONV_TF
    nb = D_FF // tf
    tps = S // ts
    hb = ts // _HALO

    def body(a_ref, p_ref, b_ref, w_ref, cb_ref, o_ref):
        start = (pl.program_id(0) % tps) == 0
        a = a_ref[...].astype(F32)
        prev = jnp.where(start, 0.0, p_ref[...].astype(F32))
        ext = jnp.concatenate([prev, a], axis=0)
        a1 = pltpu.roll(ext, 1, 0)[_HALO:, :]
        a2 = pltpu.roll(ext, 2, 0)[_HALO:, :]
        ac = cb_ref[...] + w_ref[0] * a2 + w_ref[1] * a1 + w_ref[2] * a
        o_ref[...] = (ac * _sigmoid(ac) * b_ref[...].astype(F32)).astype(o_ref.dtype)

    return _pallas(
        body, name="conv_fwd", grid=(T // ts, nb),
        in_specs=[pl.BlockSpec((ts, tf), lambda i, j: (i, j)),
                  pl.BlockSpec((_HALO, tf), lambda i, j: (jnp.maximum(i * hb - 1, 0), j)),
                  pl.BlockSpec((ts, tf), lambda i, j: (i, j + nb)),
                  pl.BlockSpec((3, 1, tf), lambda i, j: (0, 0, j)),
                  pl.BlockSpec((1, tf), lambda i, j: (0, j))],
        out_specs=pl.BlockSpec((ts, tf), lambda i, j: (i, j)),
        out_shape=jax.ShapeDtypeStruct((T, D_FF), BF16), compiler_params=_cp("parallel", "parallel"),
    )(ab, ab, ab, cw, cb)


def _conv_bwd(ab, d_ff, cw, cb, B, S, riders=()):
    T = B * S
    ts = _row_tile(S, _CONV_TS)
    tf = _CONV_TF
    nb = D_FF // tf
    tps = S // ts
    hb = ts // _HALO
    last_h = T // _HALO - 1
    n_ext = ts + _HALO

    def body(a_ref, ap_ref, an_ref, b_ref, bn_ref, d_ref, dn_ref, w_ref, cb_ref, dab_ref, dw_ref, dcb_ref):
        i = pl.program_id(1)

        @pl.when(i == 0)
        def _():
            dw_ref[...] = jnp.zeros_like(dw_ref)
            dcb_ref[...] = jnp.zeros_like(dcb_ref)

        start = (i % tps) == 0
        end = (i % tps) == tps - 1
        a = a_ref[...].astype(F32)
        ext = jnp.concatenate([jnp.where(start, 0.0, ap_ref[...].astype(F32)), a, an_ref[...].astype(F32)], axis=0)
        r1 = pltpu.roll(ext, 1, 0)[_HALO:, :]
        r2 = pltpu.roll(ext, 2, 0)[_HALO:, :]
        ac = cb_ref[...] + w_ref[0] * r2 + w_ref[1] * r1 + w_ref[2] * ext[_HALO:, :]
        sg = _sigmoid(ac)
        d_e = jnp.concatenate([d_ref[...].astype(F32), jnp.where(end, 0.0, dn_ref[...].astype(F32))], axis=0)
        b_e = jnp.concatenate([b_ref[...].astype(F32), bn_ref[...].astype(F32)], axis=0)
        dab_ref[1] = (d_e[:ts, :] * (ac * sg)[:ts, :]).astype(dab_ref.dtype)
        dac = d_e * b_e * sg * (1.0 + ac * (1.0 - sg))
        u1 = pltpu.roll(dac, n_ext - 1, 0)[:ts, :]
        u2 = pltpu.roll(dac, n_ext - 2, 0)[:ts, :]
        dac0 = dac[:ts, :]
        dab_ref[0] = (w_ref[2] * dac0 + w_ref[1] * u1 + w_ref[0] * u2).astype(dab_ref.dtype)
        dcb_ref[...] += jnp.sum(dac0, axis=0, keepdims=True)
        dw_ref[2] += jnp.sum(dac0 * a, axis=0, keepdims=True)
        dw_ref[1] += jnp.sum(dac0 * r1[:ts, :], axis=0, keepdims=True)
        dw_ref[0] += jnp.sum(dac0 * r2[:ts, :], axis=0, keepdims=True)

    def cur(off):
        return pl.BlockSpec((ts, tf), lambda j, i: (i, j + off))

    def nxt(off):
        return pl.BlockSpec((_HALO, tf), lambda j, i: (jnp.minimum((i + 1) * hb, last_h), j + off))

    return _carried(*_pcall(
        body, (ab, ab, ab, ab, ab, d_ff, d_ff, cw, cb), name="conv_bwd", grid=(nb, T // ts),
        in_specs=[cur(0), pl.BlockSpec((_HALO, tf), lambda j, i: (jnp.maximum(i * hb - 1, 0), j)), nxt(0),
                  cur(nb), nxt(nb), cur(0), nxt(0),
                  pl.BlockSpec((3, 1, tf), lambda j, i: (0, 0, j)), pl.BlockSpec((1, tf), lambda j, i: (0, j))],
        out_specs=(pl.BlockSpec((2, ts, tf), lambda j, i: (0, i, j)), pl.BlockSpec((3, 1, tf), lambda j, i: (0, 0, j)),
                   pl.BlockSpec((1, tf), lambda j, i: (0, j))),
        out_shape=(jax.ShapeDtypeStruct((2, T, D_FF), BF16),
                   jax.ShapeDtypeStruct((3, 1, D_FF), F32), jax.ShapeDtypeStruct((1, D_FF), F32)),
        semantics=("parallel", "arbitrary"), riders=riders), riders)


def _local_step(x, mem, tgt, p, comm, B, S):
    g = {}
    h = _rms_fwd("norm1", x, p["norm1_g"])
    proj = comm.carry("in_proj", lambda r: _mm_cs("in_proj", h, comm.w("w_in"), BF16, riders=r))
    a_out = _gmlp_fwd(proj, p["ln_v_g"], p["ln_v_b"], p["w_spatial"], p["b_spatial"])
    o_h, b_out, states = comm.carry(
        "hgrn_fwd", lambda r: _hgrn_fwd(proj, p["lb_logits"], p["hgrn_norm_g"], B, S, riders=r))
    memn = _rms_fwd("mem_norm", mem, p["mem_norm_g"])
    kv = _mm_rs("mem_kv", memn, comm.w("w_mem_kv"), F32)
    c_out = _attn_fwd(proj, kv, B, S)
    merged, ups = comm.carry(
        "merge_fwd", lambda r: _merge_fwd(a_out, b_out, c_out, comm.w("w_branch"), proj, riders=r))
    x1, h2 = _proj_res_norm("out_proj_norm2", merged, comm.w("w_out"), x, p["norm2_g"])
    ab = comm.carry("up_proj", lambda r: _mm_cs("up_proj", h2, comm.w("w_up"), BF16, riders=r))
    conv_w = comm.w("conv_w")
    ff = _conv_fwd(ab, conv_w, p["conv_b"], B, S)
    dx2, g["final_g"], loss = _proj_res_loss("down_proj_loss", ff, comm.w("w_down"), x1, tgt, p["final_g"])

    comm.grad("w_down", _mm_tn_rs("g_w_down", ff, dx2, to=D_FF // 2))
    d_ff = comm.carry("d_ff", lambda r: _mm_nt_rs("d_ff", dx2, comm.w("w_down"), BF16, riders=r))
    d_ab, g["conv_w"], g["conv_b"] = comm.carry(
        "conv_bwd", lambda r: _conv_bwd(ab, d_ff, conv_w, p["conv_b"], B, S, riders=r))
    comm.grad("w_up", _mm_tn_cs("g_w_up", h2, d_ab, N_CHIPS, to=512, stacked=True))
    d_x1, g["norm2_g"] = comm.carry("d_h2", lambda r: _mm_nt_cs(
        "d_h2_norm2_bwd", d_ab, comm.w("w_up"), F32, riders=r, stacked=True, norm_bwd=(x1, p["norm2_g"], dx2)))
    comm.grad("w_out", _mm_tn_rs("g_w_out", merged, d_x1, to=512))
    d_merged = _mm_nt_rs("d_merged", d_x1, comm.w("w_out"), F32)
    d_ups, d_gates = comm.carry("merge_bwd", lambda r: _merge_bwd(d_merged, ups, proj, riders=r))

    d_br = comm.carry("d_branch", lambda r: _branch_bwd_act(d_ups, comm.w("w_branch"), riders=r))
    comm.grad("w_branch", jnp.concatenate(
        [_branch_bwd_weight("g_w_branch%d" % n, br, d_ups, n) for n, br in enumerate((a_out, b_out, c_out))],
        axis=1))

    d_gm, g["w_spatial"], g["b_spatial"], g["ln_v_g"], g["ln_v_b"] = comm.carry(
        "gmlp_bwd", lambda r: _gmlp_bwd(proj, d_br, p["ln_v_g"], p["ln_v_b"], p["w_spatial"], p["b_spatial"],
                                        riders=r))
    d_xq, d_kv = _attn_bwd(proj, kv, d_br, B, S)
    comm.grad("w_mem_kv", _mm_tn_rs("g_w_mem_kv", memn, d_kv, to=512))
    d_memn = _mm_nt_rs("d_memn", d_kv, comm.w("w_mem_kv"), F32)
    _, g["mem_norm_g"] = _rms_bwd("mem_norm_bwd", mem, p["mem_norm_g"], d_memn, None)
    d_proj, g["lb_logits"], g["hgrn_norm_g"] = comm.carry(
        "hgrn_bwd", lambda r: _hgrn_bwd(proj, o_h, states, d_br, p["lb_logits"], p["hgrn_norm_g"],
                                        (d_gm, d_xq, d_gates), B, S, riders=r))
    comm.small_grads([g[n].reshape(_SMALL_SHAPE[n]) for n in _SMALL_EARLY] + [loss])
    comm.grad("w_in", comm.carry("g_w_in", lambda r: _mm_tn_cs("g_w_in", h, d_proj, N_CHIPS, to=512, riders=r)))
    d_h = comm.carry("d_h", lambda r: _mm_nt_cs("d_h", d_proj, comm.w("w_in"), F32, riders=r))
    grad_x, g["norm1_g"] = _rms_bwd("norm1_bwd", x, p["norm1_g"], d_h, d_x1)
    return loss, grad_x, g


HBM_SPEC = pl.BlockSpec(memory_space=pltpu.HBM)


def _place():
    x, y, c = lax.axis_index("x"), lax.axis_index("y"), lax.axis_index("c")
    other_chips = [(1 - x, y), (x, 1 - y), (1 - x, 1 - y)]
    return x, y, c, other_chips


def _remote(src, dst, send_sem, recv_sem, dev):
    return pltpu.make_async_remote_copy(src_ref=src, dst_ref=dst, send_sem=send_sem, recv_sem=recv_sem,
                                        device_id=dev, device_id_type=MESH_ID)


class _Exchange:
    def __init__(self, operands, out_shape, aliases, scratch, start, finish):
        self.operands, self.out_shape, self.aliases, self.scratch = operands, out_shape, aliases, scratch
        self.start, self.finish = start, finish


def _run_exchanges(name, exs):
    n_in = [len(ex.operands) for ex in exs]
    n_out = [len(ex.out_shape) for ex in exs]
    n_scr = [len(ex.scratch) for ex in exs]

    def body(*refs):
        ins, outs, scr = refs[:sum(n_in)], refs[sum(n_in):sum(n_in) + sum(n_out)], refs[sum(n_in) + sum(n_out):]
        parts, oi, oo, os_ = [], 0, 0, 0
        for k in range(len(exs)):
            parts.append((ins[oi:oi + n_in[k]], outs[oo:oo + n_out[k]], scr[os_:os_ + n_scr[k]]))
            oi, oo, os_ = oi + n_in[k], oo + n_out[k], os_ + n_scr[k]
        for ex, part in zip(exs, parts):
            ex.start(*part)
        for ex, part in zip(exs, parts):
            ex.finish(*part)

    aliases, ops, shapes, scratch, oi, oo = {}, [], [], [], 0, 0
    for k, ex in enumerate(exs):
        aliases.update({oi + a: oo + b for a, b in ex.aliases.items()})
        oi, oo = oi + n_in[k], oo + n_out[k]
        ops += list(ex.operands)
        shapes += [pltpu.HBM(s.shape, s.dtype) for s in ex.out_shape]
        scratch += list(ex.scratch)
    res = _pallas(
        body, name=name, in_specs=[HBM_SPEC] * len(ops), out_specs=(HBM_SPEC,) * len(shapes), out_shape=tuple(shapes),
        input_output_aliases=aliases, scratch_shapes=scratch,
    )(*ops)
    out, oo = [], 0
    for k in range(len(exs)):
        out.append(list(res[oo:oo + n_out[k]]))
        oo += n_out[k]
    return out


def _ex_all_gather(slabs, halved, part=(0, 1)):
    n = len(slabs)

    def rows(a, cc):
        if not halved[a]:
            return slice(None)
        pr = slabs[a].shape[1] // part[1]
        return pl.ds(part[0] * pr + cc * (pr // 2), pr // 2)

    def ici(bufs, scr, a, j, chip, c, mine):
        px, py = chip
        x, y, _, _ = _place()
        qs = 2 * x + y if mine else 2 * px + py
        piece = bufs[a].at[qs, rows(a, c)]
        return _remote(piece, piece, scr[0].at[3 * a + j], scr[1].at[3 * a + j], (px, py, c))

    def d2d(bufs, scr, a, j, chip, cc):
        px, py = chip
        x, y, c, _ = _place()
        piece = bufs[a].at[2 * px + py, rows(a, cc)]
        return _remote(piece, piece, scr[2].at[3 * a + j], scr[3].at[3 * a + j], (x, y, 1 - c))

    def start(ins, outs, scr):
        _, _, c, chips = _place()
        for j, chip in enumerate(chips):
            for a in range(n):
                ici(outs, scr, a, j, chip, c, True).start()

    def finish(ins, outs, scr):
        _, _, c, chips = _place()
        for j, chip in enumerate(chips):
            for a in range(n):
                ici(outs, scr, a, j, chip, c, False).wait_recv()
                if halved[a]:
                    d2d(outs, scr, a, j, chip, c).start()
        for j, chip in enumerate(chips):
            for a in range(n):
                if halved[a]:
                    d2d(outs, scr, a, j, chip, 1 - c).wait_recv()
        for j, chip in enumerate(chips):
            for a in range(n):
                ici(outs, scr, a, j, chip, c, True).wait_send()
                if halved[a]:
                    d2d(outs, scr, a, j, chip, c).wait_send()

    return _Exchange(list(slabs), [jax.ShapeDtypeStruct(s.shape, s.dtype) for s in slabs],
                     {a: a for a in range(n)}, [pltpu.SemaphoreType.DMA((3 * n,))] * 4, start, finish)


def _ex_to_sibling(grads):
    n = len(grads)

    def copy(ins, outs, scr, a):
        x, y, c, _ = _place()
        hr = grads[a].shape[1] // 2
        return _remote(ins[a].at[:, pl.ds((1 - c) * hr, hr), :], outs[a], scr[0].at[a], scr[1].at[a], (x, y, 1 - c))

    def start(ins, outs, scr):
        for a in range(n):
            copy(ins, outs, scr, a).start()

    def finish(ins, outs, scr):
        for a in range(n):
            copy(ins, outs, scr, a).wait()

    out_shape = [jax.ShapeDtypeStruct((g.shape[0], g.shape[1] // 2, g.shape[2]), g.dtype) for g in grads]
    return _Exchange(list(grads), out_shape, {}, [pltpu.SemaphoreType.DMA((n,))] * 2, start, finish)


def _ex_to_owner(parts, part=(0, 1), landing=None):
    n = len(parts)

    def copy(ins, outs, scr, a, j, chip):
        _, _, c, _ = _place()
        px, py = chip
        pr = parts[a].shape[1] // part[1]
        rows = pl.ds(part[0] * pr, pr)
        return _remote(ins[a].at[2 * px + py, rows], outs[a].at[j, rows], scr[0].at[3 * a + j],
                       scr[1].at[3 * a + j], (px, py, c))

    def start(ins, outs, scr):
        for j, chip in enumerate(_place()[3]):
            for a in range(n):
                copy(ins, outs, scr, a, j, chip).start()

    def finish(ins, outs, scr):
        for j, chip in enumerate(_place()[3]):
            for a in range(n):
                copy(ins, outs, scr, a, j, chip).wait()

    out_shape = [jax.ShapeDtypeStruct((3,) + p.shape[1:], p.dtype) for p in parts]
    operands, aliases = list(parts), {}
    if landing is not None:
        operands, aliases = operands + list(landing), {n + a: a for a in range(n)}
    return _Exchange(operands, out_shape, aliases, [pltpu.SemaphoreType.DMA((3 * n,))] * 2, start, finish)


def _ex_share_halves(bufs):
    n = len(bufs)

    def copy(outs, scr, a, cc):
        x, y, c, _ = _place()
        hr = bufs[a].shape[0] // 2
        piece = outs[a].at[pl.ds(cc * hr, hr), :]
        return _remote(piece, piece, scr[0].at[a], scr[1].at[a], (x, y, 1 - c))

    def start(ins, outs, scr):
        c = _place()[2]
        for a in range(n):
            copy(outs, scr, a, c).start()

    def finish(ins, outs, scr):
        c = _place()[2]
        for a in range(n):
            copy(outs, scr, a, c).wait_send()
            copy(outs, scr, a, 1 - c).wait_recv()

    return _Exchange(list(bufs), [jax.ShapeDtypeStruct(b.shape, b.dtype) for b in bufs], {a: a for a in range(n)},
                     [pltpu.SemaphoreType.DMA((n,))] * 2, start, finish)


def _ex_gather_small(arrs):
    n = len(arrs)

    def peer_of(m):
        x, y, c, _ = _place()
        return (1 - x if m & 4 else x, 1 - y if m & 2 else y, 1 - c if m & 1 else c)

    def start(ins, outs, scr):
        x, y, c, _ = _place()
        for m in range(1, N_DEV):
            for a in range(n):
                k = (N_DEV - 1) * a + m - 1
                _remote(ins[a], outs[a].at[4 * x + 2 * y + c], scr[0].at[k], scr[1].at[k], peer_of(m)).start()

    def finish(ins, outs, scr):
        for m in range(1, N_DEV):
            px, py, pc = peer_of(m)
            for a in range(n):
                k = (N_DEV - 1) * a + m - 1
                slot = outs[a].at[4 * px + 2 * py + pc]
                cp = _remote(ins[a], slot, scr[0].at[k], scr[1].at[k], (px, py, pc))
                cp.wait_send()
                cp.wait_recv()

    slots = [jnp.zeros((N_DEV,) + a.shape, a.dtype) for a in arrs]
    out_shape = [jax.ShapeDtypeStruct(s.shape, s.dtype) for s in slots]
    return _Exchange(list(arrs) + slots, out_shape, {n + a: a for a in range(n)},
                     [pltpu.SemaphoreType.DMA(((N_DEV - 1) * n,))] * 2, start, finish)


def _div_tile(n, want):
    best = None
    for t in range(8, min(n, want) + 1, 8):
        if n % t == 0:
            best = t
    assert best is not None, n
    return best


def _cast_into_slab(name, w, place, dtype):
    r, cc = w.shape
    tr = r if r * cc <= 128 * 1024 else _div_tile(r, 256)

    def body(s_ref, w_ref, o_ref):
        o_ref[...] = w_ref[...].astype(o_ref.dtype)

    return _pallas(
        body, name=name,
        grid_spec=pltpu.PrefetchScalarGridSpec(
            num_scalar_prefetch=1, grid=(r // tr,),
            in_specs=[pl.BlockSpec((tr, cc), lambda i, s: (i, 0))],
            out_specs=pl.BlockSpec((None, tr, cc), lambda i, s: (s[0], i, 0))),
        out_shape=jax.ShapeDtypeStruct((N_CHIPS, r, cc), dtype), compiler_params=_cp("parallel"),
    )(place, w)


def _add_half(name, g, rcv, place):
    nq, r, cc = g.shape
    hr = r // 2

    def body(s_ref, g_ref, r_ref, o_ref):
        o_ref[...] = (g_ref[...] + r_ref[...]).astype(o_ref.dtype)

    spec = pl.BlockSpec((None, hr, cc), lambda i, s: (i, 0, 0))
    return _pallas(
        body, name=name,
        grid_spec=pltpu.PrefetchScalarGridSpec(
            num_scalar_prefetch=1, grid=(nq,),
            in_specs=[pl.BlockSpec((None, hr, cc), lambda i, s: (i, s[1], 0)), spec], out_specs=spec),
        out_shape=jax.ShapeDtypeStruct((nq, hr, cc), BF16), compiler_params=_cp("parallel"),
    )(place, g, rcv)


def _sum_owner(name, part, rcv, place):
    _, hr, cc = part.shape
    tr = _div_tile(hr, 128)
    nb = hr // tr

    def body(s_ref, p_ref, r_ref, o_ref):
        o_ref[...] = ((p_ref[...].astype(F32) + r_ref[0].astype(F32)) + r_ref[1].astype(F32)) + r_ref[2].astype(F32)

    return _pallas(
        body, name=name,
        grid_spec=pltpu.PrefetchScalarGridSpec(
            num_scalar_prefetch=1, grid=(nb,),
            in_specs=[pl.BlockSpec((None, tr, cc), lambda i, s: (s[0], i, 0)),
                      pl.BlockSpec((3, tr, cc), lambda i, s: (0, i, 0))],
            out_specs=pl.BlockSpec((tr, cc), lambda i, s: (s[1] * nb + i, 0))),
        out_shape=jax.ShapeDtypeStruct((2 * hr, cc), F32), compiler_params=_cp("parallel"),
    )(place, part, rcv)


def _sum_small(gathered, local, place):
    n = len(gathered)

    def body(s_ref, *refs):
        g_refs, l_refs, o_refs = refs[:n], refs[n:2 * n], refs[2 * n:]
        me = s_ref[2]
        for g_ref, l_ref, o_ref in zip(g_refs, l_refs, o_refs):
            acc = None
            for d in range(N_DEV):
                term = jnp.where(me == d, l_ref[...], g_ref[d])
                acc = term if acc is None else acc + term
            o_ref[...] = acc

    def whole(shape):
        return pl.BlockSpec(shape, lambda i, s, nd=len(shape): (0,) * nd)

    return _pallas(
        body, name="sum_small",
        grid_spec=pltpu.PrefetchScalarGridSpec(
            num_scalar_prefetch=1, grid=(1,),
            in_specs=[whole(g.shape) for g in gathered] + [whole(a.shape) for a in local],
            out_specs=tuple(whole(a.shape) for a in local)),
        out_shape=tuple(jax.ShapeDtypeStruct(a.shape, a.dtype) for a in local), compiler_params=_cp("arbitrary"),
    )(place, *gathered, *local)


def _adamw(name, w, g, m, v):
    r, cc = w.shape
    tr = r if r * cc <= 128 * 1024 else _div_tile(r, 256)

    def body(w_ref, g_ref, m_ref, v_ref, d_ref, mo_ref, vo_ref):
        gv = g_ref[...]
        mn = ADAM_B1 * m_ref[...] + (1.0 - ADAM_B1) * gv
        vn = ADAM_B2 * v_ref[...] + (1.0 - ADAM_B2) * (gv * gv)
        m_hat = mn / (1.0 - ADAM_B1 ** ADAM_STEP)
        v_hat = vn / (1.0 - ADAM_B2 ** ADAM_STEP)
        d_ref[...] = -ADAM_LR * (m_hat / (jnp.sqrt(v_hat) + ADAM_EPS) + ADAM_WD * w_ref[...])
        mo_ref[...] = mn
        vo_ref[...] = vn

    spec = pl.BlockSpec((tr, cc), lambda i: (i, 0))
    sd = jax.ShapeDtypeStruct((r, cc), F32)
    return _pallas(
        body, name=name, grid=(r // tr,), in_specs=[spec] * 4, out_specs=(spec,) * 3, out_shape=(sd,) * 3,
        compiler_params=_cp("parallel"),
    )(w, g, m, v)


_BIG = ("w_in", "w_up", "w_branch", "w_mem_kv", "w_out", "w_down")
_BIG_SHARD_SHAPE = {"w_in": (1024, 1664), "w_up": (1024, 1408), "w_branch": (1536, 256),
                    "w_mem_kv": (256, 1024), "w_out": (256, 1024), "w_down": (704, 1024)}
_SMALL_SHAPE = {"norm1_g": (1, D_MODEL), "ln_v_g": (1, GM_WIDTH), "ln_v_b": (1, GM_WIDTH),
                "w_spatial": (GM_GROUPS * GM_CHUNK, GM_CHUNK), "b_spatial": (GM_GROUPS, GM_CHUNK),
                "lb_logits": (2, HG_HEADS * HG_DIM), "hgrn_norm_g": (1, HG_DIM), "mem_norm_g": (1, D_MODEL),
                "norm2_g": (1, D_MODEL), "conv_w": (3, D_FF), "conv_b": (1, D_FF), "final_g": (1, D_MODEL)}
_SMALL_EARLY = tuple(n for n in _SMALL_SHAPE if n != "norm1_g")
_PARAM_ORDER = ("norm1_g", "w_in", "ln_v_g", "ln_v_b", "w_spatial", "b_spatial", "lb_logits", "hgrn_norm_g",
                "mem_norm_g", "w_mem_kv", "w_branch", "w_out", "norm2_g", "w_up", "conv_w", "conv_b", "w_down",
                "final_g")


def _adamw_small(ws, gs, ms, vs):
    n = len(ws)

    def body(*refs):
        w_refs, g_refs, m_refs, v_refs = refs[:n], refs[n:2 * n], refs[2 * n:3 * n], refs[3 * n:4 * n]
        d_refs, mo_refs, vo_refs = refs[4 * n:5 * n], refs[5 * n:6 * n], refs[6 * n:]
        for k in range(n):
            gv = g_refs[k][...]
            mn = ADAM_B1 * m_refs[k][...] + (1.0 - ADAM_B1) * gv
            vn = ADAM_B2 * v_refs[k][...] + (1.0 - ADAM_B2) * (gv * gv)
            m_hat = mn / (1.0 - ADAM_B1 ** ADAM_STEP)
            v_hat = vn / (1.0 - ADAM_B2 ** ADAM_STEP)
            d_refs[k][...] = -ADAM_LR * (m_hat / (jnp.sqrt(v_hat) + ADAM_EPS) + ADAM_WD * w_refs[k][...])
            mo_refs[k][...] = mn
            vo_refs[k][...] = vn

    specs = [pl.BlockSpec(a.shape, lambda i: (0, 0)) for a in ws]
    shapes = tuple(jax.ShapeDtypeStruct(a.shape, F32) for a in ws)
    res = _pallas(
        body, name="adamw_small", grid=(1,), in_specs=specs * 4, out_specs=tuple(specs * 3), out_shape=shapes * 3,
        compiler_params=_cp("arbitrary"),
    )(*ws, *gs, *ms, *vs)
    return res[:n], res[n:2 * n], res[2 * n:]


class _Comm:
    _ROW_SHARDED = ("w_mem_kv", "w_out", "w_down")

    def __init__(self, slabs, place):
        self.slabs, self.place = slabs, place
        self.full, self.raw, self.parts, self.landing, self.bufs, self.done = {}, {}, {}, {}, {}, {}
        ex, deliver = self._gather(["w_in"])
        deliver(_run_exchanges("all_gather_w_in", [ex])[0])

    def w(self, name):
        a = self.full[name]
        if name in self._ROW_SHARDED:
            return a.reshape(-1, a.shape[-1])
        if name == "conv_w":
            return jnp.transpose(a, (1, 0, 2)).reshape(3, 1, D_FF)
        return a

    def grad(self, name, arr):
        self.raw[name] = arr.reshape((N_CHIPS, -1, arr.shape[-1]))
        if name == "w_in":
            ex, deliver = self._to_sibling(["w_in"])
            deliver(_run_exchanges("rs_sibling_w_in", [ex])[0])

    def small_grads(self, arrays):
        self.small_local = list(arrays)

    def carry(self, tag, call):
        plan = self._plan(tag)
        if not plan:
            return call(())
        out, carried = call([ex for ex, _ in plan])
        for (_, deliver), res in zip(plan, carried):
            deliver(res)
        return out

    def finish(self, last_small):
        ex, deliver = self._share(["w_out", "w_branch", "w_mem_kv", "w_in"])
        shared, small = _run_exchanges("share_and_gather_last", [ex, _ex_gather_small(last_small)])
        deliver(shared)
        return self.done, self.small_local + list(last_small), self.small_everyone + small

    def _plan(self, tag):
        if tag == "in_proj":
            return [self._gather(["w_branch", "w_out", "w_mem_kv", "w_down", "conv_w"])]
        if tag == "hgrn_fwd":
            return [self._gather(["w_up"])]
        if tag == "d_h2":
            return [self._to_sibling(["w_down", "w_up"])]
        if tag == "hgrn_bwd":
            return [self._to_owner(["w_down", "w_up"]), self._to_sibling(["w_out", "w_branch", "w_mem_kv"])]
        if tag == "g_w_in":
            def keep(res):
                self.small_everyone = res

            return [self._to_owner(["w_out", "w_branch", "w_mem_kv"]), self._share(["w_down", "w_up"]),
                    (_ex_gather_small(self.small_local), keep)]
        if tag == "d_h":
            return [self._to_owner(["w_in"])]
        return []

    def _gather(self, names, part=(0, 1)):
        def deliver(res):
            self.slabs.update(zip(names, res))
            self.full.update(zip(names, res))

        return _ex_all_gather([self.slabs[n] for n in names], [n != "conv_w" for n in names], part), deliver

    def _to_sibling(self, names):
        def deliver(res):
            for n, r in zip(names, res):
                self.parts[n] = _add_half("rs_add_" + n, self.raw[n], r, self.place)

        return _ex_to_sibling([self.raw[n] for n in names]), deliver

    def _to_owner(self, names, part=(0, 1)):
        def deliver(res):
            for n, r in zip(names, res):
                if part[0] + 1 < part[1]:
                    self.landing[n] = r
                else:
                    self.bufs[n] = _sum_owner("rs_sum_" + n, self.parts[n], r, self.place)

        landing = [self.landing[n] for n in names] if part[0] else None
        return _ex_to_owner([self.parts[n] for n in names], part, landing), deliver

    def _share(self, names):
        return _ex_share_halves([self.bufs[n] for n in names]), lambda res: self.done.update(zip(names, res))


def kernel(x, mem, norm1_g, w_in, ln_v_g, ln_v_b, w_spatial, b_spatial, lb_logits, hgrn_norm_g, mem_norm_g, w_mem_kv, w_branch, w_out, norm2_g, w_up, conv_w, conv_b, w_down, final_g, loss_target, m_norm1_g, m_w_in, m_ln_v_g, m_ln_v_b, m_w_spatial, m_b_spatial, m_lb_logits, m_hgrn_norm_g, m_mem_norm_g, m_w_mem_kv, m_w_branch, m_w_out, m_norm2_g, m_w_up, m_conv_w, m_conv_b, m_w_down, m_final_g, v_norm1_g, v_w_in, v_ln_v_g, v_ln_v_b, v_w_spatial, v_b_spatial, v_lb_logits, v_hgrn_norm_g, v_mem_norm_g, v_w_mem_kv, v_w_branch, v_w_out, v_norm2_g, v_w_up, v_conv_w, v_conv_b, v_w_down, v_final_g):
    w = dict(norm1_g=norm1_g, w_in=w_in, ln_v_g=ln_v_g, ln_v_b=ln_v_b, w_spatial=w_spatial, b_spatial=b_spatial,
             lb_logits=lb_logits, hgrn_norm_g=hgrn_norm_g, mem_norm_g=mem_norm_g, w_mem_kv=w_mem_kv,
             w_branch=w_branch, w_out=w_out, norm2_g=norm2_g, w_up=w_up, conv_w=conv_w, conv_b=conv_b,
             w_down=w_down, final_g=final_g)
    mom = dict(norm1_g=m_norm1_g, w_in=m_w_in, ln_v_g=m_ln_v_g, ln_v_b=m_ln_v_b, w_spatial=m_w_spatial,
               b_spatial=m_b_spatial, lb_logits=m_lb_logits, hgrn_norm_g=m_hgrn_norm_g, mem_norm_g=m_mem_norm_g,
               w_mem_kv=m_w_mem_kv, w_branch=m_w_branch, w_out=m_w_out, norm2_g=m_norm2_g, w_up=m_w_up,
               conv_w=m_conv_w, conv_b=m_conv_b, w_down=m_w_down, final_g=m_final_g)
    var = dict(norm1_g=v_norm1_g, w_in=v_w_in, ln_v_g=v_ln_v_g, ln_v_b=v_ln_v_b, w_spatial=v_w_spatial,
               b_spatial=v_b_spatial, lb_logits=v_lb_logits, hgrn_norm_g=v_hgrn_norm_g, mem_norm_g=v_mem_norm_g,
               w_mem_kv=v_w_mem_kv, w_branch=v_w_branch, w_out=v_w_out, norm2_g=v_norm2_g, w_up=v_w_up,
               conv_w=v_conv_w, conv_b=v_conv_b, w_down=v_w_down, final_g=v_final_g)
    B, S, D = x.shape
    T = B * S
    ci = lax.axis_index("c")
    q = 2 * lax.axis_index("x") + lax.axis_index("y")
    place = jnp.stack([q, ci, 2 * q + ci]).astype(jnp.int32)

    slabs = {n: _cast_into_slab("slab_" + n, w[n].reshape(_BIG_SHARD_SHAPE[n]), place, BF16) for n in _BIG}
    slabs["conv_w"] = _cast_into_slab("slab_conv_w", conv_w[0], place, F32)
    comm = _Comm(slabs, place)
    p = dict(
        norm1_g=norm1_g, ln_v_g=ln_v_g, ln_v_b=ln_v_b, w_spatial=w_spatial[0],
        b_spatial=b_spatial.reshape(GM_GROUPS, GM_CHUNK, 1), lb_logits=lb_logits, hgrn_norm_g=hgrn_norm_g,
        mem_norm_g=mem_norm_g, norm2_g=norm2_g, conv_b=conv_b, final_g=final_g.reshape(1, D))

    loss, grad_x, g = _local_step(x.reshape(T, D), mem.reshape(B * MEM_LEN, D), loss_target.reshape(T, D), p, comm,
                                  B, S)

    shard_grads, local_small, everyone = comm.finish([g["norm1_g"]])
    summed = _sum_small(everyone, local_small, place)
    small_names = list(_SMALL_EARLY) + ["norm1_g"]
    total = dict(zip(_SMALL_EARLY, summed))
    loss_total, total["norm1_g"] = summed[len(_SMALL_EARLY)][0, 0], summed[-1]

    grads, delta, new_m, new_v = {}, {}, {}, {}
    for n in _BIG:
        shp = _BIG_SHARD_SHAPE[n]
        grads[n] = shard_grads[n]
        delta[n], new_m[n], new_v[n] = _adamw("adamw_" + n, w[n].reshape(shp), shard_grads[n],
                                              mom[n].reshape(shp), var[n].reshape(shp))
    cw_shard = D_FF // N_CHIPS
    total["conv_w"] = lax.dynamic_slice(total["conv_w"], (0, q * cw_shard), (3, cw_shard))

    def flat2d(d, n):
        return d[n].reshape(total[n].shape)

    upd = _adamw_small([flat2d(w, n) for n in small_names], [total[n] for n in small_names],
                       [flat2d(mom, n) for n in small_names], [flat2d(var, n) for n in small_names])
    for k, n in enumerate(small_names):
        grads[n], delta[n], new_m[n], new_v[n] = total[n], upd[0][k], upd[1][k], upd[2][k]

    def shaped(d):
        return [d[n].reshape(w[n].shape) for n in _PARAM_ORDER]

    return (loss_total, grad_x.reshape(B, S, D), *shaped(grads), *shaped(delta), *shaped(new_m), *shaped(new_v))
```

```python
import functools
import math

import jax
import jax.numpy as jnp
from jax import lax
from jax.experimental import pallas as pl
from jax.experimental.pallas import tpu as pltpu

F32 = jnp.float32
BF16 = jnp.bfloat16
EPS = 1e-6

D_MODEL = 1024
MEM_LEN = 256
GM_WIDTH = 512
GM_CHUNK = 128
GM_GROUPS = 4
HG_HEADS = 4
HG_DIM = 128
HG_CHUNK = 64
XA_HEADS = 4
XA_DIM = 128
BR_WIDTH = 512
D_FF = 2816
IN_WIDTH = 6656
N_CHIPS = 4
N_DEV = 8

ADAM_LR = 0.001
ADAM_B1 = 0.9
ADAM_B2 = 0.999
ADAM_EPS = 1e-08
ADAM_WD = 0.01
ADAM_STEP = 10

COL_ZU, COL_ZV, COL_HQ, COL_HF, COL_HI, COL_HG, COL_XQ = 0, 1, 2, 3, 4, 5, 6
COL_GATE0 = 3584

VMEM_LIMIT_BYTES = 48 * 1024 * 1024
MESH_ID = pl.DeviceIdType.MESH


def _cp(*sem):
    return pltpu.CompilerParams(dimension_semantics=sem, vmem_limit_bytes=VMEM_LIMIT_BYTES)


def _pallas(body, *, out_shape, **kw):
    def pin(s):
        return pltpu.HBM(s.shape, s.dtype) if isinstance(s, jax.ShapeDtypeStruct) else s

    out_shape = tuple(pin(s) for s in out_shape) if isinstance(out_shape, (tuple, list)) else pin(out_shape)
    call = pl.pallas_call(body, out_shape=out_shape, **kw)

    def run(*operands):
        return call(*[pltpu.with_memory_space_constraint(o, pltpu.HBM) if jnp.issubdtype(o.dtype, jnp.floating)
                      else o for o in operands])

    return run


def _dot(a, b):
    return lax.dot_general(a.astype(BF16), b.astype(BF16), (((1,), (0,)), ((), ())), preferred_element_type=F32)


def _dot_nt(a, b):
    return lax.dot_general(a.astype(BF16), b.astype(BF16), (((1,), (1,)), ((), ())), preferred_element_type=F32)


def _dot_tn(a, b):
    return lax.dot_general(a.astype(BF16), b.astype(BF16), (((0,), (0,)), ((), ())), preferred_element_type=F32)


def _dot_01(mask01, x):
    hi = x.astype(BF16)
    r1 = x - hi.astype(F32)
    mid = r1.astype(BF16)
    lo = (r1 - mid.astype(F32)).astype(BF16)
    m = mask01.astype(BF16)
    dn = (((1,), (0,)), ((), ()))
    return (lax.dot_general(m, hi, dn, preferred_element_type=F32)
            + lax.dot_general(m, mid, dn, preferred_element_type=F32)
            + lax.dot_general(m, lo, dn, preferred_element_type=F32))


def _sigmoid(z):
    return 1.0 / (1.0 + jnp.exp(-z))


_GELU_C = math.sqrt(2.0 / math.pi)


def _gelu_and_grad(z):
    inner = _GELU_C * (z + 0.044715 * z * z * z)
    t = jnp.tanh(inner)
    val = 0.5 * z * (1.0 + t)
    grad = 0.5 * (1.0 + t) + 0.5 * z * (1.0 - t * t) * _GELU_C * (1.0 + 3.0 * 0.044715 * z * z)
    return val, grad


def _row_tile(n, want=512):
    t = min(want, n)
    assert n % t == 0
    return t


def _pcall(body, operands, *, name, grid, in_specs, out_specs, out_shape, scratch_shapes=(), semantics, riders=()):
    single = not isinstance(out_shape, (tuple, list))
    out_specs = (out_specs,) if single else tuple(out_specs)
    out_shape = (out_shape,) if single else tuple(out_shape)
    if not riders:
        res = _pallas(body, name=name, grid=grid, in_specs=list(in_specs), out_specs=out_specs,
                      out_shape=out_shape, scratch_shapes=list(scratch_shapes),
                      compiler_params=_cp(*semantics))(*operands)
        return (res[0] if single else res), []
    n_in, n_out, n_scr = len(in_specs), len(out_shape), len(scratch_shapes)
    ex_in = [len(ex.operands) for ex in riders]
    ex_out = [len(ex.out_shape) for ex in riders]
    ex_scr = [len(ex.scratch) for ex in riders]
    tot_in, tot_out = n_in + sum(ex_in), n_out + sum(ex_out)

    def wrapped(*refs):
        ins, outs, scr = refs[:tot_in], refs[tot_in:tot_in + tot_out], refs[tot_in + tot_out:]
        ids = [pl.program_id(d) for d in range(len(grid))]
        first = functools.reduce(lambda p, t: p & t, [i == 0 for i in ids])
        last = functools.reduce(lambda p, t: p & t, [i == n - 1 for i, n in zip(ids, grid)])
        parts, oi, oo, os_ = [], n_in, n_out, n_scr
        for k in range(len(riders)):
            parts.append((ins[oi:oi + ex_in[k]], outs[oo:oo + ex_out[k]], scr[os_:os_ + ex_scr[k]]))
            oi, oo, os_ = oi + ex_in[k], oo + ex_out[k], os_ + ex_scr[k]

        @pl.when(first)
        def _():
            for ex, part in zip(riders, parts):
                ex.start(*part)

        body(*ins[:n_in], *outs[:n_out], *scr[:n_scr])

        @pl.when(last)
        def _():
            for ex, part in zip(riders, parts):
                ex.finish(*part)

    aliases, oi, oo = {}, n_in, n_out
    all_ops, all_shapes, all_scr = list(operands), list(out_shape), list(scratch_shapes)
    for k, ex in enumerate(riders):
        aliases.update({oi + a: oo + b for a, b in ex.aliases.items()})
        oi, oo = oi + ex_in[k], oo + ex_out[k]
        all_ops += list(ex.operands)
        all_shapes += [pltpu.HBM(s.shape, s.dtype) for s in ex.out_shape]
        all_scr += list(ex.scratch)
    res = _pallas(
        wrapped, name=name, grid=grid, in_specs=list(in_specs) + [HBM_SPEC] * sum(ex_in),
        out_specs=out_specs + (HBM_SPEC,) * sum(ex_out), out_shape=tuple(all_shapes), scratch_shapes=all_scr,
        input_output_aliases=aliases, compiler_params=_cp(*(["arbitrary"] * len(grid))))(*all_ops)
    own = res[0] if single else tuple(res[:n_out])
    carried, oo = [], n_out
    for k in range(len(riders)):
        carried.append(list(res[oo:oo + ex_out[k]]))
        oo += ex_out[k]
    return own, carried


def _carried(out, carried, riders):
    return (out, carried) if riders else out


def _matmul(name, operands, *, grid, in_specs, o_spec, out_shape, out_dtype, dims, has_res=False, riders=()):
    nk = grid[2]
    assert nk == 1 or (out_dtype == F32 and not has_res)

    def body(*refs):
        if has_res:
            a_ref, b_ref, r_ref, o_ref = refs
        else:
            a_ref, b_ref, o_ref = refs
            r_ref = None
        part = lax.dot_general(a_ref[...].astype(BF16), b_ref[...].astype(BF16), (dims, ((), ())),
                               preferred_element_type=F32)
        if nk == 1:
            if r_ref is not None:
                part = part + r_ref[...]
            o_ref[...] = part.astype(o_ref.dtype)
        else:
            k = pl.program_id(2)

            @pl.when(k == 0)
            def _():
                o_ref[...] = part

            @pl.when(k > 0)
            def _():
                o_ref[...] += part

    out, carried = _pcall(body, operands, name=name, grid=grid, in_specs=in_specs, out_specs=o_spec,
                          out_shape=jax.ShapeDtypeStruct(out_shape, out_dtype),
                          semantics=("parallel", "parallel", "arbitrary"), riders=riders)
    return (out, carried) if riders else out


NN = ((1,), (0,))
NT = ((1,), (1,))
TN = ((0,), (0,))
_TN_TOKENS = 4096


def _mm_cs(name, a, w, out_dtype, riders=()):
    M, K = a.shape
    nq, _, wd = w.shape
    tm = _row_tile(M)
    return _matmul(name, (a, w), grid=(nq, M // tm, 1),
                   in_specs=[pl.BlockSpec((tm, K), lambda j, i, k: (i, 0)),
                             pl.BlockSpec((None, K, wd), lambda j, i, k: (j, 0, 0))],
                   o_spec=pl.BlockSpec((tm, wd), lambda j, i, k: (i, j)),
                   out_shape=(M, nq * wd), out_dtype=out_dtype, dims=NN, riders=riders)


def _mm_rs(name, a, w, out_dtype, res=None):
    M, K = a.shape
    N = w.shape[1]
    tm = _row_tile(M)
    tn = N
    ops = (a, w) if res is None else (a, w, res)
    in_specs = [pl.BlockSpec((tm, K), lambda i, j, k: (i, 0)),
                pl.BlockSpec((K, tn), lambda i, j, k: (0, j))]
    if res is not None:
        in_specs.append(pl.BlockSpec((tm, tn), lambda i, j, k: (i, j)))
    return _matmul(name, ops, grid=(M // tm, N // tn, 1), in_specs=in_specs,
                   o_spec=pl.BlockSpec((tm, tn), lambda i, j, k: (i, j)),
                   out_shape=(M, N), out_dtype=out_dtype, dims=NN, has_res=res is not None)


def _mm_nt_rs(name, g, w, out_dtype, riders=()):
    M, N = g.shape
    K = w.shape[0]
    to = K
    tm = _row_tile(M)
    return _matmul(name, (g, w), grid=(M // tm, K // to, 1),
                   in_specs=[pl.BlockSpec((tm, N), lambda i, j, k: (i, 0)),
                             pl.BlockSpec((to, N), lambda i, j, k: (j, 0))],
                   o_spec=pl.BlockSpec((tm, to), lambda i, j, k: (i, j)),
                   out_shape=(M, K), out_dtype=out_dtype, dims=NT, riders=riders)


def _mm_nt_cs(name, g, w, out_dtype, riders=(), stacked=False, norm_bwd=None):
    M = g.shape[-2]
    nq, K, wd = w.shape
    tm = _row_tile(M, 256)

    def product(g_ref, w_ref):
        acc = None
        for q in range(nq):
            gq = g_ref[q // 2, :, (q % 2) * wd:(q % 2 + 1) * wd] if stacked else g_ref[:, q * wd:(q + 1) * wd]
            part = _dot_nt(gq, w_ref[q])
            acc = part if acc is None else acc + part
        return acc

    def body(g_ref, w_ref, o_ref):
        o_ref[...] = product(g_ref, w_ref).astype(o_ref.dtype)

    def body_norm(g_ref, w_ref, x_ref, gain_ref, dr_ref, dx_ref, dg_ref):
        @pl.when(pl.program_id(0) == 0)
        def _():
            dg_ref[...] = jnp.zeros_like(dg_ref)

        dx, dg = _rms_bwd_rows(x_ref[...], gain_ref[...], product(g_ref, w_ref))
        dg_ref[...] += dg
        dx_ref[...] = dx + dr_ref[...]

    g_spec = (pl.BlockSpec((2, tm, 2 * wd), lambda i: (0, i, 0)) if stacked
              else pl.BlockSpec((tm, nq * wd), lambda i: (i, 0)))
    w_spec = pl.BlockSpec((nq, K, wd), lambda i: (0, 0, 0))
    row = pl.BlockSpec((tm, K), lambda i: (i, 0))
    if norm_bwd is None:
        return _carried(*_pcall(
            body, (g, w), name=name, grid=(M // tm,), in_specs=[g_spec, w_spec], out_specs=row,
            out_shape=jax.ShapeDtypeStruct((M, K), out_dtype), semantics=("parallel",), riders=riders), riders)
    vec = pl.BlockSpec((1, K), lambda i: (0, 0))
    return _carried(*_pcall(
        body_norm, (g, w) + tuple(norm_bwd), name=name, grid=(M // tm,),
        in_specs=[g_spec, w_spec, row, vec, row], out_specs=(row, vec),
        out_shape=(jax.ShapeDtypeStruct((M, K), F32), jax.ShapeDtypeStruct((1, K), F32)),
        semantics=("arbitrary",), riders=riders), riders)


def _mm_tn_rs(name, a, g, to, tn=512):
    T, M = a.shape
    N = g.shape[1]
    tt = _row_tile(T, _TN_TOKENS)
    tn = min(tn, N)
    return _matmul(name, (a, g), grid=(M // to, N // tn, T // tt),
                   in_specs=[pl.BlockSpec((tt, to), lambda i, j, k: (k, i)),
                             pl.BlockSpec((tt, tn), lambda i, j, k: (k, j))],
                   o_spec=pl.BlockSpec((to, tn), lambda i, j, k: (i, j)),
                   out_shape=(M, N), out_dtype=F32, dims=TN)


def _mm_tn_cs(name, a, g, nq, to, riders=(), stacked=False):
    T, M = a.shape
    wd = g.shape[-1] * (2 if stacked else 1) // nq
    tt = _row_tile(T, _TN_TOKENS)
    g_spec = (pl.BlockSpec((None, tt, wd), lambda i, j, k: (j // 2, k, j % 2)) if stacked
              else pl.BlockSpec((tt, wd), lambda i, j, k: (k, j)))
    return _matmul(name, (a, g), grid=(M // to, nq, T // tt),
                   in_specs=[pl.BlockSpec((tt, to), lambda i, j, k: (k, i)), g_spec],
                   o_spec=pl.BlockSpec((None, to, wd), lambda i, j, k: (j, i, 0)),
                   out_shape=(nq, M, wd), out_dtype=F32, dims=TN, riders=riders)


def _rms_fwd(name, x, g):
    T, D = x.shape
    tm = _row_tile(T)

    def body(x_ref, g_ref, o_ref):
        xv = x_ref[...]
        r = lax.rsqrt(jnp.mean(xv * xv, axis=-1, keepdims=True) + EPS)
        o_ref[...] = (xv * r * g_ref[...]).astype(o_ref.dtype)

    return _pallas(
        body, name=name, grid=(T // tm,),
        in_specs=[pl.BlockSpec((tm, D), lambda i: (i, 0)), pl.BlockSpec((1, D), lambda i: (0, 0))],
        out_specs=pl.BlockSpec((tm, D), lambda i: (i, 0)),
        out_shape=jax.ShapeDtypeStruct((T, D), BF16), compiler_params=_cp("parallel"),
    )(x, g)


def _rms_rows(xv, gain):
    return xv * lax.rsqrt(jnp.mean(xv * xv, axis=-1, keepdims=True) + EPS) * gain


def _rms_bwd_rows(xv, gain, dh):
    r = lax.rsqrt(jnp.mean(xv * xv, axis=-1, keepdims=True) + EPS)
    n = xv * r
    dn = dh * gain
    return r * (dn - n * jnp.mean(dn * n, axis=-1, keepdims=True)), jnp.sum(dh * n, axis=0, keepdims=True)


def _rms_bwd(name, x, g, dh, dres):
    T, D = x.shape
    tm = _row_tile(T)
    has_res = dres is not None

    def body(*refs):
        if has_res:
            x_ref, g_ref, dh_ref, dr_ref, dx_ref, dg_ref = refs
        else:
            x_ref, g_ref, dh_ref, dx_ref, dg_ref = refs

        @pl.when(pl.program_id(0) == 0)
        def _():
            dg_ref[...] = jnp.zeros_like(dg_ref)

        dx, dg = _rms_bwd_rows(x_ref[...], g_ref[...], dh_ref[...])
        dg_ref[...] += dg
        if has_res:
            dx = dx + dr_ref[...]
        dx_ref[...] = dx

    row = pl.BlockSpec((tm, D), lambda i: (i, 0))
    vec = pl.BlockSpec((1, D), lambda i: (0, 0))
    ops = (x, g, dh, dres) if has_res else (x, g, dh)
    return _pallas(
        body, name=name, grid=(T // tm,), in_specs=[row, vec, row] + ([row] if has_res else []),
        out_specs=(row, vec),
        out_shape=(jax.ShapeDtypeStruct((T, D), F32), jax.ShapeDtypeStruct((1, D), F32)),
        compiler_params=_cp("arbitrary"),
    )(*ops)


def _proj_res_norm(name, a, w, res, gain):
    M, K = a.shape
    N = w.shape[1]
    tm = _row_tile(M)

    def body(a_ref, w_ref, r_ref, g_ref, x_ref, h_ref):
        xv = _dot(a_ref[...], w_ref[...]) + r_ref[...]
        x_ref[...] = xv
        h_ref[...] = _rms_rows(xv, g_ref[...]).astype(h_ref.dtype)

    row = pl.BlockSpec((tm, N), lambda i: (i, 0))
    return _pallas(
        body, name=name, grid=(M // tm,),
        in_specs=[pl.BlockSpec((tm, K), lambda i: (i, 0)), pl.BlockSpec((K, N), lambda i: (0, 0)), row,
                  pl.BlockSpec((1, N), lambda i: (0, 0))],
        out_specs=(row, row), out_shape=(jax.ShapeDtypeStruct((M, N), F32), jax.ShapeDtypeStruct((M, N), BF16)),
        compiler_params=_cp("parallel"),
    )(a, w, res, gain)


def _proj_res_loss(name, a, w, res, tgt, gain):
    M, K = a.shape
    D = w.shape[1]
    tm = _row_tile(M)

    def body(a_ref, w_ref, r_ref, t_ref, g_ref, dx_ref, dg_ref, loss_ref):
        @pl.when(pl.program_id(0) == 0)
        def _():
            dg_ref[...] = jnp.zeros_like(dg_ref)
            loss_ref[...] = jnp.zeros_like(loss_ref)

        xv = _dot(a_ref[...], w_ref[...]) + r_ref[...]
        gv = g_ref[...]
        diff = _rms_rows(xv, gv) - t_ref[...]
        loss_ref[...] += 0.5 * jnp.sum(jnp.mean(diff * diff, axis=-1, keepdims=True))
        dx, dg = _rms_bwd_rows(xv, gv, diff * (1.0 / D))
        dg_ref[...] += dg
        dx_ref[...] = dx

    row = pl.BlockSpec((tm, D), lambda i: (i, 0))
    vec = pl.BlockSpec((1, D), lambda i: (0, 0))
    return _pallas(
        body, name=name, grid=(M // tm,),
        in_specs=[pl.BlockSpec((tm, K), lambda i: (i, 0)), pl.BlockSpec((K, D), lambda i: (0, 0)), row, row, vec],
        out_specs=(row, vec, pl.BlockSpec((8, 128), lambda i: (0, 0))),
        out_shape=(jax.ShapeDtypeStruct((M, D), F32), jax.ShapeDtypeStruct((1, D), F32),
                   jax.ShapeDtypeStruct((8, 128), F32)),
        compiler_params=_cp("arbitrary"),
    )(a, w, res, tgt, gain)


def _gmlp_pieces(zu, zv, lng, lnb, ws_ref, bs_ref):
    u, du = _gelu_and_grad(zu)
    v, dv = _gelu_and_grad(zv)
    mu = jnp.mean(v, axis=-1, keepdims=True)
    vc = v - mu
    rstd = lax.rsqrt(jnp.mean(vc * vc, axis=-1, keepdims=True) + EPS)
    vhat = vc * rstd
    vn = vhat * lng + lnb
    row = lax.broadcasted_iota(jnp.int32, (GM_CHUNK, GM_CHUNK), 0)
    col = lax.broadcasted_iota(jnp.int32, (GM_CHUNK, GM_CHUNK), 1)
    tril = row >= col
    wms, mixed = [], []
    for g in range(GM_GROUPS):
        sl = slice(g * 128, (g + 1) * 128)
        wm = jnp.where(tril, ws_ref[g], 0.0)
        wms.append(wm)
        mixed.append(_dot(wm, vn[:, sl]) + bs_ref[g])
    return u, du, dv, rstd, vhat, vn, wms, mixed, tril


def _gmlp_fwd(proj, lng, lnb, ws, bs_col):
    T = proj.shape[0]
    n = T // GM_CHUNK

    def body(zu_ref, zv_ref, lng_ref, lnb_ref, ws_ref, bs_ref, o_ref):
        u, _, _, _, _, _, _, mixed, _ = _gmlp_pieces(zu_ref[...].astype(F32), zv_ref[...].astype(F32),
                                                     lng_ref[...], lnb_ref[...],
                                                     ws_ref, bs_ref)
        for g in range(GM_GROUPS):
            sl = slice(g * 128, (g + 1) * 128)
            o_ref[:, sl] = (u[:, sl] * mixed[g]).astype(o_ref.dtype)

    vec = pl.BlockSpec((1, GM_WIDTH), lambda i: (0, 0))
    return _pallas(
        body, name="gmlp_fwd", grid=(n,),
        in_specs=[pl.BlockSpec((GM_CHUNK, 512), lambda i: (i, COL_ZU)),
                  pl.BlockSpec((GM_CHUNK, 512), lambda i: (i, COL_ZV)),
                  vec, vec,
                  pl.BlockSpec((GM_GROUPS, 128, 128), lambda i: (0, 0, 0)),
                  pl.BlockSpec((GM_GROUPS, 128, 1), lambda i: (0, 0, 0))],
        out_specs=pl.BlockSpec((GM_CHUNK, 512), lambda i: (i, 0)),
        out_shape=jax.ShapeDtypeStruct((T, GM_WIDTH), BF16), compiler_params=_cp("parallel"),
    )(proj, proj, lng, lnb, ws, bs_col)


def _gmlp_bwd(proj, d_out, lng, lnb, ws, bs_col, riders=()):
    T = proj.shape[0]
    n = T // GM_CHUNK

    def body(zu_ref, zv_ref, do_ref, lng_ref, lnb_ref, ws_ref, bs_ref,
             dz_ref, dws_ref, dbs_ref, dlng_ref, dlnb_ref, dm_acc):
        i = pl.program_id(0)

        @pl.when(i == 0)
        def _():
            dws_ref[...] = jnp.zeros_like(dws_ref)
            dlng_ref[...] = jnp.zeros_like(dlng_ref)
            dlnb_ref[...] = jnp.zeros_like(dlnb_ref)
            dm_acc[...] = jnp.zeros_like(dm_acc)

        lng_v = lng_ref[...]
        u, du, dv, rstd, vhat, vn, wms, mixed, tril = _gmlp_pieces(zu_ref[...].astype(F32), zv_ref[...].astype(F32),
                                                                  lng_v, lnb_ref[...],
                                                                  ws_ref, bs_ref)
        do = do_ref[...]
        dvn_parts = []
        for g in range(GM_GROUPS):
            sl = slice(g * 128, (g + 1) * 128)
            dog = do[:, sl]
            dz_ref[:, sl] = (dog * mixed[g] * du[:, sl]).astype(dz_ref.dtype)
            dmix = dog * u[:, sl]
            dm_acc[:, sl] += dmix
            dws_ref[g] += jnp.where(tril, _dot_nt(dmix, vn[:, sl]), 0.0)
            dvn_parts.append(_dot_tn(wms[g], dmix))
        dvn = jnp.concatenate(dvn_parts, axis=1)
        dlng_ref[...] += jnp.sum(dvn * vhat, axis=0, keepdims=True)
        dlnb_ref[...] += jnp.sum(dvn, axis=0, keepdims=True)
        dvh = dvn * lng_v
        dvv = rstd * (dvh - jnp.mean(dvh, axis=-1, keepdims=True)
                      - vhat * jnp.mean(dvh * vhat, axis=-1, keepdims=True))
        dz_ref[:, GM_WIDTH:] = (dvv * dv).astype(dz_ref.dtype)

        @pl.when(i == n - 1)
        def _():
            for g in range(GM_GROUPS):
                dbs_ref[g] = jnp.sum(dm_acc[:, g * 128:(g + 1) * 128], axis=1, keepdims=True)

    vec = pl.BlockSpec((1, GM_WIDTH), lambda i: (0, 0))
    wsp = pl.BlockSpec((GM_GROUPS, 128, 128), lambda i: (0, 0, 0))
    bsp = pl.BlockSpec((GM_GROUPS, 128, 1), lambda i: (0, 0, 0))
    return _carried(*_pcall(
        body, (proj, proj, d_out, lng, lnb, ws, bs_col), name="gmlp_bwd", grid=(n,),
        in_specs=[pl.BlockSpec((GM_CHUNK, 512), lambda i: (i, COL_ZU)),
                  pl.BlockSpec((GM_CHUNK, 512), lambda i: (i, COL_ZV)),
                  pl.BlockSpec((None, GM_CHUNK, 512), lambda i: (0, i, 0)), vec, vec, wsp, bsp],
        out_specs=(pl.BlockSpec((GM_CHUNK, 2 * GM_WIDTH), lambda i: (i, 0)), wsp, bsp, vec, vec),
        out_shape=(jax.ShapeDtypeStruct((T, 2 * GM_WIDTH), BF16),
                   jax.ShapeDtypeStruct((GM_GROUPS, 128, 128), F32), jax.ShapeDtypeStruct((GM_GROUPS, 128, 1), F32),
                   jax.ShapeDtypeStruct((1, GM_WIDTH), F32), jax.ShapeDtypeStruct((1, GM_WIDTH), F32)),
        scratch_shapes=[pltpu.VMEM((GM_CHUNK, GM_WIDTH), F32)],
        semantics=("arbitrary",), riders=riders), riders)


def _hgrn_lower_bound(lbl):
    return 1.0 / (1.0 + jnp.exp(lbl[1:2, :] - lbl[0:1, :]))


def _hgrn_gates(hq, hf, lb):
    C = HG_CHUNK
    sg = _sigmoid(hf)
    fg = lb + (1.0 - lb) * sg
    sq = _sigmoid(hq)
    row = lax.broadcasted_iota(jnp.int32, (C, C), 0)
    col = lax.broadcasted_iota(jnp.int32, (C, C), 1)
    tril = row >= col
    logf = jnp.log(fg)
    a = _dot_01(tril, logf)
    a_last = jnp.sum(logf, axis=0, keepdims=True)
    first_half = lax.broadcasted_iota(jnp.int32, logf.shape, 0) < (C // 2)
    a_mid = jnp.sum(jnp.where(first_half, logf, 0.0), axis=0, keepdims=True)
    ea, ei, eki, ekl = jnp.exp(a), jnp.exp(a - a_mid), jnp.exp(a_mid - a), jnp.exp(a_last - a)
    k = 1.0 - fg
    q = hq * sq
    qi = (q * ei).astype(BF16).astype(F32)
    ki = (k * eki).astype(BF16).astype(F32)
    return dict(sg=sg, fg=fg, sq=sq, tril=tril, ea=ea, ei=ei, eki=eki, ekl=ekl, e_last=jnp.exp(a_last),
                qe=q * ea, qi=qi, ki=ki, kl=k * ekl)


def _heads(x):
    return [x[:, h * HG_DIM:(h + 1) * HG_DIM] for h in range(HG_HEADS)]


def _hgrn_fwd(proj, lbl, gh, B, S, riders=()):
    C = HG_CHUNK
    NC = S // C
    W = HG_HEADS * HG_DIM

    def body(q_ref, f_ref, i_ref, g_ref, lbl_ref, gh_ref, o_ref, bo_ref, st_ref, state):
        @pl.when(pl.program_id(0) == 0)
        def _():
            state[...] = jnp.zeros_like(state)

        lb = _hgrn_lower_bound(lbl_ref[...])
        ghv = gh_ref[...]
        for b in range(B):
            gt = _hgrn_gates(q_ref[b].astype(F32), f_ref[b].astype(F32), lb)
            v = _heads(i_ref[b])
            qe, qi, ki, kl, e_last = (_heads(gt[n]) for n in ("qe", "qi", "ki", "kl", "e_last"))
            outs, normed = [], []
            for h in range(HG_HEADS):
                p = jnp.where(gt["tril"], _dot_nt(qi[h], ki[h]), 0.0)
                st = state[b, h]
                st_ref[b, h] = st
                o = _dot_nt(qe[h], st) + _dot(p, v[h])
                state[b, h] = st * e_last[h] + _dot_tn(v[h], kl[h])
                outs.append(o)
                normed.append(o * lax.rsqrt(jnp.mean(o * o, axis=-1, keepdims=True) + EPS) * ghv)
            o_ref[b] = jnp.concatenate(outs, axis=1)
            hg = g_ref[b].astype(F32)
            bo_ref[b] = (jnp.concatenate(normed, axis=1) * (hg * _sigmoid(hg))).astype(bo_ref.dtype)

    def col(cb):
        return pl.BlockSpec((B, C, 512), lambda c: (0, c, cb))

    tile = pl.BlockSpec((B, C, W), lambda c: (0, c, 0))
    proj3 = proj.reshape(B, S, proj.shape[-1])
    out, carried = _pcall(
        body, (proj3, proj3, proj3, proj3, lbl, gh), name="hgrn_fwd", grid=(NC,),
        in_specs=[col(COL_HQ), col(COL_HF), col(COL_HI), col(COL_HG),
                  pl.BlockSpec((2, W), lambda c: (0, 0)), pl.BlockSpec((1, HG_DIM), lambda c: (0, 0))],
        out_specs=(tile, tile, pl.BlockSpec((B, None, HG_HEADS, 128, 128), lambda c: (0, c, 0, 0, 0))),
        out_shape=(jax.ShapeDtypeStruct((B, S, W), F32), jax.ShapeDtypeStruct((B, S, W), BF16),
                   jax.ShapeDtypeStruct((B, NC, HG_HEADS, 128, 128), F32)),
        scratch_shapes=[pltpu.VMEM((B, HG_HEADS, 128, 128), F32)],
        semantics=("arbitrary",), riders=riders)
    o_h, b_out, states = out
    out = (o_h, b_out.reshape(B * S, W), states)
    return (out, carried) if riders else out


def _hgrn_bwd(proj, o_saved, states, d_out, lbl, gh, others, B, S, riders=()):
    C = HG_CHUNK
    NC = S // C
    W = HG_HEADS * HG_DIM
    d_gm, d_xq, d_gates = (t.reshape(B, S, t.shape[-1]) for t in others)
    own0 = d_gm.shape[-1]
    xq0 = own0 + 4 * W
    gates0 = xq0 + d_xq.shape[-1]

    def body(q_ref, f_ref, i_ref, g_ref, o_ref, st_ref, do_ref, lbl_ref, gh_ref, gm_ref, xq_ref, gates_ref,
             d_ref, dlbl_ref, dgh_ref, dstate, dlb_acc):
        c = pl.program_id(0)
        d_ref[:, :, :own0] = gm_ref[...]
        d_ref[:, :, xq0:gates0] = xq_ref[...]
        d_ref[:, :, gates0:] = gates_ref[...]

        def put(b, k, val):
            d_ref[b, :, own0 + k * W:own0 + (k + 1) * W] = val.astype(d_ref.dtype)

        @pl.when(c == 0)
        def _():
            dstate[...] = jnp.zeros_like(dstate)
            dgh_ref[...] = jnp.zeros_like(dgh_ref)
            dlb_acc[...] = jnp.zeros_like(dlb_acc)

        lb = _hgrn_lower_bound(lbl_ref[...])
        ghv = gh_ref[...]
        row = lax.broadcasted_iota(jnp.int32, (C, C), 0)
        colm = lax.broadcasted_iota(jnp.int32, (C, C), 1)
        triu = colm >= row
        for b in range(B):
            hq, hg = q_ref[b].astype(F32), g_ref[b].astype(F32)
            gt = _hgrn_gates(hq, f_ref[b].astype(F32), lb)
            tril = gt["tril"]
            v = _heads(i_ref[b])
            qe, qi, ki, kl, e_last = (_heads(gt[n]) for n in ("qe", "qi", "ki", "kl", "e_last"))
            sgg = _sigmoid(hg)
            don_all = do_ref[b] * (hg * sgg)
            o, don = _heads(o_ref[b]), _heads(don_all)
            d_qe, d_qi, d_ki, d_kl, dv, n_all, dal = [], [], [], [], [], [], []
            for h in range(HG_HEADS):
                r = lax.rsqrt(jnp.mean(o[h] * o[h], axis=-1, keepdims=True) + EPS)
                n = o[h] * r
                n_all.append(n)
                dgh_ref[...] += jnp.sum(don[h] * n, axis=0, keepdims=True)
                dn = don[h] * ghv
                d_o = r * (dn - n * jnp.mean(dn * n, axis=-1, keepdims=True))
                st, dst = st_ref[b, h], dstate[b, h]
                p = jnp.where(tril, _dot_nt(qi[h], ki[h]), 0.0)
                dp = jnp.where(tril, _dot_nt(d_o, v[h]), 0.0)
                d_qe.append(_dot(d_o, st))
                d_qi.append(_dot(dp, ki[h]))
                d_ki.append(_dot_tn(dp, qi[h]))
                d_kl.append(_dot(v[h], dst))
                dv.append(_dot_tn(p, d_o) + _dot_nt(kl[h], dst))
                dstate[b, h] = dst * e_last[h] + _dot_tn(d_o, qe[h])
                dal.append(jnp.sum(dst * st, axis=0, keepdims=True) * e_last[h])
            d_qe, d_qi, d_ki, d_kl, n_all, dal = (jnp.concatenate(t, axis=1)
                                                  for t in (d_qe, d_qi, d_ki, d_kl, n_all, dal))
            put(b, 3, do_ref[b] * n_all * jnp.tile(ghv, (1, HG_HEADS)) * (sgg * (1.0 + hg * (1.0 - sgg))))
            put(b, 2, jnp.concatenate(dv, axis=1))
            d_a_last = dal + jnp.sum(d_kl * gt["kl"], axis=0, keepdims=True)
            dq = d_qe * gt["ea"] + d_qi * gt["ei"]
            dk = d_ki * gt["eki"] + d_kl * gt["ekl"]
            da = d_qe * gt["qe"] + d_qi * gt["qi"] - d_ki * gt["ki"] - d_kl * gt["kl"]
            dlogf = _dot_01(triu, da) + d_a_last
            sg, sq = gt["sg"], gt["sq"]
            dfg = dlogf / gt["fg"] - dk
            put(b, 1, dfg * (1.0 - lb) * sg * (1.0 - sg))
            dlb_acc[...] += jnp.sum(dfg * (1.0 - sg), axis=0, keepdims=True)
            put(b, 0, dq * (sq * (1.0 + hq * (1.0 - sq))))

        @pl.when(c == NC - 1)
        def _():
            dlb = dlb_acc[...]
            first = lax.broadcasted_iota(jnp.int32, (2, W), 0) == 0
            dlbl_ref[...] = jnp.where(first, dlb * lb * (1.0 - lb), -dlb * lb * (1.0 - lb))

    def col(cb):
        return pl.BlockSpec((B, C, 512), lambda c: (0, NC - 1 - c, cb))

    tile = pl.BlockSpec((B, C, W), lambda c: (0, NC - 1 - c, 0))
    proj3 = proj.reshape(B, S, proj.shape[-1])

    def rows(width):
        return pl.BlockSpec((B, C, width), lambda c: (0, NC - 1 - c, 0))

    width = proj.shape[-1]
    out, carried = _pcall(
        body, (proj3, proj3, proj3, proj3, o_saved, states, d_out.reshape(3, B, S, W), lbl, gh, d_gm, d_xq, d_gates),
        name="hgrn_bwd", grid=(NC,),
        in_specs=[col(COL_HQ), col(COL_HF), col(COL_HI), col(COL_HG), tile,
                  pl.BlockSpec((B, None, HG_HEADS, 128, 128), lambda c: (0, NC - 1 - c, 0, 0, 0)),
                  pl.BlockSpec((None, B, C, W), lambda c: (1, 0, NC - 1 - c, 0)),
                  pl.BlockSpec((2, W), lambda c: (0, 0)), pl.BlockSpec((1, HG_DIM), lambda c: (0, 0)),
                  rows(d_gm.shape[-1]), rows(d_xq.shape[-1]), rows(d_gates.shape[-1])],
        out_specs=(rows(width), pl.BlockSpec((2, W), lambda c: (0, 0)), pl.BlockSpec((1, HG_DIM), lambda c: (0, 0))),
        out_shape=(jax.ShapeDtypeStruct((B, S, width), BF16), jax.ShapeDtypeStruct((2, W), F32),
                   jax.ShapeDtypeStruct((1, HG_DIM), F32)),
        scratch_shapes=[pltpu.VMEM((B, HG_HEADS, 128, 128), F32), pltpu.VMEM((1, W), F32)],
        semantics=("arbitrary",), riders=riders)
    out = (out[0].reshape(B * S, width),) + tuple(out[1:])
    return (out, carried) if riders else out


_XA_SCALE = XA_DIM ** -0.5


def _attn_probs(qh, kh):
    s = _dot_nt(qh, kh) * _XA_SCALE
    e = jnp.exp(s - jnp.max(s, axis=-1, keepdims=True))
    return e / jnp.sum(e, axis=-1, keepdims=True)


def _attn_fwd(proj, kv, B, S):
    T = B * S
    tq = _row_tile(S)
    nq = S // tq
    W = XA_HEADS * XA_DIM

    def body(q_ref, kv_ref, o_ref):
        for h in range(XA_HEADS):
            sl = slice(h * 128, (h + 1) * 128)
            p = _attn_probs(q_ref[:, sl], kv_ref[:, sl])
            o_ref[:, sl] = _dot(p, kv_ref[:, W + h * 128:W + (h + 1) * 128]).astype(o_ref.dtype)

    return _pallas(
        body, name="attn_fwd", grid=(B, nq),
        in_specs=[pl.BlockSpec((tq, 512), lambda b, i: (b * nq + i, COL_XQ)),
                  pl.BlockSpec((MEM_LEN, 2 * W), lambda b, i: (b, 0))],
        out_specs=pl.BlockSpec((tq, W), lambda b, i: (b * nq + i, 0)),
        out_shape=jax.ShapeDtypeStruct((T, W), BF16), compiler_params=_cp("parallel", "parallel"),
    )(proj, kv)


def _attn_bwd(proj, kv, d_out, B, S):
    T = B * S
    tq = _row_tile(S)
    nq = S // tq
    W = XA_HEADS * XA_DIM

    def body(q_ref, kv_ref, do_ref, dq_ref, dkv_ref):
        @pl.when(pl.program_id(1) == 0)
        def _():
            dkv_ref[...] = jnp.zeros_like(dkv_ref)

        for h in range(XA_HEADS):
            sl = slice(h * 128, (h + 1) * 128)
            slv = slice(W + h * 128, W + (h + 1) * 128)
            qh = q_ref[:, sl]
            kh = kv_ref[:, sl]
            p = _attn_probs(qh, kh)
            dc = do_ref[:, sl]
            dp = _dot_nt(dc, kv_ref[:, slv])
            ds = p * (dp - jnp.sum(dp * p, axis=-1, keepdims=True)) * _XA_SCALE
            dq_ref[:, sl] = _dot(ds, kh).astype(dq_ref.dtype)
            dkv_ref[:, sl] += _dot_tn(ds, qh)
            dkv_ref[:, slv] += _dot_tn(p, dc)

    kvspec = pl.BlockSpec((MEM_LEN, 2 * W), lambda b, i: (b, 0))
    tile = pl.BlockSpec((tq, W), lambda b, i: (b * nq + i, 0))
    return _pallas(
        body, name="attn_bwd", grid=(B, nq),
        in_specs=[pl.BlockSpec((tq, 512), lambda b, i: (b * nq + i, COL_XQ)), kvspec,
                  pl.BlockSpec((None, tq, W), lambda b, i: (2, b * nq + i, 0))],
        out_specs=(tile, kvspec),
        out_shape=(jax.ShapeDtypeStruct((T, W), BF16), jax.ShapeDtypeStruct((B * MEM_LEN, 2 * W), F32)),
        compiler_params=_cp("parallel", "arbitrary"),
    )(proj, kv, d_out)


_MERGE_TM = 256
_GATE_W = 512


def _gate_specs(tm):
    base = COL_GATE0 // _GATE_W
    return [pl.BlockSpec((tm, _GATE_W), functools.partial(lambda i, k: (i, base + k), k=k)) for k in range(6)]


def _merge_fwd(a_out, b_out, c_out, wb, proj, riders=()):
    T = a_out.shape[0]
    tm = _row_tile(T, _MERGE_TM)
    nq, _, wd = wb.shape
    per_half = _GATE_W // wd

    def body(a_ref, b_ref, c_ref, w_ref, *rest):
        gates, (m_ref, up_ref) = rest[:6], rest[6:]
        for hf in range(2):
            cols = slice(hf * _GATE_W, (hf + 1) * _GATE_W)
            acc = None
            for n, br in enumerate((a_ref, b_ref, c_ref)):
                x = br[...]
                up = jnp.concatenate([_dot(x, w_ref[per_half * hf + j, n * BR_WIDTH:(n + 1) * BR_WIDTH, :])
                                      for j in range(per_half)], axis=1)
                up_ref[n, :, cols] = up.astype(up_ref.dtype)
                term = _sigmoid(gates[2 * n + hf][...].astype(F32)) * up
                acc = term if acc is None else acc + term
            m_ref[:, cols] = acc.astype(m_ref.dtype)

    br_spec = pl.BlockSpec((tm, BR_WIDTH), lambda i: (i, 0))
    return _carried(*_pcall(
        body, (a_out, b_out, c_out, wb, *([proj] * 6)), name="merge_fwd", grid=(T // tm,),
        in_specs=[br_spec, br_spec, br_spec,
                  pl.BlockSpec((nq, 3 * BR_WIDTH, wd), lambda i: (0, 0, 0))] + _gate_specs(tm),
        out_specs=(pl.BlockSpec((tm, D_MODEL), lambda i: (i, 0)), pl.BlockSpec((3, tm, D_MODEL), lambda i: (0, i, 0))),
        out_shape=(jax.ShapeDtypeStruct((T, D_MODEL), BF16), jax.ShapeDtypeStruct((3, T, D_MODEL), BF16)),
        semantics=("parallel",), riders=riders), riders)


def _branch_bwd_act(d_ups, wb, riders=()):
    _, T, D = d_ups.shape
    nq, _, wd = wb.shape
    tm = _row_tile(T)

    def body(d_ref, w_ref, o_ref):
        acc = None
        for q in range(nq):
            part = _dot_nt(d_ref[:, q * wd:(q + 1) * wd], w_ref[q])
            acc = part if acc is None else acc + part
        o_ref[...] = acc

    return _carried(*_pcall(
        body, (d_ups, wb), name="d_branch", grid=(3, T // tm),
        in_specs=[pl.BlockSpec((None, tm, D), lambda n, i: (n, i, 0)),
                  pl.BlockSpec((nq, BR_WIDTH, wd), lambda n, i: (0, n, 0))],
        out_specs=pl.BlockSpec((None, tm, BR_WIDTH), lambda n, i: (n, i, 0)),
        out_shape=jax.ShapeDtypeStruct((3, T, BR_WIDTH), F32), semantics=("parallel", "parallel"),
        riders=riders), riders)


def _branch_bwd_weight(name, br, d_ups, n):
    T = br.shape[0]
    D = d_ups.shape[2]
    wd = D // N_CHIPS
    tt = _row_tile(T, _TN_TOKENS)

    def body(b_ref, d_ref, o_ref):
        k = pl.program_id(0)
        for q in range(N_CHIPS):
            part = _dot_tn(b_ref[...], d_ref[:, q * wd:(q + 1) * wd])

            @pl.when(k == 0)
            def _():
                o_ref[q] = part

            @pl.when(k > 0)
            def _():
                o_ref[q] += part

    return _pallas(
        body, name=name, grid=(T // tt,),
        in_specs=[pl.BlockSpec((tt, BR_WIDTH), lambda k: (k, 0)),
                  pl.BlockSpec((None, tt, D), lambda k: (n, k, 0))],
        out_specs=pl.BlockSpec((N_CHIPS, BR_WIDTH, wd), lambda k: (0, 0, 0)),
        out_shape=jax.ShapeDtypeStruct((N_CHIPS, BR_WIDTH, wd), F32), compiler_params=_cp("arbitrary"),
    )(br, d_ups)


def _merge_bwd(d_merged, ups, proj, riders=()):
    T = d_merged.shape[0]
    tm = _row_tile(T, _MERGE_TM)

    def body(dm_ref, up_ref, *rest):
        gates, (dup_ref, dg_ref) = rest[:6], rest[6:]
        for hf in range(2):
            cols = slice(hf * _GATE_W, (hf + 1) * _GATE_W)
            dm = dm_ref[:, cols]
            for n in range(3):
                gate = _sigmoid(gates[2 * n + hf][...].astype(F32))
                dup_ref[n, :, cols] = (dm * gate).astype(dup_ref.dtype)
                dg_ref[:, n * D_MODEL + hf * _GATE_W:n * D_MODEL + (hf + 1) * _GATE_W] = (
                    dm * up_ref[n, :, cols].astype(F32) * gate * (1.0 - gate)).astype(dg_ref.dtype)

    tile = pl.BlockSpec((tm, D_MODEL), lambda i: (i, 0))
    tile3 = pl.BlockSpec((3, tm, D_MODEL), lambda i: (0, i, 0))
    return _carried(*_pcall(
        body, (d_merged, ups, *([proj] * 6)), name="merge_bwd", grid=(T // tm,),
        in_specs=[tile, tile3] + _gate_specs(tm),
        out_specs=(tile3, pl.BlockSpec((tm, 3 * D_MODEL), lambda i: (i, 0))),
        out_shape=(jax.ShapeDtypeStruct((3, T, D_MODEL), BF16), jax.ShapeDtypeStruct((T, 3 * D_MODEL), BF16)),
        semantics=("parallel",), riders=riders), riders)


_CONV_TF = D_FF // 2
_CONV_TS = 256
_HALO = 16


def _conv_fwd(ab, cw, cb, B, S):
    T = B * S
    ts = _row_tile(S, _CONV_TS)
    tf = _CONV_TF
    nb = D_FF // tf
    tps = S // ts
    hb = ts // _HALO

    def body(a_ref, p_ref, b_ref, w_ref, cb_ref, o_ref):
        start = (pl.program_id(0) % tps) == 0
        a = a_ref[...].astype(F32)
        prev = jnp.where(start, 0.0, p_ref[...].astype(F32))
        ext = jnp.concatenate([prev, a], axis=0)
        a1 = pltpu.roll(ext, 1, 0)[_HALO:, :]
        a2 = pltpu.roll(ext, 2, 0)[_HALO:, :]
        ac = cb_ref[...] + w_ref[0] * a2 + w_ref[1] * a1 + w_ref[2] * a
        o_ref[...] = (ac * _sigmoid(ac) * b_ref[...].astype(F32)).astype(o_ref.dtype)

    return _pallas(
        body, name="conv_fwd", grid=(T // ts, nb),
        in_specs=[pl.BlockSpec((ts, tf), lambda i, j: (i, j)),
                  pl.BlockSpec((_HALO, tf), lambda i, j: (jnp.maximum(i * hb - 1, 0), j)),
                  pl.BlockSpec((ts, tf), lambda i, j: (i, j + nb)),
                  pl.BlockSpec((3, 1, tf), lambda i, j: (0, 0, j)),
                  pl.BlockSpec((1, tf), lambda i, j: (0, j))],
        out_specs=pl.BlockSpec((ts, tf), lambda i, j: (i, j)),
        out_shape=jax.ShapeDtypeStruct((T, D_FF), BF16), compiler_params=_cp("parallel", "parallel"),
    )(ab, ab, ab, cw, cb)


def _conv_bwd(ab, d_ff, cw, cb, B, S, riders=()):
    T = B * S
    ts = _row_tile(S, _CONV_TS)
    tf = _CONV_TF
    nb = D_FF // tf
    tps = S // ts
    hb = ts // _HALO
    last_h = T // _HALO - 1
    n_ext = ts + _HALO

    def body(a_ref, ap_ref, an_ref, b_ref, bn_ref, d_ref, dn_ref, w_ref, cb_ref, dab_ref, dw_ref, dcb_ref):
        i = pl.program_id(1)

        @pl.when(i == 0)
        def _():
            dw_ref[...] = jnp.zeros_like(dw_ref)
            dcb_ref[...] = jnp.zeros_like(dcb_ref)

        start = (i % tps) == 0
        end = (i % tps) == tps - 1
        a = a_ref[...].astype(F32)
        ext = jnp.concatenate([jnp.where(start, 0.0, ap_ref[...].astype(F32)), a, an_ref[...].astype(F32)], axis=0)
        r1 = pltpu.roll(ext, 1, 0)[_HALO:, :]
        r2 = pltpu.roll(ext, 2, 0)[_HALO:, :]
        ac = cb_ref[...] + w_ref[0] * r2 + w_ref[1] * r1 + w_ref[2] * ext[_HALO:, :]
        sg = _sigmoid(ac)
        d_e = jnp.concatenate([d_ref[...].astype(F32), jnp.where(end, 0.0, dn_ref[...].astype(F32))], axis=0)
        b_e = jnp.concatenate([b_ref[...].astype(F32), bn_ref[...].astype(F32)], axis=0)
        dab_ref[1] = (d_e[:ts, :] * (ac * sg)[:ts, :]).astype(dab_ref.dtype)
        dac = d_e * b_e * sg * (1.0 + ac * (1.0 - sg))
        u1 = pltpu.roll(dac, n_ext - 1, 0)[:ts, :]
        u2 = pltpu.roll(dac, n_ext - 2, 0)[:ts, :]
        dac0 = dac[:ts, :]
        dab_ref[0] = (w_ref[2] * dac0 + w_ref[1] * u1 + w_ref[0] * u2).astype(dab_ref.dtype)
        dcb_ref[...] += jnp.sum(dac0, axis=0, keepdims=True)
        dw_ref[2] += jnp.sum(dac0 * a, axis=0, keepdims=True)
        dw_ref[1] += jnp.sum(dac0 * r1[:ts, :], axis=0, keepdims=True)
        dw_ref[0] += jnp.sum(dac0 * r2[:ts, :], axis=0, keepdims=True)

    def cur(off):
        return pl.BlockSpec((ts, tf), lambda j, i: (i, j + off))

    def nxt(off):
        return pl.BlockSpec((_HALO, tf), lambda j, i: (jnp.minimum((i + 1) * hb, last_h), j + off))

    return _carried(*_pcall(
        body, (ab, ab, ab, ab, ab, d_ff, d_ff, cw, cb), name="conv_bwd", grid=(nb, T // ts),
        in_specs=[cur(0), pl.BlockSpec((_HALO, tf), lambda j, i: (jnp.maximum(i * hb - 1, 0), j)), nxt(0),
                  cur(nb), nxt(nb), cur(0), nxt(0),
                  pl.BlockSpec((3, 1, tf), lambda j, i: (0, 0, j)), pl.BlockSpec((1, tf), lambda j, i: (0, j))],
        out_specs=(pl.BlockSpec((2, ts, tf), lambda j, i: (0, i, j)), pl.BlockSpec((3, 1, tf), lambda j, i: (0, 0, j)),
                   pl.BlockSpec((1, tf), lambda j, i: (0, j))),
        out_shape=(jax.ShapeDtypeStruct((2, T, D_FF), BF16),
                   jax.ShapeDtypeStruct((3, 1, D_FF), F32), jax.ShapeDtypeStruct((1, D_FF), F32)),
        semantics=("parallel", "arbitrary"), riders=riders), riders)


def _local_step(x, mem, tgt, p, comm, B, S):
    g = {}
    h = _rms_fwd("norm1", x, p["norm1_g"])
    proj = comm.carry("in_proj", lambda r: _mm_cs("in_proj", h, comm.w("w_in"), BF16, riders=r))
    a_out = _gmlp_fwd(proj, p["ln_v_g"], p["ln_v_b"], p["w_spatial"], p["b_spatial"])
    o_h, b_out, states = comm.carry(
        "hgrn_fwd", lambda r: _hgrn_fwd(proj, p["lb_logits"], p["hgrn_norm_g"], B, S, riders=r))
    memn = _rms_fwd("mem_norm", mem, p["mem_norm_g"])
    kv = _mm_rs("mem_kv", memn, comm.w("w_mem_kv"), F32)
    c_out = _attn_fwd(proj, kv, B, S)
    merged, ups = comm.carry(
        "merge_fwd", lambda r: _merge_fwd(a_out, b_out, c_out, comm.w("w_branch"), proj, riders=r))
    x1, h2 = _proj_res_norm("out_proj_norm2", merged, comm.w("w_out"), x, p["norm2_g"])
    ab = comm.carry("up_proj", lambda r: _mm_cs("up_proj", h2, comm.w("w_up"), BF16, riders=r))
    conv_w = comm.w("conv_w")
    ff = _conv_fwd(ab, conv_w, p["conv_b"], B, S)
    dx2, g["final_g"], loss = _proj_res_loss("down_proj_loss", ff, comm.w("w_down"), x1, tgt, p["final_g"])

    comm.grad("w_down", _mm_tn_rs("g_w_down", ff, dx2, to=D_FF // 2))
    d_ff = comm.carry("d_ff", lambda r: _mm_nt_rs("d_ff", dx2, comm.w("w_down"), BF16, riders=r))
    d_ab, g["conv_w"], g["conv_b"] = comm.carry(
        "conv_bwd", lambda r: _conv_bwd(ab, d_ff, conv_w, p["conv_b"], B, S, riders=r))
    comm.grad("w_up", _mm_tn_cs("g_w_up", h2, d_ab, N_CHIPS, to=512, stacked=True))
    d_x1, g["norm2_g"] = comm.carry("d_h2", lambda r: _mm_nt_cs(
        "d_h2_norm2_bwd", d_ab, comm.w("w_up"), F32, riders=r, stacked=True, norm_bwd=(x1, p["norm2_g"], dx2)))
    comm.grad("w_out", _mm_tn_rs("g_w_out", merged, d_x1, to=512))
    d_merged = _mm_nt_rs("d_merged", d_x1, comm.w("w_out"), F32)
    d_ups, d_gates = comm.carry("merge_bwd", lambda r: _merge_bwd(d_merged, ups, proj, riders=r))

    d_br = comm.carry("d_branch", lambda r: _branch_bwd_act(d_ups, comm.w("w_branch"), riders=r))
    comm.grad("w_branch", jnp.concatenate(
        [_branch_bwd_weight("g_w_branch%d" % n, br, d_ups, n) for n, br in enumerate((a_out, b_out, c_out))],
        axis=1))

    d_gm, g["w_spatial"], g["b_spatial"], g["ln_v_g"], g["ln_v_b"] = comm.carry(
        "gmlp_bwd", lambda r: _gmlp_bwd(proj, d_br, p["ln_v_g"], p["ln_v_b"], p["w_spatial"], p["b_spatial"],
                                        riders=r))
    d_xq, d_kv = _attn_bwd(proj, kv, d_br, B, S)
    comm.grad("w_mem_kv", _mm_tn_rs("g_w_mem_kv", memn, d_kv, to=512))
    d_memn = _mm_nt_rs("d_memn", d_kv, comm.w("w_mem_kv"), F32)
    _, g["mem_norm_g"] = _rms_bwd("mem_norm_bwd", mem, p["mem_norm_g"], d_memn, None)
    d_proj, g["lb_logits"], g["hgrn_norm_g"] = comm.carry(
        "hgrn_bwd", lambda r: _hgrn_bwd(proj, o_h, states, d_br, p["lb_logits"], p["hgrn_norm_g"],
                                        (d_gm, d_xq, d_gates), B, S, riders=r))
    comm.small_grads([g[n].reshape(_SMALL_SHAPE[n]) for n in _SMALL_EARLY] + [loss])
    comm.grad("w_in", comm.carry("g_w_in", lambda r: _mm_tn_cs("g_w_in", h, d_proj, N_CHIPS, to=512, riders=r)))
    grad_x, g["norm1_g"] = comm.carry("d_h", lambda r: _mm_nt_cs(
        "d_h_norm1_bwd", d_proj, comm.w("w_in"), F32, riders=r, norm_bwd=(x, p["norm1_g"], d_x1)))
    return loss, grad_x, g


HBM_SPEC = pl.BlockSpec(memory_space=pltpu.HBM)


def _place():
    x, y, c = lax.axis_index("x"), lax.axis_index("y"), lax.axis_index("c")
    other_chips = [(1 - x, y), (x, 1 - y), (1 - x, 1 - y)]
    return x, y, c, other_chips


def _remote(src, dst, send_sem, recv_sem, dev):
    return pltpu.make_async_remote_copy(src_ref=src, dst_ref=dst, send_sem=send_sem, recv_sem=recv_sem,
                                        device_id=dev, device_id_type=MESH_ID)


class _Exchange:
    def __init__(self, operands, out_shape, aliases, scratch, start, finish):
        self.operands, self.out_shape, self.aliases, self.scratch = operands, out_shape, aliases, scratch
        self.start, self.finish = start, finish


def _run_exchanges(name, exs):
    n_in = [len(ex.operands) for ex in exs]
    n_out = [len(ex.out_shape) for ex in exs]
    n_scr = [len(ex.scratch) for ex in exs]

    def body(*refs):
        ins, outs, scr = refs[:sum(n_in)], refs[sum(n_in):sum(n_in) + sum(n_out)], refs[sum(n_in) + sum(n_out):]
        parts, oi, oo, os_ = [], 0, 0, 0
        for k in range(len(exs)):
            parts.append((ins[oi:oi + n_in[k]], outs[oo:oo + n_out[k]], scr[os_:os_ + n_scr[k]]))
            oi, oo, os_ = oi + n_in[k], oo + n_out[k], os_ + n_scr[k]
        for ex, part in zip(exs, parts):
            ex.start(*part)
        for ex, part in zip(exs, parts):
            ex.finish(*part)

    aliases, ops, shapes, scratch, oi, oo = {}, [], [], [], 0, 0
    for k, ex in enumerate(exs):
        aliases.update({oi + a: oo + b for a, b in ex.aliases.items()})
        oi, oo = oi + n_in[k], oo + n_out[k]
        ops += list(ex.operands)
        shapes += [pltpu.HBM(s.shape, s.dtype) for s in ex.out_shape]
        scratch += list(ex.scratch)
    res = _pallas(
        body, name=name, in_specs=[HBM_SPEC] * len(ops), out_specs=(HBM_SPEC,) * len(shapes), out_shape=tuple(shapes),
        input_output_aliases=aliases, scratch_shapes=scratch,
    )(*ops)
    out, oo = [], 0
    for k in range(len(exs)):
        out.append(list(res[oo:oo + n_out[k]]))
        oo += n_out[k]
    return out


def _ex_all_gather(slabs, halved, part=(0, 1)):
    n = len(slabs)

    def rows(a, cc):
        if not halved[a]:
            return slice(None)
        pr = slabs[a].shape[1] // part[1]
        return pl.ds(part[0] * pr + cc * (pr // 2), pr // 2)

    def ici(bufs, scr, a, j, chip, c, mine):
        px, py = chip
        x, y, _, _ = _place()
        qs = 2 * x + y if mine else 2 * px + py
        piece = bufs[a].at[qs, rows(a, c)]
        return _remote(piece, piece, scr[0].at[3 * a + j], scr[1].at[3 * a + j], (px, py, c))

    def d2d(bufs, scr, a, j, chip, cc):
        px, py = chip
        x, y, c, _ = _place()
        piece = bufs[a].at[2 * px + py, rows(a, cc)]
        return _remote(piece, piece, scr[2].at[3 * a + j], scr[3].at[3 * a + j], (x, y, 1 - c))

    def start(ins, outs, scr):
        _, _, c, chips = _place()
        for j, chip in enumerate(chips):
            for a in range(n):
                ici(outs, scr, a, j, chip, c, True).start()

    def finish(ins, outs, scr):
        _, _, c, chips = _place()
        for j, chip in enumerate(chips):
            for a in range(n):
                ici(outs, scr, a, j, chip, c, False).wait_recv()
                if halved[a]:
                    d2d(outs, scr, a, j, chip, c).start()
        for j, chip in enumerate(chips):
            for a in range(n):
                if halved[a]:
                    d2d(outs, scr, a, j, chip, 1 - c).wait_recv()
        for j, chip in enumerate(chips):
            for a in range(n):
                ici(outs, scr, a, j, chip, c, True).wait_send()
                if halved[a]:
                    d2d(outs, scr, a, j, chip, c).wait_send()

    return _Exchange(list(slabs), [jax.ShapeDtypeStruct(s.shape, s.dtype) for s in slabs],
                     {a: a for a in range(n)}, [pltpu.SemaphoreType.DMA((3 * n,))] * 4, start, finish)


def _ex_to_sibling(grads):
    n = len(grads)

    def copy(ins, outs, scr, a):
        x, y, c, _ = _place()
        hr = grads[a].shape[1] // 2
        return _remote(ins[a].at[:, pl.ds((1 - c) * hr, hr), :], outs[a], scr[0].at[a], scr[1].at[a], (x, y, 1 - c))

    def start(ins, outs, scr):
        for a in range(n):
            copy(ins, outs, scr, a).start()

    def finish(ins, outs, scr):
        for a in range(n):
            copy(ins, outs, scr, a).wait()

    out_shape = [jax.ShapeDtypeStruct((g.shape[0], g.shape[1] // 2, g.shape[2]), g.dtype) for g in grads]
    return _Exchange(list(grads), out_shape, {}, [pltpu.SemaphoreType.DMA((n,))] * 2, start, finish)


def _ex_to_owner(parts, part=(0, 1), landing=None):
    n = len(parts)

    def copy(ins, outs, scr, a, j, chip):
        _, _, c, _ = _place()
        px, py = chip
        pr = parts[a].shape[1] // part[1]
        rows = pl.ds(part[0] * pr, pr)
        return _remote(ins[a].at[2 * px + py, rows], outs[a].at[j, rows], scr[0].at[3 * a + j],
                       scr[1].at[3 * a + j], (px, py, c))

    def start(ins, outs, scr):
        for j, chip in enumerate(_place()[3]):
            for a in range(n):
                copy(ins, outs, scr, a, j, chip).start()

    def finish(ins, outs, scr):
        for j, chip in enumerate(_place()[3]):
            for a in range(n):
                copy(ins, outs, scr, a, j, chip).wait()

    out_shape = [jax.ShapeDtypeStruct((3,) + p.shape[1:], p.dtype) for p in parts]
    operands, aliases = list(parts), {}
    if landing is not None:
        operands, aliases = operands + list(landing), {n + a: a for a in range(n)}
    return _Exchange(operands, out_shape, aliases, [pltpu.SemaphoreType.DMA((3 * n,))] * 2, start, finish)


def _ex_share_halves(bufs):
    n = len(bufs)

    def copy(outs, scr, a, cc):
        x, y, c, _ = _place()
        hr = bufs[a].shape[0] // 2
        piece = outs[a].at[pl.ds(cc * hr, hr), :]
        return _remote(piece, piece, scr[0].at[a], scr[1].at[a], (x, y, 1 - c))

    def start(ins, outs, scr):
        c = _place()[2]
        for a in range(n):
            copy(outs, scr, a, c).start()

    def finish(ins, outs, scr):
        c = _place()[2]
        for a in range(n):
            copy(outs, scr, a, c).wait_send()
            copy(outs, scr, a, 1 - c).wait_recv()

    return _Exchange(list(bufs), [jax.ShapeDtypeStruct(b.shape, b.dtype) for b in bufs], {a: a for a in range(n)},
                     [pltpu.SemaphoreType.DMA((n,))] * 2, start, finish)


def _ex_gather_small(arrs):
    n = len(arrs)

    def peer_of(m):
        x, y, c, _ = _place()
        return (1 - x if m & 4 else x, 1 - y if m & 2 else y, 1 - c if m & 1 else c)

    def start(ins, outs, scr):
        x, y, c, _ = _place()
        for m in range(1, N_DEV):
            for a in range(n):
                k = (N_DEV - 1) * a + m - 1
                _remote(ins[a], outs[a].at[4 * x + 2 * y + c], scr[0].at[k], scr[1].at[k], peer_of(m)).start()

    def finish(ins, outs, scr):
        for m in range(1, N_DEV):
            px, py, pc = peer_of(m)
            for a in range(n):
                k = (N_DEV - 1) * a + m - 1
                slot = outs[a].at[4 * px + 2 * py + pc]
                cp = _remote(ins[a], slot, scr[0].at[k], scr[1].at[k], (px, py, pc))
                cp.wait_send()
                cp.wait_recv()

    slots = [jnp.zeros((N_DEV,) + a.shape, a.dtype) for a in arrs]
    out_shape = [jax.ShapeDtypeStruct(s.shape, s.dtype) for s in slots]
    return _Exchange(list(arrs) + slots, out_shape, {n + a: a for a in range(n)},
                     [pltpu.SemaphoreType.DMA(((N_DEV - 1) * n,))] * 2, start, finish)


def _div_tile(n, want):
    best = None
    for t in range(8, min(n, want) + 1, 8):
        if n % t == 0:
            best = t
    assert best is not None, n
    return best


def _cast_into_slab(name, w, place, dtype):
    r, cc = w.shape
    tr = r if r * cc <= 128 * 1024 else _div_tile(r, 256)

    def body(s_ref, w_ref, o_ref):
        o_ref[...] = w_ref[...].astype(o_ref.dtype)

    return _pallas(
        body, name=name,
        grid_spec=pltpu.PrefetchScalarGridSpec(
            num_scalar_prefetch=1, grid=(r // tr,),
            in_specs=[pl.BlockSpec((tr, cc), lambda i, s: (i, 0))],
            out_specs=pl.BlockSpec((None, tr, cc), lambda i, s: (s[0], i, 0))),
        out_shape=jax.ShapeDtypeStruct((N_CHIPS, r, cc), dtype), compiler_params=_cp("parallel"),
    )(place, w)


def _add_half(name, g, rcv, place):
    nq, r, cc = g.shape
    hr = r // 2

    def body(s_ref, g_ref, r_ref, o_ref):
        o_ref[...] = (g_ref[...] + r_ref[...]).astype(o_ref.dtype)

    spec = pl.BlockSpec((None, hr, cc), lambda i, s: (i, 0, 0))
    return _pallas(
        body, name=name,
        grid_spec=pltpu.PrefetchScalarGridSpec(
            num_scalar_prefetch=1, grid=(nq,),
            in_specs=[pl.BlockSpec((None, hr, cc), lambda i, s: (i, s[1], 0)), spec], out_specs=spec),
        out_shape=jax.ShapeDtypeStruct((nq, hr, cc), BF16), compiler_params=_cp("parallel"),
    )(place, g, rcv)


def _sum_owner(name, part, rcv, place):
    _, hr, cc = part.shape
    tr = _div_tile(hr, 128)
    nb = hr // tr

    def body(s_ref, p_ref, r_ref, o_ref):
        o_ref[...] = ((p_ref[...].astype(F32) + r_ref[0].astype(F32)) + r_ref[1].astype(F32)) + r_ref[2].astype(F32)

    return _pallas(
        body, name=name,
        grid_spec=pltpu.PrefetchScalarGridSpec(
            num_scalar_prefetch=1, grid=(nb,),
            in_specs=[pl.BlockSpec((None, tr, cc), lambda i, s: (s[0], i, 0)),
                      pl.BlockSpec((3, tr, cc), lambda i, s: (0, i, 0))],
            out_specs=pl.BlockSpec((tr, cc), lambda i, s: (s[1] * nb + i, 0))),
        out_shape=jax.ShapeDtypeStruct((2 * hr, cc), F32), compiler_params=_cp("parallel"),
    )(place, part, rcv)


def _sum_small(gathered, local, place):
    n = len(gathered)

    def body(s_ref, *refs):
        g_refs, l_refs, o_refs = refs[:n], refs[n:2 * n], refs[2 * n:]
        me = s_ref[2]
        for g_ref, l_ref, o_ref in zip(g_refs, l_refs, o_refs):
            acc = None
            for d in range(N_DEV):
                term = jnp.where(me == d, l_ref[...], g_ref[d])
                acc = term if acc is None else acc + term
            o_ref[...] = acc

    def whole(shape):
        return pl.BlockSpec(shape, lambda i, s, nd=len(shape): (0,) * nd)

    return _pallas(
        body, name="sum_small",
        grid_spec=pltpu.PrefetchScalarGridSpec(
            num_scalar_prefetch=1, grid=(1,),
            in_specs=[whole(g.shape) for g in gathered] + [whole(a.shape) for a in local],
            out_specs=tuple(whole(a.shape) for a in local)),
        out_shape=tuple(jax.ShapeDtypeStruct(a.shape, a.dtype) for a in local), compiler_params=_cp("arbitrary"),
    )(place, *gathered, *local)


def _adamw(name, w, g, m, v):
    r, cc = w.shape
    tr = r if r * cc <= 128 * 1024 else _div_tile(r, 256)

    def body(w_ref, g_ref, m_ref, v_ref, d_ref, mo_ref, vo_ref):
        gv = g_ref[...]
        mn = ADAM_B1 * m_ref[...] + (1.0 - ADAM_B1) * gv
        vn = ADAM_B2 * v_ref[...] + (1.0 - ADAM_B2) * (gv * gv)
        m_hat = mn / (1.0 - ADAM_B1 ** ADAM_STEP)
        v_hat = vn / (1.0 - ADAM_B2 ** ADAM_STEP)
        d_ref[...] = -ADAM_LR * (m_hat / (jnp.sqrt(v_hat) + ADAM_EPS) + ADAM_WD * w_ref[...])
        mo_ref[...] = mn
        vo_ref[...] = vn

    spec = pl.BlockSpec((tr, cc), lambda i: (i, 0))
    sd = jax.ShapeDtypeStruct((r, cc), F32)
    return _pallas(
        body, name=name, grid=(r // tr,), in_specs=[spec] * 4, out_specs=(spec,) * 3, out_shape=(sd,) * 3,
        compiler_params=_cp("parallel"),
    )(w, g, m, v)


_BIG = ("w_in", "w_up", "w_branch", "w_mem_kv", "w_out", "w_down")
_BIG_SHARD_SHAPE = {"w_in": (1024, 1664), "w_up": (1024, 1408), "w_branch": (1536, 256),
                    "w_mem_kv": (256, 1024), "w_out": (256, 1024), "w_down": (704, 1024)}
_SMALL_SHAPE = {"norm1_g": (1, D_MODEL), "ln_v_g": (1, GM_WIDTH), "ln_v_b": (1, GM_WIDTH),
                "w_spatial": (GM_GROUPS * GM_CHUNK, GM_CHUNK), "b_spatial": (GM_GROUPS, GM_CHUNK),
                "lb_logits": (2, HG_HEADS * HG_DIM), "hgrn_norm_g": (1, HG_DIM), "mem_norm_g": (1, D_MODEL),
                "norm2_g": (1, D_MODEL), "conv_w": (3, D_FF), "conv_b": (1, D_FF), "final_g": (1, D_MODEL)}
_SMALL_EARLY = tuple(n for n in _SMALL_SHAPE if n != "norm1_g")
_PARAM_ORDER = ("norm1_g", "w_in", "ln_v_g", "ln_v_b", "w_spatial", "b_spatial", "lb_logits", "hgrn_norm_g",
                "mem_norm_g", "w_mem_kv", "w_branch", "w_out", "norm2_g", "w_up", "conv_w", "conv_b", "w_down",
                "final_g")


def _adamw_small(ws, gs, ms, vs):
    n = len(ws)

    def body(*refs):
        w_refs, g_refs, m_refs, v_refs = refs[:n], refs[n:2 * n], refs[2 * n:3 * n], refs[3 * n:4 * n]
        d_refs, mo_refs, vo_refs = refs[4 * n:5 * n], refs[5 * n:6 * n], refs[6 * n:]
        for k in range(n):
            gv = g_refs[k][...]
            mn = ADAM_B1 * m_refs[k][...] + (1.0 - ADAM_B1) * gv
            vn = ADAM_B2 * v_refs[k][...] + (1.0 - ADAM_B2) * (gv * gv)
            m_hat = mn / (1.0 - ADAM_B1 ** ADAM_STEP)
            v_hat = vn / (1.0 - ADAM_B2 ** ADAM_STEP)
            d_refs[k][...] = -ADAM_LR * (m_hat / (jnp.sqrt(v_hat) + ADAM_EPS) + ADAM_WD * w_refs[k][...])
            mo_refs[k][...] = mn
            vo_refs[k][...] = vn

    specs = [pl.BlockSpec(a.shape, lambda i: (0, 0)) for a in ws]
    shapes = tuple(jax.ShapeDtypeStruct(a.shape, F32) for a in ws)
    res = _pallas(
        body, name="adamw_small", grid=(1,), in_specs=specs * 4, out_specs=tuple(specs * 3), out_shape=shapes * 3,
        compiler_params=_cp("arbitrary"),
    )(*ws, *gs, *ms, *vs)
    return res[:n], res[n:2 * n], res[2 * n:]


class _Comm:
    _ROW_SHARDED = ("w_mem_kv", "w_out", "w_down")

    def __init__(self, slabs, place):
        self.slabs, self.place = slabs, place
        self.full, self.raw, self.parts, self.landing, self.bufs, self.done = {}, {}, {}, {}, {}, {}
        ex, deliver = self._gather(["w_in"])
        deliver(_run_exchanges("all_gather_w_in", [ex])[0])

    def w(self, name):
        a = self.full[name]
        if name in self._ROW_SHARDED:
            return a.reshape(-1, a.shape[-1])
        if name == "conv_w":
            return jnp.transpose(a, (1, 0, 2)).reshape(3, 1, D_FF)
        return a

    def grad(self, name, arr):
        self.raw[name] = arr.reshape((N_CHIPS, -1, arr.shape[-1]))
        if name == "w_in":
            ex, deliver = self._to_sibling(["w_in"])
            deliver(_run_exchanges("rs_sibling_w_in", [ex])[0])

    def small_grads(self, arrays):
        self.small_local = list(arrays)

    def carry(self, tag, call):
        plan = self._plan(tag)
        if not plan:
            return call(())
        out, carried = call([ex for ex, _ in plan])
        for (_, deliver), res in zip(plan, carried):
            deliver(res)
        return out

    def finish(self, last_small):
        ex, deliver = self._share(["w_out", "w_branch", "w_mem_kv", "w_in"])
        shared, small = _run_exchanges("share_and_gather_last", [ex, _ex_gather_small(last_small)])
        deliver(shared)
        return self.done, self.small_local + list(last_small), self.small_everyone + small

    def _plan(self, tag):
        if tag == "in_proj":
            return [self._gather(["w_branch", "w_out", "w_mem_kv", "w_down", "conv_w"])]
        if tag == "hgrn_fwd":
            return [self._gather(["w_up"])]
        if tag == "d_h2":
            return [self._to_sibling(["w_down", "w_up"])]
        if tag == "hgrn_bwd":
            return [self._to_owner(["w_down", "w_up"]), self._to_sibling(["w_out", "w_branch", "w_mem_kv"])]
        if tag == "g_w_in":
            def keep(res):
                self.small_everyone = res

            return [self._to_owner(["w_out", "w_branch", "w_mem_kv"]), self._share(["w_down", "w_up"]),
                    (_ex_gather_small(self.small_local), keep)]
        if tag == "d_h":
            return [self._to_owner(["w_in"])]
        return []

    def _gather(self, names, part=(0, 1)):
        def deliver(res):
            self.slabs.update(zip(names, res))
            self.full.update(zip(names, res))

        return _ex_all_gather([self.slabs[n] for n in names], [n != "conv_w" for n in names], part), deliver

    def _to_sibling(self, names):
        def deliver(res):
            for n, r in zip(names, res):
                self.parts[n] = _add_half("rs_add_" + n, self.raw[n], r, self.place)

        return _ex_to_sibling([self.raw[n] for n in names]), deliver

    def _to_owner(self, names, part=(0, 1)):
        def deliver(res):
            for n, r in zip(names, res):
                if part[0] + 1 < part[1]:
                    self.landing[n] = r
                else:
                    self.bufs[n] = _sum_owner("rs_sum_" + n, self.parts[n], r, self.place)

        landing = [self.landing[n] for n in names] if part[0] else None
        return _ex_to_owner([self.parts[n] for n in names], part, landing), deliver

    def _share(self, names):
        return _ex_share_halves([self.bufs[n] for n in names]), lambda res: self.done.update(zip(names, res))


def kernel(x, mem, norm1_g, w_in, ln_v_g, ln_v_b, w_spatial, b_spatial, lb_logits, hgrn_norm_g, mem_norm_g, w_mem_kv, w_branch, w_out, norm2_g, w_up, conv_w, conv_b, w_down, final_g, loss_target, m_norm1_g, m_w_in, m_ln_v_g, m_ln_v_b, m_w_spatial, m_b_spatial, m_lb_logits, m_hgrn_norm_g, m_mem_norm_g, m_w_mem_kv, m_w_branch, m_w_out, m_norm2_g, m_w_up, m_conv_w, m_conv_b, m_w_down, m_final_g, v_norm1_g, v_w_in, v_ln_v_g, v_ln_v_b, v_w_spatial, v_b_spatial, v_lb_logits, v_hgrn_norm_g, v_mem_norm_g, v_w_mem_kv, v_w_branch, v_w_out, v_norm2_g, v_w_up, v_conv_w, v_conv_b, v_w_down, v_final_g):
    w = dict(norm1_g=norm1_g, w_in=w_in, ln_v_g=ln_v_g, ln_v_b=ln_v_b, w_spatial=w_spatial, b_spatial=b_spatial,
             lb_logits=lb_logits, hgrn_norm_g=hgrn_norm_g, mem_norm_g=mem_norm_g, w_mem_kv=w_mem_kv,
             w_branch=w_branch, w_out=w_out, norm2_g=norm2_g, w_up=w_up, conv_w=conv_w, conv_b=conv_b,
             w_down=w_down, final_g=final_g)
    mom = dict(norm1_g=m_norm1_g, w_in=m_w_in, ln_v_g=m_ln_v_g, ln_v_b=m_ln_v_b, w_spatial=m_w_spatial,
               b_spatial=m_b_spatial, lb_logits=m_lb_logits, hgrn_norm_g=m_hgrn_norm_g, mem_norm_g=m_mem_norm_g,
               w_mem_kv=m_w_mem_kv, w_branch=m_w_branch, w_out=m_w_out, norm2_g=m_norm2_g, w_up=m_w_up,
               conv_w=m_conv_w, conv_b=m_conv_b, w_down=m_w_down, final_g=m_final_g)
    var = dict(norm1_g=v_norm1_g, w_in=v_w_in, ln_v_g=v_ln_v_g, ln_v_b=v_ln_v_b, w_spatial=v_w_spatial,
               b_spatial=v_b_spatial, lb_logits=v_lb_logits, hgrn_norm_g=v_hgrn_norm_g, mem_norm_g=v_mem_norm_g,
               w_mem_kv=v_w_mem_kv, w_branch=v_w_branch, w_out=v_w_out, norm2_g=v_norm2_g, w_up=v_w_up,
               conv_w=v_conv_w, conv_b=v_conv_b, w_down=v_w_down, final_g=v_final_g)
    B, S, D = x.shape
    T = B * S
    ci = lax.axis_index("c")
    q = 2 * lax.axis_index("x") + lax.axis_index("y")
    place = jnp.stack([q, ci, 2 * q + ci]).astype(jnp.int32)

    slabs = {n: _cast_into_slab("slab_" + n, w[n].reshape(_BIG_SHARD_SHAPE[n]), place, BF16) for n in _BIG}
    slabs["conv_w"] = _cast_into_slab("slab_conv_w", conv_w[0], place, F32)
    comm = _Comm(slabs, place)
    p = dict(
        norm1_g=norm1_g, ln_v_g=ln_v_g, ln_v_b=ln_v_b, w_spatial=w_spatial[0],
        b_spatial=b_spatial.reshape(GM_GROUPS, GM_CHUNK, 1), lb_logits=lb_logits, hgrn_norm_g=hgrn_norm_g,
        mem_norm_g=mem_norm_g, norm2_g=norm2_g, conv_b=conv_b, final_g=final_g.reshape(1, D))

    loss, grad_x, g = _local_step(x.reshape(T, D), mem.reshape(B * MEM_LEN, D), loss_target.reshape(T, D), p, comm,
                                  B, S)

    shard_grads, local_small, everyone = comm.finish([g["norm1_g"]])
    summed = _sum_small(everyone, local_small, place)
    small_names = list(_SMALL_EARLY) + ["norm1_g"]
    total = dict(zip(_SMALL_EARLY, summed))
    loss_total, total["norm1_g"] = summed[len(_SMALL_EARLY)][0, 0], summed[-1]

    grads, delta, new_m, new_v = {}, {}, {}, {}
    for n in _BIG:
        shp = _BIG_SHARD_SHAPE[n]
        grads[n] = shard_grads[n]
        delta[n], new_m[n], new_v[n] = _adamw("adamw_" + n, w[n].reshape(shp), shard_grads[n],
                                              mom[n].reshape(shp), var[n].reshape(shp))
    cw_shard = D_FF // N_CHIPS
    total["conv_w"] = lax.dynamic_slice(total["conv_w"], (0, q * cw_shard), (3, cw_shard))

    def flat2d(d, n):
        return d[n].reshape(total[n].shape)

    upd = _adamw_small([flat2d(w, n) for n in small_names], [total[n] for n in small_names],
                       [flat2d(mom, n) for n in small_names], [flat2d(var, n) for n in small_names])
    for k, n in enumerate(small_names):
        grads[n], delta[n], new_m[n], new_v[n] = total[n], upd[0][k], upd[1][k], upd[2][k]

    def shaped(d):
        return [d[n].reshape(w[n].shape) for n in _PARAM_ORDER]

    return (loss_total, grad_x.reshape(B, S, D), *shaped(grads), *shaped(delta), *shaped(new_m), *shaped(new_v))
```

```python
import functools
import math

import jax
import jax.numpy as jnp
from jax import lax
from jax.experimental import pallas as pl
from jax.experimental.pallas import tpu as pltpu

F32 = jnp.float32
BF16 = jnp.bfloat16
EPS = 1e-6

D_MODEL = 1024
MEM_LEN = 256
GM_WIDTH = 512
GM_CHUNK = 128
GM_GROUPS = 4
HG_HEADS = 4
HG_DIM = 128
HG_CHUNK = 64
XA_HEADS = 4
XA_DIM = 128
BR_WIDTH = 512
D_FF = 2816
IN_WIDTH = 6656
N_CHIPS = 4
N_DEV = 8

ADAM_LR = 0.001
ADAM_B1 = 0.9
ADAM_B2 = 0.999
ADAM_EPS = 1e-08
ADAM_WD = 0.01
ADAM_STEP = 10

COL_ZU, COL_ZV, COL_HQ, COL_HF, COL_HI, COL_HG, COL_XQ = 0, 1, 2, 3, 4, 5, 6
COL_GATE0 = 3584

VMEM_LIMIT_BYTES = 48 * 1024 * 1024
MESH_ID = pl.DeviceIdType.MESH


def _cp(*sem):
    return pltpu.CompilerParams(dimension_semantics=sem, vmem_limit_bytes=VMEM_LIMIT_BYTES)


def _pallas(body, *, out_shape, **kw):
    def pin(s):
        return pltpu.HBM(s.shape, s.dtype) if isinstance(s, jax.ShapeDtypeStruct) else s

    out_shape = tuple(pin(s) for s in out_shape) if isinstance(out_shape, (tuple, list)) else pin(out_shape)
    call = pl.pallas_call(body, out_shape=out_shape, **kw)

    def run(*operands):
        return call(*[pltpu.with_memory_space_constraint(o, pltpu.HBM) if jnp.issubdtype(o.dtype, jnp.floating)
                      else o for o in operands])

    return run


def _dot(a, b):
    return lax.dot_general(a.astype(BF16), b.astype(BF16), (((1,), (0,)), ((), ())), preferred_element_type=F32)


def _dot_nt(a, b):
    return lax.dot_general(a.astype(BF16), b.astype(BF16), (((1,), (1,)), ((), ())), preferred_element_type=F32)


def _dot_tn(a, b):
    return lax.dot_general(a.astype(BF16), b.astype(BF16), (((0,), (0,)), ((), ())), preferred_element_type=F32)


def _dot_01(mask01, x):
    hi = x.astype(BF16)
    r1 = x - hi.astype(F32)
    mid = r1.astype(BF16)
    lo = (r1 - mid.astype(F32)).astype(BF16)
    m = mask01.astype(BF16)
    dn = (((1,), (0,)), ((), ()))
    return (lax.dot_general(m, hi, dn, preferred_element_type=F32)
            + lax.dot_general(m, mid, dn, preferred_element_type=F32)
            + lax.dot_general(m, lo, dn, preferred_element_type=F32))


def _sigmoid(z):
    return 1.0 / (1.0 + jnp.exp(-z))


_GELU_C = math.sqrt(2.0 / math.pi)


def _gelu_and_grad(z):
    inner = _GELU_C * (z + 0.044715 * z * z * z)
    t = jnp.tanh(inner)
    val = 0.5 * z * (1.0 + t)
    grad = 0.5 * (1.0 + t) + 0.5 * z * (1.0 - t * t) * _GELU_C * (1.0 + 3.0 * 0.044715 * z * z)
    return val, grad


def _row_tile(n, want=512):
    t = min(want, n)
    assert n % t == 0
    return t


def _pcall(body, operands, *, name, grid, in_specs, out_specs, out_shape, scratch_shapes=(), semantics, riders=()):
    single = not isinstance(out_shape, (tuple, list))
    out_specs = (out_specs,) if single else tuple(out_specs)
    out_shape = (out_shape,) if single else tuple(out_shape)
    if not riders:
        res = _pallas(body, name=name, grid=grid, in_specs=list(in_specs), out_specs=out_specs,
                      out_shape=out_shape, scratch_shapes=list(scratch_shapes),
                      compiler_params=_cp(*semantics))(*operands)
        return (res[0] if single else res), []
    n_in, n_out, n_scr = len(in_specs), len(out_shape), len(scratch_shapes)
    ex_in = [len(ex.operands) for ex in riders]
    ex_out = [len(ex.out_shape) for ex in riders]
    ex_scr = [len(ex.scratch) for ex in riders]
    tot_in, tot_out = n_in + sum(ex_in), n_out + sum(ex_out)

    def wrapped(*refs):
        ins, outs, scr = refs[:tot_in], refs[tot_in:tot_in + tot_out], refs[tot_in + tot_out:]
        ids = [pl.program_id(d) for d in range(len(grid))]
        first = functools.reduce(lambda p, t: p & t, [i == 0 for i in ids])
        last = functools.reduce(lambda p, t: p & t, [i == n - 1 for i, n in zip(ids, grid)])
        parts, oi, oo, os_ = [], n_in, n_out, n_scr
        for k in range(len(riders)):
            parts.append((ins[oi:oi + ex_in[k]], outs[oo:oo + ex_out[k]], scr[os_:os_ + ex_scr[k]]))
            oi, oo, os_ = oi + ex_in[k], oo + ex_out[k], os_ + ex_scr[k]

        @pl.when(first)
        def _():
            for ex, part in zip(riders, parts):
                ex.start(*part)

        body(*ins[:n_in], *outs[:n_out], *scr[:n_scr])

        @pl.when(last)
        def _():
            for ex, part in zip(riders, parts):
                ex.finish(*part)

    aliases, oi, oo = {}, n_in, n_out
    all_ops, all_shapes, all_scr = list(operands), list(out_shape), list(scratch_shapes)
    for k, ex in enumerate(riders):
        aliases.update({oi + a: oo + b for a, b in ex.aliases.items()})
        oi, oo = oi + ex_in[k], oo + ex_out[k]
        all_ops += list(ex.operands)
        all_shapes += [pltpu.HBM(s.shape, s.dtype) for s in ex.out_shape]
        all_scr += list(ex.scratch)
    res = _pallas(
        wrapped, name=name, grid=grid, in_specs=list(in_specs) + [HBM_SPEC] * sum(ex_in),
        out_specs=out_specs + (HBM_SPEC,) * sum(ex_out), out_shape=tuple(all_shapes), scratch_shapes=all_scr,
        input_output_aliases=aliases, compiler_params=_cp(*(["arbitrary"] * len(grid))))(*all_ops)
    own = res[0] if single else tuple(res[:n_out])
    carried, oo = [], n_out
    for k in range(len(riders)):
        carried.append(list(res[oo:oo + ex_out[k]]))
        oo += ex_out[k]
    return own, carried


def _carried(out, carried, riders):
    return (out, carried) if riders else out


def _matmul(name, operands, *, grid, in_specs, o_spec, out_shape, out_dtype, dims, riders=()):
    nk = grid[2]
    assert nk == 1 or out_dtype == F32

    def body(a_ref, b_ref, o_ref):
        part = lax.dot_general(a_ref[...].astype(BF16), b_ref[...].astype(BF16), (dims, ((), ())),
                               preferred_element_type=F32)
        if nk == 1:
            o_ref[...] = part.astype(o_ref.dtype)
        else:
            k = pl.program_id(2)

            @pl.when(k == 0)
            def _():
                o_ref[...] = part

            @pl.when(k > 0)
            def _():
                o_ref[...] += part

    out, carried = _pcall(body, operands, name=name, grid=grid, in_specs=in_specs, out_specs=o_spec,
                          out_shape=jax.ShapeDtypeStruct(out_shape, out_dtype),
                          semantics=("parallel", "parallel", "arbitrary"), riders=riders)
    return (out, carried) if riders else out


NN = ((1,), (0,))
NT = ((1,), (1,))
TN = ((0,), (0,))
_TN_TOKENS = 4096


def _mm_cs(name, a, w, out_dtype, riders=()):
    M, K = a.shape
    nq, _, wd = w.shape
    tm = _row_tile(M)
    return _matmul(name, (a, w), grid=(nq, M // tm, 1),
                   in_specs=[pl.BlockSpec((tm, K), lambda j, i, k: (i, 0)),
                             pl.BlockSpec((None, K, wd), lambda j, i, k: (j, 0, 0))],
                   o_spec=pl.BlockSpec((tm, wd), lambda j, i, k: (i, j)),
                   out_shape=(M, nq * wd), out_dtype=out_dtype, dims=NN, riders=riders)


def _mm_rs(name, a, w, out_dtype):
    M, K = a.shape
    N = w.shape[1]
    tm = _row_tile(M)
    return _matmul(name, (a, w), grid=(M // tm, 1, 1),
                   in_specs=[pl.BlockSpec((tm, K), lambda i, j, k: (i, 0)), pl.BlockSpec((K, N), lambda i, j, k: (0, 0))],
                   o_spec=pl.BlockSpec((tm, N), lambda i, j, k: (i, 0)),
                   out_shape=(M, N), out_dtype=out_dtype, dims=NN)


def _mm_nt_rs(name, g, w, out_dtype, riders=()):
    M, N = g.shape
    K = w.shape[0]
    to = K
    tm = _row_tile(M)
    return _matmul(name, (g, w), grid=(M // tm, K // to, 1),
                   in_specs=[pl.BlockSpec((tm, N), lambda i, j, k: (i, 0)),
                             pl.BlockSpec((to, N), lambda i, j, k: (j, 0))],
                   o_spec=pl.BlockSpec((tm, to), lambda i, j, k: (i, j)),
                   out_shape=(M, K), out_dtype=out_dtype, dims=NT, riders=riders)


def _mm_nt_cs(name, g, w, out_dtype, riders=(), stacked=False, norm_bwd=None):
    M = g.shape[-2]
    nq, K, wd = w.shape
    tm = _row_tile(M, 256)

    def product(g_ref, w_ref):
        acc = None
        for q in range(nq):
            gq = g_ref[q // 2, :, (q % 2) * wd:(q % 2 + 1) * wd] if stacked else g_ref[:, q * wd:(q + 1) * wd]
            part = _dot_nt(gq, w_ref[q])
            acc = part if acc is None else acc + part
        return acc

    def body(g_ref, w_ref, o_ref):
        o_ref[...] = product(g_ref, w_ref).astype(o_ref.dtype)

    def body_norm(g_ref, w_ref, x_ref, gain_ref, dr_ref, dx_ref, dg_ref):
        @pl.when(pl.program_id(0) == 0)
        def _():
            dg_ref[...] = jnp.zeros_like(dg_ref)

        dx, dg = _rms_bwd_rows(x_ref[...], gain_ref[...], product(g_ref, w_ref))
        dg_ref[...] += dg
        dx_ref[...] = dx + dr_ref[...]

    g_spec = (pl.BlockSpec((2, tm, 2 * wd), lambda i: (0, i, 0)) if stacked
              else pl.BlockSpec((tm, nq * wd), lambda i: (i, 0)))
    w_spec = pl.BlockSpec((nq, K, wd), lambda i: (0, 0, 0))
    row = pl.BlockSpec((tm, K), lambda i: (i, 0))
    if norm_bwd is None:
        return _carried(*_pcall(
            body, (g, w), name=name, grid=(M // tm,), in_specs=[g_spec, w_spec], out_specs=row,
            out_shape=jax.ShapeDtypeStruct((M, K), out_dtype), semantics=("parallel",), riders=riders), riders)
    vec = pl.BlockSpec((1, K), lambda i: (0, 0))
    return _carried(*_pcall(
        body_norm, (g, w) + tuple(norm_bwd), name=name, grid=(M // tm,),
        in_specs=[g_spec, w_spec, row, vec, row], out_specs=(row, vec),
        out_shape=(jax.ShapeDtypeStruct((M, K), F32), jax.ShapeDtypeStruct((1, K), F32)),
        semantics=("arbitrary",), riders=riders), riders)


def _mm_tn_rs(name, a, g, to, tn=512):
    T, M = a.shape
    N = g.shape[1]
    tt = _row_tile(T, _TN_TOKENS)
    tn = min(tn, N)
    return _matmul(name, (a, g), grid=(M // to, N // tn, T // tt),
                   in_specs=[pl.BlockSpec((tt, to), lambda i, j, k: (k, i)),
                             pl.BlockSpec((tt, tn), lambda i, j, k: (k, j))],
                   o_spec=pl.BlockSpec((to, tn), lambda i, j, k: (i, j)),
                   out_shape=(M, N), out_dtype=F32, dims=TN)


def _mm_tn_cs(name, a, g, nq, to, riders=(), stacked=False):
    T, M = a.shape
    wd = g.shape[-1] * (2 if stacked else 1) // nq
    tt = _row_tile(T, _TN_TOKENS)
    g_spec = (pl.BlockSpec((None, tt, wd), lambda i, j, k: (j // 2, k, j % 2)) if stacked
              else pl.BlockSpec((tt, wd), lambda i, j, k: (k, j)))
    return _matmul(name, (a, g), grid=(M // to, nq, T // tt),
                   in_specs=[pl.BlockSpec((tt, to), lambda i, j, k: (k, i)), g_spec],
                   o_spec=pl.BlockSpec((None, to, wd), lambda i, j, k: (j, i, 0)),
                   out_shape=(nq, M, wd), out_dtype=F32, dims=TN, riders=riders)


def _rms_fwd(name, x, g):
    T, D = x.shape
    tm = _row_tile(T)

    def body(x_ref, g_ref, o_ref):
        xv = x_ref[...]
        r = lax.rsqrt(jnp.mean(xv * xv, axis=-1, keepdims=True) + EPS)
        o_ref[...] = (xv * r * g_ref[...]).astype(o_ref.dtype)

    return _pallas(
        body, name=name, grid=(T // tm,),
        in_specs=[pl.BlockSpec((tm, D), lambda i: (i, 0)), pl.BlockSpec((1, D), lambda i: (0, 0))],
        out_specs=pl.BlockSpec((tm, D), lambda i: (i, 0)),
        out_shape=jax.ShapeDtypeStruct((T, D), BF16), compiler_params=_cp("parallel"),
    )(x, g)


def _rms_rows(xv, gain):
    return xv * lax.rsqrt(jnp.mean(xv * xv, axis=-1, keepdims=True) + EPS) * gain


def _rms_bwd_rows(xv, gain, dh):
    r = lax.rsqrt(jnp.mean(xv * xv, axis=-1, keepdims=True) + EPS)
    n = xv * r
    dn = dh * gain
    return r * (dn - n * jnp.mean(dn * n, axis=-1, keepdims=True)), jnp.sum(dh * n, axis=0, keepdims=True)


def _rms_bwd(name, x, g, dh, dres):
    T, D = x.shape
    tm = _row_tile(T)
    has_res = dres is not None

    def body(*refs):
        if has_res:
            x_ref, g_ref, dh_ref, dr_ref, dx_ref, dg_ref = refs
        else:
            x_ref, g_ref, dh_ref, dx_ref, dg_ref = refs

        @pl.when(pl.program_id(0) == 0)
        def _():
            dg_ref[...] = jnp.zeros_like(dg_ref)

        dx, dg = _rms_bwd_rows(x_ref[...], g_ref[...], dh_ref[...])
        dg_ref[...] += dg
        if has_res:
            dx = dx + dr_ref[...]
        dx_ref[...] = dx

    row = pl.BlockSpec((tm, D), lambda i: (i, 0))
    vec = pl.BlockSpec((1, D), lambda i: (0, 0))
    ops = (x, g, dh, dres) if has_res else (x, g, dh)
    return _pallas(
        body, name=name, grid=(T // tm,), in_specs=[row, vec, row] + ([row] if has_res else []),
        out_specs=(row, vec),
        out_shape=(jax.ShapeDtypeStruct((T, D), F32), jax.ShapeDtypeStruct((1, D), F32)),
        compiler_params=_cp("arbitrary"),
    )(*ops)


def _proj_res_norm(name, a, w, res, gain):
    M, K = a.shape
    N = w.shape[1]
    tm = _row_tile(M)

    def body(a_ref, w_ref, r_ref, g_ref, x_ref, h_ref):
        xv = _dot(a_ref[...], w_ref[...]) + r_ref[...]
        x_ref[...] = xv
        h_ref[...] = _rms_rows(xv, g_ref[...]).astype(h_ref.dtype)

    row = pl.BlockSpec((tm, N), lambda i: (i, 0))
    return _pallas(
        body, name=name, grid=(M // tm,),
        in_specs=[pl.BlockSpec((tm, K), lambda i: (i, 0)), pl.BlockSpec((K, N), lambda i: (0, 0)), row,
                  pl.BlockSpec((1, N), lambda i: (0, 0))],
        out_specs=(row, row), out_shape=(jax.ShapeDtypeStruct((M, N), F32), jax.ShapeDtypeStruct((M, N), BF16)),
        compiler_params=_cp("parallel"),
    )(a, w, res, gain)


def _proj_res_loss(name, a, w, res, tgt, gain):
    M, K = a.shape
    D = w.shape[1]
    tm = _row_tile(M)

    def body(a_ref, w_ref, r_ref, t_ref, g_ref, dx_ref, dg_ref, loss_ref):
        @pl.when(pl.program_id(0) == 0)
        def _():
            dg_ref[...] = jnp.zeros_like(dg_ref)
            loss_ref[...] = jnp.zeros_like(loss_ref)

        xv = _dot(a_ref[...], w_ref[...]) + r_ref[...]
        gv = g_ref[...]
        diff = _rms_rows(xv, gv) - t_ref[...]
        loss_ref[...] += 0.5 * jnp.sum(jnp.mean(diff * diff, axis=-1, keepdims=True))
        dx, dg = _rms_bwd_rows(xv, gv, diff * (1.0 / D))
        dg_ref[...] += dg
        dx_ref[...] = dx

    row = pl.BlockSpec((tm, D), lambda i: (i, 0))
    vec = pl.BlockSpec((1, D), lambda i: (0, 0))
    return _pallas(
        body, name=name, grid=(M // tm,),
        in_specs=[pl.BlockSpec((tm, K), lambda i: (i, 0)), pl.BlockSpec((K, D), lambda i: (0, 0)), row, row, vec],
        out_specs=(row, vec, pl.BlockSpec((8, 128), lambda i: (0, 0))),
        out_shape=(jax.ShapeDtypeStruct((M, D), F32), jax.ShapeDtypeStruct((1, D), F32),
                   jax.ShapeDtypeStruct((8, 128), F32)),
        compiler_params=_cp("arbitrary"),
    )(a, w, res, tgt, gain)


def _gmlp_pieces(zu, zv, lng, lnb, ws_ref, bs_ref):
    u, du = _gelu_and_grad(zu)
    v, dv = _gelu_and_grad(zv)
    mu = jnp.mean(v, axis=-1, keepdims=True)
    vc = v - mu
    rstd = lax.rsqrt(jnp.mean(vc * vc, axis=-1, keepdims=True) + EPS)
    vhat = vc * rstd
    vn = vhat * lng + lnb
    row = lax.broadcasted_iota(jnp.int32, (GM_CHUNK, GM_CHUNK), 0)
    col = lax.broadcasted_iota(jnp.int32, (GM_CHUNK, GM_CHUNK), 1)
    tril = row >= col
    wms, mixed = [], []
    for g in range(GM_GROUPS):
        sl = slice(g * 128, (g + 1) * 128)
        wm = jnp.where(tril, ws_ref[g], 0.0)
        wms.append(wm)
        mixed.append(_dot(wm, vn[:, sl]) + bs_ref[g])
    return u, du, dv, rstd, vhat, vn, wms, mixed, tril


def _gmlp_fwd(proj, lng, lnb, ws, bs_col):
    T = proj.shape[0]
    n = T // GM_CHUNK

    def body(zu_ref, zv_ref, lng_ref, lnb_ref, ws_ref, bs_ref, o_ref):
        u, _, _, _, _, _, _, mixed, _ = _gmlp_pieces(zu_ref[...].astype(F32), zv_ref[...].astype(F32),
                                                     lng_ref[...], lnb_ref[...],
                                                     ws_ref, bs_ref)
        for g in range(GM_GROUPS):
            sl = slice(g * 128, (g + 1) * 128)
            o_ref[:, sl] = (u[:, sl] * mixed[g]).astype(o_ref.dtype)

    vec = pl.BlockSpec((1, GM_WIDTH), lambda i: (0, 0))
    return _pallas(
        body, name="gmlp_fwd", grid=(n,),
        in_specs=[pl.BlockSpec((GM_CHUNK, 512), lambda i: (i, COL_ZU)),
                  pl.BlockSpec((GM_CHUNK, 512), lambda i: (i, COL_ZV)),
                  vec, vec,
                  pl.BlockSpec((GM_GROUPS, 128, 128), lambda i: (0, 0, 0)),
                  pl.BlockSpec((GM_GROUPS, 128, 1), lambda i: (0, 0, 0))],
        out_specs=pl.BlockSpec((GM_CHUNK, 512), lambda i: (i, 0)),
        out_shape=jax.ShapeDtypeStruct((T, GM_WIDTH), BF16), compiler_params=_cp("parallel"),
    )(proj, proj, lng, lnb, ws, bs_col)


def _gmlp_bwd(proj, d_out, lng, lnb, ws, bs_col, riders=()):
    T = proj.shape[0]
    n = T // GM_CHUNK

    def body(zu_ref, zv_ref, do_ref, lng_ref, lnb_ref, ws_ref, bs_ref,
             dz_ref, dws_ref, dbs_ref, dlng_ref, dlnb_ref, dm_acc):
        i = pl.program_id(0)

        @pl.when(i == 0)
        def _():
            dws_ref[...] = jnp.zeros_like(dws_ref)
            dlng_ref[...] = jnp.zeros_like(dlng_ref)
            dlnb_ref[...] = jnp.zeros_like(dlnb_ref)
            dm_acc[...] = jnp.zeros_like(dm_acc)

        lng_v = lng_ref[...]
        u, du, dv, rstd, vhat, vn, wms, mixed, tril = _gmlp_pieces(zu_ref[...].astype(F32), zv_ref[...].astype(F32),
                                                                  lng_v, lnb_ref[...],
                                                                  ws_ref, bs_ref)
        do = do_ref[...]
        dvn_parts = []
        for g in range(GM_GROUPS):
            sl = slice(g * 128, (g + 1) * 128)
            dog = do[:, sl]
            dz_ref[:, sl] = (dog * mixed[g] * du[:, sl]).astype(dz_ref.dtype)
            dmix = dog * u[:, sl]
            dm_acc[:, sl] += dmix
            dws_ref[g] += jnp.where(tril, _dot_nt(dmix, vn[:, sl]), 0.0)
            dvn_parts.append(_dot_tn(wms[g], dmix))
        dvn = jnp.concatenate(dvn_parts, axis=1)
        dlng_ref[...] += jnp.sum(dvn * vhat, axis=0, keepdims=True)
        dlnb_ref[...] += jnp.sum(dvn, axis=0, keepdims=True)
        dvh = dvn * lng_v
        dvv = rstd * (dvh - jnp.mean(dvh, axis=-1, keepdims=True)
                      - vhat * jnp.mean(dvh * vhat, axis=-1, keepdims=True))
        dz_ref[:, GM_WIDTH:] = (dvv * dv).astype(dz_ref.dtype)

        @pl.when(i == n - 1)
        def _():
            for g in range(GM_GROUPS):
                dbs_ref[g] = jnp.sum(dm_acc[:, g * 128:(g + 1) * 128], axis=1, keepdims=True)

    vec = pl.BlockSpec((1, GM_WIDTH), lambda i: (0, 0))
    wsp = pl.BlockSpec((GM_GROUPS, 128, 128), lambda i: (0, 0, 0))
    bsp = pl.BlockSpec((GM_GROUPS, 128, 1), lambda i: (0, 0, 0))
    return _carried(*_pcall(
        body, (proj, proj, d_out, lng, lnb, ws, bs_col), name="gmlp_bwd", grid=(n,),
        in_specs=[pl.BlockSpec((GM_CHUNK, 512), lambda i: (i, COL_ZU)),
                  pl.BlockSpec((GM_CHUNK, 512), lambda i: (i, COL_ZV)),
                  pl.BlockSpec((None, GM_CHUNK, 512), lambda i: (0, i, 0)), vec, vec, wsp, bsp],
        out_specs=(pl.BlockSpec((GM_CHUNK, 2 * GM_WIDTH), lambda i: (i, 0)), wsp, bsp, vec, vec),
        out_shape=(jax.ShapeDtypeStruct((T, 2 * GM_WIDTH), BF16),
                   jax.ShapeDtypeStruct((GM_GROUPS, 128, 128), F32), jax.ShapeDtypeStruct((GM_GROUPS, 128, 1), F32),
                   jax.ShapeDtypeStruct((1, GM_WIDTH), F32), jax.ShapeDtypeStruct((1, GM_WIDTH), F32)),
        scratch_shapes=[pltpu.VMEM((GM_CHUNK, GM_WIDTH), F32)],
        semantics=("arbitrary",), riders=riders), riders)


def _hgrn_lower_bound(lbl):
    return 1.0 / (1.0 + jnp.exp(lbl[1:2, :] - lbl[0:1, :]))


def _hgrn_gates(hq, hf, lb):
    C = HG_CHUNK
    sg = _sigmoid(hf)
    fg = lb + (1.0 - lb) * sg
    sq = _sigmoid(hq)
    row = lax.broadcasted_iota(jnp.int32, (C, C), 0)
    col = lax.broadcasted_iota(jnp.int32, (C, C), 1)
    tril = row >= col
    logf = jnp.log(fg)
    a = _dot_01(tril, logf)
    a_last = jnp.sum(logf, axis=0, keepdims=True)
    first_half = lax.broadcasted_iota(jnp.int32, logf.shape, 0) < (C // 2)
    a_mid = jnp.sum(jnp.where(first_half, logf, 0.0), axis=0, keepdims=True)
    ea, ei, eki, ekl = jnp.exp(a), jnp.exp(a - a_mid), jnp.exp(a_mid - a), jnp.exp(a_last - a)
    k = 1.0 - fg
    q = hq * sq
    qi = (q * ei).astype(BF16).astype(F32)
    ki = (k * eki).astype(BF16).astype(F32)
    return dict(sg=sg, fg=fg, sq=sq, tril=tril, ea=ea, ei=ei, eki=eki, ekl=ekl, e_last=jnp.exp(a_last),
                qe=q * ea, qi=qi, ki=ki, kl=k * ekl)


def _heads(x):
    return [x[:, h * HG_DIM:(h + 1) * HG_DIM] for h in range(HG_HEADS)]


def _hgrn_fwd(proj, lbl, gh, B, S, riders=()):
    C = HG_CHUNK
    NC = S // C
    W = HG_HEADS * HG_DIM

    def body(q_ref, f_ref, i_ref, g_ref, lbl_ref, gh_ref, o_ref, bo_ref, st_ref, state):
        @pl.when(pl.program_id(0) == 0)
        def _():
            state[...] = jnp.zeros_like(state)

        lb = _hgrn_lower_bound(lbl_ref[...])
        ghv = gh_ref[...]
        for b in range(B):
            gt = _hgrn_gates(q_ref[b].astype(F32), f_ref[b].astype(F32), lb)
            v = _heads(i_ref[b])
            qe, qi, ki, kl, e_last = (_heads(gt[n]) for n in ("qe", "qi", "ki", "kl", "e_last"))
            outs, normed = [], []
            for h in range(HG_HEADS):
                p = jnp.where(gt["tril"], _dot_nt(qi[h], ki[h]), 0.0)
                st = state[b, h]
                st_ref[b, h] = st
                o = _dot_nt(qe[h], st) + _dot(p, v[h])
                state[b, h] = st * e_last[h] + _dot_tn(v[h], kl[h])
                outs.append(o)
                normed.append(o * lax.rsqrt(jnp.mean(o * o, axis=-1, keepdims=True) + EPS) * ghv)
            o_ref[b] = jnp.concatenate(outs, axis=1)
            hg = g_ref[b].astype(F32)
            bo_ref[b] = (jnp.concatenate(normed, axis=1) * (hg * _sigmoid(hg))).astype(bo_ref.dtype)

    def col(cb):
        return pl.BlockSpec((B, C, 512), lambda c: (0, c, cb))

    tile = pl.BlockSpec((B, C, W), lambda c: (0, c, 0))
    proj3 = proj.reshape(B, S, proj.shape[-1])
    out, carried = _pcall(
        body, (proj3, proj3, proj3, proj3, lbl, gh), name="hgrn_fwd", grid=(NC,),
        in_specs=[col(COL_HQ), col(COL_HF), col(COL_HI), col(COL_HG),
                  pl.BlockSpec((2, W), lambda c: (0, 0)), pl.BlockSpec((1, HG_DIM), lambda c: (0, 0))],
        out_specs=(tile, tile, pl.BlockSpec((B, None, HG_HEADS, 128, 128), lambda c: (0, c, 0, 0, 0))),
        out_shape=(jax.ShapeDtypeStruct((B, S, W), F32), jax.ShapeDtypeStruct((B, S, W), BF16),
                   jax.ShapeDtypeStruct((B, NC, HG_HEADS, 128, 128), F32)),
        scratch_shapes=[pltpu.VMEM((B, HG_HEADS, 128, 128), F32)],
        semantics=("arbitrary",), riders=riders)
    o_h, b_out, states = out
    out = (o_h, b_out.reshape(B * S, W), states)
    return (out, carried) if riders else out


def _hgrn_bwd(proj, o_saved, states, d_out, lbl, gh, others, B, S, riders=()):
    C = HG_CHUNK
    NC = S // C
    W = HG_HEADS * HG_DIM
    d_gm, d_xq, d_gates = (t.reshape(B, S, t.shape[-1]) for t in others)
    own0 = d_gm.shape[-1]
    xq0 = own0 + 4 * W
    gates0 = xq0 + d_xq.shape[-1]

    def body(q_ref, f_ref, i_ref, g_ref, o_ref, st_ref, do_ref, lbl_ref, gh_ref, gm_ref, xq_ref, gates_ref,
             d_ref, dlbl_ref, dgh_ref, dstate, dlb_acc):
        c = pl.program_id(0)
        d_ref[:, :, :own0] = gm_ref[...]
        d_ref[:, :, xq0:gates0] = xq_ref[...]
        d_ref[:, :, gates0:] = gates_ref[...]

        def put(b, k, val):
            d_ref[b, :, own0 + k * W:own0 + (k + 1) * W] = val.astype(d_ref.dtype)

        @pl.when(c == 0)
        def _():
            dstate[...] = jnp.zeros_like(dstate)
            dgh_ref[...] = jnp.zeros_like(dgh_ref)
            dlb_acc[...] = jnp.zeros_like(dlb_acc)

        lb = _hgrn_lower_bound(lbl_ref[...])
        ghv = gh_ref[...]
        row = lax.broadcasted_iota(jnp.int32, (C, C), 0)
        colm = lax.broadcasted_iota(jnp.int32, (C, C), 1)
        triu = colm >= row
        for b in range(B):
            hq, hg = q_ref[b].astype(F32), g_ref[b].astype(F32)
            gt = _hgrn_gates(hq, f_ref[b].astype(F32), lb)
            tril = gt["tril"]
            v = _heads(i_ref[b])
            qe, qi, ki, kl, e_last = (_heads(gt[n]) for n in ("qe", "qi", "ki", "kl", "e_last"))
            sgg = _sigmoid(hg)
            don_all = do_ref[b] * (hg * sgg)
            o, don = _heads(o_ref[b]), _heads(don_all)
            d_qe, d_qi, d_ki, d_kl, dv, n_all, dal = [], [], [], [], [], [], []
            for h in range(HG_HEADS):
                r = lax.rsqrt(jnp.mean(o[h] * o[h], axis=-1, keepdims=True) + EPS)
                n = o[h] * r
                n_all.append(n)
                dgh_ref[...] += jnp.sum(don[h] * n, axis=0, keepdims=True)
                dn = don[h] * ghv
                d_o = r * (dn - n * jnp.mean(dn * n, axis=-1, keepdims=True))
                st, dst = st_ref[b, h], dstate[b, h]
                p = jnp.where(tril, _dot_nt(qi[h], ki[h]), 0.0)
                dp = jnp.where(tril, _dot_nt(d_o, v[h]), 0.0)
                d_qe.append(_dot(d_o, st))
                d_qi.append(_dot(dp, ki[h]))
                d_ki.append(_dot_tn(dp, qi[h]))
                d_kl.append(_dot(v[h], dst))
                dv.append(_dot_tn(p, d_o) + _dot_nt(kl[h], dst))
                dstate[b, h] = dst * e_last[h] + _dot_tn(d_o, qe[h])
                dal.append(jnp.sum(dst * st, axis=0, keepdims=True) * e_last[h])
            d_qe, d_qi, d_ki, d_kl, n_all, dal = (jnp.concatenate(t, axis=1)
                                                  for t in (d_qe, d_qi, d_ki, d_kl, n_all, dal))
            put(b, 3, do_ref[b] * n_all * jnp.tile(ghv, (1, HG_HEADS)) * (sgg * (1.0 + hg * (1.0 - sgg))))
            put(b, 2, jnp.concatenate(dv, axis=1))
            d_a_last = dal + jnp.sum(d_kl * gt["kl"], axis=0, keepdims=True)
            dq = d_qe * gt["ea"] + d_qi * gt["ei"]
            dk = d_ki * gt["eki"] + d_kl * gt["ekl"]
            da = d_qe * gt["qe"] + d_qi * gt["qi"] - d_ki * gt["ki"] - d_kl * gt["kl"]
            dlogf = _dot_01(triu, da) + d_a_last
            sg, sq = gt["sg"], gt["sq"]
            dfg = dlogf / gt["fg"] - dk
            put(b, 1, dfg * (1.0 - lb) * sg * (1.0 - sg))
            dlb_acc[...] += jnp.sum(dfg * (1.0 - sg), axis=0, keepdims=True)
            put(b, 0, dq * (sq * (1.0 + hq * (1.0 - sq))))

        @pl.when(c == NC - 1)
        def _():
            dlb = dlb_acc[...]
            first = lax.broadcasted_iota(jnp.int32, (2, W), 0) == 0
            dlbl_ref[...] = jnp.where(first, dlb * lb * (1.0 - lb), -dlb * lb * (1.0 - lb))

    def col(cb):
        return pl.BlockSpec((B, C, 512), lambda c: (0, NC - 1 - c, cb))

    tile = pl.BlockSpec((B, C, W), lambda c: (0, NC - 1 - c, 0))
    proj3 = proj.reshape(B, S, proj.shape[-1])

    def rows(width):
        return pl.BlockSpec((B, C, width), lambda c: (0, NC - 1 - c, 0))

    width = proj.shape[-1]
    out, carried = _pcall(
        body, (proj3, proj3, proj3, proj3, o_saved, states, d_out.reshape(3, B, S, W), lbl, gh, d_gm, d_xq, d_gates),
        name="hgrn_bwd", grid=(NC,),
        in_specs=[col(COL_HQ), col(COL_HF), col(COL_HI), col(COL_HG), tile,
                  pl.BlockSpec((B, None, HG_HEADS, 128, 128), lambda c: (0, NC - 1 - c, 0, 0, 0)),
                  pl.BlockSpec((None, B, C, W), lambda c: (1, 0, NC - 1 - c, 0)),
                  pl.BlockSpec((2, W), lambda c: (0, 0)), pl.BlockSpec((1, HG_DIM), lambda c: (0, 0)),
                  rows(d_gm.shape[-1]), rows(d_xq.shape[-1]), rows(d_gates.shape[-1])],
        out_specs=(rows(width), pl.BlockSpec((2, W), lambda c: (0, 0)), pl.BlockSpec((1, HG_DIM), lambda c: (0, 0))),
        out_shape=(jax.ShapeDtypeStruct((B, S, width), BF16), jax.ShapeDtypeStruct((2, W), F32),
                   jax.ShapeDtypeStruct((1, HG_DIM), F32)),
        scratch_shapes=[pltpu.VMEM((B, HG_HEADS, 128, 128), F32), pltpu.VMEM((1, W), F32)],
        semantics=("arbitrary",), riders=riders)
    out = (out[0].reshape(B * S, width),) + tuple(out[1:])
    return (out, carried) if riders else out


_XA_SCALE = XA_DIM ** -0.5


def _attn_probs(qh, kh):
    s = _dot_nt(qh, kh) * _XA_SCALE
    e = jnp.exp(s - jnp.max(s, axis=-1, keepdims=True))
    return e / jnp.sum(e, axis=-1, keepdims=True)


def _attn_fwd(proj, kv, B, S):
    T = B * S
    tq = _row_tile(S)
    nq = S // tq
    W = XA_HEADS * XA_DIM

    def body(q_ref, kv_ref, o_ref):
        for h in range(XA_HEADS):
            sl = slice(h * 128, (h + 1) * 128)
            p = _attn_probs(q_ref[:, sl], kv_ref[:, sl])
            o_ref[:, sl] = _dot(p, kv_ref[:, W + h * 128:W + (h + 1) * 128]).astype(o_ref.dtype)

    return _pallas(
        body, name="attn_fwd", grid=(B, nq),
        in_specs=[pl.BlockSpec((tq, 512), lambda b, i: (b * nq + i, COL_XQ)),
                  pl.BlockSpec((MEM_LEN, 2 * W), lambda b, i: (b, 0))],
        out_specs=pl.BlockSpec((tq, W), lambda b, i: (b * nq + i, 0)),
        out_shape=jax.ShapeDtypeStruct((T, W), BF16), compiler_params=_cp("parallel", "parallel"),
    )(proj, kv)


def _attn_bwd(proj, kv, d_out, B, S):
    T = B * S
    tq = _row_tile(S)
    nq = S // tq
    W = XA_HEADS * XA_DIM

    def body(q_ref, kv_ref, do_ref, dq_ref, dkv_ref):
        @pl.when(pl.program_id(1) == 0)
        def _():
            dkv_ref[...] = jnp.zeros_like(dkv_ref)

        for h in range(XA_HEADS):
            sl = slice(h * 128, (h + 1) * 128)
            slv = slice(W + h * 128, W + (h + 1) * 128)
            qh = q_ref[:, sl]
            kh = kv_ref[:, sl]
            p = _attn_probs(qh, kh)
            dc = do_ref[:, sl]
            dp = _dot_nt(dc, kv_ref[:, slv])
            ds = p * (dp - jnp.sum(dp * p, axis=-1, keepdims=True)) * _XA_SCALE
            dq_ref[:, sl] = _dot(ds, kh).astype(dq_ref.dtype)
            dkv_ref[:, sl] += _dot_tn(ds, qh)
            dkv_ref[:, slv] += _dot_tn(p, dc)

    kvspec = pl.BlockSpec((MEM_LEN, 2 * W), lambda b, i: (b, 0))
    tile = pl.BlockSpec((tq, W), lambda b, i: (b * nq + i, 0))
    return _pallas(
        body, name="attn_bwd", grid=(B, nq),
        in_specs=[pl.BlockSpec((tq, 512), lambda b, i: (b * nq + i, COL_XQ)), kvspec,
                  pl.BlockSpec((None, tq, W), lambda b, i: (2, b * nq + i, 0))],
        out_specs=(tile, kvspec),
        out_shape=(jax.ShapeDtypeStruct((T, W), BF16), jax.ShapeDtypeStruct((B * MEM_LEN, 2 * W), F32)),
        compiler_params=_cp("parallel", "arbitrary"),
    )(proj, kv, d_out)


_MERGE_TM = 256
_GATE_W = 512


def _gate_specs(tm):
    base = COL_GATE0 // _GATE_W
    return [pl.BlockSpec((tm, _GATE_W), functools.partial(lambda i, k: (i, base + k), k=k)) for k in range(6)]


def _merge_fwd(a_out, b_out, c_out, wb, proj, riders=()):
    T = a_out.shape[0]
    tm = _row_tile(T, _MERGE_TM)
    nq, _, wd = wb.shape
    per_half = _GATE_W // wd

    def body(a_ref, b_ref, c_ref, w_ref, *rest):
        gates, (m_ref, up_ref) = rest[:6], rest[6:]
        for hf in range(2):
            cols = slice(hf * _GATE_W, (hf + 1) * _GATE_W)
            acc = None
            for n, br in enumerate((a_ref, b_ref, c_ref)):
                x = br[...]
                up = jnp.concatenate([_dot(x, w_ref[per_half * hf + j, n * BR_WIDTH:(n + 1) * BR_WIDTH, :])
                                      for j in range(per_half)], axis=1)
                up_ref[n, :, cols] = up.astype(up_ref.dtype)
                term = _sigmoid(gates[2 * n + hf][...].astype(F32)) * up
                acc = term if acc is None else acc + term
            m_ref[:, cols] = acc.astype(m_ref.dtype)

    br_spec = pl.BlockSpec((tm, BR_WIDTH), lambda i: (i, 0))
    return _carried(*_pcall(
        body, (a_out, b_out, c_out, wb, *([proj] * 6)), name="merge_fwd", grid=(T // tm,),
        in_specs=[br_spec, br_spec, br_spec,
                  pl.BlockSpec((nq, 3 * BR_WIDTH, wd), lambda i: (0, 0, 0))] + _gate_specs(tm),
        out_specs=(pl.BlockSpec((tm, D_MODEL), lambda i: (i, 0)), pl.BlockSpec((3, tm, D_MODEL), lambda i: (0, i, 0))),
        out_shape=(jax.ShapeDtypeStruct((T, D_MODEL), BF16), jax.ShapeDtypeStruct((3, T, D_MODEL), BF16)),
        semantics=("parallel",), riders=riders), riders)


def _branch_bwd_act(d_ups, wb, riders=()):
    _, T, D = d_ups.shape
    nq, _, wd = wb.shape
    tm = _row_tile(T)

    def body(d_ref, w_ref, o_ref):
        acc = None
        for q in range(nq):
            part = _dot_nt(d_ref[:, q * wd:(q + 1) * wd], w_ref[q])
            acc = part if acc is None else acc + part
        o_ref[...] = acc

    return _carried(*_pcall(
        body, (d_ups, wb), name="d_branch", grid=(3, T // tm),
        in_specs=[pl.BlockSpec((None, tm, D), lambda n, i: (n, i, 0)),
                  pl.BlockSpec((nq, BR_WIDTH, wd), lambda n, i: (0, n, 0))],
        out_specs=pl.BlockSpec((None, tm, BR_WIDTH), lambda n, i: (n, i, 0)),
        out_shape=jax.ShapeDtypeStruct((3, T, BR_WIDTH), F32), semantics=("parallel", "parallel"),
        riders=riders), riders)


def _branch_bwd_weight(name, br, d_ups, n):
    T = br.shape[0]
    D = d_ups.shape[2]
    wd = D // N_CHIPS
    tt = _row_tile(T, _TN_TOKENS)

    def body(b_ref, d_ref, o_ref):
        k = pl.program_id(0)
        for q in range(N_CHIPS):
            part = _dot_tn(b_ref[...], d_ref[:, q * wd:(q + 1) * wd])

            @pl.when(k == 0)
            def _():
                o_ref[q] = part

            @pl.when(k > 0)
            def _():
                o_ref[q] += part

    return _pallas(
        body, name=name, grid=(T // tt,),
        in_specs=[pl.BlockSpec((tt, BR_WIDTH), lambda k: (k, 0)),
                  pl.BlockSpec((None, tt, D), lambda k: (n, k, 0))],
        out_specs=pl.BlockSpec((N_CHIPS, BR_WIDTH, wd), lambda k: (0, 0, 0)),
        out_shape=jax.ShapeDtypeStruct((N_CHIPS, BR_WIDTH, wd), F32), compiler_params=_cp("arbitrary"),
    )(br, d_ups)


def _merge_bwd(d_merged, ups, proj, riders=()):
    T = d_merged.shape[0]
    tm = _row_tile(T, _MERGE_TM)

    def body(dm_ref, up_ref, *rest):
        gates, (dup_ref, dg_ref) = rest[:6], rest[6:]
        for hf in range(2):
            cols = slice(hf * _GATE_W, (hf + 1) * _GATE_W)
            dm = dm_ref[:, cols]
            for n in range(3):
                gate = _sigmoid(gates[2 * n + hf][...].astype(F32))
                dup_ref[n, :, cols] = (dm * gate).astype(dup_ref.dtype)
                dg_ref[:, n * D_MODEL + hf * _GATE_W:n * D_MODEL + (hf + 1) * _GATE_W] = (
                    dm * up_ref[n, :, cols].astype(F32) * gate * (1.0 - gate)).astype(dg_ref.dtype)

    tile = pl.BlockSpec((tm, D_MODEL), lambda i: (i, 0))
    tile3 = pl.BlockSpec((3, tm, D_MODEL), lambda i: (0, i, 0))
    return _carried(*_pcall(
        body, (d_merged, ups, *([proj] * 6)), name="merge_bwd", grid=(T // tm,),
        in_specs=[tile, tile3] + _gate_specs(tm),
        out_specs=(tile3, pl.BlockSpec((tm, 3 * D_MODEL), lambda i: (i, 0))),
        out_shape=(jax.ShapeDtypeStruct((3, T, D_MODEL), BF16), jax.ShapeDtypeStruct((T, 3 * D_MODEL), BF16)),
        semantics=("parallel",), riders=riders), riders)


_CONV_TF = D_FF // 2
_CONV_TS = 256
_HALO = 16


def _conv_fwd(ab, cw, cb, B, S):
    T = B * S
    ts = _row_tile(S, _CONV_TS)
    tf = _CONV_TF
    nb = D_FF // tf
    tps = S // ts
    hb = ts // _HALO

    def body(a_ref, p_ref, b_ref, w_ref, cb_ref, o_ref):
        start = (pl.program_id(0) % tps) == 0
        a = a_ref[...].astype(F32)
        prev = jnp.where(start, 0.0, p_ref[...].astype(F32))
        ext = jnp.concatenate([prev, a], axis=0)
        a1 = pltpu.roll(ext, 1, 0)[_HALO:, :]
        a2 = pltpu.roll(ext, 2, 0)[_HALO:, :]
        ac = cb_ref[...] + w_ref[0] * a2 + w_ref[1] * a1 + w_ref[2] * a
        o_ref[...] = (ac * _sigmoid(ac) * b_ref[...].astype(F32)).astype(o_ref.dtype)

    return _pallas(
        body, name="conv_fwd", grid=(T // ts, nb),
        in_specs=[pl.BlockSpec((ts, tf), lambda i, j: (i, j)),
                  pl.BlockSpec((_HALO, tf), lambda i, j: (jnp.maximum(i * hb - 1, 0), j)),
                  pl.BlockSpec((ts, tf), lambda i, j: (i, j + nb)),
                  pl.BlockSpec((3, 1, tf), lambda i, j: (0, 0, j)),
                  pl.BlockSpec((1, tf), lambda i, j: (0, j))],
        out_specs=pl.BlockSpec((ts, tf), lambda i, j: (i, j)),
        out_shape=jax.ShapeDtypeStruct((T, D_FF), BF16), compiler_params=_cp("parallel", "parallel"),
    )(ab, ab, ab, cw, cb)


def _conv_bwd(ab, d_ff, cw, cb, B, S, riders=()):
    T = B * S
    ts = _row_tile(S, _CONV_TS)
    tf = _CONV_TF
    nb = D_FF // tf
    tps = S // ts
    hb = ts // _HALO
    last_h = T // _HALO - 1
    n_ext = ts + _HALO

    def body(a_ref, ap_ref, an_ref, b_ref, bn_ref, d_ref, dn_ref, w_ref, cb_ref, dab_ref, dw_ref, dcb_ref):
        i = pl.program_id(1)

        @pl.when(i == 0)
        def _():
            dw_ref[...] = jnp.zeros_like(dw_ref)
            dcb_ref[...] = jnp.zeros_like(dcb_ref)

        start = (i % tps) == 0
        end = (i % tps) == tps - 1
        a = a_ref[...].astype(F32)
        ext = jnp.concatenate([jnp.where(start, 0.0, ap_ref[...].astype(F32)), a, an_ref[...].astype(F32)], axis=0)
        r1 = pltpu.roll(ext, 1, 0)[_HALO:, :]
        r2 = pltpu.roll(ext, 2, 0)[_HALO:, :]
        ac = cb_ref[...] + w_ref[0] * r2 + w_ref[1] * r1 + w_ref[2] * ext[_HALO:, :]
        sg = _sigmoid(ac)
        d_e = jnp.concatenate([d_ref[...].astype(F32), jnp.where(end, 0.0, dn_ref[...].astype(F32))], axis=0)
        b_e = jnp.concatenate([b_ref[...].astype(F32), bn_ref[...].astype(F32)], axis=0)
        dab_ref[1] = (d_e[:ts, :] * (ac * sg)[:ts, :]).astype(dab_ref.dtype)
        dac = d_e * b_e * sg * (1.0 + ac * (1.0 - sg))
        u1 = pltpu.roll(dac, n_ext - 1, 0)[:ts, :]
        u2 = pltpu.roll(dac, n_ext - 2, 0)[:ts, :]
        dac0 = dac[:ts, :]
        dab_ref[0] = (w_ref[2] * dac0 + w_ref[1] * u1 + w_ref[0] * u2).astype(dab_ref.dtype)
        dcb_ref[...] += jnp.sum(dac0, axis=0, keepdims=True)
        dw_ref[2] += jnp.sum(dac0 * a, axis=0, keepdims=True)
        dw_ref[1] += jnp.sum(dac0 * r1[:ts, :], axis=0, keepdims=True)
        dw_ref[0] += jnp.sum(dac0 * r2[:ts, :], axis=0, keepdims=True)

    def cur(off):
        return pl.BlockSpec((ts, tf), lambda j, i: (i, j + off))

    def nxt(off):
        return pl.BlockSpec((_HALO, tf), lambda j, i: (jnp.minimum((i + 1) * hb, last_h), j + off))

    return _carried(*_pcall(
        body, (ab, ab, ab, ab, ab, d_ff, d_ff, cw, cb), name="conv_bwd", grid=(nb, T // ts),
        in_specs=[cur(0), pl.BlockSpec((_HALO, tf), lambda j, i: (jnp.maximum(i * hb - 1, 0), j)), nxt(0),
                  cur(nb), nxt(nb), cur(0), nxt(0),
                  pl.BlockSpec((3, 1, tf), lambda j, i: (0, 0, j)), pl.BlockSpec((1, tf), lambda j, i: (0, j))],
        out_specs=(pl.BlockSpec((2, ts, tf), lambda j, i: (0, i, j)), pl.BlockSpec((3, 1, tf), lambda j, i: (0, 0, j)),
                   pl.BlockSpec((1, tf), lambda j, i: (0, j))),
        out_shape=(jax.ShapeDtypeStruct((2, T, D_FF), BF16),
                   jax.ShapeDtypeStruct((3, 1, D_FF), F32), jax.ShapeDtypeStruct((1, D_FF), F32)),
        semantics=("parallel", "arbitrary"), riders=riders), riders)


def _local_step(x, mem, tgt, p, comm, B, S):
    g = {}
    h = _rms_fwd("norm1", x, p["norm1_g"])
    proj = comm.carry("in_proj", lambda r: _mm_cs("in_proj", h, comm.w("w_in"), BF16, riders=r))
    a_out = _gmlp_fwd(proj, p["ln_v_g"], p["ln_v_b"], p["w_spatial"], p["b_spatial"])
    o_h, b_out, states = comm.carry(
        "hgrn_fwd", lambda r: _hgrn_fwd(proj, p["lb_logits"], p["hgrn_norm_g"], B, S, riders=r))
    memn = _rms_fwd("mem_norm", mem, p["mem_norm_g"])
    kv = _mm_rs("mem_kv", memn, comm.w("w_mem_kv"), F32)
    c_out = _attn_fwd(proj, kv, B, S)
    merged, ups = comm.carry(
        "merge_fwd", lambda r: _merge_fwd(a_out, b_out, c_out, comm.w("w_branch"), proj, riders=r))
    x1, h2 = _proj_res_norm("out_proj_norm2", merged, comm.w("w_out"), x, p["norm2_g"])
    ab = comm.carry("up_proj", lambda r: _mm_cs("up_proj", h2, comm.w("w_up"), BF16, riders=r))
    conv_w = comm.w("conv_w")
    ff = _conv_fwd(ab, conv_w, p["conv_b"], B, S)
    dx2, g["final_g"], loss = _proj_res_loss("down_proj_loss", ff, comm.w("w_down"), x1, tgt, p["final_g"])

    comm.grad("w_down", _mm_tn_rs("g_w_down", ff, dx2, to=D_FF // 2))
    d_ff = comm.carry("d_ff", lambda r: _mm_nt_rs("d_ff", dx2, comm.w("w_down"), BF16, riders=r))
    d_ab, g["conv_w"], g["conv_b"] = comm.carry(
        "conv_bwd", lambda r: _conv_bwd(ab, d_ff, conv_w, p["conv_b"], B, S, riders=r))
    comm.grad("w_up", _mm_tn_cs("g_w_up", h2, d_ab, N_CHIPS, to=512, stacked=True))
    d_x1, g["norm2_g"] = comm.carry("d_h2", lambda r: _mm_nt_cs(
        "d_h2_norm2_bwd", d_ab, comm.w("w_up"), F32, riders=r, stacked=True, norm_bwd=(x1, p["norm2_g"], dx2)))
    comm.grad("w_out", _mm_tn_rs("g_w_out", merged, d_x1, to=512))
    d_merged = _mm_nt_rs("d_merged", d_x1, comm.w("w_out"), F32)
    d_ups, d_gates = comm.carry("merge_bwd", lambda r: _merge_bwd(d_merged, ups, proj, riders=r))

    d_br = comm.carry("d_branch", lambda r: _branch_bwd_act(d_ups, comm.w("w_branch"), riders=r))
    comm.grad("w_branch", jnp.concatenate(
        [_branch_bwd_weight("g_w_branch%d" % n, br, d_ups, n) for n, br in enumerate((a_out, b_out, c_out))],
        axis=1))

    d_gm, g["w_spatial"], g["b_spatial"], g["ln_v_g"], g["ln_v_b"] = comm.carry(
        "gmlp_bwd", lambda r: _gmlp_bwd(proj, d_br, p["ln_v_g"], p["ln_v_b"], p["w_spatial"], p["b_spatial"],
                                        riders=r))
    d_xq, d_kv = _attn_bwd(proj, kv, d_br, B, S)
    comm.grad("w_mem_kv", _mm_tn_rs("g_w_mem_kv", memn, d_kv, to=512))
    d_memn = _mm_nt_rs("d_memn", d_kv, comm.w("w_mem_kv"), F32)
    _, g["mem_norm_g"] = _rms_bwd("mem_norm_bwd", mem, p["mem_norm_g"], d_memn, None)
    d_proj, g["lb_logits"], g["hgrn_norm_g"] = comm.carry(
        "hgrn_bwd", lambda r: _hgrn_bwd(proj, o_h, states, d_br, p["lb_logits"], p["hgrn_norm_g"],
                                        (d_gm, d_xq, d_gates), B, S, riders=r))
    comm.small_grads([g[n].reshape(_SMALL_SHAPE[n]) for n in _SMALL_EARLY] + [loss])
    comm.grad("w_in", comm.carry("g_w_in", lambda r: _mm_tn_cs("g_w_in", h, d_proj, N_CHIPS, to=512, riders=r)))
    grad_x, g["norm1_g"] = comm.carry("d_h", lambda r: _mm_nt_cs(
        "d_h_norm1_bwd", d_proj, comm.w("w_in"), F32, riders=r, norm_bwd=(x, p["norm1_g"], d_x1)))
    return loss, grad_x, g


HBM_SPEC = pl.BlockSpec(memory_space=pltpu.HBM)


def _place():
    x, y, c = lax.axis_index("x"), lax.axis_index("y"), lax.axis_index("c")
    other_chips = [(1 - x, y), (x, 1 - y), (1 - x, 1 - y)]
    return x, y, c, other_chips


def _remote(src, dst, send_sem, recv_sem, dev):
    return pltpu.make_async_remote_copy(src_ref=src, dst_ref=dst, send_sem=send_sem, recv_sem=recv_sem,
                                        device_id=dev, device_id_type=MESH_ID)


class _Exchange:
    def __init__(self, operands, out_shape, aliases, scratch, start, finish):
        self.operands, self.out_shape, self.aliases, self.scratch = operands, out_shape, aliases, scratch
        self.start, self.finish = start, finish


def _run_exchanges(name, exs):
    n_in = [len(ex.operands) for ex in exs]
    n_out = [len(ex.out_shape) for ex in exs]
    n_scr = [len(ex.scratch) for ex in exs]

    def body(*refs):
        ins, outs, scr = refs[:sum(n_in)], refs[sum(n_in):sum(n_in) + sum(n_out)], refs[sum(n_in) + sum(n_out):]
        parts, oi, oo, os_ = [], 0, 0, 0
        for k in range(len(exs)):
            parts.append((ins[oi:oi + n_in[k]], outs[oo:oo + n_out[k]], scr[os_:os_ + n_scr[k]]))
            oi, oo, os_ = oi + n_in[k], oo + n_out[k], os_ + n_scr[k]
        for ex, part in zip(exs, parts):
            ex.start(*part)
        for ex, part in zip(exs, parts):
            ex.finish(*part)

    aliases, ops, shapes, scratch, oi, oo = {}, [], [], [], 0, 0
    for k, ex in enumerate(exs):
        aliases.update({oi + a: oo + b for a, b in ex.aliases.items()})
        oi, oo = oi + n_in[k], oo + n_out[k]
        ops += list(ex.operands)
        shapes += [pltpu.HBM(s.shape, s.dtype) for s in ex.out_shape]
        scratch += list(ex.scratch)
    res = _pallas(
        body, name=name, in_specs=[HBM_SPEC] * len(ops), out_specs=(HBM_SPEC,) * len(shapes), out_shape=tuple(shapes),
        input_output_aliases=aliases, scratch_shapes=scratch,
    )(*ops)
    out, oo = [], 0
    for k in range(len(exs)):
        out.append(list(res[oo:oo + n_out[k]]))
        oo += n_out[k]
    return out


def _ex_all_gather(slabs, halved, part=(0, 1)):
    n = len(slabs)

    def rows(a, cc):
        if not halved[a]:
            return slice(None)
        pr = slabs[a].shape[1] // part[1]
        return pl.ds(part[0] * pr + cc * (pr // 2), pr // 2)

    def ici(bufs, scr, a, j, chip, c, mine):
        px, py = chip
        x, y, _, _ = _place()
        qs = 2 * x + y if mine else 2 * px + py
        piece = bufs[a].at[qs, rows(a, c)]
        return _remote(piece, piece, scr[0].at[3 * a + j], scr[1].at[3 * a + j], (px, py, c))

    def d2d(bufs, scr, a, j, chip, cc):
        px, py = chip
        x, y, c, _ = _place()
        piece = bufs[a].at[2 * px + py, rows(a, cc)]
        return _remote(piece, piece, scr[2].at[3 * a + j], scr[3].at[3 * a + j], (x, y, 1 - c))

    def start(ins, outs, scr):
        _, _, c, chips = _place()
        for j, chip in enumerate(chips):
            for a in range(n):
                ici(outs, scr, a, j, chip, c, True).start()

    def finish(ins, outs, scr):
        _, _, c, chips = _place()
        for j, chip in enumerate(chips):
            for a in range(n):
                ici(outs, scr, a, j, chip, c, False).wait_recv()
                if halved[a]:
                    d2d(outs, scr, a, j, chip, c).start()
        for j, chip in enumerate(chips):
            for a in range(n):
                if halved[a]:
                    d2d(outs, scr, a, j, chip, 1 - c).wait_recv()
        for j, chip in enumerate(chips):
            for a in range(n):
                ici(outs, scr, a, j, chip, c, True).wait_send()
                if halved[a]:
                    d2d(outs, scr, a, j, chip, c).wait_send()

    return _Exchange(list(slabs), [jax.ShapeDtypeStruct(s.shape, s.dtype) for s in slabs],
                     {a: a for a in range(n)}, [pltpu.SemaphoreType.DMA((3 * n,))] * 4, start, finish)


def _ex_to_sibling(grads):
    n = len(grads)

    def copy(ins, outs, scr, a):
        x, y, c, _ = _place()
        hr = grads[a].shape[1] // 2
        return _remote(ins[a].at[:, pl.ds((1 - c) * hr, hr), :], outs[a], scr[0].at[a], scr[1].at[a], (x, y, 1 - c))

    def start(ins, outs, scr):
        for a in range(n):
            copy(ins, outs, scr, a).start()

    def finish(ins, outs, scr):
        for a in range(n):
            copy(ins, outs, scr, a).wait()

    out_shape = [jax.ShapeDtypeStruct((g.shape[0], g.shape[1] // 2, g.shape[2]), g.dtype) for g in grads]
    return _Exchange(list(grads), out_shape, {}, [pltpu.SemaphoreType.DMA((n,))] * 2, start, finish)


def _ex_to_owner(parts, part=(0, 1), landing=None):
    n = len(parts)

    def copy(ins, outs, scr, a, j, chip):
        _, _, c, _ = _place()
        px, py = chip
        pr = parts[a].shape[1] // part[1]
        rows = pl.ds(part[0] * pr, pr)
        return _remote(ins[a].at[2 * px + py, rows], outs[a].at[j, rows], scr[0].at[3 * a + j],
                       scr[1].at[3 * a + j], (px, py, c))

    def start(ins, outs, scr):
        for j, chip in enumerate(_place()[3]):
            for a in range(n):
                copy(ins, outs, scr, a, j, chip).start()

    def finish(ins, outs, scr):
        for j, chip in enumerate(_place()[3]):
            for a in range(n):
                copy(ins, outs, scr, a, j, chip).wait()

    out_shape = [jax.ShapeDtypeStruct((3,) + p.shape[1:], p.dtype) for p in parts]
    operands, aliases = list(parts), {}
    if landing is not None:
        operands, aliases = operands + list(landing), {n + a: a for a in range(n)}
    return _Exchange(operands, out_shape, aliases, [pltpu.SemaphoreType.DMA((3 * n,))] * 2, start, finish)


def _ex_share_halves(bufs):
    n = len(bufs)

    def copy(outs, scr, a, cc):
        x, y, c, _ = _place()
        hr = bufs[a].shape[0] // 2
        piece = outs[a].at[pl.ds(cc * hr, hr), :]
        return _remote(piece, piece, scr[0].at[a], scr[1].at[a], (x, y, 1 - c))

    def start(ins, outs, scr):
        c = _place()[2]
        for a in range(n):
            copy(outs, scr, a, c).start()

    def finish(ins, outs, scr):
        c = _place()[2]
        for a in range(n):
            copy(outs, scr, a, c).wait_send()
            copy(outs, scr, a, 1 - c).wait_recv()

    return _Exchange(list(bufs), [jax.ShapeDtypeStruct(b.shape, b.dtype) for b in bufs], {a: a for a in range(n)},
                     [pltpu.SemaphoreType.DMA((n,))] * 2, start, finish)


def _ex_gather_small(arrs):
    n = len(arrs)

    def peer_of(m):
        x, y, c, _ = _place()
        return (1 - x if m & 4 else x, 1 - y if m & 2 else y, 1 - c if m & 1 else c)

    def start(ins, outs, scr):
        x, y, c, _ = _place()
        for m in range(1, N_DEV):
            for a in range(n):
                k = (N_DEV - 1) * a + m - 1
                _remote(ins[a], outs[a].at[4 * x + 2 * y + c], scr[0].at[k], scr[1].at[k], peer_of(m)).start()

    def finish(ins, outs, scr):
        for m in range(1, N_DEV):
            px, py, pc = peer_of(m)
            for a in range(n):
                k = (N_DEV - 1) * a + m - 1
                slot = outs[a].at[4 * px + 2 * py + pc]
                cp = _remote(ins[a], slot, scr[0].at[k], scr[1].at[k], (px, py, pc))
                cp.wait_send()
                cp.wait_recv()

    slots = [jnp.zeros((N_DEV,) + a.shape, a.dtype) for a in arrs]
    out_shape = [jax.ShapeDtypeStruct(s.shape, s.dtype) for s in slots]
    return _Exchange(list(arrs) + slots, out_shape, {n + a: a for a in range(n)},
                     [pltpu.SemaphoreType.DMA(((N_DEV - 1) * n,))] * 2, start, finish)


def _div_tile(n, want):
    best = None
    for t in range(8, min(n, want) + 1, 8):
        if n % t == 0:
            best = t
    assert best is not None, n
    return best


def _cast_into_slab(name, w, place, dtype):
    r, cc = w.shape
    tr = r if r * cc <= 128 * 1024 else _div_tile(r, 256)

    def body(s_ref, w_ref, o_ref):
        o_ref[...] = w_ref[...].astype(o_ref.dtype)

    return _pallas(
        body, name=name,
        grid_spec=pltpu.PrefetchScalarGridSpec(
            num_scalar_prefetch=1, grid=(r // tr,),
            in_specs=[pl.BlockSpec((tr, cc), lambda i, s: (i, 0))],
            out_specs=pl.BlockSpec((None, tr, cc), lambda i, s: (s[0], i, 0))),
        out_shape=jax.ShapeDtypeStruct((N_CHIPS, r, cc), dtype), compiler_params=_cp("parallel"),
    )(place, w)


def _add_half(name, g, rcv, place):
    nq, r, cc = g.shape
    hr = r // 2
    tr = _div_tile(hr, 128)
    nb = hr // tr

    def body(s_ref, g_ref, r_ref, o_ref):
        o_ref[...] = (g_ref[...] + r_ref[...]).astype(o_ref.dtype)

    spec = pl.BlockSpec((None, tr, cc), lambda i, j, s: (i, j, 0))
    return _pallas(
        body, name=name,
        grid_spec=pltpu.PrefetchScalarGridSpec(
            num_scalar_prefetch=1, grid=(nq, nb),
            in_specs=[pl.BlockSpec((None, tr, cc), lambda i, j, s: (i, s[1] * nb + j, 0)), spec], out_specs=spec),
        out_shape=jax.ShapeDtypeStruct((nq, hr, cc), BF16), compiler_params=_cp("parallel", "parallel"),
    )(place, g, rcv)


def _sum_owner(name, part, rcv, place):
    _, hr, cc = part.shape
    tr = _div_tile(hr, 128)
    nb = hr // tr

    def body(s_ref, p_ref, r_ref, o_ref):
        o_ref[...] = ((p_ref[...].astype(F32) + r_ref[0].astype(F32)) + r_ref[1].astype(F32)) + r_ref[2].astype(F32)

    return _pallas(
        body, name=name,
        grid_spec=pltpu.PrefetchScalarGridSpec(
            num_scalar_prefetch=1, grid=(nb,),
            in_specs=[pl.BlockSpec((None, tr, cc), lambda i, s: (s[0], i, 0)),
                      pl.BlockSpec((3, tr, cc), lambda i, s: (0, i, 0))],
            out_specs=pl.BlockSpec((tr, cc), lambda i, s: (s[1] * nb + i, 0))),
        out_shape=jax.ShapeDtypeStruct((2 * hr, cc), F32), compiler_params=_cp("parallel"),
    )(place, part, rcv)


def _sum_small(gathered, local, place):
    n = len(gathered)

    def body(s_ref, *refs):
        g_refs, l_refs, o_refs = refs[:n], refs[n:2 * n], refs[2 * n:]
        me = s_ref[2]
        for g_ref, l_ref, o_ref in zip(g_refs, l_refs, o_refs):
            acc = None
            for d in range(N_DEV):
                term = jnp.where(me == d, l_ref[...], g_ref[d])
                acc = term if acc is None else acc + term
            o_ref[...] = acc

    def whole(shape):
        return pl.BlockSpec(shape, lambda i, s, nd=len(shape): (0,) * nd)

    return _pallas(
        body, name="sum_small",
        grid_spec=pltpu.PrefetchScalarGridSpec(
            num_scalar_prefetch=1, grid=(1,),
            in_specs=[whole(g.shape) for g in gathered] + [whole(a.shape) for a in local],
            out_specs=tuple(whole(a.shape) for a in local)),
        out_shape=tuple(jax.ShapeDtypeStruct(a.shape, a.dtype) for a in local), compiler_params=_cp("arbitrary"),
    )(place, *gathered, *local)


def _adamw(name, w, g, m, v):
    r, cc = w.shape
    tr = r if r * cc <= 128 * 1024 else _div_tile(r, 256)

    def body(w_ref, g_ref, m_ref, v_ref, d_ref, mo_ref, vo_ref):
        gv = g_ref[...]
        mn = ADAM_B1 * m_ref[...] + (1.0 - ADAM_B1) * gv
        vn = ADAM_B2 * v_ref[...] + (1.0 - ADAM_B2) * (gv * gv)
        m_hat = mn / (1.0 - ADAM_B1 ** ADAM_STEP)
        v_hat = vn / (1.0 - ADAM_B2 ** ADAM_STEP)
        d_ref[...] = -ADAM_LR * (m_hat / (jnp.sqrt(v_hat) + ADAM_EPS) + ADAM_WD * w_ref[...])
        mo_ref[...] = mn
        vo_ref[...] = vn

    spec = pl.BlockSpec((tr, cc), lambda i: (i, 0))
    sd = jax.ShapeDtypeStruct((r, cc), F32)
    return _pallas(
        body, name=name, grid=(r // tr,), in_specs=[spec] * 4, out_specs=(spec,) * 3, out_shape=(sd,) * 3,
        compiler_params=_cp("parallel"),
    )(w, g, m, v)


_BIG = ("w_in", "w_up", "w_branch", "w_mem_kv", "w_out", "w_down")
_BIG_SHARD_SHAPE = {"w_in": (1024, 1664), "w_up": (1024, 1408), "w_branch": (1536, 256),
                    "w_mem_kv": (256, 1024), "w_out": (256, 1024), "w_down": (704, 1024)}
_SMALL_SHAPE = {"norm1_g": (1, D_MODEL), "ln_v_g": (1, GM_WIDTH), "ln_v_b": (1, GM_WIDTH),
                "w_spatial": (GM_GROUPS * GM_CHUNK, GM_CHUNK), "b_spatial": (GM_GROUPS, GM_CHUNK),
                "lb_logits": (2, HG_HEADS * HG_DIM), "hgrn_norm_g": (1, HG_DIM), "mem_norm_g": (1, D_MODEL),
                "norm2_g": (1, D_MODEL), "conv_w": (3, D_FF), "conv_b": (1, D_FF), "final_g": (1, D_MODEL)}
_SMALL_EARLY = tuple(n for n in _SMALL_SHAPE if n != "norm1_g")
_PARAM_ORDER = ("norm1_g", "w_in", "ln_v_g", "ln_v_b", "w_spatial", "b_spatial", "lb_logits", "hgrn_norm_g",
                "mem_norm_g", "w_mem_kv", "w_branch", "w_out", "norm2_g", "w_up", "conv_w", "conv_b", "w_down",
                "final_g")


def _adamw_small(ws, gs, ms, vs):
    n = len(ws)

    def body(*refs):
        w_refs, g_refs, m_refs, v_refs = refs[:n], refs[n:2 * n], refs[2 * n:3 * n], refs[3 * n:4 * n]
        d_refs, mo_refs, vo_refs = refs[4 * n:5 * n], refs[5 * n:6 * n], refs[6 * n:]
        for k in range(n):
            gv = g_refs[k][...]
            mn = ADAM_B1 * m_refs[k][...] + (1.0 - ADAM_B1) * gv
            vn = ADAM_B2 * v_refs[k][...] + (1.0 - ADAM_B2) * (gv * gv)
            m_hat = mn / (1.0 - ADAM_B1 ** ADAM_STEP)
            v_hat = vn / (1.0 - ADAM_B2 ** ADAM_STEP)
            d_refs[k][...] = -ADAM_LR * (m_hat / (jnp.sqrt(v_hat) + ADAM_EPS) + ADAM_WD * w_refs[k][...])
            mo_refs[k][...] = mn
            vo_refs[k][...] = vn

    specs = [pl.BlockSpec(a.shape, lambda i: (0, 0)) for a in ws]
    shapes = tuple(jax.ShapeDtypeStruct(a.shape, F32) for a in ws)
    res = _pallas(
        body, name="adamw_small", grid=(1,), in_specs=specs * 4, out_specs=tuple(specs * 3), out_shape=shapes * 3,
        compiler_params=_cp("arbitrary"),
    )(*ws, *gs, *ms, *vs)
    return res[:n], res[n:2 * n], res[2 * n:]


class _Comm:
    _ROW_SHARDED = ("w_mem_kv", "w_out", "w_down")

    def __init__(self, slabs, place):
        self.slabs, self.place = slabs, place
        self.full, self.raw, self.parts, self.landing, self.bufs, self.done = {}, {}, {}, {}, {}, {}
        ex, deliver = self._gather(["w_in"])
        deliver(_run_exchanges("all_gather_w_in", [ex])[0])

    def w(self, name):
        a = self.full[name]
        if name in self._ROW_SHARDED:
            return a.reshape(-1, a.shape[-1])
        if name == "conv_w":
            return jnp.transpose(a, (1, 0, 2)).reshape(3, 1, D_FF)
        return a

    def grad(self, name, arr):
        self.raw[name] = arr.reshape((N_CHIPS, -1, arr.shape[-1]))
        if name == "w_in":
            ex, deliver = self._to_sibling(["w_in"])
            deliver(_run_exchanges("rs_sibling_w_in", [ex])[0])

    def small_grads(self, arrays):
        self.small_local = list(arrays)

    def carry(self, tag, call):
        plan = self._plan(tag)
        if not plan:
            return call(())
        out, carried = call([ex for ex, _ in plan])
        for (_, deliver), res in zip(plan, carried):
            deliver(res)
        return out

    def finish(self, last_small):
        ex, deliver = self._share(["w_out", "w_branch", "w_mem_kv", "w_in"])
        shared, small = _run_exchanges("share_and_gather_last", [ex, _ex_gather_small(last_small)])
        deliver(shared)
        return self.done, self.small_local + list(last_small), self.small_everyone + small

    def _plan(self, tag):
        if tag == "in_proj":
            return [self._gather(["w_branch", "w_out", "w_mem_kv", "w_down", "conv_w"])]
        if tag == "hgrn_fwd":
            return [self._gather(["w_up"])]
        if tag == "d_h2":
            return [self._to_sibling(["w_down", "w_up"])]
        if tag == "hgrn_bwd":
            return [self._to_owner(["w_down", "w_up"]), self._to_sibling(["w_out", "w_branch", "w_mem_kv"])]
        if tag == "g_w_in":
            def keep(res):
                self.small_everyone = res

            return [self._to_owner(["w_out", "w_branch", "w_mem_kv"]), self._share(["w_down", "w_up"]),
                    (_ex_gather_small(self.small_local), keep)]
        if tag == "d_h":
            return [self._to_owner(["w_in"])]
        return []

    def _gather(self, names, part=(0, 1)):
        def deliver(res):
            self.slabs.update(zip(names, res))
            self.full.update(zip(names, res))

        return _ex_all_gather([self.slabs[n] for n in names], [n != "conv_w" for n in names], part), deliver

    def _to_sibling(self, names):
        def deliver(res):
            for n, r in zip(names, res):
                self.parts[n] = _add_half("rs_add_" + n, self.raw[n], r, self.place)

        return _ex_to_sibling([self.raw[n] for n in names]), deliver

    def _to_owner(self, names, part=(0, 1)):
        def deliver(res):
            for n, r in zip(names, res):
                if part[0] + 1 < part[1]:
                    self.landing[n] = r
                else:
                    self.bufs[n] = _sum_owner("rs_sum_" + n, self.parts[n], r, self.place)

        landing = [self.landing[n] for n in names] if part[0] else None
        return _ex_to_owner([self.parts[n] for n in names], part, landing), deliver

    def _share(self, names):
        return _ex_share_halves([self.bufs[n] for n in names]), lambda res: self.done.update(zip(names, res))


def kernel(x, mem, norm1_g, w_in, ln_v_g, ln_v_b, w_spatial, b_spatial, lb_logits, hgrn_norm_g, mem_norm_g, w_mem_kv, w_branch, w_out, norm2_g, w_up, conv_w, conv_b, w_down, final_g, loss_target, m_norm1_g, m_w_in, m_ln_v_g, m_ln_v_b, m_w_spatial, m_b_spatial, m_lb_logits, m_hgrn_norm_g, m_mem_norm_g, m_w_mem_kv, m_w_branch, m_w_out, m_norm2_g, m_w_up, m_conv_w, m_conv_b, m_w_down, m_final_g, v_norm1_g, v_w_in, v_ln_v_g, v_ln_v_b, v_w_spatial, v_b_spatial, v_lb_logits, v_hgrn_norm_g, v_mem_norm_g, v_w_mem_kv, v_w_branch, v_w_out, v_norm2_g, v_w_up, v_conv_w, v_conv_b, v_w_down, v_final_g):
    w = dict(norm1_g=norm1_g, w_in=w_in, ln_v_g=ln_v_g, ln_v_b=ln_v_b, w_spatial=w_spatial, b_spatial=b_spatial,
             lb_logits=lb_logits, hgrn_norm_g=hgrn_norm_g, mem_norm_g=mem_norm_g, w_mem_kv=w_mem_kv,
             w_branch=w_branch, w_out=w_out, norm2_g=norm2_g, w_up=w_up, conv_w=conv_w, conv_b=conv_b,
             w_down=w_down, final_g=final_g)
    mom = dict(norm1_g=m_norm1_g, w_in=m_w_in, ln_v_g=m_ln_v_g, ln_v_b=m_ln_v_b, w_spatial=m_w_spatial,
               b_spatial=m_b_spatial, lb_logits=m_lb_logits, hgrn_norm_g=m_hgrn_norm_g, mem_norm_g=m_mem_norm_g,
               w_mem_kv=m_w_mem_kv, w_branch=m_w_branch, w_out=m_w_out, norm2_g=m_norm2_g, w_up=m_w_up,
               conv_w=m_conv_w, conv_b=m_conv_b, w_down=m_w_down, final_g=m_final_g)
    var = dict(norm1_g=v_norm1_g, w_in=v_w_in, ln_v_g=v_ln_v_g, ln_v_b=v_ln_v_b, w_spatial=v_w_spatial,
               b_spatial=v_b_spatial, lb_logits=v_lb_logits, hgrn_norm_g=v_hgrn_norm_g, mem_norm_g=v_mem_norm_g,
               w_mem_kv=v_w_mem_kv, w_branch=v_w_branch, w_out=v_w_out, norm2_g=v_norm2_g, w_up=v_w_up,
               conv_w=v_conv_w, conv_b=v_conv_b, w_down=v_w_down, final_g=v_final_g)
    B, S, D = x.shape
    T = B * S
    ci = lax.axis_index("c")
    q = 2 * lax.axis_index("x") + lax.axis_index("y")
    place = jnp.stack([q, ci, 2 * q + ci]).astype(jnp.int32)

    slabs = {n: _cast_into_slab("slab_" + n, w[n].reshape(_BIG_SHARD_SHAPE[n]), place, BF16) for n in _BIG}
    slabs["conv_w"] = _cast_into_slab("slab_conv_w", conv_w[0], place, F32)
    comm = _Comm(slabs, place)
    p = dict(
        norm1_g=norm1_g, ln_v_g=ln_v_g, ln_v_b=ln_v_b, w_spatial=w_spatial[0],
        b_spatial=b_spatial.reshape(GM_GROUPS, GM_CHUNK, 1), lb_logits=lb_logits, hgrn_norm_g=hgrn_norm_g,
        mem_norm_g=mem_norm_g, norm2_g=norm2_g, conv_b=conv_b, final_g=final_g.reshape(1, D))

    loss, grad_x, g = _local_step(x.reshape(T, D), mem.reshape(B * MEM_LEN, D), loss_target.reshape(T, D), p, comm,
                                  B, S)

    shard_grads, local_small, everyone = comm.finish([g["norm1_g"]])
    summed = _sum_small(everyone, local_small, place)
    small_names = list(_SMALL_EARLY) + ["norm1_g"]
    total = dict(zip(_SMALL_EARLY, summed))
    loss_total, total["norm1_g"] = summed[len(_SMALL_EARLY)][0, 0], summed[-1]

    grads, delta, new_m, new_v = {}, {}, {}, {}
    for n in _BIG:
        shp = _BIG_SHARD_SHAPE[n]
        grads[n] = shard_grads[n]
        delta[n], new_m[n], new_v[n] = _adamw("adamw_" + n, w[n].reshape(shp), shard_grads[n],
                                              mom[n].reshape(shp), var[n].reshape(shp))
    cw_shard = D_FF // N_CHIPS
    total["conv_w"] = lax.dynamic_slice(total["conv_w"], (0, q * cw_shard), (3, cw_shard))

    def flat2d(d, n):
        return d[n].reshape(total[n].shape)

    upd = _adamw_small([flat2d(w, n) for n in small_names], [total[n] for n in small_names],
                       [flat2d(mom, n) for n in small_names], [flat2d(var, n) for n in small_names])
    for k, n in enumerate(small_names):
        grads[n], delta[n], new_m[n], new_v[n] = total[n], upd[0][k], upd[1][k], upd[2][k]

    def shaped(d):
        return [d[n].reshape(w[n].shape) for n in _PARAM_ORDER]

    return (loss_total, grad_x.reshape(B, S, D), *shaped(grads), *shaped(delta), *shaped(new_m), *shaped(new_v))
```

```python
import functools
import math

import jax
import jax.numpy as jnp
from jax import lax
from jax.experimental import pallas as pl
from jax.experimental.pallas import tpu as pltpu

F32 = jnp.float32
BF16 = jnp.bfloat16
EPS = 1e-6

D_MODEL = 1024
MEM_LEN = 256
GM_WIDTH = 512
GM_CHUNK = 128
GM_GROUPS = 4
HG_HEADS = 4
HG_DIM = 128
HG_CHUNK = 64
XA_HEADS = 4
XA_DIM = 128
BR_WIDTH = 512
D_FF = 2816
IN_WIDTH = 6656
N_CHIPS = 4
N_DEV = 8

ADAM_LR = 0.001
ADAM_B1 = 0.9
ADAM_B2 = 0.999
ADAM_EPS = 1e-08
ADAM_WD = 0.01
ADAM_STEP = 10

COL_ZU, COL_ZV, COL_HQ, COL_HF, COL_HI, COL_HG, COL_XQ = 0, 1, 2, 3, 4, 5, 6
COL_GATE0 = 3584

VMEM_LIMIT_BYTES = 48 * 1024 * 1024
MESH_ID = pl.DeviceIdType.MESH


def _cp(*sem):
    return pltpu.CompilerParams(dimension_semantics=sem, vmem_limit_bytes=VMEM_LIMIT_BYTES)


def _pallas(body, *, out_shape, **kw):
    def pin(s):
        return pltpu.HBM(s.shape, s.dtype) if isinstance(s, jax.ShapeDtypeStruct) else s

    out_shape = tuple(pin(s) for s in out_shape) if isinstance(out_shape, (tuple, list)) else pin(out_shape)
    call = pl.pallas_call(body, out_shape=out_shape, **kw)

    def run(*operands):
        return call(*[pltpu.with_memory_space_constraint(o, pltpu.HBM) if jnp.issubdtype(o.dtype, jnp.floating)
                      else o for o in operands])

    return run


def _dot(a, b):
    return lax.dot_general(a.astype(BF16), b.astype(BF16), (((1,), (0,)), ((), ())), preferred_element_type=F32)


def _dot_nt(a, b):
    return lax.dot_general(a.astype(BF16), b.astype(BF16), (((1,), (1,)), ((), ())), preferred_element_type=F32)


def _dot_tn(a, b):
    return lax.dot_general(a.astype(BF16), b.astype(BF16), (((0,), (0,)), ((), ())), preferred_element_type=F32)


def _dot_01(mask01, x):
    hi = x.astype(BF16)
    r1 = x - hi.astype(F32)
    mid = r1.astype(BF16)
    lo = (r1 - mid.astype(F32)).astype(BF16)
    m = mask01.astype(BF16)
    dn = (((1,), (0,)), ((), ()))
    return (lax.dot_general(m, hi, dn, preferred_element_type=F32)
            + lax.dot_general(m, mid, dn, preferred_element_type=F32)
            + lax.dot_general(m, lo, dn, preferred_element_type=F32))


def _sigmoid(z):
    return 1.0 / (1.0 + jnp.exp(-z))


_GELU_C = math.sqrt(2.0 / math.pi)


def _gelu_and_grad(z):
    inner = _GELU_C * (z + 0.044715 * z * z * z)
    t = jnp.tanh(inner)
    val = 0.5 * z * (1.0 + t)
    grad = 0.5 * (1.0 + t) + 0.5 * z * (1.0 - t * t) * _GELU_C * (1.0 + 3.0 * 0.044715 * z * z)
    return val, grad


def _row_tile(n, want=512):
    t = min(want, n)
    assert n % t == 0
    return t


def _pcall(body, operands, *, name, grid, in_specs, out_specs, out_shape, scratch_shapes=(), semantics, riders=()):
    single = not isinstance(out_shape, (tuple, list))
    out_specs = (out_specs,) if single else tuple(out_specs)
    out_shape = (out_shape,) if single else tuple(out_shape)
    if not riders:
        res = _pallas(body, name=name, grid=grid, in_specs=list(in_specs), out_specs=out_specs,
                      out_shape=out_shape, scratch_shapes=list(scratch_shapes),
                      compiler_params=_cp(*semantics))(*operands)
        return (res[0] if single else res), []
    n_in, n_out, n_scr = len(in_specs), len(out_shape), len(scratch_shapes)
    ex_in = [len(ex.operands) for ex in riders]
    ex_out = [len(ex.out_shape) for ex in riders]
    ex_scr = [len(ex.scratch) for ex in riders]
    tot_in, tot_out = n_in + sum(ex_in), n_out + sum(ex_out)

    def wrapped(*refs):
        ins, outs, scr = refs[:tot_in], refs[tot_in:tot_in + tot_out], refs[tot_in + tot_out:]
        ids = [pl.program_id(d) for d in range(len(grid))]
        first = functools.reduce(lambda p, t: p & t, [i == 0 for i in ids])
        last = functools.reduce(lambda p, t: p & t, [i == n - 1 for i, n in zip(ids, grid)])
        parts, oi, oo, os_ = [], n_in, n_out, n_scr
        for k in range(len(riders)):
            parts.append((ins[oi:oi + ex_in[k]], outs[oo:oo + ex_out[k]], scr[os_:os_ + ex_scr[k]]))
            oi, oo, os_ = oi + ex_in[k], oo + ex_out[k], os_ + ex_scr[k]

        @pl.when(first)
        def _():
            for ex, part in zip(riders, parts):
                ex.start(*part)

        body(*ins[:n_in], *outs[:n_out], *scr[:n_scr])

        @pl.when(last)
        def _():
            for ex, part in zip(riders, parts):
                ex.finish(*part)

    aliases, oi, oo = {}, n_in, n_out
    all_ops, all_shapes, all_scr = list(operands), list(out_shape), list(scratch_shapes)
    for k, ex in enumerate(riders):
        aliases.update({oi + a: oo + b for a, b in ex.aliases.items()})
        oi, oo = oi + ex_in[k], oo + ex_out[k]
        all_ops += list(ex.operands)
        all_shapes += [pltpu.HBM(s.shape, s.dtype) for s in ex.out_shape]
        all_scr += list(ex.scratch)
    res = _pallas(
        wrapped, name=name, grid=grid, in_specs=list(in_specs) + [HBM_SPEC] * sum(ex_in),
        out_specs=out_specs + (HBM_SPEC,) * sum(ex_out), out_shape=tuple(all_shapes), scratch_shapes=all_scr,
        input_output_aliases=aliases, compiler_params=_cp(*(["arbitrary"] * len(grid))))(*all_ops)
    own = res[0] if single else tuple(res[:n_out])
    carried, oo = [], n_out
    for k in range(len(riders)):
        carried.append(list(res[oo:oo + ex_out[k]]))
        oo += ex_out[k]
    return own, carried


def _carried(out, carried, riders):
    return (out, carried) if riders else out


def _matmul(name, operands, *, grid, in_specs, o_spec, out_shape, out_dtype, dims, riders=()):
    nk = grid[2]
    assert nk == 1 or out_dtype == F32

    def body(a_ref, b_ref, o_ref):
        part = lax.dot_general(a_ref[...].astype(BF16), b_ref[...].astype(BF16), (dims, ((), ())),
                               preferred_element_type=F32)
        if nk == 1:
            o_ref[...] = part.astype(o_ref.dtype)
        else:
            k = pl.program_id(2)

            @pl.when(k == 0)
            def _():
                o_ref[...] = part

            @pl.when(k > 0)
            def _():
                o_ref[...] += part

    out, carried = _pcall(body, operands, name=name, grid=grid, in_specs=in_specs, out_specs=o_spec,
                          out_shape=jax.ShapeDtypeStruct(out_shape, out_dtype),
                          semantics=("parallel", "parallel", "arbitrary"), riders=riders)
    return (out, carried) if riders else out


NN = ((1,), (0,))
NT = ((1,), (1,))
TN = ((0,), (0,))
_TN_TOKENS = 4096


def _mm_cs(name, a, w, out_dtype, riders=()):
    M, K = a.shape
    nq, _, wd = w.shape
    tm = _row_tile(M)
    return _matmul(name, (a, w), grid=(nq, M // tm, 1),
                   in_specs=[pl.BlockSpec((tm, K), lambda j, i, k: (i, 0)),
                             pl.BlockSpec((None, K, wd), lambda j, i, k: (j, 0, 0))],
                   o_spec=pl.BlockSpec((tm, wd), lambda j, i, k: (i, j)),
                   out_shape=(M, nq * wd), out_dtype=out_dtype, dims=NN, riders=riders)


def _mm_rs(name, a, w, out_dtype):
    M, K = a.shape
    N = w.shape[1]
    tm = _row_tile(M)
    return _matmul(name, (a, w), grid=(M // tm, 1, 1),
                   in_specs=[pl.BlockSpec((tm, K), lambda i, j, k: (i, 0)), pl.BlockSpec((K, N), lambda i, j, k: (0, 0))],
                   o_spec=pl.BlockSpec((tm, N), lambda i, j, k: (i, 0)),
                   out_shape=(M, N), out_dtype=out_dtype, dims=NN)


def _mm_nt_rs(name, g, w, out_dtype, riders=()):
    M, N = g.shape
    K = w.shape[0]
    to = K
    tm = _row_tile(M)
    return _matmul(name, (g, w), grid=(M // tm, K // to, 1),
                   in_specs=[pl.BlockSpec((tm, N), lambda i, j, k: (i, 0)),
                             pl.BlockSpec((to, N), lambda i, j, k: (j, 0))],
                   o_spec=pl.BlockSpec((tm, to), lambda i, j, k: (i, j)),
                   out_shape=(M, K), out_dtype=out_dtype, dims=NT, riders=riders)


def _mm_nt_cs(name, g, w, out_dtype, riders=(), stacked=False, norm_bwd=None):
    M = g.shape[-2]
    nq, K, wd = w.shape
    tm = _row_tile(M, 256)

    def product(g_ref, w_ref):
        acc = None
        for q in range(nq):
            gq = g_ref[q // 2, :, (q % 2) * wd:(q % 2 + 1) * wd] if stacked else g_ref[:, q * wd:(q + 1) * wd]
            part = _dot_nt(gq, w_ref[q])
            acc = part if acc is None else acc + part
        return acc

    def body(g_ref, w_ref, o_ref):
        o_ref[...] = product(g_ref, w_ref).astype(o_ref.dtype)

    def body_norm(g_ref, w_ref, x_ref, gain_ref, dr_ref, dx_ref, dg_ref):
        @pl.when(pl.program_id(0) == 0)
        def _():
            dg_ref[...] = jnp.zeros_like(dg_ref)

        dx, dg = _rms_bwd_rows(x_ref[...], gain_ref[...], product(g_ref, w_ref))
        dg_ref[...] += dg
        dx_ref[...] = dx + dr_ref[...]

    g_spec = (pl.BlockSpec((2, tm, 2 * wd), lambda i: (0, i, 0)) if stacked
              else pl.BlockSpec((tm, nq * wd), lambda i: (i, 0)))
    w_spec = pl.BlockSpec((nq, K, wd), lambda i: (0, 0, 0))
    row = pl.BlockSpec((tm, K), lambda i: (i, 0))
    if norm_bwd is None:
        return _carried(*_pcall(
            body, (g, w), name=name, grid=(M // tm,), in_specs=[g_spec, w_spec], out_specs=row,
            out_shape=jax.ShapeDtypeStruct((M, K), out_dtype), semantics=("parallel",), riders=riders), riders)
    vec = pl.BlockSpec((1, K), lambda i: (0, 0))
    return _carried(*_pcall(
        body_norm, (g, w) + tuple(norm_bwd), name=name, grid=(M // tm,),
        in_specs=[g_spec, w_spec, row, vec, row], out_specs=(row, vec),
        out_shape=(jax.ShapeDtypeStruct((M, K), F32), jax.ShapeDtypeStruct((1, K), F32)),
        semantics=("arbitrary",), riders=riders), riders)


def _mm_tn_rs(name, a, g, to, tn=512):
    T, M = a.shape
    N = g.shape[1]
    tt = _row_tile(T, _TN_TOKENS)
    tn = min(tn, N)
    return _matmul(name, (a, g), grid=(M // to, N // tn, T // tt),
                   in_specs=[pl.BlockSpec((tt, to), lambda i, j, k: (k, i)),
                             pl.BlockSpec((tt, tn), lambda i, j, k: (k, j))],
                   o_spec=pl.BlockSpec((to, tn), lambda i, j, k: (i, j)),
                   out_shape=(M, N), out_dtype=F32, dims=TN)


def _mm_tn_cs(name, a, g, nq, to, riders=(), stacked=False):
    T, M = a.shape
    wd = g.shape[-1] * (2 if stacked else 1) // nq
    tt = _row_tile(T, _TN_TOKENS)
    g_spec = (pl.BlockSpec((None, tt, wd), lambda i, j, k: (j // 2, k, j % 2)) if stacked
              else pl.BlockSpec((tt, wd), lambda i, j, k: (k, j)))
    return _matmul(name, (a, g), grid=(M // to, nq, T // tt),
                   in_specs=[pl.BlockSpec((tt, to), lambda i, j, k: (k, i)), g_spec],
                   o_spec=pl.BlockSpec((None, to, wd), lambda i, j, k: (j, i, 0)),
                   out_shape=(nq, M, wd), out_dtype=F32, dims=TN, riders=riders)


def _rms_fwd(name, x, g, riders=()):
    T, D = x.shape
    tm = _row_tile(T)

    def body(x_ref, g_ref, o_ref):
        o_ref[...] = _rms_rows(x_ref[...], g_ref[...]).astype(o_ref.dtype)

    return _carried(*_pcall(
        body, (x, g), name=name, grid=(T // tm,),
        in_specs=[pl.BlockSpec((tm, D), lambda i: (i, 0)), pl.BlockSpec((1, D), lambda i: (0, 0))],
        out_specs=pl.BlockSpec((tm, D), lambda i: (i, 0)),
        out_shape=jax.ShapeDtypeStruct((T, D), BF16), semantics=("parallel",), riders=riders), riders)


def _rms_rows(xv, gain):
    return xv * lax.rsqrt(jnp.mean(xv * xv, axis=-1, keepdims=True) + EPS) * gain


def _rms_bwd_rows(xv, gain, dh):
    r = lax.rsqrt(jnp.mean(xv * xv, axis=-1, keepdims=True) + EPS)
    n = xv * r
    dn = dh * gain
    return r * (dn - n * jnp.mean(dn * n, axis=-1, keepdims=True)), jnp.sum(dh * n, axis=0, keepdims=True)


def _rms_bwd(name, x, g, dh, dres):
    T, D = x.shape
    tm = _row_tile(T)
    has_res = dres is not None

    def body(*refs):
        if has_res:
            x_ref, g_ref, dh_ref, dr_ref, dx_ref, dg_ref = refs
        else:
            x_ref, g_ref, dh_ref, dx_ref, dg_ref = refs

        @pl.when(pl.program_id(0) == 0)
        def _():
            dg_ref[...] = jnp.zeros_like(dg_ref)

        dx, dg = _rms_bwd_rows(x_ref[...], g_ref[...], dh_ref[...])
        dg_ref[...] += dg
        if has_res:
            dx = dx + dr_ref[...]
        dx_ref[...] = dx

    row = pl.BlockSpec((tm, D), lambda i: (i, 0))
    vec = pl.BlockSpec((1, D), lambda i: (0, 0))
    ops = (x, g, dh, dres) if has_res else (x, g, dh)
    return _pallas(
        body, name=name, grid=(T // tm,), in_specs=[row, vec, row] + ([row] if has_res else []),
        out_specs=(row, vec),
        out_shape=(jax.ShapeDtypeStruct((T, D), F32), jax.ShapeDtypeStruct((1, D), F32)),
        compiler_params=_cp("arbitrary"),
    )(*ops)


def _proj_res_norm(name, a, w, res, gain):
    M, K = a.shape
    N = w.shape[1]
    tm = _row_tile(M)

    def body(a_ref, w_ref, r_ref, g_ref, x_ref, h_ref):
        xv = _dot(a_ref[...], w_ref[...]) + r_ref[...]
        x_ref[...] = xv
        h_ref[...] = _rms_rows(xv, g_ref[...]).astype(h_ref.dtype)

    row = pl.BlockSpec((tm, N), lambda i: (i, 0))
    return _pallas(
        body, name=name, grid=(M // tm,),
        in_specs=[pl.BlockSpec((tm, K), lambda i: (i, 0)), pl.BlockSpec((K, N), lambda i: (0, 0)), row,
                  pl.BlockSpec((1, N), lambda i: (0, 0))],
        out_specs=(row, row), out_shape=(jax.ShapeDtypeStruct((M, N), F32), jax.ShapeDtypeStruct((M, N), BF16)),
        compiler_params=_cp("parallel"),
    )(a, w, res, gain)


def _proj_res_loss(name, a, w, res, tgt, gain):
    M, K = a.shape
    D = w.shape[1]
    tm = _row_tile(M)

    def body(a_ref, w_ref, r_ref, t_ref, g_ref, dx_ref, dg_ref, loss_ref):
        @pl.when(pl.program_id(0) == 0)
        def _():
            dg_ref[...] = jnp.zeros_like(dg_ref)
            loss_ref[...] = jnp.zeros_like(loss_ref)

        xv = _dot(a_ref[...], w_ref[...]) + r_ref[...]
        gv = g_ref[...]
        diff = _rms_rows(xv, gv) - t_ref[...]
        loss_ref[...] += 0.5 * jnp.sum(jnp.mean(diff * diff, axis=-1, keepdims=True))
        dx, dg = _rms_bwd_rows(xv, gv, diff * (1.0 / D))
        dg_ref[...] += dg
        dx_ref[...] = dx

    row = pl.BlockSpec((tm, D), lambda i: (i, 0))
    vec = pl.BlockSpec((1, D), lambda i: (0, 0))
    return _pallas(
        body, name=name, grid=(M // tm,),
        in_specs=[pl.BlockSpec((tm, K), lambda i: (i, 0)), pl.BlockSpec((K, D), lambda i: (0, 0)), row, row, vec],
        out_specs=(row, vec, pl.BlockSpec((8, 128), lambda i: (0, 0))),
        out_shape=(jax.ShapeDtypeStruct((M, D), F32), jax.ShapeDtypeStruct((1, D), F32),
                   jax.ShapeDtypeStruct((8, 128), F32)),
        compiler_params=_cp("arbitrary"),
    )(a, w, res, tgt, gain)


def _gmlp_pieces(zu, zv, lng, lnb, ws_ref, bs_ref):
    u, du = _gelu_and_grad(zu)
    v, dv = _gelu_and_grad(zv)
    mu = jnp.mean(v, axis=-1, keepdims=True)
    vc = v - mu
    rstd = lax.rsqrt(jnp.mean(vc * vc, axis=-1, keepdims=True) + EPS)
    vhat = vc * rstd
    vn = vhat * lng + lnb
    row = lax.broadcasted_iota(jnp.int32, (GM_CHUNK, GM_CHUNK), 0)
    col = lax.broadcasted_iota(jnp.int32, (GM_CHUNK, GM_CHUNK), 1)
    tril = row >= col
    wms, mixed = [], []
    for g in range(GM_GROUPS):
        sl = slice(g * 128, (g + 1) * 128)
        wm = jnp.where(tril, ws_ref[g], 0.0)
        wms.append(wm)
        mixed.append(_dot(wm, vn[:, sl]) + bs_ref[g])
    return u, du, dv, rstd, vhat, vn, wms, mixed, tril


def _gmlp_fwd(proj, lng, lnb, ws, bs_col):
    T = proj.shape[0]
    n = T // GM_CHUNK

    def body(zu_ref, zv_ref, lng_ref, lnb_ref, ws_ref, bs_ref, o_ref):
        u, _, _, _, _, _, _, mixed, _ = _gmlp_pieces(zu_ref[...].astype(F32), zv_ref[...].astype(F32),
                                                     lng_ref[...], lnb_ref[...],
                                                     ws_ref, bs_ref)
        for g in range(GM_GROUPS):
            sl = slice(g * 128, (g + 1) * 128)
            o_ref[:, sl] = (u[:, sl] * mixed[g]).astype(o_ref.dtype)

    vec = pl.BlockSpec((1, GM_WIDTH), lambda i: (0, 0))
    return _pallas(
        body, name="gmlp_fwd", grid=(n,),
        in_specs=[pl.BlockSpec((GM_CHUNK, 512), lambda i: (i, COL_ZU)),
                  pl.BlockSpec((GM_CHUNK, 512), lambda i: (i, COL_ZV)),
                  vec, vec,
                  pl.BlockSpec((GM_GROUPS, 128, 128), lambda i: (0, 0, 0)),
                  pl.BlockSpec((GM_GROUPS, 128, 1), lambda i: (0, 0, 0))],
        out_specs=pl.BlockSpec((GM_CHUNK, 512), lambda i: (i, 0)),
        out_shape=jax.ShapeDtypeStruct((T, GM_WIDTH), BF16), compiler_params=_cp("parallel"),
    )(proj, proj, lng, lnb, ws, bs_col)


def _gmlp_bwd(proj, d_out, lng, lnb, ws, bs_col, riders=()):
    T = proj.shape[0]
    n = T // GM_CHUNK

    def body(zu_ref, zv_ref, do_ref, lng_ref, lnb_ref, ws_ref, bs_ref,
             dz_ref, dws_ref, dbs_ref, dlng_ref, dlnb_ref, dm_acc):
        i = pl.program_id(0)

        @pl.when(i == 0)
        def _():
            dws_ref[...] = jnp.zeros_like(dws_ref)
            dlng_ref[...] = jnp.zeros_like(dlng_ref)
            dlnb_ref[...] = jnp.zeros_like(dlnb_ref)
            dm_acc[...] = jnp.zeros_like(dm_acc)

        lng_v = lng_ref[...]
        u, du, dv, rstd, vhat, vn, wms, mixed, tril = _gmlp_pieces(zu_ref[...].astype(F32), zv_ref[...].astype(F32),
                                                                  lng_v, lnb_ref[...],
                                                                  ws_ref, bs_ref)
        do = do_ref[...]
        dvn_parts = []
        for g in range(GM_GROUPS):
            sl = slice(g * 128, (g + 1) * 128)
            dog = do[:, sl]
            dz_ref[:, sl] = (dog * mixed[g] * du[:, sl]).astype(dz_ref.dtype)
            dmix = dog * u[:, sl]
            dm_acc[:, sl] += dmix
            dws_ref[g] += jnp.where(tril, _dot_nt(dmix, vn[:, sl]), 0.0)
            dvn_parts.append(_dot_tn(wms[g], dmix))
        dvn = jnp.concatenate(dvn_parts, axis=1)
        dlng_ref[...] += jnp.sum(dvn * vhat, axis=0, keepdims=True)
        dlnb_ref[...] += jnp.sum(dvn, axis=0, keepdims=True)
        dvh = dvn * lng_v
        dvv = rstd * (dvh - jnp.mean(dvh, axis=-1, keepdims=True)
                      - vhat * jnp.mean(dvh * vhat, axis=-1, keepdims=True))
        dz_ref[:, GM_WIDTH:] = (dvv * dv).astype(dz_ref.dtype)

        @pl.when(i == n - 1)
        def _():
            for g in range(GM_GROUPS):
                dbs_ref[g] = jnp.sum(dm_acc[:, g * 128:(g + 1) * 128], axis=1, keepdims=True)

    vec = pl.BlockSpec((1, GM_WIDTH), lambda i: (0, 0))
    wsp = pl.BlockSpec((GM_GROUPS, 128, 128), lambda i: (0, 0, 0))
    bsp = pl.BlockSpec((GM_GROUPS, 128, 1), lambda i: (0, 0, 0))
    return _carried(*_pcall(
        body, (proj, proj, d_out, lng, lnb, ws, bs_col), name="gmlp_bwd", grid=(n,),
        in_specs=[pl.BlockSpec((GM_CHUNK, 512), lambda i: (i, COL_ZU)),
                  pl.BlockSpec((GM_CHUNK, 512), lambda i: (i, COL_ZV)),
                  pl.BlockSpec((None, GM_CHUNK, 512), lambda i: (0, i, 0)), vec, vec, wsp, bsp],
        out_specs=(pl.BlockSpec((GM_CHUNK, 2 * GM_WIDTH), lambda i: (i, 0)), wsp, bsp, vec, vec),
        out_shape=(jax.ShapeDtypeStruct((T, 2 * GM_WIDTH), BF16),
                   jax.ShapeDtypeStruct((GM_GROUPS, 128, 128), F32), jax.ShapeDtypeStruct((GM_GROUPS, 128, 1), F32),
                   jax.ShapeDtypeStruct((1, GM_WIDTH), F32), jax.ShapeDtypeStruct((1, GM_WIDTH), F32)),
        scratch_shapes=[pltpu.VMEM((GM_CHUNK, GM_WIDTH), F32)],
        semantics=("arbitrary",), riders=riders), riders)


def _hgrn_lower_bound(lbl):
    return 1.0 / (1.0 + jnp.exp(lbl[1:2, :] - lbl[0:1, :]))


def _hgrn_gates(hq, hf, lb):
    C = HG_CHUNK
    sg = _sigmoid(hf)
    fg = lb + (1.0 - lb) * sg
    sq = _sigmoid(hq)
    row = lax.broadcasted_iota(jnp.int32, (C, C), 0)
    col = lax.broadcasted_iota(jnp.int32, (C, C), 1)
    tril = row >= col
    logf = jnp.log(fg)
    a = _dot_01(tril, logf)
    a_last = jnp.sum(logf, axis=0, keepdims=True)
    first_half = lax.broadcasted_iota(jnp.int32, logf.shape, 0) < (C // 2)
    a_mid = jnp.sum(jnp.where(first_half, logf, 0.0), axis=0, keepdims=True)
    ea, ei, eki, ekl = jnp.exp(a), jnp.exp(a - a_mid), jnp.exp(a_mid - a), jnp.exp(a_last - a)
    k = 1.0 - fg
    q = hq * sq
    qi = (q * ei).astype(BF16).astype(F32)
    ki = (k * eki).astype(BF16).astype(F32)
    return dict(sg=sg, fg=fg, sq=sq, tril=tril, ea=ea, ei=ei, eki=eki, ekl=ekl, e_last=jnp.exp(a_last),
                qe=q * ea, qi=qi, ki=ki, kl=k * ekl)


def _heads(x):
    return [x[:, h * HG_DIM:(h + 1) * HG_DIM] for h in range(HG_HEADS)]


def _hgrn_fwd(proj, lbl, gh, B, S, riders=()):
    C = HG_CHUNK
    NC = S // C
    W = HG_HEADS * HG_DIM

    def body(q_ref, f_ref, i_ref, g_ref, lbl_ref, gh_ref, o_ref, bo_ref, st_ref, state):
        @pl.when(pl.program_id(0) == 0)
        def _():
            state[...] = jnp.zeros_like(state)

        lb = _hgrn_lower_bound(lbl_ref[...])
        ghv = gh_ref[...]
        for b in range(B):
            gt = _hgrn_gates(q_ref[b].astype(F32), f_ref[b].astype(F32), lb)
            v = _heads(i_ref[b])
            qe, qi, ki, kl, e_last = (_heads(gt[n]) for n in ("qe", "qi", "ki", "kl", "e_last"))
            outs, normed = [], []
            for h in range(HG_HEADS):
                p = jnp.where(gt["tril"], _dot_nt(qi[h], ki[h]), 0.0)
                st = state[b, h]
                st_ref[b, h] = st
                o = _dot_nt(qe[h], st) + _dot(p, v[h])
                state[b, h] = st * e_last[h] + _dot_tn(v[h], kl[h])
                outs.append(o)
                normed.append(o * lax.rsqrt(jnp.mean(o * o, axis=-1, keepdims=True) + EPS) * ghv)
            o_ref[b] = jnp.concatenate(outs, axis=1)
            hg = g_ref[b].astype(F32)
            bo_ref[b] = (jnp.concatenate(normed, axis=1) * (hg * _sigmoid(hg))).astype(bo_ref.dtype)

    def col(cb):
        return pl.BlockSpec((B, C, 512), lambda c: (0, c, cb))

    tile = pl.BlockSpec((B, C, W), lambda c: (0, c, 0))
    proj3 = proj.reshape(B, S, proj.shape[-1])
    out, carried = _pcall(
        body, (proj3, proj3, proj3, proj3, lbl, gh), name="hgrn_fwd", grid=(NC,),
        in_specs=[col(COL_HQ), col(COL_HF), col(COL_HI), col(COL_HG),
                  pl.BlockSpec((2, W), lambda c: (0, 0)), pl.BlockSpec((1, HG_DIM), lambda c: (0, 0))],
        out_specs=(tile, tile, pl.BlockSpec((B, None, HG_HEADS, 128, 128), lambda c: (0, c, 0, 0, 0))),
        out_shape=(jax.ShapeDtypeStruct((B, S, W), F32), jax.ShapeDtypeStruct((B, S, W), BF16),
                   jax.ShapeDtypeStruct((B, NC, HG_HEADS, 128, 128), F32)),
        scratch_shapes=[pltpu.VMEM((B, HG_HEADS, 128, 128), F32)],
        semantics=("arbitrary",), riders=riders)
    o_h, b_out, states = out
    out = (o_h, b_out.reshape(B * S, W), states)
    return (out, carried) if riders else out


def _hgrn_bwd(proj, o_saved, states, d_out, lbl, gh, others, B, S, riders=()):
    C = HG_CHUNK
    NC = S // C
    W = HG_HEADS * HG_DIM
    d_gm, d_xq, d_gates = (t.reshape(B, S, t.shape[-1]) for t in others)
    own0 = d_gm.shape[-1]
    xq0 = own0 + 4 * W
    gates0 = xq0 + d_xq.shape[-1]

    def body(q_ref, f_ref, i_ref, g_ref, o_ref, st_ref, do_ref, lbl_ref, gh_ref, gm_ref, xq_ref, gates_ref,
             d_ref, dlbl_ref, dgh_ref, dstate, dlb_acc):
        c = pl.program_id(0)
        d_ref[:, :, :own0] = gm_ref[...]
        d_ref[:, :, xq0:gates0] = xq_ref[...]
        d_ref[:, :, gates0:] = gates_ref[...]

        def put(b, k, val):
            d_ref[b, :, own0 + k * W:own0 + (k + 1) * W] = val.astype(d_ref.dtype)

        @pl.when(c == 0)
        def _():
            dstate[...] = jnp.zeros_like(dstate)
            dgh_ref[...] = jnp.zeros_like(dgh_ref)
            dlb_acc[...] = jnp.zeros_like(dlb_acc)

        lb = _hgrn_lower_bound(lbl_ref[...])
        ghv = gh_ref[...]
        row = lax.broadcasted_iota(jnp.int32, (C, C), 0)
        colm = lax.broadcasted_iota(jnp.int32, (C, C), 1)
        triu = colm >= row
        for b in range(B):
            hq, hg = q_ref[b].astype(F32), g_ref[b].astype(F32)
            gt = _hgrn_gates(hq, f_ref[b].astype(F32), lb)
            tril = gt["tril"]
            v = _heads(i_ref[b])
            qe, qi, ki, kl, e_last = (_heads(gt[n]) for n in ("qe", "qi", "ki", "kl", "e_last"))
            sgg = _sigmoid(hg)
            don_all = do_ref[b] * (hg * sgg)
            o, don = _heads(o_ref[b]), _heads(don_all)
            d_qe, d_qi, d_ki, d_kl, dv, n_all, dal = [], [], [], [], [], [], []
            for h in range(HG_HEADS):
                r = lax.rsqrt(jnp.mean(o[h] * o[h], axis=-1, keepdims=True) + EPS)
                n = o[h] * r
                n_all.append(n)
                dgh_ref[...] += jnp.sum(don[h] * n, axis=0, keepdims=True)
                dn = don[h] * ghv
                d_o = r * (dn - n * jnp.mean(dn * n, axis=-1, keepdims=True))
                st, dst = st_ref[b, h], dstate[b, h]
                p = jnp.where(tril, _dot_nt(qi[h], ki[h]), 0.0)
                dp = jnp.where(tril, _dot_nt(d_o, v[h]), 0.0)
                d_qe.append(_dot(d_o, st))
                d_qi.append(_dot(dp, ki[h]))
                d_ki.append(_dot_tn(dp, qi[h]))
                d_kl.append(_dot(v[h], dst))
                dv.append(_dot_tn(p, d_o) + _dot_nt(kl[h], dst))
                dstate[b, h] = dst * e_last[h] + _dot_tn(d_o, qe[h])
                dal.append(jnp.sum(dst * st, axis=0, keepdims=True) * e_last[h])
            d_qe, d_qi, d_ki, d_kl, n_all, dal = (jnp.concatenate(t, axis=1)
                                                  for t in (d_qe, d_qi, d_ki, d_kl, n_all, dal))
            put(b, 3, do_ref[b] * n_all * jnp.tile(ghv, (1, HG_HEADS)) * (sgg * (1.0 + hg * (1.0 - sgg))))
            put(b, 2, jnp.concatenate(dv, axis=1))
            d_a_last = dal + jnp.sum(d_kl * gt["kl"], axis=0, keepdims=True)
            dq = d_qe * gt["ea"] + d_qi * gt["ei"]
            dk = d_ki * gt["eki"] + d_kl * gt["ekl"]
            da = d_qe * gt["qe"] + d_qi * gt["qi"] - d_ki * gt["ki"] - d_kl * gt["kl"]
            dlogf = _dot_01(triu, da) + d_a_last
            sg, sq = gt["sg"], gt["sq"]
            dfg = dlogf / gt["fg"] - dk
            put(b, 1, dfg * (1.0 - lb) * sg * (1.0 - sg))
            dlb_acc[...] += jnp.sum(dfg * (1.0 - sg), axis=0, keepdims=True)
            put(b, 0, dq * (sq * (1.0 + hq * (1.0 - sq))))

        @pl.when(c == NC - 1)
        def _():
            dlb = dlb_acc[...]
            first = lax.broadcasted_iota(jnp.int32, (2, W), 0) == 0
            dlbl_ref[...] = jnp.where(first, dlb * lb * (1.0 - lb), -dlb * lb * (1.0 - lb))

    def col(cb):
        return pl.BlockSpec((B, C, 512), lambda c: (0, NC - 1 - c, cb))

    tile = pl.BlockSpec((B, C, W), lambda c: (0, NC - 1 - c, 0))
    proj3 = proj.reshape(B, S, proj.shape[-1])

    def rows(width):
        return pl.BlockSpec((B, C, width), lambda c: (0, NC - 1 - c, 0))

    width = proj.shape[-1]
    out, carried = _pcall(
        body, (proj3, proj3, proj3, proj3, o_saved, states, d_out.reshape(3, B, S, W), lbl, gh, d_gm, d_xq, d_gates),
        name="hgrn_bwd", grid=(NC,),
        in_specs=[col(COL_HQ), col(COL_HF), col(COL_HI), col(COL_HG), tile,
                  pl.BlockSpec((B, None, HG_HEADS, 128, 128), lambda c: (0, NC - 1 - c, 0, 0, 0)),
                  pl.BlockSpec((None, B, C, W), lambda c: (1, 0, NC - 1 - c, 0)),
                  pl.BlockSpec((2, W), lambda c: (0, 0)), pl.BlockSpec((1, HG_DIM), lambda c: (0, 0)),
                  rows(d_gm.shape[-1]), rows(d_xq.shape[-1]), rows(d_gates.shape[-1])],
        out_specs=(rows(width), pl.BlockSpec((2, W), lambda c: (0, 0)), pl.BlockSpec((1, HG_DIM), lambda c: (0, 0))),
        out_shape=(jax.ShapeDtypeStruct((B, S, width), BF16), jax.ShapeDtypeStruct((2, W), F32),
                   jax.ShapeDtypeStruct((1, HG_DIM), F32)),
        scratch_shapes=[pltpu.VMEM((B, HG_HEADS, 128, 128), F32), pltpu.VMEM((1, W), F32)],
        semantics=("arbitrary",), riders=riders)
    out = (out[0].reshape(B * S, width),) + tuple(out[1:])
    return (out, carried) if riders else out


_XA_SCALE = XA_DIM ** -0.5


def _attn_probs(qh, kh):
    s = _dot_nt(qh, kh) * _XA_SCALE
    e = jnp.exp(s - jnp.max(s, axis=-1, keepdims=True))
    return e / jnp.sum(e, axis=-1, keepdims=True)


def _attn_fwd(proj, kv, B, S):
    T = B * S
    tq = _row_tile(S)
    nq = S // tq
    W = XA_HEADS * XA_DIM

    def body(q_ref, kv_ref, o_ref):
        for h in range(XA_HEADS):
            sl = slice(h * 128, (h + 1) * 128)
            p = _attn_probs(q_ref[:, sl], kv_ref[:, sl])
            o_ref[:, sl] = _dot(p, kv_ref[:, W + h * 128:W + (h + 1) * 128]).astype(o_ref.dtype)

    return _pallas(
        body, name="attn_fwd", grid=(B, nq),
        in_specs=[pl.BlockSpec((tq, 512), lambda b, i: (b * nq + i, COL_XQ)),
                  pl.BlockSpec((MEM_LEN, 2 * W), lambda b, i: (b, 0))],
        out_specs=pl.BlockSpec((tq, W), lambda b, i: (b * nq + i, 0)),
        out_shape=jax.ShapeDtypeStruct((T, W), BF16), compiler_params=_cp("parallel", "parallel"),
    )(proj, kv)


def _attn_bwd(proj, kv, d_out, B, S):
    T = B * S
    tq = _row_tile(S)
    nq = S // tq
    W = XA_HEADS * XA_DIM

    def body(q_ref, kv_ref, do_ref, dq_ref, dkv_ref):
        @pl.when(pl.program_id(1) == 0)
        def _():
            dkv_ref[...] = jnp.zeros_like(dkv_ref)

        for h in range(XA_HEADS):
            sl = slice(h * 128, (h + 1) * 128)
            slv = slice(W + h * 128, W + (h + 1) * 128)
            qh = q_ref[:, sl]
            kh = kv_ref[:, sl]
            p = _attn_probs(qh, kh)
            dc = do_ref[:, sl]
            dp = _dot_nt(dc, kv_ref[:, slv])
            ds = p * (dp - jnp.sum(dp * p, axis=-1, keepdims=True)) * _XA_SCALE
            dq_ref[:, sl] = _dot(ds, kh).astype(dq_ref.dtype)
            dkv_ref[:, sl] += _dot_tn(ds, qh)
            dkv_ref[:, slv] += _dot_tn(p, dc)

    kvspec = pl.BlockSpec((MEM_LEN, 2 * W), lambda b, i: (b, 0))
    tile = pl.BlockSpec((tq, W), lambda b, i: (b * nq + i, 0))
    return _pallas(
        body, name="attn_bwd", grid=(B, nq),
        in_specs=[pl.BlockSpec((tq, 512), lambda b, i: (b * nq + i, COL_XQ)), kvspec,
                  pl.BlockSpec((None, tq, W), lambda b, i: (2, b * nq + i, 0))],
        out_specs=(tile, kvspec),
        out_shape=(jax.ShapeDtypeStruct((T, W), BF16), jax.ShapeDtypeStruct((B * MEM_LEN, 2 * W), F32)),
        compiler_params=_cp("parallel", "arbitrary"),
    )(proj, kv, d_out)


_MERGE_TM = 256
_GATE_W = 512


def _gate_specs(tm):
    base = COL_GATE0 // _GATE_W
    return [pl.BlockSpec((tm, _GATE_W), functools.partial(lambda i, k: (i, base + k), k=k)) for k in range(6)]


def _merge_fwd(a_out, b_out, c_out, wb, proj, riders=()):
    T = a_out.shape[0]
    tm = _row_tile(T, _MERGE_TM)
    nq, _, wd = wb.shape
    per_half = _GATE_W // wd

    def body(a_ref, b_ref, c_ref, w_ref, *rest):
        gates, (m_ref, up_ref) = rest[:6], rest[6:]
        for hf in range(2):
            cols = slice(hf * _GATE_W, (hf + 1) * _GATE_W)
            acc = None
            for n, br in enumerate((a_ref, b_ref, c_ref)):
                x = br[...]
                up = jnp.concatenate([_dot(x, w_ref[per_half * hf + j, n * BR_WIDTH:(n + 1) * BR_WIDTH, :])
                                      for j in range(per_half)], axis=1)
                up_ref[n, :, cols] = up.astype(up_ref.dtype)
                term = _sigmoid(gates[2 * n + hf][...].astype(F32)) * up
                acc = term if acc is None else acc + term
            m_ref[:, cols] = acc.astype(m_ref.dtype)

    br_spec = pl.BlockSpec((tm, BR_WIDTH), lambda i: (i, 0))
    return _carried(*_pcall(
        body, (a_out, b_out, c_out, wb, *([proj] * 6)), name="merge_fwd", grid=(T // tm,),
        in_specs=[br_spec, br_spec, br_spec,
                  pl.BlockSpec((nq, 3 * BR_WIDTH, wd), lambda i: (0, 0, 0))] + _gate_specs(tm),
        out_specs=(pl.BlockSpec((tm, D_MODEL), lambda i: (i, 0)), pl.BlockSpec((3, tm, D_MODEL), lambda i: (0, i, 0))),
        out_shape=(jax.ShapeDtypeStruct((T, D_MODEL), BF16), jax.ShapeDtypeStruct((3, T, D_MODEL), BF16)),
        semantics=("parallel",), riders=riders), riders)


def _branch_bwd_act(d_ups, wb, riders=()):
    _, T, D = d_ups.shape
    nq, _, wd = wb.shape
    tm = _row_tile(T)

    def body(d_ref, w_ref, o_ref):
        acc = None
        for q in range(nq):
            part = _dot_nt(d_ref[:, q * wd:(q + 1) * wd], w_ref[q])
            acc = part if acc is None else acc + part
        o_ref[...] = acc

    return _carried(*_pcall(
        body, (d_ups, wb), name="d_branch", grid=(3, T // tm),
        in_specs=[pl.BlockSpec((None, tm, D), lambda n, i: (n, i, 0)),
                  pl.BlockSpec((nq, BR_WIDTH, wd), lambda n, i: (0, n, 0))],
        out_specs=pl.BlockSpec((None, tm, BR_WIDTH), lambda n, i: (n, i, 0)),
        out_shape=jax.ShapeDtypeStruct((3, T, BR_WIDTH), F32), semantics=("parallel", "parallel"),
        riders=riders), riders)


def _branch_bwd_weight(name, br, d_ups, n):
    T = br.shape[0]
    D = d_ups.shape[2]
    wd = D // N_CHIPS
    tt = _row_tile(T, _TN_TOKENS)

    def body(b_ref, d_ref, o_ref):
        k = pl.program_id(0)
        for q in range(N_CHIPS):
            part = _dot_tn(b_ref[...], d_ref[:, q * wd:(q + 1) * wd])

            @pl.when(k == 0)
            def _():
                o_ref[q] = part

            @pl.when(k > 0)
            def _():
                o_ref[q] += part

    return _pallas(
        body, name=name, grid=(T // tt,),
        in_specs=[pl.BlockSpec((tt, BR_WIDTH), lambda k: (k, 0)),
                  pl.BlockSpec((None, tt, D), lambda k: (n, k, 0))],
        out_specs=pl.BlockSpec((N_CHIPS, BR_WIDTH, wd), lambda k: (0, 0, 0)),
        out_shape=jax.ShapeDtypeStruct((N_CHIPS, BR_WIDTH, wd), F32), compiler_params=_cp("arbitrary"),
    )(br, d_ups)


def _merge_bwd(d_merged, ups, proj, riders=()):
    T = d_merged.shape[0]
    tm = _row_tile(T, _MERGE_TM)

    def body(dm_ref, up_ref, *rest):
        gates, (dup_ref, dg_ref) = rest[:6], rest[6:]
        for hf in range(2):
            cols = slice(hf * _GATE_W, (hf + 1) * _GATE_W)
            dm = dm_ref[:, cols]
            for n in range(3):
                gate = _sigmoid(gates[2 * n + hf][...].astype(F32))
                dup_ref[n, :, cols] = (dm * gate).astype(dup_ref.dtype)
                dg_ref[:, n * D_MODEL + hf * _GATE_W:n * D_MODEL + (hf + 1) * _GATE_W] = (
                    dm * up_ref[n, :, cols].astype(F32) * gate * (1.0 - gate)).astype(dg_ref.dtype)

    tile = pl.BlockSpec((tm, D_MODEL), lambda i: (i, 0))
    tile3 = pl.BlockSpec((3, tm, D_MODEL), lambda i: (0, i, 0))
    return _carried(*_pcall(
        body, (d_merged, ups, *([proj] * 6)), name="merge_bwd", grid=(T // tm,),
        in_specs=[tile, tile3] + _gate_specs(tm),
        out_specs=(tile3, pl.BlockSpec((tm, 3 * D_MODEL), lambda i: (i, 0))),
        out_shape=(jax.ShapeDtypeStruct((3, T, D_MODEL), BF16), jax.ShapeDtypeStruct((T, 3 * D_MODEL), BF16)),
        semantics=("parallel",), riders=riders), riders)


_CONV_TF = D_FF // 2
_CONV_TS = 256
_HALO = 16


def _conv_fwd(ab, cw, cb, B, S):
    T = B * S
    ts = _row_tile(S, _CONV_TS)
    tf = _CONV_TF
    nb = D_FF // tf
    tps = S // ts
    hb = ts // _HALO

    def body(a_ref, p_ref, b_ref, w_ref, cb_ref, o_ref):
        start = (pl.program_id(0) % tps) == 0
        a = a_ref[...].astype(F32)
        prev = jnp.where(start, 0.0, p_ref[...].astype(F32))
        ext = jnp.concatenate([prev, a], axis=0)
        a1 = pltpu.roll(ext, 1, 0)[_HALO:, :]
        a2 = pltpu.roll(ext, 2, 0)[_HALO:, :]
        ac = cb_ref[...] + w_ref[0] * a2 + w_ref[1] * a1 + w_ref[2] * a
        o_ref[...] = (ac * _sigmoid(ac) * b_ref[...].astype(F32)).astype(o_ref.dtype)

    return _pallas(
        body, name="conv_fwd", grid=(T // ts, nb),
        in_specs=[pl.BlockSpec((ts, tf), lambda i, j: (i, j)),
                  pl.BlockSpec((_HALO, tf), lambda i, j: (jnp.maximum(i * hb - 1, 0), j)),
                  pl.BlockSpec((ts, tf), lambda i, j: (i, j + nb)),
                  pl.BlockSpec((3, 1, tf), lambda i, j: (0, 0, j)),
                  pl.BlockSpec((1, tf), lambda i, j: (0, j))],
        out_specs=pl.BlockSpec((ts, tf), lambda i, j: (i, j)),
        out_shape=jax.ShapeDtypeStruct((T, D_FF), BF16), compiler_params=_cp("parallel", "parallel"),
    )(ab, ab, ab, cw, cb)


def _conv_bwd(ab, d_ff, cw, cb, B, S, riders=()):
    T = B * S
    ts = _row_tile(S, _CONV_TS)
    tf = _CONV_TF
    nb = D_FF // tf
    tps = S // ts
    hb = ts // _HALO
    last_h = T // _HALO - 1
    n_ext = ts + _HALO

    def body(a_ref, ap_ref, an_ref, b_ref, bn_ref, d_ref, dn_ref, w_ref, cb_ref, dab_ref, dw_ref, dcb_ref):
        i = pl.program_id(1)

        @pl.when(i == 0)
        def _():
            dw_ref[...] = jnp.zeros_like(dw_ref)
            dcb_ref[...] = jnp.zeros_like(dcb_ref)

        start = (i % tps) == 0
        end = (i % tps) == tps - 1
        a = a_ref[...].astype(F32)
        ext = jnp.concatenate([jnp.where(start, 0.0, ap_ref[...].astype(F32)), a, an_ref[...].astype(F32)], axis=0)
        r1 = pltpu.roll(ext, 1, 0)[_HALO:, :]
        r2 = pltpu.roll(ext, 2, 0)[_HALO:, :]
        ac = cb_ref[...] + w_ref[0] * r2 + w_ref[1] * r1 + w_ref[2] * ext[_HALO:, :]
        sg = _sigmoid(ac)
        d_e = jnp.concatenate([d_ref[...].astype(F32), jnp.where(end, 0.0, dn_ref[...].astype(F32))], axis=0)
        b_e = jnp.concatenate([b_ref[...].astype(F32), bn_ref[...].astype(F32)], axis=0)
        dab_ref[1] = (d_e[:ts, :] * (ac * sg)[:ts, :]).astype(dab_ref.dtype)
        dac = d_e * b_e * sg * (1.0 + ac * (1.0 - sg))
        u1 = pltpu.roll(dac, n_ext - 1, 0)[:ts, :]
        u2 = pltpu.roll(dac, n_ext - 2, 0)[:ts, :]
        dac0 = dac[:ts, :]
        dab_ref[0] = (w_ref[2] * dac0 + w_ref[1] * u1 + w_ref[0] * u2).astype(dab_ref.dtype)
        dcb_ref[...] += jnp.sum(dac0, axis=0, keepdims=True)
        dw_ref[2] += jnp.sum(dac0 * a, axis=0, keepdims=True)
        dw_ref[1] += jnp.sum(dac0 * r1[:ts, :], axis=0, keepdims=True)
        dw_ref[0] += jnp.sum(dac0 * r2[:ts, :], axis=0, keepdims=True)

    def cur(off):
        return pl.BlockSpec((ts, tf), lambda j, i: (i, j + off))

    def nxt(off):
        return pl.BlockSpec((_HALO, tf), lambda j, i: (jnp.minimum((i + 1) * hb, last_h), j + off))

    return _carried(*_pcall(
        body, (ab, ab, ab, ab, ab, d_ff, d_ff, cw, cb), name="conv_bwd", grid=(nb, T // ts),
        in_specs=[cur(0), pl.BlockSpec((_HALO, tf), lambda j, i: (jnp.maximum(i * hb - 1, 0), j)), nxt(0),
                  cur(nb), nxt(nb), cur(0), nxt(0),
                  pl.BlockSpec((3, 1, tf), lambda j, i: (0, 0, j)), pl.BlockSpec((1, tf), lambda j, i: (0, j))],
        out_specs=(pl.BlockSpec((2, ts, tf), lambda j, i: (0, i, j)), pl.BlockSpec((3, 1, tf), lambda j, i: (0, 0, j)),
                   pl.BlockSpec((1, tf), lambda j, i: (0, j))),
        out_shape=(jax.ShapeDtypeStruct((2, T, D_FF), BF16),
                   jax.ShapeDtypeStruct((3, 1, D_FF), F32), jax.ShapeDtypeStruct((1, D_FF), F32)),
        semantics=("parallel", "arbitrary"), riders=riders), riders)


def _local_step(x, mem, tgt, p, comm, B, S):
    g = {}
    h = comm.carry("norm1", lambda r: _rms_fwd("norm1", x, p["norm1_g"], riders=r))
    proj = comm.carry("in_proj", lambda r: _mm_cs("in_proj", h, comm.w("w_in"), BF16, riders=r))
    a_out = _gmlp_fwd(proj, p["ln_v_g"], p["ln_v_b"], p["w_spatial"], p["b_spatial"])
    o_h, b_out, states = comm.carry(
        "hgrn_fwd", lambda r: _hgrn_fwd(proj, p["lb_logits"], p["hgrn_norm_g"], B, S, riders=r))
    memn = _rms_fwd("mem_norm", mem, p["mem_norm_g"])
    kv = _mm_rs("mem_kv", memn, comm.w("w_mem_kv"), F32)
    c_out = _attn_fwd(proj, kv, B, S)
    merged, ups = comm.carry(
        "merge_fwd", lambda r: _merge_fwd(a_out, b_out, c_out, comm.w("w_branch"), proj, riders=r))
    x1, h2 = _proj_res_norm("out_proj_norm2", merged, comm.w("w_out"), x, p["norm2_g"])
    ab = comm.carry("up_proj", lambda r: _mm_cs("up_proj", h2, comm.w("w_up"), BF16, riders=r))
    conv_w = comm.w("conv_w")
    ff = _conv_fwd(ab, conv_w, p["conv_b"], B, S)
    dx2, g["final_g"], loss = _proj_res_loss("down_proj_loss", ff, comm.w("w_down"), x1, tgt, p["final_g"])

    comm.grad("w_down", _mm_tn_rs("g_w_down", ff, dx2, to=D_FF // 2))
    d_ff = comm.carry("d_ff", lambda r: _mm_nt_rs("d_ff", dx2, comm.w("w_down"), BF16, riders=r))
    d_ab, g["conv_w"], g["conv_b"] = comm.carry(
        "conv_bwd", lambda r: _conv_bwd(ab, d_ff, conv_w, p["conv_b"], B, S, riders=r))
    comm.grad("w_up", _mm_tn_cs("g_w_up", h2, d_ab, N_CHIPS, to=512, stacked=True))
    d_x1, g["norm2_g"] = comm.carry("d_h2", lambda r: _mm_nt_cs(
        "d_h2_norm2_bwd", d_ab, comm.w("w_up"), F32, riders=r, stacked=True, norm_bwd=(x1, p["norm2_g"], dx2)))
    comm.grad("w_out", _mm_tn_rs("g_w_out", merged, d_x1, to=512))
    d_merged = _mm_nt_rs("d_merged", d_x1, comm.w("w_out"), F32)
    d_ups, d_gates = comm.carry("merge_bwd", lambda r: _merge_bwd(d_merged, ups, proj, riders=r))

    d_br = comm.carry("d_branch", lambda r: _branch_bwd_act(d_ups, comm.w("w_branch"), riders=r))
    comm.grad("w_branch", jnp.concatenate(
        [_branch_bwd_weight("g_w_branch%d" % n, br, d_ups, n) for n, br in enumerate((a_out, b_out, c_out))],
        axis=1))

    d_gm, g["w_spatial"], g["b_spatial"], g["ln_v_g"], g["ln_v_b"] = comm.carry(
        "gmlp_bwd", lambda r: _gmlp_bwd(proj, d_br, p["ln_v_g"], p["ln_v_b"], p["w_spatial"], p["b_spatial"],
                                        riders=r))
    d_xq, d_kv = _attn_bwd(proj, kv, d_br, B, S)
    comm.grad("w_mem_kv", _mm_tn_rs("g_w_mem_kv", memn, d_kv, to=512))
    d_memn = _mm_nt_rs("d_memn", d_kv, comm.w("w_mem_kv"), F32)
    _, g["mem_norm_g"] = _rms_bwd("mem_norm_bwd", mem, p["mem_norm_g"], d_memn, None)
    d_proj, g["lb_logits"], g["hgrn_norm_g"] = comm.carry(
        "hgrn_bwd", lambda r: _hgrn_bwd(proj, o_h, states, d_br, p["lb_logits"], p["hgrn_norm_g"],
                                        (d_gm, d_xq, d_gates), B, S, riders=r))
    comm.small_grads([g[n].reshape(_SMALL_SHAPE[n]) for n in _SMALL_EARLY] + [loss])
    comm.grad("w_in", comm.carry("g_w_in", lambda r: _mm_tn_cs("g_w_in", h, d_proj, N_CHIPS, to=512, riders=r)))
    grad_x, g["norm1_g"] = comm.carry("d_h", lambda r: _mm_nt_cs(
        "d_h_norm1_bwd", d_proj, comm.w("w_in"), F32, riders=r, norm_bwd=(x, p["norm1_g"], d_x1)))
    return loss, grad_x, g


HBM_SPEC = pl.BlockSpec(memory_space=pltpu.HBM)


def _place():
    x, y, c = lax.axis_index("x"), lax.axis_index("y"), lax.axis_index("c")
    other_chips = [(1 - x, y), (x, 1 - y), (1 - x, 1 - y)]
    return x, y, c, other_chips


def _remote(src, dst, send_sem, recv_sem, dev):
    return pltpu.make_async_remote_copy(src_ref=src, dst_ref=dst, send_sem=send_sem, recv_sem=recv_sem,
                                        device_id=dev, device_id_type=MESH_ID)


class _Exchange:
    def __init__(self, operands, out_shape, aliases, scratch, start, finish):
        self.operands, self.out_shape, self.aliases, self.scratch = operands, out_shape, aliases, scratch
        self.start, self.finish = start, finish


def _run_exchanges(name, exs):
    n_in = [len(ex.operands) for ex in exs]
    n_out = [len(ex.out_shape) for ex in exs]
    n_scr = [len(ex.scratch) for ex in exs]

    def body(*refs):
        ins, outs, scr = refs[:sum(n_in)], refs[sum(n_in):sum(n_in) + sum(n_out)], refs[sum(n_in) + sum(n_out):]
        parts, oi, oo, os_ = [], 0, 0, 0
        for k in range(len(exs)):
            parts.append((ins[oi:oi + n_in[k]], outs[oo:oo + n_out[k]], scr[os_:os_ + n_scr[k]]))
            oi, oo, os_ = oi + n_in[k], oo + n_out[k], os_ + n_scr[k]
        for ex, part in zip(exs, parts):
            ex.start(*part)
        for ex, part in zip(exs, parts):
            ex.finish(*part)

    aliases, ops, shapes, scratch, oi, oo = {}, [], [], [], 0, 0
    for k, ex in enumerate(exs):
        aliases.update({oi + a: oo + b for a, b in ex.aliases.items()})
        oi, oo = oi + n_in[k], oo + n_out[k]
        ops += list(ex.operands)
        shapes += [pltpu.HBM(s.shape, s.dtype) for s in ex.out_shape]
        scratch += list(ex.scratch)
    res = _pallas(
        body, name=name, in_specs=[HBM_SPEC] * len(ops), out_specs=(HBM_SPEC,) * len(shapes), out_shape=tuple(shapes),
        input_output_aliases=aliases, scratch_shapes=scratch,
    )(*ops)
    out, oo = [], 0
    for k in range(len(exs)):
        out.append(list(res[oo:oo + n_out[k]]))
        oo += n_out[k]
    return out


def _ex_all_gather(slabs, halved, part=(0, 1)):
    n = len(slabs)

    def rows(a, cc):
        if not halved[a]:
            return slice(None)
        pr = slabs[a].shape[1] // part[1]
        return pl.ds(part[0] * pr + cc * (pr // 2), pr // 2)

    def ici(bufs, scr, a, j, chip, c, mine):
        px, py = chip
        x, y, _, _ = _place()
        qs = 2 * x + y if mine else 2 * px + py
        piece = bufs[a].at[qs, rows(a, c)]
        return _remote(piece, piece, scr[0].at[3 * a + j], scr[1].at[3 * a + j], (px, py, c))

    def d2d(bufs, scr, a, j, chip, cc):
        px, py = chip
        x, y, c, _ = _place()
        piece = bufs[a].at[2 * px + py, rows(a, cc)]
        return _remote(piece, piece, scr[2].at[3 * a + j], scr[3].at[3 * a + j], (x, y, 1 - c))

    def start(ins, outs, scr):
        _, _, c, chips = _place()
        for j, chip in enumerate(chips):
            for a in range(n):
                ici(outs, scr, a, j, chip, c, True).start()

    def finish(ins, outs, scr):
        _, _, c, chips = _place()
        for j, chip in enumerate(chips):
            for a in range(n):
                ici(outs, scr, a, j, chip, c, False).wait_recv()
                if halved[a]:
                    d2d(outs, scr, a, j, chip, c).start()
        for j, chip in enumerate(chips):
            for a in range(n):
                if halved[a]:
                    d2d(outs, scr, a, j, chip, 1 - c).wait_recv()
        for j, chip in enumerate(chips):
            for a in range(n):
                ici(outs, scr, a, j, chip, c, True).wait_send()
                if halved[a]:
                    d2d(outs, scr, a, j, chip, c).wait_send()

    return _Exchange(list(slabs), [jax.ShapeDtypeStruct(s.shape, s.dtype) for s in slabs],
                     {a: a for a in range(n)}, [pltpu.SemaphoreType.DMA((3 * n,))] * 4, start, finish)


def _ex_to_sibling(grads):
    n = len(grads)

    def copy(ins, outs, scr, a):
        x, y, c, _ = _place()
        hr = grads[a].shape[1] // 2
        return _remote(ins[a].at[:, pl.ds((1 - c) * hr, hr), :], outs[a], scr[0].at[a], scr[1].at[a], (x, y, 1 - c))

    def start(ins, outs, scr):
        for a in range(n):
            copy(ins, outs, scr, a).start()

    def finish(ins, outs, scr):
        for a in range(n):
            copy(ins, outs, scr, a).wait()

    out_shape = [jax.ShapeDtypeStruct((g.shape[0], g.shape[1] // 2, g.shape[2]), g.dtype) for g in grads]
    return _Exchange(list(grads), out_shape, {}, [pltpu.SemaphoreType.DMA((n,))] * 2, start, finish)


def _ex_to_owner(parts, part=(0, 1), landing=None):
    n = len(parts)

    def copy(ins, outs, scr, a, j, chip):
        _, _, c, _ = _place()
        px, py = chip
        pr = parts[a].shape[1] // part[1]
        rows = pl.ds(part[0] * pr, pr)
        return _remote(ins[a].at[2 * px + py, rows], outs[a].at[j, rows], scr[0].at[3 * a + j],
                       scr[1].at[3 * a + j], (px, py, c))

    def start(ins, outs, scr):
        for j, chip in enumerate(_place()[3]):
            for a in range(n):
                copy(ins, outs, scr, a, j, chip).start()

    def finish(ins, outs, scr):
        for j, chip in enumerate(_place()[3]):
            for a in range(n):
                copy(ins, outs, scr, a, j, chip).wait()

    out_shape = [jax.ShapeDtypeStruct((3,) + p.shape[1:], p.dtype) for p in parts]
    operands, aliases = list(parts), {}
    if landing is not None:
        operands, aliases = operands + list(landing), {n + a: a for a in range(n)}
    return _Exchange(operands, out_shape, aliases, [pltpu.SemaphoreType.DMA((3 * n,))] * 2, start, finish)


def _ex_share_halves(bufs):
    n = len(bufs)

    def copy(outs, scr, a, cc):
        x, y, c, _ = _place()
        hr = bufs[a].shape[0] // 2
        piece = outs[a].at[pl.ds(cc * hr, hr), :]
        return _remote(piece, piece, scr[0].at[a], scr[1].at[a], (x, y, 1 - c))

    def start(ins, outs, scr):
        c = _place()[2]
        for a in range(n):
            copy(outs, scr, a, c).start()

    def finish(ins, outs, scr):
        c = _place()[2]
        for a in range(n):
            copy(outs, scr, a, c).wait_send()
            copy(outs, scr, a, 1 - c).wait_recv()

    return _Exchange(list(bufs), [jax.ShapeDtypeStruct(b.shape, b.dtype) for b in bufs], {a: a for a in range(n)},
                     [pltpu.SemaphoreType.DMA((n,))] * 2, start, finish)


def _ex_gather_small(arrs):
    n = len(arrs)

    def peer_of(m):
        x, y, c, _ = _place()
        return (1 - x if m & 4 else x, 1 - y if m & 2 else y, 1 - c if m & 1 else c)

    def start(ins, outs, scr):
        x, y, c, _ = _place()
        for m in range(1, N_DEV):
            for a in range(n):
                k = (N_DEV - 1) * a + m - 1
                _remote(ins[a], outs[a].at[4 * x + 2 * y + c], scr[0].at[k], scr[1].at[k], peer_of(m)).start()

    def finish(ins, outs, scr):
        for m in range(1, N_DEV):
            px, py, pc = peer_of(m)
            for a in range(n):
                k = (N_DEV - 1) * a + m - 1
                slot = outs[a].at[4 * px + 2 * py + pc]
                cp = _remote(ins[a], slot, scr[0].at[k], scr[1].at[k], (px, py, pc))
                cp.wait_send()
                cp.wait_recv()

    slots = [jnp.zeros((N_DEV,) + a.shape, a.dtype) for a in arrs]
    out_shape = [jax.ShapeDtypeStruct(s.shape, s.dtype) for s in slots]
    return _Exchange(list(arrs) + slots, out_shape, {n + a: a for a in range(n)},
                     [pltpu.SemaphoreType.DMA(((N_DEV - 1) * n,))] * 2, start, finish)


def _div_tile(n, want):
    best = None
    for t in range(8, min(n, want) + 1, 8):
        if n % t == 0:
            best = t
    assert best is not None, n
    return best


def _cast_into_slab(name, w, place, dtype):
    r, cc = w.shape
    tr = r if r * cc <= 128 * 1024 else _div_tile(r, 256)

    def body(s_ref, w_ref, o_ref):
        o_ref[...] = w_ref[...].astype(o_ref.dtype)

    return _pallas(
        body, name=name,
        grid_spec=pltpu.PrefetchScalarGridSpec(
            num_scalar_prefetch=1, grid=(r // tr,),
            in_specs=[pl.BlockSpec((tr, cc), lambda i, s: (i, 0))],
            out_specs=pl.BlockSpec((None, tr, cc), lambda i, s: (s[0], i, 0))),
        out_shape=jax.ShapeDtypeStruct((N_CHIPS, r, cc), dtype), compiler_params=_cp("parallel"),
    )(place, w)


def _add_half(name, g, rcv, place):
    nq, r, cc = g.shape
    hr = r // 2

    def body(s_ref, g_ref, r_ref, o_ref):
        o_ref[...] = (g_ref[...] + r_ref[...]).astype(o_ref.dtype)

    spec = pl.BlockSpec((None, hr, cc), lambda i, s: (i, 0, 0))
    return _pallas(
        body, name=name,
        grid_spec=pltpu.PrefetchScalarGridSpec(
            num_scalar_prefetch=1, grid=(nq,),
            in_specs=[pl.BlockSpec((None, hr, cc), lambda i, s: (i, s[1], 0)), spec], out_specs=spec),
        out_shape=jax.ShapeDtypeStruct((nq, hr, cc), BF16), compiler_params=_cp("parallel"),
    )(place, g, rcv)


def _sum_owner(name, part, rcv, place):
    _, hr, cc = part.shape
    tr = _div_tile(hr, 128)
    nb = hr // tr

    def body(s_ref, p_ref, r_ref, o_ref):
        o_ref[...] = ((p_ref[...].astype(F32) + r_ref[0].astype(F32)) + r_ref[1].astype(F32)) + r_ref[2].astype(F32)

    return _pallas(
        body, name=name,
        grid_spec=pltpu.PrefetchScalarGridSpec(
            num_scalar_prefetch=1, grid=(nb,),
            in_specs=[pl.BlockSpec((None, tr, cc), lambda i, s: (s[0], i, 0)),
                      pl.BlockSpec((3, tr, cc), lambda i, s: (0, i, 0))],
            out_specs=pl.BlockSpec((tr, cc), lambda i, s: (s[1] * nb + i, 0))),
        out_shape=jax.ShapeDtypeStruct((2 * hr, cc), F32), compiler_params=_cp("parallel"),
    )(place, part, rcv)


def _sum_small(gathered, local, place):
    n = len(gathered)

    def body(s_ref, *refs):
        g_refs, l_refs, o_refs = refs[:n], refs[n:2 * n], refs[2 * n:]
        me = s_ref[2]
        for g_ref, l_ref, o_ref in zip(g_refs, l_refs, o_refs):
            acc = None
            for d in range(N_DEV):
                term = jnp.where(me == d, l_ref[...], g_ref[d])
                acc = term if acc is None else acc + term
            o_ref[...] = acc

    def whole(shape):
        return pl.BlockSpec(shape, lambda i, s, nd=len(shape): (0,) * nd)

    return _pallas(
        body, name="sum_small",
        grid_spec=pltpu.PrefetchScalarGridSpec(
            num_scalar_prefetch=1, grid=(1,),
            in_specs=[whole(g.shape) for g in gathered] + [whole(a.shape) for a in local],
            out_specs=tuple(whole(a.shape) for a in local)),
        out_shape=tuple(jax.ShapeDtypeStruct(a.shape, a.dtype) for a in local), compiler_params=_cp("arbitrary"),
    )(place, *gathered, *local)


def _adamw(name, w, g, m, v):
    r, cc = w.shape
    tr = r if r * cc <= 128 * 1024 else _div_tile(r, 256)

    def body(w_ref, g_ref, m_ref, v_ref, d_ref, mo_ref, vo_ref):
        gv = g_ref[...]
        mn = ADAM_B1 * m_ref[...] + (1.0 - ADAM_B1) * gv
        vn = ADAM_B2 * v_ref[...] + (1.0 - ADAM_B2) * (gv * gv)
        m_hat = mn / (1.0 - ADAM_B1 ** ADAM_STEP)
        v_hat = vn / (1.0 - ADAM_B2 ** ADAM_STEP)
        d_ref[...] = -ADAM_LR * (m_hat / (jnp.sqrt(v_hat) + ADAM_EPS) + ADAM_WD * w_ref[...])
        mo_ref[...] = mn
        vo_ref[...] = vn

    spec = pl.BlockSpec((tr, cc), lambda i: (i, 0))
    sd = jax.ShapeDtypeStruct((r, cc), F32)
    return _pallas(
        body, name=name, grid=(r // tr,), in_specs=[spec] * 4, out_specs=(spec,) * 3, out_shape=(sd,) * 3,
        compiler_params=_cp("parallel"),
    )(w, g, m, v)


_BIG = ("w_in", "w_up", "w_branch", "w_mem_kv", "w_out", "w_down")
_BIG_SHARD_SHAPE = {"w_in": (1024, 1664), "w_up": (1024, 1408), "w_branch": (1536, 256),
                    "w_mem_kv": (256, 1024), "w_out": (256, 1024), "w_down": (704, 1024)}
_SMALL_SHAPE = {"norm1_g": (1, D_MODEL), "ln_v_g": (1, GM_WIDTH), "ln_v_b": (1, GM_WIDTH),
                "w_spatial": (GM_GROUPS * GM_CHUNK, GM_CHUNK), "b_spatial": (GM_GROUPS, GM_CHUNK),
                "lb_logits": (2, HG_HEADS * HG_DIM), "hgrn_norm_g": (1, HG_DIM), "mem_norm_g": (1, D_MODEL),
                "norm2_g": (1, D_MODEL), "conv_w": (3, D_FF), "conv_b": (1, D_FF), "final_g": (1, D_MODEL)}
_SMALL_EARLY = tuple(n for n in _SMALL_SHAPE if n != "norm1_g")
_PARAM_ORDER = ("norm1_g", "w_in", "ln_v_g", "ln_v_b", "w_spatial", "b_spatial", "lb_logits", "hgrn_norm_g",
                "mem_norm_g", "w_mem_kv", "w_branch", "w_out", "norm2_g", "w_up", "conv_w", "conv_b", "w_down",
                "final_g")


def _adamw_small(ws, gs, ms, vs):
    n = len(ws)

    def body(*refs):
        w_refs, g_refs, m_refs, v_refs = refs[:n], refs[n:2 * n], refs[2 * n:3 * n], refs[3 * n:4 * n]
        d_refs, mo_refs, vo_refs = refs[4 * n:5 * n], refs[5 * n:6 * n], refs[6 * n:]
        for k in range(n):
            gv = g_refs[k][...]
            mn = ADAM_B1 * m_refs[k][...] + (1.0 - ADAM_B1) * gv
            vn = ADAM_B2 * v_refs[k][...] + (1.0 - ADAM_B2) * (gv * gv)
            m_hat = mn / (1.0 - ADAM_B1 ** ADAM_STEP)
            v_hat = vn / (1.0 - ADAM_B2 ** ADAM_STEP)
            d_refs[k][...] = -ADAM_LR * (m_hat / (jnp.sqrt(v_hat) + ADAM_EPS) + ADAM_WD * w_refs[k][...])
            mo_refs[k][...] = mn
            vo_refs[k][...] = vn

    specs = [pl.BlockSpec(a.shape, lambda i: (0, 0)) for a in ws]
    shapes = tuple(jax.ShapeDtypeStruct(a.shape, F32) for a in ws)
    res = _pallas(
        body, name="adamw_small", grid=(1,), in_specs=specs * 4, out_specs=tuple(specs * 3), out_shape=shapes * 3,
        compiler_params=_cp("arbitrary"),
    )(*ws, *gs, *ms, *vs)
    return res[:n], res[n:2 * n], res[2 * n:]


class _Comm:
    _ROW_SHARDED = ("w_mem_kv", "w_out", "w_down")

    def __init__(self, slabs, place):
        self.slabs, self.place = slabs, place
        self.full, self.raw, self.parts, self.landing, self.bufs, self.done = {}, {}, {}, {}, {}, {}

    def w(self, name):
        a = self.full[name]
        if name in self._ROW_SHARDED:
            return a.reshape(-1, a.shape[-1])
        if name == "conv_w":
            return jnp.transpose(a, (1, 0, 2)).reshape(3, 1, D_FF)
        return a

    def grad(self, name, arr):
        self.raw[name] = arr.reshape((N_CHIPS, -1, arr.shape[-1]))
        if name == "w_in":
            ex, deliver = self._to_sibling(["w_in"])
            deliver(_run_exchanges("rs_sibling_w_in", [ex])[0])

    def small_grads(self, arrays):
        self.small_local = list(arrays)

    def carry(self, tag, call):
        plan = self._plan(tag)
        if not plan:
            return call(())
        out, carried = call([ex for ex, _ in plan])
        for (_, deliver), res in zip(plan, carried):
            deliver(res)
        return out

    def finish(self, last_small):
        ex, deliver = self._share(["w_out", "w_branch", "w_mem_kv", "w_in"])
        shared, small = _run_exchanges("share_and_gather_last", [ex, _ex_gather_small(last_small)])
        deliver(shared)
        return self.done, self.small_local + list(last_small), self.small_everyone + small

    def _plan(self, tag):
        if tag == "norm1":
            return [self._gather(["w_in"])]
        if tag == "in_proj":
            return [self._gather(["w_branch", "w_out", "w_mem_kv", "w_down", "conv_w"])]
        if tag == "hgrn_fwd":
            return [self._gather(["w_up"])]
        if tag == "d_h2":
            return [self._to_sibling(["w_down", "w_up"])]
        if tag == "hgrn_bwd":
            return [self._to_owner(["w_down", "w_up"]), self._to_sibling(["w_out", "w_branch", "w_mem_kv"])]
        if tag == "g_w_in":
            def keep(res):
                self.small_everyone = res

            return [self._to_owner(["w_out", "w_branch", "w_mem_kv"]), self._share(["w_down", "w_up"]),
                    (_ex_gather_small(self.small_local), keep)]
        if tag == "d_h":
            return [self._to_owner(["w_in"])]
        return []

    def _gather(self, names, part=(0, 1)):
        def deliver(res):
            self.slabs.update(zip(names, res))
            self.full.update(zip(names, res))

        return _ex_all_gather([self.slabs[n] for n in names], [n != "conv_w" for n in names], part), deliver

    def _to_sibling(self, names):
        def deliver(res):
            for n, r in zip(names, res):
                self.parts[n] = _add_half("rs_add_" + n, self.raw[n], r, self.place)

        return _ex_to_sibling([self.raw[n] for n in names]), deliver

    def _to_owner(self, names, part=(0, 1)):
        def deliver(res):
            for n, r in zip(names, res):
                if part[0] + 1 < part[1]:
                    self.landing[n] = r
                else:
                    self.bufs[n] = _sum_owner("rs_sum_" + n, self.parts[n], r, self.place)

        landing = [self.landing[n] for n in names] if part[0] else None
        return _ex_to_owner([self.parts[n] for n in names], part, landing), deliver

    def _share(self, names):
        return _ex_share_halves([self.bufs[n] for n in names]), lambda res: self.done.update(zip(names, res))


def kernel(x, mem, norm1_g, w_in, ln_v_g, ln_v_b, w_spatial, b_spatial, lb_logits, hgrn_norm_g, mem_norm_g, w_mem_kv, w_branch, w_out, norm2_g, w_up, conv_w, conv_b, w_down, final_g, loss_target, m_norm1_g, m_w_in, m_ln_v_g, m_ln_v_b, m_w_spatial, m_b_spatial, m_lb_logits, m_hgrn_norm_g, m_mem_norm_g, m_w_mem_kv, m_w_branch, m_w_out, m_norm2_g, m_w_up, m_conv_w, m_conv_b, m_w_down, m_final_g, v_norm1_g, v_w_in, v_ln_v_g, v_ln_v_b, v_w_spatial, v_b_spatial, v_lb_logits, v_hgrn_norm_g, v_mem_norm_g, v_w_mem_kv, v_w_branch, v_w_out, v_norm2_g, v_w_up, v_conv_w, v_conv_b, v_w_down, v_final_g):
    w = dict(norm1_g=norm1_g, w_in=w_in, ln_v_g=ln_v_g, ln_v_b=ln_v_b, w_spatial=w_spatial, b_spatial=b_spatial,
             lb_logits=lb_logits, hgrn_norm_g=hgrn_norm_g, mem_norm_g=mem_norm_g, w_mem_kv=w_mem_kv,
             w_branch=w_branch, w_out=w_out, norm2_g=norm2_g, w_up=w_up, conv_w=conv_w, conv_b=conv_b,
             w_down=w_down, final_g=final_g)
    mom = dict(norm1_g=m_norm1_g, w_in=m_w_in, ln_v_g=m_ln_v_g, ln_v_b=m_ln_v_b, w_spatial=m_w_spatial,
               b_spatial=m_b_spatial, lb_logits=m_lb_logits, hgrn_norm_g=m_hgrn_norm_g, mem_norm_g=m_mem_norm_g,
               w_mem_kv=m_w_mem_kv, w_branch=m_w_branch, w_out=m_w_out, norm2_g=m_norm2_g, w_up=m_w_up,
               conv_w=m_conv_w, conv_b=m_conv_b, w_down=m_w_down, final_g=m_final_g)
    var = dict(norm1_g=v_norm1_g, w_in=v_w_in, ln_v_g=v_ln_v_g, ln_v_b=v_ln_v_b, w_spatial=v_w_spatial,
               b_spatial=v_b_spatial, lb_logits=v_lb_logits, hgrn_norm_g=v_hgrn_norm_g, mem_norm_g=v_mem_norm_g,
               w_mem_kv=v_w_mem_kv, w_branch=v_w_branch, w_out=v_w_out, norm2_g=v_norm2_g, w_up=v_w_up,
               conv_w=v_conv_w, conv_b=v_conv_b, w_down=v_w_down, final_g=v_final_g)
    B, S, D = x.shape
    T = B * S
    ci = lax.axis_index("c")
    q = 2 * lax.axis_index("x") + lax.axis_index("y")
    place = jnp.stack([q, ci, 2 * q + ci]).astype(jnp.int32)

    slabs = {n: _cast_into_slab("slab_" + n, w[n].reshape(_BIG_SHARD_SHAPE[n]), place, BF16) for n in _BIG}
    slabs["conv_w"] = _cast_into_slab("slab_conv_w", conv_w[0], place, F32)
    comm = _Comm(slabs, place)
    p = dict(
        norm1_g=norm1_g, ln_v_g=ln_v_g, ln_v_b=ln_v_b, w_spatial=w_spatial[0],
        b_spatial=b_spatial.reshape(GM_GROUPS, GM_CHUNK, 1), lb_logits=lb_logits, hgrn_norm_g=hgrn_norm_g,
        mem_norm_g=mem_norm_g, norm2_g=norm2_g, conv_b=conv_b, final_g=final_g.reshape(1, D))

    loss, grad_x, g = _local_step(x.reshape(T, D), mem.reshape(B * MEM_LEN, D), loss_target.reshape(T, D), p, comm,
                                  B, S)

    shard_grads, local_small, everyone = comm.finish([g["norm1_g"]])
    summed = _sum_small(everyone, local_small, place)
    small_names = list(_SMALL_EARLY) + ["norm1_g"]
    total = dict(zip(_SMALL_EARLY, summed))
    loss_total, total["norm1_g"] = summed[len(_SMALL_EARLY)][0, 0], summed[-1]

    grads, delta, new_m, new_v = {}, {}, {}, {}
    for n in _BIG:
        shp = _BIG_SHARD_SHAPE[n]
        grads[n] = shard_grads[n]
        delta[n], new_m[n], new_v[n] = _adamw("adamw_" + n, w[n].reshape(shp), shard_grads[n],
                                              mom[n].reshape(shp), var[n].reshape(shp))
    cw_shard = D_FF // N_CHIPS
    total["conv_w"] = lax.dynamic_slice(total["conv_w"], (0, q * cw_shard), (3, cw_shard))

    def flat2d(d, n):
        return d[n].reshape(total[n].shape)

    upd = _adamw_small([flat2d(w, n) for n in small_names], [total[n] for n in small_names],
                       [flat2d(mom, n) for n in small_names], [flat2d(var, n) for n in small_names])
    for k, n in enumerate(small_names):
        grads[n], delta[n], new_m[n], new_v[n] = total[n], upd[0][k], upd[1][k], upd[2][k]

    def shaped(d):
        return [d[n].reshape(w[n].shape) for n in _PARAM_ORDER]

    return (loss_total, grad_x.reshape(B, S, D), *shaped(grads), *shaped(delta), *shaped(new_m), *shaped(new_v))
```

```python
import functools
import math

import jax
import jax.numpy as jnp
from jax import lax
from jax.experimental import pallas as pl
from jax.experimental.pallas import tpu as pltpu

F32 = jnp.float32
BF16 = jnp.bfloat16
EPS = 1e-6

D_MODEL = 1024
MEM_LEN = 256
GM_WIDTH = 512
GM_CHUNK = 128
GM_GROUPS = 4
HG_HEADS = 4
HG_DIM = 128
HG_CHUNK = 64
XA_HEADS = 4
XA_DIM = 128
BR_WIDTH = 512
D_FF = 2816
IN_WIDTH = 6656
N_CHIPS = 4
N_DEV = 8

ADAM_LR = 0.001
ADAM_B1 = 0.9
ADAM_B2 = 0.999
ADAM_EPS = 1e-08
ADAM_WD = 0.01
ADAM_STEP = 10

COL_ZU, COL_ZV, COL_HQ, COL_HF, COL_HI, COL_HG, COL_XQ = 0, 1, 2, 3, 4, 5, 6
COL_GATE0 = 3584

VMEM_LIMIT_BYTES = 48 * 1024 * 1024
MESH_ID = pl.DeviceIdType.MESH


def _cp(*sem):
    return pltpu.CompilerParams(dimension_semantics=sem, vmem_limit_bytes=VMEM_LIMIT_BYTES)


def _pallas(body, *, out_shape, **kw):
    def pin(s):
        return pltpu.HBM(s.shape, s.dtype) if isinstance(s, jax.ShapeDtypeStruct) else s

    out_shape = tuple(pin(s) for s in out_shape) if isinstance(out_shape, (tuple, list)) else pin(out_shape)
    call = pl.pallas_call(body, out_shape=out_shape, **kw)

    def run(*operands):
        return call(*[pltpu.with_memory_space_constraint(o, pltpu.HBM) if jnp.issubdtype(o.dtype, jnp.floating)
                      else o for o in operands])

    return run


def _dot(a, b):
    return lax.dot_general(a.astype(BF16), b.astype(BF16), (((1,), (0,)), ((), ())), preferred_element_type=F32)


def _dot_nt(a, b):
    return lax.dot_general(a.astype(BF16), b.astype(BF16), (((1,), (1,)), ((), ())), preferred_element_type=F32)


def _dot_tn(a, b):
    return lax.dot_general(a.astype(BF16), b.astype(BF16), (((0,), (0,)), ((), ())), preferred_element_type=F32)


def _dot_01(mask01, x):
    hi = x.astype(BF16)
    r1 = x - hi.astype(F32)
    mid = r1.astype(BF16)
    lo = (r1 - mid.astype(F32)).astype(BF16)
    m = mask01.astype(BF16)
    dn = (((1,), (0,)), ((), ()))
    return (lax.dot_general(m, hi, dn, preferred_element_type=F32)
            + lax.dot_general(m, mid, dn, preferred_element_type=F32)
            + lax.dot_general(m, lo, dn, preferred_element_type=F32))


def _sigmoid(z):
    return 1.0 / (1.0 + jnp.exp(-z))


_GELU_C = math.sqrt(2.0 / math.pi)


def _gelu_and_grad(z):
    inner = _GELU_C * (z + 0.044715 * z * z * z)
    t = jnp.tanh(inner)
    val = 0.5 * z * (1.0 + t)
    grad = 0.5 * (1.0 + t) + 0.5 * z * (1.0 - t * t) * _GELU_C * (1.0 + 3.0 * 0.044715 * z * z)
    return val, grad


def _row_tile(n, want=512):
    t = min(want, n)
    assert n % t == 0
    return t


def _pcall(body, operands, *, name, grid, in_specs, out_specs, out_shape, scratch_shapes=(), semantics, riders=()):
    single = not isinstance(out_shape, (tuple, list))
    out_specs = (out_specs,) if single else tuple(out_specs)
    out_shape = (out_shape,) if single else tuple(out_shape)
    if not riders:
        res = _pallas(body, name=name, grid=grid, in_specs=list(in_specs), out_specs=out_specs,
                      out_shape=out_shape, scratch_shapes=list(scratch_shapes),
                      compiler_params=_cp(*semantics))(*operands)
        return (res[0] if single else res), []
    n_in, n_out, n_scr = len(in_specs), len(out_shape), len(scratch_shapes)
    ex_in = [len(ex.operands) for ex in riders]
    ex_out = [len(ex.out_shape) for ex in riders]
    ex_scr = [len(ex.scratch) for ex in riders]
    tot_in, tot_out = n_in + sum(ex_in), n_out + sum(ex_out)

    def wrapped(*refs):
        ins, outs, scr = refs[:tot_in], refs[tot_in:tot_in + tot_out], refs[tot_in + tot_out:]
        ids = [pl.program_id(d) for d in range(len(grid))]
        first = functools.reduce(lambda p, t: p & t, [i == 0 for i in ids])
        last = functools.reduce(lambda p, t: p & t, [i == n - 1 for i, n in zip(ids, grid)])
        parts, oi, oo, os_ = [], n_in, n_out, n_scr
        for k in range(len(riders)):
            parts.append((ins[oi:oi + ex_in[k]], outs[oo:oo + ex_out[k]], scr[os_:os_ + ex_scr[k]]))
            oi, oo, os_ = oi + ex_in[k], oo + ex_out[k], os_ + ex_scr[k]

        @pl.when(first)
        def _():
            for ex, part in zip(riders, parts):
                ex.start(*part)

        body(*ins[:n_in], *outs[:n_out], *scr[:n_scr])

        @pl.when(last)
        def _():
            for ex, part in zip(riders, parts):
                ex.finish(*part)

    aliases, oi, oo = {}, n_in, n_out
    all_ops, all_shapes, all_scr = list(operands), list(out_shape), list(scratch_shapes)
    for k, ex in enumerate(riders):
        aliases.update({oi + a: oo + b for a, b in ex.aliases.items()})
        oi, oo = oi + ex_in[k], oo + ex_out[k]
        all_ops += list(ex.operands)
        all_shapes += [pltpu.HBM(s.shape, s.dtype) for s in ex.out_shape]
        all_scr += list(ex.scratch)
    res = _pallas(
        wrapped, name=name, grid=grid, in_specs=list(in_specs) + [HBM_SPEC] * sum(ex_in),
        out_specs=out_specs + (HBM_SPEC,) * sum(ex_out), out_shape=tuple(all_shapes), scratch_shapes=all_scr,
        input_output_aliases=aliases, compiler_params=_cp(*(["arbitrary"] * len(grid))))(*all_ops)
    own = res[0] if single else tuple(res[:n_out])
    carried, oo = [], n_out
    for k in range(len(riders)):
        carried.append(list(res[oo:oo + ex_out[k]]))
        oo += ex_out[k]
    return own, carried


def _carried(out, carried, riders):
    return (out, carried) if riders else out


def _matmul(name, operands, *, grid, in_specs, o_spec, out_shape, out_dtype, dims, riders=()):
    nk = grid[2]
    assert nk == 1 or out_dtype == F32

    def body(a_ref, b_ref, o_ref):
        part = lax.dot_general(a_ref[...].astype(BF16), b_ref[...].astype(BF16), (dims, ((), ())),
                               preferred_element_type=F32)
        if nk == 1:
            o_ref[...] = part.astype(o_ref.dtype)
        else:
            k = pl.program_id(2)

            @pl.when(k == 0)
            def _():
                o_ref[...] = part

            @pl.when(k > 0)
            def _():
                o_ref[...] += part

    out, carried = _pcall(body, operands, name=name, grid=grid, in_specs=in_specs, out_specs=o_spec,
                          out_shape=jax.ShapeDtypeStruct(out_shape, out_dtype),
                          semantics=("parallel", "parallel", "arbitrary"), riders=riders)
    return (out, carried) if riders else out


NN = ((1,), (0,))
NT = ((1,), (1,))
TN = ((0,), (0,))
_TN_TOKENS = 4096


def _mm_cs(name, a, w, out_dtype, riders=()):
    M, K = a.shape
    nq, _, wd = w.shape
    tm = _row_tile(M)
    return _matmul(name, (a, w), grid=(nq, M // tm, 1),
                   in_specs=[pl.BlockSpec((tm, K), lambda j, i, k: (i, 0)),
                             pl.BlockSpec((None, K, wd), lambda j, i, k: (j, 0, 0))],
                   o_spec=pl.BlockSpec((tm, wd), lambda j, i, k: (i, j)),
                   out_shape=(M, nq * wd), out_dtype=out_dtype, dims=NN, riders=riders)


def _mm_rs(name, a, w, out_dtype):
    M, K = a.shape
    N = w.shape[1]
    tm = _row_tile(M)
    return _matmul(name, (a, w), grid=(M // tm, 1, 1),
                   in_specs=[pl.BlockSpec((tm, K), lambda i, j, k: (i, 0)), pl.BlockSpec((K, N), lambda i, j, k: (0, 0))],
                   o_spec=pl.BlockSpec((tm, N), lambda i, j, k: (i, 0)),
                   out_shape=(M, N), out_dtype=out_dtype, dims=NN)


def _mm_nt_rs(name, g, w, out_dtype, riders=()):
    M, N = g.shape
    K = w.shape[0]
    to = K
    tm = _row_tile(M)
    return _matmul(name, (g, w), grid=(M // tm, K // to, 1),
                   in_specs=[pl.BlockSpec((tm, N), lambda i, j, k: (i, 0)),
                             pl.BlockSpec((to, N), lambda i, j, k: (j, 0))],
                   o_spec=pl.BlockSpec((tm, to), lambda i, j, k: (i, j)),
                   out_shape=(M, K), out_dtype=out_dtype, dims=NT, riders=riders)


def _mm_nt_cs(name, g, w, out_dtype, riders=(), stacked=False, norm_bwd=None):
    M = g.shape[-2]
    nq, K, wd = w.shape
    tm = _row_tile(M, 256)

    def product(g_ref, w_ref):
        acc = None
        for q in range(nq):
            gq = g_ref[q // 2, :, (q % 2) * wd:(q % 2 + 1) * wd] if stacked else g_ref[:, q * wd:(q + 1) * wd]
            part = _dot_nt(gq, w_ref[q])
            acc = part if acc is None else acc + part
        return acc

    def body(g_ref, w_ref, o_ref):
        o_ref[...] = product(g_ref, w_ref).astype(o_ref.dtype)

    def body_norm(g_ref, w_ref, x_ref, gain_ref, dr_ref, dx_ref, dg_ref):
        @pl.when(pl.program_id(0) == 0)
        def _():
            dg_ref[...] = jnp.zeros_like(dg_ref)

        dx, dg = _rms_bwd_rows(x_ref[...], gain_ref[...], product(g_ref, w_ref))
        dg_ref[...] += dg
        dx_ref[...] = dx + dr_ref[...]

    g_spec = (pl.BlockSpec((2, tm, 2 * wd), lambda i: (0, i, 0)) if stacked
              else pl.BlockSpec((tm, nq * wd), lambda i: (i, 0)))
    w_spec = pl.BlockSpec((nq, K, wd), lambda i: (0, 0, 0))
    row = pl.BlockSpec((tm, K), lambda i: (i, 0))
    if norm_bwd is None:
        return _carried(*_pcall(
            body, (g, w), name=name, grid=(M // tm,), in_specs=[g_spec, w_spec], out_specs=row,
            out_shape=jax.ShapeDtypeStruct((M, K), out_dtype), semantics=("parallel",), riders=riders), riders)
    vec = pl.BlockSpec((1, K), lambda i: (0, 0))
    return _carried(*_pcall(
        body_norm, (g, w) + tuple(norm_bwd), name=name, grid=(M // tm,),
        in_specs=[g_spec, w_spec, row, vec, row], out_specs=(row, vec),
        out_shape=(jax.ShapeDtypeStruct((M, K), F32), jax.ShapeDtypeStruct((1, K), F32)),
        semantics=("arbitrary",), riders=riders), riders)


def _mm_tn_rs(name, a, g, to, tn=512):
    T, M = a.shape
    N = g.shape[1]
    tt = _row_tile(T, _TN_TOKENS)
    tn = min(tn, N)
    return _matmul(name, (a, g), grid=(M // to, N // tn, T // tt),
                   in_specs=[pl.BlockSpec((tt, to), lambda i, j, k: (k, i)),
                             pl.BlockSpec((tt, tn), lambda i, j, k: (k, j))],
                   o_spec=pl.BlockSpec((to, tn), lambda i, j, k: (i, j)),
                   out_shape=(M, N), out_dtype=F32, dims=TN)


def _mm_tn_cs(name, a, g, nq, to, riders=(), stacked=False):
    T, M = a.shape
    wd = g.shape[-1] * (2 if stacked else 1) // nq
    tt = _row_tile(T, _TN_TOKENS)
    g_spec = (pl.BlockSpec((None, tt, wd), lambda i, j, k: (j // 2, k, j % 2)) if stacked
              else pl.BlockSpec((tt, wd), lambda i, j, k: (k, j)))
    return _matmul(name, (a, g), grid=(M // to, nq, T // tt),
                   in_specs=[pl.BlockSpec((tt, to), lambda i, j, k: (k, i)), g_spec],
                   o_spec=pl.BlockSpec((None, to, wd), lambda i, j, k: (j, i, 0)),
                   out_shape=(nq, M, wd), out_dtype=F32, dims=TN, riders=riders)


def _rms_fwd(name, x, g, riders=()):
    T, D = x.shape
    tm = _row_tile(T)

    def body(x_ref, g_ref, o_ref):
        o_ref[...] = _rms_rows(x_ref[...], g_ref[...]).astype(o_ref.dtype)

    return _carried(*_pcall(
        body, (x, g), name=name, grid=(T // tm,),
        in_specs=[pl.BlockSpec((tm, D), lambda i: (i, 0)), pl.BlockSpec((1, D), lambda i: (0, 0))],
        out_specs=pl.BlockSpec((tm, D), lambda i: (i, 0)),
        out_shape=jax.ShapeDtypeStruct((T, D), BF16), semantics=("parallel",), riders=riders), riders)


def _rms_rows(xv, gain):
    return xv * lax.rsqrt(jnp.mean(xv * xv, axis=-1, keepdims=True) + EPS) * gain


def _rms_bwd_rows(xv, gain, dh):
    r = lax.rsqrt(jnp.mean(xv * xv, axis=-1, keepdims=True) + EPS)
    n = xv * r
    dn = dh * gain
    return r * (dn - n * jnp.mean(dn * n, axis=-1, keepdims=True)), jnp.sum(dh * n, axis=0, keepdims=True)


def _rms_bwd(name, x, g, dh, dres):
    T, D = x.shape
    tm = _row_tile(T)
    has_res = dres is not None

    def body(*refs):
        if has_res:
            x_ref, g_ref, dh_ref, dr_ref, dx_ref, dg_ref = refs
        else:
            x_ref, g_ref, dh_ref, dx_ref, dg_ref = refs

        @pl.when(pl.program_id(0) == 0)
        def _():
            dg_ref[...] = jnp.zeros_like(dg_ref)

        dx, dg = _rms_bwd_rows(x_ref[...], g_ref[...], dh_ref[...])
        dg_ref[...] += dg
        if has_res:
            dx = dx + dr_ref[...]
        dx_ref[...] = dx

    row = pl.BlockSpec((tm, D), lambda i: (i, 0))
    vec = pl.BlockSpec((1, D), lambda i: (0, 0))
    ops = (x, g, dh, dres) if has_res else (x, g, dh)
    return _pallas(
        body, name=name, grid=(T // tm,), in_specs=[row, vec, row] + ([row] if has_res else []),
        out_specs=(row, vec),
        out_shape=(jax.ShapeDtypeStruct((T, D), F32), jax.ShapeDtypeStruct((1, D), F32)),
        compiler_params=_cp("arbitrary"),
    )(*ops)


def _proj_res_norm(name, a, w, res, gain):
    M, K = a.shape
    N = w.shape[1]
    tm = _row_tile(M)

    def body(a_ref, w_ref, r_ref, g_ref, x_ref, h_ref):
        xv = _dot(a_ref[...], w_ref[...]) + r_ref[...]
        x_ref[...] = xv
        h_ref[...] = _rms_rows(xv, g_ref[...]).astype(h_ref.dtype)

    row = pl.BlockSpec((tm, N), lambda i: (i, 0))
    return _pallas(
        body, name=name, grid=(M // tm,),
        in_specs=[pl.BlockSpec((tm, K), lambda i: (i, 0)), pl.BlockSpec((K, N), lambda i: (0, 0)), row,
                  pl.BlockSpec((1, N), lambda i: (0, 0))],
        out_specs=(row, row), out_shape=(jax.ShapeDtypeStruct((M, N), F32), jax.ShapeDtypeStruct((M, N), BF16)),
        compiler_params=_cp("parallel"),
    )(a, w, res, gain)


def _proj_res_loss(name, a, w, res, tgt, gain):
    M, K = a.shape
    D = w.shape[1]
    tm = _row_tile(M)

    def body(a_ref, w_ref, r_ref, t_ref, g_ref, dx_ref, dg_ref, loss_ref):
        @pl.when(pl.program_id(0) == 0)
        def _():
            dg_ref[...] = jnp.zeros_like(dg_ref)
            loss_ref[...] = jnp.zeros_like(loss_ref)

        xv = _dot(a_ref[...], w_ref[...]) + r_ref[...]
        gv = g_ref[...]
        diff = _rms_rows(xv, gv) - t_ref[...]
        loss_ref[...] += 0.5 * jnp.sum(jnp.mean(diff * diff, axis=-1, keepdims=True))
        dx, dg = _rms_bwd_rows(xv, gv, diff * (1.0 / D))
        dg_ref[...] += dg
        dx_ref[...] = dx

    row = pl.BlockSpec((tm, D), lambda i: (i, 0))
    vec = pl.BlockSpec((1, D), lambda i: (0, 0))
    return _pallas(
        body, name=name, grid=(M // tm,),
        in_specs=[pl.BlockSpec((tm, K), lambda i: (i, 0)), pl.BlockSpec((K, D), lambda i: (0, 0)), row, row, vec],
        out_specs=(row, vec, pl.BlockSpec((8, 128), lambda i: (0, 0))),
        out_shape=(jax.ShapeDtypeStruct((M, D), F32), jax.ShapeDtypeStruct((1, D), F32),
                   jax.ShapeDtypeStruct((8, 128), F32)),
        compiler_params=_cp("arbitrary"),
    )(a, w, res, tgt, gain)


def _gmlp_pieces(zu, zv, lng, lnb, ws_ref, bs_ref):
    u, du = _gelu_and_grad(zu)
    v, dv = _gelu_and_grad(zv)
    mu = jnp.mean(v, axis=-1, keepdims=True)
    vc = v - mu
    rstd = lax.rsqrt(jnp.mean(vc * vc, axis=-1, keepdims=True) + EPS)
    vhat = vc * rstd
    vn = vhat * lng + lnb
    row = lax.broadcasted_iota(jnp.int32, (GM_CHUNK, GM_CHUNK), 0)
    col = lax.broadcasted_iota(jnp.int32, (GM_CHUNK, GM_CHUNK), 1)
    tril = row >= col
    wms, mixed = [], []
    for g in range(GM_GROUPS):
        sl = slice(g * 128, (g + 1) * 128)
        wm = jnp.where(tril, ws_ref[g], 0.0)
        wms.append(wm)
        mixed.append(_dot(wm, vn[:, sl]) + bs_ref[g])
    return u, du, dv, rstd, vhat, vn, wms, mixed, tril


def _gmlp_fwd(proj, lng, lnb, ws, bs_col):
    T = proj.shape[0]
    n = T // GM_CHUNK

    def body(zu_ref, zv_ref, lng_ref, lnb_ref, ws_ref, bs_ref, o_ref):
        u, _, _, _, _, _, _, mixed, _ = _gmlp_pieces(zu_ref[...].astype(F32), zv_ref[...].astype(F32),
                                                     lng_ref[...], lnb_ref[...],
                                                     ws_ref, bs_ref)
        for g in range(GM_GROUPS):
            sl = slice(g * 128, (g + 1) * 128)
            o_ref[:, sl] = (u[:, sl] * mixed[g]).astype(o_ref.dtype)

    vec = pl.BlockSpec((1, GM_WIDTH), lambda i: (0, 0))
    return _pallas(
        body, name="gmlp_fwd", grid=(n,),
        in_specs=[pl.BlockSpec((GM_CHUNK, 512), lambda i: (i, COL_ZU)),
                  pl.BlockSpec((GM_CHUNK, 512), lambda i: (i, COL_ZV)),
                  vec, vec,
                  pl.BlockSpec((GM_GROUPS, 128, 128), lambda i: (0, 0, 0)),
                  pl.BlockSpec((GM_GROUPS, 128, 1), lambda i: (0, 0, 0))],
        out_specs=pl.BlockSpec((GM_CHUNK, 512), lambda i: (i, 0)),
        out_shape=jax.ShapeDtypeStruct((T, GM_WIDTH), BF16), compiler_params=_cp("parallel"),
    )(proj, proj, lng, lnb, ws, bs_col)


def _gmlp_bwd(proj, d_out, lng, lnb, ws, bs_col, riders=()):
    T = proj.shape[0]
    n = T // GM_CHUNK

    def body(zu_ref, zv_ref, do_ref, lng_ref, lnb_ref, ws_ref, bs_ref,
             dz_ref, dws_ref, dbs_ref, dlng_ref, dlnb_ref, dm_acc):
        i = pl.program_id(0)

        @pl.when(i == 0)
        def _():
            dws_ref[...] = jnp.zeros_like(dws_ref)
            dlng_ref[...] = jnp.zeros_like(dlng_ref)
            dlnb_ref[...] = jnp.zeros_like(dlnb_ref)
            dm_acc[...] = jnp.zeros_like(dm_acc)

        lng_v = lng_ref[...]
        u, du, dv, rstd, vhat, vn, wms, mixed, tril = _gmlp_pieces(zu_ref[...].astype(F32), zv_ref[...].astype(F32),
                                                                  lng_v, lnb_ref[...],
                                                                  ws_ref, bs_ref)
        do = do_ref[...]
        dvn_parts = []
        for g in range(GM_GROUPS):
            sl = slice(g * 128, (g + 1) * 128)
            dog = do[:, sl]
            dz_ref[:, sl] = (dog * mixed[g] * du[:, sl]).astype(dz_ref.dtype)
            dmix = dog * u[:, sl]
            dm_acc[:, sl] += dmix
            dws_ref[g] += jnp.where(tril, _dot_nt(dmix, vn[:, sl]), 0.0)
            dvn_parts.append(_dot_tn(wms[g], dmix))
        dvn = jnp.concatenate(dvn_parts, axis=1)
        dlng_ref[...] += jnp.sum(dvn * vhat, axis=0, keepdims=True)
        dlnb_ref[...] += jnp.sum(dvn, axis=0, keepdims=True)
        dvh = dvn * lng_v
        dvv = rstd * (dvh - jnp.mean(dvh, axis=-1, keepdims=True)
                      - vhat * jnp.mean(dvh * vhat, axis=-1, keepdims=True))
        dz_ref[:, GM_WIDTH:] = (dvv * dv).astype(dz_ref.dtype)

        @pl.when(i == n - 1)
        def _():
            for g in range(GM_GROUPS):
                dbs_ref[g] = jnp.sum(dm_acc[:, g * 128:(g + 1) * 128], axis=1, keepdims=True)

    vec = pl.BlockSpec((1, GM_WIDTH), lambda i: (0, 0))
    wsp = pl.BlockSpec((GM_GROUPS, 128, 128), lambda i: (0, 0, 0))
    bsp = pl.BlockSpec((GM_GROUPS, 128, 1), lambda i: (0, 0, 0))
    return _carried(*_pcall(
        body, (proj, proj, d_out, lng, lnb, ws, bs_col), name="gmlp_bwd", grid=(n,),
        in_specs=[pl.BlockSpec((GM_CHUNK, 512), lambda i: (i, COL_ZU)),
                  pl.BlockSpec((GM_CHUNK, 512), lambda i: (i, COL_ZV)),
                  pl.BlockSpec((None, GM_CHUNK, 512), lambda i: (0, i, 0)), vec, vec, wsp, bsp],
        out_specs=(pl.BlockSpec((GM_CHUNK, 2 * GM_WIDTH), lambda i: (i, 0)), wsp, bsp, vec, vec),
        out_shape=(jax.ShapeDtypeStruct((T, 2 * GM_WIDTH), BF16),
                   jax.ShapeDtypeStruct((GM_GROUPS, 128, 128), F32), jax.ShapeDtypeStruct((GM_GROUPS, 128, 1), F32),
                   jax.ShapeDtypeStruct((1, GM_WIDTH), F32), jax.ShapeDtypeStruct((1, GM_WIDTH), F32)),
        scratch_shapes=[pltpu.VMEM((GM_CHUNK, GM_WIDTH), F32)],
        semantics=("arbitrary",), riders=riders), riders)


def _hgrn_lower_bound(lbl):
    return 1.0 / (1.0 + jnp.exp(lbl[1:2, :] - lbl[0:1, :]))


def _hgrn_gates(hq, hf, lb):
    C = HG_CHUNK
    sg = _sigmoid(hf)
    fg = lb + (1.0 - lb) * sg
    sq = _sigmoid(hq)
    row = lax.broadcasted_iota(jnp.int32, (C, C), 0)
    col = lax.broadcasted_iota(jnp.int32, (C, C), 1)
    tril = row >= col
    logf = jnp.log(fg)
    a = _dot_01(tril, logf)
    a_last = jnp.sum(logf, axis=0, keepdims=True)
    first_half = lax.broadcasted_iota(jnp.int32, logf.shape, 0) < (C // 2)
    a_mid = jnp.sum(jnp.where(first_half, logf, 0.0), axis=0, keepdims=True)
    ea, ei, eki, ekl = jnp.exp(a), jnp.exp(a - a_mid), jnp.exp(a_mid - a), jnp.exp(a_last - a)
    k = 1.0 - fg
    q = hq * sq
    qi = (q * ei).astype(BF16).astype(F32)
    ki = (k * eki).astype(BF16).astype(F32)
    return dict(sg=sg, fg=fg, sq=sq, tril=tril, ea=ea, ei=ei, eki=eki, ekl=ekl, e_last=jnp.exp(a_last),
                qe=q * ea, qi=qi, ki=ki, kl=k * ekl)


def _heads(x):
    return [x[:, h * HG_DIM:(h + 1) * HG_DIM] for h in range(HG_HEADS)]


def _hgrn_fwd(proj, lbl, gh, B, S, riders=()):
    C = HG_CHUNK
    NC = S // C
    W = HG_HEADS * HG_DIM

    def body(q_ref, f_ref, i_ref, g_ref, lbl_ref, gh_ref, o_ref, bo_ref, st_ref, state):
        @pl.when(pl.program_id(0) == 0)
        def _():
            state[...] = jnp.zeros_like(state)

        lb = _hgrn_lower_bound(lbl_ref[...])
        ghv = gh_ref[...]
        for b in range(B):
            gt = _hgrn_gates(q_ref[b].astype(F32), f_ref[b].astype(F32), lb)
            v = _heads(i_ref[b])
            qe, qi, ki, kl, e_last = (_heads(gt[n]) for n in ("qe", "qi", "ki", "kl", "e_last"))
            outs, normed = [], []
            for h in range(HG_HEADS):
                p = jnp.where(gt["tril"], _dot_nt(qi[h], ki[h]), 0.0)
                st = state[b, h]
                st_ref[b, h] = st
                o = _dot_nt(qe[h], st) + _dot(p, v[h])
                state[b, h] = st * e_last[h] + _dot_tn(v[h], kl[h])
                outs.append(o)
                normed.append(o * lax.rsqrt(jnp.mean(o * o, axis=-1, keepdims=True) + EPS) * ghv)
            o_ref[b] = jnp.concatenate(outs, axis=1)
            hg = g_ref[b].astype(F32)
            bo_ref[b] = (jnp.concatenate(normed, axis=1) * (hg * _sigmoid(hg))).astype(bo_ref.dtype)

    def col(cb):
        return pl.BlockSpec((B, C, 512), lambda c: (0, c, cb))

    tile = pl.BlockSpec((B, C, W), lambda c: (0, c, 0))
    proj3 = proj.reshape(B, S, proj.shape[-1])
    out, carried = _pcall(
        body, (proj3, proj3, proj3, proj3, lbl, gh), name="hgrn_fwd", grid=(NC,),
        in_specs=[col(COL_HQ), col(COL_HF), col(COL_HI), col(COL_HG),
                  pl.BlockSpec((2, W), lambda c: (0, 0)), pl.BlockSpec((1, HG_DIM), lambda c: (0, 0))],
        out_specs=(tile, tile, pl.BlockSpec((B, None, HG_HEADS, 128, 128), lambda c: (0, c, 0, 0, 0))),
        out_shape=(jax.ShapeDtypeStruct((B, S, W), F32), jax.ShapeDtypeStruct((B, S, W), BF16),
                   jax.ShapeDtypeStruct((B, NC, HG_HEADS, 128, 128), F32)),
        scratch_shapes=[pltpu.VMEM((B, HG_HEADS, 128, 128), F32)],
        semantics=("arbitrary",), riders=riders)
    o_h, b_out, states = out
    out = (o_h, b_out.reshape(B * S, W), states)
    return (out, carried) if riders else out


def _hgrn_bwd(proj, o_saved, states, d_out, lbl, gh, others, B, S, riders=()):
    C = HG_CHUNK
    NC = S // C
    W = HG_HEADS * HG_DIM
    d_gm, d_xq, d_gates = (t.reshape(B, S, t.shape[-1]) for t in others)
    own0 = d_gm.shape[-1]
    xq0 = own0 + 4 * W
    gates0 = xq0 + d_xq.shape[-1]

    def body(q_ref, f_ref, i_ref, g_ref, o_ref, st_ref, do_ref, lbl_ref, gh_ref, gm_ref, xq_ref, gates_ref,
             d_ref, dlbl_ref, dgh_ref, dstate, dlb_acc):
        c = pl.program_id(0)
        d_ref[:, :, :own0] = gm_ref[...]
        d_ref[:, :, xq0:gates0] = xq_ref[...]
        d_ref[:, :, gates0:] = gates_ref[...]

        def put(b, k, val):
            d_ref[b, :, own0 + k * W:own0 + (k + 1) * W] = val.astype(d_ref.dtype)

        @pl.when(c == 0)
        def _():
            dstate[...] = jnp.zeros_like(dstate)
            dgh_ref[...] = jnp.zeros_like(dgh_ref)
            dlb_acc[...] = jnp.zeros_like(dlb_acc)

        lb = _hgrn_lower_bound(lbl_ref[...])
        ghv = gh_ref[...]
        row = lax.broadcasted_iota(jnp.int32, (C, C), 0)
        colm = lax.broadcasted_iota(jnp.int32, (C, C), 1)
        triu = colm >= row
        for b in range(B):
            hq, hg = q_ref[b].astype(F32), g_ref[b].astype(F32)
            gt = _hgrn_gates(hq, f_ref[b].astype(F32), lb)
            tril = gt["tril"]
            v = _heads(i_ref[b])
            qe, qi, ki, kl, e_last = (_heads(gt[n]) for n in ("qe", "qi", "ki", "kl", "e_last"))
            sgg = _sigmoid(hg)
            don_all = do_ref[b] * (hg * sgg)
            o, don = _heads(o_ref[b]), _heads(don_all)
            d_qe, d_qi, d_ki, d_kl, dv, n_all, dal = [], [], [], [], [], [], []
            for h in range(HG_HEADS):
                r = lax.rsqrt(jnp.mean(o[h] * o[h], axis=-1, keepdims=True) + EPS)
                n = o[h] * r
                n_all.append(n)
                dgh_ref[...] += jnp.sum(don[h] * n, axis=0, keepdims=True)
                dn = don[h] * ghv
                d_o = r * (dn - n * jnp.mean(dn * n, axis=-1, keepdims=True))
                st, dst = st_ref[b, h], dstate[b, h]
                p = jnp.where(tril, _dot_nt(qi[h], ki[h]), 0.0)
                dp = jnp.where(tril, _dot_nt(d_o, v[h]), 0.0)
                d_qe.append(_dot(d_o, st))
                d_qi.append(_dot(dp, ki[h]))
                d_ki.append(_dot_tn(dp, qi[h]))
                d_kl.append(_dot(v[h], dst))
                dv.append(_dot_tn(p, d_o) + _dot_nt(kl[h], dst))
                dstate[b, h] = dst * e_last[h] + _dot_tn(d_o, qe[h])
                dal.append(jnp.sum(dst * st, axis=0, keepdims=True) * e_last[h])
            d_qe, d_qi, d_ki, d_kl, n_all, dal = (jnp.concatenate(t, axis=1)
                                                  for t in (d_qe, d_qi, d_ki, d_kl, n_all, dal))
            put(b, 3, do_ref[b] * n_all * jnp.tile(ghv, (1, HG_HEADS)) * (sgg * (1.0 + hg * (1.0 - sgg))))
            put(b, 2, jnp.concatenate(dv, axis=1))
            d_a_last = dal + jnp.sum(d_kl * gt["kl"], axis=0, keepdims=True)
            dq = d_qe * gt["ea"] + d_qi * gt["ei"]
            dk = d_ki * gt["eki"] + d_kl * gt["ekl"]
            da = d_qe * gt["qe"] + d_qi * gt["qi"] - d_ki * gt["ki"] - d_kl * gt["kl"]
            dlogf = _dot_01(triu, da) + d_a_last
            sg, sq = gt["sg"], gt["sq"]
            dfg = dlogf / gt["fg"] - dk
            put(b, 1, dfg * (1.0 - lb) * sg * (1.0 - sg))
            dlb_acc[...] += jnp.sum(dfg * (1.0 - sg), axis=0, keepdims=True)
            put(b, 0, dq * (sq * (1.0 + hq * (1.0 - sq))))

        @pl.when(c == NC - 1)
        def _():
            dlb = dlb_acc[...]
            first = lax.broadcasted_iota(jnp.int32, (2, W), 0) == 0
            dlbl_ref[...] = jnp.where(first, dlb * lb * (1.0 - lb), -dlb * lb * (1.0 - lb))

    def col(cb):
        return pl.BlockSpec((B, C, 512), lambda c: (0, NC - 1 - c, cb))

    tile = pl.BlockSpec((B, C, W), lambda c: (0, NC - 1 - c, 0))
    proj3 = proj.reshape(B, S, proj.shape[-1])

    def rows(width):
        return pl.BlockSpec((B, C, width), lambda c: (0, NC - 1 - c, 0))

    width = proj.shape[-1]
    out, carried = _pcall(
        body, (proj3, proj3, proj3, proj3, o_saved, states, d_out.reshape(3, B, S, W), lbl, gh, d_gm, d_xq, d_gates),
        name="hgrn_bwd", grid=(NC,),
        in_specs=[col(COL_HQ), col(COL_HF), col(COL_HI), col(COL_HG), tile,
                  pl.BlockSpec((B, None, HG_HEADS, 128, 128), lambda c: (0, NC - 1 - c, 0, 0, 0)),
                  pl.BlockSpec((None, B, C, W), lambda c: (1, 0, NC - 1 - c, 0)),
                  pl.BlockSpec((2, W), lambda c: (0, 0)), pl.BlockSpec((1, HG_DIM), lambda c: (0, 0)),
                  rows(d_gm.shape[-1]), rows(d_xq.shape[-1]), rows(d_gates.shape[-1])],
        out_specs=(rows(width), pl.BlockSpec((2, W), lambda c: (0, 0)), pl.BlockSpec((1, HG_DIM), lambda c: (0, 0))),
        out_shape=(jax.ShapeDtypeStruct((B, S, width), BF16), jax.ShapeDtypeStruct((2, W), F32),
                   jax.ShapeDtypeStruct((1, HG_DIM), F32)),
        scratch_shapes=[pltpu.VMEM((B, HG_HEADS, 128, 128), F32), pltpu.VMEM((1, W), F32)],
        semantics=("arbitrary",), riders=riders)
    out = (out[0].reshape(B * S, width),) + tuple(out[1:])
    return (out, carried) if riders else out


_XA_SCALE = XA_DIM ** -0.5


def _attn_probs(qh, kh):
    s = _dot_nt(qh, kh) * _XA_SCALE
    e = jnp.exp(s - jnp.max(s, axis=-1, keepdims=True))
    return e / jnp.sum(e, axis=-1, keepdims=True)


def _attn_fwd(proj, kv, B, S):
    T = B * S
    tq = _row_tile(S)
    nq = S // tq
    W = XA_HEADS * XA_DIM

    def body(q_ref, kv_ref, o_ref):
        for h in range(XA_HEADS):
            sl = slice(h * 128, (h + 1) * 128)
            p = _attn_probs(q_ref[:, sl], kv_ref[:, sl])
            o_ref[:, sl] = _dot(p, kv_ref[:, W + h * 128:W + (h + 1) * 128]).astype(o_ref.dtype)

    return _pallas(
        body, name="attn_fwd", grid=(B, nq),
        in_specs=[pl.BlockSpec((tq, 512), lambda b, i: (b * nq + i, COL_XQ)),
                  pl.BlockSpec((MEM_LEN, 2 * W), lambda b, i: (b, 0))],
        out_specs=pl.BlockSpec((tq, W), lambda b, i: (b * nq + i, 0)),
        out_shape=jax.ShapeDtypeStruct((T, W), BF16), compiler_params=_cp("parallel", "parallel"),
    )(proj, kv)


def _attn_bwd(proj, kv, d_out, B, S):
    T = B * S
    tq = _row_tile(S)
    nq = S // tq
    W = XA_HEADS * XA_DIM

    def body(q_ref, kv_ref, do_ref, dq_ref, dkv_ref):
        @pl.when(pl.program_id(1) == 0)
        def _():
            dkv_ref[...] = jnp.zeros_like(dkv_ref)

        for h in range(XA_HEADS):
            sl = slice(h * 128, (h + 1) * 128)
            slv = slice(W + h * 128, W + (h + 1) * 128)
            qh = q_ref[:, sl]
            kh = kv_ref[:, sl]
            p = _attn_probs(qh, kh)
            dc = do_ref[:, sl]
            dp = _dot_nt(dc, kv_ref[:, slv])
            ds = p * (dp - jnp.sum(dp * p, axis=-1, keepdims=True)) * _XA_SCALE
            dq_ref[:, sl] = _dot(ds, kh).astype(dq_ref.dtype)
            dkv_ref[:, sl] += _dot_tn(ds, qh)
            dkv_ref[:, slv] += _dot_tn(p, dc)

    kvspec = pl.BlockSpec((MEM_LEN, 2 * W), lambda b, i: (b, 0))
    tile = pl.BlockSpec((tq, W), lambda b, i: (b * nq + i, 0))
    return _pallas(
        body, name="attn_bwd", grid=(B, nq),
        in_specs=[pl.BlockSpec((tq, 512), lambda b, i: (b * nq + i, COL_XQ)), kvspec,
                  pl.BlockSpec((None, tq, W), lambda b, i: (2, b * nq + i, 0))],
        out_specs=(tile, kvspec),
        out_shape=(jax.ShapeDtypeStruct((T, W), BF16), jax.ShapeDtypeStruct((B * MEM_LEN, 2 * W), F32)),
        compiler_params=_cp("parallel", "arbitrary"),
    )(proj, kv, d_out)


_MERGE_TM = 256
_GATE_W = 512


def _gate_specs(tm):
    base = COL_GATE0 // _GATE_W
    return [pl.BlockSpec((tm, _GATE_W), functools.partial(lambda i, k: (i, base + k), k=k)) for k in range(6)]


def _merge_fwd(a_out, b_out, c_out, wb, proj, riders=()):
    T = a_out.shape[0]
    tm = _row_tile(T, _MERGE_TM)
    nq, _, wd = wb.shape
    per_half = _GATE_W // wd

    def body(a_ref, b_ref, c_ref, w_ref, *rest):
        gates, (m_ref, up_ref) = rest[:6], rest[6:]
        for hf in range(2):
            cols = slice(hf * _GATE_W, (hf + 1) * _GATE_W)
            acc = None
            for n, br in enumerate((a_ref, b_ref, c_ref)):
                x = br[...]
                up = jnp.concatenate([_dot(x, w_ref[per_half * hf + j, n * BR_WIDTH:(n + 1) * BR_WIDTH, :])
                                      for j in range(per_half)], axis=1)
                up_ref[n, :, cols] = up.astype(up_ref.dtype)
                term = _sigmoid(gates[2 * n + hf][...].astype(F32)) * up
                acc = term if acc is None else acc + term
            m_ref[:, cols] = acc.astype(m_ref.dtype)

    br_spec = pl.BlockSpec((tm, BR_WIDTH), lambda i: (i, 0))
    return _carried(*_pcall(
        body, (a_out, b_out, c_out, wb, *([proj] * 6)), name="merge_fwd", grid=(T // tm,),
        in_specs=[br_spec, br_spec, br_spec,
                  pl.BlockSpec((nq, 3 * BR_WIDTH, wd), lambda i: (0, 0, 0))] + _gate_specs(tm),
        out_specs=(pl.BlockSpec((tm, D_MODEL), lambda i: (i, 0)), pl.BlockSpec((3, tm, D_MODEL), lambda i: (0, i, 0))),
        out_shape=(jax.ShapeDtypeStruct((T, D_MODEL), BF16), jax.ShapeDtypeStruct((3, T, D_MODEL), BF16)),
        semantics=("parallel",), riders=riders), riders)


def _branch_bwd_act(d_ups, wb, riders=()):
    _, T, D = d_ups.shape
    nq, _, wd = wb.shape
    tm = _row_tile(T)

    def body(d_ref, w_ref, o_ref):
        acc = None
        for q in range(nq):
            part = _dot_nt(d_ref[:, q * wd:(q + 1) * wd], w_ref[q])
            acc = part if acc is None else acc + part
        o_ref[...] = acc

    return _carried(*_pcall(
        body, (d_ups, wb), name="d_branch", grid=(3, T // tm),
        in_specs=[pl.BlockSpec((None, tm, D), lambda n, i: (n, i, 0)),
                  pl.BlockSpec((nq, BR_WIDTH, wd), lambda n, i: (0, n, 0))],
        out_specs=pl.BlockSpec((None, tm, BR_WIDTH), lambda n, i: (n, i, 0)),
        out_shape=jax.ShapeDtypeStruct((3, T, BR_WIDTH), F32), semantics=("parallel", "parallel"),
        riders=riders), riders)


def _branch_bwd_weight(name, br, d_ups, n):
    T = br.shape[0]
    D = d_ups.shape[2]
    wd = D // N_CHIPS
    tt = _row_tile(T, _TN_TOKENS)

    def body(b_ref, d_ref, o_ref):
        k = pl.program_id(0)
        for q in range(N_CHIPS):
            part = _dot_tn(b_ref[...], d_ref[:, q * wd:(q + 1) * wd])

            @pl.when(k == 0)
            def _():
                o_ref[q] = part

            @pl.when(k > 0)
            def _():
                o_ref[q] += part

    return _pallas(
        body, name=name, grid=(T // tt,),
        in_specs=[pl.BlockSpec((tt, BR_WIDTH), lambda k: (k, 0)),
                  pl.BlockSpec((None, tt, D), lambda k: (n, k, 0))],
        out_specs=pl.BlockSpec((N_CHIPS, BR_WIDTH, wd), lambda k: (0, 0, 0)),
        out_shape=jax.ShapeDtypeStruct((N_CHIPS, BR_WIDTH, wd), F32), compiler_params=_cp("arbitrary"),
    )(br, d_ups)


def _merge_bwd(d_merged, ups, proj, riders=()):
    T = d_merged.shape[0]
    tm = _row_tile(T, _MERGE_TM)

    def body(dm_ref, up_ref, *rest):
        gates, (dup_ref, dg_ref) = rest[:6], rest[6:]
        for hf in range(2):
            cols = slice(hf * _GATE_W, (hf + 1) * _GATE_W)
            dm = dm_ref[:, cols]
            for n in range(3):
                gate = _sigmoid(gates[2 * n + hf][...].astype(F32))
                dup_ref[n, :, cols] = (dm * gate).astype(dup_ref.dtype)
                dg_ref[:, n * D_MODEL + hf * _GATE_W:n * D_MODEL + (hf + 1) * _GATE_W] = (
                    dm * up_ref[n, :, cols].astype(F32) * gate * (1.0 - gate)).astype(dg_ref.dtype)

    tile = pl.BlockSpec((tm, D_MODEL), lambda i: (i, 0))
    tile3 = pl.BlockSpec((3, tm, D_MODEL), lambda i: (0, i, 0))
    return _carried(*_pcall(
        body, (d_merged, ups, *([proj] * 6)), name="merge_bwd", grid=(T // tm,),
        in_specs=[tile, tile3] + _gate_specs(tm),
        out_specs=(tile3, pl.BlockSpec((tm, 3 * D_MODEL), lambda i: (i, 0))),
        out_shape=(jax.ShapeDtypeStruct((3, T, D_MODEL), BF16), jax.ShapeDtypeStruct((T, 3 * D_MODEL), BF16)),
        semantics=("parallel",), riders=riders), riders)


_CONV_TF = D_FF // 2
_CONV_TS = 256
_HALO = 16


def _conv_fwd(ab, cw, cb, B, S):
    T = B * S
    ts = _row_tile(S, _CONV_TS)
    tf = _CONV_TF
    nb = D_FF // tf
    tps = S // ts
    hb = ts // _HALO

    def body(a_ref, p_ref, b_ref, w_ref, cb_ref, o_ref):
        start = (pl.program_id(0) % tps) == 0
        a = a_ref[...].astype(F32)
        prev = jnp.where(start, 0.0, p_ref[...].astype(F32))
        ext = jnp.concatenate([prev, a], axis=0)
        a1 = pltpu.roll(ext, 1, 0)[_HALO:, :]
        a2 = pltpu.roll(ext, 2, 0)[_HALO:, :]
        ac = cb_ref[...] + w_ref[0] * a2 + w_ref[1] * a1 + w_ref[2] * a
        o_ref[...] = (ac * _sigmoid(ac) * b_ref[...].astype(F32)).astype(o_ref.dtype)

    return _pallas(
        body, name="conv_fwd", grid=(T // ts, nb),
        in_specs=[pl.BlockSpec((ts, tf), lambda i, j: (i, j)),
                  pl.BlockSpec((_HALO, tf), lambda i, j: (jnp.maximum(i * hb - 1, 0), j)),
                  pl.BlockSpec((ts, tf), lambda i, j: (i, j + nb)),
                  pl.BlockSpec((3, 1, tf), lambda i, j: (0, 0, j)),
                  pl.BlockSpec((1, tf), lambda i, j: (0, j))],
        out_specs=pl.BlockSpec((ts, tf), lambda i, j: (i, j)),
        out_shape=jax.ShapeDtypeStruct((T, D_FF), BF16), compiler_params=_cp("parallel", "parallel"),
    )(ab, ab, ab, cw, cb)


def _conv_bwd(ab, d_ff, cw, cb, B, S, riders=()):
    T = B * S
    ts = _row_tile(S, _CONV_TS)
    tf = _CONV_TF
    nb = D_FF // tf
    tps = S // ts
    hb = ts // _HALO
    last_h = T // _HALO - 1
    n_ext = ts + _HALO

    def body(a_ref, ap_ref, an_ref, b_ref, bn_ref, d_ref, dn_ref, w_ref, cb_ref, dab_ref, dw_ref, dcb_ref):
        i = pl.program_id(1)

        @pl.when(i == 0)
        def _():
            dw_ref[...] = jnp.zeros_like(dw_ref)
            dcb_ref[...] = jnp.zeros_like(dcb_ref)

        start = (i % tps) == 0
        end = (i % tps) == tps - 1
        a = a_ref[...].astype(F32)
        ext = jnp.concatenate([jnp.where(start, 0.0, ap_ref[...].astype(F32)), a, an_ref[...].astype(F32)], axis=0)
        r1 = pltpu.roll(ext, 1, 0)[_HALO:, :]
        r2 = pltpu.roll(ext, 2, 0)[_HALO:, :]
        ac = cb_ref[...] + w_ref[0] * r2 + w_ref[1] * r1 + w_ref[2] * ext[_HALO:, :]
        sg = _sigmoid(ac)
        d_e = jnp.concatenate([d_ref[...].astype(F32), jnp.where(end, 0.0, dn_ref[...].astype(F32))], axis=0)
        b_e = jnp.concatenate([b_ref[...].astype(F32), bn_ref[...].astype(F32)], axis=0)
        dab_ref[1] = (d_e[:ts, :] * (ac * sg)[:ts, :]).astype(dab_ref.dtype)
        dac = d_e * b_e * sg * (1.0 + ac * (1.0 - sg))
        u1 = pltpu.roll(dac, n_ext - 1, 0)[:ts, :]
        u2 = pltpu.roll(dac, n_ext - 2, 0)[:ts, :]
        dac0 = dac[:ts, :]
        dab_ref[0] = (w_ref[2] * dac0 + w_ref[1] * u1 + w_ref[0] * u2).astype(dab_ref.dtype)
        dcb_ref[...] += jnp.sum(dac0, axis=0, keepdims=True)
        dw_ref[2] += jnp.sum(dac0 * a, axis=0, keepdims=True)
        dw_ref[1] += jnp.sum(dac0 * r1[:ts, :], axis=0, keepdims=True)
        dw_ref[0] += jnp.sum(dac0 * r2[:ts, :], axis=0, keepdims=True)

    def cur(off):
        return pl.BlockSpec((ts, tf), lambda j, i: (i, j + off))

    def nxt(off):
        return pl.BlockSpec((_HALO, tf), lambda j, i: (jnp.minimum((i + 1) * hb, last_h), j + off))

    return _carried(*_pcall(
        body, (ab, ab, ab, ab, ab, d_ff, d_ff, cw, cb), name="conv_bwd", grid=(nb, T // ts),
        in_specs=[cur(0), pl.BlockSpec((_HALO, tf), lambda j, i: (jnp.maximum(i * hb - 1, 0), j)), nxt(0),
                  cur(nb), nxt(nb), cur(0), nxt(0),
                  pl.BlockSpec((3, 1, tf), lambda j, i: (0, 0, j)), pl.BlockSpec((1, tf), lambda j, i: (0, j))],
        out_specs=(pl.BlockSpec((2, ts, tf), lambda j, i: (0, i, j)), pl.BlockSpec((3, 1, tf), lambda j, i: (0, 0, j)),
                   pl.BlockSpec((1, tf), lambda j, i: (0, j))),
        out_shape=(jax.ShapeDtypeStruct((2, T, D_FF), BF16),
                   jax.ShapeDtypeStruct((3, 1, D_FF), F32), jax.ShapeDtypeStruct((1, D_FF), F32)),
        semantics=("parallel", "arbitrary"), riders=riders), riders)


def _local_step(x, mem, tgt, p, comm, B, S):
    g = {}
    h = comm.carry("norm1", lambda r: _rms_fwd("norm1", x, p["norm1_g"], riders=r))
    proj = comm.carry("in_proj", lambda r: _mm_cs("in_proj", h, comm.w("w_in"), BF16, riders=r))
    a_out = _gmlp_fwd(proj, p["ln_v_g"], p["ln_v_b"], p["w_spatial"], p["b_spatial"])
    o_h, b_out, states = comm.carry(
        "hgrn_fwd", lambda r: _hgrn_fwd(proj, p["lb_logits"], p["hgrn_norm_g"], B, S, riders=r))
    memn = _rms_fwd("mem_norm", mem, p["mem_norm_g"])
    kv = _mm_rs("mem_kv", memn, comm.w("w_mem_kv"), F32)
    c_out = _attn_fwd(proj, kv, B, S)
    merged, ups = comm.carry(
        "merge_fwd", lambda r: _merge_fwd(a_out, b_out, c_out, comm.w("w_branch"), proj, riders=r))
    x1, h2 = _proj_res_norm("out_proj_norm2", merged, comm.w("w_out"), x, p["norm2_g"])
    ab = comm.carry("up_proj", lambda r: _mm_cs("up_proj", h2, comm.w("w_up"), BF16, riders=r))
    conv_w = comm.w("conv_w")
    ff = _conv_fwd(ab, conv_w, p["conv_b"], B, S)
    dx2, g["final_g"], loss = _proj_res_loss("down_proj_loss", ff, comm.w("w_down"), x1, tgt, p["final_g"])

    comm.grad("w_down", _mm_tn_rs("g_w_down", ff, dx2, to=D_FF // 2))
    d_ff = comm.carry("d_ff", lambda r: _mm_nt_rs("d_ff", dx2, comm.w("w_down"), BF16, riders=r))
    d_ab, g["conv_w"], g["conv_b"] = comm.carry(
        "conv_bwd", lambda r: _conv_bwd(ab, d_ff, conv_w, p["conv_b"], B, S, riders=r))
    comm.grad("w_up", _mm_tn_cs("g_w_up", h2, d_ab, N_CHIPS, to=512, stacked=True))
    d_x1, g["norm2_g"] = comm.carry("d_h2", lambda r: _mm_nt_cs(
        "d_h2_norm2_bwd", d_ab, comm.w("w_up"), F32, riders=r, stacked=True, norm_bwd=(x1, p["norm2_g"], dx2)))
    comm.grad("w_out", _mm_tn_rs("g_w_out", merged, d_x1, to=512))
    d_merged = _mm_nt_rs("d_merged", d_x1, comm.w("w_out"), F32)
    d_ups, d_gates = comm.carry("merge_bwd", lambda r: _merge_bwd(d_merged, ups, proj, riders=r))

    d_br = comm.carry("d_branch", lambda r: _branch_bwd_act(d_ups, comm.w("w_branch"), riders=r))
    comm.grad("w_branch", jnp.concatenate(
        [_branch_bwd_weight("g_w_branch%d" % n, br, d_ups, n) for n, br in enumerate((a_out, b_out, c_out))],
        axis=1))

    d_gm, g["w_spatial"], g["b_spatial"], g["ln_v_g"], g["ln_v_b"] = comm.carry(
        "gmlp_bwd", lambda r: _gmlp_bwd(proj, d_br, p["ln_v_g"], p["ln_v_b"], p["w_spatial"], p["b_spatial"],
                                        riders=r))
    d_xq, d_kv = _attn_bwd(proj, kv, d_br, B, S)
    comm.grad("w_mem_kv", _mm_tn_rs("g_w_mem_kv", memn, d_kv, to=512))
    d_memn = _mm_nt_rs("d_memn", d_kv, comm.w("w_mem_kv"), F32)
    _, g["mem_norm_g"] = _rms_bwd("mem_norm_bwd", mem, p["mem_norm_g"], d_memn, None)
    d_proj, g["lb_logits"], g["hgrn_norm_g"] = comm.carry(
        "hgrn_bwd", lambda r: _hgrn_bwd(proj, o_h, states, d_br, p["lb_logits"], p["hgrn_norm_g"],
                                        (d_gm, d_xq, d_gates), B, S, riders=r))
    comm.small_grads([g[n].reshape(_SMALL_SHAPE[n]) for n in _SMALL_EARLY] + [loss])
    comm.grad("w_in", comm.carry("g_w_in", lambda r: _mm_tn_cs("g_w_in", h, d_proj, N_CHIPS, to=512, riders=r)))
    grad_x, g["norm1_g"] = comm.carry("d_h", lambda r: _mm_nt_cs(
        "d_h_norm1_bwd", d_proj, comm.w("w_in"), F32, riders=r, norm_bwd=(x, p["norm1_g"], d_x1)))
    return loss, grad_x, g


HBM_SPEC = pl.BlockSpec(memory_space=pltpu.HBM)


def _place():
    x, y, c = lax.axis_index("x"), lax.axis_index("y"), lax.axis_index("c")
    other_chips = [(1 - x, y), (x, 1 - y), (1 - x, 1 - y)]
    return x, y, c, other_chips


def _remote(src, dst, send_sem, recv_sem, dev):
    return pltpu.make_async_remote_copy(src_ref=src, dst_ref=dst, send_sem=send_sem, recv_sem=recv_sem,
                                        device_id=dev, device_id_type=MESH_ID)


class _Exchange:
    def __init__(self, operands, out_shape, aliases, scratch, start, finish):
        self.operands, self.out_shape, self.aliases, self.scratch = operands, out_shape, aliases, scratch
        self.start, self.finish = start, finish


def _run_exchanges(name, exs):
    n_in = [len(ex.operands) for ex in exs]
    n_out = [len(ex.out_shape) for ex in exs]
    n_scr = [len(ex.scratch) for ex in exs]

    def body(*refs):
        ins, outs, scr = refs[:sum(n_in)], refs[sum(n_in):sum(n_in) + sum(n_out)], refs[sum(n_in) + sum(n_out):]
        parts, oi, oo, os_ = [], 0, 0, 0
        for k in range(len(exs)):
            parts.append((ins[oi:oi + n_in[k]], outs[oo:oo + n_out[k]], scr[os_:os_ + n_scr[k]]))
            oi, oo, os_ = oi + n_in[k], oo + n_out[k], os_ + n_scr[k]
        for ex, part in zip(exs, parts):
            ex.start(*part)
        for ex, part in zip(exs, parts):
            ex.finish(*part)

    aliases, ops, shapes, scratch, oi, oo = {}, [], [], [], 0, 0
    for k, ex in enumerate(exs):
        aliases.update({oi + a: oo + b for a, b in ex.aliases.items()})
        oi, oo = oi + n_in[k], oo + n_out[k]
        ops += list(ex.operands)
        shapes += [pltpu.HBM(s.shape, s.dtype) for s in ex.out_shape]
        scratch += list(ex.scratch)
    res = _pallas(
        body, name=name, in_specs=[HBM_SPEC] * len(ops), out_specs=(HBM_SPEC,) * len(shapes), out_shape=tuple(shapes),
        input_output_aliases=aliases, scratch_shapes=scratch,
    )(*ops)
    out, oo = [], 0
    for k in range(len(exs)):
        out.append(list(res[oo:oo + n_out[k]]))
        oo += n_out[k]
    return out


def _ex_all_gather(slabs, halved, part=(0, 1)):
    n = len(slabs)

    def rows(a, cc):
        if not halved[a]:
            return slice(None)
        pr = slabs[a].shape[1] // part[1]
        return pl.ds(part[0] * pr + cc * (pr // 2), pr // 2)

    def ici(bufs, scr, a, j, chip, c, mine):
        px, py = chip
        x, y, _, _ = _place()
        qs = 2 * x + y if mine else 2 * px + py
        piece = bufs[a].at[qs, rows(a, c)]
        return _remote(piece, piece, scr[0].at[3 * a + j], scr[1].at[3 * a + j], (px, py, c))

    def d2d(bufs, scr, a, j, chip, cc):
        px, py = chip
        x, y, c, _ = _place()
        piece = bufs[a].at[2 * px + py, rows(a, cc)]
        return _remote(piece, piece, scr[2].at[3 * a + j], scr[3].at[3 * a + j], (x, y, 1 - c))

    def start(ins, outs, scr):
        _, _, c, chips = _place()
        for j, chip in enumerate(chips):
            for a in range(n):
                ici(outs, scr, a, j, chip, c, True).start()

    def finish(ins, outs, scr):
        _, _, c, chips = _place()
        for j, chip in enumerate(chips):
            for a in range(n):
                ici(outs, scr, a, j, chip, c, False).wait_recv()
                if halved[a]:
                    d2d(outs, scr, a, j, chip, c).start()
        for j, chip in enumerate(chips):
            for a in range(n):
                if halved[a]:
                    d2d(outs, scr, a, j, chip, 1 - c).wait_recv()
        for j, chip in enumerate(chips):
            for a in range(n):
                ici(outs, scr, a, j, chip, c, True).wait_send()
                if halved[a]:
                    d2d(outs, scr, a, j, chip, c).wait_send()

    return _Exchange(list(slabs), [jax.ShapeDtypeStruct(s.shape, s.dtype) for s in slabs],
                     {a: a for a in range(n)}, [pltpu.SemaphoreType.DMA((3 * n,))] * 4, start, finish)


def _ex_gather_relay(slabs):
    n = len(slabs)

    def rows(a, cc):
        hr = slabs[a].shape[1] // 2
        return pl.ds(cc * hr, hr)

    def peers():
        x, y, c, _ = _place()
        nbr0 = ((x + c) % 2, (y + 1 - c) % 2)
        nbr1 = ((x + 1 - c) % 2, (y + c) % 2)
        return x, y, c, nbr0, nbr1, (1 - x, 1 - y)

    def ici(bufs, scr, a, k, chip, dev, cc):
        _, _, c, _, _, _ = peers()
        piece = bufs[a].at[2 * chip[0] + chip[1], rows(a, cc)]
        return _remote(piece, piece, scr[0].at[3 * a + k], scr[1].at[3 * a + k], (dev[0], dev[1], c))

    def d2d(bufs, scr, a, k, chip, cc):
        x, y, c, _, _, _ = peers()
        piece = bufs[a].at[2 * chip[0] + chip[1], rows(a, cc)]
        return _remote(piece, piece, scr[2].at[3 * a + k], scr[3].at[3 * a + k], (x, y, 1 - c))

    def start(ins, outs, scr):
        x, y, c, nbr0, nbr1, _ = peers()
        for a in range(n):
            ici(outs, scr, a, 0, (x, y), nbr0, c).start()
            ici(outs, scr, a, 1, (x, y), nbr1, c).start()

    def finish(ins, outs, scr):
        x, y, c, nbr0, nbr1, diag = peers()
        for a in range(n):
            ici(outs, scr, a, 0, nbr0, nbr0, c).wait_recv()
            ici(outs, scr, a, 2, nbr0, nbr1, c).start()
            d2d(outs, scr, a, 0, nbr0, c).start()
        for a in range(n):
            ici(outs, scr, a, 1, nbr1, nbr1, c).wait_recv()
            d2d(outs, scr, a, 1, nbr1, c).start()
        for a in range(n):
            ici(outs, scr, a, 2, diag, nbr1, c).wait_recv()
            d2d(outs, scr, a, 2, diag, c).start()
        for a in range(n):
            d2d(outs, scr, a, 0, nbr1, 1 - c).wait_recv()
            d2d(outs, scr, a, 1, nbr0, 1 - c).wait_recv()
            d2d(outs, scr, a, 2, diag, 1 - c).wait_recv()
        for a in range(n):
            ici(outs, scr, a, 0, (x, y), nbr0, c).wait_send()
            ici(outs, scr, a, 1, (x, y), nbr1, c).wait_send()
            ici(outs, scr, a, 2, nbr0, nbr1, c).wait_send()
            d2d(outs, scr, a, 0, nbr0, c).wait_send()
            d2d(outs, scr, a, 1, nbr1, c).wait_send()
            d2d(outs, scr, a, 2, diag, c).wait_send()

    return _Exchange(list(slabs), [jax.ShapeDtypeStruct(s.shape, s.dtype) for s in slabs],
                     {a: a for a in range(n)}, [pltpu.SemaphoreType.DMA((3 * n,))] * 4, start, finish)


def _ex_to_sibling(grads):
    n = len(grads)

    def copy(ins, outs, scr, a):
        x, y, c, _ = _place()
        hr = grads[a].shape[1] // 2
        return _remote(ins[a].at[:, pl.ds((1 - c) * hr, hr), :], outs[a], scr[0].at[a], scr[1].at[a], (x, y, 1 - c))

    def start(ins, outs, scr):
        for a in range(n):
            copy(ins, outs, scr, a).start()

    def finish(ins, outs, scr):
        for a in range(n):
            copy(ins, outs, scr, a).wait()

    out_shape = [jax.ShapeDtypeStruct((g.shape[0], g.shape[1] // 2, g.shape[2]), g.dtype) for g in grads]
    return _Exchange(list(grads), out_shape, {}, [pltpu.SemaphoreType.DMA((n,))] * 2, start, finish)


def _ex_to_owner(parts, part=(0, 1), landing=None):
    n = len(parts)

    def copy(ins, outs, scr, a, j, chip):
        _, _, c, _ = _place()
        px, py = chip
        pr = parts[a].shape[1] // part[1]
        rows = pl.ds(part[0] * pr, pr)
        return _remote(ins[a].at[2 * px + py, rows], outs[a].at[j, rows], scr[0].at[3 * a + j],
                       scr[1].at[3 * a + j], (px, py, c))

    def start(ins, outs, scr):
        for j, chip in enumerate(_place()[3]):
            for a in range(n):
                copy(ins, outs, scr, a, j, chip).start()

    def finish(ins, outs, scr):
        for j, chip in enumerate(_place()[3]):
            for a in range(n):
                copy(ins, outs, scr, a, j, chip).wait()

    out_shape = [jax.ShapeDtypeStruct((3,) + p.shape[1:], p.dtype) for p in parts]
    operands, aliases = list(parts), {}
    if landing is not None:
        operands, aliases = operands + list(landing), {n + a: a for a in range(n)}
    return _Exchange(operands, out_shape, aliases, [pltpu.SemaphoreType.DMA((3 * n,))] * 2, start, finish)


def _ex_share_halves(bufs):
    n = len(bufs)

    def copy(outs, scr, a, cc):
        x, y, c, _ = _place()
        hr = bufs[a].shape[0] // 2
        piece = outs[a].at[pl.ds(cc * hr, hr), :]
        return _remote(piece, piece, scr[0].at[a], scr[1].at[a], (x, y, 1 - c))

    def start(ins, outs, scr):
        c = _place()[2]
        for a in range(n):
            copy(outs, scr, a, c).start()

    def finish(ins, outs, scr):
        c = _place()[2]
        for a in range(n):
            copy(outs, scr, a, c).wait_send()
            copy(outs, scr, a, 1 - c).wait_recv()

    return _Exchange(list(bufs), [jax.ShapeDtypeStruct(b.shape, b.dtype) for b in bufs], {a: a for a in range(n)},
                     [pltpu.SemaphoreType.DMA((n,))] * 2, start, finish)


def _ex_gather_small(arrs):
    n = len(arrs)

    def peer_of(m):
        x, y, c, _ = _place()
        return (1 - x if m & 4 else x, 1 - y if m & 2 else y, 1 - c if m & 1 else c)

    def start(ins, outs, scr):
        x, y, c, _ = _place()
        for m in range(1, N_DEV):
            for a in range(n):
                k = (N_DEV - 1) * a + m - 1
                _remote(ins[a], outs[a].at[4 * x + 2 * y + c], scr[0].at[k], scr[1].at[k], peer_of(m)).start()

    def finish(ins, outs, scr):
        for m in range(1, N_DEV):
            px, py, pc = peer_of(m)
            for a in range(n):
                k = (N_DEV - 1) * a + m - 1
                slot = outs[a].at[4 * px + 2 * py + pc]
                cp = _remote(ins[a], slot, scr[0].at[k], scr[1].at[k], (px, py, pc))
                cp.wait_send()
                cp.wait_recv()

    slots = [jnp.zeros((N_DEV,) + a.shape, a.dtype) for a in arrs]
    out_shape = [jax.ShapeDtypeStruct(s.shape, s.dtype) for s in slots]
    return _Exchange(list(arrs) + slots, out_shape, {n + a: a for a in range(n)},
                     [pltpu.SemaphoreType.DMA(((N_DEV - 1) * n,))] * 2, start, finish)


def _div_tile(n, want):
    best = None
    for t in range(8, min(n, want) + 1, 8):
        if n % t == 0:
            best = t
    assert best is not None, n
    return best


def _cast_into_slab(name, w, place, dtype):
    r, cc = w.shape
    tr = r if r * cc <= 128 * 1024 else _div_tile(r, 256)

    def body(s_ref, w_ref, o_ref):
        o_ref[...] = w_ref[...].astype(o_ref.dtype)

    return _pallas(
        body, name=name,
        grid_spec=pltpu.PrefetchScalarGridSpec(
            num_scalar_prefetch=1, grid=(r // tr,),
            in_specs=[pl.BlockSpec((tr, cc), lambda i, s: (i, 0))],
            out_specs=pl.BlockSpec((None, tr, cc), lambda i, s: (s[0], i, 0))),
        out_shape=jax.ShapeDtypeStruct((N_CHIPS, r, cc), dtype), compiler_params=_cp("parallel"),
    )(place, w)


def _add_half(name, g, rcv, place):
    nq, r, cc = g.shape
    hr = r // 2

    def body(s_ref, g_ref, r_ref, o_ref):
        o_ref[...] = (g_ref[...] + r_ref[...]).astype(o_ref.dtype)

    spec = pl.BlockSpec((None, hr, cc), lambda i, s: (i, 0, 0))
    return _pallas(
        body, name=name,
        grid_spec=pltpu.PrefetchScalarGridSpec(
            num_scalar_prefetch=1, grid=(nq,),
            in_specs=[pl.BlockSpec((None, hr, cc), lambda i, s: (i, s[1], 0)), spec], out_specs=spec),
        out_shape=jax.ShapeDtypeStruct((nq, hr, cc), BF16), compiler_params=_cp("parallel"),
    )(place, g, rcv)


def _sum_owner(name, part, rcv, place):
    _, hr, cc = part.shape
    tr = _div_tile(hr, 128)
    nb = hr // tr

    def body(s_ref, p_ref, r_ref, o_ref):
        o_ref[...] = ((p_ref[...].astype(F32) + r_ref[0].astype(F32)) + r_ref[1].astype(F32)) + r_ref[2].astype(F32)

    return _pallas(
        body, name=name,
        grid_spec=pltpu.PrefetchScalarGridSpec(
            num_scalar_prefetch=1, grid=(nb,),
            in_specs=[pl.BlockSpec((None, tr, cc), lambda i, s: (s[0], i, 0)),
                      pl.BlockSpec((3, tr, cc), lambda i, s: (0, i, 0))],
            out_specs=pl.BlockSpec((tr, cc), lambda i, s: (s[1] * nb + i, 0))),
        out_shape=jax.ShapeDtypeStruct((2 * hr, cc), F32), compiler_params=_cp("parallel"),
    )(place, part, rcv)


def _sum_small(gathered, local, place):
    n = len(gathered)

    def body(s_ref, *refs):
        g_refs, l_refs, o_refs = refs[:n], refs[n:2 * n], refs[2 * n:]
        me = s_ref[2]
        for g_ref, l_ref, o_ref in zip(g_refs, l_refs, o_refs):
            acc = None
            for d in range(N_DEV):
                term = jnp.where(me == d, l_ref[...], g_ref[d])
                acc = term if acc is None else acc + term
            o_ref[...] = acc

    def whole(shape):
        return pl.BlockSpec(shape, lambda i, s, nd=len(shape): (0,) * nd)

    return _pallas(
        body, name="sum_small",
        grid_spec=pltpu.PrefetchScalarGridSpec(
            num_scalar_prefetch=1, grid=(1,),
            in_specs=[whole(g.shape) for g in gathered] + [whole(a.shape) for a in local],
            out_specs=tuple(whole(a.shape) for a in local)),
        out_shape=tuple(jax.ShapeDtypeStruct(a.shape, a.dtype) for a in local), compiler_params=_cp("arbitrary"),
    )(place, *gathered, *local)


def _adamw(name, w, g, m, v):
    r, cc = w.shape
    tr = r if r * cc <= 128 * 1024 else _div_tile(r, 256)

    def body(w_ref, g_ref, m_ref, v_ref, d_ref, mo_ref, vo_ref):
        gv = g_ref[...]
        mn = ADAM_B1 * m_ref[...] + (1.0 - ADAM_B1) * gv
        vn = ADAM_B2 * v_ref[...] + (1.0 - ADAM_B2) * (gv * gv)
        m_hat = mn / (1.0 - ADAM_B1 ** ADAM_STEP)
        v_hat = vn / (1.0 - ADAM_B2 ** ADAM_STEP)
        d_ref[...] = -ADAM_LR * (m_hat / (jnp.sqrt(v_hat) + ADAM_EPS) + ADAM_WD * w_ref[...])
        mo_ref[...] = mn
        vo_ref[...] = vn

    spec = pl.BlockSpec((tr, cc), lambda i: (i, 0))
    sd = jax.ShapeDtypeStruct((r, cc), F32)
    return _pallas(
        body, name=name, grid=(r // tr,), in_specs=[spec] * 4, out_specs=(spec,) * 3, out_shape=(sd,) * 3,
        compiler_params=_cp("parallel"),
    )(w, g, m, v)


_BIG = ("w_in", "w_up", "w_branch", "w_mem_kv", "w_out", "w_down")
_BIG_SHARD_SHAPE = {"w_in": (1024, 1664), "w_up": (1024, 1408), "w_branch": (1536, 256),
                    "w_mem_kv": (256, 1024), "w_out": (256, 1024), "w_down": (704, 1024)}
_SMALL_SHAPE = {"norm1_g": (1, D_MODEL), "ln_v_g": (1, GM_WIDTH), "ln_v_b": (1, GM_WIDTH),
                "w_spatial": (GM_GROUPS * GM_CHUNK, GM_CHUNK), "b_spatial": (GM_GROUPS, GM_CHUNK),
                "lb_logits": (2, HG_HEADS * HG_DIM), "hgrn_norm_g": (1, HG_DIM), "mem_norm_g": (1, D_MODEL),
                "norm2_g": (1, D_MODEL), "conv_w": (3, D_FF), "conv_b": (1, D_FF), "final_g": (1, D_MODEL)}
_SMALL_EARLY = tuple(n for n in _SMALL_SHAPE if n != "norm1_g")
_PARAM_ORDER = ("norm1_g", "w_in", "ln_v_g", "ln_v_b", "w_spatial", "b_spatial", "lb_logits", "hgrn_norm_g",
                "mem_norm_g", "w_mem_kv", "w_branch", "w_out", "norm2_g", "w_up", "conv_w", "conv_b", "w_down",
                "final_g")


def _adamw_small(ws, gs, ms, vs):
    n = len(ws)

    def body(*refs):
        w_refs, g_refs, m_refs, v_refs = refs[:n], refs[n:2 * n], refs[2 * n:3 * n], refs[3 * n:4 * n]
        d_refs, mo_refs, vo_refs = refs[4 * n:5 * n], refs[5 * n:6 * n], refs[6 * n:]
        for k in range(n):
            gv = g_refs[k][...]
            mn = ADAM_B1 * m_refs[k][...] + (1.0 - ADAM_B1) * gv
            vn = ADAM_B2 * v_refs[k][...] + (1.0 - ADAM_B2) * (gv * gv)
            m_hat = mn / (1.0 - ADAM_B1 ** ADAM_STEP)
            v_hat = vn / (1.0 - ADAM_B2 ** ADAM_STEP)
            d_refs[k][...] = -ADAM_LR * (m_hat / (jnp.sqrt(v_hat) + ADAM_EPS) + ADAM_WD * w_refs[k][...])
            mo_refs[k][...] = mn
            vo_refs[k][...] = vn

    specs = [pl.BlockSpec(a.shape, lambda i: (0, 0)) for a in ws]
    shapes = tuple(jax.ShapeDtypeStruct(a.shape, F32) for a in ws)
    res = _pallas(
        body, name="adamw_small", grid=(1,), in_specs=specs * 4, out_specs=tuple(specs * 3), out_shape=shapes * 3,
        compiler_params=_cp("arbitrary"),
    )(*ws, *gs, *ms, *vs)
    return res[:n], res[n:2 * n], res[2 * n:]


class _Comm:
    _ROW_SHARDED = ("w_mem_kv", "w_out", "w_down")

    def __init__(self, slabs, place):
        self.slabs, self.place = slabs, place
        self.full, self.raw, self.parts, self.landing, self.bufs, self.done = {}, {}, {}, {}, {}, {}

    def w(self, name):
        a = self.full[name]
        if name in self._ROW_SHARDED:
            return a.reshape(-1, a.shape[-1])
        if name == "conv_w":
            return jnp.transpose(a, (1, 0, 2)).reshape(3, 1, D_FF)
        return a

    def grad(self, name, arr):
        self.raw[name] = arr.reshape((N_CHIPS, -1, arr.shape[-1]))
        if name == "w_in":
            ex, deliver = self._to_sibling(["w_in"])
            deliver(_run_exchanges("rs_sibling_w_in", [ex])[0])

    def small_grads(self, arrays):
        self.small_local = list(arrays)

    def carry(self, tag, call):
        plan = self._plan(tag)
        if not plan:
            return call(())
        out, carried = call([ex for ex, _ in plan])
        for (_, deliver), res in zip(plan, carried):
            deliver(res)
        return out

    def finish(self, last_small):
        ex, deliver = self._share(["w_out", "w_branch", "w_mem_kv", "w_in"])
        shared, small = _run_exchanges("share_and_gather_last", [ex, _ex_gather_small(last_small)])
        deliver(shared)
        return self.done, self.small_local + list(last_small), self.small_everyone + small

    def _plan(self, tag):
        if tag == "norm1":
            def deliver(res):
                self.full["w_in"] = res[0]

            return [(_ex_gather_relay([self.slabs["w_in"]]), deliver)]
        if tag == "in_proj":
            return [self._gather(["w_branch", "w_out", "w_mem_kv", "w_down", "conv_w"])]
        if tag == "hgrn_fwd":
            return [self._gather(["w_up"])]
        if tag == "d_h2":
            return [self._to_sibling(["w_down", "w_up"])]
        if tag == "hgrn_bwd":
            return [self._to_owner(["w_down", "w_up"]), self._to_sibling(["w_out", "w_branch", "w_mem_kv"])]
        if tag == "g_w_in":
            def keep(res):
                self.small_everyone = res

            return [self._to_owner(["w_out", "w_branch", "w_mem_kv"]), self._share(["w_down", "w_up"]),
                    (_ex_gather_small(self.small_local), keep)]
        if tag == "d_h":
            return [self._to_owner(["w_in"])]
        return []

    def _gather(self, names, part=(0, 1)):
        def deliver(res):
            self.slabs.update(zip(names, res))
            self.full.update(zip(names, res))

        return _ex_all_gather([self.slabs[n] for n in names], [n != "conv_w" for n in names], part), deliver

    def _to_sibling(self, names):
        def deliver(res):
            for n, r in zip(names, res):
                self.parts[n] = _add_half("rs_add_" + n, self.raw[n], r, self.place)

        return _ex_to_sibling([self.raw[n] for n in names]), deliver

    def _to_owner(self, names, part=(0, 1)):
        def deliver(res):
            for n, r in zip(names, res):
                if part[0] + 1 < part[1]:
                    self.landing[n] = r
                else:
                    self.bufs[n] = _sum_owner("rs_sum_" + n, self.parts[n], r, self.place)

        landing = [self.landing[n] for n in names] if part[0] else None
        return _ex_to_owner([self.parts[n] for n in names], part, landing), deliver

    def _share(self, names):
        return _ex_share_halves([self.bufs[n] for n in names]), lambda res: self.done.update(zip(names, res))


def kernel(x, mem, norm1_g, w_in, ln_v_g, ln_v_b, w_spatial, b_spatial, lb_logits, hgrn_norm_g, mem_norm_g, w_mem_kv, w_branch, w_out, norm2_g, w_up, conv_w, conv_b, w_down, final_g, loss_target, m_norm1_g, m_w_in, m_ln_v_g, m_ln_v_b, m_w_spatial, m_b_spatial, m_lb_logits, m_hgrn_norm_g, m_mem_norm_g, m_w_mem_kv, m_w_branch, m_w_out, m_norm2_g, m_w_up, m_conv_w, m_conv_b, m_w_down, m_final_g, v_norm1_g, v_w_in, v_ln_v_g, v_ln_v_b, v_w_spatial, v_b_spatial, v_lb_logits, v_hgrn_norm_g, v_mem_norm_g, v_w_mem_kv, v_w_branch, v_w_out, v_norm2_g, v_w_up, v_conv_w, v_conv_b, v_w_down, v_final_g):
    w = dict(norm1_g=norm1_g, w_in=w_in, ln_v_g=ln_v_g, ln_v_b=ln_v_b, w_spatial=w_spatial, b_spatial=b_spatial,
             lb_logits=lb_logits, hgrn_norm_g=hgrn_norm_g, mem_norm_g=mem_norm_g, w_mem_kv=w_mem_kv,
             w_branch=w_branch, w_out=w_out, norm2_g=norm2_g, w_up=w_up, conv_w=conv_w, conv_b=conv_b,
             w_down=w_down, final_g=final_g)
    mom = dict(norm1_g=m_norm1_g, w_in=m_w_in, ln_v_g=m_ln_v_g, ln_v_b=m_ln_v_b, w_spatial=m_w_spatial,
               b_spatial=m_b_spatial, lb_logits=m_lb_logits, hgrn_norm_g=m_hgrn_norm_g, mem_norm_g=m_mem_norm_g,
               w_mem_kv=m_w_mem_kv, w_branch=m_w_branch, w_out=m_w_out, norm2_g=m_norm2_g, w_up=m_w_up,
               conv_w=m_conv_w, conv_b=m_conv_b, w_down=m_w_down, final_g=m_final_g)
    var = dict(norm1_g=v_norm1_g, w_in=v_w_in, ln_v_g=v_ln_v_g, ln_v_b=v_ln_v_b, w_spatial=v_w_spatial,
               b_spatial=v_b_spatial, lb_logits=v_lb_logits, hgrn_norm_g=v_hgrn_norm_g, mem_norm_g=v_mem_norm_g,
               w_mem_kv=v_w_mem_kv, w_branch=v_w_branch, w_out=v_w_out, norm2_g=v_norm2_g, w_up=v_w_up,
               conv_w=v_conv_w, conv_b=v_conv_b, w_down=v_w_down, final_g=v_final_g)
    B, S, D = x.shape
    T = B * S
    ci = lax.axis_index("c")
    q = 2 * lax.axis_index("x") + lax.axis_index("y")
    place = jnp.stack([q, ci, 2 * q + ci]).astype(jnp.int32)

    slabs = {n: _cast_into_slab("slab_" + n, w[n].reshape(_BIG_SHARD_SHAPE[n]), place, BF16) for n in _BIG}
    slabs["conv_w"] = _cast_into_slab("slab_conv_w", conv_w[0], place, F32)
    comm = _Comm(slabs, place)
    p = dict(
        norm1_g=norm1_g, ln_v_g=ln_v_g, ln_v_b=ln_v_b, w_spatial=w_spatial[0],
        b_spatial=b_spatial.reshape(GM_GROUPS, GM_CHUNK, 1), lb_logits=lb_logits, hgrn_norm_g=hgrn_norm_g,
        mem_norm_g=mem_norm_g, norm2_g=norm2_g, conv_b=conv_b, final_g=final_g.reshape(1, D))

    loss, grad_x, g = _local_step(x.reshape(T, D), mem.reshape(B * MEM_LEN, D), loss_target.reshape(T, D), p, comm,
                                  B, S)

    shard_grads, local_small, everyone = comm.finish([g["norm1_g"]])
    summed = _sum_small(everyone, local_small, place)
    small_names = list(_SMALL_EARLY) + ["norm1_g"]
    total = dict(zip(_SMALL_EARLY, summed))
    loss_total, total["norm1_g"] = summed[len(_SMALL_EARLY)][0, 0], summed[-1]

    grads, delta, new_m, new_v = {}, {}, {}, {}
    for n in _BIG:
        shp = _BIG_SHARD_SHAPE[n]
        grads[n] = shard_grads[n]
        delta[n], new_m[n], new_v[n] = _adamw("adamw_" + n, w[n].reshape(shp), shard_grads[n],
                                              mom[n].reshape(shp), var[n].reshape(shp))
    cw_shard = D_FF // N_CHIPS
    total["conv_w"] = lax.dynamic_slice(total["conv_w"], (0, q * cw_shard), (3, cw_shard))

    def flat2d(d, n):
        return d[n].reshape(total[n].shape)

    upd = _adamw_small([flat2d(w, n) for n in small_names], [total[n] for n in small_names],
                       [flat2d(mom, n) for n in small_names], [flat2d(var, n) for n in small_names])
    for k, n in enumerate(small_names):
        grads[n], delta[n], new_m[n], new_v[n] = total[n], upd[0][k], upd[1][k], upd[2][k]

    def shaped(d):
        return [d[n].reshape(w[n].shape) for n in _PARAM_ORDER]

    return (loss_total, grad_x.reshape(B, S, D), *shaped(grads), *shaped(delta), *shaped(new_m), *shaped(new_v))
```

```python
import functools
import math

import jax
import jax.numpy as jnp
from jax import lax
from jax.experimental import pallas as pl
from jax.experimental.pallas import tpu as pltpu

F32 = jnp.float32
BF16 = jnp.bfloat16
EPS = 1e-6

D_MODEL = 1024
MEM_LEN = 256
GM_WIDTH = 512
GM_CHUNK = 128
GM_GROUPS = 4
HG_HEADS = 4
HG_DIM = 128
HG_CHUNK = 64
XA_HEADS = 4
XA_DIM = 128
BR_WIDTH = 512
D_FF = 2816
IN_WIDTH = 6656
N_CHIPS = 4
N_DEV = 8

ADAM_LR = 0.001
ADAM_B1 = 0.9
ADAM_B2 = 0.999
ADAM_EPS = 1e-08
ADAM_WD = 0.01
ADAM_STEP = 10

COL_ZU, COL_ZV, COL_HQ, COL_HF, COL_HI, COL_HG, COL_XQ = 0, 1, 2, 3, 4, 5, 6
COL_GATE0 = 3584

VMEM_LIMIT_BYTES = 48 * 1024 * 1024
MESH_ID = pl.DeviceIdType.MESH


def _cp(*sem):
    return pltpu.CompilerParams(dimension_semantics=sem, vmem_limit_bytes=VMEM_LIMIT_BYTES)


def _pallas(body, *, out_shape, **kw):
    def pin(s):
        return pltpu.HBM(s.shape, s.dtype) if isinstance(s, jax.ShapeDtypeStruct) else s

    out_shape = tuple(pin(s) for s in out_shape) if isinstance(out_shape, (tuple, list)) else pin(out_shape)
    call = pl.pallas_call(body, out_shape=out_shape, **kw)

    def run(*operands):
        return call(*[pltpu.with_memory_space_constraint(o, pltpu.HBM) if jnp.issubdtype(o.dtype, jnp.floating)
                      else o for o in operands])

    return run


def _dot(a, b):
    return lax.dot_general(a.astype(BF16), b.astype(BF16), (((1,), (0,)), ((), ())), preferred_element_type=F32)


def _dot_nt(a, b):
    return lax.dot_general(a.astype(BF16), b.astype(BF16), (((1,), (1,)), ((), ())), preferred_element_type=F32)


def _dot_tn(a, b):
    return lax.dot_general(a.astype(BF16), b.astype(BF16), (((0,), (0,)), ((), ())), preferred_element_type=F32)


def _dot_01(mask01, x):
    hi = x.astype(BF16)
    r1 = x - hi.astype(F32)
    mid = r1.astype(BF16)
    lo = (r1 - mid.astype(F32)).astype(BF16)
    m = mask01.astype(BF16)
    dn = (((1,), (0,)), ((), ()))
    return (lax.dot_general(m, hi, dn, preferred_element_type=F32)
            + lax.dot_general(m, mid, dn, preferred_element_type=F32)
            + lax.dot_general(m, lo, dn, preferred_element_type=F32))


def _sigmoid(z):
    return 1.0 / (1.0 + jnp.exp(-z))


_GELU_C = math.sqrt(2.0 / math.pi)


def _gelu_and_grad(z):
    inner = _GELU_C * (z + 0.044715 * z * z * z)
    t = jnp.tanh(inner)
    val = 0.5 * z * (1.0 + t)
    grad = 0.5 * (1.0 + t) + 0.5 * z * (1.0 - t * t) * _GELU_C * (1.0 + 3.0 * 0.044715 * z * z)
    return val, grad


def _row_tile(n, want=512):
    t = min(want, n)
    assert n % t == 0
    return t


def _pcall(body, operands, *, name, grid, in_specs, out_specs, out_shape, scratch_shapes=(), semantics, riders=()):
    single = not isinstance(out_shape, (tuple, list))
    out_specs = (out_specs,) if single else tuple(out_specs)
    out_shape = (out_shape,) if single else tuple(out_shape)
    if not riders:
        res = _pallas(body, name=name, grid=grid, in_specs=list(in_specs), out_specs=out_specs,
                      out_shape=out_shape, scratch_shapes=list(scratch_shapes),
                      compiler_params=_cp(*semantics))(*operands)
        return (res[0] if single else res), []
    n_in, n_out, n_scr = len(in_specs), len(out_shape), len(scratch_shapes)
    ex_in = [len(ex.operands) for ex in riders]
    ex_out = [len(ex.out_shape) for ex in riders]
    ex_scr = [len(ex.scratch) for ex in riders]
    tot_in, tot_out = n_in + sum(ex_in), n_out + sum(ex_out)

    def wrapped(*refs):
        ins, outs, scr = refs[:tot_in], refs[tot_in:tot_in + tot_out], refs[tot_in + tot_out:]
        ids = [pl.program_id(d) for d in range(len(grid))]
        first = functools.reduce(lambda p, t: p & t, [i == 0 for i in ids])
        last = functools.reduce(lambda p, t: p & t, [i == n - 1 for i, n in zip(ids, grid)])
        parts, oi, oo, os_ = [], n_in, n_out, n_scr
        for k in range(len(riders)):
            parts.append((ins[oi:oi + ex_in[k]], outs[oo:oo + ex_out[k]], scr[os_:os_ + ex_scr[k]]))
            oi, oo, os_ = oi + ex_in[k], oo + ex_out[k], os_ + ex_scr[k]

        @pl.when(first)
        def _():
            for ex, part in zip(riders, parts):
                ex.start(*part)

        step, total = 0, 1
        for i, n in zip(ids, grid):
            step, total = step * n + i, total * n
        for ex, part in zip(riders, parts):
            if ex.mid is not None:
                @pl.when(step == min(total - 1, int(total * ex.mid_at)))
                def _(ex=ex, part=part):
                    ex.mid(*part)

        body(*ins[:n_in], *outs[:n_out], *scr[:n_scr])

        @pl.when(last)
        def _():
            for ex, part in zip(riders, parts):
                ex.finish(*part)

    aliases, oi, oo = {}, n_in, n_out
    all_ops, all_shapes, all_scr = list(operands), list(out_shape), list(scratch_shapes)
    for k, ex in enumerate(riders):
        aliases.update({oi + a: oo + b for a, b in ex.aliases.items()})
        oi, oo = oi + ex_in[k], oo + ex_out[k]
        all_ops += list(ex.operands)
        all_shapes += [pltpu.HBM(s.shape, s.dtype) for s in ex.out_shape]
        all_scr += list(ex.scratch)
    res = _pallas(
        wrapped, name=name, grid=grid, in_specs=list(in_specs) + [HBM_SPEC] * sum(ex_in),
        out_specs=out_specs + (HBM_SPEC,) * sum(ex_out), out_shape=tuple(all_shapes), scratch_shapes=all_scr,
        input_output_aliases=aliases, compiler_params=_cp(*(["arbitrary"] * len(grid))))(*all_ops)
    own = res[0] if single else tuple(res[:n_out])
    carried, oo = [], n_out
    for k in range(len(riders)):
        carried.append(list(res[oo:oo + ex_out[k]]))
        oo += ex_out[k]
    return own, carried


def _carried(out, carried, riders):
    return (out, carried) if riders else out


def _matmul(name, operands, *, grid, in_specs, o_spec, out_shape, out_dtype, dims, riders=()):
    nk = grid[2]
    assert nk == 1 or out_dtype == F32

    def body(a_ref, b_ref, o_ref):
        part = lax.dot_general(a_ref[...].astype(BF16), b_ref[...].astype(BF16), (dims, ((), ())),
                               preferred_element_type=F32)
        if nk == 1:
            o_ref[...] = part.astype(o_ref.dtype)
        else:
            k = pl.program_id(2)

            @pl.when(k == 0)
            def _():
                o_ref[...] = part

            @pl.when(k > 0)
            def _():
                o_ref[...] += part

    out, carried = _pcall(body, operands, name=name, grid=grid, in_specs=in_specs, out_specs=o_spec,
                          out_shape=jax.ShapeDtypeStruct(out_shape, out_dtype),
                          semantics=("parallel", "parallel", "arbitrary"), riders=riders)
    return (out, carried) if riders else out


NN = ((1,), (0,))
NT = ((1,), (1,))
TN = ((0,), (0,))
_TN_TOKENS = 4096


def _mm_cs(name, a, w, out_dtype, riders=()):
    M, K = a.shape
    nq, _, wd = w.shape
    tm = _row_tile(M)
    return _matmul(name, (a, w), grid=(nq, M // tm, 1),
                   in_specs=[pl.BlockSpec((tm, K), lambda j, i, k: (i, 0)),
                             pl.BlockSpec((None, K, wd), lambda j, i, k: (j, 0, 0))],
                   o_spec=pl.BlockSpec((tm, wd), lambda j, i, k: (i, j)),
                   out_shape=(M, nq * wd), out_dtype=out_dtype, dims=NN, riders=riders)


def _mm_rs(name, a, w, out_dtype):
    M, K = a.shape
    N = w.shape[1]
    tm = _row_tile(M)
    return _matmul(name, (a, w), grid=(M // tm, 1, 1),
                   in_specs=[pl.BlockSpec((tm, K), lambda i, j, k: (i, 0)), pl.BlockSpec((K, N), lambda i, j, k: (0, 0))],
                   o_spec=pl.BlockSpec((tm, N), lambda i, j, k: (i, 0)),
                   out_shape=(M, N), out_dtype=out_dtype, dims=NN)


def _mm_nt_rs(name, g, w, out_dtype, riders=()):
    M, N = g.shape
    K = w.shape[0]
    to = K
    tm = _row_tile(M)
    return _matmul(name, (g, w), grid=(M // tm, K // to, 1),
                   in_specs=[pl.BlockSpec((tm, N), lambda i, j, k: (i, 0)),
                             pl.BlockSpec((to, N), lambda i, j, k: (j, 0))],
                   o_spec=pl.BlockSpec((tm, to), lambda i, j, k: (i, j)),
                   out_shape=(M, K), out_dtype=out_dtype, dims=NT, riders=riders)


def _mm_nt_cs(name, g, w, out_dtype, riders=(), stacked=False, norm_bwd=None):
    M = g.shape[-2]
    nq, K, wd = w.shape
    tm = _row_tile(M, 256)

    def product(g_ref, w_ref):
        acc = None
        for q in range(nq):
            gq = g_ref[q // 2, :, (q % 2) * wd:(q % 2 + 1) * wd] if stacked else g_ref[:, q * wd:(q + 1) * wd]
            part = _dot_nt(gq, w_ref[q])
            acc = part if acc is None else acc + part
        return acc

    def body(g_ref, w_ref, o_ref):
        o_ref[...] = product(g_ref, w_ref).astype(o_ref.dtype)

    def body_norm(g_ref, w_ref, x_ref, gain_ref, dr_ref, dx_ref, dg_ref):
        @pl.when(pl.program_id(0) == 0)
        def _():
            dg_ref[...] = jnp.zeros_like(dg_ref)

        dx, dg = _rms_bwd_rows(x_ref[...], gain_ref[...], product(g_ref, w_ref))
        dg_ref[...] += dg
        dx_ref[...] = dx + dr_ref[...]

    g_spec = (pl.BlockSpec((2, tm, 2 * wd), lambda i: (0, i, 0)) if stacked
              else pl.BlockSpec((tm, nq * wd), lambda i: (i, 0)))
    w_spec = pl.BlockSpec((nq, K, wd), lambda i: (0, 0, 0))
    row = pl.BlockSpec((tm, K), lambda i: (i, 0))
    if norm_bwd is None:
        return _carried(*_pcall(
            body, (g, w), name=name, grid=(M // tm,), in_specs=[g_spec, w_spec], out_specs=row,
            out_shape=jax.ShapeDtypeStruct((M, K), out_dtype), semantics=("parallel",), riders=riders), riders)
    vec = pl.BlockSpec((1, K), lambda i: (0, 0))
    return _carried(*_pcall(
        body_norm, (g, w) + tuple(norm_bwd), name=name, grid=(M // tm,),
        in_specs=[g_spec, w_spec, row, vec, row], out_specs=(row, vec),
        out_shape=(jax.ShapeDtypeStruct((M, K), F32), jax.ShapeDtypeStruct((1, K), F32)),
        semantics=("arbitrary",), riders=riders), riders)


def _mm_tn_rs(name, a, g, to, tn=512):
    T, M = a.shape
    N = g.shape[1]
    tt = _row_tile(T, _TN_TOKENS)
    tn = min(tn, N)
    return _matmul(name, (a, g), grid=(M // to, N // tn, T // tt),
                   in_specs=[pl.BlockSpec((tt, to), lambda i, j, k: (k, i)),
                             pl.BlockSpec((tt, tn), lambda i, j, k: (k, j))],
                   o_spec=pl.BlockSpec((to, tn), lambda i, j, k: (i, j)),
                   out_shape=(M, N), out_dtype=F32, dims=TN)


def _mm_tn_cs(name, a, g, nq, to, riders=(), stacked=False):
    T, M = a.shape
    wd = g.shape[-1] * (2 if stacked else 1) // nq
    tt = _row_tile(T, _TN_TOKENS)
    g_spec = (pl.BlockSpec((None, tt, wd), lambda i, j, k: (j // 2, k, j % 2)) if stacked
              else pl.BlockSpec((tt, wd), lambda i, j, k: (k, j)))
    return _matmul(name, (a, g), grid=(M // to, nq, T // tt),
                   in_specs=[pl.BlockSpec((tt, to), lambda i, j, k: (k, i)), g_spec],
                   o_spec=pl.BlockSpec((None, to, wd), lambda i, j, k: (j, i, 0)),
                   out_shape=(nq, M, wd), out_dtype=F32, dims=TN, riders=riders)


def _rms_fwd(name, x, g, riders=()):
    T, D = x.shape
    tm = _row_tile(T)

    def body(x_ref, g_ref, o_ref):
        o_ref[...] = _rms_rows(x_ref[...], g_ref[...]).astype(o_ref.dtype)

    return _carried(*_pcall(
        body, (x, g), name=name, grid=(T // tm,),
        in_specs=[pl.BlockSpec((tm, D), lambda i: (i, 0)), pl.BlockSpec((1, D), lambda i: (0, 0))],
        out_specs=pl.BlockSpec((tm, D), lambda i: (i, 0)),
        out_shape=jax.ShapeDtypeStruct((T, D), BF16), semantics=("parallel",), riders=riders), riders)


def _rms_rows(xv, gain):
    return xv * lax.rsqrt(jnp.mean(xv * xv, axis=-1, keepdims=True) + EPS) * gain


def _rms_bwd_rows(xv, gain, dh):
    r = lax.rsqrt(jnp.mean(xv * xv, axis=-1, keepdims=True) + EPS)
    n = xv * r
    dn = dh * gain
    return r * (dn - n * jnp.mean(dn * n, axis=-1, keepdims=True)), jnp.sum(dh * n, axis=0, keepdims=True)


def _rms_bwd(name, x, g, dh, dres):
    T, D = x.shape
    tm = _row_tile(T)
    has_res = dres is not None

    def body(*refs):
        if has_res:
            x_ref, g_ref, dh_ref, dr_ref, dx_ref, dg_ref = refs
        else:
            x_ref, g_ref, dh_ref, dx_ref, dg_ref = refs

        @pl.when(pl.program_id(0) == 0)
        def _():
            dg_ref[...] = jnp.zeros_like(dg_ref)

        dx, dg = _rms_bwd_rows(x_ref[...], g_ref[...], dh_ref[...])
        dg_ref[...] += dg
        if has_res:
            dx = dx + dr_ref[...]
        dx_ref[...] = dx

    row = pl.BlockSpec((tm, D), lambda i: (i, 0))
    vec = pl.BlockSpec((1, D), lambda i: (0, 0))
    ops = (x, g, dh, dres) if has_res else (x, g, dh)
    return _pallas(
        body, name=name, grid=(T // tm,), in_specs=[row, vec, row] + ([row] if has_res else []),
        out_specs=(row, vec),
        out_shape=(jax.ShapeDtypeStruct((T, D), F32), jax.ShapeDtypeStruct((1, D), F32)),
        compiler_params=_cp("arbitrary"),
    )(*ops)


def _proj_res_norm(name, a, w, res, gain):
    M, K = a.shape
    N = w.shape[1]
    tm = _row_tile(M)

    def body(a_ref, w_ref, r_ref, g_ref, x_ref, h_ref):
        xv = _dot(a_ref[...], w_ref[...]) + r_ref[...]
        x_ref[...] = xv
        h_ref[...] = _rms_rows(xv, g_ref[...]).astype(h_ref.dtype)

    row = pl.BlockSpec((tm, N), lambda i: (i, 0))
    return _pallas(
        body, name=name, grid=(M // tm,),
        in_specs=[pl.BlockSpec((tm, K), lambda i: (i, 0)), pl.BlockSpec((K, N), lambda i: (0, 0)), row,
                  pl.BlockSpec((1, N), lambda i: (0, 0))],
        out_specs=(row, row), out_shape=(jax.ShapeDtypeStruct((M, N), F32), jax.ShapeDtypeStruct((M, N), BF16)),
        compiler_params=_cp("parallel"),
    )(a, w, res, gain)


def _proj_res_loss(name, a, w, res, tgt, gain):
    M, K = a.shape
    D = w.shape[1]
    tm = _row_tile(M)

    def body(a_ref, w_ref, r_ref, t_ref, g_ref, dx_ref, dg_ref, loss_ref):
        @pl.when(pl.program_id(0) == 0)
        def _():
            dg_ref[...] = jnp.zeros_like(dg_ref)
            loss_ref[...] = jnp.zeros_like(loss_ref)

        xv = _dot(a_ref[...], w_ref[...]) + r_ref[...]
        gv = g_ref[...]
        diff = _rms_rows(xv, gv) - t_ref[...]
        loss_ref[...] += 0.5 * jnp.sum(jnp.mean(diff * diff, axis=-1, keepdims=True))
        dx, dg = _rms_bwd_rows(xv, gv, diff * (1.0 / D))
        dg_ref[...] += dg
        dx_ref[...] = dx

    row = pl.BlockSpec((tm, D), lambda i: (i, 0))
    vec = pl.BlockSpec((1, D), lambda i: (0, 0))
    return _pallas(
        body, name=name, grid=(M // tm,),
        in_specs=[pl.BlockSpec((tm, K), lambda i: (i, 0)), pl.BlockSpec((K, D), lambda i: (0, 0)), row, row, vec],
        out_specs=(row, vec, pl.BlockSpec((8, 128), lambda i: (0, 0))),
        out_shape=(jax.ShapeDtypeStruct((M, D), F32), jax.ShapeDtypeStruct((1, D), F32),
                   jax.ShapeDtypeStruct((8, 128), F32)),
        compiler_params=_cp("arbitrary"),
    )(a, w, res, tgt, gain)


def _gmlp_pieces(zu, zv, lng, lnb, ws_ref, bs_ref):
    u, du = _gelu_and_grad(zu)
    v, dv = _gelu_and_grad(zv)
    mu = jnp.mean(v, axis=-1, keepdims=True)
    vc = v - mu
    rstd = lax.rsqrt(jnp.mean(vc * vc, axis=-1, keepdims=True) + EPS)
    vhat = vc * rstd
    vn = vhat * lng + lnb
    row = lax.broadcasted_iota(jnp.int32, (GM_CHUNK, GM_CHUNK), 0)
    col = lax.broadcasted_iota(jnp.int32, (GM_CHUNK, GM_CHUNK), 1)
    tril = row >= col
    wms, mixed = [], []
    for g in range(GM_GROUPS):
        sl = slice(g * 128, (g + 1) * 128)
        wm = jnp.where(tril, ws_ref[g], 0.0)
        wms.append(wm)
        mixed.append(_dot(wm, vn[:, sl]) + bs_ref[g])
    return u, du, dv, rstd, vhat, vn, wms, mixed, tril


def _gmlp_fwd(proj, lng, lnb, ws, bs_col):
    T = proj.shape[0]
    n = T // GM_CHUNK

    def body(zu_ref, zv_ref, lng_ref, lnb_ref, ws_ref, bs_ref, o_ref):
        u, _, _, _, _, _, _, mixed, _ = _gmlp_pieces(zu_ref[...].astype(F32), zv_ref[...].astype(F32),
                                                     lng_ref[...], lnb_ref[...],
                                                     ws_ref, bs_ref)
        for g in range(GM_GROUPS):
            sl = slice(g * 128, (g + 1) * 128)
            o_ref[:, sl] = (u[:, sl] * mixed[g]).astype(o_ref.dtype)

    vec = pl.BlockSpec((1, GM_WIDTH), lambda i: (0, 0))
    return _pallas(
        body, name="gmlp_fwd", grid=(n,),
        in_specs=[pl.BlockSpec((GM_CHUNK, 512), lambda i: (i, COL_ZU)),
                  pl.BlockSpec((GM_CHUNK, 512), lambda i: (i, COL_ZV)),
                  vec, vec,
                  pl.BlockSpec((GM_GROUPS, 128, 128), lambda i: (0, 0, 0)),
                  pl.BlockSpec((GM_GROUPS, 128, 1), lambda i: (0, 0, 0))],
        out_specs=pl.BlockSpec((GM_CHUNK, 512), lambda i: (i, 0)),
        out_shape=jax.ShapeDtypeStruct((T, GM_WIDTH), BF16), compiler_params=_cp("parallel"),
    )(proj, proj, lng, lnb, ws, bs_col)


def _gmlp_bwd(proj, d_out, lng, lnb, ws, bs_col, riders=()):
    T = proj.shape[0]
    n = T // GM_CHUNK

    def body(zu_ref, zv_ref, do_ref, lng_ref, lnb_ref, ws_ref, bs_ref,
             dz_ref, dws_ref, dbs_ref, dlng_ref, dlnb_ref, dm_acc):
        i = pl.program_id(0)

        @pl.when(i == 0)
        def _():
            dws_ref[...] = jnp.zeros_like(dws_ref)
            dlng_ref[...] = jnp.zeros_like(dlng_ref)
            dlnb_ref[...] = jnp.zeros_like(dlnb_ref)
            dm_acc[...] = jnp.zeros_like(dm_acc)

        lng_v = lng_ref[...]
        u, du, dv, rstd, vhat, vn, wms, mixed, tril = _gmlp_pieces(zu_ref[...].astype(F32), zv_ref[...].astype(F32),
                                                                  lng_v, lnb_ref[...],
                                                                  ws_ref, bs_ref)
        do = do_ref[...]
        dvn_parts = []
        for g in range(GM_GROUPS):
            sl = slice(g * 128, (g + 1) * 128)
            dog = do[:, sl]
            dz_ref[:, sl] = (dog * mixed[g] * du[:, sl]).astype(dz_ref.dtype)
            dmix = dog * u[:, sl]
            dm_acc[:, sl] += dmix
            dws_ref[g] += jnp.where(tril, _dot_nt(dmix, vn[:, sl]), 0.0)
            dvn_parts.append(_dot_tn(wms[g], dmix))
        dvn = jnp.concatenate(dvn_parts, axis=1)
        dlng_ref[...] += jnp.sum(dvn * vhat, axis=0, keepdims=True)
        dlnb_ref[...] += jnp.sum(dvn, axis=0, keepdims=True)
        dvh = dvn * lng_v
        dvv = rstd * (dvh - jnp.mean(dvh, axis=-1, keepdims=True)
                      - vhat * jnp.mean(dvh * vhat, axis=-1, keepdims=True))
        dz_ref[:, GM_WIDTH:] = (dvv * dv).astype(dz_ref.dtype)

        @pl.when(i == n - 1)
        def _():
            for g in range(GM_GROUPS):
                dbs_ref[g] = jnp.sum(dm_acc[:, g * 128:(g + 1) * 128], axis=1, keepdims=True)

    vec = pl.BlockSpec((1, GM_WIDTH), lambda i: (0, 0))
    wsp = pl.BlockSpec((GM_GROUPS, 128, 128), lambda i: (0, 0, 0))
    bsp = pl.BlockSpec((GM_GROUPS, 128, 1), lambda i: (0, 0, 0))
    return _carried(*_pcall(
        body, (proj, proj, d_out, lng, lnb, ws, bs_col), name="gmlp_bwd", grid=(n,),
        in_specs=[pl.BlockSpec((GM_CHUNK, 512), lambda i: (i, COL_ZU)),
                  pl.BlockSpec((GM_CHUNK, 512), lambda i: (i, COL_ZV)),
                  pl.BlockSpec((None, GM_CHUNK, 512), lambda i: (0, i, 0)), vec, vec, wsp, bsp],
        out_specs=(pl.BlockSpec((GM_CHUNK, 2 * GM_WIDTH), lambda i: (i, 0)), wsp, bsp, vec, vec),
        out_shape=(jax.ShapeDtypeStruct((T, 2 * GM_WIDTH), BF16),
                   jax.ShapeDtypeStruct((GM_GROUPS, 128, 128), F32), jax.ShapeDtypeStruct((GM_GROUPS, 128, 1), F32),
                   jax.ShapeDtypeStruct((1, GM_WIDTH), F32), jax.ShapeDtypeStruct((1, GM_WIDTH), F32)),
        scratch_shapes=[pltpu.VMEM((GM_CHUNK, GM_WIDTH), F32)],
        semantics=("arbitrary",), riders=riders), riders)


def _hgrn_lower_bound(lbl):
    return 1.0 / (1.0 + jnp.exp(lbl[1:2, :] - lbl[0:1, :]))


def _hgrn_gates(hq, hf, lb):
    C = HG_CHUNK
    sg = _sigmoid(hf)
    fg = lb + (1.0 - lb) * sg
    sq = _sigmoid(hq)
    row = lax.broadcasted_iota(jnp.int32, (C, C), 0)
    col = lax.broadcasted_iota(jnp.int32, (C, C), 1)
    tril = row >= col
    logf = jnp.log(fg)
    a = _dot_01(tril, logf)
    a_last = jnp.sum(logf, axis=0, keepdims=True)
    first_half = lax.broadcasted_iota(jnp.int32, logf.shape, 0) < (C // 2)
    a_mid = jnp.sum(jnp.where(first_half, logf, 0.0), axis=0, keepdims=True)
    ea, ei, eki, ekl = jnp.exp(a), jnp.exp(a - a_mid), jnp.exp(a_mid - a), jnp.exp(a_last - a)
    k = 1.0 - fg
    q = hq * sq
    qi = (q * ei).astype(BF16).astype(F32)
    ki = (k * eki).astype(BF16).astype(F32)
    return dict(sg=sg, fg=fg, sq=sq, tril=tril, ea=ea, ei=ei, eki=eki, ekl=ekl, e_last=jnp.exp(a_last),
                qe=q * ea, qi=qi, ki=ki, kl=k * ekl)


def _heads(x):
    return [x[:, h * HG_DIM:(h + 1) * HG_DIM] for h in range(HG_HEADS)]


def _hgrn_fwd(proj, lbl, gh, B, S, riders=()):
    C = HG_CHUNK
    NC = S // C
    W = HG_HEADS * HG_DIM

    def body(q_ref, f_ref, i_ref, g_ref, lbl_ref, gh_ref, o_ref, bo_ref, st_ref, state):
        @pl.when(pl.program_id(0) == 0)
        def _():
            state[...] = jnp.zeros_like(state)

        lb = _hgrn_lower_bound(lbl_ref[...])
        ghv = gh_ref[...]
        for b in range(B):
            gt = _hgrn_gates(q_ref[b].astype(F32), f_ref[b].astype(F32), lb)
            v = _heads(i_ref[b])
            qe, qi, ki, kl, e_last = (_heads(gt[n]) for n in ("qe", "qi", "ki", "kl", "e_last"))
            outs, normed = [], []
            for h in range(HG_HEADS):
                p = jnp.where(gt["tril"], _dot_nt(qi[h], ki[h]), 0.0)
                st = state[b, h]
                st_ref[b, h] = st
                o = _dot_nt(qe[h], st) + _dot(p, v[h])
                state[b, h] = st * e_last[h] + _dot_tn(v[h], kl[h])
                outs.append(o)
                normed.append(o * lax.rsqrt(jnp.mean(o * o, axis=-1, keepdims=True) + EPS) * ghv)
            o_ref[b] = jnp.concatenate(outs, axis=1)
            hg = g_ref[b].astype(F32)
            bo_ref[b] = (jnp.concatenate(normed, axis=1) * (hg * _sigmoid(hg))).astype(bo_ref.dtype)

    def col(cb):
        return pl.BlockSpec((B, C, 512), lambda c: (0, c, cb))

    tile = pl.BlockSpec((B, C, W), lambda c: (0, c, 0))
    proj3 = proj.reshape(B, S, proj.shape[-1])
    out, carried = _pcall(
        body, (proj3, proj3, proj3, proj3, lbl, gh), name="hgrn_fwd", grid=(NC,),
        in_specs=[col(COL_HQ), col(COL_HF), col(COL_HI), col(COL_HG),
                  pl.BlockSpec((2, W), lambda c: (0, 0)), pl.BlockSpec((1, HG_DIM), lambda c: (0, 0))],
        out_specs=(tile, tile, pl.BlockSpec((B, None, HG_HEADS, 128, 128), lambda c: (0, c, 0, 0, 0))),
        out_shape=(jax.ShapeDtypeStruct((B, S, W), F32), jax.ShapeDtypeStruct((B, S, W), BF16),
                   jax.ShapeDtypeStruct((B, NC, HG_HEADS, 128, 128), F32)),
        scratch_shapes=[pltpu.VMEM((B, HG_HEADS, 128, 128), F32)],
        semantics=("arbitrary",), riders=riders)
    o_h, b_out, states = out
    out = (o_h, b_out.reshape(B * S, W), states)
    return (out, carried) if riders else out


def _hgrn_bwd(proj, o_saved, states, d_out, lbl, gh, others, B, S, riders=()):
    C = HG_CHUNK
    NC = S // C
    W = HG_HEADS * HG_DIM
    d_gm, d_xq, d_gates = (t.reshape(B, S, t.shape[-1]) for t in others)
    own0 = d_gm.shape[-1]
    xq0 = own0 + 4 * W
    gates0 = xq0 + d_xq.shape[-1]

    def body(q_ref, f_ref, i_ref, g_ref, o_ref, st_ref, do_ref, lbl_ref, gh_ref, gm_ref, xq_ref, gates_ref,
             d_ref, dlbl_ref, dgh_ref, dstate, dlb_acc):
        c = pl.program_id(0)
        d_ref[:, :, :own0] = gm_ref[...]
        d_ref[:, :, xq0:gates0] = xq_ref[...]
        d_ref[:, :, gates0:] = gates_ref[...]

        def put(b, k, val):
            d_ref[b, :, own0 + k * W:own0 + (k + 1) * W] = val.astype(d_ref.dtype)

        @pl.when(c == 0)
        def _():
            dstate[...] = jnp.zeros_like(dstate)
            dgh_ref[...] = jnp.zeros_like(dgh_ref)
            dlb_acc[...] = jnp.zeros_like(dlb_acc)

        lb = _hgrn_lower_bound(lbl_ref[...])
        ghv = gh_ref[...]
        row = lax.broadcasted_iota(jnp.int32, (C, C), 0)
        colm = lax.broadcasted_iota(jnp.int32, (C, C), 1)
        triu = colm >= row
        for b in range(B):
            hq, hg = q_ref[b].astype(F32), g_ref[b].astype(F32)
            gt = _hgrn_gates(hq, f_ref[b].astype(F32), lb)
            tril = gt["tril"]
            v = _heads(i_ref[b])
            qe, qi, ki, kl, e_last = (_heads(gt[n]) for n in ("qe", "qi", "ki", "kl", "e_last"))
            sgg = _sigmoid(hg)
            don_all = do_ref[b] * (hg * sgg)
            o, don = _heads(o_ref[b]), _heads(don_all)
            d_qe, d_qi, d_ki, d_kl, dv, n_all, dal = [], [], [], [], [], [], []
            for h in range(HG_HEADS):
                r = lax.rsqrt(jnp.mean(o[h] * o[h], axis=-1, keepdims=True) + EPS)
                n = o[h] * r
                n_all.append(n)
                dgh_ref[...] += jnp.sum(don[h] * n, axis=0, keepdims=True)
                dn = don[h] * ghv
                d_o = r * (dn - n * jnp.mean(dn * n, axis=-1, keepdims=True))
                st, dst = st_ref[b, h], dstate[b, h]
                p = jnp.where(tril, _dot_nt(qi[h], ki[h]), 0.0)
                dp = jnp.where(tril, _dot_nt(d_o, v[h]), 0.0)
                d_qe.append(_dot(d_o, st))
                d_qi.append(_dot(dp, ki[h]))
                d_ki.append(_dot_tn(dp, qi[h]))
                d_kl.append(_dot(v[h], dst))
                dv.append(_dot_tn(p, d_o) + _dot_nt(kl[h], dst))
                dstate[b, h] = dst * e_last[h] + _dot_tn(d_o, qe[h])
                dal.append(jnp.sum(dst * st, axis=0, keepdims=True) * e_last[h])
            d_qe, d_qi, d_ki, d_kl, n_all, dal = (jnp.concatenate(t, axis=1)
                                                  for t in (d_qe, d_qi, d_ki, d_kl, n_all, dal))
            put(b, 3, do_ref[b] * n_all * jnp.tile(ghv, (1, HG_HEADS)) * (sgg * (1.0 + hg * (1.0 - sgg))))
            put(b, 2, jnp.concatenate(dv, axis=1))
            d_a_last = dal + jnp.sum(d_kl * gt["kl"], axis=0, keepdims=True)
            dq = d_qe * gt["ea"] + d_qi * gt["ei"]
            dk = d_ki * gt["eki"] + d_kl * gt["ekl"]
            da = d_qe * gt["qe"] + d_qi * gt["qi"] - d_ki * gt["ki"] - d_kl * gt["kl"]
            dlogf = _dot_01(triu, da) + d_a_last
            sg, sq = gt["sg"], gt["sq"]
            dfg = dlogf / gt["fg"] - dk
            put(b, 1, dfg * (1.0 - lb) * sg * (1.0 - sg))
            dlb_acc[...] += jnp.sum(dfg * (1.0 - sg), axis=0, keepdims=True)
            put(b, 0, dq * (sq * (1.0 + hq * (1.0 - sq))))

        @pl.when(c == NC - 1)
        def _():
            dlb = dlb_acc[...]
            first = lax.broadcasted_iota(jnp.int32, (2, W), 0) == 0
            dlbl_ref[...] = jnp.where(first, dlb * lb * (1.0 - lb), -dlb * lb * (1.0 - lb))

    def col(cb):
        return pl.BlockSpec((B, C, 512), lambda c: (0, NC - 1 - c, cb))

    tile = pl.BlockSpec((B, C, W), lambda c: (0, NC - 1 - c, 0))
    proj3 = proj.reshape(B, S, proj.shape[-1])

    def rows(width):
        return pl.BlockSpec((B, C, width), lambda c: (0, NC - 1 - c, 0))

    width = proj.shape[-1]
    out, carried = _pcall(
        body, (proj3, proj3, proj3, proj3, o_saved, states, d_out.reshape(3, B, S, W), lbl, gh, d_gm, d_xq, d_gates),
        name="hgrn_bwd", grid=(NC,),
        in_specs=[col(COL_HQ), col(COL_HF), col(COL_HI), col(COL_HG), tile,
                  pl.BlockSpec((B, None, HG_HEADS, 128, 128), lambda c: (0, NC - 1 - c, 0, 0, 0)),
                  pl.BlockSpec((None, B, C, W), lambda c: (1, 0, NC - 1 - c, 0)),
                  pl.BlockSpec((2, W), lambda c: (0, 0)), pl.BlockSpec((1, HG_DIM), lambda c: (0, 0)),
                  rows(d_gm.shape[-1]), rows(d_xq.shape[-1]), rows(d_gates.shape[-1])],
        out_specs=(rows(width), pl.BlockSpec((2, W), lambda c: (0, 0)), pl.BlockSpec((1, HG_DIM), lambda c: (0, 0))),
        out_shape=(jax.ShapeDtypeStruct((B, S, width), BF16), jax.ShapeDtypeStruct((2, W), F32),
                   jax.ShapeDtypeStruct((1, HG_DIM), F32)),
        scratch_shapes=[pltpu.VMEM((B, HG_HEADS, 128, 128), F32), pltpu.VMEM((1, W), F32)],
        semantics=("arbitrary",), riders=riders)
    out = (out[0].reshape(B * S, width),) + tuple(out[1:])
    return (out, carried) if riders else out


_XA_SCALE = XA_DIM ** -0.5


def _attn_probs(qh, kh):
    s = _dot_nt(qh, kh) * _XA_SCALE
    e = jnp.exp(s - jnp.max(s, axis=-1, keepdims=True))
    return e / jnp.sum(e, axis=-1, keepdims=True)


def _attn_fwd(proj, kv, B, S):
    T = B * S
    tq = _row_tile(S)
    nq = S // tq
    W = XA_HEADS * XA_DIM

    def body(q_ref, kv_ref, o_ref):
        for h in range(XA_HEADS):
            sl = slice(h * 128, (h + 1) * 128)
            p = _attn_probs(q_ref[:, sl], kv_ref[:, sl])
            o_ref[:, sl] = _dot(p, kv_ref[:, W + h * 128:W + (h + 1) * 128]).astype(o_ref.dtype)

    return _pallas(
        body, name="attn_fwd", grid=(B, nq),
        in_specs=[pl.BlockSpec((tq, 512), lambda b, i: (b * nq + i, COL_XQ)),
                  pl.BlockSpec((MEM_LEN, 2 * W), lambda b, i: (b, 0))],
        out_specs=pl.BlockSpec((tq, W), lambda b, i: (b * nq + i, 0)),
        out_shape=jax.ShapeDtypeStruct((T, W), BF16), compiler_params=_cp("parallel", "parallel"),
    )(proj, kv)


def _attn_bwd(proj, kv, d_out, B, S):
    T = B * S
    tq = _row_tile(S)
    nq = S // tq
    W = XA_HEADS * XA_DIM

    def body(q_ref, kv_ref, do_ref, dq_ref, dkv_ref):
        @pl.when(pl.program_id(1) == 0)
        def _():
            dkv_ref[...] = jnp.zeros_like(dkv_ref)

        for h in range(XA_HEADS):
            sl = slice(h * 128, (h + 1) * 128)
            slv = slice(W + h * 128, W + (h + 1) * 128)
            qh = q_ref[:, sl]
            kh = kv_ref[:, sl]
            p = _attn_probs(qh, kh)
            dc = do_ref[:, sl]
            dp = _dot_nt(dc, kv_ref[:, slv])
            ds = p * (dp - jnp.sum(dp * p, axis=-1, keepdims=True)) * _XA_SCALE
            dq_ref[:, sl] = _dot(ds, kh).astype(dq_ref.dtype)
            dkv_ref[:, sl] += _dot_tn(ds, qh)
            dkv_ref[:, slv] += _dot_tn(p, dc)

    kvspec = pl.BlockSpec((MEM_LEN, 2 * W), lambda b, i: (b, 0))
    tile = pl.BlockSpec((tq, W), lambda b, i: (b * nq + i, 0))
    return _pallas(
        body, name="attn_bwd", grid=(B, nq),
        in_specs=[pl.BlockSpec((tq, 512), lambda b, i: (b * nq + i, COL_XQ)), kvspec,
                  pl.BlockSpec((None, tq, W), lambda b, i: (2, b * nq + i, 0))],
        out_specs=(tile, kvspec),
        out_shape=(jax.ShapeDtypeStruct((T, W), BF16), jax.ShapeDtypeStruct((B * MEM_LEN, 2 * W), F32)),
        compiler_params=_cp("parallel", "arbitrary"),
    )(proj, kv, d_out)


_MERGE_TM = 256
_GATE_W = 512


def _gate_specs(tm):
    base = COL_GATE0 // _GATE_W
    return [pl.BlockSpec((tm, _GATE_W), functools.partial(lambda i, k: (i, base + k), k=k)) for k in range(6)]


def _merge_fwd(a_out, b_out, c_out, wb, proj, riders=()):
    T = a_out.shape[0]
    tm = _row_tile(T, _MERGE_TM)
    nq, _, wd = wb.shape
    per_half = _GATE_W // wd

    def body(a_ref, b_ref, c_ref, w_ref, *rest):
        gates, (m_ref, up_ref) = rest[:6], rest[6:]
        for hf in range(2):
            cols = slice(hf * _GATE_W, (hf + 1) * _GATE_W)
            acc = None
            for n, br in enumerate((a_ref, b_ref, c_ref)):
                x = br[...]
                up = jnp.concatenate([_dot(x, w_ref[per_half * hf + j, n * BR_WIDTH:(n + 1) * BR_WIDTH, :])
                                      for j in range(per_half)], axis=1)
                up_ref[n, :, cols] = up.astype(up_ref.dtype)
                term = _sigmoid(gates[2 * n + hf][...].astype(F32)) * up
                acc = term if acc is None else acc + term
            m_ref[:, cols] = acc.astype(m_ref.dtype)

    br_spec = pl.BlockSpec((tm, BR_WIDTH), lambda i: (i, 0))
    return _carried(*_pcall(
        body, (a_out, b_out, c_out, wb, *([proj] * 6)), name="merge_fwd", grid=(T // tm,),
        in_specs=[br_spec, br_spec, br_spec,
                  pl.BlockSpec((nq, 3 * BR_WIDTH, wd), lambda i: (0, 0, 0))] + _gate_specs(tm),
        out_specs=(pl.BlockSpec((tm, D_MODEL), lambda i: (i, 0)), pl.BlockSpec((3, tm, D_MODEL), lambda i: (0, i, 0))),
        out_shape=(jax.ShapeDtypeStruct((T, D_MODEL), BF16), jax.ShapeDtypeStruct((3, T, D_MODEL), BF16)),
        semantics=("parallel",), riders=riders), riders)


def _branch_bwd_act(d_ups, wb, riders=()):
    _, T, D = d_ups.shape
    nq, _, wd = wb.shape
    tm = _row_tile(T)

    def body(d_ref, w_ref, o_ref):
        acc = None
        for q in range(nq):
            part = _dot_nt(d_ref[:, q * wd:(q + 1) * wd], w_ref[q])
            acc = part if acc is None else acc + part
        o_ref[...] = acc

    return _carried(*_pcall(
        body, (d_ups, wb), name="d_branch", grid=(3, T // tm),
        in_specs=[pl.BlockSpec((None, tm, D), lambda n, i: (n, i, 0)),
                  pl.BlockSpec((nq, BR_WIDTH, wd), lambda n, i: (0, n, 0))],
        out_specs=pl.BlockSpec((None, tm, BR_WIDTH), lambda n, i: (n, i, 0)),
        out_shape=jax.ShapeDtypeStruct((3, T, BR_WIDTH), F32), semantics=("parallel", "parallel"),
        riders=riders), riders)


def _branch_bwd_weight(name, br, d_ups, n):
    T = br.shape[0]
    D = d_ups.shape[2]
    wd = D // N_CHIPS
    tt = _row_tile(T, _TN_TOKENS)

    def body(b_ref, d_ref, o_ref):
        k = pl.program_id(0)
        for q in range(N_CHIPS):
            part = _dot_tn(b_ref[...], d_ref[:, q * wd:(q + 1) * wd])

            @pl.when(k == 0)
            def _():
                o_ref[q] = part

            @pl.when(k > 0)
            def _():
                o_ref[q] += part

    return _pallas(
        body, name=name, grid=(T // tt,),
        in_specs=[pl.BlockSpec((tt, BR_WIDTH), lambda k: (k, 0)),
                  pl.BlockSpec((None, tt, D), lambda k: (n, k, 0))],
        out_specs=pl.BlockSpec((N_CHIPS, BR_WIDTH, wd), lambda k: (0, 0, 0)),
        out_shape=jax.ShapeDtypeStruct((N_CHIPS, BR_WIDTH, wd), F32), compiler_params=_cp("arbitrary"),
    )(br, d_ups)


def _merge_bwd(d_merged, ups, proj, riders=()):
    T = d_merged.shape[0]
    tm = _row_tile(T, _MERGE_TM)

    def body(dm_ref, up_ref, *rest):
        gates, (dup_ref, dg_ref) = rest[:6], rest[6:]
        for hf in range(2):
            cols = slice(hf * _GATE_W, (hf + 1) * _GATE_W)
            dm = dm_ref[:, cols]
            for n in range(3):
                gate = _sigmoid(gates[2 * n + hf][...].astype(F32))
                dup_ref[n, :, cols] = (dm * gate).astype(dup_ref.dtype)
                dg_ref[:, n * D_MODEL + hf * _GATE_W:n * D_MODEL + (hf + 1) * _GATE_W] = (
                    dm * up_ref[n, :, cols].astype(F32) * gate * (1.0 - gate)).astype(dg_ref.dtype)

    tile = pl.BlockSpec((tm, D_MODEL), lambda i: (i, 0))
    tile3 = pl.BlockSpec((3, tm, D_MODEL), lambda i: (0, i, 0))
    return _carried(*_pcall(
        body, (d_merged, ups, *([proj] * 6)), name="merge_bwd", grid=(T // tm,),
        in_specs=[tile, tile3] + _gate_specs(tm),
        out_specs=(tile3, pl.BlockSpec((tm, 3 * D_MODEL), lambda i: (i, 0))),
        out_shape=(jax.ShapeDtypeStruct((3, T, D_MODEL), BF16), jax.ShapeDtypeStruct((T, 3 * D_MODEL), BF16)),
        semantics=("parallel",), riders=riders), riders)


_CONV_TF = D_FF // 2
_CONV_TS = 256
_HALO = 16


def _conv_fwd(ab, cw, cb, B, S):
    T = B * S
    ts = _row_tile(S, _CONV_TS)
    tf = _CONV_TF
    nb = D_FF // tf
    tps = S // ts
    hb = ts // _HALO

    def body(a_ref, p_ref, b_ref, w_ref, cb_ref, o_ref):
        start = (pl.program_id(0) % tps) == 0
        a = a_ref[...].astype(F32)
        prev = jnp.where(start, 0.0, p_ref[...].astype(F32))
        ext = jnp.concatenate([prev, a], axis=0)
        a1 = pltpu.roll(ext, 1, 0)[_HALO:, :]
        a2 = pltpu.roll(ext, 2, 0)[_HALO:, :]
        ac = cb_ref[...] + w_ref[0] * a2 + w_ref[1] * a1 + w_ref[2] * a
        o_ref[...] = (ac * _sigmoid(ac) * b_ref[...].astype(F32)).astype(o_ref.dtype)

    return _pallas(
        body, name="conv_fwd", grid=(T // ts, nb),
        in_specs=[pl.BlockSpec((ts, tf), lambda i, j: (i, j)),
                  pl.BlockSpec((_HALO, tf), lambda i, j: (jnp.maximum(i * hb - 1, 0), j)),
                  pl.BlockSpec((ts, tf), lambda i, j: (i, j + nb)),
                  pl.BlockSpec((3, 1, tf), lambda i, j: (0, 0, j)),
                  pl.BlockSpec((1, tf), lambda i, j: (0, j))],
        out_specs=pl.BlockSpec((ts, tf), lambda i, j: (i, j)),
        out_shape=jax.ShapeDtypeStruct((T, D_FF), BF16), compiler_params=_cp("parallel", "parallel"),
    )(ab, ab, ab, cw, cb)


def _conv_bwd(ab, d_ff, cw, cb, B, S, riders=()):
    T = B * S
    ts = _row_tile(S, _CONV_TS)
    tf = _CONV_TF
    nb = D_FF // tf
    tps = S // ts
    hb = ts // _HALO
    last_h = T // _HALO - 1
    n_ext = ts + _HALO

    def body(a_ref, ap_ref, an_ref, b_ref, bn_ref, d_ref, dn_ref, w_ref, cb_ref, dab_ref, dw_ref, dcb_ref):
        i = pl.program_id(1)

        @pl.when(i == 0)
        def _():
            dw_ref[...] = jnp.zeros_like(dw_ref)
            dcb_ref[...] = jnp.zeros_like(dcb_ref)

        start = (i % tps) == 0
        end = (i % tps) == tps - 1
        a = a_ref[...].astype(F32)
        ext = jnp.concatenate([jnp.where(start, 0.0, ap_ref[...].astype(F32)), a, an_ref[...].astype(F32)], axis=0)
        r1 = pltpu.roll(ext, 1, 0)[_HALO:, :]
        r2 = pltpu.roll(ext, 2, 0)[_HALO:, :]
        ac = cb_ref[...] + w_ref[0] * r2 + w_ref[1] * r1 + w_ref[2] * ext[_HALO:, :]
        sg = _sigmoid(ac)
        d_e = jnp.concatenate([d_ref[...].astype(F32), jnp.where(end, 0.0, dn_ref[...].astype(F32))], axis=0)
        b_e = jnp.concatenate([b_ref[...].astype(F32), bn_ref[...].astype(F32)], axis=0)
        dab_ref[1] = (d_e[:ts, :] * (ac * sg)[:ts, :]).astype(dab_ref.dtype)
        dac = d_e * b_e * sg * (1.0 + ac * (1.0 - sg))
        u1 = pltpu.roll(dac, n_ext - 1, 0)[:ts, :]
        u2 = pltpu.roll(dac, n_ext - 2, 0)[:ts, :]
        dac0 = dac[:ts, :]
        dab_ref[0] = (w_ref[2] * dac0 + w_ref[1] * u1 + w_ref[0] * u2).astype(dab_ref.dtype)
        dcb_ref[...] += jnp.sum(dac0, axis=0, keepdims=True)
        dw_ref[2] += jnp.sum(dac0 * a, axis=0, keepdims=True)
        dw_ref[1] += jnp.sum(dac0 * r1[:ts, :], axis=0, keepdims=True)
        dw_ref[0] += jnp.sum(dac0 * r2[:ts, :], axis=0, keepdims=True)

    def cur(off):
        return pl.BlockSpec((ts, tf), lambda j, i: (i, j + off))

    def nxt(off):
        return pl.BlockSpec((_HALO, tf), lambda j, i: (jnp.minimum((i + 1) * hb, last_h), j + off))

    return _carried(*_pcall(
        body, (ab, ab, ab, ab, ab, d_ff, d_ff, cw, cb), name="conv_bwd", grid=(nb, T // ts),
        in_specs=[cur(0), pl.BlockSpec((_HALO, tf), lambda j, i: (jnp.maximum(i * hb - 1, 0), j)), nxt(0),
                  cur(nb), nxt(nb), cur(0), nxt(0),
                  pl.BlockSpec((3, 1, tf), lambda j, i: (0, 0, j)), pl.BlockSpec((1, tf), lambda j, i: (0, j))],
        out_specs=(pl.BlockSpec((2, ts, tf), lambda j, i: (0, i, j)), pl.BlockSpec((3, 1, tf), lambda j, i: (0, 0, j)),
                   pl.BlockSpec((1, tf), lambda j, i: (0, j))),
        out_shape=(jax.ShapeDtypeStruct((2, T, D_FF), BF16),
                   jax.ShapeDtypeStruct((3, 1, D_FF), F32), jax.ShapeDtypeStruct((1, D_FF), F32)),
        semantics=("parallel", "arbitrary"), riders=riders), riders)


def _local_step(x, mem, tgt, p, comm, B, S):
    g = {}
    h = comm.carry("norm1", lambda r: _rms_fwd("norm1", x, p["norm1_g"], riders=r))
    proj = comm.carry("in_proj", lambda r: _mm_cs("in_proj", h, comm.w("w_in"), BF16, riders=r))
    a_out = _gmlp_fwd(proj, p["ln_v_g"], p["ln_v_b"], p["w_spatial"], p["b_spatial"])
    o_h, b_out, states = comm.carry(
        "hgrn_fwd", lambda r: _hgrn_fwd(proj, p["lb_logits"], p["hgrn_norm_g"], B, S, riders=r))
    memn = _rms_fwd("mem_norm", mem, p["mem_norm_g"])
    kv = _mm_rs("mem_kv", memn, comm.w("w_mem_kv"), F32)
    c_out = _attn_fwd(proj, kv, B, S)
    merged, ups = comm.carry(
        "merge_fwd", lambda r: _merge_fwd(a_out, b_out, c_out, comm.w("w_branch"), proj, riders=r))
    x1, h2 = _proj_res_norm("out_proj_norm2", merged, comm.w("w_out"), x, p["norm2_g"])
    ab = comm.carry("up_proj", lambda r: _mm_cs("up_proj", h2, comm.w("w_up"), BF16, riders=r))
    conv_w = comm.w("conv_w")
    ff = _conv_fwd(ab, conv_w, p["conv_b"], B, S)
    dx2, g["final_g"], loss = _proj_res_loss("down_proj_loss", ff, comm.w("w_down"), x1, tgt, p["final_g"])

    comm.grad("w_down", _mm_tn_rs("g_w_down", ff, dx2, to=D_FF // 2))
    d_ff = comm.carry("d_ff", lambda r: _mm_nt_rs("d_ff", dx2, comm.w("w_down"), BF16, riders=r))
    d_ab, g["conv_w"], g["conv_b"] = comm.carry(
        "conv_bwd", lambda r: _conv_bwd(ab, d_ff, conv_w, p["conv_b"], B, S, riders=r))
    comm.grad("w_up", _mm_tn_cs("g_w_up", h2, d_ab, N_CHIPS, to=512, stacked=True))
    d_x1, g["norm2_g"] = comm.carry("d_h2", lambda r: _mm_nt_cs(
        "d_h2_norm2_bwd", d_ab, comm.w("w_up"), F32, riders=r, stacked=True, norm_bwd=(x1, p["norm2_g"], dx2)))
    comm.grad("w_out", _mm_tn_rs("g_w_out", merged, d_x1, to=512))
    d_merged = _mm_nt_rs("d_merged", d_x1, comm.w("w_out"), F32)
    d_ups, d_gates = comm.carry("merge_bwd", lambda r: _merge_bwd(d_merged, ups, proj, riders=r))

    d_br = comm.carry("d_branch", lambda r: _branch_bwd_act(d_ups, comm.w("w_branch"), riders=r))
    comm.grad("w_branch", jnp.concatenate(
        [_branch_bwd_weight("g_w_branch%d" % n, br, d_ups, n) for n, br in enumerate((a_out, b_out, c_out))],
        axis=1))

    d_gm, g["w_spatial"], g["b_spatial"], g["ln_v_g"], g["ln_v_b"] = comm.carry(
        "gmlp_bwd", lambda r: _gmlp_bwd(proj, d_br, p["ln_v_g"], p["ln_v_b"], p["w_spatial"], p["b_spatial"],
                                        riders=r))
    d_xq, d_kv = _attn_bwd(proj, kv, d_br, B, S)
    comm.grad("w_mem_kv", _mm_tn_rs("g_w_mem_kv", memn, d_kv, to=512))
    d_memn = _mm_nt_rs("d_memn", d_kv, comm.w("w_mem_kv"), F32)
    _, g["mem_norm_g"] = _rms_bwd("mem_norm_bwd", mem, p["mem_norm_g"], d_memn, None)
    d_proj, g["lb_logits"], g["hgrn_norm_g"] = comm.carry(
        "hgrn_bwd", lambda r: _hgrn_bwd(proj, o_h, states, d_br, p["lb_logits"], p["hgrn_norm_g"],
                                        (d_gm, d_xq, d_gates), B, S, riders=r))
    comm.small_grads([g[n].reshape(_SMALL_SHAPE[n]) for n in _SMALL_EARLY] + [loss])
    comm.grad("w_in", comm.carry("g_w_in", lambda r: _mm_tn_cs("g_w_in", h, d_proj, N_CHIPS, to=512, riders=r)))
    grad_x, g["norm1_g"] = comm.carry("d_h", lambda r: _mm_nt_cs(
        "d_h_norm1_bwd", d_proj, comm.w("w_in"), F32, riders=r, norm_bwd=(x, p["norm1_g"], d_x1)))
    return loss, grad_x, g


HBM_SPEC = pl.BlockSpec(memory_space=pltpu.HBM)


def _place():
    x, y, c = lax.axis_index("x"), lax.axis_index("y"), lax.axis_index("c")
    other_chips = [(1 - x, y), (x, 1 - y), (1 - x, 1 - y)]
    return x, y, c, other_chips


def _remote(src, dst, send_sem, recv_sem, dev):
    return pltpu.make_async_remote_copy(src_ref=src, dst_ref=dst, send_sem=send_sem, recv_sem=recv_sem,
                                        device_id=dev, device_id_type=MESH_ID)


class _Exchange:
    def __init__(self, operands, out_shape, aliases, scratch, start, finish, mid=None, mid_at=0.5):
        self.operands, self.out_shape, self.aliases, self.scratch = operands, out_shape, aliases, scratch
        self.start, self.finish, self.mid, self.mid_at = start, finish, mid, mid_at


def _run_exchanges(name, exs):
    n_in = [len(ex.operands) for ex in exs]
    n_out = [len(ex.out_shape) for ex in exs]
    n_scr = [len(ex.scratch) for ex in exs]

    def body(*refs):
        ins, outs, scr = refs[:sum(n_in)], refs[sum(n_in):sum(n_in) + sum(n_out)], refs[sum(n_in) + sum(n_out):]
        parts, oi, oo, os_ = [], 0, 0, 0
        for k in range(len(exs)):
            parts.append((ins[oi:oi + n_in[k]], outs[oo:oo + n_out[k]], scr[os_:os_ + n_scr[k]]))
            oi, oo, os_ = oi + n_in[k], oo + n_out[k], os_ + n_scr[k]
        for ex, part in zip(exs, parts):
            ex.start(*part)
        for ex, part in zip(exs, parts):
            if ex.mid is not None:
                ex.mid(*part)
        for ex, part in zip(exs, parts):
            ex.finish(*part)

    aliases, ops, shapes, scratch, oi, oo = {}, [], [], [], 0, 0
    for k, ex in enumerate(exs):
        aliases.update({oi + a: oo + b for a, b in ex.aliases.items()})
        oi, oo = oi + n_in[k], oo + n_out[k]
        ops += list(ex.operands)
        shapes += [pltpu.HBM(s.shape, s.dtype) for s in ex.out_shape]
        scratch += list(ex.scratch)
    res = _pallas(
        body, name=name, in_specs=[HBM_SPEC] * len(ops), out_specs=(HBM_SPEC,) * len(shapes), out_shape=tuple(shapes),
        input_output_aliases=aliases, scratch_shapes=scratch,
    )(*ops)
    out, oo = [], 0
    for k in range(len(exs)):
        out.append(list(res[oo:oo + n_out[k]]))
        oo += n_out[k]
    return out


def _ex_all_gather(slabs, halved, part=(0, 1)):
    n = len(slabs)

    def rows(a, cc):
        if not halved[a]:
            return slice(None)
        pr = slabs[a].shape[1] // part[1]
        return pl.ds(part[0] * pr + cc * (pr // 2), pr // 2)

    def ici(bufs, scr, a, j, chip, c, mine):
        px, py = chip
        x, y, _, _ = _place()
        qs = 2 * x + y if mine else 2 * px + py
        piece = bufs[a].at[qs, rows(a, c)]
        return _remote(piece, piece, scr[0].at[3 * a + j], scr[1].at[3 * a + j], (px, py, c))

    def d2d(bufs, scr, a, j, chip, cc):
        px, py = chip
        x, y, c, _ = _place()
        piece = bufs[a].at[2 * px + py, rows(a, cc)]
        return _remote(piece, piece, scr[2].at[3 * a + j], scr[3].at[3 * a + j], (x, y, 1 - c))

    def start(ins, outs, scr):
        _, _, c, chips = _place()
        for j, chip in enumerate(chips):
            for a in range(n):
                ici(outs, scr, a, j, chip, c, True).start()

    def finish(ins, outs, scr):
        _, _, c, chips = _place()
        for j, chip in enumerate(chips):
            for a in range(n):
                ici(outs, scr, a, j, chip, c, False).wait_recv()
                if halved[a]:
                    d2d(outs, scr, a, j, chip, c).start()
        for j, chip in enumerate(chips):
            for a in range(n):
                if halved[a]:
                    d2d(outs, scr, a, j, chip, 1 - c).wait_recv()
        for j, chip in enumerate(chips):
            for a in range(n):
                ici(outs, scr, a, j, chip, c, True).wait_send()
                if halved[a]:
                    d2d(outs, scr, a, j, chip, c).wait_send()

    return _Exchange(list(slabs), [jax.ShapeDtypeStruct(s.shape, s.dtype) for s in slabs],
                     {a: a for a in range(n)}, [pltpu.SemaphoreType.DMA((3 * n,))] * 4, start, finish)


def _ex_gather_relay(slabs, mid_at=0.5):
    n = len(slabs)

    def rows(a, cc):
        hr = slabs[a].shape[1] // 2
        return pl.ds(cc * hr, hr)

    def peers():
        x, y, c, _ = _place()
        nbr0 = ((x + c) % 2, (y + 1 - c) % 2)
        nbr1 = ((x + 1 - c) % 2, (y + c) % 2)
        return x, y, c, nbr0, nbr1, (1 - x, 1 - y)

    def ici(bufs, scr, a, k, chip, dev, cc):
        _, _, c, _, _, _ = peers()
        piece = bufs[a].at[2 * chip[0] + chip[1], rows(a, cc)]
        return _remote(piece, piece, scr[0].at[3 * a + k], scr[1].at[3 * a + k], (dev[0], dev[1], c))

    def d2d(bufs, scr, a, k, chip, cc):
        x, y, c, _, _, _ = peers()
        piece = bufs[a].at[2 * chip[0] + chip[1], rows(a, cc)]
        return _remote(piece, piece, scr[2].at[3 * a + k], scr[3].at[3 * a + k], (x, y, 1 - c))

    def start(ins, outs, scr):
        x, y, c, nbr0, nbr1, _ = peers()
        for a in range(n):
            ici(outs, scr, a, 0, (x, y), nbr0, c).start()
            ici(outs, scr, a, 1, (x, y), nbr1, c).start()

    def mid(ins, outs, scr):
        x, y, c, nbr0, nbr1, diag = peers()
        for a in range(n):
            ici(outs, scr, a, 0, nbr0, nbr0, c).wait_recv()
            ici(outs, scr, a, 2, nbr0, nbr1, c).start()
            d2d(outs, scr, a, 0, nbr0, c).start()
        for a in range(n):
            ici(outs, scr, a, 1, nbr1, nbr1, c).wait_recv()
            d2d(outs, scr, a, 1, nbr1, c).start()

    def finish(ins, outs, scr):
        x, y, c, nbr0, nbr1, diag = peers()
        for a in range(n):
            ici(outs, scr, a, 2, diag, nbr1, c).wait_recv()
            d2d(outs, scr, a, 2, diag, c).start()
        for a in range(n):
            d2d(outs, scr, a, 0, nbr1, 1 - c).wait_recv()
            d2d(outs, scr, a, 1, nbr0, 1 - c).wait_recv()
            d2d(outs, scr, a, 2, diag, 1 - c).wait_recv()
        for a in range(n):
            ici(outs, scr, a, 0, (x, y), nbr0, c).wait_send()
            ici(outs, scr, a, 1, (x, y), nbr1, c).wait_send()
            ici(outs, scr, a, 2, nbr0, nbr1, c).wait_send()
            d2d(outs, scr, a, 0, nbr0, c).wait_send()
            d2d(outs, scr, a, 1, nbr1, c).wait_send()
            d2d(outs, scr, a, 2, diag, c).wait_send()

    return _Exchange(list(slabs), [jax.ShapeDtypeStruct(s.shape, s.dtype) for s in slabs],
                     {a: a for a in range(n)}, [pltpu.SemaphoreType.DMA((3 * n,))] * 4, start, finish, mid, mid_at)


def _ex_to_sibling(grads):
    n = len(grads)

    def copy(ins, outs, scr, a):
        x, y, c, _ = _place()
        hr = grads[a].shape[1] // 2
        return _remote(ins[a].at[:, pl.ds((1 - c) * hr, hr), :], outs[a], scr[0].at[a], scr[1].at[a], (x, y, 1 - c))

    def start(ins, outs, scr):
        for a in range(n):
            copy(ins, outs, scr, a).start()

    def finish(ins, outs, scr):
        for a in range(n):
            copy(ins, outs, scr, a).wait()

    out_shape = [jax.ShapeDtypeStruct((g.shape[0], g.shape[1] // 2, g.shape[2]), g.dtype) for g in grads]
    return _Exchange(list(grads), out_shape, {}, [pltpu.SemaphoreType.DMA((n,))] * 2, start, finish)


def _ex_to_owner(parts, part=(0, 1), landing=None):
    n = len(parts)

    def copy(ins, outs, scr, a, j, chip):
        _, _, c, _ = _place()
        px, py = chip
        pr = parts[a].shape[1] // part[1]
        rows = pl.ds(part[0] * pr, pr)
        return _remote(ins[a].at[2 * px + py, rows], outs[a].at[j, rows], scr[0].at[3 * a + j],
                       scr[1].at[3 * a + j], (px, py, c))

    def start(ins, outs, scr):
        for j, chip in enumerate(_place()[3]):
            for a in range(n):
                copy(ins, outs, scr, a, j, chip).start()

    def finish(ins, outs, scr):
        for j, chip in enumerate(_place()[3]):
            for a in range(n):
                copy(ins, outs, scr, a, j, chip).wait()

    out_shape = [jax.ShapeDtypeStruct((3,) + p.shape[1:], p.dtype) for p in parts]
    operands, aliases = list(parts), {}
    if landing is not None:
        operands, aliases = operands + list(landing), {n + a: a for a in range(n)}
    return _Exchange(operands, out_shape, aliases, [pltpu.SemaphoreType.DMA((3 * n,))] * 2, start, finish)


def _ex_share_halves(bufs):
    n = len(bufs)

    def copy(outs, scr, a, cc):
        x, y, c, _ = _place()
        hr = bufs[a].shape[0] // 2
        piece = outs[a].at[pl.ds(cc * hr, hr), :]
        return _remote(piece, piece, scr[0].at[a], scr[1].at[a], (x, y, 1 - c))

    def start(ins, outs, scr):
        c = _place()[2]
        for a in range(n):
            copy(outs, scr, a, c).start()

    def finish(ins, outs, scr):
        c = _place()[2]
        for a in range(n):
            copy(outs, scr, a, c).wait_send()
            copy(outs, scr, a, 1 - c).wait_recv()

    return _Exchange(list(bufs), [jax.ShapeDtypeStruct(b.shape, b.dtype) for b in bufs], {a: a for a in range(n)},
                     [pltpu.SemaphoreType.DMA((n,))] * 2, start, finish)


def _ex_gather_small(arrs):
    n = len(arrs)

    def peer_of(m):
        x, y, c, _ = _place()
        return (1 - x if m & 4 else x, 1 - y if m & 2 else y, 1 - c if m & 1 else c)

    def start(ins, outs, scr):
        x, y, c, _ = _place()
        for m in range(1, N_DEV):
            for a in range(n):
                k = (N_DEV - 1) * a + m - 1
                _remote(ins[a], outs[a].at[4 * x + 2 * y + c], scr[0].at[k], scr[1].at[k], peer_of(m)).start()

    def finish(ins, outs, scr):
        for m in range(1, N_DEV):
            px, py, pc = peer_of(m)
            for a in range(n):
                k = (N_DEV - 1) * a + m - 1
                slot = outs[a].at[4 * px + 2 * py + pc]
                cp = _remote(ins[a], slot, scr[0].at[k], scr[1].at[k], (px, py, pc))
                cp.wait_send()
                cp.wait_recv()

    slots = [jnp.zeros((N_DEV,) + a.shape, a.dtype) for a in arrs]
    out_shape = [jax.ShapeDtypeStruct(s.shape, s.dtype) for s in slots]
    return _Exchange(list(arrs) + slots, out_shape, {n + a: a for a in range(n)},
                     [pltpu.SemaphoreType.DMA(((N_DEV - 1) * n,))] * 2, start, finish)


def _div_tile(n, want):
    best = None
    for t in range(8, min(n, want) + 1, 8):
        if n % t == 0:
            best = t
    assert best is not None, n
    return best


def _cast_into_slab(name, w, place, dtype):
    r, cc = w.shape
    tr = r if r * cc <= 128 * 1024 else _div_tile(r, 256)

    def body(s_ref, w_ref, o_ref):
        o_ref[...] = w_ref[...].astype(o_ref.dtype)

    return _pallas(
        body, name=name,
        grid_spec=pltpu.PrefetchScalarGridSpec(
            num_scalar_prefetch=1, grid=(r // tr,),
            in_specs=[pl.BlockSpec((tr, cc), lambda i, s: (i, 0))],
            out_specs=pl.BlockSpec((None, tr, cc), lambda i, s: (s[0], i, 0))),
        out_shape=jax.ShapeDtypeStruct((N_CHIPS, r, cc), dtype), compiler_params=_cp("parallel"),
    )(place, w)


def _add_half(name, g, rcv, place):
    nq, r, cc = g.shape
    hr = r // 2

    def body(s_ref, g_ref, r_ref, o_ref):
        o_ref[...] = (g_ref[...] + r_ref[...]).astype(o_ref.dtype)

    spec = pl.BlockSpec((None, hr, cc), lambda i, s: (i, 0, 0))
    return _pallas(
        body, name=name,
        grid_spec=pltpu.PrefetchScalarGridSpec(
            num_scalar_prefetch=1, grid=(nq,),
            in_specs=[pl.BlockSpec((None, hr, cc), lambda i, s: (i, s[1], 0)), spec], out_specs=spec),
        out_shape=jax.ShapeDtypeStruct((nq, hr, cc), BF16), compiler_params=_cp("parallel"),
    )(place, g, rcv)


def _sum_owner(name, part, rcv, place):
    _, hr, cc = part.shape
    tr = _div_tile(hr, 128)
    nb = hr // tr

    def body(s_ref, p_ref, r_ref, o_ref):
        o_ref[...] = ((p_ref[...].astype(F32) + r_ref[0].astype(F32)) + r_ref[1].astype(F32)) + r_ref[2].astype(F32)

    return _pallas(
        body, name=name,
        grid_spec=pltpu.PrefetchScalarGridSpec(
            num_scalar_prefetch=1, grid=(nb,),
            in_specs=[pl.BlockSpec((None, tr, cc), lambda i, s: (s[0], i, 0)),
                      pl.BlockSpec((3, tr, cc), lambda i, s: (0, i, 0))],
            out_specs=pl.BlockSpec((tr, cc), lambda i, s: (s[1] * nb + i, 0))),
        out_shape=jax.ShapeDtypeStruct((2 * hr, cc), F32), compiler_params=_cp("parallel"),
    )(place, part, rcv)


def _sum_small(gathered, local, place):
    n = len(gathered)

    def body(s_ref, *refs):
        g_refs, l_refs, o_refs = refs[:n], refs[n:2 * n], refs[2 * n:]
        me = s_ref[2]
        for g_ref, l_ref, o_ref in zip(g_refs, l_refs, o_refs):
            acc = None
            for d in range(N_DEV):
                term = jnp.where(me == d, l_ref[...], g_ref[d])
                acc = term if acc is None else acc + term
            o_ref[...] = acc

    def whole(shape):
        return pl.BlockSpec(shape, lambda i, s, nd=len(shape): (0,) * nd)

    return _pallas(
        body, name="sum_small",
        grid_spec=pltpu.PrefetchScalarGridSpec(
            num_scalar_prefetch=1, grid=(1,),
            in_specs=[whole(g.shape) for g in gathered] + [whole(a.shape) for a in local],
            out_specs=tuple(whole(a.shape) for a in local)),
        out_shape=tuple(jax.ShapeDtypeStruct(a.shape, a.dtype) for a in local), compiler_params=_cp("arbitrary"),
    )(place, *gathered, *local)


def _adamw(name, w, g, m, v):
    r, cc = w.shape
    tr = r if r * cc <= 128 * 1024 else _div_tile(r, 256)

    def body(w_ref, g_ref, m_ref, v_ref, d_ref, mo_ref, vo_ref):
        gv = g_ref[...]
        mn = ADAM_B1 * m_ref[...] + (1.0 - ADAM_B1) * gv
        vn = ADAM_B2 * v_ref[...] + (1.0 - ADAM_B2) * (gv * gv)
        m_hat = mn / (1.0 - ADAM_B1 ** ADAM_STEP)
        v_hat = vn / (1.0 - ADAM_B2 ** ADAM_STEP)
        d_ref[...] = -ADAM_LR * (m_hat / (jnp.sqrt(v_hat) + ADAM_EPS) + ADAM_WD * w_ref[...])
        mo_ref[...] = mn
        vo_ref[...] = vn

    spec = pl.BlockSpec((tr, cc), lambda i: (i, 0))
    sd = jax.ShapeDtypeStruct((r, cc), F32)
    return _pallas(
        body, name=name, grid=(r // tr,), in_specs=[spec] * 4, out_specs=(spec,) * 3, out_shape=(sd,) * 3,
        compiler_params=_cp("parallel"),
    )(w, g, m, v)


_BIG = ("w_in", "w_up", "w_branch", "w_mem_kv", "w_out", "w_down")
_BIG_SHARD_SHAPE = {"w_in": (1024, 1664), "w_up": (1024, 1408), "w_branch": (1536, 256),
                    "w_mem_kv": (256, 1024), "w_out": (256, 1024), "w_down": (704, 1024)}
_SMALL_SHAPE = {"norm1_g": (1, D_MODEL), "ln_v_g": (1, GM_WIDTH), "ln_v_b": (1, GM_WIDTH),
                "w_spatial": (GM_GROUPS * GM_CHUNK, GM_CHUNK), "b_spatial": (GM_GROUPS, GM_CHUNK),
                "lb_logits": (2, HG_HEADS * HG_DIM), "hgrn_norm_g": (1, HG_DIM), "mem_norm_g": (1, D_MODEL),
                "norm2_g": (1, D_MODEL), "conv_w": (3, D_FF), "conv_b": (1, D_FF), "final_g": (1, D_MODEL)}
_SMALL_EARLY = tuple(n for n in _SMALL_SHAPE if n != "norm1_g")
_PARAM_ORDER = ("norm1_g", "w_in", "ln_v_g", "ln_v_b", "w_spatial", "b_spatial", "lb_logits", "hgrn_norm_g",
                "mem_norm_g", "w_mem_kv", "w_branch", "w_out", "norm2_g", "w_up", "conv_w", "conv_b", "w_down",
                "final_g")


def _adamw_small(ws, gs, ms, vs):
    n = len(ws)

    def body(*refs):
        w_refs, g_refs, m_refs, v_refs = refs[:n], refs[n:2 * n], refs[2 * n:3 * n], refs[3 * n:4 * n]
        d_refs, mo_refs, vo_refs = refs[4 * n:5 * n], refs[5 * n:6 * n], refs[6 * n:]
        for k in range(n):
            gv = g_refs[k][...]
            mn = ADAM_B1 * m_refs[k][...] + (1.0 - ADAM_B1) * gv
            vn = ADAM_B2 * v_refs[k][...] + (1.0 - ADAM_B2) * (gv * gv)
            m_hat = mn / (1.0 - ADAM_B1 ** ADAM_STEP)
            v_hat = vn / (1.0 - ADAM_B2 ** ADAM_STEP)
            d_refs[k][...] = -ADAM_LR * (m_hat / (jnp.sqrt(v_hat) + ADAM_EPS) + ADAM_WD * w_refs[k][...])
            mo_refs[k][...] = mn
            vo_refs[k][...] = vn

    specs = [pl.BlockSpec(a.shape, lambda i: (0, 0)) for a in ws]
    shapes = tuple(jax.ShapeDtypeStruct(a.shape, F32) for a in ws)
    res = _pallas(
        body, name="adamw_small", grid=(1,), in_specs=specs * 4, out_specs=tuple(specs * 3), out_shape=shapes * 3,
        compiler_params=_cp("arbitrary"),
    )(*ws, *gs, *ms, *vs)
    return res[:n], res[n:2 * n], res[2 * n:]


class _Comm:
    _ROW_SHARDED = ("w_mem_kv", "w_out", "w_down")

    def __init__(self, slabs, place):
        self.slabs, self.place = slabs, place
        self.full, self.raw, self.parts, self.landing, self.bufs, self.done = {}, {}, {}, {}, {}, {}

    def w(self, name):
        a = self.full[name]
        if name in self._ROW_SHARDED:
            return a.reshape(-1, a.shape[-1])
        if name == "conv_w":
            return jnp.transpose(a, (1, 0, 2)).reshape(3, 1, D_FF)
        return a

    def grad(self, name, arr):
        self.raw[name] = arr.reshape((N_CHIPS, -1, arr.shape[-1]))
        if name == "w_in":
            ex, deliver = self._to_sibling(["w_in"])
            deliver(_run_exchanges("rs_sibling_w_in", [ex])[0])

    def small_grads(self, arrays):
        self.small_local = list(arrays)

    def carry(self, tag, call):
        plan = self._plan(tag)
        if not plan:
            return call(())
        out, carried = call([ex for ex, _ in plan])
        for (_, deliver), res in zip(plan, carried):
            deliver(res)
        return out

    def finish(self, last_small):
        ex, deliver = self._share(["w_out", "w_branch", "w_mem_kv", "w_in"])
        shared, small = _run_exchanges("share_and_gather_last", [ex, _ex_gather_small(last_small)])
        deliver(shared)
        return self.done, self.small_local + list(last_small), self.small_everyone + small

    def _plan(self, tag):
        if tag == "norm1":
            def deliver(res):
                self.full["w_in"] = res[0]

            return [(_ex_gather_relay([self.slabs["w_in"]]), deliver)]
        if tag == "in_proj":
            return [self._gather_relay(["w_branch", "w_out", "w_mem_kv", "w_down"], 0.6), self._gather(["conv_w"])]
        if tag == "hgrn_fwd":
            return [self._gather_relay(["w_up"], 0.8)]
        if tag == "d_h2":
            return [self._to_sibling(["w_down", "w_up"])]
        if tag == "hgrn_bwd":
            return [self._to_owner(["w_down", "w_up"]), self._to_sibling(["w_out", "w_branch", "w_mem_kv"])]
        if tag == "g_w_in":
            def keep(res):
                self.small_everyone = res

            return [self._to_owner(["w_out", "w_branch", "w_mem_kv"]), self._share(["w_down", "w_up"]),
                    (_ex_gather_small(self.small_local), keep)]
        if tag == "d_h":
            return [self._to_owner(["w_in"])]
        return []

    def _gather(self, names, part=(0, 1)):
        def deliver(res):
            self.slabs.update(zip(names, res))
            self.full.update(zip(names, res))

        return _ex_all_gather([self.slabs[n] for n in names], [n != "conv_w" for n in names], part), deliver

    def _gather_relay(self, names, mid_at):
        return _ex_gather_relay([self.slabs[n] for n in names], mid_at), lambda res: self.full.update(zip(names, res))

    def _to_sibling(self, names):
        def deliver(res):
            for n, r in zip(names, res):
                self.parts[n] = _add_half("rs_add_" + n, self.raw[n], r, self.place)

        return _ex_to_sibling([self.raw[n] for n in names]), deliver

    def _to_owner(self, names, part=(0, 1)):
        def deliver(res):
            for n, r in zip(names, res):
                if part[0] + 1 < part[1]:
                    self.landing[n] = r
                else:
                    self.bufs[n] = _sum_owner("rs_sum_" + n, self.parts[n], r, self.place)

        landing = [self.landing[n] for n in names] if part[0] else None
        return _ex_to_owner([self.parts[n] for n in names], part, landing), deliver

    def _share(self, names):
        return _ex_share_halves([self.bufs[n] for n in names]), lambda res: self.done.update(zip(names, res))


def kernel(x, mem, norm1_g, w_in, ln_v_g, ln_v_b, w_spatial, b_spatial, lb_logits, hgrn_norm_g, mem_norm_g, w_mem_kv, w_branch, w_out, norm2_g, w_up, conv_w, conv_b, w_down, final_g, loss_target, m_norm1_g, m_w_in, m_ln_v_g, m_ln_v_b, m_w_spatial, m_b_spatial, m_lb_logits, m_hgrn_norm_g, m_mem_norm_g, m_w_mem_kv, m_w_branch, m_w_out, m_norm2_g, m_w_up, m_conv_w, m_conv_b, m_w_down, m_final_g, v_norm1_g, v_w_in, v_ln_v_g, v_ln_v_b, v_w_spatial, v_b_spatial, v_lb_logits, v_hgrn_norm_g, v_mem_norm_g, v_w_mem_kv, v_w_branch, v_w_out, v_norm2_g, v_w_up, v_conv_w, v_conv_b, v_w_down, v_final_g):
    w = dict(norm1_g=norm1_g, w_in=w_in, ln_v_g=ln_v_g, ln_v_b=ln_v_b, w_spatial=w_spatial, b_spatial=b_spatial,
             lb_logits=lb_logits, hgrn_norm_g=hgrn_norm_g, mem_norm_g=mem_norm_g, w_mem_kv=w_mem_kv,
             w_branch=w_branch, w_out=w_out, norm2_g=norm2_g, w_up=w_up, conv_w=conv_w, conv_b=conv_b,
             w_down=w_down, final_g=final_g)
    mom = dict(norm1_g=m_norm1_g, w_in=m_w_in, ln_v_g=m_ln_v_g, ln_v_b=m_ln_v_b, w_spatial=m_w_spatial,
               b_spatial=m_b_spatial, lb_logits=m_lb_logits, hgrn_norm_g=m_hgrn_norm_g, mem_norm_g=m_mem_norm_g,
               w_mem_kv=m_w_mem_kv, w_branch=m_w_branch, w_out=m_w_out, norm2_g=m_norm2_g, w_up=m_w_up,
               conv_w=m_conv_w, conv_b=m_conv_b, w_down=m_w_down, final_g=m_final_g)
    var = dict(norm1_g=v_norm1_g, w_in=v_w_in, ln_v_g=v_ln_v_g, ln_v_b=v_ln_v_b, w_spatial=v_w_spatial,
               b_spatial=v_b_spatial, lb_logits=v_lb_logits, hgrn_norm_g=v_hgrn_norm_g, mem_norm_g=v_mem_norm_g,
               w_mem_kv=v_w_mem_kv, w_branch=v_w_branch, w_out=v_w_out, norm2_g=v_norm2_g, w_up=v_w_up,
               conv_w=v_conv_w, conv_b=v_conv_b, w_down=v_w_down, final_g=v_final_g)
    B, S, D = x.shape
    T = B * S
    ci = lax.axis_index("c")
    q = 2 * lax.axis_index("x") + lax.axis_index("y")
    place = jnp.stack([q, ci, 2 * q + ci]).astype(jnp.int32)

    slabs = {n: _cast_into_slab("slab_" + n, w[n].reshape(_BIG_SHARD_SHAPE[n]), place, BF16) for n in _BIG}
    slabs["conv_w"] = _cast_into_slab("slab_conv_w", conv_w[0], place, F32)
    comm = _Comm(slabs, place)
    p = dict(
        norm1_g=norm1_g, ln_v_g=ln_v_g, ln_v_b=ln_v_b, w_spatial=w_spatial[0],
        b_spatial=b_spatial.reshape(GM_GROUPS, GM_CHUNK, 1), lb_logits=lb_logits, hgrn_norm_g=hgrn_norm_g,
        mem_norm_g=mem_norm_g, norm2_g=norm2_g, conv_b=conv_b, final_g=final_g.reshape(1, D))

    loss, grad_x, g = _local_step(x.reshape(T, D), mem.reshape(B * MEM_LEN, D), loss_target.reshape(T, D), p, comm,
                                  B, S)

    shard_grads, local_small, everyone = comm.finish([g["norm1_g"]])
    summed = _sum_small(everyone, local_small, place)
    small_names = list(_SMALL_EARLY) + ["norm1_g"]
    total = dict(zip(_SMALL_EARLY, summed))
    loss_total, total["norm1_g"] = summed[len(_SMALL_EARLY)][0, 0], summed[-1]

    grads, delta, new_m, new_v = {}, {}, {}, {}
    for n in _BIG:
        shp = _BIG_SHARD_SHAPE[n]
        grads[n] = shard_grads[n]
        delta[n], new_m[n], new_v[n] = _adamw("adamw_" + n, w[n].reshape(shp), shard_grads[n],
                                              mom[n].reshape(shp), var[n].reshape(shp))
    cw_shard = D_FF // N_CHIPS
    total["conv_w"] = lax.dynamic_slice(total["conv_w"], (0, q * cw_shard), (3, cw_shard))

    def flat2d(d, n):
        return d[n].reshape(total[n].shape)

    upd = _adamw_small([flat2d(w, n) for n in small_names], [total[n] for n in small_names],
                       [flat2d(mom, n) for n in small_names], [flat2d(var, n) for n in small_names])
    for k, n in enumerate(small_names):
        grads[n], delta[n], new_m[n], new_v[n] = total[n], upd[0][k], upd[1][k], upd[2][k]

    def shaped(d):
        return [d[n].reshape(w[n].shape) for n in _PARAM_ORDER]

    return (loss_total, grad_x.reshape(B, S, D), *shaped(grads), *shaped(delta), *shaped(new_m), *shaped(new_v))
```

```python
import functools
import math

import jax
import jax.numpy as jnp
from jax import lax
from jax.experimental import pallas as pl
from jax.experimental.pallas import tpu as pltpu

F32 = jnp.float32
BF16 = jnp.bfloat16
EPS = 1e-6

D_MODEL = 1024
MEM_LEN = 256
GM_WIDTH = 512
GM_CHUNK = 128
GM_GROUPS = 4
HG_HEADS = 4
HG_DIM = 128
HG_CHUNK = 64
XA_HEADS = 4
XA_DIM = 128
BR_WIDTH = 512
D_FF = 2816
IN_WIDTH = 6656
N_CHIPS = 4
N_DEV = 8

ADAM_LR = 0.001
ADAM_B1 = 0.9
ADAM_B2 = 0.999
ADAM_EPS = 1e-08
ADAM_WD = 0.01
ADAM_STEP = 10

COL_ZU, COL_ZV, COL_HQ, COL_HF, COL_HI, COL_HG, COL_XQ = 0, 1, 2, 3, 4, 5, 6
COL_GATE0 = 3584

VMEM_LIMIT_BYTES = 48 * 1024 * 1024
MESH_ID = pl.DeviceIdType.MESH


def _cp(*sem):
    return pltpu.CompilerParams(dimension_semantics=sem, vmem_limit_bytes=VMEM_LIMIT_BYTES)


def _pallas(body, *, out_shape, **kw):
    def pin(s):
        return pltpu.HBM(s.shape, s.dtype) if isinstance(s, jax.ShapeDtypeStruct) else s

    out_shape = tuple(pin(s) for s in out_shape) if isinstance(out_shape, (tuple, list)) else pin(out_shape)
    call = pl.pallas_call(body, out_shape=out_shape, **kw)

    def run(*operands):
        return call(*[pltpu.with_memory_space_constraint(o, pltpu.HBM) if jnp.issubdtype(o.dtype, jnp.floating)
                      else o for o in operands])

    return run


def _dot(a, b):
    return lax.dot_general(a.astype(BF16), b.astype(BF16), (((1,), (0,)), ((), ())), preferred_element_type=F32)


def _dot_nt(a, b):
    return lax.dot_general(a.astype(BF16), b.astype(BF16), (((1,), (1,)), ((), ())), preferred_element_type=F32)


def _dot_tn(a, b):
    return lax.dot_general(a.astype(BF16), b.astype(BF16), (((0,), (0,)), ((), ())), preferred_element_type=F32)


def _dot_01(mask01, x):
    hi = x.astype(BF16)
    r1 = x - hi.astype(F32)
    mid = r1.astype(BF16)
    lo = (r1 - mid.astype(F32)).astype(BF16)
    m = mask01.astype(BF16)
    dn = (((1,), (0,)), ((), ()))
    return (lax.dot_general(m, hi, dn, preferred_element_type=F32)
            + lax.dot_general(m, mid, dn, preferred_element_type=F32)
            + lax.dot_general(m, lo, dn, preferred_element_type=F32))


def _sigmoid(z):
    return 1.0 / (1.0 + jnp.exp(-z))


_GELU_C = math.sqrt(2.0 / math.pi)


def _gelu_and_grad(z):
    inner = _GELU_C * (z + 0.044715 * z * z * z)
    t = jnp.tanh(inner)
    val = 0.5 * z * (1.0 + t)
    grad = 0.5 * (1.0 + t) + 0.5 * z * (1.0 - t * t) * _GELU_C * (1.0 + 3.0 * 0.044715 * z * z)
    return val, grad


def _row_tile(n, want=512):
    t = min(want, n)
    assert n % t == 0
    return t


def _pcall(body, operands, *, name, grid, in_specs, out_specs, out_shape, scratch_shapes=(), semantics, riders=()):
    single = not isinstance(out_shape, (tuple, list))
    out_specs = (out_specs,) if single else tuple(out_specs)
    out_shape = (out_shape,) if single else tuple(out_shape)
    if not riders:
        res = _pallas(body, name=name, grid=grid, in_specs=list(in_specs), out_specs=out_specs,
                      out_shape=out_shape, scratch_shapes=list(scratch_shapes),
                      compiler_params=_cp(*semantics))(*operands)
        return (res[0] if single else res), []
    n_in, n_out, n_scr = len(in_specs), len(out_shape), len(scratch_shapes)
    ex_in = [len(ex.operands) for ex in riders]
    ex_out = [len(ex.out_shape) for ex in riders]
    ex_scr = [len(ex.scratch) for ex in riders]
    tot_in, tot_out = n_in + sum(ex_in), n_out + sum(ex_out)

    def wrapped(*refs):
        ins, outs, scr = refs[:tot_in], refs[tot_in:tot_in + tot_out], refs[tot_in + tot_out:]
        ids = [pl.program_id(d) for d in range(len(grid))]
        first = functools.reduce(lambda p, t: p & t, [i == 0 for i in ids])
        last = functools.reduce(lambda p, t: p & t, [i == n - 1 for i, n in zip(ids, grid)])
        parts, oi, oo, os_ = [], n_in, n_out, n_scr
        for k in range(len(riders)):
            parts.append((ins[oi:oi + ex_in[k]], outs[oo:oo + ex_out[k]], scr[os_:os_ + ex_scr[k]]))
            oi, oo, os_ = oi + ex_in[k], oo + ex_out[k], os_ + ex_scr[k]

        @pl.when(first)
        def _():
            for ex, part in zip(riders, parts):
                ex.start(*part)

        step, total = 0, 1
        for i, n in zip(ids, grid):
            step, total = step * n + i, total * n
        for ex, part in zip(riders, parts):
            if ex.mid is not None:
                @pl.when(step == min(total - 1, int(total * ex.mid_at)))
                def _(ex=ex, part=part):
                    ex.mid(*part)

        body(*ins[:n_in], *outs[:n_out], *scr[:n_scr])

        @pl.when(last)
        def _():
            for ex, part in zip(riders, parts):
                ex.finish(*part)

    aliases, oi, oo = {}, n_in, n_out
    all_ops, all_shapes, all_scr = list(operands), list(out_shape), list(scratch_shapes)
    for k, ex in enumerate(riders):
        aliases.update({oi + a: oo + b for a, b in ex.aliases.items()})
        oi, oo = oi + ex_in[k], oo + ex_out[k]
        all_ops += list(ex.operands)
        all_shapes += [pltpu.HBM(s.shape, s.dtype) for s in ex.out_shape]
        all_scr += list(ex.scratch)
    res = _pallas(
        wrapped, name=name, grid=grid, in_specs=list(in_specs) + [HBM_SPEC] * sum(ex_in),
        out_specs=out_specs + (HBM_SPEC,) * sum(ex_out), out_shape=tuple(all_shapes), scratch_shapes=all_scr,
        input_output_aliases=aliases, compiler_params=_cp(*(["arbitrary"] * len(grid))))(*all_ops)
    own = res[0] if single else tuple(res[:n_out])
    carried, oo = [], n_out
    for k in range(len(riders)):
        carried.append(list(res[oo:oo + ex_out[k]]))
        oo += ex_out[k]
    return own, carried


def _carried(out, carried, riders):
    return (out, carried) if riders else out


def _matmul(name, operands, *, grid, in_specs, o_spec, out_shape, out_dtype, dims, riders=()):
    nk = grid[2]
    assert nk == 1 or out_dtype == F32

    def body(a_ref, b_ref, o_ref):
        part = lax.dot_general(a_ref[...].astype(BF16), b_ref[...].astype(BF16), (dims, ((), ())),
                               preferred_element_type=F32)
        if nk == 1:
            o_ref[...] = part.astype(o_ref.dtype)
        else:
            k = pl.program_id(2)

            @pl.when(k == 0)
            def _():
                o_ref[...] = part

            @pl.when(k > 0)
            def _():
                o_ref[...] += part

    out, carried = _pcall(body, operands, name=name, grid=grid, in_specs=in_specs, out_specs=o_spec,
                          out_shape=jax.ShapeDtypeStruct(out_shape, out_dtype),
                          semantics=("parallel", "parallel", "arbitrary"), riders=riders)
    return (out, carried) if riders else out


NN = ((1,), (0,))
NT = ((1,), (1,))
TN = ((0,), (0,))
_TN_TOKENS = 4096


def _mm_cs(name, a, w, out_dtype, riders=()):
    M, K = a.shape
    nq, _, wd = w.shape
    tm = _row_tile(M)
    return _matmul(name, (a, w), grid=(nq, M // tm, 1),
                   in_specs=[pl.BlockSpec((tm, K), lambda j, i, k: (i, 0)),
                             pl.BlockSpec((None, K, wd), lambda j, i, k: (j, 0, 0))],
                   o_spec=pl.BlockSpec((tm, wd), lambda j, i, k: (i, j)),
                   out_shape=(M, nq * wd), out_dtype=out_dtype, dims=NN, riders=riders)


def _mm_rs(name, a, w, out_dtype):
    M, K = a.shape
    N = w.shape[1]
    tm = _row_tile(M)
    return _matmul(name, (a, w), grid=(M // tm, 1, 1),
                   in_specs=[pl.BlockSpec((tm, K), lambda i, j, k: (i, 0)), pl.BlockSpec((K, N), lambda i, j, k: (0, 0))],
                   o_spec=pl.BlockSpec((tm, N), lambda i, j, k: (i, 0)),
                   out_shape=(M, N), out_dtype=out_dtype, dims=NN)


def _mm_nt_rs(name, g, w, out_dtype, riders=()):
    M, N = g.shape
    K = w.shape[0]
    to = K
    tm = _row_tile(M)
    return _matmul(name, (g, w), grid=(M // tm, K // to, 1),
                   in_specs=[pl.BlockSpec((tm, N), lambda i, j, k: (i, 0)),
                             pl.BlockSpec((to, N), lambda i, j, k: (j, 0))],
                   o_spec=pl.BlockSpec((tm, to), lambda i, j, k: (i, j)),
                   out_shape=(M, K), out_dtype=out_dtype, dims=NT, riders=riders)


def _mm_nt_cs(name, g, w, out_dtype, riders=(), stacked=False, norm_bwd=None):
    M = g.shape[-2]
    nq, K, wd = w.shape
    tm = _row_tile(M, 256)

    def product(g_ref, w_ref):
        acc = None
        for q in range(nq):
            gq = g_ref[q // 2, :, (q % 2) * wd:(q % 2 + 1) * wd] if stacked else g_ref[:, q * wd:(q + 1) * wd]
            part = _dot_nt(gq, w_ref[q])
            acc = part if acc is None else acc + part
        return acc

    def body(g_ref, w_ref, o_ref):
        o_ref[...] = product(g_ref, w_ref).astype(o_ref.dtype)

    def body_norm(g_ref, w_ref, x_ref, gain_ref, dr_ref, dx_ref, dg_ref):
        @pl.when(pl.program_id(0) == 0)
        def _():
            dg_ref[...] = jnp.zeros_like(dg_ref)

        dx, dg = _rms_bwd_rows(x_ref[...], gain_ref[...], product(g_ref, w_ref))
        dg_ref[...] += dg
        dx_ref[...] = dx + dr_ref[...]

    g_spec = (pl.BlockSpec((2, tm, 2 * wd), lambda i: (0, i, 0)) if stacked
              else pl.BlockSpec((tm, nq * wd), lambda i: (i, 0)))
    w_spec = pl.BlockSpec((nq, K, wd), lambda i: (0, 0, 0))
    row = pl.BlockSpec((tm, K), lambda i: (i, 0))
    if norm_bwd is None:
        return _carried(*_pcall(
            body, (g, w), name=name, grid=(M // tm,), in_specs=[g_spec, w_spec], out_specs=row,
            out_shape=jax.ShapeDtypeStruct((M, K), out_dtype), semantics=("parallel",), riders=riders), riders)
    vec = pl.BlockSpec((1, K), lambda i: (0, 0))
    return _carried(*_pcall(
        body_norm, (g, w) + tuple(norm_bwd), name=name, grid=(M // tm,),
        in_specs=[g_spec, w_spec, row, vec, row], out_specs=(row, vec),
        out_shape=(jax.ShapeDtypeStruct((M, K), F32), jax.ShapeDtypeStruct((1, K), F32)),
        semantics=("arbitrary",), riders=riders), riders)


def _mm_tn_rs(name, a, g, to, tn=512):
    T, M = a.shape
    N = g.shape[1]
    tt = _row_tile(T, _TN_TOKENS)
    tn = min(tn, N)
    return _matmul(name, (a, g), grid=(M // to, N // tn, T // tt),
                   in_specs=[pl.BlockSpec((tt, to), lambda i, j, k: (k, i)),
                             pl.BlockSpec((tt, tn), lambda i, j, k: (k, j))],
                   o_spec=pl.BlockSpec((to, tn), lambda i, j, k: (i, j)),
                   out_shape=(M, N), out_dtype=F32, dims=TN)


def _mm_tn_cs(name, a, g, nq, to, riders=(), stacked=False):
    T, M = a.shape
    wd = g.shape[-1] * (2 if stacked else 1) // nq
    tt = _row_tile(T, _TN_TOKENS)
    g_spec = (pl.BlockSpec((None, tt, wd), lambda i, j, k: (j // 2, k, j % 2)) if stacked
              else pl.BlockSpec((tt, wd), lambda i, j, k: (k, j)))
    return _matmul(name, (a, g), grid=(M // to, nq, T // tt),
                   in_specs=[pl.BlockSpec((tt, to), lambda i, j, k: (k, i)), g_spec],
                   o_spec=pl.BlockSpec((None, to, wd), lambda i, j, k: (j, i, 0)),
                   out_shape=(nq, M, wd), out_dtype=F32, dims=TN, riders=riders)


def _rms_fwd(name, x, g, riders=()):
    T, D = x.shape
    tm = _row_tile(T)

    def body(x_ref, g_ref, o_ref):
        o_ref[...] = _rms_rows(x_ref[...], g_ref[...]).astype(o_ref.dtype)

    return _carried(*_pcall(
        body, (x, g), name=name, grid=(T // tm,),
        in_specs=[pl.BlockSpec((tm, D), lambda i: (i, 0)), pl.BlockSpec((1, D), lambda i: (0, 0))],
        out_specs=pl.BlockSpec((tm, D), lambda i: (i, 0)),
        out_shape=jax.ShapeDtypeStruct((T, D), BF16), semantics=("parallel",), riders=riders), riders)


def _rms_rows(xv, gain):
    return xv * lax.rsqrt(jnp.mean(xv * xv, axis=-1, keepdims=True) + EPS) * gain


def _rms_bwd_rows(xv, gain, dh):
    r = lax.rsqrt(jnp.mean(xv * xv, axis=-1, keepdims=True) + EPS)
    n = xv * r
    dn = dh * gain
    return r * (dn - n * jnp.mean(dn * n, axis=-1, keepdims=True)), jnp.sum(dh * n, axis=0, keepdims=True)


def _rms_bwd(name, x, g, dh, dres):
    T, D = x.shape
    tm = _row_tile(T)
    has_res = dres is not None

    def body(*refs):
        if has_res:
            x_ref, g_ref, dh_ref, dr_ref, dx_ref, dg_ref = refs
        else:
            x_ref, g_ref, dh_ref, dx_ref, dg_ref = refs

        @pl.when(pl.program_id(0) == 0)
        def _():
            dg_ref[...] = jnp.zeros_like(dg_ref)

        dx, dg = _rms_bwd_rows(x_ref[...], g_ref[...], dh_ref[...])
        dg_ref[...] += dg
        if has_res:
            dx = dx + dr_ref[...]
        dx_ref[...] = dx

    row = pl.BlockSpec((tm, D), lambda i: (i, 0))
    vec = pl.BlockSpec((1, D), lambda i: (0, 0))
    ops = (x, g, dh, dres) if has_res else (x, g, dh)
    return _pallas(
        body, name=name, grid=(T // tm,), in_specs=[row, vec, row] + ([row] if has_res else []),
        out_specs=(row, vec),
        out_shape=(jax.ShapeDtypeStruct((T, D), F32), jax.ShapeDtypeStruct((1, D), F32)),
        compiler_params=_cp("arbitrary"),
    )(*ops)


def _proj_res_norm(name, a, w, res, gain):
    M, K = a.shape
    N = w.shape[1]
    tm = _row_tile(M)

    def body(a_ref, w_ref, r_ref, g_ref, x_ref, h_ref):
        xv = _dot(a_ref[...], w_ref[...]) + r_ref[...]
        x_ref[...] = xv
        h_ref[...] = _rms_rows(xv, g_ref[...]).astype(h_ref.dtype)

    row = pl.BlockSpec((tm, N), lambda i: (i, 0))
    return _pallas(
        body, name=name, grid=(M // tm,),
        in_specs=[pl.BlockSpec((tm, K), lambda i: (i, 0)), pl.BlockSpec((K, N), lambda i: (0, 0)), row,
                  pl.BlockSpec((1, N), lambda i: (0, 0))],
        out_specs=(row, row), out_shape=(jax.ShapeDtypeStruct((M, N), F32), jax.ShapeDtypeStruct((M, N), BF16)),
        compiler_params=_cp("parallel"),
    )(a, w, res, gain)


def _proj_res_loss(name, a, w, res, tgt, gain):
    M, K = a.shape
    D = w.shape[1]
    tm = _row_tile(M)

    def body(a_ref, w_ref, r_ref, t_ref, g_ref, dx_ref, dg_ref, loss_ref):
        @pl.when(pl.program_id(0) == 0)
        def _():
            dg_ref[...] = jnp.zeros_like(dg_ref)
            loss_ref[...] = jnp.zeros_like(loss_ref)

        xv = _dot(a_ref[...], w_ref[...]) + r_ref[...]
        gv = g_ref[...]
        diff = _rms_rows(xv, gv) - t_ref[...]
        loss_ref[...] += 0.5 * jnp.sum(jnp.mean(diff * diff, axis=-1, keepdims=True))
        dx, dg = _rms_bwd_rows(xv, gv, diff * (1.0 / D))
        dg_ref[...] += dg
        dx_ref[...] = dx

    row = pl.BlockSpec((tm, D), lambda i: (i, 0))
    vec = pl.BlockSpec((1, D), lambda i: (0, 0))
    return _pallas(
        body, name=name, grid=(M // tm,),
        in_specs=[pl.BlockSpec((tm, K), lambda i: (i, 0)), pl.BlockSpec((K, D), lambda i: (0, 0)), row, row, vec],
        out_specs=(row, vec, pl.BlockSpec((8, 128), lambda i: (0, 0))),
        out_shape=(jax.ShapeDtypeStruct((M, D), F32), jax.ShapeDtypeStruct((1, D), F32),
                   jax.ShapeDtypeStruct((8, 128), F32)),
        compiler_params=_cp("arbitrary"),
    )(a, w, res, tgt, gain)


def _gmlp_pieces(zu, zv, lng, lnb, ws_ref, bs_ref):
    u, du = _gelu_and_grad(zu)
    v, dv = _gelu_and_grad(zv)
    mu = jnp.mean(v, axis=-1, keepdims=True)
    vc = v - mu
    rstd = lax.rsqrt(jnp.mean(vc * vc, axis=-1, keepdims=True) + EPS)
    vhat = vc * rstd
    vn = vhat * lng + lnb
    row = lax.broadcasted_iota(jnp.int32, (GM_CHUNK, GM_CHUNK), 0)
    col = lax.broadcasted_iota(jnp.int32, (GM_CHUNK, GM_CHUNK), 1)
    tril = row >= col
    wms, mixed = [], []
    for g in range(GM_GROUPS):
        sl = slice(g * 128, (g + 1) * 128)
        wm = jnp.where(tril, ws_ref[g], 0.0)
        wms.append(wm)
        mixed.append(_dot(wm, vn[:, sl]) + bs_ref[g])
    return u, du, dv, rstd, vhat, vn, wms, mixed, tril


def _gmlp_fwd(proj, lng, lnb, ws, bs_col):
    T = proj.shape[0]
    n = T // GM_CHUNK

    def body(zu_ref, zv_ref, lng_ref, lnb_ref, ws_ref, bs_ref, o_ref):
        u, _, _, _, _, _, _, mixed, _ = _gmlp_pieces(zu_ref[...].astype(F32), zv_ref[...].astype(F32),
                                                     lng_ref[...], lnb_ref[...],
                                                     ws_ref, bs_ref)
        for g in range(GM_GROUPS):
            sl = slice(g * 128, (g + 1) * 128)
            o_ref[:, sl] = (u[:, sl] * mixed[g]).astype(o_ref.dtype)

    vec = pl.BlockSpec((1, GM_WIDTH), lambda i: (0, 0))
    return _pallas(
        body, name="gmlp_fwd", grid=(n,),
        in_specs=[pl.BlockSpec((GM_CHUNK, 512), lambda i: (i, COL_ZU)),
                  pl.BlockSpec((GM_CHUNK, 512), lambda i: (i, COL_ZV)),
                  vec, vec,
                  pl.BlockSpec((GM_GROUPS, 128, 128), lambda i: (0, 0, 0)),
                  pl.BlockSpec((GM_GROUPS, 128, 1), lambda i: (0, 0, 0))],
        out_specs=pl.BlockSpec((GM_CHUNK, 512), lambda i: (i, 0)),
        out_shape=jax.ShapeDtypeStruct((T, GM_WIDTH), BF16), compiler_params=_cp("parallel"),
    )(proj, proj, lng, lnb, ws, bs_col)


def _gmlp_bwd(proj, d_out, lng, lnb, ws, bs_col, riders=()):
    T = proj.shape[0]
    n = T // GM_CHUNK

    def body(zu_ref, zv_ref, do_ref, lng_ref, lnb_ref, ws_ref, bs_ref,
             dz_ref, dws_ref, dbs_ref, dlng_ref, dlnb_ref, dm_acc):
        i = pl.program_id(0)

        @pl.when(i == 0)
        def _():
            dws_ref[...] = jnp.zeros_like(dws_ref)
            dlng_ref[...] = jnp.zeros_like(dlng_ref)
            dlnb_ref[...] = jnp.zeros_like(dlnb_ref)
            dm_acc[...] = jnp.zeros_like(dm_acc)

        lng_v = lng_ref[...]
        u, du, dv, rstd, vhat, vn, wms, mixed, tril = _gmlp_pieces(zu_ref[...].astype(F32), zv_ref[...].astype(F32),
                                                                  lng_v, lnb_ref[...],
                                                                  ws_ref, bs_ref)
        do = do_ref[...]
        dvn_parts = []
        for g in range(GM_GROUPS):
            sl = slice(g * 128, (g + 1) * 128)
            dog = do[:, sl]
            dz_ref[:, sl] = (dog * mixed[g] * du[:, sl]).astype(dz_ref.dtype)
            dmix = dog * u[:, sl]
            dm_acc[:, sl] += dmix
            dws_ref[g] += jnp.where(tril, _dot_nt(dmix, vn[:, sl]), 0.0)
            dvn_parts.append(_dot_tn(wms[g], dmix))
        dvn = jnp.concatenate(dvn_parts, axis=1)
        dlng_ref[...] += jnp.sum(dvn * vhat, axis=0, keepdims=True)
        dlnb_ref[...] += jnp.sum(dvn, axis=0, keepdims=True)
        dvh = dvn * lng_v
        dvv = rstd * (dvh - jnp.mean(dvh, axis=-1, keepdims=True)
                      - vhat * jnp.mean(dvh * vhat, axis=-1, keepdims=True))
        dz_ref[:, GM_WIDTH:] = (dvv * dv).astype(dz_ref.dtype)

        @pl.when(i == n - 1)
        def _():
            for g in range(GM_GROUPS):
                dbs_ref[g] = jnp.sum(dm_acc[:, g * 128:(g + 1) * 128], axis=1, keepdims=True)

    vec = pl.BlockSpec((1, GM_WIDTH), lambda i: (0, 0))
    wsp = pl.BlockSpec((GM_GROUPS, 128, 128), lambda i: (0, 0, 0))
    bsp = pl.BlockSpec((GM_GROUPS, 128, 1), lambda i: (0, 0, 0))
    return _carried(*_pcall(
        body, (proj, proj, d_out, lng, lnb, ws, bs_col), name="gmlp_bwd", grid=(n,),
        in_specs=[pl.BlockSpec((GM_CHUNK, 512), lambda i: (i, COL_ZU)),
                  pl.BlockSpec((GM_CHUNK, 512), lambda i: (i, COL_ZV)),
                  pl.BlockSpec((None, GM_CHUNK, 512), lambda i: (0, i, 0)), vec, vec, wsp, bsp],
        out_specs=(pl.BlockSpec((GM_CHUNK, 2 * GM_WIDTH), lambda i: (i, 0)), wsp, bsp, vec, vec),
        out_shape=(jax.ShapeDtypeStruct((T, 2 * GM_WIDTH), BF16),
                   jax.ShapeDtypeStruct((GM_GROUPS, 128, 128), F32), jax.ShapeDtypeStruct((GM_GROUPS, 128, 1), F32),
                   jax.ShapeDtypeStruct((1, GM_WIDTH), F32), jax.ShapeDtypeStruct((1, GM_WIDTH), F32)),
        scratch_shapes=[pltpu.VMEM((GM_CHUNK, GM_WIDTH), F32)],
        semantics=("arbitrary",), riders=riders), riders)


def _hgrn_lower_bound(lbl):
    return 1.0 / (1.0 + jnp.exp(lbl[1:2, :] - lbl[0:1, :]))


def _hgrn_gates(hq, hf, lb):
    C = HG_CHUNK
    sg = _sigmoid(hf)
    fg = lb + (1.0 - lb) * sg
    sq = _sigmoid(hq)
    row = lax.broadcasted_iota(jnp.int32, (C, C), 0)
    col = lax.broadcasted_iota(jnp.int32, (C, C), 1)
    tril = row >= col
    logf = jnp.log(fg)
    a = _dot_01(tril, logf)
    a_last = jnp.sum(logf, axis=0, keepdims=True)
    first_half = lax.broadcasted_iota(jnp.int32, logf.shape, 0) < (C // 2)
    a_mid = jnp.sum(jnp.where(first_half, logf, 0.0), axis=0, keepdims=True)
    ea, ei, eki, ekl = jnp.exp(a), jnp.exp(a - a_mid), jnp.exp(a_mid - a), jnp.exp(a_last - a)
    k = 1.0 - fg
    q = hq * sq
    qi = (q * ei).astype(BF16).astype(F32)
    ki = (k * eki).astype(BF16).astype(F32)
    return dict(sg=sg, fg=fg, sq=sq, tril=tril, ea=ea, ei=ei, eki=eki, ekl=ekl, e_last=jnp.exp(a_last),
                qe=q * ea, qi=qi, ki=ki, kl=k * ekl)


def _heads(x):
    return [x[:, h * HG_DIM:(h + 1) * HG_DIM] for h in range(HG_HEADS)]


def _hgrn_fwd(proj, lbl, gh, B, S, riders=()):
    C = HG_CHUNK
    NC = S // C
    W = HG_HEADS * HG_DIM

    def body(q_ref, f_ref, i_ref, g_ref, lbl_ref, gh_ref, o_ref, bo_ref, st_ref, state):
        @pl.when(pl.program_id(0) == 0)
        def _():
            state[...] = jnp.zeros_like(state)

        lb = _hgrn_lower_bound(lbl_ref[...])
        ghv = gh_ref[...]
        for b in range(B):
            gt = _hgrn_gates(q_ref[b].astype(F32), f_ref[b].astype(F32), lb)
            v = _heads(i_ref[b])
            qe, qi, ki, kl, e_last = (_heads(gt[n]) for n in ("qe", "qi", "ki", "kl", "e_last"))
            outs, normed = [], []
            for h in range(HG_HEADS):
                p = jnp.where(gt["tril"], _dot_nt(qi[h], ki[h]), 0.0)
                st = state[b, h]
                st_ref[b, h] = st
                o = _dot_nt(qe[h], st) + _dot(p, v[h])
                state[b, h] = st * e_last[h] + _dot_tn(v[h], kl[h])
                outs.append(o)
                normed.append(o * lax.rsqrt(jnp.mean(o * o, axis=-1, keepdims=True) + EPS) * ghv)
            o_ref[b] = jnp.concatenate(outs, axis=1)
            hg = g_ref[b].astype(F32)
            bo_ref[b] = (jnp.concatenate(normed, axis=1) * (hg * _sigmoid(hg))).astype(bo_ref.dtype)

    def col(cb):
        return pl.BlockSpec((B, C, 512), lambda c: (0, c, cb))

    tile = pl.BlockSpec((B, C, W), lambda c: (0, c, 0))
    proj3 = proj.reshape(B, S, proj.shape[-1])
    out, carried = _pcall(
        body, (proj3, proj3, proj3, proj3, lbl, gh), name="hgrn_fwd", grid=(NC,),
        in_specs=[col(COL_HQ), col(COL_HF), col(COL_HI), col(COL_HG),
                  pl.BlockSpec((2, W), lambda c: (0, 0)), pl.BlockSpec((1, HG_DIM), lambda c: (0, 0))],
        out_specs=(tile, tile, pl.BlockSpec((B, None, HG_HEADS, 128, 128), lambda c: (0, c, 0, 0, 0))),
        out_shape=(jax.ShapeDtypeStruct((B, S, W), F32), jax.ShapeDtypeStruct((B, S, W), BF16),
                   jax.ShapeDtypeStruct((B, NC, HG_HEADS, 128, 128), F32)),
        scratch_shapes=[pltpu.VMEM((B, HG_HEADS, 128, 128), F32)],
        semantics=("arbitrary",), riders=riders)
    o_h, b_out, states = out
    out = (o_h, b_out.reshape(B * S, W), states)
    return (out, carried) if riders else out


def _hgrn_bwd(proj, o_saved, states, d_out, lbl, gh, others, B, S, riders=()):
    C = HG_CHUNK
    NC = S // C
    W = HG_HEADS * HG_DIM
    d_gm, d_xq, d_gates = (t.reshape(B, S, t.shape[-1]) for t in others)
    own0 = d_gm.shape[-1]
    xq0 = own0 + 4 * W
    gates0 = xq0 + d_xq.shape[-1]

    def body(q_ref, f_ref, i_ref, g_ref, o_ref, st_ref, do_ref, lbl_ref, gh_ref, gm_ref, xq_ref, gates_ref,
             d_ref, dlbl_ref, dgh_ref, dstate, dlb_acc):
        c = pl.program_id(0)
        d_ref[:, :, :own0] = gm_ref[...]
        d_ref[:, :, xq0:gates0] = xq_ref[...]
        d_ref[:, :, gates0:] = gates_ref[...]

        def put(b, k, val):
            d_ref[b, :, own0 + k * W:own0 + (k + 1) * W] = val.astype(d_ref.dtype)

        @pl.when(c == 0)
        def _():
            dstate[...] = jnp.zeros_like(dstate)
            dgh_ref[...] = jnp.zeros_like(dgh_ref)
            dlb_acc[...] = jnp.zeros_like(dlb_acc)

        lb = _hgrn_lower_bound(lbl_ref[...])
        ghv = gh_ref[...]
        row = lax.broadcasted_iota(jnp.int32, (C, C), 0)
        colm = lax.broadcasted_iota(jnp.int32, (C, C), 1)
        triu = colm >= row
        for b in range(B):
            hq, hg = q_ref[b].astype(F32), g_ref[b].astype(F32)
            gt = _hgrn_gates(hq, f_ref[b].astype(F32), lb)
            tril = gt["tril"]
            v = _heads(i_ref[b])
            qe, qi, ki, kl, e_last = (_heads(gt[n]) for n in ("qe", "qi", "ki", "kl", "e_last"))
            sgg = _sigmoid(hg)
            don_all = do_ref[b] * (hg * sgg)
            o, don = _heads(o_ref[b]), _heads(don_all)
            d_qe, d_qi, d_ki, d_kl, dv, n_all, dal = [], [], [], [], [], [], []
            for h in range(HG_HEADS):
                r = lax.rsqrt(jnp.mean(o[h] * o[h], axis=-1, keepdims=True) + EPS)
                n = o[h] * r
                n_all.append(n)
                dgh_ref[...] += jnp.sum(don[h] * n, axis=0, keepdims=True)
                dn = don[h] * ghv
                d_o = r * (dn - n * jnp.mean(dn * n, axis=-1, keepdims=True))
                st, dst = st_ref[b, h], dstate[b, h]
                p = jnp.where(tril, _dot_nt(qi[h], ki[h]), 0.0)
                dp = jnp.where(tril, _dot_nt(d_o, v[h]), 0.0)
                d_qe.append(_dot(d_o, st))
                d_qi.append(_dot(dp, ki[h]))
                d_ki.append(_dot_tn(dp, qi[h]))
                d_kl.append(_dot(v[h], dst))
                dv.append(_dot_tn(p, d_o) + _dot_nt(kl[h], dst))
                dstate[b, h] = dst * e_last[h] + _dot_tn(d_o, qe[h])
                dal.append(jnp.sum(dst * st, axis=0, keepdims=True) * e_last[h])
            d_qe, d_qi, d_ki, d_kl, n_all, dal = (jnp.concatenate(t, axis=1)
                                                  for t in (d_qe, d_qi, d_ki, d_kl, n_all, dal))
            put(b, 3, do_ref[b] * n_all * jnp.tile(ghv, (1, HG_HEADS)) * (sgg * (1.0 + hg * (1.0 - sgg))))
            put(b, 2, jnp.concatenate(dv, axis=1))
            d_a_last = dal + jnp.sum(d_kl * gt["kl"], axis=0, keepdims=True)
            dq = d_qe * gt["ea"] + d_qi * gt["ei"]
            dk = d_ki * gt["eki"] + d_kl * gt["ekl"]
            da = d_qe * gt["qe"] + d_qi * gt["qi"] - d_ki * gt["ki"] - d_kl * gt["kl"]
            dlogf = _dot_01(triu, da) + d_a_last
            sg, sq = gt["sg"], gt["sq"]
            dfg = dlogf / gt["fg"] - dk
            put(b, 1, dfg * (1.0 - lb) * sg * (1.0 - sg))
            dlb_acc[...] += jnp.sum(dfg * (1.0 - sg), axis=0, keepdims=True)
            put(b, 0, dq * (sq * (1.0 + hq * (1.0 - sq))))

        @pl.when(c == NC - 1)
        def _():
            dlb = dlb_acc[...]
            first = lax.broadcasted_iota(jnp.int32, (2, W), 0) == 0
            dlbl_ref[...] = jnp.where(first, dlb * lb * (1.0 - lb), -dlb * lb * (1.0 - lb))

    def col(cb):
        return pl.BlockSpec((B, C, 512), lambda c: (0, NC - 1 - c, cb))

    tile = pl.BlockSpec((B, C, W), lambda c: (0, NC - 1 - c, 0))
    proj3 = proj.reshape(B, S, proj.shape[-1])

    def rows(width):
        return pl.BlockSpec((B, C, width), lambda c: (0, NC - 1 - c, 0))

    width = proj.shape[-1]
    out, carried = _pcall(
        body, (proj3, proj3, proj3, proj3, o_saved, states, d_out.reshape(3, B, S, W), lbl, gh, d_gm, d_xq, d_gates),
        name="hgrn_bwd", grid=(NC,),
        in_specs=[col(COL_HQ), col(COL_HF), col(COL_HI), col(COL_HG), tile,
                  pl.BlockSpec((B, None, HG_HEADS, 128, 128), lambda c: (0, NC - 1 - c, 0, 0, 0)),
                  pl.BlockSpec((None, B, C, W), lambda c: (1, 0, NC - 1 - c, 0)),
                  pl.BlockSpec((2, W), lambda c: (0, 0)), pl.BlockSpec((1, HG_DIM), lambda c: (0, 0)),
                  rows(d_gm.shape[-1]), rows(d_xq.shape[-1]), rows(d_gates.shape[-1])],
        out_specs=(rows(width), pl.BlockSpec((2, W), lambda c: (0, 0)), pl.BlockSpec((1, HG_DIM), lambda c: (0, 0))),
        out_shape=(jax.ShapeDtypeStruct((B, S, width), BF16), jax.ShapeDtypeStruct((2, W), F32),
                   jax.ShapeDtypeStruct((1, HG_DIM), F32)),
        scratch_shapes=[pltpu.VMEM((B, HG_HEADS, 128, 128), F32), pltpu.VMEM((1, W), F32)],
        semantics=("arbitrary",), riders=riders)
    out = (out[0].reshape(B * S, width),) + tuple(out[1:])
    return (out, carried) if riders else out


_XA_SCALE = XA_DIM ** -0.5


def _attn_probs(qh, kh):
    s = _dot_nt(qh, kh) * _XA_SCALE
    e = jnp.exp(s - jnp.max(s, axis=-1, keepdims=True))
    return e / jnp.sum(e, axis=-1, keepdims=True)


def _attn_fwd(proj, kv, B, S):
    T = B * S
    tq = _row_tile(S)
    nq = S // tq
    W = XA_HEADS * XA_DIM

    def body(q_ref, kv_ref, o_ref):
        for h in range(XA_HEADS):
            sl = slice(h * 128, (h + 1) * 128)
            p = _attn_probs(q_ref[:, sl], kv_ref[:, sl])
            o_ref[:, sl] = _dot(p, kv_ref[:, W + h * 128:W + (h + 1) * 128]).astype(o_ref.dtype)

    return _pallas(
        body, name="attn_fwd", grid=(B, nq),
        in_specs=[pl.BlockSpec((tq, 512), lambda b, i: (b * nq + i, COL_XQ)),
                  pl.BlockSpec((MEM_LEN, 2 * W), lambda b, i: (b, 0))],
        out_specs=pl.BlockSpec((tq, W), lambda b, i: (b * nq + i, 0)),
        out_shape=jax.ShapeDtypeStruct((T, W), BF16), compiler_params=_cp("parallel", "parallel"),
    )(proj, kv)


def _attn_bwd(proj, kv, d_out, B, S):
    T = B * S
    tq = _row_tile(S)
    nq = S // tq
    W = XA_HEADS * XA_DIM

    def body(q_ref, kv_ref, do_ref, dq_ref, dkv_ref):
        @pl.when(pl.program_id(1) == 0)
        def _():
            dkv_ref[...] = jnp.zeros_like(dkv_ref)

        for h in range(XA_HEADS):
            sl = slice(h * 128, (h + 1) * 128)
            slv = slice(W + h * 128, W + (h + 1) * 128)
            qh = q_ref[:, sl]
            kh = kv_ref[:, sl]
            p = _attn_probs(qh, kh)
            dc = do_ref[:, sl]
            dp = _dot_nt(dc, kv_ref[:, slv])
            ds = p * (dp - jnp.sum(dp * p, axis=-1, keepdims=True)) * _XA_SCALE
            dq_ref[:, sl] = _dot(ds, kh).astype(dq_ref.dtype)
            dkv_ref[:, sl] += _dot_tn(ds, qh)
            dkv_ref[:, slv] += _dot_tn(p, dc)

    kvspec = pl.BlockSpec((MEM_LEN, 2 * W), lambda b, i: (b, 0))
    tile = pl.BlockSpec((tq, W), lambda b, i: (b * nq + i, 0))
    return _pallas(
        body, name="attn_bwd", grid=(B, nq),
        in_specs=[pl.BlockSpec((tq, 512), lambda b, i: (b * nq + i, COL_XQ)), kvspec,
                  pl.BlockSpec((None, tq, W), lambda b, i: (2, b * nq + i, 0))],
        out_specs=(tile, kvspec),
        out_shape=(jax.ShapeDtypeStruct((T, W), BF16), jax.ShapeDtypeStruct((B * MEM_LEN, 2 * W), F32)),
        compiler_params=_cp("parallel", "arbitrary"),
    )(proj, kv, d_out)


_MERGE_TM = 256
_GATE_W = 512


def _gate_specs(tm):
    base = COL_GATE0 // _GATE_W
    return [pl.BlockSpec((tm, _GATE_W), functools.partial(lambda i, k: (i, base + k), k=k)) for k in range(6)]


def _merge_fwd(a_out, b_out, c_out, wb, proj, riders=()):
    T = a_out.shape[0]
    tm = _row_tile(T, _MERGE_TM)
    nq, _, wd = wb.shape
    per_half = _GATE_W // wd

    def body(a_ref, b_ref, c_ref, w_ref, *rest):
        gates, (m_ref, up_ref) = rest[:6], rest[6:]
        for hf in range(2):
            cols = slice(hf * _GATE_W, (hf + 1) * _GATE_W)
            acc = None
            for n, br in enumerate((a_ref, b_ref, c_ref)):
                x = br[...]
                up = jnp.concatenate([_dot(x, w_ref[per_half * hf + j, n * BR_WIDTH:(n + 1) * BR_WIDTH, :])
                                      for j in range(per_half)], axis=1)
                up_ref[n, :, cols] = up.astype(up_ref.dtype)
                term = _sigmoid(gates[2 * n + hf][...].astype(F32)) * up
                acc = term if acc is None else acc + term
            m_ref[:, cols] = acc.astype(m_ref.dtype)

    br_spec = pl.BlockSpec((tm, BR_WIDTH), lambda i: (i, 0))
    return _carried(*_pcall(
        body, (a_out, b_out, c_out, wb, *([proj] * 6)), name="merge_fwd", grid=(T // tm,),
        in_specs=[br_spec, br_spec, br_spec,
                  pl.BlockSpec((nq, 3 * BR_WIDTH, wd), lambda i: (0, 0, 0))] + _gate_specs(tm),
        out_specs=(pl.BlockSpec((tm, D_MODEL), lambda i: (i, 0)), pl.BlockSpec((3, tm, D_MODEL), lambda i: (0, i, 0))),
        out_shape=(jax.ShapeDtypeStruct((T, D_MODEL), BF16), jax.ShapeDtypeStruct((3, T, D_MODEL), BF16)),
        semantics=("parallel",), riders=riders), riders)


def _branch_bwd_act(d_ups, wb, riders=()):
    _, T, D = d_ups.shape
    nq, _, wd = wb.shape
    tm = _row_tile(T)

    def body(d_ref, w_ref, o_ref):
        acc = None
        for q in range(nq):
            part = _dot_nt(d_ref[:, q * wd:(q + 1) * wd], w_ref[q])
            acc = part if acc is None else acc + part
        o_ref[...] = acc

    return _carried(*_pcall(
        body, (d_ups, wb), name="d_branch", grid=(3, T // tm),
        in_specs=[pl.BlockSpec((None, tm, D), lambda n, i: (n, i, 0)),
                  pl.BlockSpec((nq, BR_WIDTH, wd), lambda n, i: (0, n, 0))],
        out_specs=pl.BlockSpec((None, tm, BR_WIDTH), lambda n, i: (n, i, 0)),
        out_shape=jax.ShapeDtypeStruct((3, T, BR_WIDTH), F32), semantics=("parallel", "parallel"),
        riders=riders), riders)


def _branch_bwd_weight(name, br, d_ups, n):
    T = br.shape[0]
    D = d_ups.shape[2]
    wd = D // N_CHIPS
    tt = _row_tile(T, _TN_TOKENS)

    def body(b_ref, d_ref, o_ref):
        k = pl.program_id(0)
        for q in range(N_CHIPS):
            part = _dot_tn(b_ref[...], d_ref[:, q * wd:(q + 1) * wd])

            @pl.when(k == 0)
            def _():
                o_ref[q] = part

            @pl.when(k > 0)
            def _():
                o_ref[q] += part

    return _pallas(
        body, name=name, grid=(T // tt,),
        in_specs=[pl.BlockSpec((tt, BR_WIDTH), lambda k: (k, 0)),
                  pl.BlockSpec((None, tt, D), lambda k: (n, k, 0))],
        out_specs=pl.BlockSpec((N_CHIPS, BR_WIDTH, wd), lambda k: (0, 0, 0)),
        out_shape=jax.ShapeDtypeStruct((N_CHIPS, BR_WIDTH, wd), F32), compiler_params=_cp("arbitrary"),
    )(br, d_ups)


def _merge_bwd(d_merged, ups, proj, riders=()):
    T = d_merged.shape[0]
    tm = _row_tile(T, _MERGE_TM)

    def body(dm_ref, up_ref, *rest):
        gates, (dup_ref, dg_ref) = rest[:6], rest[6:]
        for hf in range(2):
            cols = slice(hf * _GATE_W, (hf + 1) * _GATE_W)
            dm = dm_ref[:, cols]
            for n in range(3):
                gate = _sigmoid(gates[2 * n + hf][...].astype(F32))
                dup_ref[n, :, cols] = (dm * gate).astype(dup_ref.dtype)
                dg_ref[:, n * D_MODEL + hf * _GATE_W:n * D_MODEL + (hf + 1) * _GATE_W] = (
                    dm * up_ref[n, :, cols].astype(F32) * gate * (1.0 - gate)).astype(dg_ref.dtype)

    tile = pl.BlockSpec((tm, D_MODEL), lambda i: (i, 0))
    tile3 = pl.BlockSpec((3, tm, D_MODEL), lambda i: (0, i, 0))
    return _carried(*_pcall(
        body, (d_merged, ups, *([proj] * 6)), name="merge_bwd", grid=(T // tm,),
        in_specs=[tile, tile3] + _gate_specs(tm),
        out_specs=(tile3, pl.BlockSpec((tm, 3 * D_MODEL), lambda i: (i, 0))),
        out_shape=(jax.ShapeDtypeStruct((3, T, D_MODEL), BF16), jax.ShapeDtypeStruct((T, 3 * D_MODEL), BF16)),
        semantics=("parallel",), riders=riders), riders)


_CONV_TF = D_FF // 2
_CONV_TS = 256
_HALO = 16


def _conv_fwd(ab, cw, cb, B, S):
    T = B * S
    ts = _row_tile(S, _CONV_TS)
    tf = _CONV_TF
    nb = D_FF // tf
    tps = S // ts
    hb = ts // _HALO

    def body(a_ref, p_ref, b_ref, w_ref, cb_ref, o_ref):
        start = (pl.program_id(0) % tps) == 0
        a = a_ref[...].astype(F32)
        prev = jnp.where(start, 0.0, p_ref[...].astype(F32))
        ext = jnp.concatenate([prev, a], axis=0)
        a1 = pltpu.roll(ext, 1, 0)[_HALO:, :]
        a2 = pltpu.roll(ext, 2, 0)[_HALO:, :]
        ac = cb_ref[...] + w_ref[0] * a2 + w_ref[1] * a1 + w_ref[2] * a
        o_ref[...] = (ac * _sigmoid(ac) * b_ref[...].astype(F32)).astype(o_ref.dtype)

    return _pallas(
        body, name="conv_fwd", grid=(T // ts, nb),
        in_specs=[pl.BlockSpec((ts, tf), lambda i, j: (i, j)),
                  pl.BlockSpec((_HALO, tf), lambda i, j: (jnp.maximum(i * hb - 1, 0), j)),
                  pl.BlockSpec((ts, tf), lambda i, j: (i, j + nb)),
                  pl.BlockSpec((3, 1, tf), lambda i, j: (0, 0, j)),
                  pl.BlockSpec((1, tf), lambda i, j: (0, j))],
        out_specs=pl.BlockSpec((ts, tf), lambda i, j: (i, j)),
        out_shape=jax.ShapeDtypeStruct((T, D_FF), BF16), compiler_params=_cp("parallel", "parallel"),
    )(ab, ab, ab, cw, cb)


def _conv_bwd(ab, d_ff, cw, cb, B, S, riders=()):
    T = B * S
    ts = _row_tile(S, _CONV_TS)
    tf = _CONV_TF
    nb = D_FF // tf
    tps = S // ts
    hb = ts // _HALO
    last_h = T // _HALO - 1
    n_ext = ts + _HALO

    def body(a_ref, ap_ref, an_ref, b_ref, bn_ref, d_ref, dn_ref, w_ref, cb_ref, dab_ref, dw_ref, dcb_ref):
        i = pl.program_id(1)

        @pl.when(i == 0)
        def _():
            dw_ref[...] = jnp.zeros_like(dw_ref)
            dcb_ref[...] = jnp.zeros_like(dcb_ref)

        start = (i % tps) == 0
        end = (i % tps) == tps - 1
        a = a_ref[...].astype(F32)
        ext = jnp.concatenate([jnp.where(start, 0.0, ap_ref[...].astype(F32)), a, an_ref[...].astype(F32)], axis=0)
        r1 = pltpu.roll(ext, 1, 0)[_HALO:, :]
        r2 = pltpu.roll(ext, 2, 0)[_HALO:, :]
        ac = cb_ref[...] + w_ref[0] * r2 + w_ref[1] * r1 + w_ref[2] * ext[_HALO:, :]
        sg = _sigmoid(ac)
        d_e = jnp.concatenate([d_ref[...].astype(F32), jnp.where(end, 0.0, dn_ref[...].astype(F32))], axis=0)
        b_e = jnp.concatenate([b_ref[...].astype(F32), bn_ref[...].astype(F32)], axis=0)
        dab_ref[1] = (d_e[:ts, :] * (ac * sg)[:ts, :]).astype(dab_ref.dtype)
        dac = d_e * b_e * sg * (1.0 + ac * (1.0 - sg))
        u1 = pltpu.roll(dac, n_ext - 1, 0)[:ts, :]
        u2 = pltpu.roll(dac, n_ext - 2, 0)[:ts, :]
        dac0 = dac[:ts, :]
        dab_ref[0] = (w_ref[2] * dac0 + w_ref[1] * u1 + w_ref[0] * u2).astype(dab_ref.dtype)
        dcb_ref[...] += jnp.sum(dac0, axis=0, keepdims=True)
        dw_ref[2] += jnp.sum(dac0 * a, axis=0, keepdims=True)
        dw_ref[1] += jnp.sum(dac0 * r1[:ts, :], axis=0, keepdims=True)
        dw_ref[0] += jnp.sum(dac0 * r2[:ts, :], axis=0, keepdims=True)

    def cur(off):
        return pl.BlockSpec((ts, tf), lambda j, i: (i, j + off))

    def nxt(off):
        return pl.BlockSpec((_HALO, tf), lambda j, i: (jnp.minimum((i + 1) * hb, last_h), j + off))

    return _carried(*_pcall(
        body, (ab, ab, ab, ab, ab, d_ff, d_ff, cw, cb), name="conv_bwd", grid=(nb, T // ts),
        in_specs=[cur(0), pl.BlockSpec((_HALO, tf), lambda j, i: (jnp.maximum(i * hb - 1, 0), j)), nxt(0),
                  cur(nb), nxt(nb), cur(0), nxt(0),
                  pl.BlockSpec((3, 1, tf), lambda j, i: (0, 0, j)), pl.BlockSpec((1, tf), lambda j, i: (0, j))],
        out_specs=(pl.BlockSpec((2, ts, tf), lambda j, i: (0, i, j)), pl.BlockSpec((3, 1, tf), lambda j, i: (0, 0, j)),
                   pl.BlockSpec((1, tf), lambda j, i: (0, j))),
        out_shape=(jax.ShapeDtypeStruct((2, T, D_FF), BF16),
                   jax.ShapeDtypeStruct((3, 1, D_FF), F32), jax.ShapeDtypeStruct((1, D_FF), F32)),
        semantics=("parallel", "arbitrary"), riders=riders), riders)


def _local_step(x, mem, tgt, p, comm, B, S):
    g = {}
    h = comm.carry("norm1", lambda r: _rms_fwd("norm1", x, p["norm1_g"], riders=r))
    proj = comm.carry("in_proj", lambda r: _mm_cs("in_proj", h, comm.w("w_in"), BF16, riders=r))
    a_out = _gmlp_fwd(proj, p["ln_v_g"], p["ln_v_b"], p["w_spatial"], p["b_spatial"])
    o_h, b_out, states = comm.carry(
        "hgrn_fwd", lambda r: _hgrn_fwd(proj, p["lb_logits"], p["hgrn_norm_g"], B, S, riders=r))
    memn = _rms_fwd("mem_norm", mem, p["mem_norm_g"])
    kv = _mm_rs("mem_kv", memn, comm.w("w_mem_kv"), F32)
    c_out = _attn_fwd(proj, kv, B, S)
    merged, ups = comm.carry(
        "merge_fwd", lambda r: _merge_fwd(a_out, b_out, c_out, comm.w("w_branch"), proj, riders=r))
    x1, h2 = _proj_res_norm("out_proj_norm2", merged, comm.w("w_out"), x, p["norm2_g"])
    ab = comm.carry("up_proj", lambda r: _mm_cs("up_proj", h2, comm.w("w_up"), BF16, riders=r))
    conv_w = comm.w("conv_w")
    ff = _conv_fwd(ab, conv_w, p["conv_b"], B, S)
    dx2, g["final_g"], loss = _proj_res_loss("down_proj_loss", ff, comm.w("w_down"), x1, tgt, p["final_g"])

    comm.grad("w_down", _mm_tn_rs("g_w_down", ff, dx2, to=D_FF // 2))
    d_ff = comm.carry("d_ff", lambda r: _mm_nt_rs("d_ff", dx2, comm.w("w_down"), BF16, riders=r))
    d_ab, g["conv_w"], g["conv_b"] = comm.carry(
        "conv_bwd", lambda r: _conv_bwd(ab, d_ff, conv_w, p["conv_b"], B, S, riders=r))
    comm.grad("w_up", _mm_tn_cs("g_w_up", h2, d_ab, N_CHIPS, to=512, stacked=True))
    d_x1, g["norm2_g"] = comm.carry("d_h2", lambda r: _mm_nt_cs(
        "d_h2_norm2_bwd", d_ab, comm.w("w_up"), F32, riders=r, stacked=True, norm_bwd=(x1, p["norm2_g"], dx2)))
    comm.grad("w_out", _mm_tn_rs("g_w_out", merged, d_x1, to=512))
    d_merged = _mm_nt_rs("d_merged", d_x1, comm.w("w_out"), F32)
    d_ups, d_gates = comm.carry("merge_bwd", lambda r: _merge_bwd(d_merged, ups, proj, riders=r))

    d_br = comm.carry("d_branch", lambda r: _branch_bwd_act(d_ups, comm.w("w_branch"), riders=r))
    comm.grad("w_branch", jnp.concatenate(
        [_branch_bwd_weight("g_w_branch%d" % n, br, d_ups, n) for n, br in enumerate((a_out, b_out, c_out))],
        axis=1))

    d_gm, g["w_spatial"], g["b_spatial"], g["ln_v_g"], g["ln_v_b"] = comm.carry(
        "gmlp_bwd", lambda r: _gmlp_bwd(proj, d_br, p["ln_v_g"], p["ln_v_b"], p["w_spatial"], p["b_spatial"],
                                        riders=r))
    d_xq, d_kv = _attn_bwd(proj, kv, d_br, B, S)
    comm.grad("w_mem_kv", _mm_tn_rs("g_w_mem_kv", memn, d_kv, to=512))
    d_memn = _mm_nt_rs("d_memn", d_kv, comm.w("w_mem_kv"), F32)
    _, g["mem_norm_g"] = _rms_bwd("mem_norm_bwd", mem, p["mem_norm_g"], d_memn, None)
    d_proj, g["lb_logits"], g["hgrn_norm_g"] = comm.carry(
        "hgrn_bwd", lambda r: _hgrn_bwd(proj, o_h, states, d_br, p["lb_logits"], p["hgrn_norm_g"],
                                        (d_gm, d_xq, d_gates), B, S, riders=r))
    comm.small_grads([g[n].reshape(_SMALL_SHAPE[n]) for n in _SMALL_EARLY] + [loss])
    comm.grad("w_in", comm.carry("g_w_in", lambda r: _mm_tn_cs("g_w_in", h, d_proj, N_CHIPS, to=512, riders=r)))
    grad_x, g["norm1_g"] = comm.carry("d_h", lambda r: _mm_nt_cs(
        "d_h_norm1_bwd", d_proj, comm.w("w_in"), F32, riders=r, norm_bwd=(x, p["norm1_g"], d_x1)))
    return loss, grad_x, g


HBM_SPEC = pl.BlockSpec(memory_space=pltpu.HBM)


def _place():
    x, y, c = lax.axis_index("x"), lax.axis_index("y"), lax.axis_index("c")
    other_chips = [(1 - x, y), (x, 1 - y), (1 - x, 1 - y)]
    return x, y, c, other_chips


def _remote(src, dst, send_sem, recv_sem, dev):
    return pltpu.make_async_remote_copy(src_ref=src, dst_ref=dst, send_sem=send_sem, recv_sem=recv_sem,
                                        device_id=dev, device_id_type=MESH_ID)


class _Exchange:
    def __init__(self, operands, out_shape, aliases, scratch, start, finish, mid=None, mid_at=0.5):
        self.operands, self.out_shape, self.aliases, self.scratch = operands, out_shape, aliases, scratch
        self.start, self.finish, self.mid, self.mid_at = start, finish, mid, mid_at


def _run_exchanges(name, exs):
    n_in = [len(ex.operands) for ex in exs]
    n_out = [len(ex.out_shape) for ex in exs]
    n_scr = [len(ex.scratch) for ex in exs]

    def body(*refs):
        ins, outs, scr = refs[:sum(n_in)], refs[sum(n_in):sum(n_in) + sum(n_out)], refs[sum(n_in) + sum(n_out):]
        parts, oi, oo, os_ = [], 0, 0, 0
        for k in range(len(exs)):
            parts.append((ins[oi:oi + n_in[k]], outs[oo:oo + n_out[k]], scr[os_:os_ + n_scr[k]]))
            oi, oo, os_ = oi + n_in[k], oo + n_out[k], os_ + n_scr[k]
        for ex, part in zip(exs, parts):
            ex.start(*part)
        for ex, part in zip(exs, parts):
            if ex.mid is not None:
                ex.mid(*part)
        for ex, part in zip(exs, parts):
            ex.finish(*part)

    aliases, ops, shapes, scratch, oi, oo = {}, [], [], [], 0, 0
    for k, ex in enumerate(exs):
        aliases.update({oi + a: oo + b for a, b in ex.aliases.items()})
        oi, oo = oi + n_in[k], oo + n_out[k]
        ops += list(ex.operands)
        shapes += [pltpu.HBM(s.shape, s.dtype) for s in ex.out_shape]
        scratch += list(ex.scratch)
    res = _pallas(
        body, name=name, in_specs=[HBM_SPEC] * len(ops), out_specs=(HBM_SPEC,) * len(shapes), out_shape=tuple(shapes),
        input_output_aliases=aliases, scratch_shapes=scratch,
    )(*ops)
    out, oo = [], 0
    for k in range(len(exs)):
        out.append(list(res[oo:oo + n_out[k]]))
        oo += n_out[k]
    return out


def _ex_all_gather(slabs, halved, part=(0, 1)):
    n = len(slabs)

    def rows(a, cc):
        if not halved[a]:
            return slice(None)
        pr = slabs[a].shape[1] // part[1]
        return pl.ds(part[0] * pr + cc * (pr // 2), pr // 2)

    def ici(bufs, scr, a, j, chip, c, mine):
        px, py = chip
        x, y, _, _ = _place()
        qs = 2 * x + y if mine else 2 * px + py
        piece = bufs[a].at[qs, rows(a, c)]
        return _remote(piece, piece, scr[0].at[3 * a + j], scr[1].at[3 * a + j], (px, py, c))

    def d2d(bufs, scr, a, j, chip, cc):
        px, py = chip
        x, y, c, _ = _place()
        piece = bufs[a].at[2 * px + py, rows(a, cc)]
        return _remote(piece, piece, scr[2].at[3 * a + j], scr[3].at[3 * a + j], (x, y, 1 - c))

    def start(ins, outs, scr):
        _, _, c, chips = _place()
        for j, chip in enumerate(chips):
            for a in range(n):
                ici(outs, scr, a, j, chip, c, True).start()

    def finish(ins, outs, scr):
        _, _, c, chips = _place()
        for j, chip in enumerate(chips):
            for a in range(n):
                ici(outs, scr, a, j, chip, c, False).wait_recv()
                if halved[a]:
                    d2d(outs, scr, a, j, chip, c).start()
        for j, chip in enumerate(chips):
            for a in range(n):
                if halved[a]:
                    d2d(outs, scr, a, j, chip, 1 - c).wait_recv()
        for j, chip in enumerate(chips):
            for a in range(n):
                ici(outs, scr, a, j, chip, c, True).wait_send()
                if halved[a]:
                    d2d(outs, scr, a, j, chip, c).wait_send()

    return _Exchange(list(slabs), [jax.ShapeDtypeStruct(s.shape, s.dtype) for s in slabs],
                     {a: a for a in range(n)}, [pltpu.SemaphoreType.DMA((3 * n,))] * 4, start, finish)


def _ex_gather_relay(slabs, mid_at=0.5):
    n = len(slabs)

    def rows(a, cc):
        hr = slabs[a].shape[1] // 2
        return pl.ds(cc * hr, hr)

    def peers():
        x, y, c, _ = _place()
        nbr0 = ((x + c) % 2, (y + 1 - c) % 2)
        nbr1 = ((x + 1 - c) % 2, (y + c) % 2)
        return x, y, c, nbr0, nbr1, (1 - x, 1 - y)

    def ici(bufs, scr, a, k, chip, dev, cc):
        _, _, c, _, _, _ = peers()
        piece = bufs[a].at[2 * chip[0] + chip[1], rows(a, cc)]
        return _remote(piece, piece, scr[0].at[3 * a + k], scr[1].at[3 * a + k], (dev[0], dev[1], c))

    def d2d(bufs, scr, a, k, chip, cc):
        x, y, c, _, _, _ = peers()
        piece = bufs[a].at[2 * chip[0] + chip[1], rows(a, cc)]
        return _remote(piece, piece, scr[2].at[3 * a + k], scr[3].at[3 * a + k], (x, y, 1 - c))

    def start(ins, outs, scr):
        x, y, c, nbr0, nbr1, _ = peers()
        for a in range(n):
            ici(outs, scr, a, 0, (x, y), nbr0, c).start()
            ici(outs, scr, a, 1, (x, y), nbr1, c).start()

    def mid(ins, outs, scr):
        x, y, c, nbr0, nbr1, diag = peers()
        for a in range(n):
            ici(outs, scr, a, 0, nbr0, nbr0, c).wait_recv()
            ici(outs, scr, a, 2, nbr0, nbr1, c).start()
            d2d(outs, scr, a, 0, nbr0, c).start()
        for a in range(n):
            ici(outs, scr, a, 1, nbr1, nbr1, c).wait_recv()
            d2d(outs, scr, a, 1, nbr1, c).start()

    def finish(ins, outs, scr):
        x, y, c, nbr0, nbr1, diag = peers()
        for a in range(n):
            ici(outs, scr, a, 2, diag, nbr1, c).wait_recv()
            d2d(outs, scr, a, 2, diag, c).start()
        for a in range(n):
            d2d(outs, scr, a, 0, nbr1, 1 - c).wait_recv()
            d2d(outs, scr, a, 1, nbr0, 1 - c).wait_recv()
            d2d(outs, scr, a, 2, diag, 1 - c).wait_recv()
        for a in range(n):
            ici(outs, scr, a, 0, (x, y), nbr0, c).wait_send()
            ici(outs, scr, a, 1, (x, y), nbr1, c).wait_send()
            ici(outs, scr, a, 2, nbr0, nbr1, c).wait_send()
            d2d(outs, scr, a, 0, nbr0, c).wait_send()
            d2d(outs, scr, a, 1, nbr1, c).wait_send()
            d2d(outs, scr, a, 2, diag, c).wait_send()

    return _Exchange(list(slabs), [jax.ShapeDtypeStruct(s.shape, s.dtype) for s in slabs],
                     {a: a for a in range(n)}, [pltpu.SemaphoreType.DMA((3 * n,))] * 4, start, finish, mid, mid_at)


def _ex_to_sibling(grads):
    n = len(grads)

    def copy(ins, outs, scr, a):
        x, y, c, _ = _place()
        hr = grads[a].shape[1] // 2
        return _remote(ins[a].at[:, pl.ds((1 - c) * hr, hr), :], outs[a], scr[0].at[a], scr[1].at[a], (x, y, 1 - c))

    def start(ins, outs, scr):
        for a in range(n):
            copy(ins, outs, scr, a).start()

    def finish(ins, outs, scr):
        for a in range(n):
            copy(ins, outs, scr, a).wait()

    out_shape = [jax.ShapeDtypeStruct((g.shape[0], g.shape[1] // 2, g.shape[2]), g.dtype) for g in grads]
    return _Exchange(list(grads), out_shape, {}, [pltpu.SemaphoreType.DMA((n,))] * 2, start, finish)


def _ex_to_owner(parts, part=(0, 1), landing=None):
    n = len(parts)

    def copy(ins, outs, scr, a, j, chip):
        _, _, c, _ = _place()
        px, py = chip
        pr = parts[a].shape[1] // part[1]
        rows = pl.ds(part[0] * pr, pr)
        return _remote(ins[a].at[2 * px + py, rows], outs[a].at[j, rows], scr[0].at[3 * a + j],
                       scr[1].at[3 * a + j], (px, py, c))

    def start(ins, outs, scr):
        for j, chip in enumerate(_place()[3]):
            for a in range(n):
                copy(ins, outs, scr, a, j, chip).start()

    def finish(ins, outs, scr):
        for j, chip in enumerate(_place()[3]):
            for a in range(n):
                copy(ins, outs, scr, a, j, chip).wait()

    out_shape = [jax.ShapeDtypeStruct((3,) + p.shape[1:], p.dtype) for p in parts]
    operands, aliases = list(parts), {}
    if landing is not None:
        operands, aliases = operands + list(landing), {n + a: a for a in range(n)}
    return _Exchange(operands, out_shape, aliases, [pltpu.SemaphoreType.DMA((3 * n,))] * 2, start, finish)


def _ex_share_halves(bufs):
    n = len(bufs)

    def copy(outs, scr, a, cc):
        x, y, c, _ = _place()
        hr = bufs[a].shape[0] // 2
        piece = outs[a].at[pl.ds(cc * hr, hr), :]
        return _remote(piece, piece, scr[0].at[a], scr[1].at[a], (x, y, 1 - c))

    def start(ins, outs, scr):
        c = _place()[2]
        for a in range(n):
            copy(outs, scr, a, c).start()

    def finish(ins, outs, scr):
        c = _place()[2]
        for a in range(n):
            copy(outs, scr, a, c).wait_send()
            copy(outs, scr, a, 1 - c).wait_recv()

    return _Exchange(list(bufs), [jax.ShapeDtypeStruct(b.shape, b.dtype) for b in bufs], {a: a for a in range(n)},
                     [pltpu.SemaphoreType.DMA((n,))] * 2, start, finish)


def _ex_gather_small(arrs):
    n = len(arrs)

    def peer_of(m):
        x, y, c, _ = _place()
        return (1 - x if m & 4 else x, 1 - y if m & 2 else y, 1 - c if m & 1 else c)

    def start(ins, outs, scr):
        x, y, c, _ = _place()
        for m in range(1, N_DEV):
            for a in range(n):
                k = (N_DEV - 1) * a + m - 1
                _remote(ins[a], outs[a].at[4 * x + 2 * y + c], scr[0].at[k], scr[1].at[k], peer_of(m)).start()

    def finish(ins, outs, scr):
        for m in range(1, N_DEV):
            px, py, pc = peer_of(m)
            for a in range(n):
                k = (N_DEV - 1) * a + m - 1
                slot = outs[a].at[4 * px + 2 * py + pc]
                cp = _remote(ins[a], slot, scr[0].at[k], scr[1].at[k], (px, py, pc))
                cp.wait_send()
                cp.wait_recv()

    slots = [jnp.zeros((N_DEV,) + a.shape, a.dtype) for a in arrs]
    out_shape = [jax.ShapeDtypeStruct(s.shape, s.dtype) for s in slots]
    return _Exchange(list(arrs) + slots, out_shape, {n + a: a for a in range(n)},
                     [pltpu.SemaphoreType.DMA(((N_DEV - 1) * n,))] * 2, start, finish)


def _div_tile(n, want):
    best = None
    for t in range(8, min(n, want) + 1, 8):
        if n % t == 0:
            best = t
    assert best is not None, n
    return best


def _cast_into_slab(name, w, place, dtype):
    r, cc = w.shape
    tr = r if r * cc <= 128 * 1024 else _div_tile(r, 256)

    def body(s_ref, w_ref, o_ref):
        o_ref[...] = w_ref[...].astype(o_ref.dtype)

    return _pallas(
        body, name=name,
        grid_spec=pltpu.PrefetchScalarGridSpec(
            num_scalar_prefetch=1, grid=(r // tr,),
            in_specs=[pl.BlockSpec((tr, cc), lambda i, s: (i, 0))],
            out_specs=pl.BlockSpec((None, tr, cc), lambda i, s: (s[0], i, 0))),
        out_shape=jax.ShapeDtypeStruct((N_CHIPS, r, cc), dtype), compiler_params=_cp("parallel"),
    )(place, w)


def _add_half(name, g, rcv, place):
    nq, r, cc = g.shape
    hr = r // 2

    def body(s_ref, g_ref, r_ref, o_ref):
        o_ref[...] = (g_ref[...] + r_ref[...]).astype(o_ref.dtype)

    spec = pl.BlockSpec((None, hr, cc), lambda i, s: (i, 0, 0))
    return _pallas(
        body, name=name,
        grid_spec=pltpu.PrefetchScalarGridSpec(
            num_scalar_prefetch=1, grid=(nq,),
            in_specs=[pl.BlockSpec((None, hr, cc), lambda i, s: (i, s[1], 0)), spec], out_specs=spec),
        out_shape=jax.ShapeDtypeStruct((nq, hr, cc), BF16), compiler_params=_cp("parallel"),
    )(place, g, rcv)


def _sum_owner(name, part, rcv, place):
    _, hr, cc = part.shape
    tr = _div_tile(hr, 128)
    nb = hr // tr

    def body(s_ref, p_ref, r_ref, o_ref):
        o_ref[...] = ((p_ref[...].astype(F32) + r_ref[0].astype(F32)) + r_ref[1].astype(F32)) + r_ref[2].astype(F32)

    return _pallas(
        body, name=name,
        grid_spec=pltpu.PrefetchScalarGridSpec(
            num_scalar_prefetch=1, grid=(nb,),
            in_specs=[pl.BlockSpec((None, tr, cc), lambda i, s: (s[0], i, 0)),
                      pl.BlockSpec((3, tr, cc), lambda i, s: (0, i, 0))],
            out_specs=pl.BlockSpec((tr, cc), lambda i, s: (s[1] * nb + i, 0))),
        out_shape=jax.ShapeDtypeStruct((2 * hr, cc), F32), compiler_params=_cp("parallel"),
    )(place, part, rcv)


def _sum_small(gathered, local, place):
    n = len(gathered)

    def body(s_ref, *refs):
        g_refs, l_refs, o_refs = refs[:n], refs[n:2 * n], refs[2 * n:]
        me = s_ref[2]
        for g_ref, l_ref, o_ref in zip(g_refs, l_refs, o_refs):
            acc = None
            for d in range(N_DEV):
                term = jnp.where(me == d, l_ref[...], g_ref[d])
                acc = term if acc is None else acc + term
            o_ref[...] = acc

    def whole(shape):
        return pl.BlockSpec(shape, lambda i, s, nd=len(shape): (0,) * nd)

    return _pallas(
        body, name="sum_small",
        grid_spec=pltpu.PrefetchScalarGridSpec(
            num_scalar_prefetch=1, grid=(1,),
            in_specs=[whole(g.shape) for g in gathered] + [whole(a.shape) for a in local],
            out_specs=tuple(whole(a.shape) for a in local)),
        out_shape=tuple(jax.ShapeDtypeStruct(a.shape, a.dtype) for a in local), compiler_params=_cp("arbitrary"),
    )(place, *gathered, *local)


def _adamw(name, w, g, m, v):
    r, cc = w.shape
    tr = r if r * cc <= 128 * 1024 else _div_tile(r, 256)

    def body(w_ref, g_ref, m_ref, v_ref, d_ref, mo_ref, vo_ref):
        gv = g_ref[...]
        mn = ADAM_B1 * m_ref[...] + (1.0 - ADAM_B1) * gv
        vn = ADAM_B2 * v_ref[...] + (1.0 - ADAM_B2) * (gv * gv)
        m_hat = mn / (1.0 - ADAM_B1 ** ADAM_STEP)
        v_hat = vn / (1.0 - ADAM_B2 ** ADAM_STEP)
        d_ref[...] = -ADAM_LR * (m_hat / (jnp.sqrt(v_hat) + ADAM_EPS) + ADAM_WD * w_ref[...])
        mo_ref[...] = mn
        vo_ref[...] = vn

    spec = pl.BlockSpec((tr, cc), lambda i: (i, 0))
    sd = jax.ShapeDtypeStruct((r, cc), F32)
    return _pallas(
        body, name=name, grid=(r // tr,), in_specs=[spec] * 4, out_specs=(spec,) * 3, out_shape=(sd,) * 3,
        compiler_params=_cp("parallel"),
    )(w, g, m, v)


_BIG = ("w_in", "w_up", "w_branch", "w_mem_kv", "w_out", "w_down")
_BIG_SHARD_SHAPE = {"w_in": (1024, 1664), "w_up": (1024, 1408), "w_branch": (1536, 256),
                    "w_mem_kv": (256, 1024), "w_out": (256, 1024), "w_down": (704, 1024)}
_SMALL_SHAPE = {"norm1_g": (1, D_MODEL), "ln_v_g": (1, GM_WIDTH), "ln_v_b": (1, GM_WIDTH),
                "w_spatial": (GM_GROUPS * GM_CHUNK, GM_CHUNK), "b_spatial": (GM_GROUPS, GM_CHUNK),
                "lb_logits": (2, HG_HEADS * HG_DIM), "hgrn_norm_g": (1, HG_DIM), "mem_norm_g": (1, D_MODEL),
                "norm2_g": (1, D_MODEL), "conv_w": (3, D_FF), "conv_b": (1, D_FF), "final_g": (1, D_MODEL)}
_SMALL_EARLY = tuple(n for n in _SMALL_SHAPE if n != "norm1_g")
_PARAM_ORDER = ("norm1_g", "w_in", "ln_v_g", "ln_v_b", "w_spatial", "b_spatial", "lb_logits", "hgrn_norm_g",
                "mem_norm_g", "w_mem_kv", "w_branch", "w_out", "norm2_g", "w_up", "conv_w", "conv_b", "w_down",
                "final_g")


def _adamw_small(ws, gs, ms, vs):
    n = len(ws)

    def body(*refs):
        w_refs, g_refs, m_refs, v_refs = refs[:n], refs[n:2 * n], refs[2 * n:3 * n], refs[3 * n:4 * n]
        d_refs, mo_refs, vo_refs = refs[4 * n:5 * n], refs[5 * n:6 * n], refs[6 * n:]
        for k in range(n):
            gv = g_refs[k][...]
            mn = ADAM_B1 * m_refs[k][...] + (1.0 - ADAM_B1) * gv
            vn = ADAM_B2 * v_refs[k][...] + (1.0 - ADAM_B2) * (gv * gv)
            m_hat = mn / (1.0 - ADAM_B1 ** ADAM_STEP)
            v_hat = vn / (1.0 - ADAM_B2 ** ADAM_STEP)
            d_refs[k][...] = -ADAM_LR * (m_hat / (jnp.sqrt(v_hat) + ADAM_EPS) + ADAM_WD * w_refs[k][...])
            mo_refs[k][...] = mn
            vo_refs[k][...] = vn

    specs = [pl.BlockSpec(a.shape, lambda i: (0, 0)) for a in ws]
    shapes = tuple(jax.ShapeDtypeStruct(a.shape, F32) for a in ws)
    res = _pallas(
        body, name="adamw_small", grid=(1,), in_specs=specs * 4, out_specs=tuple(specs * 3), out_shape=shapes * 3,
        compiler_params=_cp("arbitrary"),
    )(*ws, *gs, *ms, *vs)
    return res[:n], res[n:2 * n], res[2 * n:]


class _Comm:
    _ROW_SHARDED = ("w_mem_kv", "w_out", "w_down")

    def __init__(self, slabs, place):
        self.slabs, self.place = slabs, place
        self.full, self.raw, self.parts, self.landing, self.bufs, self.done = {}, {}, {}, {}, {}, {}

    def w(self, name):
        a = self.full[name]
        if name in self._ROW_SHARDED:
            return a.reshape(-1, a.shape[-1])
        if name == "conv_w":
            return jnp.transpose(a, (1, 0, 2)).reshape(3, 1, D_FF)
        return a

    def grad(self, name, arr):
        self.raw[name] = arr.reshape((N_CHIPS, -1, arr.shape[-1]))
        if name == "w_in":
            ex, deliver = self._to_sibling(["w_in"])
            deliver(_run_exchanges("rs_sibling_w_in", [ex])[0])

    def small_grads(self, arrays):
        self.small_local = list(arrays)

    def carry(self, tag, call):
        plan = self._plan(tag)
        if not plan:
            return call(())
        out, carried = call([ex for ex, _ in plan])
        for (_, deliver), res in zip(plan, carried):
            deliver(res)
        return out

    def finish(self, last_small):
        ex, deliver = self._share(["w_out", "w_branch", "w_mem_kv", "w_in"])
        shared, small = _run_exchanges("share_and_gather_last", [ex, _ex_gather_small(last_small)])
        deliver(shared)
        return self.done, self.small_local + list(last_small), self.small_everyone + small

    def _plan(self, tag):
        if tag == "norm1":
            def deliver(res):
                self.full["w_in"] = res[0]

            return [(_ex_gather_relay([self.slabs["w_in"]]), deliver)]
        if tag == "in_proj":
            return [self._gather_relay(["w_branch", "w_out", "w_mem_kv", "w_down"], 0.6), self._gather(["conv_w"])]
        if tag == "hgrn_fwd":
            return [self._gather_relay(["w_up"], 0.8)]
        if tag == "d_h2":
            return [self._to_sibling(["w_down", "w_up"])]
        if tag == "merge_bwd":
            return [self._to_owner(["w_up"], (0, 2))]
        if tag == "hgrn_bwd":
            return [self._to_owner(["w_down"]), self._to_owner(["w_up"], (1, 2)),
                    self._to_sibling(["w_out", "w_branch", "w_mem_kv"])]
        if tag == "g_w_in":
            def keep(res):
                self.small_everyone = res

            return [self._to_owner(["w_out", "w_branch", "w_mem_kv"]), (_ex_gather_small(self.small_local), keep)]
        if tag == "d_h":
            return [self._to_owner(["w_in"]), self._share(["w_down", "w_up"])]
        return []

    def _gather(self, names, part=(0, 1)):
        def deliver(res):
            self.slabs.update(zip(names, res))
            self.full.update(zip(names, res))

        return _ex_all_gather([self.slabs[n] for n in names], [n != "conv_w" for n in names], part), deliver

    def _gather_relay(self, names, mid_at):
        return _ex_gather_relay([self.slabs[n] for n in names], mid_at), lambda res: self.full.update(zip(names, res))

    def _to_sibling(self, names):
        def deliver(res):
            for n, r in zip(names, res):
                self.parts[n] = _add_half("rs_add_" + n, self.raw[n], r, self.place)

        return _ex_to_sibling([self.raw[n] for n in names]), deliver

    def _to_owner(self, names, part=(0, 1)):
        def deliver(res):
            for n, r in zip(names, res):
                if part[0] + 1 < part[1]:
                    self.landing[n] = r
                else:
                    self.bufs[n] = _sum_owner("rs_sum_" + n, self.parts[n], r, self.place)

        landing = [self.landing[n] for n in names] if part[0] else None
        return _ex_to_owner([self.parts[n] for n in names], part, landing), deliver

    def _share(self, names):
        return _ex_share_halves([self.bufs[n] for n in names]), lambda res: self.done.update(zip(names, res))


def kernel(x, mem, norm1_g, w_in, ln_v_g, ln_v_b, w_spatial, b_spatial, lb_logits, hgrn_norm_g, mem_norm_g, w_mem_kv, w_branch, w_out, norm2_g, w_up, conv_w, conv_b, w_down, final_g, loss_target, m_norm1_g, m_w_in, m_ln_v_g, m_ln_v_b, m_w_spatial, m_b_spatial, m_lb_logits, m_hgrn_norm_g, m_mem_norm_g, m_w_mem_kv, m_w_branch, m_w_out, m_norm2_g, m_w_up, m_conv_w, m_conv_b, m_w_down, m_final_g, v_norm1_g, v_w_in, v_ln_v_g, v_ln_v_b, v_w_spatial, v_b_spatial, v_lb_logits, v_hgrn_norm_g, v_mem_norm_g, v_w_mem_kv, v_w_branch, v_w_out, v_norm2_g, v_w_up, v_conv_w, v_conv_b, v_w_down, v_final_g):
    w = dict(norm1_g=norm1_g, w_in=w_in, ln_v_g=ln_v_g, ln_v_b=ln_v_b, w_spatial=w_spatial, b_spatial=b_spatial,
             lb_logits=lb_logits, hgrn_norm_g=hgrn_norm_g, mem_norm_g=mem_norm_g, w_mem_kv=w_mem_kv,
             w_branch=w_branch, w_out=w_out, norm2_g=norm2_g, w_up=w_up, conv_w=conv_w, conv_b=conv_b,
             w_down=w_down, final_g=final_g)
    mom = dict(norm1_g=m_norm1_g, w_in=m_w_in, ln_v_g=m_ln_v_g, ln_v_b=m_ln_v_b, w_spatial=m_w_spatial,
               b_spatial=m_b_spatial, lb_logits=m_lb_logits, hgrn_norm_g=m_hgrn_norm_g, mem_norm_g=m_mem_norm_g,
               w_mem_kv=m_w_mem_kv, w_branch=m_w_branch, w_out=m_w_out, norm2_g=m_norm2_g, w_up=m_w_up,
               conv_w=m_conv_w, conv_b=m_conv_b, w_down=m_w_down, final_g=m_final_g)
    var = dict(norm1_g=v_norm1_g, w_in=v_w_in, ln_v_g=v_ln_v_g, ln_v_b=v_ln_v_b, w_spatial=v_w_spatial,
               b_spatial=v_b_spatial, lb_logits=v_lb_logits, hgrn_norm_g=v_hgrn_norm_g, mem_norm_g=v_mem_norm_g,
               w_mem_kv=v_w_mem_kv, w_branch=v_w_branch, w_out=v_w_out, norm2_g=v_norm2_g, w_up=v_w_up,
               conv_w=v_conv_w, conv_b=v_conv_b, w_down=v_w_down, final_g=v_final_g)
    B, S, D = x.shape
    T = B * S
    ci = lax.axis_index("c")
    q = 2 * lax.axis_index("x") + lax.axis_index("y")
    place = jnp.stack([q, ci, 2 * q + ci]).astype(jnp.int32)

    slabs = {n: _cast_into_slab("slab_" + n, w[n].reshape(_BIG_SHARD_SHAPE[n]), place, BF16) for n in _BIG}
    slabs["conv_w"] = _cast_into_slab("slab_conv_w", conv_w[0], place, F32)
    comm = _Comm(slabs, place)
    p = dict(
        norm1_g=norm1_g, ln_v_g=ln_v_g, ln_v_b=ln_v_b, w_spatial=w_spatial[0],
        b_spatial=b_spatial.reshape(GM_GROUPS, GM_CHUNK, 1), lb_logits=lb_logits, hgrn_norm_g=hgrn_norm_g,
        mem_norm_g=mem_norm_g, norm2_g=norm2_g, conv_b=conv_b, final_g=final_g.reshape(1, D))

    loss, grad_x, g = _local_step(x.reshape(T, D), mem.reshape(B * MEM_LEN, D), loss_target.reshape(T, D), p, comm,
                                  B, S)

    shard_grads, local_small, everyone = comm.finish([g["norm1_g"]])
    summed = _sum_small(everyone, local_small, place)
    small_names = list(_SMALL_EARLY) + ["norm1_g"]
    total = dict(zip(_SMALL_EARLY, summed))
    loss_total, total["norm1_g"] = summed[len(_SMALL_EARLY)][0, 0], summed[-1]

    grads, delta, new_m, new_v = {}, {}, {}, {}
    for n in _BIG:
        shp = _BIG_SHARD_SHAPE[n]
        grads[n] = shard_grads[n]
        delta[n], new_m[n], new_v[n] = _adamw("adamw_" + n, w[n].reshape(shp), shard_grads[n],
                                              mom[n].reshape(shp), var[n].reshape(shp))
    cw_shard = D_FF // N_CHIPS
    total["conv_w"] = lax.dynamic_slice(total["conv_w"], (0, q * cw_shard), (3, cw_shard))

    def flat2d(d, n):
        return d[n].reshape(total[n].shape)

    upd = _adamw_small([flat2d(w, n) for n in small_names], [total[n] for n in small_names],
                       [flat2d(mom, n) for n in small_names], [flat2d(var, n) for n in small_names])
    for k, n in enumerate(small_names):
        grads[n], delta[n], new_m[n], new_v[n] = total[n], upd[0][k], upd[1][k], upd[2][k]

    def shaped(d):
        return [d[n].reshape(w[n].shape) for n in _PARAM_ORDER]

    return (loss_total, grad_x.reshape(B, S, D), *shaped(grads), *shaped(delta), *shaped(new_m), *shaped(new_v))
```

```python
import functools
import math

import jax
import jax.numpy as jnp
from jax import lax
from jax.experimental import pallas as pl
from jax.experimental.pallas import tpu as pltpu

F32 = jnp.float32
BF16 = jnp.bfloat16
EPS = 1e-6

D_MODEL = 1024
MEM_LEN = 256
GM_WIDTH = 512
GM_CHUNK = 128
GM_GROUPS = 4
HG_HEADS = 4
HG_DIM = 128
HG_CHUNK = 64
XA_HEADS = 4
XA_DIM = 128
BR_WIDTH = 512
D_FF = 2816
IN_WIDTH = 6656
N_CHIPS = 4
N_DEV = 8

ADAM_LR = 0.001
ADAM_B1 = 0.9
ADAM_B2 = 0.999
ADAM_EPS = 1e-08
ADAM_WD = 0.01
ADAM_STEP = 10

COL_ZU, COL_ZV, COL_HQ, COL_HF, COL_HI, COL_HG, COL_XQ = 0, 1, 2, 3, 4, 5, 6
COL_GATE0 = 3584

VMEM_LIMIT_BYTES = 48 * 1024 * 1024
MESH_ID = pl.DeviceIdType.MESH


def _cp(*sem):
    return pltpu.CompilerParams(dimension_semantics=sem, vmem_limit_bytes=VMEM_LIMIT_BYTES)


def _pallas(body, *, out_shape, **kw):
    def pin(s):
        return pltpu.HBM(s.shape, s.dtype) if isinstance(s, jax.ShapeDtypeStruct) else s

    out_shape = tuple(pin(s) for s in out_shape) if isinstance(out_shape, (tuple, list)) else pin(out_shape)
    call = pl.pallas_call(body, out_shape=out_shape, **kw)

    def run(*operands):
        return call(*[pltpu.with_memory_space_constraint(o, pltpu.HBM) if jnp.issubdtype(o.dtype, jnp.floating)
                      else o for o in operands])

    return run


def _dot(a, b):
    return lax.dot_general(a.astype(BF16), b.astype(BF16), (((1,), (0,)), ((), ())), preferred_element_type=F32)


def _dot_nt(a, b):
    return lax.dot_general(a.astype(BF16), b.astype(BF16), (((1,), (1,)), ((), ())), preferred_element_type=F32)


def _dot_tn(a, b):
    return lax.dot_general(a.astype(BF16), b.astype(BF16), (((0,), (0,)), ((), ())), preferred_element_type=F32)


def _dot_01(mask01, x):
    hi = x.astype(BF16)
    r1 = x - hi.astype(F32)
    mid = r1.astype(BF16)
    lo = (r1 - mid.astype(F32)).astype(BF16)
    m = mask01.astype(BF16)
    dn = (((1,), (0,)), ((), ()))
    return (lax.dot_general(m, hi, dn, preferred_element_type=F32)
            + lax.dot_general(m, mid, dn, preferred_element_type=F32)
            + lax.dot_general(m, lo, dn, preferred_element_type=F32))


def _sigmoid(z):
    return 1.0 / (1.0 + jnp.exp(-z))


_GELU_C = math.sqrt(2.0 / math.pi)


def _gelu_and_grad(z):
    inner = _GELU_C * (z + 0.044715 * z * z * z)
    t = jnp.tanh(inner)
    val = 0.5 * z * (1.0 + t)
    grad = 0.5 * (1.0 + t) + 0.5 * z * (1.0 - t * t) * _GELU_C * (1.0 + 3.0 * 0.044715 * z * z)
    return val, grad


def _row_tile(n, want=512):
    t = min(want, n)
    assert n % t == 0
    return t


def _pcall(body, operands, *, name, grid, in_specs, out_specs, out_shape, scratch_shapes=(), semantics, riders=()):
    single = not isinstance(out_shape, (tuple, list))
    out_specs = (out_specs,) if single else tuple(out_specs)
    out_shape = (out_shape,) if single else tuple(out_shape)
    if not riders:
        res = _pallas(body, name=name, grid=grid, in_specs=list(in_specs), out_specs=out_specs,
                      out_shape=out_shape, scratch_shapes=list(scratch_shapes),
                      compiler_params=_cp(*semantics))(*operands)
        return (res[0] if single else res), []
    n_in, n_out, n_scr = len(in_specs), len(out_shape), len(scratch_shapes)
    ex_in = [len(ex.operands) for ex in riders]
    ex_out = [len(ex.out_shape) for ex in riders]
    ex_scr = [len(ex.scratch) for ex in riders]
    tot_in, tot_out = n_in + sum(ex_in), n_out + sum(ex_out)

    def wrapped(*refs):
        ins, outs, scr = refs[:tot_in], refs[tot_in:tot_in + tot_out], refs[tot_in + tot_out:]
        ids = [pl.program_id(d) for d in range(len(grid))]
        first = functools.reduce(lambda p, t: p & t, [i == 0 for i in ids])
        last = functools.reduce(lambda p, t: p & t, [i == n - 1 for i, n in zip(ids, grid)])
        parts, oi, oo, os_ = [], n_in, n_out, n_scr
        for k in range(len(riders)):
            parts.append((ins[oi:oi + ex_in[k]], outs[oo:oo + ex_out[k]], scr[os_:os_ + ex_scr[k]]))
            oi, oo, os_ = oi + ex_in[k], oo + ex_out[k], os_ + ex_scr[k]

        @pl.when(first)
        def _():
            for ex, part in zip(riders, parts):
                ex.start(*part)

        step, total = 0, 1
        for i, n in zip(ids, grid):
            step, total = step * n + i, total * n
        for ex, part in zip(riders, parts):
            if ex.mid is not None:
                @pl.when(step == min(total - 1, int(total * ex.mid_at)))
                def _(ex=ex, part=part):
                    ex.mid(*part)

        body(*ins[:n_in], *outs[:n_out], *scr[:n_scr])

        @pl.when(last)
        def _():
            for ex, part in zip(riders, parts):
                ex.finish(*part)

    aliases, oi, oo = {}, n_in, n_out
    all_ops, all_shapes, all_scr = list(operands), list(out_shape), list(scratch_shapes)
    for k, ex in enumerate(riders):
        aliases.update({oi + a: oo + b for a, b in ex.aliases.items()})
        oi, oo = oi + ex_in[k], oo + ex_out[k]
        all_ops += list(ex.operands)
        all_shapes += [pltpu.HBM(s.shape, s.dtype) for s in ex.out_shape]
        all_scr += list(ex.scratch)
    res = _pallas(
        wrapped, name=name, grid=grid, in_specs=list(in_specs) + [HBM_SPEC] * sum(ex_in),
        out_specs=out_specs + (HBM_SPEC,) * sum(ex_out), out_shape=tuple(all_shapes), scratch_shapes=all_scr,
        input_output_aliases=aliases, compiler_params=_cp(*(["arbitrary"] * len(grid))))(*all_ops)
    own = res[0] if single else tuple(res[:n_out])
    carried, oo = [], n_out
    for k in range(len(riders)):
        carried.append(list(res[oo:oo + ex_out[k]]))
        oo += ex_out[k]
    return own, carried


def _carried(out, carried, riders):
    return (out, carried) if riders else out


def _matmul(name, operands, *, grid, in_specs, o_spec, out_shape, out_dtype, dims, riders=()):
    nk = grid[2]
    assert nk == 1 or out_dtype == F32

    def body(a_ref, b_ref, o_ref):
        part = lax.dot_general(a_ref[...].astype(BF16), b_ref[...].astype(BF16), (dims, ((), ())),
                               preferred_element_type=F32)
        if nk == 1:
            o_ref[...] = part.astype(o_ref.dtype)
        else:
            k = pl.program_id(2)

            @pl.when(k == 0)
            def _():
                o_ref[...] = part

            @pl.when(k > 0)
            def _():
                o_ref[...] += part

    out, carried = _pcall(body, operands, name=name, grid=grid, in_specs=in_specs, out_specs=o_spec,
                          out_shape=jax.ShapeDtypeStruct(out_shape, out_dtype),
                          semantics=("parallel", "parallel", "arbitrary"), riders=riders)
    return (out, carried) if riders else out


NN = ((1,), (0,))
NT = ((1,), (1,))
TN = ((0,), (0,))
_TN_TOKENS = 4096


def _mm_cs(name, a, w, out_dtype, riders=()):
    M, K = a.shape
    nq, _, wd = w.shape
    tm = _row_tile(M)
    return _matmul(name, (a, w), grid=(nq, M // tm, 1),
                   in_specs=[pl.BlockSpec((tm, K), lambda j, i, k: (i, 0)),
                             pl.BlockSpec((None, K, wd), lambda j, i, k: (j, 0, 0))],
                   o_spec=pl.BlockSpec((tm, wd), lambda j, i, k: (i, j)),
                   out_shape=(M, nq * wd), out_dtype=out_dtype, dims=NN, riders=riders)


def _mm_rs(name, a, w, out_dtype):
    M, K = a.shape
    N = w.shape[1]
    tm = _row_tile(M)
    return _matmul(name, (a, w), grid=(M // tm, 1, 1),
                   in_specs=[pl.BlockSpec((tm, K), lambda i, j, k: (i, 0)), pl.BlockSpec((K, N), lambda i, j, k: (0, 0))],
                   o_spec=pl.BlockSpec((tm, N), lambda i, j, k: (i, 0)),
                   out_shape=(M, N), out_dtype=out_dtype, dims=NN)


def _mm_nt_rs(name, g, w, out_dtype, riders=()):
    M, N = g.shape
    K = w.shape[0]
    to = K
    tm = _row_tile(M)
    return _matmul(name, (g, w), grid=(M // tm, K // to, 1),
                   in_specs=[pl.BlockSpec((tm, N), lambda i, j, k: (i, 0)),
                             pl.BlockSpec((to, N), lambda i, j, k: (j, 0))],
                   o_spec=pl.BlockSpec((tm, to), lambda i, j, k: (i, j)),
                   out_shape=(M, K), out_dtype=out_dtype, dims=NT, riders=riders)


def _mm_nt_cs(name, g, w, out_dtype, riders=(), stacked=False, norm_bwd=None):
    M = g.shape[-2]
    nq, K, wd = w.shape
    tm = _row_tile(M, 256)

    def product(g_ref, w_ref):
        acc = None
        for q in range(nq):
            gq = g_ref[q // 2, :, (q % 2) * wd:(q % 2 + 1) * wd] if stacked else g_ref[:, q * wd:(q + 1) * wd]
            part = _dot_nt(gq, w_ref[q])
            acc = part if acc is None else acc + part
        return acc

    def body(g_ref, w_ref, o_ref):
        o_ref[...] = product(g_ref, w_ref).astype(o_ref.dtype)

    def body_norm(g_ref, w_ref, x_ref, gain_ref, dr_ref, dx_ref, dg_ref):
        @pl.when(pl.program_id(0) == 0)
        def _():
            dg_ref[...] = jnp.zeros_like(dg_ref)

        dx, dg = _rms_bwd_rows(x_ref[...], gain_ref[...], product(g_ref, w_ref))
        dg_ref[...] += dg
        dx_ref[...] = dx + dr_ref[...]

    g_spec = (pl.BlockSpec((2, tm, 2 * wd), lambda i: (0, i, 0)) if stacked
              else pl.BlockSpec((tm, nq * wd), lambda i: (i, 0)))
    w_spec = pl.BlockSpec((nq, K, wd), lambda i: (0, 0, 0))
    row = pl.BlockSpec((tm, K), lambda i: (i, 0))
    if norm_bwd is None:
        return _carried(*_pcall(
            body, (g, w), name=name, grid=(M // tm,), in_specs=[g_spec, w_spec], out_specs=row,
            out_shape=jax.ShapeDtypeStruct((M, K), out_dtype), semantics=("parallel",), riders=riders), riders)
    vec = pl.BlockSpec((1, K), lambda i: (0, 0))
    return _carried(*_pcall(
        body_norm, (g, w) + tuple(norm_bwd), name=name, grid=(M // tm,),
        in_specs=[g_spec, w_spec, row, vec, row], out_specs=(row, vec),
        out_shape=(jax.ShapeDtypeStruct((M, K), F32), jax.ShapeDtypeStruct((1, K), F32)),
        semantics=("arbitrary",), riders=riders), riders)


def _mm_tn_rs(name, a, g, to, tn=512):
    T, M = a.shape
    N = g.shape[1]
    tt = _row_tile(T, _TN_TOKENS)
    tn = min(tn, N)
    return _matmul(name, (a, g), grid=(M // to, N // tn, T // tt),
                   in_specs=[pl.BlockSpec((tt, to), lambda i, j, k: (k, i)),
                             pl.BlockSpec((tt, tn), lambda i, j, k: (k, j))],
                   o_spec=pl.BlockSpec((to, tn), lambda i, j, k: (i, j)),
                   out_shape=(M, N), out_dtype=F32, dims=TN)


def _mm_tn_cs(name, a, g, nq, to, riders=(), stacked=False):
    T, M = a.shape
    wd = g.shape[-1] * (2 if stacked else 1) // nq
    tt = _row_tile(T, _TN_TOKENS)
    g_spec = (pl.BlockSpec((None, tt, wd), lambda i, j, k: (j // 2, k, j % 2)) if stacked
              else pl.BlockSpec((tt, wd), lambda i, j, k: (k, j)))
    return _matmul(name, (a, g), grid=(M // to, nq, T // tt),
                   in_specs=[pl.BlockSpec((tt, to), lambda i, j, k: (k, i)), g_spec],
                   o_spec=pl.BlockSpec((None, to, wd), lambda i, j, k: (j, i, 0)),
                   out_shape=(nq, M, wd), out_dtype=F32, dims=TN, riders=riders)


def _rms_fwd(name, x, g, riders=()):
    T, D = x.shape
    tm = _row_tile(T)

    def body(x_ref, g_ref, o_ref):
        o_ref[...] = _rms_rows(x_ref[...], g_ref[...]).astype(o_ref.dtype)

    return _carried(*_pcall(
        body, (x, g), name=name, grid=(T // tm,),
        in_specs=[pl.BlockSpec((tm, D), lambda i: (i, 0)), pl.BlockSpec((1, D), lambda i: (0, 0))],
        out_specs=pl.BlockSpec((tm, D), lambda i: (i, 0)),
        out_shape=jax.ShapeDtypeStruct((T, D), BF16), semantics=("parallel",), riders=riders), riders)


def _rms_rows(xv, gain):
    return xv * lax.rsqrt(jnp.mean(xv * xv, axis=-1, keepdims=True) + EPS) * gain


def _rms_bwd_rows(xv, gain, dh):
    r = lax.rsqrt(jnp.mean(xv * xv, axis=-1, keepdims=True) + EPS)
    n = xv * r
    dn = dh * gain
    return r * (dn - n * jnp.mean(dn * n, axis=-1, keepdims=True)), jnp.sum(dh * n, axis=0, keepdims=True)


def _rms_bwd(name, x, g, dh, dres):
    T, D = x.shape
    tm = _row_tile(T)
    has_res = dres is not None

    def body(*refs):
        if has_res:
            x_ref, g_ref, dh_ref, dr_ref, dx_ref, dg_ref = refs
        else:
            x_ref, g_ref, dh_ref, dx_ref, dg_ref = refs

        @pl.when(pl.program_id(0) == 0)
        def _():
            dg_ref[...] = jnp.zeros_like(dg_ref)

        dx, dg = _rms_bwd_rows(x_ref[...], g_ref[...], dh_ref[...])
        dg_ref[...] += dg
        if has_res:
            dx = dx + dr_ref[...]
        dx_ref[...] = dx

    row = pl.BlockSpec((tm, D), lambda i: (i, 0))
    vec = pl.BlockSpec((1, D), lambda i: (0, 0))
    ops = (x, g, dh, dres) if has_res else (x, g, dh)
    return _pallas(
        body, name=name, grid=(T // tm,), in_specs=[row, vec, row] + ([row] if has_res else []),
        out_specs=(row, vec),
        out_shape=(jax.ShapeDtypeStruct((T, D), F32), jax.ShapeDtypeStruct((1, D), F32)),
        compiler_params=_cp("arbitrary"),
    )(*ops)


def _proj_res_norm(name, a, w, res, gain):
    M, K = a.shape
    N = w.shape[1]
    tm = _row_tile(M)

    def body(a_ref, w_ref, r_ref, g_ref, x_ref, h_ref):
        xv = _dot(a_ref[...], w_ref[...]) + r_ref[...]
        x_ref[...] = xv
        h_ref[...] = _rms_rows(xv, g_ref[...]).astype(h_ref.dtype)

    row = pl.BlockSpec((tm, N), lambda i: (i, 0))
    return _pallas(
        body, name=name, grid=(M // tm,),
        in_specs=[pl.BlockSpec((tm, K), lambda i: (i, 0)), pl.BlockSpec((K, N), lambda i: (0, 0)), row,
                  pl.BlockSpec((1, N), lambda i: (0, 0))],
        out_specs=(row, row), out_shape=(jax.ShapeDtypeStruct((M, N), F32), jax.ShapeDtypeStruct((M, N), BF16)),
        compiler_params=_cp("parallel"),
    )(a, w, res, gain)


def _proj_res_loss(name, a, w, res, tgt, gain):
    M, K = a.shape
    D = w.shape[1]
    tm = _row_tile(M)

    def body(a_ref, w_ref, r_ref, t_ref, g_ref, dx_ref, dg_ref, loss_ref):
        @pl.when(pl.program_id(0) == 0)
        def _():
            dg_ref[...] = jnp.zeros_like(dg_ref)
            loss_ref[...] = jnp.zeros_like(loss_ref)

        xv = _dot(a_ref[...], w_ref[...]) + r_ref[...]
        gv = g_ref[...]
        diff = _rms_rows(xv, gv) - t_ref[...]
        loss_ref[...] += 0.5 * jnp.sum(jnp.mean(diff * diff, axis=-1, keepdims=True))
        dx, dg = _rms_bwd_rows(xv, gv, diff * (1.0 / D))
        dg_ref[...] += dg
        dx_ref[...] = dx

    row = pl.BlockSpec((tm, D), lambda i: (i, 0))
    vec = pl.BlockSpec((1, D), lambda i: (0, 0))
    return _pallas(
        body, name=name, grid=(M // tm,),
        in_specs=[pl.BlockSpec((tm, K), lambda i: (i, 0)), pl.BlockSpec((K, D), lambda i: (0, 0)), row, row, vec],
        out_specs=(row, vec, pl.BlockSpec((8, 128), lambda i: (0, 0))),
        out_shape=(jax.ShapeDtypeStruct((M, D), F32), jax.ShapeDtypeStruct((1, D), F32),
                   jax.ShapeDtypeStruct((8, 128), F32)),
        compiler_params=_cp("arbitrary"),
    )(a, w, res, tgt, gain)


def _gmlp_pieces(zu, zv, lng, lnb, ws_ref, bs_ref):
    u, du = _gelu_and_grad(zu)
    v, dv = _gelu_and_grad(zv)
    mu = jnp.mean(v, axis=-1, keepdims=True)
    vc = v - mu
    rstd = lax.rsqrt(jnp.mean(vc * vc, axis=-1, keepdims=True) + EPS)
    vhat = vc * rstd
    vn = vhat * lng + lnb
    row = lax.broadcasted_iota(jnp.int32, (GM_CHUNK, GM_CHUNK), 0)
    col = lax.broadcasted_iota(jnp.int32, (GM_CHUNK, GM_CHUNK), 1)
    tril = row >= col
    wms, mixed = [], []
    for g in range(GM_GROUPS):
        sl = slice(g * 128, (g + 1) * 128)
        wm = jnp.where(tril, ws_ref[g], 0.0)
        wms.append(wm)
        mixed.append(_dot(wm, vn[:, sl]) + bs_ref[g])
    return u, du, dv, rstd, vhat, vn, wms, mixed, tril


def _gmlp_fwd(proj, lng, lnb, ws, bs_col):
    T = proj.shape[0]
    n = T // GM_CHUNK

    def body(zu_ref, zv_ref, lng_ref, lnb_ref, ws_ref, bs_ref, o_ref):
        u, _, _, _, _, _, _, mixed, _ = _gmlp_pieces(zu_ref[...].astype(F32), zv_ref[...].astype(F32),
                                                     lng_ref[...], lnb_ref[...],
                                                     ws_ref, bs_ref)
        for g in range(GM_GROUPS):
            sl = slice(g * 128, (g + 1) * 128)
            o_ref[:, sl] = (u[:, sl] * mixed[g]).astype(o_ref.dtype)

    vec = pl.BlockSpec((1, GM_WIDTH), lambda i: (0, 0))
    return _pallas(
        body, name="gmlp_fwd", grid=(n,),
        in_specs=[pl.BlockSpec((GM_CHUNK, 512), lambda i: (i, COL_ZU)),
                  pl.BlockSpec((GM_CHUNK, 512), lambda i: (i, COL_ZV)),
                  vec, vec,
                  pl.BlockSpec((GM_GROUPS, 128, 128), lambda i: (0, 0, 0)),
                  pl.BlockSpec((GM_GROUPS, 128, 1), lambda i: (0, 0, 0))],
        out_specs=pl.BlockSpec((GM_CHUNK, 512), lambda i: (i, 0)),
        out_shape=jax.ShapeDtypeStruct((T, GM_WIDTH), BF16), compiler_params=_cp("parallel"),
    )(proj, proj, lng, lnb, ws, bs_col)


def _gmlp_bwd(proj, d_out, lng, lnb, ws, bs_col, riders=()):
    T = proj.shape[0]
    n = T // GM_CHUNK

    def body(zu_ref, zv_ref, do_ref, lng_ref, lnb_ref, ws_ref, bs_ref,
             dz_ref, dws_ref, dbs_ref, dlng_ref, dlnb_ref, dm_acc):
        i = pl.program_id(0)

        @pl.when(i == 0)
        def _():
            dws_ref[...] = jnp.zeros_like(dws_ref)
            dlng_ref[...] = jnp.zeros_like(dlng_ref)
            dlnb_ref[...] = jnp.zeros_like(dlnb_ref)
            dm_acc[...] = jnp.zeros_like(dm_acc)

        lng_v = lng_ref[...]
        u, du, dv, rstd, vhat, vn, wms, mixed, tril = _gmlp_pieces(zu_ref[...].astype(F32), zv_ref[...].astype(F32),
                                                                  lng_v, lnb_ref[...],
                                                                  ws_ref, bs_ref)
        do = do_ref[...]
        dvn_parts = []
        for g in range(GM_GROUPS):
            sl = slice(g * 128, (g + 1) * 128)
            dog = do[:, sl]
            dz_ref[:, sl] = (dog * mixed[g] * du[:, sl]).astype(dz_ref.dtype)
            dmix = dog * u[:, sl]
            dm_acc[:, sl] += dmix
            dws_ref[g] += jnp.where(tril, _dot_nt(dmix, vn[:, sl]), 0.0)
            dvn_parts.append(_dot_tn(wms[g], dmix))
        dvn = jnp.concatenate(dvn_parts, axis=1)
        dlng_ref[...] += jnp.sum(dvn * vhat, axis=0, keepdims=True)
        dlnb_ref[...] += jnp.sum(dvn, axis=0, keepdims=True)
        dvh = dvn * lng_v
        dvv = rstd * (dvh - jnp.mean(dvh, axis=-1, keepdims=True)
                      - vhat * jnp.mean(dvh * vhat, axis=-1, keepdims=True))
        dz_ref[:, GM_WIDTH:] = (dvv * dv).astype(dz_ref.dtype)

        @pl.when(i == n - 1)
        def _():
            for g in range(GM_GROUPS):
                dbs_ref[g] = jnp.sum(dm_acc[:, g * 128:(g + 1) * 128], axis=1, keepdims=True)

    vec = pl.BlockSpec((1, GM_WIDTH), lambda i: (0, 0))
    wsp = pl.BlockSpec((GM_GROUPS, 128, 128), lambda i: (0, 0, 0))
    bsp = pl.BlockSpec((GM_GROUPS, 128, 1), lambda i: (0, 0, 0))
    return _carried(*_pcall(
        body, (proj, proj, d_out, lng, lnb, ws, bs_col), name="gmlp_bwd", grid=(n,),
        in_specs=[pl.BlockSpec((GM_CHUNK, 512), lambda i: (i, COL_ZU)),
                  pl.BlockSpec((GM_CHUNK, 512), lambda i: (i, COL_ZV)),
                  pl.BlockSpec((None, GM_CHUNK, 512), lambda i: (0, i, 0)), vec, vec, wsp, bsp],
        out_specs=(pl.BlockSpec((GM_CHUNK, 2 * GM_WIDTH), lambda i: (i, 0)), wsp, bsp, vec, vec),
        out_shape=(jax.ShapeDtypeStruct((T, 2 * GM_WIDTH), BF16),
                   jax.ShapeDtypeStruct((GM_GROUPS, 128, 128), F32), jax.ShapeDtypeStruct((GM_GROUPS, 128, 1), F32),
                   jax.ShapeDtypeStruct((1, GM_WIDTH), F32), jax.ShapeDtypeStruct((1, GM_WIDTH), F32)),
        scratch_shapes=[pltpu.VMEM((GM_CHUNK, GM_WIDTH), F32)],
        semantics=("arbitrary",), riders=riders), riders)


def _hgrn_lower_bound(lbl):
    return 1.0 / (1.0 + jnp.exp(lbl[1:2, :] - lbl[0:1, :]))


def _hgrn_gates(hq, hf, lb):
    C = HG_CHUNK
    sg = _sigmoid(hf)
    fg = lb + (1.0 - lb) * sg
    sq = _sigmoid(hq)
    row = lax.broadcasted_iota(jnp.int32, (C, C), 0)
    col = lax.broadcasted_iota(jnp.int32, (C, C), 1)
    tril = row >= col
    logf = jnp.log(fg)
    a = _dot_01(tril, logf)
    a_last = jnp.sum(logf, axis=0, keepdims=True)
    first_half = lax.broadcasted_iota(jnp.int32, logf.shape, 0) < (C // 2)
    a_mid = jnp.sum(jnp.where(first_half, logf, 0.0), axis=0, keepdims=True)
    ea, ei, eki, ekl = jnp.exp(a), jnp.exp(a - a_mid), jnp.exp(a_mid - a), jnp.exp(a_last - a)
    k = 1.0 - fg
    q = hq * sq
    qi = (q * ei).astype(BF16).astype(F32)
    ki = (k * eki).astype(BF16).astype(F32)
    return dict(sg=sg, fg=fg, sq=sq, tril=tril, ea=ea, ei=ei, eki=eki, ekl=ekl, e_last=jnp.exp(a_last),
                qe=q * ea, qi=qi, ki=ki, kl=k * ekl)


def _heads(x):
    return [x[:, h * HG_DIM:(h + 1) * HG_DIM] for h in range(HG_HEADS)]


def _hgrn_fwd(proj, lbl, gh, B, S, riders=()):
    C = HG_CHUNK
    NC = S // C
    W = HG_HEADS * HG_DIM

    def body(q_ref, f_ref, i_ref, g_ref, lbl_ref, gh_ref, o_ref, bo_ref, st_ref, state):
        @pl.when(pl.program_id(0) == 0)
        def _():
            state[...] = jnp.zeros_like(state)

        lb = _hgrn_lower_bound(lbl_ref[...])
        ghv = gh_ref[...]
        for b in range(B):
            gt = _hgrn_gates(q_ref[b].astype(F32), f_ref[b].astype(F32), lb)
            v = _heads(i_ref[b])
            qe, qi, ki, kl, e_last = (_heads(gt[n]) for n in ("qe", "qi", "ki", "kl", "e_last"))
            outs, normed = [], []
            for h in range(HG_HEADS):
                p = jnp.where(gt["tril"], _dot_nt(qi[h], ki[h]), 0.0)
                st = state[b, h]
                st_ref[b, h] = st
                o = _dot_nt(qe[h], st) + _dot(p, v[h])
                state[b, h] = st * e_last[h] + _dot_tn(v[h], kl[h])
                outs.append(o)
                normed.append(o * lax.rsqrt(jnp.mean(o * o, axis=-1, keepdims=True) + EPS) * ghv)
            o_ref[b] = jnp.concatenate(outs, axis=1)
            hg = g_ref[b].astype(F32)
            bo_ref[b] = (jnp.concatenate(normed, axis=1) * (hg * _sigmoid(hg))).astype(bo_ref.dtype)

    def col(cb):
        return pl.BlockSpec((B, C, 512), lambda c: (0, c, cb))

    tile = pl.BlockSpec((B, C, W), lambda c: (0, c, 0))
    proj3 = proj.reshape(B, S, proj.shape[-1])
    out, carried = _pcall(
        body, (proj3, proj3, proj3, proj3, lbl, gh), name="hgrn_fwd", grid=(NC,),
        in_specs=[col(COL_HQ), col(COL_HF), col(COL_HI), col(COL_HG),
                  pl.BlockSpec((2, W), lambda c: (0, 0)), pl.BlockSpec((1, HG_DIM), lambda c: (0, 0))],
        out_specs=(tile, tile, pl.BlockSpec((B, None, HG_HEADS, 128, 128), lambda c: (0, c, 0, 0, 0))),
        out_shape=(jax.ShapeDtypeStruct((B, S, W), F32), jax.ShapeDtypeStruct((B, S, W), BF16),
                   jax.ShapeDtypeStruct((B, NC, HG_HEADS, 128, 128), F32)),
        scratch_shapes=[pltpu.VMEM((B, HG_HEADS, 128, 128), F32)],
        semantics=("arbitrary",), riders=riders)
    o_h, b_out, states = out
    out = (o_h, b_out.reshape(B * S, W), states)
    return (out, carried) if riders else out


def _hgrn_bwd(proj, o_saved, states, d_out, lbl, gh, others, B, S, riders=()):
    C = HG_CHUNK
    NC = S // C
    W = HG_HEADS * HG_DIM
    d_gm, d_xq, d_gates = (t.reshape(B, S, t.shape[-1]) for t in others)
    own0 = d_gm.shape[-1]
    xq0 = own0 + 4 * W
    gates0 = xq0 + d_xq.shape[-1]

    def body(q_ref, f_ref, i_ref, g_ref, o_ref, st_ref, do_ref, lbl_ref, gh_ref, gm_ref, xq_ref, gates_ref,
             d_ref, dlbl_ref, dgh_ref, dstate, dlb_acc):
        c = pl.program_id(0)
        d_ref[:, :, :own0] = gm_ref[...]
        d_ref[:, :, xq0:gates0] = xq_ref[...]
        d_ref[:, :, gates0:] = gates_ref[...]

        def put(b, k, val):
            d_ref[b, :, own0 + k * W:own0 + (k + 1) * W] = val.astype(d_ref.dtype)

        @pl.when(c == 0)
        def _():
            dstate[...] = jnp.zeros_like(dstate)
            dgh_ref[...] = jnp.zeros_like(dgh_ref)
            dlb_acc[...] = jnp.zeros_like(dlb_acc)

        lb = _hgrn_lower_bound(lbl_ref[...])
        ghv = gh_ref[...]
        row = lax.broadcasted_iota(jnp.int32, (C, C), 0)
        colm = lax.broadcasted_iota(jnp.int32, (C, C), 1)
        triu = colm >= row
        for b in range(B):
            hq, hg = q_ref[b].astype(F32), g_ref[b].astype(F32)
            gt = _hgrn_gates(hq, f_ref[b].astype(F32), lb)
            tril = gt["tril"]
            v = _heads(i_ref[b])
            qe, qi, ki, kl, e_last = (_heads(gt[n]) for n in ("qe", "qi", "ki", "kl", "e_last"))
            sgg = _sigmoid(hg)
            don_all = do_ref[b] * (hg * sgg)
            o, don = _heads(o_ref[b]), _heads(don_all)
            d_qe, d_qi, d_ki, d_kl, dv, n_all, dal = [], [], [], [], [], [], []
            for h in range(HG_HEADS):
                r = lax.rsqrt(jnp.mean(o[h] * o[h], axis=-1, keepdims=True) + EPS)
                n = o[h] * r
                n_all.append(n)
                dgh_ref[...] += jnp.sum(don[h] * n, axis=0, keepdims=True)
                dn = don[h] * ghv
                d_o = r * (dn - n * jnp.mean(dn * n, axis=-1, keepdims=True))
                st, dst = st_ref[b, h], dstate[b, h]
                p = jnp.where(tril, _dot_nt(qi[h], ki[h]), 0.0)
                dp = jnp.where(tril, _dot_nt(d_o, v[h]), 0.0)
                d_qe.append(_dot(d_o, st))
                d_qi.append(_dot(dp, ki[h]))
                d_ki.append(_dot_tn(dp, qi[h]))
                d_kl.append(_dot(v[h], dst))
                dv.append(_dot_tn(p, d_o) + _dot_nt(kl[h], dst))
                dstate[b, h] = dst * e_last[h] + _dot_tn(d_o, qe[h])
                dal.append(jnp.sum(dst * st, axis=0, keepdims=True) * e_last[h])
            d_qe, d_qi, d_ki, d_kl, n_all, dal = (jnp.concatenate(t, axis=1)
                                                  for t in (d_qe, d_qi, d_ki, d_kl, n_all, dal))
            put(b, 3, do_ref[b] * n_all * jnp.tile(ghv, (1, HG_HEADS)) * (sgg * (1.0 + hg * (1.0 - sgg))))
            put(b, 2, jnp.concatenate(dv, axis=1))
            d_a_last = dal + jnp.sum(d_kl * gt["kl"], axis=0, keepdims=True)
            dq = d_qe * gt["ea"] + d_qi * gt["ei"]
            dk = d_ki * gt["eki"] + d_kl * gt["ekl"]
            da = d_qe * gt["qe"] + d_qi * gt["qi"] - d_ki * gt["ki"] - d_kl * gt["kl"]
            dlogf = _dot_01(triu, da) + d_a_last
            sg, sq = gt["sg"], gt["sq"]
            dfg = dlogf / gt["fg"] - dk
            put(b, 1, dfg * (1.0 - lb) * sg * (1.0 - sg))
            dlb_acc[...] += jnp.sum(dfg * (1.0 - sg), axis=0, keepdims=True)
            put(b, 0, dq * (sq * (1.0 + hq * (1.0 - sq))))

        @pl.when(c == NC - 1)
        def _():
            dlb = dlb_acc[...]
            first = lax.broadcasted_iota(jnp.int32, (2, W), 0) == 0
            dlbl_ref[...] = jnp.where(first, dlb * lb * (1.0 - lb), -dlb * lb * (1.0 - lb))

    def col(cb):
        return pl.BlockSpec((B, C, 512), lambda c: (0, NC - 1 - c, cb))

    tile = pl.BlockSpec((B, C, W), lambda c: (0, NC - 1 - c, 0))
    proj3 = proj.reshape(B, S, proj.shape[-1])

    def rows(width):
        return pl.BlockSpec((B, C, width), lambda c: (0, NC - 1 - c, 0))

    width = proj.shape[-1]
    out, carried = _pcall(
        body, (proj3, proj3, proj3, proj3, o_saved, states, d_out.reshape(3, B, S, W), lbl, gh, d_gm, d_xq, d_gates),
        name="hgrn_bwd", grid=(NC,),
        in_specs=[col(COL_HQ), col(COL_HF), col(COL_HI), col(COL_HG), tile,
                  pl.BlockSpec((B, None, HG_HEADS, 128, 128), lambda c: (0, NC - 1 - c, 0, 0, 0)),
                  pl.BlockSpec((None, B, C, W), lambda c: (1, 0, NC - 1 - c, 0)),
                  pl.BlockSpec((2, W), lambda c: (0, 0)), pl.BlockSpec((1, HG_DIM), lambda c: (0, 0)),
                  rows(d_gm.shape[-1]), rows(d_xq.shape[-1]), rows(d_gates.shape[-1])],
        out_specs=(rows(width), pl.BlockSpec((2, W), lambda c: (0, 0)), pl.BlockSpec((1, HG_DIM), lambda c: (0, 0))),
        out_shape=(jax.ShapeDtypeStruct((B, S, width), BF16), jax.ShapeDtypeStruct((2, W), F32),
                   jax.ShapeDtypeStruct((1, HG_DIM), F32)),
        scratch_shapes=[pltpu.VMEM((B, HG_HEADS, 128, 128), F32), pltpu.VMEM((1, W), F32)],
        semantics=("arbitrary",), riders=riders)
    out = (out[0].reshape(B * S, width),) + tuple(out[1:])
    return (out, carried) if riders else out


_XA_SCALE = XA_DIM ** -0.5


def _attn_probs(qh, kh):
    s = _dot_nt(qh, kh) * _XA_SCALE
    e = jnp.exp(s - jnp.max(s, axis=-1, keepdims=True))
    return e / jnp.sum(e, axis=-1, keepdims=True)


def _attn_fwd(proj, kv, B, S):
    T = B * S
    tq = _row_tile(S)
    nq = S // tq
    W = XA_HEADS * XA_DIM

    def body(q_ref, kv_ref, o_ref):
        for h in range(XA_HEADS):
            sl = slice(h * 128, (h + 1) * 128)
            p = _attn_probs(q_ref[:, sl], kv_ref[:, sl])
            o_ref[:, sl] = _dot(p, kv_ref[:, W + h * 128:W + (h + 1) * 128]).astype(o_ref.dtype)

    return _pallas(
        body, name="attn_fwd", grid=(B, nq),
        in_specs=[pl.BlockSpec((tq, 512), lambda b, i: (b * nq + i, COL_XQ)),
                  pl.BlockSpec((MEM_LEN, 2 * W), lambda b, i: (b, 0))],
        out_specs=pl.BlockSpec((tq, W), lambda b, i: (b * nq + i, 0)),
        out_shape=jax.ShapeDtypeStruct((T, W), BF16), compiler_params=_cp("parallel", "parallel"),
    )(proj, kv)


def _attn_bwd(proj, kv, d_out, B, S):
    T = B * S
    tq = _row_tile(S)
    nq = S // tq
    W = XA_HEADS * XA_DIM

    def body(q_ref, kv_ref, do_ref, dq_ref, dkv_ref):
        @pl.when(pl.program_id(1) == 0)
        def _():
            dkv_ref[...] = jnp.zeros_like(dkv_ref)

        for h in range(XA_HEADS):
            sl = slice(h * 128, (h + 1) * 128)
            slv = slice(W + h * 128, W + (h + 1) * 128)
            qh = q_ref[:, sl]
            kh = kv_ref[:, sl]
            p = _attn_probs(qh, kh)
            dc = do_ref[:, sl]
            dp = _dot_nt(dc, kv_ref[:, slv])
            ds = p * (dp - jnp.sum(dp * p, axis=-1, keepdims=True)) * _XA_SCALE
            dq_ref[:, sl] = _dot(ds, kh).astype(dq_ref.dtype)
            dkv_ref[:, sl] += _dot_tn(ds, qh)
            dkv_ref[:, slv] += _dot_tn(p, dc)

    kvspec = pl.BlockSpec((MEM_LEN, 2 * W), lambda b, i: (b, 0))
    tile = pl.BlockSpec((tq, W), lambda b, i: (b * nq + i, 0))
    return _pallas(
        body, name="attn_bwd", grid=(B, nq),
        in_specs=[pl.BlockSpec((tq, 512), lambda b, i: (b * nq + i, COL_XQ)), kvspec,
                  pl.BlockSpec((None, tq, W), lambda b, i: (2, b * nq + i, 0))],
        out_specs=(tile, kvspec),
        out_shape=(jax.ShapeDtypeStruct((T, W), BF16), jax.ShapeDtypeStruct((B * MEM_LEN, 2 * W), F32)),
        compiler_params=_cp("parallel", "arbitrary"),
    )(proj, kv, d_out)


_MERGE_TM = 256
_GATE_W = 512


def _gate_specs(tm):
    base = COL_GATE0 // _GATE_W
    return [pl.BlockSpec((tm, _GATE_W), functools.partial(lambda i, k: (i, base + k), k=k)) for k in range(6)]


def _merge_fwd(a_out, b_out, c_out, wb, proj, riders=()):
    T = a_out.shape[0]
    tm = _row_tile(T, _MERGE_TM)
    nq, _, wd = wb.shape
    per_half = _GATE_W // wd

    def body(a_ref, b_ref, c_ref, w_ref, *rest):
        gates, (m_ref, up_ref) = rest[:6], rest[6:]
        for hf in range(2):
            cols = slice(hf * _GATE_W, (hf + 1) * _GATE_W)
            acc = None
            for n, br in enumerate((a_ref, b_ref, c_ref)):
                x = br[...]
                up = jnp.concatenate([_dot(x, w_ref[per_half * hf + j, n * BR_WIDTH:(n + 1) * BR_WIDTH, :])
                                      for j in range(per_half)], axis=1)
                up_ref[n, :, cols] = up.astype(up_ref.dtype)
                term = _sigmoid(gates[2 * n + hf][...].astype(F32)) * up
                acc = term if acc is None else acc + term
            m_ref[:, cols] = acc.astype(m_ref.dtype)

    br_spec = pl.BlockSpec((tm, BR_WIDTH), lambda i: (i, 0))
    return _carried(*_pcall(
        body, (a_out, b_out, c_out, wb, *([proj] * 6)), name="merge_fwd", grid=(T // tm,),
        in_specs=[br_spec, br_spec, br_spec,
                  pl.BlockSpec((nq, 3 * BR_WIDTH, wd), lambda i: (0, 0, 0))] + _gate_specs(tm),
        out_specs=(pl.BlockSpec((tm, D_MODEL), lambda i: (i, 0)), pl.BlockSpec((3, tm, D_MODEL), lambda i: (0, i, 0))),
        out_shape=(jax.ShapeDtypeStruct((T, D_MODEL), BF16), jax.ShapeDtypeStruct((3, T, D_MODEL), BF16)),
        semantics=("parallel",), riders=riders), riders)


def _branch_bwd_act(d_ups, wb, riders=()):
    _, T, D = d_ups.shape
    nq, _, wd = wb.shape
    tm = _row_tile(T)

    def body(d_ref, w_ref, o_ref):
        acc = None
        for q in range(nq):
            part = _dot_nt(d_ref[:, q * wd:(q + 1) * wd], w_ref[q])
            acc = part if acc is None else acc + part
        o_ref[...] = acc

    return _carried(*_pcall(
        body, (d_ups, wb), name="d_branch", grid=(3, T // tm),
        in_specs=[pl.BlockSpec((None, tm, D), lambda n, i: (n, i, 0)),
                  pl.BlockSpec((nq, BR_WIDTH, wd), lambda n, i: (0, n, 0))],
        out_specs=pl.BlockSpec((None, tm, BR_WIDTH), lambda n, i: (n, i, 0)),
        out_shape=jax.ShapeDtypeStruct((3, T, BR_WIDTH), F32), semantics=("parallel", "parallel"),
        riders=riders), riders)


def _branch_bwd_weight(name, br, d_ups, n, into=None):
    T = br.shape[0]
    D = d_ups.shape[2]
    wd = D // N_CHIPS
    tt = _row_tile(T, _TN_TOKENS)
    n_br = d_ups.shape[0]

    def body(b_ref, d_ref, *rest):
        o_ref = rest[-1]
        k = pl.program_id(0)
        for q in range(N_CHIPS):
            part = _dot_tn(b_ref[...], d_ref[:, q * wd:(q + 1) * wd])

            @pl.when(k == 0)
            def _():
                o_ref[q] = part

            @pl.when(k > 0)
            def _():
                o_ref[q] += part

    return _pallas(
        body, name=name, grid=(T // tt,),
        in_specs=[pl.BlockSpec((tt, BR_WIDTH), lambda k: (k, 0)),
                  pl.BlockSpec((None, tt, D), lambda k: (n, k, 0))] + ([] if into is None else [HBM_SPEC]),
        out_specs=pl.BlockSpec((N_CHIPS, BR_WIDTH, wd), lambda k: (0, n, 0)),
        out_shape=jax.ShapeDtypeStruct((N_CHIPS, n_br * BR_WIDTH, wd), F32),
        input_output_aliases={} if into is None else {2: 0}, compiler_params=_cp("arbitrary"),
    )(br, d_ups, *(() if into is None else (into,)))


def _merge_bwd(d_merged, ups, proj, riders=()):
    T = d_merged.shape[0]
    tm = _row_tile(T, _MERGE_TM)

    def body(dm_ref, up_ref, *rest):
        gates, (dup_ref, dg_ref) = rest[:6], rest[6:]
        for hf in range(2):
            cols = slice(hf * _GATE_W, (hf + 1) * _GATE_W)
            dm = dm_ref[:, cols]
            for n in range(3):
                gate = _sigmoid(gates[2 * n + hf][...].astype(F32))
                dup_ref[n, :, cols] = (dm * gate).astype(dup_ref.dtype)
                dg_ref[:, n * D_MODEL + hf * _GATE_W:n * D_MODEL + (hf + 1) * _GATE_W] = (
                    dm * up_ref[n, :, cols].astype(F32) * gate * (1.0 - gate)).astype(dg_ref.dtype)

    tile = pl.BlockSpec((tm, D_MODEL), lambda i: (i, 0))
    tile3 = pl.BlockSpec((3, tm, D_MODEL), lambda i: (0, i, 0))
    return _carried(*_pcall(
        body, (d_merged, ups, *([proj] * 6)), name="merge_bwd", grid=(T // tm,),
        in_specs=[tile, tile3] + _gate_specs(tm),
        out_specs=(tile3, pl.BlockSpec((tm, 3 * D_MODEL), lambda i: (i, 0))),
        out_shape=(jax.ShapeDtypeStruct((3, T, D_MODEL), BF16), jax.ShapeDtypeStruct((T, 3 * D_MODEL), BF16)),
        semantics=("parallel",), riders=riders), riders)


_CONV_TF = D_FF // 2
_CONV_TS = 256
_HALO = 16


def _conv_fwd(ab, cw, cb, B, S):
    T = B * S
    ts = _row_tile(S, _CONV_TS)
    tf = _CONV_TF
    nb = D_FF // tf
    tps = S // ts
    hb = ts // _HALO

    def body(a_ref, p_ref, b_ref, w_ref, cb_ref, o_ref):
        start = (pl.program_id(0) % tps) == 0
        a = a_ref[...].astype(F32)
        prev = jnp.where(start, 0.0, p_ref[...].astype(F32))
        ext = jnp.concatenate([prev, a], axis=0)
        a1 = pltpu.roll(ext, 1, 0)[_HALO:, :]
        a2 = pltpu.roll(ext, 2, 0)[_HALO:, :]
        ac = cb_ref[...] + w_ref[0] * a2 + w_ref[1] * a1 + w_ref[2] * a
        o_ref[...] = (ac * _sigmoid(ac) * b_ref[...].astype(F32)).astype(o_ref.dtype)

    return _pallas(
        body, name="conv_fwd", grid=(T // ts, nb),
        in_specs=[pl.BlockSpec((ts, tf), lambda i, j: (i, j)),
                  pl.BlockSpec((_HALO, tf), lambda i, j: (jnp.maximum(i * hb - 1, 0), j)),
                  pl.BlockSpec((ts, tf), lambda i, j: (i, j + nb)),
                  pl.BlockSpec((3, 1, tf), lambda i, j: (0, 0, j)),
                  pl.BlockSpec((1, tf), lambda i, j: (0, j))],
        out_specs=pl.BlockSpec((ts, tf), lambda i, j: (i, j)),
        out_shape=jax.ShapeDtypeStruct((T, D_FF), BF16), compiler_params=_cp("parallel", "parallel"),
    )(ab, ab, ab, cw, cb)


def _conv_bwd(ab, d_ff, cw, cb, B, S, riders=()):
    T = B * S
    ts = _row_tile(S, _CONV_TS)
    tf = _CONV_TF
    nb = D_FF // tf
    tps = S // ts
    hb = ts // _HALO
    last_h = T // _HALO - 1
    n_ext = ts + _HALO

    def body(a_ref, ap_ref, an_ref, b_ref, bn_ref, d_ref, dn_ref, w_ref, cb_ref, dab_ref, dw_ref, dcb_ref):
        i = pl.program_id(1)

        @pl.when(i == 0)
        def _():
            dw_ref[...] = jnp.zeros_like(dw_ref)
            dcb_ref[...] = jnp.zeros_like(dcb_ref)

        start = (i % tps) == 0
        end = (i % tps) == tps - 1
        a = a_ref[...].astype(F32)
        ext = jnp.concatenate([jnp.where(start, 0.0, ap_ref[...].astype(F32)), a, an_ref[...].astype(F32)], axis=0)
        r1 = pltpu.roll(ext, 1, 0)[_HALO:, :]
        r2 = pltpu.roll(ext, 2, 0)[_HALO:, :]
        ac = cb_ref[...] + w_ref[0] * r2 + w_ref[1] * r1 + w_ref[2] * ext[_HALO:, :]
        sg = _sigmoid(ac)
        d_e = jnp.concatenate([d_ref[...].astype(F32), jnp.where(end, 0.0, dn_ref[...].astype(F32))], axis=0)
        b_e = jnp.concatenate([b_ref[...].astype(F32), bn_ref[...].astype(F32)], axis=0)
        dab_ref[1] = (d_e[:ts, :] * (ac * sg)[:ts, :]).astype(dab_ref.dtype)
        dac = d_e * b_e * sg * (1.0 + ac * (1.0 - sg))
        u1 = pltpu.roll(dac, n_ext - 1, 0)[:ts, :]
        u2 = pltpu.roll(dac, n_ext - 2, 0)[:ts, :]
        dac0 = dac[:ts, :]
        dab_ref[0] = (w_ref[2] * dac0 + w_ref[1] * u1 + w_ref[0] * u2).astype(dab_ref.dtype)
        dcb_ref[...] += jnp.sum(dac0, axis=0, keepdims=True)
        dw_ref[2] += jnp.sum(dac0 * a, axis=0, keepdims=True)
        dw_ref[1] += jnp.sum(dac0 * r1[:ts, :], axis=0, keepdims=True)
        dw_ref[0] += jnp.sum(dac0 * r2[:ts, :], axis=0, keepdims=True)

    def cur(off):
        return pl.BlockSpec((ts, tf), lambda j, i: (i, j + off))

    def nxt(off):
        return pl.BlockSpec((_HALO, tf), lambda j, i: (jnp.minimum((i + 1) * hb, last_h), j + off))

    return _carried(*_pcall(
        body, (ab, ab, ab, ab, ab, d_ff, d_ff, cw, cb), name="conv_bwd", grid=(nb, T // ts),
        in_specs=[cur(0), pl.BlockSpec((_HALO, tf), lambda j, i: (jnp.maximum(i * hb - 1, 0), j)), nxt(0),
                  cur(nb), nxt(nb), cur(0), nxt(0),
                  pl.BlockSpec((3, 1, tf), lambda j, i: (0, 0, j)), pl.BlockSpec((1, tf), lambda j, i: (0, j))],
        out_specs=(pl.BlockSpec((2, ts, tf), lambda j, i: (0, i, j)), pl.BlockSpec((3, 1, tf), lambda j, i: (0, 0, j)),
                   pl.BlockSpec((1, tf), lambda j, i: (0, j))),
        out_shape=(jax.ShapeDtypeStruct((2, T, D_FF), BF16),
                   jax.ShapeDtypeStruct((3, 1, D_FF), F32), jax.ShapeDtypeStruct((1, D_FF), F32)),
        semantics=("parallel", "arbitrary"), riders=riders), riders)


def _local_step(x, mem, tgt, p, comm, B, S):
    g = {}
    h = comm.carry("norm1", lambda r: _rms_fwd("norm1", x, p["norm1_g"], riders=r))
    proj = comm.carry("in_proj", lambda r: _mm_cs("in_proj", h, comm.w("w_in"), BF16, riders=r))
    a_out = _gmlp_fwd(proj, p["ln_v_g"], p["ln_v_b"], p["w_spatial"], p["b_spatial"])
    o_h, b_out, states = comm.carry(
        "hgrn_fwd", lambda r: _hgrn_fwd(proj, p["lb_logits"], p["hgrn_norm_g"], B, S, riders=r))
    memn = _rms_fwd("mem_norm", mem, p["mem_norm_g"])
    kv = _mm_rs("mem_kv", memn, comm.w("w_mem_kv"), F32)
    c_out = _attn_fwd(proj, kv, B, S)
    merged, ups = comm.carry(
        "merge_fwd", lambda r: _merge_fwd(a_out, b_out, c_out, comm.w("w_branch"), proj, riders=r))
    x1, h2 = _proj_res_norm("out_proj_norm2", merged, comm.w("w_out"), x, p["norm2_g"])
    ab = comm.carry("up_proj", lambda r: _mm_cs("up_proj", h2, comm.w("w_up"), BF16, riders=r))
    conv_w = comm.w("conv_w")
    ff = _conv_fwd(ab, conv_w, p["conv_b"], B, S)
    dx2, g["final_g"], loss = _proj_res_loss("down_proj_loss", ff, comm.w("w_down"), x1, tgt, p["final_g"])

    comm.grad("w_down", _mm_tn_rs("g_w_down", ff, dx2, to=D_FF // 2))
    d_ff = comm.carry("d_ff", lambda r: _mm_nt_rs("d_ff", dx2, comm.w("w_down"), BF16, riders=r))
    d_ab, g["conv_w"], g["conv_b"] = comm.carry(
        "conv_bwd", lambda r: _conv_bwd(ab, d_ff, conv_w, p["conv_b"], B, S, riders=r))
    comm.grad("w_up", _mm_tn_cs("g_w_up", h2, d_ab, N_CHIPS, to=512, stacked=True))
    d_x1, g["norm2_g"] = comm.carry("d_h2", lambda r: _mm_nt_cs(
        "d_h2_norm2_bwd", d_ab, comm.w("w_up"), F32, riders=r, stacked=True, norm_bwd=(x1, p["norm2_g"], dx2)))
    comm.grad("w_out", _mm_tn_rs("g_w_out", merged, d_x1, to=512))
    d_merged = _mm_nt_rs("d_merged", d_x1, comm.w("w_out"), F32)
    d_ups, d_gates = comm.carry("merge_bwd", lambda r: _merge_bwd(d_merged, ups, proj, riders=r))

    d_br = comm.carry("d_branch", lambda r: _branch_bwd_act(d_ups, comm.w("w_branch"), riders=r))
    g_branch = None
    for n, br in enumerate((a_out, b_out, c_out)):
        g_branch = _branch_bwd_weight("g_w_branch%d" % n, br, d_ups, n, into=g_branch)
    comm.grad("w_branch", g_branch)

    d_gm, g["w_spatial"], g["b_spatial"], g["ln_v_g"], g["ln_v_b"] = comm.carry(
        "gmlp_bwd", lambda r: _gmlp_bwd(proj, d_br, p["ln_v_g"], p["ln_v_b"], p["w_spatial"], p["b_spatial"],
                                        riders=r))
    d_xq, d_kv = _attn_bwd(proj, kv, d_br, B, S)
    comm.grad("w_mem_kv", _mm_tn_rs("g_w_mem_kv", memn, d_kv, to=512))
    d_memn = _mm_nt_rs("d_memn", d_kv, comm.w("w_mem_kv"), F32)
    _, g["mem_norm_g"] = _rms_bwd("mem_norm_bwd", mem, p["mem_norm_g"], d_memn, None)
    d_proj, g["lb_logits"], g["hgrn_norm_g"] = comm.carry(
        "hgrn_bwd", lambda r: _hgrn_bwd(proj, o_h, states, d_br, p["lb_logits"], p["hgrn_norm_g"],
                                        (d_gm, d_xq, d_gates), B, S, riders=r))
    comm.small_grads([g[n].reshape(_SMALL_SHAPE[n]) for n in _SMALL_EARLY] + [loss])
    comm.grad("w_in", comm.carry("g_w_in", lambda r: _mm_tn_cs("g_w_in", h, d_proj, N_CHIPS, to=512, riders=r)))
    grad_x, g["norm1_g"] = comm.carry("d_h", lambda r: _mm_nt_cs(
        "d_h_norm1_bwd", d_proj, comm.w("w_in"), F32, riders=r, norm_bwd=(x, p["norm1_g"], d_x1)))
    return loss, grad_x, g


HBM_SPEC = pl.BlockSpec(memory_space=pltpu.HBM)


def _place():
    x, y, c = lax.axis_index("x"), lax.axis_index("y"), lax.axis_index("c")
    other_chips = [(1 - x, y), (x, 1 - y), (1 - x, 1 - y)]
    return x, y, c, other_chips


def _remote(src, dst, send_sem, recv_sem, dev):
    return pltpu.make_async_remote_copy(src_ref=src, dst_ref=dst, send_sem=send_sem, recv_sem=recv_sem,
                                        device_id=dev, device_id_type=MESH_ID)


class _Exchange:
    def __init__(self, operands, out_shape, aliases, scratch, start, finish, mid=None, mid_at=0.5):
        self.operands, self.out_shape, self.aliases, self.scratch = operands, out_shape, aliases, scratch
        self.start, self.finish, self.mid, self.mid_at = start, finish, mid, mid_at


def _run_exchanges(name, exs):
    n_in = [len(ex.operands) for ex in exs]
    n_out = [len(ex.out_shape) for ex in exs]
    n_scr = [len(ex.scratch) for ex in exs]

    def body(*refs):
        ins, outs, scr = refs[:sum(n_in)], refs[sum(n_in):sum(n_in) + sum(n_out)], refs[sum(n_in) + sum(n_out):]
        parts, oi, oo, os_ = [], 0, 0, 0
        for k in range(len(exs)):
            parts.append((ins[oi:oi + n_in[k]], outs[oo:oo + n_out[k]], scr[os_:os_ + n_scr[k]]))
            oi, oo, os_ = oi + n_in[k], oo + n_out[k], os_ + n_scr[k]
        for ex, part in zip(exs, parts):
            ex.start(*part)
        for ex, part in zip(exs, parts):
            if ex.mid is not None:
                ex.mid(*part)
        for ex, part in zip(exs, parts):
            ex.finish(*part)

    aliases, ops, shapes, scratch, oi, oo = {}, [], [], [], 0, 0
    for k, ex in enumerate(exs):
        aliases.update({oi + a: oo + b for a, b in ex.aliases.items()})
        oi, oo = oi + n_in[k], oo + n_out[k]
        ops += list(ex.operands)
        shapes += [pltpu.HBM(s.shape, s.dtype) for s in ex.out_shape]
        scratch += list(ex.scratch)
    res = _pallas(
        body, name=name, in_specs=[HBM_SPEC] * len(ops), out_specs=(HBM_SPEC,) * len(shapes), out_shape=tuple(shapes),
        input_output_aliases=aliases, scratch_shapes=scratch,
    )(*ops)
    out, oo = [], 0
    for k in range(len(exs)):
        out.append(list(res[oo:oo + n_out[k]]))
        oo += n_out[k]
    return out


def _ex_all_gather(slabs, halved, part=(0, 1)):
    n = len(slabs)

    def rows(a, cc):
        if not halved[a]:
            return slice(None)
        pr = slabs[a].shape[1] // part[1]
        return pl.ds(part[0] * pr + cc * (pr // 2), pr // 2)

    def ici(bufs, scr, a, j, chip, c, mine):
        px, py = chip
        x, y, _, _ = _place()
        qs = 2 * x + y if mine else 2 * px + py
        piece = bufs[a].at[qs, rows(a, c)]
        return _remote(piece, piece, scr[0].at[3 * a + j], scr[1].at[3 * a + j], (px, py, c))

    def d2d(bufs, scr, a, j, chip, cc):
        px, py = chip
        x, y, c, _ = _place()
        piece = bufs[a].at[2 * px + py, rows(a, cc)]
        return _remote(piece, piece, scr[2].at[3 * a + j], scr[3].at[3 * a + j], (x, y, 1 - c))

    def start(ins, outs, scr):
        _, _, c, chips = _place()
        for j, chip in enumerate(chips):
            for a in range(n):
                ici(outs, scr, a, j, chip, c, True).start()

    def finish(ins, outs, scr):
        _, _, c, chips = _place()
        for j, chip in enumerate(chips):
            for a in range(n):
                ici(outs, scr, a, j, chip, c, False).wait_recv()
                if halved[a]:
                    d2d(outs, scr, a, j, chip, c).start()
        for j, chip in enumerate(chips):
            for a in range(n):
                if halved[a]:
                    d2d(outs, scr, a, j, chip, 1 - c).wait_recv()
        for j, chip in enumerate(chips):
            for a in range(n):
                ici(outs, scr, a, j, chip, c, True).wait_send()
                if halved[a]:
                    d2d(outs, scr, a, j, chip, c).wait_send()

    return _Exchange(list(slabs), [jax.ShapeDtypeStruct(s.shape, s.dtype) for s in slabs],
                     {a: a for a in range(n)}, [pltpu.SemaphoreType.DMA((3 * n,))] * 4, start, finish)


def _ex_gather_relay(slabs, mid_at=0.5):
    n = len(slabs)

    def rows(a, cc):
        hr = slabs[a].shape[1] // 2
        return pl.ds(cc * hr, hr)

    def peers():
        x, y, c, _ = _place()
        nbr0 = ((x + c) % 2, (y + 1 - c) % 2)
        nbr1 = ((x + 1 - c) % 2, (y + c) % 2)
        return x, y, c, nbr0, nbr1, (1 - x, 1 - y)

    def ici(bufs, scr, a, k, chip, dev, cc):
        _, _, c, _, _, _ = peers()
        piece = bufs[a].at[2 * chip[0] + chip[1], rows(a, cc)]
        return _remote(piece, piece, scr[0].at[3 * a + k], scr[1].at[3 * a + k], (dev[0], dev[1], c))

    def d2d(bufs, scr, a, k, chip, cc):
        x, y, c, _, _, _ = peers()
        piece = bufs[a].at[2 * chip[0] + chip[1], rows(a, cc)]
        return _remote(piece, piece, scr[2].at[3 * a + k], scr[3].at[3 * a + k], (x, y, 1 - c))

    def start(ins, outs, scr):
        x, y, c, nbr0, nbr1, _ = peers()
        for a in range(n):
            ici(outs, scr, a, 0, (x, y), nbr0, c).start()
            ici(outs, scr, a, 1, (x, y), nbr1, c).start()

    def mid(ins, outs, scr):
        x, y, c, nbr0, nbr1, diag = peers()
        for a in range(n):
            ici(outs, scr, a, 0, nbr0, nbr0, c).wait_recv()
            ici(outs, scr, a, 2, nbr0, nbr1, c).start()
            d2d(outs, scr, a, 0, nbr0, c).start()
        for a in range(n):
            ici(outs, scr, a, 1, nbr1, nbr1, c).wait_recv()
            d2d(outs, scr, a, 1, nbr1, c).start()

    def finish(ins, outs, scr):
        x, y, c, nbr0, nbr1, diag = peers()
        for a in range(n):
            ici(outs, scr, a, 2, diag, nbr1, c).wait_recv()
            d2d(outs, scr, a, 2, diag, c).start()
        for a in range(n):
            d2d(outs, scr, a, 0, nbr1, 1 - c).wait_recv()
            d2d(outs, scr, a, 1, nbr0, 1 - c).wait_recv()
            d2d(outs, scr, a, 2, diag, 1 - c).wait_recv()
        for a in range(n):
            ici(outs, scr, a, 0, (x, y), nbr0, c).wait_send()
            ici(outs, scr, a, 1, (x, y), nbr1, c).wait_send()
            ici(outs, scr, a, 2, nbr0, nbr1, c).wait_send()
            d2d(outs, scr, a, 0, nbr0, c).wait_send()
            d2d(outs, scr, a, 1, nbr1, c).wait_send()
            d2d(outs, scr, a, 2, diag, c).wait_send()

    return _Exchange(list(slabs), [jax.ShapeDtypeStruct(s.shape, s.dtype) for s in slabs],
                     {a: a for a in range(n)}, [pltpu.SemaphoreType.DMA((3 * n,))] * 4, start, finish, mid, mid_at)


def _ex_to_sibling(grads):
    n = len(grads)

    def copy(ins, outs, scr, a):
        x, y, c, _ = _place()
        hr = grads[a].shape[1] // 2
        return _remote(ins[a].at[:, pl.ds((1 - c) * hr, hr), :], outs[a], scr[0].at[a], scr[1].at[a], (x, y, 1 - c))

    def start(ins, outs, scr):
        for a in range(n):
            copy(ins, outs, scr, a).start()

    def finish(ins, outs, scr):
        for a in range(n):
            copy(ins, outs, scr, a).wait()

    out_shape = [jax.ShapeDtypeStruct((g.shape[0], g.shape[1] // 2, g.shape[2]), g.dtype) for g in grads]
    return _Exchange(list(grads), out_shape, {}, [pltpu.SemaphoreType.DMA((n,))] * 2, start, finish)


def _ex_to_owner(parts, part=(0, 1), landing=None):
    n = len(parts)

    def copy(ins, outs, scr, a, j, chip):
        _, _, c, _ = _place()
        px, py = chip
        pr = parts[a].shape[1] // part[1]
        rows = pl.ds(part[0] * pr, pr)
        return _remote(ins[a].at[2 * px + py, rows], outs[a].at[j, rows], scr[0].at[3 * a + j],
                       scr[1].at[3 * a + j], (px, py, c))

    def start(ins, outs, scr):
        for j, chip in enumerate(_place()[3]):
            for a in range(n):
                copy(ins, outs, scr, a, j, chip).start()

    def finish(ins, outs, scr):
        for j, chip in enumerate(_place()[3]):
            for a in range(n):
                copy(ins, outs, scr, a, j, chip).wait()

    out_shape = [jax.ShapeDtypeStruct((3,) + p.shape[1:], p.dtype) for p in parts]
    operands, aliases = list(parts), {}
    if landing is not None:
        operands, aliases = operands + list(landing), {n + a: a for a in range(n)}
    return _Exchange(operands, out_shape, aliases, [pltpu.SemaphoreType.DMA((3 * n,))] * 2, start, finish)


def _ex_share_halves(bufs):
    n = len(bufs)

    def copy(outs, scr, a, cc):
        x, y, c, _ = _place()
        hr = bufs[a].shape[0] // 2
        piece = outs[a].at[pl.ds(cc * hr, hr), :]
        return _remote(piece, piece, scr[0].at[a], scr[1].at[a], (x, y, 1 - c))

    def start(ins, outs, scr):
        c = _place()[2]
        for a in range(n):
            copy(outs, scr, a, c).start()

    def finish(ins, outs, scr):
        c = _place()[2]
        for a in range(n):
            copy(outs, scr, a, c).wait_send()
            copy(outs, scr, a, 1 - c).wait_recv()

    return _Exchange(list(bufs), [jax.ShapeDtypeStruct(b.shape, b.dtype) for b in bufs], {a: a for a in range(n)},
                     [pltpu.SemaphoreType.DMA((n,))] * 2, start, finish)


def _ex_gather_small(arrs):
    n = len(arrs)

    def peer_of(m):
        x, y, c, _ = _place()
        return (1 - x if m & 4 else x, 1 - y if m & 2 else y, 1 - c if m & 1 else c)

    def own(ins, outs, scr, a):
        x, y, c, _ = _place()
        return pltpu.make_async_copy(ins[a], outs[a].at[4 * x + 2 * y + c], scr[2].at[a])

    def start(ins, outs, scr):
        x, y, c, _ = _place()
        for a in range(n):
            own(ins, outs, scr, a).start()
        for m in range(1, N_DEV):
            for a in range(n):
                k = (N_DEV - 1) * a + m - 1
                _remote(ins[a], outs[a].at[4 * x + 2 * y + c], scr[0].at[k], scr[1].at[k], peer_of(m)).start()

    def finish(ins, outs, scr):
        for a in range(n):
            own(ins, outs, scr, a).wait()
        for m in range(1, N_DEV):
            px, py, pc = peer_of(m)
            for a in range(n):
                k = (N_DEV - 1) * a + m - 1
                slot = outs[a].at[4 * px + 2 * py + pc]
                cp = _remote(ins[a], slot, scr[0].at[k], scr[1].at[k], (px, py, pc))
                cp.wait_send()
                cp.wait_recv()

    out_shape = [jax.ShapeDtypeStruct((N_DEV,) + a.shape, a.dtype) for a in arrs]
    return _Exchange(list(arrs), out_shape, {},
                     [pltpu.SemaphoreType.DMA(((N_DEV - 1) * n,))] * 2 + [pltpu.SemaphoreType.DMA((n,))], start, finish)


def _div_tile(n, want):
    best = None
    for t in range(8, min(n, want) + 1, 8):
        if n % t == 0:
            best = t
    assert best is not None, n
    return best


def _cast_into_slab(name, w, place, dtype):
    r, cc = w.shape
    tr = r if r * cc <= 128 * 1024 else _div_tile(r, 256)

    def body(s_ref, w_ref, o_ref):
        o_ref[...] = w_ref[...].astype(o_ref.dtype)

    return _pallas(
        body, name=name,
        grid_spec=pltpu.PrefetchScalarGridSpec(
            num_scalar_prefetch=1, grid=(r // tr,),
            in_specs=[pl.BlockSpec((tr, cc), lambda i, s: (i, 0))],
            out_specs=pl.BlockSpec((None, tr, cc), lambda i, s: (s[0], i, 0))),
        out_shape=jax.ShapeDtypeStruct((N_CHIPS, r, cc), dtype), compiler_params=_cp("parallel"),
    )(place, w)


def _add_half(name, g, rcv, place):
    nq, r, cc = g.shape
    hr = r // 2

    def body(s_ref, g_ref, r_ref, o_ref):
        o_ref[...] = (g_ref[...] + r_ref[...]).astype(o_ref.dtype)

    spec = pl.BlockSpec((None, hr, cc), lambda i, s: (i, 0, 0))
    return _pallas(
        body, name=name,
        grid_spec=pltpu.PrefetchScalarGridSpec(
            num_scalar_prefetch=1, grid=(nq,),
            in_specs=[pl.BlockSpec((None, hr, cc), lambda i, s: (i, s[1], 0)), spec], out_specs=spec),
        out_shape=jax.ShapeDtypeStruct((nq, hr, cc), BF16), compiler_params=_cp("parallel"),
    )(place, g, rcv)


def _sum_owner(name, part, rcv, place):
    _, hr, cc = part.shape
    tr = _div_tile(hr, 128)
    nb = hr // tr

    def body(s_ref, p_ref, r_ref, o_ref):
        o_ref[...] = ((p_ref[...].astype(F32) + r_ref[0].astype(F32)) + r_ref[1].astype(F32)) + r_ref[2].astype(F32)

    return _pallas(
        body, name=name,
        grid_spec=pltpu.PrefetchScalarGridSpec(
            num_scalar_prefetch=1, grid=(nb,),
            in_specs=[pl.BlockSpec((None, tr, cc), lambda i, s: (s[0], i, 0)),
                      pl.BlockSpec((3, tr, cc), lambda i, s: (0, i, 0))],
            out_specs=pl.BlockSpec((tr, cc), lambda i, s: (s[1] * nb + i, 0))),
        out_shape=jax.ShapeDtypeStruct((2 * hr, cc), F32), compiler_params=_cp("parallel"),
    )(place, part, rcv)


def _sum_small(gathered, local, place):
    n = len(gathered)

    def body(s_ref, *refs):
        g_refs, l_refs, o_refs = refs[:n], refs[n:2 * n], refs[2 * n:]
        me = s_ref[2]
        for g_ref, l_ref, o_ref in zip(g_refs, l_refs, o_refs):
            acc = None
            for d in range(N_DEV):
                term = jnp.where(me == d, l_ref[...], g_ref[d])
                acc = term if acc is None else acc + term
            o_ref[...] = acc

    def whole(shape):
        return pl.BlockSpec(shape, lambda i, s, nd=len(shape): (0,) * nd)

    return _pallas(
        body, name="sum_small",
        grid_spec=pltpu.PrefetchScalarGridSpec(
            num_scalar_prefetch=1, grid=(1,),
            in_specs=[whole(g.shape) for g in gathered] + [whole(a.shape) for a in local],
            out_specs=tuple(whole(a.shape) for a in local)),
        out_shape=tuple(jax.ShapeDtypeStruct(a.shape, a.dtype) for a in local), compiler_params=_cp("arbitrary"),
    )(place, *gathered, *local)


def _adamw(name, w, g, m, v):
    r, cc = w.shape
    tr = r if r * cc <= 128 * 1024 else _div_tile(r, 256)

    def body(w_ref, g_ref, m_ref, v_ref, d_ref, mo_ref, vo_ref, go_ref):
        gv = g_ref[...]
        go_ref[...] = gv
        mn = ADAM_B1 * m_ref[...] + (1.0 - ADAM_B1) * gv
        vn = ADAM_B2 * v_ref[...] + (1.0 - ADAM_B2) * (gv * gv)
        m_hat = mn / (1.0 - ADAM_B1 ** ADAM_STEP)
        v_hat = vn / (1.0 - ADAM_B2 ** ADAM_STEP)
        d_ref[...] = -ADAM_LR * (m_hat / (jnp.sqrt(v_hat) + ADAM_EPS) + ADAM_WD * w_ref[...])
        mo_ref[...] = mn
        vo_ref[...] = vn

    spec = pl.BlockSpec((tr, cc), lambda i: (i, 0))
    sd = jax.ShapeDtypeStruct((r, cc), F32)
    return _pallas(
        body, name=name, grid=(r // tr,), in_specs=[spec] * 4, out_specs=(spec,) * 4, out_shape=(sd,) * 4,
        compiler_params=_cp("parallel"),
    )(w, g, m, v)


_BIG = ("w_in", "w_up", "w_branch", "w_mem_kv", "w_out", "w_down")
_BIG_SHARD_SHAPE = {"w_in": (1024, 1664), "w_up": (1024, 1408), "w_branch": (1536, 256),
                    "w_mem_kv": (256, 1024), "w_out": (256, 1024), "w_down": (704, 1024)}
_SMALL_SHAPE = {"norm1_g": (1, D_MODEL), "ln_v_g": (1, GM_WIDTH), "ln_v_b": (1, GM_WIDTH),
                "w_spatial": (GM_GROUPS * GM_CHUNK, GM_CHUNK), "b_spatial": (GM_GROUPS, GM_CHUNK),
                "lb_logits": (2, HG_HEADS * HG_DIM), "hgrn_norm_g": (1, HG_DIM), "mem_norm_g": (1, D_MODEL),
                "norm2_g": (1, D_MODEL), "conv_w": (3, D_FF), "conv_b": (1, D_FF), "final_g": (1, D_MODEL)}
_SMALL_EARLY = tuple(n for n in _SMALL_SHAPE if n != "norm1_g")
_PARAM_ORDER = ("norm1_g", "w_in", "ln_v_g", "ln_v_b", "w_spatial", "b_spatial", "lb_logits", "hgrn_norm_g",
                "mem_norm_g", "w_mem_kv", "w_branch", "w_out", "norm2_g", "w_up", "conv_w", "conv_b", "w_down",
                "final_g")


def _adamw_small(ws, gs, ms, vs):
    n = len(ws)

    def body(*refs):
        w_refs, g_refs, m_refs, v_refs = refs[:n], refs[n:2 * n], refs[2 * n:3 * n], refs[3 * n:4 * n]
        d_refs, mo_refs, vo_refs = refs[4 * n:5 * n], refs[5 * n:6 * n], refs[6 * n:]
        for k in range(n):
            gv = g_refs[k][...]
            mn = ADAM_B1 * m_refs[k][...] + (1.0 - ADAM_B1) * gv
            vn = ADAM_B2 * v_refs[k][...] + (1.0 - ADAM_B2) * (gv * gv)
            m_hat = mn / (1.0 - ADAM_B1 ** ADAM_STEP)
            v_hat = vn / (1.0 - ADAM_B2 ** ADAM_STEP)
            d_refs[k][...] = -ADAM_LR * (m_hat / (jnp.sqrt(v_hat) + ADAM_EPS) + ADAM_WD * w_refs[k][...])
            mo_refs[k][...] = mn
            vo_refs[k][...] = vn

    specs = [pl.BlockSpec(a.shape, lambda i, nd=a.ndim: (0,) * nd) for a in ws]
    shapes = tuple(jax.ShapeDtypeStruct(a.shape, F32) for a in ws)
    res = _pallas(
        body, name="adamw_small", grid=(1,), in_specs=specs * 4, out_specs=tuple(specs * 3), out_shape=shapes * 3,
        compiler_params=_cp("arbitrary"),
    )(*ws, *gs, *ms, *vs)
    return res[:n], res[n:2 * n], res[2 * n:]


class _Comm:
    _ROW_SHARDED = ("w_mem_kv", "w_out", "w_down")

    def __init__(self, slabs, place):
        self.slabs, self.place = slabs, place
        self.full, self.raw, self.parts, self.landing, self.bufs, self.done = {}, {}, {}, {}, {}, {}

    def w(self, name):
        a = self.full[name]
        if name in self._ROW_SHARDED:
            return a.reshape(-1, a.shape[-1])
        if name == "conv_w":
            return jnp.transpose(a, (1, 0, 2)).reshape(3, 1, D_FF)
        return a

    def grad(self, name, arr):
        self.raw[name] = arr.reshape((N_CHIPS, -1, arr.shape[-1]))
        if name == "w_in":
            ex, deliver = self._to_sibling(["w_in"])
            deliver(_run_exchanges("rs_sibling_w_in", [ex])[0])

    def small_grads(self, arrays):
        self.small_local = list(arrays)

    def carry(self, tag, call):
        plan = self._plan(tag)
        if not plan:
            return call(())
        out, carried = call([ex for ex, _ in plan])
        for (_, deliver), res in zip(plan, carried):
            deliver(res)
        return out

    def finish(self, last_small):
        ex, deliver = self._share(["w_out", "w_branch", "w_mem_kv", "w_in"])
        shared, small = _run_exchanges("share_and_gather_last", [ex, _ex_gather_small(last_small)])
        deliver(shared)
        return self.done, self.small_local + list(last_small), self.small_everyone + small

    def _plan(self, tag):
        if tag == "norm1":
            def deliver(res):
                self.full["w_in"] = res[0]

            return [(_ex_gather_relay([self.slabs["w_in"]]), deliver)]
        if tag == "in_proj":
            return [self._gather_relay(["w_branch", "w_out", "w_mem_kv", "w_down"], 0.6), self._gather(["conv_w"])]
        if tag == "hgrn_fwd":
            return [self._gather_relay(["w_up"], 0.8)]
        if tag == "d_h2":
            return [self._to_sibling(["w_down", "w_up"])]
        if tag == "merge_bwd":
            return [self._to_owner(["w_up"], (0, 2))]
        if tag == "hgrn_bwd":
            return [self._to_owner(["w_down"]), self._to_owner(["w_up"], (1, 2)),
                    self._to_sibling(["w_out", "w_branch", "w_mem_kv"])]
        if tag == "g_w_in":
            def keep(res):
                self.small_everyone = res

            return [self._to_owner(["w_out", "w_branch", "w_mem_kv"]), (_ex_gather_small(self.small_local), keep)]
        if tag == "d_h":
            return [self._to_owner(["w_in"]), self._share(["w_down", "w_up"])]
        return []

    def _gather(self, names, part=(0, 1)):
        def deliver(res):
            self.slabs.update(zip(names, res))
            self.full.update(zip(names, res))

        return _ex_all_gather([self.slabs[n] for n in names], [n != "conv_w" for n in names], part), deliver

    def _gather_relay(self, names, mid_at):
        return _ex_gather_relay([self.slabs[n] for n in names], mid_at), lambda res: self.full.update(zip(names, res))

    def _to_sibling(self, names):
        def deliver(res):
            for n, r in zip(names, res):
                self.parts[n] = _add_half("rs_add_" + n, self.raw[n], r, self.place)

        return _ex_to_sibling([self.raw[n] for n in names]), deliver

    def _to_owner(self, names, part=(0, 1)):
        def deliver(res):
            for n, r in zip(names, res):
                if part[0] + 1 < part[1]:
                    self.landing[n] = r
                else:
                    self.bufs[n] = _sum_owner("rs_sum_" + n, self.parts[n], r, self.place)

        landing = [self.landing[n] for n in names] if part[0] else None
        return _ex_to_owner([self.parts[n] for n in names], part, landing), deliver

    def _share(self, names):
        return _ex_share_halves([self.bufs[n] for n in names]), lambda res: self.done.update(zip(names, res))


def kernel(x, mem, norm1_g, w_in, ln_v_g, ln_v_b, w_spatial, b_spatial, lb_logits, hgrn_norm_g, mem_norm_g, w_mem_kv, w_branch, w_out, norm2_g, w_up, conv_w, conv_b, w_down, final_g, loss_target, m_norm1_g, m_w_in, m_ln_v_g, m_ln_v_b, m_w_spatial, m_b_spatial, m_lb_logits, m_hgrn_norm_g, m_mem_norm_g, m_w_mem_kv, m_w_branch, m_w_out, m_norm2_g, m_w_up, m_conv_w, m_conv_b, m_w_down, m_final_g, v_norm1_g, v_w_in, v_ln_v_g, v_ln_v_b, v_w_spatial, v_b_spatial, v_lb_logits, v_hgrn_norm_g, v_mem_norm_g, v_w_mem_kv, v_w_branch, v_w_out, v_norm2_g, v_w_up, v_conv_w, v_conv_b, v_w_down, v_final_g):
    w = dict(norm1_g=norm1_g, w_in=w_in, ln_v_g=ln_v_g, ln_v_b=ln_v_b, w_spatial=w_spatial, b_spatial=b_spatial,
             lb_logits=lb_logits, hgrn_norm_g=hgrn_norm_g, mem_norm_g=mem_norm_g, w_mem_kv=w_mem_kv,
             w_branch=w_branch, w_out=w_out, norm2_g=norm2_g, w_up=w_up, conv_w=conv_w, conv_b=conv_b,
             w_down=w_down, final_g=final_g)
    mom = dict(norm1_g=m_norm1_g, w_in=m_w_in, ln_v_g=m_ln_v_g, ln_v_b=m_ln_v_b, w_spatial=m_w_spatial,
               b_spatial=m_b_spatial, lb_logits=m_lb_logits, hgrn_norm_g=m_hgrn_norm_g, mem_norm_g=m_mem_norm_g,
               w_mem_kv=m_w_mem_kv, w_branch=m_w_branch, w_out=m_w_out, norm2_g=m_norm2_g, w_up=m_w_up,
               conv_w=m_conv_w, conv_b=m_conv_b, w_down=m_w_down, final_g=m_final_g)
    var = dict(norm1_g=v_norm1_g, w_in=v_w_in, ln_v_g=v_ln_v_g, ln_v_b=v_ln_v_b, w_spatial=v_w_spatial,
               b_spatial=v_b_spatial, lb_logits=v_lb_logits, hgrn_norm_g=v_hgrn_norm_g, mem_norm_g=v_mem_norm_g,
               w_mem_kv=v_w_mem_kv, w_branch=v_w_branch, w_out=v_w_out, norm2_g=v_norm2_g, w_up=v_w_up,
               conv_w=v_conv_w, conv_b=v_conv_b, w_down=v_w_down, final_g=v_final_g)
    B, S, D = x.shape
    T = B * S
    ci = lax.axis_index("c")
    q = 2 * lax.axis_index("x") + lax.axis_index("y")
    place = jnp.stack([q, ci, 2 * q + ci]).astype(jnp.int32)

    slabs = {n: _cast_into_slab("slab_" + n, w[n].reshape(_BIG_SHARD_SHAPE[n]), place, BF16) for n in _BIG}
    slabs["conv_w"] = _cast_into_slab("slab_conv_w", conv_w[0], place, F32)
    comm = _Comm(slabs, place)
    p = dict(
        norm1_g=norm1_g, ln_v_g=ln_v_g, ln_v_b=ln_v_b, w_spatial=w_spatial[0],
        b_spatial=b_spatial.reshape(GM_GROUPS, GM_CHUNK, 1), lb_logits=lb_logits, hgrn_norm_g=hgrn_norm_g,
        mem_norm_g=mem_norm_g, norm2_g=norm2_g, conv_b=conv_b, final_g=final_g.reshape(1, D))

    loss, grad_x, g = _local_step(x.reshape(T, D), mem.reshape(B * MEM_LEN, D), loss_target.reshape(T, D), p, comm,
                                  B, S)

    shard_grads, local_small, everyone = comm.finish([g["norm1_g"]])
    summed = _sum_small(everyone, local_small, place)
    small_names = list(_SMALL_EARLY) + ["norm1_g"]
    total = dict(zip(_SMALL_EARLY, summed))
    loss_total, total["norm1_g"] = summed[len(_SMALL_EARLY)][0, 0], summed[-1]

    grads, delta, new_m, new_v = {}, {}, {}, {}
    for n in _BIG:
        shp = _BIG_SHARD_SHAPE[n]
        delta[n], new_m[n], new_v[n], grads[n] = _adamw("adamw_" + n, w[n].reshape(shp), shard_grads[n],
                                                        mom[n].reshape(shp), var[n].reshape(shp))
    cw_shard = D_FF // N_CHIPS
    total["conv_w"] = lax.dynamic_slice(total["conv_w"], (0, q * cw_shard), (3, cw_shard)).reshape(3, 1, cw_shard)

    def flat2d(d, n):
        return d[n].reshape(total[n].shape)

    upd = _adamw_small([flat2d(w, n) for n in small_names], [total[n] for n in small_names],
                       [flat2d(mom, n) for n in small_names], [flat2d(var, n) for n in small_names])
    for k, n in enumerate(small_names):
        grads[n], delta[n], new_m[n], new_v[n] = total[n], upd[0][k], upd[1][k], upd[2][k]

    def shaped(d):
        return [d[n].reshape(w[n].shape) for n in _PARAM_ORDER]

    return (loss_total, grad_x.reshape(B, S, D), *shaped(grads), *shaped(delta), *shaped(new_m), *shaped(new_v))
```

```python
import functools
import math

import jax
import jax.numpy as jnp
from jax import lax
from jax.experimental import pallas as pl
from jax.experimental.pallas import tpu as pltpu

F32 = jnp.float32
BF16 = jnp.bfloat16
EPS = 1e-6

D_MODEL = 1024
MEM_LEN = 256
GM_WIDTH = 512
GM_CHUNK = 128
GM_GROUPS = 4
HG_HEADS = 4
HG_DIM = 128
HG_CHUNK = 64
XA_HEADS = 4
XA_DIM = 128
BR_WIDTH = 512
D_FF = 2816
IN_WIDTH = 6656
N_CHIPS = 4
N_DEV = 8

ADAM_LR = 0.001
ADAM_B1 = 0.9
ADAM_B2 = 0.999
ADAM_EPS = 1e-08
ADAM_WD = 0.01
ADAM_STEP = 10

COL_ZU, COL_ZV, COL_HQ, COL_HF, COL_HI, COL_HG, COL_XQ = 0, 1, 2, 3, 4, 5, 6
COL_GATE0 = 3584

VMEM_LIMIT_BYTES = 48 * 1024 * 1024
MESH_ID = pl.DeviceIdType.MESH


def _cp(*sem):
    return pltpu.CompilerParams(dimension_semantics=sem, vmem_limit_bytes=VMEM_LIMIT_BYTES)


def _pallas(body, *, out_shape, **kw):
    def pin(s):
        return pltpu.HBM(s.shape, s.dtype) if isinstance(s, jax.ShapeDtypeStruct) else s

    out_shape = tuple(pin(s) for s in out_shape) if isinstance(out_shape, (tuple, list)) else pin(out_shape)
    call = pl.pallas_call(body, out_shape=out_shape, **kw)

    def run(*operands):
        return call(*[pltpu.with_memory_space_constraint(o, pltpu.HBM) if jnp.issubdtype(o.dtype, jnp.floating)
                      else o for o in operands])

    return run


def _dot(a, b):
    return lax.dot_general(a.astype(BF16), b.astype(BF16), (((1,), (0,)), ((), ())), preferred_element_type=F32)


def _dot_nt(a, b):
    return lax.dot_general(a.astype(BF16), b.astype(BF16), (((1,), (1,)), ((), ())), preferred_element_type=F32)


def _dot_tn(a, b):
    return lax.dot_general(a.astype(BF16), b.astype(BF16), (((0,), (0,)), ((), ())), preferred_element_type=F32)


def _dot_01(mask01, x):
    hi = x.astype(BF16)
    r1 = x - hi.astype(F32)
    mid = r1.astype(BF16)
    lo = (r1 - mid.astype(F32)).astype(BF16)
    m = mask01.astype(BF16)
    dn = (((1,), (0,)), ((), ()))
    return (lax.dot_general(m, hi, dn, preferred_element_type=F32)
            + lax.dot_general(m, mid, dn, preferred_element_type=F32)
            + lax.dot_general(m, lo, dn, preferred_element_type=F32))


def _sigmoid(z):
    return 1.0 / (1.0 + jnp.exp(-z))


_GELU_C = math.sqrt(2.0 / math.pi)


def _gelu_and_grad(z):
    inner = _GELU_C * (z + 0.044715 * z * z * z)
    t = jnp.tanh(inner)
    val = 0.5 * z * (1.0 + t)
    grad = 0.5 * (1.0 + t) + 0.5 * z * (1.0 - t * t) * _GELU_C * (1.0 + 3.0 * 0.044715 * z * z)
    return val, grad


def _row_tile(n, want=512):
    t = min(want, n)
    assert n % t == 0
    return t


def _pcall(body, operands, *, name, grid, in_specs, out_specs, out_shape, scratch_shapes=(), semantics, riders=(),
           prefetch=None):
    single = not isinstance(out_shape, (tuple, list))
    out_specs = (out_specs,) if single else tuple(out_specs)
    out_shape = (out_shape,) if single else tuple(out_shape)
    n_pre = 0 if prefetch is None else 1

    def call(fn, ins_, outs_, shapes_, scr_, ops, sem, aliases):
        if prefetch is None:
            return _pallas(fn, name=name, grid=grid, in_specs=ins_, out_specs=outs_, out_shape=shapes_,
                           scratch_shapes=scr_, input_output_aliases=aliases, compiler_params=_cp(*sem))(*ops)
        spec = pltpu.PrefetchScalarGridSpec(num_scalar_prefetch=1, grid=grid, in_specs=ins_, out_specs=outs_,
                                            scratch_shapes=scr_)
        return _pallas(fn, name=name, grid_spec=spec, out_shape=shapes_, input_output_aliases=aliases,
                       compiler_params=_cp(*sem))(prefetch, *ops)

    if not riders:
        res = call(body, list(in_specs), out_specs, out_shape, list(scratch_shapes), operands, semantics, {})
        return (res[0] if single else res), []
    n_in, n_out, n_scr = len(in_specs), len(out_shape), len(scratch_shapes)
    ex_in = [len(ex.operands) for ex in riders]
    ex_out = [len(ex.out_shape) for ex in riders]
    ex_scr = [len(ex.scratch) for ex in riders]
    tot_in, tot_out = n_in + sum(ex_in), n_out + sum(ex_out)

    def wrapped(*refs):
        pre, refs = refs[:n_pre], refs[n_pre:]
        ins, outs, scr = refs[:tot_in], refs[tot_in:tot_in + tot_out], refs[tot_in + tot_out:]
        ids = [pl.program_id(d) for d in range(len(grid))]
        first = functools.reduce(lambda p, t: p & t, [i == 0 for i in ids])
        last = functools.reduce(lambda p, t: p & t, [i == n - 1 for i, n in zip(ids, grid)])
        parts, oi, oo, os_ = [], n_in, n_out, n_scr
        for k in range(len(riders)):
            parts.append((ins[oi:oi + ex_in[k]], outs[oo:oo + ex_out[k]], scr[os_:os_ + ex_scr[k]]))
            oi, oo, os_ = oi + ex_in[k], oo + ex_out[k], os_ + ex_scr[k]

        @pl.when(first)
        def _():
            for ex, part in zip(riders, parts):
                ex.start(*part)

        step, total = 0, 1
        for i, n in zip(ids, grid):
            step, total = step * n + i, total * n
        for ex, part in zip(riders, parts):
            if ex.mid is not None:
                @pl.when(step == min(total - 1, int(total * ex.mid_at)))
                def _(ex=ex, part=part):
                    ex.mid(*part)

        body(*pre, *ins[:n_in], *outs[:n_out], *scr[:n_scr])

        @pl.when(last)
        def _():
            for ex, part in zip(riders, parts):
                ex.finish(*part)

    aliases, oi, oo = {}, n_in, n_out
    all_ops, all_shapes, all_scr = list(operands), list(out_shape), list(scratch_shapes)
    for k, ex in enumerate(riders):
        aliases.update({n_pre + oi + a: oo + b for a, b in ex.aliases.items()})
        oi, oo = oi + ex_in[k], oo + ex_out[k]
        all_ops += list(ex.operands)
        all_shapes += [pltpu.HBM(s.shape, s.dtype) for s in ex.out_shape]
        all_scr += list(ex.scratch)
    res = call(wrapped, list(in_specs) + [HBM_SPEC] * sum(ex_in), out_specs + (HBM_SPEC,) * sum(ex_out),
               tuple(all_shapes), all_scr, all_ops, ["arbitrary"] * len(grid), aliases)
    own = res[0] if single else tuple(res[:n_out])
    carried, oo = [], n_out
    for k in range(len(riders)):
        carried.append(list(res[oo:oo + ex_out[k]]))
        oo += ex_out[k]
    return own, carried


def _carried(out, carried, riders):
    return (out, carried) if riders else out


def _matmul(name, operands, *, grid, in_specs, o_spec, out_shape, out_dtype, dims, riders=()):
    nk = grid[2]
    assert nk == 1 or out_dtype == F32

    def body(a_ref, b_ref, o_ref):
        part = lax.dot_general(a_ref[...].astype(BF16), b_ref[...].astype(BF16), (dims, ((), ())),
                               preferred_element_type=F32)
        if nk == 1:
            o_ref[...] = part.astype(o_ref.dtype)
        else:
            k = pl.program_id(2)

            @pl.when(k == 0)
            def _():
                o_ref[...] = part

            @pl.when(k > 0)
            def _():
                o_ref[...] += part

    out, carried = _pcall(body, operands, name=name, grid=grid, in_specs=in_specs, out_specs=o_spec,
                          out_shape=jax.ShapeDtypeStruct(out_shape, out_dtype),
                          semantics=("parallel", "parallel", "arbitrary"), riders=riders)
    return (out, carried) if riders else out


NN = ((1,), (0,))
NT = ((1,), (1,))
TN = ((0,), (0,))
_TN_TOKENS = 4096


def _mm_cs(name, a, w, out_dtype, riders=()):
    M, K = a.shape
    nq, _, wd = w.shape
    tm = _row_tile(M)
    return _matmul(name, (a, w), grid=(nq, M // tm, 1),
                   in_specs=[pl.BlockSpec((tm, K), lambda j, i, k: (i, 0)),
                             pl.BlockSpec((None, K, wd), lambda j, i, k: (j, 0, 0))],
                   o_spec=pl.BlockSpec((tm, wd), lambda j, i, k: (i, j)),
                   out_shape=(M, nq * wd), out_dtype=out_dtype, dims=NN, riders=riders)


def _mm_rs(name, a, w, out_dtype):
    M, K = a.shape
    N = w.shape[1]
    tm = _row_tile(M)
    return _matmul(name, (a, w), grid=(M // tm, 1, 1),
                   in_specs=[pl.BlockSpec((tm, K), lambda i, j, k: (i, 0)), pl.BlockSpec((K, N), lambda i, j, k: (0, 0))],
                   o_spec=pl.BlockSpec((tm, N), lambda i, j, k: (i, 0)),
                   out_shape=(M, N), out_dtype=out_dtype, dims=NN)


def _mm_nt_rs(name, g, w, out_dtype, riders=()):
    M, N = g.shape
    K = w.shape[0]
    to = K
    tm = _row_tile(M)
    return _matmul(name, (g, w), grid=(M // tm, K // to, 1),
                   in_specs=[pl.BlockSpec((tm, N), lambda i, j, k: (i, 0)),
                             pl.BlockSpec((to, N), lambda i, j, k: (j, 0))],
                   o_spec=pl.BlockSpec((tm, to), lambda i, j, k: (i, j)),
                   out_shape=(M, K), out_dtype=out_dtype, dims=NT, riders=riders)


def _mm_nt_cs(name, g, w, out_dtype, riders=(), stacked=False, norm_bwd=None):
    M = g.shape[-2]
    nq, K, wd = w.shape
    tm = _row_tile(M, 256)

    def product(g_ref, w_ref):
        acc = None
        for q in range(nq):
            gq = g_ref[q // 2, :, (q % 2) * wd:(q % 2 + 1) * wd] if stacked else g_ref[:, q * wd:(q + 1) * wd]
            part = _dot_nt(gq, w_ref[q])
            acc = part if acc is None else acc + part
        return acc

    def body(g_ref, w_ref, o_ref):
        o_ref[...] = product(g_ref, w_ref).astype(o_ref.dtype)

    def body_norm(g_ref, w_ref, x_ref, gain_ref, dr_ref, dx_ref, dg_ref):
        @pl.when(pl.program_id(0) == 0)
        def _():
            dg_ref[...] = jnp.zeros_like(dg_ref)

        dx, dg = _rms_bwd_rows(x_ref[...], gain_ref[...], product(g_ref, w_ref))
        dg_ref[...] += dg
        dx_ref[...] = dx + dr_ref[...]

    g_spec = (pl.BlockSpec((2, tm, 2 * wd), lambda i: (0, i, 0)) if stacked
              else pl.BlockSpec((tm, nq * wd), lambda i: (i, 0)))
    w_spec = pl.BlockSpec((nq, K, wd), lambda i: (0, 0, 0))
    row = pl.BlockSpec((tm, K), lambda i: (i, 0))
    if norm_bwd is None:
        return _carried(*_pcall(
            body, (g, w), name=name, grid=(M // tm,), in_specs=[g_spec, w_spec], out_specs=row,
            out_shape=jax.ShapeDtypeStruct((M, K), out_dtype), semantics=("parallel",), riders=riders), riders)
    vec = pl.BlockSpec((1, K), lambda i: (0, 0))
    return _carried(*_pcall(
        body_norm, (g, w) + tuple(norm_bwd), name=name, grid=(M // tm,),
        in_specs=[g_spec, w_spec, row, vec, row], out_specs=(row, vec),
        out_shape=(jax.ShapeDtypeStruct((M, K), F32), jax.ShapeDtypeStruct((1, K), F32)),
        semantics=("arbitrary",), riders=riders), riders)


def _mm_tn_rs(name, a, g, to, tn=512):
    T, M = a.shape
    N = g.shape[1]
    tt = _row_tile(T, _TN_TOKENS)
    tn = min(tn, N)
    return _matmul(name, (a, g), grid=(M // to, N // tn, T // tt),
                   in_specs=[pl.BlockSpec((tt, to), lambda i, j, k: (k, i)),
                             pl.BlockSpec((tt, tn), lambda i, j, k: (k, j))],
                   o_spec=pl.BlockSpec((to, tn), lambda i, j, k: (i, j)),
                   out_shape=(M, N), out_dtype=F32, dims=TN)


def _mm_tn_cs(name, a, g, nq, to, riders=(), stacked=False):
    T, M = a.shape
    wd = g.shape[-1] * (2 if stacked else 1) // nq
    tt = _row_tile(T, _TN_TOKENS)
    g_spec = (pl.BlockSpec((None, tt, wd), lambda i, j, k: (j // 2, k, j % 2)) if stacked
              else pl.BlockSpec((tt, wd), lambda i, j, k: (k, j)))
    return _matmul(name, (a, g), grid=(M // to, nq, T // tt),
                   in_specs=[pl.BlockSpec((tt, to), lambda i, j, k: (k, i)), g_spec],
                   o_spec=pl.BlockSpec((None, to, wd), lambda i, j, k: (j, i, 0)),
                   out_shape=(nq, M, wd), out_dtype=F32, dims=TN, riders=riders)


def _rms_fwd(name, x, g, riders=()):
    T, D = x.shape
    tm = _row_tile(T)

    def body(x_ref, g_ref, o_ref):
        o_ref[...] = _rms_rows(x_ref[...], g_ref[...]).astype(o_ref.dtype)

    return _carried(*_pcall(
        body, (x, g), name=name, grid=(T // tm,),
        in_specs=[pl.BlockSpec((tm, D), lambda i: (i, 0)), pl.BlockSpec((1, D), lambda i: (0, 0))],
        out_specs=pl.BlockSpec((tm, D), lambda i: (i, 0)),
        out_shape=jax.ShapeDtypeStruct((T, D), BF16), semantics=("parallel",), riders=riders), riders)


_NORM1_STEPS = 4


def _norm1_and_casts(x, g, shards, place, riders=()):
    T, D = x.shape
    tm = T // _NORM1_STEPS
    names = list(shards)

    def body(s_ref, x_ref, g_ref, *refs):
        w_refs, o_ref, slab_refs = refs[:len(names)], refs[len(names)], refs[len(names) + 1:]
        o_ref[...] = _rms_rows(x_ref[...], g_ref[...]).astype(o_ref.dtype)
        for w_ref, slab_ref in zip(w_refs, slab_refs):
            slab_ref[...] = w_ref[...].astype(slab_ref.dtype)

    in_specs = [pl.BlockSpec((tm, D), lambda i, s: (i, 0)), pl.BlockSpec((1, D), lambda i, s: (0, 0))]
    out_specs = [pl.BlockSpec((tm, D), lambda i, s: (i, 0))]
    out_shape = [jax.ShapeDtypeStruct((T, D), BF16)]
    for n in names:
        r, cc = shards[n].shape
        assert r % (_NORM1_STEPS * 16) == 0
        in_specs.append(pl.BlockSpec((r // _NORM1_STEPS, cc), lambda i, s: (i, 0)))
        out_specs.append(pl.BlockSpec((None, r // _NORM1_STEPS, cc), lambda i, s: (s[0], i, 0)))
        out_shape.append(jax.ShapeDtypeStruct((N_CHIPS, r, cc), BF16))
    out, carried = _pcall(body, (x, g, *[shards[n] for n in names]), name="norm1", grid=(_NORM1_STEPS,),
                          in_specs=in_specs, out_specs=out_specs, out_shape=out_shape, semantics=("parallel",),
                          riders=riders, prefetch=place)
    return _carried((out[0], dict(zip(names, out[1:]))), carried, riders)


def _rms_rows(xv, gain):
    return xv * lax.rsqrt(jnp.mean(xv * xv, axis=-1, keepdims=True) + EPS) * gain


def _rms_bwd_rows(xv, gain, dh):
    r = lax.rsqrt(jnp.mean(xv * xv, axis=-1, keepdims=True) + EPS)
    n = xv * r
    dn = dh * gain
    return r * (dn - n * jnp.mean(dn * n, axis=-1, keepdims=True)), jnp.sum(dh * n, axis=0, keepdims=True)


def _rms_bwd(name, x, g, dh, dres):
    T, D = x.shape
    tm = _row_tile(T)
    has_res = dres is not None

    def body(*refs):
        if has_res:
            x_ref, g_ref, dh_ref, dr_ref, dx_ref, dg_ref = refs
        else:
            x_ref, g_ref, dh_ref, dx_ref, dg_ref = refs

        @pl.when(pl.program_id(0) == 0)
        def _():
            dg_ref[...] = jnp.zeros_like(dg_ref)

        dx, dg = _rms_bwd_rows(x_ref[...], g_ref[...], dh_ref[...])
        dg_ref[...] += dg
        if has_res:
            dx = dx + dr_ref[...]
        dx_ref[...] = dx

    row = pl.BlockSpec((tm, D), lambda i: (i, 0))
    vec = pl.BlockSpec((1, D), lambda i: (0, 0))
    ops = (x, g, dh, dres) if has_res else (x, g, dh)
    return _pallas(
        body, name=name, grid=(T // tm,), in_specs=[row, vec, row] + ([row] if has_res else []),
        out_specs=(row, vec),
        out_shape=(jax.ShapeDtypeStruct((T, D), F32), jax.ShapeDtypeStruct((1, D), F32)),
        compiler_params=_cp("arbitrary"),
    )(*ops)


def _proj_res_norm(name, a, w, res, gain):
    M, K = a.shape
    N = w.shape[1]
    tm = _row_tile(M)

    def body(a_ref, w_ref, r_ref, g_ref, x_ref, h_ref):
        xv = _dot(a_ref[...], w_ref[...]) + r_ref[...]
        x_ref[...] = xv
        h_ref[...] = _rms_rows(xv, g_ref[...]).astype(h_ref.dtype)

    row = pl.BlockSpec((tm, N), lambda i: (i, 0))
    return _pallas(
        body, name=name, grid=(M // tm,),
        in_specs=[pl.BlockSpec((tm, K), lambda i: (i, 0)), pl.BlockSpec((K, N), lambda i: (0, 0)), row,
                  pl.BlockSpec((1, N), lambda i: (0, 0))],
        out_specs=(row, row), out_shape=(jax.ShapeDtypeStruct((M, N), F32), jax.ShapeDtypeStruct((M, N), BF16)),
        compiler_params=_cp("parallel"),
    )(a, w, res, gain)


def _proj_res_loss(name, a, w, res, tgt, gain):
    M, K = a.shape
    D = w.shape[1]
    tm = _row_tile(M)

    def body(a_ref, w_ref, r_ref, t_ref, g_ref, dx_ref, dg_ref, loss_ref):
        @pl.when(pl.program_id(0) == 0)
        def _():
            dg_ref[...] = jnp.zeros_like(dg_ref)
            loss_ref[...] = jnp.zeros_like(loss_ref)

        xv = _dot(a_ref[...], w_ref[...]) + r_ref[...]
        gv = g_ref[...]
        diff = _rms_rows(xv, gv) - t_ref[...]
        loss_ref[...] += 0.5 * jnp.sum(jnp.mean(diff * diff, axis=-1, keepdims=True))
        dx, dg = _rms_bwd_rows(xv, gv, diff * (1.0 / D))
        dg_ref[...] += dg
        dx_ref[...] = dx

    row = pl.BlockSpec((tm, D), lambda i: (i, 0))
    vec = pl.BlockSpec((1, D), lambda i: (0, 0))
    return _pallas(
        body, name=name, grid=(M // tm,),
        in_specs=[pl.BlockSpec((tm, K), lambda i: (i, 0)), pl.BlockSpec((K, D), lambda i: (0, 0)), row, row, vec],
        out_specs=(row, vec, pl.BlockSpec((8, 128), lambda i: (0, 0))),
        out_shape=(jax.ShapeDtypeStruct((M, D), F32), jax.ShapeDtypeStruct((1, D), F32),
                   jax.ShapeDtypeStruct((8, 128), F32)),
        compiler_params=_cp("arbitrary"),
    )(a, w, res, tgt, gain)


def _gmlp_pieces(zu, zv, lng, lnb, ws_ref, bs_ref):
    u, du = _gelu_and_grad(zu)
    v, dv = _gelu_and_grad(zv)
    mu = jnp.mean(v, axis=-1, keepdims=True)
    vc = v - mu
    rstd = lax.rsqrt(jnp.mean(vc * vc, axis=-1, keepdims=True) + EPS)
    vhat = vc * rstd
    vn = vhat * lng + lnb
    row = lax.broadcasted_iota(jnp.int32, (GM_CHUNK, GM_CHUNK), 0)
    col = lax.broadcasted_iota(jnp.int32, (GM_CHUNK, GM_CHUNK), 1)
    tril = row >= col
    wms, mixed = [], []
    for g in range(GM_GROUPS):
        sl = slice(g * 128, (g + 1) * 128)
        wm = jnp.where(tril, ws_ref[g], 0.0)
        wms.append(wm)
        mixed.append(_dot(wm, vn[:, sl]) + bs_ref[g])
    return u, du, dv, rstd, vhat, vn, wms, mixed, tril


def _gmlp_fwd(proj, lng, lnb, ws, bs_col):
    T = proj.shape[0]
    n = T // GM_CHUNK

    def body(zu_ref, zv_ref, lng_ref, lnb_ref, ws_ref, bs_ref, o_ref):
        u, _, _, _, _, _, _, mixed, _ = _gmlp_pieces(zu_ref[...].astype(F32), zv_ref[...].astype(F32),
                                                     lng_ref[...], lnb_ref[...],
                                                     ws_ref, bs_ref)
        for g in range(GM_GROUPS):
            sl = slice(g * 128, (g + 1) * 128)
            o_ref[:, sl] = (u[:, sl] * mixed[g]).astype(o_ref.dtype)

    vec = pl.BlockSpec((1, GM_WIDTH), lambda i: (0, 0))
    return _pallas(
        body, name="gmlp_fwd", grid=(n,),
        in_specs=[pl.BlockSpec((GM_CHUNK, 512), lambda i: (i, COL_ZU)),
                  pl.BlockSpec((GM_CHUNK, 512), lambda i: (i, COL_ZV)),
                  vec, vec,
                  pl.BlockSpec((GM_GROUPS, 128, 128), lambda i: (0, 0, 0)),
                  pl.BlockSpec((GM_GROUPS, 128, 1), lambda i: (0, 0, 0))],
        out_specs=pl.BlockSpec((GM_CHUNK, 512), lambda i: (i, 0)),
        out_shape=jax.ShapeDtypeStruct((T, GM_WIDTH), BF16), compiler_params=_cp("parallel"),
    )(proj, proj, lng, lnb, ws, bs_col)


def _gmlp_bwd(proj, d_out, lng, lnb, ws, bs_col, riders=()):
    T = proj.shape[0]
    n = T // GM_CHUNK

    def body(zu_ref, zv_ref, do_ref, lng_ref, lnb_ref, ws_ref, bs_ref,
             dz_ref, dws_ref, dbs_ref, dlng_ref, dlnb_ref, dm_acc):
        i = pl.program_id(0)

        @pl.when(i == 0)
        def _():
            dws_ref[...] = jnp.zeros_like(dws_ref)
            dlng_ref[...] = jnp.zeros_like(dlng_ref)
            dlnb_ref[...] = jnp.zeros_like(dlnb_ref)
            dm_acc[...] = jnp.zeros_like(dm_acc)

        lng_v = lng_ref[...]
        u, du, dv, rstd, vhat, vn, wms, mixed, tril = _gmlp_pieces(zu_ref[...].astype(F32), zv_ref[...].astype(F32),
                                                                  lng_v, lnb_ref[...],
                                                                  ws_ref, bs_ref)
        do = do_ref[...]
        dvn_parts = []
        for g in range(GM_GROUPS):
            sl = slice(g * 128, (g + 1) * 128)
            dog = do[:, sl]
            dz_ref[:, sl] = (dog * mixed[g] * du[:, sl]).astype(dz_ref.dtype)
            dmix = dog * u[:, sl]
            dm_acc[:, sl] += dmix
            dws_ref[g] += jnp.where(tril, _dot_nt(dmix, vn[:, sl]), 0.0)
            dvn_parts.append(_dot_tn(wms[g], dmix))
        dvn = jnp.concatenate(dvn_parts, axis=1)
        dlng_ref[...] += jnp.sum(dvn * vhat, axis=0, keepdims=True)
        dlnb_ref[...] += jnp.sum(dvn, axis=0, keepdims=True)
        dvh = dvn * lng_v
        dvv = rstd * (dvh - jnp.mean(dvh, axis=-1, keepdims=True)
                      - vhat * jnp.mean(dvh * vhat, axis=-1, keepdims=True))
        dz_ref[:, GM_WIDTH:] = (dvv * dv).astype(dz_ref.dtype)

        @pl.when(i == n - 1)
        def _():
            for g in range(GM_GROUPS):
                dbs_ref[g] = jnp.sum(dm_acc[:, g * 128:(g + 1) * 128], axis=1, keepdims=True)

    vec = pl.BlockSpec((1, GM_WIDTH), lambda i: (0, 0))
    wsp = pl.BlockSpec((GM_GROUPS, 128, 128), lambda i: (0, 0, 0))
    bsp = pl.BlockSpec((GM_GROUPS, 128, 1), lambda i: (0, 0, 0))
    return _carried(*_pcall(
        body, (proj, proj, d_out, lng, lnb, ws, bs_col), name="gmlp_bwd", grid=(n,),
        in_specs=[pl.BlockSpec((GM_CHUNK, 512), lambda i: (i, COL_ZU)),
                  pl.BlockSpec((GM_CHUNK, 512), lambda i: (i, COL_ZV)),
                  pl.BlockSpec((None, GM_CHUNK, 512), lambda i: (0, i, 0)), vec, vec, wsp, bsp],
        out_specs=(pl.BlockSpec((GM_CHUNK, 2 * GM_WIDTH), lambda i: (i, 0)), wsp, bsp, vec, vec),
        out_shape=(jax.ShapeDtypeStruct((T, 2 * GM_WIDTH), BF16),
                   jax.ShapeDtypeStruct((GM_GROUPS, 128, 128), F32), jax.ShapeDtypeStruct((GM_GROUPS, 128, 1), F32),
                   jax.ShapeDtypeStruct((1, GM_WIDTH), F32), jax.ShapeDtypeStruct((1, GM_WIDTH), F32)),
        scratch_shapes=[pltpu.VMEM((GM_CHUNK, GM_WIDTH), F32)],
        semantics=("arbitrary",), riders=riders), riders)


def _hgrn_lower_bound(lbl):
    return 1.0 / (1.0 + jnp.exp(lbl[1:2, :] - lbl[0:1, :]))


def _hgrn_gates(hq, hf, lb):
    C = HG_CHUNK
    sg = _sigmoid(hf)
    fg = lb + (1.0 - lb) * sg
    sq = _sigmoid(hq)
    row = lax.broadcasted_iota(jnp.int32, (C, C), 0)
    col = lax.broadcasted_iota(jnp.int32, (C, C), 1)
    tril = row >= col
    logf = jnp.log(fg)
    a = _dot_01(tril, logf)
    a_last = jnp.sum(logf, axis=0, keepdims=True)
    first_half = lax.broadcasted_iota(jnp.int32, logf.shape, 0) < (C // 2)
    a_mid = jnp.sum(jnp.where(first_half, logf, 0.0), axis=0, keepdims=True)
    ea, ei, eki, ekl = jnp.exp(a), jnp.exp(a - a_mid), jnp.exp(a_mid - a), jnp.exp(a_last - a)
    k = 1.0 - fg
    q = hq * sq
    qi = (q * ei).astype(BF16).astype(F32)
    ki = (k * eki).astype(BF16).astype(F32)
    return dict(sg=sg, fg=fg, sq=sq, tril=tril, ea=ea, ei=ei, eki=eki, ekl=ekl, e_last=jnp.exp(a_last),
                qe=q * ea, qi=qi, ki=ki, kl=k * ekl)


def _heads(x):
    return [x[:, h * HG_DIM:(h + 1) * HG_DIM] for h in range(HG_HEADS)]


def _hgrn_fwd(proj, lbl, gh, B, S, riders=()):
    C = HG_CHUNK
    NC = S // C
    W = HG_HEADS * HG_DIM

    def body(q_ref, f_ref, i_ref, g_ref, lbl_ref, gh_ref, o_ref, bo_ref, st_ref, state):
        @pl.when(pl.program_id(0) == 0)
        def _():
            state[...] = jnp.zeros_like(state)

        lb = _hgrn_lower_bound(lbl_ref[...])
        ghv = gh_ref[...]
        for b in range(B):
            gt = _hgrn_gates(q_ref[b].astype(F32), f_ref[b].astype(F32), lb)
            v = _heads(i_ref[b])
            qe, qi, ki, kl, e_last = (_heads(gt[n]) for n in ("qe", "qi", "ki", "kl", "e_last"))
            outs, normed = [], []
            for h in range(HG_HEADS):
                p = jnp.where(gt["tril"], _dot_nt(qi[h], ki[h]), 0.0)
                st = state[b, h]
                st_ref[b, h] = st
                o = _dot_nt(qe[h], st) + _dot(p, v[h])
                state[b, h] = st * e_last[h] + _dot_tn(v[h], kl[h])
                outs.append(o)
                normed.append(o * lax.rsqrt(jnp.mean(o * o, axis=-1, keepdims=True) + EPS) * ghv)
            o_ref[b] = jnp.concatenate(outs, axis=1)
            hg = g_ref[b].astype(F32)
            bo_ref[b] = (jnp.concatenate(normed, axis=1) * (hg * _sigmoid(hg))).astype(bo_ref.dtype)

    def col(cb):
        return pl.BlockSpec((B, C, 512), lambda c: (0, c, cb))

    tile = pl.BlockSpec((B, C, W), lambda c: (0, c, 0))
    proj3 = proj.reshape(B, S, proj.shape[-1])
    out, carried = _pcall(
        body, (proj3, proj3, proj3, proj3, lbl, gh), name="hgrn_fwd", grid=(NC,),
        in_specs=[col(COL_HQ), col(COL_HF), col(COL_HI), col(COL_HG),
                  pl.BlockSpec((2, W), lambda c: (0, 0)), pl.BlockSpec((1, HG_DIM), lambda c: (0, 0))],
        out_specs=(tile, tile, pl.BlockSpec((B, None, HG_HEADS, 128, 128), lambda c: (0, c, 0, 0, 0))),
        out_shape=(jax.ShapeDtypeStruct((B, S, W), F32), jax.ShapeDtypeStruct((B, S, W), BF16),
                   jax.ShapeDtypeStruct((B, NC, HG_HEADS, 128, 128), F32)),
        scratch_shapes=[pltpu.VMEM((B, HG_HEADS, 128, 128), F32)],
        semantics=("arbitrary",), riders=riders)
    o_h, b_out, states = out
    out = (o_h, b_out.reshape(B * S, W), states)
    return (out, carried) if riders else out


def _hgrn_bwd(proj, o_saved, states, d_out, lbl, gh, others, B, S, riders=()):
    C = HG_CHUNK
    NC = S // C
    W = HG_HEADS * HG_DIM
    d_gm, d_xq, d_gates = (t.reshape(B, S, t.shape[-1]) for t in others)
    own0 = d_gm.shape[-1]
    xq0 = own0 + 4 * W
    gates0 = xq0 + d_xq.shape[-1]

    def body(q_ref, f_ref, i_ref, g_ref, o_ref, st_ref, do_ref, lbl_ref, gh_ref, gm_ref, xq_ref, gates_ref,
             d_ref, dlbl_ref, dgh_ref, dstate, dlb_acc):
        c = pl.program_id(0)
        d_ref[:, :, :own0] = gm_ref[...]
        d_ref[:, :, xq0:gates0] = xq_ref[...]
        d_ref[:, :, gates0:] = gates_ref[...]

        def put(b, k, val):
            d_ref[b, :, own0 + k * W:own0 + (k + 1) * W] = val.astype(d_ref.dtype)

        @pl.when(c == 0)
        def _():
            dstate[...] = jnp.zeros_like(dstate)
            dgh_ref[...] = jnp.zeros_like(dgh_ref)
            dlb_acc[...] = jnp.zeros_like(dlb_acc)

        lb = _hgrn_lower_bound(lbl_ref[...])
        ghv = gh_ref[...]
        row = lax.broadcasted_iota(jnp.int32, (C, C), 0)
        colm = lax.broadcasted_iota(jnp.int32, (C, C), 1)
        triu = colm >= row
        for b in range(B):
            hq, hg = q_ref[b].astype(F32), g_ref[b].astype(F32)
            gt = _hgrn_gates(hq, f_ref[b].astype(F32), lb)
            tril = gt["tril"]
            v = _heads(i_ref[b])
            qe, qi, ki, kl, e_last = (_heads(gt[n]) for n in ("qe", "qi", "ki", "kl", "e_last"))
            sgg = _sigmoid(hg)
            don_all = do_ref[b] * (hg * sgg)
            o, don = _heads(o_ref[b]), _heads(don_all)
            d_qe, d_qi, d_ki, d_kl, dv, n_all, dal = [], [], [], [], [], [], []
            for h in range(HG_HEADS):
                r = lax.rsqrt(jnp.mean(o[h] * o[h], axis=-1, keepdims=True) + EPS)
                n = o[h] * r
                n_all.append(n)
                dgh_ref[...] += jnp.sum(don[h] * n, axis=0, keepdims=True)
                dn = don[h] * ghv
                d_o = r * (dn - n * jnp.mean(dn * n, axis=-1, keepdims=True))
                st, dst = st_ref[b, h], dstate[b, h]
                p = jnp.where(tril, _dot_nt(qi[h], ki[h]), 0.0)
                dp = jnp.where(tril, _dot_nt(d_o, v[h]), 0.0)
                d_qe.append(_dot(d_o, st))
                d_qi.append(_dot(dp, ki[h]))
                d_ki.append(_dot_tn(dp, qi[h]))
                d_kl.append(_dot(v[h], dst))
                dv.append(_dot_tn(p, d_o) + _dot_nt(kl[h], dst))
                dstate[b, h] = dst * e_last[h] + _dot_tn(d_o, qe[h])
                dal.append(jnp.sum(dst * st, axis=0, keepdims=True) * e_last[h])
            d_qe, d_qi, d_ki, d_kl, n_all, dal = (jnp.concatenate(t, axis=1)
                                                  for t in (d_qe, d_qi, d_ki, d_kl, n_all, dal))
            put(b, 3, do_ref[b] * n_all * jnp.tile(ghv, (1, HG_HEADS)) * (sgg * (1.0 + hg * (1.0 - sgg))))
            put(b, 2, jnp.concatenate(dv, axis=1))
            d_a_last = dal + jnp.sum(d_kl * gt["kl"], axis=0, keepdims=True)
            dq = d_qe * gt["ea"] + d_qi * gt["ei"]
            dk = d_ki * gt["eki"] + d_kl * gt["ekl"]
            da = d_qe * gt["qe"] + d_qi * gt["qi"] - d_ki * gt["ki"] - d_kl * gt["kl"]
            dlogf = _dot_01(triu, da) + d_a_last
            sg, sq = gt["sg"], gt["sq"]
            dfg = dlogf / gt["fg"] - dk
            put(b, 1, dfg * (1.0 - lb) * sg * (1.0 - sg))
            dlb_acc[...] += jnp.sum(dfg * (1.0 - sg), axis=0, keepdims=True)
            put(b, 0, dq * (sq * (1.0 + hq * (1.0 - sq))))

        @pl.when(c == NC - 1)
        def _():
            dlb = dlb_acc[...]
            first = lax.broadcasted_iota(jnp.int32, (2, W), 0) == 0
            dlbl_ref[...] = jnp.where(first, dlb * lb * (1.0 - lb), -dlb * lb * (1.0 - lb))

    def col(cb):
        return pl.BlockSpec((B, C, 512), lambda c: (0, NC - 1 - c, cb))

    tile = pl.BlockSpec((B, C, W), lambda c: (0, NC - 1 - c, 0))
    proj3 = proj.reshape(B, S, proj.shape[-1])

    def rows(width):
        return pl.BlockSpec((B, C, width), lambda c: (0, NC - 1 - c, 0))

    width = proj.shape[-1]
    out, carried = _pcall(
        body, (proj3, proj3, proj3, proj3, o_saved, states, d_out.reshape(3, B, S, W), lbl, gh, d_gm, d_xq, d_gates),
        name="hgrn_bwd", grid=(NC,),
        in_specs=[col(COL_HQ), col(COL_HF), col(COL_HI), col(COL_HG), tile,
                  pl.BlockSpec((B, None, HG_HEADS, 128, 128), lambda c: (0, NC - 1 - c, 0, 0, 0)),
                  pl.BlockSpec((None, B, C, W), lambda c: (1, 0, NC - 1 - c, 0)),
                  pl.BlockSpec((2, W), lambda c: (0, 0)), pl.BlockSpec((1, HG_DIM), lambda c: (0, 0)),
                  rows(d_gm.shape[-1]), rows(d_xq.shape[-1]), rows(d_gates.shape[-1])],
        out_specs=(rows(width), pl.BlockSpec((2, W), lambda c: (0, 0)), pl.BlockSpec((1, HG_DIM), lambda c: (0, 0))),
        out_shape=(jax.ShapeDtypeStruct((B, S, width), BF16), jax.ShapeDtypeStruct((2, W), F32),
                   jax.ShapeDtypeStruct((1, HG_DIM), F32)),
        scratch_shapes=[pltpu.VMEM((B, HG_HEADS, 128, 128), F32), pltpu.VMEM((1, W), F32)],
        semantics=("arbitrary",), riders=riders)
    out = (out[0].reshape(B * S, width),) + tuple(out[1:])
    return (out, carried) if riders else out


_XA_SCALE = XA_DIM ** -0.5


def _attn_probs(qh, kh):
    s = _dot_nt(qh, kh) * _XA_SCALE
    e = jnp.exp(s - jnp.max(s, axis=-1, keepdims=True))
    return e / jnp.sum(e, axis=-1, keepdims=True)


def _attn_fwd(proj, kv, B, S):
    T = B * S
    tq = _row_tile(S)
    nq = S // tq
    W = XA_HEADS * XA_DIM

    def body(q_ref, kv_ref, o_ref):
        for h in range(XA_HEADS):
            sl = slice(h * 128, (h + 1) * 128)
            p = _attn_probs(q_ref[:, sl], kv_ref[:, sl])
            o_ref[:, sl] = _dot(p, kv_ref[:, W + h * 128:W + (h + 1) * 128]).astype(o_ref.dtype)

    return _pallas(
        body, name="attn_fwd", grid=(B, nq),
        in_specs=[pl.BlockSpec((tq, 512), lambda b, i: (b * nq + i, COL_XQ)),
                  pl.BlockSpec((MEM_LEN, 2 * W), lambda b, i: (b, 0))],
        out_specs=pl.BlockSpec((tq, W), lambda b, i: (b * nq + i, 0)),
        out_shape=jax.ShapeDtypeStruct((T, W), BF16), compiler_params=_cp("parallel", "parallel"),
    )(proj, kv)


def _attn_bwd(proj, kv, d_out, B, S):
    T = B * S
    tq = _row_tile(S)
    nq = S // tq
    W = XA_HEADS * XA_DIM

    def body(q_ref, kv_ref, do_ref, dq_ref, dkv_ref):
        @pl.when(pl.program_id(1) == 0)
        def _():
            dkv_ref[...] = jnp.zeros_like(dkv_ref)

        for h in range(XA_HEADS):
            sl = slice(h * 128, (h + 1) * 128)
            slv = slice(W + h * 128, W + (h + 1) * 128)
            qh = q_ref[:, sl]
            kh = kv_ref[:, sl]
            p = _attn_probs(qh, kh)
            dc = do_ref[:, sl]
            dp = _dot_nt(dc, kv_ref[:, slv])
            ds = p * (dp - jnp.sum(dp * p, axis=-1, keepdims=True)) * _XA_SCALE
            dq_ref[:, sl] = _dot(ds, kh).astype(dq_ref.dtype)
            dkv_ref[:, sl] += _dot_tn(ds, qh)
            dkv_ref[:, slv] += _dot_tn(p, dc)

    kvspec = pl.BlockSpec((MEM_LEN, 2 * W), lambda b, i: (b, 0))
    tile = pl.BlockSpec((tq, W), lambda b, i: (b * nq + i, 0))
    return _pallas(
        body, name="attn_bwd", grid=(B, nq),
        in_specs=[pl.BlockSpec((tq, 512), lambda b, i: (b * nq + i, COL_XQ)), kvspec,
                  pl.BlockSpec((None, tq, W), lambda b, i: (2, b * nq + i, 0))],
        out_specs=(tile, kvspec),
        out_shape=(jax.ShapeDtypeStruct((T, W), BF16), jax.ShapeDtypeStruct((B * MEM_LEN, 2 * W), F32)),
        compiler_params=_cp("parallel", "arbitrary"),
    )(proj, kv, d_out)


_MERGE_TM = 256
_GATE_W = 512


def _gate_specs(tm):
    base = COL_GATE0 // _GATE_W
    return [pl.BlockSpec((tm, _GATE_W), functools.partial(lambda i, k: (i, base + k), k=k)) for k in range(6)]


def _merge_fwd(a_out, b_out, c_out, wb, proj, riders=()):
    T = a_out.shape[0]
    tm = _row_tile(T, _MERGE_TM)
    nq, _, wd = wb.shape
    per_half = _GATE_W // wd

    def body(a_ref, b_ref, c_ref, w_ref, *rest):
        gates, (m_ref, up_ref) = rest[:6], rest[6:]
        for hf in range(2):
            cols = slice(hf * _GATE_W, (hf + 1) * _GATE_W)
            acc = None
            for n, br in enumerate((a_ref, b_ref, c_ref)):
                x = br[...]
                up = jnp.concatenate([_dot(x, w_ref[per_half * hf + j, n * BR_WIDTH:(n + 1) * BR_WIDTH, :])
                                      for j in range(per_half)], axis=1)
                up_ref[n, :, cols] = up.astype(up_ref.dtype)
                term = _sigmoid(gates[2 * n + hf][...].astype(F32)) * up
                acc = term if acc is None else acc + term
            m_ref[:, cols] = acc.astype(m_ref.dtype)

    br_spec = pl.BlockSpec((tm, BR_WIDTH), lambda i: (i, 0))
    return _carried(*_pcall(
        body, (a_out, b_out, c_out, wb, *([proj] * 6)), name="merge_fwd", grid=(T // tm,),
        in_specs=[br_spec, br_spec, br_spec,
                  pl.BlockSpec((nq, 3 * BR_WIDTH, wd), lambda i: (0, 0, 0))] + _gate_specs(tm),
        out_specs=(pl.BlockSpec((tm, D_MODEL), lambda i: (i, 0)), pl.BlockSpec((3, tm, D_MODEL), lambda i: (0, i, 0))),
        out_shape=(jax.ShapeDtypeStruct((T, D_MODEL), BF16), jax.ShapeDtypeStruct((3, T, D_MODEL), BF16)),
        semantics=("parallel",), riders=riders), riders)


def _branch_bwd_act(d_ups, wb, riders=()):
    _, T, D = d_ups.shape
    nq, _, wd = wb.shape
    tm = _row_tile(T)

    def body(d_ref, w_ref, o_ref):
        acc = None
        for q in range(nq):
            part = _dot_nt(d_ref[:, q * wd:(q + 1) * wd], w_ref[q])
            acc = part if acc is None else acc + part
        o_ref[...] = acc

    return _carried(*_pcall(
        body, (d_ups, wb), name="d_branch", grid=(3, T // tm),
        in_specs=[pl.BlockSpec((None, tm, D), lambda n, i: (n, i, 0)),
                  pl.BlockSpec((nq, BR_WIDTH, wd), lambda n, i: (0, n, 0))],
        out_specs=pl.BlockSpec((None, tm, BR_WIDTH), lambda n, i: (n, i, 0)),
        out_shape=jax.ShapeDtypeStruct((3, T, BR_WIDTH), F32), semantics=("parallel", "parallel"),
        riders=riders), riders)


def _branch_bwd_weight(name, br, d_ups, n, into=None):
    T = br.shape[0]
    D = d_ups.shape[2]
    wd = D // N_CHIPS
    tt = _row_tile(T, _TN_TOKENS)
    n_br = d_ups.shape[0]

    def body(b_ref, d_ref, *rest):
        o_ref = rest[-1]
        k = pl.program_id(0)
        for q in range(N_CHIPS):
            part = _dot_tn(b_ref[...], d_ref[:, q * wd:(q + 1) * wd])

            @pl.when(k == 0)
            def _():
                o_ref[q] = part

            @pl.when(k > 0)
            def _():
                o_ref[q] += part

    return _pallas(
        body, name=name, grid=(T // tt,),
        in_specs=[pl.BlockSpec((tt, BR_WIDTH), lambda k: (k, 0)),
                  pl.BlockSpec((None, tt, D), lambda k: (n, k, 0))] + ([] if into is None else [HBM_SPEC]),
        out_specs=pl.BlockSpec((N_CHIPS, BR_WIDTH, wd), lambda k: (0, n, 0)),
        out_shape=jax.ShapeDtypeStruct((N_CHIPS, n_br * BR_WIDTH, wd), F32),
        input_output_aliases={} if into is None else {2: 0}, compiler_params=_cp("arbitrary"),
    )(br, d_ups, *(() if into is None else (into,)))


def _merge_bwd(d_merged, ups, proj, riders=()):
    T = d_merged.shape[0]
    tm = _row_tile(T, _MERGE_TM)

    def body(dm_ref, up_ref, *rest):
        gates, (dup_ref, dg_ref) = rest[:6], rest[6:]
        for hf in range(2):
            cols = slice(hf * _GATE_W, (hf + 1) * _GATE_W)
            dm = dm_ref[:, cols]
            for n in range(3):
                gate = _sigmoid(gates[2 * n + hf][...].astype(F32))
                dup_ref[n, :, cols] = (dm * gate).astype(dup_ref.dtype)
                dg_ref[:, n * D_MODEL + hf * _GATE_W:n * D_MODEL + (hf + 1) * _GATE_W] = (
                    dm * up_ref[n, :, cols].astype(F32) * gate * (1.0 - gate)).astype(dg_ref.dtype)

    tile = pl.BlockSpec((tm, D_MODEL), lambda i: (i, 0))
    tile3 = pl.BlockSpec((3, tm, D_MODEL), lambda i: (0, i, 0))
    return _carried(*_pcall(
        body, (d_merged, ups, *([proj] * 6)), name="merge_bwd", grid=(T // tm,),
        in_specs=[tile, tile3] + _gate_specs(tm),
        out_specs=(tile3, pl.BlockSpec((tm, 3 * D_MODEL), lambda i: (i, 0))),
        out_shape=(jax.ShapeDtypeStruct((3, T, D_MODEL), BF16), jax.ShapeDtypeStruct((T, 3 * D_MODEL), BF16)),
        semantics=("parallel",), riders=riders), riders)


_CONV_TF = D_FF // 2
_CONV_TS = 256
_HALO = 16


def _conv_fwd(ab, cw, cb, B, S):
    T = B * S
    ts = _row_tile(S, _CONV_TS)
    tf = _CONV_TF
    nb = D_FF // tf
    tps = S // ts
    hb = ts // _HALO

    def body(a_ref, p_ref, b_ref, w_ref, cb_ref, o_ref):
        start = (pl.program_id(0) % tps) == 0
        a = a_ref[...].astype(F32)
        prev = jnp.where(start, 0.0, p_ref[...].astype(F32))
        ext = jnp.concatenate([prev, a], axis=0)
        a1 = pltpu.roll(ext, 1, 0)[_HALO:, :]
        a2 = pltpu.roll(ext, 2, 0)[_HALO:, :]
        ac = cb_ref[...] + w_ref[0] * a2 + w_ref[1] * a1 + w_ref[2] * a
        o_ref[...] = (ac * _sigmoid(ac) * b_ref[...].astype(F32)).astype(o_ref.dtype)

    return _pallas(
        body, name="conv_fwd", grid=(T // ts, nb),
        in_specs=[pl.BlockSpec((ts, tf), lambda i, j: (i, j)),
                  pl.BlockSpec((_HALO, tf), lambda i, j: (jnp.maximum(i * hb - 1, 0), j)),
                  pl.BlockSpec((ts, tf), lambda i, j: (i, j + nb)),
                  pl.BlockSpec((3, 1, tf), lambda i, j: (0, 0, j)),
                  pl.BlockSpec((1, tf), lambda i, j: (0, j))],
        out_specs=pl.BlockSpec((ts, tf), lambda i, j: (i, j)),
        out_shape=jax.ShapeDtypeStruct((T, D_FF), BF16), compiler_params=_cp("parallel", "parallel"),
    )(ab, ab, ab, cw, cb)


def _conv_bwd(ab, d_ff, cw, cb, B, S, riders=()):
    T = B * S
    ts = _row_tile(S, _CONV_TS)
    tf = _CONV_TF
    nb = D_FF // tf
    tps = S // ts
    hb = ts // _HALO
    last_h = T // _HALO - 1
    n_ext = ts + _HALO

    def body(a_ref, ap_ref, an_ref, b_ref, bn_ref, d_ref, dn_ref, w_ref, cb_ref, dab_ref, dw_ref, dcb_ref):
        i = pl.program_id(1)

        @pl.when(i == 0)
        def _():
            dw_ref[...] = jnp.zeros_like(dw_ref)
            dcb_ref[...] = jnp.zeros_like(dcb_ref)

        start = (i % tps) == 0
        end = (i % tps) == tps - 1
        a = a_ref[...].astype(F32)
        ext = jnp.concatenate([jnp.where(start, 0.0, ap_ref[...].astype(F32)), a, an_ref[...].astype(F32)], axis=0)
        r1 = pltpu.roll(ext, 1, 0)[_HALO:, :]
        r2 = pltpu.roll(ext, 2, 0)[_HALO:, :]
        ac = cb_ref[...] + w_ref[0] * r2 + w_ref[1] * r1 + w_ref[2] * ext[_HALO:, :]
        sg = _sigmoid(ac)
        d_e = jnp.concatenate([d_ref[...].astype(F32), jnp.where(end, 0.0, dn_ref[...].astype(F32))], axis=0)
        b_e = jnp.concatenate([b_ref[...].astype(F32), bn_ref[...].astype(F32)], axis=0)
        dab_ref[1] = (d_e[:ts, :] * (ac * sg)[:ts, :]).astype(dab_ref.dtype)
        dac = d_e * b_e * sg * (1.0 + ac * (1.0 - sg))
        u1 = pltpu.roll(dac, n_ext - 1, 0)[:ts, :]
        u2 = pltpu.roll(dac, n_ext - 2, 0)[:ts, :]
        dac0 = dac[:ts, :]
        dab_ref[0] = (w_ref[2] * dac0 + w_ref[1] * u1 + w_ref[0] * u2).astype(dab_ref.dtype)
        dcb_ref[...] += jnp.sum(dac0, axis=0, keepdims=True)
        dw_ref[2] += jnp.sum(dac0 * a, axis=0, keepdims=True)
        dw_ref[1] += jnp.sum(dac0 * r1[:ts, :], axis=0, keepdims=True)
        dw_ref[0] += jnp.sum(dac0 * r2[:ts, :], axis=0, keepdims=True)

    def cur(off):
        return pl.BlockSpec((ts, tf), lambda j, i: (i, j + off))

    def nxt(off):
        return pl.BlockSpec((_HALO, tf), lambda j, i: (jnp.minimum((i + 1) * hb, last_h), j + off))

    return _carried(*_pcall(
        body, (ab, ab, ab, ab, ab, d_ff, d_ff, cw, cb), name="conv_bwd", grid=(nb, T // ts),
        in_specs=[cur(0), pl.BlockSpec((_HALO, tf), lambda j, i: (jnp.maximum(i * hb - 1, 0), j)), nxt(0),
                  cur(nb), nxt(nb), cur(0), nxt(0),
                  pl.BlockSpec((3, 1, tf), lambda j, i: (0, 0, j)), pl.BlockSpec((1, tf), lambda j, i: (0, j))],
        out_specs=(pl.BlockSpec((2, ts, tf), lambda j, i: (0, i, j)), pl.BlockSpec((3, 1, tf), lambda j, i: (0, 0, j)),
                   pl.BlockSpec((1, tf), lambda j, i: (0, j))),
        out_shape=(jax.ShapeDtypeStruct((2, T, D_FF), BF16),
                   jax.ShapeDtypeStruct((3, 1, D_FF), F32), jax.ShapeDtypeStruct((1, D_FF), F32)),
        semantics=("parallel", "arbitrary"), riders=riders), riders)


def _local_step(x, mem, tgt, p, comm, B, S):
    g = {}
    h, slabs = comm.carry(
        "norm1", lambda r: _norm1_and_casts(x, p["norm1_g"], p["cast_beside_norm1"], comm.place, riders=r))
    comm.slabs.update(slabs)
    proj = comm.carry("in_proj", lambda r: _mm_cs("in_proj", h, comm.w("w_in"), BF16, riders=r))
    a_out = _gmlp_fwd(proj, p["ln_v_g"], p["ln_v_b"], p["w_spatial"], p["b_spatial"])
    o_h, b_out, states = comm.carry(
        "hgrn_fwd", lambda r: _hgrn_fwd(proj, p["lb_logits"], p["hgrn_norm_g"], B, S, riders=r))
    memn = _rms_fwd("mem_norm", mem, p["mem_norm_g"])
    kv = _mm_rs("mem_kv", memn, comm.w("w_mem_kv"), F32)
    c_out = _attn_fwd(proj, kv, B, S)
    merged, ups = comm.carry(
        "merge_fwd", lambda r: _merge_fwd(a_out, b_out, c_out, comm.w("w_branch"), proj, riders=r))
    x1, h2 = _proj_res_norm("out_proj_norm2", merged, comm.w("w_out"), x, p["norm2_g"])
    ab = comm.carry("up_proj", lambda r: _mm_cs("up_proj", h2, comm.w("w_up"), BF16, riders=r))
    conv_w = comm.w("conv_w")
    ff = _conv_fwd(ab, conv_w, p["conv_b"], B, S)
    dx2, g["final_g"], loss = _proj_res_loss("down_proj_loss", ff, comm.w("w_down"), x1, tgt, p["final_g"])

    comm.grad("w_down", _mm_tn_rs("g_w_down", ff, dx2, to=D_FF // 2))
    d_ff = comm.carry("d_ff", lambda r: _mm_nt_rs("d_ff", dx2, comm.w("w_down"), BF16, riders=r))
    d_ab, g["conv_w"], g["conv_b"] = comm.carry(
        "conv_bwd", lambda r: _conv_bwd(ab, d_ff, conv_w, p["conv_b"], B, S, riders=r))
    comm.grad("w_up", _mm_tn_cs("g_w_up", h2, d_ab, N_CHIPS, to=512, stacked=True))
    d_x1, g["norm2_g"] = comm.carry("d_h2", lambda r: _mm_nt_cs(
        "d_h2_norm2_bwd", d_ab, comm.w("w_up"), F32, riders=r, stacked=True, norm_bwd=(x1, p["norm2_g"], dx2)))
    comm.grad("w_out", _mm_tn_rs("g_w_out", merged, d_x1, to=512))
    d_merged = _mm_nt_rs("d_merged", d_x1, comm.w("w_out"), F32)
    d_ups, d_gates = comm.carry("merge_bwd", lambda r: _merge_bwd(d_merged, ups, proj, riders=r))

    d_br = comm.carry("d_branch", lambda r: _branch_bwd_act(d_ups, comm.w("w_branch"), riders=r))
    g_branch = None
    for n, br in enumerate((a_out, b_out, c_out)):
        g_branch = _branch_bwd_weight("g_w_branch%d" % n, br, d_ups, n, into=g_branch)
    comm.grad("w_branch", g_branch)

    d_gm, g["w_spatial"], g["b_spatial"], g["ln_v_g"], g["ln_v_b"] = comm.carry(
        "gmlp_bwd", lambda r: _gmlp_bwd(proj, d_br, p["ln_v_g"], p["ln_v_b"], p["w_spatial"], p["b_spatial"],
                                        riders=r))
    d_xq, d_kv = _attn_bwd(proj, kv, d_br, B, S)
    comm.grad("w_mem_kv", _mm_tn_rs("g_w_mem_kv", memn, d_kv, to=512))
    d_memn = _mm_nt_rs("d_memn", d_kv, comm.w("w_mem_kv"), F32)
    _, g["mem_norm_g"] = _rms_bwd("mem_norm_bwd", mem, p["mem_norm_g"], d_memn, None)
    d_proj, g["lb_logits"], g["hgrn_norm_g"] = comm.carry(
        "hgrn_bwd", lambda r: _hgrn_bwd(proj, o_h, states, d_br, p["lb_logits"], p["hgrn_norm_g"],
                                        (d_gm, d_xq, d_gates), B, S, riders=r))
    comm.small_grads([g[n].reshape(_SMALL_SHAPE[n]) for n in _SMALL_EARLY] + [loss])
    comm.grad("w_in", comm.carry("g_w_in", lambda r: _mm_tn_cs("g_w_in", h, d_proj, N_CHIPS, to=512, riders=r)))
    grad_x, g["norm1_g"] = comm.carry("d_h", lambda r: _mm_nt_cs(
        "d_h_norm1_bwd", d_proj, comm.w("w_in"), F32, riders=r, norm_bwd=(x, p["norm1_g"], d_x1)))
    return loss, grad_x, g


HBM_SPEC = pl.BlockSpec(memory_space=pltpu.HBM)


def _place():
    x, y, c = lax.axis_index("x"), lax.axis_index("y"), lax.axis_index("c")
    other_chips = [(1 - x, y), (x, 1 - y), (1 - x, 1 - y)]
    return x, y, c, other_chips


def _remote(src, dst, send_sem, recv_sem, dev):
    return pltpu.make_async_remote_copy(src_ref=src, dst_ref=dst, send_sem=send_sem, recv_sem=recv_sem,
                                        device_id=dev, device_id_type=MESH_ID)


class _Exchange:
    def __init__(self, operands, out_shape, aliases, scratch, start, finish, mid=None, mid_at=0.5):
        self.operands, self.out_shape, self.aliases, self.scratch = operands, out_shape, aliases, scratch
        self.start, self.finish, self.mid, self.mid_at = start, finish, mid, mid_at


def _run_exchanges(name, exs):
    n_in = [len(ex.operands) for ex in exs]
    n_out = [len(ex.out_shape) for ex in exs]
    n_scr = [len(ex.scratch) for ex in exs]

    def body(*refs):
        ins, outs, scr = refs[:sum(n_in)], refs[sum(n_in):sum(n_in) + sum(n_out)], refs[sum(n_in) + sum(n_out):]
        parts, oi, oo, os_ = [], 0, 0, 0
        for k in range(len(exs)):
            parts.append((ins[oi:oi + n_in[k]], outs[oo:oo + n_out[k]], scr[os_:os_ + n_scr[k]]))
            oi, oo, os_ = oi + n_in[k], oo + n_out[k], os_ + n_scr[k]
        for ex, part in zip(exs, parts):
            ex.start(*part)
        for ex, part in zip(exs, parts):
            if ex.mid is not None:
                ex.mid(*part)
        for ex, part in zip(exs, parts):
            ex.finish(*part)

    aliases, ops, shapes, scratch, oi, oo = {}, [], [], [], 0, 0
    for k, ex in enumerate(exs):
        aliases.update({oi + a: oo + b for a, b in ex.aliases.items()})
        oi, oo = oi + n_in[k], oo + n_out[k]
        ops += list(ex.operands)
        shapes += [pltpu.HBM(s.shape, s.dtype) for s in ex.out_shape]
        scratch += list(ex.scratch)
    res = _pallas(
        body, name=name, in_specs=[HBM_SPEC] * len(ops), out_specs=(HBM_SPEC,) * len(shapes), out_shape=tuple(shapes),
        input_output_aliases=aliases, scratch_shapes=scratch,
    )(*ops)
    out, oo = [], 0
    for k in range(len(exs)):
        out.append(list(res[oo:oo + n_out[k]]))
        oo += n_out[k]
    return out


def _ex_all_gather(slabs, halved, part=(0, 1)):
    n = len(slabs)

    def rows(a, cc):
        if not halved[a]:
            return slice(None)
        pr = slabs[a].shape[1] // part[1]
        return pl.ds(part[0] * pr + cc * (pr // 2), pr // 2)

    def ici(bufs, scr, a, j, chip, c, mine):
        px, py = chip
        x, y, _, _ = _place()
        qs = 2 * x + y if mine else 2 * px + py
        piece = bufs[a].at[qs, rows(a, c)]
        return _remote(piece, piece, scr[0].at[3 * a + j], scr[1].at[3 * a + j], (px, py, c))

    def d2d(bufs, scr, a, j, chip, cc):
        px, py = chip
        x, y, c, _ = _place()
        piece = bufs[a].at[2 * px + py, rows(a, cc)]
        return _remote(piece, piece, scr[2].at[3 * a + j], scr[3].at[3 * a + j], (x, y, 1 - c))

    def start(ins, outs, scr):
        _, _, c, chips = _place()
        for j, chip in enumerate(chips):
            for a in range(n):
                ici(outs, scr, a, j, chip, c, True).start()

    def finish(ins, outs, scr):
        _, _, c, chips = _place()
        for j, chip in enumerate(chips):
            for a in range(n):
                ici(outs, scr, a, j, chip, c, False).wait_recv()
                if halved[a]:
                    d2d(outs, scr, a, j, chip, c).start()
        for j, chip in enumerate(chips):
            for a in range(n):
                if halved[a]:
                    d2d(outs, scr, a, j, chip, 1 - c).wait_recv()
        for j, chip in enumerate(chips):
            for a in range(n):
                ici(outs, scr, a, j, chip, c, True).wait_send()
                if halved[a]:
                    d2d(outs, scr, a, j, chip, c).wait_send()

    return _Exchange(list(slabs), [jax.ShapeDtypeStruct(s.shape, s.dtype) for s in slabs],
                     {a: a for a in range(n)}, [pltpu.SemaphoreType.DMA((3 * n,))] * 4, start, finish)


def _ex_gather_relay(slabs, mid_at=0.5):
    n = len(slabs)

    def rows(a, cc):
        hr = slabs[a].shape[1] // 2
        return pl.ds(cc * hr, hr)

    def peers():
        x, y, c, _ = _place()
        nbr0 = ((x + c) % 2, (y + 1 - c) % 2)
        nbr1 = ((x + 1 - c) % 2, (y + c) % 2)
        return x, y, c, nbr0, nbr1, (1 - x, 1 - y)

    def ici(bufs, scr, a, k, chip, dev, cc):
        _, _, c, _, _, _ = peers()
        piece = bufs[a].at[2 * chip[0] + chip[1], rows(a, cc)]
        return _remote(piece, piece, scr[0].at[3 * a + k], scr[1].at[3 * a + k], (dev[0], dev[1], c))

    def d2d(bufs, scr, a, k, chip, cc):
        x, y, c, _, _, _ = peers()
        piece = bufs[a].at[2 * chip[0] + chip[1], rows(a, cc)]
        return _remote(piece, piece, scr[2].at[3 * a + k], scr[3].at[3 * a + k], (x, y, 1 - c))

    def start(ins, outs, scr):
        x, y, c, nbr0, nbr1, _ = peers()
        for a in range(n):
            ici(outs, scr, a, 0, (x, y), nbr0, c).start()
            ici(outs, scr, a, 1, (x, y), nbr1, c).start()

    def mid(ins, outs, scr):
        x, y, c, nbr0, nbr1, diag = peers()
        for a in range(n):
            ici(outs, scr, a, 0, nbr0, nbr0, c).wait_recv()
            ici(outs, scr, a, 2, nbr0, nbr1, c).start()
            d2d(outs, scr, a, 0, nbr0, c).start()
        for a in range(n):
            ici(outs, scr, a, 1, nbr1, nbr1, c).wait_recv()
            d2d(outs, scr, a, 1, nbr1, c).start()

    def finish(ins, outs, scr):
        x, y, c, nbr0, nbr1, diag = peers()
        for a in range(n):
            ici(outs, scr, a, 2, diag, nbr1, c).wait_recv()
            d2d(outs, scr, a, 2, diag, c).start()
        for a in range(n):
            d2d(outs, scr, a, 0, nbr1, 1 - c).wait_recv()
            d2d(outs, scr, a, 1, nbr0, 1 - c).wait_recv()
            d2d(outs, scr, a, 2, diag, 1 - c).wait_recv()
        for a in range(n):
            ici(outs, scr, a, 0, (x, y), nbr0, c).wait_send()
            ici(outs, scr, a, 1, (x, y), nbr1, c).wait_send()
            ici(outs, scr, a, 2, nbr0, nbr1, c).wait_send()
            d2d(outs, scr, a, 0, nbr0, c).wait_send()
            d2d(outs, scr, a, 1, nbr1, c).wait_send()
            d2d(outs, scr, a, 2, diag, c).wait_send()

    return _Exchange(list(slabs), [jax.ShapeDtypeStruct(s.shape, s.dtype) for s in slabs],
                     {a: a for a in range(n)}, [pltpu.SemaphoreType.DMA((3 * n,))] * 4, start, finish, mid, mid_at)


def _ex_to_sibling(grads):
    n = len(grads)

    def copy(ins, outs, scr, a):
        x, y, c, _ = _place()
        hr = grads[a].shape[1] // 2
        return _remote(ins[a].at[:, pl.ds((1 - c) * hr, hr), :], outs[a], scr[0].at[a], scr[1].at[a], (x, y, 1 - c))

    def start(ins, outs, scr):
        for a in range(n):
            copy(ins, outs, scr, a).start()

    def finish(ins, outs, scr):
        for a in range(n):
            copy(ins, outs, scr, a).wait()

    out_shape = [jax.ShapeDtypeStruct((g.shape[0], g.shape[1] // 2, g.shape[2]), g.dtype) for g in grads]
    return _Exchange(list(grads), out_shape, {}, [pltpu.SemaphoreType.DMA((n,))] * 2, start, finish)


def _ex_to_owner(parts, part=(0, 1), landing=None):
    n = len(parts)

    def copy(ins, outs, scr, a, j, chip):
        _, _, c, _ = _place()
        px, py = chip
        pr = parts[a].shape[1] // part[1]
        rows = pl.ds(part[0] * pr, pr)
        return _remote(ins[a].at[2 * px + py, rows], outs[a].at[j, rows], scr[0].at[3 * a + j],
                       scr[1].at[3 * a + j], (px, py, c))

    def start(ins, outs, scr):
        for j, chip in enumerate(_place()[3]):
            for a in range(n):
                copy(ins, outs, scr, a, j, chip).start()

    def finish(ins, outs, scr):
        for j, chip in enumerate(_place()[3]):
            for a in range(n):
                copy(ins, outs, scr, a, j, chip).wait()

    out_shape = [jax.ShapeDtypeStruct((3,) + p.shape[1:], p.dtype) for p in parts]
    operands, aliases = list(parts), {}
    if landing is not None:
        operands, aliases = operands + list(landing), {n + a: a for a in range(n)}
    return _Exchange(operands, out_shape, aliases, [pltpu.SemaphoreType.DMA((3 * n,))] * 2, start, finish)


def _ex_share_halves(bufs):
    n = len(bufs)

    def copy(outs, scr, a, cc):
        x, y, c, _ = _place()
        hr = bufs[a].shape[0] // 2
        piece = outs[a].at[pl.ds(cc * hr, hr), :]
        return _remote(piece, piece, scr[0].at[a], scr[1].at[a], (x, y, 1 - c))

    def start(ins, outs, scr):
        c = _place()[2]
        for a in range(n):
            copy(outs, scr, a, c).start()

    def finish(ins, outs, scr):
        c = _place()[2]
        for a in range(n):
            copy(outs, scr, a, c).wait_send()
            copy(outs, scr, a, 1 - c).wait_recv()

    return _Exchange(list(bufs), [jax.ShapeDtypeStruct(b.shape, b.dtype) for b in bufs], {a: a for a in range(n)},
                     [pltpu.SemaphoreType.DMA((n,))] * 2, start, finish)


def _ex_gather_small(arrs):
    n = len(arrs)

    def peer_of(m):
        x, y, c, _ = _place()
        return (1 - x if m & 4 else x, 1 - y if m & 2 else y, 1 - c if m & 1 else c)

    def own(ins, outs, scr, a):
        x, y, c, _ = _place()
        return pltpu.make_async_copy(ins[a], outs[a].at[4 * x + 2 * y + c], scr[2].at[a])

    def start(ins, outs, scr):
        x, y, c, _ = _place()
        for a in range(n):
            own(ins, outs, scr, a).start()
        for m in range(1, N_DEV):
            for a in range(n):
                k = (N_DEV - 1) * a + m - 1
                _remote(ins[a], outs[a].at[4 * x + 2 * y + c], scr[0].at[k], scr[1].at[k], peer_of(m)).start()

    def finish(ins, outs, scr):
        for a in range(n):
            own(ins, outs, scr, a).wait()
        for m in range(1, N_DEV):
            px, py, pc = peer_of(m)
            for a in range(n):
                k = (N_DEV - 1) * a + m - 1
                slot = outs[a].at[4 * px + 2 * py + pc]
                cp = _remote(ins[a], slot, scr[0].at[k], scr[1].at[k], (px, py, pc))
                cp.wait_send()
                cp.wait_recv()

    out_shape = [jax.ShapeDtypeStruct((N_DEV,) + a.shape, a.dtype) for a in arrs]
    return _Exchange(list(arrs), out_shape, {},
                     [pltpu.SemaphoreType.DMA(((N_DEV - 1) * n,))] * 2 + [pltpu.SemaphoreType.DMA((n,))], start, finish)


def _div_tile(n, want):
    best = None
    for t in range(8, min(n, want) + 1, 8):
        if n % t == 0:
            best = t
    assert best is not None, n
    return best


def _cast_into_slab(name, w, place, dtype):
    r, cc = w.shape
    tr = r if r * cc <= 128 * 1024 else _div_tile(r, 256)

    def body(s_ref, w_ref, o_ref):
        o_ref[...] = w_ref[...].astype(o_ref.dtype)

    return _pallas(
        body, name=name,
        grid_spec=pltpu.PrefetchScalarGridSpec(
            num_scalar_prefetch=1, grid=(r // tr,),
            in_specs=[pl.BlockSpec((tr, cc), lambda i, s: (i, 0))],
            out_specs=pl.BlockSpec((None, tr, cc), lambda i, s: (s[0], i, 0))),
        out_shape=jax.ShapeDtypeStruct((N_CHIPS, r, cc), dtype), compiler_params=_cp("parallel"),
    )(place, w)


def _add_half(name, g, rcv, place):
    nq, r, cc = g.shape
    hr = r // 2

    def body(s_ref, g_ref, r_ref, o_ref):
        o_ref[...] = (g_ref[...] + r_ref[...]).astype(o_ref.dtype)

    spec = pl.BlockSpec((None, hr, cc), lambda i, s: (i, 0, 0))
    return _pallas(
        body, name=name,
        grid_spec=pltpu.PrefetchScalarGridSpec(
            num_scalar_prefetch=1, grid=(nq,),
            in_specs=[pl.BlockSpec((None, hr, cc), lambda i, s: (i, s[1], 0)), spec], out_specs=spec),
        out_shape=jax.ShapeDtypeStruct((nq, hr, cc), BF16), compiler_params=_cp("parallel"),
    )(place, g, rcv)


def _sum_owner(name, part, rcv, place):
    _, hr, cc = part.shape
    tr = _div_tile(hr, 128)
    nb = hr // tr

    def body(s_ref, p_ref, r_ref, o_ref):
        o_ref[...] = ((p_ref[...].astype(F32) + r_ref[0].astype(F32)) + r_ref[1].astype(F32)) + r_ref[2].astype(F32)

    return _pallas(
        body, name=name,
        grid_spec=pltpu.PrefetchScalarGridSpec(
            num_scalar_prefetch=1, grid=(nb,),
            in_specs=[pl.BlockSpec((None, tr, cc), lambda i, s: (s[0], i, 0)),
                      pl.BlockSpec((3, tr, cc), lambda i, s: (0, i, 0))],
            out_specs=pl.BlockSpec((tr, cc), lambda i, s: (s[1] * nb + i, 0))),
        out_shape=jax.ShapeDtypeStruct((2 * hr, cc), F32), compiler_params=_cp("parallel"),
    )(place, part, rcv)


def _sum_small(gathered, local, place):
    n = len(gathered)

    def body(s_ref, *refs):
        g_refs, l_refs, o_refs = refs[:n], refs[n:2 * n], refs[2 * n:]
        me = s_ref[2]
        for g_ref, l_ref, o_ref in zip(g_refs, l_refs, o_refs):
            acc = None
            for d in range(N_DEV):
                term = jnp.where(me == d, l_ref[...], g_ref[d])
                acc = term if acc is None else acc + term
            o_ref[...] = acc

    def whole(shape):
        return pl.BlockSpec(shape, lambda i, s, nd=len(shape): (0,) * nd)

    return _pallas(
        body, name="sum_small",
        grid_spec=pltpu.PrefetchScalarGridSpec(
            num_scalar_prefetch=1, grid=(1,),
            in_specs=[whole(g.shape) for g in gathered] + [whole(a.shape) for a in local],
            out_specs=tuple(whole(a.shape) for a in local)),
        out_shape=tuple(jax.ShapeDtypeStruct(a.shape, a.dtype) for a in local), compiler_params=_cp("arbitrary"),
    )(place, *gathered, *local)


def _adamw(name, w, g, m, v):
    r, cc = w.shape
    tr = r if r * cc <= 128 * 1024 else _div_tile(r, 256)

    def body(w_ref, g_ref, m_ref, v_ref, d_ref, mo_ref, vo_ref, go_ref):
        gv = g_ref[...]
        go_ref[...] = gv
        mn = ADAM_B1 * m_ref[...] + (1.0 - ADAM_B1) * gv
        vn = ADAM_B2 * v_ref[...] + (1.0 - ADAM_B2) * (gv * gv)
        m_hat = mn / (1.0 - ADAM_B1 ** ADAM_STEP)
        v_hat = vn / (1.0 - ADAM_B2 ** ADAM_STEP)
        d_ref[...] = -ADAM_LR * (m_hat / (jnp.sqrt(v_hat) + ADAM_EPS) + ADAM_WD * w_ref[...])
        mo_ref[...] = mn
        vo_ref[...] = vn

    spec = pl.BlockSpec((tr, cc), lambda i: (i, 0))
    sd = jax.ShapeDtypeStruct((r, cc), F32)
    return _pallas(
        body, name=name, grid=(r // tr,), in_specs=[spec] * 4, out_specs=(spec,) * 4, out_shape=(sd,) * 4,
        compiler_params=_cp("parallel"),
    )(w, g, m, v)


_BIG = ("w_in", "w_up", "w_branch", "w_mem_kv", "w_out", "w_down")
_BIG_SHARD_SHAPE = {"w_in": (1024, 1664), "w_up": (1024, 1408), "w_branch": (1536, 256),
                    "w_mem_kv": (256, 1024), "w_out": (256, 1024), "w_down": (704, 1024)}
_SMALL_SHAPE = {"norm1_g": (1, D_MODEL), "ln_v_g": (1, GM_WIDTH), "ln_v_b": (1, GM_WIDTH),
                "w_spatial": (GM_GROUPS * GM_CHUNK, GM_CHUNK), "b_spatial": (GM_GROUPS, GM_CHUNK),
                "lb_logits": (2, HG_HEADS * HG_DIM), "hgrn_norm_g": (1, HG_DIM), "mem_norm_g": (1, D_MODEL),
                "norm2_g": (1, D_MODEL), "conv_w": (3, D_FF), "conv_b": (1, D_FF), "final_g": (1, D_MODEL)}
_SMALL_EARLY = tuple(n for n in _SMALL_SHAPE if n != "norm1_g")
_PARAM_ORDER = ("norm1_g", "w_in", "ln_v_g", "ln_v_b", "w_spatial", "b_spatial", "lb_logits", "hgrn_norm_g",
                "mem_norm_g", "w_mem_kv", "w_branch", "w_out", "norm2_g", "w_up", "conv_w", "conv_b", "w_down",
                "final_g")


def _adamw_small(ws, gs, ms, vs):
    n = len(ws)

    def body(*refs):
        w_refs, g_refs, m_refs, v_refs = refs[:n], refs[n:2 * n], refs[2 * n:3 * n], refs[3 * n:4 * n]
        d_refs, mo_refs, vo_refs = refs[4 * n:5 * n], refs[5 * n:6 * n], refs[6 * n:]
        for k in range(n):
            gv = g_refs[k][...]
            mn = ADAM_B1 * m_refs[k][...] + (1.0 - ADAM_B1) * gv
            vn = ADAM_B2 * v_refs[k][...] + (1.0 - ADAM_B2) * (gv * gv)
            m_hat = mn / (1.0 - ADAM_B1 ** ADAM_STEP)
            v_hat = vn / (1.0 - ADAM_B2 ** ADAM_STEP)
            d_refs[k][...] = -ADAM_LR * (m_hat / (jnp.sqrt(v_hat) + ADAM_EPS) + ADAM_WD * w_refs[k][...])
            mo_refs[k][...] = mn
            vo_refs[k][...] = vn

    specs = [pl.BlockSpec(a.shape, lambda i, nd=a.ndim: (0,) * nd) for a in ws]
    shapes = tuple(jax.ShapeDtypeStruct(a.shape, F32) for a in ws)
    res = _pallas(
        body, name="adamw_small", grid=(1,), in_specs=specs * 4, out_specs=tuple(specs * 3), out_shape=shapes * 3,
        compiler_params=_cp("arbitrary"),
    )(*ws, *gs, *ms, *vs)
    return res[:n], res[n:2 * n], res[2 * n:]


class _Comm:
    _ROW_SHARDED = ("w_mem_kv", "w_out", "w_down")

    def __init__(self, slabs, place):
        self.slabs, self.place = slabs, place
        self.full, self.raw, self.parts, self.landing, self.bufs, self.done = {}, {}, {}, {}, {}, {}

    def w(self, name):
        a = self.full[name]
        if name in self._ROW_SHARDED:
            return a.reshape(-1, a.shape[-1])
        if name == "conv_w":
            return jnp.transpose(a, (1, 0, 2)).reshape(3, 1, D_FF)
        return a

    def grad(self, name, arr):
        self.raw[name] = arr.reshape((N_CHIPS, -1, arr.shape[-1]))
        if name == "w_in":
            ex, deliver = self._to_sibling(["w_in"])
            deliver(_run_exchanges("rs_sibling_w_in", [ex])[0])

    def small_grads(self, arrays):
        self.small_local = list(arrays)

    def carry(self, tag, call):
        plan = self._plan(tag)
        if not plan:
            return call(())
        out, carried = call([ex for ex, _ in plan])
        for (_, deliver), res in zip(plan, carried):
            deliver(res)
        return out

    def finish(self, last_small):
        ex, deliver = self._share(["w_out", "w_branch", "w_mem_kv", "w_in"])
        shared, small = _run_exchanges("share_and_gather_last", [ex, _ex_gather_small(last_small)])
        deliver(shared)
        return self.done, self.small_local + list(last_small), self.small_everyone + small

    def _plan(self, tag):
        if tag == "norm1":
            def deliver(res):
                self.full["w_in"] = res[0]

            return [(_ex_gather_relay([self.slabs["w_in"]]), deliver)]
        if tag == "in_proj":
            return [self._gather_relay(["w_branch", "w_out", "w_mem_kv", "w_down"], 0.6), self._gather(["conv_w"])]
        if tag == "hgrn_fwd":
            return [self._gather_relay(["w_up"], 0.8)]
        if tag == "d_h2":
            return [self._to_sibling(["w_down", "w_up"])]
        if tag == "merge_bwd":
            return [self._to_owner(["w_up"], (0, 2))]
        if tag == "hgrn_bwd":
            return [self._to_owner(["w_down"]), self._to_owner(["w_up"], (1, 2)),
                    self._to_sibling(["w_out", "w_branch", "w_mem_kv"])]
        if tag == "g_w_in":
            def keep(res):
                self.small_everyone = res

            return [self._to_owner(["w_out", "w_branch", "w_mem_kv"]), (_ex_gather_small(self.small_local), keep)]
        if tag == "d_h":
            return [self._to_owner(["w_in"]), self._share(["w_down", "w_up"])]
        return []

    def _gather(self, names, part=(0, 1)):
        def deliver(res):
            self.slabs.update(zip(names, res))
            self.full.update(zip(names, res))

        return _ex_all_gather([self.slabs[n] for n in names], [n != "conv_w" for n in names], part), deliver

    def _gather_relay(self, names, mid_at):
        return _ex_gather_relay([self.slabs[n] for n in names], mid_at), lambda res: self.full.update(zip(names, res))

    def _to_sibling(self, names):
        def deliver(res):
            for n, r in zip(names, res):
                self.parts[n] = _add_half("rs_add_" + n, self.raw[n], r, self.place)

        return _ex_to_sibling([self.raw[n] for n in names]), deliver

    def _to_owner(self, names, part=(0, 1)):
        def deliver(res):
            for n, r in zip(names, res):
                if part[0] + 1 < part[1]:
                    self.landing[n] = r
                else:
                    self.bufs[n] = _sum_owner("rs_sum_" + n, self.parts[n], r, self.place)

        landing = [self.landing[n] for n in names] if part[0] else None
        return _ex_to_owner([self.parts[n] for n in names], part, landing), deliver

    def _share(self, names):
        return _ex_share_halves([self.bufs[n] for n in names]), lambda res: self.done.update(zip(names, res))


def kernel(x, mem, norm1_g, w_in, ln_v_g, ln_v_b, w_spatial, b_spatial, lb_logits, hgrn_norm_g, mem_norm_g, w_mem_kv, w_branch, w_out, norm2_g, w_up, conv_w, conv_b, w_down, final_g, loss_target, m_norm1_g, m_w_in, m_ln_v_g, m_ln_v_b, m_w_spatial, m_b_spatial, m_lb_logits, m_hgrn_norm_g, m_mem_norm_g, m_w_mem_kv, m_w_branch, m_w_out, m_norm2_g, m_w_up, m_conv_w, m_conv_b, m_w_down, m_final_g, v_norm1_g, v_w_in, v_ln_v_g, v_ln_v_b, v_w_spatial, v_b_spatial, v_lb_logits, v_hgrn_norm_g, v_mem_norm_g, v_w_mem_kv, v_w_branch, v_w_out, v_norm2_g, v_w_up, v_conv_w, v_conv_b, v_w_down, v_final_g):
    w = dict(norm1_g=norm1_g, w_in=w_in, ln_v_g=ln_v_g, ln_v_b=ln_v_b, w_spatial=w_spatial, b_spatial=b_spatial,
             lb_logits=lb_logits, hgrn_norm_g=hgrn_norm_g, mem_norm_g=mem_norm_g, w_mem_kv=w_mem_kv,
             w_branch=w_branch, w_out=w_out, norm2_g=norm2_g, w_up=w_up, conv_w=conv_w, conv_b=conv_b,
             w_down=w_down, final_g=final_g)
    mom = dict(norm1_g=m_norm1_g, w_in=m_w_in, ln_v_g=m_ln_v_g, ln_v_b=m_ln_v_b, w_spatial=m_w_spatial,
               b_spatial=m_b_spatial, lb_logits=m_lb_logits, hgrn_norm_g=m_hgrn_norm_g, mem_norm_g=m_mem_norm_g,
               w_mem_kv=m_w_mem_kv, w_branch=m_w_branch, w_out=m_w_out, norm2_g=m_norm2_g, w_up=m_w_up,
               conv_w=m_conv_w, conv_b=m_conv_b, w_down=m_w_down, final_g=m_final_g)
    var = dict(norm1_g=v_norm1_g, w_in=v_w_in, ln_v_g=v_ln_v_g, ln_v_b=v_ln_v_b, w_spatial=v_w_spatial,
               b_spatial=v_b_spatial, lb_logits=v_lb_logits, hgrn_norm_g=v_hgrn_norm_g, mem_norm_g=v_mem_norm_g,
               w_mem_kv=v_w_mem_kv, w_branch=v_w_branch, w_out=v_w_out, norm2_g=v_norm2_g, w_up=v_w_up,
               conv_w=v_conv_w, conv_b=v_conv_b, w_down=v_w_down, final_g=v_final_g)
    B, S, D = x.shape
    T = B * S
    ci = lax.axis_index("c")
    q = 2 * lax.axis_index("x") + lax.axis_index("y")
    place = jnp.stack([q, ci, 2 * q + ci]).astype(jnp.int32)

    shards = {n: w[n].reshape(_BIG_SHARD_SHAPE[n]) for n in _BIG}
    slabs = {"w_in": _cast_into_slab("slab_w_in", shards.pop("w_in"), place, BF16),
             "conv_w": _cast_into_slab("slab_conv_w", conv_w[0], place, F32)}
    comm = _Comm(slabs, place)
    p = dict(
        cast_beside_norm1=shards,
        norm1_g=norm1_g, ln_v_g=ln_v_g, ln_v_b=ln_v_b, w_spatial=w_spatial[0],
        b_spatial=b_spatial.reshape(GM_GROUPS, GM_CHUNK, 1), lb_logits=lb_logits, hgrn_norm_g=hgrn_norm_g,
        mem_norm_g=mem_norm_g, norm2_g=norm2_g, conv_b=conv_b, final_g=final_g.reshape(1, D))

    loss, grad_x, g = _local_step(x.reshape(T, D), mem.reshape(B * MEM_LEN, D), loss_target.reshape(T, D), p, comm,
                                  B, S)

    shard_grads, local_small, everyone = comm.finish([g["norm1_g"]])
    summed = _sum_small(everyone, local_small, place)
    small_names = list(_SMALL_EARLY) + ["norm1_g"]
    total = dict(zip(_SMALL_EARLY, summed))
    loss_total, total["norm1_g"] = summed[len(_SMALL_EARLY)][0, 0], summed[-1]

    grads, delta, new_m, new_v = {}, {}, {}, {}
    for n in _BIG:
        shp = _BIG_SHARD_SHAPE[n]
        delta[n], new_m[n], new_v[n], grads[n] = _adamw("adamw_" + n, w[n].reshape(shp), shard_grads[n],
                                                        mom[n].reshape(shp), var[n].reshape(shp))
    cw_shard = D_FF // N_CHIPS
    total["conv_w"] = lax.dynamic_slice(total["conv_w"], (0, q * cw_shard), (3, cw_shard)).reshape(3, 1, cw_shard)

    def flat2d(d, n):
        return d[n].reshape(total[n].shape)

    upd = _adamw_small([flat2d(w, n) for n in small_names], [total[n] for n in small_names],
                       [flat2d(mom, n) for n in small_names], [flat2d(var, n) for n in small_names])
    for k, n in enumerate(small_names):
        grads[n], delta[n], new_m[n], new_v[n] = total[n], upd[0][k], upd[1][k], upd[2][k]

    def shaped(d):
        return [d[n].reshape(w[n].shape) for n in _PARAM_ORDER]

    return (loss_total, grad_x.reshape(B, S, D), *shaped(grads), *shaped(delta), *shaped(new_m), *shaped(new_v))
```

```python
import functools
import math

import jax
import jax.numpy as jnp
from jax import lax
from jax.experimental import pallas as pl
from jax.experimental.pallas import tpu as pltpu

F32 = jnp.float32
BF16 = jnp.bfloat16
EPS = 1e-6

D_MODEL = 1024
MEM_LEN = 256
GM_WIDTH = 512
GM_CHUNK = 128
GM_GROUPS = 4
HG_HEADS = 4
HG_DIM = 128
HG_CHUNK = 64
XA_HEADS = 4
XA_DIM = 128
BR_WIDTH = 512
D_FF = 2816
IN_WIDTH = 6656
N_CHIPS = 4
N_DEV = 8

ADAM_LR = 0.001
ADAM_B1 = 0.9
ADAM_B2 = 0.999
ADAM_EPS = 1e-08
ADAM_WD = 0.01
ADAM_STEP = 10

COL_ZU, COL_ZV, COL_HQ, COL_HF, COL_HI, COL_HG, COL_XQ = 0, 1, 2, 3, 4, 5, 6
COL_GATE0 = 3584

VMEM_LIMIT_BYTES = 48 * 1024 * 1024
MESH_ID = pl.DeviceIdType.MESH


def _cp(*sem):
    return pltpu.CompilerParams(dimension_semantics=sem, vmem_limit_bytes=VMEM_LIMIT_BYTES)


def _pallas(body, *, out_shape, **kw):
    def pin(s):
        return pltpu.HBM(s.shape, s.dtype) if isinstance(s, jax.ShapeDtypeStruct) else s

    out_shape = tuple(pin(s) for s in out_shape) if isinstance(out_shape, (tuple, list)) else pin(out_shape)
    call = pl.pallas_call(body, out_shape=out_shape, **kw)

    def run(*operands):
        return call(*[pltpu.with_memory_space_constraint(o, pltpu.HBM) if jnp.issubdtype(o.dtype, jnp.floating)
                      else o for o in operands])

    return run


def _dot(a, b):
    return lax.dot_general(a.astype(BF16), b.astype(BF16), (((1,), (0,)), ((), ())), preferred_element_type=F32)


def _dot_nt(a, b):
    return lax.dot_general(a.astype(BF16), b.astype(BF16), (((1,), (1,)), ((), ())), preferred_element_type=F32)


def _dot_tn(a, b):
    return lax.dot_general(a.astype(BF16), b.astype(BF16), (((0,), (0,)), ((), ())), preferred_element_type=F32)


def _dot_01(mask01, x):
    hi = x.astype(BF16)
    r1 = x - hi.astype(F32)
    mid = r1.astype(BF16)
    lo = (r1 - mid.astype(F32)).astype(BF16)
    m = mask01.astype(BF16)
    dn = (((1,), (0,)), ((), ()))
    return (lax.dot_general(m, hi, dn, preferred_element_type=F32)
            + lax.dot_general(m, mid, dn, preferred_element_type=F32)
            + lax.dot_general(m, lo, dn, preferred_element_type=F32))


def _sigmoid(z):
    return 1.0 / (1.0 + jnp.exp(-z))


_GELU_C = math.sqrt(2.0 / math.pi)


def _gelu_and_grad(z):
    inner = _GELU_C * (z + 0.044715 * z * z * z)
    t = jnp.tanh(inner)
    val = 0.5 * z * (1.0 + t)
    grad = 0.5 * (1.0 + t) + 0.5 * z * (1.0 - t * t) * _GELU_C * (1.0 + 3.0 * 0.044715 * z * z)
    return val, grad


def _row_tile(n, want=512):
    t = min(want, n)
    assert n % t == 0
    return t


def _pcall(body, operands, *, name, grid, in_specs, out_specs, out_shape, scratch_shapes=(), semantics, riders=(),
           prefetch=None):
    single = not isinstance(out_shape, (tuple, list))
    out_specs = (out_specs,) if single else tuple(out_specs)
    out_shape = (out_shape,) if single else tuple(out_shape)
    n_pre = 0 if prefetch is None else 1

    def call(fn, ins_, outs_, shapes_, scr_, ops, sem, aliases):
        if prefetch is None:
            return _pallas(fn, name=name, grid=grid, in_specs=ins_, out_specs=outs_, out_shape=shapes_,
                           scratch_shapes=scr_, input_output_aliases=aliases, compiler_params=_cp(*sem))(*ops)
        spec = pltpu.PrefetchScalarGridSpec(num_scalar_prefetch=1, grid=grid, in_specs=ins_, out_specs=outs_,
                                            scratch_shapes=scr_)
        return _pallas(fn, name=name, grid_spec=spec, out_shape=shapes_, input_output_aliases=aliases,
                       compiler_params=_cp(*sem))(prefetch, *ops)

    if not riders:
        res = call(body, list(in_specs), out_specs, out_shape, list(scratch_shapes), operands, semantics, {})
        return (res[0] if single else res), []
    n_in, n_out, n_scr = len(in_specs), len(out_shape), len(scratch_shapes)
    ex_in = [len(ex.operands) for ex in riders]
    ex_out = [len(ex.out_shape) for ex in riders]
    ex_scr = [len(ex.scratch) for ex in riders]
    tot_in, tot_out = n_in + sum(ex_in), n_out + sum(ex_out)

    def wrapped(*refs):
        pre, refs = refs[:n_pre], refs[n_pre:]
        ins, outs, scr = refs[:tot_in], refs[tot_in:tot_in + tot_out], refs[tot_in + tot_out:]
        ids = [pl.program_id(d) for d in range(len(grid))]
        first = functools.reduce(lambda p, t: p & t, [i == 0 for i in ids])
        last = functools.reduce(lambda p, t: p & t, [i == n - 1 for i, n in zip(ids, grid)])
        parts, oi, oo, os_ = [], n_in, n_out, n_scr
        for k in range(len(riders)):
            parts.append((ins[oi:oi + ex_in[k]], outs[oo:oo + ex_out[k]], scr[os_:os_ + ex_scr[k]]))
            oi, oo, os_ = oi + ex_in[k], oo + ex_out[k], os_ + ex_scr[k]

        @pl.when(first)
        def _():
            for ex, part in zip(riders, parts):
                ex.start(*part)

        step, total = 0, 1
        for i, n in zip(ids, grid):
            step, total = step * n + i, total * n
        for ex, part in zip(riders, parts):
            if ex.mid is not None:
                @pl.when(step == min(total - 1, int(total * ex.mid_at)))
                def _(ex=ex, part=part):
                    ex.mid(*part)

        body(*pre, *ins[:n_in], *outs[:n_out], *scr[:n_scr])

        @pl.when(last)
        def _():
            for ex, part in zip(riders, parts):
                ex.finish(*part)

    aliases, oi, oo = {}, n_in, n_out
    all_ops, all_shapes, all_scr = list(operands), list(out_shape), list(scratch_shapes)
    for k, ex in enumerate(riders):
        aliases.update({n_pre + oi + a: oo + b for a, b in ex.aliases.items()})
        oi, oo = oi + ex_in[k], oo + ex_out[k]
        all_ops += list(ex.operands)
        all_shapes += [pltpu.HBM(s.shape, s.dtype) for s in ex.out_shape]
        all_scr += list(ex.scratch)
    res = call(wrapped, list(in_specs) + [HBM_SPEC] * sum(ex_in), out_specs + (HBM_SPEC,) * sum(ex_out),
               tuple(all_shapes), all_scr, all_ops, ["arbitrary"] * len(grid), aliases)
    own = res[0] if single else tuple(res[:n_out])
    carried, oo = [], n_out
    for k in range(len(riders)):
        carried.append(list(res[oo:oo + ex_out[k]]))
        oo += ex_out[k]
    return own, carried


def _carried(out, carried, riders):
    return (out, carried) if riders else out


def _matmul(name, operands, *, grid, in_specs, o_spec, out_shape, out_dtype, dims, riders=()):
    nk = grid[2]
    assert nk == 1 or out_dtype == F32

    def body(a_ref, b_ref, o_ref):
        part = lax.dot_general(a_ref[...].astype(BF16), b_ref[...].astype(BF16), (dims, ((), ())),
                               preferred_element_type=F32)
        if nk == 1:
            o_ref[...] = part.astype(o_ref.dtype)
        else:
            k = pl.program_id(2)

            @pl.when(k == 0)
            def _():
                o_ref[...] = part

            @pl.when(k > 0)
            def _():
                o_ref[...] += part

    out, carried = _pcall(body, operands, name=name, grid=grid, in_specs=in_specs, out_specs=o_spec,
                          out_shape=jax.ShapeDtypeStruct(out_shape, out_dtype),
                          semantics=("parallel", "parallel", "arbitrary"), riders=riders)
    return (out, carried) if riders else out


NN = ((1,), (0,))
NT = ((1,), (1,))
TN = ((0,), (0,))
_TN_TOKENS = 4096


def _mm_cs(name, a, w, out_dtype, riders=()):
    M, K = a.shape
    nq, _, wd = w.shape
    tm = _row_tile(M)
    return _matmul(name, (a, w), grid=(nq, M // tm, 1),
                   in_specs=[pl.BlockSpec((tm, K), lambda j, i, k: (i, 0)),
                             pl.BlockSpec((None, K, wd), lambda j, i, k: (j, 0, 0))],
                   o_spec=pl.BlockSpec((tm, wd), lambda j, i, k: (i, j)),
                   out_shape=(M, nq * wd), out_dtype=out_dtype, dims=NN, riders=riders)


def _mm_rs(name, a, w, out_dtype):
    M, K = a.shape
    N = w.shape[1]
    tm = _row_tile(M)
    return _matmul(name, (a, w), grid=(M // tm, 1, 1),
                   in_specs=[pl.BlockSpec((tm, K), lambda i, j, k: (i, 0)), pl.BlockSpec((K, N), lambda i, j, k: (0, 0))],
                   o_spec=pl.BlockSpec((tm, N), lambda i, j, k: (i, 0)),
                   out_shape=(M, N), out_dtype=out_dtype, dims=NN)


def _mm_nt_rs(name, g, w, out_dtype, riders=()):
    M, N = g.shape
    K = w.shape[0]
    to = K
    tm = _row_tile(M)
    return _matmul(name, (g, w), grid=(M // tm, K // to, 1),
                   in_specs=[pl.BlockSpec((tm, N), lambda i, j, k: (i, 0)),
                             pl.BlockSpec((to, N), lambda i, j, k: (j, 0))],
                   o_spec=pl.BlockSpec((tm, to), lambda i, j, k: (i, j)),
                   out_shape=(M, K), out_dtype=out_dtype, dims=NT, riders=riders)


def _mm_nt_cs(name, g, w, out_dtype, riders=(), stacked=False, norm_bwd=None):
    M = g.shape[-2]
    nq, K, wd = w.shape
    tm = _row_tile(M, 256)

    def product(g_ref, w_ref):
        acc = None
        for q in range(nq):
            gq = g_ref[q // 2, :, (q % 2) * wd:(q % 2 + 1) * wd] if stacked else g_ref[:, q * wd:(q + 1) * wd]
            part = _dot_nt(gq, w_ref[q])
            acc = part if acc is None else acc + part
        return acc

    def body(g_ref, w_ref, o_ref):
        o_ref[...] = product(g_ref, w_ref).astype(o_ref.dtype)

    def body_norm(g_ref, w_ref, x_ref, gain_ref, dr_ref, dx_ref, dg_ref):
        @pl.when(pl.program_id(0) == 0)
        def _():
            dg_ref[...] = jnp.zeros_like(dg_ref)

        dx, dg = _rms_bwd_rows(x_ref[...], gain_ref[...], product(g_ref, w_ref))
        dg_ref[...] += dg
        dx_ref[...] = dx + dr_ref[...]

    g_spec = (pl.BlockSpec((2, tm, 2 * wd), lambda i: (0, i, 0)) if stacked
              else pl.BlockSpec((tm, nq * wd), lambda i: (i, 0)))
    w_spec = pl.BlockSpec((nq, K, wd), lambda i: (0, 0, 0))
    row = pl.BlockSpec((tm, K), lambda i: (i, 0))
    if norm_bwd is None:
        return _carried(*_pcall(
            body, (g, w), name=name, grid=(M // tm,), in_specs=[g_spec, w_spec], out_specs=row,
            out_shape=jax.ShapeDtypeStruct((M, K), out_dtype), semantics=("parallel",), riders=riders), riders)
    vec = pl.BlockSpec((1, K), lambda i: (0, 0))
    return _carried(*_pcall(
        body_norm, (g, w) + tuple(norm_bwd), name=name, grid=(M // tm,),
        in_specs=[g_spec, w_spec, row, vec, row], out_specs=(row, vec),
        out_shape=(jax.ShapeDtypeStruct((M, K), F32), jax.ShapeDtypeStruct((1, K), F32)),
        semantics=("arbitrary",), riders=riders), riders)


def _mm_tn_rs(name, a, g, to, tn=512):
    T, M = a.shape
    N = g.shape[1]
    tt = _row_tile(T, _TN_TOKENS)
    tn = min(tn, N)
    return _matmul(name, (a, g), grid=(M // to, N // tn, T // tt),
                   in_specs=[pl.BlockSpec((tt, to), lambda i, j, k: (k, i)),
                             pl.BlockSpec((tt, tn), lambda i, j, k: (k, j))],
                   o_spec=pl.BlockSpec((to, tn), lambda i, j, k: (i, j)),
                   out_shape=(M, N), out_dtype=F32, dims=TN)


def _mm_tn_cs(name, a, g, nq, to, riders=(), stacked=False):
    T, M = a.shape
    wd = g.shape[-1] * (2 if stacked else 1) // nq
    tt = _row_tile(T, _TN_TOKENS)
    g_spec = (pl.BlockSpec((None, tt, wd), lambda i, j, k: (j // 2, k, j % 2)) if stacked
              else pl.BlockSpec((tt, wd), lambda i, j, k: (k, j)))
    return _matmul(name, (a, g), grid=(M // to, nq, T // tt),
                   in_specs=[pl.BlockSpec((tt, to), lambda i, j, k: (k, i)), g_spec],
                   o_spec=pl.BlockSpec((None, to, wd), lambda i, j, k: (j, i, 0)),
                   out_shape=(nq, M, wd), out_dtype=F32, dims=TN, riders=riders)


def _mm_tn_cs_to_sibling(name, a, g, nq, place, riders=(), send=True):
    T, M = a.shape
    wd = g.shape[-1] // nq
    to = M // 2
    steps = 2 * nq

    def body(s_ref, a_ref, g_ref, o_hbm, land_hbm, acc, wsem, send_sem, recv_sem):
        t = pl.program_id(0)
        c = s_ref[1]

        def writeback(tt):
            half = (tt // nq + 1 + c) % 2
            return pltpu.make_async_copy(acc.at[tt % 2], o_hbm.at[tt % nq, pl.ds(half * to, to), :], wsem.at[tt % 2])

        @pl.when(t >= 2)
        def _():
            writeback(t - 2).wait()

        if send:
            x, y, _, _ = _place()
            to_sibling = _remote(o_hbm.at[:, pl.ds((1 - c) * to, to), :], land_hbm, send_sem.at[0], recv_sem.at[0],
                                 (x, y, 1 - c))

            @pl.when(t == nq + 1)
            def _():
                to_sibling.start()

        acc[t % 2] = _dot_tn(a_ref[...], g_ref[...])
        writeback(t).start()

        @pl.when(t == steps - 1)
        def _():
            writeback(t - 1).wait()
            writeback(t).wait()
            if send:
                to_sibling.wait()

    out, carried = _pcall(
        body, (a, g), name=name, grid=(steps,),
        in_specs=[pl.BlockSpec((T, to), lambda t, s: (0, (t // nq + 1 + s[1]) % 2)),
                  pl.BlockSpec((T, wd), lambda t, s: (0, t % nq))],
        out_specs=(HBM_SPEC, HBM_SPEC),
        out_shape=(jax.ShapeDtypeStruct((nq, M, wd), F32), jax.ShapeDtypeStruct((nq, to, wd), F32)),
        scratch_shapes=[pltpu.VMEM((2, to, wd), F32), pltpu.SemaphoreType.DMA((2,)),
                        pltpu.SemaphoreType.DMA((1,)), pltpu.SemaphoreType.DMA((1,))],
        semantics=("arbitrary",), riders=riders, prefetch=place)
    return _carried(out, carried, riders)


def _rms_fwd(name, x, g, riders=()):
    T, D = x.shape
    tm = _row_tile(T)

    def body(x_ref, g_ref, o_ref):
        o_ref[...] = _rms_rows(x_ref[...], g_ref[...]).astype(o_ref.dtype)

    return _carried(*_pcall(
        body, (x, g), name=name, grid=(T // tm,),
        in_specs=[pl.BlockSpec((tm, D), lambda i: (i, 0)), pl.BlockSpec((1, D), lambda i: (0, 0))],
        out_specs=pl.BlockSpec((tm, D), lambda i: (i, 0)),
        out_shape=jax.ShapeDtypeStruct((T, D), BF16), semantics=("parallel",), riders=riders), riders)


_NORM1_STEPS = 4


def _norm1_and_casts(x, g, shards, place, riders=()):
    T, D = x.shape
    tm = T // _NORM1_STEPS
    names = list(shards)

    def body(s_ref, x_ref, g_ref, *refs):
        w_refs, o_ref, slab_refs = refs[:len(names)], refs[len(names)], refs[len(names) + 1:]
        o_ref[...] = _rms_rows(x_ref[...], g_ref[...]).astype(o_ref.dtype)
        for w_ref, slab_ref in zip(w_refs, slab_refs):
            slab_ref[...] = w_ref[...].astype(slab_ref.dtype)

    in_specs = [pl.BlockSpec((tm, D), lambda i, s: (i, 0)), pl.BlockSpec((1, D), lambda i, s: (0, 0))]
    out_specs = [pl.BlockSpec((tm, D), lambda i, s: (i, 0))]
    out_shape = [jax.ShapeDtypeStruct((T, D), BF16)]
    for n in names:
        r, cc = shards[n].shape
        assert r % (_NORM1_STEPS * 16) == 0
        in_specs.append(pl.BlockSpec((r // _NORM1_STEPS, cc), lambda i, s: (i, 0)))
        out_specs.append(pl.BlockSpec((None, r // _NORM1_STEPS, cc), lambda i, s: (s[0], i, 0)))
        out_shape.append(jax.ShapeDtypeStruct((N_CHIPS, r, cc), BF16))
    out, carried = _pcall(body, (x, g, *[shards[n] for n in names]), name="norm1", grid=(_NORM1_STEPS,),
                          in_specs=in_specs, out_specs=out_specs, out_shape=out_shape, semantics=("parallel",),
                          riders=riders, prefetch=place)
    return _carried((out[0], dict(zip(names, out[1:]))), carried, riders)


def _rms_rows(xv, gain):
    return xv * lax.rsqrt(jnp.mean(xv * xv, axis=-1, keepdims=True) + EPS) * gain


def _rms_bwd_rows(xv, gain, dh):
    r = lax.rsqrt(jnp.mean(xv * xv, axis=-1, keepdims=True) + EPS)
    n = xv * r
    dn = dh * gain
    return r * (dn - n * jnp.mean(dn * n, axis=-1, keepdims=True)), jnp.sum(dh * n, axis=0, keepdims=True)


def _rms_bwd(name, x, g, dh, dres):
    T, D = x.shape
    tm = _row_tile(T)
    has_res = dres is not None

    def body(*refs):
        if has_res:
            x_ref, g_ref, dh_ref, dr_ref, dx_ref, dg_ref = refs
        else:
            x_ref, g_ref, dh_ref, dx_ref, dg_ref = refs

        @pl.when(pl.program_id(0) == 0)
        def _():
            dg_ref[...] = jnp.zeros_like(dg_ref)

        dx, dg = _rms_bwd_rows(x_ref[...], g_ref[...], dh_ref[...])
        dg_ref[...] += dg
        if has_res:
            dx = dx + dr_ref[...]
        dx_ref[...] = dx

    row = pl.BlockSpec((tm, D), lambda i: (i, 0))
    vec = pl.BlockSpec((1, D), lambda i: (0, 0))
    ops = (x, g, dh, dres) if has_res else (x, g, dh)
    return _pallas(
        body, name=name, grid=(T // tm,), in_specs=[row, vec, row] + ([row] if has_res else []),
        out_specs=(row, vec),
        out_shape=(jax.ShapeDtypeStruct((T, D), F32), jax.ShapeDtypeStruct((1, D), F32)),
        compiler_params=_cp("arbitrary"),
    )(*ops)


def _proj_res_norm(name, a, w, res, gain):
    M, K = a.shape
    N = w.shape[1]
    tm = _row_tile(M)

    def body(a_ref, w_ref, r_ref, g_ref, x_ref, h_ref):
        xv = _dot(a_ref[...], w_ref[...]) + r_ref[...]
        x_ref[...] = xv
        h_ref[...] = _rms_rows(xv, g_ref[...]).astype(h_ref.dtype)

    row = pl.BlockSpec((tm, N), lambda i: (i, 0))
    return _pallas(
        body, name=name, grid=(M // tm,),
        in_specs=[pl.BlockSpec((tm, K), lambda i: (i, 0)), pl.BlockSpec((K, N), lambda i: (0, 0)), row,
                  pl.BlockSpec((1, N), lambda i: (0, 0))],
        out_specs=(row, row), out_shape=(jax.ShapeDtypeStruct((M, N), F32), jax.ShapeDtypeStruct((M, N), BF16)),
        compiler_params=_cp("parallel"),
    )(a, w, res, gain)


def _proj_res_loss(name, a, w, res, tgt, gain):
    M, K = a.shape
    D = w.shape[1]
    tm = _row_tile(M)

    def body(a_ref, w_ref, r_ref, t_ref, g_ref, dx_ref, dg_ref, loss_ref):
        @pl.when(pl.program_id(0) == 0)
        def _():
            dg_ref[...] = jnp.zeros_like(dg_ref)
            loss_ref[...] = jnp.zeros_like(loss_ref)

        xv = _dot(a_ref[...], w_ref[...]) + r_ref[...]
        gv = g_ref[...]
        diff = _rms_rows(xv, gv) - t_ref[...]
        loss_ref[...] += 0.5 * jnp.sum(jnp.mean(diff * diff, axis=-1, keepdims=True))
        dx, dg = _rms_bwd_rows(xv, gv, diff * (1.0 / D))
        dg_ref[...] += dg
        dx_ref[...] = dx

    row = pl.BlockSpec((tm, D), lambda i: (i, 0))
    vec = pl.BlockSpec((1, D), lambda i: (0, 0))
    return _pallas(
        body, name=name, grid=(M // tm,),
        in_specs=[pl.BlockSpec((tm, K), lambda i: (i, 0)), pl.BlockSpec((K, D), lambda i: (0, 0)), row, row, vec],
        out_specs=(row, vec, pl.BlockSpec((8, 128), lambda i: (0, 0))),
        out_shape=(jax.ShapeDtypeStruct((M, D), F32), jax.ShapeDtypeStruct((1, D), F32),
                   jax.ShapeDtypeStruct((8, 128), F32)),
        compiler_params=_cp("arbitrary"),
    )(a, w, res, tgt, gain)


def _gmlp_pieces(zu, zv, lng, lnb, ws_ref, bs_ref):
    u, du = _gelu_and_grad(zu)
    v, dv = _gelu_and_grad(zv)
    mu = jnp.mean(v, axis=-1, keepdims=True)
    vc = v - mu
    rstd = lax.rsqrt(jnp.mean(vc * vc, axis=-1, keepdims=True) + EPS)
    vhat = vc * rstd
    vn = vhat * lng + lnb
    row = lax.broadcasted_iota(jnp.int32, (GM_CHUNK, GM_CHUNK), 0)
    col = lax.broadcasted_iota(jnp.int32, (GM_CHUNK, GM_CHUNK), 1)
    tril = row >= col
    wms, mixed = [], []
    for g in range(GM_GROUPS):
        sl = slice(g * 128, (g + 1) * 128)
        wm = jnp.where(tril, ws_ref[g], 0.0)
        wms.append(wm)
        mixed.append(_dot(wm, vn[:, sl]) + bs_ref[g])
    return u, du, dv, rstd, vhat, vn, wms, mixed, tril


def _gmlp_fwd(proj, lng, lnb, ws, bs_col):
    T = proj.shape[0]
    n = T // GM_CHUNK

    def body(zu_ref, zv_ref, lng_ref, lnb_ref, ws_ref, bs_ref, o_ref):
        u, _, _, _, _, _, _, mixed, _ = _gmlp_pieces(zu_ref[...].astype(F32), zv_ref[...].astype(F32),
                                                     lng_ref[...], lnb_ref[...],
                                                     ws_ref, bs_ref)
        for g in range(GM_GROUPS):
            sl = slice(g * 128, (g + 1) * 128)
            o_ref[:, sl] = (u[:, sl] * mixed[g]).astype(o_ref.dtype)

    vec = pl.BlockSpec((1, GM_WIDTH), lambda i: (0, 0))
    return _pallas(
        body, name="gmlp_fwd", grid=(n,),
        in_specs=[pl.BlockSpec((GM_CHUNK, 512), lambda i: (i, COL_ZU)),
                  pl.BlockSpec((GM_CHUNK, 512), lambda i: (i, COL_ZV)),
                  vec, vec,
                  pl.BlockSpec((GM_GROUPS, 128, 128), lambda i: (0, 0, 0)),
                  pl.BlockSpec((GM_GROUPS, 128, 1), lambda i: (0, 0, 0))],
        out_specs=pl.BlockSpec((GM_CHUNK, 512), lambda i: (i, 0)),
        out_shape=jax.ShapeDtypeStruct((T, GM_WIDTH), BF16), compiler_params=_cp("parallel"),
    )(proj, proj, lng, lnb, ws, bs_col)


def _gmlp_bwd(proj, d_out, lng, lnb, ws, bs_col, riders=()):
    T = proj.shape[0]
    n = T // GM_CHUNK

    def body(zu_ref, zv_ref, do_ref, lng_ref, lnb_ref, ws_ref, bs_ref,
             dz_ref, dws_ref, dbs_ref, dlng_ref, dlnb_ref, dm_acc):
        i = pl.program_id(0)

        @pl.when(i == 0)
        def _():
            dws_ref[...] = jnp.zeros_like(dws_ref)
            dlng_ref[...] = jnp.zeros_like(dlng_ref)
            dlnb_ref[...] = jnp.zeros_like(dlnb_ref)
            dm_acc[...] = jnp.zeros_like(dm_acc)

        lng_v = lng_ref[...]
        u, du, dv, rstd, vhat, vn, wms, mixed, tril = _gmlp_pieces(zu_ref[...].astype(F32), zv_ref[...].astype(F32),
                                                                  lng_v, lnb_ref[...],
                                                                  ws_ref, bs_ref)
        do = do_ref[...]
        dvn_parts = []
        for g in range(GM_GROUPS):
            sl = slice(g * 128, (g + 1) * 128)
            dog = do[:, sl]
            dz_ref[:, sl] = (dog * mixed[g] * du[:, sl]).astype(dz_ref.dtype)
            dmix = dog * u[:, sl]
            dm_acc[:, sl] += dmix
            dws_ref[g] += jnp.where(tril, _dot_nt(dmix, vn[:, sl]), 0.0)
            dvn_parts.append(_dot_tn(wms[g], dmix))
        dvn = jnp.concatenate(dvn_parts, axis=1)
        dlng_ref[...] += jnp.sum(dvn * vhat, axis=0, keepdims=True)
        dlnb_ref[...] += jnp.sum(dvn, axis=0, keepdims=True)
        dvh = dvn * lng_v
        dvv = rstd * (dvh - jnp.mean(dvh, axis=-1, keepdims=True)
                      - vhat * jnp.mean(dvh * vhat, axis=-1, keepdims=True))
        dz_ref[:, GM_WIDTH:] = (dvv * dv).astype(dz_ref.dtype)

        @pl.when(i == n - 1)
        def _():
            for g in range(GM_GROUPS):
                dbs_ref[g] = jnp.sum(dm_acc[:, g * 128:(g + 1) * 128], axis=1, keepdims=True)

    vec = pl.BlockSpec((1, GM_WIDTH), lambda i: (0, 0))
    wsp = pl.BlockSpec((GM_GROUPS, 128, 128), lambda i: (0, 0, 0))
    bsp = pl.BlockSpec((GM_GROUPS, 128, 1), lambda i: (0, 0, 0))
    return _carried(*_pcall(
        body, (proj, proj, d_out, lng, lnb, ws, bs_col), name="gmlp_bwd", grid=(n,),
        in_specs=[pl.BlockSpec((GM_CHUNK, 512), lambda i: (i, COL_ZU)),
                  pl.BlockSpec((GM_CHUNK, 512), lambda i: (i, COL_ZV)),
                  pl.BlockSpec((None, GM_CHUNK, 512), lambda i: (0, i, 0)), vec, vec, wsp, bsp],
        out_specs=(pl.BlockSpec((GM_CHUNK, 2 * GM_WIDTH), lambda i: (i, 0)), wsp, bsp, vec, vec),
        out_shape=(jax.ShapeDtypeStruct((T, 2 * GM_WIDTH), BF16),
                   jax.ShapeDtypeStruct((GM_GROUPS, 128, 128), F32), jax.ShapeDtypeStruct((GM_GROUPS, 128, 1), F32),
                   jax.ShapeDtypeStruct((1, GM_WIDTH), F32), jax.ShapeDtypeStruct((1, GM_WIDTH), F32)),
        scratch_shapes=[pltpu.VMEM((GM_CHUNK, GM_WIDTH), F32)],
        semantics=("arbitrary",), riders=riders), riders)


def _hgrn_lower_bound(lbl):
    return 1.0 / (1.0 + jnp.exp(lbl[1:2, :] - lbl[0:1, :]))


def _hgrn_gates(hq, hf, lb):
    C = HG_CHUNK
    sg = _sigmoid(hf)
    fg = lb + (1.0 - lb) * sg
    sq = _sigmoid(hq)
    row = lax.broadcasted_iota(jnp.int32, (C, C), 0)
    col = lax.broadcasted_iota(jnp.int32, (C, C), 1)
    tril = row >= col
    logf = jnp.log(fg)
    a = _dot_01(tril, logf)
    a_last = jnp.sum(logf, axis=0, keepdims=True)
    first_half = lax.broadcasted_iota(jnp.int32, logf.shape, 0) < (C // 2)
    a_mid = jnp.sum(jnp.where(first_half, logf, 0.0), axis=0, keepdims=True)
    ea, ei, eki, ekl = jnp.exp(a), jnp.exp(a - a_mid), jnp.exp(a_mid - a), jnp.exp(a_last - a)
    k = 1.0 - fg
    q = hq * sq
    qi = (q * ei).astype(BF16).astype(F32)
    ki = (k * eki).astype(BF16).astype(F32)
    return dict(sg=sg, fg=fg, sq=sq, tril=tril, ea=ea, ei=ei, eki=eki, ekl=ekl, e_last=jnp.exp(a_last),
                qe=q * ea, qi=qi, ki=ki, kl=k * ekl)


def _heads(x):
    return [x[:, h * HG_DIM:(h + 1) * HG_DIM] for h in range(HG_HEADS)]


def _hgrn_fwd(proj, lbl, gh, B, S, riders=()):
    C = HG_CHUNK
    NC = S // C
    W = HG_HEADS * HG_DIM

    def body(q_ref, f_ref, i_ref, g_ref, lbl_ref, gh_ref, o_ref, bo_ref, st_ref, state):
        @pl.when(pl.program_id(0) == 0)
        def _():
            state[...] = jnp.zeros_like(state)

        lb = _hgrn_lower_bound(lbl_ref[...])
        ghv = gh_ref[...]
        for b in range(B):
            gt = _hgrn_gates(q_ref[b].astype(F32), f_ref[b].astype(F32), lb)
            v = _heads(i_ref[b])
            qe, qi, ki, kl, e_last = (_heads(gt[n]) for n in ("qe", "qi", "ki", "kl", "e_last"))
            outs, normed = [], []
            for h in range(HG_HEADS):
                p = jnp.where(gt["tril"], _dot_nt(qi[h], ki[h]), 0.0)
                st = state[b, h]
                st_ref[b, h] = st
                o = _dot_nt(qe[h], st) + _dot(p, v[h])
                state[b, h] = st * e_last[h] + _dot_tn(v[h], kl[h])
                outs.append(o)
                normed.append(o * lax.rsqrt(jnp.mean(o * o, axis=-1, keepdims=True) + EPS) * ghv)
            o_ref[b] = jnp.concatenate(outs, axis=1)
            hg = g_ref[b].astype(F32)
            bo_ref[b] = (jnp.concatenate(normed, axis=1) * (hg * _sigmoid(hg))).astype(bo_ref.dtype)

    def col(cb):
        return pl.BlockSpec((B, C, 512), lambda c: (0, c, cb))

    tile = pl.BlockSpec((B, C, W), lambda c: (0, c, 0))
    proj3 = proj.reshape(B, S, proj.shape[-1])
    out, carried = _pcall(
        body, (proj3, proj3, proj3, proj3, lbl, gh), name="hgrn_fwd", grid=(NC,),
        in_specs=[col(COL_HQ), col(COL_HF), col(COL_HI), col(COL_HG),
                  pl.BlockSpec((2, W), lambda c: (0, 0)), pl.BlockSpec((1, HG_DIM), lambda c: (0, 0))],
        out_specs=(tile, tile, pl.BlockSpec((B, None, HG_HEADS, 128, 128), lambda c: (0, c, 0, 0, 0))),
        out_shape=(jax.ShapeDtypeStruct((B, S, W), F32), jax.ShapeDtypeStruct((B, S, W), BF16),
                   jax.ShapeDtypeStruct((B, NC, HG_HEADS, 128, 128), F32)),
        scratch_shapes=[pltpu.VMEM((B, HG_HEADS, 128, 128), F32)],
        semantics=("arbitrary",), riders=riders)
    o_h, b_out, states = out
    out = (o_h, b_out.reshape(B * S, W), states)
    return (out, carried) if riders else out


def _hgrn_bwd(proj, o_saved, states, d_out, lbl, gh, others, B, S, riders=()):
    C = HG_CHUNK
    NC = S // C
    W = HG_HEADS * HG_DIM
    d_gm, d_xq, d_gates = (t.reshape(B, S, t.shape[-1]) for t in others)
    own0 = d_gm.shape[-1]
    xq0 = own0 + 4 * W
    gates0 = xq0 + d_xq.shape[-1]

    def body(q_ref, f_ref, i_ref, g_ref, o_ref, st_ref, do_ref, lbl_ref, gh_ref, gm_ref, xq_ref, gates_ref,
             d_ref, dlbl_ref, dgh_ref, dstate, dlb_acc):
        c = pl.program_id(0)
        d_ref[:, :, :own0] = gm_ref[...]
        d_ref[:, :, xq0:gates0] = xq_ref[...]
        d_ref[:, :, gates0:] = gates_ref[...]

        def put(b, k, val):
            d_ref[b, :, own0 + k * W:own0 + (k + 1) * W] = val.astype(d_ref.dtype)

        @pl.when(c == 0)
        def _():
            dstate[...] = jnp.zeros_like(dstate)
            dgh_ref[...] = jnp.zeros_like(dgh_ref)
            dlb_acc[...] = jnp.zeros_like(dlb_acc)

        lb = _hgrn_lower_bound(lbl_ref[...])
        ghv = gh_ref[...]
        row = lax.broadcasted_iota(jnp.int32, (C, C), 0)
        colm = lax.broadcasted_iota(jnp.int32, (C, C), 1)
        triu = colm >= row
        for b in range(B):
            hq, hg = q_ref[b].astype(F32), g_ref[b].astype(F32)
            gt = _hgrn_gates(hq, f_ref[b].astype(F32), lb)
            tril = gt["tril"]
            v = _heads(i_ref[b])
            qe, qi, ki, kl, e_last = (_heads(gt[n]) for n in ("qe", "qi", "ki", "kl", "e_last"))
            sgg = _sigmoid(hg)
            don_all = do_ref[b] * (hg * sgg)
            o, don = _heads(o_ref[b]), _heads(don_all)
            d_qe, d_qi, d_ki, d_kl, dv, n_all, dal = [], [], [], [], [], [], []
            for h in range(HG_HEADS):
                r = lax.rsqrt(jnp.mean(o[h] * o[h], axis=-1, keepdims=True) + EPS)
                n = o[h] * r
                n_all.append(n)
                dgh_ref[...] += jnp.sum(don[h] * n, axis=0, keepdims=True)
                dn = don[h] * ghv
                d_o = r * (dn - n * jnp.mean(dn * n, axis=-1, keepdims=True))
                st, dst = st_ref[b, h], dstate[b, h]
                p = jnp.where(tril, _dot_nt(qi[h], ki[h]), 0.0)
                dp = jnp.where(tril, _dot_nt(d_o, v[h]), 0.0)
                d_qe.append(_dot(d_o, st))
                d_qi.append(_dot(dp, ki[h]))
                d_ki.append(_dot_tn(dp, qi[h]))
                d_kl.append(_dot(v[h], dst))
                dv.append(_dot_tn(p, d_o) + _dot_nt(kl[h], dst))
                dstate[b, h] = dst * e_last[h] + _dot_tn(d_o, qe[h])
                dal.append(jnp.sum(dst * st, axis=0, keepdims=True) * e_last[h])
            d_qe, d_qi, d_ki, d_kl, n_all, dal = (jnp.concatenate(t, axis=1)
                                                  for t in (d_qe, d_qi, d_ki, d_kl, n_all, dal))
            put(b, 3, do_ref[b] * n_all * jnp.tile(ghv, (1, HG_HEADS)) * (sgg * (1.0 + hg * (1.0 - sgg))))
            put(b, 2, jnp.concatenate(dv, axis=1))
            d_a_last = dal + jnp.sum(d_kl * gt["kl"], axis=0, keepdims=True)
            dq = d_qe * gt["ea"] + d_qi * gt["ei"]
            dk = d_ki * gt["eki"] + d_kl * gt["ekl"]
            da = d_qe * gt["qe"] + d_qi * gt["qi"] - d_ki * gt["ki"] - d_kl * gt["kl"]
            dlogf = _dot_01(triu, da) + d_a_last
            sg, sq = gt["sg"], gt["sq"]
            dfg = dlogf / gt["fg"] - dk
            put(b, 1, dfg * (1.0 - lb) * sg * (1.0 - sg))
            dlb_acc[...] += jnp.sum(dfg * (1.0 - sg), axis=0, keepdims=True)
            put(b, 0, dq * (sq * (1.0 + hq * (1.0 - sq))))

        @pl.when(c == NC - 1)
        def _():
            dlb = dlb_acc[...]
            first = lax.broadcasted_iota(jnp.int32, (2, W), 0) == 0
            dlbl_ref[...] = jnp.where(first, dlb * lb * (1.0 - lb), -dlb * lb * (1.0 - lb))

    def col(cb):
        return pl.BlockSpec((B, C, 512), lambda c: (0, NC - 1 - c, cb))

    tile = pl.BlockSpec((B, C, W), lambda c: (0, NC - 1 - c, 0))
    proj3 = proj.reshape(B, S, proj.shape[-1])

    def rows(width):
        return pl.BlockSpec((B, C, width), lambda c: (0, NC - 1 - c, 0))

    width = proj.shape[-1]
    out, carried = _pcall(
        body, (proj3, proj3, proj3, proj3, o_saved, states, d_out.reshape(3, B, S, W), lbl, gh, d_gm, d_xq, d_gates),
        name="hgrn_bwd", grid=(NC,),
        in_specs=[col(COL_HQ), col(COL_HF), col(COL_HI), col(COL_HG), tile,
                  pl.BlockSpec((B, None, HG_HEADS, 128, 128), lambda c: (0, NC - 1 - c, 0, 0, 0)),
                  pl.BlockSpec((None, B, C, W), lambda c: (1, 0, NC - 1 - c, 0)),
                  pl.BlockSpec((2, W), lambda c: (0, 0)), pl.BlockSpec((1, HG_DIM), lambda c: (0, 0)),
                  rows(d_gm.shape[-1]), rows(d_xq.shape[-1]), rows(d_gates.shape[-1])],
        out_specs=(rows(width), pl.BlockSpec((2, W), lambda c: (0, 0)), pl.BlockSpec((1, HG_DIM), lambda c: (0, 0))),
        out_shape=(jax.ShapeDtypeStruct((B, S, width), BF16), jax.ShapeDtypeStruct((2, W), F32),
                   jax.ShapeDtypeStruct((1, HG_DIM), F32)),
        scratch_shapes=[pltpu.VMEM((B, HG_HEADS, 128, 128), F32), pltpu.VMEM((1, W), F32)],
        semantics=("arbitrary",), riders=riders)
    out = (out[0].reshape(B * S, width),) + tuple(out[1:])
    return (out, carried) if riders else out


_XA_SCALE = XA_DIM ** -0.5


def _attn_probs(qh, kh):
    s = _dot_nt(qh, kh) * _XA_SCALE
    e = jnp.exp(s - jnp.max(s, axis=-1, keepdims=True))
    return e / jnp.sum(e, axis=-1, keepdims=True)


def _attn_fwd(proj, kv, B, S):
    T = B * S
    tq = _row_tile(S)
    nq = S // tq
    W = XA_HEADS * XA_DIM

    def body(q_ref, kv_ref, o_ref):
        for h in range(XA_HEADS):
            sl = slice(h * 128, (h + 1) * 128)
            p = _attn_probs(q_ref[:, sl], kv_ref[:, sl])
            o_ref[:, sl] = _dot(p, kv_ref[:, W + h * 128:W + (h + 1) * 128]).astype(o_ref.dtype)

    return _pallas(
        body, name="attn_fwd", grid=(B, nq),
        in_specs=[pl.BlockSpec((tq, 512), lambda b, i: (b * nq + i, COL_XQ)),
                  pl.BlockSpec((MEM_LEN, 2 * W), lambda b, i: (b, 0))],
        out_specs=pl.BlockSpec((tq, W), lambda b, i: (b * nq + i, 0)),
        out_shape=jax.ShapeDtypeStruct((T, W), BF16), compiler_params=_cp("parallel", "parallel"),
    )(proj, kv)


def _attn_bwd(proj, kv, d_out, B, S):
    T = B * S
    tq = _row_tile(S)
    nq = S // tq
    W = XA_HEADS * XA_DIM

    def body(q_ref, kv_ref, do_ref, dq_ref, dkv_ref):
        @pl.when(pl.program_id(1) == 0)
        def _():
            dkv_ref[...] = jnp.zeros_like(dkv_ref)

        for h in range(XA_HEADS):
            sl = slice(h * 128, (h + 1) * 128)
            slv = slice(W + h * 128, W + (h + 1) * 128)
            qh = q_ref[:, sl]
            kh = kv_ref[:, sl]
            p = _attn_probs(qh, kh)
            dc = do_ref[:, sl]
            dp = _dot_nt(dc, kv_ref[:, slv])
            ds = p * (dp - jnp.sum(dp * p, axis=-1, keepdims=True)) * _XA_SCALE
            dq_ref[:, sl] = _dot(ds, kh).astype(dq_ref.dtype)
            dkv_ref[:, sl] += _dot_tn(ds, qh)
            dkv_ref[:, slv] += _dot_tn(p, dc)

    kvspec = pl.BlockSpec((MEM_LEN, 2 * W), lambda b, i: (b, 0))
    tile = pl.BlockSpec((tq, W), lambda b, i: (b * nq + i, 0))
    return _pallas(
        body, name="attn_bwd", grid=(B, nq),
        in_specs=[pl.BlockSpec((tq, 512), lambda b, i: (b * nq + i, COL_XQ)), kvspec,
                  pl.BlockSpec((None, tq, W), lambda b, i: (2, b * nq + i, 0))],
        out_specs=(tile, kvspec),
        out_shape=(jax.ShapeDtypeStruct((T, W), BF16), jax.ShapeDtypeStruct((B * MEM_LEN, 2 * W), F32)),
        compiler_params=_cp("parallel", "arbitrary"),
    )(proj, kv, d_out)


_MERGE_TM = 256
_GATE_W = 512


def _gate_specs(tm):
    base = COL_GATE0 // _GATE_W
    return [pl.BlockSpec((tm, _GATE_W), functools.partial(lambda i, k: (i, base + k), k=k)) for k in range(6)]


def _merge_fwd(a_out, b_out, c_out, wb, proj, riders=()):
    T = a_out.shape[0]
    tm = _row_tile(T, _MERGE_TM)
    nq, _, wd = wb.shape
    per_half = _GATE_W // wd

    def body(a_ref, b_ref, c_ref, w_ref, *rest):
        gates, (m_ref, up_ref) = rest[:6], rest[6:]
        for hf in range(2):
            cols = slice(hf * _GATE_W, (hf + 1) * _GATE_W)
            acc = None
            for n, br in enumerate((a_ref, b_ref, c_ref)):
                x = br[...]
                up = jnp.concatenate([_dot(x, w_ref[per_half * hf + j, n * BR_WIDTH:(n + 1) * BR_WIDTH, :])
                                      for j in range(per_half)], axis=1)
                up_ref[n, :, cols] = up.astype(up_ref.dtype)
                term = _sigmoid(gates[2 * n + hf][...].astype(F32)) * up
                acc = term if acc is None else acc + term
            m_ref[:, cols] = acc.astype(m_ref.dtype)

    br_spec = pl.BlockSpec((tm, BR_WIDTH), lambda i: (i, 0))
    return _carried(*_pcall(
        body, (a_out, b_out, c_out, wb, *([proj] * 6)), name="merge_fwd", grid=(T // tm,),
        in_specs=[br_spec, br_spec, br_spec,
                  pl.BlockSpec((nq, 3 * BR_WIDTH, wd), lambda i: (0, 0, 0))] + _gate_specs(tm),
        out_specs=(pl.BlockSpec((tm, D_MODEL), lambda i: (i, 0)), pl.BlockSpec((3, tm, D_MODEL), lambda i: (0, i, 0))),
        out_shape=(jax.ShapeDtypeStruct((T, D_MODEL), BF16), jax.ShapeDtypeStruct((3, T, D_MODEL), BF16)),
        semantics=("parallel",), riders=riders), riders)


def _branch_bwd_act(d_ups, wb, riders=()):
    _, T, D = d_ups.shape
    nq, _, wd = wb.shape
    tm = _row_tile(T)

    def body(d_ref, w_ref, o_ref):
        acc = None
        for q in range(nq):
            part = _dot_nt(d_ref[:, q * wd:(q + 1) * wd], w_ref[q])
            acc = part if acc is None else acc + part
        o_ref[...] = acc

    return _carried(*_pcall(
        body, (d_ups, wb), name="d_branch", grid=(3, T // tm),
        in_specs=[pl.BlockSpec((None, tm, D), lambda n, i: (n, i, 0)),
                  pl.BlockSpec((nq, BR_WIDTH, wd), lambda n, i: (0, n, 0))],
        out_specs=pl.BlockSpec((None, tm, BR_WIDTH), lambda n, i: (n, i, 0)),
        out_shape=jax.ShapeDtypeStruct((3, T, BR_WIDTH), F32), semantics=("parallel", "parallel"),
        riders=riders), riders)


def _branch_bwd_weight(name, br, d_ups, n, into=None):
    T = br.shape[0]
    D = d_ups.shape[2]
    wd = D // N_CHIPS
    tt = _row_tile(T, _TN_TOKENS)
    n_br = d_ups.shape[0]

    def body(b_ref, d_ref, *rest):
        o_ref = rest[-1]
        k = pl.program_id(0)
        for q in range(N_CHIPS):
            part = _dot_tn(b_ref[...], d_ref[:, q * wd:(q + 1) * wd])

            @pl.when(k == 0)
            def _():
                o_ref[q] = part

            @pl.when(k > 0)
            def _():
                o_ref[q] += part

    return _pallas(
        body, name=name, grid=(T // tt,),
        in_specs=[pl.BlockSpec((tt, BR_WIDTH), lambda k: (k, 0)),
                  pl.BlockSpec((None, tt, D), lambda k: (n, k, 0))] + ([] if into is None else [HBM_SPEC]),
        out_specs=pl.BlockSpec((N_CHIPS, BR_WIDTH, wd), lambda k: (0, n, 0)),
        out_shape=jax.ShapeDtypeStruct((N_CHIPS, n_br * BR_WIDTH, wd), F32),
        input_output_aliases={} if into is None else {2: 0}, compiler_params=_cp("arbitrary"),
    )(br, d_ups, *(() if into is None else (into,)))


def _merge_bwd(d_merged, ups, proj, riders=()):
    T = d_merged.shape[0]
    tm = _row_tile(T, _MERGE_TM)

    def body(dm_ref, up_ref, *rest):
        gates, (dup_ref, dg_ref) = rest[:6], rest[6:]
        for hf in range(2):
            cols = slice(hf * _GATE_W, (hf + 1) * _GATE_W)
            dm = dm_ref[:, cols]
            for n in range(3):
                gate = _sigmoid(gates[2 * n + hf][...].astype(F32))
                dup_ref[n, :, cols] = (dm * gate).astype(dup_ref.dtype)
                dg_ref[:, n * D_MODEL + hf * _GATE_W:n * D_MODEL + (hf + 1) * _GATE_W] = (
                    dm * up_ref[n, :, cols].astype(F32) * gate * (1.0 - gate)).astype(dg_ref.dtype)

    tile = pl.BlockSpec((tm, D_MODEL), lambda i: (i, 0))
    tile3 = pl.BlockSpec((3, tm, D_MODEL), lambda i: (0, i, 0))
    return _carried(*_pcall(
        body, (d_merged, ups, *([proj] * 6)), name="merge_bwd", grid=(T // tm,),
        in_specs=[tile, tile3] + _gate_specs(tm),
        out_specs=(tile3, pl.BlockSpec((tm, 3 * D_MODEL), lambda i: (i, 0))),
        out_shape=(jax.ShapeDtypeStruct((3, T, D_MODEL), BF16), jax.ShapeDtypeStruct((T, 3 * D_MODEL), BF16)),
        semantics=("parallel",), riders=riders), riders)


_CONV_TF = D_FF // 2
_CONV_TS = 256
_HALO = 16


def _conv_fwd(ab, cw, cb, B, S):
    T = B * S
    ts = _row_tile(S, _CONV_TS)
    tf = _CONV_TF
    nb = D_FF // tf
    tps = S // ts
    hb = ts // _HALO

    def body(a_ref, p_ref, b_ref, w_ref, cb_ref, o_ref):
        start = (pl.program_id(0) % tps) == 0
        a = a_ref[...].astype(F32)
        prev = jnp.where(start, 0.0, p_ref[...].astype(F32))
        ext = jnp.concatenate([prev, a], axis=0)
        a1 = pltpu.roll(ext, 1, 0)[_HALO:, :]
        a2 = pltpu.roll(ext, 2, 0)[_HALO:, :]
        ac = cb_ref[...] + w_ref[0] * a2 + w_ref[1] * a1 + w_ref[2] * a
        o_ref[...] = (ac * _sigmoid(ac) * b_ref[...].astype(F32)).astype(o_ref.dtype)

    return _pallas(
        body, name="conv_fwd", grid=(T // ts, nb),
        in_specs=[pl.BlockSpec((ts, tf), lambda i, j: (i, j)),
                  pl.BlockSpec((_HALO, tf), lambda i, j: (jnp.maximum(i * hb - 1, 0), j)),
                  pl.BlockSpec((ts, tf), lambda i, j: (i, j + nb)),
                  pl.BlockSpec((3, 1, tf), lambda i, j: (0, 0, j)),
                  pl.BlockSpec((1, tf), lambda i, j: (0, j))],
        out_specs=pl.BlockSpec((ts, tf), lambda i, j: (i, j)),
        out_shape=jax.ShapeDtypeStruct((T, D_FF), BF16), compiler_params=_cp("parallel", "parallel"),
    )(ab, ab, ab, cw, cb)


def _conv_bwd(ab, d_ff, cw, cb, B, S, riders=()):
    T = B * S
    ts = _row_tile(S, _CONV_TS)
    tf = _CONV_TF
    nb = D_FF // tf
    tps = S // ts
    hb = ts // _HALO
    last_h = T // _HALO - 1
    n_ext = ts + _HALO

    def body(a_ref, ap_ref, an_ref, b_ref, bn_ref, d_ref, dn_ref, w_ref, cb_ref, dab_ref, dw_ref, dcb_ref):
        i = pl.program_id(1)

        @pl.when(i == 0)
        def _():
            dw_ref[...] = jnp.zeros_like(dw_ref)
            dcb_ref[...] = jnp.zeros_like(dcb_ref)

        start = (i % tps) == 0
        end = (i % tps) == tps - 1
        a = a_ref[...].astype(F32)
        ext = jnp.concatenate([jnp.where(start, 0.0, ap_ref[...].astype(F32)), a, an_ref[...].astype(F32)], axis=0)
        r1 = pltpu.roll(ext, 1, 0)[_HALO:, :]
        r2 = pltpu.roll(ext, 2, 0)[_HALO:, :]
        ac = cb_ref[...] + w_ref[0] * r2 + w_ref[1] * r1 + w_ref[2] * ext[_HALO:, :]
        sg = _sigmoid(ac)
        d_e = jnp.concatenate([d_ref[...].astype(F32), jnp.where(end, 0.0, dn_ref[...].astype(F32))], axis=0)
        b_e = jnp.concatenate([b_ref[...].astype(F32), bn_ref[...].astype(F32)], axis=0)
        dab_ref[1] = (d_e[:ts, :] * (ac * sg)[:ts, :]).astype(dab_ref.dtype)
        dac = d_e * b_e * sg * (1.0 + ac * (1.0 - sg))
        u1 = pltpu.roll(dac, n_ext - 1, 0)[:ts, :]
        u2 = pltpu.roll(dac, n_ext - 2, 0)[:ts, :]
        dac0 = dac[:ts, :]
        dab_ref[0] = (w_ref[2] * dac0 + w_ref[1] * u1 + w_ref[0] * u2).astype(dab_ref.dtype)
        dcb_ref[...] += jnp.sum(dac0, axis=0, keepdims=True)
        dw_ref[2] += jnp.sum(dac0 * a, axis=0, keepdims=True)
        dw_ref[1] += jnp.sum(dac0 * r1[:ts, :], axis=0, keepdims=True)
        dw_ref[0] += jnp.sum(dac0 * r2[:ts, :], axis=0, keepdims=True)

    def cur(off):
        return pl.BlockSpec((ts, tf), lambda j, i: (i, j + off))

    def nxt(off):
        return pl.BlockSpec((_HALO, tf), lambda j, i: (jnp.minimum((i + 1) * hb, last_h), j + off))

    return _carried(*_pcall(
        body, (ab, ab, ab, ab, ab, d_ff, d_ff, cw, cb), name="conv_bwd", grid=(nb, T // ts),
        in_specs=[cur(0), pl.BlockSpec((_HALO, tf), lambda j, i: (jnp.maximum(i * hb - 1, 0), j)), nxt(0),
                  cur(nb), nxt(nb), cur(0), nxt(0),
                  pl.BlockSpec((3, 1, tf), lambda j, i: (0, 0, j)), pl.BlockSpec((1, tf), lambda j, i: (0, j))],
        out_specs=(pl.BlockSpec((2, ts, tf), lambda j, i: (0, i, j)), pl.BlockSpec((3, 1, tf), lambda j, i: (0, 0, j)),
                   pl.BlockSpec((1, tf), lambda j, i: (0, j))),
        out_shape=(jax.ShapeDtypeStruct((2, T, D_FF), BF16),
                   jax.ShapeDtypeStruct((3, 1, D_FF), F32), jax.ShapeDtypeStruct((1, D_FF), F32)),
        semantics=("parallel", "arbitrary"), riders=riders), riders)


def _local_step(x, mem, tgt, p, comm, B, S):
    g = {}
    h, slabs = comm.carry(
        "norm1", lambda r: _norm1_and_casts(x, p["norm1_g"], p["cast_beside_norm1"], comm.place, riders=r))
    comm.slabs.update(slabs)
    proj = comm.carry("in_proj", lambda r: _mm_cs("in_proj", h, comm.w("w_in"), BF16, riders=r))
    a_out = _gmlp_fwd(proj, p["ln_v_g"], p["ln_v_b"], p["w_spatial"], p["b_spatial"])
    o_h, b_out, states = comm.carry(
        "hgrn_fwd", lambda r: _hgrn_fwd(proj, p["lb_logits"], p["hgrn_norm_g"], B, S, riders=r))
    memn = _rms_fwd("mem_norm", mem, p["mem_norm_g"])
    kv = _mm_rs("mem_kv", memn, comm.w("w_mem_kv"), F32)
    c_out = _attn_fwd(proj, kv, B, S)
    merged, ups = comm.carry(
        "merge_fwd", lambda r: _merge_fwd(a_out, b_out, c_out, comm.w("w_branch"), proj, riders=r))
    x1, h2 = _proj_res_norm("out_proj_norm2", merged, comm.w("w_out"), x, p["norm2_g"])
    ab = comm.carry("up_proj", lambda r: _mm_cs("up_proj", h2, comm.w("w_up"), BF16, riders=r))
    conv_w = comm.w("conv_w")
    ff = _conv_fwd(ab, conv_w, p["conv_b"], B, S)
    dx2, g["final_g"], loss = _proj_res_loss("down_proj_loss", ff, comm.w("w_down"), x1, tgt, p["final_g"])

    comm.grad("w_down", _mm_tn_rs("g_w_down", ff, dx2, to=D_FF // 2))
    d_ff = comm.carry("d_ff", lambda r: _mm_nt_rs("d_ff", dx2, comm.w("w_down"), BF16, riders=r))
    d_ab, g["conv_w"], g["conv_b"] = comm.carry(
        "conv_bwd", lambda r: _conv_bwd(ab, d_ff, conv_w, p["conv_b"], B, S, riders=r))
    comm.grad("w_up", _mm_tn_cs("g_w_up", h2, d_ab, N_CHIPS, to=512, stacked=True))
    d_x1, g["norm2_g"] = comm.carry("d_h2", lambda r: _mm_nt_cs(
        "d_h2_norm2_bwd", d_ab, comm.w("w_up"), F32, riders=r, stacked=True, norm_bwd=(x1, p["norm2_g"], dx2)))
    comm.grad("w_out", _mm_tn_rs("g_w_out", merged, d_x1, to=512))
    d_merged = _mm_nt_rs("d_merged", d_x1, comm.w("w_out"), F32)
    d_ups, d_gates = comm.carry("merge_bwd", lambda r: _merge_bwd(d_merged, ups, proj, riders=r))

    d_br = comm.carry("d_branch", lambda r: _branch_bwd_act(d_ups, comm.w("w_branch"), riders=r))
    g_branch = None
    for n, br in enumerate((a_out, b_out, c_out)):
        g_branch = _branch_bwd_weight("g_w_branch%d" % n, br, d_ups, n, into=g_branch)
    comm.grad("w_branch", g_branch)

    d_gm, g["w_spatial"], g["b_spatial"], g["ln_v_g"], g["ln_v_b"] = comm.carry(
        "gmlp_bwd", lambda r: _gmlp_bwd(proj, d_br, p["ln_v_g"], p["ln_v_b"], p["w_spatial"], p["b_spatial"],
                                        riders=r))
    d_xq, d_kv = _attn_bwd(proj, kv, d_br, B, S)
    comm.grad("w_mem_kv", _mm_tn_rs("g_w_mem_kv", memn, d_kv, to=512))
    d_memn = _mm_nt_rs("d_memn", d_kv, comm.w("w_mem_kv"), F32)
    _, g["mem_norm_g"] = _rms_bwd("mem_norm_bwd", mem, p["mem_norm_g"], d_memn, None)
    d_proj, g["lb_logits"], g["hgrn_norm_g"] = comm.carry(
        "hgrn_bwd", lambda r: _hgrn_bwd(proj, o_h, states, d_br, p["lb_logits"], p["hgrn_norm_g"],
                                        (d_gm, d_xq, d_gates), B, S, riders=r))
    comm.small_grads([g[n].reshape(_SMALL_SHAPE[n]) for n in _SMALL_EARLY] + [loss])
    comm.grad("w_in", *comm.carry("g_w_in", lambda r: _mm_tn_cs_to_sibling(
        "g_w_in", h, d_proj, N_CHIPS, comm.place, riders=r, send=comm.sends)))
    grad_x, g["norm1_g"] = comm.carry("d_h", lambda r: _mm_nt_cs(
        "d_h_norm1_bwd", d_proj, comm.w("w_in"), F32, riders=r, norm_bwd=(x, p["norm1_g"], d_x1)))
    return loss, grad_x, g


HBM_SPEC = pl.BlockSpec(memory_space=pltpu.HBM)


def _place():
    x, y, c = lax.axis_index("x"), lax.axis_index("y"), lax.axis_index("c")
    other_chips = [(1 - x, y), (x, 1 - y), (1 - x, 1 - y)]
    return x, y, c, other_chips


def _remote(src, dst, send_sem, recv_sem, dev):
    return pltpu.make_async_remote_copy(src_ref=src, dst_ref=dst, send_sem=send_sem, recv_sem=recv_sem,
                                        device_id=dev, device_id_type=MESH_ID)


class _Exchange:
    def __init__(self, operands, out_shape, aliases, scratch, start, finish, mid=None, mid_at=0.5):
        self.operands, self.out_shape, self.aliases, self.scratch = operands, out_shape, aliases, scratch
        self.start, self.finish, self.mid, self.mid_at = start, finish, mid, mid_at


def _run_exchanges(name, exs):
    n_in = [len(ex.operands) for ex in exs]
    n_out = [len(ex.out_shape) for ex in exs]
    n_scr = [len(ex.scratch) for ex in exs]

    def body(*refs):
        ins, outs, scr = refs[:sum(n_in)], refs[sum(n_in):sum(n_in) + sum(n_out)], refs[sum(n_in) + sum(n_out):]
        parts, oi, oo, os_ = [], 0, 0, 0
        for k in range(len(exs)):
            parts.append((ins[oi:oi + n_in[k]], outs[oo:oo + n_out[k]], scr[os_:os_ + n_scr[k]]))
            oi, oo, os_ = oi + n_in[k], oo + n_out[k], os_ + n_scr[k]
        for ex, part in zip(exs, parts):
            ex.start(*part)
        for ex, part in zip(exs, parts):
            if ex.mid is not None:
                ex.mid(*part)
        for ex, part in zip(exs, parts):
            ex.finish(*part)

    aliases, ops, shapes, scratch, oi, oo = {}, [], [], [], 0, 0
    for k, ex in enumerate(exs):
        aliases.update({oi + a: oo + b for a, b in ex.aliases.items()})
        oi, oo = oi + n_in[k], oo + n_out[k]
        ops += list(ex.operands)
        shapes += [pltpu.HBM(s.shape, s.dtype) for s in ex.out_shape]
        scratch += list(ex.scratch)
    res = _pallas(
        body, name=name, in_specs=[HBM_SPEC] * len(ops), out_specs=(HBM_SPEC,) * len(shapes), out_shape=tuple(shapes),
        input_output_aliases=aliases, scratch_shapes=scratch,
    )(*ops)
    out, oo = [], 0
    for k in range(len(exs)):
        out.append(list(res[oo:oo + n_out[k]]))
        oo += n_out[k]
    return out


def _ex_all_gather(slabs, halved, part=(0, 1)):
    n = len(slabs)

    def rows(a, cc):
        if not halved[a]:
            return slice(None)
        pr = slabs[a].shape[1] // part[1]
        return pl.ds(part[0] * pr + cc * (pr // 2), pr // 2)

    def ici(bufs, scr, a, j, chip, c, mine):
        px, py = chip
        x, y, _, _ = _place()
        qs = 2 * x + y if mine else 2 * px + py
        piece = bufs[a].at[qs, rows(a, c)]
        return _remote(piece, piece, scr[0].at[3 * a + j], scr[1].at[3 * a + j], (px, py, c))

    def d2d(bufs, scr, a, j, chip, cc):
        px, py = chip
        x, y, c, _ = _place()
        piece = bufs[a].at[2 * px + py, rows(a, cc)]
        return _remote(piece, piece, scr[2].at[3 * a + j], scr[3].at[3 * a + j], (x, y, 1 - c))

    def start(ins, outs, scr):
        _, _, c, chips = _place()
        for j, chip in enumerate(chips):
            for a in range(n):
                ici(outs, scr, a, j, chip, c, True).start()

    def finish(ins, outs, scr):
        _, _, c, chips = _place()
        for j, chip in enumerate(chips):
            for a in range(n):
                ici(outs, scr, a, j, chip, c, False).wait_recv()
                if halved[a]:
                    d2d(outs, scr, a, j, chip, c).start()
        for j, chip in enumerate(chips):
            for a in range(n):
                if halved[a]:
                    d2d(outs, scr, a, j, chip, 1 - c).wait_recv()
        for j, chip in enumerate(chips):
            for a in range(n):
                ici(outs, scr, a, j, chip, c, True).wait_send()
                if halved[a]:
                    d2d(outs, scr, a, j, chip, c).wait_send()

    return _Exchange(list(slabs), [jax.ShapeDtypeStruct(s.shape, s.dtype) for s in slabs],
                     {a: a for a in range(n)}, [pltpu.SemaphoreType.DMA((3 * n,))] * 4, start, finish)


def _ex_gather_relay(slabs, mid_at=0.5):
    n = len(slabs)

    def rows(a, cc):
        hr = slabs[a].shape[1] // 2
        return pl.ds(cc * hr, hr)

    def peers():
        x, y, c, _ = _place()
        nbr0 = ((x + c) % 2, (y + 1 - c) % 2)
        nbr1 = ((x + 1 - c) % 2, (y + c) % 2)
        return x, y, c, nbr0, nbr1, (1 - x, 1 - y)

    def ici(bufs, scr, a, k, chip, dev, cc):
        _, _, c, _, _, _ = peers()
        piece = bufs[a].at[2 * chip[0] + chip[1], rows(a, cc)]
        return _remote(piece, piece, scr[0].at[3 * a + k], scr[1].at[3 * a + k], (dev[0], dev[1], c))

    def d2d(bufs, scr, a, k, chip, cc):
        x, y, c, _, _, _ = peers()
        piece = bufs[a].at[2 * chip[0] + chip[1], rows(a, cc)]
        return _remote(piece, piece, scr[2].at[3 * a + k], scr[3].at[3 * a + k], (x, y, 1 - c))

    def start(ins, outs, scr):
        x, y, c, nbr0, nbr1, _ = peers()
        for a in range(n):
            ici(outs, scr, a, 0, (x, y), nbr0, c).start()
            ici(outs, scr, a, 1, (x, y), nbr1, c).start()

    def mid(ins, outs, scr):
        x, y, c, nbr0, nbr1, diag = peers()
        for a in range(n):
            ici(outs, scr, a, 0, nbr0, nbr0, c).wait_recv()
            ici(outs, scr, a, 2, nbr0, nbr1, c).start()
            d2d(outs, scr, a, 0, nbr0, c).start()
        for a in range(n):
            ici(outs, scr, a, 1, nbr1, nbr1, c).wait_recv()
            d2d(outs, scr, a, 1, nbr1, c).start()

    def finish(ins, outs, scr):
        x, y, c, nbr0, nbr1, diag = peers()
        for a in range(n):
            ici(outs, scr, a, 2, diag, nbr1, c).wait_recv()
            d2d(outs, scr, a, 2, diag, c).start()
        for a in range(n):
            d2d(outs, scr, a, 0, nbr1, 1 - c).wait_recv()
            d2d(outs, scr, a, 1, nbr0, 1 - c).wait_recv()
            d2d(outs, scr, a, 2, diag, 1 - c).wait_recv()
        for a in range(n):
            ici(outs, scr, a, 0, (x, y), nbr0, c).wait_send()
            ici(outs, scr, a, 1, (x, y), nbr1, c).wait_send()
            ici(outs, scr, a, 2, nbr0, nbr1, c).wait_send()
            d2d(outs, scr, a, 0, nbr0, c).wait_send()
            d2d(outs, scr, a, 1, nbr1, c).wait_send()
            d2d(outs, scr, a, 2, diag, c).wait_send()

    return _Exchange(list(slabs), [jax.ShapeDtypeStruct(s.shape, s.dtype) for s in slabs],
                     {a: a for a in range(n)}, [pltpu.SemaphoreType.DMA((3 * n,))] * 4, start, finish, mid, mid_at)


def _ex_to_sibling(grads):
    n = len(grads)

    def copy(ins, outs, scr, a):
        x, y, c, _ = _place()
        hr = grads[a].shape[1] // 2
        return _remote(ins[a].at[:, pl.ds((1 - c) * hr, hr), :], outs[a], scr[0].at[a], scr[1].at[a], (x, y, 1 - c))

    def start(ins, outs, scr):
        for a in range(n):
            copy(ins, outs, scr, a).start()

    def finish(ins, outs, scr):
        for a in range(n):
            copy(ins, outs, scr, a).wait()

    out_shape = [jax.ShapeDtypeStruct((g.shape[0], g.shape[1] // 2, g.shape[2]), g.dtype) for g in grads]
    return _Exchange(list(grads), out_shape, {}, [pltpu.SemaphoreType.DMA((n,))] * 2, start, finish)


def _ex_to_owner(parts, part=(0, 1), landing=None):
    n = len(parts)

    def copy(ins, outs, scr, a, j, chip):
        _, _, c, _ = _place()
        px, py = chip
        pr = parts[a].shape[1] // part[1]
        rows = pl.ds(part[0] * pr, pr)
        return _remote(ins[a].at[2 * px + py, rows], outs[a].at[j, rows], scr[0].at[3 * a + j],
                       scr[1].at[3 * a + j], (px, py, c))

    def start(ins, outs, scr):
        for j, chip in enumerate(_place()[3]):
            for a in range(n):
                copy(ins, outs, scr, a, j, chip).start()

    def finish(ins, outs, scr):
        for j, chip in enumerate(_place()[3]):
            for a in range(n):
                copy(ins, outs, scr, a, j, chip).wait()

    out_shape = [jax.ShapeDtypeStruct((3,) + p.shape[1:], p.dtype) for p in parts]
    operands, aliases = list(parts), {}
    if landing is not None:
        operands, aliases = operands + list(landing), {n + a: a for a in range(n)}
    return _Exchange(operands, out_shape, aliases, [pltpu.SemaphoreType.DMA((3 * n,))] * 2, start, finish)


def _ex_share_halves(bufs):
    n = len(bufs)

    def copy(outs, scr, a, cc):
        x, y, c, _ = _place()
        hr = bufs[a].shape[0] // 2
        piece = outs[a].at[pl.ds(cc * hr, hr), :]
        return _remote(piece, piece, scr[0].at[a], scr[1].at[a], (x, y, 1 - c))

    def start(ins, outs, scr):
        c = _place()[2]
        for a in range(n):
            copy(outs, scr, a, c).start()

    def finish(ins, outs, scr):
        c = _place()[2]
        for a in range(n):
            copy(outs, scr, a, c).wait_send()
            copy(outs, scr, a, 1 - c).wait_recv()

    return _Exchange(list(bufs), [jax.ShapeDtypeStruct(b.shape, b.dtype) for b in bufs], {a: a for a in range(n)},
                     [pltpu.SemaphoreType.DMA((n,))] * 2, start, finish)


def _ex_gather_small(arrs):
    n = len(arrs)

    def peer_of(m):
        x, y, c, _ = _place()
        return (1 - x if m & 4 else x, 1 - y if m & 2 else y, 1 - c if m & 1 else c)

    def own(ins, outs, scr, a):
        x, y, c, _ = _place()
        return pltpu.make_async_copy(ins[a], outs[a].at[4 * x + 2 * y + c], scr[2].at[a])

    def start(ins, outs, scr):
        x, y, c, _ = _place()
        for a in range(n):
            own(ins, outs, scr, a).start()
        for m in range(1, N_DEV):
            for a in range(n):
                k = (N_DEV - 1) * a + m - 1
                _remote(ins[a], outs[a].at[4 * x + 2 * y + c], scr[0].at[k], scr[1].at[k], peer_of(m)).start()

    def finish(ins, outs, scr):
        for a in range(n):
            own(ins, outs, scr, a).wait()
        for m in range(1, N_DEV):
            px, py, pc = peer_of(m)
            for a in range(n):
                k = (N_DEV - 1) * a + m - 1
                slot = outs[a].at[4 * px + 2 * py + pc]
                cp = _remote(ins[a], slot, scr[0].at[k], scr[1].at[k], (px, py, pc))
                cp.wait_send()
                cp.wait_recv()

    out_shape = [jax.ShapeDtypeStruct((N_DEV,) + a.shape, a.dtype) for a in arrs]
    return _Exchange(list(arrs), out_shape, {},
                     [pltpu.SemaphoreType.DMA(((N_DEV - 1) * n,))] * 2 + [pltpu.SemaphoreType.DMA((n,))], start, finish)


def _div_tile(n, want):
    best = None
    for t in range(8, min(n, want) + 1, 8):
        if n % t == 0:
            best = t
    assert best is not None, n
    return best


def _cast_into_slab(name, w, place, dtype):
    r, cc = w.shape
    tr = r if r * cc <= 128 * 1024 else _div_tile(r, 256)

    def body(s_ref, w_ref, o_ref):
        o_ref[...] = w_ref[...].astype(o_ref.dtype)

    return _pallas(
        body, name=name,
        grid_spec=pltpu.PrefetchScalarGridSpec(
            num_scalar_prefetch=1, grid=(r // tr,),
            in_specs=[pl.BlockSpec((tr, cc), lambda i, s: (i, 0))],
            out_specs=pl.BlockSpec((None, tr, cc), lambda i, s: (s[0], i, 0))),
        out_shape=jax.ShapeDtypeStruct((N_CHIPS, r, cc), dtype), compiler_params=_cp("parallel"),
    )(place, w)


def _add_half(name, g, rcv, place):
    nq, r, cc = g.shape
    hr = r // 2

    def body(s_ref, g_ref, r_ref, o_ref):
        o_ref[...] = (g_ref[...] + r_ref[...]).astype(o_ref.dtype)

    spec = pl.BlockSpec((None, hr, cc), lambda i, s: (i, 0, 0))
    return _pallas(
        body, name=name,
        grid_spec=pltpu.PrefetchScalarGridSpec(
            num_scalar_prefetch=1, grid=(nq,),
            in_specs=[pl.BlockSpec((None, hr, cc), lambda i, s: (i, s[1], 0)), spec], out_specs=spec),
        out_shape=jax.ShapeDtypeStruct((nq, hr, cc), BF16), compiler_params=_cp("parallel"),
    )(place, g, rcv)


def _sum_owner(name, part, rcv, place):
    _, hr, cc = part.shape
    tr = _div_tile(hr, 128)
    nb = hr // tr

    def body(s_ref, p_ref, r_ref, o_ref):
        o_ref[...] = ((p_ref[...].astype(F32) + r_ref[0].astype(F32)) + r_ref[1].astype(F32)) + r_ref[2].astype(F32)

    return _pallas(
        body, name=name,
        grid_spec=pltpu.PrefetchScalarGridSpec(
            num_scalar_prefetch=1, grid=(nb,),
            in_specs=[pl.BlockSpec((None, tr, cc), lambda i, s: (s[0], i, 0)),
                      pl.BlockSpec((3, tr, cc), lambda i, s: (0, i, 0))],
            out_specs=pl.BlockSpec((tr, cc), lambda i, s: (s[1] * nb + i, 0))),
        out_shape=jax.ShapeDtypeStruct((2 * hr, cc), F32), compiler_params=_cp("parallel"),
    )(place, part, rcv)


def _sum_small(gathered, local, place):
    n = len(gathered)

    def body(s_ref, *refs):
        g_refs, l_refs, o_refs = refs[:n], refs[n:2 * n], refs[2 * n:]
        me = s_ref[2]
        for g_ref, l_ref, o_ref in zip(g_refs, l_refs, o_refs):
            acc = None
            for d in range(N_DEV):
                term = jnp.where(me == d, l_ref[...], g_ref[d])
                acc = term if acc is None else acc + term
            o_ref[...] = acc

    def whole(shape):
        return pl.BlockSpec(shape, lambda i, s, nd=len(shape): (0,) * nd)

    return _pallas(
        body, name="sum_small",
        grid_spec=pltpu.PrefetchScalarGridSpec(
            num_scalar_prefetch=1, grid=(1,),
            in_specs=[whole(g.shape) for g in gathered] + [whole(a.shape) for a in local],
            out_specs=tuple(whole(a.shape) for a in local)),
        out_shape=tuple(jax.ShapeDtypeStruct(a.shape, a.dtype) for a in local), compiler_params=_cp("arbitrary"),
    )(place, *gathered, *local)


def _adamw(name, w, g, m, v):
    r, cc = w.shape
    tr = r if r * cc <= 128 * 1024 else _div_tile(r, 256)

    def body(w_ref, g_ref, m_ref, v_ref, d_ref, mo_ref, vo_ref, go_ref):
        gv = g_ref[...]
        go_ref[...] = gv
        mn = ADAM_B1 * m_ref[...] + (1.0 - ADAM_B1) * gv
        vn = ADAM_B2 * v_ref[...] + (1.0 - ADAM_B2) * (gv * gv)
        m_hat = mn / (1.0 - ADAM_B1 ** ADAM_STEP)
        v_hat = vn / (1.0 - ADAM_B2 ** ADAM_STEP)
        d_ref[...] = -ADAM_LR * (m_hat / (jnp.sqrt(v_hat) + ADAM_EPS) + ADAM_WD * w_ref[...])
        mo_ref[...] = mn
        vo_ref[...] = vn

    spec = pl.BlockSpec((tr, cc), lambda i: (i, 0))
    sd = jax.ShapeDtypeStruct((r, cc), F32)
    return _pallas(
        body, name=name, grid=(r // tr,), in_specs=[spec] * 4, out_specs=(spec,) * 4, out_shape=(sd,) * 4,
        compiler_params=_cp("parallel"),
    )(w, g, m, v)


_BIG = ("w_in", "w_up", "w_branch", "w_mem_kv", "w_out", "w_down")
_BIG_SHARD_SHAPE = {"w_in": (1024, 1664), "w_up": (1024, 1408), "w_branch": (1536, 256),
                    "w_mem_kv": (256, 1024), "w_out": (256, 1024), "w_down": (704, 1024)}
_SMALL_SHAPE = {"norm1_g": (1, D_MODEL), "ln_v_g": (1, GM_WIDTH), "ln_v_b": (1, GM_WIDTH),
                "w_spatial": (GM_GROUPS * GM_CHUNK, GM_CHUNK), "b_spatial": (GM_GROUPS, GM_CHUNK),
                "lb_logits": (2, HG_HEADS * HG_DIM), "hgrn_norm_g": (1, HG_DIM), "mem_norm_g": (1, D_MODEL),
                "norm2_g": (1, D_MODEL), "conv_w": (3, D_FF), "conv_b": (1, D_FF), "final_g": (1, D_MODEL)}
_SMALL_EARLY = tuple(n for n in _SMALL_SHAPE if n != "norm1_g")
_PARAM_ORDER = ("norm1_g", "w_in", "ln_v_g", "ln_v_b", "w_spatial", "b_spatial", "lb_logits", "hgrn_norm_g",
                "mem_norm_g", "w_mem_kv", "w_branch", "w_out", "norm2_g", "w_up", "conv_w", "conv_b", "w_down",
                "final_g")


def _adamw_small(ws, gs, ms, vs):
    n = len(ws)

    def body(*refs):
        w_refs, g_refs, m_refs, v_refs = refs[:n], refs[n:2 * n], refs[2 * n:3 * n], refs[3 * n:4 * n]
        d_refs, mo_refs, vo_refs = refs[4 * n:5 * n], refs[5 * n:6 * n], refs[6 * n:]
        for k in range(n):
            gv = g_refs[k][...]
            mn = ADAM_B1 * m_refs[k][...] + (1.0 - ADAM_B1) * gv
            vn = ADAM_B2 * v_refs[k][...] + (1.0 - ADAM_B2) * (gv * gv)
            m_hat = mn / (1.0 - ADAM_B1 ** ADAM_STEP)
            v_hat = vn / (1.0 - ADAM_B2 ** ADAM_STEP)
            d_refs[k][...] = -ADAM_LR * (m_hat / (jnp.sqrt(v_hat) + ADAM_EPS) + ADAM_WD * w_refs[k][...])
            mo_refs[k][...] = mn
            vo_refs[k][...] = vn

    specs = [pl.BlockSpec(a.shape, lambda i, nd=a.ndim: (0,) * nd) for a in ws]
    shapes = tuple(jax.ShapeDtypeStruct(a.shape, F32) for a in ws)
    res = _pallas(
        body, name="adamw_small", grid=(1,), in_specs=specs * 4, out_specs=tuple(specs * 3), out_shape=shapes * 3,
        compiler_params=_cp("arbitrary"),
    )(*ws, *gs, *ms, *vs)
    return res[:n], res[n:2 * n], res[2 * n:]


class _Comm:
    _ROW_SHARDED = ("w_mem_kv", "w_out", "w_down")

    def __init__(self, slabs, place):
        self.slabs, self.place = slabs, place
        self.full, self.raw, self.parts, self.landing, self.bufs, self.done = {}, {}, {}, {}, {}, {}

    def w(self, name):
        a = self.full[name]
        if name in self._ROW_SHARDED:
            return a.reshape(-1, a.shape[-1])
        if name == "conv_w":
            return jnp.transpose(a, (1, 0, 2)).reshape(3, 1, D_FF)
        return a

    sends = True

    def grad(self, name, arr, from_sibling=None):
        self.raw[name] = arr.reshape((N_CHIPS, -1, arr.shape[-1]))
        if from_sibling is not None:
            self.parts[name] = _add_half("rs_add_" + name, self.raw[name], from_sibling, self.place)

    def small_grads(self, arrays):
        self.small_local = list(arrays)

    def carry(self, tag, call):
        plan = self._plan(tag)
        if not plan:
            return call(())
        out, carried = call([ex for ex, _ in plan])
        for (_, deliver), res in zip(plan, carried):
            deliver(res)
        return out

    def finish(self, last_small):
        ex, deliver = self._share(["w_out", "w_branch", "w_mem_kv", "w_in"])
        shared, small = _run_exchanges("share_and_gather_last", [ex, _ex_gather_small(last_small)])
        deliver(shared)
        return self.done, self.small_local + list(last_small), self.small_everyone + small

    def _plan(self, tag):
        if tag == "norm1":
            def deliver(res):
                self.full["w_in"] = res[0]

            return [(_ex_gather_relay([self.slabs["w_in"]]), deliver)]
        if tag == "in_proj":
            return [self._gather_relay(["w_branch", "w_out", "w_mem_kv", "w_down"], 0.6), self._gather(["conv_w"])]
        if tag == "hgrn_fwd":
            return [self._gather_relay(["w_up"], 0.8)]
        if tag == "d_h2":
            return [self._to_sibling(["w_down", "w_up"])]
        if tag == "merge_bwd":
            return [self._to_owner(["w_up"], (0, 2))]
        if tag == "hgrn_bwd":
            return [self._to_owner(["w_down"]), self._to_owner(["w_up"], (1, 2)),
                    self._to_sibling(["w_out", "w_branch", "w_mem_kv"])]
        if tag == "g_w_in":
            def keep(res):
                self.small_everyone = res

            return [self._to_owner(["w_out", "w_branch", "w_mem_kv"]), (_ex_gather_small(self.small_local), keep)]
        if tag == "d_h":
            return [self._to_owner(["w_in"]), self._share(["w_down", "w_up"])]
        return []

    def _gather(self, names, part=(0, 1)):
        def deliver(res):
            self.slabs.update(zip(names, res))
            self.full.update(zip(names, res))

        return _ex_all_gather([self.slabs[n] for n in names], [n != "conv_w" for n in names], part), deliver

    def _gather_relay(self, names, mid_at):
        return _ex_gather_relay([self.slabs[n] for n in names], mid_at), lambda res: self.full.update(zip(names, res))

    def _to_sibling(self, names):
        def deliver(res):
            for n, r in zip(names, res):
                self.parts[n] = _add_half("rs_add_" + n, self.raw[n], r, self.place)

        return _ex_to_sibling([self.raw[n] for n in names]), deliver

    def _to_owner(self, names, part=(0, 1)):
        def deliver(res):
            for n, r in zip(names, res):
                if part[0] + 1 < part[1]:
                    self.landing[n] = r
                else:
                    self.bufs[n] = _sum_owner("rs_sum_" + n, self.parts[n], r, self.place)

        landing = [self.landing[n] for n in names] if part[0] else None
        return _ex_to_owner([self.parts[n] for n in names], part, landing), deliver

    def _share(self, names):
        return _ex_share_halves([self.bufs[n] for n in names]), lambda res: self.done.update(zip(names, res))


def kernel(x, mem, norm1_g, w_in, ln_v_g, ln_v_b, w_spatial, b_spatial, lb_logits, hgrn_norm_g, mem_norm_g, w_mem_kv, w_branch, w_out, norm2_g, w_up, conv_w, conv_b, w_down, final_g, loss_target, m_norm1_g, m_w_in, m_ln_v_g, m_ln_v_b, m_w_spatial, m_b_spatial, m_lb_logits, m_hgrn_norm_g, m_mem_norm_g, m_w_mem_kv, m_w_branch, m_w_out, m_norm2_g, m_w_up, m_conv_w, m_conv_b, m_w_down, m_final_g, v_norm1_g, v_w_in, v_ln_v_g, v_ln_v_b, v_w_spatial, v_b_spatial, v_lb_logits, v_hgrn_norm_g, v_mem_norm_g, v_w_mem_kv, v_w_branch, v_w_out, v_norm2_g, v_w_up, v_conv_w, v_conv_b, v_w_down, v_final_g):
    w = dict(norm1_g=norm1_g, w_in=w_in, ln_v_g=ln_v_g, ln_v_b=ln_v_b, w_spatial=w_spatial, b_spatial=b_spatial,
             lb_logits=lb_logits, hgrn_norm_g=hgrn_norm_g, mem_norm_g=mem_norm_g, w_mem_kv=w_mem_kv,
             w_branch=w_branch, w_out=w_out, norm2_g=norm2_g, w_up=w_up, conv_w=conv_w, conv_b=conv_b,
             w_down=w_down, final_g=final_g)
    mom = dict(norm1_g=m_norm1_g, w_in=m_w_in, ln_v_g=m_ln_v_g, ln_v_b=m_ln_v_b, w_spatial=m_w_spatial,
               b_spatial=m_b_spatial, lb_logits=m_lb_logits, hgrn_norm_g=m_hgrn_norm_g, mem_norm_g=m_mem_norm_g,
               w_mem_kv=m_w_mem_kv, w_branch=m_w_branch, w_out=m_w_out, norm2_g=m_norm2_g, w_up=m_w_up,
               conv_w=m_conv_w, conv_b=m_conv_b, w_down=m_w_down, final_g=m_final_g)
    var = dict(norm1_g=v_norm1_g, w_in=v_w_in, ln_v_g=v_ln_v_g, ln_v_b=v_ln_v_b, w_spatial=v_w_spatial,
               b_spatial=v_b_spatial, lb_logits=v_lb_logits, hgrn_norm_g=v_hgrn_norm_g, mem_norm_g=v_mem_norm_g,
               w_mem_kv=v_w_mem_kv, w_branch=v_w_branch, w_out=v_w_out, norm2_g=v_norm2_g, w_up=v_w_up,
               conv_w=v_conv_w, conv_b=v_conv_b, w_down=v_w_down, final_g=v_final_g)
    B, S, D = x.shape
    T = B * S
    ci = lax.axis_index("c")
    q = 2 * lax.axis_index("x") + lax.axis_index("y")
    place = jnp.stack([q, ci, 2 * q + ci]).astype(jnp.int32)

    shards = {n: w[n].reshape(_BIG_SHARD_SHAPE[n]) for n in _BIG}
    slabs = {"w_in": _cast_into_slab("slab_w_in", shards.pop("w_in"), place, BF16),
             "conv_w": _cast_into_slab("slab_conv_w", conv_w[0], place, F32)}
    comm = _Comm(slabs, place)
    p = dict(
        cast_beside_norm1=shards,
        norm1_g=norm1_g, ln_v_g=ln_v_g, ln_v_b=ln_v_b, w_spatial=w_spatial[0],
        b_spatial=b_spatial.reshape(GM_GROUPS, GM_CHUNK, 1), lb_logits=lb_logits, hgrn_norm_g=hgrn_norm_g,
        mem_norm_g=mem_norm_g, norm2_g=norm2_g, conv_b=conv_b, final_g=final_g.reshape(1, D))

    loss, grad_x, g = _local_step(x.reshape(T, D), mem.reshape(B * MEM_LEN, D), loss_target.reshape(T, D), p, comm,
                                  B, S)

    shard_grads, local_small, everyone = comm.finish([g["norm1_g"]])
    summed = _sum_small(everyone, local_small, place)
    small_names = list(_SMALL_EARLY) + ["norm1_g"]
    total = dict(zip(_SMALL_EARLY, summed))
    loss_total, total["norm1_g"] = summed[len(_SMALL_EARLY)][0, 0], summed[-1]

    grads, delta, new_m, new_v = {}, {}, {}, {}
    for n in _BIG:
        shp = _BIG_SHARD_SHAPE[n]
        delta[n], new_m[n], new_v[n], grads[n] = _adamw("adamw_" + n, w[n].reshape(shp), shard_grads[n],
                                                        mom[n].reshape(shp), var[n].reshape(shp))
    cw_shard = D_FF // N_CHIPS
    total["conv_w"] = lax.dynamic_slice(total["conv_w"], (0, q * cw_shard), (3, cw_shard)).reshape(3, 1, cw_shard)

    def flat2d(d, n):
        return d[n].reshape(total[n].shape)

    upd = _adamw_small([flat2d(w, n) for n in small_names], [total[n] for n in small_names],
                       [flat2d(mom, n) for n in small_names], [flat2d(var, n) for n in small_names])
    for k, n in enumerate(small_names):
        grads[n], delta[n], new_m[n], new_v[n] = total[n], upd[0][k], upd[1][k], upd[2][k]

    def shaped(d):
        return [d[n].reshape(w[n].shape) for n in _PARAM_ORDER]

    return (loss_total, grad_x.reshape(B, S, D), *shaped(grads), *shaped(delta), *shaped(new_m), *shaped(new_v))
```

```python
import functools
import math

import jax
import jax.numpy as jnp
from jax import lax
from jax.experimental import pallas as pl
from jax.experimental.pallas import tpu as pltpu

F32 = jnp.float32
BF16 = jnp.bfloat16
EPS = 1e-6

D_MODEL = 1024
MEM_LEN = 256
GM_WIDTH = 512
GM_CHUNK = 128
GM_GROUPS = 4
HG_HEADS = 4
HG_DIM = 128
HG_CHUNK = 64
XA_HEADS = 4
XA_DIM = 128
BR_WIDTH = 512
D_FF = 2816
IN_WIDTH = 6656
N_CHIPS = 4
N_DEV = 8

ADAM_LR = 0.001
ADAM_B1 = 0.9
ADAM_B2 = 0.999
ADAM_EPS = 1e-08
ADAM_WD = 0.01
ADAM_STEP = 10

COL_ZU, COL_ZV, COL_HQ, COL_HF, COL_HI, COL_HG, COL_XQ = 0, 1, 2, 3, 4, 5, 6
COL_GATE0 = 3584

VMEM_LIMIT_BYTES = 48 * 1024 * 1024
MESH_ID = pl.DeviceIdType.MESH


def _cp(*sem):
    return pltpu.CompilerParams(dimension_semantics=sem, vmem_limit_bytes=VMEM_LIMIT_BYTES)


def _pallas(body, *, out_shape, **kw):
    def pin(s):
        return pltpu.HBM(s.shape, s.dtype) if isinstance(s, jax.ShapeDtypeStruct) else s

    out_shape = tuple(pin(s) for s in out_shape) if isinstance(out_shape, (tuple, list)) else pin(out_shape)
    call = pl.pallas_call(body, out_shape=out_shape, **kw)

    def run(*operands):
        return call(*[pltpu.with_memory_space_constraint(o, pltpu.HBM) if jnp.issubdtype(o.dtype, jnp.floating)
                      else o for o in operands])

    return run


def _dot(a, b):
    return lax.dot_general(a.astype(BF16), b.astype(BF16), (((1,), (0,)), ((), ())), preferred_element_type=F32)


def _dot_nt(a, b):
    return lax.dot_general(a.astype(BF16), b.astype(BF16), (((1,), (1,)), ((), ())), preferred_element_type=F32)


def _dot_tn(a, b):
    return lax.dot_general(a.astype(BF16), b.astype(BF16), (((0,), (0,)), ((), ())), preferred_element_type=F32)


def _dot_01(mask01, x):
    hi = x.astype(BF16)
    r1 = x - hi.astype(F32)
    mid = r1.astype(BF16)
    lo = (r1 - mid.astype(F32)).astype(BF16)
    m = mask01.astype(BF16)
    dn = (((1,), (0,)), ((), ()))
    return (lax.dot_general(m, hi, dn, preferred_element_type=F32)
            + lax.dot_general(m, mid, dn, preferred_element_type=F32)
            + lax.dot_general(m, lo, dn, preferred_element_type=F32))


def _sigmoid(z):
    return 1.0 / (1.0 + jnp.exp(-z))


_GELU_C = math.sqrt(2.0 / math.pi)


def _gelu_and_grad(z):
    inner = _GELU_C * (z + 0.044715 * z * z * z)
    t = jnp.tanh(inner)
    val = 0.5 * z * (1.0 + t)
    grad = 0.5 * (1.0 + t) + 0.5 * z * (1.0 - t * t) * _GELU_C * (1.0 + 3.0 * 0.044715 * z * z)
    return val, grad


def _row_tile(n, want=512):
    t = min(want, n)
    assert n % t == 0
    return t


def _pcall(body, operands, *, name, grid, in_specs, out_specs, out_shape, scratch_shapes=(), semantics, riders=(),
           prefetch=None):
    single = not isinstance(out_shape, (tuple, list))
    out_specs = (out_specs,) if single else tuple(out_specs)
    out_shape = (out_shape,) if single else tuple(out_shape)
    n_pre = 0 if prefetch is None else 1

    def call(fn, ins_, outs_, shapes_, scr_, ops, sem, aliases):
        if prefetch is None:
            return _pallas(fn, name=name, grid=grid, in_specs=ins_, out_specs=outs_, out_shape=shapes_,
                           scratch_shapes=scr_, input_output_aliases=aliases, compiler_params=_cp(*sem))(*ops)
        spec = pltpu.PrefetchScalarGridSpec(num_scalar_prefetch=1, grid=grid, in_specs=ins_, out_specs=outs_,
                                            scratch_shapes=scr_)
        return _pallas(fn, name=name, grid_spec=spec, out_shape=shapes_, input_output_aliases=aliases,
                       compiler_params=_cp(*sem))(prefetch, *ops)

    if not riders:
        res = call(body, list(in_specs), out_specs, out_shape, list(scratch_shapes), operands, semantics, {})
        return (res[0] if single else res), []
    n_in, n_out, n_scr = len(in_specs), len(out_shape), len(scratch_shapes)
    ex_in = [len(ex.operands) for ex in riders]
    ex_out = [len(ex.out_shape) for ex in riders]
    ex_scr = [len(ex.scratch) for ex in riders]
    tot_in, tot_out = n_in + sum(ex_in), n_out + sum(ex_out)

    def wrapped(*refs):
        pre, refs = refs[:n_pre], refs[n_pre:]
        ins, outs, scr = refs[:tot_in], refs[tot_in:tot_in + tot_out], refs[tot_in + tot_out:]
        ids = [pl.program_id(d) for d in range(len(grid))]
        first = functools.reduce(lambda p, t: p & t, [i == 0 for i in ids])
        last = functools.reduce(lambda p, t: p & t, [i == n - 1 for i, n in zip(ids, grid)])
        parts, oi, oo, os_ = [], n_in, n_out, n_scr
        for k in range(len(riders)):
            parts.append((ins[oi:oi + ex_in[k]], outs[oo:oo + ex_out[k]], scr[os_:os_ + ex_scr[k]]))
            oi, oo, os_ = oi + ex_in[k], oo + ex_out[k], os_ + ex_scr[k]

        @pl.when(first)
        def _():
            for ex, part in zip(riders, parts):
                ex.start(*part)

        step, total = 0, 1
        for i, n in zip(ids, grid):
            step, total = step * n + i, total * n
        for ex, part in zip(riders, parts):
            if ex.mid is not None:
                @pl.when(step == min(total - 1, int(total * ex.mid_at)))
                def _(ex=ex, part=part):
                    ex.mid(*part)

        body(*pre, *ins[:n_in], *outs[:n_out], *scr[:n_scr])

        @pl.when(last)
        def _():
            for ex, part in zip(riders, parts):
                ex.finish(*part)

    aliases, oi, oo = {}, n_in, n_out
    all_ops, all_shapes, all_scr = list(operands), list(out_shape), list(scratch_shapes)
    for k, ex in enumerate(riders):
        aliases.update({n_pre + oi + a: oo + b for a, b in ex.aliases.items()})
        oi, oo = oi + ex_in[k], oo + ex_out[k]
        all_ops += list(ex.operands)
        all_shapes += [pltpu.HBM(s.shape, s.dtype) for s in ex.out_shape]
        all_scr += list(ex.scratch)
    res = call(wrapped, list(in_specs) + [HBM_SPEC] * sum(ex_in), out_specs + (HBM_SPEC,) * sum(ex_out),
               tuple(all_shapes), all_scr, all_ops, ["arbitrary"] * len(grid), aliases)
    own = res[0] if single else tuple(res[:n_out])
    carried, oo = [], n_out
    for k in range(len(riders)):
        carried.append(list(res[oo:oo + ex_out[k]]))
        oo += ex_out[k]
    return own, carried


def _carried(out, carried, riders):
    return (out, carried) if riders else out


def _matmul(name, operands, *, grid, in_specs, o_spec, out_shape, out_dtype, dims, riders=()):
    nk = grid[2]
    assert nk == 1 or out_dtype == F32

    def body(a_ref, b_ref, o_ref):
        part = lax.dot_general(a_ref[...].astype(BF16), b_ref[...].astype(BF16), (dims, ((), ())),
                               preferred_element_type=F32)
        if nk == 1:
            o_ref[...] = part.astype(o_ref.dtype)
        else:
            k = pl.program_id(2)

            @pl.when(k == 0)
            def _():
                o_ref[...] = part

            @pl.when(k > 0)
            def _():
                o_ref[...] += part

    out, carried = _pcall(body, operands, name=name, grid=grid, in_specs=in_specs, out_specs=o_spec,
                          out_shape=jax.ShapeDtypeStruct(out_shape, out_dtype),
                          semantics=("parallel", "parallel", "arbitrary"), riders=riders)
    return (out, carried) if riders else out


NN = ((1,), (0,))
NT = ((1,), (1,))
TN = ((0,), (0,))
_TN_TOKENS = 4096


def _mm_cs(name, a, w, out_dtype, riders=()):
    M, K = a.shape
    nq, _, wd = w.shape
    tm = _row_tile(M)
    return _matmul(name, (a, w), grid=(nq, M // tm, 1),
                   in_specs=[pl.BlockSpec((tm, K), lambda j, i, k: (i, 0)),
                             pl.BlockSpec((None, K, wd), lambda j, i, k: (j, 0, 0))],
                   o_spec=pl.BlockSpec((tm, wd), lambda j, i, k: (i, j)),
                   out_shape=(M, nq * wd), out_dtype=out_dtype, dims=NN, riders=riders)


def _mm_rs(name, a, w, out_dtype):
    M, K = a.shape
    N = w.shape[1]
    tm = _row_tile(M)
    return _matmul(name, (a, w), grid=(M // tm, 1, 1),
                   in_specs=[pl.BlockSpec((tm, K), lambda i, j, k: (i, 0)), pl.BlockSpec((K, N), lambda i, j, k: (0, 0))],
                   o_spec=pl.BlockSpec((tm, N), lambda i, j, k: (i, 0)),
                   out_shape=(M, N), out_dtype=out_dtype, dims=NN)


def _mm_nt_rs(name, g, w, out_dtype, riders=()):
    M, N = g.shape
    K = w.shape[0]
    to = K
    tm = _row_tile(M)
    return _matmul(name, (g, w), grid=(M // tm, K // to, 1),
                   in_specs=[pl.BlockSpec((tm, N), lambda i, j, k: (i, 0)),
                             pl.BlockSpec((to, N), lambda i, j, k: (j, 0))],
                   o_spec=pl.BlockSpec((tm, to), lambda i, j, k: (i, j)),
                   out_shape=(M, K), out_dtype=out_dtype, dims=NT, riders=riders)


def _mm_nt_cs(name, g, w, out_dtype, riders=(), stacked=False, norm_bwd=None):
    M = g.shape[-2]
    nq, K, wd = w.shape
    tm = _row_tile(M, 256)

    def product(g_ref, w_ref):
        acc = None
        for q in range(nq):
            gq = g_ref[q // 2, :, (q % 2) * wd:(q % 2 + 1) * wd] if stacked else g_ref[:, q * wd:(q + 1) * wd]
            part = _dot_nt(gq, w_ref[q])
            acc = part if acc is None else acc + part
        return acc

    def body(g_ref, w_ref, o_ref):
        o_ref[...] = product(g_ref, w_ref).astype(o_ref.dtype)

    def body_norm(g_ref, w_ref, x_ref, gain_ref, dr_ref, dx_ref, dg_ref):
        @pl.when(pl.program_id(0) == 0)
        def _():
            dg_ref[...] = jnp.zeros_like(dg_ref)

        dx, dg = _rms_bwd_rows(x_ref[...], gain_ref[...], product(g_ref, w_ref))
        dg_ref[...] += dg
        dx_ref[...] = dx + dr_ref[...]

    g_spec = (pl.BlockSpec((2, tm, 2 * wd), lambda i: (0, i, 0)) if stacked
              else pl.BlockSpec((tm, nq * wd), lambda i: (i, 0)))
    w_spec = pl.BlockSpec((nq, K, wd), lambda i: (0, 0, 0))
    row = pl.BlockSpec((tm, K), lambda i: (i, 0))
    if norm_bwd is None:
        return _carried(*_pcall(
            body, (g, w), name=name, grid=(M // tm,), in_specs=[g_spec, w_spec], out_specs=row,
            out_shape=jax.ShapeDtypeStruct((M, K), out_dtype), semantics=("parallel",), riders=riders), riders)
    vec = pl.BlockSpec((1, K), lambda i: (0, 0))
    return _carried(*_pcall(
        body_norm, (g, w) + tuple(norm_bwd), name=name, grid=(M // tm,),
        in_specs=[g_spec, w_spec, row, vec, row], out_specs=(row, vec),
        out_shape=(jax.ShapeDtypeStruct((M, K), F32), jax.ShapeDtypeStruct((1, K), F32)),
        semantics=("arbitrary",), riders=riders), riders)


def _mm_tn_rs(name, a, g, to, tn=512):
    T, M = a.shape
    N = g.shape[1]
    tt = _row_tile(T, _TN_TOKENS)
    tn = min(tn, N)
    return _matmul(name, (a, g), grid=(M // to, N // tn, T // tt),
                   in_specs=[pl.BlockSpec((tt, to), lambda i, j, k: (k, i)),
                             pl.BlockSpec((tt, tn), lambda i, j, k: (k, j))],
                   o_spec=pl.BlockSpec((to, tn), lambda i, j, k: (i, j)),
                   out_shape=(M, N), out_dtype=F32, dims=TN)


def _mm_tn_cs(name, a, g, nq, to, riders=(), stacked=False):
    T, M = a.shape
    wd = g.shape[-1] * (2 if stacked else 1) // nq
    tt = _row_tile(T, _TN_TOKENS)
    g_spec = (pl.BlockSpec((None, tt, wd), lambda i, j, k: (j // 2, k, j % 2)) if stacked
              else pl.BlockSpec((tt, wd), lambda i, j, k: (k, j)))
    return _matmul(name, (a, g), grid=(M // to, nq, T // tt),
                   in_specs=[pl.BlockSpec((tt, to), lambda i, j, k: (k, i)), g_spec],
                   o_spec=pl.BlockSpec((None, to, wd), lambda i, j, k: (j, i, 0)),
                   out_shape=(nq, M, wd), out_dtype=F32, dims=TN, riders=riders)


def _mm_tn_cs_to_sibling(name, a, g, nq, place, riders=(), send=True):
    T, M = a.shape
    wd = g.shape[-1] // nq
    to = M // 2
    steps = 2 * nq

    def body(s_ref, a_ref, g_ref, o_hbm, land_hbm, acc, wsem, send_sem, recv_sem):
        t = pl.program_id(0)
        c = s_ref[1]

        def writeback(tt):
            half = (tt // nq + 1 + c) % 2
            return pltpu.make_async_copy(acc.at[tt % 2], o_hbm.at[tt % nq, pl.ds(half * to, to), :], wsem.at[tt % 2])

        @pl.when(t >= 2)
        def _():
            writeback(t - 2).wait()

        if send:
            x, y, _, _ = _place()
            to_sibling = _remote(o_hbm.at[:, pl.ds((1 - c) * to, to), :], land_hbm, send_sem.at[0], recv_sem.at[0],
                                 (x, y, 1 - c))

            @pl.when(t == nq + 1)
            def _():
                to_sibling.start()

        acc[t % 2] = _dot_tn(a_ref[...], g_ref[...])
        writeback(t).start()

        @pl.when(t == steps - 1)
        def _():
            writeback(t - 1).wait()
            writeback(t).wait()
            if send:
                to_sibling.wait()

    out, carried = _pcall(
        body, (a, g), name=name, grid=(steps,),
        in_specs=[pl.BlockSpec((T, to), lambda t, s: (0, (t // nq + 1 + s[1]) % 2)),
                  pl.BlockSpec((T, wd), lambda t, s: (0, t % nq))],
        out_specs=(HBM_SPEC, HBM_SPEC),
        out_shape=(jax.ShapeDtypeStruct((nq, M, wd), F32), jax.ShapeDtypeStruct((nq, to, wd), F32)),
        scratch_shapes=[pltpu.VMEM((2, to, wd), F32), pltpu.SemaphoreType.DMA((2,)),
                        pltpu.SemaphoreType.DMA((1,)), pltpu.SemaphoreType.DMA((1,))],
        semantics=("arbitrary",), riders=riders, prefetch=place)
    return _carried(out, carried, riders)


def _rms_fwd(name, x, g, riders=()):
    T, D = x.shape
    tm = _row_tile(T)

    def body(x_ref, g_ref, o_ref):
        o_ref[...] = _rms_rows(x_ref[...], g_ref[...]).astype(o_ref.dtype)

    return _carried(*_pcall(
        body, (x, g), name=name, grid=(T // tm,),
        in_specs=[pl.BlockSpec((tm, D), lambda i: (i, 0)), pl.BlockSpec((1, D), lambda i: (0, 0))],
        out_specs=pl.BlockSpec((tm, D), lambda i: (i, 0)),
        out_shape=jax.ShapeDtypeStruct((T, D), BF16), semantics=("parallel",), riders=riders), riders)


_NORM1_STEPS = 4


def _norm1_and_casts(x, g, shards, place, riders=()):
    T, D = x.shape
    tm = T // _NORM1_STEPS
    names = list(shards)

    def body(s_ref, x_ref, g_ref, *refs):
        w_refs, o_ref, slab_refs = refs[:len(names)], refs[len(names)], refs[len(names) + 1:]
        o_ref[...] = _rms_rows(x_ref[...], g_ref[...]).astype(o_ref.dtype)
        for w_ref, slab_ref in zip(w_refs, slab_refs):
            slab_ref[...] = w_ref[...].astype(slab_ref.dtype)

    in_specs = [pl.BlockSpec((tm, D), lambda i, s: (i, 0)), pl.BlockSpec((1, D), lambda i, s: (0, 0))]
    out_specs = [pl.BlockSpec((tm, D), lambda i, s: (i, 0))]
    out_shape = [jax.ShapeDtypeStruct((T, D), BF16)]
    for n in names:
        r, cc = shards[n].shape
        assert r % (_NORM1_STEPS * 16) == 0
        in_specs.append(pl.BlockSpec((r // _NORM1_STEPS, cc), lambda i, s: (i, 0)))
        out_specs.append(pl.BlockSpec((None, r // _NORM1_STEPS, cc), lambda i, s: (s[0], i, 0)))
        out_shape.append(jax.ShapeDtypeStruct((N_CHIPS, r, cc), BF16))
    out, carried = _pcall(body, (x, g, *[shards[n] for n in names]), name="norm1", grid=(_NORM1_STEPS,),
                          in_specs=in_specs, out_specs=out_specs, out_shape=out_shape, semantics=("parallel",),
                          riders=riders, prefetch=place)
    return _carried((out[0], dict(zip(names, out[1:]))), carried, riders)


def _rms_rows(xv, gain):
    return xv * lax.rsqrt(jnp.mean(xv * xv, axis=-1, keepdims=True) + EPS) * gain


def _rms_bwd_rows(xv, gain, dh):
    r = lax.rsqrt(jnp.mean(xv * xv, axis=-1, keepdims=True) + EPS)
    n = xv * r
    dn = dh * gain
    return r * (dn - n * jnp.mean(dn * n, axis=-1, keepdims=True)), jnp.sum(dh * n, axis=0, keepdims=True)


def _rms_bwd(name, x, g, dh, dres):
    T, D = x.shape
    tm = _row_tile(T)
    has_res = dres is not None

    def body(*refs):
        if has_res:
            x_ref, g_ref, dh_ref, dr_ref, dx_ref, dg_ref = refs
        else:
            x_ref, g_ref, dh_ref, dx_ref, dg_ref = refs

        @pl.when(pl.program_id(0) == 0)
        def _():
            dg_ref[...] = jnp.zeros_like(dg_ref)

        dx, dg = _rms_bwd_rows(x_ref[...], g_ref[...], dh_ref[...])
        dg_ref[...] += dg
        if has_res:
            dx = dx + dr_ref[...]
        dx_ref[...] = dx

    row = pl.BlockSpec((tm, D), lambda i: (i, 0))
    vec = pl.BlockSpec((1, D), lambda i: (0, 0))
    ops = (x, g, dh, dres) if has_res else (x, g, dh)
    return _pallas(
        body, name=name, grid=(T // tm,), in_specs=[row, vec, row] + ([row] if has_res else []),
        out_specs=(row, vec),
        out_shape=(jax.ShapeDtypeStruct((T, D), F32), jax.ShapeDtypeStruct((1, D), F32)),
        compiler_params=_cp("arbitrary"),
    )(*ops)


def _proj_res_norm(name, a, w, res, gain):
    M, K = a.shape
    N = w.shape[1]
    tm = _row_tile(M)

    def body(a_ref, w_ref, r_ref, g_ref, x_ref, h_ref):
        xv = _dot(a_ref[...], w_ref[...]) + r_ref[...]
        x_ref[...] = xv
        h_ref[...] = _rms_rows(xv, g_ref[...]).astype(h_ref.dtype)

    row = pl.BlockSpec((tm, N), lambda i: (i, 0))
    return _pallas(
        body, name=name, grid=(M // tm,),
        in_specs=[pl.BlockSpec((tm, K), lambda i: (i, 0)), pl.BlockSpec((K, N), lambda i: (0, 0)), row,
                  pl.BlockSpec((1, N), lambda i: (0, 0))],
        out_specs=(row, row), out_shape=(jax.ShapeDtypeStruct((M, N), F32), jax.ShapeDtypeStruct((M, N), BF16)),
        compiler_params=_cp("parallel"),
    )(a, w, res, gain)


def _proj_res_loss(name, a, w, res, tgt, gain):
    M, K = a.shape
    D = w.shape[1]
    tm = _row_tile(M)

    def body(a_ref, w_ref, r_ref, t_ref, g_ref, dx_ref, dg_ref, loss_ref):
        @pl.when(pl.program_id(0) == 0)
        def _():
            dg_ref[...] = jnp.zeros_like(dg_ref)
            loss_ref[...] = jnp.zeros_like(loss_ref)

        xv = _dot(a_ref[...], w_ref[...]) + r_ref[...]
        gv = g_ref[...]
        diff = _rms_rows(xv, gv) - t_ref[...]
        loss_ref[...] += 0.5 * jnp.sum(jnp.mean(diff * diff, axis=-1, keepdims=True))
        dx, dg = _rms_bwd_rows(xv, gv, diff * (1.0 / D))
        dg_ref[...] += dg
        dx_ref[...] = dx

    row = pl.BlockSpec((tm, D), lambda i: (i, 0))
    vec = pl.BlockSpec((1, D), lambda i: (0, 0))
    return _pallas(
        body, name=name, grid=(M // tm,),
        in_specs=[pl.BlockSpec((tm, K), lambda i: (i, 0)), pl.BlockSpec((K, D), lambda i: (0, 0)), row, row, vec],
        out_specs=(row, vec, pl.BlockSpec((8, 128), lambda i: (0, 0))),
        out_shape=(jax.ShapeDtypeStruct((M, D), F32), jax.ShapeDtypeStruct((1, D), F32),
                   jax.ShapeDtypeStruct((8, 128), F32)),
        compiler_params=_cp("arbitrary"),
    )(a, w, res, tgt, gain)


def _gmlp_pieces(zu, zv, lng, lnb, ws_ref, bs_ref):
    u, du = _gelu_and_grad(zu)
    v, dv = _gelu_and_grad(zv)
    mu = jnp.mean(v, axis=-1, keepdims=True)
    vc = v - mu
    rstd = lax.rsqrt(jnp.mean(vc * vc, axis=-1, keepdims=True) + EPS)
    vhat = vc * rstd
    vn = vhat * lng + lnb
    row = lax.broadcasted_iota(jnp.int32, (GM_CHUNK, GM_CHUNK), 0)
    col = lax.broadcasted_iota(jnp.int32, (GM_CHUNK, GM_CHUNK), 1)
    tril = row >= col
    wms, mixed = [], []
    for g in range(GM_GROUPS):
        sl = slice(g * 128, (g + 1) * 128)
        wm = jnp.where(tril, ws_ref[g], 0.0)
        wms.append(wm)
        mixed.append(_dot(wm, vn[:, sl]) + bs_ref[g])
    return u, du, dv, rstd, vhat, vn, wms, mixed, tril


def _gmlp_fwd(proj, lng, lnb, ws, bs_col):
    T = proj.shape[0]
    n = T // GM_CHUNK

    def body(zu_ref, zv_ref, lng_ref, lnb_ref, ws_ref, bs_ref, o_ref):
        u, _, _, _, _, _, _, mixed, _ = _gmlp_pieces(zu_ref[...].astype(F32), zv_ref[...].astype(F32),
                                                     lng_ref[...], lnb_ref[...],
                                                     ws_ref, bs_ref)
        for g in range(GM_GROUPS):
            sl = slice(g * 128, (g + 1) * 128)
            o_ref[:, sl] = (u[:, sl] * mixed[g]).astype(o_ref.dtype)

    vec = pl.BlockSpec((1, GM_WIDTH), lambda i: (0, 0))
    return _pallas(
        body, name="gmlp_fwd", grid=(n,),
        in_specs=[pl.BlockSpec((GM_CHUNK, 512), lambda i: (i, COL_ZU)),
                  pl.BlockSpec((GM_CHUNK, 512), lambda i: (i, COL_ZV)),
                  vec, vec,
                  pl.BlockSpec((GM_GROUPS, 128, 128), lambda i: (0, 0, 0)),
                  pl.BlockSpec((GM_GROUPS, 128, 1), lambda i: (0, 0, 0))],
        out_specs=pl.BlockSpec((GM_CHUNK, 512), lambda i: (i, 0)),
        out_shape=jax.ShapeDtypeStruct((T, GM_WIDTH), BF16), compiler_params=_cp("parallel"),
    )(proj, proj, lng, lnb, ws, bs_col)


def _gmlp_bwd(proj, d_out, lng, lnb, ws, bs_col, riders=()):
    T = proj.shape[0]
    n = T // GM_CHUNK

    def body(zu_ref, zv_ref, do_ref, lng_ref, lnb_ref, ws_ref, bs_ref,
             dz_ref, dws_ref, dbs_ref, dlng_ref, dlnb_ref, dm_acc):
        i = pl.program_id(0)

        @pl.when(i == 0)
        def _():
            dws_ref[...] = jnp.zeros_like(dws_ref)
            dlng_ref[...] = jnp.zeros_like(dlng_ref)
            dlnb_ref[...] = jnp.zeros_like(dlnb_ref)
            dm_acc[...] = jnp.zeros_like(dm_acc)

        lng_v = lng_ref[...]
        u, du, dv, rstd, vhat, vn, wms, mixed, tril = _gmlp_pieces(zu_ref[...].astype(F32), zv_ref[...].astype(F32),
                                                                  lng_v, lnb_ref[...],
                                                                  ws_ref, bs_ref)
        do = do_ref[...]
        dvn_parts = []
        for g in range(GM_GROUPS):
            sl = slice(g * 128, (g + 1) * 128)
            dog = do[:, sl]
            dz_ref[:, sl] = (dog * mixed[g] * du[:, sl]).astype(dz_ref.dtype)
            dmix = dog * u[:, sl]
            dm_acc[:, sl] += dmix
            dws_ref[g] += jnp.where(tril, _dot_nt(dmix, vn[:, sl]), 0.0)
            dvn_parts.append(_dot_tn(wms[g], dmix))
        dvn = jnp.concatenate(dvn_parts, axis=1)
        dlng_ref[...] += jnp.sum(dvn * vhat, axis=0, keepdims=True)
        dlnb_ref[...] += jnp.sum(dvn, axis=0, keepdims=True)
        dvh = dvn * lng_v
        dvv = rstd * (dvh - jnp.mean(dvh, axis=-1, keepdims=True)
                      - vhat * jnp.mean(dvh * vhat, axis=-1, keepdims=True))
        dz_ref[:, GM_WIDTH:] = (dvv * dv).astype(dz_ref.dtype)

        @pl.when(i == n - 1)
        def _():
            for g in range(GM_GROUPS):
                dbs_ref[g] = jnp.sum(dm_acc[:, g * 128:(g + 1) * 128], axis=1, keepdims=True)

    vec = pl.BlockSpec((1, GM_WIDTH), lambda i: (0, 0))
    wsp = pl.BlockSpec((GM_GROUPS, 128, 128), lambda i: (0, 0, 0))
    bsp = pl.BlockSpec((GM_GROUPS, 128, 1), lambda i: (0, 0, 0))
    return _carried(*_pcall(
        body, (proj, proj, d_out, lng, lnb, ws, bs_col), name="gmlp_bwd", grid=(n,),
        in_specs=[pl.BlockSpec((GM_CHUNK, 512), lambda i: (i, COL_ZU)),
                  pl.BlockSpec((GM_CHUNK, 512), lambda i: (i, COL_ZV)),
                  pl.BlockSpec((None, GM_CHUNK, 512), lambda i: (0, i, 0)), vec, vec, wsp, bsp],
        out_specs=(pl.BlockSpec((GM_CHUNK, 2 * GM_WIDTH), lambda i: (i, 0)), wsp, bsp, vec, vec),
        out_shape=(jax.ShapeDtypeStruct((T, 2 * GM_WIDTH), BF16),
                   jax.ShapeDtypeStruct((GM_GROUPS, 128, 128), F32), jax.ShapeDtypeStruct((GM_GROUPS, 128, 1), F32),
                   jax.ShapeDtypeStruct((1, GM_WIDTH), F32), jax.ShapeDtypeStruct((1, GM_WIDTH), F32)),
        scratch_shapes=[pltpu.VMEM((GM_CHUNK, GM_WIDTH), F32)],
        semantics=("arbitrary",), riders=riders), riders)


def _hgrn_lower_bound(lbl):
    return 1.0 / (1.0 + jnp.exp(lbl[1:2, :] - lbl[0:1, :]))


def _hgrn_gates(hq, hf, lb):
    C = HG_CHUNK
    sg = _sigmoid(hf)
    fg = lb + (1.0 - lb) * sg
    sq = _sigmoid(hq)
    row = lax.broadcasted_iota(jnp.int32, (C, C), 0)
    col = lax.broadcasted_iota(jnp.int32, (C, C), 1)
    tril = row >= col
    logf = jnp.log(fg)
    a = _dot_01(tril, logf)
    a_last = jnp.sum(logf, axis=0, keepdims=True)
    first_half = lax.broadcasted_iota(jnp.int32, logf.shape, 0) < (C // 2)
    a_mid = jnp.sum(jnp.where(first_half, logf, 0.0), axis=0, keepdims=True)
    ea, ei, eki, ekl = jnp.exp(a), jnp.exp(a - a_mid), jnp.exp(a_mid - a), jnp.exp(a_last - a)
    k = 1.0 - fg
    q = hq * sq
    qi = (q * ei).astype(BF16).astype(F32)
    ki = (k * eki).astype(BF16).astype(F32)
    return dict(sg=sg, fg=fg, sq=sq, tril=tril, ea=ea, ei=ei, eki=eki, ekl=ekl, e_last=jnp.exp(a_last),
                qe=q * ea, qi=qi, ki=ki, kl=k * ekl)


def _heads(x):
    return [x[:, h * HG_DIM:(h + 1) * HG_DIM] for h in range(HG_HEADS)]


def _hgrn_fwd(proj, lbl, gh, B, S, riders=()):
    C = HG_CHUNK
    NC = S // C
    W = HG_HEADS * HG_DIM

    def body(q_ref, f_ref, i_ref, g_ref, lbl_ref, gh_ref, o_ref, bo_ref, st_ref, state):
        @pl.when(pl.program_id(0) == 0)
        def _():
            state[...] = jnp.zeros_like(state)

        lb = _hgrn_lower_bound(lbl_ref[...])
        ghv = gh_ref[...]
        for b in range(B):
            gt = _hgrn_gates(q_ref[b].astype(F32), f_ref[b].astype(F32), lb)
            v = _heads(i_ref[b])
            qe, qi, ki, kl, e_last = (_heads(gt[n]) for n in ("qe", "qi", "ki", "kl", "e_last"))
            outs, normed = [], []
            for h in range(HG_HEADS):
                p = jnp.where(gt["tril"], _dot_nt(qi[h], ki[h]), 0.0)
                st = state[b, h]
                st_ref[b, h] = st
                o = _dot_nt(qe[h], st) + _dot(p, v[h])
                state[b, h] = st * e_last[h] + _dot_tn(v[h], kl[h])
                outs.append(o)
                normed.append(o * lax.rsqrt(jnp.mean(o * o, axis=-1, keepdims=True) + EPS) * ghv)
            o_ref[b] = jnp.concatenate(outs, axis=1)
            hg = g_ref[b].astype(F32)
            bo_ref[b] = (jnp.concatenate(normed, axis=1) * (hg * _sigmoid(hg))).astype(bo_ref.dtype)

    def col(cb):
        return pl.BlockSpec((B, C, 512), lambda c: (0, c, cb))

    tile = pl.BlockSpec((B, C, W), lambda c: (0, c, 0))
    proj3 = proj.reshape(B, S, proj.shape[-1])
    out, carried = _pcall(
        body, (proj3, proj3, proj3, proj3, lbl, gh), name="hgrn_fwd", grid=(NC,),
        in_specs=[col(COL_HQ), col(COL_HF), col(COL_HI), col(COL_HG),
                  pl.BlockSpec((2, W), lambda c: (0, 0)), pl.BlockSpec((1, HG_DIM), lambda c: (0, 0))],
        out_specs=(tile, tile, pl.BlockSpec((B, None, HG_HEADS, 128, 128), lambda c: (0, c, 0, 0, 0))),
        out_shape=(jax.ShapeDtypeStruct((B, S, W), F32), jax.ShapeDtypeStruct((B, S, W), BF16),
                   jax.ShapeDtypeStruct((B, NC, HG_HEADS, 128, 128), F32)),
        scratch_shapes=[pltpu.VMEM((B, HG_HEADS, 128, 128), F32)],
        semantics=("arbitrary",), riders=riders)
    o_h, b_out, states = out
    out = (o_h, b_out.reshape(B * S, W), states)
    return (out, carried) if riders else out


def _hgrn_bwd(proj, o_saved, states, d_out, lbl, gh, others, B, S, riders=()):
    C = HG_CHUNK
    NC = S // C
    W = HG_HEADS * HG_DIM
    d_gm, d_xq, d_gates = (t.reshape(B, S, t.shape[-1]) for t in others)
    own0 = d_gm.shape[-1]
    xq0 = own0 + 4 * W
    gates0 = xq0 + d_xq.shape[-1]

    def body(q_ref, f_ref, i_ref, g_ref, o_ref, st_ref, do_ref, lbl_ref, gh_ref, gm_ref, xq_ref, gates_ref,
             d_ref, dlbl_ref, dgh_ref, dstate, dlb_acc):
        c = pl.program_id(0)
        d_ref[:, :, :own0] = gm_ref[...]
        d_ref[:, :, xq0:gates0] = xq_ref[...]
        d_ref[:, :, gates0:] = gates_ref[...]

        def put(b, k, val):
            d_ref[b, :, own0 + k * W:own0 + (k + 1) * W] = val.astype(d_ref.dtype)

        @pl.when(c == 0)
        def _():
            dstate[...] = jnp.zeros_like(dstate)
            dgh_ref[...] = jnp.zeros_like(dgh_ref)
            dlb_acc[...] = jnp.zeros_like(dlb_acc)

        lb = _hgrn_lower_bound(lbl_ref[...])
        ghv = gh_ref[...]
        row = lax.broadcasted_iota(jnp.int32, (C, C), 0)
        colm = lax.broadcasted_iota(jnp.int32, (C, C), 1)
        triu = colm >= row
        for b in range(B):
            hq, hg = q_ref[b].astype(F32), g_ref[b].astype(F32)
            gt = _hgrn_gates(hq, f_ref[b].astype(F32), lb)
            tril = gt["tril"]
            v = _heads(i_ref[b])
            qe, qi, ki, kl, e_last = (_heads(gt[n]) for n in ("qe", "qi", "ki", "kl", "e_last"))
            sgg = _sigmoid(hg)
            don_all = do_ref[b] * (hg * sgg)
            o, don = _heads(o_ref[b]), _heads(don_all)
            d_qe, d_qi, d_ki, d_kl, dv, n_all, dal = [], [], [], [], [], [], []
            for h in range(HG_HEADS):
                r = lax.rsqrt(jnp.mean(o[h] * o[h], axis=-1, keepdims=True) + EPS)
                n = o[h] * r
                n_all.append(n)
                dgh_ref[...] += jnp.sum(don[h] * n, axis=0, keepdims=True)
                dn = don[h] * ghv
                d_o = r * (dn - n * jnp.mean(dn * n, axis=-1, keepdims=True))
                st, dst = st_ref[b, h], dstate[b, h]
                p = jnp.where(tril, _dot_nt(qi[h], ki[h]), 0.0)
                dp = jnp.where(tril, _dot_nt(d_o, v[h]), 0.0)
                d_qe.append(_dot(d_o, st))
                d_qi.append(_dot(dp, ki[h]))
                d_ki.append(_dot_tn(dp, qi[h]))
                d_kl.append(_dot(v[h], dst))
                dv.append(_dot_tn(p, d_o) + _dot_nt(kl[h], dst))
                dstate[b, h] = dst * e_last[h] + _dot_tn(d_o, qe[h])
                dal.append(jnp.sum(dst * st, axis=0, keepdims=True) * e_last[h])
            d_qe, d_qi, d_ki, d_kl, n_all, dal = (jnp.concatenate(t, axis=1)
                                                  for t in (d_qe, d_qi, d_ki, d_kl, n_all, dal))
            put(b, 3, do_ref[b] * n_all * jnp.tile(ghv, (1, HG_HEADS)) * (sgg * (1.0 + hg * (1.0 - sgg))))
            put(b, 2, jnp.concatenate(dv, axis=1))
            d_a_last = dal + jnp.sum(d_kl * gt["kl"], axis=0, keepdims=True)
            dq = d_qe * gt["ea"] + d_qi * gt["ei"]
            dk = d_ki * gt["eki"] + d_kl * gt["ekl"]
            da = d_qe * gt["qe"] + d_qi * gt["qi"] - d_ki * gt["ki"] - d_kl * gt["kl"]
            dlogf = _dot_01(triu, da) + d_a_last
            sg, sq = gt["sg"], gt["sq"]
            dfg = dlogf / gt["fg"] - dk
            put(b, 1, dfg * (1.0 - lb) * sg * (1.0 - sg))
            dlb_acc[...] += jnp.sum(dfg * (1.0 - sg), axis=0, keepdims=True)
            put(b, 0, dq * (sq * (1.0 + hq * (1.0 - sq))))

        @pl.when(c == NC - 1)
        def _():
            dlb = dlb_acc[...]
            first = lax.broadcasted_iota(jnp.int32, (2, W), 0) == 0
            dlbl_ref[...] = jnp.where(first, dlb * lb * (1.0 - lb), -dlb * lb * (1.0 - lb))

    def col(cb):
        return pl.BlockSpec((B, C, 512), lambda c: (0, NC - 1 - c, cb))

    tile = pl.BlockSpec((B, C, W), lambda c: (0, NC - 1 - c, 0))
    proj3 = proj.reshape(B, S, proj.shape[-1])

    def rows(width):
        return pl.BlockSpec((B, C, width), lambda c: (0, NC - 1 - c, 0))

    width = proj.shape[-1]
    out, carried = _pcall(
        body, (proj3, proj3, proj3, proj3, o_saved, states, d_out.reshape(3, B, S, W), lbl, gh, d_gm, d_xq, d_gates),
        name="hgrn_bwd", grid=(NC,),
        in_specs=[col(COL_HQ), col(COL_HF), col(COL_HI), col(COL_HG), tile,
                  pl.BlockSpec((B, None, HG_HEADS, 128, 128), lambda c: (0, NC - 1 - c, 0, 0, 0)),
                  pl.BlockSpec((None, B, C, W), lambda c: (1, 0, NC - 1 - c, 0)),
                  pl.BlockSpec((2, W), lambda c: (0, 0)), pl.BlockSpec((1, HG_DIM), lambda c: (0, 0)),
                  rows(d_gm.shape[-1]), rows(d_xq.shape[-1]), rows(d_gates.shape[-1])],
        out_specs=(rows(width), pl.BlockSpec((2, W), lambda c: (0, 0)), pl.BlockSpec((1, HG_DIM), lambda c: (0, 0))),
        out_shape=(jax.ShapeDtypeStruct((B, S, width), BF16), jax.ShapeDtypeStruct((2, W), F32),
                   jax.ShapeDtypeStruct((1, HG_DIM), F32)),
        scratch_shapes=[pltpu.VMEM((B, HG_HEADS, 128, 128), F32), pltpu.VMEM((1, W), F32)],
        semantics=("arbitrary",), riders=riders)
    out = (out[0].reshape(B * S, width),) + tuple(out[1:])
    return (out, carried) if riders else out


_XA_SCALE = XA_DIM ** -0.5


def _attn_probs(qh, kh):
    s = _dot_nt(qh, kh) * _XA_SCALE
    e = jnp.exp(s - jnp.max(s, axis=-1, keepdims=True))
    return e / jnp.sum(e, axis=-1, keepdims=True)


def _attn_fwd(proj, kv, B, S):
    T = B * S
    tq = _row_tile(S)
    nq = S // tq
    W = XA_HEADS * XA_DIM

    def body(q_ref, kv_ref, o_ref):
        for h in range(XA_HEADS):
            sl = slice(h * 128, (h + 1) * 128)
            p = _attn_probs(q_ref[:, sl], kv_ref[:, sl])
            o_ref[:, sl] = _dot(p, kv_ref[:, W + h * 128:W + (h + 1) * 128]).astype(o_ref.dtype)

    return _pallas(
        body, name="attn_fwd", grid=(B, nq),
        in_specs=[pl.BlockSpec((tq, 512), lambda b, i: (b * nq + i, COL_XQ)),
                  pl.BlockSpec((MEM_LEN, 2 * W), lambda b, i: (b, 0))],
        out_specs=pl.BlockSpec((tq, W), lambda b, i: (b * nq + i, 0)),
        out_shape=jax.ShapeDtypeStruct((T, W), BF16), compiler_params=_cp("parallel", "parallel"),
    )(proj, kv)


def _attn_bwd(proj, kv, d_out, B, S):
    T = B * S
    tq = _row_tile(S)
    nq = S // tq
    W = XA_HEADS * XA_DIM

    def body(q_ref, kv_ref, do_ref, dq_ref, dkv_ref):
        @pl.when(pl.program_id(1) == 0)
        def _():
            dkv_ref[...] = jnp.zeros_like(dkv_ref)

        for h in range(XA_HEADS):
            sl = slice(h * 128, (h + 1) * 128)
            slv = slice(W + h * 128, W + (h + 1) * 128)
            qh = q_ref[:, sl]
            kh = kv_ref[:, sl]
            p = _attn_probs(qh, kh)
            dc = do_ref[:, sl]
            dp = _dot_nt(dc, kv_ref[:, slv])
            ds = p * (dp - jnp.sum(dp * p, axis=-1, keepdims=True)) * _XA_SCALE
            dq_ref[:, sl] = _dot(ds, kh).astype(dq_ref.dtype)
            dkv_ref[:, sl] += _dot_tn(ds, qh)
            dkv_ref[:, slv] += _dot_tn(p, dc)

    kvspec = pl.BlockSpec((MEM_LEN, 2 * W), lambda b, i: (b, 0))
    tile = pl.BlockSpec((tq, W), lambda b, i: (b * nq + i, 0))
    return _pallas(
        body, name="attn_bwd", grid=(B, nq),
        in_specs=[pl.BlockSpec((tq, 512), lambda b, i: (b * nq + i, COL_XQ)), kvspec,
                  pl.BlockSpec((None, tq, W), lambda b, i: (2, b * nq + i, 0))],
        out_specs=(tile, kvspec),
        out_shape=(jax.ShapeDtypeStruct((T, W), BF16), jax.ShapeDtypeStruct((B * MEM_LEN, 2 * W), F32)),
        compiler_params=_cp("parallel", "arbitrary"),
    )(proj, kv, d_out)


_MERGE_TM = 256
_GATE_W = 512


def _gate_specs(tm):
    base = COL_GATE0 // _GATE_W
    return [pl.BlockSpec((tm, _GATE_W), functools.partial(lambda i, k: (i, base + k), k=k)) for k in range(6)]


def _merge_fwd(a_out, b_out, c_out, wb, proj, riders=()):
    T = a_out.shape[0]
    tm = _row_tile(T, _MERGE_TM)
    nq, _, wd = wb.shape
    per_half = _GATE_W // wd

    def body(a_ref, b_ref, c_ref, w_ref, *rest):
        gates, (m_ref, up_ref) = rest[:6], rest[6:]
        for hf in range(2):
            cols = slice(hf * _GATE_W, (hf + 1) * _GATE_W)
            acc = None
            for n, br in enumerate((a_ref, b_ref, c_ref)):
                x = br[...]
                up = jnp.concatenate([_dot(x, w_ref[per_half * hf + j, n * BR_WIDTH:(n + 1) * BR_WIDTH, :])
                                      for j in range(per_half)], axis=1)
                up_ref[n, :, cols] = up.astype(up_ref.dtype)
                term = _sigmoid(gates[2 * n + hf][...].astype(F32)) * up
                acc = term if acc is None else acc + term
            m_ref[:, cols] = acc.astype(m_ref.dtype)

    br_spec = pl.BlockSpec((tm, BR_WIDTH), lambda i: (i, 0))
    return _carried(*_pcall(
        body, (a_out, b_out, c_out, wb, *([proj] * 6)), name="merge_fwd", grid=(T // tm,),
        in_specs=[br_spec, br_spec, br_spec,
                  pl.BlockSpec((nq, 3 * BR_WIDTH, wd), lambda i: (0, 0, 0))] + _gate_specs(tm),
        out_specs=(pl.BlockSpec((tm, D_MODEL), lambda i: (i, 0)), pl.BlockSpec((3, tm, D_MODEL), lambda i: (0, i, 0))),
        out_shape=(jax.ShapeDtypeStruct((T, D_MODEL), BF16), jax.ShapeDtypeStruct((3, T, D_MODEL), BF16)),
        semantics=("parallel",), riders=riders), riders)


def _branch_bwd_act(d_ups, wb, riders=()):
    _, T, D = d_ups.shape
    nq, _, wd = wb.shape
    tm = _row_tile(T)

    def body(d_ref, w_ref, o_ref):
        acc = None
        for q in range(nq):
            part = _dot_nt(d_ref[:, q * wd:(q + 1) * wd], w_ref[q])
            acc = part if acc is None else acc + part
        o_ref[...] = acc

    return _carried(*_pcall(
        body, (d_ups, wb), name="d_branch", grid=(3, T // tm),
        in_specs=[pl.BlockSpec((None, tm, D), lambda n, i: (n, i, 0)),
                  pl.BlockSpec((nq, BR_WIDTH, wd), lambda n, i: (0, n, 0))],
        out_specs=pl.BlockSpec((None, tm, BR_WIDTH), lambda n, i: (n, i, 0)),
        out_shape=jax.ShapeDtypeStruct((3, T, BR_WIDTH), F32), semantics=("parallel", "parallel"),
        riders=riders), riders)


_BRANCH_TOKENS = 2048


def _branch_bwd_weights(brs, d_ups):
    T = brs[0].shape[0]
    n_br, _, D = d_ups.shape
    wd = D // N_CHIPS
    tt = _row_tile(T, _BRANCH_TOKENS)
    nk = T // tt

    def body(*refs):
        b_refs, d_ref, o_ref = refs[:n_br], refs[n_br], refs[n_br + 1]
        n, k = pl.program_id(0), pl.program_id(1)
        for idx in range(n_br):
            @pl.when(n == idx)
            def _(idx=idx):
                for q in range(N_CHIPS):
                    part = _dot_tn(b_refs[idx][...], d_ref[:, q * wd:(q + 1) * wd])

                    @pl.when(k == 0)
                    def _():
                        o_ref[q] = part

                    @pl.when(k > 0)
                    def _():
                        o_ref[q] += part

    def br_spec(idx):
        return pl.BlockSpec(
            (tt, BR_WIDTH), lambda n, k: (jnp.where(n == idx, k, jnp.where(n < idx, 0, nk - 1)), 0))

    return _pallas(
        body, name="g_w_branch", grid=(n_br, nk),
        in_specs=[br_spec(idx) for idx in range(n_br)] + [pl.BlockSpec((None, tt, D), lambda n, k: (n, k, 0))],
        out_specs=pl.BlockSpec((N_CHIPS, BR_WIDTH, wd), lambda n, k: (0, n, 0)),
        out_shape=jax.ShapeDtypeStruct((N_CHIPS, n_br * BR_WIDTH, wd), F32),
        compiler_params=_cp("arbitrary", "arbitrary"),
    )(*brs, d_ups)


def _merge_bwd(d_merged, ups, proj, riders=()):
    T = d_merged.shape[0]
    tm = _row_tile(T, _MERGE_TM)

    def body(dm_ref, up_ref, *rest):
        gates, (dup_ref, dg_ref) = rest[:6], rest[6:]
        for hf in range(2):
            cols = slice(hf * _GATE_W, (hf + 1) * _GATE_W)
            dm = dm_ref[:, cols]
            for n in range(3):
                gate = _sigmoid(gates[2 * n + hf][...].astype(F32))
                dup_ref[n, :, cols] = (dm * gate).astype(dup_ref.dtype)
                dg_ref[:, n * D_MODEL + hf * _GATE_W:n * D_MODEL + (hf + 1) * _GATE_W] = (
                    dm * up_ref[n, :, cols].astype(F32) * gate * (1.0 - gate)).astype(dg_ref.dtype)

    tile = pl.BlockSpec((tm, D_MODEL), lambda i: (i, 0))
    tile3 = pl.BlockSpec((3, tm, D_MODEL), lambda i: (0, i, 0))
    return _carried(*_pcall(
        body, (d_merged, ups, *([proj] * 6)), name="merge_bwd", grid=(T // tm,),
        in_specs=[tile, tile3] + _gate_specs(tm),
        out_specs=(tile3, pl.BlockSpec((tm, 3 * D_MODEL), lambda i: (i, 0))),
        out_shape=(jax.ShapeDtypeStruct((3, T, D_MODEL), BF16), jax.ShapeDtypeStruct((T, 3 * D_MODEL), BF16)),
        semantics=("parallel",), riders=riders), riders)


_CONV_TF = D_FF // 2
_CONV_TS = 256
_HALO = 16


def _conv_fwd(ab, cw, cb, B, S):
    T = B * S
    ts = _row_tile(S, _CONV_TS)
    tf = _CONV_TF
    nb = D_FF // tf
    tps = S // ts
    hb = ts // _HALO

    def body(a_ref, p_ref, b_ref, w_ref, cb_ref, o_ref):
        start = (pl.program_id(0) % tps) == 0
        a = a_ref[...].astype(F32)
        prev = jnp.where(start, 0.0, p_ref[...].astype(F32))
        ext = jnp.concatenate([prev, a], axis=0)
        a1 = pltpu.roll(ext, 1, 0)[_HALO:, :]
        a2 = pltpu.roll(ext, 2, 0)[_HALO:, :]
        ac = cb_ref[...] + w_ref[0] * a2 + w_ref[1] * a1 + w_ref[2] * a
        o_ref[...] = (ac * _sigmoid(ac) * b_ref[...].astype(F32)).astype(o_ref.dtype)

    return _pallas(
        body, name="conv_fwd", grid=(T // ts, nb),
        in_specs=[pl.BlockSpec((ts, tf), lambda i, j: (i, j)),
                  pl.BlockSpec((_HALO, tf), lambda i, j: (jnp.maximum(i * hb - 1, 0), j)),
                  pl.BlockSpec((ts, tf), lambda i, j: (i, j + nb)),
                  pl.BlockSpec((3, 1, tf), lambda i, j: (0, 0, j)),
                  pl.BlockSpec((1, tf), lambda i, j: (0, j))],
        out_specs=pl.BlockSpec((ts, tf), lambda i, j: (i, j)),
        out_shape=jax.ShapeDtypeStruct((T, D_FF), BF16), compiler_params=_cp("parallel", "parallel"),
    )(ab, ab, ab, cw, cb)


def _conv_bwd(ab, d_ff, cw, cb, B, S, riders=()):
    T = B * S
    ts = _row_tile(S, _CONV_TS)
    tf = _CONV_TF
    nb = D_FF // tf
    tps = S // ts
    hb = ts // _HALO
    last_h = T // _HALO - 1
    n_ext = ts + _HALO

    def body(a_ref, ap_ref, an_ref, b_ref, bn_ref, d_ref, dn_ref, w_ref, cb_ref, dab_ref, dw_ref, dcb_ref):
        i = pl.program_id(1)

        @pl.when(i == 0)
        def _():
            dw_ref[...] = jnp.zeros_like(dw_ref)
            dcb_ref[...] = jnp.zeros_like(dcb_ref)

        start = (i % tps) == 0
        end = (i % tps) == tps - 1
        a = a_ref[...].astype(F32)
        ext = jnp.concatenate([jnp.where(start, 0.0, ap_ref[...].astype(F32)), a, an_ref[...].astype(F32)], axis=0)
        r1 = pltpu.roll(ext, 1, 0)[_HALO:, :]
        r2 = pltpu.roll(ext, 2, 0)[_HALO:, :]
        ac = cb_ref[...] + w_ref[0] * r2 + w_ref[1] * r1 + w_ref[2] * ext[_HALO:, :]
        sg = _sigmoid(ac)
        d_e = jnp.concatenate([d_ref[...].astype(F32), jnp.where(end, 0.0, dn_ref[...].astype(F32))], axis=0)
        b_e = jnp.concatenate([b_ref[...].astype(F32), bn_ref[...].astype(F32)], axis=0)
        dab_ref[1] = (d_e[:ts, :] * (ac * sg)[:ts, :]).astype(dab_ref.dtype)
        dac = d_e * b_e * sg * (1.0 + ac * (1.0 - sg))
        u1 = pltpu.roll(dac, n_ext - 1, 0)[:ts, :]
        u2 = pltpu.roll(dac, n_ext - 2, 0)[:ts, :]
        dac0 = dac[:ts, :]
        dab_ref[0] = (w_ref[2] * dac0 + w_ref[1] * u1 + w_ref[0] * u2).astype(dab_ref.dtype)
        dcb_ref[...] += jnp.sum(dac0, axis=0, keepdims=True)
        dw_ref[2] += jnp.sum(dac0 * a, axis=0, keepdims=True)
        dw_ref[1] += jnp.sum(dac0 * r1[:ts, :], axis=0, keepdims=True)
        dw_ref[0] += jnp.sum(dac0 * r2[:ts, :], axis=0, keepdims=True)

    def cur(off):
        return pl.BlockSpec((ts, tf), lambda j, i: (i, j + off))

    def nxt(off):
        return pl.BlockSpec((_HALO, tf), lambda j, i: (jnp.minimum((i + 1) * hb, last_h), j + off))

    return _carried(*_pcall(
        body, (ab, ab, ab, ab, ab, d_ff, d_ff, cw, cb), name="conv_bwd", grid=(nb, T // ts),
        in_specs=[cur(0), pl.BlockSpec((_HALO, tf), lambda j, i: (jnp.maximum(i * hb - 1, 0), j)), nxt(0),
                  cur(nb), nxt(nb), cur(0), nxt(0),
                  pl.BlockSpec((3, 1, tf), lambda j, i: (0, 0, j)), pl.BlockSpec((1, tf), lambda j, i: (0, j))],
        out_specs=(pl.BlockSpec((2, ts, tf), lambda j, i: (0, i, j)), pl.BlockSpec((3, 1, tf), lambda j, i: (0, 0, j)),
                   pl.BlockSpec((1, tf), lambda j, i: (0, j))),
        out_shape=(jax.ShapeDtypeStruct((2, T, D_FF), BF16),
                   jax.ShapeDtypeStruct((3, 1, D_FF), F32), jax.ShapeDtypeStruct((1, D_FF), F32)),
        semantics=("parallel", "arbitrary"), riders=riders), riders)


def _local_step(x, mem, tgt, p, comm, B, S):
    g = {}
    h, slabs = comm.carry(
        "norm1", lambda r: _norm1_and_casts(x, p["norm1_g"], p["cast_beside_norm1"], comm.place, riders=r))
    comm.slabs.update(slabs)
    proj = comm.carry("in_proj", lambda r: _mm_cs("in_proj", h, comm.w("w_in"), BF16, riders=r))
    a_out = _gmlp_fwd(proj, p["ln_v_g"], p["ln_v_b"], p["w_spatial"], p["b_spatial"])
    o_h, b_out, states = comm.carry(
        "hgrn_fwd", lambda r: _hgrn_fwd(proj, p["lb_logits"], p["hgrn_norm_g"], B, S, riders=r))
    memn = _rms_fwd("mem_norm", mem, p["mem_norm_g"])
    kv = _mm_rs("mem_kv", memn, comm.w("w_mem_kv"), F32)
    c_out = _attn_fwd(proj, kv, B, S)
    merged, ups = comm.carry(
        "merge_fwd", lambda r: _merge_fwd(a_out, b_out, c_out, comm.w("w_branch"), proj, riders=r))
    x1, h2 = _proj_res_norm("out_proj_norm2", merged, comm.w("w_out"), x, p["norm2_g"])
    ab = comm.carry("up_proj", lambda r: _mm_cs("up_proj", h2, comm.w("w_up"), BF16, riders=r))
    conv_w = comm.w("conv_w")
    ff = _conv_fwd(ab, conv_w, p["conv_b"], B, S)
    dx2, g["final_g"], loss = _proj_res_loss("down_proj_loss", ff, comm.w("w_down"), x1, tgt, p["final_g"])

    comm.grad("w_down", _mm_tn_rs("g_w_down", ff, dx2, to=D_FF // 2))
    d_ff = comm.carry("d_ff", lambda r: _mm_nt_rs("d_ff", dx2, comm.w("w_down"), BF16, riders=r))
    d_ab, g["conv_w"], g["conv_b"] = comm.carry(
        "conv_bwd", lambda r: _conv_bwd(ab, d_ff, conv_w, p["conv_b"], B, S, riders=r))
    comm.grad("w_up", _mm_tn_cs("g_w_up", h2, d_ab, N_CHIPS, to=512, stacked=True))
    d_x1, g["norm2_g"] = comm.carry("d_h2", lambda r: _mm_nt_cs(
        "d_h2_norm2_bwd", d_ab, comm.w("w_up"), F32, riders=r, stacked=True, norm_bwd=(x1, p["norm2_g"], dx2)))
    comm.grad("w_out", _mm_tn_rs("g_w_out", merged, d_x1, to=512))
    d_merged = _mm_nt_rs("d_merged", d_x1, comm.w("w_out"), F32)
    d_ups, d_gates = comm.carry("merge_bwd", lambda r: _merge_bwd(d_merged, ups, proj, riders=r))

    d_br = comm.carry("d_branch", lambda r: _branch_bwd_act(d_ups, comm.w("w_branch"), riders=r))
    comm.grad("w_branch", _branch_bwd_weights((a_out, b_out, c_out), d_ups))

    d_gm, g["w_spatial"], g["b_spatial"], g["ln_v_g"], g["ln_v_b"] = comm.carry(
        "gmlp_bwd", lambda r: _gmlp_bwd(proj, d_br, p["ln_v_g"], p["ln_v_b"], p["w_spatial"], p["b_spatial"],
                                        riders=r))
    d_xq, d_kv = _attn_bwd(proj, kv, d_br, B, S)
    comm.grad("w_mem_kv", _mm_tn_rs("g_w_mem_kv", memn, d_kv, to=512))
    d_memn = _mm_nt_rs("d_memn", d_kv, comm.w("w_mem_kv"), F32)
    _, g["mem_norm_g"] = _rms_bwd("mem_norm_bwd", mem, p["mem_norm_g"], d_memn, None)
    d_proj, g["lb_logits"], g["hgrn_norm_g"] = comm.carry(
        "hgrn_bwd", lambda r: _hgrn_bwd(proj, o_h, states, d_br, p["lb_logits"], p["hgrn_norm_g"],
                                        (d_gm, d_xq, d_gates), B, S, riders=r))
    comm.small_grads([g[n].reshape(_SMALL_SHAPE[n]) for n in _SMALL_EARLY] + [loss])
    comm.grad("w_in", *comm.carry("g_w_in", lambda r: _mm_tn_cs_to_sibling(
        "g_w_in", h, d_proj, N_CHIPS, comm.place, riders=r, send=comm.sends)))
    grad_x, g["norm1_g"] = comm.carry("d_h", lambda r: _mm_nt_cs(
        "d_h_norm1_bwd", d_proj, comm.w("w_in"), F32, riders=r, norm_bwd=(x, p["norm1_g"], d_x1)))
    return loss, grad_x, g


HBM_SPEC = pl.BlockSpec(memory_space=pltpu.HBM)


def _place():
    x, y, c = lax.axis_index("x"), lax.axis_index("y"), lax.axis_index("c")
    other_chips = [(1 - x, y), (x, 1 - y), (1 - x, 1 - y)]
    return x, y, c, other_chips


def _remote(src, dst, send_sem, recv_sem, dev):
    return pltpu.make_async_remote_copy(src_ref=src, dst_ref=dst, send_sem=send_sem, recv_sem=recv_sem,
                                        device_id=dev, device_id_type=MESH_ID)


class _Exchange:
    def __init__(self, operands, out_shape, aliases, scratch, start, finish, mid=None, mid_at=0.5):
        self.operands, self.out_shape, self.aliases, self.scratch = operands, out_shape, aliases, scratch
        self.start, self.finish, self.mid, self.mid_at = start, finish, mid, mid_at


def _run_exchanges(name, exs):
    n_in = [len(ex.operands) for ex in exs]
    n_out = [len(ex.out_shape) for ex in exs]
    n_scr = [len(ex.scratch) for ex in exs]

    def body(*refs):
        ins, outs, scr = refs[:sum(n_in)], refs[sum(n_in):sum(n_in) + sum(n_out)], refs[sum(n_in) + sum(n_out):]
        parts, oi, oo, os_ = [], 0, 0, 0
        for k in range(len(exs)):
            parts.append((ins[oi:oi + n_in[k]], outs[oo:oo + n_out[k]], scr[os_:os_ + n_scr[k]]))
            oi, oo, os_ = oi + n_in[k], oo + n_out[k], os_ + n_scr[k]
        for ex, part in zip(exs, parts):
            ex.start(*part)
        for ex, part in zip(exs, parts):
            if ex.mid is not None:
                ex.mid(*part)
        for ex, part in zip(exs, parts):
            ex.finish(*part)

    aliases, ops, shapes, scratch, oi, oo = {}, [], [], [], 0, 0
    for k, ex in enumerate(exs):
        aliases.update({oi + a: oo + b for a, b in ex.aliases.items()})
        oi, oo = oi + n_in[k], oo + n_out[k]
        ops += list(ex.operands)
        shapes += [pltpu.HBM(s.shape, s.dtype) for s in ex.out_shape]
        scratch += list(ex.scratch)
    res = _pallas(
        body, name=name, in_specs=[HBM_SPEC] * len(ops), out_specs=(HBM_SPEC,) * len(shapes), out_shape=tuple(shapes),
        input_output_aliases=aliases, scratch_shapes=scratch,
    )(*ops)
    out, oo = [], 0
    for k in range(len(exs)):
        out.append(list(res[oo:oo + n_out[k]]))
        oo += n_out[k]
    return out


def _ex_all_gather(slabs, halved, part=(0, 1)):
    n = len(slabs)

    def rows(a, cc):
        if not halved[a]:
            return slice(None)
        pr = slabs[a].shape[1] // part[1]
        return pl.ds(part[0] * pr + cc * (pr // 2), pr // 2)

    def ici(bufs, scr, a, j, chip, c, mine):
        px, py = chip
        x, y, _, _ = _place()
        qs = 2 * x + y if mine else 2 * px + py
        piece = bufs[a].at[qs, rows(a, c)]
        return _remote(piece, piece, scr[0].at[3 * a + j], scr[1].at[3 * a + j], (px, py, c))

    def d2d(bufs, scr, a, j, chip, cc):
        px, py = chip
        x, y, c, _ = _place()
        piece = bufs[a].at[2 * px + py, rows(a, cc)]
        return _remote(piece, piece, scr[2].at[3 * a + j], scr[3].at[3 * a + j], (x, y, 1 - c))

    def start(ins, outs, scr):
        _, _, c, chips = _place()
        for j, chip in enumerate(chips):
            for a in range(n):
                ici(outs, scr, a, j, chip, c, True).start()

    def finish(ins, outs, scr):
        _, _, c, chips = _place()
        for j, chip in enumerate(chips):
            for a in range(n):
                ici(outs, scr, a, j, chip, c, False).wait_recv()
                if halved[a]:
                    d2d(outs, scr, a, j, chip, c).start()
        for j, chip in enumerate(chips):
            for a in range(n):
                if halved[a]:
                    d2d(outs, scr, a, j, chip, 1 - c).wait_recv()
        for j, chip in enumerate(chips):
            for a in range(n):
                ici(outs, scr, a, j, chip, c, True).wait_send()
                if halved[a]:
                    d2d(outs, scr, a, j, chip, c).wait_send()

    return _Exchange(list(slabs), [jax.ShapeDtypeStruct(s.shape, s.dtype) for s in slabs],
                     {a: a for a in range(n)}, [pltpu.SemaphoreType.DMA((3 * n,))] * 4, start, finish)


def _ex_gather_relay(slabs, mid_at=0.5):
    n = len(slabs)

    def rows(a, cc):
        hr = slabs[a].shape[1] // 2
        return pl.ds(cc * hr, hr)

    def peers():
        x, y, c, _ = _place()
        nbr0 = ((x + c) % 2, (y + 1 - c) % 2)
        nbr1 = ((x + 1 - c) % 2, (y + c) % 2)
        return x, y, c, nbr0, nbr1, (1 - x, 1 - y)

    def ici(bufs, scr, a, k, chip, dev, cc):
        _, _, c, _, _, _ = peers()
        piece = bufs[a].at[2 * chip[0] + chip[1], rows(a, cc)]
        return _remote(piece, piece, scr[0].at[3 * a + k], scr[1].at[3 * a + k], (dev[0], dev[1], c))

    def d2d(bufs, scr, a, k, chip, cc):
        x, y, c, _, _, _ = peers()
        piece = bufs[a].at[2 * chip[0] + chip[1], rows(a, cc)]
        return _remote(piece, piece, scr[2].at[3 * a + k], scr[3].at[3 * a + k], (x, y, 1 - c))

    def start(ins, outs, scr):
        x, y, c, nbr0, nbr1, _ = peers()
        for a in range(n):
            ici(outs, scr, a, 0, (x, y), nbr0, c).start()
            ici(outs, scr, a, 1, (x, y), nbr1, c).start()

    def mid(ins, outs, scr):
        x, y, c, nbr0, nbr1, diag = peers()
        for a in range(n):
            ici(outs, scr, a, 0, nbr0, nbr0, c).wait_recv()
            ici(outs, scr, a, 2, nbr0, nbr1, c).start()
            d2d(outs, scr, a, 0, nbr0, c).start()
        for a in range(n):
            ici(outs, scr, a, 1, nbr1, nbr1, c).wait_recv()
            d2d(outs, scr, a, 1, nbr1, c).start()

    def finish(ins, outs, scr):
        x, y, c, nbr0, nbr1, diag = peers()
        for a in range(n):
            ici(outs, scr, a, 2, diag, nbr1, c).wait_recv()
            d2d(outs, scr, a, 2, diag, c).start()
        for a in range(n):
            d2d(outs, scr, a, 0, nbr1, 1 - c).wait_recv()
            d2d(outs, scr, a, 1, nbr0, 1 - c).wait_recv()
            d2d(outs, scr, a, 2, diag, 1 - c).wait_recv()
        for a in range(n):
            ici(outs, scr, a, 0, (x, y), nbr0, c).wait_send()
            ici(outs, scr, a, 1, (x, y), nbr1, c).wait_send()
            ici(outs, scr, a, 2, nbr0, nbr1, c).wait_send()
            d2d(outs, scr, a, 0, nbr0, c).wait_send()
            d2d(outs, scr, a, 1, nbr1, c).wait_send()
            d2d(outs, scr, a, 2, diag, c).wait_send()

    return _Exchange(list(slabs), [jax.ShapeDtypeStruct(s.shape, s.dtype) for s in slabs],
                     {a: a for a in range(n)}, [pltpu.SemaphoreType.DMA((3 * n,))] * 4, start, finish, mid, mid_at)


def _ex_to_sibling(grads):
    n = len(grads)

    def copy(ins, outs, scr, a):
        x, y, c, _ = _place()
        hr = grads[a].shape[1] // 2
        return _remote(ins[a].at[:, pl.ds((1 - c) * hr, hr), :], outs[a], scr[0].at[a], scr[1].at[a], (x, y, 1 - c))

    def start(ins, outs, scr):
        for a in range(n):
            copy(ins, outs, scr, a).start()

    def finish(ins, outs, scr):
        for a in range(n):
            copy(ins, outs, scr, a).wait()

    out_shape = [jax.ShapeDtypeStruct((g.shape[0], g.shape[1] // 2, g.shape[2]), g.dtype) for g in grads]
    return _Exchange(list(grads), out_shape, {}, [pltpu.SemaphoreType.DMA((n,))] * 2, start, finish)


def _ex_to_owner(parts, part=(0, 1), landing=None):
    n = len(parts)

    def copy(ins, outs, scr, a, j, chip):
        _, _, c, _ = _place()
        px, py = chip
        pr = parts[a].shape[1] // part[1]
        rows = pl.ds(part[0] * pr, pr)
        return _remote(ins[a].at[2 * px + py, rows], outs[a].at[j, rows], scr[0].at[3 * a + j],
                       scr[1].at[3 * a + j], (px, py, c))

    def start(ins, outs, scr):
        for j, chip in enumerate(_place()[3]):
            for a in range(n):
                copy(ins, outs, scr, a, j, chip).start()

    def finish(ins, outs, scr):
        for j, chip in enumerate(_place()[3]):
            for a in range(n):
                copy(ins, outs, scr, a, j, chip).wait()

    out_shape = [jax.ShapeDtypeStruct((3,) + p.shape[1:], p.dtype) for p in parts]
    operands, aliases = list(parts), {}
    if landing is not None:
        operands, aliases = operands + list(landing), {n + a: a for a in range(n)}
    return _Exchange(operands, out_shape, aliases, [pltpu.SemaphoreType.DMA((3 * n,))] * 2, start, finish)


def _ex_share_halves(bufs):
    n = len(bufs)

    def copy(outs, scr, a, cc):
        x, y, c, _ = _place()
        hr = bufs[a].shape[0] // 2
        piece = outs[a].at[pl.ds(cc * hr, hr), :]
        return _remote(piece, piece, scr[0].at[a], scr[1].at[a], (x, y, 1 - c))

    def start(ins, outs, scr):
        c = _place()[2]
        for a in range(n):
            copy(outs, scr, a, c).start()

    def finish(ins, outs, scr):
        c = _place()[2]
        for a in range(n):
            copy(outs, scr, a, c).wait_send()
            copy(outs, scr, a, 1 - c).wait_recv()

    return _Exchange(list(bufs), [jax.ShapeDtypeStruct(b.shape, b.dtype) for b in bufs], {a: a for a in range(n)},
                     [pltpu.SemaphoreType.DMA((n,))] * 2, start, finish)


def _ex_gather_small(arrs):
    n = len(arrs)

    def peer_of(m):
        x, y, c, _ = _place()
        return (1 - x if m & 4 else x, 1 - y if m & 2 else y, 1 - c if m & 1 else c)

    def own(ins, outs, scr, a):
        x, y, c, _ = _place()
        return pltpu.make_async_copy(ins[a], outs[a].at[4 * x + 2 * y + c], scr[2].at[a])

    def start(ins, outs, scr):
        x, y, c, _ = _place()
        for a in range(n):
            own(ins, outs, scr, a).start()
        for m in range(1, N_DEV):
            for a in range(n):
                k = (N_DEV - 1) * a + m - 1
                _remote(ins[a], outs[a].at[4 * x + 2 * y + c], scr[0].at[k], scr[1].at[k], peer_of(m)).start()

    def finish(ins, outs, scr):
        for a in range(n):
            own(ins, outs, scr, a).wait()
        for m in range(1, N_DEV):
            px, py, pc = peer_of(m)
            for a in range(n):
                k = (N_DEV - 1) * a + m - 1
                slot = outs[a].at[4 * px + 2 * py + pc]
                cp = _remote(ins[a], slot, scr[0].at[k], scr[1].at[k], (px, py, pc))
                cp.wait_send()
                cp.wait_recv()

    out_shape = [jax.ShapeDtypeStruct((N_DEV,) + a.shape, a.dtype) for a in arrs]
    return _Exchange(list(arrs), out_shape, {},
                     [pltpu.SemaphoreType.DMA(((N_DEV - 1) * n,))] * 2 + [pltpu.SemaphoreType.DMA((n,))], start, finish)


def _div_tile(n, want):
    best = None
    for t in range(8, min(n, want) + 1, 8):
        if n % t == 0:
            best = t
    assert best is not None, n
    return best


def _cast_into_slab(name, w, place, dtype):
    r, cc = w.shape
    tr = r if r * cc <= 128 * 1024 else _div_tile(r, 256)

    def body(s_ref, w_ref, o_ref):
        o_ref[...] = w_ref[...].astype(o_ref.dtype)

    return _pallas(
        body, name=name,
        grid_spec=pltpu.PrefetchScalarGridSpec(
            num_scalar_prefetch=1, grid=(r // tr,),
            in_specs=[pl.BlockSpec((tr, cc), lambda i, s: (i, 0))],
            out_specs=pl.BlockSpec((None, tr, cc), lambda i, s: (s[0], i, 0))),
        out_shape=jax.ShapeDtypeStruct((N_CHIPS, r, cc), dtype), compiler_params=_cp("parallel"),
    )(place, w)


def _add_half(name, g, rcv, place):
    nq, r, cc = g.shape
    hr = r // 2

    def body(s_ref, g_ref, r_ref, o_ref):
        o_ref[...] = (g_ref[...] + r_ref[...]).astype(o_ref.dtype)

    spec = pl.BlockSpec((None, hr, cc), lambda i, s: (i, 0, 0))
    return _pallas(
        body, name=name,
        grid_spec=pltpu.PrefetchScalarGridSpec(
            num_scalar_prefetch=1, grid=(nq,),
            in_specs=[pl.BlockSpec((None, hr, cc), lambda i, s: (i, s[1], 0)), spec], out_specs=spec),
        out_shape=jax.ShapeDtypeStruct((nq, hr, cc), BF16), compiler_params=_cp("parallel"),
    )(place, g, rcv)


def _sum_owner(name, part, rcv, place):
    _, hr, cc = part.shape
    tr = _div_tile(hr, 128)
    nb = hr // tr

    def body(s_ref, p_ref, r_ref, o_ref):
        o_ref[...] = ((p_ref[...].astype(F32) + r_ref[0].astype(F32)) + r_ref[1].astype(F32)) + r_ref[2].astype(F32)

    return _pallas(
        body, name=name,
        grid_spec=pltpu.PrefetchScalarGridSpec(
            num_scalar_prefetch=1, grid=(nb,),
            in_specs=[pl.BlockSpec((None, tr, cc), lambda i, s: (s[0], i, 0)),
                      pl.BlockSpec((3, tr, cc), lambda i, s: (0, i, 0))],
            out_specs=pl.BlockSpec((tr, cc), lambda i, s: (s[1] * nb + i, 0))),
        out_shape=jax.ShapeDtypeStruct((2 * hr, cc), F32), compiler_params=_cp("parallel"),
    )(place, part, rcv)


def _sum_small(gathered, local, place):
    n = len(gathered)

    def body(s_ref, *refs):
        g_refs, l_refs, o_refs = refs[:n], refs[n:2 * n], refs[2 * n:]
        me = s_ref[2]
        for g_ref, l_ref, o_ref in zip(g_refs, l_refs, o_refs):
            acc = None
            for d in range(N_DEV):
                term = jnp.where(me == d, l_ref[...], g_ref[d])
                acc = term if acc is None else acc + term
            o_ref[...] = acc

    def whole(shape):
        return pl.BlockSpec(shape, lambda i, s, nd=len(shape): (0,) * nd)

    return _pallas(
        body, name="sum_small",
        grid_spec=pltpu.PrefetchScalarGridSpec(
            num_scalar_prefetch=1, grid=(1,),
            in_specs=[whole(g.shape) for g in gathered] + [whole(a.shape) for a in local],
            out_specs=tuple(whole(a.shape) for a in local)),
        out_shape=tuple(jax.ShapeDtypeStruct(a.shape, a.dtype) for a in local), compiler_params=_cp("arbitrary"),
    )(place, *gathered, *local)


def _adamw(name, w, g, m, v):
    r, cc = w.shape
    tr = r if r * cc <= 128 * 1024 else _div_tile(r, 256)

    def body(w_ref, g_ref, m_ref, v_ref, d_ref, mo_ref, vo_ref, go_ref):
        gv = g_ref[...]
        go_ref[...] = gv
        mn = ADAM_B1 * m_ref[...] + (1.0 - ADAM_B1) * gv
        vn = ADAM_B2 * v_ref[...] + (1.0 - ADAM_B2) * (gv * gv)
        m_hat = mn / (1.0 - ADAM_B1 ** ADAM_STEP)
        v_hat = vn / (1.0 - ADAM_B2 ** ADAM_STEP)
        d_ref[...] = -ADAM_LR * (m_hat / (jnp.sqrt(v_hat) + ADAM_EPS) + ADAM_WD * w_ref[...])
        mo_ref[...] = mn
        vo_ref[...] = vn

    spec = pl.BlockSpec((tr, cc), lambda i: (i, 0))
    sd = jax.ShapeDtypeStruct((r, cc), F32)
    return _pallas(
        body, name=name, grid=(r // tr,), in_specs=[spec] * 4, out_specs=(spec,) * 4, out_shape=(sd,) * 4,
        compiler_params=_cp("parallel"),
    )(w, g, m, v)


_BIG = ("w_in", "w_up", "w_branch", "w_mem_kv", "w_out", "w_down")
_BIG_SHARD_SHAPE = {"w_in": (1024, 1664), "w_up": (1024, 1408), "w_branch": (1536, 256),
                    "w_mem_kv": (256, 1024), "w_out": (256, 1024), "w_down": (704, 1024)}
_SMALL_SHAPE = {"norm1_g": (1, D_MODEL), "ln_v_g": (1, GM_WIDTH), "ln_v_b": (1, GM_WIDTH),
                "w_spatial": (GM_GROUPS * GM_CHUNK, GM_CHUNK), "b_spatial": (GM_GROUPS, GM_CHUNK),
                "lb_logits": (2, HG_HEADS * HG_DIM), "hgrn_norm_g": (1, HG_DIM), "mem_norm_g": (1, D_MODEL),
                "norm2_g": (1, D_MODEL), "conv_w": (3, D_FF), "conv_b": (1, D_FF), "final_g": (1, D_MODEL)}
_SMALL_EARLY = tuple(n for n in _SMALL_SHAPE if n != "norm1_g")
_PARAM_ORDER = ("norm1_g", "w_in", "ln_v_g", "ln_v_b", "w_spatial", "b_spatial", "lb_logits", "hgrn_norm_g",
                "mem_norm_g", "w_mem_kv", "w_branch", "w_out", "norm2_g", "w_up", "conv_w", "conv_b", "w_down",
                "final_g")


def _adamw_small(ws, gs, ms, vs):
    n = len(ws)

    def body(*refs):
        w_refs, g_refs, m_refs, v_refs = refs[:n], refs[n:2 * n], refs[2 * n:3 * n], refs[3 * n:4 * n]
        d_refs, mo_refs, vo_refs = refs[4 * n:5 * n], refs[5 * n:6 * n], refs[6 * n:]
        for k in range(n):
            gv = g_refs[k][...]
            mn = ADAM_B1 * m_refs[k][...] + (1.0 - ADAM_B1) * gv
            vn = ADAM_B2 * v_refs[k][...] + (1.0 - ADAM_B2) * (gv * gv)
            m_hat = mn / (1.0 - ADAM_B1 ** ADAM_STEP)
            v_hat = vn / (1.0 - ADAM_B2 ** ADAM_STEP)
            d_refs[k][...] = -ADAM_LR * (m_hat / (jnp.sqrt(v_hat) + ADAM_EPS) + ADAM_WD * w_refs[k][...])
            mo_refs[k][...] = mn
            vo_refs[k][...] = vn

    specs = [pl.BlockSpec(a.shape, lambda i, nd=a.ndim: (0,) * nd) for a in ws]
    shapes = tuple(jax.ShapeDtypeStruct(a.shape, F32) for a in ws)
    res = _pallas(
        body, name="adamw_small", grid=(1,), in_specs=specs * 4, out_specs=tuple(specs * 3), out_shape=shapes * 3,
        compiler_params=_cp("arbitrary"),
    )(*ws, *gs, *ms, *vs)
    return res[:n], res[n:2 * n], res[2 * n:]


class _Comm:
    _ROW_SHARDED = ("w_mem_kv", "w_out", "w_down")

    def __init__(self, slabs, place):
        self.slabs, self.place = slabs, place
        self.full, self.raw, self.parts, self.landing, self.bufs, self.done = {}, {}, {}, {}, {}, {}

    def w(self, name):
        a = self.full[name]
        if name in self._ROW_SHARDED:
            return a.reshape(-1, a.shape[-1])
        if name == "conv_w":
            return jnp.transpose(a, (1, 0, 2)).reshape(3, 1, D_FF)
        return a

    sends = True

    def grad(self, name, arr, from_sibling=None):
        self.raw[name] = arr.reshape((N_CHIPS, -1, arr.shape[-1]))
        if from_sibling is not None:
            self.parts[name] = _add_half("rs_add_" + name, self.raw[name], from_sibling, self.place)

    def small_grads(self, arrays):
        self.small_local = list(arrays)

    def carry(self, tag, call):
        plan = self._plan(tag)
        if not plan:
            return call(())
        out, carried = call([ex for ex, _ in plan])
        for (_, deliver), res in zip(plan, carried):
            deliver(res)
        return out

    def finish(self, last_small):
        ex, deliver = self._share(["w_out", "w_branch", "w_mem_kv", "w_in"])
        shared, small = _run_exchanges("share_and_gather_last", [ex, _ex_gather_small(last_small)])
        deliver(shared)
        return self.done, self.small_local + list(last_small), self.small_everyone + small

    def _plan(self, tag):
        if tag == "norm1":
            def deliver(res):
                self.full["w_in"] = res[0]

            return [(_ex_gather_relay([self.slabs["w_in"]]), deliver)]
        if tag == "in_proj":
            return [self._gather_relay(["w_branch", "w_out", "w_mem_kv", "w_down"], 0.6), self._gather(["conv_w"])]
        if tag == "hgrn_fwd":
            return [self._gather_relay(["w_up"], 0.8)]
        if tag == "d_h2":
            return [self._to_sibling(["w_down", "w_up"])]
        if tag == "merge_bwd":
            return [self._to_owner(["w_up"], (0, 2))]
        if tag == "hgrn_bwd":
            return [self._to_owner(["w_down"]), self._to_owner(["w_up"], (1, 2)),
                    self._to_sibling(["w_out", "w_branch", "w_mem_kv"])]
        if tag == "g_w_in":
            def keep(res):
                self.small_everyone = res

            return [self._to_owner(["w_out", "w_branch", "w_mem_kv"]), (_ex_gather_small(self.small_local), keep)]
        if tag == "d_h":
            return [self._to_owner(["w_in"]), self._share(["w_down", "w_up"])]
        return []

    def _gather(self, names, part=(0, 1)):
        def deliver(res):
            self.slabs.update(zip(names, res))
            self.full.update(zip(names, res))

        return _ex_all_gather([self.slabs[n] for n in names], [n != "conv_w" for n in names], part), deliver

    def _gather_relay(self, names, mid_at):
        return _ex_gather_relay([self.slabs[n] for n in names], mid_at), lambda res: self.full.update(zip(names, res))

    def _to_sibling(self, names):
        def deliver(res):
            for n, r in zip(names, res):
                self.parts[n] = _add_half("rs_add_" + n, self.raw[n], r, self.place)

        return _ex_to_sibling([self.raw[n] for n in names]), deliver

    def _to_owner(self, names, part=(0, 1)):
        def deliver(res):
            for n, r in zip(names, res):
                if part[0] + 1 < part[1]:
                    self.landing[n] = r
                else:
                    self.bufs[n] = _sum_owner("rs_sum_" + n, self.parts[n], r, self.place)

        landing = [self.landing[n] for n in names] if part[0] else None
        return _ex_to_owner([self.parts[n] for n in names], part, landing), deliver

    def _share(self, names):
        return _ex_share_halves([self.bufs[n] for n in names]), lambda res: self.done.update(zip(names, res))


def kernel(x, mem, norm1_g, w_in, ln_v_g, ln_v_b, w_spatial, b_spatial, lb_logits, hgrn_norm_g, mem_norm_g, w_mem_kv, w_branch, w_out, norm2_g, w_up, conv_w, conv_b, w_down, final_g, loss_target, m_norm1_g, m_w_in, m_ln_v_g, m_ln_v_b, m_w_spatial, m_b_spatial, m_lb_logits, m_hgrn_norm_g, m_mem_norm_g, m_w_mem_kv, m_w_branch, m_w_out, m_norm2_g, m_w_up, m_conv_w, m_conv_b, m_w_down, m_final_g, v_norm1_g, v_w_in, v_ln_v_g, v_ln_v_b, v_w_spatial, v_b_spatial, v_lb_logits, v_hgrn_norm_g, v_mem_norm_g, v_w_mem_kv, v_w_branch, v_w_out, v_norm2_g, v_w_up, v_conv_w, v_conv_b, v_w_down, v_final_g):
    w = dict(norm1_g=norm1_g, w_in=w_in, ln_v_g=ln_v_g, ln_v_b=ln_v_b, w_spatial=w_spatial, b_spatial=b_spatial,
             lb_logits=lb_logits, hgrn_norm_g=hgrn_norm_g, mem_norm_g=mem_norm_g, w_mem_kv=w_mem_kv,
             w_branch=w_branch, w_out=w_out, norm2_g=norm2_g, w_up=w_up, conv_w=conv_w, conv_b=conv_b,
             w_down=w_down, final_g=final_g)
    mom = dict(norm1_g=m_norm1_g, w_in=m_w_in, ln_v_g=m_ln_v_g, ln_v_b=m_ln_v_b, w_spatial=m_w_spatial,
               b_spatial=m_b_spatial, lb_logits=m_lb_logits, hgrn_norm_g=m_hgrn_norm_g, mem_norm_g=m_mem_norm_g,
               w_mem_kv=m_w_mem_kv, w_branch=m_w_branch, w_out=m_w_out, norm2_g=m_norm2_g, w_up=m_w_up,
               conv_w=m_conv_w, conv_b=m_conv_b, w_down=m_w_down, final_g=m_final_g)
    var = dict(norm1_g=v_norm1_g, w_in=v_w_in, ln_v_g=v_ln_v_g, ln_v_b=v_ln_v_b, w_spatial=v_w_spatial,
               b_spatial=v_b_spatial, lb_logits=v_lb_logits, hgrn_norm_g=v_hgrn_norm_g, mem_norm_g=v_mem_norm_g,
               w_mem_kv=v_w_mem_kv, w_branch=v_w_branch, w_out=v_w_out, norm2_g=v_norm2_g, w_up=v_w_up,
               conv_w=v_conv_w, conv_b=v_conv_b, w_down=v_w_down, final_g=v_final_g)
    B, S, D = x.shape
    T = B * S
    ci = lax.axis_index("c")
    q = 2 * lax.axis_index("x") + lax.axis_index("y")
    place = jnp.stack([q, ci, 2 * q + ci]).astype(jnp.int32)

    shards = {n: w[n].reshape(_BIG_SHARD_SHAPE[n]) for n in _BIG}
    slabs = {"w_in": _cast_into_slab("slab_w_in", shards.pop("w_in"), place, BF16),
             "conv_w": _cast_into_slab("slab_conv_w", conv_w[0], place, F32)}
    comm = _Comm(slabs, place)
    p = dict(
        cast_beside_norm1=shards,
        norm1_g=norm1_g, ln_v_g=ln_v_g, ln_v_b=ln_v_b, w_spatial=w_spatial[0],
        b_spatial=b_spatial.reshape(GM_GROUPS, GM_CHUNK, 1), lb_logits=lb_logits, hgrn_norm_g=hgrn_norm_g,
        mem_norm_g=mem_norm_g, norm2_g=norm2_g, conv_b=conv_b, final_g=final_g.reshape(1, D))

    loss, grad_x, g = _local_step(x.reshape(T, D), mem.reshape(B * MEM_LEN, D), loss_target.reshape(T, D), p, comm,
                                  B, S)

    shard_grads, local_small, everyone = comm.finish([g["norm1_g"]])
    summed = _sum_small(everyone, local_small, place)
    small_names = list(_SMALL_EARLY) + ["norm1_g"]
    total = dict(zip(_SMALL_EARLY, summed))
    loss_total, total["norm1_g"] = summed[len(_SMALL_EARLY)][0, 0], summed[-1]

    grads, delta, new_m, new_v = {}, {}, {}, {}
    for n in _BIG:
        shp = _BIG_SHARD_SHAPE[n]
        delta[n], new_m[n], new_v[n], grads[n] = _adamw("adamw_" + n, w[n].reshape(shp), shard_grads[n],
                                                        mom[n].reshape(shp), var[n].reshape(shp))
    cw_shard = D_FF // N_CHIPS
    total["conv_w"] = lax.dynamic_slice(total["conv_w"], (0, q * cw_shard), (3, cw_shard)).reshape(3, 1, cw_shard)

    def flat2d(d, n):
        return d[n].reshape(total[n].shape)

    upd = _adamw_small([flat2d(w, n) for n in small_names], [total[n] for n in small_names],
                       [flat2d(mom, n) for n in small_names], [flat2d(var, n) for n in small_names])
    for k, n in enumerate(small_names):
        grads[n], delta[n], new_m[n], new_v[n] = total[n], upd[0][k], upd[1][k], upd[2][k]

    def shaped(d):
        return [d[n].reshape(w[n].shape) for n in _PARAM_ORDER]

    return (loss_total, grad_x.reshape(B, S, D), *shaped(grads), *shaped(delta), *shaped(new_m), *shaped(new_v))
```

```python
import functools
import math

import jax
import jax.numpy as jnp
from jax import lax
from jax.experimental import pallas as pl
from jax.experimental.pallas import tpu as pltpu

F32 = jnp.float32
BF16 = jnp.bfloat16
EPS = 1e-6

D_MODEL = 1024
MEM_LEN = 256
GM_WIDTH = 512
GM_CHUNK = 128
GM_GROUPS = 4
HG_HEADS = 4
HG_DIM = 128
HG_CHUNK = 64
XA_HEADS = 4
XA_DIM = 128
BR_WIDTH = 512
D_FF = 2816
IN_WIDTH = 6656
N_CHIPS = 4
N_DEV = 8

ADAM_LR = 0.001
ADAM_B1 = 0.9
ADAM_B2 = 0.999
ADAM_EPS = 1e-08
ADAM_WD = 0.01
ADAM_STEP = 10

COL_ZU, COL_ZV, COL_HQ, COL_HF, COL_HI, COL_HG, COL_XQ = 0, 1, 2, 3, 4, 5, 6
COL_GATE0 = 3584

VMEM_LIMIT_BYTES = 48 * 1024 * 1024
MESH_ID = pl.DeviceIdType.MESH


def _cp(*sem):
    return pltpu.CompilerParams(dimension_semantics=sem, vmem_limit_bytes=VMEM_LIMIT_BYTES)


def _pallas(body, *, out_shape, **kw):
    def pin(s):
        return pltpu.HBM(s.shape, s.dtype) if isinstance(s, jax.ShapeDtypeStruct) else s

    out_shape = tuple(pin(s) for s in out_shape) if isinstance(out_shape, (tuple, list)) else pin(out_shape)
    call = pl.pallas_call(body, out_shape=out_shape, **kw)

    def run(*operands):
        return call(*[pltpu.with_memory_space_constraint(o, pltpu.HBM) if jnp.issubdtype(o.dtype, jnp.floating)
                      else o for o in operands])

    return run


def _dot(a, b):
    return lax.dot_general(a.astype(BF16), b.astype(BF16), (((1,), (0,)), ((), ())), preferred_element_type=F32)


def _dot_nt(a, b):
    return lax.dot_general(a.astype(BF16), b.astype(BF16), (((1,), (1,)), ((), ())), preferred_element_type=F32)


def _dot_tn(a, b):
    return lax.dot_general(a.astype(BF16), b.astype(BF16), (((0,), (0,)), ((), ())), preferred_element_type=F32)


def _dot_01(mask01, x):
    hi = x.astype(BF16)
    r1 = x - hi.astype(F32)
    mid = r1.astype(BF16)
    lo = (r1 - mid.astype(F32)).astype(BF16)
    m = mask01.astype(BF16)
    dn = (((1,), (0,)), ((), ()))
    return (lax.dot_general(m, hi, dn, preferred_element_type=F32)
            + lax.dot_general(m, mid, dn, preferred_element_type=F32)
            + lax.dot_general(m, lo, dn, preferred_element_type=F32))


def _sigmoid(z):
    return 1.0 / (1.0 + jnp.exp(-z))


_GELU_C = math.sqrt(2.0 / math.pi)


def _gelu_and_grad(z):
    inner = _GELU_C * (z + 0.044715 * z * z * z)
    t = jnp.tanh(inner)
    val = 0.5 * z * (1.0 + t)
    grad = 0.5 * (1.0 + t) + 0.5 * z * (1.0 - t * t) * _GELU_C * (1.0 + 3.0 * 0.044715 * z * z)
    return val, grad


def _row_tile(n, want=512):
    t = min(want, n)
    assert n % t == 0
    return t


def _pcall(body, operands, *, name, grid, in_specs, out_specs, out_shape, scratch_shapes=(), semantics, riders=(),
           prefetch=None):
    single = not isinstance(out_shape, (tuple, list))
    out_specs = (out_specs,) if single else tuple(out_specs)
    out_shape = (out_shape,) if single else tuple(out_shape)
    n_pre = 0 if prefetch is None else 1

    def call(fn, ins_, outs_, shapes_, scr_, ops, sem, aliases):
        if prefetch is None:
            return _pallas(fn, name=name, grid=grid, in_specs=ins_, out_specs=outs_, out_shape=shapes_,
                           scratch_shapes=scr_, input_output_aliases=aliases, compiler_params=_cp(*sem))(*ops)
        spec = pltpu.PrefetchScalarGridSpec(num_scalar_prefetch=1, grid=grid, in_specs=ins_, out_specs=outs_,
                                            scratch_shapes=scr_)
        return _pallas(fn, name=name, grid_spec=spec, out_shape=shapes_, input_output_aliases=aliases,
                       compiler_params=_cp(*sem))(prefetch, *ops)

    if not riders:
        res = call(body, list(in_specs), out_specs, out_shape, list(scratch_shapes), operands, semantics, {})
        return (res[0] if single else res), []
    n_in, n_out, n_scr = len(in_specs), len(out_shape), len(scratch_shapes)
    ex_in = [len(ex.operands) for ex in riders]
    ex_out = [len(ex.out_shape) for ex in riders]
    ex_scr = [len(ex.scratch) for ex in riders]
    tot_in, tot_out = n_in + sum(ex_in), n_out + sum(ex_out)

    def wrapped(*refs):
        pre, refs = refs[:n_pre], refs[n_pre:]
        ins, outs, scr = refs[:tot_in], refs[tot_in:tot_in + tot_out], refs[tot_in + tot_out:]
        ids = [pl.program_id(d) for d in range(len(grid))]
        first = functools.reduce(lambda p, t: p & t, [i == 0 for i in ids])
        last = functools.reduce(lambda p, t: p & t, [i == n - 1 for i, n in zip(ids, grid)])
        parts, oi, oo, os_ = [], n_in, n_out, n_scr
        for k in range(len(riders)):
            parts.append((ins[oi:oi + ex_in[k]], outs[oo:oo + ex_out[k]], scr[os_:os_ + ex_scr[k]]))
            oi, oo, os_ = oi + ex_in[k], oo + ex_out[k], os_ + ex_scr[k]

        @pl.when(first)
        def _():
            for ex, part in zip(riders, parts):
                ex.start(*part)

        step, total = 0, 1
        for i, n in zip(ids, grid):
            step, total = step * n + i, total * n
        for ex, part in zip(riders, parts):
            if ex.mid is not None:
                @pl.when(step == min(total - 1, int(total * ex.mid_at)))
                def _(ex=ex, part=part):
                    ex.mid(*part)

        body(*pre, *ins[:n_in], *outs[:n_out], *scr[:n_scr])

        @pl.when(last)
        def _():
            for ex, part in zip(riders, parts):
                ex.finish(*part)

    aliases, oi, oo = {}, n_in, n_out
    all_ops, all_shapes, all_scr = list(operands), list(out_shape), list(scratch_shapes)
    for k, ex in enumerate(riders):
        aliases.update({n_pre + oi + a: oo + b for a, b in ex.aliases.items()})
        oi, oo = oi + ex_in[k], oo + ex_out[k]
        all_ops += list(ex.operands)
        all_shapes += [pltpu.HBM(s.shape, s.dtype) for s in ex.out_shape]
        all_scr += list(ex.scratch)
    res = call(wrapped, list(in_specs) + [HBM_SPEC] * sum(ex_in), out_specs + (HBM_SPEC,) * sum(ex_out),
               tuple(all_shapes), all_scr, all_ops, ["arbitrary"] * len(grid), aliases)
    own = res[0] if single else tuple(res[:n_out])
    carried, oo = [], n_out
    for k in range(len(riders)):
        carried.append(list(res[oo:oo + ex_out[k]]))
        oo += ex_out[k]
    return own, carried


def _carried(out, carried, riders):
    return (out, carried) if riders else out


def _matmul(name, operands, *, grid, in_specs, o_spec, out_shape, out_dtype, dims, riders=()):
    nk = grid[2]
    assert nk == 1 or out_dtype == F32

    def body(a_ref, b_ref, o_ref):
        part = lax.dot_general(a_ref[...].astype(BF16), b_ref[...].astype(BF16), (dims, ((), ())),
                               preferred_element_type=F32)
        if nk == 1:
            o_ref[...] = part.astype(o_ref.dtype)
        else:
            k = pl.program_id(2)

            @pl.when(k == 0)
            def _():
                o_ref[...] = part

            @pl.when(k > 0)
            def _():
                o_ref[...] += part

    out, carried = _pcall(body, operands, name=name, grid=grid, in_specs=in_specs, out_specs=o_spec,
                          out_shape=jax.ShapeDtypeStruct(out_shape, out_dtype),
                          semantics=("parallel", "parallel", "arbitrary"), riders=riders)
    return (out, carried) if riders else out


NN = ((1,), (0,))
NT = ((1,), (1,))
TN = ((0,), (0,))
_TN_TOKENS = 4096


def _mm_cs(name, a, w, out_dtype, riders=()):
    M, K = a.shape
    nq, _, wd = w.shape
    tm = _row_tile(M)
    return _matmul(name, (a, w), grid=(nq, M // tm, 1),
                   in_specs=[pl.BlockSpec((tm, K), lambda j, i, k: (i, 0)),
                             pl.BlockSpec((None, K, wd), lambda j, i, k: (j, 0, 0))],
                   o_spec=pl.BlockSpec((tm, wd), lambda j, i, k: (i, j)),
                   out_shape=(M, nq * wd), out_dtype=out_dtype, dims=NN, riders=riders)


def _mm_rs(name, a, w, out_dtype):
    M, K = a.shape
    N = w.shape[1]
    tm = _row_tile(M)
    return _matmul(name, (a, w), grid=(M // tm, 1, 1),
                   in_specs=[pl.BlockSpec((tm, K), lambda i, j, k: (i, 0)), pl.BlockSpec((K, N), lambda i, j, k: (0, 0))],
                   o_spec=pl.BlockSpec((tm, N), lambda i, j, k: (i, 0)),
                   out_shape=(M, N), out_dtype=out_dtype, dims=NN)


def _mm_nt_rs(name, g, w, out_dtype, riders=()):
    M, N = g.shape
    K = w.shape[0]
    to = K
    tm = _row_tile(M)
    return _matmul(name, (g, w), grid=(M // tm, K // to, 1),
                   in_specs=[pl.BlockSpec((tm, N), lambda i, j, k: (i, 0)),
                             pl.BlockSpec((to, N), lambda i, j, k: (j, 0))],
                   o_spec=pl.BlockSpec((tm, to), lambda i, j, k: (i, j)),
                   out_shape=(M, K), out_dtype=out_dtype, dims=NT, riders=riders)


def _mm_nt_cs(name, g, w, out_dtype, riders=(), stacked=False, norm_bwd=None):
    M = g.shape[-2]
    nq, K, wd = w.shape
    tm = _row_tile(M, 256)

    def product(g_ref, w_ref):
        acc = None
        for q in range(nq):
            gq = g_ref[q // 2, :, (q % 2) * wd:(q % 2 + 1) * wd] if stacked else g_ref[:, q * wd:(q + 1) * wd]
            part = _dot_nt(gq, w_ref[q])
            acc = part if acc is None else acc + part
        return acc

    def body(g_ref, w_ref, o_ref):
        o_ref[...] = product(g_ref, w_ref).astype(o_ref.dtype)

    def body_norm(g_ref, w_ref, x_ref, gain_ref, dr_ref, dx_ref, dg_ref):
        @pl.when(pl.program_id(0) == 0)
        def _():
            dg_ref[...] = jnp.zeros_like(dg_ref)

        dx, dg = _rms_bwd_rows(x_ref[...], gain_ref[...], product(g_ref, w_ref))
        dg_ref[...] += dg
        dx_ref[...] = dx + dr_ref[...]

    g_spec = (pl.BlockSpec((2, tm, 2 * wd), lambda i: (0, i, 0)) if stacked
              else pl.BlockSpec((tm, nq * wd), lambda i: (i, 0)))
    w_spec = pl.BlockSpec((nq, K, wd), lambda i: (0, 0, 0))
    row = pl.BlockSpec((tm, K), lambda i: (i, 0))
    if norm_bwd is None:
        return _carried(*_pcall(
            body, (g, w), name=name, grid=(M // tm,), in_specs=[g_spec, w_spec], out_specs=row,
            out_shape=jax.ShapeDtypeStruct((M, K), out_dtype), semantics=("parallel",), riders=riders), riders)
    vec = pl.BlockSpec((1, K), lambda i: (0, 0))
    return _carried(*_pcall(
        body_norm, (g, w) + tuple(norm_bwd), name=name, grid=(M // tm,),
        in_specs=[g_spec, w_spec, row, vec, row], out_specs=(row, vec),
        out_shape=(jax.ShapeDtypeStruct((M, K), F32), jax.ShapeDtypeStruct((1, K), F32)),
        semantics=("arbitrary",), riders=riders), riders)


def _mm_tn_rs(name, a, g, to, tn=512):
    T, M = a.shape
    N = g.shape[1]
    tt = _row_tile(T, _TN_TOKENS)
    tn = min(tn, N)
    return _matmul(name, (a, g), grid=(M // to, N // tn, T // tt),
                   in_specs=[pl.BlockSpec((tt, to), lambda i, j, k: (k, i)),
                             pl.BlockSpec((tt, tn), lambda i, j, k: (k, j))],
                   o_spec=pl.BlockSpec((to, tn), lambda i, j, k: (i, j)),
                   out_shape=(M, N), out_dtype=F32, dims=TN)


def _mm_tn_cs(name, a, g, nq, to, riders=(), stacked=False):
    T, M = a.shape
    wd = g.shape[-1] * (2 if stacked else 1) // nq
    tt = _row_tile(T, _TN_TOKENS)
    g_spec = (pl.BlockSpec((None, tt, wd), lambda i, j, k: (j // 2, k, j % 2)) if stacked
              else pl.BlockSpec((tt, wd), lambda i, j, k: (k, j)))
    return _matmul(name, (a, g), grid=(M // to, nq, T // tt),
                   in_specs=[pl.BlockSpec((tt, to), lambda i, j, k: (k, i)), g_spec],
                   o_spec=pl.BlockSpec((None, to, wd), lambda i, j, k: (j, i, 0)),
                   out_shape=(nq, M, wd), out_dtype=F32, dims=TN, riders=riders)


def _mm_tn_cs_to_sibling(name, a, g, nq, place, riders=(), send=True):
    T, M = a.shape
    wd = g.shape[-1] // nq
    to = M // 2
    steps = 2 * nq

    def body(s_ref, a_ref, g_ref, o_hbm, land_hbm, acc, wsem, send_sem, recv_sem):
        t = pl.program_id(0)
        c = s_ref[1]

        def writeback(tt):
            half = (tt // nq + 1 + c) % 2
            return pltpu.make_async_copy(acc.at[tt % 2], o_hbm.at[tt % nq, pl.ds(half * to, to), :], wsem.at[tt % 2])

        @pl.when(t >= 2)
        def _():
            writeback(t - 2).wait()

        if send:
            x, y, _, _ = _place()
            to_sibling = _remote(o_hbm.at[:, pl.ds((1 - c) * to, to), :], land_hbm, send_sem.at[0], recv_sem.at[0],
                                 (x, y, 1 - c))

            @pl.when(t == nq + 1)
            def _():
                to_sibling.start()

        acc[t % 2] = _dot_tn(a_ref[...], g_ref[...])
        writeback(t).start()

        @pl.when(t == steps - 1)
        def _():
            writeback(t - 1).wait()
            writeback(t).wait()
            if send:
                to_sibling.wait()

    out, carried = _pcall(
        body, (a, g), name=name, grid=(steps,),
        in_specs=[pl.BlockSpec((T, to), lambda t, s: (0, (t // nq + 1 + s[1]) % 2)),
                  pl.BlockSpec((T, wd), lambda t, s: (0, t % nq))],
        out_specs=(HBM_SPEC, HBM_SPEC),
        out_shape=(jax.ShapeDtypeStruct((nq, M, wd), F32), jax.ShapeDtypeStruct((nq, to, wd), F32)),
        scratch_shapes=[pltpu.VMEM((2, to, wd), F32), pltpu.SemaphoreType.DMA((2,)),
                        pltpu.SemaphoreType.DMA((1,)), pltpu.SemaphoreType.DMA((1,))],
        semantics=("arbitrary",), riders=riders, prefetch=place)
    return _carried(out, carried, riders)


def _rms_fwd(name, x, g, riders=()):
    T, D = x.shape
    tm = _row_tile(T)

    def body(x_ref, g_ref, o_ref):
        o_ref[...] = _rms_rows(x_ref[...], g_ref[...]).astype(o_ref.dtype)

    return _carried(*_pcall(
        body, (x, g), name=name, grid=(T // tm,),
        in_specs=[pl.BlockSpec((tm, D), lambda i: (i, 0)), pl.BlockSpec((1, D), lambda i: (0, 0))],
        out_specs=pl.BlockSpec((tm, D), lambda i: (i, 0)),
        out_shape=jax.ShapeDtypeStruct((T, D), BF16), semantics=("parallel",), riders=riders), riders)


_NORM1_STEPS = 4


def _norm1_and_casts(x, g, shards, place, riders=()):
    T, D = x.shape
    tm = T // _NORM1_STEPS
    names = list(shards)

    def body(s_ref, x_ref, g_ref, *refs):
        w_refs, o_ref, slab_refs = refs[:len(names)], refs[len(names)], refs[len(names) + 1:]
        o_ref[...] = _rms_rows(x_ref[...], g_ref[...]).astype(o_ref.dtype)
        for w_ref, slab_ref in zip(w_refs, slab_refs):
            slab_ref[...] = w_ref[...].astype(slab_ref.dtype)

    in_specs = [pl.BlockSpec((tm, D), lambda i, s: (i, 0)), pl.BlockSpec((1, D), lambda i, s: (0, 0))]
    out_specs = [pl.BlockSpec((tm, D), lambda i, s: (i, 0))]
    out_shape = [jax.ShapeDtypeStruct((T, D), BF16)]
    for n in names:
        r, cc = shards[n].shape
        assert r % (_NORM1_STEPS * 16) == 0
        in_specs.append(pl.BlockSpec((r // _NORM1_STEPS, cc), lambda i, s: (i, 0)))
        out_specs.append(pl.BlockSpec((None, r // _NORM1_STEPS, cc), lambda i, s: (s[0], i, 0)))
        out_shape.append(jax.ShapeDtypeStruct((N_CHIPS, r, cc), BF16))
    out, carried = _pcall(body, (x, g, *[shards[n] for n in names]), name="norm1", grid=(_NORM1_STEPS,),
                          in_specs=in_specs, out_specs=out_specs, out_shape=out_shape, semantics=("parallel",),
                          riders=riders, prefetch=place)
    return _carried((out[0], dict(zip(names, out[1:]))), carried, riders)


def _rms_rows(xv, gain):
    return xv * lax.rsqrt(jnp.mean(xv * xv, axis=-1, keepdims=True) + EPS) * gain


def _rms_bwd_rows(xv, gain, dh):
    r = lax.rsqrt(jnp.mean(xv * xv, axis=-1, keepdims=True) + EPS)
    n = xv * r
    dn = dh * gain
    return r * (dn - n * jnp.mean(dn * n, axis=-1, keepdims=True)), jnp.sum(dh * n, axis=0, keepdims=True)


def _rms_bwd(name, x, g, dh, dres):
    T, D = x.shape
    tm = _row_tile(T)
    has_res = dres is not None

    def body(*refs):
        if has_res:
            x_ref, g_ref, dh_ref, dr_ref, dx_ref, dg_ref = refs
        else:
            x_ref, g_ref, dh_ref, dx_ref, dg_ref = refs

        @pl.when(pl.program_id(0) == 0)
        def _():
            dg_ref[...] = jnp.zeros_like(dg_ref)

        dx, dg = _rms_bwd_rows(x_ref[...], g_ref[...], dh_ref[...])
        dg_ref[...] += dg
        if has_res:
            dx = dx + dr_ref[...]
        dx_ref[...] = dx

    row = pl.BlockSpec((tm, D), lambda i: (i, 0))
    vec = pl.BlockSpec((1, D), lambda i: (0, 0))
    ops = (x, g, dh, dres) if has_res else (x, g, dh)
    return _pallas(
        body, name=name, grid=(T // tm,), in_specs=[row, vec, row] + ([row] if has_res else []),
        out_specs=(row, vec),
        out_shape=(jax.ShapeDtypeStruct((T, D), F32), jax.ShapeDtypeStruct((1, D), F32)),
        compiler_params=_cp("arbitrary"),
    )(*ops)


_EPILOGUE_PARTS = 4


def _proj_res_norm(name, a, w, res, gain):
    M, K = a.shape
    N = w.shape[1]
    tm = _row_tile(M)

    def body(a_ref, w_ref, r_ref, g_ref, x_ref, h_ref):
        for r in range(_EPILOGUE_PARTS):
            rows = pl.ds(r * (tm // _EPILOGUE_PARTS), tm // _EPILOGUE_PARTS)
            xv = _dot(a_ref[rows, :], w_ref[...]) + r_ref[rows, :]
            x_ref[rows, :] = xv
            h_ref[rows, :] = _rms_rows(xv, g_ref[...]).astype(h_ref.dtype)

    row = pl.BlockSpec((tm, N), lambda i: (i, 0))
    return _pallas(
        body, name=name, grid=(M // tm,),
        in_specs=[pl.BlockSpec((tm, K), lambda i: (i, 0)), pl.BlockSpec((K, N), lambda i: (0, 0)), row,
                  pl.BlockSpec((1, N), lambda i: (0, 0))],
        out_specs=(row, row), out_shape=(jax.ShapeDtypeStruct((M, N), F32), jax.ShapeDtypeStruct((M, N), BF16)),
        compiler_params=_cp("parallel"),
    )(a, w, res, gain)


def _proj_res_loss(name, a, w, res, tgt, gain):
    M, K = a.shape
    D = w.shape[1]
    tm = _row_tile(M)

    def body(a_ref, w_ref, r_ref, t_ref, g_ref, dx_ref, dg_ref, loss_ref):
        @pl.when(pl.program_id(0) == 0)
        def _():
            dg_ref[...] = jnp.zeros_like(dg_ref)
            loss_ref[...] = jnp.zeros_like(loss_ref)

        gv = g_ref[...]
        for r in range(_EPILOGUE_PARTS):
            rows = pl.ds(r * (tm // _EPILOGUE_PARTS), tm // _EPILOGUE_PARTS)
            xv = _dot(a_ref[rows, :], w_ref[...]) + r_ref[rows, :]
            diff = _rms_rows(xv, gv) - t_ref[rows, :]
            loss_ref[...] += 0.5 * jnp.sum(jnp.mean(diff * diff, axis=-1, keepdims=True))
            dx, dg = _rms_bwd_rows(xv, gv, diff * (1.0 / D))
            dg_ref[...] += dg
            dx_ref[rows, :] = dx

    row = pl.BlockSpec((tm, D), lambda i: (i, 0))
    vec = pl.BlockSpec((1, D), lambda i: (0, 0))
    return _pallas(
        body, name=name, grid=(M // tm,),
        in_specs=[pl.BlockSpec((tm, K), lambda i: (i, 0)), pl.BlockSpec((K, D), lambda i: (0, 0)), row, row, vec],
        out_specs=(row, vec, pl.BlockSpec((8, 128), lambda i: (0, 0))),
        out_shape=(jax.ShapeDtypeStruct((M, D), F32), jax.ShapeDtypeStruct((1, D), F32),
                   jax.ShapeDtypeStruct((8, 128), F32)),
        compiler_params=_cp("arbitrary"),
    )(a, w, res, tgt, gain)


def _gmlp_pieces(zu, zv, lng, lnb, ws_ref, bs_ref):
    u, du = _gelu_and_grad(zu)
    v, dv = _gelu_and_grad(zv)
    mu = jnp.mean(v, axis=-1, keepdims=True)
    vc = v - mu
    rstd = lax.rsqrt(jnp.mean(vc * vc, axis=-1, keepdims=True) + EPS)
    vhat = vc * rstd
    vn = vhat * lng + lnb
    row = lax.broadcasted_iota(jnp.int32, (GM_CHUNK, GM_CHUNK), 0)
    col = lax.broadcasted_iota(jnp.int32, (GM_CHUNK, GM_CHUNK), 1)
    tril = row >= col
    wms, mixed = [], []
    for g in range(GM_GROUPS):
        sl = slice(g * 128, (g + 1) * 128)
        wm = jnp.where(tril, ws_ref[g], 0.0)
        wms.append(wm)
        mixed.append(_dot(wm, vn[:, sl]) + bs_ref[g])
    return u, du, dv, rstd, vhat, vn, wms, mixed, tril


def _gmlp_fwd(proj, lng, lnb, ws, bs_col):
    T = proj.shape[0]
    n = T // GM_CHUNK

    def body(zu_ref, zv_ref, lng_ref, lnb_ref, ws_ref, bs_ref, o_ref):
        u, _, _, _, _, _, _, mixed, _ = _gmlp_pieces(zu_ref[...].astype(F32), zv_ref[...].astype(F32),
                                                     lng_ref[...], lnb_ref[...],
                                                     ws_ref, bs_ref)
        for g in range(GM_GROUPS):
            sl = slice(g * 128, (g + 1) * 128)
            o_ref[:, sl] = (u[:, sl] * mixed[g]).astype(o_ref.dtype)

    vec = pl.BlockSpec((1, GM_WIDTH), lambda i: (0, 0))
    return _pallas(
        body, name="gmlp_fwd", grid=(n,),
        in_specs=[pl.BlockSpec((GM_CHUNK, 512), lambda i: (i, COL_ZU)),
                  pl.BlockSpec((GM_CHUNK, 512), lambda i: (i, COL_ZV)),
                  vec, vec,
                  pl.BlockSpec((GM_GROUPS, 128, 128), lambda i: (0, 0, 0)),
                  pl.BlockSpec((GM_GROUPS, 128, 1), lambda i: (0, 0, 0))],
        out_specs=pl.BlockSpec((GM_CHUNK, 512), lambda i: (i, 0)),
        out_shape=jax.ShapeDtypeStruct((T, GM_WIDTH), BF16), compiler_params=_cp("parallel"),
    )(proj, proj, lng, lnb, ws, bs_col)


def _gmlp_bwd(proj, d_out, lng, lnb, ws, bs_col, riders=()):
    T = proj.shape[0]
    n = T // GM_CHUNK

    def body(zu_ref, zv_ref, do_ref, lng_ref, lnb_ref, ws_ref, bs_ref,
             dz_ref, dws_ref, dbs_ref, dlng_ref, dlnb_ref, dm_acc):
        i = pl.program_id(0)

        @pl.when(i == 0)
        def _():
            dws_ref[...] = jnp.zeros_like(dws_ref)
            dlng_ref[...] = jnp.zeros_like(dlng_ref)
            dlnb_ref[...] = jnp.zeros_like(dlnb_ref)
            dm_acc[...] = jnp.zeros_like(dm_acc)

        lng_v = lng_ref[...]
        u, du, dv, rstd, vhat, vn, wms, mixed, tril = _gmlp_pieces(zu_ref[...].astype(F32), zv_ref[...].astype(F32),
                                                                  lng_v, lnb_ref[...],
                                                                  ws_ref, bs_ref)
        do = do_ref[...]
        dvn_parts = []
        for g in range(GM_GROUPS):
            sl = slice(g * 128, (g + 1) * 128)
            dog = do[:, sl]
            dz_ref[:, sl] = (dog * mixed[g] * du[:, sl]).astype(dz_ref.dtype)
            dmix = dog * u[:, sl]
            dm_acc[:, sl] += dmix
            dws_ref[g] += jnp.where(tril, _dot_nt(dmix, vn[:, sl]), 0.0)
            dvn_parts.append(_dot_tn(wms[g], dmix))
        dvn = jnp.concatenate(dvn_parts, axis=1)
        dlng_ref[...] += jnp.sum(dvn * vhat, axis=0, keepdims=True)
        dlnb_ref[...] += jnp.sum(dvn, axis=0, keepdims=True)
        dvh = dvn * lng_v
        dvv = rstd * (dvh - jnp.mean(dvh, axis=-1, keepdims=True)
                      - vhat * jnp.mean(dvh * vhat, axis=-1, keepdims=True))
        dz_ref[:, GM_WIDTH:] = (dvv * dv).astype(dz_ref.dtype)

        @pl.when(i == n - 1)
        def _():
            for g in range(GM_GROUPS):
                dbs_ref[g] = jnp.sum(dm_acc[:, g * 128:(g + 1) * 128], axis=1, keepdims=True)

    vec = pl.BlockSpec((1, GM_WIDTH), lambda i: (0, 0))
    wsp = pl.BlockSpec((GM_GROUPS, 128, 128), lambda i: (0, 0, 0))
    bsp = pl.BlockSpec((GM_GROUPS, 128, 1), lambda i: (0, 0, 0))
    return _carried(*_pcall(
        body, (proj, proj, d_out, lng, lnb, ws, bs_col), name="gmlp_bwd", grid=(n,),
        in_specs=[pl.BlockSpec((GM_CHUNK, 512), lambda i: (i, COL_ZU)),
                  pl.BlockSpec((GM_CHUNK, 512), lambda i: (i, COL_ZV)),
                  pl.BlockSpec((None, GM_CHUNK, 512), lambda i: (0, i, 0)), vec, vec, wsp, bsp],
        out_specs=(pl.BlockSpec((GM_CHUNK, 2 * GM_WIDTH), lambda i: (i, 0)), wsp, bsp, vec, vec),
        out_shape=(jax.ShapeDtypeStruct((T, 2 * GM_WIDTH), BF16),
                   jax.ShapeDtypeStruct((GM_GROUPS, 128, 128), F32), jax.ShapeDtypeStruct((GM_GROUPS, 128, 1), F32),
                   jax.ShapeDtypeStruct((1, GM_WIDTH), F32), jax.ShapeDtypeStruct((1, GM_WIDTH), F32)),
        scratch_shapes=[pltpu.VMEM((GM_CHUNK, GM_WIDTH), F32)],
        semantics=("arbitrary",), riders=riders), riders)


def _hgrn_lower_bound(lbl):
    return 1.0 / (1.0 + jnp.exp(lbl[1:2, :] - lbl[0:1, :]))


def _hgrn_gates(hq, hf, lb):
    C = HG_CHUNK
    sg = _sigmoid(hf)
    fg = lb + (1.0 - lb) * sg
    sq = _sigmoid(hq)
    row = lax.broadcasted_iota(jnp.int32, (C, C), 0)
    col = lax.broadcasted_iota(jnp.int32, (C, C), 1)
    tril = row >= col
    logf = jnp.log(fg)
    a = _dot_01(tril, logf)
    a_last = jnp.sum(logf, axis=0, keepdims=True)
    first_half = lax.broadcasted_iota(jnp.int32, logf.shape, 0) < (C // 2)
    a_mid = jnp.sum(jnp.where(first_half, logf, 0.0), axis=0, keepdims=True)
    ea, ei, eki, ekl = jnp.exp(a), jnp.exp(a - a_mid), jnp.exp(a_mid - a), jnp.exp(a_last - a)
    k = 1.0 - fg
    q = hq * sq
    qi = (q * ei).astype(BF16).astype(F32)
    ki = (k * eki).astype(BF16).astype(F32)
    return dict(sg=sg, fg=fg, sq=sq, tril=tril, ea=ea, ei=ei, eki=eki, ekl=ekl, e_last=jnp.exp(a_last),
                qe=q * ea, qi=qi, ki=ki, kl=k * ekl)


def _heads(x):
    return [x[:, h * HG_DIM:(h + 1) * HG_DIM] for h in range(HG_HEADS)]


def _hgrn_fwd(proj, lbl, gh, B, S, riders=()):
    C = HG_CHUNK
    NC = S // C
    W = HG_HEADS * HG_DIM

    def body(q_ref, f_ref, i_ref, g_ref, lbl_ref, gh_ref, o_ref, bo_ref, st_ref, state):
        @pl.when(pl.program_id(0) == 0)
        def _():
            state[...] = jnp.zeros_like(state)

        lb = _hgrn_lower_bound(lbl_ref[...])
        ghv = gh_ref[...]
        for b in range(B):
            gt = _hgrn_gates(q_ref[b].astype(F32), f_ref[b].astype(F32), lb)
            v = _heads(i_ref[b])
            qe, qi, ki, kl, e_last = (_heads(gt[n]) for n in ("qe", "qi", "ki", "kl", "e_last"))
            outs, normed = [], []
            for h in range(HG_HEADS):
                p = jnp.where(gt["tril"], _dot_nt(qi[h], ki[h]), 0.0)
                st = state[b, h]
                st_ref[b, h] = st
                o = _dot_nt(qe[h], st) + _dot(p, v[h])
                state[b, h] = st * e_last[h] + _dot_tn(v[h], kl[h])
                outs.append(o)
                normed.append(o * lax.rsqrt(jnp.mean(o * o, axis=-1, keepdims=True) + EPS) * ghv)
            o_ref[b] = jnp.concatenate(outs, axis=1)
            hg = g_ref[b].astype(F32)
            bo_ref[b] = (jnp.concatenate(normed, axis=1) * (hg * _sigmoid(hg))).astype(bo_ref.dtype)

    def col(cb):
        return pl.BlockSpec((B, C, 512), lambda c: (0, c, cb))

    tile = pl.BlockSpec((B, C, W), lambda c: (0, c, 0))
    proj3 = proj.reshape(B, S, proj.shape[-1])
    out, carried = _pcall(
        body, (proj3, proj3, proj3, proj3, lbl, gh), name="hgrn_fwd", grid=(NC,),
        in_specs=[col(COL_HQ), col(COL_HF), col(COL_HI), col(COL_HG),
                  pl.BlockSpec((2, W), lambda c: (0, 0)), pl.BlockSpec((1, HG_DIM), lambda c: (0, 0))],
        out_specs=(tile, tile, pl.BlockSpec((B, None, HG_HEADS, 128, 128), lambda c: (0, c, 0, 0, 0))),
        out_shape=(jax.ShapeDtypeStruct((B, S, W), F32), jax.ShapeDtypeStruct((B, S, W), BF16),
                   jax.ShapeDtypeStruct((B, NC, HG_HEADS, 128, 128), F32)),
        scratch_shapes=[pltpu.VMEM((B, HG_HEADS, 128, 128), F32)],
        semantics=("arbitrary",), riders=riders)
    o_h, b_out, states = out
    out = (o_h, b_out.reshape(B * S, W), states)
    return (out, carried) if riders else out


def _hgrn_bwd(proj, o_saved, states, d_out, lbl, gh, others, B, S, riders=()):
    C = HG_CHUNK
    NC = S // C
    W = HG_HEADS * HG_DIM
    d_gm, d_xq, d_gates = (t.reshape(B, S, t.shape[-1]) for t in others)
    own0 = d_gm.shape[-1]
    xq0 = own0 + 4 * W
    gates0 = xq0 + d_xq.shape[-1]

    def body(q_ref, f_ref, i_ref, g_ref, o_ref, st_ref, do_ref, lbl_ref, gh_ref, gm_ref, xq_ref, gates_ref,
             d_ref, dlbl_ref, dgh_ref, dstate, dlb_acc):
        c = pl.program_id(0)
        d_ref[:, :, :own0] = gm_ref[...]
        d_ref[:, :, xq0:gates0] = xq_ref[...]
        d_ref[:, :, gates0:] = gates_ref[...]

        def put(b, k, val):
            d_ref[b, :, own0 + k * W:own0 + (k + 1) * W] = val.astype(d_ref.dtype)

        @pl.when(c == 0)
        def _():
            dstate[...] = jnp.zeros_like(dstate)
            dgh_ref[...] = jnp.zeros_like(dgh_ref)
            dlb_acc[...] = jnp.zeros_like(dlb_acc)

        lb = _hgrn_lower_bound(lbl_ref[...])
        ghv = gh_ref[...]
        row = lax.broadcasted_iota(jnp.int32, (C, C), 0)
        colm = lax.broadcasted_iota(jnp.int32, (C, C), 1)
        triu = colm >= row
        for b in range(B):
            hq, hg = q_ref[b].astype(F32), g_ref[b].astype(F32)
            gt = _hgrn_gates(hq, f_ref[b].astype(F32), lb)
            tril = gt["tril"]
            v = _heads(i_ref[b])
            qe, qi, ki, kl, e_last = (_heads(gt[n]) for n in ("qe", "qi", "ki", "kl", "e_last"))
            sgg = _sigmoid(hg)
            don_all = do_ref[b] * (hg * sgg)
            o, don = _heads(o_ref[b]), _heads(don_all)
            d_qe, d_qi, d_ki, d_kl, dv, n_all, dal = [], [], [], [], [], [], []
            for h in range(HG_HEADS):
                r = lax.rsqrt(jnp.mean(o[h] * o[h], axis=-1, keepdims=True) + EPS)
                n = o[h] * r
                n_all.append(n)
                dgh_ref[...] += jnp.sum(don[h] * n, axis=0, keepdims=True)
                dn = don[h] * ghv
                d_o = r * (dn - n * jnp.mean(dn * n, axis=-1, keepdims=True))
                st, dst = st_ref[b, h], dstate[b, h]
                p = jnp.where(tril, _dot_nt(qi[h], ki[h]), 0.0)
                dp = jnp.where(tril, _dot_nt(d_o, v[h]), 0.0)
                d_qe.append(_dot(d_o, st))
                d_qi.append(_dot(dp, ki[h]))
                d_ki.append(_dot_tn(dp, qi[h]))
                d_kl.append(_dot(v[h], dst))
                dv.append(_dot_tn(p, d_o) + _dot_nt(kl[h], dst))
                dstate[b, h] = dst * e_last[h] + _dot_tn(d_o, qe[h])
                dal.append(jnp.sum(dst * st, axis=0, keepdims=True) * e_last[h])
            d_qe, d_qi, d_ki, d_kl, n_all, dal = (jnp.concatenate(t, axis=1)
                                                  for t in (d_qe, d_qi, d_ki, d_kl, n_all, dal))
            put(b, 3, do_ref[b] * n_all * jnp.tile(ghv, (1, HG_HEADS)) * (sgg * (1.0 + hg * (1.0 - sgg))))
            put(b, 2, jnp.concatenate(dv, axis=1))
            d_a_last = dal + jnp.sum(d_kl * gt["kl"], axis=0, keepdims=True)
            dq = d_qe * gt["ea"] + d_qi * gt["ei"]
            dk = d_ki * gt["eki"] + d_kl * gt["ekl"]
            da = d_qe * gt["qe"] + d_qi * gt["qi"] - d_ki * gt["ki"] - d_kl * gt["kl"]
            dlogf = _dot_01(triu, da) + d_a_last
            sg, sq = gt["sg"], gt["sq"]
            dfg = dlogf / gt["fg"] - dk
            put(b, 1, dfg * (1.0 - lb) * sg * (1.0 - sg))
            dlb_acc[...] += jnp.sum(dfg * (1.0 - sg), axis=0, keepdims=True)
            put(b, 0, dq * (sq * (1.0 + hq * (1.0 - sq))))

        @pl.when(c == NC - 1)
        def _():
            dlb = dlb_acc[...]
            first = lax.broadcasted_iota(jnp.int32, (2, W), 0) == 0
            dlbl_ref[...] = jnp.where(first, dlb * lb * (1.0 - lb), -dlb * lb * (1.0 - lb))

    def col(cb):
        return pl.BlockSpec((B, C, 512), lambda c: (0, NC - 1 - c, cb))

    tile = pl.BlockSpec((B, C, W), lambda c: (0, NC - 1 - c, 0))
    proj3 = proj.reshape(B, S, proj.shape[-1])

    def rows(width):
        return pl.BlockSpec((B, C, width), lambda c: (0, NC - 1 - c, 0))

    width = proj.shape[-1]
    out, carried = _pcall(
        body, (proj3, proj3, proj3, proj3, o_saved, states, d_out.reshape(3, B, S, W), lbl, gh, d_gm, d_xq, d_gates),
        name="hgrn_bwd", grid=(NC,),
        in_specs=[col(COL_HQ), col(COL_HF), col(COL_HI), col(COL_HG), tile,
                  pl.BlockSpec((B, None, HG_HEADS, 128, 128), lambda c: (0, NC - 1 - c, 0, 0, 0)),
                  pl.BlockSpec((None, B, C, W), lambda c: (1, 0, NC - 1 - c, 0)),
                  pl.BlockSpec((2, W), lambda c: (0, 0)), pl.BlockSpec((1, HG_DIM), lambda c: (0, 0)),
                  rows(d_gm.shape[-1]), rows(d_xq.shape[-1]), rows(d_gates.shape[-1])],
        out_specs=(rows(width), pl.BlockSpec((2, W), lambda c: (0, 0)), pl.BlockSpec((1, HG_DIM), lambda c: (0, 0))),
        out_shape=(jax.ShapeDtypeStruct((B, S, width), BF16), jax.ShapeDtypeStruct((2, W), F32),
                   jax.ShapeDtypeStruct((1, HG_DIM), F32)),
        scratch_shapes=[pltpu.VMEM((B, HG_HEADS, 128, 128), F32), pltpu.VMEM((1, W), F32)],
        semantics=("arbitrary",), riders=riders)
    out = (out[0].reshape(B * S, width),) + tuple(out[1:])
    return (out, carried) if riders else out


_XA_SCALE = XA_DIM ** -0.5


def _attn_probs(qh, kh):
    s = _dot_nt(qh, kh) * _XA_SCALE
    e = jnp.exp(s - jnp.max(s, axis=-1, keepdims=True))
    return e / jnp.sum(e, axis=-1, keepdims=True)


def _attn_fwd(proj, kv, B, S):
    T = B * S
    tq = _row_tile(S)
    nq = S // tq
    W = XA_HEADS * XA_DIM

    def body(q_ref, kv_ref, o_ref):
        for h in range(XA_HEADS):
            sl = slice(h * 128, (h + 1) * 128)
            p = _attn_probs(q_ref[:, sl], kv_ref[:, sl])
            o_ref[:, sl] = _dot(p, kv_ref[:, W + h * 128:W + (h + 1) * 128]).astype(o_ref.dtype)

    return _pallas(
        body, name="attn_fwd", grid=(B, nq),
        in_specs=[pl.BlockSpec((tq, 512), lambda b, i: (b * nq + i, COL_XQ)),
                  pl.BlockSpec((MEM_LEN, 2 * W), lambda b, i: (b, 0))],
        out_specs=pl.BlockSpec((tq, W), lambda b, i: (b * nq + i, 0)),
        out_shape=jax.ShapeDtypeStruct((T, W), BF16), compiler_params=_cp("parallel", "parallel"),
    )(proj, kv)


def _attn_bwd(proj, kv, d_out, B, S):
    T = B * S
    tq = _row_tile(S)
    nq = S // tq
    W = XA_HEADS * XA_DIM

    def body(q_ref, kv_ref, do_ref, dq_ref, dkv_ref):
        @pl.when(pl.program_id(1) == 0)
        def _():
            dkv_ref[...] = jnp.zeros_like(dkv_ref)

        for h in range(XA_HEADS):
            sl = slice(h * 128, (h + 1) * 128)
            slv = slice(W + h * 128, W + (h + 1) * 128)
            qh = q_ref[:, sl]
            kh = kv_ref[:, sl]
            p = _attn_probs(qh, kh)
            dc = do_ref[:, sl]
            dp = _dot_nt(dc, kv_ref[:, slv])
            ds = p * (dp - jnp.sum(dp * p, axis=-1, keepdims=True)) * _XA_SCALE
            dq_ref[:, sl] = _dot(ds, kh).astype(dq_ref.dtype)
            dkv_ref[:, sl] += _dot_tn(ds, qh)
            dkv_ref[:, slv] += _dot_tn(p, dc)

    kvspec = pl.BlockSpec((MEM_LEN, 2 * W), lambda b, i: (b, 0))
    tile = pl.BlockSpec((tq, W), lambda b, i: (b * nq + i, 0))
    return _pallas(
        body, name="attn_bwd", grid=(B, nq),
        in_specs=[pl.BlockSpec((tq, 512), lambda b, i: (b * nq + i, COL_XQ)), kvspec,
                  pl.BlockSpec((None, tq, W), lambda b, i: (2, b * nq + i, 0))],
        out_specs=(tile, kvspec),
        out_shape=(jax.ShapeDtypeStruct((T, W), BF16), jax.ShapeDtypeStruct((B * MEM_LEN, 2 * W), F32)),
        compiler_params=_cp("parallel", "arbitrary"),
    )(proj, kv, d_out)


_MERGE_TM = 256
_GATE_W = 512


def _gate_specs(tm):
    base = COL_GATE0 // _GATE_W
    return [pl.BlockSpec((tm, _GATE_W), functools.partial(lambda i, k: (i, base + k), k=k)) for k in range(6)]


def _merge_fwd(a_out, b_out, c_out, wb, proj, riders=()):
    T = a_out.shape[0]
    tm = _row_tile(T, _MERGE_TM)
    nq, _, wd = wb.shape
    per_half = _GATE_W // wd

    def body(a_ref, b_ref, c_ref, w_ref, *rest):
        gates, (m_ref, up_ref) = rest[:6], rest[6:]
        for hf in range(2):
            cols = slice(hf * _GATE_W, (hf + 1) * _GATE_W)
            acc = None
            for n, br in enumerate((a_ref, b_ref, c_ref)):
                x = br[...]
                up = jnp.concatenate([_dot(x, w_ref[per_half * hf + j, n * BR_WIDTH:(n + 1) * BR_WIDTH, :])
                                      for j in range(per_half)], axis=1)
                up_ref[n, :, cols] = up.astype(up_ref.dtype)
                term = _sigmoid(gates[2 * n + hf][...].astype(F32)) * up
                acc = term if acc is None else acc + term
            m_ref[:, cols] = acc.astype(m_ref.dtype)

    br_spec = pl.BlockSpec((tm, BR_WIDTH), lambda i: (i, 0))
    return _carried(*_pcall(
        body, (a_out, b_out, c_out, wb, *([proj] * 6)), name="merge_fwd", grid=(T // tm,),
        in_specs=[br_spec, br_spec, br_spec,
                  pl.BlockSpec((nq, 3 * BR_WIDTH, wd), lambda i: (0, 0, 0))] + _gate_specs(tm),
        out_specs=(pl.BlockSpec((tm, D_MODEL), lambda i: (i, 0)), pl.BlockSpec((3, tm, D_MODEL), lambda i: (0, i, 0))),
        out_shape=(jax.ShapeDtypeStruct((T, D_MODEL), BF16), jax.ShapeDtypeStruct((3, T, D_MODEL), BF16)),
        semantics=("parallel",), riders=riders), riders)


def _branch_bwd_act(d_ups, wb, riders=()):
    _, T, D = d_ups.shape
    nq, _, wd = wb.shape
    tm = _row_tile(T)

    def body(d_ref, w_ref, o_ref):
        acc = None
        for q in range(nq):
            part = _dot_nt(d_ref[:, q * wd:(q + 1) * wd], w_ref[q])
            acc = part if acc is None else acc + part
        o_ref[...] = acc

    return _carried(*_pcall(
        body, (d_ups, wb), name="d_branch", grid=(3, T // tm),
        in_specs=[pl.BlockSpec((None, tm, D), lambda n, i: (n, i, 0)),
                  pl.BlockSpec((nq, BR_WIDTH, wd), lambda n, i: (0, n, 0))],
        out_specs=pl.BlockSpec((None, tm, BR_WIDTH), lambda n, i: (n, i, 0)),
        out_shape=jax.ShapeDtypeStruct((3, T, BR_WIDTH), F32), semantics=("parallel", "parallel"),
        riders=riders), riders)


def _branch_bwd_weight(name, br, d_ups, n, into=None):
    T = br.shape[0]
    D = d_ups.shape[2]
    wd = D // N_CHIPS
    tt = _row_tile(T, _TN_TOKENS)
    n_br = d_ups.shape[0]

    def body(b_ref, d_ref, *rest):
        o_ref = rest[-1]
        k = pl.program_id(0)
        for q in range(N_CHIPS):
            part = _dot_tn(b_ref[...], d_ref[:, q * wd:(q + 1) * wd])

            @pl.when(k == 0)
            def _():
                o_ref[q] = part

            @pl.when(k > 0)
            def _():
                o_ref[q] += part

    return _pallas(
        body, name=name, grid=(T // tt,),
        in_specs=[pl.BlockSpec((tt, BR_WIDTH), lambda k: (k, 0)),
                  pl.BlockSpec((None, tt, D), lambda k: (n, k, 0))] + ([] if into is None else [HBM_SPEC]),
        out_specs=pl.BlockSpec((N_CHIPS, BR_WIDTH, wd), lambda k: (0, n, 0)),
        out_shape=jax.ShapeDtypeStruct((N_CHIPS, n_br * BR_WIDTH, wd), F32),
        input_output_aliases={} if into is None else {2: 0}, compiler_params=_cp("arbitrary"),
    )(br, d_ups, *(() if into is None else (into,)))


def _merge_bwd(d_merged, ups, proj, riders=()):
    T = d_merged.shape[0]
    tm = _row_tile(T, _MERGE_TM)

    def body(dm_ref, up_ref, *rest):
        gates, (dup_ref, dg_ref) = rest[:6], rest[6:]
        for hf in range(2):
            cols = slice(hf * _GATE_W, (hf + 1) * _GATE_W)
            dm = dm_ref[:, cols]
            for n in range(3):
                gate = _sigmoid(gates[2 * n + hf][...].astype(F32))
                dup_ref[n, :, cols] = (dm * gate).astype(dup_ref.dtype)
                dg_ref[:, n * D_MODEL + hf * _GATE_W:n * D_MODEL + (hf + 1) * _GATE_W] = (
                    dm * up_ref[n, :, cols].astype(F32) * gate * (1.0 - gate)).astype(dg_ref.dtype)

    tile = pl.BlockSpec((tm, D_MODEL), lambda i: (i, 0))
    tile3 = pl.BlockSpec((3, tm, D_MODEL), lambda i: (0, i, 0))
    return _carried(*_pcall(
        body, (d_merged, ups, *([proj] * 6)), name="merge_bwd", grid=(T // tm,),
        in_specs=[tile, tile3] + _gate_specs(tm),
        out_specs=(tile3, pl.BlockSpec((tm, 3 * D_MODEL), lambda i: (i, 0))),
        out_shape=(jax.ShapeDtypeStruct((3, T, D_MODEL), BF16), jax.ShapeDtypeStruct((T, 3 * D_MODEL), BF16)),
        semantics=("parallel",), riders=riders), riders)


_CONV_TF = D_FF // 2
_CONV_TS = 256
_HALO = 16


def _conv_fwd(ab, cw, cb, B, S):
    T = B * S
    ts = _row_tile(S, _CONV_TS)
    tf = _CONV_TF
    nb = D_FF // tf
    tps = S // ts
    hb = ts // _HALO

    def body(a_ref, p_ref, b_ref, w_ref, cb_ref, o_ref):
        start = (pl.program_id(0) % tps) == 0
        a = a_ref[...].astype(F32)
        prev = jnp.where(start, 0.0, p_ref[...].astype(F32))
        ext = jnp.concatenate([prev, a], axis=0)
        a1 = pltpu.roll(ext, 1, 0)[_HALO:, :]
        a2 = pltpu.roll(ext, 2, 0)[_HALO:, :]
        ac = cb_ref[...] + w_ref[0] * a2 + w_ref[1] * a1 + w_ref[2] * a
        o_ref[...] = (ac * _sigmoid(ac) * b_ref[...].astype(F32)).astype(o_ref.dtype)

    return _pallas(
        body, name="conv_fwd", grid=(T // ts, nb),
        in_specs=[pl.BlockSpec((ts, tf), lambda i, j: (i, j)),
                  pl.BlockSpec((_HALO, tf), lambda i, j: (jnp.maximum(i * hb - 1, 0), j)),
                  pl.BlockSpec((ts, tf), lambda i, j: (i, j + nb)),
                  pl.BlockSpec((3, 1, tf), lambda i, j: (0, 0, j)),
                  pl.BlockSpec((1, tf), lambda i, j: (0, j))],
        out_specs=pl.BlockSpec((ts, tf), lambda i, j: (i, j)),
        out_shape=jax.ShapeDtypeStruct((T, D_FF), BF16), compiler_params=_cp("parallel", "parallel"),
    )(ab, ab, ab, cw, cb)


def _conv_bwd(ab, d_ff, cw, cb, B, S, riders=()):
    T = B * S
    ts = _row_tile(S, _CONV_TS)
    tf = _CONV_TF
    nb = D_FF // tf
    tps = S // ts
    hb = ts // _HALO
    last_h = T // _HALO - 1
    n_ext = ts + _HALO

    def body(a_ref, ap_ref, an_ref, b_ref, bn_ref, d_ref, dn_ref, w_ref, cb_ref, dab_ref, dw_ref, dcb_ref):
        i = pl.program_id(1)

        @pl.when(i == 0)
        def _():
            dw_ref[...] = jnp.zeros_like(dw_ref)
            dcb_ref[...] = jnp.zeros_like(dcb_ref)

        start = (i % tps) == 0
        end = (i % tps) == tps - 1
        a = a_ref[...].astype(F32)
        ext = jnp.concatenate([jnp.where(start, 0.0, ap_ref[...].astype(F32)), a, an_ref[...].astype(F32)], axis=0)
        r1 = pltpu.roll(ext, 1, 0)[_HALO:, :]
        r2 = pltpu.roll(ext, 2, 0)[_HALO:, :]
        ac = cb_ref[...] + w_ref[0] * r2 + w_ref[1] * r1 + w_ref[2] * ext[_HALO:, :]
        sg = _sigmoid(ac)
        d_e = jnp.concatenate([d_ref[...].astype(F32), jnp.where(end, 0.0, dn_ref[...].astype(F32))], axis=0)
        b_e = jnp.concatenate([b_ref[...].astype(F32), bn_ref[...].astype(F32)], axis=0)
        dab_ref[1] = (d_e[:ts, :] * (ac * sg)[:ts, :]).astype(dab_ref.dtype)
        dac = d_e * b_e * sg * (1.0 + ac * (1.0 - sg))
        u1 = pltpu.roll(dac, n_ext - 1, 0)[:ts, :]
        u2 = pltpu.roll(dac, n_ext - 2, 0)[:ts, :]
        dac0 = dac[:ts, :]
        dab_ref[0] = (w_ref[2] * dac0 + w_ref[1] * u1 + w_ref[0] * u2).astype(dab_ref.dtype)
        dcb_ref[...] += jnp.sum(dac0, axis=0, keepdims=True)
        dw_ref[2] += jnp.sum(dac0 * a, axis=0, keepdims=True)
        dw_ref[1] += jnp.sum(dac0 * r1[:ts, :], axis=0, keepdims=True)
        dw_ref[0] += jnp.sum(dac0 * r2[:ts, :], axis=0, keepdims=True)

    def cur(off):
        return pl.BlockSpec((ts, tf), lambda j, i: (i, j + off))

    def nxt(off):
        return pl.BlockSpec((_HALO, tf), lambda j, i: (jnp.minimum((i + 1) * hb, last_h), j + off))

    return _carried(*_pcall(
        body, (ab, ab, ab, ab, ab, d_ff, d_ff, cw, cb), name="conv_bwd", grid=(nb, T // ts),
        in_specs=[cur(0), pl.BlockSpec((_HALO, tf), lambda j, i: (jnp.maximum(i * hb - 1, 0), j)), nxt(0),
                  cur(nb), nxt(nb), cur(0), nxt(0),
                  pl.BlockSpec((3, 1, tf), lambda j, i: (0, 0, j)), pl.BlockSpec((1, tf), lambda j, i: (0, j))],
        out_specs=(pl.BlockSpec((2, ts, tf), lambda j, i: (0, i, j)), pl.BlockSpec((3, 1, tf), lambda j, i: (0, 0, j)),
                   pl.BlockSpec((1, tf), lambda j, i: (0, j))),
        out_shape=(jax.ShapeDtypeStruct((2, T, D_FF), BF16),
                   jax.ShapeDtypeStruct((3, 1, D_FF), F32), jax.ShapeDtypeStruct((1, D_FF), F32)),
        semantics=("parallel", "arbitrary"), riders=riders), riders)


def _local_step(x, mem, tgt, p, comm, B, S):
    g = {}
    h, slabs = comm.carry(
        "norm1", lambda r: _norm1_and_casts(x, p["norm1_g"], p["cast_beside_norm1"], comm.place, riders=r))
    comm.slabs.update(slabs)
    proj = comm.carry("in_proj", lambda r: _mm_cs("in_proj", h, comm.w("w_in"), BF16, riders=r))
    a_out = _gmlp_fwd(proj, p["ln_v_g"], p["ln_v_b"], p["w_spatial"], p["b_spatial"])
    o_h, b_out, states = comm.carry(
        "hgrn_fwd", lambda r: _hgrn_fwd(proj, p["lb_logits"], p["hgrn_norm_g"], B, S, riders=r))
    memn = _rms_fwd("mem_norm", mem, p["mem_norm_g"])
    kv = _mm_rs("mem_kv", memn, comm.w("w_mem_kv"), F32)
    c_out = _attn_fwd(proj, kv, B, S)
    merged, ups = comm.carry(
        "merge_fwd", lambda r: _merge_fwd(a_out, b_out, c_out, comm.w("w_branch"), proj, riders=r))
    x1, h2 = _proj_res_norm("out_proj_norm2", merged, comm.w("w_out"), x, p["norm2_g"])
    ab = comm.carry("up_proj", lambda r: _mm_cs("up_proj", h2, comm.w("w_up"), BF16, riders=r))
    conv_w = comm.w("conv_w")
    ff = _conv_fwd(ab, conv_w, p["conv_b"], B, S)
    dx2, g["final_g"], loss = _proj_res_loss("down_proj_loss", ff, comm.w("w_down"), x1, tgt, p["final_g"])

    comm.grad("w_down", _mm_tn_rs("g_w_down", ff, dx2, to=D_FF // 2))
    d_ff = comm.carry("d_ff", lambda r: _mm_nt_rs("d_ff", dx2, comm.w("w_down"), BF16, riders=r))
    d_ab, g["conv_w"], g["conv_b"] = comm.carry(
        "conv_bwd", lambda r: _conv_bwd(ab, d_ff, conv_w, p["conv_b"], B, S, riders=r))
    comm.grad("w_up", _mm_tn_cs("g_w_up", h2, d_ab, N_CHIPS, to=512, stacked=True))
    d_x1, g["norm2_g"] = comm.carry("d_h2", lambda r: _mm_nt_cs(
        "d_h2_norm2_bwd", d_ab, comm.w("w_up"), F32, riders=r, stacked=True, norm_bwd=(x1, p["norm2_g"], dx2)))
    comm.grad("w_out", _mm_tn_rs("g_w_out", merged, d_x1, to=512))
    d_merged = _mm_nt_rs("d_merged", d_x1, comm.w("w_out"), F32)
    d_ups, d_gates = comm.carry("merge_bwd", lambda r: _merge_bwd(d_merged, ups, proj, riders=r))

    d_br = comm.carry("d_branch", lambda r: _branch_bwd_act(d_ups, comm.w("w_branch"), riders=r))
    g_branch = None
    for n, br in enumerate((a_out, b_out, c_out)):
        g_branch = _branch_bwd_weight("g_w_branch%d" % n, br, d_ups, n, into=g_branch)
    comm.grad("w_branch", g_branch)

    d_gm, g["w_spatial"], g["b_spatial"], g["ln_v_g"], g["ln_v_b"] = comm.carry(
        "gmlp_bwd", lambda r: _gmlp_bwd(proj, d_br, p["ln_v_g"], p["ln_v_b"], p["w_spatial"], p["b_spatial"],
                                        riders=r))
    d_xq, d_kv = _attn_bwd(proj, kv, d_br, B, S)
    comm.grad("w_mem_kv", _mm_tn_rs("g_w_mem_kv", memn, d_kv, to=512))
    d_memn = _mm_nt_rs("d_memn", d_kv, comm.w("w_mem_kv"), F32)
    _, g["mem_norm_g"] = _rms_bwd("mem_norm_bwd", mem, p["mem_norm_g"], d_memn, None)
    d_proj, g["lb_logits"], g["hgrn_norm_g"] = comm.carry(
        "hgrn_bwd", lambda r: _hgrn_bwd(proj, o_h, states, d_br, p["lb_logits"], p["hgrn_norm_g"],
                                        (d_gm, d_xq, d_gates), B, S, riders=r))
    comm.small_grads([g[n].reshape(_SMALL_SHAPE[n]) for n in _SMALL_EARLY] + [loss])
    comm.grad("w_in", *comm.carry("g_w_in", lambda r: _mm_tn_cs_to_sibling(
        "g_w_in", h, d_proj, N_CHIPS, comm.place, riders=r, send=comm.sends)))
    grad_x, g["norm1_g"] = comm.carry("d_h", lambda r: _mm_nt_cs(
        "d_h_norm1_bwd", d_proj, comm.w("w_in"), F32, riders=r, norm_bwd=(x, p["norm1_g"], d_x1)))
    return loss, grad_x, g


HBM_SPEC = pl.BlockSpec(memory_space=pltpu.HBM)


def _place():
    x, y, c = lax.axis_index("x"), lax.axis_index("y"), lax.axis_index("c")
    other_chips = [(1 - x, y), (x, 1 - y), (1 - x, 1 - y)]
    return x, y, c, other_chips


def _remote(src, dst, send_sem, recv_sem, dev):
    return pltpu.make_async_remote_copy(src_ref=src, dst_ref=dst, send_sem=send_sem, recv_sem=recv_sem,
                                        device_id=dev, device_id_type=MESH_ID)


class _Exchange:
    def __init__(self, operands, out_shape, aliases, scratch, start, finish, mid=None, mid_at=0.5):
        self.operands, self.out_shape, self.aliases, self.scratch = operands, out_shape, aliases, scratch
        self.start, self.finish, self.mid, self.mid_at = start, finish, mid, mid_at


def _run_exchanges(name, exs):
    n_in = [len(ex.operands) for ex in exs]
    n_out = [len(ex.out_shape) for ex in exs]
    n_scr = [len(ex.scratch) for ex in exs]

    def body(*refs):
        ins, outs, scr = refs[:sum(n_in)], refs[sum(n_in):sum(n_in) + sum(n_out)], refs[sum(n_in) + sum(n_out):]
        parts, oi, oo, os_ = [], 0, 0, 0
        for k in range(len(exs)):
            parts.append((ins[oi:oi + n_in[k]], outs[oo:oo + n_out[k]], scr[os_:os_ + n_scr[k]]))
            oi, oo, os_ = oi + n_in[k], oo + n_out[k], os_ + n_scr[k]
        for ex, part in zip(exs, parts):
            ex.start(*part)
        for ex, part in zip(exs, parts):
            if ex.mid is not None:
                ex.mid(*part)
        for ex, part in zip(exs, parts):
            ex.finish(*part)

    aliases, ops, shapes, scratch, oi, oo = {}, [], [], [], 0, 0
    for k, ex in enumerate(exs):
        aliases.update({oi + a: oo + b for a, b in ex.aliases.items()})
        oi, oo = oi + n_in[k], oo + n_out[k]
        ops += list(ex.operands)
        shapes += [pltpu.HBM(s.shape, s.dtype) for s in ex.out_shape]
        scratch += list(ex.scratch)
    res = _pallas(
        body, name=name, in_specs=[HBM_SPEC] * len(ops), out_specs=(HBM_SPEC,) * len(shapes), out_shape=tuple(shapes),
        input_output_aliases=aliases, scratch_shapes=scratch,
    )(*ops)
    out, oo = [], 0
    for k in range(len(exs)):
        out.append(list(res[oo:oo + n_out[k]]))
        oo += n_out[k]
    return out


def _ex_all_gather(slabs, halved, part=(0, 1)):
    n = len(slabs)

    def rows(a, cc):
        if not halved[a]:
            return slice(None)
        pr = slabs[a].shape[1] // part[1]
        return pl.ds(part[0] * pr + cc * (pr // 2), pr // 2)

    def ici(bufs, scr, a, j, chip, c, mine):
        px, py = chip
        x, y, _, _ = _place()
        qs = 2 * x + y if mine else 2 * px + py
        piece = bufs[a].at[qs, rows(a, c)]
        return _remote(piece, piece, scr[0].at[3 * a + j], scr[1].at[3 * a + j], (px, py, c))

    def d2d(bufs, scr, a, j, chip, cc):
        px, py = chip
        x, y, c, _ = _place()
        piece = bufs[a].at[2 * px + py, rows(a, cc)]
        return _remote(piece, piece, scr[2].at[3 * a + j], scr[3].at[3 * a + j], (x, y, 1 - c))

    def start(ins, outs, scr):
        _, _, c, chips = _place()
        for j, chip in enumerate(chips):
            for a in range(n):
                ici(outs, scr, a, j, chip, c, True).start()

    def finish(ins, outs, scr):
        _, _, c, chips = _place()
        for j, chip in enumerate(chips):
            for a in range(n):
                ici(outs, scr, a, j, chip, c, False).wait_recv()
                if halved[a]:
                    d2d(outs, scr, a, j, chip, c).start()
        for j, chip in enumerate(chips):
            for a in range(n):
                if halved[a]:
                    d2d(outs, scr, a, j, chip, 1 - c).wait_recv()
        for j, chip in enumerate(chips):
            for a in range(n):
                ici(outs, scr, a, j, chip, c, True).wait_send()
                if halved[a]:
                    d2d(outs, scr, a, j, chip, c).wait_send()

    return _Exchange(list(slabs), [jax.ShapeDtypeStruct(s.shape, s.dtype) for s in slabs],
                     {a: a for a in range(n)}, [pltpu.SemaphoreType.DMA((3 * n,))] * 4, start, finish)


def _ex_gather_relay(slabs, mid_at=0.5):
    n = len(slabs)

    def rows(a, cc):
        hr = slabs[a].shape[1] // 2
        return pl.ds(cc * hr, hr)

    def peers():
        x, y, c, _ = _place()
        nbr0 = ((x + c) % 2, (y + 1 - c) % 2)
        nbr1 = ((x + 1 - c) % 2, (y + c) % 2)
        return x, y, c, nbr0, nbr1, (1 - x, 1 - y)

    def ici(bufs, scr, a, k, chip, dev, cc):
        _, _, c, _, _, _ = peers()
        piece = bufs[a].at[2 * chip[0] + chip[1], rows(a, cc)]
        return _remote(piece, piece, scr[0].at[3 * a + k], scr[1].at[3 * a + k], (dev[0], dev[1], c))

    def d2d(bufs, scr, a, k, chip, cc):
        x, y, c, _, _, _ = peers()
        piece = bufs[a].at[2 * chip[0] + chip[1], rows(a, cc)]
        return _remote(piece, piece, scr[2].at[3 * a + k], scr[3].at[3 * a + k], (x, y, 1 - c))

    def start(ins, outs, scr):
        x, y, c, nbr0, nbr1, _ = peers()
        for a in range(n):
            ici(outs, scr, a, 0, (x, y), nbr0, c).start()
            ici(outs, scr, a, 1, (x, y), nbr1, c).start()

    def mid(ins, outs, scr):
        x, y, c, nbr0, nbr1, diag = peers()
        for a in range(n):
            ici(outs, scr, a, 0, nbr0, nbr0, c).wait_recv()
            ici(outs, scr, a, 2, nbr0, nbr1, c).start()
            d2d(outs, scr, a, 0, nbr0, c).start()
        for a in range(n):
            ici(outs, scr, a, 1, nbr1, nbr1, c).wait_recv()
            d2d(outs, scr, a, 1, nbr1, c).start()

    def finish(ins, outs, scr):
        x, y, c, nbr0, nbr1, diag = peers()
        for a in range(n):
            ici(outs, scr, a, 2, diag, nbr1, c).wait_recv()
            d2d(outs, scr, a, 2, diag, c).start()
        for a in range(n):
            d2d(outs, scr, a, 0, nbr1, 1 - c).wait_recv()
            d2d(outs, scr, a, 1, nbr0, 1 - c).wait_recv()
            d2d(outs, scr, a, 2, diag, 1 - c).wait_recv()
        for a in range(n):
            ici(outs, scr, a, 0, (x, y), nbr0, c).wait_send()
            ici(outs, scr, a, 1, (x, y), nbr1, c).wait_send()
            ici(outs, scr, a, 2, nbr0, nbr1, c).wait_send()
            d2d(outs, scr, a, 0, nbr0, c).wait_send()
            d2d(outs, scr, a, 1, nbr1, c).wait_send()
            d2d(outs, scr, a, 2, diag, c).wait_send()

    return _Exchange(list(slabs), [jax.ShapeDtypeStruct(s.shape, s.dtype) for s in slabs],
                     {a: a for a in range(n)}, [pltpu.SemaphoreType.DMA((3 * n,))] * 4, start, finish, mid, mid_at)


def _ex_to_sibling(grads):
    n = len(grads)

    def copy(ins, outs, scr, a):
        x, y, c, _ = _place()
        hr = grads[a].shape[1] // 2
        return _remote(ins[a].at[:, pl.ds((1 - c) * hr, hr), :], outs[a], scr[0].at[a], scr[1].at[a], (x, y, 1 - c))

    def start(ins, outs, scr):
        for a in range(n):
            copy(ins, outs, scr, a).start()

    def finish(ins, outs, scr):
        for a in range(n):
            copy(ins, outs, scr, a).wait()

    out_shape = [jax.ShapeDtypeStruct((g.shape[0], g.shape[1] // 2, g.shape[2]), g.dtype) for g in grads]
    return _Exchange(list(grads), out_shape, {}, [pltpu.SemaphoreType.DMA((n,))] * 2, start, finish)


def _ex_to_owner(parts, part=(0, 1), landing=None):
    n = len(parts)

    def copy(ins, outs, scr, a, j, chip):
        _, _, c, _ = _place()
        px, py = chip
        pr = parts[a].shape[1] // part[1]
        rows = pl.ds(part[0] * pr, pr)
        return _remote(ins[a].at[2 * px + py, rows], outs[a].at[j, rows], scr[0].at[3 * a + j],
                       scr[1].at[3 * a + j], (px, py, c))

    def start(ins, outs, scr):
        for j, chip in enumerate(_place()[3]):
            for a in range(n):
                copy(ins, outs, scr, a, j, chip).start()

    def finish(ins, outs, scr):
        for j, chip in enumerate(_place()[3]):
            for a in range(n):
                copy(ins, outs, scr, a, j, chip).wait()

    out_shape = [jax.ShapeDtypeStruct((3,) + p.shape[1:], p.dtype) for p in parts]
    operands, aliases = list(parts), {}
    if landing is not None:
        operands, aliases = operands + list(landing), {n + a: a for a in range(n)}
    return _Exchange(operands, out_shape, aliases, [pltpu.SemaphoreType.DMA((3 * n,))] * 2, start, finish)


def _ex_share_halves(bufs):
    n = len(bufs)

    def copy(outs, scr, a, cc):
        x, y, c, _ = _place()
        hr = bufs[a].shape[0] // 2
        piece = outs[a].at[pl.ds(cc * hr, hr), :]
        return _remote(piece, piece, scr[0].at[a], scr[1].at[a], (x, y, 1 - c))

    def start(ins, outs, scr):
        c = _place()[2]
        for a in range(n):
            copy(outs, scr, a, c).start()

    def finish(ins, outs, scr):
        c = _place()[2]
        for a in range(n):
            copy(outs, scr, a, c).wait_send()
            copy(outs, scr, a, 1 - c).wait_recv()

    return _Exchange(list(bufs), [jax.ShapeDtypeStruct(b.shape, b.dtype) for b in bufs], {a: a for a in range(n)},
                     [pltpu.SemaphoreType.DMA((n,))] * 2, start, finish)


def _ex_gather_small(arrs):
    n = len(arrs)

    def peer_of(m):
        x, y, c, _ = _place()
        return (1 - x if m & 4 else x, 1 - y if m & 2 else y, 1 - c if m & 1 else c)

    def own(ins, outs, scr, a):
        x, y, c, _ = _place()
        return pltpu.make_async_copy(ins[a], outs[a].at[4 * x + 2 * y + c], scr[2].at[a])

    def start(ins, outs, scr):
        x, y, c, _ = _place()
        for a in range(n):
            own(ins, outs, scr, a).start()
        for m in range(1, N_DEV):
            for a in range(n):
                k = (N_DEV - 1) * a + m - 1
                _remote(ins[a], outs[a].at[4 * x + 2 * y + c], scr[0].at[k], scr[1].at[k], peer_of(m)).start()

    def finish(ins, outs, scr):
        for a in range(n):
            own(ins, outs, scr, a).wait()
        for m in range(1, N_DEV):
            px, py, pc = peer_of(m)
            for a in range(n):
                k = (N_DEV - 1) * a + m - 1
                slot = outs[a].at[4 * px + 2 * py + pc]
                cp = _remote(ins[a], slot, scr[0].at[k], scr[1].at[k], (px, py, pc))
                cp.wait_send()
                cp.wait_recv()

    out_shape = [jax.ShapeDtypeStruct((N_DEV,) + a.shape, a.dtype) for a in arrs]
    return _Exchange(list(arrs), out_shape, {},
                     [pltpu.SemaphoreType.DMA(((N_DEV - 1) * n,))] * 2 + [pltpu.SemaphoreType.DMA((n,))], start, finish)


def _div_tile(n, want):
    best = None
    for t in range(8, min(n, want) + 1, 8):
        if n % t == 0:
            best = t
    assert best is not None, n
    return best


def _cast_into_slab(name, w, place, dtype):
    r, cc = w.shape
    tr = r if r * cc <= 128 * 1024 else _div_tile(r, 256)

    def body(s_ref, w_ref, o_ref):
        o_ref[...] = w_ref[...].astype(o_ref.dtype)

    return _pallas(
        body, name=name,
        grid_spec=pltpu.PrefetchScalarGridSpec(
            num_scalar_prefetch=1, grid=(r // tr,),
            in_specs=[pl.BlockSpec((tr, cc), lambda i, s: (i, 0))],
            out_specs=pl.BlockSpec((None, tr, cc), lambda i, s: (s[0], i, 0))),
        out_shape=jax.ShapeDtypeStruct((N_CHIPS, r, cc), dtype), compiler_params=_cp("parallel"),
    )(place, w)


def _add_half(name, g, rcv, place):
    nq, r, cc = g.shape
    hr = r // 2

    def body(s_ref, g_ref, r_ref, o_ref):
        o_ref[...] = (g_ref[...] + r_ref[...]).astype(o_ref.dtype)

    spec = pl.BlockSpec((None, hr, cc), lambda i, s: (i, 0, 0))
    return _pallas(
        body, name=name,
        grid_spec=pltpu.PrefetchScalarGridSpec(
            num_scalar_prefetch=1, grid=(nq,),
            in_specs=[pl.BlockSpec((None, hr, cc), lambda i, s: (i, s[1], 0)), spec], out_specs=spec),
        out_shape=jax.ShapeDtypeStruct((nq, hr, cc), BF16), compiler_params=_cp("parallel"),
    )(place, g, rcv)


def _sum_owner(name, part, rcv, place):
    _, hr, cc = part.shape
    tr = _div_tile(hr, 128)
    nb = hr // tr

    def body(s_ref, p_ref, r_ref, o_ref):
        o_ref[...] = ((p_ref[...].astype(F32) + r_ref[0].astype(F32)) + r_ref[1].astype(F32)) + r_ref[2].astype(F32)

    return _pallas(
        body, name=name,
        grid_spec=pltpu.PrefetchScalarGridSpec(
            num_scalar_prefetch=1, grid=(nb,),
            in_specs=[pl.BlockSpec((None, tr, cc), lambda i, s: (s[0], i, 0)),
                      pl.BlockSpec((3, tr, cc), lambda i, s: (0, i, 0))],
            out_specs=pl.BlockSpec((tr, cc), lambda i, s: (s[1] * nb + i, 0))),
        out_shape=jax.ShapeDtypeStruct((2 * hr, cc), F32), compiler_params=_cp("parallel"),
    )(place, part, rcv)


def _sum_small(gathered, local, place):
    n = len(gathered)

    def body(s_ref, *refs):
        g_refs, l_refs, o_refs = refs[:n], refs[n:2 * n], refs[2 * n:]
        me = s_ref[2]
        for g_ref, l_ref, o_ref in zip(g_refs, l_refs, o_refs):
            acc = None
            for d in range(N_DEV):
                term = jnp.where(me == d, l_ref[...], g_ref[d])
                acc = term if acc is None else acc + term
            o_ref[...] = acc

    def whole(shape):
        return pl.BlockSpec(shape, lambda i, s, nd=len(shape): (0,) * nd)

    return _pallas(
        body, name="sum_small",
        grid_spec=pltpu.PrefetchScalarGridSpec(
            num_scalar_prefetch=1, grid=(1,),
            in_specs=[whole(g.shape) for g in gathered] + [whole(a.shape) for a in local],
            out_specs=tuple(whole(a.shape) for a in local)),
        out_shape=tuple(jax.ShapeDtypeStruct(a.shape, a.dtype) for a in local), compiler_params=_cp("arbitrary"),
    )(place, *gathered, *local)


def _adamw(name, w, g, m, v):
    r, cc = w.shape
    tr = r if r * cc <= 128 * 1024 else _div_tile(r, 256)

    def body(w_ref, g_ref, m_ref, v_ref, d_ref, mo_ref, vo_ref, go_ref):
        gv = g_ref[...]
        go_ref[...] = gv
        mn = ADAM_B1 * m_ref[...] + (1.0 - ADAM_B1) * gv
        vn = ADAM_B2 * v_ref[...] + (1.0 - ADAM_B2) * (gv * gv)
        m_hat = mn / (1.0 - ADAM_B1 ** ADAM_STEP)
        v_hat = vn / (1.0 - ADAM_B2 ** ADAM_STEP)
        d_ref[...] = -ADAM_LR * (m_hat / (jnp.sqrt(v_hat) + ADAM_EPS) + ADAM_WD * w_ref[...])
        mo_ref[...] = mn
        vo_ref[...] = vn

    spec = pl.BlockSpec((tr, cc), lambda i: (i, 0))
    sd = jax.ShapeDtypeStruct((r, cc), F32)
    return _pallas(
        body, name=name, grid=(r // tr,), in_specs=[spec] * 4, out_specs=(spec,) * 4, out_shape=(sd,) * 4,
        compiler_params=_cp("parallel"),
    )(w, g, m, v)


_BIG = ("w_in", "w_up", "w_branch", "w_mem_kv", "w_out", "w_down")
_BIG_SHARD_SHAPE = {"w_in": (1024, 1664), "w_up": (1024, 1408), "w_branch": (1536, 256),
                    "w_mem_kv": (256, 1024), "w_out": (256, 1024), "w_down": (704, 1024)}
_SMALL_SHAPE = {"norm1_g": (1, D_MODEL), "ln_v_g": (1, GM_WIDTH), "ln_v_b": (1, GM_WIDTH),
                "w_spatial": (GM_GROUPS * GM_CHUNK, GM_CHUNK), "b_spatial": (GM_GROUPS, GM_CHUNK),
                "lb_logits": (2, HG_HEADS * HG_DIM), "hgrn_norm_g": (1, HG_DIM), "mem_norm_g": (1, D_MODEL),
                "norm2_g": (1, D_MODEL), "conv_w": (3, D_FF), "conv_b": (1, D_FF), "final_g": (1, D_MODEL)}
_SMALL_EARLY = tuple(n for n in _SMALL_SHAPE if n != "norm1_g")
_PARAM_ORDER = ("norm1_g", "w_in", "ln_v_g", "ln_v_b", "w_spatial", "b_spatial", "lb_logits", "hgrn_norm_g",
                "mem_norm_g", "w_mem_kv", "w_branch", "w_out", "norm2_g", "w_up", "conv_w", "conv_b", "w_down",
                "final_g")


def _adamw_small(ws, gs, ms, vs):
    n = len(ws)

    def body(*refs):
        w_refs, g_refs, m_refs, v_refs = refs[:n], refs[n:2 * n], refs[2 * n:3 * n], refs[3 * n:4 * n]
        d_refs, mo_refs, vo_refs = refs[4 * n:5 * n], refs[5 * n:6 * n], refs[6 * n:]
        for k in range(n):
            gv = g_refs[k][...]
            mn = ADAM_B1 * m_refs[k][...] + (1.0 - ADAM_B1) * gv
            vn = ADAM_B2 * v_refs[k][...] + (1.0 - ADAM_B2) * (gv * gv)
            m_hat = mn / (1.0 - ADAM_B1 ** ADAM_STEP)
            v_hat = vn / (1.0 - ADAM_B2 ** ADAM_STEP)
            d_refs[k][...] = -ADAM_LR * (m_hat / (jnp.sqrt(v_hat) + ADAM_EPS) + ADAM_WD * w_refs[k][...])
            mo_refs[k][...] = mn
            vo_refs[k][...] = vn

    specs = [pl.BlockSpec(a.shape, lambda i, nd=a.ndim: (0,) * nd) for a in ws]
    shapes = tuple(jax.ShapeDtypeStruct(a.shape, F32) for a in ws)
    res = _pallas(
        body, name="adamw_small", grid=(1,), in_specs=specs * 4, out_specs=tuple(specs * 3), out_shape=shapes * 3,
        compiler_params=_cp("arbitrary"),
    )(*ws, *gs, *ms, *vs)
    return res[:n], res[n:2 * n], res[2 * n:]


class _Comm:
    _ROW_SHARDED = ("w_mem_kv", "w_out", "w_down")

    def __init__(self, slabs, place):
        self.slabs, self.place = slabs, place
        self.full, self.raw, self.parts, self.landing, self.bufs, self.done = {}, {}, {}, {}, {}, {}

    def w(self, name):
        a = self.full[name]
        if name in self._ROW_SHARDED:
            return a.reshape(-1, a.shape[-1])
        if name == "conv_w":
            return jnp.transpose(a, (1, 0, 2)).reshape(3, 1, D_FF)
        return a

    sends = True

    def grad(self, name, arr, from_sibling=None):
        self.raw[name] = arr.reshape((N_CHIPS, -1, arr.shape[-1]))
        if from_sibling is not None:
            self.parts[name] = _add_half("rs_add_" + name, self.raw[name], from_sibling, self.place)

    def small_grads(self, arrays):
        self.small_local = list(arrays)

    def carry(self, tag, call):
        plan = self._plan(tag)
        if not plan:
            return call(())
        out, carried = call([ex for ex, _ in plan])
        for (_, deliver), res in zip(plan, carried):
            deliver(res)
        return out

    def finish(self, last_small):
        ex, deliver = self._share(["w_out", "w_branch", "w_mem_kv", "w_in"])
        shared, small = _run_exchanges("share_and_gather_last", [ex, _ex_gather_small(last_small)])
        deliver(shared)
        return self.done, self.small_local + list(last_small), self.small_everyone + small

    def _plan(self, tag):
        if tag == "norm1":
            def deliver(res):
                self.full["w_in"] = res[0]

            return [(_ex_gather_relay([self.slabs["w_in"]]), deliver)]
        if tag == "in_proj":
            return [self._gather_relay(["w_branch", "w_out", "w_mem_kv", "w_down"], 0.6), self._gather(["conv_w"])]
        if tag == "hgrn_fwd":
            return [self._gather_relay(["w_up"], 0.8)]
        if tag == "d_h2":
            return [self._to_sibling(["w_down", "w_up"])]
        if tag == "merge_bwd":
            return [self._to_owner(["w_up"], (0, 2))]
        if tag == "hgrn_bwd":
            return [self._to_owner(["w_down"]), self._to_owner(["w_up"], (1, 2)),
                    self._to_sibling(["w_out", "w_branch", "w_mem_kv"])]
        if tag == "g_w_in":
            def keep(res):
                self.small_everyone = res

            return [self._to_owner(["w_out", "w_branch", "w_mem_kv"]), (_ex_gather_small(self.small_local), keep)]
        if tag == "d_h":
            return [self._to_owner(["w_in"]), self._share(["w_down", "w_up"])]
        return []

    def _gather(self, names, part=(0, 1)):
        def deliver(res):
            self.slabs.update(zip(names, res))
            self.full.update(zip(names, res))

        return _ex_all_gather([self.slabs[n] for n in names], [n != "conv_w" for n in names], part), deliver

    def _gather_relay(self, names, mid_at):
        return _ex_gather_relay([self.slabs[n] for n in names], mid_at), lambda res: self.full.update(zip(names, res))

    def _to_sibling(self, names):
        def deliver(res):
            for n, r in zip(names, res):
                self.parts[n] = _add_half("rs_add_" + n, self.raw[n], r, self.place)

        return _ex_to_sibling([self.raw[n] for n in names]), deliver

    def _to_owner(self, names, part=(0, 1)):
        def deliver(res):
            for n, r in zip(names, res):
                if part[0] + 1 < part[1]:
                    self.landing[n] = r
                else:
                    self.bufs[n] = _sum_owner("rs_sum_" + n, self.parts[n], r, self.place)

        landing = [self.landing[n] for n in names] if part[0] else None
        return _ex_to_owner([self.parts[n] for n in names], part, landing), deliver

    def _share(self, names):
        return _ex_share_halves([self.bufs[n] for n in names]), lambda res: self.done.update(zip(names, res))


def kernel(x, mem, norm1_g, w_in, ln_v_g, ln_v_b, w_spatial, b_spatial, lb_logits, hgrn_norm_g, mem_norm_g, w_mem_kv, w_branch, w_out, norm2_g, w_up, conv_w, conv_b, w_down, final_g, loss_target, m_norm1_g, m_w_in, m_ln_v_g, m_ln_v_b, m_w_spatial, m_b_spatial, m_lb_logits, m_hgrn_norm_g, m_mem_norm_g, m_w_mem_kv, m_w_branch, m_w_out, m_norm2_g, m_w_up, m_conv_w, m_conv_b, m_w_down, m_final_g, v_norm1_g, v_w_in, v_ln_v_g, v_ln_v_b, v_w_spatial, v_b_spatial, v_lb_logits, v_hgrn_norm_g, v_mem_norm_g, v_w_mem_kv, v_w_branch, v_w_out, v_norm2_g, v_w_up, v_conv_w, v_conv_b, v_w_down, v_final_g):
    w = dict(norm1_g=norm1_g, w_in=w_in, ln_v_g=ln_v_g, ln_v_b=ln_v_b, w_spatial=w_spatial, b_spatial=b_spatial,
             lb_logits=lb_logits, hgrn_norm_g=hgrn_norm_g, mem_norm_g=mem_norm_g, w_mem_kv=w_mem_kv,
             w_branch=w_branch, w_out=w_out, norm2_g=norm2_g, w_up=w_up, conv_w=conv_w, conv_b=conv_b,
             w_down=w_down, final_g=final_g)
    mom = dict(norm1_g=m_norm1_g, w_in=m_w_in, ln_v_g=m_ln_v_g, ln_v_b=m_ln_v_b, w_spatial=m_w_spatial,
               b_spatial=m_b_spatial, lb_logits=m_lb_logits, hgrn_norm_g=m_hgrn_norm_g, mem_norm_g=m_mem_norm_g,
               w_mem_kv=m_w_mem_kv, w_branch=m_w_branch, w_out=m_w_out, norm2_g=m_norm2_g, w_up=m_w_up,
               conv_w=m_conv_w, conv_b=m_conv_b, w_down=m_w_down, final_g=m_final_g)
    var = dict(norm1_g=v_norm1_g, w_in=v_w_in, ln_v_g=v_ln_v_g, ln_v_b=v_ln_v_b, w_spatial=v_w_spatial,
               b_spatial=v_b_spatial, lb_logits=v_lb_logits, hgrn_norm_g=v_hgrn_norm_g, mem_norm_g=v_mem_norm_g,
               w_mem_kv=v_w_mem_kv, w_branch=v_w_branch, w_out=v_w_out, norm2_g=v_norm2_g, w_up=v_w_up,
               conv_w=v_conv_w, conv_b=v_conv_b, w_down=v_w_down, final_g=v_final_g)
    B, S, D = x.shape
    T = B * S
    ci = lax.axis_index("c")
    q = 2 * lax.axis_index("x") + lax.axis_index("y")
    place = jnp.stack([q, ci, 2 * q + ci]).astype(jnp.int32)

    shards = {n: w[n].reshape(_BIG_SHARD_SHAPE[n]) for n in _BIG}
    slabs = {"w_in": _cast_into_slab("slab_w_in", shards.pop("w_in"), place, BF16),
             "conv_w": _cast_into_slab("slab_conv_w", conv_w[0], place, F32)}
    comm = _Comm(slabs, place)
    p = dict(
        cast_beside_norm1=shards,
        norm1_g=norm1_g, ln_v_g=ln_v_g, ln_v_b=ln_v_b, w_spatial=w_spatial[0],
        b_spatial=b_spatial.reshape(GM_GROUPS, GM_CHUNK, 1), lb_logits=lb_logits, hgrn_norm_g=hgrn_norm_g,
        mem_norm_g=mem_norm_g, norm2_g=norm2_g, conv_b=conv_b, final_g=final_g.reshape(1, D))

    loss, grad_x, g = _local_step(x.reshape(T, D), mem.reshape(B * MEM_LEN, D), loss_target.reshape(T, D), p, comm,
                                  B, S)

    shard_grads, local_small, everyone = comm.finish([g["norm1_g"]])
    summed = _sum_small(everyone, local_small, place)
    small_names = list(_SMALL_EARLY) + ["norm1_g"]
    total = dict(zip(_SMALL_EARLY, summed))
    loss_total, total["norm1_g"] = summed[len(_SMALL_EARLY)][0, 0], summed[-1]

    grads, delta, new_m, new_v = {}, {}, {}, {}
    for n in _BIG:
        shp = _BIG_SHARD_SHAPE[n]
        delta[n], new_m[n], new_v[n], grads[n] = _adamw("adamw_" + n, w[n].reshape(shp), shard_grads[n],
                                                        mom[n].reshape(shp), var[n].reshape(shp))
    cw_shard = D_FF // N_CHIPS
    total["conv_w"] = lax.dynamic_slice(total["conv_w"], (0, q * cw_shard), (3, cw_shard)).reshape(3, 1, cw_shard)

    def flat2d(d, n):
        return d[n].reshape(total[n].shape)

    upd = _adamw_small([flat2d(w, n) for n in small_names], [total[n] for n in small_names],
                       [flat2d(mom, n) for n in small_names], [flat2d(var, n) for n in small_names])
    for k, n in enumerate(small_names):
        grads[n], delta[n], new_m[n], new_v[n] = total[n], upd[0][k], upd[1][k], upd[2][k]

    def shaped(d):
        return [d[n].reshape(w[n].shape) for n in _PARAM_ORDER]

    return (loss_total, grad_x.reshape(B, S, D), *shaped(grads), *shaped(delta), *shaped(new_m), *shaped(new_v))
```

```python
import functools
import math

import jax
import jax.numpy as jnp
from jax import lax
from jax.experimental import pallas as pl
from jax.experimental.pallas import tpu as pltpu

F32 = jnp.float32
BF16 = jnp.bfloat16
EPS = 1e-6

D_MODEL = 1024
MEM_LEN = 256
GM_WIDTH = 512
GM_CHUNK = 128
GM_GROUPS = 4
HG_HEADS = 4
HG_DIM = 128
HG_CHUNK = 64
XA_HEADS = 4
XA_DIM = 128
BR_WIDTH = 512
D_FF = 2816
IN_WIDTH = 6656
N_CHIPS = 4
N_DEV = 8

ADAM_LR = 0.001
ADAM_B1 = 0.9
ADAM_B2 = 0.999
ADAM_EPS = 1e-08
ADAM_WD = 0.01
ADAM_STEP = 10

COL_ZU, COL_ZV, COL_HQ, COL_HF, COL_HI, COL_HG, COL_XQ = 0, 1, 2, 3, 4, 5, 6
COL_GATE0 = 3584

VMEM_LIMIT_BYTES = 48 * 1024 * 1024
MESH_ID = pl.DeviceIdType.MESH


def _cp(*sem):
    return pltpu.CompilerParams(dimension_semantics=sem, vmem_limit_bytes=VMEM_LIMIT_BYTES)


def _pallas(body, *, out_shape, **kw):
    def pin(s):
        return pltpu.HBM(s.shape, s.dtype) if isinstance(s, jax.ShapeDtypeStruct) else s

    out_shape = tuple(pin(s) for s in out_shape) if isinstance(out_shape, (tuple, list)) else pin(out_shape)
    call = pl.pallas_call(body, out_shape=out_shape, **kw)

    def run(*operands):
        return call(*[pltpu.with_memory_space_constraint(o, pltpu.HBM) if jnp.issubdtype(o.dtype, jnp.floating)
                      else o for o in operands])

    return run


def _dot(a, b):
    return lax.dot_general(a.astype(BF16), b.astype(BF16), (((1,), (0,)), ((), ())), preferred_element_type=F32)


def _dot_nt(a, b):
    return lax.dot_general(a.astype(BF16), b.astype(BF16), (((1,), (1,)), ((), ())), preferred_element_type=F32)


def _dot_tn(a, b):
    return lax.dot_general(a.astype(BF16), b.astype(BF16), (((0,), (0,)), ((), ())), preferred_element_type=F32)


def _dot_01(mask01, x):
    hi = x.astype(BF16)
    r1 = x - hi.astype(F32)
    mid = r1.astype(BF16)
    lo = (r1 - mid.astype(F32)).astype(BF16)
    m = mask01.astype(BF16)
    dn = (((1,), (0,)), ((), ()))
    return (lax.dot_general(m, hi, dn, preferred_element_type=F32)
            + lax.dot_general(m, mid, dn, preferred_element_type=F32)
            + lax.dot_general(m, lo, dn, preferred_element_type=F32))


def _sigmoid(z):
    return 1.0 / (1.0 + jnp.exp(-z))


_GELU_C = math.sqrt(2.0 / math.pi)


def _gelu_and_grad(z):
    inner = _GELU_C * (z + 0.044715 * z * z * z)
    t = jnp.tanh(inner)
    val = 0.5 * z * (1.0 + t)
    grad = 0.5 * (1.0 + t) + 0.5 * z * (1.0 - t * t) * _GELU_C * (1.0 + 3.0 * 0.044715 * z * z)
    return val, grad


def _row_tile(n, want=512):
    t = min(want, n)
    assert n % t == 0
    return t


def _pcall(body, operands, *, name, grid, in_specs, out_specs, out_shape, scratch_shapes=(), semantics, riders=(),
           prefetch=None):
    single = not isinstance(out_shape, (tuple, list))
    out_specs = (out_specs,) if single else tuple(out_specs)
    out_shape = (out_shape,) if single else tuple(out_shape)
    n_pre = 0 if prefetch is None else 1

    def call(fn, ins_, outs_, shapes_, scr_, ops, sem, aliases):
        if prefetch is None:
            return _pallas(fn, name=name, grid=grid, in_specs=ins_, out_specs=outs_, out_shape=shapes_,
                           scratch_shapes=scr_, input_output_aliases=aliases, compiler_params=_cp(*sem))(*ops)
        spec = pltpu.PrefetchScalarGridSpec(num_scalar_prefetch=1, grid=grid, in_specs=ins_, out_specs=outs_,
                                            scratch_shapes=scr_)
        return _pallas(fn, name=name, grid_spec=spec, out_shape=shapes_, input_output_aliases=aliases,
                       compiler_params=_cp(*sem))(prefetch, *ops)

    if not riders:
        res = call(body, list(in_specs), out_specs, out_shape, list(scratch_shapes), operands, semantics, {})
        return (res[0] if single else res), []
    n_in, n_out, n_scr = len(in_specs), len(out_shape), len(scratch_shapes)
    ex_in = [len(ex.operands) for ex in riders]
    ex_out = [len(ex.out_shape) for ex in riders]
    ex_scr = [len(ex.scratch) for ex in riders]
    tot_in, tot_out = n_in + sum(ex_in), n_out + sum(ex_out)

    def wrapped(*refs):
        pre, refs = refs[:n_pre], refs[n_pre:]
        ins, outs, scr = refs[:tot_in], refs[tot_in:tot_in + tot_out], refs[tot_in + tot_out:]
        ids = [pl.program_id(d) for d in range(len(grid))]
        first = functools.reduce(lambda p, t: p & t, [i == 0 for i in ids])
        last = functools.reduce(lambda p, t: p & t, [i == n - 1 for i, n in zip(ids, grid)])
        parts, oi, oo, os_ = [], n_in, n_out, n_scr
        for k in range(len(riders)):
            parts.append((ins[oi:oi + ex_in[k]], outs[oo:oo + ex_out[k]], scr[os_:os_ + ex_scr[k]]))
            oi, oo, os_ = oi + ex_in[k], oo + ex_out[k], os_ + ex_scr[k]

        @pl.when(first)
        def _():
            for ex, part in zip(riders, parts):
                ex.start(*part)

        step, total = 0, 1
        for i, n in zip(ids, grid):
            step, total = step * n + i, total * n
        for ex, part in zip(riders, parts):
            if ex.mid is not None:
                @pl.when(step == min(total - 1, int(total * ex.mid_at)))
                def _(ex=ex, part=part):
                    ex.mid(*part)

        body(*pre, *ins[:n_in], *outs[:n_out], *scr[:n_scr])

        @pl.when(last)
        def _():
            for ex, part in zip(riders, parts):
                ex.finish(*part)

    aliases, oi, oo = {}, n_in, n_out
    all_ops, all_shapes, all_scr = list(operands), list(out_shape), list(scratch_shapes)
    for k, ex in enumerate(riders):
        aliases.update({n_pre + oi + a: oo + b for a, b in ex.aliases.items()})
        oi, oo = oi + ex_in[k], oo + ex_out[k]
        all_ops += list(ex.operands)
        all_shapes += [pltpu.HBM(s.shape, s.dtype) for s in ex.out_shape]
        all_scr += list(ex.scratch)
    res = call(wrapped, list(in_specs) + [HBM_SPEC] * sum(ex_in), out_specs + (HBM_SPEC,) * sum(ex_out),
               tuple(all_shapes), all_scr, all_ops, ["arbitrary"] * len(grid), aliases)
    own = res[0] if single else tuple(res[:n_out])
    carried, oo = [], n_out
    for k in range(len(riders)):
        carried.append(list(res[oo:oo + ex_out[k]]))
        oo += ex_out[k]
    return own, carried


def _carried(out, carried, riders):
    return (out, carried) if riders else out


def _matmul(name, operands, *, grid, in_specs, o_spec, out_shape, out_dtype, dims, riders=()):
    nk = grid[2]
    assert nk == 1 or out_dtype == F32

    def body(a_ref, b_ref, o_ref):
        part = lax.dot_general(a_ref[...].astype(BF16), b_ref[...].astype(BF16), (dims, ((), ())),
                               preferred_element_type=F32)
        if nk == 1:
            o_ref[...] = part.astype(o_ref.dtype)
        else:
            k = pl.program_id(2)

            @pl.when(k == 0)
            def _():
                o_ref[...] = part

            @pl.when(k > 0)
            def _():
                o_ref[...] += part

    out, carried = _pcall(body, operands, name=name, grid=grid, in_specs=in_specs, out_specs=o_spec,
                          out_shape=jax.ShapeDtypeStruct(out_shape, out_dtype),
                          semantics=("parallel", "parallel", "arbitrary"), riders=riders)
    return (out, carried) if riders else out


NN = ((1,), (0,))
NT = ((1,), (1,))
TN = ((0,), (0,))
_TN_TOKENS = 4096


def _mm_cs(name, a, w, out_dtype, riders=()):
    M, K = a.shape
    nq, _, wd = w.shape
    tm = _row_tile(M)
    return _matmul(name, (a, w), grid=(nq, M // tm, 1),
                   in_specs=[pl.BlockSpec((tm, K), lambda j, i, k: (i, 0)),
                             pl.BlockSpec((None, K, wd), lambda j, i, k: (j, 0, 0))],
                   o_spec=pl.BlockSpec((tm, wd), lambda j, i, k: (i, j)),
                   out_shape=(M, nq * wd), out_dtype=out_dtype, dims=NN, riders=riders)


def _mm_rs(name, a, w, out_dtype):
    M, K = a.shape
    N = w.shape[1]
    tm = _row_tile(M)
    return _matmul(name, (a, w), grid=(M // tm, 1, 1),
                   in_specs=[pl.BlockSpec((tm, K), lambda i, j, k: (i, 0)), pl.BlockSpec((K, N), lambda i, j, k: (0, 0))],
                   o_spec=pl.BlockSpec((tm, N), lambda i, j, k: (i, 0)),
                   out_shape=(M, N), out_dtype=out_dtype, dims=NN)


def _mm_nt_rs(name, g, w, out_dtype, riders=()):
    M, N = g.shape
    K = w.shape[0]
    to = K
    tm = _row_tile(M)
    return _matmul(name, (g, w), grid=(M // tm, K // to, 1),
                   in_specs=[pl.BlockSpec((tm, N), lambda i, j, k: (i, 0)),
                             pl.BlockSpec((to, N), lambda i, j, k: (j, 0))],
                   o_spec=pl.BlockSpec((tm, to), lambda i, j, k: (i, j)),
                   out_shape=(M, K), out_dtype=out_dtype, dims=NT, riders=riders)


def _mm_nt_cs(name, g, w, out_dtype, riders=(), stacked=False, norm_bwd=None):
    M = g.shape[-2]
    nq, K, wd = w.shape
    tm = _row_tile(M, 256)

    def product(g_ref, w_ref):
        acc = None
        for q in range(nq):
            gq = g_ref[q // 2, :, (q % 2) * wd:(q % 2 + 1) * wd] if stacked else g_ref[:, q * wd:(q + 1) * wd]
            part = _dot_nt(gq, w_ref[q])
            acc = part if acc is None else acc + part
        return acc

    def body(g_ref, w_ref, o_ref):
        o_ref[...] = product(g_ref, w_ref).astype(o_ref.dtype)

    def body_norm(g_ref, w_ref, x_ref, gain_ref, dr_ref, dx_ref, dg_ref):
        @pl.when(pl.program_id(0) == 0)
        def _():
            dg_ref[...] = jnp.zeros_like(dg_ref)

        dx, dg = _rms_bwd_rows(x_ref[...], gain_ref[...], product(g_ref, w_ref))
        dg_ref[...] += dg
        dx_ref[...] = dx + dr_ref[...]

    g_spec = (pl.BlockSpec((2, tm, 2 * wd), lambda i: (0, i, 0)) if stacked
              else pl.BlockSpec((tm, nq * wd), lambda i: (i, 0)))
    w_spec = pl.BlockSpec((nq, K, wd), lambda i: (0, 0, 0))
    row = pl.BlockSpec((tm, K), lambda i: (i, 0))
    if norm_bwd is None:
        return _carried(*_pcall(
            body, (g, w), name=name, grid=(M // tm,), in_specs=[g_spec, w_spec], out_specs=row,
            out_shape=jax.ShapeDtypeStruct((M, K), out_dtype), semantics=("parallel",), riders=riders), riders)
    vec = pl.BlockSpec((1, K), lambda i: (0, 0))
    return _carried(*_pcall(
        body_norm, (g, w) + tuple(norm_bwd), name=name, grid=(M // tm,),
        in_specs=[g_spec, w_spec, row, vec, row], out_specs=(row, vec),
        out_shape=(jax.ShapeDtypeStruct((M, K), F32), jax.ShapeDtypeStruct((1, K), F32)),
        semantics=("arbitrary",), riders=riders), riders)


def _mm_tn_rs(name, a, g, to, tn=512):
    T, M = a.shape
    N = g.shape[1]
    tt = _row_tile(T, _TN_TOKENS)
    tn = min(tn, N)
    return _matmul(name, (a, g), grid=(M // to, N // tn, T // tt),
                   in_specs=[pl.BlockSpec((tt, to), lambda i, j, k: (k, i)),
                             pl.BlockSpec((tt, tn), lambda i, j, k: (k, j))],
                   o_spec=pl.BlockSpec((to, tn), lambda i, j, k: (i, j)),
                   out_shape=(M, N), out_dtype=F32, dims=TN)


def _mm_tn_cs(name, a, g, nq, to, riders=(), stacked=False):
    T, M = a.shape
    wd = g.shape[-1] * (2 if stacked else 1) // nq
    tt = _row_tile(T, _TN_TOKENS)
    g_spec = (pl.BlockSpec((None, tt, wd), lambda i, j, k: (j // 2, k, j % 2)) if stacked
              else pl.BlockSpec((tt, wd), lambda i, j, k: (k, j)))
    return _matmul(name, (a, g), grid=(M // to, nq, T // tt),
                   in_specs=[pl.BlockSpec((tt, to), lambda i, j, k: (k, i)), g_spec],
                   o_spec=pl.BlockSpec((None, to, wd), lambda i, j, k: (j, i, 0)),
                   out_shape=(nq, M, wd), out_dtype=F32, dims=TN, riders=riders)


def _mm_tn_cs_to_sibling(name, a, g, nq, place, riders=(), send=True):
    T, M = a.shape
    wd = g.shape[-1] // nq
    to = M // 2
    steps = 2 * nq

    def body(s_ref, a_ref, g_ref, o_hbm, land_hbm, acc, wsem, send_sem, recv_sem):
        t = pl.program_id(0)
        c = s_ref[1]

        def writeback(tt):
            half = (tt // nq + 1 + c) % 2
            return pltpu.make_async_copy(acc.at[tt % 2], o_hbm.at[tt % nq, pl.ds(half * to, to), :], wsem.at[tt % 2])

        @pl.when(t >= 2)
        def _():
            writeback(t - 2).wait()

        if send:
            x, y, _, _ = _place()
            to_sibling = _remote(o_hbm.at[:, pl.ds((1 - c) * to, to), :], land_hbm, send_sem.at[0], recv_sem.at[0],
                                 (x, y, 1 - c))

            @pl.when(t == nq + 1)
            def _():
                to_sibling.start()

        acc[t % 2] = _dot_tn(a_ref[...], g_ref[...])
        writeback(t).start()

        @pl.when(t == steps - 1)
        def _():
            writeback(t - 1).wait()
            writeback(t).wait()
            if send:
                to_sibling.wait()

    out, carried = _pcall(
        body, (a, g), name=name, grid=(steps,),
        in_specs=[pl.BlockSpec((T, to), lambda t, s: (0, (t // nq + 1 + s[1]) % 2)),
                  pl.BlockSpec((T, wd), lambda t, s: (0, t % nq))],
        out_specs=(HBM_SPEC, HBM_SPEC),
        out_shape=(jax.ShapeDtypeStruct((nq, M, wd), F32), jax.ShapeDtypeStruct((nq, to, wd), F32)),
        scratch_shapes=[pltpu.VMEM((2, to, wd), F32), pltpu.SemaphoreType.DMA((2,)),
                        pltpu.SemaphoreType.DMA((1,)), pltpu.SemaphoreType.DMA((1,))],
        semantics=("arbitrary",), riders=riders, prefetch=place)
    return _carried(out, carried, riders)


def _rms_fwd(name, x, g, riders=()):
    T, D = x.shape
    tm = _row_tile(T)

    def body(x_ref, g_ref, o_ref):
        o_ref[...] = _rms_rows(x_ref[...], g_ref[...]).astype(o_ref.dtype)

    return _carried(*_pcall(
        body, (x, g), name=name, grid=(T // tm,),
        in_specs=[pl.BlockSpec((tm, D), lambda i: (i, 0)), pl.BlockSpec((1, D), lambda i: (0, 0))],
        out_specs=pl.BlockSpec((tm, D), lambda i: (i, 0)),
        out_shape=jax.ShapeDtypeStruct((T, D), BF16), semantics=("parallel",), riders=riders), riders)


_NORM1_STEPS = 4


def _norm1_and_casts(x, g, shards, place, riders=()):
    T, D = x.shape
    tm = T // _NORM1_STEPS
    names = list(shards)

    def body(s_ref, x_ref, g_ref, *refs):
        w_refs, o_ref, slab_refs = refs[:len(names)], refs[len(names)], refs[len(names) + 1:]
        o_ref[...] = _rms_rows(x_ref[...], g_ref[...]).astype(o_ref.dtype)
        for w_ref, slab_ref in zip(w_refs, slab_refs):
            slab_ref[...] = w_ref[...].astype(slab_ref.dtype)

    in_specs = [pl.BlockSpec((tm, D), lambda i, s: (i, 0)), pl.BlockSpec((1, D), lambda i, s: (0, 0))]
    out_specs = [pl.BlockSpec((tm, D), lambda i, s: (i, 0))]
    out_shape = [jax.ShapeDtypeStruct((T, D), BF16)]
    for n in names:
        r, cc = shards[n].shape
        assert r % (_NORM1_STEPS * 16) == 0
        in_specs.append(pl.BlockSpec((r // _NORM1_STEPS, cc), lambda i, s: (i, 0)))
        out_specs.append(pl.BlockSpec((None, r // _NORM1_STEPS, cc), lambda i, s: (s[0], i, 0)))
        out_shape.append(jax.ShapeDtypeStruct((N_CHIPS, r, cc), BF16))
    out, carried = _pcall(body, (x, g, *[shards[n] for n in names]), name="norm1", grid=(_NORM1_STEPS,),
                          in_specs=in_specs, out_specs=out_specs, out_shape=out_shape, semantics=("parallel",),
                          riders=riders, prefetch=place)
    return _carried((out[0], dict(zip(names, out[1:]))), carried, riders)


def _rms_rows(xv, gain):
    return xv * lax.rsqrt(jnp.mean(xv * xv, axis=-1, keepdims=True) + EPS) * gain


def _rms_bwd_rows(xv, gain, dh):
    r = lax.rsqrt(jnp.mean(xv * xv, axis=-1, keepdims=True) + EPS)
    n = xv * r
    dn = dh * gain
    return r * (dn - n * jnp.mean(dn * n, axis=-1, keepdims=True)), jnp.sum(dh * n, axis=0, keepdims=True)


def _rms_bwd(name, x, g, dh, dres):
    T, D = x.shape
    tm = _row_tile(T)
    has_res = dres is not None

    def body(*refs):
        if has_res:
            x_ref, g_ref, dh_ref, dr_ref, dx_ref, dg_ref = refs
        else:
            x_ref, g_ref, dh_ref, dx_ref, dg_ref = refs

        @pl.when(pl.program_id(0) == 0)
        def _():
            dg_ref[...] = jnp.zeros_like(dg_ref)

        dx, dg = _rms_bwd_rows(x_ref[...], g_ref[...], dh_ref[...])
        dg_ref[...] += dg
        if has_res:
            dx = dx + dr_ref[...]
        dx_ref[...] = dx

    row = pl.BlockSpec((tm, D), lambda i: (i, 0))
    vec = pl.BlockSpec((1, D), lambda i: (0, 0))
    ops = (x, g, dh, dres) if has_res else (x, g, dh)
    return _pallas(
        body, name=name, grid=(T // tm,), in_specs=[row, vec, row] + ([row] if has_res else []),
        out_specs=(row, vec),
        out_shape=(jax.ShapeDtypeStruct((T, D), F32), jax.ShapeDtypeStruct((1, D), F32)),
        compiler_params=_cp("arbitrary"),
    )(*ops)


def _proj_res_norm(name, a, w, res, gain):
    M, K = a.shape
    N = w.shape[1]
    tm = _row_tile(M)

    def body(a_ref, w_ref, r_ref, g_ref, x_ref, h_ref):
        xv = _dot(a_ref[...], w_ref[...]) + r_ref[...]
        x_ref[...] = xv
        h_ref[...] = _rms_rows(xv, g_ref[...]).astype(h_ref.dtype)

    row = pl.BlockSpec((tm, N), lambda i: (i, 0))
    return _pallas(
        body, name=name, grid=(M // tm,),
        in_specs=[pl.BlockSpec((tm, K), lambda i: (i, 0)), pl.BlockSpec((K, N), lambda i: (0, 0)), row,
                  pl.BlockSpec((1, N), lambda i: (0, 0))],
        out_specs=(row, row), out_shape=(jax.ShapeDtypeStruct((M, N), F32), jax.ShapeDtypeStruct((M, N), BF16)),
        compiler_params=_cp("parallel"),
    )(a, w, res, gain)


def _proj_res_loss(name, a, w, res, tgt, gain):
    M, K = a.shape
    D = w.shape[1]
    tm = _row_tile(M)

    def body(a_ref, w_ref, r_ref, t_ref, g_ref, dx_ref, dg_ref, loss_ref):
        @pl.when(pl.program_id(0) == 0)
        def _():
            dg_ref[...] = jnp.zeros_like(dg_ref)
            loss_ref[...] = jnp.zeros_like(loss_ref)

        xv = _dot(a_ref[...], w_ref[...]) + r_ref[...]
        gv = g_ref[...]
        diff = _rms_rows(xv, gv) - t_ref[...]
        loss_ref[...] += 0.5 * jnp.sum(jnp.mean(diff * diff, axis=-1, keepdims=True))
        dx, dg = _rms_bwd_rows(xv, gv, diff * (1.0 / D))
        dg_ref[...] += dg
        dx_ref[...] = dx

    row = pl.BlockSpec((tm, D), lambda i: (i, 0))
    vec = pl.BlockSpec((1, D), lambda i: (0, 0))
    return _pallas(
        body, name=name, grid=(M // tm,),
        in_specs=[pl.BlockSpec((tm, K), lambda i: (i, 0)), pl.BlockSpec((K, D), lambda i: (0, 0)), row, row, vec],
        out_specs=(row, vec, pl.BlockSpec((8, 128), lambda i: (0, 0))),
        out_shape=(jax.ShapeDtypeStruct((M, D), F32), jax.ShapeDtypeStruct((1, D), F32),
                   jax.ShapeDtypeStruct((8, 128), F32)),
        compiler_params=_cp("arbitrary"),
    )(a, w, res, tgt, gain)


def _gmlp_pieces(zu, zv, lng, lnb, ws_ref, bs_ref):
    u, du = _gelu_and_grad(zu)
    v, dv = _gelu_and_grad(zv)
    mu = jnp.mean(v, axis=-1, keepdims=True)
    vc = v - mu
    rstd = lax.rsqrt(jnp.mean(vc * vc, axis=-1, keepdims=True) + EPS)
    vhat = vc * rstd
    vn = vhat * lng + lnb
    row = lax.broadcasted_iota(jnp.int32, (GM_CHUNK, GM_CHUNK), 0)
    col = lax.broadcasted_iota(jnp.int32, (GM_CHUNK, GM_CHUNK), 1)
    tril = row >= col
    wms, mixed = [], []
    for g in range(GM_GROUPS):
        sl = slice(g * 128, (g + 1) * 128)
        wm = jnp.where(tril, ws_ref[g], 0.0)
        wms.append(wm)
        mixed.append(_dot(wm, vn[:, sl]) + bs_ref[g])
    return u, du, dv, rstd, vhat, vn, wms, mixed, tril


def _gmlp_fwd(proj, lng, lnb, ws, bs_col):
    T = proj.shape[0]
    n = T // GM_CHUNK

    def body(zu_ref, zv_ref, lng_ref, lnb_ref, ws_ref, bs_ref, o_ref):
        u, _, _, _, _, _, _, mixed, _ = _gmlp_pieces(zu_ref[...].astype(F32), zv_ref[...].astype(F32),
                                                     lng_ref[...], lnb_ref[...],
                                                     ws_ref, bs_ref)
        for g in range(GM_GROUPS):
            sl = slice(g * 128, (g + 1) * 128)
            o_ref[:, sl] = (u[:, sl] * mixed[g]).astype(o_ref.dtype)

    vec = pl.BlockSpec((1, GM_WIDTH), lambda i: (0, 0))
    return _pallas(
        body, name="gmlp_fwd", grid=(n,),
        in_specs=[pl.BlockSpec((GM_CHUNK, 512), lambda i: (i, COL_ZU)),
                  pl.BlockSpec((GM_CHUNK, 512), lambda i: (i, COL_ZV)),
                  vec, vec,
                  pl.BlockSpec((GM_GROUPS, 128, 128), lambda i: (0, 0, 0)),
                  pl.BlockSpec((GM_GROUPS, 128, 1), lambda i: (0, 0, 0))],
        out_specs=pl.BlockSpec((GM_CHUNK, 512), lambda i: (i, 0)),
        out_shape=jax.ShapeDtypeStruct((T, GM_WIDTH), BF16), compiler_params=_cp("parallel"),
    )(proj, proj, lng, lnb, ws, bs_col)


def _gmlp_bwd(proj, d_out, lng, lnb, ws, bs_col, riders=()):
    T = proj.shape[0]
    n = T // GM_CHUNK

    def body(zu_ref, zv_ref, do_ref, lng_ref, lnb_ref, ws_ref, bs_ref,
             dz_ref, dws_ref, dbs_ref, dlng_ref, dlnb_ref, dm_acc):
        i = pl.program_id(0)

        @pl.when(i == 0)
        def _():
            dws_ref[...] = jnp.zeros_like(dws_ref)
            dlng_ref[...] = jnp.zeros_like(dlng_ref)
            dlnb_ref[...] = jnp.zeros_like(dlnb_ref)
            dm_acc[...] = jnp.zeros_like(dm_acc)

        lng_v = lng_ref[...]
        u, du, dv, rstd, vhat, vn, wms, mixed, tril = _gmlp_pieces(zu_ref[...].astype(F32), zv_ref[...].astype(F32),
                                                                  lng_v, lnb_ref[...],
                                                                  ws_ref, bs_ref)
        do = do_ref[...]
        dvn_parts = []
        for g in range(GM_GROUPS):
            sl = slice(g * 128, (g + 1) * 128)
            dog = do[:, sl]
            dz_ref[:, sl] = (dog * mixed[g] * du[:, sl]).astype(dz_ref.dtype)
            dmix = dog * u[:, sl]
            dm_acc[:, sl] += dmix
            dws_ref[g] += jnp.where(tril, _dot_nt(dmix, vn[:, sl]), 0.0)
            dvn_parts.append(_dot_tn(wms[g], dmix))
        dvn = jnp.concatenate(dvn_parts, axis=1)
        dlng_ref[...] += jnp.sum(dvn * vhat, axis=0, keepdims=True)
        dlnb_ref[...] += jnp.sum(dvn, axis=0, keepdims=True)
        dvh = dvn * lng_v
        dvv = rstd * (dvh - jnp.mean(dvh, axis=-1, keepdims=True)
                      - vhat * jnp.mean(dvh * vhat, axis=-1, keepdims=True))
        dz_ref[:, GM_WIDTH:] = (dvv * dv).astype(dz_ref.dtype)

        @pl.when(i == n - 1)
        def _():
            for g in range(GM_GROUPS):
                dbs_ref[g] = jnp.sum(dm_acc[:, g * 128:(g + 1) * 128], axis=1, keepdims=True)

    vec = pl.BlockSpec((1, GM_WIDTH), lambda i: (0, 0))
    wsp = pl.BlockSpec((GM_GROUPS, 128, 128), lambda i: (0, 0, 0))
    bsp = pl.BlockSpec((GM_GROUPS, 128, 1), lambda i: (0, 0, 0))
    return _carried(*_pcall(
        body, (proj, proj, d_out, lng, lnb, ws, bs_col), name="gmlp_bwd", grid=(n,),
        in_specs=[pl.BlockSpec((GM_CHUNK, 512), lambda i: (i, COL_ZU)),
                  pl.BlockSpec((GM_CHUNK, 512), lambda i: (i, COL_ZV)),
                  pl.BlockSpec((None, GM_CHUNK, 512), lambda i: (0, i, 0)), vec, vec, wsp, bsp],
        out_specs=(pl.BlockSpec((GM_CHUNK, 2 * GM_WIDTH), lambda i: (i, 0)), wsp, bsp, vec, vec),
        out_shape=(jax.ShapeDtypeStruct((T, 2 * GM_WIDTH), BF16),
                   jax.ShapeDtypeStruct((GM_GROUPS, 128, 128), F32), jax.ShapeDtypeStruct((GM_GROUPS, 128, 1), F32),
                   jax.ShapeDtypeStruct((1, GM_WIDTH), F32), jax.ShapeDtypeStruct((1, GM_WIDTH), F32)),
        scratch_shapes=[pltpu.VMEM((GM_CHUNK, GM_WIDTH), F32)],
        semantics=("arbitrary",), riders=riders), riders)


def _hgrn_lower_bound(lbl):
    return 1.0 / (1.0 + jnp.exp(lbl[1:2, :] - lbl[0:1, :]))


def _hgrn_gates(hq, hf, lb):
    C = HG_CHUNK
    sg = _sigmoid(hf)
    fg = lb + (1.0 - lb) * sg
    sq = _sigmoid(hq)
    row = lax.broadcasted_iota(jnp.int32, (C, C), 0)
    col = lax.broadcasted_iota(jnp.int32, (C, C), 1)
    tril = row >= col
    logf = jnp.log(fg)
    a = _dot_01(tril, logf)
    a_last = jnp.sum(logf, axis=0, keepdims=True)
    first_half = lax.broadcasted_iota(jnp.int32, logf.shape, 0) < (C // 2)
    a_mid = jnp.sum(jnp.where(first_half, logf, 0.0), axis=0, keepdims=True)
    ea, ei, eki, ekl = jnp.exp(a), jnp.exp(a - a_mid), jnp.exp(a_mid - a), jnp.exp(a_last - a)
    k = 1.0 - fg
    q = hq * sq
    qi = (q * ei).astype(BF16).astype(F32)
    ki = (k * eki).astype(BF16).astype(F32)
    return dict(sg=sg, fg=fg, sq=sq, tril=tril, ea=ea, ei=ei, eki=eki, ekl=ekl, e_last=jnp.exp(a_last),
                qe=q * ea, qi=qi, ki=ki, kl=k * ekl)


def _heads(x):
    return [x[:, h * HG_DIM:(h + 1) * HG_DIM] for h in range(HG_HEADS)]


def _hgrn_fwd(proj, lbl, gh, B, S, riders=()):
    C = HG_CHUNK
    NC = S // C
    W = HG_HEADS * HG_DIM

    def body(q_ref, f_ref, i_ref, g_ref, lbl_ref, gh_ref, o_ref, bo_ref, st_ref, state):
        @pl.when(pl.program_id(0) == 0)
        def _():
            state[...] = jnp.zeros_like(state)

        lb = _hgrn_lower_bound(lbl_ref[...])
        ghv = gh_ref[...]
        for b in range(B):
            gt = _hgrn_gates(q_ref[b].astype(F32), f_ref[b].astype(F32), lb)
            v = _heads(i_ref[b])
            qe, qi, ki, kl, e_last = (_heads(gt[n]) for n in ("qe", "qi", "ki", "kl", "e_last"))
            outs, normed = [], []
            for h in range(HG_HEADS):
                p = jnp.where(gt["tril"], _dot_nt(qi[h], ki[h]), 0.0)
                st = state[b, h]
                st_ref[b, h] = st
                o = _dot_nt(qe[h], st) + _dot(p, v[h])
                state[b, h] = st * e_last[h] + _dot_tn(v[h], kl[h])
                outs.append(o)
                normed.append(o * lax.rsqrt(jnp.mean(o * o, axis=-1, keepdims=True) + EPS) * ghv)
            o_ref[b] = jnp.concatenate(outs, axis=1)
            hg = g_ref[b].astype(F32)
            bo_ref[b] = (jnp.concatenate(normed, axis=1) * (hg * _sigmoid(hg))).astype(bo_ref.dtype)

    def col(cb):
        return pl.BlockSpec((B, C, 512), lambda c: (0, c, cb))

    tile = pl.BlockSpec((B, C, W), lambda c: (0, c, 0))
    proj3 = proj.reshape(B, S, proj.shape[-1])
    out, carried = _pcall(
        body, (proj3, proj3, proj3, proj3, lbl, gh), name="hgrn_fwd", grid=(NC,),
        in_specs=[col(COL_HQ), col(COL_HF), col(COL_HI), col(COL_HG),
                  pl.BlockSpec((2, W), lambda c: (0, 0)), pl.BlockSpec((1, HG_DIM), lambda c: (0, 0))],
        out_specs=(tile, tile, pl.BlockSpec((B, None, HG_HEADS, 128, 128), lambda c: (0, c, 0, 0, 0))),
        out_shape=(jax.ShapeDtypeStruct((B, S, W), F32), jax.ShapeDtypeStruct((B, S, W), BF16),
                   jax.ShapeDtypeStruct((B, NC, HG_HEADS, 128, 128), F32)),
        scratch_shapes=[pltpu.VMEM((B, HG_HEADS, 128, 128), F32)],
        semantics=("arbitrary",), riders=riders)
    o_h, b_out, states = out
    out = (o_h, b_out.reshape(B * S, W), states)
    return (out, carried) if riders else out


def _hgrn_bwd(proj, o_saved, states, d_out, lbl, gh, others, B, S, riders=()):
    C = HG_CHUNK
    NC = S // C
    W = HG_HEADS * HG_DIM
    d_gm, d_xq, d_gates = (t.reshape(B, S, t.shape[-1]) for t in others)
    own0 = d_gm.shape[-1]
    xq0 = own0 + 4 * W
    gates0 = xq0 + d_xq.shape[-1]

    def body(q_ref, f_ref, i_ref, g_ref, o_ref, st_ref, do_ref, lbl_ref, gh_ref, gm_ref, xq_ref, gates_ref,
             d_ref, dlbl_ref, dgh_ref, dstate, dlb_acc):
        c = pl.program_id(0)
        d_ref[:, :, :own0] = gm_ref[...]
        d_ref[:, :, xq0:gates0] = xq_ref[...]
        d_ref[:, :, gates0:] = gates_ref[...]

        def put(b, k, val):
            d_ref[b, :, own0 + k * W:own0 + (k + 1) * W] = val.astype(d_ref.dtype)

        @pl.when(c == 0)
        def _():
            dstate[...] = jnp.zeros_like(dstate)
            dgh_ref[...] = jnp.zeros_like(dgh_ref)
            dlb_acc[...] = jnp.zeros_like(dlb_acc)

        lb = _hgrn_lower_bound(lbl_ref[...])
        ghv = gh_ref[...]
        row = lax.broadcasted_iota(jnp.int32, (C, C), 0)
        colm = lax.broadcasted_iota(jnp.int32, (C, C), 1)
        triu = colm >= row
        for b in range(B):
            hq, hg = q_ref[b].astype(F32), g_ref[b].astype(F32)
            gt = _hgrn_gates(hq, f_ref[b].astype(F32), lb)
            tril = gt["tril"]
            v = _heads(i_ref[b])
            qe, qi, ki, kl, e_last = (_heads(gt[n]) for n in ("qe", "qi", "ki", "kl", "e_last"))
            sgg = _sigmoid(hg)
            don_all = do_ref[b] * (hg * sgg)
            o, don = _heads(o_ref[b]), _heads(don_all)
            d_qe, d_qi, d_ki, d_kl, dv, n_all, dal = [], [], [], [], [], [], []
            for h in range(HG_HEADS):
                r = lax.rsqrt(jnp.mean(o[h] * o[h], axis=-1, keepdims=True) + EPS)
                n = o[h] * r
                n_all.append(n)
                dgh_ref[...] += jnp.sum(don[h] * n, axis=0, keepdims=True)
                dn = don[h] * ghv
                d_o = r * (dn - n * jnp.mean(dn * n, axis=-1, keepdims=True))
                st, dst = st_ref[b, h], dstate[b, h]
                p = jnp.where(tril, _dot_nt(qi[h], ki[h]), 0.0)
                dp = jnp.where(tril, _dot_nt(d_o, v[h]), 0.0)
                d_qe.append(_dot(d_o, st))
                d_qi.append(_dot(dp, ki[h]))
                d_ki.append(_dot_tn(dp, qi[h]))
                d_kl.append(_dot(v[h], dst))
                dv.append(_dot_tn(p, d_o) + _dot_nt(kl[h], dst))
                dstate[b, h] = dst * e_last[h] + _dot_tn(d_o, qe[h])
                dal.append(jnp.sum(dst * st, axis=0, keepdims=True) * e_last[h])
            d_qe, d_qi, d_ki, d_kl, n_all, dal = (jnp.concatenate(t, axis=1)
                                                  for t in (d_qe, d_qi, d_ki, d_kl, n_all, dal))
            put(b, 3, do_ref[b] * n_all * jnp.tile(ghv, (1, HG_HEADS)) * (sgg * (1.0 + hg * (1.0 - sgg))))
            put(b, 2, jnp.concatenate(dv, axis=1))
            d_a_last = dal + jnp.sum(d_kl * gt["kl"], axis=0, keepdims=True)
            dq = d_qe * gt["ea"] + d_qi * gt["ei"]
            dk = d_ki * gt["eki"] + d_kl * gt["ekl"]
            da = d_qe * gt["qe"] + d_qi * gt["qi"] - d_ki * gt["ki"] - d_kl * gt["kl"]
            dlogf = _dot_01(triu, da) + d_a_last
            sg, sq = gt["sg"], gt["sq"]
            dfg = dlogf / gt["fg"] - dk
            put(b, 1, dfg * (1.0 - lb) * sg * (1.0 - sg))
            dlb_acc[...] += jnp.sum(dfg * (1.0 - sg), axis=0, keepdims=True)
            put(b, 0, dq * (sq * (1.0 + hq * (1.0 - sq))))

        @pl.when(c == NC - 1)
        def _():
            dlb = dlb_acc[...]
            first = lax.broadcasted_iota(jnp.int32, (2, W), 0) == 0
            dlbl_ref[...] = jnp.where(first, dlb * lb * (1.0 - lb), -dlb * lb * (1.0 - lb))

    def col(cb):
        return pl.BlockSpec((B, C, 512), lambda c: (0, NC - 1 - c, cb))

    tile = pl.BlockSpec((B, C, W), lambda c: (0, NC - 1 - c, 0))
    proj3 = proj.reshape(B, S, proj.shape[-1])

    def rows(width):
        return pl.BlockSpec((B, C, width), lambda c: (0, NC - 1 - c, 0))

    width = proj.shape[-1]
    out, carried = _pcall(
        body, (proj3, proj3, proj3, proj3, o_saved, states, d_out.reshape(3, B, S, W), lbl, gh, d_gm, d_xq, d_gates),
        name="hgrn_bwd", grid=(NC,),
        in_specs=[col(COL_HQ), col(COL_HF), col(COL_HI), col(COL_HG), tile,
                  pl.BlockSpec((B, None, HG_HEADS, 128, 128), lambda c: (0, NC - 1 - c, 0, 0, 0)),
                  pl.BlockSpec((None, B, C, W), lambda c: (1, 0, NC - 1 - c, 0)),
                  pl.BlockSpec((2, W), lambda c: (0, 0)), pl.BlockSpec((1, HG_DIM), lambda c: (0, 0)),
                  rows(d_gm.shape[-1]), rows(d_xq.shape[-1]), rows(d_gates.shape[-1])],
        out_specs=(rows(width), pl.BlockSpec((2, W), lambda c: (0, 0)), pl.BlockSpec((1, HG_DIM), lambda c: (0, 0))),
        out_shape=(jax.ShapeDtypeStruct((B, S, width), BF16), jax.ShapeDtypeStruct((2, W), F32),
                   jax.ShapeDtypeStruct((1, HG_DIM), F32)),
        scratch_shapes=[pltpu.VMEM((B, HG_HEADS, 128, 128), F32), pltpu.VMEM((1, W), F32)],
        semantics=("arbitrary",), riders=riders)
    out = (out[0].reshape(B * S, width),) + tuple(out[1:])
    return (out, carried) if riders else out


_XA_SCALE = XA_DIM ** -0.5


def _attn_probs(qh, kh):
    s = _dot_nt(qh, kh) * _XA_SCALE
    e = jnp.exp(s - jnp.max(s, axis=-1, keepdims=True))
    return e / jnp.sum(e, axis=-1, keepdims=True)


def _attn_fwd(proj, kv, B, S):
    T = B * S
    tq = _row_tile(S)
    nq = S // tq
    W = XA_HEADS * XA_DIM

    def body(q_ref, kv_ref, o_ref):
        for h in range(XA_HEADS):
            sl = slice(h * 128, (h + 1) * 128)
            p = _attn_probs(q_ref[:, sl], kv_ref[:, sl])
            o_ref[:, sl] = _dot(p, kv_ref[:, W + h * 128:W + (h + 1) * 128]).astype(o_ref.dtype)

    return _pallas(
        body, name="attn_fwd", grid=(B, nq),
        in_specs=[pl.BlockSpec((tq, 512), lambda b, i: (b * nq + i, COL_XQ)),
                  pl.BlockSpec((MEM_LEN, 2 * W), lambda b, i: (b, 0))],
        out_specs=pl.BlockSpec((tq, W), lambda b, i: (b * nq + i, 0)),
        out_shape=jax.ShapeDtypeStruct((T, W), BF16), compiler_params=_cp("parallel", "parallel"),
    )(proj, kv)


def _attn_bwd(proj, kv, d_out, B, S):
    T = B * S
    tq = _row_tile(S)
    nq = S // tq
    W = XA_HEADS * XA_DIM

    def body(q_ref, kv_ref, do_ref, dq_ref, dkv_ref):
        @pl.when(pl.program_id(1) == 0)
        def _():
            dkv_ref[...] = jnp.zeros_like(dkv_ref)

        for h in range(XA_HEADS):
            sl = slice(h * 128, (h + 1) * 128)
            slv = slice(W + h * 128, W + (h + 1) * 128)
            qh = q_ref[:, sl]
            kh = kv_ref[:, sl]
            p = _attn_probs(qh, kh)
            dc = do_ref[:, sl]
            dp = _dot_nt(dc, kv_ref[:, slv])
            ds = p * (dp - jnp.sum(dp * p, axis=-1, keepdims=True)) * _XA_SCALE
            dq_ref[:, sl] = _dot(ds, kh).astype(dq_ref.dtype)
            dkv_ref[:, sl] += _dot_tn(ds, qh)
            dkv_ref[:, slv] += _dot_tn(p, dc)

    kvspec = pl.BlockSpec((MEM_LEN, 2 * W), lambda b, i: (b, 0))
    tile = pl.BlockSpec((tq, W), lambda b, i: (b * nq + i, 0))
    return _pallas(
        body, name="attn_bwd", grid=(B, nq),
        in_specs=[pl.BlockSpec((tq, 512), lambda b, i: (b * nq + i, COL_XQ)), kvspec,
                  pl.BlockSpec((None, tq, W), lambda b, i: (2, b * nq + i, 0))],
        out_specs=(tile, kvspec),
        out_shape=(jax.ShapeDtypeStruct((T, W), BF16), jax.ShapeDtypeStruct((B * MEM_LEN, 2 * W), F32)),
        compiler_params=_cp("parallel", "arbitrary"),
    )(proj, kv, d_out)


_MERGE_TM = 256
_GATE_W = 512


def _gate_specs(tm):
    base = COL_GATE0 // _GATE_W
    return [pl.BlockSpec((tm, _GATE_W), functools.partial(lambda i, k: (i, base + k), k=k)) for k in range(6)]


def _merge_fwd(a_out, b_out, c_out, wb, proj, riders=()):
    T = a_out.shape[0]
    tm = _row_tile(T, _MERGE_TM)
    nq, _, wd = wb.shape
    per_half = _GATE_W // wd

    def body(a_ref, b_ref, c_ref, w_ref, *rest):
        gates, (m_ref, up_ref) = rest[:6], rest[6:]
        for hf in range(2):
            cols = slice(hf * _GATE_W, (hf + 1) * _GATE_W)
            acc = None
            for n, br in enumerate((a_ref, b_ref, c_ref)):
                x = br[...]
                up = jnp.concatenate([_dot(x, w_ref[per_half * hf + j, n * BR_WIDTH:(n + 1) * BR_WIDTH, :])
                                      for j in range(per_half)], axis=1)
                up_ref[n, :, cols] = up.astype(up_ref.dtype)
                term = _sigmoid(gates[2 * n + hf][...].astype(F32)) * up
                acc = term if acc is None else acc + term
            m_ref[:, cols] = acc.astype(m_ref.dtype)

    br_spec = pl.BlockSpec((tm, BR_WIDTH), lambda i: (i, 0))
    return _carried(*_pcall(
        body, (a_out, b_out, c_out, wb, *([proj] * 6)), name="merge_fwd", grid=(T // tm,),
        in_specs=[br_spec, br_spec, br_spec,
                  pl.BlockSpec((nq, 3 * BR_WIDTH, wd), lambda i: (0, 0, 0))] + _gate_specs(tm),
        out_specs=(pl.BlockSpec((tm, D_MODEL), lambda i: (i, 0)), pl.BlockSpec((3, tm, D_MODEL), lambda i: (0, i, 0))),
        out_shape=(jax.ShapeDtypeStruct((T, D_MODEL), BF16), jax.ShapeDtypeStruct((3, T, D_MODEL), BF16)),
        semantics=("parallel",), riders=riders), riders)


def _branch_bwd_act(d_ups, wb, riders=()):
    _, T, D = d_ups.shape
    nq, _, wd = wb.shape
    tm = _row_tile(T)

    def body(d_ref, w_ref, o_ref):
        acc = None
        for q in range(nq):
            part = _dot_nt(d_ref[:, q * wd:(q + 1) * wd], w_ref[q])
            acc = part if acc is None else acc + part
        o_ref[...] = acc

    return _carried(*_pcall(
        body, (d_ups, wb), name="d_branch", grid=(3, T // tm),
        in_specs=[pl.BlockSpec((None, tm, D), lambda n, i: (n, i, 0)),
                  pl.BlockSpec((nq, BR_WIDTH, wd), lambda n, i: (0, n, 0))],
        out_specs=pl.BlockSpec((None, tm, BR_WIDTH), lambda n, i: (n, i, 0)),
        out_shape=jax.ShapeDtypeStruct((3, T, BR_WIDTH), F32), semantics=("parallel", "parallel"),
        riders=riders), riders)


def _branch_bwd_weight(name, br, d_ups, n, into=None):
    T = br.shape[0]
    D = d_ups.shape[2]
    wd = D // N_CHIPS
    tt = _row_tile(T, _TN_TOKENS)
    n_br = d_ups.shape[0]

    def body(b_ref, d_ref, *rest):
        o_ref = rest[-1]
        k = pl.program_id(0)
        for q in range(N_CHIPS):
            part = _dot_tn(b_ref[...], d_ref[:, q * wd:(q + 1) * wd])

            @pl.when(k == 0)
            def _():
                o_ref[q] = part

            @pl.when(k > 0)
            def _():
                o_ref[q] += part

    return _pallas(
        body, name=name, grid=(T // tt,),
        in_specs=[pl.BlockSpec((tt, BR_WIDTH), lambda k: (k, 0)),
                  pl.BlockSpec((None, tt, D), lambda k: (n, k, 0))] + ([] if into is None else [HBM_SPEC]),
        out_specs=pl.BlockSpec((N_CHIPS, BR_WIDTH, wd), lambda k: (0, n, 0)),
        out_shape=jax.ShapeDtypeStruct((N_CHIPS, n_br * BR_WIDTH, wd), F32),
        input_output_aliases={} if into is None else {2: 0}, compiler_params=_cp("arbitrary"),
    )(br, d_ups, *(() if into is None else (into,)))


def _merge_bwd(d_merged, ups, proj, riders=()):
    T = d_merged.shape[0]
    tm = _row_tile(T, _MERGE_TM)

    def body(dm_ref, up_ref, *rest):
        gates, (dup_ref, dg_ref) = rest[:6], rest[6:]
        for hf in range(2):
            cols = slice(hf * _GATE_W, (hf + 1) * _GATE_W)
            dm = dm_ref[:, cols]
            for n in range(3):
                gate = _sigmoid(gates[2 * n + hf][...].astype(F32))
                dup_ref[n, :, cols] = (dm * gate).astype(dup_ref.dtype)
                dg_ref[:, n * D_MODEL + hf * _GATE_W:n * D_MODEL + (hf + 1) * _GATE_W] = (
                    dm * up_ref[n, :, cols].astype(F32) * gate * (1.0 - gate)).astype(dg_ref.dtype)

    tile = pl.BlockSpec((tm, D_MODEL), lambda i: (i, 0))
    tile3 = pl.BlockSpec((3, tm, D_MODEL), lambda i: (0, i, 0))
    return _carried(*_pcall(
        body, (d_merged, ups, *([proj] * 6)), name="merge_bwd", grid=(T // tm,),
        in_specs=[tile, tile3] + _gate_specs(tm),
        out_specs=(tile3, pl.BlockSpec((tm, 3 * D_MODEL), lambda i: (i, 0))),
        out_shape=(jax.ShapeDtypeStruct((3, T, D_MODEL), BF16), jax.ShapeDtypeStruct((T, 3 * D_MODEL), BF16)),
        semantics=("parallel",), riders=riders), riders)


_CONV_TF = D_FF // 2
_CONV_TS = 256
_HALO = 16


def _conv_fwd(ab, cw, cb, B, S):
    T = B * S
    ts = _row_tile(S, _CONV_TS)
    tf = _CONV_TF
    nb = D_FF // tf
    tps = S // ts
    hb = ts // _HALO

    steps = (T // ts) * nb
    ring = 3

    def body(ab_hbm, p_ref, w_ref, cb_ref, o_ref, a_buf, b_buf, a_sem, b_sem):
        s = pl.program_id(0) * nb + pl.program_id(1)

        def fetch(t):
            rows, slot = pl.ds((t // nb) * ts, ts), t % ring
            return (pltpu.make_async_copy(ab_hbm.at[rows, pl.ds((t % nb) * tf, tf)], a_buf.at[slot], a_sem.at[slot]),
                    pltpu.make_async_copy(ab_hbm.at[rows, pl.ds((t % nb + nb) * tf, tf)], b_buf.at[slot],
                                          b_sem.at[slot]))

        @pl.when(s == 0)
        def _():
            for t in range(ring - 1):
                for cp in fetch(t):
                    cp.start()

        @pl.when(s + ring - 1 < steps)
        def _():
            for cp in fetch(s + ring - 1):
                cp.start()

        for cp in fetch(s):
            cp.wait()
        start = (pl.program_id(0) % tps) == 0
        a = a_buf[s % ring].astype(F32)
        prev = jnp.where(start, 0.0, p_ref[...].astype(F32))
        ext = jnp.concatenate([prev, a], axis=0)
        a1 = pltpu.roll(ext, 1, 0)[_HALO:, :]
        a2 = pltpu.roll(ext, 2, 0)[_HALO:, :]
        ac = cb_ref[...] + w_ref[0] * a2 + w_ref[1] * a1 + w_ref[2] * a
        o_ref[...] = (ac * _sigmoid(ac) * b_buf[s % ring].astype(F32)).astype(o_ref.dtype)

    assert steps >= ring
    return _pallas(
        body, name="conv_fwd", grid=(T // ts, nb),
        in_specs=[HBM_SPEC,
                  pl.BlockSpec((_HALO, tf), lambda i, j: (jnp.maximum(i * hb - 1, 0), j)),
                  pl.BlockSpec((3, 1, tf), lambda i, j: (0, 0, j)),
                  pl.BlockSpec((1, tf), lambda i, j: (0, j))],
        out_specs=pl.BlockSpec((ts, tf), lambda i, j: (i, j)),
        out_shape=jax.ShapeDtypeStruct((T, D_FF), BF16),
        scratch_shapes=[pltpu.VMEM((ring, ts, tf), BF16), pltpu.VMEM((ring, ts, tf), BF16),
                        pltpu.SemaphoreType.DMA((ring,)), pltpu.SemaphoreType.DMA((ring,))],
        compiler_params=_cp("arbitrary", "arbitrary"),
    )(ab, ab, cw, cb)


def _conv_bwd(ab, d_ff, cw, cb, B, S, riders=()):
    T = B * S
    ts = _row_tile(S, _CONV_TS)
    tf = _CONV_TF
    nb = D_FF // tf
    tps = S // ts
    hb = ts // _HALO
    last_h = T // _HALO - 1
    n_ext = ts + _HALO

    def body(a_ref, ap_ref, an_ref, b_ref, bn_ref, d_ref, dn_ref, w_ref, cb_ref, dab_ref, dw_ref, dcb_ref):
        i = pl.program_id(1)

        @pl.when(i == 0)
        def _():
            dw_ref[...] = jnp.zeros_like(dw_ref)
            dcb_ref[...] = jnp.zeros_like(dcb_ref)

        start = (i % tps) == 0
        end = (i % tps) == tps - 1
        a = a_ref[...].astype(F32)
        ext = jnp.concatenate([jnp.where(start, 0.0, ap_ref[...].astype(F32)), a, an_ref[...].astype(F32)], axis=0)
        r1 = pltpu.roll(ext, 1, 0)[_HALO:, :]
        r2 = pltpu.roll(ext, 2, 0)[_HALO:, :]
        ac = cb_ref[...] + w_ref[0] * r2 + w_ref[1] * r1 + w_ref[2] * ext[_HALO:, :]
        sg = _sigmoid(ac)
        d_e = jnp.concatenate([d_ref[...].astype(F32), jnp.where(end, 0.0, dn_ref[...].astype(F32))], axis=0)
        b_e = jnp.concatenate([b_ref[...].astype(F32), bn_ref[...].astype(F32)], axis=0)
        dab_ref[1] = (d_e[:ts, :] * (ac * sg)[:ts, :]).astype(dab_ref.dtype)
        dac = d_e * b_e * sg * (1.0 + ac * (1.0 - sg))
        u1 = pltpu.roll(dac, n_ext - 1, 0)[:ts, :]
        u2 = pltpu.roll(dac, n_ext - 2, 0)[:ts, :]
        dac0 = dac[:ts, :]
        dab_ref[0] = (w_ref[2] * dac0 + w_ref[1] * u1 + w_ref[0] * u2).astype(dab_ref.dtype)
        dcb_ref[...] += jnp.sum(dac0, axis=0, keepdims=True)
        dw_ref[2] += jnp.sum(dac0 * a, axis=0, keepdims=True)
        dw_ref[1] += jnp.sum(dac0 * r1[:ts, :], axis=0, keepdims=True)
        dw_ref[0] += jnp.sum(dac0 * r2[:ts, :], axis=0, keepdims=True)

    def cur(off):
        return pl.BlockSpec((ts, tf), lambda j, i: (i, j + off))

    def nxt(off):
        return pl.BlockSpec((_HALO, tf), lambda j, i: (jnp.minimum((i + 1) * hb, last_h), j + off))

    return _carried(*_pcall(
        body, (ab, ab, ab, ab, ab, d_ff, d_ff, cw, cb), name="conv_bwd", grid=(nb, T // ts),
        in_specs=[cur(0), pl.BlockSpec((_HALO, tf), lambda j, i: (jnp.maximum(i * hb - 1, 0), j)), nxt(0),
                  cur(nb), nxt(nb), cur(0), nxt(0),
                  pl.BlockSpec((3, 1, tf), lambda j, i: (0, 0, j)), pl.BlockSpec((1, tf), lambda j, i: (0, j))],
        out_specs=(pl.BlockSpec((2, ts, tf), lambda j, i: (0, i, j)), pl.BlockSpec((3, 1, tf), lambda j, i: (0, 0, j)),
                   pl.BlockSpec((1, tf), lambda j, i: (0, j))),
        out_shape=(jax.ShapeDtypeStruct((2, T, D_FF), BF16),
                   jax.ShapeDtypeStruct((3, 1, D_FF), F32), jax.ShapeDtypeStruct((1, D_FF), F32)),
        semantics=("parallel", "arbitrary"), riders=riders), riders)


def _local_step(x, mem, tgt, p, comm, B, S):
    g = {}
    h, slabs = comm.carry(
        "norm1", lambda r: _norm1_and_casts(x, p["norm1_g"], p["cast_beside_norm1"], comm.place, riders=r))
    comm.slabs.update(slabs)
    proj = comm.carry("in_proj", lambda r: _mm_cs("in_proj", h, comm.w("w_in"), BF16, riders=r))
    a_out = _gmlp_fwd(proj, p["ln_v_g"], p["ln_v_b"], p["w_spatial"], p["b_spatial"])
    o_h, b_out, states = comm.carry(
        "hgrn_fwd", lambda r: _hgrn_fwd(proj, p["lb_logits"], p["hgrn_norm_g"], B, S, riders=r))
    memn = _rms_fwd("mem_norm", mem, p["mem_norm_g"])
    kv = _mm_rs("mem_kv", memn, comm.w("w_mem_kv"), F32)
    c_out = _attn_fwd(proj, kv, B, S)
    merged, ups = comm.carry(
        "merge_fwd", lambda r: _merge_fwd(a_out, b_out, c_out, comm.w("w_branch"), proj, riders=r))
    x1, h2 = _proj_res_norm("out_proj_norm2", merged, comm.w("w_out"), x, p["norm2_g"])
    ab = comm.carry("up_proj", lambda r: _mm_cs("up_proj", h2, comm.w("w_up"), BF16, riders=r))
    conv_w = comm.w("conv_w")
    ff = _conv_fwd(ab, conv_w, p["conv_b"], B, S)
    dx2, g["final_g"], loss = _proj_res_loss("down_proj_loss", ff, comm.w("w_down"), x1, tgt, p["final_g"])

    comm.grad("w_down", _mm_tn_rs("g_w_down", ff, dx2, to=D_FF // 2))
    d_ff = comm.carry("d_ff", lambda r: _mm_nt_rs("d_ff", dx2, comm.w("w_down"), BF16, riders=r))
    d_ab, g["conv_w"], g["conv_b"] = comm.carry(
        "conv_bwd", lambda r: _conv_bwd(ab, d_ff, conv_w, p["conv_b"], B, S, riders=r))
    comm.grad("w_up", _mm_tn_cs("g_w_up", h2, d_ab, N_CHIPS, to=512, stacked=True))
    d_x1, g["norm2_g"] = comm.carry("d_h2", lambda r: _mm_nt_cs(
        "d_h2_norm2_bwd", d_ab, comm.w("w_up"), F32, riders=r, stacked=True, norm_bwd=(x1, p["norm2_g"], dx2)))
    comm.grad("w_out", _mm_tn_rs("g_w_out", merged, d_x1, to=512))
    d_merged = _mm_nt_rs("d_merged", d_x1, comm.w("w_out"), F32)
    d_ups, d_gates = comm.carry("merge_bwd", lambda r: _merge_bwd(d_merged, ups, proj, riders=r))

    d_br = comm.carry("d_branch", lambda r: _branch_bwd_act(d_ups, comm.w("w_branch"), riders=r))
    g_branch = None
    for n, br in enumerate((a_out, b_out, c_out)):
        g_branch = _branch_bwd_weight("g_w_branch%d" % n, br, d_ups, n, into=g_branch)
    comm.grad("w_branch", g_branch)

    d_gm, g["w_spatial"], g["b_spatial"], g["ln_v_g"], g["ln_v_b"] = comm.carry(
        "gmlp_bwd", lambda r: _gmlp_bwd(proj, d_br, p["ln_v_g"], p["ln_v_b"], p["w_spatial"], p["b_spatial"],
                                        riders=r))
    d_xq, d_kv = _attn_bwd(proj, kv, d_br, B, S)
    comm.grad("w_mem_kv", _mm_tn_rs("g_w_mem_kv", memn, d_kv, to=512))
    d_memn = _mm_nt_rs("d_memn", d_kv, comm.w("w_mem_kv"), F32)
    _, g["mem_norm_g"] = _rms_bwd("mem_norm_bwd", mem, p["mem_norm_g"], d_memn, None)
    d_proj, g["lb_logits"], g["hgrn_norm_g"] = comm.carry(
        "hgrn_bwd", lambda r: _hgrn_bwd(proj, o_h, states, d_br, p["lb_logits"], p["hgrn_norm_g"],
                                        (d_gm, d_xq, d_gates), B, S, riders=r))
    comm.small_grads([g[n].reshape(_SMALL_SHAPE[n]) for n in _SMALL_EARLY] + [loss])
    comm.grad("w_in", *comm.carry("g_w_in", lambda r: _mm_tn_cs_to_sibling(
        "g_w_in", h, d_proj, N_CHIPS, comm.place, riders=r, send=comm.sends)))
    grad_x, g["norm1_g"] = comm.carry("d_h", lambda r: _mm_nt_cs(
        "d_h_norm1_bwd", d_proj, comm.w("w_in"), F32, riders=r, norm_bwd=(x, p["norm1_g"], d_x1)))
    return loss, grad_x, g


HBM_SPEC = pl.BlockSpec(memory_space=pltpu.HBM)


def _place():
    x, y, c = lax.axis_index("x"), lax.axis_index("y"), lax.axis_index("c")
    other_chips = [(1 - x, y), (x, 1 - y), (1 - x, 1 - y)]
    return x, y, c, other_chips


def _remote(src, dst, send_sem, recv_sem, dev):
    return pltpu.make_async_remote_copy(src_ref=src, dst_ref=dst, send_sem=send_sem, recv_sem=recv_sem,
                                        device_id=dev, device_id_type=MESH_ID)


class _Exchange:
    def __init__(self, operands, out_shape, aliases, scratch, start, finish, mid=None, mid_at=0.5):
        self.operands, self.out_shape, self.aliases, self.scratch = operands, out_shape, aliases, scratch
        self.start, self.finish, self.mid, self.mid_at = start, finish, mid, mid_at


def _run_exchanges(name, exs):
    n_in = [len(ex.operands) for ex in exs]
    n_out = [len(ex.out_shape) for ex in exs]
    n_scr = [len(ex.scratch) for ex in exs]

    def body(*refs):
        ins, outs, scr = refs[:sum(n_in)], refs[sum(n_in):sum(n_in) + sum(n_out)], refs[sum(n_in) + sum(n_out):]
        parts, oi, oo, os_ = [], 0, 0, 0
        for k in range(len(exs)):
            parts.append((ins[oi:oi + n_in[k]], outs[oo:oo + n_out[k]], scr[os_:os_ + n_scr[k]]))
            oi, oo, os_ = oi + n_in[k], oo + n_out[k], os_ + n_scr[k]
        for ex, part in zip(exs, parts):
            ex.start(*part)
        for ex, part in zip(exs, parts):
            if ex.mid is not None:
                ex.mid(*part)
        for ex, part in zip(exs, parts):
            ex.finish(*part)

    aliases, ops, shapes, scratch, oi, oo = {}, [], [], [], 0, 0
    for k, ex in enumerate(exs):
        aliases.update({oi + a: oo + b for a, b in ex.aliases.items()})
        oi, oo = oi + n_in[k], oo + n_out[k]
        ops += list(ex.operands)
        shapes += [pltpu.HBM(s.shape, s.dtype) for s in ex.out_shape]
        scratch += list(ex.scratch)
    res = _pallas(
        body, name=name, in_specs=[HBM_SPEC] * len(ops), out_specs=(HBM_SPEC,) * len(shapes), out_shape=tuple(shapes),
        input_output_aliases=aliases, scratch_shapes=scratch,
    )(*ops)
    out, oo = [], 0
    for k in range(len(exs)):
        out.append(list(res[oo:oo + n_out[k]]))
        oo += n_out[k]
    return out


def _ex_all_gather(slabs, halved, part=(0, 1)):
    n = len(slabs)

    def rows(a, cc):
        if not halved[a]:
            return slice(None)
        pr = slabs[a].shape[1] // part[1]
        return pl.ds(part[0] * pr + cc * (pr // 2), pr // 2)

    def ici(bufs, scr, a, j, chip, c, mine):
        px, py = chip
        x, y, _, _ = _place()
        qs = 2 * x + y if mine else 2 * px + py
        piece = bufs[a].at[qs, rows(a, c)]
        return _remote(piece, piece, scr[0].at[3 * a + j], scr[1].at[3 * a + j], (px, py, c))

    def d2d(bufs, scr, a, j, chip, cc):
        px, py = chip
        x, y, c, _ = _place()
        piece = bufs[a].at[2 * px + py, rows(a, cc)]
        return _remote(piece, piece, scr[2].at[3 * a + j], scr[3].at[3 * a + j], (x, y, 1 - c))

    def start(ins, outs, scr):
        _, _, c, chips = _place()
        for j, chip in enumerate(chips):
            for a in range(n):
                ici(outs, scr, a, j, chip, c, True).start()

    def finish(ins, outs, scr):
        _, _, c, chips = _place()
        for j, chip in enumerate(chips):
            for a in range(n):
                ici(outs, scr, a, j, chip, c, False).wait_recv()
                if halved[a]:
                    d2d(outs, scr, a, j, chip, c).start()
        for j, chip in enumerate(chips):
            for a in range(n):
                if halved[a]:
                    d2d(outs, scr, a, j, chip, 1 - c).wait_recv()
        for j, chip in enumerate(chips):
            for a in range(n):
                ici(outs, scr, a, j, chip, c, True).wait_send()
                if halved[a]:
                    d2d(outs, scr, a, j, chip, c).wait_send()

    return _Exchange(list(slabs), [jax.ShapeDtypeStruct(s.shape, s.dtype) for s in slabs],
                     {a: a for a in range(n)}, [pltpu.SemaphoreType.DMA((3 * n,))] * 4, start, finish)


def _ex_gather_relay(slabs, mid_at=0.5):
    n = len(slabs)

    def rows(a, cc):
        hr = slabs[a].shape[1] // 2
        return pl.ds(cc * hr, hr)

    def peers():
        x, y, c, _ = _place()
        nbr0 = ((x + c) % 2, (y + 1 - c) % 2)
        nbr1 = ((x + 1 - c) % 2, (y + c) % 2)
        return x, y, c, nbr0, nbr1, (1 - x, 1 - y)

    def ici(bufs, scr, a, k, chip, dev, cc):
        _, _, c, _, _, _ = peers()
        piece = bufs[a].at[2 * chip[0] + chip[1], rows(a, cc)]
        return _remote(piece, piece, scr[0].at[3 * a + k], scr[1].at[3 * a + k], (dev[0], dev[1], c))

    def d2d(bufs, scr, a, k, chip, cc):
        x, y, c, _, _, _ = peers()
        piece = bufs[a].at[2 * chip[0] + chip[1], rows(a, cc)]
        return _remote(piece, piece, scr[2].at[3 * a + k], scr[3].at[3 * a + k], (x, y, 1 - c))

    def start(ins, outs, scr):
        x, y, c, nbr0, nbr1, _ = peers()
        for a in range(n):
            ici(outs, scr, a, 0, (x, y), nbr0, c).start()
            ici(outs, scr, a, 1, (x, y), nbr1, c).start()

    def mid(ins, outs, scr):
        x, y, c, nbr0, nbr1, diag = peers()
        for a in range(n):
            ici(outs, scr, a, 0, nbr0, nbr0, c).wait_recv()
            ici(outs, scr, a, 2, nbr0, nbr1, c).start()
            d2d(outs, scr, a, 0, nbr0, c).start()
        for a in range(n):
            ici(outs, scr, a, 1, nbr1, nbr1, c).wait_recv()
            d2d(outs, scr, a, 1, nbr1, c).start()

    def finish(ins, outs, scr):
        x, y, c, nbr0, nbr1, diag = peers()
        for a in range(n):
            ici(outs, scr, a, 2, diag, nbr1, c).wait_recv()
            d2d(outs, scr, a, 2, diag, c).start()
        for a in range(n):
            d2d(outs, scr, a, 0, nbr1, 1 - c).wait_recv()
            d2d(outs, scr, a, 1, nbr0, 1 - c).wait_recv()
            d2d(outs, scr, a, 2, diag, 1 - c).wait_recv()
        for a in range(n):
            ici(outs, scr, a, 0, (x, y), nbr0, c).wait_send()
            ici(outs, scr, a, 1, (x, y), nbr1, c).wait_send()
            ici(outs, scr, a, 2, nbr0, nbr1, c).wait_send()
            d2d(outs, scr, a, 0, nbr0, c).wait_send()
            d2d(outs, scr, a, 1, nbr1, c).wait_send()
            d2d(outs, scr, a, 2, diag, c).wait_send()

    return _Exchange(list(slabs), [jax.ShapeDtypeStruct(s.shape, s.dtype) for s in slabs],
                     {a: a for a in range(n)}, [pltpu.SemaphoreType.DMA((3 * n,))] * 4, start, finish, mid, mid_at)


def _ex_to_sibling(grads):
    n = len(grads)

    def copy(ins, outs, scr, a):
        x, y, c, _ = _place()
        hr = grads[a].shape[1] // 2
        return _remote(ins[a].at[:, pl.ds((1 - c) * hr, hr), :], outs[a], scr[0].at[a], scr[1].at[a], (x, y, 1 - c))

    def start(ins, outs, scr):
        for a in range(n):
            copy(ins, outs, scr, a).start()

    def finish(ins, outs, scr):
        for a in range(n):
            copy(ins, outs, scr, a).wait()

    out_shape = [jax.ShapeDtypeStruct((g.shape[0], g.shape[1] // 2, g.shape[2]), g.dtype) for g in grads]
    return _Exchange(list(grads), out_shape, {}, [pltpu.SemaphoreType.DMA((n,))] * 2, start, finish)


def _ex_to_owner(parts, part=(0, 1), landing=None):
    n = len(parts)

    def copy(ins, outs, scr, a, j, chip):
        _, _, c, _ = _place()
        px, py = chip
        pr = parts[a].shape[1] // part[1]
        rows = pl.ds(part[0] * pr, pr)
        return _remote(ins[a].at[2 * px + py, rows], outs[a].at[j, rows], scr[0].at[3 * a + j],
                       scr[1].at[3 * a + j], (px, py, c))

    def start(ins, outs, scr):
        for j, chip in enumerate(_place()[3]):
            for a in range(n):
                copy(ins, outs, scr, a, j, chip).start()

    def finish(ins, outs, scr):
        for j, chip in enumerate(_place()[3]):
            for a in range(n):
                copy(ins, outs, scr, a, j, chip).wait()

    out_shape = [jax.ShapeDtypeStruct((3,) + p.shape[1:], p.dtype) for p in parts]
    operands, aliases = list(parts), {}
    if landing is not None:
        operands, aliases = operands + list(landing), {n + a: a for a in range(n)}
    return _Exchange(operands, out_shape, aliases, [pltpu.SemaphoreType.DMA((3 * n,))] * 2, start, finish)


def _ex_share_halves(bufs):
    n = len(bufs)

    def copy(outs, scr, a, cc):
        x, y, c, _ = _place()
        hr = bufs[a].shape[0] // 2
        piece = outs[a].at[pl.ds(cc * hr, hr), :]
        return _remote(piece, piece, scr[0].at[a], scr[1].at[a], (x, y, 1 - c))

    def start(ins, outs, scr):
        c = _place()[2]
        for a in range(n):
            copy(outs, scr, a, c).start()

    def finish(ins, outs, scr):
        c = _place()[2]
        for a in range(n):
            copy(outs, scr, a, c).wait_send()
            copy(outs, scr, a, 1 - c).wait_recv()

    return _Exchange(list(bufs), [jax.ShapeDtypeStruct(b.shape, b.dtype) for b in bufs], {a: a for a in range(n)},
                     [pltpu.SemaphoreType.DMA((n,))] * 2, start, finish)


def _ex_gather_small(arrs):
    n = len(arrs)

    def peer_of(m):
        x, y, c, _ = _place()
        return (1 - x if m & 4 else x, 1 - y if m & 2 else y, 1 - c if m & 1 else c)

    def own(ins, outs, scr, a):
        x, y, c, _ = _place()
        return pltpu.make_async_copy(ins[a], outs[a].at[4 * x + 2 * y + c], scr[2].at[a])

    def start(ins, outs, scr):
        x, y, c, _ = _place()
        for a in range(n):
            own(ins, outs, scr, a).start()
        for m in range(1, N_DEV):
            for a in range(n):
                k = (N_DEV - 1) * a + m - 1
                _remote(ins[a], outs[a].at[4 * x + 2 * y + c], scr[0].at[k], scr[1].at[k], peer_of(m)).start()

    def finish(ins, outs, scr):
        for a in range(n):
            own(ins, outs, scr, a).wait()
        for m in range(1, N_DEV):
            px, py, pc = peer_of(m)
            for a in range(n):
                k = (N_DEV - 1) * a + m - 1
                slot = outs[a].at[4 * px + 2 * py + pc]
                cp = _remote(ins[a], slot, scr[0].at[k], scr[1].at[k], (px, py, pc))
                cp.wait_send()
                cp.wait_recv()

    out_shape = [jax.ShapeDtypeStruct((N_DEV,) + a.shape, a.dtype) for a in arrs]
    return _Exchange(list(arrs), out_shape, {},
                     [pltpu.SemaphoreType.DMA(((N_DEV - 1) * n,))] * 2 + [pltpu.SemaphoreType.DMA((n,))], start, finish)


def _div_tile(n, want):
    best = None
    for t in range(8, min(n, want) + 1, 8):
        if n % t == 0:
            best = t
    assert best is not None, n
    return best


def _cast_into_slab(name, w, place, dtype):
    r, cc = w.shape
    tr = r if r * cc <= 128 * 1024 else _div_tile(r, 256)

    def body(s_ref, w_ref, o_ref):
        o_ref[...] = w_ref[...].astype(o_ref.dtype)

    return _pallas(
        body, name=name,
        grid_spec=pltpu.PrefetchScalarGridSpec(
            num_scalar_prefetch=1, grid=(r // tr,),
            in_specs=[pl.BlockSpec((tr, cc), lambda i, s: (i, 0))],
            out_specs=pl.BlockSpec((None, tr, cc), lambda i, s: (s[0], i, 0))),
        out_shape=jax.ShapeDtypeStruct((N_CHIPS, r, cc), dtype), compiler_params=_cp("parallel"),
    )(place, w)


def _add_half(name, g, rcv, place):
    nq, r, cc = g.shape
    hr = r // 2

    def body(s_ref, g_ref, r_ref, o_ref):
        o_ref[...] = (g_ref[...] + r_ref[...]).astype(o_ref.dtype)

    spec = pl.BlockSpec((None, hr, cc), lambda i, s: (i, 0, 0))
    return _pallas(
        body, name=name,
        grid_spec=pltpu.PrefetchScalarGridSpec(
            num_scalar_prefetch=1, grid=(nq,),
            in_specs=[pl.BlockSpec((None, hr, cc), lambda i, s: (i, s[1], 0)), spec], out_specs=spec),
        out_shape=jax.ShapeDtypeStruct((nq, hr, cc), BF16), compiler_params=_cp("parallel"),
    )(place, g, rcv)


def _sum_owner(name, part, rcv, place):
    _, hr, cc = part.shape
    tr = _div_tile(hr, 128)
    nb = hr // tr

    def body(s_ref, p_ref, r_ref, o_ref):
        o_ref[...] = ((p_ref[...].astype(F32) + r_ref[0].astype(F32)) + r_ref[1].astype(F32)) + r_ref[2].astype(F32)

    return _pallas(
        body, name=name,
        grid_spec=pltpu.PrefetchScalarGridSpec(
            num_scalar_prefetch=1, grid=(nb,),
            in_specs=[pl.BlockSpec((None, tr, cc), lambda i, s: (s[0], i, 0)),
                      pl.BlockSpec((3, tr, cc), lambda i, s: (0, i, 0))],
            out_specs=pl.BlockSpec((tr, cc), lambda i, s: (s[1] * nb + i, 0))),
        out_shape=jax.ShapeDtypeStruct((2 * hr, cc), F32), compiler_params=_cp("parallel"),
    )(place, part, rcv)


def _sum_small(gathered, local, place):
    n = len(gathered)

    def body(s_ref, *refs):
        g_refs, l_refs, o_refs = refs[:n], refs[n:2 * n], refs[2 * n:]
        me = s_ref[2]
        for g_ref, l_ref, o_ref in zip(g_refs, l_refs, o_refs):
            acc = None
            for d in range(N_DEV):
                term = jnp.where(me == d, l_ref[...], g_ref[d])
                acc = term if acc is None else acc + term
            o_ref[...] = acc

    def whole(shape):
        return pl.BlockSpec(shape, lambda i, s, nd=len(shape): (0,) * nd)

    return _pallas(
        body, name="sum_small",
        grid_spec=pltpu.PrefetchScalarGridSpec(
            num_scalar_prefetch=1, grid=(1,),
            in_specs=[whole(g.shape) for g in gathered] + [whole(a.shape) for a in local],
            out_specs=tuple(whole(a.shape) for a in local)),
        out_shape=tuple(jax.ShapeDtypeStruct(a.shape, a.dtype) for a in local), compiler_params=_cp("arbitrary"),
    )(place, *gathered, *local)


def _adamw(name, w, g, m, v):
    r, cc = w.shape
    tr = r if r * cc <= 128 * 1024 else _div_tile(r, 256)

    def body(w_ref, g_ref, m_ref, v_ref, d_ref, mo_ref, vo_ref, go_ref):
        gv = g_ref[...]
        go_ref[...] = gv
        mn = ADAM_B1 * m_ref[...] + (1.0 - ADAM_B1) * gv
        vn = ADAM_B2 * v_ref[...] + (1.0 - ADAM_B2) * (gv * gv)
        m_hat = mn / (1.0 - ADAM_B1 ** ADAM_STEP)
        v_hat = vn / (1.0 - ADAM_B2 ** ADAM_STEP)
        d_ref[...] = -ADAM_LR * (m_hat / (jnp.sqrt(v_hat) + ADAM_EPS) + ADAM_WD * w_ref[...])
        mo_ref[...] = mn
        vo_ref[...] = vn

    spec = pl.BlockSpec((tr, cc), lambda i: (i, 0))
    sd = jax.ShapeDtypeStruct((r, cc), F32)
    return _pallas(
        body, name=name, grid=(r // tr,), in_specs=[spec] * 4, out_specs=(spec,) * 4, out_shape=(sd,) * 4,
        compiler_params=_cp("parallel"),
    )(w, g, m, v)


_BIG = ("w_in", "w_up", "w_branch", "w_mem_kv", "w_out", "w_down")
_BIG_SHARD_SHAPE = {"w_in": (1024, 1664), "w_up": (1024, 1408), "w_branch": (1536, 256),
                    "w_mem_kv": (256, 1024), "w_out": (256, 1024), "w_down": (704, 1024)}
_SMALL_SHAPE = {"norm1_g": (1, D_MODEL), "ln_v_g": (1, GM_WIDTH), "ln_v_b": (1, GM_WIDTH),
                "w_spatial": (GM_GROUPS * GM_CHUNK, GM_CHUNK), "b_spatial": (GM_GROUPS, GM_CHUNK),
                "lb_logits": (2, HG_HEADS * HG_DIM), "hgrn_norm_g": (1, HG_DIM), "mem_norm_g": (1, D_MODEL),
                "norm2_g": (1, D_MODEL), "conv_w": (3, D_FF), "conv_b": (1, D_FF), "final_g": (1, D_MODEL)}
_SMALL_EARLY = tuple(n for n in _SMALL_SHAPE if n != "norm1_g")
_PARAM_ORDER = ("norm1_g", "w_in", "ln_v_g", "ln_v_b", "w_spatial", "b_spatial", "lb_logits", "hgrn_norm_g",
                "mem_norm_g", "w_mem_kv", "w_branch", "w_out", "norm2_g", "w_up", "conv_w", "conv_b", "w_down",
                "final_g")


def _adamw_small(ws, gs, ms, vs):
    n = len(ws)

    def body(*refs):
        w_refs, g_refs, m_refs, v_refs = refs[:n], refs[n:2 * n], refs[2 * n:3 * n], refs[3 * n:4 * n]
        d_refs, mo_refs, vo_refs = refs[4 * n:5 * n], refs[5 * n:6 * n], refs[6 * n:]
        for k in range(n):
            gv = g_refs[k][...]
            mn = ADAM_B1 * m_refs[k][...] + (1.0 - ADAM_B1) * gv
            vn = ADAM_B2 * v_refs[k][...] + (1.0 - ADAM_B2) * (gv * gv)
            m_hat = mn / (1.0 - ADAM_B1 ** ADAM_STEP)
            v_hat = vn / (1.0 - ADAM_B2 ** ADAM_STEP)
            d_refs[k][...] = -ADAM_LR * (m_hat / (jnp.sqrt(v_hat) + ADAM_EPS) + ADAM_WD * w_refs[k][...])
            mo_refs[k][...] = mn
            vo_refs[k][...] = vn

    specs = [pl.BlockSpec(a.shape, lambda i, nd=a.ndim: (0,) * nd) for a in ws]
    shapes = tuple(jax.ShapeDtypeStruct(a.shape, F32) for a in ws)
    res = _pallas(
        body, name="adamw_small", grid=(1,), in_specs=specs * 4, out_specs=tuple(specs * 3), out_shape=shapes * 3,
        compiler_params=_cp("arbitrary"),
    )(*ws, *gs, *ms, *vs)
    return res[:n], res[n:2 * n], res[2 * n:]


class _Comm:
    _ROW_SHARDED = ("w_mem_kv", "w_out", "w_down")

    def __init__(self, slabs, place):
        self.slabs, self.place = slabs, place
        self.full, self.raw, self.parts, self.landing, self.bufs, self.done = {}, {}, {}, {}, {}, {}

    def w(self, name):
        a = self.full[name]
        if name in self._ROW_SHARDED:
            return a.reshape(-1, a.shape[-1])
        if name == "conv_w":
            return jnp.transpose(a, (1, 0, 2)).reshape(3, 1, D_FF)
        return a

    sends = True

    def grad(self, name, arr, from_sibling=None):
        self.raw[name] = arr.reshape((N_CHIPS, -1, arr.shape[-1]))
        if from_sibling is not None:
            self.parts[name] = _add_half("rs_add_" + name, self.raw[name], from_sibling, self.place)

    def small_grads(self, arrays):
        self.small_local = list(arrays)

    def carry(self, tag, call):
        plan = self._plan(tag)
        if not plan:
            return call(())
        out, carried = call([ex for ex, _ in plan])
        for (_, deliver), res in zip(plan, carried):
            deliver(res)
        return out

    def finish(self, last_small):
        ex, deliver = self._share(["w_out", "w_branch", "w_mem_kv", "w_in"])
        shared, small = _run_exchanges("share_and_gather_last", [ex, _ex_gather_small(last_small)])
        deliver(shared)
        return self.done, self.small_local + list(last_small), self.small_everyone + small

    def _plan(self, tag):
        if tag == "norm1":
            def deliver(res):
                self.full["w_in"] = res[0]

            return [(_ex_gather_relay([self.slabs["w_in"]]), deliver)]
        if tag == "in_proj":
            return [self._gather_relay(["w_branch", "w_out", "w_mem_kv", "w_down"], 0.6), self._gather(["conv_w"])]
        if tag == "hgrn_fwd":
            return [self._gather_relay(["w_up"], 0.8)]
        if tag == "d_h2":
            return [self._to_sibling(["w_down", "w_up"])]
        if tag == "merge_bwd":
            return [self._to_owner(["w_up"], (0, 2))]
        if tag == "hgrn_bwd":
            return [self._to_owner(["w_down"]), self._to_owner(["w_up"], (1, 2)),
                    self._to_sibling(["w_out", "w_branch", "w_mem_kv"])]
        if tag == "g_w_in":
            def keep(res):
                self.small_everyone = res

            return [self._to_owner(["w_out", "w_branch", "w_mem_kv"]), (_ex_gather_small(self.small_local), keep)]
        if tag == "d_h":
            return [self._to_owner(["w_in"]), self._share(["w_down", "w_up"])]
        return []

    def _gather(self, names, part=(0, 1)):
        def deliver(res):
            self.slabs.update(zip(names, res))
            self.full.update(zip(names, res))

        return _ex_all_gather([self.slabs[n] for n in names], [n != "conv_w" for n in names], part), deliver

    def _gather_relay(self, names, mid_at):
        return _ex_gather_relay([self.slabs[n] for n in names], mid_at), lambda res: self.full.update(zip(names, res))

    def _to_sibling(self, names):
        def deliver(res):
            for n, r in zip(names, res):
                self.parts[n] = _add_half("rs_add_" + n, self.raw[n], r, self.place)

        return _ex_to_sibling([self.raw[n] for n in names]), deliver

    def _to_owner(self, names, part=(0, 1)):
        def deliver(res):
            for n, r in zip(names, res):
                if part[0] + 1 < part[1]:
                    self.landing[n] = r
                else:
                    self.bufs[n] = _sum_owner("rs_sum_" + n, self.parts[n], r, self.place)

        landing = [self.landing[n] for n in names] if part[0] else None
        return _ex_to_owner([self.parts[n] for n in names], part, landing), deliver

    def _share(self, names):
        return _ex_share_halves([self.bufs[n] for n in names]), lambda res: self.done.update(zip(names, res))


def kernel(x, mem, norm1_g, w_in, ln_v_g, ln_v_b, w_spatial, b_spatial, lb_logits, hgrn_norm_g, mem_norm_g, w_mem_kv, w_branch, w_out, norm2_g, w_up, conv_w, conv_b, w_down, final_g, loss_target, m_norm1_g, m_w_in, m_ln_v_g, m_ln_v_b, m_w_spatial, m_b_spatial, m_lb_logits, m_hgrn_norm_g, m_mem_norm_g, m_w_mem_kv, m_w_branch, m_w_out, m_norm2_g, m_w_up, m_conv_w, m_conv_b, m_w_down, m_final_g, v_norm1_g, v_w_in, v_ln_v_g, v_ln_v_b, v_w_spatial, v_b_spatial, v_lb_logits, v_hgrn_norm_g, v_mem_norm_g, v_w_mem_kv, v_w_branch, v_w_out, v_norm2_g, v_w_up, v_conv_w, v_conv_b, v_w_down, v_final_g):
    w = dict(norm1_g=norm1_g, w_in=w_in, ln_v_g=ln_v_g, ln_v_b=ln_v_b, w_spatial=w_spatial, b_spatial=b_spatial,
             lb_logits=lb_logits, hgrn_norm_g=hgrn_norm_g, mem_norm_g=mem_norm_g, w_mem_kv=w_mem_kv,
             w_branch=w_branch, w_out=w_out, norm2_g=norm2_g, w_up=w_up, conv_w=conv_w, conv_b=conv_b,
             w_down=w_down, final_g=final_g)
    mom = dict(norm1_g=m_norm1_g, w_in=m_w_in, ln_v_g=m_ln_v_g, ln_v_b=m_ln_v_b, w_spatial=m_w_spatial,
               b_spatial=m_b_spatial, lb_logits=m_lb_logits, hgrn_norm_g=m_hgrn_norm_g, mem_norm_g=m_mem_norm_g,
               w_mem_kv=m_w_mem_kv, w_branch=m_w_branch, w_out=m_w_out, norm2_g=m_norm2_g, w_up=m_w_up,
               conv_w=m_conv_w, conv_b=m_conv_b, w_down=m_w_down, final_g=m_final_g)
    var = dict(norm1_g=v_norm1_g, w_in=v_w_in, ln_v_g=v_ln_v_g, ln_v_b=v_ln_v_b, w_spatial=v_w_spatial,
               b_spatial=v_b_spatial, lb_logits=v_lb_logits, hgrn_norm_g=v_hgrn_norm_g, mem_norm_g=v_mem_norm_g,
               w_mem_kv=v_w_mem_kv, w_branch=v_w_branch, w_out=v_w_out, norm2_g=v_norm2_g, w_up=v_w_up,
               conv_w=v_conv_w, conv_b=v_conv_b, w_down=v_w_down, final_g=v_final_g)
    B, S, D = x.shape
    T = B * S
    ci = lax.axis_index("c")
    q = 2 * lax.axis_index("x") + lax.axis_index("y")
    place = jnp.stack([q, ci, 2 * q + ci]).astype(jnp.int32)

    shards = {n: w[n].reshape(_BIG_SHARD_SHAPE[n]) for n in _BIG}
    slabs = {"w_in": _cast_into_slab("slab_w_in", shards.pop("w_in"), place, BF16),
             "conv_w": _cast_into_slab("slab_conv_w", conv_w[0], place, F32)}
    comm = _Comm(slabs, place)
    p = dict(
        cast_beside_norm1=shards,
        norm1_g=norm1_g, ln_v_g=ln_v_g, ln_v_b=ln_v_b, w_spatial=w_spatial[0],
        b_spatial=b_spatial.reshape(GM_GROUPS, GM_CHUNK, 1), lb_logits=lb_logits, hgrn_norm_g=hgrn_norm_g,
        mem_norm_g=mem_norm_g, norm2_g=norm2_g, conv_b=conv_b, final_g=final_g.reshape(1, D))

    loss, grad_x, g = _local_step(x.reshape(T, D), mem.reshape(B * MEM_LEN, D), loss_target.reshape(T, D), p, comm,
                                  B, S)

    shard_grads, local_small, everyone = comm.finish([g["norm1_g"]])
    summed = _sum_small(everyone, local_small, place)
    small_names = list(_SMALL_EARLY) + ["norm1_g"]
    total = dict(zip(_SMALL_EARLY, summed))
    loss_total, total["norm1_g"] = summed[len(_SMALL_EARLY)][0, 0], summed[-1]

    grads, delta, new_m, new_v = {}, {}, {}, {}
    for n in _BIG:
        shp = _BIG_SHARD_SHAPE[n]
        delta[n], new_m[n], new_v[n], grads[n] = _adamw("adamw_" + n, w[n].reshape(shp), shard_grads[n],
                                                        mom[n].reshape(shp), var[n].reshape(shp))
    cw_shard = D_FF // N_CHIPS
    total["conv_w"] = lax.dynamic_slice(total["conv_w"], (0, q * cw_shard), (3, cw_shard)).reshape(3, 1, cw_shard)

    def flat2d(d, n):
        return d[n].reshape(total[n].shape)

    upd = _adamw_small([flat2d(w, n) for n in small_names], [total[n] for n in small_names],
                       [flat2d(mom, n) for n in small_names], [flat2d(var, n) for n in small_names])
    for k, n in enumerate(small_names):
        grads[n], delta[n], new_m[n], new_v[n] = total[n], upd[0][k], upd[1][k], upd[2][k]

    def shaped(d):
        return [d[n].reshape(w[n].shape) for n in _PARAM_ORDER]

    return (loss_total, grad_x.reshape(B, S, D), *shaped(grads), *shaped(delta), *shaped(new_m), *shaped(new_v))
```

```python
import functools
import math

import jax
import jax.numpy as jnp
from jax import lax
from jax.experimental import pallas as pl
from jax.experimental.pallas import tpu as pltpu

F32 = jnp.float32
BF16 = jnp.bfloat16
EPS = 1e-6

D_MODEL = 1024
MEM_LEN = 256
GM_WIDTH = 512
GM_CHUNK = 128
GM_GROUPS = 4
HG_HEADS = 4
HG_DIM = 128
HG_CHUNK = 64
XA_HEADS = 4
XA_DIM = 128
BR_WIDTH = 512
D_FF = 2816
IN_WIDTH = 6656
N_CHIPS = 4
N_DEV = 8

ADAM_LR = 0.001
ADAM_B1 = 0.9
ADAM_B2 = 0.999
ADAM_EPS = 1e-08
ADAM_WD = 0.01
ADAM_STEP = 10

COL_ZU, COL_ZV, COL_HQ, COL_HF, COL_HI, COL_HG, COL_XQ = 0, 1, 2, 3, 4, 5, 6
COL_GATE0 = 3584

VMEM_LIMIT_BYTES = 48 * 1024 * 1024
MESH_ID = pl.DeviceIdType.MESH


def _cp(*sem):
    return pltpu.CompilerParams(dimension_semantics=sem, vmem_limit_bytes=VMEM_LIMIT_BYTES)


def _pallas(body, *, out_shape, **kw):
    def pin(s):
        return pltpu.HBM(s.shape, s.dtype) if isinstance(s, jax.ShapeDtypeStruct) else s

    out_shape = tuple(pin(s) for s in out_shape) if isinstance(out_shape, (tuple, list)) else pin(out_shape)
    call = pl.pallas_call(body, out_shape=out_shape, **kw)

    def run(*operands):
        return call(*[pltpu.with_memory_space_constraint(o, pltpu.HBM) if jnp.issubdtype(o.dtype, jnp.floating)
                      else o for o in operands])

    return run


def _dot(a, b):
    return lax.dot_general(a.astype(BF16), b.astype(BF16), (((1,), (0,)), ((), ())), preferred_element_type=F32)


def _dot_nt(a, b):
    return lax.dot_general(a.astype(BF16), b.astype(BF16), (((1,), (1,)), ((), ())), preferred_element_type=F32)


def _dot_tn(a, b):
    return lax.dot_general(a.astype(BF16), b.astype(BF16), (((0,), (0,)), ((), ())), preferred_element_type=F32)


def _dot_01(mask01, x):
    hi = x.astype(BF16)
    r1 = x - hi.astype(F32)
    mid = r1.astype(BF16)
    lo = (r1 - mid.astype(F32)).astype(BF16)
    m = mask01.astype(BF16)
    dn = (((1,), (0,)), ((), ()))
    return (lax.dot_general(m, hi, dn, preferred_element_type=F32)
            + lax.dot_general(m, mid, dn, preferred_element_type=F32)
            + lax.dot_general(m, lo, dn, preferred_element_type=F32))


def _sigmoid(z):
    return 1.0 / (1.0 + jnp.exp(-z))


_GELU_C = math.sqrt(2.0 / math.pi)


def _gelu_and_grad(z):
    inner = _GELU_C * (z + 0.044715 * z * z * z)
    t = jnp.tanh(inner)
    val = 0.5 * z * (1.0 + t)
    grad = 0.5 * (1.0 + t) + 0.5 * z * (1.0 - t * t) * _GELU_C * (1.0 + 3.0 * 0.044715 * z * z)
    return val, grad


def _row_tile(n, want=512):
    t = min(want, n)
    assert n % t == 0
    return t


def _pcall(body, operands, *, name, grid, in_specs, out_specs, out_shape, scratch_shapes=(), semantics, riders=(),
           prefetch=None):
    single = not isinstance(out_shape, (tuple, list))
    out_specs = (out_specs,) if single else tuple(out_specs)
    out_shape = (out_shape,) if single else tuple(out_shape)
    n_pre = 0 if prefetch is None else 1

    def call(fn, ins_, outs_, shapes_, scr_, ops, sem, aliases):
        if prefetch is None:
            return _pallas(fn, name=name, grid=grid, in_specs=ins_, out_specs=outs_, out_shape=shapes_,
                           scratch_shapes=scr_, input_output_aliases=aliases, compiler_params=_cp(*sem))(*ops)
        spec = pltpu.PrefetchScalarGridSpec(num_scalar_prefetch=1, grid=grid, in_specs=ins_, out_specs=outs_,
                                            scratch_shapes=scr_)
        return _pallas(fn, name=name, grid_spec=spec, out_shape=shapes_, input_output_aliases=aliases,
                       compiler_params=_cp(*sem))(prefetch, *ops)

    if not riders:
        res = call(body, list(in_specs), out_specs, out_shape, list(scratch_shapes), operands, semantics, {})
        return (res[0] if single else res), []
    n_in, n_out, n_scr = len(in_specs), len(out_shape), len(scratch_shapes)
    ex_in = [len(ex.operands) for ex in riders]
    ex_out = [len(ex.out_shape) for ex in riders]
    ex_scr = [len(ex.scratch) for ex in riders]
    tot_in, tot_out = n_in + sum(ex_in), n_out + sum(ex_out)

    def wrapped(*refs):
        pre, refs = refs[:n_pre], refs[n_pre:]
        ins, outs, scr = refs[:tot_in], refs[tot_in:tot_in + tot_out], refs[tot_in + tot_out:]
        ids = [pl.program_id(d) for d in range(len(grid))]
        first = functools.reduce(lambda p, t: p & t, [i == 0 for i in ids])
        last = functools.reduce(lambda p, t: p & t, [i == n - 1 for i, n in zip(ids, grid)])
        parts, oi, oo, os_ = [], n_in, n_out, n_scr
        for k in range(len(riders)):
            parts.append((ins[oi:oi + ex_in[k]], outs[oo:oo + ex_out[k]], scr[os_:os_ + ex_scr[k]]))
            oi, oo, os_ = oi + ex_in[k], oo + ex_out[k], os_ + ex_scr[k]

        @pl.when(first)
        def _():
            for ex, part in zip(riders, parts):
                ex.start(*part)

        step, total = 0, 1
        for i, n in zip(ids, grid):
            step, total = step * n + i, total * n
        for ex, part in zip(riders, parts):
            if ex.mid is not None:
                @pl.when(step == min(total - 1, int(total * ex.mid_at)))
                def _(ex=ex, part=part):
                    ex.mid(*part)

        body(*pre, *ins[:n_in], *outs[:n_out], *scr[:n_scr])

        @pl.when(last)
        def _():
            for ex, part in zip(riders, parts):
                ex.finish(*part)

    aliases, oi, oo = {}, n_in, n_out
    all_ops, all_shapes, all_scr = list(operands), list(out_shape), list(scratch_shapes)
    for k, ex in enumerate(riders):
        aliases.update({n_pre + oi + a: oo + b for a, b in ex.aliases.items()})
        oi, oo = oi + ex_in[k], oo + ex_out[k]
        all_ops += list(ex.operands)
        all_shapes += [pltpu.HBM(s.shape, s.dtype) for s in ex.out_shape]
        all_scr += list(ex.scratch)
    res = call(wrapped, list(in_specs) + [HBM_SPEC] * sum(ex_in), out_specs + (HBM_SPEC,) * sum(ex_out),
               tuple(all_shapes), all_scr, all_ops, ["arbitrary"] * len(grid), aliases)
    own = res[0] if single else tuple(res[:n_out])
    carried, oo = [], n_out
    for k in range(len(riders)):
        carried.append(list(res[oo:oo + ex_out[k]]))
        oo += ex_out[k]
    return own, carried


def _carried(out, carried, riders):
    return (out, carried) if riders else out


def _matmul(name, operands, *, grid, in_specs, o_spec, out_shape, out_dtype, dims, riders=()):
    nk = grid[2]
    assert nk == 1 or out_dtype == F32

    def body(a_ref, b_ref, o_ref):
        part = lax.dot_general(a_ref[...].astype(BF16), b_ref[...].astype(BF16), (dims, ((), ())),
                               preferred_element_type=F32)
        if nk == 1:
            o_ref[...] = part.astype(o_ref.dtype)
        else:
            k = pl.program_id(2)

            @pl.when(k == 0)
            def _():
                o_ref[...] = part

            @pl.when(k > 0)
            def _():
                o_ref[...] += part

    out, carried = _pcall(body, operands, name=name, grid=grid, in_specs=in_specs, out_specs=o_spec,
                          out_shape=jax.ShapeDtypeStruct(out_shape, out_dtype),
                          semantics=("parallel", "parallel", "arbitrary"), riders=riders)
    return (out, carried) if riders else out


NN = ((1,), (0,))
NT = ((1,), (1,))
TN = ((0,), (0,))
_TN_TOKENS = 4096


def _mm_cs(name, a, w, out_dtype, riders=()):
    M, K = a.shape
    nq, _, wd = w.shape
    tm = _row_tile(M)
    return _matmul(name, (a, w), grid=(nq, M // tm, 1),
                   in_specs=[pl.BlockSpec((tm, K), lambda j, i, k: (i, 0)),
                             pl.BlockSpec((None, K, wd), lambda j, i, k: (j, 0, 0))],
                   o_spec=pl.BlockSpec((tm, wd), lambda j, i, k: (i, j)),
                   out_shape=(M, nq * wd), out_dtype=out_dtype, dims=NN, riders=riders)


def _mm_rs(name, a, w, out_dtype):
    M, K = a.shape
    N = w.shape[1]
    tm = _row_tile(M)
    return _matmul(name, (a, w), grid=(M // tm, 1, 1),
                   in_specs=[pl.BlockSpec((tm, K), lambda i, j, k: (i, 0)), pl.BlockSpec((K, N), lambda i, j, k: (0, 0))],
                   o_spec=pl.BlockSpec((tm, N), lambda i, j, k: (i, 0)),
                   out_shape=(M, N), out_dtype=out_dtype, dims=NN)


def _mm_nt_rs(name, g, w, out_dtype, riders=()):
    M, N = g.shape
    K = w.shape[0]
    to = K
    tm = _row_tile(M)
    return _matmul(name, (g, w), grid=(M // tm, K // to, 1),
                   in_specs=[pl.BlockSpec((tm, N), lambda i, j, k: (i, 0)),
                             pl.BlockSpec((to, N), lambda i, j, k: (j, 0))],
                   o_spec=pl.BlockSpec((tm, to), lambda i, j, k: (i, j)),
                   out_shape=(M, K), out_dtype=out_dtype, dims=NT, riders=riders)


def _mm_nt_cs(name, g, w, out_dtype, riders=(), stacked=False, norm_bwd=None):
    M = g.shape[-2]
    nq, K, wd = w.shape
    tm = _row_tile(M, 256)

    def product(g_ref, w_ref):
        acc = None
        for q in range(nq):
            gq = g_ref[q // 2, :, (q % 2) * wd:(q % 2 + 1) * wd] if stacked else g_ref[:, q * wd:(q + 1) * wd]
            part = _dot_nt(gq, w_ref[q])
            acc = part if acc is None else acc + part
        return acc

    def body(g_ref, w_ref, o_ref):
        o_ref[...] = product(g_ref, w_ref).astype(o_ref.dtype)

    def body_norm(g_ref, w_ref, x_ref, gain_ref, dr_ref, dx_ref, dg_ref):
        @pl.when(pl.program_id(0) == 0)
        def _():
            dg_ref[...] = jnp.zeros_like(dg_ref)

        dx, dg = _rms_bwd_rows(x_ref[...], gain_ref[...], product(g_ref, w_ref))
        dg_ref[...] += dg
        dx_ref[...] = dx + dr_ref[...]

    g_spec = (pl.BlockSpec((2, tm, 2 * wd), lambda i: (0, i, 0)) if stacked
              else pl.BlockSpec((tm, nq * wd), lambda i: (i, 0)))
    w_spec = pl.BlockSpec((nq, K, wd), lambda i: (0, 0, 0))
    row = pl.BlockSpec((tm, K), lambda i: (i, 0))
    if norm_bwd is None:
        return _carried(*_pcall(
            body, (g, w), name=name, grid=(M // tm,), in_specs=[g_spec, w_spec], out_specs=row,
            out_shape=jax.ShapeDtypeStruct((M, K), out_dtype), semantics=("parallel",), riders=riders), riders)
    vec = pl.BlockSpec((1, K), lambda i: (0, 0))
    return _carried(*_pcall(
        body_norm, (g, w) + tuple(norm_bwd), name=name, grid=(M // tm,),
        in_specs=[g_spec, w_spec, row, vec, row], out_specs=(row, vec),
        out_shape=(jax.ShapeDtypeStruct((M, K), F32), jax.ShapeDtypeStruct((1, K), F32)),
        semantics=("arbitrary",), riders=riders), riders)


def _mm_tn_rs(name, a, g, to, tn=512):
    T, M = a.shape
    N = g.shape[1]
    tt = _row_tile(T, _TN_TOKENS)
    tn = min(tn, N)
    return _matmul(name, (a, g), grid=(M // to, N // tn, T // tt),
                   in_specs=[pl.BlockSpec((tt, to), lambda i, j, k: (k, i)),
                             pl.BlockSpec((tt, tn), lambda i, j, k: (k, j))],
                   o_spec=pl.BlockSpec((to, tn), lambda i, j, k: (i, j)),
                   out_shape=(M, N), out_dtype=F32, dims=TN)


def _mm_tn_cs(name, a, g, nq, to, riders=(), stacked=False):
    T, M = a.shape
    wd = g.shape[-1] * (2 if stacked else 1) // nq
    tt = _row_tile(T, _TN_TOKENS)
    g_spec = (pl.BlockSpec((None, tt, wd), lambda i, j, k: (j // 2, k, j % 2)) if stacked
              else pl.BlockSpec((tt, wd), lambda i, j, k: (k, j)))
    return _matmul(name, (a, g), grid=(M // to, nq, T // tt),
                   in_specs=[pl.BlockSpec((tt, to), lambda i, j, k: (k, i)), g_spec],
                   o_spec=pl.BlockSpec((None, to, wd), lambda i, j, k: (j, i, 0)),
                   out_shape=(nq, M, wd), out_dtype=F32, dims=TN, riders=riders)


def _mm_tn_cs_to_sibling(name, a, g, nq, place, riders=(), send=True):
    T, M = a.shape
    wd = g.shape[-1] // nq
    to = M // 2
    steps = 2 * nq

    def body(s_ref, a_ref, g_ref, o_hbm, land_hbm, acc, wsem, send_sem, recv_sem):
        t = pl.program_id(0)
        c = s_ref[1]

        def writeback(tt):
            half = (tt // nq + 1 + c) % 2
            return pltpu.make_async_copy(acc.at[tt % 2], o_hbm.at[tt % nq, pl.ds(half * to, to), :], wsem.at[tt % 2])

        @pl.when(t >= 2)
        def _():
            writeback(t - 2).wait()

        if send:
            x, y, _, _ = _place()
            to_sibling = _remote(o_hbm.at[:, pl.ds((1 - c) * to, to), :], land_hbm, send_sem.at[0], recv_sem.at[0],
                                 (x, y, 1 - c))

            @pl.when(t == nq + 1)
            def _():
                to_sibling.start()

        acc[t % 2] = _dot_tn(a_ref[...], g_ref[...])
        writeback(t).start()

        @pl.when(t == steps - 1)
        def _():
            writeback(t - 1).wait()
            writeback(t).wait()
            if send:
                to_sibling.wait()

    out, carried = _pcall(
        body, (a, g), name=name, grid=(steps,),
        in_specs=[pl.BlockSpec((T, to), lambda t, s: (0, (t // nq + 1 + s[1]) % 2)),
                  pl.BlockSpec((T, wd), lambda t, s: (0, t % nq))],
        out_specs=(HBM_SPEC, HBM_SPEC),
        out_shape=(jax.ShapeDtypeStruct((nq, M, wd), F32), jax.ShapeDtypeStruct((nq, to, wd), F32)),
        scratch_shapes=[pltpu.VMEM((2, to, wd), F32), pltpu.SemaphoreType.DMA((2,)),
                        pltpu.SemaphoreType.DMA((1,)), pltpu.SemaphoreType.DMA((1,))],
        semantics=("arbitrary",), riders=riders, prefetch=place)
    return _carried(out, carried, riders)


def _rms_fwd(name, x, g, riders=()):
    T, D = x.shape
    tm = _row_tile(T)

    def body(x_ref, g_ref, o_ref):
        o_ref[...] = _rms_rows(x_ref[...], g_ref[...]).astype(o_ref.dtype)

    return _carried(*_pcall(
        body, (x, g), name=name, grid=(T // tm,),
        in_specs=[pl.BlockSpec((tm, D), lambda i: (i, 0)), pl.BlockSpec((1, D), lambda i: (0, 0))],
        out_specs=pl.BlockSpec((tm, D), lambda i: (i, 0)),
        out_shape=jax.ShapeDtypeStruct((T, D), BF16), semantics=("parallel",), riders=riders), riders)


_NORM1_STEPS = 4


def _norm1_and_casts(x, g, shards, place, riders=()):
    T, D = x.shape
    tm = T // _NORM1_STEPS
    names = list(shards)

    def body(s_ref, x_ref, g_ref, *refs):
        w_refs, o_ref, slab_refs = refs[:len(names)], refs[len(names)], refs[len(names) + 1:]
        o_ref[...] = _rms_rows(x_ref[...], g_ref[...]).astype(o_ref.dtype)
        for w_ref, slab_ref in zip(w_refs, slab_refs):
            slab_ref[...] = w_ref[...].astype(slab_ref.dtype)

    in_specs = [pl.BlockSpec((tm, D), lambda i, s: (i, 0)), pl.BlockSpec((1, D), lambda i, s: (0, 0))]
    out_specs = [pl.BlockSpec((tm, D), lambda i, s: (i, 0))]
    out_shape = [jax.ShapeDtypeStruct((T, D), BF16)]
    for n in names:
        r, cc = shards[n].shape
        assert r % (_NORM1_STEPS * 16) == 0
        in_specs.append(pl.BlockSpec((r // _NORM1_STEPS, cc), lambda i, s: (i, 0)))
        out_specs.append(pl.BlockSpec((None, r // _NORM1_STEPS, cc), lambda i, s: (s[0], i, 0)))
        out_shape.append(jax.ShapeDtypeStruct((N_CHIPS, r, cc), BF16))
    out, carried = _pcall(body, (x, g, *[shards[n] for n in names]), name="norm1", grid=(_NORM1_STEPS,),
                          in_specs=in_specs, out_specs=out_specs, out_shape=out_shape, semantics=("parallel",),
                          riders=riders, prefetch=place)
    return _carried((out[0], dict(zip(names, out[1:]))), carried, riders)


def _rms_rows(xv, gain):
    return xv * lax.rsqrt(jnp.mean(xv * xv, axis=-1, keepdims=True) + EPS) * gain


def _rms_bwd_rows(xv, gain, dh):
    r = lax.rsqrt(jnp.mean(xv * xv, axis=-1, keepdims=True) + EPS)
    n = xv * r
    dn = dh * gain
    return r * (dn - n * jnp.mean(dn * n, axis=-1, keepdims=True)), jnp.sum(dh * n, axis=0, keepdims=True)


def _rms_bwd(name, x, g, dh, dres):
    T, D = x.shape
    tm = _row_tile(T)
    has_res = dres is not None

    def body(*refs):
        if has_res:
            x_ref, g_ref, dh_ref, dr_ref, dx_ref, dg_ref = refs
        else:
            x_ref, g_ref, dh_ref, dx_ref, dg_ref = refs

        @pl.when(pl.program_id(0) == 0)
        def _():
            dg_ref[...] = jnp.zeros_like(dg_ref)

        dx, dg = _rms_bwd_rows(x_ref[...], g_ref[...], dh_ref[...])
        dg_ref[...] += dg
        if has_res:
            dx = dx + dr_ref[...]
        dx_ref[...] = dx

    row = pl.BlockSpec((tm, D), lambda i: (i, 0))
    vec = pl.BlockSpec((1, D), lambda i: (0, 0))
    ops = (x, g, dh, dres) if has_res else (x, g, dh)
    return _pallas(
        body, name=name, grid=(T // tm,), in_specs=[row, vec, row] + ([row] if has_res else []),
        out_specs=(row, vec),
        out_shape=(jax.ShapeDtypeStruct((T, D), F32), jax.ShapeDtypeStruct((1, D), F32)),
        compiler_params=_cp("arbitrary"),
    )(*ops)


def _proj_res_norm(name, a, w, res, gain):
    M, K = a.shape
    N = w.shape[1]
    tm = _row_tile(M)

    def body(a_ref, w_ref, r_ref, g_ref, x_ref, h_ref):
        xv = _dot(a_ref[...], w_ref[...]) + r_ref[...]
        x_ref[...] = xv
        h_ref[...] = _rms_rows(xv, g_ref[...]).astype(h_ref.dtype)

    row = pl.BlockSpec((tm, N), lambda i: (i, 0))
    return _pallas(
        body, name=name, grid=(M // tm,),
        in_specs=[pl.BlockSpec((tm, K), lambda i: (i, 0)), pl.BlockSpec((K, N), lambda i: (0, 0)), row,
                  pl.BlockSpec((1, N), lambda i: (0, 0))],
        out_specs=(row, row), out_shape=(jax.ShapeDtypeStruct((M, N), F32), jax.ShapeDtypeStruct((M, N), BF16)),
        compiler_params=_cp("parallel"),
    )(a, w, res, gain)


def _proj_res_loss(name, a, w, res, tgt, gain):
    M, K = a.shape
    D = w.shape[1]
    tm = _row_tile(M)

    def body(a_ref, w_ref, r_ref, t_ref, g_ref, dx_ref, dg_ref, loss_ref):
        @pl.when(pl.program_id(0) == 0)
        def _():
            dg_ref[...] = jnp.zeros_like(dg_ref)
            loss_ref[...] = jnp.zeros_like(loss_ref)

        xv = _dot(a_ref[...], w_ref[...]) + r_ref[...]
        gv = g_ref[...]
        diff = _rms_rows(xv, gv) - t_ref[...]
        loss_ref[...] += 0.5 * jnp.sum(jnp.mean(diff * diff, axis=-1, keepdims=True))
        dx, dg = _rms_bwd_rows(xv, gv, diff * (1.0 / D))
        dg_ref[...] += dg
        dx_ref[...] = dx

    row = pl.BlockSpec((tm, D), lambda i: (i, 0))
    vec = pl.BlockSpec((1, D), lambda i: (0, 0))
    return _pallas(
        body, name=name, grid=(M // tm,),
        in_specs=[pl.BlockSpec((tm, K), lambda i: (i, 0)), pl.BlockSpec((K, D), lambda i: (0, 0)), row, row, vec],
        out_specs=(row, vec, pl.BlockSpec((8, 128), lambda i: (0, 0))),
        out_shape=(jax.ShapeDtypeStruct((M, D), F32), jax.ShapeDtypeStruct((1, D), F32),
                   jax.ShapeDtypeStruct((8, 128), F32)),
        compiler_params=_cp("arbitrary"),
    )(a, w, res, tgt, gain)


def _gmlp_pieces(zu, zv, lng, lnb, ws_ref, bs_ref):
    u, du = _gelu_and_grad(zu)
    v, dv = _gelu_and_grad(zv)
    mu = jnp.mean(v, axis=-1, keepdims=True)
    vc = v - mu
    rstd = lax.rsqrt(jnp.mean(vc * vc, axis=-1, keepdims=True) + EPS)
    vhat = vc * rstd
    vn = vhat * lng + lnb
    row = lax.broadcasted_iota(jnp.int32, (GM_CHUNK, GM_CHUNK), 0)
    col = lax.broadcasted_iota(jnp.int32, (GM_CHUNK, GM_CHUNK), 1)
    tril = row >= col
    wms, mixed = [], []
    for g in range(GM_GROUPS):
        sl = slice(g * 128, (g + 1) * 128)
        wm = jnp.where(tril, ws_ref[g], 0.0)
        wms.append(wm)
        mixed.append(_dot(wm, vn[:, sl]) + bs_ref[g])
    return u, du, dv, rstd, vhat, vn, wms, mixed, tril


def _gmlp_fwd(proj, lng, lnb, ws, bs_col):
    T = proj.shape[0]
    n = T // GM_CHUNK

    ring = 3
    assert COL_ZV == COL_ZU + 1 and n >= ring

    def body(proj_hbm, lng_ref, lnb_ref, ws_ref, bs_ref, o_ref, z_buf, z_sem):
        s = pl.program_id(0)

        def fetch(t):
            return pltpu.make_async_copy(
                proj_hbm.at[pl.ds(t * GM_CHUNK, GM_CHUNK), pl.ds(COL_ZU * GM_WIDTH, 2 * GM_WIDTH)],
                z_buf.at[t % ring], z_sem.at[t % ring])

        @pl.when(s == 0)
        def _():
            for t in range(ring - 1):
                fetch(t).start()

        @pl.when(s + ring - 1 < n)
        def _():
            fetch(s + ring - 1).start()

        fetch(s).wait()
        z = z_buf[s % ring].astype(F32)
        u, _, _, _, _, _, _, mixed, _ = _gmlp_pieces(z[:, :GM_WIDTH], z[:, GM_WIDTH:], lng_ref[...], lnb_ref[...],
                                                     ws_ref, bs_ref)
        for g in range(GM_GROUPS):
            sl = slice(g * 128, (g + 1) * 128)
            o_ref[:, sl] = (u[:, sl] * mixed[g]).astype(o_ref.dtype)

    vec = pl.BlockSpec((1, GM_WIDTH), lambda i: (0, 0))
    return _pallas(
        body, name="gmlp_fwd", grid=(n,),
        in_specs=[HBM_SPEC, vec, vec,
                  pl.BlockSpec((GM_GROUPS, 128, 128), lambda i: (0, 0, 0)),
                  pl.BlockSpec((GM_GROUPS, 128, 1), lambda i: (0, 0, 0))],
        out_specs=pl.BlockSpec((GM_CHUNK, 512), lambda i: (i, 0)),
        out_shape=jax.ShapeDtypeStruct((T, GM_WIDTH), BF16),
        scratch_shapes=[pltpu.VMEM((ring, GM_CHUNK, 2 * GM_WIDTH), BF16), pltpu.SemaphoreType.DMA((ring,))],
        compiler_params=_cp("arbitrary"),
    )(proj, lng, lnb, ws, bs_col)


def _gmlp_bwd(proj, d_out, lng, lnb, ws, bs_col, riders=()):
    T = proj.shape[0]
    n = T // GM_CHUNK

    def body(zu_ref, zv_ref, do_ref, lng_ref, lnb_ref, ws_ref, bs_ref,
             dz_ref, dws_ref, dbs_ref, dlng_ref, dlnb_ref, dm_acc):
        i = pl.program_id(0)

        @pl.when(i == 0)
        def _():
            dws_ref[...] = jnp.zeros_like(dws_ref)
            dlng_ref[...] = jnp.zeros_like(dlng_ref)
            dlnb_ref[...] = jnp.zeros_like(dlnb_ref)
            dm_acc[...] = jnp.zeros_like(dm_acc)

        lng_v = lng_ref[...]
        u, du, dv, rstd, vhat, vn, wms, mixed, tril = _gmlp_pieces(zu_ref[...].astype(F32), zv_ref[...].astype(F32),
                                                                  lng_v, lnb_ref[...],
                                                                  ws_ref, bs_ref)
        do = do_ref[...]
        dvn_parts = []
        for g in range(GM_GROUPS):
            sl = slice(g * 128, (g + 1) * 128)
            dog = do[:, sl]
            dz_ref[:, sl] = (dog * mixed[g] * du[:, sl]).astype(dz_ref.dtype)
            dmix = dog * u[:, sl]
            dm_acc[:, sl] += dmix
            dws_ref[g] += jnp.where(tril, _dot_nt(dmix, vn[:, sl]), 0.0)
            dvn_parts.append(_dot_tn(wms[g], dmix))
        dvn = jnp.concatenate(dvn_parts, axis=1)
        dlng_ref[...] += jnp.sum(dvn * vhat, axis=0, keepdims=True)
        dlnb_ref[...] += jnp.sum(dvn, axis=0, keepdims=True)
        dvh = dvn * lng_v
        dvv = rstd * (dvh - jnp.mean(dvh, axis=-1, keepdims=True)
                      - vhat * jnp.mean(dvh * vhat, axis=-1, keepdims=True))
        dz_ref[:, GM_WIDTH:] = (dvv * dv).astype(dz_ref.dtype)

        @pl.when(i == n - 1)
        def _():
            for g in range(GM_GROUPS):
                dbs_ref[g] = jnp.sum(dm_acc[:, g * 128:(g + 1) * 128], axis=1, keepdims=True)

    vec = pl.BlockSpec((1, GM_WIDTH), lambda i: (0, 0))
    wsp = pl.BlockSpec((GM_GROUPS, 128, 128), lambda i: (0, 0, 0))
    bsp = pl.BlockSpec((GM_GROUPS, 128, 1), lambda i: (0, 0, 0))
    return _carried(*_pcall(
        body, (proj, proj, d_out, lng, lnb, ws, bs_col), name="gmlp_bwd", grid=(n,),
        in_specs=[pl.BlockSpec((GM_CHUNK, 512), lambda i: (i, COL_ZU)),
                  pl.BlockSpec((GM_CHUNK, 512), lambda i: (i, COL_ZV)),
                  pl.BlockSpec((None, GM_CHUNK, 512), lambda i: (0, i, 0)), vec, vec, wsp, bsp],
        out_specs=(pl.BlockSpec((GM_CHUNK, 2 * GM_WIDTH), lambda i: (i, 0)), wsp, bsp, vec, vec),
        out_shape=(jax.ShapeDtypeStruct((T, 2 * GM_WIDTH), BF16),
                   jax.ShapeDtypeStruct((GM_GROUPS, 128, 128), F32), jax.ShapeDtypeStruct((GM_GROUPS, 128, 1), F32),
                   jax.ShapeDtypeStruct((1, GM_WIDTH), F32), jax.ShapeDtypeStruct((1, GM_WIDTH), F32)),
        scratch_shapes=[pltpu.VMEM((GM_CHUNK, GM_WIDTH), F32)],
        semantics=("arbitrary",), riders=riders), riders)


def _hgrn_lower_bound(lbl):
    return 1.0 / (1.0 + jnp.exp(lbl[1:2, :] - lbl[0:1, :]))


def _hgrn_gates(hq, hf, lb):
    C = HG_CHUNK
    sg = _sigmoid(hf)
    fg = lb + (1.0 - lb) * sg
    sq = _sigmoid(hq)
    row = lax.broadcasted_iota(jnp.int32, (C, C), 0)
    col = lax.broadcasted_iota(jnp.int32, (C, C), 1)
    tril = row >= col
    logf = jnp.log(fg)
    a = _dot_01(tril, logf)
    a_last = jnp.sum(logf, axis=0, keepdims=True)
    first_half = lax.broadcasted_iota(jnp.int32, logf.shape, 0) < (C // 2)
    a_mid = jnp.sum(jnp.where(first_half, logf, 0.0), axis=0, keepdims=True)
    ea, ei, eki, ekl = jnp.exp(a), jnp.exp(a - a_mid), jnp.exp(a_mid - a), jnp.exp(a_last - a)
    k = 1.0 - fg
    q = hq * sq
    qi = (q * ei).astype(BF16).astype(F32)
    ki = (k * eki).astype(BF16).astype(F32)
    return dict(sg=sg, fg=fg, sq=sq, tril=tril, ea=ea, ei=ei, eki=eki, ekl=ekl, e_last=jnp.exp(a_last),
                qe=q * ea, qi=qi, ki=ki, kl=k * ekl)


def _heads(x):
    return [x[:, h * HG_DIM:(h + 1) * HG_DIM] for h in range(HG_HEADS)]


def _hgrn_fwd(proj, lbl, gh, B, S, riders=()):
    C = HG_CHUNK
    NC = S // C
    W = HG_HEADS * HG_DIM

    def body(q_ref, f_ref, i_ref, g_ref, lbl_ref, gh_ref, o_ref, bo_ref, st_ref, state):
        @pl.when(pl.program_id(0) == 0)
        def _():
            state[...] = jnp.zeros_like(state)

        lb = _hgrn_lower_bound(lbl_ref[...])
        ghv = gh_ref[...]
        for b in range(B):
            gt = _hgrn_gates(q_ref[b].astype(F32), f_ref[b].astype(F32), lb)
            v = _heads(i_ref[b])
            qe, qi, ki, kl, e_last = (_heads(gt[n]) for n in ("qe", "qi", "ki", "kl", "e_last"))
            outs, normed = [], []
            for h in range(HG_HEADS):
                p = jnp.where(gt["tril"], _dot_nt(qi[h], ki[h]), 0.0)
                st = state[b, h]
                st_ref[b, h] = st
                o = _dot_nt(qe[h], st) + _dot(p, v[h])
                state[b, h] = st * e_last[h] + _dot_tn(v[h], kl[h])
                outs.append(o)
                normed.append(o * lax.rsqrt(jnp.mean(o * o, axis=-1, keepdims=True) + EPS) * ghv)
            o_ref[b] = jnp.concatenate(outs, axis=1)
            hg = g_ref[b].astype(F32)
            bo_ref[b] = (jnp.concatenate(normed, axis=1) * (hg * _sigmoid(hg))).astype(bo_ref.dtype)

    def col(cb):
        return pl.BlockSpec((B, C, 512), lambda c: (0, c, cb))

    tile = pl.BlockSpec((B, C, W), lambda c: (0, c, 0))
    proj3 = proj.reshape(B, S, proj.shape[-1])
    out, carried = _pcall(
        body, (proj3, proj3, proj3, proj3, lbl, gh), name="hgrn_fwd", grid=(NC,),
        in_specs=[col(COL_HQ), col(COL_HF), col(COL_HI), col(COL_HG),
                  pl.BlockSpec((2, W), lambda c: (0, 0)), pl.BlockSpec((1, HG_DIM), lambda c: (0, 0))],
        out_specs=(tile, tile, pl.BlockSpec((B, None, HG_HEADS, 128, 128), lambda c: (0, c, 0, 0, 0))),
        out_shape=(jax.ShapeDtypeStruct((B, S, W), F32), jax.ShapeDtypeStruct((B, S, W), BF16),
                   jax.ShapeDtypeStruct((B, NC, HG_HEADS, 128, 128), F32)),
        scratch_shapes=[pltpu.VMEM((B, HG_HEADS, 128, 128), F32)],
        semantics=("arbitrary",), riders=riders)
    o_h, b_out, states = out
    out = (o_h, b_out.reshape(B * S, W), states)
    return (out, carried) if riders else out


def _hgrn_bwd(proj, o_saved, states, d_out, lbl, gh, others, B, S, riders=()):
    C = HG_CHUNK
    NC = S // C
    W = HG_HEADS * HG_DIM
    d_gm, d_xq, d_gates = (t.reshape(B, S, t.shape[-1]) for t in others)
    own0 = d_gm.shape[-1]
    xq0 = own0 + 4 * W
    gates0 = xq0 + d_xq.shape[-1]

    def body(q_ref, f_ref, i_ref, g_ref, o_ref, st_ref, do_ref, lbl_ref, gh_ref, gm_ref, xq_ref, gates_ref,
             d_ref, dlbl_ref, dgh_ref, dstate, dlb_acc):
        c = pl.program_id(0)
        d_ref[:, :, :own0] = gm_ref[...]
        d_ref[:, :, xq0:gates0] = xq_ref[...]
        d_ref[:, :, gates0:] = gates_ref[...]

        def put(b, k, val):
            d_ref[b, :, own0 + k * W:own0 + (k + 1) * W] = val.astype(d_ref.dtype)

        @pl.when(c == 0)
        def _():
            dstate[...] = jnp.zeros_like(dstate)
            dgh_ref[...] = jnp.zeros_like(dgh_ref)
            dlb_acc[...] = jnp.zeros_like(dlb_acc)

        lb = _hgrn_lower_bound(lbl_ref[...])
        ghv = gh_ref[...]
        row = lax.broadcasted_iota(jnp.int32, (C, C), 0)
        colm = lax.broadcasted_iota(jnp.int32, (C, C), 1)
        triu = colm >= row
        for b in range(B):
            hq, hg = q_ref[b].astype(F32), g_ref[b].astype(F32)
            gt = _hgrn_gates(hq, f_ref[b].astype(F32), lb)
            tril = gt["tril"]
            v = _heads(i_ref[b])
            qe, qi, ki, kl, e_last = (_heads(gt[n]) for n in ("qe", "qi", "ki", "kl", "e_last"))
            sgg = _sigmoid(hg)
            don_all = do_ref[b] * (hg * sgg)
            o, don = _heads(o_ref[b]), _heads(don_all)
            d_qe, d_qi, d_ki, d_kl, dv, n_all, dal = [], [], [], [], [], [], []
            for h in range(HG_HEADS):
                r = lax.rsqrt(jnp.mean(o[h] * o[h], axis=-1, keepdims=True) + EPS)
                n = o[h] * r
                n_all.append(n)
                dgh_ref[...] += jnp.sum(don[h] * n, axis=0, keepdims=True)
                dn = don[h] * ghv
                d_o = r * (dn - n * jnp.mean(dn * n, axis=-1, keepdims=True))
                st, dst = st_ref[b, h], dstate[b, h]
                p = jnp.where(tril, _dot_nt(qi[h], ki[h]), 0.0)
                dp = jnp.where(tril, _dot_nt(d_o, v[h]), 0.0)
                d_qe.append(_dot(d_o, st))
                d_qi.append(_dot(dp, ki[h]))
                d_ki.append(_dot_tn(dp, qi[h]))
                d_kl.append(_dot(v[h], dst))
                dv.append(_dot_tn(p, d_o) + _dot_nt(kl[h], dst))
                dstate[b, h] = dst * e_last[h] + _dot_tn(d_o, qe[h])
                dal.append(jnp.sum(dst * st, axis=0, keepdims=True) * e_last[h])
            d_qe, d_qi, d_ki, d_kl, n_all, dal = (jnp.concatenate(t, axis=1)
                                                  for t in (d_qe, d_qi, d_ki, d_kl, n_all, dal))
            put(b, 3, do_ref[b] * n_all * jnp.tile(ghv, (1, HG_HEADS)) * (sgg * (1.0 + hg * (1.0 - sgg))))
            put(b, 2, jnp.concatenate(dv, axis=1))
            d_a_last = dal + jnp.sum(d_kl * gt["kl"], axis=0, keepdims=True)
            dq = d_qe * gt["ea"] + d_qi * gt["ei"]
            dk = d_ki * gt["eki"] + d_kl * gt["ekl"]
            da = d_qe * gt["qe"] + d_qi * gt["qi"] - d_ki * gt["ki"] - d_kl * gt["kl"]
            dlogf = _dot_01(triu, da) + d_a_last
            sg, sq = gt["sg"], gt["sq"]
            dfg = dlogf / gt["fg"] - dk
            put(b, 1, dfg * (1.0 - lb) * sg * (1.0 - sg))
            dlb_acc[...] += jnp.sum(dfg * (1.0 - sg), axis=0, keepdims=True)
            put(b, 0, dq * (sq * (1.0 + hq * (1.0 - sq))))

        @pl.when(c == NC - 1)
        def _():
            dlb = dlb_acc[...]
            first = lax.broadcasted_iota(jnp.int32, (2, W), 0) == 0
            dlbl_ref[...] = jnp.where(first, dlb * lb * (1.0 - lb), -dlb * lb * (1.0 - lb))

    def col(cb):
        return pl.BlockSpec((B, C, 512), lambda c: (0, NC - 1 - c, cb))

    tile = pl.BlockSpec((B, C, W), lambda c: (0, NC - 1 - c, 0))
    proj3 = proj.reshape(B, S, proj.shape[-1])

    def rows(width):
        return pl.BlockSpec((B, C, width), lambda c: (0, NC - 1 - c, 0))

    width = proj.shape[-1]
    out, carried = _pcall(
        body, (proj3, proj3, proj3, proj3, o_saved, states, d_out.reshape(3, B, S, W), lbl, gh, d_gm, d_xq, d_gates),
        name="hgrn_bwd", grid=(NC,),
        in_specs=[col(COL_HQ), col(COL_HF), col(COL_HI), col(COL_HG), tile,
                  pl.BlockSpec((B, None, HG_HEADS, 128, 128), lambda c: (0, NC - 1 - c, 0, 0, 0)),
                  pl.BlockSpec((None, B, C, W), lambda c: (1, 0, NC - 1 - c, 0)),
                  pl.BlockSpec((2, W), lambda c: (0, 0)), pl.BlockSpec((1, HG_DIM), lambda c: (0, 0)),
                  rows(d_gm.shape[-1]), rows(d_xq.shape[-1]), rows(d_gates.shape[-1])],
        out_specs=(rows(width), pl.BlockSpec((2, W), lambda c: (0, 0)), pl.BlockSpec((1, HG_DIM), lambda c: (0, 0))),
        out_shape=(jax.ShapeDtypeStruct((B, S, width), BF16), jax.ShapeDtypeStruct((2, W), F32),
                   jax.ShapeDtypeStruct((1, HG_DIM), F32)),
        scratch_shapes=[pltpu.VMEM((B, HG_HEADS, 128, 128), F32), pltpu.VMEM((1, W), F32)],
        semantics=("arbitrary",), riders=riders)
    out = (out[0].reshape(B * S, width),) + tuple(out[1:])
    return (out, carried) if riders else out


_XA_SCALE = XA_DIM ** -0.5


def _attn_probs(qh, kh):
    s = _dot_nt(qh, kh) * _XA_SCALE
    e = jnp.exp(s - jnp.max(s, axis=-1, keepdims=True))
    return e / jnp.sum(e, axis=-1, keepdims=True)


def _attn_fwd(proj, kv, B, S):
    T = B * S
    tq = _row_tile(S)
    nq = S // tq
    W = XA_HEADS * XA_DIM

    def body(q_ref, kv_ref, o_ref):
        for h in range(XA_HEADS):
            sl = slice(h * 128, (h + 1) * 128)
            p = _attn_probs(q_ref[:, sl], kv_ref[:, sl])
            o_ref[:, sl] = _dot(p, kv_ref[:, W + h * 128:W + (h + 1) * 128]).astype(o_ref.dtype)

    return _pallas(
        body, name="attn_fwd", grid=(B, nq),
        in_specs=[pl.BlockSpec((tq, 512), lambda b, i: (b * nq + i, COL_XQ)),
                  pl.BlockSpec((MEM_LEN, 2 * W), lambda b, i: (b, 0))],
        out_specs=pl.BlockSpec((tq, W), lambda b, i: (b * nq + i, 0)),
        out_shape=jax.ShapeDtypeStruct((T, W), BF16), compiler_params=_cp("parallel", "parallel"),
    )(proj, kv)


def _attn_bwd(proj, kv, d_out, B, S):
    T = B * S
    tq = _row_tile(S)
    nq = S // tq
    W = XA_HEADS * XA_DIM

    def body(q_ref, kv_ref, do_ref, dq_ref, dkv_ref):
        @pl.when(pl.program_id(1) == 0)
        def _():
            dkv_ref[...] = jnp.zeros_like(dkv_ref)

        for h in range(XA_HEADS):
            sl = slice(h * 128, (h + 1) * 128)
            slv = slice(W + h * 128, W + (h + 1) * 128)
            qh = q_ref[:, sl]
            kh = kv_ref[:, sl]
            p = _attn_probs(qh, kh)
            dc = do_ref[:, sl]
            dp = _dot_nt(dc, kv_ref[:, slv])
            ds = p * (dp - jnp.sum(dp * p, axis=-1, keepdims=True)) * _XA_SCALE
            dq_ref[:, sl] = _dot(ds, kh).astype(dq_ref.dtype)
            dkv_ref[:, sl] += _dot_tn(ds, qh)
            dkv_ref[:, slv] += _dot_tn(p, dc)

    kvspec = pl.BlockSpec((MEM_LEN, 2 * W), lambda b, i: (b, 0))
    tile = pl.BlockSpec((tq, W), lambda b, i: (b * nq + i, 0))
    return _pallas(
        body, name="attn_bwd", grid=(B, nq),
        in_specs=[pl.BlockSpec((tq, 512), lambda b, i: (b * nq + i, COL_XQ)), kvspec,
                  pl.BlockSpec((None, tq, W), lambda b, i: (2, b * nq + i, 0))],
        out_specs=(tile, kvspec),
        out_shape=(jax.ShapeDtypeStruct((T, W), BF16), jax.ShapeDtypeStruct((B * MEM_LEN, 2 * W), F32)),
        compiler_params=_cp("parallel", "arbitrary"),
    )(proj, kv, d_out)


_MERGE_TM = 256
_GATE_W = 512


def _gate_specs(tm):
    base = COL_GATE0 // _GATE_W
    return [pl.BlockSpec((tm, _GATE_W), functools.partial(lambda i, k: (i, base + k), k=k)) for k in range(6)]


def _merge_fwd(a_out, b_out, c_out, wb, proj, riders=()):
    T = a_out.shape[0]
    tm = _row_tile(T, _MERGE_TM)
    nq, _, wd = wb.shape
    per_half = _GATE_W // wd

    def body(a_ref, b_ref, c_ref, w_ref, *rest):
        gates, (m_ref, up_ref) = rest[:6], rest[6:]
        for hf in range(2):
            cols = slice(hf * _GATE_W, (hf + 1) * _GATE_W)
            acc = None
            for n, br in enumerate((a_ref, b_ref, c_ref)):
                x = br[...]
                up = jnp.concatenate([_dot(x, w_ref[per_half * hf + j, n * BR_WIDTH:(n + 1) * BR_WIDTH, :])
                                      for j in range(per_half)], axis=1)
                up_ref[n, :, cols] = up.astype(up_ref.dtype)
                term = _sigmoid(gates[2 * n + hf][...].astype(F32)) * up
                acc = term if acc is None else acc + term
            m_ref[:, cols] = acc.astype(m_ref.dtype)

    br_spec = pl.BlockSpec((tm, BR_WIDTH), lambda i: (i, 0))
    return _carried(*_pcall(
        body, (a_out, b_out, c_out, wb, *([proj] * 6)), name="merge_fwd", grid=(T // tm,),
        in_specs=[br_spec, br_spec, br_spec,
                  pl.BlockSpec((nq, 3 * BR_WIDTH, wd), lambda i: (0, 0, 0))] + _gate_specs(tm),
        out_specs=(pl.BlockSpec((tm, D_MODEL), lambda i: (i, 0)), pl.BlockSpec((3, tm, D_MODEL), lambda i: (0, i, 0))),
        out_shape=(jax.ShapeDtypeStruct((T, D_MODEL), BF16), jax.ShapeDtypeStruct((3, T, D_MODEL), BF16)),
        semantics=("parallel",), riders=riders), riders)


def _branch_bwd_act(d_ups, wb, riders=()):
    _, T, D = d_ups.shape
    nq, _, wd = wb.shape
    tm = _row_tile(T)

    def body(d_ref, w_ref, o_ref):
        acc = None
        for q in range(nq):
            part = _dot_nt(d_ref[:, q * wd:(q + 1) * wd], w_ref[q])
            acc = part if acc is None else acc + part
        o_ref[...] = acc

    return _carried(*_pcall(
        body, (d_ups, wb), name="d_branch", grid=(3, T // tm),
        in_specs=[pl.BlockSpec((None, tm, D), lambda n, i: (n, i, 0)),
                  pl.BlockSpec((nq, BR_WIDTH, wd), lambda n, i: (0, n, 0))],
        out_specs=pl.BlockSpec((None, tm, BR_WIDTH), lambda n, i: (n, i, 0)),
        out_shape=jax.ShapeDtypeStruct((3, T, BR_WIDTH), F32), semantics=("parallel", "parallel"),
        riders=riders), riders)


def _branch_bwd_weight(name, br, d_ups, n, into=None):
    T = br.shape[0]
    D = d_ups.shape[2]
    wd = D // N_CHIPS
    tt = _row_tile(T, _TN_TOKENS)
    n_br = d_ups.shape[0]

    def body(b_ref, d_ref, *rest):
        o_ref = rest[-1]
        k = pl.program_id(0)
        for q in range(N_CHIPS):
            part = _dot_tn(b_ref[...], d_ref[:, q * wd:(q + 1) * wd])

            @pl.when(k == 0)
            def _():
                o_ref[q] = part

            @pl.when(k > 0)
            def _():
                o_ref[q] += part

    return _pallas(
        body, name=name, grid=(T // tt,),
        in_specs=[pl.BlockSpec((tt, BR_WIDTH), lambda k: (k, 0)),
                  pl.BlockSpec((None, tt, D), lambda k: (n, k, 0))] + ([] if into is None else [HBM_SPEC]),
        out_specs=pl.BlockSpec((N_CHIPS, BR_WIDTH, wd), lambda k: (0, n, 0)),
        out_shape=jax.ShapeDtypeStruct((N_CHIPS, n_br * BR_WIDTH, wd), F32),
        input_output_aliases={} if into is None else {2: 0}, compiler_params=_cp("arbitrary"),
    )(br, d_ups, *(() if into is None else (into,)))


def _merge_bwd(d_merged, ups, proj, riders=()):
    T = d_merged.shape[0]
    tm = _row_tile(T, _MERGE_TM)

    def body(dm_ref, up_ref, *rest):
        gates, (dup_ref, dg_ref) = rest[:6], rest[6:]
        for hf in range(2):
            cols = slice(hf * _GATE_W, (hf + 1) * _GATE_W)
            dm = dm_ref[:, cols]
            for n in range(3):
                gate = _sigmoid(gates[2 * n + hf][...].astype(F32))
                dup_ref[n, :, cols] = (dm * gate).astype(dup_ref.dtype)
                dg_ref[:, n * D_MODEL + hf * _GATE_W:n * D_MODEL + (hf + 1) * _GATE_W] = (
                    dm * up_ref[n, :, cols].astype(F32) * gate * (1.0 - gate)).astype(dg_ref.dtype)

    tile = pl.BlockSpec((tm, D_MODEL), lambda i: (i, 0))
    tile3 = pl.BlockSpec((3, tm, D_MODEL), lambda i: (0, i, 0))
    return _carried(*_pcall(
        body, (d_merged, ups, *([proj] * 6)), name="merge_bwd", grid=(T // tm,),
        in_specs=[tile, tile3] + _gate_specs(tm),
        out_specs=(tile3, pl.BlockSpec((tm, 3 * D_MODEL), lambda i: (i, 0))),
        out_shape=(jax.ShapeDtypeStruct((3, T, D_MODEL), BF16), jax.ShapeDtypeStruct((T, 3 * D_MODEL), BF16)),
        semantics=("parallel",), riders=riders), riders)


_CONV_TF = D_FF // 2
_CONV_TS = 256
_HALO = 16


def _conv_fwd(ab, cw, cb, B, S):
    T = B * S
    ts = _row_tile(S, _CONV_TS)
    tf = _CONV_TF
    nb = D_FF // tf
    tps = S // ts
    hb = ts // _HALO

    steps = (T // ts) * nb
    ring = 3

    def body(ab_hbm, p_ref, w_ref, cb_ref, o_ref, a_buf, b_buf, a_sem, b_sem):
        s = pl.program_id(0) * nb + pl.program_id(1)

        def fetch(t):
            rows, slot = pl.ds((t // nb) * ts, ts), t % ring
            return (pltpu.make_async_copy(ab_hbm.at[rows, pl.ds((t % nb) * tf, tf)], a_buf.at[slot], a_sem.at[slot]),
                    pltpu.make_async_copy(ab_hbm.at[rows, pl.ds((t % nb + nb) * tf, tf)], b_buf.at[slot],
                                          b_sem.at[slot]))

        @pl.when(s == 0)
        def _():
            for t in range(ring - 1):
                for cp in fetch(t):
                    cp.start()

        @pl.when(s + ring - 1 < steps)
        def _():
            for cp in fetch(s + ring - 1):
                cp.start()

        for cp in fetch(s):
            cp.wait()
        start = (pl.program_id(0) % tps) == 0
        a = a_buf[s % ring].astype(F32)
        prev = jnp.where(start, 0.0, p_ref[...].astype(F32))
        ext = jnp.concatenate([prev, a], axis=0)
        a1 = pltpu.roll(ext, 1, 0)[_HALO:, :]
        a2 = pltpu.roll(ext, 2, 0)[_HALO:, :]
        ac = cb_ref[...] + w_ref[0] * a2 + w_ref[1] * a1 + w_ref[2] * a
        o_ref[...] = (ac * _sigmoid(ac) * b_buf[s % ring].astype(F32)).astype(o_ref.dtype)

    assert steps >= ring
    return _pallas(
        body, name="conv_fwd", grid=(T // ts, nb),
        in_specs=[HBM_SPEC,
                  pl.BlockSpec((_HALO, tf), lambda i, j: (jnp.maximum(i * hb - 1, 0), j)),
                  pl.BlockSpec((3, 1, tf), lambda i, j: (0, 0, j)),
                  pl.BlockSpec((1, tf), lambda i, j: (0, j))],
        out_specs=pl.BlockSpec((ts, tf), lambda i, j: (i, j)),
        out_shape=jax.ShapeDtypeStruct((T, D_FF), BF16),
        scratch_shapes=[pltpu.VMEM((ring, ts, tf), BF16), pltpu.VMEM((ring, ts, tf), BF16),
                        pltpu.SemaphoreType.DMA((ring,)), pltpu.SemaphoreType.DMA((ring,))],
        compiler_params=_cp("arbitrary", "arbitrary"),
    )(ab, ab, cw, cb)


def _conv_bwd(ab, d_ff, cw, cb, B, S, riders=()):
    T = B * S
    ts = _row_tile(S, _CONV_TS)
    tf = _CONV_TF
    nb = D_FF // tf
    tps = S // ts
    hb = ts // _HALO
    last_h = T // _HALO - 1
    n_ext = ts + _HALO

    def body(a_ref, ap_ref, an_ref, b_ref, bn_ref, d_ref, dn_ref, w_ref, cb_ref, dab_ref, dw_ref, dcb_ref):
        i = pl.program_id(1)

        @pl.when(i == 0)
        def _():
            dw_ref[...] = jnp.zeros_like(dw_ref)
            dcb_ref[...] = jnp.zeros_like(dcb_ref)

        start = (i % tps) == 0
        end = (i % tps) == tps - 1
        a = a_ref[...].astype(F32)
        ext = jnp.concatenate([jnp.where(start, 0.0, ap_ref[...].astype(F32)), a, an_ref[...].astype(F32)], axis=0)
        r1 = pltpu.roll(ext, 1, 0)[_HALO:, :]
        r2 = pltpu.roll(ext, 2, 0)[_HALO:, :]
        ac = cb_ref[...] + w_ref[0] * r2 + w_ref[1] * r1 + w_ref[2] * ext[_HALO:, :]
        sg = _sigmoid(ac)
        d_e = jnp.concatenate([d_ref[...].astype(F32), jnp.where(end, 0.0, dn_ref[...].astype(F32))], axis=0)
        b_e = jnp.concatenate([b_ref[...].astype(F32), bn_ref[...].astype(F32)], axis=0)
        dab_ref[1] = (d_e[:ts, :] * (ac * sg)[:ts, :]).astype(dab_ref.dtype)
        dac = d_e * b_e * sg * (1.0 + ac * (1.0 - sg))
        u1 = pltpu.roll(dac, n_ext - 1, 0)[:ts, :]
        u2 = pltpu.roll(dac, n_ext - 2, 0)[:ts, :]
        dac0 = dac[:ts, :]
        dab_ref[0] = (w_ref[2] * dac0 + w_ref[1] * u1 + w_ref[0] * u2).astype(dab_ref.dtype)
        dcb_ref[...] += jnp.sum(dac0, axis=0, keepdims=True)
        dw_ref[2] += jnp.sum(dac0 * a, axis=0, keepdims=True)
        dw_ref[1] += jnp.sum(dac0 * r1[:ts, :], axis=0, keepdims=True)
        dw_ref[0] += jnp.sum(dac0 * r2[:ts, :], axis=0, keepdims=True)

    def cur(off):
        return pl.BlockSpec((ts, tf), lambda j, i: (i, j + off))

    def nxt(off):
        return pl.BlockSpec((_HALO, tf), lambda j, i: (jnp.minimum((i + 1) * hb, last_h), j + off))

    return _carried(*_pcall(
        body, (ab, ab, ab, ab, ab, d_ff, d_ff, cw, cb), name="conv_bwd", grid=(nb, T // ts),
        in_specs=[cur(0), pl.BlockSpec((_HALO, tf), lambda j, i: (jnp.maximum(i * hb - 1, 0), j)), nxt(0),
                  cur(nb), nxt(nb), cur(0), nxt(0),
                  pl.BlockSpec((3, 1, tf), lambda j, i: (0, 0, j)), pl.BlockSpec((1, tf), lambda j, i: (0, j))],
        out_specs=(pl.BlockSpec((2, ts, tf), lambda j, i: (0, i, j)), pl.BlockSpec((3, 1, tf), lambda j, i: (0, 0, j)),
                   pl.BlockSpec((1, tf), lambda j, i: (0, j))),
        out_shape=(jax.ShapeDtypeStruct((2, T, D_FF), BF16),
                   jax.ShapeDtypeStruct((3, 1, D_FF), F32), jax.ShapeDtypeStruct((1, D_FF), F32)),
        semantics=("parallel", "arbitrary"), riders=riders), riders)


def _local_step(x, mem, tgt, p, comm, B, S):
    g = {}
    h, slabs = comm.carry(
        "norm1", lambda r: _norm1_and_casts(x, p["norm1_g"], p["cast_beside_norm1"], comm.place, riders=r))
    comm.slabs.update(slabs)
    proj = comm.carry("in_proj", lambda r: _mm_cs("in_proj", h, comm.w("w_in"), BF16, riders=r))
    a_out = _gmlp_fwd(proj, p["ln_v_g"], p["ln_v_b"], p["w_spatial"], p["b_spatial"])
    o_h, b_out, states = comm.carry(
        "hgrn_fwd", lambda r: _hgrn_fwd(proj, p["lb_logits"], p["hgrn_norm_g"], B, S, riders=r))
    memn = _rms_fwd("mem_norm", mem, p["mem_norm_g"])
    kv = _mm_rs("mem_kv", memn, comm.w("w_mem_kv"), F32)
    c_out = _attn_fwd(proj, kv, B, S)
    merged, ups = comm.carry(
        "merge_fwd", lambda r: _merge_fwd(a_out, b_out, c_out, comm.w("w_branch"), proj, riders=r))
    x1, h2 = _proj_res_norm("out_proj_norm2", merged, comm.w("w_out"), x, p["norm2_g"])
    ab = comm.carry("up_proj", lambda r: _mm_cs("up_proj", h2, comm.w("w_up"), BF16, riders=r))
    conv_w = comm.w("conv_w")
    ff = _conv_fwd(ab, conv_w, p["conv_b"], B, S)
    dx2, g["final_g"], loss = _proj_res_loss("down_proj_loss", ff, comm.w("w_down"), x1, tgt, p["final_g"])

    comm.grad("w_down", _mm_tn_rs("g_w_down", ff, dx2, to=D_FF // 2))
    d_ff = comm.carry("d_ff", lambda r: _mm_nt_rs("d_ff", dx2, comm.w("w_down"), BF16, riders=r))
    d_ab, g["conv_w"], g["conv_b"] = comm.carry(
        "conv_bwd", lambda r: _conv_bwd(ab, d_ff, conv_w, p["conv_b"], B, S, riders=r))
    comm.grad("w_up", _mm_tn_cs("g_w_up", h2, d_ab, N_CHIPS, to=512, stacked=True))
    d_x1, g["norm2_g"] = comm.carry("d_h2", lambda r: _mm_nt_cs(
        "d_h2_norm2_bwd", d_ab, comm.w("w_up"), F32, riders=r, stacked=True, norm_bwd=(x1, p["norm2_g"], dx2)))
    comm.grad("w_out", _mm_tn_rs("g_w_out", merged, d_x1, to=512))
    d_merged = _mm_nt_rs("d_merged", d_x1, comm.w("w_out"), F32)
    d_ups, d_gates = comm.carry("merge_bwd", lambda r: _merge_bwd(d_merged, ups, proj, riders=r))

    d_br = comm.carry("d_branch", lambda r: _branch_bwd_act(d_ups, comm.w("w_branch"), riders=r))
    g_branch = None
    for n, br in enumerate((a_out, b_out, c_out)):
        g_branch = _branch_bwd_weight("g_w_branch%d" % n, br, d_ups, n, into=g_branch)
    comm.grad("w_branch", g_branch)

    d_gm, g["w_spatial"], g["b_spatial"], g["ln_v_g"], g["ln_v_b"] = comm.carry(
        "gmlp_bwd", lambda r: _gmlp_bwd(proj, d_br, p["ln_v_g"], p["ln_v_b"], p["w_spatial"], p["b_spatial"],
                                        riders=r))
    d_xq, d_kv = _attn_bwd(proj, kv, d_br, B, S)
    comm.grad("w_mem_kv", _mm_tn_rs("g_w_mem_kv", memn, d_kv, to=512))
    d_memn = _mm_nt_rs("d_memn", d_kv, comm.w("w_mem_kv"), F32)
    _, g["mem_norm_g"] = _rms_bwd("mem_norm_bwd", mem, p["mem_norm_g"], d_memn, None)
    d_proj, g["lb_logits"], g["hgrn_norm_g"] = comm.carry(
        "hgrn_bwd", lambda r: _hgrn_bwd(proj, o_h, states, d_br, p["lb_logits"], p["hgrn_norm_g"],
                                        (d_gm, d_xq, d_gates), B, S, riders=r))
    comm.small_grads([g[n].reshape(_SMALL_SHAPE[n]) for n in _SMALL_EARLY] + [loss])
    comm.grad("w_in", *comm.carry("g_w_in", lambda r: _mm_tn_cs_to_sibling(
        "g_w_in", h, d_proj, N_CHIPS, comm.place, riders=r, send=comm.sends)))
    grad_x, g["norm1_g"] = comm.carry("d_h", lambda r: _mm_nt_cs(
        "d_h_norm1_bwd", d_proj, comm.w("w_in"), F32, riders=r, norm_bwd=(x, p["norm1_g"], d_x1)))
    return loss, grad_x, g


HBM_SPEC = pl.BlockSpec(memory_space=pltpu.HBM)


def _place():
    x, y, c = lax.axis_index("x"), lax.axis_index("y"), lax.axis_index("c")
    other_chips = [(1 - x, y), (x, 1 - y), (1 - x, 1 - y)]
    return x, y, c, other_chips


def _remote(src, dst, send_sem, recv_sem, dev):
    return pltpu.make_async_remote_copy(src_ref=src, dst_ref=dst, send_sem=send_sem, recv_sem=recv_sem,
                                        device_id=dev, device_id_type=MESH_ID)


class _Exchange:
    def __init__(self, operands, out_shape, aliases, scratch, start, finish, mid=None, mid_at=0.5):
        self.operands, self.out_shape, self.aliases, self.scratch = operands, out_shape, aliases, scratch
        self.start, self.finish, self.mid, self.mid_at = start, finish, mid, mid_at


def _run_exchanges(name, exs):
    n_in = [len(ex.operands) for ex in exs]
    n_out = [len(ex.out_shape) for ex in exs]
    n_scr = [len(ex.scratch) for ex in exs]

    def body(*refs):
        ins, outs, scr = refs[:sum(n_in)], refs[sum(n_in):sum(n_in) + sum(n_out)], refs[sum(n_in) + sum(n_out):]
        parts, oi, oo, os_ = [], 0, 0, 0
        for k in range(len(exs)):
            parts.append((ins[oi:oi + n_in[k]], outs[oo:oo + n_out[k]], scr[os_:os_ + n_scr[k]]))
            oi, oo, os_ = oi + n_in[k], oo + n_out[k], os_ + n_scr[k]
        for ex, part in zip(exs, parts):
            ex.start(*part)
        for ex, part in zip(exs, parts):
            if ex.mid is not None:
                ex.mid(*part)
        for ex, part in zip(exs, parts):
            ex.finish(*part)

    aliases, ops, shapes, scratch, oi, oo = {}, [], [], [], 0, 0
    for k, ex in enumerate(exs):
        aliases.update({oi + a: oo + b for a, b in ex.aliases.items()})
        oi, oo = oi + n_in[k], oo + n_out[k]
        ops += list(ex.operands)
        shapes += [pltpu.HBM(s.shape, s.dtype) for s in ex.out_shape]
        scratch += list(ex.scratch)
    res = _pallas(
        body, name=name, in_specs=[HBM_SPEC] * len(ops), out_specs=(HBM_SPEC,) * len(shapes), out_shape=tuple(shapes),
        input_output_aliases=aliases, scratch_shapes=scratch,
    )(*ops)
    out, oo = [], 0
    for k in range(len(exs)):
        out.append(list(res[oo:oo + n_out[k]]))
        oo += n_out[k]
    return out


def _ex_all_gather(slabs, halved, part=(0, 1)):
    n = len(slabs)

    def rows(a, cc):
        if not halved[a]:
            return slice(None)
        pr = slabs[a].shape[1] // part[1]
        return pl.ds(part[0] * pr + cc * (pr // 2), pr // 2)

    def ici(bufs, scr, a, j, chip, c, mine):
        px, py = chip
        x, y, _, _ = _place()
        qs = 2 * x + y if mine else 2 * px + py
        piece = bufs[a].at[qs, rows(a, c)]
        return _remote(piece, piece, scr[0].at[3 * a + j], scr[1].at[3 * a + j], (px, py, c))

    def d2d(bufs, scr, a, j, chip, cc):
        px, py = chip
        x, y, c, _ = _place()
        piece = bufs[a].at[2 * px + py, rows(a, cc)]
        return _remote(piece, piece, scr[2].at[3 * a + j], scr[3].at[3 * a + j], (x, y, 1 - c))

    def start(ins, outs, scr):
        _, _, c, chips = _place()
        for j, chip in enumerate(chips):
            for a in range(n):
                ici(outs, scr, a, j, chip, c, True).start()

    def finish(ins, outs, scr):
        _, _, c, chips = _place()
        for j, chip in enumerate(chips):
            for a in range(n):
                ici(outs, scr, a, j, chip, c, False).wait_recv()
                if halved[a]:
                    d2d(outs, scr, a, j, chip, c).start()
        for j, chip in enumerate(chips):
            for a in range(n):
                if halved[a]:
                    d2d(outs, scr, a, j, chip, 1 - c).wait_recv()
        for j, chip in enumerate(chips):
            for a in range(n):
                ici(outs, scr, a, j, chip, c, True).wait_send()
                if halved[a]:
                    d2d(outs, scr, a, j, chip, c).wait_send()

    return _Exchange(list(slabs), [jax.ShapeDtypeStruct(s.shape, s.dtype) for s in slabs],
                     {a: a for a in range(n)}, [pltpu.SemaphoreType.DMA((3 * n,))] * 4, start, finish)


def _ex_gather_relay(slabs, mid_at=0.5):
    n = len(slabs)

    def rows(a, cc):
        hr = slabs[a].shape[1] // 2
        return pl.ds(cc * hr, hr)

    def peers():
        x, y, c, _ = _place()
        nbr0 = ((x + c) % 2, (y + 1 - c) % 2)
        nbr1 = ((x + 1 - c) % 2, (y + c) % 2)
        return x, y, c, nbr0, nbr1, (1 - x, 1 - y)

    def ici(bufs, scr, a, k, chip, dev, cc):
        _, _, c, _, _, _ = peers()
        piece = bufs[a].at[2 * chip[0] + chip[1], rows(a, cc)]
        return _remote(piece, piece, scr[0].at[3 * a + k], scr[1].at[3 * a + k], (dev[0], dev[1], c))

    def d2d(bufs, scr, a, k, chip, cc):
        x, y, c, _, _, _ = peers()
        piece = bufs[a].at[2 * chip[0] + chip[1], rows(a, cc)]
        return _remote(piece, piece, scr[2].at[3 * a + k], scr[3].at[3 * a + k], (x, y, 1 - c))

    def start(ins, outs, scr):
        x, y, c, nbr0, nbr1, _ = peers()
        for a in range(n):
            ici(outs, scr, a, 0, (x, y), nbr0, c).start()
            ici(outs, scr, a, 1, (x, y), nbr1, c).start()

    def mid(ins, outs, scr):
        x, y, c, nbr0, nbr1, diag = peers()
        for a in range(n):
            ici(outs, scr, a, 0, nbr0, nbr0, c).wait_recv()
            ici(outs, scr, a, 2, nbr0, nbr1, c).start()
            d2d(outs, scr, a, 0, nbr0, c).start()
        for a in range(n):
            ici(outs, scr, a, 1, nbr1, nbr1, c).wait_recv()
            d2d(outs, scr, a, 1, nbr1, c).start()

    def finish(ins, outs, scr):
        x, y, c, nbr0, nbr1, diag = peers()
        for a in range(n):
            ici(outs, scr, a, 2, diag, nbr1, c).wait_recv()
            d2d(outs, scr, a, 2, diag, c).start()
        for a in range(n):
            d2d(outs, scr, a, 0, nbr1, 1 - c).wait_recv()
            d2d(outs, scr, a, 1, nbr0, 1 - c).wait_recv()
            d2d(outs, scr, a, 2, diag, 1 - c).wait_recv()
        for a in range(n):
            ici(outs, scr, a, 0, (x, y), nbr0, c).wait_send()
            ici(outs, scr, a, 1, (x, y), nbr1, c).wait_send()
            ici(outs, scr, a, 2, nbr0, nbr1, c).wait_send()
            d2d(outs, scr, a, 0, nbr0, c).wait_send()
            d2d(outs, scr, a, 1, nbr1, c).wait_send()
            d2d(outs, scr, a, 2, diag, c).wait_send()

    return _Exchange(list(slabs), [jax.ShapeDtypeStruct(s.shape, s.dtype) for s in slabs],
                     {a: a for a in range(n)}, [pltpu.SemaphoreType.DMA((3 * n,))] * 4, start, finish, mid, mid_at)


def _ex_to_sibling(grads):
    n = len(grads)

    def copy(ins, outs, scr, a):
        x, y, c, _ = _place()
        hr = grads[a].shape[1] // 2
        return _remote(ins[a].at[:, pl.ds((1 - c) * hr, hr), :], outs[a], scr[0].at[a], scr[1].at[a], (x, y, 1 - c))

    def start(ins, outs, scr):
        for a in range(n):
            copy(ins, outs, scr, a).start()

    def finish(ins, outs, scr):
        for a in range(n):
            copy(ins, outs, scr, a).wait()

    out_shape = [jax.ShapeDtypeStruct((g.shape[0], g.shape[1] // 2, g.shape[2]), g.dtype) for g in grads]
    return _Exchange(list(grads), out_shape, {}, [pltpu.SemaphoreType.DMA((n,))] * 2, start, finish)


def _ex_to_owner(parts, part=(0, 1), landing=None):
    n = len(parts)

    def copy(ins, outs, scr, a, j, chip):
        _, _, c, _ = _place()
        px, py = chip
        pr = parts[a].shape[1] // part[1]
        rows = pl.ds(part[0] * pr, pr)
        return _remote(ins[a].at[2 * px + py, rows], outs[a].at[j, rows], scr[0].at[3 * a + j],
                       scr[1].at[3 * a + j], (px, py, c))

    def start(ins, outs, scr):
        for j, chip in enumerate(_place()[3]):
            for a in range(n):
                copy(ins, outs, scr, a, j, chip).start()

    def finish(ins, outs, scr):
        for j, chip in enumerate(_place()[3]):
            for a in range(n):
                copy(ins, outs, scr, a, j, chip).wait()

    out_shape = [jax.ShapeDtypeStruct((3,) + p.shape[1:], p.dtype) for p in parts]
    operands, aliases = list(parts), {}
    if landing is not None:
        operands, aliases = operands + list(landing), {n + a: a for a in range(n)}
    return _Exchange(operands, out_shape, aliases, [pltpu.SemaphoreType.DMA((3 * n,))] * 2, start, finish)


def _ex_share_halves(bufs):
    n = len(bufs)

    def copy(outs, scr, a, cc):
        x, y, c, _ = _place()
        hr = bufs[a].shape[0] // 2
        piece = outs[a].at[pl.ds(cc * hr, hr), :]
        return _remote(piece, piece, scr[0].at[a], scr[1].at[a], (x, y, 1 - c))

    def start(ins, outs, scr):
        c = _place()[2]
        for a in range(n):
            copy(outs, scr, a, c).start()

    def finish(ins, outs, scr):
        c = _place()[2]
        for a in range(n):
            copy(outs, scr, a, c).wait_send()
            copy(outs, scr, a, 1 - c).wait_recv()

    return _Exchange(list(bufs), [jax.ShapeDtypeStruct(b.shape, b.dtype) for b in bufs], {a: a for a in range(n)},
                     [pltpu.SemaphoreType.DMA((n,))] * 2, start, finish)


def _ex_gather_small(arrs):
    n = len(arrs)

    def peer_of(m):
        x, y, c, _ = _place()
        return (1 - x if m & 4 else x, 1 - y if m & 2 else y, 1 - c if m & 1 else c)

    def own(ins, outs, scr, a):
        x, y, c, _ = _place()
        return pltpu.make_async_copy(ins[a], outs[a].at[4 * x + 2 * y + c], scr[2].at[a])

    def start(ins, outs, scr):
        x, y, c, _ = _place()
        for a in range(n):
            own(ins, outs, scr, a).start()
        for m in range(1, N_DEV):
            for a in range(n):
                k = (N_DEV - 1) * a + m - 1
                _remote(ins[a], outs[a].at[4 * x + 2 * y + c], scr[0].at[k], scr[1].at[k], peer_of(m)).start()

    def finish(ins, outs, scr):
        for a in range(n):
            own(ins, outs, scr, a).wait()
        for m in range(1, N_DEV):
            px, py, pc = peer_of(m)
            for a in range(n):
                k = (N_DEV - 1) * a + m - 1
                slot = outs[a].at[4 * px + 2 * py + pc]
                cp = _remote(ins[a], slot, scr[0].at[k], scr[1].at[k], (px, py, pc))
                cp.wait_send()
                cp.wait_recv()

    out_shape = [jax.ShapeDtypeStruct((N_DEV,) + a.shape, a.dtype) for a in arrs]
    return _Exchange(list(arrs), out_shape, {},
                     [pltpu.SemaphoreType.DMA(((N_DEV - 1) * n,))] * 2 + [pltpu.SemaphoreType.DMA((n,))], start, finish)


def _div_tile(n, want):
    best = None
    for t in range(8, min(n, want) + 1, 8):
        if n % t == 0:
            best = t
    assert best is not None, n
    return best


def _cast_into_slab(name, w, place, dtype):
    r, cc = w.shape
    tr = r if r * cc <= 128 * 1024 else _div_tile(r, 256)

    def body(s_ref, w_ref, o_ref):
        o_ref[...] = w_ref[...].astype(o_ref.dtype)

    return _pallas(
        body, name=name,
        grid_spec=pltpu.PrefetchScalarGridSpec(
            num_scalar_prefetch=1, grid=(r // tr,),
            in_specs=[pl.BlockSpec((tr, cc), lambda i, s: (i, 0))],
            out_specs=pl.BlockSpec((None, tr, cc), lambda i, s: (s[0], i, 0))),
        out_shape=jax.ShapeDtypeStruct((N_CHIPS, r, cc), dtype), compiler_params=_cp("parallel"),
    )(place, w)


def _add_half(name, g, rcv, place):
    nq, r, cc = g.shape
    hr = r // 2

    def body(s_ref, g_ref, r_ref, o_ref):
        o_ref[...] = (g_ref[...] + r_ref[...]).astype(o_ref.dtype)

    spec = pl.BlockSpec((None, hr, cc), lambda i, s: (i, 0, 0))
    return _pallas(
        body, name=name,
        grid_spec=pltpu.PrefetchScalarGridSpec(
            num_scalar_prefetch=1, grid=(nq,),
            in_specs=[pl.BlockSpec((None, hr, cc), lambda i, s: (i, s[1], 0)), spec], out_specs=spec),
        out_shape=jax.ShapeDtypeStruct((nq, hr, cc), BF16), compiler_params=_cp("parallel"),
    )(place, g, rcv)


def _sum_owner(name, part, rcv, place):
    _, hr, cc = part.shape
    tr = _div_tile(hr, 128)
    nb = hr // tr

    def body(s_ref, p_ref, r_ref, o_ref):
        o_ref[...] = ((p_ref[...].astype(F32) + r_ref[0].astype(F32)) + r_ref[1].astype(F32)) + r_ref[2].astype(F32)

    return _pallas(
        body, name=name,
        grid_spec=pltpu.PrefetchScalarGridSpec(
            num_scalar_prefetch=1, grid=(nb,),
            in_specs=[pl.BlockSpec((None, tr, cc), lambda i, s: (s[0], i, 0)),
                      pl.BlockSpec((3, tr, cc), lambda i, s: (0, i, 0))],
            out_specs=pl.BlockSpec((tr, cc), lambda i, s: (s[1] * nb + i, 0))),
        out_shape=jax.ShapeDtypeStruct((2 * hr, cc), F32), compiler_params=_cp("parallel"),
    )(place, part, rcv)


def _sum_small(gathered, local, place):
    n = len(gathered)

    def body(s_ref, *refs):
        g_refs, l_refs, o_refs = refs[:n], refs[n:2 * n], refs[2 * n:]
        me = s_ref[2]
        for g_ref, l_ref, o_ref in zip(g_refs, l_refs, o_refs):
            acc = None
            for d in range(N_DEV):
                term = jnp.where(me == d, l_ref[...], g_ref[d])
                acc = term if acc is None else acc + term
            o_ref[...] = acc

    def whole(shape):
        return pl.BlockSpec(shape, lambda i, s, nd=len(shape): (0,) * nd)

    return _pallas(
        body, name="sum_small",
        grid_spec=pltpu.PrefetchScalarGridSpec(
            num_scalar_prefetch=1, grid=(1,),
            in_specs=[whole(g.shape) for g in gathered] + [whole(a.shape) for a in local],
            out_specs=tuple(whole(a.shape) for a in local)),
        out_shape=tuple(jax.ShapeDtypeStruct(a.shape, a.dtype) for a in local), compiler_params=_cp("arbitrary"),
    )(place, *gathered, *local)


def _adamw(name, w, g, m, v):
    r, cc = w.shape
    tr = r if r * cc <= 128 * 1024 else _div_tile(r, 256)

    def body(w_ref, g_ref, m_ref, v_ref, d_ref, mo_ref, vo_ref, go_ref):
        gv = g_ref[...]
        go_ref[...] = gv
        mn = ADAM_B1 * m_ref[...] + (1.0 - ADAM_B1) * gv
        vn = ADAM_B2 * v_ref[...] + (1.0 - ADAM_B2) * (gv * gv)
        m_hat = mn / (1.0 - ADAM_B1 ** ADAM_STEP)
        v_hat = vn / (1.0 - ADAM_B2 ** ADAM_STEP)
        d_ref[...] = -ADAM_LR * (m_hat / (jnp.sqrt(v_hat) + ADAM_EPS) + ADAM_WD * w_ref[...])
        mo_ref[...] = mn
        vo_ref[...] = vn

    spec = pl.BlockSpec((tr, cc), lambda i: (i, 0))
    sd = jax.ShapeDtypeStruct((r, cc), F32)
    return _pallas(
        body, name=name, grid=(r // tr,), in_specs=[spec] * 4, out_specs=(spec,) * 4, out_shape=(sd,) * 4,
        compiler_params=_cp("parallel"),
    )(w, g, m, v)


_BIG = ("w_in", "w_up", "w_branch", "w_mem_kv", "w_out", "w_down")
_BIG_SHARD_SHAPE = {"w_in": (1024, 1664), "w_up": (1024, 1408), "w_branch": (1536, 256),
                    "w_mem_kv": (256, 1024), "w_out": (256, 1024), "w_down": (704, 1024)}
_SMALL_SHAPE = {"norm1_g": (1, D_MODEL), "ln_v_g": (1, GM_WIDTH), "ln_v_b": (1, GM_WIDTH),
                "w_spatial": (GM_GROUPS * GM_CHUNK, GM_CHUNK), "b_spatial": (GM_GROUPS, GM_CHUNK),
                "lb_logits": (2, HG_HEADS * HG_DIM), "hgrn_norm_g": (1, HG_DIM), "mem_norm_g": (1, D_MODEL),
                "norm2_g": (1, D_MODEL), "conv_w": (3, D_FF), "conv_b": (1, D_FF), "final_g": (1, D_MODEL)}
_SMALL_EARLY = tuple(n for n in _SMALL_SHAPE if n != "norm1_g")
_PARAM_ORDER = ("norm1_g", "w_in", "ln_v_g", "ln_v_b", "w_spatial", "b_spatial", "lb_logits", "hgrn_norm_g",
                "mem_norm_g", "w_mem_kv", "w_branch", "w_out", "norm2_g", "w_up", "conv_w", "conv_b", "w_down",
                "final_g")


def _adamw_small(ws, gs, ms, vs):
    n = len(ws)

    def body(*refs):
        w_refs, g_refs, m_refs, v_refs = refs[:n], refs[n:2 * n], refs[2 * n:3 * n], refs[3 * n:4 * n]
        d_refs, mo_refs, vo_refs = refs[4 * n:5 * n], refs[5 * n:6 * n], refs[6 * n:]
        for k in range(n):
            gv = g_refs[k][...]
            mn = ADAM_B1 * m_refs[k][...] + (1.0 - ADAM_B1) * gv
            vn = ADAM_B2 * v_refs[k][...] + (1.0 - ADAM_B2) * (gv * gv)
            m_hat = mn / (1.0 - ADAM_B1 ** ADAM_STEP)
            v_hat = vn / (1.0 - ADAM_B2 ** ADAM_STEP)
            d_refs[k][...] = -ADAM_LR * (m_hat / (jnp.sqrt(v_hat) + ADAM_EPS) + ADAM_WD * w_refs[k][...])
            mo_refs[k][...] = mn
            vo_refs[k][...] = vn

    specs = [pl.BlockSpec(a.shape, lambda i, nd=a.ndim: (0,) * nd) for a in ws]
    shapes = tuple(jax.ShapeDtypeStruct(a.shape, F32) for a in ws)
    res = _pallas(
        body, name="adamw_small", grid=(1,), in_specs=specs * 4, out_specs=tuple(specs * 3), out_shape=shapes * 3,
        compiler_params=_cp("arbitrary"),
    )(*ws, *gs, *ms, *vs)
    return res[:n], res[n:2 * n], res[2 * n:]


class _Comm:
    _ROW_SHARDED = ("w_mem_kv", "w_out", "w_down")

    def __init__(self, slabs, place):
        self.slabs, self.place = slabs, place
        self.full, self.raw, self.parts, self.landing, self.bufs, self.done = {}, {}, {}, {}, {}, {}

    def w(self, name):
        a = self.full[name]
        if name in self._ROW_SHARDED:
            return a.reshape(-1, a.shape[-1])
        if name == "conv_w":
            return jnp.transpose(a, (1, 0, 2)).reshape(3, 1, D_FF)
        return a

    sends = True

    def grad(self, name, arr, from_sibling=None):
        self.raw[name] = arr.reshape((N_CHIPS, -1, arr.shape[-1]))
        if from_sibling is not None:
            self.parts[name] = _add_half("rs_add_" + name, self.raw[name], from_sibling, self.place)

    def small_grads(self, arrays):
        self.small_local = list(arrays)

    def carry(self, tag, call):
        plan = self._plan(tag)
        if not plan:
            return call(())
        out, carried = call([ex for ex, _ in plan])
        for (_, deliver), res in zip(plan, carried):
            deliver(res)
        return out

    def finish(self, last_small):
        ex, deliver = self._share(["w_out", "w_branch", "w_mem_kv", "w_in"])
        shared, small = _run_exchanges("share_and_gather_last", [ex, _ex_gather_small(last_small)])
        deliver(shared)
        return self.done, self.small_local + list(last_small), self.small_everyone + small

    def _plan(self, tag):
        if tag == "norm1":
            def deliver(res):
                self.full["w_in"] = res[0]

            return [(_ex_gather_relay([self.slabs["w_in"]]), deliver)]
        if tag == "in_proj":
            return [self._gather_relay(["w_branch", "w_out", "w_mem_kv", "w_down"], 0.6), self._gather(["conv_w"])]
        if tag == "hgrn_fwd":
            return [self._gather_relay(["w_up"], 0.8)]
        if tag == "d_h2":
            return [self._to_sibling(["w_down", "w_up"])]
        if tag == "merge_bwd":
            return [self._to_owner(["w_up"], (0, 2))]
        if tag == "hgrn_bwd":
            return [self._to_owner(["w_down"]), self._to_owner(["w_up"], (1, 2)),
                    self._to_sibling(["w_out", "w_branch", "w_mem_kv"])]
        if tag == "g_w_in":
            def keep(res):
                self.small_everyone = res

            return [self._to_owner(["w_out", "w_branch", "w_mem_kv"]), (_ex_gather_small(self.small_local), keep)]
        if tag == "d_h":
            return [self._to_owner(["w_in"]), self._share(["w_down", "w_up"])]
        return []

    def _gather(self, names, part=(0, 1)):
        def deliver(res):
            self.slabs.update(zip(names, res))
            self.full.update(zip(names, res))

        return _ex_all_gather([self.slabs[n] for n in names], [n != "conv_w" for n in names], part), deliver

    def _gather_relay(self, names, mid_at):
        return _ex_gather_relay([self.slabs[n] for n in names], mid_at), lambda res: self.full.update(zip(names, res))

    def _to_sibling(self, names):
        def deliver(res):
            for n, r in zip(names, res):
                self.parts[n] = _add_half("rs_add_" + n, self.raw[n], r, self.place)

        return _ex_to_sibling([self.raw[n] for n in names]), deliver

    def _to_owner(self, names, part=(0, 1)):
        def deliver(res):
            for n, r in zip(names, res):
                if part[0] + 1 < part[1]:
                    self.landing[n] = r
                else:
                    self.bufs[n] = _sum_owner("rs_sum_" + n, self.parts[n], r, self.place)

        landing = [self.landing[n] for n in names] if part[0] else None
        return _ex_to_owner([self.parts[n] for n in names], part, landing), deliver

    def _share(self, names):
        return _ex_share_halves([self.bufs[n] for n in names]), lambda res: self.done.update(zip(names, res))


def kernel(x, mem, norm1_g, w_in, ln_v_g, ln_v_b, w_spatial, b_spatial, lb_logits, hgrn_norm_g, mem_norm_g, w_mem_kv, w_branch, w_out, norm2_g, w_up, conv_w, conv_b, w_down, final_g, loss_target, m_norm1_g, m_w_in, m_ln_v_g, m_ln_v_b, m_w_spatial, m_b_spatial, m_lb_logits, m_hgrn_norm_g, m_mem_norm_g, m_w_mem_kv, m_w_branch, m_w_out, m_norm2_g, m_w_up, m_conv_w, m_conv_b, m_w_down, m_final_g, v_norm1_g, v_w_in, v_ln_v_g, v_ln_v_b, v_w_spatial, v_b_spatial, v_lb_logits, v_hgrn_norm_g, v_mem_norm_g, v_w_mem_kv, v_w_branch, v_w_out, v_norm2_g, v_w_up, v_conv_w, v_conv_b, v_w_down, v_final_g):
    w = dict(norm1_g=norm1_g, w_in=w_in, ln_v_g=ln_v_g, ln_v_b=ln_v_b, w_spatial=w_spatial, b_spatial=b_spatial,
             lb_logits=lb_logits, hgrn_norm_g=hgrn_norm_g, mem_norm_g=mem_norm_g, w_mem_kv=w_mem_kv,
             w_branch=w_branch, w_out=w_out, norm2_g=norm2_g, w_up=w_up, conv_w=conv_w, conv_b=conv_b,
             w_down=w_down, final_g=final_g)
    mom = dict(norm1_g=m_norm1_g, w_in=m_w_in, ln_v_g=m_ln_v_g, ln_v_b=m_ln_v_b, w_spatial=m_w_spatial,
               b_spatial=m_b_spatial, lb_logits=m_lb_logits, hgrn_norm_g=m_hgrn_norm_g, mem_norm_g=m_mem_norm_g,
               w_mem_kv=m_w_mem_kv, w_branch=m_w_branch, w_out=m_w_out, norm2_g=m_norm2_g, w_up=m_w_up,
               conv_w=m_conv_w, conv_b=m_conv_b, w_down=m_w_down, final_g=m_final_g)
    var = dict(norm1_g=v_norm1_g, w_in=v_w_in, ln_v_g=v_ln_v_g, ln_v_b=v_ln_v_b, w_spatial=v_w_spatial,
               b_spatial=v_b_spatial, lb_logits=v_lb_logits, hgrn_norm_g=v_hgrn_norm_g, mem_norm_g=v_mem_norm_g,
               w_mem_kv=v_w_mem_kv, w_branch=v_w_branch, w_out=v_w_out, norm2_g=v_norm2_g, w_up=v_w_up,
               conv_w=v_conv_w, conv_b=v_conv_b, w_down=v_w_down, final_g=v_final_g)
    B, S, D = x.shape
    T = B * S
    ci = lax.axis_index("c")
    q = 2 * lax.axis_index("x") + lax.axis_index("y")
    place = jnp.stack([q, ci, 2 * q + ci]).astype(jnp.int32)

    shards = {n: w[n].reshape(_BIG_SHARD_SHAPE[n]) for n in _BIG}
    slabs = {"w_in": _cast_into_slab("slab_w_in", shards.pop("w_in"), place, BF16),
             "conv_w": _cast_into_slab("slab_conv_w", conv_w[0], place, F32)}
    comm = _Comm(slabs, place)
    p = dict(
        cast_beside_norm1=shards,
        norm1_g=norm1_g, ln_v_g=ln_v_g, ln_v_b=ln_v_b, w_spatial=w_spatial[0],
        b_spatial=b_spatial.reshape(GM_GROUPS, GM_CHUNK, 1), lb_logits=lb_logits, hgrn_norm_g=hgrn_norm_g,
        mem_norm_g=mem_norm_g, norm2_g=norm2_g, conv_b=conv_b, final_g=final_g.reshape(1, D))

    loss, grad_x, g = _local_step(x.reshape(T, D), mem.reshape(B * MEM_LEN, D), loss_target.reshape(T, D), p, comm,
                                  B, S)

    shard_grads, local_small, everyone = comm.finish([g["norm1_g"]])
    summed = _sum_small(everyone, local_small, place)
    small_names = list(_SMALL_EARLY) + ["norm1_g"]
    total = dict(zip(_SMALL_EARLY, summed))
    loss_total, total["norm1_g"] = summed[len(_SMALL_EARLY)][0, 0], summed[-1]

    grads, delta, new_m, new_v = {}, {}, {}, {}
    for n in _BIG:
        shp = _BIG_SHARD_SHAPE[n]
        delta[n], new_m[n], new_v[n], grads[n] = _adamw("adamw_" + n, w[n].reshape(shp), shard_grads[n],
                                                        mom[n].reshape(shp), var[n].reshape(shp))
    cw_shard = D_FF // N_CHIPS
    total["conv_w"] = lax.dynamic_slice(total["conv_w"], (0, q * cw_shard), (3, cw_shard)).reshape(3, 1, cw_shard)

    def flat2d(d, n):
        return d[n].reshape(total[n].shape)

    upd = _adamw_small([flat2d(w, n) for n in small_names], [total[n] for n in small_names],
                       [flat2d(mom, n) for n in small_names], [flat2d(var, n) for n in small_names])
    for k, n in enumerate(small_names):
        grads[n], delta[n], new_m[n], new_v[n] = total[n], upd[0][k], upd[1][k], upd[2][k]

    def shaped(d):
        return [d[n].reshape(w[n].shape) for n in _PARAM_ORDER]

    return (loss_total, grad_x.reshape(B, S, D), *shaped(grads), *shaped(delta), *shaped(new_m), *shaped(new_v))
```

```python
import functools
import math

import jax
import jax.numpy as jnp
from jax import lax
from jax.experimental import pallas as pl
from jax.experimental.pallas import tpu as pltpu

F32 = jnp.float32
BF16 = jnp.bfloat16
EPS = 1e-6

D_MODEL = 1024
MEM_LEN = 256
GM_WIDTH = 512
GM_CHUNK = 128
GM_GROUPS = 4
HG_HEADS = 4
HG_DIM = 128
HG_CHUNK = 64
XA_HEADS = 4
XA_DIM = 128
BR_WIDTH = 512
D_FF = 2816
IN_WIDTH = 6656
N_CHIPS = 4
N_DEV = 8

ADAM_LR = 0.001
ADAM_B1 = 0.9
ADAM_B2 = 0.999
ADAM_EPS = 1e-08
ADAM_WD = 0.01
ADAM_STEP = 10

COL_ZU, COL_ZV, COL_HQ, COL_HF, COL_HI, COL_HG, COL_XQ = 0, 1, 2, 3, 4, 5, 6
COL_GATE0 = 3584

VMEM_LIMIT_BYTES = 48 * 1024 * 1024
MESH_ID = pl.DeviceIdType.MESH


def _cp(*sem):
    return pltpu.CompilerParams(dimension_semantics=sem, vmem_limit_bytes=VMEM_LIMIT_BYTES)


def _pallas(body, *, out_shape, **kw):
    def pin(s):
        return pltpu.HBM(s.shape, s.dtype) if isinstance(s, jax.ShapeDtypeStruct) else s

    out_shape = tuple(pin(s) for s in out_shape) if isinstance(out_shape, (tuple, list)) else pin(out_shape)
    call = pl.pallas_call(body, out_shape=out_shape, **kw)

    def run(*operands):
        return call(*[pltpu.with_memory_space_constraint(o, pltpu.HBM) if jnp.issubdtype(o.dtype, jnp.floating)
                      else o for o in operands])

    return run


def _dot(a, b):
    return lax.dot_general(a.astype(BF16), b.astype(BF16), (((1,), (0,)), ((), ())), preferred_element_type=F32)


def _dot_nt(a, b):
    return lax.dot_general(a.astype(BF16), b.astype(BF16), (((1,), (1,)), ((), ())), preferred_element_type=F32)


def _dot_tn(a, b):
    return lax.dot_general(a.astype(BF16), b.astype(BF16), (((0,), (0,)), ((), ())), preferred_element_type=F32)


def _dot_01(mask01, x):
    hi = x.astype(BF16)
    r1 = x - hi.astype(F32)
    mid = r1.astype(BF16)
    lo = (r1 - mid.astype(F32)).astype(BF16)
    m = mask01.astype(BF16)
    dn = (((1,), (0,)), ((), ()))
    return (lax.dot_general(m, hi, dn, preferred_element_type=F32)
            + lax.dot_general(m, mid, dn, preferred_element_type=F32)
            + lax.dot_general(m, lo, dn, preferred_element_type=F32))


def _sigmoid(z):
    return 1.0 / (1.0 + jnp.exp(-z))


_GELU_C = math.sqrt(2.0 / math.pi)


def _gelu_and_grad(z):
    inner = _GELU_C * (z + 0.044715 * z * z * z)
    t = jnp.tanh(inner)
    val = 0.5 * z * (1.0 + t)
    grad = 0.5 * (1.0 + t) + 0.5 * z * (1.0 - t * t) * _GELU_C * (1.0 + 3.0 * 0.044715 * z * z)
    return val, grad


def _row_tile(n, want=512):
    t = min(want, n)
    assert n % t == 0
    return t


def _pcall(body, operands, *, name, grid, in_specs, out_specs, out_shape, scratch_shapes=(), semantics, riders=(),
           prefetch=None):
    single = not isinstance(out_shape, (tuple, list))
    out_specs = (out_specs,) if single else tuple(out_specs)
    out_shape = (out_shape,) if single else tuple(out_shape)
    n_pre = 0 if prefetch is None else 1

    def call(fn, ins_, outs_, shapes_, scr_, ops, sem, aliases):
        if prefetch is None:
            return _pallas(fn, name=name, grid=grid, in_specs=ins_, out_specs=outs_, out_shape=shapes_,
                           scratch_shapes=scr_, input_output_aliases=aliases, compiler_params=_cp(*sem))(*ops)
        spec = pltpu.PrefetchScalarGridSpec(num_scalar_prefetch=1, grid=grid, in_specs=ins_, out_specs=outs_,
                                            scratch_shapes=scr_)
        return _pallas(fn, name=name, grid_spec=spec, out_shape=shapes_, input_output_aliases=aliases,
                       compiler_params=_cp(*sem))(prefetch, *ops)

    if not riders:
        res = call(body, list(in_specs), out_specs, out_shape, list(scratch_shapes), operands, semantics, {})
        return (res[0] if single else res), []
    n_in, n_out, n_scr = len(in_specs), len(out_shape), len(scratch_shapes)
    ex_in = [len(ex.operands) for ex in riders]
    ex_out = [len(ex.out_shape) for ex in riders]
    ex_scr = [len(ex.scratch) for ex in riders]
    tot_in, tot_out = n_in + sum(ex_in), n_out + sum(ex_out)

    def wrapped(*refs):
        pre, refs = refs[:n_pre], refs[n_pre:]
        ins, outs, scr = refs[:tot_in], refs[tot_in:tot_in + tot_out], refs[tot_in + tot_out:]
        ids = [pl.program_id(d) for d in range(len(grid))]
        first = functools.reduce(lambda p, t: p & t, [i == 0 for i in ids])
        last = functools.reduce(lambda p, t: p & t, [i == n - 1 for i, n in zip(ids, grid)])
        parts, oi, oo, os_ = [], n_in, n_out, n_scr
        for k in range(len(riders)):
            parts.append((ins[oi:oi + ex_in[k]], outs[oo:oo + ex_out[k]], scr[os_:os_ + ex_scr[k]]))
            oi, oo, os_ = oi + ex_in[k], oo + ex_out[k], os_ + ex_scr[k]

        @pl.when(first)
        def _():
            for ex, part in zip(riders, parts):
                ex.start(*part)

        step, total = 0, 1
        for i, n in zip(ids, grid):
            step, total = step * n + i, total * n
        for ex, part in zip(riders, parts):
            if ex.mid is not None:
                @pl.when(step == min(total - 1, int(total * ex.mid_at)))
                def _(ex=ex, part=part):
                    ex.mid(*part)

        body(*pre, *ins[:n_in], *outs[:n_out], *scr[:n_scr])

        @pl.when(last)
        def _():
            for ex, part in zip(riders, parts):
                ex.finish(*part)

    aliases, oi, oo = {}, n_in, n_out
    all_ops, all_shapes, all_scr = list(operands), list(out_shape), list(scratch_shapes)
    for k, ex in enumerate(riders):
        aliases.update({n_pre + oi + a: oo + b for a, b in ex.aliases.items()})
        oi, oo = oi + ex_in[k], oo + ex_out[k]
        all_ops += list(ex.operands)
        all_shapes += [pltpu.HBM(s.shape, s.dtype) for s in ex.out_shape]
        all_scr += list(ex.scratch)
    res = call(wrapped, list(in_specs) + [HBM_SPEC] * sum(ex_in), out_specs + (HBM_SPEC,) * sum(ex_out),
               tuple(all_shapes), all_scr, all_ops, ["arbitrary"] * len(grid), aliases)
    own = res[0] if single else tuple(res[:n_out])
    carried, oo = [], n_out
    for k in range(len(riders)):
        carried.append(list(res[oo:oo + ex_out[k]]))
        oo += ex_out[k]
    return own, carried


def _carried(out, carried, riders):
    return (out, carried) if riders else out


def _matmul(name, operands, *, grid, in_specs, o_spec, out_shape, out_dtype, dims, riders=()):
    nk = grid[2]
    assert nk == 1 or out_dtype == F32

    def body(a_ref, b_ref, o_ref):
        part = lax.dot_general(a_ref[...].astype(BF16), b_ref[...].astype(BF16), (dims, ((), ())),
                               preferred_element_type=F32)
        if nk == 1:
            o_ref[...] = part.astype(o_ref.dtype)
        else:
            k = pl.program_id(2)

            @pl.when(k == 0)
            def _():
                o_ref[...] = part

            @pl.when(k > 0)
            def _():
                o_ref[...] += part

    out, carried = _pcall(body, operands, name=name, grid=grid, in_specs=in_specs, out_specs=o_spec,
                          out_shape=jax.ShapeDtypeStruct(out_shape, out_dtype),
                          semantics=("parallel", "parallel", "arbitrary"), riders=riders)
    return (out, carried) if riders else out


NN = ((1,), (0,))
NT = ((1,), (1,))
TN = ((0,), (0,))
_TN_TOKENS = 4096


def _mm_cs(name, a, w, out_dtype, riders=()):
    M, K = a.shape
    nq, _, wd = w.shape
    tm = _row_tile(M)
    return _matmul(name, (a, w), grid=(nq, M // tm, 1),
                   in_specs=[pl.BlockSpec((tm, K), lambda j, i, k: (i, 0)),
                             pl.BlockSpec((None, K, wd), lambda j, i, k: (j, 0, 0))],
                   o_spec=pl.BlockSpec((tm, wd), lambda j, i, k: (i, j)),
                   out_shape=(M, nq * wd), out_dtype=out_dtype, dims=NN, riders=riders)


def _mm_rs(name, a, w, out_dtype):
    M, K = a.shape
    N = w.shape[1]
    tm = _row_tile(M)
    return _matmul(name, (a, w), grid=(M // tm, 1, 1),
                   in_specs=[pl.BlockSpec((tm, K), lambda i, j, k: (i, 0)), pl.BlockSpec((K, N), lambda i, j, k: (0, 0))],
                   o_spec=pl.BlockSpec((tm, N), lambda i, j, k: (i, 0)),
                   out_shape=(M, N), out_dtype=out_dtype, dims=NN)


def _mm_nt_rs(name, g, w, out_dtype, riders=()):
    M, N = g.shape
    K = w.shape[0]
    to = K
    tm = _row_tile(M)
    return _matmul(name, (g, w), grid=(M // tm, K // to, 1),
                   in_specs=[pl.BlockSpec((tm, N), lambda i, j, k: (i, 0)),
                             pl.BlockSpec((to, N), lambda i, j, k: (j, 0))],
                   o_spec=pl.BlockSpec((tm, to), lambda i, j, k: (i, j)),
                   out_shape=(M, K), out_dtype=out_dtype, dims=NT, riders=riders)


def _mm_nt_cs(name, g, w, out_dtype, riders=(), stacked=False, norm_bwd=None):
    M = g.shape[-2]
    nq, K, wd = w.shape
    tm = _row_tile(M, 256)

    def product(g_ref, w_ref):
        acc = None
        for q in range(nq):
            gq = g_ref[q // 2, :, (q % 2) * wd:(q % 2 + 1) * wd] if stacked else g_ref[:, q * wd:(q + 1) * wd]
            part = _dot_nt(gq, w_ref[q])
            acc = part if acc is None else acc + part
        return acc

    def body(g_ref, w_ref, o_ref):
        o_ref[...] = product(g_ref, w_ref).astype(o_ref.dtype)

    def body_norm(g_ref, w_ref, x_ref, gain_ref, dr_ref, dx_ref, dg_ref):
        @pl.when(pl.program_id(0) == 0)
        def _():
            dg_ref[...] = jnp.zeros_like(dg_ref)

        dx, dg = _rms_bwd_rows(x_ref[...], gain_ref[...], product(g_ref, w_ref))
        dg_ref[...] += dg
        dx_ref[...] = dx + dr_ref[...]

    g_spec = (pl.BlockSpec((2, tm, 2 * wd), lambda i: (0, i, 0)) if stacked
              else pl.BlockSpec((tm, nq * wd), lambda i: (i, 0)))
    w_spec = pl.BlockSpec((nq, K, wd), lambda i: (0, 0, 0))
    row = pl.BlockSpec((tm, K), lambda i: (i, 0))
    if norm_bwd is None:
        return _carried(*_pcall(
            body, (g, w), name=name, grid=(M // tm,), in_specs=[g_spec, w_spec], out_specs=row,
            out_shape=jax.ShapeDtypeStruct((M, K), out_dtype), semantics=("parallel",), riders=riders), riders)
    vec = pl.BlockSpec((1, K), lambda i: (0, 0))
    return _carried(*_pcall(
        body_norm, (g, w) + tuple(norm_bwd), name=name, grid=(M // tm,),
        in_specs=[g_spec, w_spec, row, vec, row], out_specs=(row, vec),
        out_shape=(jax.ShapeDtypeStruct((M, K), F32), jax.ShapeDtypeStruct((1, K), F32)),
        semantics=("arbitrary",), riders=riders), riders)


def _mm_tn_rs(name, a, g, to, tn=512):
    T, M = a.shape
    N = g.shape[1]
    tt = _row_tile(T, _TN_TOKENS)
    tn = min(tn, N)
    return _matmul(name, (a, g), grid=(M // to, N // tn, T // tt),
                   in_specs=[pl.BlockSpec((tt, to), lambda i, j, k: (k, i)),
                             pl.BlockSpec((tt, tn), lambda i, j, k: (k, j))],
                   o_spec=pl.BlockSpec((to, tn), lambda i, j, k: (i, j)),
                   out_shape=(M, N), out_dtype=F32, dims=TN)


def _mm_tn_cs(name, a, g, nq, to, riders=(), stacked=False):
    T, M = a.shape
    wd = g.shape[-1] * (2 if stacked else 1) // nq
    tt = _row_tile(T, _TN_TOKENS)
    g_spec = (pl.BlockSpec((None, tt, wd), lambda i, j, k: (j // 2, k, j % 2)) if stacked
              else pl.BlockSpec((tt, wd), lambda i, j, k: (k, j)))
    return _matmul(name, (a, g), grid=(M // to, nq, T // tt),
                   in_specs=[pl.BlockSpec((tt, to), lambda i, j, k: (k, i)), g_spec],
                   o_spec=pl.BlockSpec((None, to, wd), lambda i, j, k: (j, i, 0)),
                   out_shape=(nq, M, wd), out_dtype=F32, dims=TN, riders=riders)


def _mm_tn_cs_to_sibling(name, a, g, nq, place, riders=(), send=True):
    T, M = a.shape
    wd = g.shape[-1] // nq
    to = M // 2
    steps = 2 * nq

    def body(s_ref, a_ref, g_ref, o_hbm, land_hbm, acc, wsem, send_sem, recv_sem):
        t = pl.program_id(0)
        c = s_ref[1]

        def writeback(tt):
            half = (tt // nq + 1 + c) % 2
            return pltpu.make_async_copy(acc.at[tt % 2], o_hbm.at[tt % nq, pl.ds(half * to, to), :], wsem.at[tt % 2])

        @pl.when(t >= 2)
        def _():
            writeback(t - 2).wait()

        if send:
            x, y, _, _ = _place()
            to_sibling = _remote(o_hbm.at[:, pl.ds((1 - c) * to, to), :], land_hbm, send_sem.at[0], recv_sem.at[0],
                                 (x, y, 1 - c))

            @pl.when(t == nq + 1)
            def _():
                to_sibling.start()

        acc[t % 2] = _dot_tn(a_ref[...], g_ref[...])
        writeback(t).start()

        @pl.when(t == steps - 1)
        def _():
            writeback(t - 1).wait()
            writeback(t).wait()
            if send:
                to_sibling.wait()

    out, carried = _pcall(
        body, (a, g), name=name, grid=(steps,),
        in_specs=[pl.BlockSpec((T, to), lambda t, s: (0, (t // nq + 1 + s[1]) % 2)),
                  pl.BlockSpec((T, wd), lambda t, s: (0, t % nq))],
        out_specs=(HBM_SPEC, HBM_SPEC),
        out_shape=(jax.ShapeDtypeStruct((nq, M, wd), F32), jax.ShapeDtypeStruct((nq, to, wd), F32)),
        scratch_shapes=[pltpu.VMEM((2, to, wd), F32), pltpu.SemaphoreType.DMA((2,)),
                        pltpu.SemaphoreType.DMA((1,)), pltpu.SemaphoreType.DMA((1,))],
        semantics=("arbitrary",), riders=riders, prefetch=place)
    return _carried(out, carried, riders)


def _rms_fwd(name, x, g, riders=()):
    T, D = x.shape
    tm = _row_tile(T)

    def body(x_ref, g_ref, o_ref):
        o_ref[...] = _rms_rows(x_ref[...], g_ref[...]).astype(o_ref.dtype)

    return _carried(*_pcall(
        body, (x, g), name=name, grid=(T // tm,),
        in_specs=[pl.BlockSpec((tm, D), lambda i: (i, 0)), pl.BlockSpec((1, D), lambda i: (0, 0))],
        out_specs=pl.BlockSpec((tm, D), lambda i: (i, 0)),
        out_shape=jax.ShapeDtypeStruct((T, D), BF16), semantics=("parallel",), riders=riders), riders)


_NORM1_STEPS = 4


def _norm1_and_casts(x, g, shards, place, riders=()):
    T, D = x.shape
    tm = T // _NORM1_STEPS
    names = list(shards)

    def body(s_ref, x_ref, g_ref, *refs):
        w_refs, o_ref, slab_refs = refs[:len(names)], refs[len(names)], refs[len(names) + 1:]
        o_ref[...] = _rms_rows(x_ref[...], g_ref[...]).astype(o_ref.dtype)
        for w_ref, slab_ref in zip(w_refs, slab_refs):
            slab_ref[...] = w_ref[...].astype(slab_ref.dtype)

    in_specs = [pl.BlockSpec((tm, D), lambda i, s: (i, 0)), pl.BlockSpec((1, D), lambda i, s: (0, 0))]
    out_specs = [pl.BlockSpec((tm, D), lambda i, s: (i, 0))]
    out_shape = [jax.ShapeDtypeStruct((T, D), BF16)]
    for n in names:
        r, cc = shards[n].shape
        assert r % (_NORM1_STEPS * 16) == 0
        in_specs.append(pl.BlockSpec((r // _NORM1_STEPS, cc), lambda i, s: (i, 0)))
        out_specs.append(pl.BlockSpec((None, r // _NORM1_STEPS, cc), lambda i, s: (s[0], i, 0)))
        out_shape.append(jax.ShapeDtypeStruct((N_CHIPS, r, cc), BF16))
    out, carried = _pcall(body, (x, g, *[shards[n] for n in names]), name="norm1", grid=(_NORM1_STEPS,),
                          in_specs=in_specs, out_specs=out_specs, out_shape=out_shape, semantics=("parallel",),
                          riders=riders, prefetch=place)
    return _carried((out[0], dict(zip(names, out[1:]))), carried, riders)


def _rms_rows(xv, gain):
    return xv * lax.rsqrt(jnp.mean(xv * xv, axis=-1, keepdims=True) + EPS) * gain


def _rms_bwd_rows(xv, gain, dh):
    r = lax.rsqrt(jnp.mean(xv * xv, axis=-1, keepdims=True) + EPS)
    n = xv * r
    dn = dh * gain
    return r * (dn - n * jnp.mean(dn * n, axis=-1, keepdims=True)), jnp.sum(dh * n, axis=0, keepdims=True)


def _rms_bwd(name, x, g, dh, dres):
    T, D = x.shape
    tm = _row_tile(T)
    has_res = dres is not None

    def body(*refs):
        if has_res:
            x_ref, g_ref, dh_ref, dr_ref, dx_ref, dg_ref = refs
        else:
            x_ref, g_ref, dh_ref, dx_ref, dg_ref = refs

        @pl.when(pl.program_id(0) == 0)
        def _():
            dg_ref[...] = jnp.zeros_like(dg_ref)

        dx, dg = _rms_bwd_rows(x_ref[...], g_ref[...], dh_ref[...])
        dg_ref[...] += dg
        if has_res:
            dx = dx + dr_ref[...]
        dx_ref[...] = dx

    row = pl.BlockSpec((tm, D), lambda i: (i, 0))
    vec = pl.BlockSpec((1, D), lambda i: (0, 0))
    ops = (x, g, dh, dres) if has_res else (x, g, dh)
    return _pallas(
        body, name=name, grid=(T // tm,), in_specs=[row, vec, row] + ([row] if has_res else []),
        out_specs=(row, vec),
        out_shape=(jax.ShapeDtypeStruct((T, D), F32), jax.ShapeDtypeStruct((1, D), F32)),
        compiler_params=_cp("arbitrary"),
    )(*ops)


def _proj_res_norm(name, a, w, res, gain):
    M, K = a.shape
    N = w.shape[1]
    tm = _row_tile(M)

    def body(a_ref, w_ref, r_ref, g_ref, x_ref, h_ref):
        xv = _dot(a_ref[...], w_ref[...]) + r_ref[...]
        x_ref[...] = xv
        h_ref[...] = _rms_rows(xv, g_ref[...]).astype(h_ref.dtype)

    row = pl.BlockSpec((tm, N), lambda i: (i, 0))
    return _pallas(
        body, name=name, grid=(M // tm,),
        in_specs=[pl.BlockSpec((tm, K), lambda i: (i, 0)), pl.BlockSpec((K, N), lambda i: (0, 0)), row,
                  pl.BlockSpec((1, N), lambda i: (0, 0))],
        out_specs=(row, row), out_shape=(jax.ShapeDtypeStruct((M, N), F32), jax.ShapeDtypeStruct((M, N), BF16)),
        compiler_params=_cp("parallel"),
    )(a, w, res, gain)


def _proj_res_loss(name, a, w, res, tgt, gain):
    M, K = a.shape
    D = w.shape[1]
    tm = _row_tile(M)

    def body(a_ref, w_ref, r_ref, t_ref, g_ref, dx_ref, dg_ref, loss_ref):
        @pl.when(pl.program_id(0) == 0)
        def _():
            dg_ref[...] = jnp.zeros_like(dg_ref)
            loss_ref[...] = jnp.zeros_like(loss_ref)

        xv = _dot(a_ref[...], w_ref[...]) + r_ref[...]
        gv = g_ref[...]
        diff = _rms_rows(xv, gv) - t_ref[...]
        loss_ref[...] += 0.5 * jnp.sum(jnp.mean(diff * diff, axis=-1, keepdims=True))
        dx, dg = _rms_bwd_rows(xv, gv, diff * (1.0 / D))
        dg_ref[...] += dg
        dx_ref[...] = dx

    row = pl.BlockSpec((tm, D), lambda i: (i, 0))
    vec = pl.BlockSpec((1, D), lambda i: (0, 0))
    return _pallas(
        body, name=name, grid=(M // tm,),
        in_specs=[pl.BlockSpec((tm, K), lambda i: (i, 0)), pl.BlockSpec((K, D), lambda i: (0, 0)), row, row, vec],
        out_specs=(row, vec, pl.BlockSpec((8, 128), lambda i: (0, 0))),
        out_shape=(jax.ShapeDtypeStruct((M, D), F32), jax.ShapeDtypeStruct((1, D), F32),
                   jax.ShapeDtypeStruct((8, 128), F32)),
        compiler_params=_cp("arbitrary"),
    )(a, w, res, tgt, gain)


def _gmlp_pieces(zu, zv, lng, lnb, ws_ref, bs_ref):
    u, du = _gelu_and_grad(zu)
    v, dv = _gelu_and_grad(zv)
    mu = jnp.mean(v, axis=-1, keepdims=True)
    vc = v - mu
    rstd = lax.rsqrt(jnp.mean(vc * vc, axis=-1, keepdims=True) + EPS)
    vhat = vc * rstd
    vn = vhat * lng + lnb
    row = lax.broadcasted_iota(jnp.int32, (GM_CHUNK, GM_CHUNK), 0)
    col = lax.broadcasted_iota(jnp.int32, (GM_CHUNK, GM_CHUNK), 1)
    tril = row >= col
    wms, mixed = [], []
    for g in range(GM_GROUPS):
        sl = slice(g * 128, (g + 1) * 128)
        wm = jnp.where(tril, ws_ref[g], 0.0)
        wms.append(wm)
        mixed.append(_dot(wm, vn[:, sl]) + bs_ref[g])
    return u, du, dv, rstd, vhat, vn, wms, mixed, tril


def _gmlp_fwd(proj, lng, lnb, ws, bs_col):
    T = proj.shape[0]
    n = T // GM_CHUNK

    ring = 3
    assert COL_ZV == COL_ZU + 1 and n >= ring

    def body(proj_hbm, lng_ref, lnb_ref, ws_ref, bs_ref, o_ref, z_buf, z_sem):
        s = pl.program_id(0)

        def fetch(t):
            return pltpu.make_async_copy(
                proj_hbm.at[pl.ds(t * GM_CHUNK, GM_CHUNK), pl.ds(COL_ZU * GM_WIDTH, 2 * GM_WIDTH)],
                z_buf.at[t % ring], z_sem.at[t % ring])

        @pl.when(s == 0)
        def _():
            for t in range(ring - 1):
                fetch(t).start()

        @pl.when(s + ring - 1 < n)
        def _():
            fetch(s + ring - 1).start()

        fetch(s).wait()
        z = z_buf[s % ring].astype(F32)
        u, _, _, _, _, _, _, mixed, _ = _gmlp_pieces(z[:, :GM_WIDTH], z[:, GM_WIDTH:], lng_ref[...], lnb_ref[...],
                                                     ws_ref, bs_ref)
        for g in range(GM_GROUPS):
            sl = slice(g * 128, (g + 1) * 128)
            o_ref[:, sl] = (u[:, sl] * mixed[g]).astype(o_ref.dtype)

    vec = pl.BlockSpec((1, GM_WIDTH), lambda i: (0, 0))
    return _pallas(
        body, name="gmlp_fwd", grid=(n,),
        in_specs=[HBM_SPEC, vec, vec,
                  pl.BlockSpec((GM_GROUPS, 128, 128), lambda i: (0, 0, 0)),
                  pl.BlockSpec((GM_GROUPS, 128, 1), lambda i: (0, 0, 0))],
        out_specs=pl.BlockSpec((GM_CHUNK, 512), lambda i: (i, 0)),
        out_shape=jax.ShapeDtypeStruct((T, GM_WIDTH), BF16),
        scratch_shapes=[pltpu.VMEM((ring, GM_CHUNK, 2 * GM_WIDTH), BF16), pltpu.SemaphoreType.DMA((ring,))],
        compiler_params=_cp("arbitrary"),
    )(proj, lng, lnb, ws, bs_col)


def _gmlp_bwd(proj, d_out, lng, lnb, ws, bs_col, riders=()):
    T = proj.shape[0]
    n = T // GM_CHUNK

    def body(zu_ref, zv_ref, do_ref, lng_ref, lnb_ref, ws_ref, bs_ref,
             dz_ref, dws_ref, dbs_ref, dlng_ref, dlnb_ref, dm_acc):
        i = pl.program_id(0)

        @pl.when(i == 0)
        def _():
            dws_ref[...] = jnp.zeros_like(dws_ref)
            dlng_ref[...] = jnp.zeros_like(dlng_ref)
            dlnb_ref[...] = jnp.zeros_like(dlnb_ref)
            dm_acc[...] = jnp.zeros_like(dm_acc)

        lng_v = lng_ref[...]
        u, du, dv, rstd, vhat, vn, wms, mixed, tril = _gmlp_pieces(zu_ref[...].astype(F32), zv_ref[...].astype(F32),
                                                                  lng_v, lnb_ref[...],
                                                                  ws_ref, bs_ref)
        do = do_ref[...]
        dvn_parts = []
        for g in range(GM_GROUPS):
            sl = slice(g * 128, (g + 1) * 128)
            dog = do[:, sl]
            dz_ref[:, sl] = (dog * mixed[g] * du[:, sl]).astype(dz_ref.dtype)
            dmix = dog * u[:, sl]
            dm_acc[:, sl] += dmix
            dws_ref[g] += jnp.where(tril, _dot_nt(dmix, vn[:, sl]), 0.0)
            dvn_parts.append(_dot_tn(wms[g], dmix))
        dvn = jnp.concatenate(dvn_parts, axis=1)
        dlng_ref[...] += jnp.sum(dvn * vhat, axis=0, keepdims=True)
        dlnb_ref[...] += jnp.sum(dvn, axis=0, keepdims=True)
        dvh = dvn * lng_v
        dvv = rstd * (dvh - jnp.mean(dvh, axis=-1, keepdims=True)
                      - vhat * jnp.mean(dvh * vhat, axis=-1, keepdims=True))
        dz_ref[:, GM_WIDTH:] = (dvv * dv).astype(dz_ref.dtype)

        @pl.when(i == n - 1)
        def _():
            for g in range(GM_GROUPS):
                dbs_ref[g] = jnp.sum(dm_acc[:, g * 128:(g + 1) * 128], axis=1, keepdims=True)

    vec = pl.BlockSpec((1, GM_WIDTH), lambda i: (0, 0))
    wsp = pl.BlockSpec((GM_GROUPS, 128, 128), lambda i: (0, 0, 0))
    bsp = pl.BlockSpec((GM_GROUPS, 128, 1), lambda i: (0, 0, 0))
    return _carried(*_pcall(
        body, (proj, proj, d_out, lng, lnb, ws, bs_col), name="gmlp_bwd", grid=(n,),
        in_specs=[pl.BlockSpec((GM_CHUNK, 512), lambda i: (i, COL_ZU)),
                  pl.BlockSpec((GM_CHUNK, 512), lambda i: (i, COL_ZV)),
                  pl.BlockSpec((None, GM_CHUNK, 512), lambda i: (0, i, 0)), vec, vec, wsp, bsp],
        out_specs=(pl.BlockSpec((GM_CHUNK, 2 * GM_WIDTH), lambda i: (i, 0)), wsp, bsp, vec, vec),
        out_shape=(jax.ShapeDtypeStruct((T, 2 * GM_WIDTH), BF16),
                   jax.ShapeDtypeStruct((GM_GROUPS, 128, 128), F32), jax.ShapeDtypeStruct((GM_GROUPS, 128, 1), F32),
                   jax.ShapeDtypeStruct((1, GM_WIDTH), F32), jax.ShapeDtypeStruct((1, GM_WIDTH), F32)),
        scratch_shapes=[pltpu.VMEM((GM_CHUNK, GM_WIDTH), F32)],
        semantics=("arbitrary",), riders=riders), riders)


def _hgrn_lower_bound(lbl):
    return 1.0 / (1.0 + jnp.exp(lbl[1:2, :] - lbl[0:1, :]))


def _hgrn_gates(hq, hf, lb):
    C = HG_CHUNK
    sg = _sigmoid(hf)
    fg = lb + (1.0 - lb) * sg
    sq = _sigmoid(hq)
    row = lax.broadcasted_iota(jnp.int32, (C, C), 0)
    col = lax.broadcasted_iota(jnp.int32, (C, C), 1)
    tril = row >= col
    logf = jnp.log(fg)
    a = _dot_01(tril, logf)
    a_last = jnp.sum(logf, axis=0, keepdims=True)
    first_half = lax.broadcasted_iota(jnp.int32, logf.shape, 0) < (C // 2)
    a_mid = jnp.sum(jnp.where(first_half, logf, 0.0), axis=0, keepdims=True)
    ea, ei, eki, ekl = jnp.exp(a), jnp.exp(a - a_mid), jnp.exp(a_mid - a), jnp.exp(a_last - a)
    k = 1.0 - fg
    q = hq * sq
    qi = (q * ei).astype(BF16).astype(F32)
    ki = (k * eki).astype(BF16).astype(F32)
    return dict(sg=sg, fg=fg, sq=sq, tril=tril, ea=ea, ei=ei, eki=eki, ekl=ekl, e_last=jnp.exp(a_last),
                qe=q * ea, qi=qi, ki=ki, kl=k * ekl)


def _heads(x):
    return [x[:, h * HG_DIM:(h + 1) * HG_DIM] for h in range(HG_HEADS)]


def _hgrn_fwd(proj, lbl, gh, B, S, riders=()):
    C = HG_CHUNK
    NC = S // C
    W = HG_HEADS * HG_DIM

    def body(q_ref, f_ref, i_ref, g_ref, lbl_ref, gh_ref, o_ref, bo_ref, st_ref, state):
        @pl.when(pl.program_id(0) == 0)
        def _():
            state[...] = jnp.zeros_like(state)

        lb = _hgrn_lower_bound(lbl_ref[...])
        ghv = gh_ref[...]
        for b in range(B):
            gt = _hgrn_gates(q_ref[b].astype(F32), f_ref[b].astype(F32), lb)
            v = _heads(i_ref[b])
            qe, qi, ki, kl, e_last = (_heads(gt[n]) for n in ("qe", "qi", "ki", "kl", "e_last"))
            outs, normed = [], []
            for h in range(HG_HEADS):
                p = jnp.where(gt["tril"], _dot_nt(qi[h], ki[h]), 0.0)
                st = state[b, h]
                st_ref[b, h] = st
                o = _dot_nt(qe[h], st) + _dot(p, v[h])
                state[b, h] = st * e_last[h] + _dot_tn(v[h], kl[h])
                outs.append(o)
                normed.append(o * lax.rsqrt(jnp.mean(o * o, axis=-1, keepdims=True) + EPS) * ghv)
            o_ref[b] = jnp.concatenate(outs, axis=1)
            hg = g_ref[b].astype(F32)
            bo_ref[b] = (jnp.concatenate(normed, axis=1) * (hg * _sigmoid(hg))).astype(bo_ref.dtype)

    def col(cb):
        return pl.BlockSpec((B, C, 512), lambda c: (0, c, cb))

    tile = pl.BlockSpec((B, C, W), lambda c: (0, c, 0))
    proj3 = proj.reshape(B, S, proj.shape[-1])
    out, carried = _pcall(
        body, (proj3, proj3, proj3, proj3, lbl, gh), name="hgrn_fwd", grid=(NC,),
        in_specs=[col(COL_HQ), col(COL_HF), col(COL_HI), col(COL_HG),
                  pl.BlockSpec((2, W), lambda c: (0, 0)), pl.BlockSpec((1, HG_DIM), lambda c: (0, 0))],
        out_specs=(tile, tile, pl.BlockSpec((B, None, HG_HEADS, 128, 128), lambda c: (0, c, 0, 0, 0))),
        out_shape=(jax.ShapeDtypeStruct((B, S, W), F32), jax.ShapeDtypeStruct((B, S, W), BF16),
                   jax.ShapeDtypeStruct((B, NC, HG_HEADS, 128, 128), F32)),
        scratch_shapes=[pltpu.VMEM((B, HG_HEADS, 128, 128), F32)],
        semantics=("arbitrary",), riders=riders)
    o_h, b_out, states = out
    out = (o_h, b_out.reshape(B * S, W), states)
    return (out, carried) if riders else out


def _hgrn_bwd(proj, o_saved, states, d_out, lbl, gh, others, B, S, riders=()):
    C = HG_CHUNK
    NC = S // C
    W = HG_HEADS * HG_DIM
    d_gm, d_xq, d_gates = (t.reshape(B, S, t.shape[-1]) for t in others)
    own0 = d_gm.shape[-1]
    xq0 = own0 + 4 * W
    gates0 = xq0 + d_xq.shape[-1]

    def body(q_ref, f_ref, i_ref, g_ref, o_ref, st_ref, do_ref, lbl_ref, gh_ref, gm_ref, xq_ref, gates_ref,
             d_ref, dlbl_ref, dgh_ref, dstate, dlb_acc):
        c = pl.program_id(0)
        d_ref[:, :, :own0] = gm_ref[...]
        d_ref[:, :, xq0:gates0] = xq_ref[...]
        d_ref[:, :, gates0:] = gates_ref[...]

        def put(b, k, val):
            d_ref[b, :, own0 + k * W:own0 + (k + 1) * W] = val.astype(d_ref.dtype)

        @pl.when(c == 0)
        def _():
            dstate[...] = jnp.zeros_like(dstate)
            dgh_ref[...] = jnp.zeros_like(dgh_ref)
            dlb_acc[...] = jnp.zeros_like(dlb_acc)

        lb = _hgrn_lower_bound(lbl_ref[...])
        ghv = gh_ref[...]
        row = lax.broadcasted_iota(jnp.int32, (C, C), 0)
        colm = lax.broadcasted_iota(jnp.int32, (C, C), 1)
        triu = colm >= row
        for b in range(B):
            hq, hg = q_ref[b].astype(F32), g_ref[b].astype(F32)
            gt = _hgrn_gates(hq, f_ref[b].astype(F32), lb)
            tril = gt["tril"]
            v = _heads(i_ref[b])
            qe, qi, ki, kl, e_last = (_heads(gt[n]) for n in ("qe", "qi", "ki", "kl", "e_last"))
            sgg = _sigmoid(hg)
            don_all = do_ref[b] * (hg * sgg)
            o, don = _heads(o_ref[b]), _heads(don_all)
            d_qe, d_qi, d_ki, d_kl, dv, n_all, dal = [], [], [], [], [], [], []
            for h in range(HG_HEADS):
                r = lax.rsqrt(jnp.mean(o[h] * o[h], axis=-1, keepdims=True) + EPS)
                n = o[h] * r
                n_all.append(n)
                dgh_ref[...] += jnp.sum(don[h] * n, axis=0, keepdims=True)
                dn = don[h] * ghv
                d_o = r * (dn - n * jnp.mean(dn * n, axis=-1, keepdims=True))
                st, dst = st_ref[b, h], dstate[b, h]
                p = jnp.where(tril, _dot_nt(qi[h], ki[h]), 0.0)
                dp = jnp.where(tril, _dot_nt(d_o, v[h]), 0.0)
                d_qe.append(_dot(d_o, st))
                d_qi.append(_dot(dp, ki[h]))
                d_ki.append(_dot_tn(dp, qi[h]))
                d_kl.append(_dot(v[h], dst))
                dv.append(_dot_tn(p, d_o) + _dot_nt(kl[h], dst))
                dstate[b, h] = dst * e_last[h] + _dot_tn(d_o, qe[h])
                dal.append(jnp.sum(dst * st, axis=0, keepdims=True) * e_last[h])
            d_qe, d_qi, d_ki, d_kl, n_all, dal = (jnp.concatenate(t, axis=1)
                                                  for t in (d_qe, d_qi, d_ki, d_kl, n_all, dal))
            put(b, 3, do_ref[b] * n_all * jnp.tile(ghv, (1, HG_HEADS)) * (sgg * (1.0 + hg * (1.0 - sgg))))
            put(b, 2, jnp.concatenate(dv, axis=1))
            d_a_last = dal + jnp.sum(d_kl * gt["kl"], axis=0, keepdims=True)
            dq = d_qe * gt["ea"] + d_qi * gt["ei"]
            dk = d_ki * gt["eki"] + d_kl * gt["ekl"]
            da = d_qe * gt["qe"] + d_qi * gt["qi"] - d_ki * gt["ki"] - d_kl * gt["kl"]
            dlogf = _dot_01(triu, da) + d_a_last
            sg, sq = gt["sg"], gt["sq"]
            dfg = dlogf / gt["fg"] - dk
            put(b, 1, dfg * (1.0 - lb) * sg * (1.0 - sg))
            dlb_acc[...] += jnp.sum(dfg * (1.0 - sg), axis=0, keepdims=True)
            put(b, 0, dq * (sq * (1.0 + hq * (1.0 - sq))))

        @pl.when(c == NC - 1)
        def _():
            dlb = dlb_acc[...]
            first = lax.broadcasted_iota(jnp.int32, (2, W), 0) == 0
            dlbl_ref[...] = jnp.where(first, dlb * lb * (1.0 - lb), -dlb * lb * (1.0 - lb))

    def col(cb):
        return pl.BlockSpec((B, C, 512), lambda c: (0, NC - 1 - c, cb))

    tile = pl.BlockSpec((B, C, W), lambda c: (0, NC - 1 - c, 0))
    proj3 = proj.reshape(B, S, proj.shape[-1])

    def rows(width):
        return pl.BlockSpec((B, C, width), lambda c: (0, NC - 1 - c, 0))

    width = proj.shape[-1]
    out, carried = _pcall(
        body, (proj3, proj3, proj3, proj3, o_saved, states, d_out.reshape(3, B, S, W), lbl, gh, d_gm, d_xq, d_gates),
        name="hgrn_bwd", grid=(NC,),
        in_specs=[col(COL_HQ), col(COL_HF), col(COL_HI), col(COL_HG), tile,
                  pl.BlockSpec((B, None, HG_HEADS, 128, 128), lambda c: (0, NC - 1 - c, 0, 0, 0)),
                  pl.BlockSpec((None, B, C, W), lambda c: (1, 0, NC - 1 - c, 0)),
                  pl.BlockSpec((2, W), lambda c: (0, 0)), pl.BlockSpec((1, HG_DIM), lambda c: (0, 0)),
                  rows(d_gm.shape[-1]), rows(d_xq.shape[-1]), rows(d_gates.shape[-1])],
        out_specs=(rows(width), pl.BlockSpec((2, W), lambda c: (0, 0)), pl.BlockSpec((1, HG_DIM), lambda c: (0, 0))),
        out_shape=(jax.ShapeDtypeStruct((B, S, width), BF16), jax.ShapeDtypeStruct((2, W), F32),
                   jax.ShapeDtypeStruct((1, HG_DIM), F32)),
        scratch_shapes=[pltpu.VMEM((B, HG_HEADS, 128, 128), F32), pltpu.VMEM((1, W), F32)],
        semantics=("arbitrary",), riders=riders)
    out = (out[0].reshape(B * S, width),) + tuple(out[1:])
    return (out, carried) if riders else out


_XA_SCALE = XA_DIM ** -0.5


def _attn_probs(qh, kh):
    s = _dot_nt(qh, kh) * _XA_SCALE
    e = jnp.exp(s - jnp.max(s, axis=-1, keepdims=True))
    return e / jnp.sum(e, axis=-1, keepdims=True)


def _attn_fwd(proj, kv, B, S):
    T = B * S
    tq = _row_tile(S)
    nq = S // tq
    W = XA_HEADS * XA_DIM

    def body(q_ref, kv_ref, o_ref):
        for h in range(XA_HEADS):
            sl = slice(h * 128, (h + 1) * 128)
            p = _attn_probs(q_ref[:, sl], kv_ref[:, sl])
            o_ref[:, sl] = _dot(p, kv_ref[:, W + h * 128:W + (h + 1) * 128]).astype(o_ref.dtype)

    return _pallas(
        body, name="attn_fwd", grid=(B, nq),
        in_specs=[pl.BlockSpec((tq, 512), lambda b, i: (b * nq + i, COL_XQ)),
                  pl.BlockSpec((MEM_LEN, 2 * W), lambda b, i: (b, 0))],
        out_specs=pl.BlockSpec((tq, W), lambda b, i: (b * nq + i, 0)),
        out_shape=jax.ShapeDtypeStruct((T, W), BF16), compiler_params=_cp("parallel", "parallel"),
    )(proj, kv)


def _attn_bwd(proj, kv, d_out, B, S):
    T = B * S
    tq = _row_tile(S)
    nq = S // tq
    W = XA_HEADS * XA_DIM

    def body(q_ref, kv_ref, do_ref, dq_ref, dkv_ref):
        @pl.when(pl.program_id(1) == 0)
        def _():
            dkv_ref[...] = jnp.zeros_like(dkv_ref)

        for h in range(XA_HEADS):
            sl = slice(h * 128, (h + 1) * 128)
            slv = slice(W + h * 128, W + (h + 1) * 128)
            qh = q_ref[:, sl]
            kh = kv_ref[:, sl]
            p = _attn_probs(qh, kh)
            dc = do_ref[:, sl]
            dp = _dot_nt(dc, kv_ref[:, slv])
            ds = p * (dp - jnp.sum(dp * p, axis=-1, keepdims=True)) * _XA_SCALE
            dq_ref[:, sl] = _dot(ds, kh).astype(dq_ref.dtype)
            dkv_ref[:, sl] += _dot_tn(ds, qh)
            dkv_ref[:, slv] += _dot_tn(p, dc)

    kvspec = pl.BlockSpec((MEM_LEN, 2 * W), lambda b, i: (b, 0))
    tile = pl.BlockSpec((tq, W), lambda b, i: (b * nq + i, 0))
    return _pallas(
        body, name="attn_bwd", grid=(B, nq),
        in_specs=[pl.BlockSpec((tq, 512), lambda b, i: (b * nq + i, COL_XQ)), kvspec,
                  pl.BlockSpec((None, tq, W), lambda b, i: (2, b * nq + i, 0))],
        out_specs=(tile, kvspec),
        out_shape=(jax.ShapeDtypeStruct((T, W), BF16), jax.ShapeDtypeStruct((B * MEM_LEN, 2 * W), F32)),
        compiler_params=_cp("parallel", "arbitrary"),
    )(proj, kv, d_out)


_MERGE_TM = 256
_GATE_W = 512


def _gate_specs(tm):
    base = COL_GATE0 // _GATE_W
    return [pl.BlockSpec((tm, _GATE_W), functools.partial(lambda i, k: (i, base + k), k=k)) for k in range(6)]


def _merge_fwd(a_out, b_out, c_out, wb, proj, riders=()):
    T = a_out.shape[0]
    tm = _row_tile(T, _MERGE_TM)
    nq, _, wd = wb.shape
    per_half = _GATE_W // wd

    def body(a_ref, b_ref, c_ref, w_ref, *rest):
        gates, (m_ref, up_ref) = rest[:6], rest[6:]
        for hf in range(2):
            cols = slice(hf * _GATE_W, (hf + 1) * _GATE_W)
            acc = None
            for n, br in enumerate((a_ref, b_ref, c_ref)):
                x = br[...]
                up = jnp.concatenate([_dot(x, w_ref[per_half * hf + j, n * BR_WIDTH:(n + 1) * BR_WIDTH, :])
                                      for j in range(per_half)], axis=1)
                up_ref[n, :, cols] = up.astype(up_ref.dtype)
                term = _sigmoid(gates[2 * n + hf][...].astype(F32)) * up
                acc = term if acc is None else acc + term
            m_ref[:, cols] = acc.astype(m_ref.dtype)

    br_spec = pl.BlockSpec((tm, BR_WIDTH), lambda i: (i, 0))
    return _carried(*_pcall(
        body, (a_out, b_out, c_out, wb, *([proj] * 6)), name="merge_fwd", grid=(T // tm,),
        in_specs=[br_spec, br_spec, br_spec,
                  pl.BlockSpec((nq, 3 * BR_WIDTH, wd), lambda i: (0, 0, 0))] + _gate_specs(tm),
        out_specs=(pl.BlockSpec((tm, D_MODEL), lambda i: (i, 0)), pl.BlockSpec((3, tm, D_MODEL), lambda i: (0, i, 0))),
        out_shape=(jax.ShapeDtypeStruct((T, D_MODEL), BF16), jax.ShapeDtypeStruct((3, T, D_MODEL), BF16)),
        semantics=("parallel",), riders=riders), riders)


def _branch_bwd_act(d_ups, wb, riders=()):
    _, T, D = d_ups.shape
    nq, _, wd = wb.shape
    tm = _row_tile(T)

    nt = T // tm
    ring = 3
    assert 3 * nt >= ring

    def body(d_hbm, w_ref, o_ref, d_buf, d_sem):
        s = pl.program_id(0) * nt + pl.program_id(1)

        def fetch(t):
            return pltpu.make_async_copy(d_hbm.at[t // nt, pl.ds((t % nt) * tm, tm), :], d_buf.at[t % ring],
                                         d_sem.at[t % ring])

        @pl.when(s == 0)
        def _():
            for t in range(ring - 1):
                fetch(t).start()

        @pl.when(s + ring - 1 < 3 * nt)
        def _():
            fetch(s + ring - 1).start()

        fetch(s).wait()
        acc = None
        for q in range(nq):
            part = _dot_nt(d_buf[s % ring, :, q * wd:(q + 1) * wd], w_ref[q])
            acc = part if acc is None else acc + part
        o_ref[...] = acc

    return _carried(*_pcall(
        body, (d_ups, wb), name="d_branch", grid=(3, nt),
        in_specs=[HBM_SPEC, pl.BlockSpec((nq, BR_WIDTH, wd), lambda n, i: (0, n, 0))],
        out_specs=pl.BlockSpec((None, tm, BR_WIDTH), lambda n, i: (n, i, 0)),
        out_shape=jax.ShapeDtypeStruct((3, T, BR_WIDTH), F32),
        scratch_shapes=[pltpu.VMEM((ring, tm, D), d_ups.dtype), pltpu.SemaphoreType.DMA((ring,))],
        semantics=("arbitrary", "arbitrary"), riders=riders), riders)


def _branch_bwd_weight(name, br, d_ups, n, into=None):
    T = br.shape[0]
    D = d_ups.shape[2]
    wd = D // N_CHIPS
    tt = _row_tile(T, _TN_TOKENS)
    n_br = d_ups.shape[0]

    def body(b_ref, d_ref, *rest):
        o_ref = rest[-1]
        k = pl.program_id(0)
        for q in range(N_CHIPS):
            part = _dot_tn(b_ref[...], d_ref[:, q * wd:(q + 1) * wd])

            @pl.when(k == 0)
            def _():
                o_ref[q] = part

            @pl.when(k > 0)
            def _():
                o_ref[q] += part

    return _pallas(
        body, name=name, grid=(T // tt,),
        in_specs=[pl.BlockSpec((tt, BR_WIDTH), lambda k: (k, 0)),
                  pl.BlockSpec((None, tt, D), lambda k: (n, k, 0))] + ([] if into is None else [HBM_SPEC]),
        out_specs=pl.BlockSpec((N_CHIPS, BR_WIDTH, wd), lambda k: (0, n, 0)),
        out_shape=jax.ShapeDtypeStruct((N_CHIPS, n_br * BR_WIDTH, wd), F32),
        input_output_aliases={} if into is None else {2: 0}, compiler_params=_cp("arbitrary"),
    )(br, d_ups, *(() if into is None else (into,)))


def _merge_bwd(d_merged, ups, proj, riders=()):
    T = d_merged.shape[0]
    tm = _row_tile(T, _MERGE_TM)

    def body(dm_ref, up_ref, *rest):
        gates, (dup_ref, dg_ref) = rest[:6], rest[6:]
        for hf in range(2):
            cols = slice(hf * _GATE_W, (hf + 1) * _GATE_W)
            dm = dm_ref[:, cols]
            for n in range(3):
                gate = _sigmoid(gates[2 * n + hf][...].astype(F32))
                dup_ref[n, :, cols] = (dm * gate).astype(dup_ref.dtype)
                dg_ref[:, n * D_MODEL + hf * _GATE_W:n * D_MODEL + (hf + 1) * _GATE_W] = (
                    dm * up_ref[n, :, cols].astype(F32) * gate * (1.0 - gate)).astype(dg_ref.dtype)

    tile = pl.BlockSpec((tm, D_MODEL), lambda i: (i, 0))
    tile3 = pl.BlockSpec((3, tm, D_MODEL), lambda i: (0, i, 0))
    return _carried(*_pcall(
        body, (d_merged, ups, *([proj] * 6)), name="merge_bwd", grid=(T // tm,),
        in_specs=[tile, tile3] + _gate_specs(tm),
        out_specs=(tile3, pl.BlockSpec((tm, 3 * D_MODEL), lambda i: (i, 0))),
        out_shape=(jax.ShapeDtypeStruct((3, T, D_MODEL), BF16), jax.ShapeDtypeStruct((T, 3 * D_MODEL), BF16)),
        semantics=("parallel",), riders=riders), riders)


_CONV_TF = D_FF // 2
_CONV_TS = 256
_HALO = 16


def _conv_fwd(ab, cw, cb, B, S):
    T = B * S
    ts = _row_tile(S, _CONV_TS)
    tf = _CONV_TF
    nb = D_FF // tf
    tps = S // ts
    hb = ts // _HALO

    steps = (T // ts) * nb
    ring = 3

    def body(ab_hbm, p_ref, w_ref, cb_ref, o_ref, a_buf, b_buf, a_sem, b_sem):
        s = pl.program_id(0) * nb + pl.program_id(1)

        def fetch(t):
            rows, slot = pl.ds((t // nb) * ts, ts), t % ring
            return (pltpu.make_async_copy(ab_hbm.at[rows, pl.ds((t % nb) * tf, tf)], a_buf.at[slot], a_sem.at[slot]),
                    pltpu.make_async_copy(ab_hbm.at[rows, pl.ds((t % nb + nb) * tf, tf)], b_buf.at[slot],
                                          b_sem.at[slot]))

        @pl.when(s == 0)
        def _():
            for t in range(ring - 1):
                for cp in fetch(t):
                    cp.start()

        @pl.when(s + ring - 1 < steps)
        def _():
            for cp in fetch(s + ring - 1):
                cp.start()

        for cp in fetch(s):
            cp.wait()
        start = (pl.program_id(0) % tps) == 0
        a = a_buf[s % ring].astype(F32)
        prev = jnp.where(start, 0.0, p_ref[...].astype(F32))
        ext = jnp.concatenate([prev, a], axis=0)
        a1 = pltpu.roll(ext, 1, 0)[_HALO:, :]
        a2 = pltpu.roll(ext, 2, 0)[_HALO:, :]
        ac = cb_ref[...] + w_ref[0] * a2 + w_ref[1] * a1 + w_ref[2] * a
        o_ref[...] = (ac * _sigmoid(ac) * b_buf[s % ring].astype(F32)).astype(o_ref.dtype)

    assert steps >= ring
    return _pallas(
        body, name="conv_fwd", grid=(T // ts, nb),
        in_specs=[HBM_SPEC,
                  pl.BlockSpec((_HALO, tf), lambda i, j: (jnp.maximum(i * hb - 1, 0), j)),
                  pl.BlockSpec((3, 1, tf), lambda i, j: (0, 0, j)),
                  pl.BlockSpec((1, tf), lambda i, j: (0, j))],
        out_specs=pl.BlockSpec((ts, tf), lambda i, j: (i, j)),
        out_shape=jax.ShapeDtypeStruct((T, D_FF), BF16),
        scratch_shapes=[pltpu.VMEM((ring, ts, tf), BF16), pltpu.VMEM((ring, ts, tf), BF16),
                        pltpu.SemaphoreType.DMA((ring,)), pltpu.SemaphoreType.DMA((ring,))],
        compiler_params=_cp("arbitrary", "arbitrary"),
    )(ab, ab, cw, cb)


def _conv_bwd(ab, d_ff, cw, cb, B, S, riders=()):
    T = B * S
    ts = _row_tile(S, _CONV_TS)
    tf = _CONV_TF
    nb = D_FF // tf
    tps = S // ts
    hb = ts // _HALO
    last_h = T // _HALO - 1
    n_ext = ts + _HALO

    def body(a_ref, ap_ref, an_ref, b_ref, bn_ref, d_ref, dn_ref, w_ref, cb_ref, dab_ref, dw_ref, dcb_ref):
        i = pl.program_id(1)

        @pl.when(i == 0)
        def _():
            dw_ref[...] = jnp.zeros_like(dw_ref)
            dcb_ref[...] = jnp.zeros_like(dcb_ref)

        start = (i % tps) == 0
        end = (i % tps) == tps - 1
        a = a_ref[...].astype(F32)
        ext = jnp.concatenate([jnp.where(start, 0.0, ap_ref[...].astype(F32)), a, an_ref[...].astype(F32)], axis=0)
        r1 = pltpu.roll(ext, 1, 0)[_HALO:, :]
        r2 = pltpu.roll(ext, 2, 0)[_HALO:, :]
        ac = cb_ref[...] + w_ref[0] * r2 + w_ref[1] * r1 + w_ref[2] * ext[_HALO:, :]
        sg = _sigmoid(ac)
        d_e = jnp.concatenate([d_ref[...].astype(F32), jnp.where(end, 0.0, dn_ref[...].astype(F32))], axis=0)
        b_e = jnp.concatenate([b_ref[...].astype(F32), bn_ref[...].astype(F32)], axis=0)
        dab_ref[1] = (d_e[:ts, :] * (ac * sg)[:ts, :]).astype(dab_ref.dtype)
        dac = d_e * b_e * sg * (1.0 + ac * (1.0 - sg))
        u1 = pltpu.roll(dac, n_ext - 1, 0)[:ts, :]
        u2 = pltpu.roll(dac, n_ext - 2, 0)[:ts, :]
        dac0 = dac[:ts, :]
        dab_ref[0] = (w_ref[2] * dac0 + w_ref[1] * u1 + w_ref[0] * u2).astype(dab_ref.dtype)
        dcb_ref[...] += jnp.sum(dac0, axis=0, keepdims=True)
        dw_ref[2] += jnp.sum(dac0 * a, axis=0, keepdims=True)
        dw_ref[1] += jnp.sum(dac0 * r1[:ts, :], axis=0, keepdims=True)
        dw_ref[0] += jnp.sum(dac0 * r2[:ts, :], axis=0, keepdims=True)

    def cur(off):
        return pl.BlockSpec((ts, tf), lambda j, i: (i, j + off))

    def nxt(off):
        return pl.BlockSpec((_HALO, tf), lambda j, i: (jnp.minimum((i + 1) * hb, last_h), j + off))

    return _carried(*_pcall(
        body, (ab, ab, ab, ab, ab, d_ff, d_ff, cw, cb), name="conv_bwd", grid=(nb, T // ts),
        in_specs=[cur(0), pl.BlockSpec((_HALO, tf), lambda j, i: (jnp.maximum(i * hb - 1, 0), j)), nxt(0),
                  cur(nb), nxt(nb), cur(0), nxt(0),
                  pl.BlockSpec((3, 1, tf), lambda j, i: (0, 0, j)), pl.BlockSpec((1, tf), lambda j, i: (0, j))],
        out_specs=(pl.BlockSpec((2, ts, tf), lambda j, i: (0, i, j)), pl.BlockSpec((3, 1, tf), lambda j, i: (0, 0, j)),
                   pl.BlockSpec((1, tf), lambda j, i: (0, j))),
        out_shape=(jax.ShapeDtypeStruct((2, T, D_FF), BF16),
                   jax.ShapeDtypeStruct((3, 1, D_FF), F32), jax.ShapeDtypeStruct((1, D_FF), F32)),
        semantics=("parallel", "arbitrary"), riders=riders), riders)


def _local_step(x, mem, tgt, p, comm, B, S):
    g = {}
    h, slabs = comm.carry(
        "norm1", lambda r: _norm1_and_casts(x, p["norm1_g"], p["cast_beside_norm1"], comm.place, riders=r))
    comm.slabs.update(slabs)
    proj = comm.carry("in_proj", lambda r: _mm_cs("in_proj", h, comm.w("w_in"), BF16, riders=r))
    a_out = _gmlp_fwd(proj, p["ln_v_g"], p["ln_v_b"], p["w_spatial"], p["b_spatial"])
    o_h, b_out, states = comm.carry(
        "hgrn_fwd", lambda r: _hgrn_fwd(proj, p["lb_logits"], p["hgrn_norm_g"], B, S, riders=r))
    memn = _rms_fwd("mem_norm", mem, p["mem_norm_g"])
    kv = _mm_rs("mem_kv", memn, comm.w("w_mem_kv"), F32)
    c_out = _attn_fwd(proj, kv, B, S)
    merged, ups = comm.carry(
        "merge_fwd", lambda r: _merge_fwd(a_out, b_out, c_out, comm.w("w_branch"), proj, riders=r))
    x1, h2 = _proj_res_norm("out_proj_norm2", merged, comm.w("w_out"), x, p["norm2_g"])
    ab = comm.carry("up_proj", lambda r: _mm_cs("up_proj", h2, comm.w("w_up"), BF16, riders=r))
    conv_w = comm.w("conv_w")
    ff = _conv_fwd(ab, conv_w, p["conv_b"], B, S)
    dx2, g["final_g"], loss = _proj_res_loss("down_proj_loss", ff, comm.w("w_down"), x1, tgt, p["final_g"])

    comm.grad("w_down", _mm_tn_rs("g_w_down", ff, dx2, to=D_FF // 2))
    d_ff = comm.carry("d_ff", lambda r: _mm_nt_rs("d_ff", dx2, comm.w("w_down"), BF16, riders=r))
    d_ab, g["conv_w"], g["conv_b"] = comm.carry(
        "conv_bwd", lambda r: _conv_bwd(ab, d_ff, conv_w, p["conv_b"], B, S, riders=r))
    comm.grad("w_up", _mm_tn_cs("g_w_up", h2, d_ab, N_CHIPS, to=512, stacked=True))
    d_x1, g["norm2_g"] = comm.carry("d_h2", lambda r: _mm_nt_cs(
        "d_h2_norm2_bwd", d_ab, comm.w("w_up"), F32, riders=r, stacked=True, norm_bwd=(x1, p["norm2_g"], dx2)))
    comm.grad("w_out", _mm_tn_rs("g_w_out", merged, d_x1, to=512))
    d_merged = _mm_nt_rs("d_merged", d_x1, comm.w("w_out"), F32)
    d_ups, d_gates = comm.carry("merge_bwd", lambda r: _merge_bwd(d_merged, ups, proj, riders=r))

    d_br = comm.carry("d_branch", lambda r: _branch_bwd_act(d_ups, comm.w("w_branch"), riders=r))
    g_branch = None
    for n, br in enumerate((a_out, b_out, c_out)):
        g_branch = _branch_bwd_weight("g_w_branch%d" % n, br, d_ups, n, into=g_branch)
    comm.grad("w_branch", g_branch)

    d_gm, g["w_spatial"], g["b_spatial"], g["ln_v_g"], g["ln_v_b"] = comm.carry(
        "gmlp_bwd", lambda r: _gmlp_bwd(proj, d_br, p["ln_v_g"], p["ln_v_b"], p["w_spatial"], p["b_spatial"],
                                        riders=r))
    d_xq, d_kv = _attn_bwd(proj, kv, d_br, B, S)
    comm.grad("w_mem_kv", _mm_tn_rs("g_w_mem_kv", memn, d_kv, to=512))
    d_memn = _mm_nt_rs("d_memn", d_kv, comm.w("w_mem_kv"), F32)
    _, g["mem_norm_g"] = _rms_bwd("mem_norm_bwd", mem, p["mem_norm_g"], d_memn, None)
    d_proj, g["lb_logits"], g["hgrn_norm_g"] = comm.carry(
        "hgrn_bwd", lambda r: _hgrn_bwd(proj, o_h, states, d_br, p["lb_logits"], p["hgrn_norm_g"],
                                        (d_gm, d_xq, d_gates), B, S, riders=r))
    comm.small_grads([g[n].reshape(_SMALL_SHAPE[n]) for n in _SMALL_EARLY] + [loss])
    comm.grad("w_in", *comm.carry("g_w_in", lambda r: _mm_tn_cs_to_sibling(
        "g_w_in", h, d_proj, N_CHIPS, comm.place, riders=r, send=comm.sends)))
    grad_x, g["norm1_g"] = comm.carry("d_h", lambda r: _mm_nt_cs(
        "d_h_norm1_bwd", d_proj, comm.w("w_in"), F32, riders=r, norm_bwd=(x, p["norm1_g"], d_x1)))
    return loss, grad_x, g


HBM_SPEC = pl.BlockSpec(memory_space=pltpu.HBM)


def _place():
    x, y, c = lax.axis_index("x"), lax.axis_index("y"), lax.axis_index("c")
    other_chips = [(1 - x, y), (x, 1 - y), (1 - x, 1 - y)]
    return x, y, c, other_chips


def _remote(src, dst, send_sem, recv_sem, dev):
    return pltpu.make_async_remote_copy(src_ref=src, dst_ref=dst, send_sem=send_sem, recv_sem=recv_sem,
                                        device_id=dev, device_id_type=MESH_ID)


class _Exchange:
    def __init__(self, operands, out_shape, aliases, scratch, start, finish, mid=None, mid_at=0.5):
        self.operands, self.out_shape, self.aliases, self.scratch = operands, out_shape, aliases, scratch
        self.start, self.finish, self.mid, self.mid_at = start, finish, mid, mid_at


def _run_exchanges(name, exs):
    n_in = [len(ex.operands) for ex in exs]
    n_out = [len(ex.out_shape) for ex in exs]
    n_scr = [len(ex.scratch) for ex in exs]

    def body(*refs):
        ins, outs, scr = refs[:sum(n_in)], refs[sum(n_in):sum(n_in) + sum(n_out)], refs[sum(n_in) + sum(n_out):]
        parts, oi, oo, os_ = [], 0, 0, 0
        for k in range(len(exs)):
            parts.append((ins[oi:oi + n_in[k]], outs[oo:oo + n_out[k]], scr[os_:os_ + n_scr[k]]))
            oi, oo, os_ = oi + n_in[k], oo + n_out[k], os_ + n_scr[k]
        for ex, part in zip(exs, parts):
            ex.start(*part)
        for ex, part in zip(exs, parts):
            if ex.mid is not None:
                ex.mid(*part)
        for ex, part in zip(exs, parts):
            ex.finish(*part)

    aliases, ops, shapes, scratch, oi, oo = {}, [], [], [], 0, 0
    for k, ex in enumerate(exs):
        aliases.update({oi + a: oo + b for a, b in ex.aliases.items()})
        oi, oo = oi + n_in[k], oo + n_out[k]
        ops += list(ex.operands)
        shapes += [pltpu.HBM(s.shape, s.dtype) for s in ex.out_shape]
        scratch += list(ex.scratch)
    res = _pallas(
        body, name=name, in_specs=[HBM_SPEC] * len(ops), out_specs=(HBM_SPEC,) * len(shapes), out_shape=tuple(shapes),
        input_output_aliases=aliases, scratch_shapes=scratch,
    )(*ops)
    out, oo = [], 0
    for k in range(len(exs)):
        out.append(list(res[oo:oo + n_out[k]]))
        oo += n_out[k]
    return out


def _ex_all_gather(slabs, halved, part=(0, 1)):
    n = len(slabs)

    def rows(a, cc):
        if not halved[a]:
            return slice(None)
        pr = slabs[a].shape[1] // part[1]
        return pl.ds(part[0] * pr + cc * (pr // 2), pr // 2)

    def ici(bufs, scr, a, j, chip, c, mine):
        px, py = chip
        x, y, _, _ = _place()
        qs = 2 * x + y if mine else 2 * px + py
        piece = bufs[a].at[qs, rows(a, c)]
        return _remote(piece, piece, scr[0].at[3 * a + j], scr[1].at[3 * a + j], (px, py, c))

    def d2d(bufs, scr, a, j, chip, cc):
        px, py = chip
        x, y, c, _ = _place()
        piece = bufs[a].at[2 * px + py, rows(a, cc)]
        return _remote(piece, piece, scr[2].at[3 * a + j], scr[3].at[3 * a + j], (x, y, 1 - c))

    def start(ins, outs, scr):
        _, _, c, chips = _place()
        for j, chip in enumerate(chips):
            for a in range(n):
                ici(outs, scr, a, j, chip, c, True).start()

    def finish(ins, outs, scr):
        _, _, c, chips = _place()
        for j, chip in enumerate(chips):
            for a in range(n):
                ici(outs, scr, a, j, chip, c, False).wait_recv()
                if halved[a]:
                    d2d(outs, scr, a, j, chip, c).start()
        for j, chip in enumerate(chips):
            for a in range(n):
                if halved[a]:
                    d2d(outs, scr, a, j, chip, 1 - c).wait_recv()
        for j, chip in enumerate(chips):
            for a in range(n):
                ici(outs, scr, a, j, chip, c, True).wait_send()
                if halved[a]:
                    d2d(outs, scr, a, j, chip, c).wait_send()

    return _Exchange(list(slabs), [jax.ShapeDtypeStruct(s.shape, s.dtype) for s in slabs],
                     {a: a for a in range(n)}, [pltpu.SemaphoreType.DMA((3 * n,))] * 4, start, finish)


def _ex_gather_relay(slabs, mid_at=0.5):
    n = len(slabs)

    def rows(a, cc):
        hr = slabs[a].shape[1] // 2
        return pl.ds(cc * hr, hr)

    def peers():
        x, y, c, _ = _place()
        nbr0 = ((x + c) % 2, (y + 1 - c) % 2)
        nbr1 = ((x + 1 - c) % 2, (y + c) % 2)
        return x, y, c, nbr0, nbr1, (1 - x, 1 - y)

    def ici(bufs, scr, a, k, chip, dev, cc):
        _, _, c, _, _, _ = peers()
        piece = bufs[a].at[2 * chip[0] + chip[1], rows(a, cc)]
        return _remote(piece, piece, scr[0].at[3 * a + k], scr[1].at[3 * a + k], (dev[0], dev[1], c))

    def d2d(bufs, scr, a, k, chip, cc):
        x, y, c, _, _, _ = peers()
        piece = bufs[a].at[2 * chip[0] + chip[1], rows(a, cc)]
        return _remote(piece, piece, scr[2].at[3 * a + k], scr[3].at[3 * a + k], (x, y, 1 - c))

    def start(ins, outs, scr):
        x, y, c, nbr0, nbr1, _ = peers()
        for a in range(n):
            ici(outs, scr, a, 0, (x, y), nbr0, c).start()
            ici(outs, scr, a, 1, (x, y), nbr1, c).start()

    def mid(ins, outs, scr):
        x, y, c, nbr0, nbr1, diag = peers()
        for a in range(n):
            ici(outs, scr, a, 0, nbr0, nbr0, c).wait_recv()
            ici(outs, scr, a, 2, nbr0, nbr1, c).start()
            d2d(outs, scr, a, 0, nbr0, c).start()
        for a in range(n):
            ici(outs, scr, a, 1, nbr1, nbr1, c).wait_recv()
            d2d(outs, scr, a, 1, nbr1, c).start()

    def finish(ins, outs, scr):
        x, y, c, nbr0, nbr1, diag = peers()
        for a in range(n):
            ici(outs, scr, a, 2, diag, nbr1, c).wait_recv()
            d2d(outs, scr, a, 2, diag, c).start()
        for a in range(n):
            d2d(outs, scr, a, 0, nbr1, 1 - c).wait_recv()
            d2d(outs, scr, a, 1, nbr0, 1 - c).wait_recv()
            d2d(outs, scr, a, 2, diag, 1 - c).wait_recv()
        for a in range(n):
            ici(outs, scr, a, 0, (x, y), nbr0, c).wait_send()
            ici(outs, scr, a, 1, (x, y), nbr1, c).wait_send()
            ici(outs, scr, a, 2, nbr0, nbr1, c).wait_send()
            d2d(outs, scr, a, 0, nbr0, c).wait_send()
            d2d(outs, scr, a, 1, nbr1, c).wait_send()
            d2d(outs, scr, a, 2, diag, c).wait_send()

    return _Exchange(list(slabs), [jax.ShapeDtypeStruct(s.shape, s.dtype) for s in slabs],
                     {a: a for a in range(n)}, [pltpu.SemaphoreType.DMA((3 * n,))] * 4, start, finish, mid, mid_at)


def _ex_to_sibling(grads):
    n = len(grads)

    def copy(ins, outs, scr, a):
        x, y, c, _ = _place()
        hr = grads[a].shape[1] // 2
        return _remote(ins[a].at[:, pl.ds((1 - c) * hr, hr), :], outs[a], scr[0].at[a], scr[1].at[a], (x, y, 1 - c))

    def start(ins, outs, scr):
        for a in range(n):
            copy(ins, outs, scr, a).start()

    def finish(ins, outs, scr):
        for a in range(n):
            copy(ins, outs, scr, a).wait()

    out_shape = [jax.ShapeDtypeStruct((g.shape[0], g.shape[1] // 2, g.shape[2]), g.dtype) for g in grads]
    return _Exchange(list(grads), out_shape, {}, [pltpu.SemaphoreType.DMA((n,))] * 2, start, finish)


def _ex_to_owner(parts, part=(0, 1), landing=None):
    n = len(parts)

    def copy(ins, outs, scr, a, j, chip):
        _, _, c, _ = _place()
        px, py = chip
        pr = parts[a].shape[1] // part[1]
        rows = pl.ds(part[0] * pr, pr)
        return _remote(ins[a].at[2 * px + py, rows], outs[a].at[j, rows], scr[0].at[3 * a + j],
                       scr[1].at[3 * a + j], (px, py, c))

    def start(ins, outs, scr):
        for j, chip in enumerate(_place()[3]):
            for a in range(n):
                copy(ins, outs, scr, a, j, chip).start()

    def finish(ins, outs, scr):
        for j, chip in enumerate(_place()[3]):
            for a in range(n):
                copy(ins, outs, scr, a, j, chip).wait()

    out_shape = [jax.ShapeDtypeStruct((3,) + p.shape[1:], p.dtype) for p in parts]
    operands, aliases = list(parts), {}
    if landing is not None:
        operands, aliases = operands + list(landing), {n + a: a for a in range(n)}
    return _Exchange(operands, out_shape, aliases, [pltpu.SemaphoreType.DMA((3 * n,))] * 2, start, finish)


def _ex_share_halves(bufs):
    n = len(bufs)

    def copy(outs, scr, a, cc):
        x, y, c, _ = _place()
        hr = bufs[a].shape[0] // 2
        piece = outs[a].at[pl.ds(cc * hr, hr), :]
        return _remote(piece, piece, scr[0].at[a], scr[1].at[a], (x, y, 1 - c))

    def start(ins, outs, scr):
        c = _place()[2]
        for a in range(n):
            copy(outs, scr, a, c).start()

    def finish(ins, outs, scr):
        c = _place()[2]
        for a in range(n):
            copy(outs, scr, a, c).wait_send()
            copy(outs, scr, a, 1 - c).wait_recv()

    return _Exchange(list(bufs), [jax.ShapeDtypeStruct(b.shape, b.dtype) for b in bufs], {a: a for a in range(n)},
                     [pltpu.SemaphoreType.DMA((n,))] * 2, start, finish)


def _ex_gather_small(arrs):
    n = len(arrs)

    def peer_of(m):
        x, y, c, _ = _place()
        return (1 - x if m & 4 else x, 1 - y if m & 2 else y, 1 - c if m & 1 else c)

    def own(ins, outs, scr, a):
        x, y, c, _ = _place()
        return pltpu.make_async_copy(ins[a], outs[a].at[4 * x + 2 * y + c], scr[2].at[a])

    def start(ins, outs, scr):
        x, y, c, _ = _place()
        for a in range(n):
            own(ins, outs, scr, a).start()
        for m in range(1, N_DEV):
            for a in range(n):
                k = (N_DEV - 1) * a + m - 1
                _remote(ins[a], outs[a].at[4 * x + 2 * y + c], scr[0].at[k], scr[1].at[k], peer_of(m)).start()

    def finish(ins, outs, scr):
        for a in range(n):
            own(ins, outs, scr, a).wait()
        for m in range(1, N_DEV):
            px, py, pc = peer_of(m)
            for a in range(n):
                k = (N_DEV - 1) * a + m - 1
                slot = outs[a].at[4 * px + 2 * py + pc]
                cp = _remote(ins[a], slot, scr[0].at[k], scr[1].at[k], (px, py, pc))
                cp.wait_send()
                cp.wait_recv()

    out_shape = [jax.ShapeDtypeStruct((N_DEV,) + a.shape, a.dtype) for a in arrs]
    return _Exchange(list(arrs), out_shape, {},
                     [pltpu.SemaphoreType.DMA(((N_DEV - 1) * n,))] * 2 + [pltpu.SemaphoreType.DMA((n,))], start, finish)


def _div_tile(n, want):
    best = None
    for t in range(8, min(n, want) + 1, 8):
        if n % t == 0:
            best = t
    assert best is not None, n
    return best


def _cast_into_slab(name, w, place, dtype):
    r, cc = w.shape
    tr = r if r * cc <= 128 * 1024 else _div_tile(r, 256)

    def body(s_ref, w_ref, o_ref):
        o_ref[...] = w_ref[...].astype(o_ref.dtype)

    return _pallas(
        body, name=name,
        grid_spec=pltpu.PrefetchScalarGridSpec(
            num_scalar_prefetch=1, grid=(r // tr,),
            in_specs=[pl.BlockSpec((tr, cc), lambda i, s: (i, 0))],
            out_specs=pl.BlockSpec((None, tr, cc), lambda i, s: (s[0], i, 0))),
        out_shape=jax.ShapeDtypeStruct((N_CHIPS, r, cc), dtype), compiler_params=_cp("parallel"),
    )(place, w)


def _add_half(name, g, rcv, place):
    nq, r, cc = g.shape
    hr = r // 2

    def body(s_ref, g_ref, r_ref, o_ref):
        o_ref[...] = (g_ref[...] + r_ref[...]).astype(o_ref.dtype)

    spec = pl.BlockSpec((None, hr, cc), lambda i, s: (i, 0, 0))
    return _pallas(
        body, name=name,
        grid_spec=pltpu.PrefetchScalarGridSpec(
            num_scalar_prefetch=1, grid=(nq,),
            in_specs=[pl.BlockSpec((None, hr, cc), lambda i, s: (i, s[1], 0)), spec], out_specs=spec),
        out_shape=jax.ShapeDtypeStruct((nq, hr, cc), BF16), compiler_params=_cp("parallel"),
    )(place, g, rcv)


def _sum_owner(name, part, rcv, place):
    _, hr, cc = part.shape
    tr = _div_tile(hr, 128)
    nb = hr // tr

    def body(s_ref, p_ref, r_ref, o_ref):
        o_ref[...] = ((p_ref[...].astype(F32) + r_ref[0].astype(F32)) + r_ref[1].astype(F32)) + r_ref[2].astype(F32)

    return _pallas(
        body, name=name,
        grid_spec=pltpu.PrefetchScalarGridSpec(
            num_scalar_prefetch=1, grid=(nb,),
            in_specs=[pl.BlockSpec((None, tr, cc), lambda i, s: (s[0], i, 0)),
                      pl.BlockSpec((3, tr, cc), lambda i, s: (0, i, 0))],
            out_specs=pl.BlockSpec((tr, cc), lambda i, s: (s[1] * nb + i, 0))),
        out_shape=jax.ShapeDtypeStruct((2 * hr, cc), F32), compiler_params=_cp("parallel"),
    )(place, part, rcv)


def _sum_small(gathered, local, place):
    n = len(gathered)

    def body(s_ref, *refs):
        g_refs, l_refs, o_refs = refs[:n], refs[n:2 * n], refs[2 * n:]
        me = s_ref[2]
        for g_ref, l_ref, o_ref in zip(g_refs, l_refs, o_refs):
            acc = None
            for d in range(N_DEV):
                term = jnp.where(me == d, l_ref[...], g_ref[d])
                acc = term if acc is None else acc + term
            o_ref[...] = acc

    def whole(shape):
        return pl.BlockSpec(shape, lambda i, s, nd=len(shape): (0,) * nd)

    return _pallas(
        body, name="sum_small",
        grid_spec=pltpu.PrefetchScalarGridSpec(
            num_scalar_prefetch=1, grid=(1,),
            in_specs=[whole(g.shape) for g in gathered] + [whole(a.shape) for a in local],
            out_specs=tuple(whole(a.shape) for a in local)),
        out_shape=tuple(jax.ShapeDtypeStruct(a.shape, a.dtype) for a in local), compiler_params=_cp("arbitrary"),
    )(place, *gathered, *local)


def _adamw(name, w, g, m, v):
    r, cc = w.shape
    tr = r if r * cc <= 128 * 1024 else _div_tile(r, 256)

    def body(w_ref, g_ref, m_ref, v_ref, d_ref, mo_ref, vo_ref, go_ref):
        gv = g_ref[...]
        go_ref[...] = gv
        mn = ADAM_B1 * m_ref[...] + (1.0 - ADAM_B1) * gv
        vn = ADAM_B2 * v_ref[...] + (1.0 - ADAM_B2) * (gv * gv)
        m_hat = mn / (1.0 - ADAM_B1 ** ADAM_STEP)
        v_hat = vn / (1.0 - ADAM_B2 ** ADAM_STEP)
        d_ref[...] = -ADAM_LR * (m_hat / (jnp.sqrt(v_hat) + ADAM_EPS) + ADAM_WD * w_ref[...])
        mo_ref[...] = mn
        vo_ref[...] = vn

    spec = pl.BlockSpec((tr, cc), lambda i: (i, 0))
    sd = jax.ShapeDtypeStruct((r, cc), F32)
    return _pallas(
        body, name=name, grid=(r // tr,), in_specs=[spec] * 4, out_specs=(spec,) * 4, out_shape=(sd,) * 4,
        compiler_params=_cp("parallel"),
    )(w, g, m, v)


_BIG = ("w_in", "w_up", "w_branch", "w_mem_kv", "w_out", "w_down")
_BIG_SHARD_SHAPE = {"w_in": (1024, 1664), "w_up": (1024, 1408), "w_branch": (1536, 256),
                    "w_mem_kv": (256, 1024), "w_out": (256, 1024), "w_down": (704, 1024)}
_SMALL_SHAPE = {"norm1_g": (1, D_MODEL), "ln_v_g": (1, GM_WIDTH), "ln_v_b": (1, GM_WIDTH),
                "w_spatial": (GM_GROUPS * GM_CHUNK, GM_CHUNK), "b_spatial": (GM_GROUPS, GM_CHUNK),
                "lb_logits": (2, HG_HEADS * HG_DIM), "hgrn_norm_g": (1, HG_DIM), "mem_norm_g": (1, D_MODEL),
                "norm2_g": (1, D_MODEL), "conv_w": (3, D_FF), "conv_b": (1, D_FF), "final_g": (1, D_MODEL)}
_SMALL_EARLY = tuple(n for n in _SMALL_SHAPE if n != "norm1_g")
_PARAM_ORDER = ("norm1_g", "w_in", "ln_v_g", "ln_v_b", "w_spatial", "b_spatial", "lb_logits", "hgrn_norm_g",
                "mem_norm_g", "w_mem_kv", "w_branch", "w_out", "norm2_g", "w_up", "conv_w", "conv_b", "w_down",
                "final_g")


def _adamw_small(ws, gs, ms, vs):
    n = len(ws)

    def body(*refs):
        w_refs, g_refs, m_refs, v_refs = refs[:n], refs[n:2 * n], refs[2 * n:3 * n], refs[3 * n:4 * n]
        d_refs, mo_refs, vo_refs = refs[4 * n:5 * n], refs[5 * n:6 * n], refs[6 * n:]
        for k in range(n):
            gv = g_refs[k][...]
            mn = ADAM_B1 * m_refs[k][...] + (1.0 - ADAM_B1) * gv
            vn = ADAM_B2 * v_refs[k][...] + (1.0 - ADAM_B2) * (gv * gv)
            m_hat = mn / (1.0 - ADAM_B1 ** ADAM_STEP)
            v_hat = vn / (1.0 - ADAM_B2 ** ADAM_STEP)
            d_refs[k][...] = -ADAM_LR * (m_hat / (jnp.sqrt(v_hat) + ADAM_EPS) + ADAM_WD * w_refs[k][...])
            mo_refs[k][...] = mn
            vo_refs[k][...] = vn

    specs = [pl.BlockSpec(a.shape, lambda i, nd=a.ndim: (0,) * nd) for a in ws]
    shapes = tuple(jax.ShapeDtypeStruct(a.shape, F32) for a in ws)
    res = _pallas(
        body, name="adamw_small", grid=(1,), in_specs=specs * 4, out_specs=tuple(specs * 3), out_shape=shapes * 3,
        compiler_params=_cp("arbitrary"),
    )(*ws, *gs, *ms, *vs)
    return res[:n], res[n:2 * n], res[2 * n:]


class _Comm:
    _ROW_SHARDED = ("w_mem_kv", "w_out", "w_down")

    def __init__(self, slabs, place):
        self.slabs, self.place = slabs, place
        self.full, self.raw, self.parts, self.landing, self.bufs, self.done = {}, {}, {}, {}, {}, {}

    def w(self, name):
        a = self.full[name]
        if name in self._ROW_SHARDED:
            return a.reshape(-1, a.shape[-1])
        if name == "conv_w":
            return jnp.transpose(a, (1, 0, 2)).reshape(3, 1, D_FF)
        return a

    sends = True

    def grad(self, name, arr, from_sibling=None):
        self.raw[name] = arr.reshape((N_CHIPS, -1, arr.shape[-1]))
        if from_sibling is not None:
            self.parts[name] = _add_half("rs_add_" + name, self.raw[name], from_sibling, self.place)

    def small_grads(self, arrays):
        self.small_local = list(arrays)

    def carry(self, tag, call):
        plan = self._plan(tag)
        if not plan:
            return call(())
        out, carried = call([ex for ex, _ in plan])
        for (_, deliver), res in zip(plan, carried):
            deliver(res)
        return out

    def finish(self, last_small):
        ex, deliver = self._share(["w_out", "w_branch", "w_mem_kv", "w_in"])
        shared, small = _run_exchanges("share_and_gather_last", [ex, _ex_gather_small(last_small)])
        deliver(shared)
        return self.done, self.small_local + list(last_small), self.small_everyone + small

    def _plan(self, tag):
        if tag == "norm1":
            def deliver(res):
                self.full["w_in"] = res[0]

            return [(_ex_gather_relay([self.slabs["w_in"]]), deliver)]
        if tag == "in_proj":
            return [self._gather_relay(["w_branch", "w_out", "w_mem_kv", "w_down"], 0.6), self._gather(["conv_w"])]
        if tag == "hgrn_fwd":
            return [self._gather_relay(["w_up"], 0.8)]
        if tag == "d_h2":
            return [self._to_sibling(["w_down", "w_up"])]
        if tag == "merge_bwd":
            return [self._to_owner(["w_up"], (0, 2))]
        if tag == "hgrn_bwd":
            return [self._to_owner(["w_down"]), self._to_owner(["w_up"], (1, 2)),
                    self._to_sibling(["w_out", "w_branch", "w_mem_kv"])]
        if tag == "g_w_in":
            def keep(res):
                self.small_everyone = res

            return [self._to_owner(["w_out", "w_branch", "w_mem_kv"]), (_ex_gather_small(self.small_local), keep)]
        if tag == "d_h":
            return [self._to_owner(["w_in"]), self._share(["w_down", "w_up"])]
        return []

    def _gather(self, names, part=(0, 1)):
        def deliver(res):
            self.slabs.update(zip(names, res))
            self.full.update(zip(names, res))

        return _ex_all_gather([self.slabs[n] for n in names], [n != "conv_w" for n in names], part), deliver

    def _gather_relay(self, names, mid_at):
        return _ex_gather_relay([self.slabs[n] for n in names], mid_at), lambda res: self.full.update(zip(names, res))

    def _to_sibling(self, names):
        def deliver(res):
            for n, r in zip(names, res):
                self.parts[n] = _add_half("rs_add_" + n, self.raw[n], r, self.place)

        return _ex_to_sibling([self.raw[n] for n in names]), deliver

    def _to_owner(self, names, part=(0, 1)):
        def deliver(res):
            for n, r in zip(names, res):
                if part[0] + 1 < part[1]:
                    self.landing[n] = r
                else:
                    self.bufs[n] = _sum_owner("rs_sum_" + n, self.parts[n], r, self.place)

        landing = [self.landing[n] for n in names] if part[0] else None
        return _ex_to_owner([self.parts[n] for n in names], part, landing), deliver

    def _share(self, names):
        return _ex_share_halves([self.bufs[n] for n in names]), lambda res: self.done.update(zip(names, res))


def kernel(x, mem, norm1_g, w_in, ln_v_g, ln_v_b, w_spatial, b_spatial, lb_logits, hgrn_norm_g, mem_norm_g, w_mem_kv, w_branch, w_out, norm2_g, w_up, conv_w, conv_b, w_down, final_g, loss_target, m_norm1_g, m_w_in, m_ln_v_g, m_ln_v_b, m_w_spatial, m_b_spatial, m_lb_logits, m_hgrn_norm_g, m_mem_norm_g, m_w_mem_kv, m_w_branch, m_w_out, m_norm2_g, m_w_up, m_conv_w, m_conv_b, m_w_down, m_final_g, v_norm1_g, v_w_in, v_ln_v_g, v_ln_v_b, v_w_spatial, v_b_spatial, v_lb_logits, v_hgrn_norm_g, v_mem_norm_g, v_w_mem_kv, v_w_branch, v_w_out, v_norm2_g, v_w_up, v_conv_w, v_conv_b, v_w_down, v_final_g):
    w = dict(norm1_g=norm1_g, w_in=w_in, ln_v_g=ln_v_g, ln_v_b=ln_v_b, w_spatial=w_spatial, b_spatial=b_spatial,
             lb_logits=lb_logits, hgrn_norm_g=hgrn_norm_g, mem_norm_g=mem_norm_g, w_mem_kv=w_mem_kv,
             w_branch=w_branch, w_out=w_out, norm2_g=norm2_g, w_up=w_up, conv_w=conv_w, conv_b=conv_b,
             w_down=w_down, final_g=final_g)
    mom = dict(norm1_g=m_norm1_g, w_in=m_w_in, ln_v_g=m_ln_v_g, ln_v_b=m_ln_v_b, w_spatial=m_w_spatial,
               b_spatial=m_b_spatial, lb_logits=m_lb_logits, hgrn_norm_g=m_hgrn_norm_g, mem_norm_g=m_mem_norm_g,
               w_mem_kv=m_w_mem_kv, w_branch=m_w_branch, w_out=m_w_out, norm2_g=m_norm2_g, w_up=m_w_up,
               conv_w=m_conv_w, conv_b=m_conv_b, w_down=m_w_down, final_g=m_final_g)
    var = dict(norm1_g=v_norm1_g, w_in=v_w_in, ln_v_g=v_ln_v_g, ln_v_b=v_ln_v_b, w_spatial=v_w_spatial,
               b_spatial=v_b_spatial, lb_logits=v_lb_logits, hgrn_norm_g=v_hgrn_norm_g, mem_norm_g=v_mem_norm_g,
               w_mem_kv=v_w_mem_kv, w_branch=v_w_branch, w_out=v_w_out, norm2_g=v_norm2_g, w_up=v_w_up,
               conv_w=v_conv_w, conv_b=v_conv_b, w_down=v_w_down, final_g=v_final_g)
    B, S, D = x.shape
    T = B * S
    ci = lax.axis_index("c")
    q = 2 * lax.axis_index("x") + lax.axis_index("y")
    place = jnp.stack([q, ci, 2 * q + ci]).astype(jnp.int32)

    shards = {n: w[n].reshape(_BIG_SHARD_SHAPE[n]) for n in _BIG}
    slabs = {"w_in": _cast_into_slab("slab_w_in", shards.pop("w_in"), place, BF16),
             "conv_w": _cast_into_slab("slab_conv_w", conv_w[0], place, F32)}
    comm = _Comm(slabs, place)
    p = dict(
        cast_beside_norm1=shards,
        norm1_g=norm1_g, ln_v_g=ln_v_g, ln_v_b=ln_v_b, w_spatial=w_spatial[0],
        b_spatial=b_spatial.reshape(GM_GROUPS, GM_CHUNK, 1), lb_logits=lb_logits, hgrn_norm_g=hgrn_norm_g,
        mem_norm_g=mem_norm_g, norm2_g=norm2_g, conv_b=conv_b, final_g=final_g.reshape(1, D))

    loss, grad_x, g = _local_step(x.reshape(T, D), mem.reshape(B * MEM_LEN, D), loss_target.reshape(T, D), p, comm,
                                  B, S)

    shard_grads, local_small, everyone = comm.finish([g["norm1_g"]])
    summed = _sum_small(everyone, local_small, place)
    small_names = list(_SMALL_EARLY) + ["norm1_g"]
    total = dict(zip(_SMALL_EARLY, summed))
    loss_total, total["norm1_g"] = summed[len(_SMALL_EARLY)][0, 0], summed[-1]

    grads, delta, new_m, new_v = {}, {}, {}, {}
    for n in _BIG:
        shp = _BIG_SHARD_SHAPE[n]
        delta[n], new_m[n], new_v[n], grads[n] = _adamw("adamw_" + n, w[n].reshape(shp), shard_grads[n],
                                                        mom[n].reshape(shp), var[n].reshape(shp))
    cw_shard = D_FF // N_CHIPS
    total["conv_w"] = lax.dynamic_slice(total["conv_w"], (0, q * cw_shard), (3, cw_shard)).reshape(3, 1, cw_shard)

    def flat2d(d, n):
        return d[n].reshape(total[n].shape)

    upd = _adamw_small([flat2d(w, n) for n in small_names], [total[n] for n in small_names],
                       [flat2d(mom, n) for n in small_names], [flat2d(var, n) for n in small_names])
    for k, n in enumerate(small_names):
        grads[n], delta[n], new_m[n], new_v[n] = total[n], upd[0][k], upd[1][k], upd[2][k]

    def shaped(d):
        return [d[n].reshape(w[n].shape) for n in _PARAM_ORDER]

    return (loss_total, grad_x.reshape(B, S, D), *shaped(grads), *shaped(delta), *shaped(new_m), *shaped(new_v))
```

```python
import functools
import math

import jax
import jax.numpy as jnp
from jax import lax
from jax.experimental import pallas as pl
from jax.experimental.pallas import tpu as pltpu

F32 = jnp.float32
BF16 = jnp.bfloat16
EPS = 1e-6

D_MODEL = 1024
MEM_LEN = 256
GM_WIDTH = 512
GM_CHUNK = 128
GM_GROUPS = 4
HG_HEADS = 4
HG_DIM = 128
HG_CHUNK = 64
XA_HEADS = 4
XA_DIM = 128
BR_WIDTH = 512
D_FF = 2816
IN_WIDTH = 6656
N_CHIPS = 4
N_DEV = 8

ADAM_LR = 0.001
ADAM_B1 = 0.9
ADAM_B2 = 0.999
ADAM_EPS = 1e-08
ADAM_WD = 0.01
ADAM_STEP = 10

COL_ZU, COL_ZV, COL_HQ, COL_HF, COL_HI, COL_HG, COL_XQ = 0, 1, 2, 3, 4, 5, 6
COL_GATE0 = 3584

VMEM_LIMIT_BYTES = 48 * 1024 * 1024
MESH_ID = pl.DeviceIdType.MESH


def _cp(*sem):
    return pltpu.CompilerParams(dimension_semantics=sem, vmem_limit_bytes=VMEM_LIMIT_BYTES)


def _pallas(body, *, out_shape, **kw):
    def pin(s):
        return pltpu.HBM(s.shape, s.dtype) if isinstance(s, jax.ShapeDtypeStruct) else s

    out_shape = tuple(pin(s) for s in out_shape) if isinstance(out_shape, (tuple, list)) else pin(out_shape)
    call = pl.pallas_call(body, out_shape=out_shape, **kw)

    def run(*operands):
        return call(*[pltpu.with_memory_space_constraint(o, pltpu.HBM) if jnp.issubdtype(o.dtype, jnp.floating)
                      else o for o in operands])

    return run


def _dot(a, b):
    return lax.dot_general(a.astype(BF16), b.astype(BF16), (((1,), (0,)), ((), ())), preferred_element_type=F32)


def _dot_nt(a, b):
    return lax.dot_general(a.astype(BF16), b.astype(BF16), (((1,), (1,)), ((), ())), preferred_element_type=F32)


def _dot_tn(a, b):
    return lax.dot_general(a.astype(BF16), b.astype(BF16), (((0,), (0,)), ((), ())), preferred_element_type=F32)


def _dot_01(mask01, x):
    hi = x.astype(BF16)
    r1 = x - hi.astype(F32)
    mid = r1.astype(BF16)
    lo = (r1 - mid.astype(F32)).astype(BF16)
    m = mask01.astype(BF16)
    dn = (((1,), (0,)), ((), ()))
    return (lax.dot_general(m, hi, dn, preferred_element_type=F32)
            + lax.dot_general(m, mid, dn, preferred_element_type=F32)
            + lax.dot_general(m, lo, dn, preferred_element_type=F32))


def _sigmoid(z):
    return 1.0 / (1.0 + jnp.exp(-z))


_GELU_C = math.sqrt(2.0 / math.pi)


def _gelu_and_grad(z):
    inner = _GELU_C * (z + 0.044715 * z * z * z)
    t = jnp.tanh(inner)
    val = 0.5 * z * (1.0 + t)
    grad = 0.5 * (1.0 + t) + 0.5 * z * (1.0 - t * t) * _GELU_C * (1.0 + 3.0 * 0.044715 * z * z)
    return val, grad


def _row_tile(n, want=512):
    t = min(want, n)
    assert n % t == 0
    return t


def _pcall(body, operands, *, name, grid, in_specs, out_specs, out_shape, scratch_shapes=(), semantics, riders=(),
           prefetch=None):
    single = not isinstance(out_shape, (tuple, list))
    out_specs = (out_specs,) if single else tuple(out_specs)
    out_shape = (out_shape,) if single else tuple(out_shape)
    n_pre = 0 if prefetch is None else 1

    def call(fn, ins_, outs_, shapes_, scr_, ops, sem, aliases):
        if prefetch is None:
            return _pallas(fn, name=name, grid=grid, in_specs=ins_, out_specs=outs_, out_shape=shapes_,
                           scratch_shapes=scr_, input_output_aliases=aliases, compiler_params=_cp(*sem))(*ops)
        spec = pltpu.PrefetchScalarGridSpec(num_scalar_prefetch=1, grid=grid, in_specs=ins_, out_specs=outs_,
                                            scratch_shapes=scr_)
        return _pallas(fn, name=name, grid_spec=spec, out_shape=shapes_, input_output_aliases=aliases,
                       compiler_params=_cp(*sem))(prefetch, *ops)

    if not riders:
        res = call(body, list(in_specs), out_specs, out_shape, list(scratch_shapes), operands, semantics, {})
        return (res[0] if single else res), []
    n_in, n_out, n_scr = len(in_specs), len(out_shape), len(scratch_shapes)
    ex_in = [len(ex.operands) for ex in riders]
    ex_out = [len(ex.out_shape) for ex in riders]
    ex_scr = [len(ex.scratch) for ex in riders]
    tot_in, tot_out = n_in + sum(ex_in), n_out + sum(ex_out)

    def wrapped(*refs):
        pre, refs = refs[:n_pre], refs[n_pre:]
        ins, outs, scr = refs[:tot_in], refs[tot_in:tot_in + tot_out], refs[tot_in + tot_out:]
        ids = [pl.program_id(d) for d in range(len(grid))]
        first = functools.reduce(lambda p, t: p & t, [i == 0 for i in ids])
        last = functools.reduce(lambda p, t: p & t, [i == n - 1 for i, n in zip(ids, grid)])
        parts, oi, oo, os_ = [], n_in, n_out, n_scr
        for k in range(len(riders)):
            parts.append((ins[oi:oi + ex_in[k]], outs[oo:oo + ex_out[k]], scr[os_:os_ + ex_scr[k]]))
            oi, oo, os_ = oi + ex_in[k], oo + ex_out[k], os_ + ex_scr[k]

        @pl.when(first)
        def _():
            for ex, part in zip(riders, parts):
                ex.start(*part)

        step, total = 0, 1
        for i, n in zip(ids, grid):
            step, total = step * n + i, total * n
        for ex, part in zip(riders, parts):
            if ex.mid is not None:
                @pl.when(step == min(total - 1, int(total * ex.mid_at)))
                def _(ex=ex, part=part):
                    ex.mid(*part)

        body(*pre, *ins[:n_in], *outs[:n_out], *scr[:n_scr])

        @pl.when(last)
        def _():
            for ex, part in zip(riders, parts):
                ex.finish(*part)

    aliases, oi, oo = {}, n_in, n_out
    all_ops, all_shapes, all_scr = list(operands), list(out_shape), list(scratch_shapes)
    for k, ex in enumerate(riders):
        aliases.update({n_pre + oi + a: oo + b for a, b in ex.aliases.items()})
        oi, oo = oi + ex_in[k], oo + ex_out[k]
        all_ops += list(ex.operands)
        all_shapes += [pltpu.HBM(s.shape, s.dtype) for s in ex.out_shape]
        all_scr += list(ex.scratch)
    res = call(wrapped, list(in_specs) + [HBM_SPEC] * sum(ex_in), out_specs + (HBM_SPEC,) * sum(ex_out),
               tuple(all_shapes), all_scr, all_ops, ["arbitrary"] * len(grid), aliases)
    own = res[0] if single else tuple(res[:n_out])
    carried, oo = [], n_out
    for k in range(len(riders)):
        carried.append(list(res[oo:oo + ex_out[k]]))
        oo += ex_out[k]
    return own, carried


def _carried(out, carried, riders):
    return (out, carried) if riders else out


def _matmul(name, operands, *, grid, in_specs, o_spec, out_shape, out_dtype, dims, riders=()):
    nk = grid[2]
    assert nk == 1 or out_dtype == F32

    def body(a_ref, b_ref, o_ref):
        part = lax.dot_general(a_ref[...].astype(BF16), b_ref[...].astype(BF16), (dims, ((), ())),
                               preferred_element_type=F32)
        if nk == 1:
            o_ref[...] = part.astype(o_ref.dtype)
        else:
            k = pl.program_id(2)

            @pl.when(k == 0)
            def _():
                o_ref[...] = part

            @pl.when(k > 0)
            def _():
                o_ref[...] += part

    out, carried = _pcall(body, operands, name=name, grid=grid, in_specs=in_specs, out_specs=o_spec,
                          out_shape=jax.ShapeDtypeStruct(out_shape, out_dtype),
                          semantics=("parallel", "parallel", "arbitrary"), riders=riders)
    return (out, carried) if riders else out


NN = ((1,), (0,))
NT = ((1,), (1,))
TN = ((0,), (0,))
_TN_TOKENS = 4096


def _mm_cs(name, a, w, out_dtype, riders=()):
    M, K = a.shape
    nq, _, wd = w.shape
    tm = _row_tile(M)
    nt = M // tm
    ring = 3
    assert nq * nt >= ring

    def body(a_hbm, w_ref, o_ref, a_buf, a_sem):
        s = pl.program_id(0) * nt + pl.program_id(1)

        def fetch(t):
            return pltpu.make_async_copy(a_hbm.at[pl.ds((t % nt) * tm, tm), :], a_buf.at[t % ring], a_sem.at[t % ring])

        @pl.when(s == 0)
        def _():
            for t in range(ring - 1):
                fetch(t).start()

        @pl.when(s + ring - 1 < nq * nt)
        def _():
            fetch(s + ring - 1).start()

        fetch(s).wait()
        o_ref[...] = _dot(a_buf[s % ring], w_ref[...]).astype(o_ref.dtype)

    return _carried(*_pcall(
        body, (a, w), name=name, grid=(nq, nt),
        in_specs=[HBM_SPEC, pl.BlockSpec((None, K, wd), lambda j, i: (j, 0, 0))],
        out_specs=pl.BlockSpec((tm, wd), lambda j, i: (i, j)),
        out_shape=jax.ShapeDtypeStruct((M, nq * wd), out_dtype),
        scratch_shapes=[pltpu.VMEM((ring, tm, K), a.dtype), pltpu.SemaphoreType.DMA((ring,))],
        semantics=("arbitrary", "arbitrary"), riders=riders), riders)


def _mm_rs(name, a, w, out_dtype):
    M, K = a.shape
    N = w.shape[1]
    tm = _row_tile(M)
    return _matmul(name, (a, w), grid=(M // tm, 1, 1),
                   in_specs=[pl.BlockSpec((tm, K), lambda i, j, k: (i, 0)), pl.BlockSpec((K, N), lambda i, j, k: (0, 0))],
                   o_spec=pl.BlockSpec((tm, N), lambda i, j, k: (i, 0)),
                   out_shape=(M, N), out_dtype=out_dtype, dims=NN)


def _mm_nt_rs(name, g, w, out_dtype, riders=()):
    M, N = g.shape
    K = w.shape[0]
    to = K
    tm = _row_tile(M)
    return _matmul(name, (g, w), grid=(M // tm, K // to, 1),
                   in_specs=[pl.BlockSpec((tm, N), lambda i, j, k: (i, 0)),
                             pl.BlockSpec((to, N), lambda i, j, k: (j, 0))],
                   o_spec=pl.BlockSpec((tm, to), lambda i, j, k: (i, j)),
                   out_shape=(M, K), out_dtype=out_dtype, dims=NT, riders=riders)


def _mm_nt_cs(name, g, w, out_dtype, riders=(), stacked=False, norm_bwd=None):
    M = g.shape[-2]
    nq, K, wd = w.shape
    tm = _row_tile(M, 256)

    def product(g_ref, w_ref):
        acc = None
        for q in range(nq):
            gq = g_ref[q // 2, :, (q % 2) * wd:(q % 2 + 1) * wd] if stacked else g_ref[:, q * wd:(q + 1) * wd]
            part = _dot_nt(gq, w_ref[q])
            acc = part if acc is None else acc + part
        return acc

    def body(g_ref, w_ref, o_ref):
        o_ref[...] = product(g_ref, w_ref).astype(o_ref.dtype)

    def body_norm(g_ref, w_ref, x_ref, gain_ref, dr_ref, dx_ref, dg_ref):
        @pl.when(pl.program_id(0) == 0)
        def _():
            dg_ref[...] = jnp.zeros_like(dg_ref)

        dx, dg = _rms_bwd_rows(x_ref[...], gain_ref[...], product(g_ref, w_ref))
        dg_ref[...] += dg
        dx_ref[...] = dx + dr_ref[...]

    g_spec = (pl.BlockSpec((2, tm, 2 * wd), lambda i: (0, i, 0)) if stacked
              else pl.BlockSpec((tm, nq * wd), lambda i: (i, 0)))
    w_spec = pl.BlockSpec((nq, K, wd), lambda i: (0, 0, 0))
    row = pl.BlockSpec((tm, K), lambda i: (i, 0))
    if norm_bwd is None:
        return _carried(*_pcall(
            body, (g, w), name=name, grid=(M // tm,), in_specs=[g_spec, w_spec], out_specs=row,
            out_shape=jax.ShapeDtypeStruct((M, K), out_dtype), semantics=("parallel",), riders=riders), riders)
    vec = pl.BlockSpec((1, K), lambda i: (0, 0))
    return _carried(*_pcall(
        body_norm, (g, w) + tuple(norm_bwd), name=name, grid=(M // tm,),
        in_specs=[g_spec, w_spec, row, vec, row], out_specs=(row, vec),
        out_shape=(jax.ShapeDtypeStruct((M, K), F32), jax.ShapeDtypeStruct((1, K), F32)),
        semantics=("arbitrary",), riders=riders), riders)


def _mm_tn_rs(name, a, g, to, tn=512):
    T, M = a.shape
    N = g.shape[1]
    tt = _row_tile(T, _TN_TOKENS)
    tn = min(tn, N)
    return _matmul(name, (a, g), grid=(M // to, N // tn, T // tt),
                   in_specs=[pl.BlockSpec((tt, to), lambda i, j, k: (k, i)),
                             pl.BlockSpec((tt, tn), lambda i, j, k: (k, j))],
                   o_spec=pl.BlockSpec((to, tn), lambda i, j, k: (i, j)),
                   out_shape=(M, N), out_dtype=F32, dims=TN)


def _mm_tn_cs(name, a, g, nq, to, riders=(), stacked=False):
    T, M = a.shape
    wd = g.shape[-1] * (2 if stacked else 1) // nq
    tt = _row_tile(T, _TN_TOKENS)
    g_spec = (pl.BlockSpec((None, tt, wd), lambda i, j, k: (j // 2, k, j % 2)) if stacked
              else pl.BlockSpec((tt, wd), lambda i, j, k: (k, j)))
    return _matmul(name, (a, g), grid=(M // to, nq, T // tt),
                   in_specs=[pl.BlockSpec((tt, to), lambda i, j, k: (k, i)), g_spec],
                   o_spec=pl.BlockSpec((None, to, wd), lambda i, j, k: (j, i, 0)),
                   out_shape=(nq, M, wd), out_dtype=F32, dims=TN, riders=riders)


def _mm_tn_cs_to_sibling(name, a, g, nq, place, riders=(), send=True):
    T, M = a.shape
    wd = g.shape[-1] // nq
    to = M // 2
    steps = 2 * nq

    def body(s_ref, a_ref, g_ref, o_hbm, land_hbm, acc, wsem, send_sem, recv_sem):
        t = pl.program_id(0)
        c = s_ref[1]

        def writeback(tt):
            half = (tt // nq + 1 + c) % 2
            return pltpu.make_async_copy(acc.at[tt % 2], o_hbm.at[tt % nq, pl.ds(half * to, to), :], wsem.at[tt % 2])

        @pl.when(t >= 2)
        def _():
            writeback(t - 2).wait()

        if send:
            x, y, _, _ = _place()
            to_sibling = _remote(o_hbm.at[:, pl.ds((1 - c) * to, to), :], land_hbm, send_sem.at[0], recv_sem.at[0],
                                 (x, y, 1 - c))

            @pl.when(t == nq + 1)
            def _():
                to_sibling.start()

        acc[t % 2] = _dot_tn(a_ref[...], g_ref[...])
        writeback(t).start()

        @pl.when(t == steps - 1)
        def _():
            writeback(t - 1).wait()
            writeback(t).wait()
            if send:
                to_sibling.wait()

    out, carried = _pcall(
        body, (a, g), name=name, grid=(steps,),
        in_specs=[pl.BlockSpec((T, to), lambda t, s: (0, (t // nq + 1 + s[1]) % 2)),
                  pl.BlockSpec((T, wd), lambda t, s: (0, t % nq))],
        out_specs=(HBM_SPEC, HBM_SPEC),
        out_shape=(jax.ShapeDtypeStruct((nq, M, wd), F32), jax.ShapeDtypeStruct((nq, to, wd), F32)),
        scratch_shapes=[pltpu.VMEM((2, to, wd), F32), pltpu.SemaphoreType.DMA((2,)),
                        pltpu.SemaphoreType.DMA((1,)), pltpu.SemaphoreType.DMA((1,))],
        semantics=("arbitrary",), riders=riders, prefetch=place)
    return _carried(out, carried, riders)


def _rms_fwd(name, x, g, riders=()):
    T, D = x.shape
    tm = _row_tile(T)

    def body(x_ref, g_ref, o_ref):
        o_ref[...] = _rms_rows(x_ref[...], g_ref[...]).astype(o_ref.dtype)

    return _carried(*_pcall(
        body, (x, g), name=name, grid=(T // tm,),
        in_specs=[pl.BlockSpec((tm, D), lambda i: (i, 0)), pl.BlockSpec((1, D), lambda i: (0, 0))],
        out_specs=pl.BlockSpec((tm, D), lambda i: (i, 0)),
        out_shape=jax.ShapeDtypeStruct((T, D), BF16), semantics=("parallel",), riders=riders), riders)


_NORM1_STEPS = 4


def _norm1_and_casts(x, g, shards, place, riders=()):
    T, D = x.shape
    tm = T // _NORM1_STEPS
    names = list(shards)

    def body(s_ref, x_ref, g_ref, *refs):
        w_refs, o_ref, slab_refs = refs[:len(names)], refs[len(names)], refs[len(names) + 1:]
        o_ref[...] = _rms_rows(x_ref[...], g_ref[...]).astype(o_ref.dtype)
        for w_ref, slab_ref in zip(w_refs, slab_refs):
            slab_ref[...] = w_ref[...].astype(slab_ref.dtype)

    in_specs = [pl.BlockSpec((tm, D), lambda i, s: (i, 0)), pl.BlockSpec((1, D), lambda i, s: (0, 0))]
    out_specs = [pl.BlockSpec((tm, D), lambda i, s: (i, 0))]
    out_shape = [jax.ShapeDtypeStruct((T, D), BF16)]
    for n in names:
        r, cc = shards[n].shape
        assert r % (_NORM1_STEPS * 16) == 0
        in_specs.append(pl.BlockSpec((r // _NORM1_STEPS, cc), lambda i, s: (i, 0)))
        out_specs.append(pl.BlockSpec((None, r // _NORM1_STEPS, cc), lambda i, s: (s[0], i, 0)))
        out_shape.append(jax.ShapeDtypeStruct((N_CHIPS, r, cc), BF16))
    out, carried = _pcall(body, (x, g, *[shards[n] for n in names]), name="norm1", grid=(_NORM1_STEPS,),
                          in_specs=in_specs, out_specs=out_specs, out_shape=out_shape, semantics=("parallel",),
                          riders=riders, prefetch=place)
    return _carried((out[0], dict(zip(names, out[1:]))), carried, riders)


def _rms_rows(xv, gain):
    return xv * lax.rsqrt(jnp.mean(xv * xv, axis=-1, keepdims=True) + EPS) * gain


def _rms_bwd_rows(xv, gain, dh):
    r = lax.rsqrt(jnp.mean(xv * xv, axis=-1, keepdims=True) + EPS)
    n = xv * r
    dn = dh * gain
    return r * (dn - n * jnp.mean(dn * n, axis=-1, keepdims=True)), jnp.sum(dh * n, axis=0, keepdims=True)


def _rms_bwd(name, x, g, dh, dres):
    T, D = x.shape
    tm = _row_tile(T)
    has_res = dres is not None

    def body(*refs):
        if has_res:
            x_ref, g_ref, dh_ref, dr_ref, dx_ref, dg_ref = refs
        else:
            x_ref, g_ref, dh_ref, dx_ref, dg_ref = refs

        @pl.when(pl.program_id(0) == 0)
        def _():
            dg_ref[...] = jnp.zeros_like(dg_ref)

        dx, dg = _rms_bwd_rows(x_ref[...], g_ref[...], dh_ref[...])
        dg_ref[...] += dg
        if has_res:
            dx = dx + dr_ref[...]
        dx_ref[...] = dx

    row = pl.BlockSpec((tm, D), lambda i: (i, 0))
    vec = pl.BlockSpec((1, D), lambda i: (0, 0))
    ops = (x, g, dh, dres) if has_res else (x, g, dh)
    return _pallas(
        body, name=name, grid=(T // tm,), in_specs=[row, vec, row] + ([row] if has_res else []),
        out_specs=(row, vec),
        out_shape=(jax.ShapeDtypeStruct((T, D), F32), jax.ShapeDtypeStruct((1, D), F32)),
        compiler_params=_cp("arbitrary"),
    )(*ops)


def _proj_res_norm(name, a, w, res, gain):
    M, K = a.shape
    N = w.shape[1]
    tm = _row_tile(M)

    def body(a_ref, w_ref, r_ref, g_ref, x_ref, h_ref):
        xv = _dot(a_ref[...], w_ref[...]) + r_ref[...]
        x_ref[...] = xv
        h_ref[...] = _rms_rows(xv, g_ref[...]).astype(h_ref.dtype)

    row = pl.BlockSpec((tm, N), lambda i: (i, 0))
    return _pallas(
        body, name=name, grid=(M // tm,),
        in_specs=[pl.BlockSpec((tm, K), lambda i: (i, 0)), pl.BlockSpec((K, N), lambda i: (0, 0)), row,
                  pl.BlockSpec((1, N), lambda i: (0, 0))],
        out_specs=(row, row), out_shape=(jax.ShapeDtypeStruct((M, N), F32), jax.ShapeDtypeStruct((M, N), BF16)),
        compiler_params=_cp("parallel"),
    )(a, w, res, gain)


def _proj_res_loss(name, a, w, res, tgt, gain):
    M, K = a.shape
    D = w.shape[1]
    tm = _row_tile(M)

    def body(a_ref, w_ref, r_ref, t_ref, g_ref, dx_ref, dg_ref, loss_ref):
        @pl.when(pl.program_id(0) == 0)
        def _():
            dg_ref[...] = jnp.zeros_like(dg_ref)
            loss_ref[...] = jnp.zeros_like(loss_ref)

        xv = _dot(a_ref[...], w_ref[...]) + r_ref[...]
        gv = g_ref[...]
        diff = _rms_rows(xv, gv) - t_ref[...]
        loss_ref[...] += 0.5 * jnp.sum(jnp.mean(diff * diff, axis=-1, keepdims=True))
        dx, dg = _rms_bwd_rows(xv, gv, diff * (1.0 / D))
        dg_ref[...] += dg
        dx_ref[...] = dx

    row = pl.BlockSpec((tm, D), lambda i: (i, 0))
    vec = pl.BlockSpec((1, D), lambda i: (0, 0))
    return _pallas(
        body, name=name, grid=(M // tm,),
        in_specs=[pl.BlockSpec((tm, K), lambda i: (i, 0)), pl.BlockSpec((K, D), lambda i: (0, 0)), row, row, vec],
        out_specs=(row, vec, pl.BlockSpec((8, 128), lambda i: (0, 0))),
        out_shape=(jax.ShapeDtypeStruct((M, D), F32), jax.ShapeDtypeStruct((1, D), F32),
                   jax.ShapeDtypeStruct((8, 128), F32)),
        compiler_params=_cp("arbitrary"),
    )(a, w, res, tgt, gain)


def _gmlp_pieces(zu, zv, lng, lnb, ws_ref, bs_ref):
    u, du = _gelu_and_grad(zu)
    v, dv = _gelu_and_grad(zv)
    mu = jnp.mean(v, axis=-1, keepdims=True)
    vc = v - mu
    rstd = lax.rsqrt(jnp.mean(vc * vc, axis=-1, keepdims=True) + EPS)
    vhat = vc * rstd
    vn = vhat * lng + lnb
    row = lax.broadcasted_iota(jnp.int32, (GM_CHUNK, GM_CHUNK), 0)
    col = lax.broadcasted_iota(jnp.int32, (GM_CHUNK, GM_CHUNK), 1)
    tril = row >= col
    wms, mixed = [], []
    for g in range(GM_GROUPS):
        sl = slice(g * 128, (g + 1) * 128)
        wm = jnp.where(tril, ws_ref[g], 0.0)
        wms.append(wm)
        mixed.append(_dot(wm, vn[:, sl]) + bs_ref[g])
    return u, du, dv, rstd, vhat, vn, wms, mixed, tril


def _gmlp_fwd(proj, lng, lnb, ws, bs_col):
    T = proj.shape[0]
    n = T // GM_CHUNK

    ring = 3
    assert COL_ZV == COL_ZU + 1 and n >= ring

    def body(proj_hbm, lng_ref, lnb_ref, ws_ref, bs_ref, o_ref, z_buf, z_sem):
        s = pl.program_id(0)

        def fetch(t):
            return pltpu.make_async_copy(
                proj_hbm.at[pl.ds(t * GM_CHUNK, GM_CHUNK), pl.ds(COL_ZU * GM_WIDTH, 2 * GM_WIDTH)],
                z_buf.at[t % ring], z_sem.at[t % ring])

        @pl.when(s == 0)
        def _():
            for t in range(ring - 1):
                fetch(t).start()

        @pl.when(s + ring - 1 < n)
        def _():
            fetch(s + ring - 1).start()

        fetch(s).wait()
        z = z_buf[s % ring].astype(F32)
        u, _, _, _, _, _, _, mixed, _ = _gmlp_pieces(z[:, :GM_WIDTH], z[:, GM_WIDTH:], lng_ref[...], lnb_ref[...],
                                                     ws_ref, bs_ref)
        for g in range(GM_GROUPS):
            sl = slice(g * 128, (g + 1) * 128)
            o_ref[:, sl] = (u[:, sl] * mixed[g]).astype(o_ref.dtype)

    vec = pl.BlockSpec((1, GM_WIDTH), lambda i: (0, 0))
    return _pallas(
        body, name="gmlp_fwd", grid=(n,),
        in_specs=[HBM_SPEC, vec, vec,
                  pl.BlockSpec((GM_GROUPS, 128, 128), lambda i: (0, 0, 0)),
                  pl.BlockSpec((GM_GROUPS, 128, 1), lambda i: (0, 0, 0))],
        out_specs=pl.BlockSpec((GM_CHUNK, 512), lambda i: (i, 0)),
        out_shape=jax.ShapeDtypeStruct((T, GM_WIDTH), BF16),
        scratch_shapes=[pltpu.VMEM((ring, GM_CHUNK, 2 * GM_WIDTH), BF16), pltpu.SemaphoreType.DMA((ring,))],
        compiler_params=_cp("arbitrary"),
    )(proj, lng, lnb, ws, bs_col)


def _gmlp_bwd(proj, d_out, lng, lnb, ws, bs_col, riders=()):
    T = proj.shape[0]
    n = T // GM_CHUNK

    def body(zu_ref, zv_ref, do_ref, lng_ref, lnb_ref, ws_ref, bs_ref,
             dz_ref, dws_ref, dbs_ref, dlng_ref, dlnb_ref, dm_acc):
        i = pl.program_id(0)

        @pl.when(i == 0)
        def _():
            dws_ref[...] = jnp.zeros_like(dws_ref)
            dlng_ref[...] = jnp.zeros_like(dlng_ref)
            dlnb_ref[...] = jnp.zeros_like(dlnb_ref)
            dm_acc[...] = jnp.zeros_like(dm_acc)

        lng_v = lng_ref[...]
        u, du, dv, rstd, vhat, vn, wms, mixed, tril = _gmlp_pieces(zu_ref[...].astype(F32), zv_ref[...].astype(F32),
                                                                  lng_v, lnb_ref[...],
                                                                  ws_ref, bs_ref)
        do = do_ref[...]
        dvn_parts = []
        for g in range(GM_GROUPS):
            sl = slice(g * 128, (g + 1) * 128)
            dog = do[:, sl]
            dz_ref[:, sl] = (dog * mixed[g] * du[:, sl]).astype(dz_ref.dtype)
            dmix = dog * u[:, sl]
            dm_acc[:, sl] += dmix
            dws_ref[g] += jnp.where(tril, _dot_nt(dmix, vn[:, sl]), 0.0)
            dvn_parts.append(_dot_tn(wms[g], dmix))
        dvn = jnp.concatenate(dvn_parts, axis=1)
        dlng_ref[...] += jnp.sum(dvn * vhat, axis=0, keepdims=True)
        dlnb_ref[...] += jnp.sum(dvn, axis=0, keepdims=True)
        dvh = dvn * lng_v
        dvv = rstd * (dvh - jnp.mean(dvh, axis=-1, keepdims=True)
                      - vhat * jnp.mean(dvh * vhat, axis=-1, keepdims=True))
        dz_ref[:, GM_WIDTH:] = (dvv * dv).astype(dz_ref.dtype)

        @pl.when(i == n - 1)
        def _():
            for g in range(GM_GROUPS):
                dbs_ref[g] = jnp.sum(dm_acc[:, g * 128:(g + 1) * 128], axis=1, keepdims=True)

    vec = pl.BlockSpec((1, GM_WIDTH), lambda i: (0, 0))
    wsp = pl.BlockSpec((GM_GROUPS, 128, 128), lambda i: (0, 0, 0))
    bsp = pl.BlockSpec((GM_GROUPS, 128, 1), lambda i: (0, 0, 0))
    return _carried(*_pcall(
        body, (proj, proj, d_out, lng, lnb, ws, bs_col), name="gmlp_bwd", grid=(n,),
        in_specs=[pl.BlockSpec((GM_CHUNK, 512), lambda i: (i, COL_ZU)),
                  pl.BlockSpec((GM_CHUNK, 512), lambda i: (i, COL_ZV)),
                  pl.BlockSpec((None, GM_CHUNK, 512), lambda i: (0, i, 0)), vec, vec, wsp, bsp],
        out_specs=(pl.BlockSpec((GM_CHUNK, 2 * GM_WIDTH), lambda i: (i, 0)), wsp, bsp, vec, vec),
        out_shape=(jax.ShapeDtypeStruct((T, 2 * GM_WIDTH), BF16),
                   jax.ShapeDtypeStruct((GM_GROUPS, 128, 128), F32), jax.ShapeDtypeStruct((GM_GROUPS, 128, 1), F32),
                   jax.ShapeDtypeStruct((1, GM_WIDTH), F32), jax.ShapeDtypeStruct((1, GM_WIDTH), F32)),
        scratch_shapes=[pltpu.VMEM((GM_CHUNK, GM_WIDTH), F32)],
        semantics=("arbitrary",), riders=riders), riders)


def _hgrn_lower_bound(lbl):
    return 1.0 / (1.0 + jnp.exp(lbl[1:2, :] - lbl[0:1, :]))


def _hgrn_gates(hq, hf, lb):
    C = HG_CHUNK
    sg = _sigmoid(hf)
    fg = lb + (1.0 - lb) * sg
    sq = _sigmoid(hq)
    row = lax.broadcasted_iota(jnp.int32, (C, C), 0)
    col = lax.broadcasted_iota(jnp.int32, (C, C), 1)
    tril = row >= col
    logf = jnp.log(fg)
    a = _dot_01(tril, logf)
    a_last = jnp.sum(logf, axis=0, keepdims=True)
    first_half = lax.broadcasted_iota(jnp.int32, logf.shape, 0) < (C // 2)
    a_mid = jnp.sum(jnp.where(first_half, logf, 0.0), axis=0, keepdims=True)
    ea, ei, eki, ekl = jnp.exp(a), jnp.exp(a - a_mid), jnp.exp(a_mid - a), jnp.exp(a_last - a)
    k = 1.0 - fg
    q = hq * sq
    qi = (q * ei).astype(BF16).astype(F32)
    ki = (k * eki).astype(BF16).astype(F32)
    return dict(sg=sg, fg=fg, sq=sq, tril=tril, ea=ea, ei=ei, eki=eki, ekl=ekl, e_last=jnp.exp(a_last),
                qe=q * ea, qi=qi, ki=ki, kl=k * ekl)


def _heads(x):
    return [x[:, h * HG_DIM:(h + 1) * HG_DIM] for h in range(HG_HEADS)]


def _hgrn_fwd(proj, lbl, gh, B, S, riders=()):
    C = HG_CHUNK
    NC = S // C
    W = HG_HEADS * HG_DIM

    def body(q_ref, f_ref, i_ref, g_ref, lbl_ref, gh_ref, o_ref, bo_ref, st_ref, state):
        @pl.when(pl.program_id(0) == 0)
        def _():
            state[...] = jnp.zeros_like(state)

        lb = _hgrn_lower_bound(lbl_ref[...])
        ghv = gh_ref[...]
        for b in range(B):
            gt = _hgrn_gates(q_ref[b].astype(F32), f_ref[b].astype(F32), lb)
            v = _heads(i_ref[b])
            qe, qi, ki, kl, e_last = (_heads(gt[n]) for n in ("qe", "qi", "ki", "kl", "e_last"))
            outs, normed = [], []
            for h in range(HG_HEADS):
                p = jnp.where(gt["tril"], _dot_nt(qi[h], ki[h]), 0.0)
                st = state[b, h]
                st_ref[b, h] = st
                o = _dot_nt(qe[h], st) + _dot(p, v[h])
                state[b, h] = st * e_last[h] + _dot_tn(v[h], kl[h])
                outs.append(o)
                normed.append(o * lax.rsqrt(jnp.mean(o * o, axis=-1, keepdims=True) + EPS) * ghv)
            o_ref[b] = jnp.concatenate(outs, axis=1)
            hg = g_ref[b].astype(F32)
            bo_ref[b] = (jnp.concatenate(normed, axis=1) * (hg * _sigmoid(hg))).astype(bo_ref.dtype)

    def col(cb):
        return pl.BlockSpec((B, C, 512), lambda c: (0, c, cb))

    tile = pl.BlockSpec((B, C, W), lambda c: (0, c, 0))
    proj3 = proj.reshape(B, S, proj.shape[-1])
    out, carried = _pcall(
        body, (proj3, proj3, proj3, proj3, lbl, gh), name="hgrn_fwd", grid=(NC,),
        in_specs=[col(COL_HQ), col(COL_HF), col(COL_HI), col(COL_HG),
                  pl.BlockSpec((2, W), lambda c: (0, 0)), pl.BlockSpec((1, HG_DIM), lambda c: (0, 0))],
        out_specs=(tile, tile, pl.BlockSpec((B, None, HG_HEADS, 128, 128), lambda c: (0, c, 0, 0, 0))),
        out_shape=(jax.ShapeDtypeStruct((B, S, W), F32), jax.ShapeDtypeStruct((B, S, W), BF16),
                   jax.ShapeDtypeStruct((B, NC, HG_HEADS, 128, 128), F32)),
        scratch_shapes=[pltpu.VMEM((B, HG_HEADS, 128, 128), F32)],
        semantics=("arbitrary",), riders=riders)
    o_h, b_out, states = out
    out = (o_h, b_out.reshape(B * S, W), states)
    return (out, carried) if riders else out


def _hgrn_bwd(proj, o_saved, states, d_out, lbl, gh, others, B, S, riders=()):
    C = HG_CHUNK
    NC = S // C
    W = HG_HEADS * HG_DIM
    d_gm, d_xq, d_gates = (t.reshape(B, S, t.shape[-1]) for t in others)
    own0 = d_gm.shape[-1]
    xq0 = own0 + 4 * W
    gates0 = xq0 + d_xq.shape[-1]

    def body(q_ref, f_ref, i_ref, g_ref, o_ref, st_ref, do_ref, lbl_ref, gh_ref, gm_ref, xq_ref, gates_ref,
             d_ref, dlbl_ref, dgh_ref, dstate, dlb_acc):
        c = pl.program_id(0)
        d_ref[:, :, :own0] = gm_ref[...]
        d_ref[:, :, xq0:gates0] = xq_ref[...]
        d_ref[:, :, gates0:] = gates_ref[...]

        def put(b, k, val):
            d_ref[b, :, own0 + k * W:own0 + (k + 1) * W] = val.astype(d_ref.dtype)

        @pl.when(c == 0)
        def _():
            dstate[...] = jnp.zeros_like(dstate)
            dgh_ref[...] = jnp.zeros_like(dgh_ref)
            dlb_acc[...] = jnp.zeros_like(dlb_acc)

        lb = _hgrn_lower_bound(lbl_ref[...])
        ghv = gh_ref[...]
        row = lax.broadcasted_iota(jnp.int32, (C, C), 0)
        colm = lax.broadcasted_iota(jnp.int32, (C, C), 1)
        triu = colm >= row
        for b in range(B):
            hq, hg = q_ref[b].astype(F32), g_ref[b].astype(F32)
            gt = _hgrn_gates(hq, f_ref[b].astype(F32), lb)
            tril = gt["tril"]
            v = _heads(i_ref[b])
            qe, qi, ki, kl, e_last = (_heads(gt[n]) for n in ("qe", "qi", "ki", "kl", "e_last"))
            sgg = _sigmoid(hg)
            don_all = do_ref[b] * (hg * sgg)
            o, don = _heads(o_ref[b]), _heads(don_all)
            d_qe, d_qi, d_ki, d_kl, dv, n_all, dal = [], [], [], [], [], [], []
            for h in range(HG_HEADS):
                r = lax.rsqrt(jnp.mean(o[h] * o[h], axis=-1, keepdims=True) + EPS)
                n = o[h] * r
                n_all.append(n)
                dgh_ref[...] += jnp.sum(don[h] * n, axis=0, keepdims=True)
                dn = don[h] * ghv
                d_o = r * (dn - n * jnp.mean(dn * n, axis=-1, keepdims=True))
                st, dst = st_ref[b, h], dstate[b, h]
                p = jnp.where(tril, _dot_nt(qi[h], ki[h]), 0.0)
                dp = jnp.where(tril, _dot_nt(d_o, v[h]), 0.0)
                d_qe.append(_dot(d_o, st))
                d_qi.append(_dot(dp, ki[h]))
                d_ki.append(_dot_tn(dp, qi[h]))
                d_kl.append(_dot(v[h], dst))
                dv.append(_dot_tn(p, d_o) + _dot_nt(kl[h], dst))
                dstate[b, h] = dst * e_last[h] + _dot_tn(d_o, qe[h])
                dal.append(jnp.sum(dst * st, axis=0, keepdims=True) * e_last[h])
            d_qe, d_qi, d_ki, d_kl, n_all, dal = (jnp.concatenate(t, axis=1)
                                                  for t in (d_qe, d_qi, d_ki, d_kl, n_all, dal))
            put(b, 3, do_ref[b] * n_all * jnp.tile(ghv, (1, HG_HEADS)) * (sgg * (1.0 + hg * (1.0 - sgg))))
            put(b, 2, jnp.concatenate(dv, axis=1))
            d_a_last = dal + jnp.sum(d_kl * gt["kl"], axis=0, keepdims=True)
            dq = d_qe * gt["ea"] + d_qi * gt["ei"]
            dk = d_ki * gt["eki"] + d_kl * gt["ekl"]
            da = d_qe * gt["qe"] + d_qi * gt["qi"] - d_ki * gt["ki"] - d_kl * gt["kl"]
            dlogf = _dot_01(triu, da) + d_a_last
            sg, sq = gt["sg"], gt["sq"]
            dfg = dlogf / gt["fg"] - dk
            put(b, 1, dfg * (1.0 - lb) * sg * (1.0 - sg))
            dlb_acc[...] += jnp.sum(dfg * (1.0 - sg), axis=0, keepdims=True)
            put(b, 0, dq * (sq * (1.0 + hq * (1.0 - sq))))

        @pl.when(c == NC - 1)
        def _():
            dlb = dlb_acc[...]
            first = lax.broadcasted_iota(jnp.int32, (2, W), 0) == 0
            dlbl_ref[...] = jnp.where(first, dlb * lb * (1.0 - lb), -dlb * lb * (1.0 - lb))

    def col(cb):
        return pl.BlockSpec((B, C, 512), lambda c: (0, NC - 1 - c, cb))

    tile = pl.BlockSpec((B, C, W), lambda c: (0, NC - 1 - c, 0))
    proj3 = proj.reshape(B, S, proj.shape[-1])

    def rows(width):
        return pl.BlockSpec((B, C, width), lambda c: (0, NC - 1 - c, 0))

    width = proj.shape[-1]
    out, carried = _pcall(
        body, (proj3, proj3, proj3, proj3, o_saved, states, d_out.reshape(3, B, S, W), lbl, gh, d_gm, d_xq, d_gates),
        name="hgrn_bwd", grid=(NC,),
        in_specs=[col(COL_HQ), col(COL_HF), col(COL_HI), col(COL_HG), tile,
                  pl.BlockSpec((B, None, HG_HEADS, 128, 128), lambda c: (0, NC - 1 - c, 0, 0, 0)),
                  pl.BlockSpec((None, B, C, W), lambda c: (1, 0, NC - 1 - c, 0)),
                  pl.BlockSpec((2, W), lambda c: (0, 0)), pl.BlockSpec((1, HG_DIM), lambda c: (0, 0)),
                  rows(d_gm.shape[-1]), rows(d_xq.shape[-1]), rows(d_gates.shape[-1])],
        out_specs=(rows(width), pl.BlockSpec((2, W), lambda c: (0, 0)), pl.BlockSpec((1, HG_DIM), lambda c: (0, 0))),
        out_shape=(jax.ShapeDtypeStruct((B, S, width), BF16), jax.ShapeDtypeStruct((2, W), F32),
                   jax.ShapeDtypeStruct((1, HG_DIM), F32)),
        scratch_shapes=[pltpu.VMEM((B, HG_HEADS, 128, 128), F32), pltpu.VMEM((1, W), F32)],
        semantics=("arbitrary",), riders=riders)
    out = (out[0].reshape(B * S, width),) + tuple(out[1:])
    return (out, carried) if riders else out


_XA_SCALE = XA_DIM ** -0.5


def _attn_probs(qh, kh):
    s = _dot_nt(qh, kh) * _XA_SCALE
    e = jnp.exp(s - jnp.max(s, axis=-1, keepdims=True))
    return e / jnp.sum(e, axis=-1, keepdims=True)


def _attn_fwd(proj, kv, B, S):
    T = B * S
    tq = _row_tile(S)
    nq = S // tq
    W = XA_HEADS * XA_DIM

    def body(q_ref, kv_ref, o_ref):
        for h in range(XA_HEADS):
            sl = slice(h * 128, (h + 1) * 128)
            p = _attn_probs(q_ref[:, sl], kv_ref[:, sl])
            o_ref[:, sl] = _dot(p, kv_ref[:, W + h * 128:W + (h + 1) * 128]).astype(o_ref.dtype)

    return _pallas(
        body, name="attn_fwd", grid=(B, nq),
        in_specs=[pl.BlockSpec((tq, 512), lambda b, i: (b * nq + i, COL_XQ)),
                  pl.BlockSpec((MEM_LEN, 2 * W), lambda b, i: (b, 0))],
        out_specs=pl.BlockSpec((tq, W), lambda b, i: (b * nq + i, 0)),
        out_shape=jax.ShapeDtypeStruct((T, W), BF16), compiler_params=_cp("parallel", "parallel"),
    )(proj, kv)


def _attn_bwd(proj, kv, d_out, B, S):
    T = B * S
    tq = _row_tile(S)
    nq = S // tq
    W = XA_HEADS * XA_DIM

    def body(q_ref, kv_ref, do_ref, dq_ref, dkv_ref):
        @pl.when(pl.program_id(1) == 0)
        def _():
            dkv_ref[...] = jnp.zeros_like(dkv_ref)

        for h in range(XA_HEADS):
            sl = slice(h * 128, (h + 1) * 128)
            slv = slice(W + h * 128, W + (h + 1) * 128)
            qh = q_ref[:, sl]
            kh = kv_ref[:, sl]
            p = _attn_probs(qh, kh)
            dc = do_ref[:, sl]
            dp = _dot_nt(dc, kv_ref[:, slv])
            ds = p * (dp - jnp.sum(dp * p, axis=-1, keepdims=True)) * _XA_SCALE
            dq_ref[:, sl] = _dot(ds, kh).astype(dq_ref.dtype)
            dkv_ref[:, sl] += _dot_tn(ds, qh)
            dkv_ref[:, slv] += _dot_tn(p, dc)

    kvspec = pl.BlockSpec((MEM_LEN, 2 * W), lambda b, i: (b, 0))
    tile = pl.BlockSpec((tq, W), lambda b, i: (b * nq + i, 0))
    return _pallas(
        body, name="attn_bwd", grid=(B, nq),
        in_specs=[pl.BlockSpec((tq, 512), lambda b, i: (b * nq + i, COL_XQ)), kvspec,
                  pl.BlockSpec((None, tq, W), lambda b, i: (2, b * nq + i, 0))],
        out_specs=(tile, kvspec),
        out_shape=(jax.ShapeDtypeStruct((T, W), BF16), jax.ShapeDtypeStruct((B * MEM_LEN, 2 * W), F32)),
        compiler_params=_cp("parallel", "arbitrary"),
    )(proj, kv, d_out)


_MERGE_TM = 256
_GATE_W = 512


def _gate_specs(tm):
    base = COL_GATE0 // _GATE_W
    return [pl.BlockSpec((tm, _GATE_W), functools.partial(lambda i, k: (i, base + k), k=k)) for k in range(6)]


def _merge_fwd(a_out, b_out, c_out, wb, proj, riders=()):
    T = a_out.shape[0]
    tm = _row_tile(T, _MERGE_TM)
    nq, _, wd = wb.shape
    per_half = _GATE_W // wd

    def body(a_ref, b_ref, c_ref, w_ref, *rest):
        gates, (m_ref, up_ref) = rest[:6], rest[6:]
        for hf in range(2):
            cols = slice(hf * _GATE_W, (hf + 1) * _GATE_W)
            acc = None
            for n, br in enumerate((a_ref, b_ref, c_ref)):
                x = br[...]
                up = jnp.concatenate([_dot(x, w_ref[per_half * hf + j, n * BR_WIDTH:(n + 1) * BR_WIDTH, :])
                                      for j in range(per_half)], axis=1)
                up_ref[n, :, cols] = up.astype(up_ref.dtype)
                term = _sigmoid(gates[2 * n + hf][...].astype(F32)) * up
                acc = term if acc is None else acc + term
            m_ref[:, cols] = acc.astype(m_ref.dtype)

    br_spec = pl.BlockSpec((tm, BR_WIDTH), lambda i: (i, 0))
    return _carried(*_pcall(
        body, (a_out, b_out, c_out, wb, *([proj] * 6)), name="merge_fwd", grid=(T // tm,),
        in_specs=[br_spec, br_spec, br_spec,
                  pl.BlockSpec((nq, 3 * BR_WIDTH, wd), lambda i: (0, 0, 0))] + _gate_specs(tm),
        out_specs=(pl.BlockSpec((tm, D_MODEL), lambda i: (i, 0)), pl.BlockSpec((3, tm, D_MODEL), lambda i: (0, i, 0))),
        out_shape=(jax.ShapeDtypeStruct((T, D_MODEL), BF16), jax.ShapeDtypeStruct((3, T, D_MODEL), BF16)),
        semantics=("parallel",), riders=riders), riders)


def _branch_bwd_act(d_ups, wb, riders=()):
    _, T, D = d_ups.shape
    nq, _, wd = wb.shape
    tm = _row_tile(T)

    nt = T // tm
    ring = 3
    assert 3 * nt >= ring

    def body(d_hbm, w_ref, o_ref, d_buf, d_sem):
        s = pl.program_id(0) * nt + pl.program_id(1)

        def fetch(t):
            return pltpu.make_async_copy(d_hbm.at[t // nt, pl.ds((t % nt) * tm, tm), :], d_buf.at[t % ring],
                                         d_sem.at[t % ring])

        @pl.when(s == 0)
        def _():
            for t in range(ring - 1):
                fetch(t).start()

        @pl.when(s + ring - 1 < 3 * nt)
        def _():
            fetch(s + ring - 1).start()

        fetch(s).wait()
        acc = None
        for q in range(nq):
            part = _dot_nt(d_buf[s % ring, :, q * wd:(q + 1) * wd], w_ref[q])
            acc = part if acc is None else acc + part
        o_ref[...] = acc

    return _carried(*_pcall(
        body, (d_ups, wb), name="d_branch", grid=(3, nt),
        in_specs=[HBM_SPEC, pl.BlockSpec((nq, BR_WIDTH, wd), lambda n, i: (0, n, 0))],
        out_specs=pl.BlockSpec((None, tm, BR_WIDTH), lambda n, i: (n, i, 0)),
        out_shape=jax.ShapeDtypeStruct((3, T, BR_WIDTH), F32),
        scratch_shapes=[pltpu.VMEM((ring, tm, D), d_ups.dtype), pltpu.SemaphoreType.DMA((ring,))],
        semantics=("arbitrary", "arbitrary"), riders=riders), riders)


def _branch_bwd_weight(name, br, d_ups, n, into=None):
    T = br.shape[0]
    D = d_ups.shape[2]
    wd = D // N_CHIPS
    tt = _row_tile(T, _TN_TOKENS)
    n_br = d_ups.shape[0]

    def body(b_ref, d_ref, *rest):
        o_ref = rest[-1]
        k = pl.program_id(0)
        for q in range(N_CHIPS):
            part = _dot_tn(b_ref[...], d_ref[:, q * wd:(q + 1) * wd])

            @pl.when(k == 0)
            def _():
                o_ref[q] = part

            @pl.when(k > 0)
            def _():
                o_ref[q] += part

    return _pallas(
        body, name=name, grid=(T // tt,),
        in_specs=[pl.BlockSpec((tt, BR_WIDTH), lambda k: (k, 0)),
                  pl.BlockSpec((None, tt, D), lambda k: (n, k, 0))] + ([] if into is None else [HBM_SPEC]),
        out_specs=pl.BlockSpec((N_CHIPS, BR_WIDTH, wd), lambda k: (0, n, 0)),
        out_shape=jax.ShapeDtypeStruct((N_CHIPS, n_br * BR_WIDTH, wd), F32),
        input_output_aliases={} if into is None else {2: 0}, compiler_params=_cp("arbitrary"),
    )(br, d_ups, *(() if into is None else (into,)))


def _merge_bwd(d_merged, ups, proj, riders=()):
    T = d_merged.shape[0]
    tm = _row_tile(T, _MERGE_TM)

    def body(dm_ref, up_ref, *rest):
        gates, (dup_ref, dg_ref) = rest[:6], rest[6:]
        for hf in range(2):
            cols = slice(hf * _GATE_W, (hf + 1) * _GATE_W)
            dm = dm_ref[:, cols]
            for n in range(3):
                gate = _sigmoid(gates[2 * n + hf][...].astype(F32))
                dup_ref[n, :, cols] = (dm * gate).astype(dup_ref.dtype)
                dg_ref[:, n * D_MODEL + hf * _GATE_W:n * D_MODEL + (hf + 1) * _GATE_W] = (
                    dm * up_ref[n, :, cols].astype(F32) * gate * (1.0 - gate)).astype(dg_ref.dtype)

    tile = pl.BlockSpec((tm, D_MODEL), lambda i: (i, 0))
    tile3 = pl.BlockSpec((3, tm, D_MODEL), lambda i: (0, i, 0))
    return _carried(*_pcall(
        body, (d_merged, ups, *([proj] * 6)), name="merge_bwd", grid=(T // tm,),
        in_specs=[tile, tile3] + _gate_specs(tm),
        out_specs=(tile3, pl.BlockSpec((tm, 3 * D_MODEL), lambda i: (i, 0))),
        out_shape=(jax.ShapeDtypeStruct((3, T, D_MODEL), BF16), jax.ShapeDtypeStruct((T, 3 * D_MODEL), BF16)),
        semantics=("parallel",), riders=riders), riders)


_CONV_TF = D_FF // 2
_CONV_TS = 256
_HALO = 16


def _conv_fwd(ab, cw, cb, B, S):
    T = B * S
    ts = _row_tile(S, _CONV_TS)
    tf = _CONV_TF
    nb = D_FF // tf
    tps = S // ts
    hb = ts // _HALO

    steps = (T // ts) * nb
    ring = 3

    def body(ab_hbm, p_ref, w_ref, cb_ref, o_ref, a_buf, b_buf, a_sem, b_sem):
        s = pl.program_id(0) * nb + pl.program_id(1)

        def fetch(t):
            rows, slot = pl.ds((t // nb) * ts, ts), t % ring
            return (pltpu.make_async_copy(ab_hbm.at[rows, pl.ds((t % nb) * tf, tf)], a_buf.at[slot], a_sem.at[slot]),
                    pltpu.make_async_copy(ab_hbm.at[rows, pl.ds((t % nb + nb) * tf, tf)], b_buf.at[slot],
                                          b_sem.at[slot]))

        @pl.when(s == 0)
        def _():
            for t in range(ring - 1):
                for cp in fetch(t):
                    cp.start()

        @pl.when(s + ring - 1 < steps)
        def _():
            for cp in fetch(s + ring - 1):
                cp.start()

        for cp in fetch(s):
            cp.wait()
        start = (pl.program_id(0) % tps) == 0
        a = a_buf[s % ring].astype(F32)
        prev = jnp.where(start, 0.0, p_ref[...].astype(F32))
        ext = jnp.concatenate([prev, a], axis=0)
        a1 = pltpu.roll(ext, 1, 0)[_HALO:, :]
        a2 = pltpu.roll(ext, 2, 0)[_HALO:, :]
        ac = cb_ref[...] + w_ref[0] * a2 + w_ref[1] * a1 + w_ref[2] * a
        o_ref[...] = (ac * _sigmoid(ac) * b_buf[s % ring].astype(F32)).astype(o_ref.dtype)

    assert steps >= ring
    return _pallas(
        body, name="conv_fwd", grid=(T // ts, nb),
        in_specs=[HBM_SPEC,
                  pl.BlockSpec((_HALO, tf), lambda i, j: (jnp.maximum(i * hb - 1, 0), j)),
                  pl.BlockSpec((3, 1, tf), lambda i, j: (0, 0, j)),
                  pl.BlockSpec((1, tf), lambda i, j: (0, j))],
        out_specs=pl.BlockSpec((ts, tf), lambda i, j: (i, j)),
        out_shape=jax.ShapeDtypeStruct((T, D_FF), BF16),
        scratch_shapes=[pltpu.VMEM((ring, ts, tf), BF16), pltpu.VMEM((ring, ts, tf), BF16),
                        pltpu.SemaphoreType.DMA((ring,)), pltpu.SemaphoreType.DMA((ring,))],
        compiler_params=_cp("arbitrary", "arbitrary"),
    )(ab, ab, cw, cb)


def _conv_bwd(ab, d_ff, cw, cb, B, S, riders=()):
    T = B * S
    ts = _row_tile(S, _CONV_TS)
    tf = _CONV_TF
    nb = D_FF // tf
    tps = S // ts
    hb = ts // _HALO
    last_h = T // _HALO - 1
    n_ext = ts + _HALO

    def body(a_ref, ap_ref, an_ref, b_ref, bn_ref, d_ref, dn_ref, w_ref, cb_ref, dab_ref, dw_ref, dcb_ref):
        i = pl.program_id(1)

        @pl.when(i == 0)
        def _():
            dw_ref[...] = jnp.zeros_like(dw_ref)
            dcb_ref[...] = jnp.zeros_like(dcb_ref)

        start = (i % tps) == 0
        end = (i % tps) == tps - 1
        a = a_ref[...].astype(F32)
        ext = jnp.concatenate([jnp.where(start, 0.0, ap_ref[...].astype(F32)), a, an_ref[...].astype(F32)], axis=0)
        r1 = pltpu.roll(ext, 1, 0)[_HALO:, :]
        r2 = pltpu.roll(ext, 2, 0)[_HALO:, :]
        ac = cb_ref[...] + w_ref[0] * r2 + w_ref[1] * r1 + w_ref[2] * ext[_HALO:, :]
        sg = _sigmoid(ac)
        d_e = jnp.concatenate([d_ref[...].astype(F32), jnp.where(end, 0.0, dn_ref[...].astype(F32))], axis=0)
        b_e = jnp.concatenate([b_ref[...].astype(F32), bn_ref[...].astype(F32)], axis=0)
        dab_ref[1] = (d_e[:ts, :] * (ac * sg)[:ts, :]).astype(dab_ref.dtype)
        dac = d_e * b_e * sg * (1.0 + ac * (1.0 - sg))
        u1 = pltpu.roll(dac, n_ext - 1, 0)[:ts, :]
        u2 = pltpu.roll(dac, n_ext - 2, 0)[:ts, :]
        dac0 = dac[:ts, :]
        dab_ref[0] = (w_ref[2] * dac0 + w_ref[1] * u1 + w_ref[0] * u2).astype(dab_ref.dtype)
        dcb_ref[...] += jnp.sum(dac0, axis=0, keepdims=True)
        dw_ref[2] += jnp.sum(dac0 * a, axis=0, keepdims=True)
        dw_ref[1] += jnp.sum(dac0 * r1[:ts, :], axis=0, keepdims=True)
        dw_ref[0] += jnp.sum(dac0 * r2[:ts, :], axis=0, keepdims=True)

    def cur(off):
        return pl.BlockSpec((ts, tf), lambda j, i: (i, j + off))

    def nxt(off):
        return pl.BlockSpec((_HALO, tf), lambda j, i: (jnp.minimum((i + 1) * hb, last_h), j + off))

    return _carried(*_pcall(
        body, (ab, ab, ab, ab, ab, d_ff, d_ff, cw, cb), name="conv_bwd", grid=(nb, T // ts),
        in_specs=[cur(0), pl.BlockSpec((_HALO, tf), lambda j, i: (jnp.maximum(i * hb - 1, 0), j)), nxt(0),
                  cur(nb), nxt(nb), cur(0), nxt(0),
                  pl.BlockSpec((3, 1, tf), lambda j, i: (0, 0, j)), pl.BlockSpec((1, tf), lambda j, i: (0, j))],
        out_specs=(pl.BlockSpec((2, ts, tf), lambda j, i: (0, i, j)), pl.BlockSpec((3, 1, tf), lambda j, i: (0, 0, j)),
                   pl.BlockSpec((1, tf), lambda j, i: (0, j))),
        out_shape=(jax.ShapeDtypeStruct((2, T, D_FF), BF16),
                   jax.ShapeDtypeStruct((3, 1, D_FF), F32), jax.ShapeDtypeStruct((1, D_FF), F32)),
        semantics=("parallel", "arbitrary"), riders=riders), riders)


def _local_step(x, mem, tgt, p, comm, B, S):
    g = {}
    h, slabs = comm.carry(
        "norm1", lambda r: _norm1_and_casts(x, p["norm1_g"], p["cast_beside_norm1"], comm.place, riders=r))
    comm.slabs.update(slabs)
    proj = comm.carry("in_proj", lambda r: _mm_cs("in_proj", h, comm.w("w_in"), BF16, riders=r))
    a_out = _gmlp_fwd(proj, p["ln_v_g"], p["ln_v_b"], p["w_spatial"], p["b_spatial"])
    o_h, b_out, states = comm.carry(
        "hgrn_fwd", lambda r: _hgrn_fwd(proj, p["lb_logits"], p["hgrn_norm_g"], B, S, riders=r))
    memn = _rms_fwd("mem_norm", mem, p["mem_norm_g"])
    kv = _mm_rs("mem_kv", memn, comm.w("w_mem_kv"), F32)
    c_out = _attn_fwd(proj, kv, B, S)
    merged, ups = comm.carry(
        "merge_fwd", lambda r: _merge_fwd(a_out, b_out, c_out, comm.w("w_branch"), proj, riders=r))
    x1, h2 = _proj_res_norm("out_proj_norm2", merged, comm.w("w_out"), x, p["norm2_g"])
    ab = comm.carry("up_proj", lambda r: _mm_cs("up_proj", h2, comm.w("w_up"), BF16, riders=r))
    conv_w = comm.w("conv_w")
    ff = _conv_fwd(ab, conv_w, p["conv_b"], B, S)
    dx2, g["final_g"], loss = _proj_res_loss("down_proj_loss", ff, comm.w("w_down"), x1, tgt, p["final_g"])

    comm.grad("w_down", _mm_tn_rs("g_w_down", ff, dx2, to=D_FF // 2))
    d_ff = comm.carry("d_ff", lambda r: _mm_nt_rs("d_ff", dx2, comm.w("w_down"), BF16, riders=r))
    d_ab, g["conv_w"], g["conv_b"] = comm.carry(
        "conv_bwd", lambda r: _conv_bwd(ab, d_ff, conv_w, p["conv_b"], B, S, riders=r))
    comm.grad("w_up", _mm_tn_cs("g_w_up", h2, d_ab, N_CHIPS, to=512, stacked=True))
    d_x1, g["norm2_g"] = comm.carry("d_h2", lambda r: _mm_nt_cs(
        "d_h2_norm2_bwd", d_ab, comm.w("w_up"), F32, riders=r, stacked=True, norm_bwd=(x1, p["norm2_g"], dx2)))
    comm.grad("w_out", _mm_tn_rs("g_w_out", merged, d_x1, to=512))
    d_merged = _mm_nt_rs("d_merged", d_x1, comm.w("w_out"), F32)
    d_ups, d_gates = comm.carry("merge_bwd", lambda r: _merge_bwd(d_merged, ups, proj, riders=r))

    d_br = comm.carry("d_branch", lambda r: _branch_bwd_act(d_ups, comm.w("w_branch"), riders=r))
    g_branch = None
    for n, br in enumerate((a_out, b_out, c_out)):
        g_branch = _branch_bwd_weight("g_w_branch%d" % n, br, d_ups, n, into=g_branch)
    comm.grad("w_branch", g_branch)

    d_gm, g["w_spatial"], g["b_spatial"], g["ln_v_g"], g["ln_v_b"] = comm.carry(
        "gmlp_bwd", lambda r: _gmlp_bwd(proj, d_br, p["ln_v_g"], p["ln_v_b"], p["w_spatial"], p["b_spatial"],
                                        riders=r))
    d_xq, d_kv = _attn_bwd(proj, kv, d_br, B, S)
    comm.grad("w_mem_kv", _mm_tn_rs("g_w_mem_kv", memn, d_kv, to=512))
    d_memn = _mm_nt_rs("d_memn", d_kv, comm.w("w_mem_kv"), F32)
    _, g["mem_norm_g"] = _rms_bwd("mem_norm_bwd", mem, p["mem_norm_g"], d_memn, None)
    d_proj, g["lb_logits"], g["hgrn_norm_g"] = comm.carry(
        "hgrn_bwd", lambda r: _hgrn_bwd(proj, o_h, states, d_br, p["lb_logits"], p["hgrn_norm_g"],
                                        (d_gm, d_xq, d_gates), B, S, riders=r))
    comm.small_grads([g[n].reshape(_SMALL_SHAPE[n]) for n in _SMALL_EARLY] + [loss])
    comm.grad("w_in", *comm.carry("g_w_in", lambda r: _mm_tn_cs_to_sibling(
        "g_w_in", h, d_proj, N_CHIPS, comm.place, riders=r, send=comm.sends)))
    grad_x, g["norm1_g"] = comm.carry("d_h", lambda r: _mm_nt_cs(
        "d_h_norm1_bwd", d_proj, comm.w("w_in"), F32, riders=r, norm_bwd=(x, p["norm1_g"], d_x1)))
    return loss, grad_x, g


HBM_SPEC = pl.BlockSpec(memory_space=pltpu.HBM)


def _place():
    x, y, c = lax.axis_index("x"), lax.axis_index("y"), lax.axis_index("c")
    other_chips = [(1 - x, y), (x, 1 - y), (1 - x, 1 - y)]
    return x, y, c, other_chips


def _remote(src, dst, send_sem, recv_sem, dev):
    return pltpu.make_async_remote_copy(src_ref=src, dst_ref=dst, send_sem=send_sem, recv_sem=recv_sem,
                                        device_id=dev, device_id_type=MESH_ID)


class _Exchange:
    def __init__(self, operands, out_shape, aliases, scratch, start, finish, mid=None, mid_at=0.5):
        self.operands, self.out_shape, self.aliases, self.scratch = operands, out_shape, aliases, scratch
        self.start, self.finish, self.mid, self.mid_at = start, finish, mid, mid_at


def _run_exchanges(name, exs):
    n_in = [len(ex.operands) for ex in exs]
    n_out = [len(ex.out_shape) for ex in exs]
    n_scr = [len(ex.scratch) for ex in exs]

    def body(*refs):
        ins, outs, scr = refs[:sum(n_in)], refs[sum(n_in):sum(n_in) + sum(n_out)], refs[sum(n_in) + sum(n_out):]
        parts, oi, oo, os_ = [], 0, 0, 0
        for k in range(len(exs)):
            parts.append((ins[oi:oi + n_in[k]], outs[oo:oo + n_out[k]], scr[os_:os_ + n_scr[k]]))
            oi, oo, os_ = oi + n_in[k], oo + n_out[k], os_ + n_scr[k]
        for ex, part in zip(exs, parts):
            ex.start(*part)
        for ex, part in zip(exs, parts):
            if ex.mid is not None:
                ex.mid(*part)
        for ex, part in zip(exs, parts):
            ex.finish(*part)

    aliases, ops, shapes, scratch, oi, oo = {}, [], [], [], 0, 0
    for k, ex in enumerate(exs):
        aliases.update({oi + a: oo + b for a, b in ex.aliases.items()})
        oi, oo = oi + n_in[k], oo + n_out[k]
        ops += list(ex.operands)
        shapes += [pltpu.HBM(s.shape, s.dtype) for s in ex.out_shape]
        scratch += list(ex.scratch)
    res = _pallas(
        body, name=name, in_specs=[HBM_SPEC] * len(ops), out_specs=(HBM_SPEC,) * len(shapes), out_shape=tuple(shapes),
        input_output_aliases=aliases, scratch_shapes=scratch,
    )(*ops)
    out, oo = [], 0
    for k in range(len(exs)):
        out.append(list(res[oo:oo + n_out[k]]))
        oo += n_out[k]
    return out


def _ex_all_gather(slabs, halved, part=(0, 1)):
    n = len(slabs)

    def rows(a, cc):
        if not halved[a]:
            return slice(None)
        pr = slabs[a].shape[1] // part[1]
        return pl.ds(part[0] * pr + cc * (pr // 2), pr // 2)

    def ici(bufs, scr, a, j, chip, c, mine):
        px, py = chip
        x, y, _, _ = _place()
        qs = 2 * x + y if mine else 2 * px + py
        piece = bufs[a].at[qs, rows(a, c)]
        return _remote(piece, piece, scr[0].at[3 * a + j], scr[1].at[3 * a + j], (px, py, c))

    def d2d(bufs, scr, a, j, chip, cc):
        px, py = chip
        x, y, c, _ = _place()
        piece = bufs[a].at[2 * px + py, rows(a, cc)]
        return _remote(piece, piece, scr[2].at[3 * a + j], scr[3].at[3 * a + j], (x, y, 1 - c))

    def start(ins, outs, scr):
        _, _, c, chips = _place()
        for j, chip in enumerate(chips):
            for a in range(n):
                ici(outs, scr, a, j, chip, c, True).start()

    def finish(ins, outs, scr):
        _, _, c, chips = _place()
        for j, chip in enumerate(chips):
            for a in range(n):
                ici(outs, scr, a, j, chip, c, False).wait_recv()
                if halved[a]:
                    d2d(outs, scr, a, j, chip, c).start()
        for j, chip in enumerate(chips):
            for a in range(n):
                if halved[a]:
                    d2d(outs, scr, a, j, chip, 1 - c).wait_recv()
        for j, chip in enumerate(chips):
            for a in range(n):
                ici(outs, scr, a, j, chip, c, True).wait_send()
                if halved[a]:
                    d2d(outs, scr, a, j, chip, c).wait_send()

    return _Exchange(list(slabs), [jax.ShapeDtypeStruct(s.shape, s.dtype) for s in slabs],
                     {a: a for a in range(n)}, [pltpu.SemaphoreType.DMA((3 * n,))] * 4, start, finish)


def _ex_gather_relay(slabs, mid_at=0.5):
    n = len(slabs)

    def rows(a, cc):
        hr = slabs[a].shape[1] // 2
        return pl.ds(cc * hr, hr)

    def peers():
        x, y, c, _ = _place()
        nbr0 = ((x + c) % 2, (y + 1 - c) % 2)
        nbr1 = ((x + 1 - c) % 2, (y + c) % 2)
        return x, y, c, nbr0, nbr1, (1 - x, 1 - y)

    def ici(bufs, scr, a, k, chip, dev, cc):
        _, _, c, _, _, _ = peers()
        piece = bufs[a].at[2 * chip[0] + chip[1], rows(a, cc)]
        return _remote(piece, piece, scr[0].at[3 * a + k], scr[1].at[3 * a + k], (dev[0], dev[1], c))

    def d2d(bufs, scr, a, k, chip, cc):
        x, y, c, _, _, _ = peers()
        piece = bufs[a].at[2 * chip[0] + chip[1], rows(a, cc)]
        return _remote(piece, piece, scr[2].at[3 * a + k], scr[3].at[3 * a + k], (x, y, 1 - c))

    def start(ins, outs, scr):
        x, y, c, nbr0, nbr1, _ = peers()
        for a in range(n):
            ici(outs, scr, a, 0, (x, y), nbr0, c).start()
            ici(outs, scr, a, 1, (x, y), nbr1, c).start()

    def mid(ins, outs, scr):
        x, y, c, nbr0, nbr1, diag = peers()
        for a in range(n):
            ici(outs, scr, a, 0, nbr0, nbr0, c).wait_recv()
            ici(outs, scr, a, 2, nbr0, nbr1, c).start()
            d2d(outs, scr, a, 0, nbr0, c).start()
        for a in range(n):
            ici(outs, scr, a, 1, nbr1, nbr1, c).wait_recv()
            d2d(outs, scr, a, 1, nbr1, c).start()

    def finish(ins, outs, scr):
        x, y, c, nbr0, nbr1, diag = peers()
        for a in range(n):
            ici(outs, scr, a, 2, diag, nbr1, c).wait_recv()
            d2d(outs, scr, a, 2, diag, c).start()
        for a in range(n):
            d2d(outs, scr, a, 0, nbr1, 1 - c).wait_recv()
            d2d(outs, scr, a, 1, nbr0, 1 - c).wait_recv()
            d2d(outs, scr, a, 2, diag, 1 - c).wait_recv()
        for a in range(n):
            ici(outs, scr, a, 0, (x, y), nbr0, c).wait_send()
            ici(outs, scr, a, 1, (x, y), nbr1, c).wait_send()
            ici(outs, scr, a, 2, nbr0, nbr1, c).wait_send()
            d2d(outs, scr, a, 0, nbr0, c).wait_send()
            d2d(outs, scr, a, 1, nbr1, c).wait_send()
            d2d(outs, scr, a, 2, diag, c).wait_send()

    return _Exchange(list(slabs), [jax.ShapeDtypeStruct(s.shape, s.dtype) for s in slabs],
                     {a: a for a in range(n)}, [pltpu.SemaphoreType.DMA((3 * n,))] * 4, start, finish, mid, mid_at)


def _ex_to_sibling(grads):
    n = len(grads)

    def copy(ins, outs, scr, a):
        x, y, c, _ = _place()
        hr = grads[a].shape[1] // 2
        return _remote(ins[a].at[:, pl.ds((1 - c) * hr, hr), :], outs[a], scr[0].at[a], scr[1].at[a], (x, y, 1 - c))

    def start(ins, outs, scr):
        for a in range(n):
            copy(ins, outs, scr, a).start()

    def finish(ins, outs, scr):
        for a in range(n):
            copy(ins, outs, scr, a).wait()

    out_shape = [jax.ShapeDtypeStruct((g.shape[0], g.shape[1] // 2, g.shape[2]), g.dtype) for g in grads]
    return _Exchange(list(grads), out_shape, {}, [pltpu.SemaphoreType.DMA((n,))] * 2, start, finish)


def _ex_to_owner(parts, part=(0, 1), landing=None):
    n = len(parts)

    def copy(ins, outs, scr, a, j, chip):
        _, _, c, _ = _place()
        px, py = chip
        pr = parts[a].shape[1] // part[1]
        rows = pl.ds(part[0] * pr, pr)
        return _remote(ins[a].at[2 * px + py, rows], outs[a].at[j, rows], scr[0].at[3 * a + j],
                       scr[1].at[3 * a + j], (px, py, c))

    def start(ins, outs, scr):
        for j, chip in enumerate(_place()[3]):
            for a in range(n):
                copy(ins, outs, scr, a, j, chip).start()

    def finish(ins, outs, scr):
        for j, chip in enumerate(_place()[3]):
            for a in range(n):
                copy(ins, outs, scr, a, j, chip).wait()

    out_shape = [jax.ShapeDtypeStruct((3,) + p.shape[1:], p.dtype) for p in parts]
    operands, aliases = list(parts), {}
    if landing is not None:
        operands, aliases = operands + list(landing), {n + a: a for a in range(n)}
    return _Exchange(operands, out_shape, aliases, [pltpu.SemaphoreType.DMA((3 * n,))] * 2, start, finish)


def _ex_share_halves(bufs):
    n = len(bufs)

    def copy(outs, scr, a, cc):
        x, y, c, _ = _place()
        hr = bufs[a].shape[0] // 2
        piece = outs[a].at[pl.ds(cc * hr, hr), :]
        return _remote(piece, piece, scr[0].at[a], scr[1].at[a], (x, y, 1 - c))

    def start(ins, outs, scr):
        c = _place()[2]
        for a in range(n):
            copy(outs, scr, a, c).start()

    def finish(ins, outs, scr):
        c = _place()[2]
        for a in range(n):
            copy(outs, scr, a, c).wait_send()
            copy(outs, scr, a, 1 - c).wait_recv()

    return _Exchange(list(bufs), [jax.ShapeDtypeStruct(b.shape, b.dtype) for b in bufs], {a: a for a in range(n)},
                     [pltpu.SemaphoreType.DMA((n,))] * 2, start, finish)


def _ex_gather_small(arrs):
    n = len(arrs)

    def peer_of(m):
        x, y, c, _ = _place()
        return (1 - x if m & 4 else x, 1 - y if m & 2 else y, 1 - c if m & 1 else c)

    def own(ins, outs, scr, a):
        x, y, c, _ = _place()
        return pltpu.make_async_copy(ins[a], outs[a].at[4 * x + 2 * y + c], scr[2].at[a])

    def start(ins, outs, scr):
        x, y, c, _ = _place()
        for a in range(n):
            own(ins, outs, scr, a).start()
        for m in range(1, N_DEV):
            for a in range(n):
                k = (N_DEV - 1) * a + m - 1
                _remote(ins[a], outs[a].at[4 * x + 2 * y + c], scr[0].at[k], scr[1].at[k], peer_of(m)).start()

    def finish(ins, outs, scr):
        for a in range(n):
            own(ins, outs, scr, a).wait()
        for m in range(1, N_DEV):
            px, py, pc = peer_of(m)
            for a in range(n):
                k = (N_DEV - 1) * a + m - 1
                slot = outs[a].at[4 * px + 2 * py + pc]
                cp = _remote(ins[a], slot, scr[0].at[k], scr[1].at[k], (px, py, pc))
                cp.wait_send()
                cp.wait_recv()

    out_shape = [jax.ShapeDtypeStruct((N_DEV,) + a.shape, a.dtype) for a in arrs]
    return _Exchange(list(arrs), out_shape, {},
                     [pltpu.SemaphoreType.DMA(((N_DEV - 1) * n,))] * 2 + [pltpu.SemaphoreType.DMA((n,))], start, finish)


def _div_tile(n, want):
    best = None
    for t in range(8, min(n, want) + 1, 8):
        if n % t == 0:
            best = t
    assert best is not None, n
    return best


def _cast_into_slab(name, w, place, dtype):
    r, cc = w.shape
    tr = r if r * cc <= 128 * 1024 else _div_tile(r, 256)

    def body(s_ref, w_ref, o_ref):
        o_ref[...] = w_ref[...].astype(o_ref.dtype)

    return _pallas(
        body, name=name,
        grid_spec=pltpu.PrefetchScalarGridSpec(
            num_scalar_prefetch=1, grid=(r // tr,),
            in_specs=[pl.BlockSpec((tr, cc), lambda i, s: (i, 0))],
            out_specs=pl.BlockSpec((None, tr, cc), lambda i, s: (s[0], i, 0))),
        out_shape=jax.ShapeDtypeStruct((N_CHIPS, r, cc), dtype), compiler_params=_cp("parallel"),
    )(place, w)


def _add_half(name, g, rcv, place):
    nq, r, cc = g.shape
    hr = r // 2

    def body(s_ref, g_ref, r_ref, o_ref):
        o_ref[...] = (g_ref[...] + r_ref[...]).astype(o_ref.dtype)

    spec = pl.BlockSpec((None, hr, cc), lambda i, s: (i, 0, 0))
    return _pallas(
        body, name=name,
        grid_spec=pltpu.PrefetchScalarGridSpec(
            num_scalar_prefetch=1, grid=(nq,),
            in_specs=[pl.BlockSpec((None, hr, cc), lambda i, s: (i, s[1], 0)), spec], out_specs=spec),
        out_shape=jax.ShapeDtypeStruct((nq, hr, cc), BF16), compiler_params=_cp("parallel"),
    )(place, g, rcv)


def _sum_owner(name, part, rcv, place):
    _, hr, cc = part.shape
    tr = _div_tile(hr, 128)
    nb = hr // tr

    def body(s_ref, p_ref, r_ref, o_ref):
        o_ref[...] = ((p_ref[...].astype(F32) + r_ref[0].astype(F32)) + r_ref[1].astype(F32)) + r_ref[2].astype(F32)

    return _pallas(
        body, name=name,
        grid_spec=pltpu.PrefetchScalarGridSpec(
            num_scalar_prefetch=1, grid=(nb,),
            in_specs=[pl.BlockSpec((None, tr, cc), lambda i, s: (s[0], i, 0)),
                      pl.BlockSpec((3, tr, cc), lambda i, s: (0, i, 0))],
            out_specs=pl.BlockSpec((tr, cc), lambda i, s: (s[1] * nb + i, 0))),
        out_shape=jax.ShapeDtypeStruct((2 * hr, cc), F32), compiler_params=_cp("parallel"),
    )(place, part, rcv)


def _sum_small(gathered, local, place):
    n = len(gathered)

    def body(s_ref, *refs):
        g_refs, l_refs, o_refs = refs[:n], refs[n:2 * n], refs[2 * n:]
        me = s_ref[2]
        for g_ref, l_ref, o_ref in zip(g_refs, l_refs, o_refs):
            acc = None
            for d in range(N_DEV):
                term = jnp.where(me == d, l_ref[...], g_ref[d])
                acc = term if acc is None else acc + term
            o_ref[...] = acc

    def whole(shape):
        return pl.BlockSpec(shape, lambda i, s, nd=len(shape): (0,) * nd)

    return _pallas(
        body, name="sum_small",
        grid_spec=pltpu.PrefetchScalarGridSpec(
            num_scalar_prefetch=1, grid=(1,),
            in_specs=[whole(g.shape) for g in gathered] + [whole(a.shape) for a in local],
            out_specs=tuple(whole(a.shape) for a in local)),
        out_shape=tuple(jax.ShapeDtypeStruct(a.shape, a.dtype) for a in local), compiler_params=_cp("arbitrary"),
    )(place, *gathered, *local)


def _adamw(name, w, g, m, v):
    r, cc = w.shape
    tr = r if r * cc <= 128 * 1024 else _div_tile(r, 256)

    def body(w_ref, g_ref, m_ref, v_ref, d_ref, mo_ref, vo_ref, go_ref):
        gv = g_ref[...]
        go_ref[...] = gv
        mn = ADAM_B1 * m_ref[...] + (1.0 - ADAM_B1) * gv
        vn = ADAM_B2 * v_ref[...] + (1.0 - ADAM_B2) * (gv * gv)
        m_hat = mn / (1.0 - ADAM_B1 ** ADAM_STEP)
        v_hat = vn / (1.0 - ADAM_B2 ** ADAM_STEP)
        d_ref[...] = -ADAM_LR * (m_hat / (jnp.sqrt(v_hat) + ADAM_EPS) + ADAM_WD * w_ref[...])
        mo_ref[...] = mn
        vo_ref[...] = vn

    spec = pl.BlockSpec((tr, cc), lambda i: (i, 0))
    sd = jax.ShapeDtypeStruct((r, cc), F32)
    return _pallas(
        body, name=name, grid=(r // tr,), in_specs=[spec] * 4, out_specs=(spec,) * 4, out_shape=(sd,) * 4,
        compiler_params=_cp("parallel"),
    )(w, g, m, v)


_BIG = ("w_in", "w_up", "w_branch", "w_mem_kv", "w_out", "w_down")
_BIG_SHARD_SHAPE = {"w_in": (1024, 1664), "w_up": (1024, 1408), "w_branch": (1536, 256),
                    "w_mem_kv": (256, 1024), "w_out": (256, 1024), "w_down": (704, 1024)}
_SMALL_SHAPE = {"norm1_g": (1, D_MODEL), "ln_v_g": (1, GM_WIDTH), "ln_v_b": (1, GM_WIDTH),
                "w_spatial": (GM_GROUPS * GM_CHUNK, GM_CHUNK), "b_spatial": (GM_GROUPS, GM_CHUNK),
                "lb_logits": (2, HG_HEADS * HG_DIM), "hgrn_norm_g": (1, HG_DIM), "mem_norm_g": (1, D_MODEL),
                "norm2_g": (1, D_MODEL), "conv_w": (3, D_FF), "conv_b": (1, D_FF), "final_g": (1, D_MODEL)}
_SMALL_EARLY = tuple(n for n in _SMALL_SHAPE if n != "norm1_g")
_PARAM_ORDER = ("norm1_g", "w_in", "ln_v_g", "ln_v_b", "w_spatial", "b_spatial", "lb_logits", "hgrn_norm_g",
                "mem_norm_g", "w_mem_kv", "w_branch", "w_out", "norm2_g", "w_up", "conv_w", "conv_b", "w_down",
                "final_g")


def _adamw_small(ws, gs, ms, vs):
    n = len(ws)

    def body(*refs):
        w_refs, g_refs, m_refs, v_refs = refs[:n], refs[n:2 * n], refs[2 * n:3 * n], refs[3 * n:4 * n]
        d_refs, mo_refs, vo_refs = refs[4 * n:5 * n], refs[5 * n:6 * n], refs[6 * n:]
        for k in range(n):
            gv = g_refs[k][...]
            mn = ADAM_B1 * m_refs[k][...] + (1.0 - ADAM_B1) * gv
            vn = ADAM_B2 * v_refs[k][...] + (1.0 - ADAM_B2) * (gv * gv)
            m_hat = mn / (1.0 - ADAM_B1 ** ADAM_STEP)
            v_hat = vn / (1.0 - ADAM_B2 ** ADAM_STEP)
            d_refs[k][...] = -ADAM_LR * (m_hat / (jnp.sqrt(v_hat) + ADAM_EPS) + ADAM_WD * w_refs[k][...])
            mo_refs[k][...] = mn
            vo_refs[k][...] = vn

    specs = [pl.BlockSpec(a.shape, lambda i, nd=a.ndim: (0,) * nd) for a in ws]
    shapes = tuple(jax.ShapeDtypeStruct(a.shape, F32) for a in ws)
    res = _pallas(
        body, name="adamw_small", grid=(1,), in_specs=specs * 4, out_specs=tuple(specs * 3), out_shape=shapes * 3,
        compiler_params=_cp("arbitrary"),
    )(*ws, *gs, *ms, *vs)
    return res[:n], res[n:2 * n], res[2 * n:]


class _Comm:
    _ROW_SHARDED = ("w_mem_kv", "w_out", "w_down")

    def __init__(self, slabs, place):
        self.slabs, self.place = slabs, place
        self.full, self.raw, self.parts, self.landing, self.bufs, self.done = {}, {}, {}, {}, {}, {}

    def w(self, name):
        a = self.full[name]
        if name in self._ROW_SHARDED:
            return a.reshape(-1, a.shape[-1])
        if name == "conv_w":
            return jnp.transpose(a, (1, 0, 2)).reshape(3, 1, D_FF)
        return a

    sends = True

    def grad(self, name, arr, from_sibling=None):
        self.raw[name] = arr.reshape((N_CHIPS, -1, arr.shape[-1]))
        if from_sibling is not None:
            self.parts[name] = _add_half("rs_add_" + name, self.raw[name], from_sibling, self.place)

    def small_grads(self, arrays):
        self.small_local = list(arrays)

    def carry(self, tag, call):
        plan = self._plan(tag)
        if not plan:
            return call(())
        out, carried = call([ex for ex, _ in plan])
        for (_, deliver), res in zip(plan, carried):
            deliver(res)
        return out

    def finish(self, last_small):
        ex, deliver = self._share(["w_out", "w_branch", "w_mem_kv", "w_in"])
        shared, small = _run_exchanges("share_and_gather_last", [ex, _ex_gather_small(last_small)])
        deliver(shared)
        return self.done, self.small_local + list(last_small), self.small_everyone + small

    def _plan(self, tag):
        if tag == "norm1":
            def deliver(res):
                self.full["w_in"] = res[0]

            return [(_ex_gather_relay([self.slabs["w_in"]]), deliver)]
        if tag == "in_proj":
            return [self._gather_relay(["w_branch", "w_out", "w_mem_kv", "w_down"], 0.6), self._gather(["conv_w"])]
        if tag == "hgrn_fwd":
            return [self._gather_relay(["w_up"], 0.8)]
        if tag == "d_h2":
            return [self._to_sibling(["w_down", "w_up"])]
        if tag == "merge_bwd":
            return [self._to_owner(["w_up"], (0, 2))]
        if tag == "hgrn_bwd":
            return [self._to_owner(["w_down"]), self._to_owner(["w_up"], (1, 2)),
                    self._to_sibling(["w_out", "w_branch", "w_mem_kv"])]
        if tag == "g_w_in":
            def keep(res):
                self.small_everyone = res

            return [self._to_owner(["w_out", "w_branch", "w_mem_kv"]), (_ex_gather_small(self.small_local), keep)]
        if tag == "d_h":
            return [self._to_owner(["w_in"]), self._share(["w_down", "w_up"])]
        return []

    def _gather(self, names, part=(0, 1)):
        def deliver(res):
            self.slabs.update(zip(names, res))
            self.full.update(zip(names, res))

        return _ex_all_gather([self.slabs[n] for n in names], [n != "conv_w" for n in names], part), deliver

    def _gather_relay(self, names, mid_at):
        return _ex_gather_relay([self.slabs[n] for n in names], mid_at), lambda res: self.full.update(zip(names, res))

    def _to_sibling(self, names):
        def deliver(res):
            for n, r in zip(names, res):
                self.parts[n] = _add_half("rs_add_" + n, self.raw[n], r, self.place)

        return _ex_to_sibling([self.raw[n] for n in names]), deliver

    def _to_owner(self, names, part=(0, 1)):
        def deliver(res):
            for n, r in zip(names, res):
                if part[0] + 1 < part[1]:
                    self.landing[n] = r
                else:
                    self.bufs[n] = _sum_owner("rs_sum_" + n, self.parts[n], r, self.place)

        landing = [self.landing[n] for n in names] if part[0] else None
        return _ex_to_owner([self.parts[n] for n in names], part, landing), deliver

    def _share(self, names):
        return _ex_share_halves([self.bufs[n] for n in names]), lambda res: self.done.update(zip(names, res))


def kernel(x, mem, norm1_g, w_in, ln_v_g, ln_v_b, w_spatial, b_spatial, lb_logits, hgrn_norm_g, mem_norm_g, w_mem_kv, w_branch, w_out, norm2_g, w_up, conv_w, conv_b, w_down, final_g, loss_target, m_norm1_g, m_w_in, m_ln_v_g, m_ln_v_b, m_w_spatial, m_b_spatial, m_lb_logits, m_hgrn_norm_g, m_mem_norm_g, m_w_mem_kv, m_w_branch, m_w_out, m_norm2_g, m_w_up, m_conv_w, m_conv_b, m_w_down, m_final_g, v_norm1_g, v_w_in, v_ln_v_g, v_ln_v_b, v_w_spatial, v_b_spatial, v_lb_logits, v_hgrn_norm_g, v_mem_norm_g, v_w_mem_kv, v_w_branch, v_w_out, v_norm2_g, v_w_up, v_conv_w, v_conv_b, v_w_down, v_final_g):
    w = dict(norm1_g=norm1_g, w_in=w_in, ln_v_g=ln_v_g, ln_v_b=ln_v_b, w_spatial=w_spatial, b_spatial=b_spatial,
             lb_logits=lb_logits, hgrn_norm_g=hgrn_norm_g, mem_norm_g=mem_norm_g, w_mem_kv=w_mem_kv,
             w_branch=w_branch, w_out=w_out, norm2_g=norm2_g, w_up=w_up, conv_w=conv_w, conv_b=conv_b,
             w_down=w_down, final_g=final_g)
    mom = dict(norm1_g=m_norm1_g, w_in=m_w_in, ln_v_g=m_ln_v_g, ln_v_b=m_ln_v_b, w_spatial=m_w_spatial,
               b_spatial=m_b_spatial, lb_logits=m_lb_logits, hgrn_norm_g=m_hgrn_norm_g, mem_norm_g=m_mem_norm_g,
               w_mem_kv=m_w_mem_kv, w_branch=m_w_branch, w_out=m_w_out, norm2_g=m_norm2_g, w_up=m_w_up,
               conv_w=m_conv_w, conv_b=m_conv_b, w_down=m_w_down, final_g=m_final_g)
    var = dict(norm1_g=v_norm1_g, w_in=v_w_in, ln_v_g=v_ln_v_g, ln_v_b=v_ln_v_b, w_spatial=v_w_spatial,
               b_spatial=v_b_spatial, lb_logits=v_lb_logits, hgrn_norm_g=v_hgrn_norm_g, mem_norm_g=v_mem_norm_g,
               w_mem_kv=v_w_mem_kv, w_branch=v_w_branch, w_out=v_w_out, norm2_g=v_norm2_g, w_up=v_w_up,
               conv_w=v_conv_w, conv_b=v_conv_b, w_down=v_w_down, final_g=v_final_g)
    B, S, D = x.shape
    T = B * S
    ci = lax.axis_index("c")
    q = 2 * lax.axis_index("x") + lax.axis_index("y")
    place = jnp.stack([q, ci, 2 * q + ci]).astype(jnp.int32)

    shards = {n: w[n].reshape(_BIG_SHARD_SHAPE[n]) for n in _BIG}
    slabs = {"w_in": _cast_into_slab("slab_w_in", shards.pop("w_in"), place, BF16),
             "conv_w": _cast_into_slab("slab_conv_w", conv_w[0], place, F32)}
    comm = _Comm(slabs, place)
    p = dict(
        cast_beside_norm1=shards,
        norm1_g=norm1_g, ln_v_g=ln_v_g, ln_v_b=ln_v_b, w_spatial=w_spatial[0],
        b_spatial=b_spatial.reshape(GM_GROUPS, GM_CHUNK, 1), lb_logits=lb_logits, hgrn_norm_g=hgrn_norm_g,
        mem_norm_g=mem_norm_g, norm2_g=norm2_g, conv_b=conv_b, final_g=final_g.reshape(1, D))

    loss, grad_x, g = _local_step(x.reshape(T, D), mem.reshape(B * MEM_LEN, D), loss_target.reshape(T, D), p, comm,
                                  B, S)

    shard_grads, local_small, everyone = comm.finish([g["norm1_g"]])
    summed = _sum_small(everyone, local_small, place)
    small_names = list(_SMALL_EARLY) + ["norm1_g"]
    total = dict(zip(_SMALL_EARLY, summed))
    loss_total, total["norm1_g"] = summed[len(_SMALL_EARLY)][0, 0], summed[-1]

    grads, delta, new_m, new_v = {}, {}, {}, {}
    for n in _BIG:
        shp = _BIG_SHARD_SHAPE[n]
        delta[n], new_m[n], new_v[n], grads[n] = _adamw("adamw_" + n, w[n].reshape(shp), shard_grads[n],
                                                        mom[n].reshape(shp), var[n].reshape(shp))
    cw_shard = D_FF // N_CHIPS
    total["conv_w"] = lax.dynamic_slice(total["conv_w"], (0, q * cw_shard), (3, cw_shard)).reshape(3, 1, cw_shard)

    def flat2d(d, n):
        return d[n].reshape(total[n].shape)

    upd = _adamw_small([flat2d(w, n) for n in small_names], [total[n] for n in small_names],
                       [flat2d(mom, n) for n in small_names], [flat2d(var, n) for n in small_names])
    for k, n in enumerate(small_names):
        grads[n], delta[n], new_m[n], new_v[n] = total[n], upd[0][k], upd[1][k], upd[2][k]

    def shaped(d):
        return [d[n].reshape(w[n].shape) for n in _PARAM_ORDER]

    return (loss_total, grad_x.reshape(B, S, D), *shaped(grads), *shaped(delta), *shaped(new_m), *shaped(new_v))
```

```python
import functools
import math

import jax
import jax.numpy as jnp
from jax import lax
from jax.experimental import pallas as pl
from jax.experimental.pallas import tpu as pltpu

F32 = jnp.float32
BF16 = jnp.bfloat16
EPS = 1e-6

D_MODEL = 1024
MEM_LEN = 256
GM_WIDTH = 512
GM_CHUNK = 128
GM_GROUPS = 4
HG_HEADS = 4
HG_DIM = 128
HG_CHUNK = 64
XA_HEADS = 4
XA_DIM = 128
BR_WIDTH = 512
D_FF = 2816
IN_WIDTH = 6656
N_CHIPS = 4
N_DEV = 8

ADAM_LR = 0.001
ADAM_B1 = 0.9
ADAM_B2 = 0.999
ADAM_EPS = 1e-08
ADAM_WD = 0.01
ADAM_STEP = 10

COL_ZU, COL_ZV, COL_HQ, COL_HF, COL_HI, COL_HG, COL_XQ = 0, 1, 2, 3, 4, 5, 6
COL_GATE0 = 3584

VMEM_LIMIT_BYTES = 48 * 1024 * 1024
MESH_ID = pl.DeviceIdType.MESH


def _cp(*sem):
    return pltpu.CompilerParams(dimension_semantics=sem, vmem_limit_bytes=VMEM_LIMIT_BYTES)


def _pallas(body, *, out_shape, **kw):
    def pin(s):
        return pltpu.HBM(s.shape, s.dtype) if isinstance(s, jax.ShapeDtypeStruct) else s

    out_shape = tuple(pin(s) for s in out_shape) if isinstance(out_shape, (tuple, list)) else pin(out_shape)
    call = pl.pallas_call(body, out_shape=out_shape, **kw)

    def run(*operands):
        return call(*[pltpu.with_memory_space_constraint(o, pltpu.HBM) if jnp.issubdtype(o.dtype, jnp.floating)
                      else o for o in operands])

    return run


def _dot(a, b):
    return lax.dot_general(a.astype(BF16), b.astype(BF16), (((1,), (0,)), ((), ())), preferred_element_type=F32)


def _dot_nt(a, b):
    return lax.dot_general(a.astype(BF16), b.astype(BF16), (((1,), (1,)), ((), ())), preferred_element_type=F32)


def _dot_tn(a, b):
    return lax.dot_general(a.astype(BF16), b.astype(BF16), (((0,), (0,)), ((), ())), preferred_element_type=F32)


def _dot_01(mask01, x):
    hi = x.astype(BF16)
    r1 = x - hi.astype(F32)
    mid = r1.astype(BF16)
    lo = (r1 - mid.astype(F32)).astype(BF16)
    m = mask01.astype(BF16)
    dn = (((1,), (0,)), ((), ()))
    return (lax.dot_general(m, hi, dn, preferred_element_type=F32)
            + lax.dot_general(m, mid, dn, preferred_element_type=F32)
            + lax.dot_general(m, lo, dn, preferred_element_type=F32))


def _sigmoid(z):
    return 1.0 / (1.0 + jnp.exp(-z))


_GELU_C = math.sqrt(2.0 / math.pi)


def _gelu_and_grad(z):
    inner = _GELU_C * (z + 0.044715 * z * z * z)
    t = jnp.tanh(inner)
    val = 0.5 * z * (1.0 + t)
    grad = 0.5 * (1.0 + t) + 0.5 * z * (1.0 - t * t) * _GELU_C * (1.0 + 3.0 * 0.044715 * z * z)
    return val, grad


def _row_tile(n, want=512):
    t = min(want, n)
    assert n % t == 0
    return t


def _pcall(body, operands, *, name, grid, in_specs, out_specs, out_shape, scratch_shapes=(), semantics, riders=(),
           prefetch=None):
    single = not isinstance(out_shape, (tuple, list))
    out_specs = (out_specs,) if single else tuple(out_specs)
    out_shape = (out_shape,) if single else tuple(out_shape)
    n_pre = 0 if prefetch is None else 1

    def call(fn, ins_, outs_, shapes_, scr_, ops, sem, aliases):
        if prefetch is None:
            return _pallas(fn, name=name, grid=grid, in_specs=ins_, out_specs=outs_, out_shape=shapes_,
                           scratch_shapes=scr_, input_output_aliases=aliases, compiler_params=_cp(*sem))(*ops)
        spec = pltpu.PrefetchScalarGridSpec(num_scalar_prefetch=1, grid=grid, in_specs=ins_, out_specs=outs_,
                                            scratch_shapes=scr_)
        return _pallas(fn, name=name, grid_spec=spec, out_shape=shapes_, input_output_aliases=aliases,
                       compiler_params=_cp(*sem))(prefetch, *ops)

    if not riders:
        res = call(body, list(in_specs), out_specs, out_shape, list(scratch_shapes), operands, semantics, {})
        return (res[0] if single else res), []
    n_in, n_out, n_scr = len(in_specs), len(out_shape), len(scratch_shapes)
    ex_in = [len(ex.operands) for ex in riders]
    ex_out = [len(ex.out_shape) for ex in riders]
    ex_scr = [len(ex.scratch) for ex in riders]
    tot_in, tot_out = n_in + sum(ex_in), n_out + sum(ex_out)

    def wrapped(*refs):
        pre, refs = refs[:n_pre], refs[n_pre:]
        ins, outs, scr = refs[:tot_in], refs[tot_in:tot_in + tot_out], refs[tot_in + tot_out:]
        ids = [pl.program_id(d) for d in range(len(grid))]
        first = functools.reduce(lambda p, t: p & t, [i == 0 for i in ids])
        last = functools.reduce(lambda p, t: p & t, [i == n - 1 for i, n in zip(ids, grid)])
        parts, oi, oo, os_ = [], n_in, n_out, n_scr
        for k in range(len(riders)):
            parts.append((ins[oi:oi + ex_in[k]], outs[oo:oo + ex_out[k]], scr[os_:os_ + ex_scr[k]]))
            oi, oo, os_ = oi + ex_in[k], oo + ex_out[k], os_ + ex_scr[k]

        @pl.when(first)
        def _():
            for ex, part in zip(riders, parts):
                ex.start(*part)

        step, total = 0, 1
        for i, n in zip(ids, grid):
            step, total = step * n + i, total * n
        for ex, part in zip(riders, parts):
            if ex.mid is not None:
                @pl.when(step == min(total - 1, int(total * ex.mid_at)))
                def _(ex=ex, part=part):
                    ex.mid(*part)

        body(*pre, *ins[:n_in], *outs[:n_out], *scr[:n_scr])

        @pl.when(last)
        def _():
            for ex, part in zip(riders, parts):
                ex.finish(*part)

    aliases, oi, oo = {}, n_in, n_out
    all_ops, all_shapes, all_scr = list(operands), list(out_shape), list(scratch_shapes)
    for k, ex in enumerate(riders):
        aliases.update({n_pre + oi + a: oo + b for a, b in ex.aliases.items()})
        oi, oo = oi + ex_in[k], oo + ex_out[k]
        all_ops += list(ex.operands)
        all_shapes += [pltpu.HBM(s.shape, s.dtype) for s in ex.out_shape]
        all_scr += list(ex.scratch)
    res = call(wrapped, list(in_specs) + [HBM_SPEC] * sum(ex_in), out_specs + (HBM_SPEC,) * sum(ex_out),
               tuple(all_shapes), all_scr, all_ops, ["arbitrary"] * len(grid), aliases)
    own = res[0] if single else tuple(res[:n_out])
    carried, oo = [], n_out
    for k in range(len(riders)):
        carried.append(list(res[oo:oo + ex_out[k]]))
        oo += ex_out[k]
    return own, carried


def _carried(out, carried, riders):
    return (out, carried) if riders else out


def _matmul(name, operands, *, grid, in_specs, o_spec, out_shape, out_dtype, dims, riders=()):
    nk = grid[2]
    assert nk == 1 or out_dtype == F32

    def body(a_ref, b_ref, o_ref):
        part = lax.dot_general(a_ref[...].astype(BF16), b_ref[...].astype(BF16), (dims, ((), ())),
                               preferred_element_type=F32)
        if nk == 1:
            o_ref[...] = part.astype(o_ref.dtype)
        else:
            k = pl.program_id(2)

            @pl.when(k == 0)
            def _():
                o_ref[...] = part

            @pl.when(k > 0)
            def _():
                o_ref[...] += part

    out, carried = _pcall(body, operands, name=name, grid=grid, in_specs=in_specs, out_specs=o_spec,
                          out_shape=jax.ShapeDtypeStruct(out_shape, out_dtype),
                          semantics=("parallel", "parallel", "arbitrary"), riders=riders)
    return (out, carried) if riders else out


NN = ((1,), (0,))
NT = ((1,), (1,))
TN = ((0,), (0,))
_TN_TOKENS = 4096


def _mm_cs(name, a, w, out_dtype, riders=()):
    M, K = a.shape
    nq, _, wd = w.shape
    tm = _row_tile(M)
    nt = M // tm
    ring = 3
    assert nq * nt >= ring

    def body(a_hbm, w_ref, o_ref, a_buf, a_sem):
        s = pl.program_id(0) * nt + pl.program_id(1)

        def fetch(t):
            return pltpu.make_async_copy(a_hbm.at[pl.ds((t % nt) * tm, tm), :], a_buf.at[t % ring], a_sem.at[t % ring])

        @pl.when(s == 0)
        def _():
            for t in range(ring - 1):
                fetch(t).start()

        @pl.when(s + ring - 1 < nq * nt)
        def _():
            fetch(s + ring - 1).start()

        fetch(s).wait()
        o_ref[...] = _dot(a_buf[s % ring], w_ref[...]).astype(o_ref.dtype)

    return _carried(*_pcall(
        body, (a, w), name=name, grid=(nq, nt),
        in_specs=[HBM_SPEC, pl.BlockSpec((None, K, wd), lambda j, i: (j, 0, 0))],
        out_specs=pl.BlockSpec((tm, wd), lambda j, i: (i, j)),
        out_shape=jax.ShapeDtypeStruct((M, nq * wd), out_dtype),
        scratch_shapes=[pltpu.VMEM((ring, tm, K), a.dtype), pltpu.SemaphoreType.DMA((ring,))],
        semantics=("arbitrary", "arbitrary"), riders=riders), riders)


def _mm_rs(name, a, w, out_dtype):
    M, K = a.shape
    N = w.shape[1]
    tm = _row_tile(M)
    return _matmul(name, (a, w), grid=(M // tm, 1, 1),
                   in_specs=[pl.BlockSpec((tm, K), lambda i, j, k: (i, 0)), pl.BlockSpec((K, N), lambda i, j, k: (0, 0))],
                   o_spec=pl.BlockSpec((tm, N), lambda i, j, k: (i, 0)),
                   out_shape=(M, N), out_dtype=out_dtype, dims=NN)


def _mm_nt_rs(name, g, w, out_dtype, riders=()):
    M, N = g.shape
    K = w.shape[0]
    to = K
    tm = _row_tile(M)
    nt = M // tm
    ring = 3
    if nt >= ring and not riders:
        def body(g_hbm, w_ref, o_ref, g_buf, g_sem):
            s = pl.program_id(0)

            def fetch(t):
                return pltpu.make_async_copy(g_hbm.at[pl.ds(t * tm, tm), :], g_buf.at[t % ring], g_sem.at[t % ring])

            @pl.when(s == 0)
            def _():
                for t in range(ring - 1):
                    fetch(t).start()

            @pl.when(s + ring - 1 < nt)
            def _():
                fetch(s + ring - 1).start()

            fetch(s).wait()
            o_ref[...] = _dot_nt(g_buf[s % ring], w_ref[...]).astype(o_ref.dtype)

        return _pcall(
            body, (g, w), name=name, grid=(nt,),
            in_specs=[HBM_SPEC, pl.BlockSpec((K, N), lambda i: (0, 0))],
            out_specs=pl.BlockSpec((tm, K), lambda i: (i, 0)), out_shape=jax.ShapeDtypeStruct((M, K), out_dtype),
            scratch_shapes=[pltpu.VMEM((ring, tm, N), g.dtype), pltpu.SemaphoreType.DMA((ring,))],
            semantics=("arbitrary",))[0]
    return _matmul(name, (g, w), grid=(M // tm, K // to, 1),
                   in_specs=[pl.BlockSpec((tm, N), lambda i, j, k: (i, 0)),
                             pl.BlockSpec((to, N), lambda i, j, k: (j, 0))],
                   o_spec=pl.BlockSpec((tm, to), lambda i, j, k: (i, j)),
                   out_shape=(M, K), out_dtype=out_dtype, dims=NT, riders=riders)


def _mm_nt_cs(name, g, w, out_dtype, riders=(), stacked=False, norm_bwd=None):
    M = g.shape[-2]
    nq, K, wd = w.shape
    tm = _row_tile(M, 256)

    def product(g_ref, w_ref):
        acc = None
        for q in range(nq):
            gq = g_ref[q // 2, :, (q % 2) * wd:(q % 2 + 1) * wd] if stacked else g_ref[:, q * wd:(q + 1) * wd]
            part = _dot_nt(gq, w_ref[q])
            acc = part if acc is None else acc + part
        return acc

    def body(g_ref, w_ref, o_ref):
        o_ref[...] = product(g_ref, w_ref).astype(o_ref.dtype)

    def body_norm(g_ref, w_ref, x_ref, gain_ref, dr_ref, dx_ref, dg_ref):
        @pl.when(pl.program_id(0) == 0)
        def _():
            dg_ref[...] = jnp.zeros_like(dg_ref)

        dx, dg = _rms_bwd_rows(x_ref[...], gain_ref[...], product(g_ref, w_ref))
        dg_ref[...] += dg
        dx_ref[...] = dx + dr_ref[...]

    g_spec = (pl.BlockSpec((2, tm, 2 * wd), lambda i: (0, i, 0)) if stacked
              else pl.BlockSpec((tm, nq * wd), lambda i: (i, 0)))
    w_spec = pl.BlockSpec((nq, K, wd), lambda i: (0, 0, 0))
    row = pl.BlockSpec((tm, K), lambda i: (i, 0))
    if norm_bwd is None:
        return _carried(*_pcall(
            body, (g, w), name=name, grid=(M // tm,), in_specs=[g_spec, w_spec], out_specs=row,
            out_shape=jax.ShapeDtypeStruct((M, K), out_dtype), semantics=("parallel",), riders=riders), riders)
    vec = pl.BlockSpec((1, K), lambda i: (0, 0))
    return _carried(*_pcall(
        body_norm, (g, w) + tuple(norm_bwd), name=name, grid=(M // tm,),
        in_specs=[g_spec, w_spec, row, vec, row], out_specs=(row, vec),
        out_shape=(jax.ShapeDtypeStruct((M, K), F32), jax.ShapeDtypeStruct((1, K), F32)),
        semantics=("arbitrary",), riders=riders), riders)


def _mm_tn_rs(name, a, g, to, tn=512):
    T, M = a.shape
    N = g.shape[1]
    tt = _row_tile(T, _TN_TOKENS)
    tn = min(tn, N)
    return _matmul(name, (a, g), grid=(M // to, N // tn, T // tt),
                   in_specs=[pl.BlockSpec((tt, to), lambda i, j, k: (k, i)),
                             pl.BlockSpec((tt, tn), lambda i, j, k: (k, j))],
                   o_spec=pl.BlockSpec((to, tn), lambda i, j, k: (i, j)),
                   out_shape=(M, N), out_dtype=F32, dims=TN)


def _mm_tn_cs(name, a, g, nq, to, riders=(), stacked=False):
    T, M = a.shape
    wd = g.shape[-1] * (2 if stacked else 1) // nq
    tt = _row_tile(T, _TN_TOKENS)
    g_spec = (pl.BlockSpec((None, tt, wd), lambda i, j, k: (j // 2, k, j % 2)) if stacked
              else pl.BlockSpec((tt, wd), lambda i, j, k: (k, j)))
    return _matmul(name, (a, g), grid=(M // to, nq, T // tt),
                   in_specs=[pl.BlockSpec((tt, to), lambda i, j, k: (k, i)), g_spec],
                   o_spec=pl.BlockSpec((None, to, wd), lambda i, j, k: (j, i, 0)),
                   out_shape=(nq, M, wd), out_dtype=F32, dims=TN, riders=riders)


def _mm_tn_cs_to_sibling(name, a, g, nq, place, riders=(), send=True):
    T, M = a.shape
    wd = g.shape[-1] // nq
    to = M // 2
    steps = 2 * nq

    def body(s_ref, a_ref, g_ref, o_hbm, land_hbm, acc, wsem, send_sem, recv_sem):
        t = pl.program_id(0)
        c = s_ref[1]

        def writeback(tt):
            half = (tt // nq + 1 + c) % 2
            return pltpu.make_async_copy(acc.at[tt % 2], o_hbm.at[tt % nq, pl.ds(half * to, to), :], wsem.at[tt % 2])

        @pl.when(t >= 2)
        def _():
            writeback(t - 2).wait()

        if send:
            x, y, _, _ = _place()
            to_sibling = _remote(o_hbm.at[:, pl.ds((1 - c) * to, to), :], land_hbm, send_sem.at[0], recv_sem.at[0],
                                 (x, y, 1 - c))

            @pl.when(t == nq + 1)
            def _():
                to_sibling.start()

        acc[t % 2] = _dot_tn(a_ref[...], g_ref[...])
        writeback(t).start()

        @pl.when(t == steps - 1)
        def _():
            writeback(t - 1).wait()
            writeback(t).wait()
            if send:
                to_sibling.wait()

    out, carried = _pcall(
        body, (a, g), name=name, grid=(steps,),
        in_specs=[pl.BlockSpec((T, to), lambda t, s: (0, (t // nq + 1 + s[1]) % 2)),
                  pl.BlockSpec((T, wd), lambda t, s: (0, t % nq))],
        out_specs=(HBM_SPEC, HBM_SPEC),
        out_shape=(jax.ShapeDtypeStruct((nq, M, wd), F32), jax.ShapeDtypeStruct((nq, to, wd), F32)),
        scratch_shapes=[pltpu.VMEM((2, to, wd), F32), pltpu.SemaphoreType.DMA((2,)),
                        pltpu.SemaphoreType.DMA((1,)), pltpu.SemaphoreType.DMA((1,))],
        semantics=("arbitrary",), riders=riders, prefetch=place)
    return _carried(out, carried, riders)


def _rms_fwd(name, x, g, riders=()):
    T, D = x.shape
    tm = _row_tile(T)

    def body(x_ref, g_ref, o_ref):
        o_ref[...] = _rms_rows(x_ref[...], g_ref[...]).astype(o_ref.dtype)

    return _carried(*_pcall(
        body, (x, g), name=name, grid=(T // tm,),
        in_specs=[pl.BlockSpec((tm, D), lambda i: (i, 0)), pl.BlockSpec((1, D), lambda i: (0, 0))],
        out_specs=pl.BlockSpec((tm, D), lambda i: (i, 0)),
        out_shape=jax.ShapeDtypeStruct((T, D), BF16), semantics=("parallel",), riders=riders), riders)


_NORM1_STEPS = 4


def _norm1_and_casts(x, g, shards, place, riders=()):
    T, D = x.shape
    tm = T // _NORM1_STEPS
    names = list(shards)

    def body(s_ref, x_ref, g_ref, *refs):
        w_refs, o_ref, slab_refs = refs[:len(names)], refs[len(names)], refs[len(names) + 1:]
        o_ref[...] = _rms_rows(x_ref[...], g_ref[...]).astype(o_ref.dtype)
        for w_ref, slab_ref in zip(w_refs, slab_refs):
            slab_ref[...] = w_ref[...].astype(slab_ref.dtype)

    in_specs = [pl.BlockSpec((tm, D), lambda i, s: (i, 0)), pl.BlockSpec((1, D), lambda i, s: (0, 0))]
    out_specs = [pl.BlockSpec((tm, D), lambda i, s: (i, 0))]
    out_shape = [jax.ShapeDtypeStruct((T, D), BF16)]
    for n in names:
        r, cc = shards[n].shape
        assert r % (_NORM1_STEPS * 16) == 0
        in_specs.append(pl.BlockSpec((r // _NORM1_STEPS, cc), lambda i, s: (i, 0)))
        out_specs.append(pl.BlockSpec((None, r // _NORM1_STEPS, cc), lambda i, s: (s[0], i, 0)))
        out_shape.append(jax.ShapeDtypeStruct((N_CHIPS, r, cc), BF16))
    out, carried = _pcall(body, (x, g, *[shards[n] for n in names]), name="norm1", grid=(_NORM1_STEPS,),
                          in_specs=in_specs, out_specs=out_specs, out_shape=out_shape, semantics=("parallel",),
                          riders=riders, prefetch=place)
    return _carried((out[0], dict(zip(names, out[1:]))), carried, riders)


def _rms_rows(xv, gain):
    return xv * lax.rsqrt(jnp.mean(xv * xv, axis=-1, keepdims=True) + EPS) * gain


def _rms_bwd_rows(xv, gain, dh):
    r = lax.rsqrt(jnp.mean(xv * xv, axis=-1, keepdims=True) + EPS)
    n = xv * r
    dn = dh * gain
    return r * (dn - n * jnp.mean(dn * n, axis=-1, keepdims=True)), jnp.sum(dh * n, axis=0, keepdims=True)


def _rms_bwd(name, x, g, dh, dres):
    T, D = x.shape
    tm = _row_tile(T)
    has_res = dres is not None

    def body(*refs):
        if has_res:
            x_ref, g_ref, dh_ref, dr_ref, dx_ref, dg_ref = refs
        else:
            x_ref, g_ref, dh_ref, dx_ref, dg_ref = refs

        @pl.when(pl.program_id(0) == 0)
        def _():
            dg_ref[...] = jnp.zeros_like(dg_ref)

        dx, dg = _rms_bwd_rows(x_ref[...], g_ref[...], dh_ref[...])
        dg_ref[...] += dg
        if has_res:
            dx = dx + dr_ref[...]
        dx_ref[...] = dx

    row = pl.BlockSpec((tm, D), lambda i: (i, 0))
    vec = pl.BlockSpec((1, D), lambda i: (0, 0))
    ops = (x, g, dh, dres) if has_res else (x, g, dh)
    return _pallas(
        body, name=name, grid=(T // tm,), in_specs=[row, vec, row] + ([row] if has_res else []),
        out_specs=(row, vec),
        out_shape=(jax.ShapeDtypeStruct((T, D), F32), jax.ShapeDtypeStruct((1, D), F32)),
        compiler_params=_cp("arbitrary"),
    )(*ops)


def _proj_res_norm(name, a, w, res, gain):
    M, K = a.shape
    N = w.shape[1]
    tm = _row_tile(M)

    def body(a_ref, w_ref, r_ref, g_ref, x_ref, h_ref):
        xv = _dot(a_ref[...], w_ref[...]) + r_ref[...]
        x_ref[...] = xv
        h_ref[...] = _rms_rows(xv, g_ref[...]).astype(h_ref.dtype)

    row = pl.BlockSpec((tm, N), lambda i: (i, 0))
    return _pallas(
        body, name=name, grid=(M // tm,),
        in_specs=[pl.BlockSpec((tm, K), lambda i: (i, 0)), pl.BlockSpec((K, N), lambda i: (0, 0)), row,
                  pl.BlockSpec((1, N), lambda i: (0, 0))],
        out_specs=(row, row), out_shape=(jax.ShapeDtypeStruct((M, N), F32), jax.ShapeDtypeStruct((M, N), BF16)),
        compiler_params=_cp("parallel"),
    )(a, w, res, gain)


def _proj_res_loss(name, a, w, res, tgt, gain):
    M, K = a.shape
    D = w.shape[1]
    tm = _row_tile(M)

    def body(a_ref, w_ref, r_ref, t_ref, g_ref, dx_ref, dg_ref, loss_ref):
        @pl.when(pl.program_id(0) == 0)
        def _():
            dg_ref[...] = jnp.zeros_like(dg_ref)
            loss_ref[...] = jnp.zeros_like(loss_ref)

        xv = _dot(a_ref[...], w_ref[...]) + r_ref[...]
        gv = g_ref[...]
        diff = _rms_rows(xv, gv) - t_ref[...]
        loss_ref[...] += 0.5 * jnp.sum(jnp.mean(diff * diff, axis=-1, keepdims=True))
        dx, dg = _rms_bwd_rows(xv, gv, diff * (1.0 / D))
        dg_ref[...] += dg
        dx_ref[...] = dx

    row = pl.BlockSpec((tm, D), lambda i: (i, 0))
    vec = pl.BlockSpec((1, D), lambda i: (0, 0))
    return _pallas(
        body, name=name, grid=(M // tm,),
        in_specs=[pl.BlockSpec((tm, K), lambda i: (i, 0)), pl.BlockSpec((K, D), lambda i: (0, 0)), row, row, vec],
        out_specs=(row, vec, pl.BlockSpec((8, 128), lambda i: (0, 0))),
        out_shape=(jax.ShapeDtypeStruct((M, D), F32), jax.ShapeDtypeStruct((1, D), F32),
                   jax.ShapeDtypeStruct((8, 128), F32)),
        compiler_params=_cp("arbitrary"),
    )(a, w, res, tgt, gain)


def _gmlp_pieces(zu, zv, lng, lnb, ws_ref, bs_ref):
    u, du = _gelu_and_grad(zu)
    v, dv = _gelu_and_grad(zv)
    mu = jnp.mean(v, axis=-1, keepdims=True)
    vc = v - mu
    rstd = lax.rsqrt(jnp.mean(vc * vc, axis=-1, keepdims=True) + EPS)
    vhat = vc * rstd
    vn = vhat * lng + lnb
    row = lax.broadcasted_iota(jnp.int32, (GM_CHUNK, GM_CHUNK), 0)
    col = lax.broadcasted_iota(jnp.int32, (GM_CHUNK, GM_CHUNK), 1)
    tril = row >= col
    wms, mixed = [], []
    for g in range(GM_GROUPS):
        sl = slice(g * 128, (g + 1) * 128)
        wm = jnp.where(tril, ws_ref[g], 0.0)
        wms.append(wm)
        mixed.append(_dot(wm, vn[:, sl]) + bs_ref[g])
    return u, du, dv, rstd, vhat, vn, wms, mixed, tril


def _gmlp_fwd(proj, lng, lnb, ws, bs_col):
    T = proj.shape[0]
    n = T // GM_CHUNK

    ring = 3
    assert COL_ZV == COL_ZU + 1 and n >= ring

    def body(proj_hbm, lng_ref, lnb_ref, ws_ref, bs_ref, o_ref, z_buf, z_sem):
        s = pl.program_id(0)

        def fetch(t):
            return pltpu.make_async_copy(
                proj_hbm.at[pl.ds(t * GM_CHUNK, GM_CHUNK), pl.ds(COL_ZU * GM_WIDTH, 2 * GM_WIDTH)],
                z_buf.at[t % ring], z_sem.at[t % ring])

        @pl.when(s == 0)
        def _():
            for t in range(ring - 1):
                fetch(t).start()

        @pl.when(s + ring - 1 < n)
        def _():
            fetch(s + ring - 1).start()

        fetch(s).wait()
        z = z_buf[s % ring].astype(F32)
        u, _, _, _, _, _, _, mixed, _ = _gmlp_pieces(z[:, :GM_WIDTH], z[:, GM_WIDTH:], lng_ref[...], lnb_ref[...],
                                                     ws_ref, bs_ref)
        for g in range(GM_GROUPS):
            sl = slice(g * 128, (g + 1) * 128)
            o_ref[:, sl] = (u[:, sl] * mixed[g]).astype(o_ref.dtype)

    vec = pl.BlockSpec((1, GM_WIDTH), lambda i: (0, 0))
    return _pallas(
        body, name="gmlp_fwd", grid=(n,),
        in_specs=[HBM_SPEC, vec, vec,
                  pl.BlockSpec((GM_GROUPS, 128, 128), lambda i: (0, 0, 0)),
                  pl.BlockSpec((GM_GROUPS, 128, 1), lambda i: (0, 0, 0))],
        out_specs=pl.BlockSpec((GM_CHUNK, 512), lambda i: (i, 0)),
        out_shape=jax.ShapeDtypeStruct((T, GM_WIDTH), BF16),
        scratch_shapes=[pltpu.VMEM((ring, GM_CHUNK, 2 * GM_WIDTH), BF16), pltpu.SemaphoreType.DMA((ring,))],
        compiler_params=_cp("arbitrary"),
    )(proj, lng, lnb, ws, bs_col)


def _gmlp_bwd(proj, d_out, lng, lnb, ws, bs_col, riders=()):
    T = proj.shape[0]
    n = T // GM_CHUNK

    def body(zu_ref, zv_ref, do_ref, lng_ref, lnb_ref, ws_ref, bs_ref,
             dz_ref, dws_ref, dbs_ref, dlng_ref, dlnb_ref, dm_acc):
        i = pl.program_id(0)

        @pl.when(i == 0)
        def _():
            dws_ref[...] = jnp.zeros_like(dws_ref)
            dlng_ref[...] = jnp.zeros_like(dlng_ref)
            dlnb_ref[...] = jnp.zeros_like(dlnb_ref)
            dm_acc[...] = jnp.zeros_like(dm_acc)

        lng_v = lng_ref[...]
        u, du, dv, rstd, vhat, vn, wms, mixed, tril = _gmlp_pieces(zu_ref[...].astype(F32), zv_ref[...].astype(F32),
                                                                  lng_v, lnb_ref[...],
                                                                  ws_ref, bs_ref)
        do = do_ref[...]
        dvn_parts = []
        for g in range(GM_GROUPS):
            sl = slice(g * 128, (g + 1) * 128)
            dog = do[:, sl]
            dz_ref[:, sl] = (dog * mixed[g] * du[:, sl]).astype(dz_ref.dtype)
            dmix = dog * u[:, sl]
            dm_acc[:, sl] += dmix
            dws_ref[g] += jnp.where(tril, _dot_nt(dmix, vn[:, sl]), 0.0)
            dvn_parts.append(_dot_tn(wms[g], dmix))
        dvn = jnp.concatenate(dvn_parts, axis=1)
        dlng_ref[...] += jnp.sum(dvn * vhat, axis=0, keepdims=True)
        dlnb_ref[...] += jnp.sum(dvn, axis=0, keepdims=True)
        dvh = dvn * lng_v
        dvv = rstd * (dvh - jnp.mean(dvh, axis=-1, keepdims=True)
                      - vhat * jnp.mean(dvh * vhat, axis=-1, keepdims=True))
        dz_ref[:, GM_WIDTH:] = (dvv * dv).astype(dz_ref.dtype)

        @pl.when(i == n - 1)
        def _():
            for g in range(GM_GROUPS):
                dbs_ref[g] = jnp.sum(dm_acc[:, g * 128:(g + 1) * 128], axis=1, keepdims=True)

    vec = pl.BlockSpec((1, GM_WIDTH), lambda i: (0, 0))
    wsp = pl.BlockSpec((GM_GROUPS, 128, 128), lambda i: (0, 0, 0))
    bsp = pl.BlockSpec((GM_GROUPS, 128, 1), lambda i: (0, 0, 0))
    return _carried(*_pcall(
        body, (proj, proj, d_out, lng, lnb, ws, bs_col), name="gmlp_bwd", grid=(n,),
        in_specs=[pl.BlockSpec((GM_CHUNK, 512), lambda i: (i, COL_ZU)),
                  pl.BlockSpec((GM_CHUNK, 512), lambda i: (i, COL_ZV)),
                  pl.BlockSpec((None, GM_CHUNK, 512), lambda i: (0, i, 0)), vec, vec, wsp, bsp],
        out_specs=(pl.BlockSpec((GM_CHUNK, 2 * GM_WIDTH), lambda i: (i, 0)), wsp, bsp, vec, vec),
        out_shape=(jax.ShapeDtypeStruct((T, 2 * GM_WIDTH), BF16),
                   jax.ShapeDtypeStruct((GM_GROUPS, 128, 128), F32), jax.ShapeDtypeStruct((GM_GROUPS, 128, 1), F32),
                   jax.ShapeDtypeStruct((1, GM_WIDTH), F32), jax.ShapeDtypeStruct((1, GM_WIDTH), F32)),
        scratch_shapes=[pltpu.VMEM((GM_CHUNK, GM_WIDTH), F32)],
        semantics=("arbitrary",), riders=riders), riders)


def _hgrn_lower_bound(lbl):
    return 1.0 / (1.0 + jnp.exp(lbl[1:2, :] - lbl[0:1, :]))


def _hgrn_gates(hq, hf, lb):
    C = HG_CHUNK
    sg = _sigmoid(hf)
    fg = lb + (1.0 - lb) * sg
    sq = _sigmoid(hq)
    row = lax.broadcasted_iota(jnp.int32, (C, C), 0)
    col = lax.broadcasted_iota(jnp.int32, (C, C), 1)
    tril = row >= col
    logf = jnp.log(fg)
    a = _dot_01(tril, logf)
    a_last = jnp.sum(logf, axis=0, keepdims=True)
    first_half = lax.broadcasted_iota(jnp.int32, logf.shape, 0) < (C // 2)
    a_mid = jnp.sum(jnp.where(first_half, logf, 0.0), axis=0, keepdims=True)
    ea, ei, eki, ekl = jnp.exp(a), jnp.exp(a - a_mid), jnp.exp(a_mid - a), jnp.exp(a_last - a)
    k = 1.0 - fg
    q = hq * sq
    qi = (q * ei).astype(BF16).astype(F32)
    ki = (k * eki).astype(BF16).astype(F32)
    return dict(sg=sg, fg=fg, sq=sq, tril=tril, ea=ea, ei=ei, eki=eki, ekl=ekl, e_last=jnp.exp(a_last),
                qe=q * ea, qi=qi, ki=ki, kl=k * ekl)


def _heads(x):
    return [x[:, h * HG_DIM:(h + 1) * HG_DIM] for h in range(HG_HEADS)]


def _hgrn_fwd(proj, lbl, gh, B, S, riders=()):
    C = HG_CHUNK
    NC = S // C
    W = HG_HEADS * HG_DIM

    def body(q_ref, f_ref, i_ref, g_ref, lbl_ref, gh_ref, o_ref, bo_ref, st_ref, state):
        @pl.when(pl.program_id(0) == 0)
        def _():
            state[...] = jnp.zeros_like(state)

        lb = _hgrn_lower_bound(lbl_ref[...])
        ghv = gh_ref[...]
        for b in range(B):
            gt = _hgrn_gates(q_ref[b].astype(F32), f_ref[b].astype(F32), lb)
            v = _heads(i_ref[b])
            qe, qi, ki, kl, e_last = (_heads(gt[n]) for n in ("qe", "qi", "ki", "kl", "e_last"))
            outs, normed = [], []
            for h in range(HG_HEADS):
                p = jnp.where(gt["tril"], _dot_nt(qi[h], ki[h]), 0.0)
                st = state[b, h]
                st_ref[b, h] = st
                o = _dot_nt(qe[h], st) + _dot(p, v[h])
                state[b, h] = st * e_last[h] + _dot_tn(v[h], kl[h])
                outs.append(o)
                normed.append(o * lax.rsqrt(jnp.mean(o * o, axis=-1, keepdims=True) + EPS) * ghv)
            o_ref[b] = jnp.concatenate(outs, axis=1)
            hg = g_ref[b].astype(F32)
            bo_ref[b] = (jnp.concatenate(normed, axis=1) * (hg * _sigmoid(hg))).astype(bo_ref.dtype)

    def col(cb):
        return pl.BlockSpec((B, C, 512), lambda c: (0, c, cb))

    tile = pl.BlockSpec((B, C, W), lambda c: (0, c, 0))
    proj3 = proj.reshape(B, S, proj.shape[-1])
    out, carried = _pcall(
        body, (proj3, proj3, proj3, proj3, lbl, gh), name="hgrn_fwd", grid=(NC,),
        in_specs=[col(COL_HQ), col(COL_HF), col(COL_HI), col(COL_HG),
                  pl.BlockSpec((2, W), lambda c: (0, 0)), pl.BlockSpec((1, HG_DIM), lambda c: (0, 0))],
        out_specs=(tile, tile, pl.BlockSpec((B, None, HG_HEADS, 128, 128), lambda c: (0, c, 0, 0, 0))),
        out_shape=(jax.ShapeDtypeStruct((B, S, W), F32), jax.ShapeDtypeStruct((B, S, W), BF16),
                   jax.ShapeDtypeStruct((B, NC, HG_HEADS, 128, 128), F32)),
        scratch_shapes=[pltpu.VMEM((B, HG_HEADS, 128, 128), F32)],
        semantics=("arbitrary",), riders=riders)
    o_h, b_out, states = out
    out = (o_h, b_out.reshape(B * S, W), states)
    return (out, carried) if riders else out


def _hgrn_bwd(proj, o_saved, states, d_out, lbl, gh, others, B, S, riders=()):
    C = HG_CHUNK
    NC = S // C
    W = HG_HEADS * HG_DIM
    d_gm, d_xq, d_gates = (t.reshape(B, S, t.shape[-1]) for t in others)
    own0 = d_gm.shape[-1]
    xq0 = own0 + 4 * W
    gates0 = xq0 + d_xq.shape[-1]

    def body(q_ref, f_ref, i_ref, g_ref, o_ref, st_ref, do_ref, lbl_ref, gh_ref, gm_ref, xq_ref, gates_ref,
             d_ref, dlbl_ref, dgh_ref, dstate, dlb_acc):
        c = pl.program_id(0)
        d_ref[:, :, :own0] = gm_ref[...]
        d_ref[:, :, xq0:gates0] = xq_ref[...]
        d_ref[:, :, gates0:] = gates_ref[...]

        def put(b, k, val):
            d_ref[b, :, own0 + k * W:own0 + (k + 1) * W] = val.astype(d_ref.dtype)

        @pl.when(c == 0)
        def _():
            dstate[...] = jnp.zeros_like(dstate)
            dgh_ref[...] = jnp.zeros_like(dgh_ref)
            dlb_acc[...] = jnp.zeros_like(dlb_acc)

        lb = _hgrn_lower_bound(lbl_ref[...])
        ghv = gh_ref[...]
        row = lax.broadcasted_iota(jnp.int32, (C, C), 0)
        colm = lax.broadcasted_iota(jnp.int32, (C, C), 1)
        triu = colm >= row
        for b in range(B):
            hq, hg = q_ref[b].astype(F32), g_ref[b].astype(F32)
            gt = _hgrn_gates(hq, f_ref[b].astype(F32), lb)
            tril = gt["tril"]
            v = _heads(i_ref[b])
            qe, qi, ki, kl, e_last = (_heads(gt[n]) for n in ("qe", "qi", "ki", "kl", "e_last"))
            sgg = _sigmoid(hg)
            don_all = do_ref[b] * (hg * sgg)
            o, don = _heads(o_ref[b]), _heads(don_all)
            d_qe, d_qi, d_ki, d_kl, dv, n_all, dal = [], [], [], [], [], [], []
            for h in range(HG_HEADS):
                r = lax.rsqrt(jnp.mean(o[h] * o[h], axis=-1, keepdims=True) + EPS)
                n = o[h] * r
                n_all.append(n)
                dgh_ref[...] += jnp.sum(don[h] * n, axis=0, keepdims=True)
                dn = don[h] * ghv
                d_o = r * (dn - n * jnp.mean(dn * n, axis=-1, keepdims=True))
                st, dst = st_ref[b, h], dstate[b, h]
                p = jnp.where(tril, _dot_nt(qi[h], ki[h]), 0.0)
                dp = jnp.where(tril, _dot_nt(d_o, v[h]), 0.0)
                d_qe.append(_dot(d_o, st))
                d_qi.append(_dot(dp, ki[h]))
                d_ki.append(_dot_tn(dp, qi[h]))
                d_kl.append(_dot(v[h], dst))
                dv.append(_dot_tn(p, d_o) + _dot_nt(kl[h], dst))
                dstate[b, h] = dst * e_last[h] + _dot_tn(d_o, qe[h])
                dal.append(jnp.sum(dst * st, axis=0, keepdims=True) * e_last[h])
            d_qe, d_qi, d_ki, d_kl, n_all, dal = (jnp.concatenate(t, axis=1)
                                                  for t in (d_qe, d_qi, d_ki, d_kl, n_all, dal))
            put(b, 3, do_ref[b] * n_all * jnp.tile(ghv, (1, HG_HEADS)) * (sgg * (1.0 + hg * (1.0 - sgg))))
            put(b, 2, jnp.concatenate(dv, axis=1))
            d_a_last = dal + jnp.sum(d_kl * gt["kl"], axis=0, keepdims=True)
            dq = d_qe * gt["ea"] + d_qi * gt["ei"]
            dk = d_ki * gt["eki"] + d_kl * gt["ekl"]
            da = d_qe * gt["qe"] + d_qi * gt["qi"] - d_ki * gt["ki"] - d_kl * gt["kl"]
            dlogf = _dot_01(triu, da) + d_a_last
            sg, sq = gt["sg"], gt["sq"]
            dfg = dlogf / gt["fg"] - dk
            put(b, 1, dfg * (1.0 - lb) * sg * (1.0 - sg))
            dlb_acc[...] += jnp.sum(dfg * (1.0 - sg), axis=0, keepdims=True)
            put(b, 0, dq * (sq * (1.0 + hq * (1.0 - sq))))

        @pl.when(c == NC - 1)
        def _():
            dlb = dlb_acc[...]
            first = lax.broadcasted_iota(jnp.int32, (2, W), 0) == 0
            dlbl_ref[...] = jnp.where(first, dlb * lb * (1.0 - lb), -dlb * lb * (1.0 - lb))

    def col(cb):
        return pl.BlockSpec((B, C, 512), lambda c: (0, NC - 1 - c, cb))

    tile = pl.BlockSpec((B, C, W), lambda c: (0, NC - 1 - c, 0))
    proj3 = proj.reshape(B, S, proj.shape[-1])

    def rows(width):
        return pl.BlockSpec((B, C, width), lambda c: (0, NC - 1 - c, 0))

    width = proj.shape[-1]
    out, carried = _pcall(
        body, (proj3, proj3, proj3, proj3, o_saved, states, d_out.reshape(3, B, S, W), lbl, gh, d_gm, d_xq, d_gates),
        name="hgrn_bwd", grid=(NC,),
        in_specs=[col(COL_HQ), col(COL_HF), col(COL_HI), col(COL_HG), tile,
                  pl.BlockSpec((B, None, HG_HEADS, 128, 128), lambda c: (0, NC - 1 - c, 0, 0, 0)),
                  pl.BlockSpec((None, B, C, W), lambda c: (1, 0, NC - 1 - c, 0)),
                  pl.BlockSpec((2, W), lambda c: (0, 0)), pl.BlockSpec((1, HG_DIM), lambda c: (0, 0)),
                  rows(d_gm.shape[-1]), rows(d_xq.shape[-1]), rows(d_gates.shape[-1])],
        out_specs=(rows(width), pl.BlockSpec((2, W), lambda c: (0, 0)), pl.BlockSpec((1, HG_DIM), lambda c: (0, 0))),
        out_shape=(jax.ShapeDtypeStruct((B, S, width), BF16), jax.ShapeDtypeStruct((2, W), F32),
                   jax.ShapeDtypeStruct((1, HG_DIM), F32)),
        scratch_shapes=[pltpu.VMEM((B, HG_HEADS, 128, 128), F32), pltpu.VMEM((1, W), F32)],
        semantics=("arbitrary",), riders=riders)
    out = (out[0].reshape(B * S, width),) + tuple(out[1:])
    return (out, carried) if riders else out


_XA_SCALE = XA_DIM ** -0.5


def _attn_probs(qh, kh):
    s = _dot_nt(qh, kh) * _XA_SCALE
    e = jnp.exp(s - jnp.max(s, axis=-1, keepdims=True))
    return e / jnp.sum(e, axis=-1, keepdims=True)


def _attn_fwd(proj, kv, B, S):
    T = B * S
    tq = _row_tile(S)
    nq = S // tq
    W = XA_HEADS * XA_DIM

    def body(q_ref, kv_ref, o_ref):
        for h in range(XA_HEADS):
            sl = slice(h * 128, (h + 1) * 128)
            p = _attn_probs(q_ref[:, sl], kv_ref[:, sl])
            o_ref[:, sl] = _dot(p, kv_ref[:, W + h * 128:W + (h + 1) * 128]).astype(o_ref.dtype)

    return _pallas(
        body, name="attn_fwd", grid=(B, nq),
        in_specs=[pl.BlockSpec((tq, 512), lambda b, i: (b * nq + i, COL_XQ)),
                  pl.BlockSpec((MEM_LEN, 2 * W), lambda b, i: (b, 0))],
        out_specs=pl.BlockSpec((tq, W), lambda b, i: (b * nq + i, 0)),
        out_shape=jax.ShapeDtypeStruct((T, W), BF16), compiler_params=_cp("parallel", "parallel"),
    )(proj, kv)


def _attn_bwd(proj, kv, d_out, B, S):
    T = B * S
    tq = _row_tile(S)
    nq = S // tq
    W = XA_HEADS * XA_DIM

    def body(q_ref, kv_ref, do_ref, dq_ref, dkv_ref):
        @pl.when(pl.program_id(1) == 0)
        def _():
            dkv_ref[...] = jnp.zeros_like(dkv_ref)

        for h in range(XA_HEADS):
            sl = slice(h * 128, (h + 1) * 128)
            slv = slice(W + h * 128, W + (h + 1) * 128)
            qh = q_ref[:, sl]
            kh = kv_ref[:, sl]
            p = _attn_probs(qh, kh)
            dc = do_ref[:, sl]
            dp = _dot_nt(dc, kv_ref[:, slv])
            ds = p * (dp - jnp.sum(dp * p, axis=-1, keepdims=True)) * _XA_SCALE
            dq_ref[:, sl] = _dot(ds, kh).astype(dq_ref.dtype)
            dkv_ref[:, sl] += _dot_tn(ds, qh)
            dkv_ref[:, slv] += _dot_tn(p, dc)

    kvspec = pl.BlockSpec((MEM_LEN, 2 * W), lambda b, i: (b, 0))
    tile = pl.BlockSpec((tq, W), lambda b, i: (b * nq + i, 0))
    return _pallas(
        body, name="attn_bwd", grid=(B, nq),
        in_specs=[pl.BlockSpec((tq, 512), lambda b, i: (b * nq + i, COL_XQ)), kvspec,
                  pl.BlockSpec((None, tq, W), lambda b, i: (2, b * nq + i, 0))],
        out_specs=(tile, kvspec),
        out_shape=(jax.ShapeDtypeStruct((T, W), BF16), jax.ShapeDtypeStruct((B * MEM_LEN, 2 * W), F32)),
        compiler_params=_cp("parallel", "arbitrary"),
    )(proj, kv, d_out)


_MERGE_TM = 256
_GATE_W = 512


def _gate_specs(tm):
    base = COL_GATE0 // _GATE_W
    return [pl.BlockSpec((tm, _GATE_W), functools.partial(lambda i, k: (i, base + k), k=k)) for k in range(6)]


def _merge_fwd(a_out, b_out, c_out, wb, proj, riders=()):
    T = a_out.shape[0]
    tm = _row_tile(T, _MERGE_TM)
    nq, _, wd = wb.shape
    per_half = _GATE_W // wd

    def body(a_ref, b_ref, c_ref, w_ref, *rest):
        gates, (m_ref, up_ref) = rest[:6], rest[6:]
        for hf in range(2):
            cols = slice(hf * _GATE_W, (hf + 1) * _GATE_W)
            acc = None
            for n, br in enumerate((a_ref, b_ref, c_ref)):
                x = br[...]
                up = jnp.concatenate([_dot(x, w_ref[per_half * hf + j, n * BR_WIDTH:(n + 1) * BR_WIDTH, :])
                                      for j in range(per_half)], axis=1)
                up_ref[n, :, cols] = up.astype(up_ref.dtype)
                term = _sigmoid(gates[2 * n + hf][...].astype(F32)) * up
                acc = term if acc is None else acc + term
            m_ref[:, cols] = acc.astype(m_ref.dtype)

    br_spec = pl.BlockSpec((tm, BR_WIDTH), lambda i: (i, 0))
    return _carried(*_pcall(
        body, (a_out, b_out, c_out, wb, *([proj] * 6)), name="merge_fwd", grid=(T // tm,),
        in_specs=[br_spec, br_spec, br_spec,
                  pl.BlockSpec((nq, 3 * BR_WIDTH, wd), lambda i: (0, 0, 0))] + _gate_specs(tm),
        out_specs=(pl.BlockSpec((tm, D_MODEL), lambda i: (i, 0)), pl.BlockSpec((3, tm, D_MODEL), lambda i: (0, i, 0))),
        out_shape=(jax.ShapeDtypeStruct((T, D_MODEL), BF16), jax.ShapeDtypeStruct((3, T, D_MODEL), BF16)),
        semantics=("parallel",), riders=riders), riders)


def _branch_bwd_act(d_ups, wb, riders=()):
    _, T, D = d_ups.shape
    nq, _, wd = wb.shape
    tm = _row_tile(T)

    nt = T // tm
    ring = 3
    assert 3 * nt >= ring

    def body(d_hbm, w_ref, o_ref, d_buf, d_sem):
        s = pl.program_id(0) * nt + pl.program_id(1)

        def fetch(t):
            return pltpu.make_async_copy(d_hbm.at[t // nt, pl.ds((t % nt) * tm, tm), :], d_buf.at[t % ring],
                                         d_sem.at[t % ring])

        @pl.when(s == 0)
        def _():
            for t in range(ring - 1):
                fetch(t).start()

        @pl.when(s + ring - 1 < 3 * nt)
        def _():
            fetch(s + ring - 1).start()

        fetch(s).wait()
        acc = None
        for q in range(nq):
            part = _dot_nt(d_buf[s % ring, :, q * wd:(q + 1) * wd], w_ref[q])
            acc = part if acc is None else acc + part
        o_ref[...] = acc

    return _carried(*_pcall(
        body, (d_ups, wb), name="d_branch", grid=(3, nt),
        in_specs=[HBM_SPEC, pl.BlockSpec((nq, BR_WIDTH, wd), lambda n, i: (0, n, 0))],
        out_specs=pl.BlockSpec((None, tm, BR_WIDTH), lambda n, i: (n, i, 0)),
        out_shape=jax.ShapeDtypeStruct((3, T, BR_WIDTH), F32),
        scratch_shapes=[pltpu.VMEM((ring, tm, D), d_ups.dtype), pltpu.SemaphoreType.DMA((ring,))],
        semantics=("arbitrary", "arbitrary"), riders=riders), riders)


def _branch_bwd_weight(name, br, d_ups, n, into=None):
    T = br.shape[0]
    D = d_ups.shape[2]
    wd = D // N_CHIPS
    tt = _row_tile(T, _TN_TOKENS)
    n_br = d_ups.shape[0]

    def body(b_ref, d_ref, *rest):
        o_ref = rest[-1]
        k = pl.program_id(0)
        for q in range(N_CHIPS):
            part = _dot_tn(b_ref[...], d_ref[:, q * wd:(q + 1) * wd])

            @pl.when(k == 0)
            def _():
                o_ref[q] = part

            @pl.when(k > 0)
            def _():
                o_ref[q] += part

    return _pallas(
        body, name=name, grid=(T // tt,),
        in_specs=[pl.BlockSpec((tt, BR_WIDTH), lambda k: (k, 0)),
                  pl.BlockSpec((None, tt, D), lambda k: (n, k, 0))] + ([] if into is None else [HBM_SPEC]),
        out_specs=pl.BlockSpec((N_CHIPS, BR_WIDTH, wd), lambda k: (0, n, 0)),
        out_shape=jax.ShapeDtypeStruct((N_CHIPS, n_br * BR_WIDTH, wd), F32),
        input_output_aliases={} if into is None else {2: 0}, compiler_params=_cp("arbitrary"),
    )(br, d_ups, *(() if into is None else (into,)))


def _merge_bwd(d_merged, ups, proj, riders=()):
    T = d_merged.shape[0]
    tm = _row_tile(T, _MERGE_TM)

    def body(dm_ref, up_ref, *rest):
        gates, (dup_ref, dg_ref) = rest[:6], rest[6:]
        for hf in range(2):
            cols = slice(hf * _GATE_W, (hf + 1) * _GATE_W)
            dm = dm_ref[:, cols]
            for n in range(3):
                gate = _sigmoid(gates[2 * n + hf][...].astype(F32))
                dup_ref[n, :, cols] = (dm * gate).astype(dup_ref.dtype)
                dg_ref[:, n * D_MODEL + hf * _GATE_W:n * D_MODEL + (hf + 1) * _GATE_W] = (
                    dm * up_ref[n, :, cols].astype(F32) * gate * (1.0 - gate)).astype(dg_ref.dtype)

    tile = pl.BlockSpec((tm, D_MODEL), lambda i: (i, 0))
    tile3 = pl.BlockSpec((3, tm, D_MODEL), lambda i: (0, i, 0))
    return _carried(*_pcall(
        body, (d_merged, ups, *([proj] * 6)), name="merge_bwd", grid=(T // tm,),
        in_specs=[tile, tile3] + _gate_specs(tm),
        out_specs=(tile3, pl.BlockSpec((tm, 3 * D_MODEL), lambda i: (i, 0))),
        out_shape=(jax.ShapeDtypeStruct((3, T, D_MODEL), BF16), jax.ShapeDtypeStruct((T, 3 * D_MODEL), BF16)),
        semantics=("parallel",), riders=riders), riders)


_CONV_TF = D_FF // 2
_CONV_TS = 256
_HALO = 16


def _conv_fwd(ab, cw, cb, B, S):
    T = B * S
    ts = _row_tile(S, _CONV_TS)
    tf = _CONV_TF
    nb = D_FF // tf
    tps = S // ts
    hb = ts // _HALO

    steps = (T // ts) * nb
    ring = 3

    def body(ab_hbm, p_ref, w_ref, cb_ref, o_ref, a_buf, b_buf, a_sem, b_sem):
        s = pl.program_id(0) * nb + pl.program_id(1)

        def fetch(t):
            rows, slot = pl.ds((t // nb) * ts, ts), t % ring
            return (pltpu.make_async_copy(ab_hbm.at[rows, pl.ds((t % nb) * tf, tf)], a_buf.at[slot], a_sem.at[slot]),
                    pltpu.make_async_copy(ab_hbm.at[rows, pl.ds((t % nb + nb) * tf, tf)], b_buf.at[slot],
                                          b_sem.at[slot]))

        @pl.when(s == 0)
        def _():
            for t in range(ring - 1):
                for cp in fetch(t):
                    cp.start()

        @pl.when(s + ring - 1 < steps)
        def _():
            for cp in fetch(s + ring - 1):
                cp.start()

        for cp in fetch(s):
            cp.wait()
        start = (pl.program_id(0) % tps) == 0
        a = a_buf[s % ring].astype(F32)
        prev = jnp.where(start, 0.0, p_ref[...].astype(F32))
        ext = jnp.concatenate([prev, a], axis=0)
        a1 = pltpu.roll(ext, 1, 0)[_HALO:, :]
        a2 = pltpu.roll(ext, 2, 0)[_HALO:, :]
        ac = cb_ref[...] + w_ref[0] * a2 + w_ref[1] * a1 + w_ref[2] * a
        o_ref[...] = (ac * _sigmoid(ac) * b_buf[s % ring].astype(F32)).astype(o_ref.dtype)

    assert steps >= ring
    return _pallas(
        body, name="conv_fwd", grid=(T // ts, nb),
        in_specs=[HBM_SPEC,
                  pl.BlockSpec((_HALO, tf), lambda i, j: (jnp.maximum(i * hb - 1, 0), j)),
                  pl.BlockSpec((3, 1, tf), lambda i, j: (0, 0, j)),
                  pl.BlockSpec((1, tf), lambda i, j: (0, j))],
        out_specs=pl.BlockSpec((ts, tf), lambda i, j: (i, j)),
        out_shape=jax.ShapeDtypeStruct((T, D_FF), BF16),
        scratch_shapes=[pltpu.VMEM((ring, ts, tf), BF16), pltpu.VMEM((ring, ts, tf), BF16),
                        pltpu.SemaphoreType.DMA((ring,)), pltpu.SemaphoreType.DMA((ring,))],
        compiler_params=_cp("arbitrary", "arbitrary"),
    )(ab, ab, cw, cb)


def _conv_bwd(ab, d_ff, cw, cb, B, S, riders=()):
    T = B * S
    ts = _row_tile(S, _CONV_TS)
    tf = _CONV_TF
    nb = D_FF // tf
    tps = S // ts
    hb = ts // _HALO
    last_h = T // _HALO - 1
    n_ext = ts + _HALO

    def body(a_ref, ap_ref, an_ref, b_ref, bn_ref, d_ref, dn_ref, w_ref, cb_ref, dab_ref, dw_ref, dcb_ref):
        i = pl.program_id(1)

        @pl.when(i == 0)
        def _():
            dw_ref[...] = jnp.zeros_like(dw_ref)
            dcb_ref[...] = jnp.zeros_like(dcb_ref)

        start = (i % tps) == 0
        end = (i % tps) == tps - 1
        a = a_ref[...].astype(F32)
        ext = jnp.concatenate([jnp.where(start, 0.0, ap_ref[...].astype(F32)), a, an_ref[...].astype(F32)], axis=0)
        r1 = pltpu.roll(ext, 1, 0)[_HALO:, :]
        r2 = pltpu.roll(ext, 2, 0)[_HALO:, :]
        ac = cb_ref[...] + w_ref[0] * r2 + w_ref[1] * r1 + w_ref[2] * ext[_HALO:, :]
        sg = _sigmoid(ac)
        d_e = jnp.concatenate([d_ref[...].astype(F32), jnp.where(end, 0.0, dn_ref[...].astype(F32))], axis=0)
        b_e = jnp.concatenate([b_ref[...].astype(F32), bn_ref[...].astype(F32)], axis=0)
        dab_ref[1] = (d_e[:ts, :] * (ac * sg)[:ts, :]).astype(dab_ref.dtype)
        dac = d_e * b_e * sg * (1.0 + ac * (1.0 - sg))
        u1 = pltpu.roll(dac, n_ext - 1, 0)[:ts, :]
        u2 = pltpu.roll(dac, n_ext - 2, 0)[:ts, :]
        dac0 = dac[:ts, :]
        dab_ref[0] = (w_ref[2] * dac0 + w_ref[1] * u1 + w_ref[0] * u2).astype(dab_ref.dtype)
        dcb_ref[...] += jnp.sum(dac0, axis=0, keepdims=True)
        dw_ref[2] += jnp.sum(dac0 * a, axis=0, keepdims=True)
        dw_ref[1] += jnp.sum(dac0 * r1[:ts, :], axis=0, keepdims=True)
        dw_ref[0] += jnp.sum(dac0 * r2[:ts, :], axis=0, keepdims=True)

    def cur(off):
        return pl.BlockSpec((ts, tf), lambda j, i: (i, j + off))

    def nxt(off):
        return pl.BlockSpec((_HALO, tf), lambda j, i: (jnp.minimum((i + 1) * hb, last_h), j + off))

    return _carried(*_pcall(
        body, (ab, ab, ab, ab, ab, d_ff, d_ff, cw, cb), name="conv_bwd", grid=(nb, T // ts),
        in_specs=[cur(0), pl.BlockSpec((_HALO, tf), lambda j, i: (jnp.maximum(i * hb - 1, 0), j)), nxt(0),
                  cur(nb), nxt(nb), cur(0), nxt(0),
                  pl.BlockSpec((3, 1, tf), lambda j, i: (0, 0, j)), pl.BlockSpec((1, tf), lambda j, i: (0, j))],
        out_specs=(pl.BlockSpec((2, ts, tf), lambda j, i: (0, i, j)), pl.BlockSpec((3, 1, tf), lambda j, i: (0, 0, j)),
                   pl.BlockSpec((1, tf), lambda j, i: (0, j))),
        out_shape=(jax.ShapeDtypeStruct((2, T, D_FF), BF16),
                   jax.ShapeDtypeStruct((3, 1, D_FF), F32), jax.ShapeDtypeStruct((1, D_FF), F32)),
        semantics=("parallel", "arbitrary"), riders=riders), riders)


def _local_step(x, mem, tgt, p, comm, B, S):
    g = {}
    h, slabs = comm.carry(
        "norm1", lambda r: _norm1_and_casts(x, p["norm1_g"], p["cast_beside_norm1"], comm.place, riders=r))
    comm.slabs.update(slabs)
    proj = comm.carry("in_proj", lambda r: _mm_cs("in_proj", h, comm.w("w_in"), BF16, riders=r))
    a_out = _gmlp_fwd(proj, p["ln_v_g"], p["ln_v_b"], p["w_spatial"], p["b_spatial"])
    o_h, b_out, states = comm.carry(
        "hgrn_fwd", lambda r: _hgrn_fwd(proj, p["lb_logits"], p["hgrn_norm_g"], B, S, riders=r))
    memn = _rms_fwd("mem_norm", mem, p["mem_norm_g"])
    kv = _mm_rs("mem_kv", memn, comm.w("w_mem_kv"), F32)
    c_out = _attn_fwd(proj, kv, B, S)
    merged, ups = comm.carry(
        "merge_fwd", lambda r: _merge_fwd(a_out, b_out, c_out, comm.w("w_branch"), proj, riders=r))
    x1, h2 = _proj_res_norm("out_proj_norm2", merged, comm.w("w_out"), x, p["norm2_g"])
    ab = comm.carry("up_proj", lambda r: _mm_cs("up_proj", h2, comm.w("w_up"), BF16, riders=r))
    conv_w = comm.w("conv_w")
    ff = _conv_fwd(ab, conv_w, p["conv_b"], B, S)
    dx2, g["final_g"], loss = _proj_res_loss("down_proj_loss", ff, comm.w("w_down"), x1, tgt, p["final_g"])

    comm.grad("w_down", _mm_tn_rs("g_w_down", ff, dx2, to=D_FF // 2))
    d_ff = comm.carry("d_ff", lambda r: _mm_nt_rs("d_ff", dx2, comm.w("w_down"), BF16, riders=r))
    d_ab, g["conv_w"], g["conv_b"] = comm.carry(
        "conv_bwd", lambda r: _conv_bwd(ab, d_ff, conv_w, p["conv_b"], B, S, riders=r))
    comm.grad("w_up", _mm_tn_cs("g_w_up", h2, d_ab, N_CHIPS, to=512, stacked=True))
    d_x1, g["norm2_g"] = comm.carry("d_h2", lambda r: _mm_nt_cs(
        "d_h2_norm2_bwd", d_ab, comm.w("w_up"), F32, riders=r, stacked=True, norm_bwd=(x1, p["norm2_g"], dx2)))
    comm.grad("w_out", _mm_tn_rs("g_w_out", merged, d_x1, to=512))
    d_merged = _mm_nt_rs("d_merged", d_x1, comm.w("w_out"), F32)
    d_ups, d_gates = comm.carry("merge_bwd", lambda r: _merge_bwd(d_merged, ups, proj, riders=r))

    d_br = comm.carry("d_branch", lambda r: _branch_bwd_act(d_ups, comm.w("w_branch"), riders=r))
    g_branch = None
    for n, br in enumerate((a_out, b_out, c_out)):
        g_branch = _branch_bwd_weight("g_w_branch%d" % n, br, d_ups, n, into=g_branch)
    comm.grad("w_branch", g_branch)

    d_gm, g["w_spatial"], g["b_spatial"], g["ln_v_g"], g["ln_v_b"] = comm.carry(
        "gmlp_bwd", lambda r: _gmlp_bwd(proj, d_br, p["ln_v_g"], p["ln_v_b"], p["w_spatial"], p["b_spatial"],
                                        riders=r))
    d_xq, d_kv = _attn_bwd(proj, kv, d_br, B, S)
    comm.grad("w_mem_kv", _mm_tn_rs("g_w_mem_kv", memn, d_kv, to=512))
    d_memn = _mm_nt_rs("d_memn", d_kv, comm.w("w_mem_kv"), F32)
    _, g["mem_norm_g"] = _rms_bwd("mem_norm_bwd", mem, p["mem_norm_g"], d_memn, None)
    d_proj, g["lb_logits"], g["hgrn_norm_g"] = comm.carry(
        "hgrn_bwd", lambda r: _hgrn_bwd(proj, o_h, states, d_br, p["lb_logits"], p["hgrn_norm_g"],
                                        (d_gm, d_xq, d_gates), B, S, riders=r))
    comm.small_grads([g[n].reshape(_SMALL_SHAPE[n]) for n in _SMALL_EARLY] + [loss])
    comm.grad("w_in", *comm.carry("g_w_in", lambda r: _mm_tn_cs_to_sibling(
        "g_w_in", h, d_proj, N_CHIPS, comm.place, riders=r, send=comm.sends)))
    grad_x, g["norm1_g"] = comm.carry("d_h", lambda r: _mm_nt_cs(
        "d_h_norm1_bwd", d_proj, comm.w("w_in"), F32, riders=r, norm_bwd=(x, p["norm1_g"], d_x1)))
    return loss, grad_x, g


HBM_SPEC = pl.BlockSpec(memory_space=pltpu.HBM)


def _place():
    x, y, c = lax.axis_index("x"), lax.axis_index("y"), lax.axis_index("c")
    other_chips = [(1 - x, y), (x, 1 - y), (1 - x, 1 - y)]
    return x, y, c, other_chips


def _remote(src, dst, send_sem, recv_sem, dev):
    return pltpu.make_async_remote_copy(src_ref=src, dst_ref=dst, send_sem=send_sem, recv_sem=recv_sem,
                                        device_id=dev, device_id_type=MESH_ID)


class _Exchange:
    def __init__(self, operands, out_shape, aliases, scratch, start, finish, mid=None, mid_at=0.5):
        self.operands, self.out_shape, self.aliases, self.scratch = operands, out_shape, aliases, scratch
        self.start, self.finish, self.mid, self.mid_at = start, finish, mid, mid_at


def _run_exchanges(name, exs):
    n_in = [len(ex.operands) for ex in exs]
    n_out = [len(ex.out_shape) for ex in exs]
    n_scr = [len(ex.scratch) for ex in exs]

    def body(*refs):
        ins, outs, scr = refs[:sum(n_in)], refs[sum(n_in):sum(n_in) + sum(n_out)], refs[sum(n_in) + sum(n_out):]
        parts, oi, oo, os_ = [], 0, 0, 0
        for k in range(len(exs)):
            parts.append((ins[oi:oi + n_in[k]], outs[oo:oo + n_out[k]], scr[os_:os_ + n_scr[k]]))
            oi, oo, os_ = oi + n_in[k], oo + n_out[k], os_ + n_scr[k]
        for ex, part in zip(exs, parts):
            ex.start(*part)
        for ex, part in zip(exs, parts):
            if ex.mid is not None:
                ex.mid(*part)
        for ex, part in zip(exs, parts):
            ex.finish(*part)

    aliases, ops, shapes, scratch, oi, oo = {}, [], [], [], 0, 0
    for k, ex in enumerate(exs):
        aliases.update({oi + a: oo + b for a, b in ex.aliases.items()})
        oi, oo = oi + n_in[k], oo + n_out[k]
        ops += list(ex.operands)
        shapes += [pltpu.HBM(s.shape, s.dtype) for s in ex.out_shape]
        scratch += list(ex.scratch)
    res = _pallas(
        body, name=name, in_specs=[HBM_SPEC] * len(ops), out_specs=(HBM_SPEC,) * len(shapes), out_shape=tuple(shapes),
        input_output_aliases=aliases, scratch_shapes=scratch,
    )(*ops)
    out, oo = [], 0
    for k in range(len(exs)):
        out.append(list(res[oo:oo + n_out[k]]))
        oo += n_out[k]
    return out


def _ex_all_gather(slabs, halved, part=(0, 1)):
    n = len(slabs)

    def rows(a, cc):
        if not halved[a]:
            return slice(None)
        pr = slabs[a].shape[1] // part[1]
        return pl.ds(part[0] * pr + cc * (pr // 2), pr // 2)

    def ici(bufs, scr, a, j, chip, c, mine):
        px, py = chip
        x, y, _, _ = _place()
        qs = 2 * x + y if mine else 2 * px + py
        piece = bufs[a].at[qs, rows(a, c)]
        return _remote(piece, piece, scr[0].at[3 * a + j], scr[1].at[3 * a + j], (px, py, c))

    def d2d(bufs, scr, a, j, chip, cc):
        px, py = chip
        x, y, c, _ = _place()
        piece = bufs[a].at[2 * px + py, rows(a, cc)]
        return _remote(piece, piece, scr[2].at[3 * a + j], scr[3].at[3 * a + j], (x, y, 1 - c))

    def start(ins, outs, scr):
        _, _, c, chips = _place()
        for j, chip in enumerate(chips):
            for a in range(n):
                ici(outs, scr, a, j, chip, c, True).start()

    def finish(ins, outs, scr):
        _, _, c, chips = _place()
        for j, chip in enumerate(chips):
            for a in range(n):
                ici(outs, scr, a, j, chip, c, False).wait_recv()
                if halved[a]:
                    d2d(outs, scr, a, j, chip, c).start()
        for j, chip in enumerate(chips):
            for a in range(n):
                if halved[a]:
                    d2d(outs, scr, a, j, chip, 1 - c).wait_recv()
        for j, chip in enumerate(chips):
            for a in range(n):
                ici(outs, scr, a, j, chip, c, True).wait_send()
                if halved[a]:
                    d2d(outs, scr, a, j, chip, c).wait_send()

    return _Exchange(list(slabs), [jax.ShapeDtypeStruct(s.shape, s.dtype) for s in slabs],
                     {a: a for a in range(n)}, [pltpu.SemaphoreType.DMA((3 * n,))] * 4, start, finish)


def _ex_gather_relay(slabs, mid_at=0.5):
    n = len(slabs)

    def rows(a, cc):
        hr = slabs[a].shape[1] // 2
        return pl.ds(cc * hr, hr)

    def peers():
        x, y, c, _ = _place()
        nbr0 = ((x + c) % 2, (y + 1 - c) % 2)
        nbr1 = ((x + 1 - c) % 2, (y + c) % 2)
        return x, y, c, nbr0, nbr1, (1 - x, 1 - y)

    def ici(bufs, scr, a, k, chip, dev, cc):
        _, _, c, _, _, _ = peers()
        piece = bufs[a].at[2 * chip[0] + chip[1], rows(a, cc)]
        return _remote(piece, piece, scr[0].at[3 * a + k], scr[1].at[3 * a + k], (dev[0], dev[1], c))

    def d2d(bufs, scr, a, k, chip, cc):
        x, y, c, _, _, _ = peers()
        piece = bufs[a].at[2 * chip[0] + chip[1], rows(a, cc)]
        return _remote(piece, piece, scr[2].at[3 * a + k], scr[3].at[3 * a + k], (x, y, 1 - c))

    def start(ins, outs, scr):
        x, y, c, nbr0, nbr1, _ = peers()
        for a in range(n):
            ici(outs, scr, a, 0, (x, y), nbr0, c).start()
            ici(outs, scr, a, 1, (x, y), nbr1, c).start()

    def mid(ins, outs, scr):
        x, y, c, nbr0, nbr1, diag = peers()
        for a in range(n):
            ici(outs, scr, a, 0, nbr0, nbr0, c).wait_recv()
            ici(outs, scr, a, 2, nbr0, nbr1, c).start()
            d2d(outs, scr, a, 0, nbr0, c).start()
        for a in range(n):
            ici(outs, scr, a, 1, nbr1, nbr1, c).wait_recv()
            d2d(outs, scr, a, 1, nbr1, c).start()

    def finish(ins, outs, scr):
        x, y, c, nbr0, nbr1, diag = peers()
        for a in range(n):
            ici(outs, scr, a, 2, diag, nbr1, c).wait_recv()
            d2d(outs, scr, a, 2, diag, c).start()
        for a in range(n):
            d2d(outs, scr, a, 0, nbr1, 1 - c).wait_recv()
            d2d(outs, scr, a, 1, nbr0, 1 - c).wait_recv()
            d2d(outs, scr, a, 2, diag, 1 - c).wait_recv()
        for a in range(n):
            ici(outs, scr, a, 0, (x, y), nbr0, c).wait_send()
            ici(outs, scr, a, 1, (x, y), nbr1, c).wait_send()
            ici(outs, scr, a, 2, nbr0, nbr1, c).wait_send()
            d2d(outs, scr, a, 0, nbr0, c).wait_send()
            d2d(outs, scr, a, 1, nbr1, c).wait_send()
            d2d(outs, scr, a, 2, diag, c).wait_send()

    return _Exchange(list(slabs), [jax.ShapeDtypeStruct(s.shape, s.dtype) for s in slabs],
                     {a: a for a in range(n)}, [pltpu.SemaphoreType.DMA((3 * n,))] * 4, start, finish, mid, mid_at)


def _ex_to_sibling(grads):
    n = len(grads)

    def copy(ins, outs, scr, a):
        x, y, c, _ = _place()
        hr = grads[a].shape[1] // 2
        return _remote(ins[a].at[:, pl.ds((1 - c) * hr, hr), :], outs[a], scr[0].at[a], scr[1].at[a], (x, y, 1 - c))

    def start(ins, outs, scr):
        for a in range(n):
            copy(ins, outs, scr, a).start()

    def finish(ins, outs, scr):
        for a in range(n):
            copy(ins, outs, scr, a).wait()

    out_shape = [jax.ShapeDtypeStruct((g.shape[0], g.shape[1] // 2, g.shape[2]), g.dtype) for g in grads]
    return _Exchange(list(grads), out_shape, {}, [pltpu.SemaphoreType.DMA((n,))] * 2, start, finish)


def _ex_to_owner(parts, part=(0, 1), landing=None):
    n = len(parts)

    def copy(ins, outs, scr, a, j, chip):
        _, _, c, _ = _place()
        px, py = chip
        pr = parts[a].shape[1] // part[1]
        rows = pl.ds(part[0] * pr, pr)
        return _remote(ins[a].at[2 * px + py, rows], outs[a].at[j, rows], scr[0].at[3 * a + j],
                       scr[1].at[3 * a + j], (px, py, c))

    def start(ins, outs, scr):
        for j, chip in enumerate(_place()[3]):
            for a in range(n):
                copy(ins, outs, scr, a, j, chip).start()

    def finish(ins, outs, scr):
        for j, chip in enumerate(_place()[3]):
            for a in range(n):
                copy(ins, outs, scr, a, j, chip).wait()

    out_shape = [jax.ShapeDtypeStruct((3,) + p.shape[1:], p.dtype) for p in parts]
    operands, aliases = list(parts), {}
    if landing is not None:
        operands, aliases = operands + list(landing), {n + a: a for a in range(n)}
    return _Exchange(operands, out_shape, aliases, [pltpu.SemaphoreType.DMA((3 * n,))] * 2, start, finish)


def _ex_share_halves(bufs):
    n = len(bufs)

    def copy(outs, scr, a, cc):
        x, y, c, _ = _place()
        hr = bufs[a].shape[0] // 2
        piece = outs[a].at[pl.ds(cc * hr, hr), :]
        return _remote(piece, piece, scr[0].at[a], scr[1].at[a], (x, y, 1 - c))

    def start(ins, outs, scr):
        c = _place()[2]
        for a in range(n):
            copy(outs, scr, a, c).start()

    def finish(ins, outs, scr):
        c = _place()[2]
        for a in range(n):
            copy(outs, scr, a, c).wait_send()
            copy(outs, scr, a, 1 - c).wait_recv()

    return _Exchange(list(bufs), [jax.ShapeDtypeStruct(b.shape, b.dtype) for b in bufs], {a: a for a in range(n)},
                     [pltpu.SemaphoreType.DMA((n,))] * 2, start, finish)


def _ex_gather_small(arrs):
    n = len(arrs)

    def peer_of(m):
        x, y, c, _ = _place()
        return (1 - x if m & 4 else x, 1 - y if m & 2 else y, 1 - c if m & 1 else c)

    def own(ins, outs, scr, a):
        x, y, c, _ = _place()
        return pltpu.make_async_copy(ins[a], outs[a].at[4 * x + 2 * y + c], scr[2].at[a])

    def start(ins, outs, scr):
        x, y, c, _ = _place()
        for a in range(n):
            own(ins, outs, scr, a).start()
        for m in range(1, N_DEV):
            for a in range(n):
                k = (N_DEV - 1) * a + m - 1
                _remote(ins[a], outs[a].at[4 * x + 2 * y + c], scr[0].at[k], scr[1].at[k], peer_of(m)).start()

    def finish(ins, outs, scr):
        for a in range(n):
            own(ins, outs, scr, a).wait()
        for m in range(1, N_DEV):
            px, py, pc = peer_of(m)
            for a in range(n):
                k = (N_DEV - 1) * a + m - 1
                slot = outs[a].at[4 * px + 2 * py + pc]
                cp = _remote(ins[a], slot, scr[0].at[k], scr[1].at[k], (px, py, pc))
                cp.wait_send()
                cp.wait_recv()

    out_shape = [jax.ShapeDtypeStruct((N_DEV,) + a.shape, a.dtype) for a in arrs]
    return _Exchange(list(arrs), out_shape, {},
                     [pltpu.SemaphoreType.DMA(((N_DEV - 1) * n,))] * 2 + [pltpu.SemaphoreType.DMA((n,))], start, finish)


def _div_tile(n, want):
    best = None
    for t in range(8, min(n, want) + 1, 8):
        if n % t == 0:
            best = t
    assert best is not None, n
    return best


def _cast_into_slab(name, w, place, dtype):
    r, cc = w.shape
    tr = r if r * cc <= 128 * 1024 else _div_tile(r, 256)

    def body(s_ref, w_ref, o_ref):
        o_ref[...] = w_ref[...].astype(o_ref.dtype)

    return _pallas(
        body, name=name,
        grid_spec=pltpu.PrefetchScalarGridSpec(
            num_scalar_prefetch=1, grid=(r // tr,),
            in_specs=[pl.BlockSpec((tr, cc), lambda i, s: (i, 0))],
            out_specs=pl.BlockSpec((None, tr, cc), lambda i, s: (s[0], i, 0))),
        out_shape=jax.ShapeDtypeStruct((N_CHIPS, r, cc), dtype), compiler_params=_cp("parallel"),
    )(place, w)


def _add_half(name, g, rcv, place):
    nq, r, cc = g.shape
    hr = r // 2

    def body(s_ref, g_ref, r_ref, o_ref):
        o_ref[...] = (g_ref[...] + r_ref[...]).astype(o_ref.dtype)

    spec = pl.BlockSpec((None, hr, cc), lambda i, s: (i, 0, 0))
    return _pallas(
        body, name=name,
        grid_spec=pltpu.PrefetchScalarGridSpec(
            num_scalar_prefetch=1, grid=(nq,),
            in_specs=[pl.BlockSpec((None, hr, cc), lambda i, s: (i, s[1], 0)), spec], out_specs=spec),
        out_shape=jax.ShapeDtypeStruct((nq, hr, cc), BF16), compiler_params=_cp("parallel"),
    )(place, g, rcv)


def _sum_owner(name, part, rcv, place):
    _, hr, cc = part.shape
    tr = _div_tile(hr, 128)
    nb = hr // tr

    def body(s_ref, p_ref, r_ref, o_ref):
        o_ref[...] = ((p_ref[...].astype(F32) + r_ref[0].astype(F32)) + r_ref[1].astype(F32)) + r_ref[2].astype(F32)

    return _pallas(
        body, name=name,
        grid_spec=pltpu.PrefetchScalarGridSpec(
            num_scalar_prefetch=1, grid=(nb,),
            in_specs=[pl.BlockSpec((None, tr, cc), lambda i, s: (s[0], i, 0)),
                      pl.BlockSpec((3, tr, cc), lambda i, s: (0, i, 0))],
            out_specs=pl.BlockSpec((tr, cc), lambda i, s: (s[1] * nb + i, 0))),
        out_shape=jax.ShapeDtypeStruct((2 * hr, cc), F32), compiler_params=_cp("parallel"),
    )(place, part, rcv)


def _sum_small(gathered, local, place):
    n = len(gathered)

    def body(s_ref, *refs):
        g_refs, l_refs, o_refs = refs[:n], refs[n:2 * n], refs[2 * n:]
        me = s_ref[2]
        for g_ref, l_ref, o_ref in zip(g_refs, l_refs, o_refs):
            acc = None
            for d in range(N_DEV):
                term = jnp.where(me == d, l_ref[...], g_ref[d])
                acc = term if acc is None else acc + term
            o_ref[...] = acc

    def whole(shape):
        return pl.BlockSpec(shape, lambda i, s, nd=len(shape): (0,) * nd)

    return _pallas(
        body, name="sum_small",
        grid_spec=pltpu.PrefetchScalarGridSpec(
            num_scalar_prefetch=1, grid=(1,),
            in_specs=[whole(g.shape) for g in gathered] + [whole(a.shape) for a in local],
            out_specs=tuple(whole(a.shape) for a in local)),
        out_shape=tuple(jax.ShapeDtypeStruct(a.shape, a.dtype) for a in local), compiler_params=_cp("arbitrary"),
    )(place, *gathered, *local)


def _adamw(name, w, g, m, v):
    r, cc = w.shape
    tr = r if r * cc <= 128 * 1024 else _div_tile(r, 256)

    def body(w_ref, g_ref, m_ref, v_ref, d_ref, mo_ref, vo_ref, go_ref):
        gv = g_ref[...]
        go_ref[...] = gv
        mn = ADAM_B1 * m_ref[...] + (1.0 - ADAM_B1) * gv
        vn = ADAM_B2 * v_ref[...] + (1.0 - ADAM_B2) * (gv * gv)
        m_hat = mn / (1.0 - ADAM_B1 ** ADAM_STEP)
        v_hat = vn / (1.0 - ADAM_B2 ** ADAM_STEP)
        d_ref[...] = -ADAM_LR * (m_hat / (jnp.sqrt(v_hat) + ADAM_EPS) + ADAM_WD * w_ref[...])
        mo_ref[...] = mn
        vo_ref[...] = vn

    spec = pl.BlockSpec((tr, cc), lambda i: (i, 0))
    sd = jax.ShapeDtypeStruct((r, cc), F32)
    return _pallas(
        body, name=name, grid=(r // tr,), in_specs=[spec] * 4, out_specs=(spec,) * 4, out_shape=(sd,) * 4,
        compiler_params=_cp("parallel"),
    )(w, g, m, v)


_BIG = ("w_in", "w_up", "w_branch", "w_mem_kv", "w_out", "w_down")
_BIG_SHARD_SHAPE = {"w_in": (1024, 1664), "w_up": (1024, 1408), "w_branch": (1536, 256),
                    "w_mem_kv": (256, 1024), "w_out": (256, 1024), "w_down": (704, 1024)}
_SMALL_SHAPE = {"norm1_g": (1, D_MODEL), "ln_v_g": (1, GM_WIDTH), "ln_v_b": (1, GM_WIDTH),
                "w_spatial": (GM_GROUPS * GM_CHUNK, GM_CHUNK), "b_spatial": (GM_GROUPS, GM_CHUNK),
                "lb_logits": (2, HG_HEADS * HG_DIM), "hgrn_norm_g": (1, HG_DIM), "mem_norm_g": (1, D_MODEL),
                "norm2_g": (1, D_MODEL), "conv_w": (3, D_FF), "conv_b": (1, D_FF), "final_g": (1, D_MODEL)}
_SMALL_EARLY = tuple(n for n in _SMALL_SHAPE if n != "norm1_g")
_PARAM_ORDER = ("norm1_g", "w_in", "ln_v_g", "ln_v_b", "w_spatial", "b_spatial", "lb_logits", "hgrn_norm_g",
                "mem_norm_g", "w_mem_kv", "w_branch", "w_out", "norm2_g", "w_up", "conv_w", "conv_b", "w_down",
                "final_g")


def _adamw_small(ws, gs, ms, vs):
    n = len(ws)

    def body(*refs):
        w_refs, g_refs, m_refs, v_refs = refs[:n], refs[n:2 * n], refs[2 * n:3 * n], refs[3 * n:4 * n]
        d_refs, mo_refs, vo_refs = refs[4 * n:5 * n], refs[5 * n:6 * n], refs[6 * n:]
        for k in range(n):
            gv = g_refs[k][...]
            mn = ADAM_B1 * m_refs[k][...] + (1.0 - ADAM_B1) * gv
            vn = ADAM_B2 * v_refs[k][...] + (1.0 - ADAM_B2) * (gv * gv)
            m_hat = mn / (1.0 - ADAM_B1 ** ADAM_STEP)
            v_hat = vn / (1.0 - ADAM_B2 ** ADAM_STEP)
            d_refs[k][...] = -ADAM_LR * (m_hat / (jnp.sqrt(v_hat) + ADAM_EPS) + ADAM_WD * w_refs[k][...])
            mo_refs[k][...] = mn
            vo_refs[k][...] = vn

    specs = [pl.BlockSpec(a.shape, lambda i, nd=a.ndim: (0,) * nd) for a in ws]
    shapes = tuple(jax.ShapeDtypeStruct(a.shape, F32) for a in ws)
    res = _pallas(
        body, name="adamw_small", grid=(1,), in_specs=specs * 4, out_specs=tuple(specs * 3), out_shape=shapes * 3,
        compiler_params=_cp("arbitrary"),
    )(*ws, *gs, *ms, *vs)
    return res[:n], res[n:2 * n], res[2 * n:]


class _Comm:
    _ROW_SHARDED = ("w_mem_kv", "w_out", "w_down")

    def __init__(self, slabs, place):
        self.slabs, self.place = slabs, place
        self.full, self.raw, self.parts, self.landing, self.bufs, self.done = {}, {}, {}, {}, {}, {}

    def w(self, name):
        a = self.full[name]
        if name in self._ROW_SHARDED:
            return a.reshape(-1, a.shape[-1])
        if name == "conv_w":
            return jnp.transpose(a, (1, 0, 2)).reshape(3, 1, D_FF)
        return a

    sends = True

    def grad(self, name, arr, from_sibling=None):
        self.raw[name] = arr.reshape((N_CHIPS, -1, arr.shape[-1]))
        if from_sibling is not None:
            self.parts[name] = _add_half("rs_add_" + name, self.raw[name], from_sibling, self.place)

    def small_grads(self, arrays):
        self.small_local = list(arrays)

    def carry(self, tag, call):
        plan = self._plan(tag)
        if not plan:
            return call(())
        out, carried = call([ex for ex, _ in plan])
        for (_, deliver), res in zip(plan, carried):
            deliver(res)
        return out

    def finish(self, last_small):
        ex, deliver = self._share(["w_out", "w_branch", "w_mem_kv", "w_in"])
        shared, small = _run_exchanges("share_and_gather_last", [ex, _ex_gather_small(last_small)])
        deliver(shared)
        return self.done, self.small_local + list(last_small), self.small_everyone + small

    def _plan(self, tag):
        if tag == "norm1":
            def deliver(res):
                self.full["w_in"] = res[0]

            return [(_ex_gather_relay([self.slabs["w_in"]]), deliver)]
        if tag == "in_proj":
            return [self._gather_relay(["w_branch", "w_out", "w_mem_kv", "w_down"], 0.6), self._gather(["conv_w"])]
        if tag == "hgrn_fwd":
            return [self._gather_relay(["w_up"], 0.8)]
        if tag == "d_h2":
            return [self._to_sibling(["w_down", "w_up"])]
        if tag == "merge_bwd":
            return [self._to_owner(["w_up"], (0, 2))]
        if tag == "hgrn_bwd":
            return [self._to_owner(["w_down"]), self._to_owner(["w_up"], (1, 2)),
                    self._to_sibling(["w_out", "w_branch", "w_mem_kv"])]
        if tag == "g_w_in":
            def keep(res):
                self.small_everyone = res

            return [self._to_owner(["w_out", "w_branch", "w_mem_kv"]), (_ex_gather_small(self.small_local), keep)]
        if tag == "d_h":
            return [self._to_owner(["w_in"]), self._share(["w_down", "w_up"])]
        return []

    def _gather(self, names, part=(0, 1)):
        def deliver(res):
            self.slabs.update(zip(names, res))
            self.full.update(zip(names, res))

        return _ex_all_gather([self.slabs[n] for n in names], [n != "conv_w" for n in names], part), deliver

    def _gather_relay(self, names, mid_at):
        return _ex_gather_relay([self.slabs[n] for n in names], mid_at), lambda res: self.full.update(zip(names, res))

    def _to_sibling(self, names):
        def deliver(res):
            for n, r in zip(names, res):
                self.parts[n] = _add_half("rs_add_" + n, self.raw[n], r, self.place)

        return _ex_to_sibling([self.raw[n] for n in names]), deliver

    def _to_owner(self, names, part=(0, 1)):
        def deliver(res):
            for n, r in zip(names, res):
                if part[0] + 1 < part[1]:
                    self.landing[n] = r
                else:
                    self.bufs[n] = _sum_owner("rs_sum_" + n, self.parts[n], r, self.place)

        landing = [self.landing[n] for n in names] if part[0] else None
        return _ex_to_owner([self.parts[n] for n in names], part, landing), deliver

    def _share(self, names):
        return _ex_share_halves([self.bufs[n] for n in names]), lambda res: self.done.update(zip(names, res))


def kernel(x, mem, norm1_g, w_in, ln_v_g, ln_v_b, w_spatial, b_spatial, lb_logits, hgrn_norm_g, mem_norm_g, w_mem_kv, w_branch, w_out, norm2_g, w_up, conv_w, conv_b, w_down, final_g, loss_target, m_norm1_g, m_w_in, m_ln_v_g, m_ln_v_b, m_w_spatial, m_b_spatial, m_lb_logits, m_hgrn_norm_g, m_mem_norm_g, m_w_mem_kv, m_w_branch, m_w_out, m_norm2_g, m_w_up, m_conv_w, m_conv_b, m_w_down, m_final_g, v_norm1_g, v_w_in, v_ln_v_g, v_ln_v_b, v_w_spatial, v_b_spatial, v_lb_logits, v_hgrn_norm_g, v_mem_norm_g, v_w_mem_kv, v_w_branch, v_w_out, v_norm2_g, v_w_up, v_conv_w, v_conv_b, v_w_down, v_final_g):
    w = dict(norm1_g=norm1_g, w_in=w_in, ln_v_g=ln_v_g, ln_v_b=ln_v_b, w_spatial=w_spatial, b_spatial=b_spatial,
             lb_logits=lb_logits, hgrn_norm_g=hgrn_norm_g, mem_norm_g=mem_norm_g, w_mem_kv=w_mem_kv,
             w_branch=w_branch, w_out=w_out, norm2_g=norm2_g, w_up=w_up, conv_w=conv_w, conv_b=conv_b,
             w_down=w_down, final_g=final_g)
    mom = dict(norm1_g=m_norm1_g, w_in=m_w_in, ln_v_g=m_ln_v_g, ln_v_b=m_ln_v_b, w_spatial=m_w_spatial,
               b_spatial=m_b_spatial, lb_logits=m_lb_logits, hgrn_norm_g=m_hgrn_norm_g, mem_norm_g=m_mem_norm_g,
               w_mem_kv=m_w_mem_kv, w_branch=m_w_branch, w_out=m_w_out, norm2_g=m_norm2_g, w_up=m_w_up,
               conv_w=m_conv_w, conv_b=m_conv_b, w_down=m_w_down, final_g=m_final_g)
    var = dict(norm1_g=v_norm1_g, w_in=v_w_in, ln_v_g=v_ln_v_g, ln_v_b=v_ln_v_b, w_spatial=v_w_spatial,
               b_spatial=v_b_spatial, lb_logits=v_lb_logits, hgrn_norm_g=v_hgrn_norm_g, mem_norm_g=v_mem_norm_g,
               w_mem_kv=v_w_mem_kv, w_branch=v_w_branch, w_out=v_w_out, norm2_g=v_norm2_g, w_up=v_w_up,
               conv_w=v_conv_w, conv_b=v_conv_b, w_down=v_w_down, final_g=v_final_g)
    B, S, D = x.shape
    T = B * S
    ci = lax.axis_index("c")
    q = 2 * lax.axis_index("x") + lax.axis_index("y")
    place = jnp.stack([q, ci, 2 * q + ci]).astype(jnp.int32)

    shards = {n: w[n].reshape(_BIG_SHARD_SHAPE[n]) for n in _BIG}
    slabs = {"w_in": _cast_into_slab("slab_w_in", shards.pop("w_in"), place, BF16),
             "conv_w": _cast_into_slab("slab_conv_w", conv_w[0], place, F32)}
    comm = _Comm(slabs, place)
    p = dict(
        cast_beside_norm1=shards,
        norm1_g=norm1_g, ln_v_g=ln_v_g, ln_v_b=ln_v_b, w_spatial=w_spatial[0],
        b_spatial=b_spatial.reshape(GM_GROUPS, GM_CHUNK, 1), lb_logits=lb_logits, hgrn_norm_g=hgrn_norm_g,
        mem_norm_g=mem_norm_g, norm2_g=norm2_g, conv_b=conv_b, final_g=final_g.reshape(1, D))

    loss, grad_x, g = _local_step(x.reshape(T, D), mem.reshape(B * MEM_LEN, D), loss_target.reshape(T, D), p, comm,
                                  B, S)

    shard_grads, local_small, everyone = comm.finish([g["norm1_g"]])
    summed = _sum_small(everyone, local_small, place)
    small_names = list(_SMALL_EARLY) + ["norm1_g"]
    total = dict(zip(_SMALL_EARLY, summed))
    loss_total, total["norm1_g"] = summed[len(_SMALL_EARLY)][0, 0], summed[-1]

    grads, delta, new_m, new_v = {}, {}, {}, {}
    for n in _BIG:
        shp = _BIG_SHARD_SHAPE[n]
        delta[n], new_m[n], new_v[n], grads[n] = _adamw("adamw_" + n, w[n].reshape(shp), shard_grads[n],
                                                        mom[n].reshape(shp), var[n].reshape(shp))
    cw_shard = D_FF // N_CHIPS
    total["conv_w"] = lax.dynamic_slice(total["conv_w"], (0, q * cw_shard), (3, cw_shard)).reshape(3, 1, cw_shard)

    def flat2d(d, n):
        return d[n].reshape(total[n].shape)

    upd = _adamw_small([flat2d(w, n) for n in small_names], [total[n] for n in small_names],
                       [flat2d(mom, n) for n in small_names], [flat2d(var, n) for n in small_names])
    for k, n in enumerate(small_names):
        grads[n], delta[n], new_m[n], new_v[n] = total[n], upd[0][k], upd[1][k], upd[2][k]

    def shaped(d):
        return [d[n].reshape(w[n].shape) for n in _PARAM_ORDER]

    return (loss_total, grad_x.reshape(B, S, D), *shaped(grads), *shaped(delta), *shaped(new_m), *shaped(new_v))
```
